```python
import math
import jax, jax.numpy as jnp
from jax import lax
import numpy as np

D_MODEL = 1024
BATCH = 8
SEQ = 4096
DEPTH = 1

CHUNK = 64
Q_BLOCK = 128
EPS = 1e-6

MLA_HEADS = 8
MLA_NOPE = 64
MLA_ROPE = 32
MLA_V = 64
MLA_QK = MLA_NOPE + MLA_ROPE
Q_LORA = 768
KV_LORA = 256
ROPE_THETA = 10000.0
MLA_WIDTH = MLA_HEADS * MLA_V

HG_HEADS = 8
HG_DK = 64
HG_DV = 64
HG_KEY_WIDTH = HG_HEADS * HG_DK
HG_WIDTH = HG_HEADS * HG_DV
HG_BLOCK = 32

N_BRANCH = 2
IN_SPLITS = (Q_LORA, KV_LORA, MLA_ROPE, MLA_WIDTH,
             HG_KEY_WIDTH, HG_KEY_WIDTH, HG_WIDTH, HG_WIDTH,
             N_BRANCH * D_MODEL)
D_IN = sum(IN_SPLITS)

kernel_name = "hybrid_mla_hgrn2_gated_merge"


def rmsnorm(x, g):
    xf = x.astype(jnp.float32)
    y = xf * lax.rsqrt(jnp.mean(xf * xf, axis=-1, keepdims=True) + EPS)
    return (y * g.astype(jnp.float32)).astype(x.dtype)


def rope_tables(seq):
    inv = ROPE_THETA ** (-jnp.arange(0, MLA_ROPE, 2, dtype=jnp.float32) / MLA_ROPE)
    ang = jnp.arange(seq, dtype=jnp.float32)[:, None] * inv[None, :]
    return jnp.cos(ang), jnp.sin(ang)


def apply_rope(x, cos, sin):
    half = MLA_ROPE // 2
    xf = x.astype(jnp.float32)
    x1, x2 = xf[..., :half], xf[..., half:]
    c = cos[None, :, None, :]
    s = sin[None, :, None, :]
    return jnp.concatenate([x1 * c - x2 * s, x1 * s + x2 * c], axis=-1).astype(x.dtype)


def chunk_causal_attention(q, k, v):
    b, h, s, dq = q.shape
    nqb = s // Q_BLOCK
    qb = q.reshape(b, h, nqb, Q_BLOCK, dq).transpose(2, 0, 1, 3, 4)
    key_chunk = jnp.arange(s) // CHUNK
    q_chunk = (jnp.arange(s) // CHUNK).reshape(nqb, Q_BLOCK)
    scale = 1.0 / math.sqrt(dq)

    def one_block(args):
        qi, qc = args
        sc = jnp.einsum('bhqd,bhkd->bhqk', qi, k).astype(jnp.float32) * scale
        mask = key_chunk[None, :] <= qc[:, None]
        sc = jnp.where(mask, sc, -jnp.inf)
        p = jax.nn.softmax(sc, axis=-1).astype(v.dtype)
        return jnp.einsum('bhqk,bhkd->bhqd', p, v)

    out = lax.map(one_block, (qb, q_chunk))
    return out.transpose(1, 2, 0, 3, 4).reshape(b, h, s, -1)


def hgrn2_chunkwise(q, k, v, log_f):
    b, s, h, dk = q.shape
    dv = v.shape[-1]
    n = s // HG_BLOCK

    def blocks(t):
        return t.astype(jnp.float32).reshape(b, n, HG_BLOCK, h, -1).transpose(0, 3, 1, 2, 4)

    qc, kc, vc, gc = blocks(q), blocks(k), blocks(v), blocks(log_f)
    cum = jnp.cumsum(gc, axis=3)
    last = cum[..., -1:, :]
    q_dec = qc * jnp.exp(cum)
    k_inv = kc * jnp.exp(-cum)
    k_end = kc * jnp.exp(last - cum)
    causal = jnp.tril(jnp.ones((HG_BLOCK, HG_BLOCK), dtype=bool))
    a = jnp.where(causal, jnp.einsum('bhntk,bhnsk->bhnts', q_dec, k_inv), 0.0)
    o_intra = jnp.einsum('bhnts,bhnsv->bhntv', a, vc)
    upd = jnp.einsum('bhnsk,bhnsv->bhnkv', k_end, vc)
    decay = jnp.exp(last[..., 0, :])

    def step(state, xs):
        d, u = xs
        return d[..., None] * state + u, state

    init = jnp.zeros((b, h, dk, dv), jnp.float32)
    _, s_prev = lax.scan(step, init, (jnp.moveaxis(decay, 2, 0), jnp.moveaxis(upd, 2, 0)))
    s_prev = jnp.moveaxis(s_prev, 0, 2)
    o_inter = jnp.einsum('bhntk,bhnkv->bhntv', q_dec, s_prev)
    return (o_intra + o_inter).transpose(0, 2, 3, 1, 4).reshape(b, s, h, dv)


def _fwd_setup_inputs(seed: int = 0) -> dict:
    key = jax.random.key(seed)
    ks = jax.random.split(key, 14)

    def nrm(k, shape, fan_in):
        return jax.random.normal(k, shape, jnp.float32) * (fan_in ** -0.5)

    def gain(k, shape):
        return 1.0 + 0.02 * jax.random.normal(k, shape, jnp.float32)

    return {
        "x": jax.random.normal(ks[0], (BATCH, SEQ, D_MODEL), jnp.float32),
        "g_pre": gain(ks[1], (DEPTH, D_MODEL)),
        "w_in": nrm(ks[2], (DEPTH, D_MODEL, D_IN), D_MODEL),
        "b_gate": 0.01 * jax.random.normal(ks[3], (DEPTH, N_BRANCH * D_MODEL), jnp.float32),
        "g_q": gain(ks[4], (DEPTH, Q_LORA)),
        "w_uq": nrm(ks[5], (DEPTH, Q_LORA, MLA_HEADS * MLA_QK), Q_LORA),
        "g_kv": gain(ks[6], (DEPTH, KV_LORA)),
        "w_ukv": nrm(ks[7], (DEPTH, KV_LORA, MLA_HEADS * (MLA_NOPE + MLA_V)), KV_LORA),
        "lb_logits": 0.1 * jax.random.normal(ks[8], (DEPTH + 1, HG_KEY_WIDTH), jnp.float32),
        "g_hgrn": gain(ks[9], (DEPTH, HG_DV)),
        "w_branch_a": nrm(ks[10], (DEPTH, MLA_WIDTH, D_MODEL), MLA_WIDTH),
        "w_branch_b": nrm(ks[11], (DEPTH, HG_WIDTH, D_MODEL), HG_WIDTH),
        "w_out": nrm(ks[12], (DEPTH, D_MODEL, D_MODEL), D_MODEL),
        "g_post": gain(ks[13], (DEPTH, D_MODEL)),
    }


def _fwd_reference(x, g_pre, w_in, b_gate, g_q, w_uq, g_kv, w_ukv, lb_logits,
              g_hgrn, w_branch_a, w_branch_b, w_out, g_post):
    b, s, _ = x.shape
    cos, sin = rope_tables(s)
    split_at = [int(o) for o in np.cumsum(IN_SPLITS)[:-1]]
    lower_bounds = jnp.cumsum(jax.nn.softmax(lb_logits.astype(jnp.float32), axis=0), axis=0)

    for l in range(DEPTH):
        h = rmsnorm(x, g_pre[l])
        proj = h @ w_in[l]
        (c_q, c_kv, k_pe, gate_a, hq, hf, hi, gate_b, merge_logits) = jnp.split(proj, split_at, axis=-1)

        q = (rmsnorm(c_q, g_q[l]) @ w_uq[l]).reshape(b, s, MLA_HEADS, MLA_QK)
        q_nope, q_pe = q[..., :MLA_NOPE], apply_rope(q[..., MLA_NOPE:], cos, sin)
        kv = (rmsnorm(c_kv, g_kv[l]) @ w_ukv[l]).reshape(b, s, MLA_HEADS, MLA_NOPE + MLA_V)
        k_nope, v = kv[..., :MLA_NOPE], kv[..., MLA_NOPE:]
        k_pe = jnp.broadcast_to(apply_rope(k_pe[:, :, None, :], cos, sin), (b, s, MLA_HEADS, MLA_ROPE))
        qf = jnp.concatenate([q_nope, q_pe], axis=-1).transpose(0, 2, 1, 3)
        kf = jnp.concatenate([k_nope, k_pe], axis=-1).transpose(0, 2, 1, 3)
        vf = v.transpose(0, 2, 1, 3)
        attn = chunk_causal_attention(qf, kf, vf).transpose(0, 2, 1, 3).reshape(b, s, MLA_WIDTH)
        y_a = (attn * jax.nn.silu(gate_a)) @ w_branch_a[l]

        lb = lower_bounds[l]
        f = lb + (1.0 - lb) * jax.nn.sigmoid(hf.astype(jnp.float32))
        log_f = jnp.log(f).reshape(b, s, HG_HEADS, HG_DK)
        k_in = (1.0 - f).reshape(b, s, HG_HEADS, HG_DK)
        o = hgrn2_chunkwise(hq.reshape(b, s, HG_HEADS, HG_DK), k_in,
                            hi.reshape(b, s, HG_HEADS, HG_DV), log_f)
        o = rmsnorm(o, g_hgrn[l]).astype(x.dtype).reshape(b, s, HG_WIDTH)
        y_b = (o * jax.nn.silu(gate_b)) @ w_branch_b[l]

        gates = jax.nn.sigmoid((merge_logits + b_gate[l]).astype(jnp.float32)).astype(x.dtype)
        m = gates[..., :D_MODEL] * y_a + gates[..., D_MODEL:] * y_b
        y = m @ w_out[l]
        x = x + rmsnorm(y, g_post[l])
    return x


import jax as _jax
import jax.numpy as _jnp

TWIN_FORMAT = 'train_step'
FWD_PARAMS = ['x', 'g_pre', 'w_in', 'b_gate', 'g_q', 'w_uq', 'g_kv', 'w_ukv', 'lb_logits', 'g_hgrn', 'w_branch_a', 'w_branch_b', 'w_out', 'g_post']
TWIN_WEIGHTS = ['g_pre', 'w_in', 'b_gate', 'g_q', 'w_uq', 'g_kv', 'w_ukv', 'lb_logits', 'g_hgrn', 'w_branch_a', 'w_branch_b', 'w_out', 'g_post']
TWIN_DIFF_INPUT = 'x'
TWIN_INPUTS = ['x', 'g_pre', 'w_in', 'b_gate', 'g_q', 'w_uq', 'g_kv', 'w_ukv', 'lb_logits', 'g_hgrn', 'w_branch_a', 'w_branch_b', 'w_out', 'g_post', 'loss_target', 'm_g_pre', 'm_w_in', 'm_b_gate', 'm_g_q', 'm_w_uq', 'm_g_kv', 'm_w_ukv', 'm_lb_logits', 'm_g_hgrn', 'm_w_branch_a', 'm_w_branch_b', 'm_w_out', 'm_g_post', 'v_g_pre', 'v_w_in', 'v_b_gate', 'v_g_q', 'v_w_uq', 'v_g_kv', 'v_w_ukv', 'v_lb_logits', 'v_g_hgrn', 'v_w_branch_a', 'v_w_branch_b', 'v_w_out', 'v_g_post']
TWIN_OUTPUTS = ['loss', 'grad_x', 'grad_g_pre', 'grad_w_in', 'grad_b_gate', 'grad_g_q', 'grad_w_uq', 'grad_g_kv', 'grad_w_ukv', 'grad_lb_logits', 'grad_g_hgrn', 'grad_w_branch_a', 'grad_w_branch_b', 'grad_w_out', 'grad_g_post', 'delta_g_pre', 'delta_w_in', 'delta_b_gate', 'delta_g_q', 'delta_w_uq', 'delta_g_kv', 'delta_w_ukv', 'delta_lb_logits', 'delta_g_hgrn', 'delta_w_branch_a', 'delta_w_branch_b', 'delta_w_out', 'delta_g_post', 'new_m_g_pre', 'new_m_w_in', 'new_m_b_gate', 'new_m_g_q', 'new_m_w_uq', 'new_m_g_kv', 'new_m_w_ukv', 'new_m_lb_logits', 'new_m_g_hgrn', 'new_m_w_branch_a', 'new_m_w_branch_b', 'new_m_w_out', 'new_m_g_post', 'new_v_g_pre', 'new_v_w_in', 'new_v_b_gate', 'new_v_g_q', 'new_v_w_uq', 'new_v_g_kv', 'new_v_w_ukv', 'new_v_lb_logits', 'new_v_g_hgrn', 'new_v_w_branch_a', 'new_v_w_branch_b', 'new_v_w_out', 'new_v_g_post']
TWIN_LEAF_KINDS = {'loss': 'loss', 'grad_x': 'grad_x', 'grad_g_pre': 'grad_w', 'grad_w_in': 'grad_w', 'grad_b_gate': 'grad_w', 'grad_g_q': 'grad_w', 'grad_w_uq': 'grad_w', 'grad_g_kv': 'grad_w', 'grad_w_ukv': 'grad_w', 'grad_lb_logits': 'grad_w', 'grad_g_hgrn': 'grad_w', 'grad_w_branch_a': 'grad_w', 'grad_w_branch_b': 'grad_w', 'grad_w_out': 'grad_w', 'grad_g_post': 'grad_w', 'delta_g_pre': 'delta_w', 'delta_w_in': 'delta_w', 'delta_b_gate': 'delta_w', 'delta_g_q': 'delta_w', 'delta_w_uq': 'delta_w', 'delta_g_kv': 'delta_w', 'delta_w_ukv': 'delta_w', 'delta_lb_logits': 'delta_w', 'delta_g_hgrn': 'delta_w', 'delta_w_branch_a': 'delta_w', 'delta_w_branch_b': 'delta_w', 'delta_w_out': 'delta_w', 'delta_g_post': 'delta_w', 'new_m_g_pre': 'new_m', 'new_m_w_in': 'new_m', 'new_m_b_gate': 'new_m', 'new_m_g_q': 'new_m', 'new_m_w_uq': 'new_m', 'new_m_g_kv': 'new_m', 'new_m_w_ukv': 'new_m', 'new_m_lb_logits': 'new_m', 'new_m_g_hgrn': 'new_m', 'new_m_w_branch_a': 'new_m', 'new_m_w_branch_b': 'new_m', 'new_m_w_out': 'new_m', 'new_m_g_post': 'new_m', 'new_v_g_pre': 'new_v', 'new_v_w_in': 'new_v', 'new_v_b_gate': 'new_v', 'new_v_g_q': 'new_v', 'new_v_w_uq': 'new_v', 'new_v_g_kv': 'new_v', 'new_v_w_ukv': 'new_v', 'new_v_lb_logits': 'new_v', 'new_v_g_hgrn': 'new_v', 'new_v_w_branch_a': 'new_v', 'new_v_w_branch_b': 'new_v', 'new_v_w_out': 'new_v', 'new_v_g_post': 'new_v'}


def _forward(args):
    return _fwd_reference(*[args[k] for k in FWD_PARAMS])


def _output_shape():
    out = _jax.eval_shape(lambda: _forward(_fwd_setup_inputs(0)))
    return out.shape, out.dtype

N_MICROBATCH = 1
ADAM_LR = 0.001
ADAM_B1 = 0.9
ADAM_B2 = 0.999
ADAM_EPS = 1e-08
ADAM_WD = 0.01
ADAM_STEP = 10
PER_EXAMPLE_BATCH_AXIS = {'x': 0, 'loss_target': 0}
SHARED_INPUTS = []
_WEIGHT_DTYPES = {'g_pre': _jnp.float32, 'w_in': _jnp.float32, 'b_gate': _jnp.float32, 'g_q': _jnp.float32, 'w_uq': _jnp.float32, 'g_kv': _jnp.float32, 'w_ukv': _jnp.float32, 'lb_logits': _jnp.float32, 'g_hgrn': _jnp.float32, 'w_branch_a': _jnp.float32, 'w_branch_b': _jnp.float32, 'w_out': _jnp.float32, 'g_post': _jnp.float32}
MOMENT_SCALE = {'g_pre': 6.037238e-01, 'w_in': 2.459520e-01, 'b_gate': 7.339963e-02, 'g_q': 4.504294e-02, 'w_uq': 4.346542e-02, 'g_kv': 1.048540e-01, 'w_ukv': 5.090329e-02, 'lb_logits': 2.424882e-01, 'g_hgrn': 1.115373e+00, 'w_branch_a': 3.937400e-02, 'w_branch_b': 2.515316e-01, 'w_out': 2.530453e-01, 'g_post': 3.211331e+01}


def _to_microbatches(a, axis):
    t = _jnp.moveaxis(a, axis, 0)
    t = t.reshape((N_MICROBATCH, t.shape[0] // N_MICROBATCH) + t.shape[1:])
    return _jnp.moveaxis(t, 1, axis + 1)


def setup_inputs(seed: int = 0) -> dict:
    inp = _fwd_setup_inputs(seed)
    key = _jax.random.fold_in(_jax.random.key(seed), 7919)
    shape, _ = _output_shape()
    out = dict(inp)
    out["loss_target"] = _jax.random.normal(_jax.random.fold_in(key, 0), shape, _jnp.float32)
    for i, name in enumerate(TWIN_WEIGHTS):
        w = inp[name].astype(_jnp.float32)
        if MOMENT_SCALE is None:
            s = _jnp.sqrt(_jnp.mean(_jnp.square(w)) + 1e-30)
        else:
            s = MOMENT_SCALE[name]
        km, kv = _jax.random.split(_jax.random.fold_in(key, i + 1))
        out[name] = w
        out["m_" + name] = s * _jax.random.normal(km, w.shape, _jnp.float32)
        out["v_" + name] = (s * s) * _jax.random.uniform(kv, w.shape, _jnp.float32, 0.5, 1.5)
    if N_MICROBATCH > 1:
        for name, axis in PER_EXAMPLE_BATCH_AXIS.items():
            out[name] = _to_microbatches(out[name], axis)
    return {'x': out['x'], 'g_pre': out['g_pre'], 'w_in': out['w_in'], 'b_gate': out['b_gate'], 'g_q': out['g_q'], 'w_uq': out['w_uq'], 'g_kv': out['g_kv'], 'w_ukv': out['w_ukv'], 'lb_logits': out['lb_logits'], 'g_hgrn': out['g_hgrn'], 'w_branch_a': out['w_branch_a'], 'w_branch_b': out['w_branch_b'], 'w_out': out['w_out'], 'g_post': out['g_post'], 'loss_target': out['loss_target'], 'm_g_pre': out['m_g_pre'], 'm_w_in': out['m_w_in'], 'm_b_gate': out['m_b_gate'], 'm_g_q': out['m_g_q'], 'm_w_uq': out['m_w_uq'], 'm_g_kv': out['m_g_kv'], 'm_w_ukv': out['m_w_ukv'], 'm_lb_logits': out['m_lb_logits'], 'm_g_hgrn': out['m_g_hgrn'], 'm_w_branch_a': out['m_w_branch_a'], 'm_w_branch_b': out['m_w_branch_b'], 'm_w_out': out['m_w_out'], 'm_g_post': out['m_g_post'], 'v_g_pre': out['v_g_pre'], 'v_w_in': out['v_w_in'], 'v_b_gate': out['v_b_gate'], 'v_g_q': out['v_g_q'], 'v_w_uq': out['v_w_uq'], 'v_g_kv': out['v_g_kv'], 'v_w_ukv': out['v_w_ukv'], 'v_lb_logits': out['v_lb_logits'], 'v_g_hgrn': out['v_g_hgrn'], 'v_w_branch_a': out['v_w_branch_a'], 'v_w_branch_b': out['v_w_branch_b'], 'v_w_out': out['v_w_out'], 'v_g_post': out['v_g_post']}


def _loss(weights, diff, rest, loss_target):
    with _jax.named_scope("forward"):
        args = {**rest, TWIN_DIFF_INPUT: diff, **{k: w.astype(_WEIGHT_DTYPES[k]) for k, w in weights.items()}}
        y = _forward(args)
    with _jax.named_scope("loss_head"):
        err = _jnp.square(y.astype(_jnp.float32) - loss_target)
        return 0.5 * _jnp.sum(_jnp.mean(err, axis=-1)) if err.ndim else 0.5 * err


def _adamw(w, g, m, v):
    m = ADAM_B1 * m + (1.0 - ADAM_B1) * g
    v = ADAM_B2 * v + (1.0 - ADAM_B2) * _jnp.square(g)
    m_hat = m / (1.0 - ADAM_B1 ** ADAM_STEP)
    v_hat = v / (1.0 - ADAM_B2 ** ADAM_STEP)
    delta = -ADAM_LR * (m_hat / (_jnp.sqrt(v_hat) + ADAM_EPS) + ADAM_WD * w)
    return delta, m, v


def reference(x, g_pre, w_in, b_gate, g_q, w_uq, g_kv, w_ukv, lb_logits, g_hgrn, w_branch_a, w_branch_b, w_out, g_post, loss_target, m_g_pre, m_w_in, m_b_gate, m_g_q, m_w_uq, m_g_kv, m_w_ukv, m_lb_logits, m_g_hgrn, m_w_branch_a, m_w_branch_b, m_w_out, m_g_post, v_g_pre, v_w_in, v_b_gate, v_g_q, v_w_uq, v_g_kv, v_w_ukv, v_lb_logits, v_g_hgrn, v_w_branch_a, v_w_branch_b, v_w_out, v_g_post):
    given = dict(x=x, g_pre=g_pre, w_in=w_in, b_gate=b_gate, g_q=g_q, w_uq=w_uq, g_kv=g_kv, w_ukv=w_ukv, lb_logits=lb_logits, g_hgrn=g_hgrn, w_branch_a=w_branch_a, w_branch_b=w_branch_b, w_out=w_out, g_post=g_post, loss_target=loss_target, m_g_pre=m_g_pre, m_w_in=m_w_in, m_b_gate=m_b_gate, m_g_q=m_g_q, m_w_uq=m_w_uq, m_g_kv=m_g_kv, m_w_ukv=m_w_ukv, m_lb_logits=m_lb_logits, m_g_hgrn=m_g_hgrn, m_w_branch_a=m_w_branch_a, m_w_branch_b=m_w_branch_b, m_w_out=m_w_out, m_g_post=m_g_post, v_g_pre=v_g_pre, v_w_in=v_w_in, v_b_gate=v_b_gate, v_g_q=v_g_q, v_w_uq=v_w_uq, v_g_kv=v_g_kv, v_w_ukv=v_w_ukv, v_lb_logits=v_lb_logits, v_g_hgrn=v_g_hgrn, v_w_branch_a=v_w_branch_a, v_w_branch_b=v_w_branch_b, v_w_out=v_w_out, v_g_post=v_g_post)
    weights = {n: given[n] for n in TWIN_WEIGHTS}
    shared = {n: given[n] for n in SHARED_INPUTS}
    per_example = {n: given[n] for n in ['x']}
    grad_fn = _jax.value_and_grad(_loss, argnums=(0, 1))

    def one_microbatch(ex, loss_target):
        ex = dict(ex)
        diff = ex.pop(TWIN_DIFF_INPUT)
        return grad_fn(weights, diff, {**shared, **ex}, loss_target)

    if N_MICROBATCH == 1:
        loss, (grad_w, grad_x) = one_microbatch(per_example, given["loss_target"])
    else:
        def body(carry, xs):
            loss_sum, grad_sum = carry
            l_k, (gw_k, gx_k) = one_microbatch(xs[0], xs[1])
            with _jax.named_scope("update"):
                return (loss_sum + l_k, _jax.tree.map(_jnp.add, grad_sum, gw_k)), gx_k

        init = (_jnp.zeros((), _jnp.float32), _jax.tree.map(_jnp.zeros_like, weights))
        (loss, grad_w), grad_x = _jax.lax.scan(body, init, (per_example, given["loss_target"]))
    with _jax.named_scope("update"):
        delta_w, new_m, new_v = {}, {}, {}
        for n in TWIN_WEIGHTS:
            delta_w[n], new_m[n], new_v[n] = _adamw(weights[n], grad_w[n], given["m_" + n], given["v_" + n])
    return (loss, grad_x, *[grad_w[n] for n in TWIN_WEIGHTS], *[delta_w[n] for n in TWIN_WEIGHTS],
            *[new_m[n] for n in TWIN_WEIGHTS], *[new_v[n] for n in TWIN_WEIGHTS])
```

```python
import functools
import math

import numpy as np
import jax
import jax.numpy as jnp
from jax import lax
from jax.experimental import pallas as pl
from jax.experimental.pallas import tpu as pltpu

F32 = jnp.float32
BF16 = jnp.bfloat16
HIGHEST = lax.Precision.HIGHEST

D = 1024
NH = 8
QK_NOPE, QK_ROPE, V_DIM = 64, 32, 64
Q_LORA, KV_LORA = 768, 256
CHUNK = 64
HG_BLOCK = 32
EPS = 1e-6
D_IN = 5664
LANE = 128
P_MERGE, P_GA, P_HQ, P_HF, P_HI, P_GB, P_CQ, P_CKV, P_KPE = 0, 2048, 2560, 3072, 3584, 4096, 4608, 5376, 5632
D_P = 5760
O_CQ, O_CKV, O_KPE, O_GA, O_HQ, O_HF, O_HI, O_GB, O_MERGE = 0, 768, 1024, 1056, 1568, 2080, 2592, 3104, 3616

TM = 256
TQ = 256
TH = 256
VMEM_LIMIT = 56 * 1024 * 1024

ADAM_LR, ADAM_B1, ADAM_B2, ADAM_EPS, ADAM_WD, ADAM_STEP = 0.001, 0.9, 0.999, 1e-08, 0.01, 10

NT_DIMS = (((1,), (1,)), ((), ()))
TN_DIMS = (((0,), (0,)), ((), ()))


def _params(sem):
    return pltpu.CompilerParams(dimension_semantics=sem, vmem_limit_bytes=VMEM_LIMIT)


def _mm(a, b):
    return jnp.dot(a, b, preferred_element_type=F32)


def _mm_nt(a, b):
    return lax.dot_general(a, b, NT_DIMS, preferred_element_type=F32)


def _mm_tn(a, b):
    return lax.dot_general(a, b, TN_DIMS, preferred_element_type=F32)


def _sigmoid(z):
    return jax.nn.sigmoid(z)


def _rope(v, c, s1, s2):
    return v * c + pltpu.roll(v, 112, 1) * s1 + pltpu.roll(v, 16, 1) * s2


def _rope_t(dy, c, s1, s2):
    return dy * c + pltpu.roll(dy * s1, 16, 1) + pltpu.roll(dy * s2, 112, 1)


def _rope_tables(s):
    inv = 10000.0 ** (-jnp.arange(0, QK_ROPE, 2, dtype=F32) / QK_ROPE)
    ang = jnp.arange(s, dtype=F32)[:, None] * inv[None, :]
    cos, sin = jnp.cos(ang), jnp.sin(ang)
    z64, z32, o64, o32 = jnp.zeros((s, 64), F32), jnp.zeros((s, 32), F32), jnp.ones((s, 64), F32), jnp.ones((s, 32), F32)
    z16 = jnp.zeros((s, 16), F32)
    c = jnp.concatenate([o64, cos, cos, o32], axis=1)
    s1 = jnp.concatenate([z64, -sin, z16, z32], axis=1)
    s2 = jnp.concatenate([z64, z16, sin, z32], axis=1)
    return c, s1, s2


def _front_fwd(x, g_pre, w_in_p):
    s = x.shape[0]
    nb, wn = 3, D_P // 3

    def body(x_ref, g_ref, w_ref, o_ref):
        xv = x_ref[...]
        r = lax.rsqrt(jnp.mean(xv * xv, axis=-1, keepdims=True) + EPS)
        h = ((xv * r) * g_ref[...]).astype(BF16)
        o_ref[...] = _mm(h, w_ref[...])

    return pl.pallas_call(
        body, name="front_fwd", grid=(nb, s // TM),
        in_specs=[pl.BlockSpec((TM, D), lambda n, i: (i, 0)), pl.BlockSpec((1, D), lambda n, i: (0, 0)),
                  pl.BlockSpec((D, wn), lambda n, i: (0, n))],
        out_specs=pl.BlockSpec((TM, wn), lambda n, i: (i, n)),
        out_shape=jax.ShapeDtypeStruct((s, D_P), F32),
        compiler_params=_params(("parallel", "parallel")),
    )(x, g_pre, w_in_p)


def _norm_rows(v, g):
    r = lax.rsqrt(jnp.mean(v * v, axis=-1, keepdims=True) + EPS)
    return (v * r) * g, r


def _qkv_fwd(proj, g_q, g_kv, w_uq_p, w_k_p, w_v_p, rc, rs1, rs2):
    s = proj.shape[0]

    def body(cq_ref, ckv_ref, kpe_ref, gq_ref, gkv_ref, wq_ref, wk_ref, wv_ref, c_ref, s1_ref, s2_ref, q_ref, k_ref, v_ref):
        c, s1, s2 = c_ref[...], s1_ref[...], s2_ref[...]
        cqn, _ = _norm_rows(cq_ref[...], gq_ref[...])
        ckvn, _ = _norm_rows(ckv_ref[...], gkv_ref[...])
        ckvn = ckvn.astype(BF16)
        qf = _mm(cqn.astype(BF16), wq_ref[...])
        kf = _mm(ckvn, wk_ref[...])
        vf = _mm(ckvn, wv_ref[...])
        kpe = _rope(kpe_ref[...], c, s1, s2)
        for h in range(NH):
            blk = slice(h * LANE, (h + 1) * LANE)
            q_ref[h] = _rope(qf[:, blk], c, s1, s2).astype(BF16)
            k_ref[h] = (kf[:, blk] + kpe).astype(BF16)
            v_ref[h] = vf[:, blk].astype(BF16)

    row = lambda w, j: pl.BlockSpec((TM, w), lambda i: (i, j))
    full = lambda a: pl.BlockSpec(a.shape, lambda i: (0,) * a.ndim)
    hs = jax.ShapeDtypeStruct((NH, s, LANE), BF16)
    return pl.pallas_call(
        body, name="qkv_fwd", grid=(s // TM,),
        in_specs=[row(Q_LORA, P_CQ // Q_LORA), row(KV_LORA, P_CKV // KV_LORA), row(LANE, P_KPE // LANE),
                  full(g_q), full(g_kv), full(w_uq_p), full(w_k_p), full(w_v_p), row(LANE, 0), row(LANE, 0), row(LANE, 0)],
        out_specs=[pl.BlockSpec((NH, TM, LANE), lambda i: (0, i, 0))] * 3,
        out_shape=[hs, hs, hs],
        compiler_params=_params(("parallel",)),
    )(proj, proj, proj, g_q, g_kv, w_uq_p, w_k_p, w_v_p, rc, rs1, rs2)


def _visible(i, t, tq, tk):
    row = i * tq + lax.broadcasted_iota(jnp.int32, (tq, tk), 0)
    col = t * tk + lax.broadcasted_iota(jnp.int32, (tq, tk), 1)
    return (col // CHUNK) <= (row // CHUNK)


def _attn_fwd(q, k, vv):
    s = q.shape[1]
    scale = 1.0 / math.sqrt(QK_NOPE + QK_ROPE)

    def body(q_ref, k_ref, v_ref, o_ref, lse_ref):
        i = pl.program_id(1)
        out = jnp.zeros((TQ, LANE), F32)
        for hh in range(2):
            qv = q_ref[hh]

            def step(t, carry, hh=hh, qv=qv):
                m, l, acc = carry
                kt = k_ref[hh, pl.ds(pl.multiple_of(t * TQ, TQ), TQ), :]
                vt = v_ref[hh, pl.ds(pl.multiple_of(t * TQ, TQ), TQ), :]
                sc = _mm_nt(qv, kt) * scale
                sc = jnp.where(_visible(i, t, TQ, TQ), sc, -jnp.inf)
                m_new = jnp.maximum(m, jnp.max(sc, axis=-1, keepdims=True))
                alpha = jnp.exp(m - m_new)
                p = jnp.exp(sc - m_new)
                l = alpha * l + jnp.sum(p, axis=-1, keepdims=True)
                acc = alpha * acc + _mm(p.astype(BF16), vt)
                return m_new, l, acc

            m0 = jnp.full((TQ, 1), -jnp.inf, F32)
            m, l, acc = lax.fori_loop(0, i + 1, step, (m0, jnp.zeros((TQ, 1), F32), jnp.zeros((TQ, LANE), F32)))
            out = out + acc / l
            lse_ref[hh] = jnp.broadcast_to(m + jnp.log(l), (TQ, LANE))
        o_ref[...] = out

    return pl.pallas_call(
        body, name="attn_fwd", grid=(NH // 2, s // TQ),
        in_specs=[pl.BlockSpec((2, TQ, LANE), lambda p, i: (p, i, 0)), pl.BlockSpec((2, s, LANE), lambda p, i: (p, 0, 0)),
                  pl.BlockSpec((2, s, LANE), lambda p, i: (p, 0, 0))],
        out_specs=[pl.BlockSpec((TQ, LANE), lambda p, i: (i, p)), pl.BlockSpec((2, TQ, LANE), lambda p, i: (p, i, 0))],
        out_shape=[jax.ShapeDtypeStruct((s, NH * V_DIM), F32), jax.ShapeDtypeStruct((NH, s, LANE), F32)],
        compiler_params=_params(("parallel", "parallel")),
    )(q, k, vv)


def _lower_bound(lbl):
    a0, a1 = lbl[0:1, :], lbl[1:2, :]
    mx = jnp.maximum(a0, a1)
    e0, e1 = jnp.exp(a0 - mx), jnp.exp(a1 - mx)
    return e0 / (e0 + e1)


def _hgrn_gates(hq, hf, lb):
    sig = _sigmoid(hf)
    f = lb + (1.0 - lb) * sig
    g = jnp.log(f)
    kk = 1.0 - f
    r = lax.broadcasted_iota(jnp.int32, (TH, TH), 0)
    c = lax.broadcasted_iota(jnp.int32, (TH, TH), 1)
    same = (r // HG_BLOCK) == (c // HG_BLOCK)
    tri = same & (r >= c)
    cum = jnp.dot(tri.astype(F32), g, precision=HIGHEST, preferred_element_type=F32)
    lastb = jnp.dot(same.astype(F32), g, precision=HIGHEST, preferred_element_type=F32)
    e, ei, ee = jnp.exp(cum), jnp.exp(-cum), jnp.exp(lastb - cum)
    return dict(sig=sig, f=f, kk=kk, same=same, tri=tri, r=r, c=c, cum=cum, lastb=lastb, e=e, ei=ei, ee=ee,
                qd=hq * e, ki=kk * ei, ke=kk * ee)


def _pair_masks():
    lane = lax.broadcasted_iota(jnp.int32, (TH, LANE), 1)
    kr = lax.broadcasted_iota(jnp.int32, (LANE, LANE), 0)
    kc = lax.broadcasted_iota(jnp.int32, (LANE, LANE), 1)
    return lane < 64, (kr // 64) == (kc // 64)


def _hgrn_fwd(proj, lbl):
    s = proj.shape[0]
    nch = TH // HG_BLOCK

    def body(hq_ref, hf_ref, hi_ref, lbl_ref, o_ref, st_ref, st):
        @pl.when(pl.program_id(1) == 0)
        def _():
            st[...] = jnp.zeros_like(st)

        lb = _lower_bound(lbl_ref[...])
        gt = _hgrn_gates(hq_ref[...], hf_ref[...], lb)
        m0, bd = _pair_masks()
        v_b = hi_ref[...].astype(BF16)
        qd, ki_b, ke_b = gt["qd"], gt["ki"].astype(BF16), gt["ke"].astype(BF16)
        qd_b = qd.astype(BF16)
        o = jnp.zeros((TH, LANE), F32)
        for hh in range(2):
            mh = m0 if hh == 0 else jnp.logical_not(m0)
            a = jnp.where(gt["tri"], _mm_nt(jnp.where(mh, qd, 0.0).astype(BF16), ki_b), 0.0)
            o = jnp.where(mh, _mm(a.astype(BF16), v_b), o)
        for n in range(nch):
            sl = slice(n * HG_BLOCK, (n + 1) * HG_BLOCK)
            cur = st[...]
            st_ref[0, n] = cur
            o_ref[sl, :] = o[sl] + _mm_nt(qd_b[sl], cur.astype(BF16))
            upd = jnp.where(bd, _mm_tn(v_b[sl], ke_b[sl]), 0.0)
            st[...] = jnp.exp(gt["lastb"][n * HG_BLOCK:n * HG_BLOCK + 1, :]) * cur + upd

    col = lambda base: pl.BlockSpec((TH, LANE), lambda p, i: (i, base // LANE + p))
    return pl.pallas_call(
        body, name="hgrn_fwd", grid=(NH // 2, s // TH),
        in_specs=[col(P_HQ), col(P_HF), col(P_HI), pl.BlockSpec((2, LANE), lambda p, i: (0, p))],
        out_specs=[pl.BlockSpec((TH, LANE), lambda p, i: (i, p)),
                   pl.BlockSpec((1, nch, LANE, LANE), lambda p, i: (p, i, 0, 0))],
        out_shape=[jax.ShapeDtypeStruct((s, 512), F32), jax.ShapeDtypeStruct((NH // 2, s // HG_BLOCK, LANE, LANE), F32)],
        scratch_shapes=[pltpu.VMEM((LANE, LANE), F32)],
        compiler_params=_params(("parallel", "arbitrary")),
    )(proj, proj, proj, lbl)


def _group_sum(v):
    r = lax.broadcasted_iota(jnp.int32, (512, 512), 0)
    c = lax.broadcasted_iota(jnp.int32, (512, 512), 1)
    return jnp.dot(v, ((r // 64) == (c // 64)).astype(F32), precision=HIGHEST, preferred_element_type=F32)


def _dsilu(z, sg):
    return sg * (1.0 + z * (1.0 - sg))


def _mid(proj, attn, o_raw, x, tgt, g_hg, b_gate, g_post, wa, wb, w_out):
    s = x.shape[0]

    def body(attn_ref, ga_ref, o_ref, gb_ref, mg_ref, x_ref, t_ref, ghg_ref, bg_ref, gp_ref, wa_ref, wb_ref, wo_ref,
             loss_ref, dout_ref, dattn_ref, dga_ref, dor_ref, dgb_ref, dmg_ref, dwo_ref, dwa_ref, dwb_ref, dgp_ref, dbg_ref, dghg_ref):
        first = pl.program_id(0) == 0

        @pl.when(first)
        def _():
            for rf in (loss_ref, dwo_ref, dwa_ref, dwb_ref, dgp_ref, dbg_ref, dghg_ref):
                rf[...] = jnp.zeros_like(rf)

        attn, za, orw, zb = attn_ref[...], ga_ref[...], o_ref[...], gb_ref[...]
        ghg, gp = ghg_ref[...], gp_ref[...]
        sga, sgb = _sigmoid(za), _sigmoid(zb)
        sa, sb = za * sga, zb * sgb
        ga = attn * sa
        rh = lax.rsqrt(_group_sum(orw * orw) * (1.0 / V_DIM) + EPS)
        on = (orw * rh) * ghg
        gb = on * sb
        ga_b, gb_b = ga.astype(BF16), gb.astype(BF16)
        ya = _mm(ga_b, wa_ref[...])
        yb = _mm(gb_b, wb_ref[...])
        gates = _sigmoid(mg_ref[...] + bg_ref[...])
        g0, g1 = gates[:, :D], gates[:, D:]
        m_b = (g0 * ya + g1 * yb).astype(BF16)
        y = _mm(m_b, wo_ref[...])
        ry = lax.rsqrt(jnp.mean(y * y, axis=-1, keepdims=True) + EPS)
        out = x_ref[...] + (y * ry) * gp
        err = out - t_ref[...]
        loss_ref[...] += 0.5 * jnp.sum(jnp.mean(err * err, axis=-1, keepdims=True), axis=0, keepdims=True)
        dout = err * (1.0 / D)
        dout_ref[...] = dout
        dgp_ref[...] += jnp.sum(dout * (y * ry), axis=0, keepdims=True)
        dgy = dout * gp
        dy = ry * dgy - y * (ry * ry * ry) * jnp.mean(y * dgy, axis=-1, keepdims=True)
        dy_b = dy.astype(BF16)
        dwo_ref[...] += _mm_tn(m_b, dy_b)
        dm = _mm_nt(dy_b, wo_ref[...])
        dya, dyb = dm * g0, dm * g1
        dg0, dg1 = dm * ya, dm * yb
        dmg = jnp.concatenate([dg0 * g0 * (1.0 - g0), dg1 * g1 * (1.0 - g1)], axis=1)
        dmg_ref[...] = dmg.astype(BF16)
        dbg_ref[...] += jnp.sum(dmg, axis=0, keepdims=True)
        dya_b, dyb_b = dya.astype(BF16), dyb.astype(BF16)
        dwa_ref[...] += _mm_tn(ga_b, dya_b)
        dwb_ref[...] += _mm_tn(gb_b, dyb_b)
        dga = _mm_nt(dya_b, wa_ref[...])
        dgb = _mm_nt(dyb_b, wb_ref[...])
        dattn_ref[...] = dga * sa
        dga_ref[...] = (dga * attn * _dsilu(za, sga)).astype(BF16)
        dgb_ref[...] = (dgb * on * _dsilu(zb, sgb)).astype(BF16)
        don = dgb * sb
        dghg_ref[...] += jnp.sum(don * (orw * rh), axis=0, keepdims=True)
        dgo = don * ghg
        dor_ref[...] = rh * dgo - orw * (rh * rh * rh) * (_group_sum(orw * dgo) * (1.0 / V_DIM))

    row = lambda w, j=0: pl.BlockSpec((TM, w), lambda i: (i, j))
    full = lambda a: pl.BlockSpec(a.shape, lambda i: (0,) * a.ndim)
    acc = lambda shape: pl.BlockSpec(shape, lambda i: (0, 0))
    sds = jax.ShapeDtypeStruct
    return pl.pallas_call(
        body, name="mid", grid=(s // TM,),
        in_specs=[row(512), row(512, P_GA // 512), row(512), row(512, P_GB // 512), row(2048, P_MERGE // 2048), row(D), row(D),
                  full(g_hg), full(b_gate), full(g_post), full(wa), full(wb), full(w_out)],
        out_specs=[acc((1, 1)), row(D), row(512), row(512), row(512), row(512), row(2048),
                   acc((D, D)), acc((512, D)), acc((512, D)), acc((1, D)), acc((1, 2048)), acc((1, 512))],
        out_shape=[sds((1, 1), F32), sds((s, D), F32), sds((s, 512), F32), sds((s, 512), BF16), sds((s, 512), F32), sds((s, 512), BF16),
                   sds((s, 2048), BF16), sds((D, D), F32), sds((512, D), F32), sds((512, D), F32), sds((1, D), F32),
                   sds((1, 2048), F32), sds((1, 512), F32)],
        compiler_params=_params(("arbitrary",)),
    )(attn, proj, o_raw, proj, proj, x, tgt, g_hg, b_gate, g_post, wa, wb, w_out)


def _attn_bwd(q, k, vv, attn, dattn, lse):
    s = q.shape[1]
    nt = s // TQ
    scale = 1.0 / math.sqrt(QK_NOPE + QK_ROPE)

    def body(q_ref, k_ref, v_ref, o_ref, do_ref, lse_ref, dq_ref, dk_ref, dv_ref):
        j = pl.program_id(1)

        @pl.when(j == 0)
        def _():
            dq_ref[...] = jnp.zeros_like(dq_ref)

        lane = lax.broadcasted_iota(jnp.int32, (TQ, LANE), 1)
        for hh in range(2):
            mh = (lane < 64) if hh == 0 else (lane >= 64)
            kj, vj = k_ref[hh], v_ref[hh]

            def step(i, carry, hh=hh, mh=mh, kj=kj, vj=vj):
                dk, dv = carry
                rows = pl.ds(pl.multiple_of(i * TQ, TQ), TQ)
                qi = q_ref[hh, rows, :]
                do = jnp.where(mh, do_ref[rows, :], 0.0)
                delta = jnp.sum(do * o_ref[rows, :], axis=-1, keepdims=True)
                lse_i = lse_ref[hh, rows, :]
                sc = _mm_nt(qi, kj) * scale
                p = jnp.where(_visible(i, j, TQ, TQ), jnp.exp(sc - jnp.concatenate([lse_i] * (TQ // LANE), axis=1)), 0.0)
                do_b = do.astype(BF16)
                dv = dv + _mm_tn(p.astype(BF16), do_b)
                dp = _mm_nt(do_b, vj)
                ds_b = ((p * (dp - delta)) * scale).astype(BF16)
                dk = dk + _mm_tn(ds_b, qi)
                dq_ref[hh, rows, :] += _mm(ds_b, kj)
                return dk, dv

            z = jnp.zeros((TQ, LANE), F32)
            dk, dv = lax.fori_loop(j, nt, step, (z, z))
            dk_ref[hh] = dk
            dv_ref[hh] = dv

    whole = pl.BlockSpec((2, s, LANE), lambda p, j: (p, 0, 0))
    tile = pl.BlockSpec((2, TQ, LANE), lambda p, j: (p, j, 0))
    cols = pl.BlockSpec((s, LANE), lambda p, j: (0, p))
    hs = jax.ShapeDtypeStruct((NH, s, LANE), F32)
    return pl.pallas_call(
        body, name="attn_bwd", grid=(NH // 2, nt),
        in_specs=[whole, tile, tile, cols, cols, whole],
        out_specs=[whole, tile, tile],
        out_shape=[hs, hs, hs],
        compiler_params=_params(("parallel", "arbitrary")),
    )(q, k, vv, attn, dattn, lse)


def _hgrn_bwd(proj, lbl, states, do_raw):
    s = proj.shape[0]
    nt = s // TH
    nch = TH // HG_BLOCK

    def body(hq_ref, hf_ref, hi_ref, lbl_ref, st_ref, do_ref, dhq_ref, dhf_ref, dhi_ref, dlbl_ref, dst, dlb):
        step = pl.program_id(1)

        @pl.when(step == 0)
        def _():
            dst[...] = jnp.zeros_like(dst)
            dlb[...] = jnp.zeros_like(dlb)

        lb = _lower_bound(lbl_ref[...])
        gt = _hgrn_gates(hq_ref[...], hf_ref[...], lb)
        m0, bd = _pair_masks()
        do = do_ref[...]
        qd, ki, ke = gt["qd"], gt["ki"], gt["ke"]
        v_b, do_b = hi_ref[...].astype(BF16), do.astype(BF16)
        qd_b, ki_b, ke_b = qd.astype(BF16), ki.astype(BF16), ke.astype(BF16)
        dv = jnp.zeros((TH, LANE), F32)
        dqd = jnp.zeros((TH, LANE), F32)
        dki = jnp.zeros((TH, LANE), F32)
        for hh in range(2):
            mh = m0 if hh == 0 else jnp.logical_not(m0)
            a_b = jnp.where(gt["tri"], _mm_nt(jnp.where(mh, qd, 0.0).astype(BF16), ki_b), 0.0).astype(BF16)
            doh_b = jnp.where(mh, do, 0.0).astype(BF16)
            da_b = jnp.where(gt["tri"], _mm_nt(doh_b, v_b), 0.0).astype(BF16)
            dv = dv + _mm_tn(a_b, doh_b)
            dqd = jnp.where(mh, _mm(da_b, ki_b), dqd)
            dki = jnp.where(mh, _mm_tn(da_b, qd_b), dki)
        dv_x, dke_x, dqd_x, dlast_x = [None] * nch, [None] * nch, [None] * nch, [None] * nch
        for n in reversed(range(nch)):
            sl = slice(n * HG_BLOCK, (n + 1) * HG_BLOCK)
            ds = dst[...]
            ds_b = ds.astype(BF16)
            sn = st_ref[0, n]
            dv_x[n] = _mm_nt(ke_b[sl], ds_b)
            dke_n = _mm(v_b[sl], ds_b)
            dke_x[n] = dke_n
            dqd_x[n] = _mm(do_b[sl], sn.astype(BF16))
            dn = jnp.exp(gt["lastb"][n * HG_BLOCK:n * HG_BLOCK + 1, :])
            dlast = jnp.sum(dke_n * ke[sl], axis=0, keepdims=True) + jnp.sum(ds * sn, axis=0, keepdims=True) * dn
            dlast_x[n] = jnp.broadcast_to(dlast, (HG_BLOCK, LANE))
            dst[...] = dn * ds + jnp.where(bd, _mm_tn(do_b[sl], qd_b[sl]), 0.0)
        dv = dv + jnp.concatenate(dv_x, axis=0)
        dqd = dqd + jnp.concatenate(dqd_x, axis=0)
        dke = jnp.concatenate(dke_x, axis=0)
        dk = dki * gt["ei"] + dke * gt["ee"]
        dcum = dqd * qd - dki * ki - dke * ke
        tri_t = gt["same"] & (gt["r"] <= gt["c"])
        dg = jnp.dot(tri_t.astype(F32), dcum, precision=HIGHEST, preferred_element_type=F32) + jnp.concatenate(dlast_x, axis=0)
        sig = gt["sig"]
        df = dg / gt["f"] - dk
        dlb[...] += jnp.sum(df * (1.0 - sig), axis=0, keepdims=True)
        dhq_ref[...] = (dqd * gt["e"]).astype(BF16)
        dhf_ref[...] = ((df * (1.0 - lb)) * sig * (1.0 - sig)).astype(BF16)
        dhi_ref[...] = dv.astype(BF16)

        @pl.when(step == nt - 1)
        def _():
            da0 = dlb[...] * lb * (1.0 - lb)
            dlbl_ref[...] = jnp.concatenate([da0, -da0], axis=0)

    col = lambda base: pl.BlockSpec((TH, LANE), lambda p, i: (nt - 1 - i, base // LANE + p))
    tile = pl.BlockSpec((TH, LANE), lambda p, i: (nt - 1 - i, p))
    sds = jax.ShapeDtypeStruct
    return pl.pallas_call(
        body, name="hgrn_bwd", grid=(NH // 2, nt),
        in_specs=[col(P_HQ), col(P_HF), col(P_HI), pl.BlockSpec((2, LANE), lambda p, i: (0, p)),
                  pl.BlockSpec((1, nch, LANE, LANE), lambda p, i: (p, nt - 1 - i, 0, 0)), tile],
        out_specs=[tile, tile, tile, pl.BlockSpec((2, LANE), lambda p, i: (0, p))],
        out_shape=[sds((s, 512), BF16), sds((s, 512), BF16), sds((s, 512), BF16), sds((2, 512), F32)],
        scratch_shapes=[pltpu.VMEM((LANE, LANE), F32), pltpu.VMEM((1, LANE), F32)],
        compiler_params=_params(("parallel", "arbitrary")),
    )(proj, proj, proj, lbl, states, do_raw)


def _norm_rows_bwd(v, r, g, dn):
    dgv = dn * g
    return r * dgv - v * (r * r * r) * jnp.mean(v * dgv, axis=-1, keepdims=True)


def _qkv_bwd(proj, dq, dk, dvv, g_q, g_kv, w_uq_p, w_k_p, w_v_p, rc, rs1, rs2):
    s = proj.shape[0]

    def body(cq_ref, ckv_ref, dq_ref, dk_ref, dv_ref, gq_ref, gkv_ref, wq_ref, wk_ref, wv_ref, c_ref, s1_ref, s2_ref,
             dcq_ref, dckv_ref, dkpe_ref, dwq_ref, dwk_ref, dwv_ref, dgq_ref, dgkv_ref):
        @pl.when(pl.program_id(0) == 0)
        def _():
            for rf in (dwq_ref, dwk_ref, dwv_ref, dgq_ref, dgkv_ref):
                rf[...] = jnp.zeros_like(rf)

        c, s1, s2 = c_ref[...], s1_ref[...], s2_ref[...]
        cq, ckv = cq_ref[...], ckv_ref[...]
        gq, gkv = gq_ref[...], gkv_ref[...]
        cqn, rq = _norm_rows(cq, gq)
        ckvn, rkv = _norm_rows(ckv, gkv)
        cqn_b, ckvn_b = cqn.astype(BF16), ckvn.astype(BF16)
        dqf = jnp.concatenate([_rope_t(dq_ref[h], c, s1, s2) for h in range(NH)], axis=1).astype(BF16)
        dkf = jnp.concatenate([dk_ref[h] for h in range(NH)], axis=1).astype(BF16)
        dvf = jnp.concatenate([dv_ref[h] for h in range(NH)], axis=1).astype(BF16)
        dkpe = dk_ref[0]
        for h in range(1, NH):
            dkpe = dkpe + dk_ref[h]
        lane = lax.broadcasted_iota(jnp.int32, (TM, LANE), 1)
        dkpe = jnp.where((lane >= QK_NOPE) & (lane < QK_NOPE + QK_ROPE), dkpe, 0.0)
        dkpe_ref[...] = _rope_t(dkpe, c, s1, s2).astype(BF16)
        dwq_ref[...] += _mm_tn(cqn_b, dqf)
        dwk_ref[...] += _mm_tn(ckvn_b, dkf)
        dwv_ref[...] += _mm_tn(ckvn_b, dvf)
        dcqn = _mm_nt(dqf, wq_ref[...])
        dckvn = _mm_nt(dkf, wk_ref[...]) + _mm_nt(dvf, wv_ref[...])
        dgq_ref[...] += jnp.sum(dcqn * (cq * rq), axis=0, keepdims=True)
        dgkv_ref[...] += jnp.sum(dckvn * (ckv * rkv), axis=0, keepdims=True)
        dcq_ref[...] = _norm_rows_bwd(cq, rq, gq, dcqn).astype(BF16)
        dckv_ref[...] = _norm_rows_bwd(ckv, rkv, gkv, dckvn).astype(BF16)

    row = lambda w, j=0: pl.BlockSpec((TM, w), lambda i: (i, j))
    full = lambda a: pl.BlockSpec(a.shape, lambda i: (0,) * a.ndim)
    acc = lambda shape: pl.BlockSpec(shape, lambda i: (0, 0))
    heads = pl.BlockSpec((NH, TM, LANE), lambda i: (0, i, 0))
    sds = jax.ShapeDtypeStruct
    return pl.pallas_call(
        body, name="qkv_bwd", grid=(s // TM,),
        in_specs=[row(Q_LORA, P_CQ // Q_LORA), row(KV_LORA, P_CKV // KV_LORA), heads, heads, heads,
                  full(g_q), full(g_kv), full(w_uq_p), full(w_k_p), full(w_v_p), row(LANE), row(LANE), row(LANE)],
        out_specs=[row(Q_LORA), row(KV_LORA), row(LANE), acc((Q_LORA, D)), acc((KV_LORA, D)), acc((KV_LORA, D)),
                   acc((1, Q_LORA)), acc((1, KV_LORA))],
        out_shape=[sds((s, Q_LORA), BF16), sds((s, KV_LORA), BF16), sds((s, LANE), BF16), sds((Q_LORA, D), F32),
                   sds((KV_LORA, D), F32), sds((KV_LORA, D), F32), sds((1, Q_LORA), F32), sds((1, KV_LORA), F32)],
        compiler_params=_params(("arbitrary",)),
    )(proj, proj, dq, dk, dvv, g_q, g_kv, w_uq_p, w_k_p, w_v_p, rc, rs1, rs2)


def _front_bwd(x, dout, dproj, g_pre, w_in_p):
    s = x.shape[0]

    def body(x_ref, do_ref, dp_ref, g_ref, w_ref, gx_ref, h_ref, dg_ref):
        @pl.when(pl.program_id(0) == 0)
        def _():
            dg_ref[...] = jnp.zeros_like(dg_ref)

        xv, g = x_ref[...], g_ref[...]
        hn, r = _norm_rows(xv, g)
        h_ref[...] = hn.astype(BF16)
        dh = _mm_nt(dp_ref[...], w_ref[...])
        dg_ref[...] += jnp.sum(dh * (xv * r), axis=0, keepdims=True)
        gx_ref[...] = do_ref[...] + _norm_rows_bwd(xv, r, g, dh)

    row = lambda w: pl.BlockSpec((TM, w), lambda i: (i, 0))
    full = lambda a: pl.BlockSpec(a.shape, lambda i: (0,) * a.ndim)
    sds = jax.ShapeDtypeStruct
    return pl.pallas_call(
        body, name="front_bwd", grid=(s // TM,),
        in_specs=[row(D), row(D), row(D_P), full(g_pre), full(w_in_p)],
        out_specs=[row(D), row(D), pl.BlockSpec((1, D), lambda i: (0, 0))],
        out_shape=[sds((s, D), F32), sds((s, D), BF16), sds((1, D), F32)],
        compiler_params=_params(("arbitrary",)),
    )(x, dout, dproj, g_pre, w_in_p)


def _win_grad(h, dproj):
    s = h.shape[0]
    tn, tk = 1152, 512

    def body(h_ref, d_ref, o_ref):
        @pl.when(pl.program_id(1) == 0)
        def _():
            o_ref[...] = jnp.zeros_like(o_ref)

        o_ref[...] += _mm_tn(h_ref[...], d_ref[...])

    return pl.pallas_call(
        body, name="win_grad", grid=(D_P // tn, s // tk),
        in_specs=[pl.BlockSpec((tk, D), lambda n, kk: (kk, 0)), pl.BlockSpec((tk, tn), lambda n, kk: (kk, n))],
        out_specs=pl.BlockSpec((D, tn), lambda n, kk: (0, n)),
        out_shape=jax.ShapeDtypeStruct((D, D_P), F32),
        compiler_params=_params(("parallel", "arbitrary")),
    )(h, dproj)


def _pad_win(w_in):
    z = lambda n: jnp.zeros((w_in.shape[0], n), w_in.dtype)
    sl = lambda o, n: w_in[:, o:o + n]
    return jnp.concatenate([sl(O_MERGE, 2048), sl(O_GA, 512), sl(O_HQ, 512), sl(O_HF, 512), sl(O_HI, 512), sl(O_GB, 512),
                            sl(O_CQ, Q_LORA), sl(O_CKV, KV_LORA), z(64), sl(O_KPE, QK_ROPE), z(32)], axis=1)


def _unpad_win(g):
    sl = lambda o, n: g[:, o:o + n]
    return jnp.concatenate([sl(P_CQ, Q_LORA), sl(P_CKV, KV_LORA), sl(P_KPE + 64, QK_ROPE), sl(P_GA, 512), sl(P_HQ, 512),
                            sl(P_HF, 512), sl(P_HI, 512), sl(P_GB, 512), sl(P_MERGE, 2048)], axis=1)


def _pad_wuq(w_uq):
    w = w_uq.reshape(Q_LORA, NH, QK_NOPE + QK_ROPE)
    return jnp.pad(w, ((0, 0), (0, 0), (0, LANE - QK_NOPE - QK_ROPE))).reshape(Q_LORA, NH * LANE)


def _unpad_wuq(g):
    return g.reshape(Q_LORA, NH, LANE)[:, :, :QK_NOPE + QK_ROPE].reshape(Q_LORA, NH * (QK_NOPE + QK_ROPE))


def _pad_wukv(w_ukv):
    w = w_ukv.reshape(KV_LORA, NH, QK_NOPE + V_DIM)
    w_k = jnp.pad(w[:, :, :QK_NOPE], ((0, 0), (0, 0), (0, LANE - QK_NOPE))).reshape(KV_LORA, NH * LANE)
    wv = w[:, :, QK_NOPE:].reshape(KV_LORA, NH // 2, 2, 1, V_DIM)
    eye = jnp.eye(2, dtype=w.dtype).reshape(1, 1, 2, 2, 1)
    return w_k, (wv * eye).reshape(KV_LORA, NH * LANE)


def _unpad_wukv(gk, gv):
    gk = gk.reshape(KV_LORA, NH, LANE)[:, :, :QK_NOPE]
    gv = gv.reshape(KV_LORA, NH // 2, 2, 2, V_DIM)
    gv = jnp.stack([gv[:, :, 0, 0], gv[:, :, 1, 1]], axis=2).reshape(KV_LORA, NH, V_DIM)
    return jnp.concatenate([gk, gv], axis=-1).reshape(KV_LORA, NH * (QK_NOPE + V_DIM))


def _local_step(x, tgt, g_pre, w_in, b_gate, g_q, w_uq, g_kv, w_ukv, lb_logits, g_hgrn, wa, wb, w_out, g_post):
    s = x.shape[0]
    w_in_p = _pad_win(w_in)
    w_uq_p = _pad_wuq(w_uq)
    w_k_p, w_v_p = _pad_wukv(w_ukv)
    rc, rs1, rs2 = _rope_tables(s)
    g_hg = jnp.tile(g_hgrn, (1, NH))

    proj = _front_fwd(x, g_pre, w_in_p)
    q, k, vv = _qkv_fwd(proj, g_q, g_kv, w_uq_p, w_k_p, w_v_p, rc, rs1, rs2)
    attn, lse = _attn_fwd(q, k, vv)
    o_raw, states = _hgrn_fwd(proj, lb_logits)
    (loss, dout, dattn, dga, dor, dgb, dmg, d_wout, d_wa, d_wb, d_gpost, d_bgate, d_ghg) = _mid(
        proj, attn, o_raw, x, tgt, g_hg, b_gate, g_post, wa, wb, w_out)
    dq, dk, dvv = _attn_bwd(q, k, vv, attn, dattn, lse)
    dhq, dhf, dhi, d_lbl = _hgrn_bwd(proj, lb_logits, states, dor)
    dcq, dckv, dkpe, d_wuq_p, d_wk_p, d_wv_p, d_gq, d_gkv = _qkv_bwd(proj, dq, dk, dvv, g_q, g_kv, w_uq_p, w_k_p, w_v_p, rc, rs1, rs2)
    dproj = jnp.concatenate([dmg, dga, dhq, dhf, dhi, dgb, dcq, dckv, dkpe], axis=1)
    grad_x, h, d_gpre = _front_bwd(x, dout, dproj, g_pre, w_in_p)
    d_win_p = _win_grad(h, dproj)
    grads = dict(g_pre=d_gpre, w_in=_unpad_win(d_win_p), b_gate=d_bgate, g_q=d_gq, w_uq=_unpad_wuq(d_wuq_p), g_kv=d_gkv,
                 w_ukv=_unpad_wukv(d_wk_p, d_wv_p), lb_logits=d_lbl, g_hgrn=d_ghg.reshape(NH, V_DIM).sum(axis=0, keepdims=True),
                 w_branch_a=d_wa, w_branch_b=d_wb, w_out=d_wout, g_post=d_gpost)
    return loss, grad_x, grads


SHARD_SHAPES = (("w_in", (1024, 1416)), ("w_uq", (192, 768)), ("w_ukv", (256, 256)), ("w_branch_a", (512, 256)),
                ("w_branch_b", (512, 256)), ("w_out", (256, 1024)))
ROW_SHARDED = ("w_uq", "w_out")
VECTORS = (("g_pre", 1024), ("b_gate", 2048), ("g_q", 768), ("g_kv", 256), ("lb_logits", 1024), ("g_hgrn", 64), ("g_post", 1024))
BIG_ROWS = sum(a * b for _, (a, b) in SHARD_SHAPES) // LANE
VEC_ROWS = 64
PACK_ROWS = BIG_ROWS + VEC_ROWS
N_CHIPS = 4


def _pack_vectors(vals):
    flat = jnp.concatenate([vals[n].reshape(-1) for n, _ in VECTORS])
    return jnp.pad(flat, (0, VEC_ROWS * LANE - flat.shape[0])).reshape(VEC_ROWS, LANE)


def _pack_shards(vals):
    return jnp.concatenate([vals[n].reshape(-1, LANE) for n, _ in SHARD_SHAPES] + [_pack_vectors(vals)], axis=0)


def _unpack_shards(pack):
    out, r = {}, 0
    for n, (a, b) in SHARD_SHAPES:
        rows = a * b // LANE
        out[n] = pack[r:r + rows].reshape(1, a, b)
        r += rows
    flat, o = pack[r:].reshape(-1), 0
    for n, size in VECTORS:
        out[n] = flat[o:o + size].reshape((2, 512) if n == "lb_logits" else (1, size))
        o += size
    return out


def _split_by_chip(full):
    parts = []
    for n, (a, b) in SHARD_SHAPES:
        g = full[n]
        if n in ROW_SHARDED:
            g = g.reshape(N_CHIPS, a, b)
        else:
            g = g.reshape(a, N_CHIPS, b).transpose(1, 0, 2)
        parts.append(g.reshape(N_CHIPS, -1, LANE))
    return jnp.concatenate(parts, axis=1)


def _join_chips(packs):
    out, r = {}, 0
    for n, (a, b) in SHARD_SHAPES:
        rows = a * b // LANE
        w = packs[:, r:r + rows].reshape(N_CHIPS, a, b)
        out[n] = w.reshape(N_CHIPS * a, b) if n in ROW_SHARDED else w.transpose(1, 0, 2).reshape(a, N_CHIPS * b)
        r += rows
    return out


MESH = pl.DeviceIdType.MESH
HBM = pl.BlockSpec(memory_space=pltpu.HBM)


def _chip_exchange(src, name):
    same = src.ndim == 2
    rows = src.shape[-2]

    def body(src_ref, out_ref, send_sems, recv_sems, local_sem):
        x, y, c = lax.axis_index("x"), lax.axis_index("y"), lax.axis_index("c")
        me = 2 * x + y
        chips = [(1 - x, y), (x, 1 - y), (1 - x, 1 - y)]
        slab = (lambda t: src_ref) if same else (lambda t: src_ref.at[t])
        mine = pltpu.make_async_copy(slab(me), out_ref.at[me], local_sem)
        mine.start()
        sends = []
        for j, (px, py) in enumerate(chips):
            cp = pltpu.make_async_remote_copy(src_ref=slab(2 * px + py), dst_ref=out_ref.at[me], send_sem=send_sems.at[j],
                                              recv_sem=recv_sems.at[j], device_id=(px, py, c), device_id_type=MESH)
            cp.start()
            sends.append(cp)
        for j, (px, py) in enumerate(chips):
            pltpu.make_async_remote_copy(src_ref=slab(me), dst_ref=out_ref.at[2 * px + py], send_sem=send_sems.at[j],
                                         recv_sem=recv_sems.at[j], device_id=(px, py, c), device_id_type=MESH).wait_recv()
        for cp in sends:
            cp.wait_send()
        mine.wait()

    return pl.pallas_call(
        body, name=name, in_specs=[HBM], out_specs=HBM,
        out_shape=jax.ShapeDtypeStruct((N_CHIPS, rows, LANE), src.dtype),
        scratch_shapes=[pltpu.SemaphoreType.DMA((3,)), pltpu.SemaphoreType.DMA((3,)), pltpu.SemaphoreType.DMA],
        compiler_params=pltpu.CompilerParams(has_side_effects=True),
    )(src)


def _sibling_exchange(src, name):
    def body(src_ref, out_ref, send_sem, recv_sem):
        sibling = (lax.axis_index("x"), lax.axis_index("y"), 1 - lax.axis_index("c"))
        cp = pltpu.make_async_remote_copy(src_ref=src_ref, dst_ref=out_ref, send_sem=send_sem, recv_sem=recv_sem,
                                          device_id=sibling, device_id_type=MESH)
        cp.start()
        cp.wait()

    return pl.pallas_call(
        body, name=name, in_specs=[HBM], out_specs=HBM, out_shape=jax.ShapeDtypeStruct(src.shape, src.dtype),
        scratch_shapes=[pltpu.SemaphoreType.DMA, pltpu.SemaphoreType.DMA],
        compiler_params=pltpu.CompilerParams(has_side_effects=True),
    )(src)


PACK_TILE = PACK_ROWS // 8


def _sum_chips(parts):
    def body(p_ref, o_ref):
        o_ref[...] = ((p_ref[0] + p_ref[1]) + p_ref[2]) + p_ref[3]

    return pl.pallas_call(
        body, name="sum_chips", grid=(PACK_ROWS // PACK_TILE,),
        in_specs=[pl.BlockSpec((N_CHIPS, PACK_TILE, LANE), lambda i: (0, i, 0))],
        out_specs=pl.BlockSpec((PACK_TILE, LANE), lambda i: (i, 0)),
        out_shape=jax.ShapeDtypeStruct((PACK_ROWS, LANE), F32),
        compiler_params=_params(("parallel",)),
    )(parts)


def _adamw(p_mine, p_sibling, w, m, v):
    def body(a_ref, b_ref, w_ref, m_ref, v_ref, g_ref, d_ref, nm_ref, nv_ref):
        g = a_ref[...] + b_ref[...]
        nm = ADAM_B1 * m_ref[...] + (1.0 - ADAM_B1) * g
        nv = ADAM_B2 * v_ref[...] + (1.0 - ADAM_B2) * (g * g)
        m_hat = nm / (1.0 - ADAM_B1 ** ADAM_STEP)
        v_hat = nv / (1.0 - ADAM_B2 ** ADAM_STEP)
        g_ref[...] = g
        d_ref[...] = -ADAM_LR * (m_hat / (jnp.sqrt(v_hat) + ADAM_EPS) + ADAM_WD * w_ref[...])
        nm_ref[...] = nm
        nv_ref[...] = nv

    tile = pl.BlockSpec((PACK_TILE, LANE), lambda i: (i, 0))
    sds = jax.ShapeDtypeStruct((PACK_ROWS, LANE), F32)
    return pl.pallas_call(
        body, name="adamw", grid=(PACK_ROWS // PACK_TILE,), in_specs=[tile] * 5, out_specs=[tile] * 4, out_shape=[sds] * 4,
        compiler_params=_params(("parallel",)),
    )(p_mine, p_sibling, w, m, v)


WEIGHTS = ("g_pre", "w_in", "b_gate", "g_q", "w_uq", "g_kv", "w_ukv", "lb_logits", "g_hgrn", "w_branch_a", "w_branch_b", "w_out", "g_post")


def kernel(x, g_pre, w_in, b_gate, g_q, w_uq, g_kv, w_ukv, lb_logits, g_hgrn, w_branch_a, w_branch_b, w_out, g_post, loss_target, m_g_pre, m_w_in, m_b_gate, m_g_q, m_w_uq, m_g_kv, m_w_ukv, m_lb_logits, m_g_hgrn, m_w_branch_a, m_w_branch_b, m_w_out, m_g_post, v_g_pre, v_w_in, v_b_gate, v_g_q, v_w_uq, v_g_kv, v_w_ukv, v_lb_logits, v_g_hgrn, v_w_branch_a, v_w_branch_b, v_w_out, v_g_post):
    w = dict(g_pre=g_pre, w_in=w_in, b_gate=b_gate, g_q=g_q, w_uq=w_uq, g_kv=g_kv, w_ukv=w_ukv, lb_logits=lb_logits, g_hgrn=g_hgrn,
             w_branch_a=w_branch_a, w_branch_b=w_branch_b, w_out=w_out, g_post=g_post)
    m = dict(g_pre=m_g_pre, w_in=m_w_in, b_gate=m_b_gate, g_q=m_g_q, w_uq=m_w_uq, g_kv=m_g_kv, w_ukv=m_w_ukv, lb_logits=m_lb_logits,
             g_hgrn=m_g_hgrn, w_branch_a=m_w_branch_a, w_branch_b=m_w_branch_b, w_out=m_w_out, g_post=m_g_post)
    v = dict(g_pre=v_g_pre, w_in=v_w_in, b_gate=v_b_gate, g_q=v_g_q, w_uq=v_w_uq, g_kv=v_g_kv, w_ukv=v_w_ukv, lb_logits=v_lb_logits,
             g_hgrn=v_g_hgrn, w_branch_a=v_w_branch_a, w_branch_b=v_w_branch_b, w_out=v_w_out, g_post=v_g_post)
    w_pack, m_pack, v_pack = _pack_shards(w), _pack_shards(m), _pack_shards(v)

    gathered = _chip_exchange(w_pack[:BIG_ROWS].astype(BF16), "gather_weights")
    big = _join_chips(gathered)
    loss, grad_x, grads = _local_step(
        x[0], loss_target[0], g_pre, big["w_in"], b_gate, g_q, big["w_uq"], g_kv, big["w_ukv"], lb_logits, g_hgrn,
        big["w_branch_a"], big["w_branch_b"], big["w_out"], g_post)

    g_split = jnp.concatenate([_split_by_chip(grads), jnp.broadcast_to(_pack_vectors(grads), (N_CHIPS, VEC_ROWS, LANE))], axis=1)
    p_mine = _sum_chips(_chip_exchange(g_split, "scatter_grads"))
    p_sibling = _sibling_exchange(p_mine, "sibling_grads")
    g_pack, d_pack, nm_pack, nv_pack = _adamw(p_mine, p_sibling, w_pack, m_pack, v_pack)

    total = lax.psum(loss[0, 0], ("x", "y", "c"))
    outs = [_unpack_shards(p) for p in (g_pack, d_pack, nm_pack, nv_pack)]
    return (total, grad_x[None], *[o[n] for o in outs for n in WEIGHTS])
```

```python
import functools
import math

import numpy as np
import jax
import jax.numpy as jnp
from jax import lax
from jax.experimental import pallas as pl
from jax.experimental.pallas import tpu as pltpu

F32 = jnp.float32
BF16 = jnp.bfloat16
HIGHEST = lax.Precision.HIGHEST

D = 1024
NH = 8
QK_NOPE, QK_ROPE, V_DIM = 64, 32, 64
Q_LORA, KV_LORA = 768, 256
CHUNK = 64
HG_BLOCK = 32
EPS = 1e-6
D_IN = 5664
LANE = 128
P_MERGE, P_GA, P_HQ, P_HF, P_HI, P_GB, P_CQ, P_CKV, P_KPE = 0, 2048, 2560, 3072, 3584, 4096, 4608, 5376, 5632
D_P = 5760
O_CQ, O_CKV, O_KPE, O_GA, O_HQ, O_HF, O_HI, O_GB, O_MERGE = 0, 768, 1024, 1056, 1568, 2080, 2592, 3104, 3616

TM = 256
TQ = 512
TH = 256
VMEM_LIMIT = 56 * 1024 * 1024

ADAM_LR, ADAM_B1, ADAM_B2, ADAM_EPS, ADAM_WD, ADAM_STEP = 0.001, 0.9, 0.999, 1e-08, 0.01, 10

NT_DIMS = (((1,), (1,)), ((), ()))
TN_DIMS = (((0,), (0,)), ((), ()))


def _params(sem):
    return pltpu.CompilerParams(dimension_semantics=sem, vmem_limit_bytes=VMEM_LIMIT)


def _mm(a, b):
    return jnp.dot(a, b, preferred_element_type=F32)


def _mm_nt(a, b):
    return lax.dot_general(a, b, NT_DIMS, preferred_element_type=F32)


def _mm_tn(a, b):
    return lax.dot_general(a, b, TN_DIMS, preferred_element_type=F32)


def _sigmoid(z):
    return jax.nn.sigmoid(z)


def _rope(v, c, s1, s2):
    return v * c + pltpu.roll(v, 112, 1) * s1 + pltpu.roll(v, 16, 1) * s2


def _rope_t(dy, c, s1, s2):
    return dy * c + pltpu.roll(dy * s1, 16, 1) + pltpu.roll(dy * s2, 112, 1)


def _rope_tables(s):
    inv = 10000.0 ** (-jnp.arange(0, QK_ROPE, 2, dtype=F32) / QK_ROPE)
    ang = jnp.arange(s, dtype=F32)[:, None] * inv[None, :]
    cos, sin = jnp.cos(ang), jnp.sin(ang)
    z64, z32, o64, o32 = jnp.zeros((s, 64), F32), jnp.zeros((s, 32), F32), jnp.ones((s, 64), F32), jnp.ones((s, 32), F32)
    z16 = jnp.zeros((s, 16), F32)
    c = jnp.concatenate([o64, cos, cos, o32], axis=1)
    s1 = jnp.concatenate([z64, -sin, z16, z32], axis=1)
    s2 = jnp.concatenate([z64, z16, sin, z32], axis=1)
    return c, s1, s2


def _front_fwd(x, g_pre, w_in_p):
    s = x.shape[0]
    nb, wn = 3, D_P // 3

    def body(x_ref, g_ref, w_ref, o_ref):
        xv = x_ref[...]
        r = lax.rsqrt(jnp.mean(xv * xv, axis=-1, keepdims=True) + EPS)
        h = ((xv * r) * g_ref[...]).astype(BF16)
        o_ref[...] = _mm(h, w_ref[...])

    return pl.pallas_call(
        body, name="front_fwd", grid=(nb, s // TM),
        in_specs=[pl.BlockSpec((TM, D), lambda n, i: (i, 0)), pl.BlockSpec((1, D), lambda n, i: (0, 0)),
                  pl.BlockSpec((D, wn), lambda n, i: (0, n))],
        out_specs=pl.BlockSpec((TM, wn), lambda n, i: (i, n)),
        out_shape=jax.ShapeDtypeStruct((s, D_P), F32),
        compiler_params=_params(("parallel", "parallel")),
    )(x, g_pre, w_in_p)


def _norm_rows(v, g):
    r = lax.rsqrt(jnp.mean(v * v, axis=-1, keepdims=True) + EPS)
    return (v * r) * g, r


def _qkv_fwd(proj, g_q, g_kv, w_uq_p, w_k_p, w_v_p, rc, rs1, rs2):
    s = proj.shape[0]

    def body(cq_ref, ckv_ref, kpe_ref, gq_ref, gkv_ref, wq_ref, wk_ref, wv_ref, c_ref, s1_ref, s2_ref, q_ref, k_ref, v_ref):
        c, s1, s2 = c_ref[...], s1_ref[...], s2_ref[...]
        cqn, _ = _norm_rows(cq_ref[...], gq_ref[...])
        ckvn, _ = _norm_rows(ckv_ref[...], gkv_ref[...])
        ckvn = ckvn.astype(BF16)
        qf = _mm(cqn.astype(BF16), wq_ref[...])
        kf = _mm(ckvn, wk_ref[...])
        vf = _mm(ckvn, wv_ref[...])
        kpe = _rope(kpe_ref[...], c, s1, s2)
        for h in range(NH):
            blk = slice(h * LANE, (h + 1) * LANE)
            q_ref[h] = _rope(qf[:, blk], c, s1, s2).astype(BF16)
            k_ref[h] = (kf[:, blk] + kpe).astype(BF16)
            v_ref[h] = vf[:, blk].astype(BF16)

    row = lambda w, j: pl.BlockSpec((TM, w), lambda i: (i, j))
    full = lambda a: pl.BlockSpec(a.shape, lambda i: (0,) * a.ndim)
    hs = jax.ShapeDtypeStruct((NH, s, LANE), BF16)
    return pl.pallas_call(
        body, name="qkv_fwd", grid=(s // TM,),
        in_specs=[row(Q_LORA, P_CQ // Q_LORA), row(KV_LORA, P_CKV // KV_LORA), row(LANE, P_KPE // LANE),
                  full(g_q), full(g_kv), full(w_uq_p), full(w_k_p), full(w_v_p), row(LANE, 0), row(LANE, 0), row(LANE, 0)],
        out_specs=[pl.BlockSpec((NH, TM, LANE), lambda i: (0, i, 0))] * 3,
        out_shape=[hs, hs, hs],
        compiler_params=_params(("parallel",)),
    )(proj, proj, proj, g_q, g_kv, w_uq_p, w_k_p, w_v_p, rc, rs1, rs2)


LOG2E = 1.4426950408889634
QK_SCALE2 = LOG2E / math.sqrt(QK_NOPE + QK_ROPE)


def _diag_visible():
    row = lax.broadcasted_iota(jnp.int32, (TQ, TQ), 0)
    col = lax.broadcasted_iota(jnp.int32, (TQ, TQ), 1)
    return (col // CHUNK) <= (row // CHUNK)


def _attn_fwd(q, k, vv):
    s = q.shape[1]

    def body(q_ref, k_ref, v_ref, o_ref, lse_ref):
        i = pl.program_id(1)
        qs = (q_ref[0], q_ref[1])

        def tile(hh, t, carry, diag):
            m, l, acc = carry
            rows = pl.ds(pl.multiple_of(t * TQ, TQ), TQ)
            sc = _mm_nt(qs[hh], k_ref[hh, rows, :])
            if diag:
                sc = jnp.where(_diag_visible(), sc, -jnp.inf)
            m_new = jnp.maximum(m, jnp.max(sc, axis=-1, keepdims=True))
            alpha = jnp.exp2((m - m_new) * QK_SCALE2)
            p = jnp.exp2((sc - m_new) * QK_SCALE2)
            l = alpha * l + jnp.sum(p, axis=-1, keepdims=True)
            acc = alpha * acc + _mm(p.astype(BF16), v_ref[hh, rows, :])
            return m_new, l, acc

        def step(t, carry):
            return tile(0, t, carry[0], False), tile(1, t, carry[1], False)

        init = (jnp.full((TQ, 1), -jnp.inf, F32), jnp.zeros((TQ, 1), F32), jnp.zeros((TQ, LANE), F32))
        carry = lax.fori_loop(0, i, step, (init, init))
        out = jnp.zeros((TQ, LANE), F32)
        for hh in range(2):
            m, l, acc = tile(hh, i, carry[hh], True)
            out = out + acc / l
            lse_ref[hh] = jnp.broadcast_to(m * QK_SCALE2 + jnp.log(l) * LOG2E, (TQ, LANE))
        o_ref[...] = out

    return pl.pallas_call(
        body, name="attn_fwd", grid=(NH // 2, s // TQ),
        in_specs=[pl.BlockSpec((2, TQ, LANE), lambda p, i: (p, i, 0)), pl.BlockSpec((2, s, LANE), lambda p, i: (p, 0, 0)),
                  pl.BlockSpec((2, s, LANE), lambda p, i: (p, 0, 0))],
        out_specs=[pl.BlockSpec((TQ, LANE), lambda p, i: (i, p)), pl.BlockSpec((2, TQ, LANE), lambda p, i: (p, i, 0))],
        out_shape=[jax.ShapeDtypeStruct((s, NH * V_DIM), F32), jax.ShapeDtypeStruct((NH, s, LANE), F32)],
        compiler_params=_params(("parallel", "parallel")),
    )(q, k, vv)


def _lower_bound(lbl):
    a0, a1 = lbl[0:1, :], lbl[1:2, :]
    mx = jnp.maximum(a0, a1)
    e0, e1 = jnp.exp(a0 - mx), jnp.exp(a1 - mx)
    return e0 / (e0 + e1)


def _hgrn_gates(hq, hf, lb):
    sig = _sigmoid(hf)
    f = lb + (1.0 - lb) * sig
    g = jnp.log(f)
    kk = 1.0 - f
    r = lax.broadcasted_iota(jnp.int32, (TH, TH), 0)
    c = lax.broadcasted_iota(jnp.int32, (TH, TH), 1)
    same = (r // HG_BLOCK) == (c // HG_BLOCK)
    tri = same & (r >= c)
    cum = jnp.dot(tri.astype(F32), g, precision=HIGHEST, preferred_element_type=F32)
    lastb = jnp.dot(same.astype(F32), g, precision=HIGHEST, preferred_element_type=F32)
    e, ei, ee = jnp.exp(cum), jnp.exp(-cum), jnp.exp(lastb - cum)
    return dict(sig=sig, f=f, kk=kk, same=same, tri=tri, r=r, c=c, cum=cum, lastb=lastb, e=e, ei=ei, ee=ee,
                qd=hq * e, ki=kk * ei, ke=kk * ee)


def _pair_masks():
    lane = lax.broadcasted_iota(jnp.int32, (TH, LANE), 1)
    kr = lax.broadcasted_iota(jnp.int32, (LANE, LANE), 0)
    kc = lax.broadcasted_iota(jnp.int32, (LANE, LANE), 1)
    return lane < 64, (kr // 64) == (kc // 64)


def _hgrn_fwd(proj, lbl):
    s = proj.shape[0]
    nch = TH // HG_BLOCK

    def body(hq_ref, hf_ref, hi_ref, lbl_ref, o_ref, st_ref, st):
        @pl.when(pl.program_id(1) == 0)
        def _():
            st[...] = jnp.zeros_like(st)

        lb = _lower_bound(lbl_ref[...])
        gt = _hgrn_gates(hq_ref[...], hf_ref[...], lb)
        m0, bd = _pair_masks()
        v_b = hi_ref[...].astype(BF16)
        qd, ki_b, ke_b = gt["qd"], gt["ki"].astype(BF16), gt["ke"].astype(BF16)
        qd_b = qd.astype(BF16)
        o = jnp.zeros((TH, LANE), F32)
        for hh in range(2):
            mh = m0 if hh == 0 else jnp.logical_not(m0)
            a = jnp.where(gt["tri"], _mm_nt(jnp.where(mh, qd, 0.0).astype(BF16), ki_b), 0.0)
            o = jnp.where(mh, _mm(a.astype(BF16), v_b), o)
        for n in range(nch):
            sl = slice(n * HG_BLOCK, (n + 1) * HG_BLOCK)
            cur = st[...]
            st_ref[0, n] = cur
            o_ref[sl, :] = o[sl] + _mm_nt(qd_b[sl], cur.astype(BF16))
            upd = jnp.where(bd, _mm_tn(v_b[sl], ke_b[sl]), 0.0)
            st[...] = jnp.exp(gt["lastb"][n * HG_BLOCK:n * HG_BLOCK + 1, :]) * cur + upd

    col = lambda base: pl.BlockSpec((TH, LANE), lambda p, i: (i, base // LANE + p))
    return pl.pallas_call(
        body, name="hgrn_fwd", grid=(NH // 2, s // TH),
        in_specs=[col(P_HQ), col(P_HF), col(P_HI), pl.BlockSpec((2, LANE), lambda p, i: (0, p))],
        out_specs=[pl.BlockSpec((TH, LANE), lambda p, i: (i, p)),
                   pl.BlockSpec((1, nch, LANE, LANE), lambda p, i: (p, i, 0, 0))],
        out_shape=[jax.ShapeDtypeStruct((s, 512), F32), jax.ShapeDtypeStruct((NH // 2, s // HG_BLOCK, LANE, LANE), F32)],
        scratch_shapes=[pltpu.VMEM((LANE, LANE), F32)],
        compiler_params=_params(("parallel", "arbitrary")),
    )(proj, proj, proj, lbl)


def _group_sum(v):
    r = lax.broadcasted_iota(jnp.int32, (512, 512), 0)
    c = lax.broadcasted_iota(jnp.int32, (512, 512), 1)
    return jnp.dot(v, ((r // 64) == (c // 64)).astype(F32), precision=HIGHEST, preferred_element_type=F32)


def _dsilu(z, sg):
    return sg * (1.0 + z * (1.0 - sg))


def _mid(proj, attn, o_raw, x, tgt, g_hg, b_gate, g_post, wa, wb, w_out):
    s = x.shape[0]

    def body(attn_ref, ga_ref, o_ref, gb_ref, mg_ref, x_ref, t_ref, ghg_ref, bg_ref, gp_ref, wa_ref, wb_ref, wo_ref,
             loss_ref, dout_ref, dattn_ref, dga_ref, dor_ref, dgb_ref, dmg_ref, dwo_ref, dwa_ref, dwb_ref, dgp_ref, dbg_ref, dghg_ref):
        first = pl.program_id(0) == 0

        @pl.when(first)
        def _():
            for rf in (loss_ref, dwo_ref, dwa_ref, dwb_ref, dgp_ref, dbg_ref, dghg_ref):
                rf[...] = jnp.zeros_like(rf)

        attn, za, orw, zb = attn_ref[...], ga_ref[...], o_ref[...], gb_ref[...]
        ghg, gp = ghg_ref[...], gp_ref[...]
        sga, sgb = _sigmoid(za), _sigmoid(zb)
        sa, sb = za * sga, zb * sgb
        ga = attn * sa
        rh = lax.rsqrt(_group_sum(orw * orw) * (1.0 / V_DIM) + EPS)
        on = (orw * rh) * ghg
        gb = on * sb
        ga_b, gb_b = ga.astype(BF16), gb.astype(BF16)
        ya = _mm(ga_b, wa_ref[...])
        yb = _mm(gb_b, wb_ref[...])
        gates = _sigmoid(mg_ref[...] + bg_ref[...])
        g0, g1 = gates[:, :D], gates[:, D:]
        m_b = (g0 * ya + g1 * yb).astype(BF16)
        y = _mm(m_b, wo_ref[...])
        ry = lax.rsqrt(jnp.mean(y * y, axis=-1, keepdims=True) + EPS)
        out = x_ref[...] + (y * ry) * gp
        err = out - t_ref[...]
        loss_ref[...] += 0.5 * jnp.sum(jnp.mean(err * err, axis=-1, keepdims=True), axis=0, keepdims=True)
        dout = err * (1.0 / D)
        dout_ref[...] = dout
        dgp_ref[...] += jnp.sum(dout * (y * ry), axis=0, keepdims=True)
        dgy = dout * gp
        dy = ry * dgy - y * (ry * ry * ry) * jnp.mean(y * dgy, axis=-1, keepdims=True)
        dy_b = dy.astype(BF16)
        dwo_ref[...] += _mm_tn(m_b, dy_b)
        dm = _mm_nt(dy_b, wo_ref[...])
        dya, dyb = dm * g0, dm * g1
        dg0, dg1 = dm * ya, dm * yb
        dmg = jnp.concatenate([dg0 * g0 * (1.0 - g0), dg1 * g1 * (1.0 - g1)], axis=1)
        dmg_ref[...] = dmg.astype(BF16)
        dbg_ref[...] += jnp.sum(dmg, axis=0, keepdims=True)
        dya_b, dyb_b = dya.astype(BF16), dyb.astype(BF16)
        dwa_ref[...] += _mm_tn(ga_b, dya_b)
        dwb_ref[...] += _mm_tn(gb_b, dyb_b)
        dga = _mm_nt(dya_b, wa_ref[...])
        dgb = _mm_nt(dyb_b, wb_ref[...])
        dattn_ref[...] = dga * sa
        dga_ref[...] = (dga * attn * _dsilu(za, sga)).astype(BF16)
        dgb_ref[...] = (dgb * on * _dsilu(zb, sgb)).astype(BF16)
        don = dgb * sb
        dghg_ref[...] += jnp.sum(don * (orw * rh), axis=0, keepdims=True)
        dgo = don * ghg
        dor_ref[...] = rh * dgo - orw * (rh * rh * rh) * (_group_sum(orw * dgo) * (1.0 / V_DIM))

    row = lambda w, j=0: pl.BlockSpec((TM, w), lambda i: (i, j))
    full = lambda a: pl.BlockSpec(a.shape, lambda i: (0,) * a.ndim)
    acc = lambda shape: pl.BlockSpec(shape, lambda i: (0, 0))
    sds = jax.ShapeDtypeStruct
    return pl.pallas_call(
        body, name="mid", grid=(s // TM,),
        in_specs=[row(512), row(512, P_GA // 512), row(512), row(512, P_GB // 512), row(2048, P_MERGE // 2048), row(D), row(D),
                  full(g_hg), full(b_gate), full(g_post), full(wa), full(wb), full(w_out)],
        out_specs=[acc((1, 1)), row(D), row(512), row(512), row(512), row(512), row(2048),
                   acc((D, D)), acc((512, D)), acc((512, D)), acc((1, D)), acc((1, 2048)), acc((1, 512))],
        out_shape=[sds((1, 1), F32), sds((s, D), F32), sds((s, 512), F32), sds((s, 512), BF16), sds((s, 512), F32), sds((s, 512), BF16),
                   sds((s, 2048), BF16), sds((D, D), F32), sds((512, D), F32), sds((512, D), F32), sds((1, D), F32),
                   sds((1, 2048), F32), sds((1, 512), F32)],
        compiler_params=_params(("arbitrary",)),
    )(attn, proj, o_raw, proj, proj, x, tgt, g_hg, b_gate, g_post, wa, wb, w_out)


def _attn_bwd(q, k, vv, attn, dattn, lse):
    s = q.shape[1]
    nt = s // TQ
    scale = 1.0 / math.sqrt(QK_NOPE + QK_ROPE)

    def body(q_ref, k_ref, v_ref, o_ref, do_ref, lse_ref, dq_ref, dk_ref, dv_ref, do_s, delta_s):
        j = pl.program_id(1)

        @pl.when(j == 0)
        def _():
            dq_ref[...] = jnp.zeros_like(dq_ref)
            lane = lax.broadcasted_iota(jnp.int32, (TQ, LANE), 1)

            @pl.loop(0, nt)
            def _(i):
                rows = pl.ds(pl.multiple_of(i * TQ, TQ), TQ)
                do, o = do_ref[rows, :], o_ref[rows, :]
                for hh in range(2):
                    doh = jnp.where((lane < 64) if hh == 0 else (lane >= 64), do, 0.0)
                    do_s[hh, rows, :] = doh.astype(BF16)
                    delta_s[hh, rows, :] = jnp.broadcast_to(jnp.sum(doh * o, axis=-1, keepdims=True), (TQ, LANE))

        kjs, vjs = (k_ref[0], k_ref[1]), (v_ref[0], v_ref[1])
        wide = lambda a: jnp.concatenate([a] * (TQ // LANE), axis=1)

        def tile(hh, i, carry, diag):
            dk, dv = carry
            rows = pl.ds(pl.multiple_of(i * TQ, TQ), TQ)
            qi, do_b = q_ref[hh, rows, :], do_s[hh, rows, :]
            p = jnp.exp2(_mm_nt(qi, kjs[hh]) * QK_SCALE2 - wide(lse_ref[hh, rows, :]))
            if diag:
                p = jnp.where(_diag_visible(), p, 0.0)
            dv = dv + _mm_tn(p.astype(BF16), do_b)
            ds_b = (p * (_mm_nt(do_b, vjs[hh]) - wide(delta_s[hh, rows, :]))).astype(BF16)
            dk = dk + _mm_tn(ds_b, qi)
            dq_ref[hh, rows, :] += _mm(ds_b, kjs[hh])
            return dk, dv

        def step(i, carry):
            return tile(0, i, carry[0], False), tile(1, i, carry[1], False)

        z = jnp.zeros((TQ, LANE), F32)
        first = (tile(0, j, (z, z), True), tile(1, j, (z, z), True))
        carry = lax.fori_loop(j + 1, nt, step, first)
        for hh in range(2):
            dk_ref[hh] = carry[hh][0] * scale
            dv_ref[hh] = carry[hh][1]

        @pl.when(j == nt - 1)
        def _():
            dq_ref[...] = dq_ref[...] * scale

    whole = pl.BlockSpec((2, s, LANE), lambda p, j: (p, 0, 0))
    tile_spec = pl.BlockSpec((2, TQ, LANE), lambda p, j: (p, j, 0))
    cols = pl.BlockSpec((s, LANE), lambda p, j: (0, p))
    hs = jax.ShapeDtypeStruct((NH, s, LANE), F32)
    return pl.pallas_call(
        body, name="attn_bwd", grid=(NH // 2, nt),
        in_specs=[whole, tile_spec, tile_spec, cols, cols, whole],
        out_specs=[whole, tile_spec, tile_spec],
        out_shape=[hs, hs, hs],
        scratch_shapes=[pltpu.VMEM((2, s, LANE), BF16), pltpu.VMEM((2, s, LANE), F32)],
        compiler_params=_params(("parallel", "arbitrary")),
    )(q, k, vv, attn, dattn, lse)


def _hgrn_bwd(proj, lbl, states, do_raw):
    s = proj.shape[0]
    nt = s // TH
    nch = TH // HG_BLOCK

    def body(hq_ref, hf_ref, hi_ref, lbl_ref, st_ref, do_ref, dhq_ref, dhf_ref, dhi_ref, dlbl_ref, dst, dlb):
        step = pl.program_id(1)

        @pl.when(step == 0)
        def _():
            dst[...] = jnp.zeros_like(dst)
            dlb[...] = jnp.zeros_like(dlb)

        lb = _lower_bound(lbl_ref[...])
        gt = _hgrn_gates(hq_ref[...], hf_ref[...], lb)
        m0, bd = _pair_masks()
        do = do_ref[...]
        qd, ki, ke = gt["qd"], gt["ki"], gt["ke"]
        v_b, do_b = hi_ref[...].astype(BF16), do.astype(BF16)
        qd_b, ki_b, ke_b = qd.astype(BF16), ki.astype(BF16), ke.astype(BF16)
        dv = jnp.zeros((TH, LANE), F32)
        dqd = jnp.zeros((TH, LANE), F32)
        dki = jnp.zeros((TH, LANE), F32)
        for hh in range(2):
            mh = m0 if hh == 0 else jnp.logical_not(m0)
            a_b = jnp.where(gt["tri"], _mm_nt(jnp.where(mh, qd, 0.0).astype(BF16), ki_b), 0.0).astype(BF16)
            doh_b = jnp.where(mh, do, 0.0).astype(BF16)
            da_b = jnp.where(gt["tri"], _mm_nt(doh_b, v_b), 0.0).astype(BF16)
            dv = dv + _mm_tn(a_b, doh_b)
            dqd = jnp.where(mh, _mm(da_b, ki_b), dqd)
            dki = jnp.where(mh, _mm_tn(da_b, qd_b), dki)
        dv_x, dke_x, dqd_x, dlast_x = [None] * nch, [None] * nch, [None] * nch, [None] * nch
        for n in reversed(range(nch)):
            sl = slice(n * HG_BLOCK, (n + 1) * HG_BLOCK)
            ds = dst[...]
            ds_b = ds.astype(BF16)
            sn = st_ref[0, n]
            dv_x[n] = _mm_nt(ke_b[sl], ds_b)
            dke_n = _mm(v_b[sl], ds_b)
            dke_x[n] = dke_n
            dqd_x[n] = _mm(do_b[sl], sn.astype(BF16))
            dn = jnp.exp(gt["lastb"][n * HG_BLOCK:n * HG_BLOCK + 1, :])
            dlast = jnp.sum(dke_n * ke[sl], axis=0, keepdims=True) + jnp.sum(ds * sn, axis=0, keepdims=True) * dn
            dlast_x[n] = jnp.broadcast_to(dlast, (HG_BLOCK, LANE))
            dst[...] = dn * ds + jnp.where(bd, _mm_tn(do_b[sl], qd_b[sl]), 0.0)
        dv = dv + jnp.concatenate(dv_x, axis=0)
        dqd = dqd + jnp.concatenate(dqd_x, axis=0)
        dke = jnp.concatenate(dke_x, axis=0)
        dk = dki * gt["ei"] + dke * gt["ee"]
        dcum = dqd * qd - dki * ki - dke * ke
        tri_t = gt["same"] & (gt["r"] <= gt["c"])
        dg = jnp.dot(tri_t.astype(F32), dcum, precision=HIGHEST, preferred_element_type=F32) + jnp.concatenate(dlast_x, axis=0)
        sig = gt["sig"]
        df = dg / gt["f"] - dk
        dlb[...] += jnp.sum(df * (1.0 - sig), axis=0, keepdims=True)
        dhq_ref[...] = (dqd * gt["e"]).astype(BF16)
        dhf_ref[...] = ((df * (1.0 - lb)) * sig * (1.0 - sig)).astype(BF16)
        dhi_ref[...] = dv.astype(BF16)

        @pl.when(step == nt - 1)
        def _():
            da0 = dlb[...] * lb * (1.0 - lb)
            dlbl_ref[...] = jnp.concatenate([da0, -da0], axis=0)

    col = lambda base: pl.BlockSpec((TH, LANE), lambda p, i: (nt - 1 - i, base // LANE + p))
    tile = pl.BlockSpec((TH, LANE), lambda p, i: (nt - 1 - i, p))
    sds = jax.ShapeDtypeStruct
    return pl.pallas_call(
        body, name="hgrn_bwd", grid=(NH // 2, nt),
        in_specs=[col(P_HQ), col(P_HF), col(P_HI), pl.BlockSpec((2, LANE), lambda p, i: (0, p)),
                  pl.BlockSpec((1, nch, LANE, LANE), lambda p, i: (p, nt - 1 - i, 0, 0)), tile],
        out_specs=[tile, tile, tile, pl.BlockSpec((2, LANE), lambda p, i: (0, p))],
        out_shape=[sds((s, 512), BF16), sds((s, 512), BF16), sds((s, 512), BF16), sds((2, 512), F32)],
        scratch_shapes=[pltpu.VMEM((LANE, LANE), F32), pltpu.VMEM((1, LANE), F32)],
        compiler_params=_params(("parallel", "arbitrary")),
    )(proj, proj, proj, lbl, states, do_raw)


def _norm_rows_bwd(v, r, g, dn):
    dgv = dn * g
    return r * dgv - v * (r * r * r) * jnp.mean(v * dgv, axis=-1, keepdims=True)


def _qkv_bwd(proj, dq, dk, dvv, g_q, g_kv, w_uq_p, w_k_p, w_v_p, rc, rs1, rs2):
    s = proj.shape[0]

    def body(cq_ref, ckv_ref, dq_ref, dk_ref, dv_ref, gq_ref, gkv_ref, wq_ref, wk_ref, wv_ref, c_ref, s1_ref, s2_ref,
             dcq_ref, dckv_ref, dkpe_ref, dwq_ref, dwk_ref, dwv_ref, dgq_ref, dgkv_ref):
        @pl.when(pl.program_id(0) == 0)
        def _():
            for rf in (dwq_ref, dwk_ref, dwv_ref, dgq_ref, dgkv_ref):
                rf[...] = jnp.zeros_like(rf)

        c, s1, s2 = c_ref[...], s1_ref[...], s2_ref[...]
        cq, ckv = cq_ref[...], ckv_ref[...]
        gq, gkv = gq_ref[...], gkv_ref[...]
        cqn, rq = _norm_rows(cq, gq)
        ckvn, rkv = _norm_rows(ckv, gkv)
        cqn_b, ckvn_b = cqn.astype(BF16), ckvn.astype(BF16)
        dqf = jnp.concatenate([_rope_t(dq_ref[h], c, s1, s2) for h in range(NH)], axis=1).astype(BF16)
        dkf = jnp.concatenate([dk_ref[h] for h in range(NH)], axis=1).astype(BF16)
        dvf = jnp.concatenate([dv_ref[h] for h in range(NH)], axis=1).astype(BF16)
        dkpe = dk_ref[0]
        for h in range(1, NH):
            dkpe = dkpe + dk_ref[h]
        lane = lax.broadcasted_iota(jnp.int32, (TM, LANE), 1)
        dkpe = jnp.where((lane >= QK_NOPE) & (lane < QK_NOPE + QK_ROPE), dkpe, 0.0)
        dkpe_ref[...] = _rope_t(dkpe, c, s1, s2).astype(BF16)
        dwq_ref[...] += _mm_tn(cqn_b, dqf)
        dwk_ref[...] += _mm_tn(ckvn_b, dkf)
        dwv_ref[...] += _mm_tn(ckvn_b, dvf)
        dcqn = _mm_nt(dqf, wq_ref[...])
        dckvn = _mm_nt(dkf, wk_ref[...]) + _mm_nt(dvf, wv_ref[...])
        dgq_ref[...] += jnp.sum(dcqn * (cq * rq), axis=0, keepdims=True)
        dgkv_ref[...] += jnp.sum(dckvn * (ckv * rkv), axis=0, keepdims=True)
        dcq_ref[...] = _norm_rows_bwd(cq, rq, gq, dcqn).astype(BF16)
        dckv_ref[...] = _norm_rows_bwd(ckv, rkv, gkv, dckvn).astype(BF16)

    row = lambda w, j=0: pl.BlockSpec((TM, w), lambda i: (i, j))
    full = lambda a: pl.BlockSpec(a.shape, lambda i: (0,) * a.ndim)
    acc = lambda shape: pl.BlockSpec(shape, lambda i: (0, 0))
    heads = pl.BlockSpec((NH, TM, LANE), lambda i: (0, i, 0))
    sds = jax.ShapeDtypeStruct
    return pl.pallas_call(
        body, name="qkv_bwd", grid=(s // TM,),
        in_specs=[row(Q_LORA, P_CQ // Q_LORA), row(KV_LORA, P_CKV // KV_LORA), heads, heads, heads,
                  full(g_q), full(g_kv), full(w_uq_p), full(w_k_p), full(w_v_p), row(LANE), row(LANE), row(LANE)],
        out_specs=[row(Q_LORA), row(KV_LORA), row(LANE), acc((Q_LORA, D)), acc((KV_LORA, D)), acc((KV_LORA, D)),
                   acc((1, Q_LORA)), acc((1, KV_LORA))],
        out_shape=[sds((s, Q_LORA), BF16), sds((s, KV_LORA), BF16), sds((s, LANE), BF16), sds((Q_LORA, D), F32),
                   sds((KV_LORA, D), F32), sds((KV_LORA, D), F32), sds((1, Q_LORA), F32), sds((1, KV_LORA), F32)],
        compiler_params=_params(("arbitrary",)),
    )(proj, proj, dq, dk, dvv, g_q, g_kv, w_uq_p, w_k_p, w_v_p, rc, rs1, rs2)


def _front_bwd(x, dout, dproj, g_pre, w_in_p):
    s = x.shape[0]

    def body(x_ref, do_ref, dp_ref, g_ref, w_ref, gx_ref, h_ref, dg_ref):
        @pl.when(pl.program_id(0) == 0)
        def _():
            dg_ref[...] = jnp.zeros_like(dg_ref)

        xv, g = x_ref[...], g_ref[...]
        hn, r = _norm_rows(xv, g)
        h_ref[...] = hn.astype(BF16)
        dh = _mm_nt(dp_ref[...], w_ref[...])
        dg_ref[...] += jnp.sum(dh * (xv * r), axis=0, keepdims=True)
        gx_ref[...] = do_ref[...] + _norm_rows_bwd(xv, r, g, dh)

    row = lambda w: pl.BlockSpec((TM, w), lambda i: (i, 0))
    full = lambda a: pl.BlockSpec(a.shape, lambda i: (0,) * a.ndim)
    sds = jax.ShapeDtypeStruct
    return pl.pallas_call(
        body, name="front_bwd", grid=(s // TM,),
        in_specs=[row(D), row(D), row(D_P), full(g_pre), full(w_in_p)],
        out_specs=[row(D), row(D), pl.BlockSpec((1, D), lambda i: (0, 0))],
        out_shape=[sds((s, D), F32), sds((s, D), BF16), sds((1, D), F32)],
        compiler_params=_params(("arbitrary",)),
    )(x, dout, dproj, g_pre, w_in_p)


def _win_grad(h, dproj):
    s = h.shape[0]
    tn, tk = 1152, 512

    def body(h_ref, d_ref, o_ref):
        @pl.when(pl.program_id(1) == 0)
        def _():
            o_ref[...] = jnp.zeros_like(o_ref)

        o_ref[...] += _mm_tn(h_ref[...], d_ref[...])

    return pl.pallas_call(
        body, name="win_grad", grid=(D_P // tn, s // tk),
        in_specs=[pl.BlockSpec((tk, D), lambda n, kk: (kk, 0)), pl.BlockSpec((tk, tn), lambda n, kk: (kk, n))],
        out_specs=pl.BlockSpec((D, tn), lambda n, kk: (0, n)),
        out_shape=jax.ShapeDtypeStruct((D, D_P), F32),
        compiler_params=_params(("parallel", "arbitrary")),
    )(h, dproj)


def _pad_win(w_in):
    z = lambda n: jnp.zeros((w_in.shape[0], n), w_in.dtype)
    sl = lambda o, n: w_in[:, o:o + n]
    return jnp.concatenate([sl(O_MERGE, 2048), sl(O_GA, 512), sl(O_HQ, 512), sl(O_HF, 512), sl(O_HI, 512), sl(O_GB, 512),
                            sl(O_CQ, Q_LORA), sl(O_CKV, KV_LORA), z(64), sl(O_KPE, QK_ROPE), z(32)], axis=1)


def _unpad_win(g):
    sl = lambda o, n: g[:, o:o + n]
    return jnp.concatenate([sl(P_CQ, Q_LORA), sl(P_CKV, KV_LORA), sl(P_KPE + 64, QK_ROPE), sl(P_GA, 512), sl(P_HQ, 512),
                            sl(P_HF, 512), sl(P_HI, 512), sl(P_GB, 512), sl(P_MERGE, 2048)], axis=1)


def _pad_wuq(w_uq):
    w = w_uq.reshape(Q_LORA, NH, QK_NOPE + QK_ROPE)
    return jnp.pad(w, ((0, 0), (0, 0), (0, LANE - QK_NOPE - QK_ROPE))).reshape(Q_LORA, NH * LANE)


def _unpad_wuq(g):
    return g.reshape(Q_LORA, NH, LANE)[:, :, :QK_NOPE + QK_ROPE].reshape(Q_LORA, NH * (QK_NOPE + QK_ROPE))


def _pad_wukv(w_ukv):
    w = w_ukv.reshape(KV_LORA, NH, QK_NOPE + V_DIM)
    w_k = jnp.pad(w[:, :, :QK_NOPE], ((0, 0), (0, 0), (0, LANE - QK_NOPE))).reshape(KV_LORA, NH * LANE)
    wv = w[:, :, QK_NOPE:].reshape(KV_LORA, NH // 2, 2, 1, V_DIM)
    eye = jnp.eye(2, dtype=w.dtype).reshape(1, 1, 2, 2, 1)
    return w_k, (wv * eye).reshape(KV_LORA, NH * LANE)


def _unpad_wukv(gk, gv):
    gk = gk.reshape(KV_LORA, NH, LANE)[:, :, :QK_NOPE]
    gv = gv.reshape(KV_LORA, NH // 2, 2, 2, V_DIM)
    gv = jnp.stack([gv[:, :, 0, 0], gv[:, :, 1, 1]], axis=2).reshape(KV_LORA, NH, V_DIM)
    return jnp.concatenate([gk, gv], axis=-1).reshape(KV_LORA, NH * (QK_NOPE + V_DIM))


def _local_step(x, tgt, g_pre, w_in, b_gate, g_q, w_uq, g_kv, w_ukv, lb_logits, g_hgrn, wa, wb, w_out, g_post):
    s = x.shape[0]
    w_in_p = _pad_win(w_in)
    w_uq_p = _pad_wuq(w_uq)
    w_k_p, w_v_p = _pad_wukv(w_ukv)
    rc, rs1, rs2 = _rope_tables(s)
    g_hg = jnp.tile(g_hgrn, (1, NH))

    proj = _front_fwd(x, g_pre, w_in_p)
    q, k, vv = _qkv_fwd(proj, g_q, g_kv, w_uq_p, w_k_p, w_v_p, rc, rs1, rs2)
    attn, lse = _attn_fwd(q, k, vv)
    o_raw, states = _hgrn_fwd(proj, lb_logits)
    (loss, dout, dattn, dga, dor, dgb, dmg, d_wout, d_wa, d_wb, d_gpost, d_bgate, d_ghg) = _mid(
        proj, attn, o_raw, x, tgt, g_hg, b_gate, g_post, wa, wb, w_out)
    dq, dk, dvv = _attn_bwd(q, k, vv, attn, dattn, lse)
    dhq, dhf, dhi, d_lbl = _hgrn_bwd(proj, lb_logits, states, dor)
    dcq, dckv, dkpe, d_wuq_p, d_wk_p, d_wv_p, d_gq, d_gkv = _qkv_bwd(proj, dq, dk, dvv, g_q, g_kv, w_uq_p, w_k_p, w_v_p, rc, rs1, rs2)
    dproj = jnp.concatenate([dmg, dga, dhq, dhf, dhi, dgb, dcq, dckv, dkpe], axis=1)
    grad_x, h, d_gpre = _front_bwd(x, dout, dproj, g_pre, w_in_p)
    d_win_p = _win_grad(h, dproj)
    grads = dict(g_pre=d_gpre, w_in=_unpad_win(d_win_p), b_gate=d_bgate, g_q=d_gq, w_uq=_unpad_wuq(d_wuq_p), g_kv=d_gkv,
                 w_ukv=_unpad_wukv(d_wk_p, d_wv_p), lb_logits=d_lbl, g_hgrn=d_ghg.reshape(NH, V_DIM).sum(axis=0, keepdims=True),
                 w_branch_a=d_wa, w_branch_b=d_wb, w_out=d_wout, g_post=d_gpost)
    return loss, grad_x, grads


SHARD_SHAPES = (("w_in", (1024, 1416)), ("w_uq", (192, 768)), ("w_ukv", (256, 256)), ("w_branch_a", (512, 256)),
                ("w_branch_b", (512, 256)), ("w_out", (256, 1024)))
ROW_SHARDED = ("w_uq", "w_out")
VECTORS = (("g_pre", 1024), ("b_gate", 2048), ("g_q", 768), ("g_kv", 256), ("lb_logits", 1024), ("g_hgrn", 64), ("g_post", 1024))
BIG_ROWS = sum(a * b for _, (a, b) in SHARD_SHAPES) // LANE
VEC_ROWS = 64
PACK_ROWS = BIG_ROWS + VEC_ROWS
N_CHIPS = 4


def _pack_vectors(vals):
    flat = jnp.concatenate([vals[n].reshape(-1) for n, _ in VECTORS])
    return jnp.pad(flat, (0, VEC_ROWS * LANE - flat.shape[0])).reshape(VEC_ROWS, LANE)


def _pack_shards(vals):
    return jnp.concatenate([vals[n].reshape(-1, LANE) for n, _ in SHARD_SHAPES] + [_pack_vectors(vals)], axis=0)


def _unpack_shards(pack):
    out, r = {}, 0
    for n, (a, b) in SHARD_SHAPES:
        rows = a * b // LANE
        out[n] = pack[r:r + rows].reshape(1, a, b)
        r += rows
    flat, o = pack[r:].reshape(-1), 0
    for n, size in VECTORS:
        out[n] = flat[o:o + size].reshape((2, 512) if n == "lb_logits" else (1, size))
        o += size
    return out


def _split_by_chip(full):
    parts = []
    for n, (a, b) in SHARD_SHAPES:
        g = full[n]
        if n in ROW_SHARDED:
            g = g.reshape(N_CHIPS, a, b)
        else:
            g = g.reshape(a, N_CHIPS, b).transpose(1, 0, 2)
        parts.append(g.reshape(N_CHIPS, -1, LANE))
    return jnp.concatenate(parts, axis=1)


def _join_chips(packs):
    out, r = {}, 0
    for n, (a, b) in SHARD_SHAPES:
        rows = a * b // LANE
        w = packs[:, r:r + rows].reshape(N_CHIPS, a, b)
        out[n] = w.reshape(N_CHIPS * a, b) if n in ROW_SHARDED else w.transpose(1, 0, 2).reshape(a, N_CHIPS * b)
        r += rows
    return out


MESH = pl.DeviceIdType.MESH
HBM = pl.BlockSpec(memory_space=pltpu.HBM)


def _chip_exchange(src, name):
    same = src.ndim == 2
    rows = src.shape[-2]

    def body(src_ref, out_ref, send_sems, recv_sems, local_sem):
        x, y, c = lax.axis_index("x"), lax.axis_index("y"), lax.axis_index("c")
        me = 2 * x + y
        chips = [(1 - x, y), (x, 1 - y), (1 - x, 1 - y)]
        slab = (lambda t: src_ref) if same else (lambda t: src_ref.at[t])
        mine = pltpu.make_async_copy(slab(me), out_ref.at[me], local_sem)
        mine.start()
        sends = []
        for j, (px, py) in enumerate(chips):
            cp = pltpu.make_async_remote_copy(src_ref=slab(2 * px + py), dst_ref=out_ref.at[me], send_sem=send_sems.at[j],
                                              recv_sem=recv_sems.at[j], device_id=(px, py, c), device_id_type=MESH)
            cp.start()
            sends.append(cp)
        for j, (px, py) in enumerate(chips):
            pltpu.make_async_remote_copy(src_ref=slab(me), dst_ref=out_ref.at[2 * px + py], send_sem=send_sems.at[j],
                                         recv_sem=recv_sems.at[j], device_id=(px, py, c), device_id_type=MESH).wait_recv()
        for cp in sends:
            cp.wait_send()
        mine.wait()

    return pl.pallas_call(
        body, name=name, in_specs=[HBM], out_specs=HBM,
        out_shape=jax.ShapeDtypeStruct((N_CHIPS, rows, LANE), src.dtype),
        scratch_shapes=[pltpu.SemaphoreType.DMA((3,)), pltpu.SemaphoreType.DMA((3,)), pltpu.SemaphoreType.DMA],
        compiler_params=pltpu.CompilerParams(has_side_effects=True),
    )(src)


def _sibling_exchange(src, name):
    def body(src_ref, out_ref, send_sem, recv_sem):
        sibling = (lax.axis_index("x"), lax.axis_index("y"), 1 - lax.axis_index("c"))
        cp = pltpu.make_async_remote_copy(src_ref=src_ref, dst_ref=out_ref, send_sem=send_sem, recv_sem=recv_sem,
                                          device_id=sibling, device_id_type=MESH)
        cp.start()
        cp.wait()

    return pl.pallas_call(
        body, name=name, in_specs=[HBM], out_specs=HBM, out_shape=jax.ShapeDtypeStruct(src.shape, src.dtype),
        scratch_shapes=[pltpu.SemaphoreType.DMA, pltpu.SemaphoreType.DMA],
        compiler_params=pltpu.CompilerParams(has_side_effects=True),
    )(src)


PACK_TILE = PACK_ROWS // 8


def _sum_chips(parts):
    def body(p_ref, o_ref):
        o_ref[...] = ((p_ref[0] + p_ref[1]) + p_ref[2]) + p_ref[3]

    return pl.pallas_call(
        body, name="sum_chips", grid=(PACK_ROWS // PACK_TILE,),
        in_specs=[pl.BlockSpec((N_CHIPS, PACK_TILE, LANE), lambda i: (0, i, 0))],
        out_specs=pl.BlockSpec((PACK_TILE, LANE), lambda i: (i, 0)),
        out_shape=jax.ShapeDtypeStruct((PACK_ROWS, LANE), F32),
        compiler_params=_params(("parallel",)),
    )(parts)


def _adamw(p_mine, p_sibling, w, m, v):
    def body(a_ref, b_ref, w_ref, m_ref, v_ref, g_ref, d_ref, nm_ref, nv_ref):
        g = a_ref[...] + b_ref[...]
        nm = ADAM_B1 * m_ref[...] + (1.0 - ADAM_B1) * g
        nv = ADAM_B2 * v_ref[...] + (1.0 - ADAM_B2) * (g * g)
        m_hat = nm / (1.0 - ADAM_B1 ** ADAM_STEP)
        v_hat = nv / (1.0 - ADAM_B2 ** ADAM_STEP)
        g_ref[...] = g
        d_ref[...] = -ADAM_LR * (m_hat / (jnp.sqrt(v_hat) + ADAM_EPS) + ADAM_WD * w_ref[...])
        nm_ref[...] = nm
        nv_ref[...] = nv

    tile = pl.BlockSpec((PACK_TILE, LANE), lambda i: (i, 0))
    sds = jax.ShapeDtypeStruct((PACK_ROWS, LANE), F32)
    return pl.pallas_call(
        body, name="adamw", grid=(PACK_ROWS // PACK_TILE,), in_specs=[tile] * 5, out_specs=[tile] * 4, out_shape=[sds] * 4,
        compiler_params=_params(("parallel",)),
    )(p_mine, p_sibling, w, m, v)


WEIGHTS = ("g_pre", "w_in", "b_gate", "g_q", "w_uq", "g_kv", "w_ukv", "lb_logits", "g_hgrn", "w_branch_a", "w_branch_b", "w_out", "g_post")


def kernel(x, g_pre, w_in, b_gate, g_q, w_uq, g_kv, w_ukv, lb_logits, g_hgrn, w_branch_a, w_branch_b, w_out, g_post, loss_target, m_g_pre, m_w_in, m_b_gate, m_g_q, m_w_uq, m_g_kv, m_w_ukv, m_lb_logits, m_g_hgrn, m_w_branch_a, m_w_branch_b, m_w_out, m_g_post, v_g_pre, v_w_in, v_b_gate, v_g_q, v_w_uq, v_g_kv, v_w_ukv, v_lb_logits, v_g_hgrn, v_w_branch_a, v_w_branch_b, v_w_out, v_g_post):
    w = dict(g_pre=g_pre, w_in=w_in, b_gate=b_gate, g_q=g_q, w_uq=w_uq, g_kv=g_kv, w_ukv=w_ukv, lb_logits=lb_logits, g_hgrn=g_hgrn,
             w_branch_a=w_branch_a, w_branch_b=w_branch_b, w_out=w_out, g_post=g_post)
    m = dict(g_pre=m_g_pre, w_in=m_w_in, b_gate=m_b_gate, g_q=m_g_q, w_uq=m_w_uq, g_kv=m_g_kv, w_ukv=m_w_ukv, lb_logits=m_lb_logits,
             g_hgrn=m_g_hgrn, w_branch_a=m_w_branch_a, w_branch_b=m_w_branch_b, w_out=m_w_out, g_post=m_g_post)
    v = dict(g_pre=v_g_pre, w_in=v_w_in, b_gate=v_b_gate, g_q=v_g_q, w_uq=v_w_uq, g_kv=v_g_kv, w_ukv=v_w_ukv, lb_logits=v_lb_logits,
             g_hgrn=v_g_hgrn, w_branch_a=v_w_branch_a, w_branch_b=v_w_branch_b, w_out=v_w_out, g_post=v_g_post)
    w_pack, m_pack, v_pack = _pack_shards(w), _pack_shards(m), _pack_shards(v)

    gathered = _chip_exchange(w_pack[:BIG_ROWS].astype(BF16), "gather_weights")
    big = _join_chips(gathered)
    loss, grad_x, grads = _local_step(
        x[0], loss_target[0], g_pre, big["w_in"], b_gate, g_q, big["w_uq"], g_kv, big["w_ukv"], lb_logits, g_hgrn,
        big["w_branch_a"], big["w_branch_b"], big["w_out"], g_post)

    g_split = jnp.concatenate([_split_by_chip(grads), jnp.broadcast_to(_pack_vectors(grads), (N_CHIPS, VEC_ROWS, LANE))], axis=1)
    p_mine = _sum_chips(_chip_exchange(g_split, "scatter_grads"))
    p_sibling = _sibling_exchange(p_mine, "sibling_grads")
    g_pack, d_pack, nm_pack, nv_pack = _adamw(p_mine, p_sibling, w_pack, m_pack, v_pack)

    total = lax.psum(loss[0, 0], ("x", "y", "c"))
    outs = [_unpack_shards(p) for p in (g_pack, d_pack, nm_pack, nv_pack)]
    return (total, grad_x[None], *[o[n] for o in outs for n in WEIGHTS])
```

```python
import functools
import math

import numpy as np
import jax
import jax.numpy as jnp
from jax import lax
from jax.experimental import pallas as pl
from jax.experimental.pallas import tpu as pltpu

F32 = jnp.float32
BF16 = jnp.bfloat16
HIGHEST = lax.Precision.HIGHEST

D = 1024
NH = 8
QK_NOPE, QK_ROPE, V_DIM = 64, 32, 64
Q_LORA, KV_LORA = 768, 256
CHUNK = 64
HG_BLOCK = 32
EPS = 1e-6
D_IN = 5664
LANE = 128
P_MERGE, P_GA, P_HQ, P_HF, P_HI, P_GB, P_CQ, P_CKV, P_KPE = 0, 2048, 2560, 3072, 3584, 4096, 4608, 5376, 5632
D_P = 5760
O_CQ, O_CKV, O_KPE, O_GA, O_HQ, O_HF, O_HI, O_GB, O_MERGE = 0, 768, 1024, 1056, 1568, 2080, 2592, 3104, 3616

TM = 256
TQ = 512
TH = 256
VMEM_LIMIT = 56 * 1024 * 1024

ADAM_LR, ADAM_B1, ADAM_B2, ADAM_EPS, ADAM_WD, ADAM_STEP = 0.001, 0.9, 0.999, 1e-08, 0.01, 10

NT_DIMS = (((1,), (1,)), ((), ()))
TN_DIMS = (((0,), (0,)), ((), ()))


def _params(sem):
    return pltpu.CompilerParams(dimension_semantics=sem, vmem_limit_bytes=VMEM_LIMIT)


def _mm(a, b):
    return jnp.dot(a, b, preferred_element_type=F32)


def _mm_nt(a, b):
    return lax.dot_general(a, b, NT_DIMS, preferred_element_type=F32)


def _mm_tn(a, b):
    return lax.dot_general(a, b, TN_DIMS, preferred_element_type=F32)


def _sigmoid(z):
    return jax.nn.sigmoid(z)


def _rope(v, c, s1, s2):
    return v * c + pltpu.roll(v, 112, 1) * s1 + pltpu.roll(v, 16, 1) * s2


def _rope_t(dy, c, s1, s2):
    return dy * c + pltpu.roll(dy * s1, 16, 1) + pltpu.roll(dy * s2, 112, 1)


def _rope_tables(s):
    inv = 10000.0 ** (-jnp.arange(0, QK_ROPE, 2, dtype=F32) / QK_ROPE)
    ang = jnp.arange(s, dtype=F32)[:, None] * inv[None, :]
    cos, sin = jnp.cos(ang), jnp.sin(ang)
    z64, z32, o64, o32 = jnp.zeros((s, 64), F32), jnp.zeros((s, 32), F32), jnp.ones((s, 64), F32), jnp.ones((s, 32), F32)
    z16 = jnp.zeros((s, 16), F32)
    c = jnp.concatenate([o64, cos, cos, o32], axis=1)
    s1 = jnp.concatenate([z64, -sin, z16, z32], axis=1)
    s2 = jnp.concatenate([z64, z16, sin, z32], axis=1)
    return c, s1, s2


def _front_fwd(x, g_pre, w_in_p):
    s = x.shape[0]

    def body(x_ref, g_ref, w_ref, o_ref, h_ref):
        xv = x_ref[...]
        r = lax.rsqrt(jnp.mean(xv * xv, axis=-1, keepdims=True) + EPS)
        h = ((xv * r) * g_ref[...]).astype(BF16)
        h_ref[...] = h
        o_ref[...] = _mm(h, w_ref[...])

    return pl.pallas_call(
        body, name="front_fwd", grid=(s // TM,),
        in_specs=[pl.BlockSpec((TM, D), lambda i: (i, 0)), pl.BlockSpec((1, D), lambda i: (0, 0)),
                  pl.BlockSpec((D, D_P), lambda i: (0, 0))],
        out_specs=[pl.BlockSpec((TM, D_P), lambda i: (i, 0)), pl.BlockSpec((TM, D), lambda i: (i, 0))],
        out_shape=[jax.ShapeDtypeStruct((s, D_P), F32), jax.ShapeDtypeStruct((s, D), BF16)],
        compiler_params=_params(("parallel",)),
    )(x, g_pre, w_in_p)


def _norm_rows(v, g):
    r = lax.rsqrt(jnp.mean(v * v, axis=-1, keepdims=True) + EPS)
    return (v * r) * g, r


def _qkv_fwd(proj, g_q, g_kv, w_uq_p, w_k_p, w_v_p, rc, rs1, rs2):
    s = proj.shape[0]

    def body(cq_ref, ckv_ref, kpe_ref, gq_ref, gkv_ref, wq_ref, wk_ref, wv_ref, c_ref, s1_ref, s2_ref, q_ref, k_ref, v_ref):
        c, s1, s2 = c_ref[...], s1_ref[...], s2_ref[...]
        cqn, _ = _norm_rows(cq_ref[...], gq_ref[...])
        ckvn, _ = _norm_rows(ckv_ref[...], gkv_ref[...])
        ckvn = ckvn.astype(BF16)
        qf = _mm(cqn.astype(BF16), wq_ref[...])
        kf = _mm(ckvn, wk_ref[...])
        vf = _mm(ckvn, wv_ref[...])
        kpe = _rope(kpe_ref[...], c, s1, s2)
        for h in range(NH):
            blk = slice(h * LANE, (h + 1) * LANE)
            q_ref[h] = _rope(qf[:, blk], c, s1, s2).astype(BF16)
            k_ref[h] = (kf[:, blk] + kpe).astype(BF16)
            v_ref[h] = vf[:, blk].astype(BF16)

    row = lambda w, j: pl.BlockSpec((TM, w), lambda i: (i, j))
    full = lambda a: pl.BlockSpec(a.shape, lambda i: (0,) * a.ndim)
    hs = jax.ShapeDtypeStruct((NH, s, LANE), BF16)
    return pl.pallas_call(
        body, name="qkv_fwd", grid=(s // TM,),
        in_specs=[row(Q_LORA, P_CQ // Q_LORA), row(KV_LORA, P_CKV // KV_LORA), row(LANE, P_KPE // LANE),
                  full(g_q), full(g_kv), full(w_uq_p), full(w_k_p), full(w_v_p), row(LANE, 0), row(LANE, 0), row(LANE, 0)],
        out_specs=[pl.BlockSpec((NH, TM, LANE), lambda i: (0, i, 0))] * 3,
        out_shape=[hs, hs, hs],
        compiler_params=_params(("parallel",)),
    )(proj, proj, proj, g_q, g_kv, w_uq_p, w_k_p, w_v_p, rc, rs1, rs2)


LOG2E = 1.4426950408889634
QK_SCALE2 = LOG2E / math.sqrt(QK_NOPE + QK_ROPE)


def _diag_visible():
    row = lax.broadcasted_iota(jnp.int32, (TQ, TQ), 0)
    col = lax.broadcasted_iota(jnp.int32, (TQ, TQ), 1)
    return (col // CHUNK) <= (row // CHUNK)


def _attn_fwd(q, k, vv):
    s = q.shape[1]

    def body(q_ref, k_ref, v_ref, o_ref, lse_ref):
        i = pl.program_id(1)
        qs = (q_ref[0], q_ref[1])

        def tile(hh, t, carry, diag):
            m, l, acc = carry
            rows = pl.ds(pl.multiple_of(t * TQ, TQ), TQ)
            sc = _mm_nt(qs[hh], k_ref[hh, rows, :])
            if diag:
                sc = jnp.where(_diag_visible(), sc, -jnp.inf)
            m_new = jnp.maximum(m, jnp.max(sc, axis=-1, keepdims=True))
            alpha = jnp.exp2((m - m_new) * QK_SCALE2)
            p = jnp.exp2((sc - m_new) * QK_SCALE2)
            l = alpha * l + jnp.sum(p, axis=-1, keepdims=True)
            acc = alpha * acc + _mm(p.astype(BF16), v_ref[hh, rows, :])
            return m_new, l, acc

        def step(t, carry):
            return tile(0, t, carry[0], False), tile(1, t, carry[1], False)

        init = (jnp.full((TQ, 1), -jnp.inf, F32), jnp.zeros((TQ, 1), F32), jnp.zeros((TQ, LANE), F32))
        carry = lax.fori_loop(0, i, step, (init, init))
        out = jnp.zeros((TQ, LANE), F32)
        for hh in range(2):
            m, l, acc = tile(hh, i, carry[hh], True)
            out = out + acc / l
            lse_ref[hh] = jnp.broadcast_to(m * QK_SCALE2 + jnp.log(l) * LOG2E, (TQ, LANE))
        o_ref[...] = out

    return pl.pallas_call(
        body, name="attn_fwd", grid=(NH // 2, s // TQ),
        in_specs=[pl.BlockSpec((2, TQ, LANE), lambda p, i: (p, i, 0)), pl.BlockSpec((2, s, LANE), lambda p, i: (p, 0, 0)),
                  pl.BlockSpec((2, s, LANE), lambda p, i: (p, 0, 0))],
        out_specs=[pl.BlockSpec((TQ, LANE), lambda p, i: (i, p)), pl.BlockSpec((2, TQ, LANE), lambda p, i: (p, i, 0))],
        out_shape=[jax.ShapeDtypeStruct((s, NH * V_DIM), F32), jax.ShapeDtypeStruct((NH, s, LANE), F32)],
        compiler_params=_params(("parallel", "parallel")),
    )(q, k, vv)


def _lower_bound(lbl):
    a0, a1 = lbl[0:1, :], lbl[1:2, :]
    mx = jnp.maximum(a0, a1)
    e0, e1 = jnp.exp(a0 - mx), jnp.exp(a1 - mx)
    return e0 / (e0 + e1)


def _hgrn_gates(hq, hf, lb):
    sig = _sigmoid(hf)
    f = lb + (1.0 - lb) * sig
    g = jnp.log(f)
    kk = 1.0 - f
    r = lax.broadcasted_iota(jnp.int32, (TH, TH), 0)
    c = lax.broadcasted_iota(jnp.int32, (TH, TH), 1)
    same = (r // HG_BLOCK) == (c // HG_BLOCK)
    tri = same & (r >= c)
    cum = jnp.dot(tri.astype(F32), g, precision=HIGHEST, preferred_element_type=F32)
    lastb = jnp.dot(same.astype(F32), g, precision=HIGHEST, preferred_element_type=F32)
    e, ei, ee = jnp.exp(cum), jnp.exp(-cum), jnp.exp(lastb - cum)
    return dict(sig=sig, f=f, kk=kk, same=same, tri=tri, r=r, c=c, cum=cum, lastb=lastb, e=e, ei=ei, ee=ee,
                qd=hq * e, ki=kk * ei, ke=kk * ee)


def _pair_masks():
    lane = lax.broadcasted_iota(jnp.int32, (TH, LANE), 1)
    kr = lax.broadcasted_iota(jnp.int32, (LANE, LANE), 0)
    kc = lax.broadcasted_iota(jnp.int32, (LANE, LANE), 1)
    return lane < 64, (kr // 64) == (kc // 64)


def _hgrn_fwd(proj, lbl):
    s = proj.shape[0]
    nch = TH // HG_BLOCK

    def body(hq_ref, hf_ref, hi_ref, lbl_ref, o_ref, st_ref, st):
        @pl.when(pl.program_id(1) == 0)
        def _():
            st[...] = jnp.zeros_like(st)

        lb = _lower_bound(lbl_ref[...])
        gt = _hgrn_gates(hq_ref[...], hf_ref[...], lb)
        m0, bd = _pair_masks()
        v_b = hi_ref[...].astype(BF16)
        qd, ki_b, ke_b = gt["qd"], gt["ki"].astype(BF16), gt["ke"].astype(BF16)
        qd_b = qd.astype(BF16)
        o = jnp.zeros((TH, LANE), F32)
        for hh in range(2):
            mh = m0 if hh == 0 else jnp.logical_not(m0)
            a = jnp.where(gt["tri"], _mm_nt(jnp.where(mh, qd, 0.0).astype(BF16), ki_b), 0.0)
            o = jnp.where(mh, _mm(a.astype(BF16), v_b), o)
        for n in range(nch):
            sl = slice(n * HG_BLOCK, (n + 1) * HG_BLOCK)
            cur = st[...]
            st_ref[0, n] = cur
            o_ref[sl, :] = o[sl] + _mm_nt(qd_b[sl], cur.astype(BF16))
            upd = jnp.where(bd, _mm_tn(v_b[sl], ke_b[sl]), 0.0)
            st[...] = jnp.exp(gt["lastb"][n * HG_BLOCK:n * HG_BLOCK + 1, :]) * cur + upd

    col = lambda base: pl.BlockSpec((TH, LANE), lambda p, i: (i, base // LANE + p))
    return pl.pallas_call(
        body, name="hgrn_fwd", grid=(NH // 2, s // TH),
        in_specs=[col(P_HQ), col(P_HF), col(P_HI), pl.BlockSpec((2, LANE), lambda p, i: (0, p))],
        out_specs=[pl.BlockSpec((TH, LANE), lambda p, i: (i, p)),
                   pl.BlockSpec((1, nch, LANE, LANE), lambda p, i: (p, i, 0, 0))],
        out_shape=[jax.ShapeDtypeStruct((s, 512), F32), jax.ShapeDtypeStruct((NH // 2, s // HG_BLOCK, LANE, LANE), F32)],
        scratch_shapes=[pltpu.VMEM((LANE, LANE), F32)],
        compiler_params=_params(("parallel", "arbitrary")),
    )(proj, proj, proj, lbl)


def _group_sum(v):
    r = lax.broadcasted_iota(jnp.int32, (512, 512), 0)
    c = lax.broadcasted_iota(jnp.int32, (512, 512), 1)
    return jnp.dot(v, ((r // 64) == (c // 64)).astype(F32), precision=HIGHEST, preferred_element_type=F32)


def _dsilu(z, sg):
    return sg * (1.0 + z * (1.0 - sg))


def _mid(proj, attn, o_raw, x, tgt, g_hg, b_gate, g_post, wa, wb, w_out):
    s = x.shape[0]

    def body(attn_ref, ga_ref, o_ref, gb_ref, mg_ref, x_ref, t_ref, ghg_ref, bg_ref, gp_ref, wa_ref, wb_ref, wo_ref,
             loss_ref, dout_ref, dattn_ref, dga_ref, dor_ref, dgb_ref, dmg_ref, dwo_ref, dwa_ref, dwb_ref, dgp_ref, dbg_ref, dghg_ref):
        first = pl.program_id(0) == 0

        @pl.when(first)
        def _():
            for rf in (loss_ref, dwo_ref, dwa_ref, dwb_ref, dgp_ref, dbg_ref, dghg_ref):
                rf[...] = jnp.zeros_like(rf)

        attn, za, orw, zb = attn_ref[...], ga_ref[...], o_ref[...], gb_ref[...]
        ghg, gp = ghg_ref[...], gp_ref[...]
        sga, sgb = _sigmoid(za), _sigmoid(zb)
        sa, sb = za * sga, zb * sgb
        ga = attn * sa
        rh = lax.rsqrt(_group_sum(orw * orw) * (1.0 / V_DIM) + EPS)
        on = (orw * rh) * ghg
        gb = on * sb
        ga_b, gb_b = ga.astype(BF16), gb.astype(BF16)
        ya = _mm(ga_b, wa_ref[...])
        yb = _mm(gb_b, wb_ref[...])
        gates = _sigmoid(mg_ref[...] + bg_ref[...])
        g0, g1 = gates[:, :D], gates[:, D:]
        m_b = (g0 * ya + g1 * yb).astype(BF16)
        y = _mm(m_b, wo_ref[...])
        ry = lax.rsqrt(jnp.mean(y * y, axis=-1, keepdims=True) + EPS)
        out = x_ref[...] + (y * ry) * gp
        err = out - t_ref[...]
        loss_ref[...] += 0.5 * jnp.sum(jnp.mean(err * err, axis=-1, keepdims=True), axis=0, keepdims=True)
        dout = err * (1.0 / D)
        dout_ref[...] = dout
        dgp_ref[...] += jnp.sum(dout * (y * ry), axis=0, keepdims=True)
        dgy = dout * gp
        dy = ry * dgy - y * (ry * ry * ry) * jnp.mean(y * dgy, axis=-1, keepdims=True)
        dy_b = dy.astype(BF16)
        dwo_ref[...] += _mm_tn(m_b, dy_b)
        dm = _mm_nt(dy_b, wo_ref[...])
        dya, dyb = dm * g0, dm * g1
        dg0, dg1 = dm * ya, dm * yb
        dmg = jnp.concatenate([dg0 * g0 * (1.0 - g0), dg1 * g1 * (1.0 - g1)], axis=1)
        dmg_ref[...] = dmg.astype(BF16)
        dbg_ref[...] += jnp.sum(dmg, axis=0, keepdims=True)
        dya_b, dyb_b = dya.astype(BF16), dyb.astype(BF16)
        dwa_ref[...] += _mm_tn(ga_b, dya_b)
        dwb_ref[...] += _mm_tn(gb_b, dyb_b)
        dga = _mm_nt(dya_b, wa_ref[...])
        dgb = _mm_nt(dyb_b, wb_ref[...])
        dattn_ref[...] = dga * sa
        dga_ref[...] = (dga * attn * _dsilu(za, sga)).astype(BF16)
        dgb_ref[...] = (dgb * on * _dsilu(zb, sgb)).astype(BF16)
        don = dgb * sb
        dghg_ref[...] += jnp.sum(don * (orw * rh), axis=0, keepdims=True)
        dgo = don * ghg
        dor_ref[...] = rh * dgo - orw * (rh * rh * rh) * (_group_sum(orw * dgo) * (1.0 / V_DIM))

    row = lambda w, j=0: pl.BlockSpec((TM, w), lambda i: (i, j))
    full = lambda a: pl.BlockSpec(a.shape, lambda i: (0,) * a.ndim)
    acc = lambda shape: pl.BlockSpec(shape, lambda i: (0, 0))
    sds = jax.ShapeDtypeStruct
    return pl.pallas_call(
        body, name="mid", grid=(s // TM,),
        in_specs=[row(512), row(512, P_GA // 512), row(512), row(512, P_GB // 512), row(2048, P_MERGE // 2048), row(D), row(D),
                  full(g_hg), full(b_gate), full(g_post), full(wa), full(wb), full(w_out)],
        out_specs=[acc((1, 1)), row(D), row(512), row(512), row(512), row(512), row(2048),
                   acc((D, D)), acc((512, D)), acc((512, D)), acc((1, D)), acc((1, 2048)), acc((1, 512))],
        out_shape=[sds((1, 1), F32), sds((s, D), F32), sds((s, 512), F32), sds((s, 512), BF16), sds((s, 512), F32), sds((s, 512), BF16),
                   sds((s, 2048), BF16), sds((D, D), F32), sds((512, D), F32), sds((512, D), F32), sds((1, D), F32),
                   sds((1, 2048), F32), sds((1, 512), F32)],
        compiler_params=_params(("arbitrary",)),
    )(attn, proj, o_raw, proj, proj, x, tgt, g_hg, b_gate, g_post, wa, wb, w_out)


def _attn_bwd(q, k, vv, attn, dattn, lse):
    s = q.shape[1]
    nt = s // TQ
    scale = 1.0 / math.sqrt(QK_NOPE + QK_ROPE)

    def body(q_ref, k_ref, v_ref, o_ref, do_ref, lse_ref, dq_ref, dk_ref, dv_ref, do_s, delta_s):
        j = pl.program_id(1)

        @pl.when(j == 0)
        def _():
            dq_ref[...] = jnp.zeros_like(dq_ref)
            lane = lax.broadcasted_iota(jnp.int32, (TQ, LANE), 1)

            @pl.loop(0, nt)
            def _(i):
                rows = pl.ds(pl.multiple_of(i * TQ, TQ), TQ)
                do, o = do_ref[rows, :], o_ref[rows, :]
                for hh in range(2):
                    doh = jnp.where((lane < 64) if hh == 0 else (lane >= 64), do, 0.0)
                    do_s[hh, rows, :] = doh.astype(BF16)
                    delta_s[hh, rows, :] = jnp.broadcast_to(jnp.sum(doh * o, axis=-1, keepdims=True), (TQ, LANE))

        kjs, vjs = (k_ref[0], k_ref[1]), (v_ref[0], v_ref[1])
        wide = lambda a: jnp.concatenate([a] * (TQ // LANE), axis=1)

        def tile(hh, i, carry, diag):
            dk, dv = carry
            rows = pl.ds(pl.multiple_of(i * TQ, TQ), TQ)
            qi, do_b = q_ref[hh, rows, :], do_s[hh, rows, :]
            p = jnp.exp2(_mm_nt(qi, kjs[hh]) * QK_SCALE2 - wide(lse_ref[hh, rows, :]))
            if diag:
                p = jnp.where(_diag_visible(), p, 0.0)
            dv = dv + _mm_tn(p.astype(BF16), do_b)
            ds_b = (p * (_mm_nt(do_b, vjs[hh]) - wide(delta_s[hh, rows, :]))).astype(BF16)
            dk = dk + _mm_tn(ds_b, qi)
            dq_ref[hh, rows, :] += _mm(ds_b, kjs[hh])
            return dk, dv

        def step(i, carry):
            return tile(0, i, carry[0], False), tile(1, i, carry[1], False)

        z = jnp.zeros((TQ, LANE), F32)
        first = (tile(0, j, (z, z), True), tile(1, j, (z, z), True))
        carry = lax.fori_loop(j + 1, nt, step, first)
        for hh in range(2):
            dk_ref[hh] = carry[hh][0] * scale
            dv_ref[hh] = carry[hh][1]

        @pl.when(j == nt - 1)
        def _():
            dq_ref[...] = dq_ref[...] * scale

    whole = pl.BlockSpec((2, s, LANE), lambda p, j: (p, 0, 0))
    tile_spec = pl.BlockSpec((2, TQ, LANE), lambda p, j: (p, j, 0))
    cols = pl.BlockSpec((s, LANE), lambda p, j: (0, p))
    hs = jax.ShapeDtypeStruct((NH, s, LANE), F32)
    return pl.pallas_call(
        body, name="attn_bwd", grid=(NH // 2, nt),
        in_specs=[whole, tile_spec, tile_spec, cols, cols, whole],
        out_specs=[whole, tile_spec, tile_spec],
        out_shape=[hs, hs, hs],
        scratch_shapes=[pltpu.VMEM((2, s, LANE), BF16), pltpu.VMEM((2, s, LANE), F32)],
        compiler_params=_params(("parallel", "arbitrary")),
    )(q, k, vv, attn, dattn, lse)


def _hgrn_bwd(proj, lbl, states, do_raw):
    s = proj.shape[0]
    nt = s // TH
    nch = TH // HG_BLOCK

    def body(hq_ref, hf_ref, hi_ref, lbl_ref, st_ref, do_ref, dh_ref, dlbl_ref, dst, dlb):
        step = pl.program_id(1)

        @pl.when(step == 0)
        def _():
            dst[...] = jnp.zeros_like(dst)
            dlb[...] = jnp.zeros_like(dlb)

        lb = _lower_bound(lbl_ref[...])
        gt = _hgrn_gates(hq_ref[...], hf_ref[...], lb)
        m0, bd = _pair_masks()
        do = do_ref[...]
        qd, ki, ke = gt["qd"], gt["ki"], gt["ke"]
        v_b, do_b = hi_ref[...].astype(BF16), do.astype(BF16)
        qd_b, ki_b, ke_b = qd.astype(BF16), ki.astype(BF16), ke.astype(BF16)
        dv = jnp.zeros((TH, LANE), F32)
        dqd = jnp.zeros((TH, LANE), F32)
        dki = jnp.zeros((TH, LANE), F32)
        for hh in range(2):
            mh = m0 if hh == 0 else jnp.logical_not(m0)
            a_b = jnp.where(gt["tri"], _mm_nt(jnp.where(mh, qd, 0.0).astype(BF16), ki_b), 0.0).astype(BF16)
            doh_b = jnp.where(mh, do, 0.0).astype(BF16)
            da_b = jnp.where(gt["tri"], _mm_nt(doh_b, v_b), 0.0).astype(BF16)
            dv = dv + _mm_tn(a_b, doh_b)
            dqd = jnp.where(mh, _mm(da_b, ki_b), dqd)
            dki = jnp.where(mh, _mm_tn(da_b, qd_b), dki)
        dv_x, dke_x, dqd_x, dlast_x = [None] * nch, [None] * nch, [None] * nch, [None] * nch
        for n in reversed(range(nch)):
            sl = slice(n * HG_BLOCK, (n + 1) * HG_BLOCK)
            ds = dst[...]
            ds_b = ds.astype(BF16)
            sn = st_ref[0, n]
            dv_x[n] = _mm_nt(ke_b[sl], ds_b)
            dke_n = _mm(v_b[sl], ds_b)
            dke_x[n] = dke_n
            dqd_x[n] = _mm(do_b[sl], sn.astype(BF16))
            dn = jnp.exp(gt["lastb"][n * HG_BLOCK:n * HG_BLOCK + 1, :])
            dlast = jnp.sum(dke_n * ke[sl], axis=0, keepdims=True) + jnp.sum(ds * sn, axis=0, keepdims=True) * dn
            dlast_x[n] = jnp.broadcast_to(dlast, (HG_BLOCK, LANE))
            dst[...] = dn * ds + jnp.where(bd, _mm_tn(do_b[sl], qd_b[sl]), 0.0)
        dv = dv + jnp.concatenate(dv_x, axis=0)
        dqd = dqd + jnp.concatenate(dqd_x, axis=0)
        dke = jnp.concatenate(dke_x, axis=0)
        dk = dki * gt["ei"] + dke * gt["ee"]
        dcum = dqd * qd - dki * ki - dke * ke
        tri_t = gt["same"] & (gt["r"] <= gt["c"])
        dg = jnp.dot(tri_t.astype(F32), dcum, precision=HIGHEST, preferred_element_type=F32) + jnp.concatenate(dlast_x, axis=0)
        sig = gt["sig"]
        df = dg / gt["f"] - dk
        dlb[...] += jnp.sum(df * (1.0 - sig), axis=0, keepdims=True)
        dh_ref[0] = (dqd * gt["e"]).astype(BF16)
        dh_ref[1] = ((df * (1.0 - lb)) * sig * (1.0 - sig)).astype(BF16)
        dh_ref[2] = dv.astype(BF16)

        @pl.when(step == nt - 1)
        def _():
            da0 = dlb[...] * lb * (1.0 - lb)
            dlbl_ref[...] = jnp.concatenate([da0, -da0], axis=0)

    col = lambda base: pl.BlockSpec((TH, LANE), lambda p, i: (nt - 1 - i, base // LANE + p))
    tile = pl.BlockSpec((TH, LANE), lambda p, i: (nt - 1 - i, p))
    sds = jax.ShapeDtypeStruct
    return pl.pallas_call(
        body, name="hgrn_bwd", grid=(NH // 2, nt),
        in_specs=[col(P_HQ), col(P_HF), col(P_HI), pl.BlockSpec((2, LANE), lambda p, i: (0, p)),
                  pl.BlockSpec((1, nch, LANE, LANE), lambda p, i: (p, nt - 1 - i, 0, 0)), tile],
        out_specs=[pl.BlockSpec((3, TH, LANE), lambda p, i: (0, nt - 1 - i, p)), pl.BlockSpec((2, LANE), lambda p, i: (0, p))],
        out_shape=[sds((3, s, 512), BF16), sds((2, 512), F32)],
        scratch_shapes=[pltpu.VMEM((LANE, LANE), F32), pltpu.VMEM((1, LANE), F32)],
        compiler_params=_params(("parallel", "arbitrary")),
    )(proj, proj, proj, lbl, states, do_raw)


def _norm_rows_bwd(v, r, g, dn):
    dgv = dn * g
    return r * dgv - v * (r * r * r) * jnp.mean(v * dgv, axis=-1, keepdims=True)


def _qkv_bwd(proj, dq, dk, dvv, g_q, g_kv, w_uq_p, w_k_p, w_v_p, rc, rs1, rs2):
    s = proj.shape[0]

    def body(cq_ref, ckv_ref, dq_ref, dk_ref, dv_ref, gq_ref, gkv_ref, wq_ref, wk_ref, wv_ref, c_ref, s1_ref, s2_ref,
             dcq_ref, dckv_ref, dkpe_ref, dwq_ref, dwk_ref, dwv_ref, dgq_ref, dgkv_ref):
        @pl.when(pl.program_id(0) == 0)
        def _():
            for rf in (dwq_ref, dwk_ref, dwv_ref, dgq_ref, dgkv_ref):
                rf[...] = jnp.zeros_like(rf)

        c, s1, s2 = c_ref[...], s1_ref[...], s2_ref[...]
        cq, ckv = cq_ref[...], ckv_ref[...]
        gq, gkv = gq_ref[...], gkv_ref[...]
        cqn, rq = _norm_rows(cq, gq)
        ckvn, rkv = _norm_rows(ckv, gkv)
        cqn_b, ckvn_b = cqn.astype(BF16), ckvn.astype(BF16)
        dqf = jnp.concatenate([_rope_t(dq_ref[h], c, s1, s2) for h in range(NH)], axis=1).astype(BF16)
        dkf = jnp.concatenate([dk_ref[h] for h in range(NH)], axis=1).astype(BF16)
        dvf = jnp.concatenate([dv_ref[h] for h in range(NH)], axis=1).astype(BF16)
        dkpe = dk_ref[0]
        for h in range(1, NH):
            dkpe = dkpe + dk_ref[h]
        lane = lax.broadcasted_iota(jnp.int32, (TM, LANE), 1)
        dkpe = jnp.where((lane >= QK_NOPE) & (lane < QK_NOPE + QK_ROPE), dkpe, 0.0)
        dkpe_ref[...] = _rope_t(dkpe, c, s1, s2).astype(BF16)
        dwq_ref[...] += _mm_tn(cqn_b, dqf)
        dwk_ref[...] += _mm_tn(ckvn_b, dkf)
        dwv_ref[...] += _mm_tn(ckvn_b, dvf)
        dcqn = _mm_nt(dqf, wq_ref[...])
        dckvn = _mm_nt(dkf, wk_ref[...]) + _mm_nt(dvf, wv_ref[...])
        dgq_ref[...] += jnp.sum(dcqn * (cq * rq), axis=0, keepdims=True)
        dgkv_ref[...] += jnp.sum(dckvn * (ckv * rkv), axis=0, keepdims=True)
        dcq_ref[...] = _norm_rows_bwd(cq, rq, gq, dcqn).astype(BF16)
        dckv_ref[...] = _norm_rows_bwd(ckv, rkv, gkv, dckvn).astype(BF16)

    row = lambda w, j=0: pl.BlockSpec((TM, w), lambda i: (i, j))
    full = lambda a: pl.BlockSpec(a.shape, lambda i: (0,) * a.ndim)
    acc = lambda shape: pl.BlockSpec(shape, lambda i: (0, 0))
    heads = pl.BlockSpec((NH, TM, LANE), lambda i: (0, i, 0))
    sds = jax.ShapeDtypeStruct
    return pl.pallas_call(
        body, name="qkv_bwd", grid=(s // TM,),
        in_specs=[row(Q_LORA, P_CQ // Q_LORA), row(KV_LORA, P_CKV // KV_LORA), heads, heads, heads,
                  full(g_q), full(g_kv), full(w_uq_p), full(w_k_p), full(w_v_p), row(LANE), row(LANE), row(LANE)],
        out_specs=[row(Q_LORA), row(KV_LORA), row(LANE), acc((Q_LORA, D)), acc((KV_LORA, D)), acc((KV_LORA, D)),
                   acc((1, Q_LORA)), acc((1, KV_LORA))],
        out_shape=[sds((s, Q_LORA), BF16), sds((s, KV_LORA), BF16), sds((s, LANE), BF16), sds((Q_LORA, D), F32),
                   sds((KV_LORA, D), F32), sds((KV_LORA, D), F32), sds((1, Q_LORA), F32), sds((1, KV_LORA), F32)],
        compiler_params=_params(("arbitrary",)),
    )(proj, proj, dq, dk, dvv, g_q, g_kv, w_uq_p, w_k_p, w_v_p, rc, rs1, rs2)


def _front_bwd(x, dout, dmg, dga, dh3, dgb, dcq, dckv, dkpe, g_pre, w_in_p):
    s = x.shape[0]

    def body(x_ref, do_ref, dmg_ref, dga_ref, dh3_ref, dgb_ref, dcq_ref, dckv_ref, dkpe_ref, g_ref, w_ref, gx_ref, dg_ref):
        @pl.when(pl.program_id(0) == 0)
        def _():
            dg_ref[...] = jnp.zeros_like(dg_ref)

        xv, g = x_ref[...], g_ref[...]
        _, r = _norm_rows(xv, g)
        pieces = ((dmg_ref[...], P_MERGE), (dga_ref[...], P_GA), (dh3_ref[0], P_HQ), (dh3_ref[1], P_HF), (dh3_ref[2], P_HI),
                  (dgb_ref[...], P_GB), (dcq_ref[...], P_CQ), (dckv_ref[...], P_CKV), (dkpe_ref[...], P_KPE))
        dh = jnp.zeros((TM, D), F32)
        for piece, off in pieces:
            dh = dh + _mm_nt(piece, w_ref[:, off:off + piece.shape[1]])
        dg_ref[...] += jnp.sum(dh * (xv * r), axis=0, keepdims=True)
        gx_ref[...] = do_ref[...] + _norm_rows_bwd(xv, r, g, dh)

    row = lambda w: pl.BlockSpec((TM, w), lambda i: (i, 0))
    full = lambda a: pl.BlockSpec(a.shape, lambda i: (0,) * a.ndim)
    sds = jax.ShapeDtypeStruct
    return pl.pallas_call(
        body, name="front_bwd", grid=(s // TM,),
        in_specs=[row(D), row(D), row(2048), row(512), pl.BlockSpec((3, TM, 512), lambda i: (0, i, 0)), row(512), row(Q_LORA),
                  row(KV_LORA), row(LANE), full(g_pre), full(w_in_p)],
        out_specs=[row(D), pl.BlockSpec((1, D), lambda i: (0, 0))],
        out_shape=[sds((s, D), F32), sds((1, D), F32)],
        compiler_params=_params(("arbitrary",)),
    )(x, dout, dmg, dga, dh3, dgb, dcq, dckv, dkpe, g_pre, w_in_p)


TK_GRAD = 512


def _win_grad(h, pieces, name):
    s = h.shape[0]
    n = len(pieces)

    def body(h_ref, *refs):
        @pl.when(pl.program_id(0) == 0)
        def _():
            for o_ref in refs[n:]:
                o_ref[...] = jnp.zeros_like(o_ref)

        hv = h_ref[...]
        for d_ref, o_ref in zip(refs[:n], refs[n:]):
            if len(d_ref.shape) == 3:
                for k in range(d_ref.shape[0]):
                    o_ref[k] += _mm_tn(hv, d_ref[k])
            else:
                o_ref[...] += _mm_tn(hv, d_ref[...])

    def in_spec(p):
        if p.ndim == 3:
            return pl.BlockSpec((p.shape[0], TK_GRAD, p.shape[2]), lambda kk: (0, kk, 0))
        return pl.BlockSpec((TK_GRAD, p.shape[1]), lambda kk: (kk, 0))

    out_shapes = [(p.shape[0], D, p.shape[2]) if p.ndim == 3 else (D, p.shape[1]) for p in pieces]
    return pl.pallas_call(
        body, name=name, grid=(s // TK_GRAD,),
        in_specs=[pl.BlockSpec((TK_GRAD, D), lambda kk: (kk, 0))] + [in_spec(p) for p in pieces],
        out_specs=[pl.BlockSpec(sh, lambda kk, nd=len(sh): (0,) * nd) for sh in out_shapes],
        out_shape=[jax.ShapeDtypeStruct(sh, F32) for sh in out_shapes],
        compiler_params=_params(("arbitrary",)),
    )(h, *pieces)


def _pad_win(w_in):
    z = lambda n: jnp.zeros((w_in.shape[0], n), w_in.dtype)
    sl = lambda o, n: w_in[:, o:o + n]
    return jnp.concatenate([sl(O_MERGE, 2048), sl(O_GA, 512), sl(O_HQ, 512), sl(O_HF, 512), sl(O_HI, 512), sl(O_GB, 512),
                            sl(O_CQ, Q_LORA), sl(O_CKV, KV_LORA), z(64), sl(O_KPE, QK_ROPE), z(32)], axis=1)


def _unpad_win(g):
    sl = lambda o, n: g[:, o:o + n]
    return jnp.concatenate([sl(P_CQ, Q_LORA), sl(P_CKV, KV_LORA), sl(P_KPE + 64, QK_ROPE), sl(P_GA, 512), sl(P_HQ, 512),
                            sl(P_HF, 512), sl(P_HI, 512), sl(P_GB, 512), sl(P_MERGE, 2048)], axis=1)


def _pad_wuq(w_uq):
    w = w_uq.reshape(Q_LORA, NH, QK_NOPE + QK_ROPE)
    return jnp.pad(w, ((0, 0), (0, 0), (0, LANE - QK_NOPE - QK_ROPE))).reshape(Q_LORA, NH * LANE)


def _unpad_wuq(g):
    return g.reshape(Q_LORA, NH, LANE)[:, :, :QK_NOPE + QK_ROPE].reshape(Q_LORA, NH * (QK_NOPE + QK_ROPE))


def _pad_wukv(w_ukv):
    w = w_ukv.reshape(KV_LORA, NH, QK_NOPE + V_DIM)
    w_k = jnp.pad(w[:, :, :QK_NOPE], ((0, 0), (0, 0), (0, LANE - QK_NOPE))).reshape(KV_LORA, NH * LANE)
    wv = w[:, :, QK_NOPE:].reshape(KV_LORA, NH // 2, 2, 1, V_DIM)
    eye = jnp.eye(2, dtype=w.dtype).reshape(1, 1, 2, 2, 1)
    return w_k, (wv * eye).reshape(KV_LORA, NH * LANE)


def _unpad_wukv(gk, gv):
    gk = gk.reshape(KV_LORA, NH, LANE)[:, :, :QK_NOPE]
    gv = gv.reshape(KV_LORA, NH // 2, 2, 2, V_DIM)
    gv = jnp.stack([gv[:, :, 0, 0], gv[:, :, 1, 1]], axis=2).reshape(KV_LORA, NH, V_DIM)
    return jnp.concatenate([gk, gv], axis=-1).reshape(KV_LORA, NH * (QK_NOPE + V_DIM))


def _local_step(x, tgt, g_pre, w_in, b_gate, g_q, w_uq, g_kv, w_ukv, lb_logits, g_hgrn, wa, wb, w_out, g_post):
    s = x.shape[0]
    w_in_p = _pad_win(w_in)
    w_uq_p = _pad_wuq(w_uq)
    w_k_p, w_v_p = _pad_wukv(w_ukv)
    rc, rs1, rs2 = _rope_tables(s)
    g_hg = jnp.tile(g_hgrn, (1, NH))

    proj, h = _front_fwd(x, g_pre, w_in_p)
    q, k, vv = _qkv_fwd(proj, g_q, g_kv, w_uq_p, w_k_p, w_v_p, rc, rs1, rs2)
    attn, lse = _attn_fwd(q, k, vv)
    o_raw, states = _hgrn_fwd(proj, lb_logits)
    (loss, dout, dattn, dga, dor, dgb, dmg, d_wout, d_wa, d_wb, d_gpost, d_bgate, d_ghg) = _mid(
        proj, attn, o_raw, x, tgt, g_hg, b_gate, g_post, wa, wb, w_out)
    w_mg, w_ga, w_gb = _win_grad(h, [dmg, dga, dgb], "win_grad_mid")
    dq, dk, dvv = _attn_bwd(q, k, vv, attn, dattn, lse)
    dh3, d_lbl = _hgrn_bwd(proj, lb_logits, states, dor)
    (w_h3,) = _win_grad(h, [dh3], "win_grad_hgrn")
    dcq, dckv, dkpe, d_wuq_p, d_wk_p, d_wv_p, d_gq, d_gkv = _qkv_bwd(proj, dq, dk, dvv, g_q, g_kv, w_uq_p, w_k_p, w_v_p, rc, rs1, rs2)
    w_cq, w_ckv, w_kpe = _win_grad(h, [dcq, dckv, dkpe], "win_grad_qkv")
    grad_x, d_gpre = _front_bwd(x, dout, dmg, dga, dh3, dgb, dcq, dckv, dkpe, g_pre, w_in_p)
    d_win = jnp.concatenate([w_cq, w_ckv, w_kpe[:, 64:64 + QK_ROPE], w_ga, w_h3[0], w_h3[1], w_h3[2], w_gb, w_mg], axis=1)
    grads = dict(w_in=d_win, w_uq=_unpad_wuq(d_wuq_p), w_ukv=_unpad_wukv(d_wk_p, d_wv_p), w_branch_a=d_wa, w_branch_b=d_wb, w_out=d_wout)
    vec_grads = dict(g_pre=d_gpre, b_gate=d_bgate, g_q=d_gq, g_kv=d_gkv, lb_logits=d_lbl, g_hgrn=d_ghg, g_post=d_gpost)
    return loss, grad_x, grads, vec_grads


SHARD_SHAPES = (("w_in", (1024, 1416)), ("w_uq", (192, 768)), ("w_ukv", (256, 256)), ("w_branch_a", (512, 256)),
                ("w_branch_b", (512, 256)), ("w_out", (256, 1024)))
BIG = tuple(n for n, _ in SHARD_SHAPES)
ROW_SHARDED = ("w_uq", "w_out")
N_CHIPS = 4
VEC_ROWS = (("g_pre", 0, 1024), ("b_gate", 1, 2048), ("g_q", 2, 768), ("g_kv", 3, 256), ("g_hgrn", 6, 64), ("g_post", 7, 1024))
VEC_LB_ROW = 4
VEC_SHAPE = (8, 2048)


def _split_by_chip(name, g):
    a, b = dict(SHARD_SHAPES)[name]
    return g.reshape(N_CHIPS, a, b) if name in ROW_SHARDED else g.reshape(a, N_CHIPS, b).transpose(1, 0, 2)


def _join_chips(name, w):
    a, b = dict(SHARD_SHAPES)[name]
    return w.reshape(N_CHIPS * a, b) if name in ROW_SHARDED else w.transpose(1, 0, 2).reshape(a, N_CHIPS * b)


MESH = pl.DeviceIdType.MESH
HBM = pl.BlockSpec(memory_space=pltpu.HBM)


def _mesh_place():
    x, y, c = lax.axis_index("x"), lax.axis_index("y"), lax.axis_index("c")
    return x, y, c, 2 * x + y, [(1 - x, y), (x, 1 - y), (1 - x, 1 - y)]


def _remote(src, dst, send_sems, recv_sems, k, to):
    return pltpu.make_async_remote_copy(src_ref=src, dst_ref=dst, send_sem=send_sems.at[k], recv_sem=recv_sems.at[k],
                                        device_id=to, device_id_type=MESH)


def _gather_weights(shards):
    n = len(shards)

    def body(*refs):
        srcs, outs = refs[:n], refs[n:2 * n]
        ici_send, ici_recv, d2d_send, d2d_recv, local_sems = refs[2 * n:]
        x, y, c, me, chips = _mesh_place()
        sibling = (x, y, 1 - c)

        def half(ref, k, which):
            rows = shards[k].shape[0] // 2
            return ref.at[pl.ds(pl.multiple_of(which * rows, rows), rows)]

        own = [pltpu.make_async_copy(srcs[k], outs[k].at[me], local_sems.at[k]) for k in range(n)]
        for cp in own:
            cp.start()
        started = []
        for k in range(n):
            for j, (px, py) in enumerate(chips):
                cp = _remote(half(srcs[k], k, c), half(outs[k].at[me], k, c), ici_send, ici_recv, 3 * k + j, (px, py, c))
                cp.start()
                started.append(cp)
        for k in range(n):
            for j, (px, py) in enumerate(chips):
                landed = half(outs[k].at[2 * px + py], k, c)
                _remote(landed, landed, ici_send, ici_recv, 3 * k + j, (px, py, c)).wait_recv()
                cp = _remote(landed, landed, d2d_send, d2d_recv, 3 * k + j, sibling)
                cp.start()
                started.append(cp)
        for k in range(n):
            for j, (px, py) in enumerate(chips):
                other = half(outs[k].at[2 * px + py], k, 1 - c)
                _remote(other, other, d2d_send, d2d_recv, 3 * k + j, sibling).wait_recv()
        for cp in started:
            cp.wait_send()
        for cp in own:
            cp.wait()

    sems = pltpu.SemaphoreType.DMA((3 * n,))
    return pl.pallas_call(
        body, name="gather_weights", in_specs=[HBM] * n, out_specs=[HBM] * n,
        out_shape=[jax.ShapeDtypeStruct((N_CHIPS,) + s.shape, s.dtype) for s in shards],
        scratch_shapes=[sems, sems, sems, sems, pltpu.SemaphoreType.DMA((n,))],
        compiler_params=pltpu.CompilerParams(has_side_effects=True),
    )(*shards)


def _chip_exchange(srcs, name):
    n = len(srcs)

    def body(*refs):
        src_refs, outs = refs[:n], refs[n:2 * n]
        send_sems, recv_sems, local_sems = refs[2 * n:]
        x, y, c, me, chips = _mesh_place()

        def slab(k, t):
            return src_refs[k] if srcs[k].ndim == 2 else src_refs[k].at[t]

        own = [pltpu.make_async_copy(slab(k, me), outs[k].at[me], local_sems.at[k]) for k in range(n)]
        for cp in own:
            cp.start()
        sends = []
        for k in range(n):
            for j, (px, py) in enumerate(chips):
                cp = _remote(slab(k, 2 * px + py), outs[k].at[me], send_sems, recv_sems, 3 * k + j, (px, py, c))
                cp.start()
                sends.append(cp)
        for k in range(n):
            for j, (px, py) in enumerate(chips):
                _remote(slab(k, me), outs[k].at[2 * px + py], send_sems, recv_sems, 3 * k + j, (px, py, c)).wait_recv()
        for cp in sends:
            cp.wait_send()
        for cp in own:
            cp.wait()

    sems = pltpu.SemaphoreType.DMA((3 * n,))
    return pl.pallas_call(
        body, name=name, in_specs=[HBM] * n, out_specs=[HBM] * n,
        out_shape=[jax.ShapeDtypeStruct((N_CHIPS,) + s.shape[-2:], s.dtype) for s in srcs],
        scratch_shapes=[sems, sems, pltpu.SemaphoreType.DMA((n,))],
        compiler_params=pltpu.CompilerParams(has_side_effects=True),
    )(*srcs)


def _sibling_exchange(srcs, name):
    n = len(srcs)

    def body(*refs):
        src_refs, outs = refs[:n], refs[n:2 * n]
        send_sems, recv_sems = refs[2 * n:]
        sibling = (lax.axis_index("x"), lax.axis_index("y"), 1 - lax.axis_index("c"))
        copies = [_remote(src_refs[k], outs[k], send_sems, recv_sems, k, sibling) for k in range(n)]
        for cp in copies:
            cp.start()
        for cp in copies:
            cp.wait()

    sems = pltpu.SemaphoreType.DMA((n,))
    return pl.pallas_call(
        body, name=name, in_specs=[HBM] * n, out_specs=[HBM] * n,
        out_shape=[jax.ShapeDtypeStruct(s.shape, s.dtype) for s in srcs],
        scratch_shapes=[sems, sems],
        compiler_params=pltpu.CompilerParams(has_side_effects=True),
    )(*srcs)


ROW_TILE = 256


def _sum_chips(parts, name):
    _, a, b = parts.shape
    ta = min(a, ROW_TILE)

    def body(p_ref, o_ref):
        f = lambda t: p_ref[t].astype(F32)
        o_ref[...] = ((f(0) + f(1)) + f(2)) + f(3)

    return pl.pallas_call(
        body, name=name, grid=(a // ta,),
        in_specs=[pl.BlockSpec((N_CHIPS, ta, b), lambda i: (0, i, 0))],
        out_specs=pl.BlockSpec((ta, b), lambda i: (i, 0)),
        out_shape=jax.ShapeDtypeStruct((a, b), F32),
        compiler_params=_params(("parallel",)),
    )(parts)


def _adamw_math(g, w, m, v):
    nm = ADAM_B1 * m + (1.0 - ADAM_B1) * g
    nv = ADAM_B2 * v + (1.0 - ADAM_B2) * (g * g)
    m_hat = nm / (1.0 - ADAM_B1 ** ADAM_STEP)
    v_hat = nv / (1.0 - ADAM_B2 ** ADAM_STEP)
    return -ADAM_LR * (m_hat / (jnp.sqrt(v_hat) + ADAM_EPS) + ADAM_WD * w), nm, nv


def _adamw(p_mine, p_sibling, w, m, v, name):
    a, b = p_mine.shape
    ta = min(a, ROW_TILE)

    def body(a_ref, b_ref, w_ref, m_ref, v_ref, g_ref, d_ref, nm_ref, nv_ref):
        g = a_ref[...] + b_ref[...]
        g_ref[0] = g
        d_ref[0], nm_ref[0], nv_ref[0] = _adamw_math(g, w_ref[0], m_ref[0], v_ref[0])

    part = pl.BlockSpec((ta, b), lambda i: (i, 0))
    tile = pl.BlockSpec((1, ta, b), lambda i: (0, i, 0))
    sds = jax.ShapeDtypeStruct((1, a, b), F32)
    return pl.pallas_call(
        body, name=name, grid=(a // ta,), in_specs=[part, part, tile, tile, tile], out_specs=[tile] * 4, out_shape=[sds] * 4,
        compiler_params=_params(("parallel",)),
    )(p_mine, p_sibling, w, m, v)


def _vec_pack(vg):
    names = [n for n, _, _ in VEC_ROWS]

    def body(*refs):
        o_ref = refs[-1]
        lb_ref = refs[len(names)]
        o_ref[...] = jnp.zeros_like(o_ref)
        for (name, row, size), ref in zip(VEC_ROWS, refs):
            if name == "g_hgrn":
                r = lax.broadcasted_iota(jnp.int32, (NH * V_DIM, LANE), 0)
                c = lax.broadcasted_iota(jnp.int32, (NH * V_DIM, LANE), 1)
                fold = ((r % V_DIM) == c).astype(F32)
                o_ref[row:row + 1, 0:LANE] = jnp.dot(ref[...], fold, precision=HIGHEST, preferred_element_type=F32)
            else:
                o_ref[row:row + 1, 0:size] = ref[...]
        o_ref[VEC_LB_ROW:VEC_LB_ROW + 2, 0:512] = lb_ref[...]

    return pl.pallas_call(body, name="vec_pack", out_shape=jax.ShapeDtypeStruct(VEC_SHAPE, F32))(
        *[vg[n] for n in names], vg["lb_logits"])


def _adamw_vec(p_mine, p_sibling, w, m, v):
    names = [n for n, _, _ in VEC_ROWS] + ["lb_logits"]
    k = len(names)

    def body(a_ref, b_ref, *refs):
        ins, outs = refs[:3 * k], refs[3 * k:]
        for i, name in enumerate(names):
            if name == "lb_logits":
                rows, cols = slice(VEC_LB_ROW, VEC_LB_ROW + 2), slice(0, 512)
            else:
                _, row, size = VEC_ROWS[i]
                rows, cols = slice(row, row + 1), slice(0, size)
            g = a_ref[rows, cols] + b_ref[rows, cols]
            d, nm, nv = _adamw_math(g, ins[i][...], ins[k + i][...], ins[2 * k + i][...])
            for o_ref, val in zip(outs[4 * i:4 * i + 4], (g, d, nm, nv)):
                o_ref[...] = val

    shapes = [jax.ShapeDtypeStruct(w[n].shape, F32) for n in names for _ in range(4)]
    res = pl.pallas_call(body, name="adamw_vec", out_shape=shapes)(
        p_mine, p_sibling, *[w[n] for n in names], *[m[n] for n in names], *[v[n] for n in names])
    return [{n: res[4 * i + j] for i, n in enumerate(names)} for j in range(4)]


WEIGHTS = ("g_pre", "w_in", "b_gate", "g_q", "w_uq", "g_kv", "w_ukv", "lb_logits", "g_hgrn", "w_branch_a", "w_branch_b", "w_out", "g_post")


def kernel(x, g_pre, w_in, b_gate, g_q, w_uq, g_kv, w_ukv, lb_logits, g_hgrn, w_branch_a, w_branch_b, w_out, g_post, loss_target, m_g_pre, m_w_in, m_b_gate, m_g_q, m_w_uq, m_g_kv, m_w_ukv, m_lb_logits, m_g_hgrn, m_w_branch_a, m_w_branch_b, m_w_out, m_g_post, v_g_pre, v_w_in, v_b_gate, v_g_q, v_w_uq, v_g_kv, v_w_ukv, v_lb_logits, v_g_hgrn, v_w_branch_a, v_w_branch_b, v_w_out, v_g_post):
    w = dict(g_pre=g_pre, w_in=w_in, b_gate=b_gate, g_q=g_q, w_uq=w_uq, g_kv=g_kv, w_ukv=w_ukv, lb_logits=lb_logits, g_hgrn=g_hgrn,
             w_branch_a=w_branch_a, w_branch_b=w_branch_b, w_out=w_out, g_post=g_post)
    m = dict(g_pre=m_g_pre, w_in=m_w_in, b_gate=m_b_gate, g_q=m_g_q, w_uq=m_w_uq, g_kv=m_g_kv, w_ukv=m_w_ukv, lb_logits=m_lb_logits,
             g_hgrn=m_g_hgrn, w_branch_a=m_w_branch_a, w_branch_b=m_w_branch_b, w_out=m_w_out, g_post=m_g_post)
    v = dict(g_pre=v_g_pre, w_in=v_w_in, b_gate=v_b_gate, g_q=v_g_q, w_uq=v_w_uq, g_kv=v_g_kv, w_ukv=v_w_ukv, lb_logits=v_lb_logits,
             g_hgrn=v_g_hgrn, w_branch_a=v_w_branch_a, w_branch_b=v_w_branch_b, w_out=v_w_out, g_post=v_g_post)
    gathered = _gather_weights([w[n][0].astype(BF16) for n in BIG])
    big = {n: _join_chips(n, g) for n, g in zip(BIG, gathered)}
    loss, grad_x, grads, vec_grads = _local_step(
        x[0], loss_target[0], g_pre, big["w_in"], b_gate, g_q, big["w_uq"], g_kv, big["w_ukv"], lb_logits, g_hgrn,
        big["w_branch_a"], big["w_branch_b"], big["w_out"], g_post)

    slabs = [_split_by_chip(n, grads[n]).astype(BF16) for n in BIG]
    landed = _chip_exchange(slabs + [_vec_pack(vec_grads)], "scatter_grads")
    mine = [_sum_chips(p, "sum_chips_" + n) for p, n in zip(landed, BIG + ("vec",))]
    theirs = _sibling_exchange(mine, "sibling_grads")
    outs = [{}, {}, {}, {}]
    for k, n in enumerate(BIG):
        for o, val in zip(outs, _adamw(mine[k], theirs[k], w[n], m[n], v[n], "adamw_" + n)):
            o[n] = val
    for o, vals in zip(outs, _adamw_vec(mine[-1], theirs[-1], w, m, v)):
        o.update(vals)

    total = lax.psum(loss[0, 0], ("x", "y", "c"))
    return (total, grad_x[None], *[o[n] for o in outs for n in WEIGHTS])
```

```python
import functools
import math

import numpy as np
import jax
import jax.numpy as jnp
from jax import lax
from jax.experimental import pallas as pl
from jax.experimental.pallas import tpu as pltpu

F32 = jnp.float32
BF16 = jnp.bfloat16
HIGHEST = lax.Precision.HIGHEST

D = 1024
NH = 8
QK_NOPE, QK_ROPE, V_DIM = 64, 32, 64
Q_LORA, KV_LORA = 768, 256
CHUNK = 64
HG_BLOCK = 32
EPS = 1e-6
D_IN = 5664
LANE = 128
P_MERGE, P_GA, P_HQ, P_HF, P_HI, P_GB, P_CQ, P_CKV, P_KPE = 0, 2048, 2560, 3072, 3584, 4096, 4608, 5376, 5632
D_P = 5760
O_CQ, O_CKV, O_KPE, O_GA, O_HQ, O_HF, O_HI, O_GB, O_MERGE = 0, 768, 1024, 1056, 1568, 2080, 2592, 3104, 3616

TM = 256
TQ = 512
TH = 256
HG_PAIRS = 2
VMEM_LIMIT = 56 * 1024 * 1024

ADAM_LR, ADAM_B1, ADAM_B2, ADAM_EPS, ADAM_WD, ADAM_STEP = 0.001, 0.9, 0.999, 1e-08, 0.01, 10

NT_DIMS = (((1,), (1,)), ((), ()))
TN_DIMS = (((0,), (0,)), ((), ()))


def _params(sem):
    return pltpu.CompilerParams(dimension_semantics=sem, vmem_limit_bytes=VMEM_LIMIT)


def _mm(a, b):
    return jnp.dot(a, b, preferred_element_type=F32)


def _mm_nt(a, b):
    return lax.dot_general(a, b, NT_DIMS, preferred_element_type=F32)


def _mm_tn(a, b):
    return lax.dot_general(a, b, TN_DIMS, preferred_element_type=F32)


def _sigmoid(z):
    return jax.nn.sigmoid(z)


def _rope(v, c, s1, s2):
    return v * c + pltpu.roll(v, 112, 1) * s1 + pltpu.roll(v, 16, 1) * s2


def _rope_t(dy, c, s1, s2):
    return dy * c + pltpu.roll(dy * s1, 16, 1) + pltpu.roll(dy * s2, 112, 1)


def _rope_tables(s):
    inv = 10000.0 ** (-jnp.arange(0, QK_ROPE, 2, dtype=F32) / QK_ROPE)
    ang = jnp.arange(s, dtype=F32)[:, None] * inv[None, :]
    cos, sin = jnp.cos(ang), jnp.sin(ang)
    z64, z32, o64, o32 = jnp.zeros((s, 64), F32), jnp.zeros((s, 32), F32), jnp.ones((s, 64), F32), jnp.ones((s, 32), F32)
    z16 = jnp.zeros((s, 16), F32)
    c = jnp.concatenate([o64, cos, cos, o32], axis=1)
    s1 = jnp.concatenate([z64, -sin, z16, z32], axis=1)
    s2 = jnp.concatenate([z64, z16, sin, z32], axis=1)
    return c, s1, s2


def _front_fwd(x, g_pre, w_in_p):
    s = x.shape[0]

    def body(x_ref, g_ref, w_ref, o_ref, h_ref):
        xv = x_ref[...]
        r = lax.rsqrt(jnp.mean(xv * xv, axis=-1, keepdims=True) + EPS)
        h = ((xv * r) * g_ref[...]).astype(BF16)
        h_ref[...] = h
        o_ref[...] = _mm(h, w_ref[...])

    return pl.pallas_call(
        body, name="front_fwd", grid=(s // TM,),
        in_specs=[pl.BlockSpec((TM, D), lambda i: (i, 0)), pl.BlockSpec((1, D), lambda i: (0, 0)),
                  pl.BlockSpec((D, D_P), lambda i: (0, 0))],
        out_specs=[pl.BlockSpec((TM, D_P), lambda i: (i, 0)), pl.BlockSpec((TM, D), lambda i: (i, 0))],
        out_shape=[jax.ShapeDtypeStruct((s, D_P), F32), jax.ShapeDtypeStruct((s, D), BF16)],
        compiler_params=_params(("parallel",)),
    )(x, g_pre, w_in_p)


def _norm_rows(v, g):
    r = lax.rsqrt(jnp.mean(v * v, axis=-1, keepdims=True) + EPS)
    return (v * r) * g, r


def _qkv_fwd(proj, g_q, g_kv, w_uq_p, w_k_p, w_v_p, rc, rs1, rs2):
    s = proj.shape[0]

    def body(cq_ref, ckv_ref, kpe_ref, gq_ref, gkv_ref, wq_ref, wk_ref, wv_ref, c_ref, s1_ref, s2_ref, q_ref, k_ref, v_ref):
        c, s1, s2 = c_ref[...], s1_ref[...], s2_ref[...]
        cqn, _ = _norm_rows(cq_ref[...], gq_ref[...])
        ckvn, _ = _norm_rows(ckv_ref[...], gkv_ref[...])
        ckvn = ckvn.astype(BF16)
        qf = _mm(cqn.astype(BF16), wq_ref[...])
        kf = _mm(ckvn, wk_ref[...])
        vf = _mm(ckvn, wv_ref[...])
        kpe = _rope(kpe_ref[...], c, s1, s2)
        for h in range(NH):
            blk = slice(h * LANE, (h + 1) * LANE)
            q_ref[h] = _rope(qf[:, blk], c, s1, s2).astype(BF16)
            k_ref[h] = (kf[:, blk] + kpe).astype(BF16)
            v_ref[h] = vf[:, blk].astype(BF16)

    row = lambda w, j: pl.BlockSpec((TM, w), lambda i: (i, j))
    full = lambda a: pl.BlockSpec(a.shape, lambda i: (0,) * a.ndim)
    hs = jax.ShapeDtypeStruct((NH, s, LANE), BF16)
    return pl.pallas_call(
        body, name="qkv_fwd", grid=(s // TM,),
        in_specs=[row(Q_LORA, P_CQ // Q_LORA), row(KV_LORA, P_CKV // KV_LORA), row(LANE, P_KPE // LANE),
                  full(g_q), full(g_kv), full(w_uq_p), full(w_k_p), full(w_v_p), row(LANE, 0), row(LANE, 0), row(LANE, 0)],
        out_specs=[pl.BlockSpec((NH, TM, LANE), lambda i: (0, i, 0))] * 3,
        out_shape=[hs, hs, hs],
        compiler_params=_params(("parallel",)),
    )(proj, proj, proj, g_q, g_kv, w_uq_p, w_k_p, w_v_p, rc, rs1, rs2)


LOG2E = 1.4426950408889634
QK_SCALE2 = LOG2E / math.sqrt(QK_NOPE + QK_ROPE)


def _diag_visible():
    row = lax.broadcasted_iota(jnp.int32, (TQ, TQ), 0)
    col = lax.broadcasted_iota(jnp.int32, (TQ, TQ), 1)
    return (col // CHUNK) <= (row // CHUNK)


def _attn_fwd(q, k, vv):
    s = q.shape[1]

    def body(q_ref, k_ref, v_ref, o_ref, lse_ref):
        i = pl.program_id(1)
        qs = (q_ref[0], q_ref[1])

        def tile(hh, t, carry, diag):
            m, l, acc = carry
            rows = pl.ds(pl.multiple_of(t * TQ, TQ), TQ)
            sc = _mm_nt(qs[hh], k_ref[hh, rows, :])
            if diag:
                sc = jnp.where(_diag_visible(), sc, -jnp.inf)
            m_new = jnp.maximum(m, jnp.max(sc, axis=-1, keepdims=True))
            alpha = jnp.exp2((m - m_new) * QK_SCALE2)
            p = jnp.exp2((sc - m_new) * QK_SCALE2)
            l = alpha * l + jnp.sum(p, axis=-1, keepdims=True)
            acc = alpha * acc + _mm(p.astype(BF16), v_ref[hh, rows, :])
            return m_new, l, acc

        def step(t, carry):
            return tile(0, t, carry[0], False), tile(1, t, carry[1], False)

        init = (jnp.full((TQ, 1), -jnp.inf, F32), jnp.zeros((TQ, 1), F32), jnp.zeros((TQ, LANE), F32))
        carry = lax.fori_loop(0, i, step, (init, init))
        out = jnp.zeros((TQ, LANE), F32)
        for hh in range(2):
            m, l, acc = tile(hh, i, carry[hh], True)
            out = out + acc / l
            lse_ref[hh] = jnp.broadcast_to(m * QK_SCALE2 + jnp.log(l) * LOG2E, (TQ, LANE))
        o_ref[...] = out

    return pl.pallas_call(
        body, name="attn_fwd", grid=(NH // 2, s // TQ),
        in_specs=[pl.BlockSpec((2, TQ, LANE), lambda p, i: (p, i, 0)), pl.BlockSpec((2, s, LANE), lambda p, i: (p, 0, 0)),
                  pl.BlockSpec((2, s, LANE), lambda p, i: (p, 0, 0))],
        out_specs=[pl.BlockSpec((TQ, LANE), lambda p, i: (i, p)), pl.BlockSpec((2, TQ, LANE), lambda p, i: (p, i, 0))],
        out_shape=[jax.ShapeDtypeStruct((s, NH * V_DIM), F32), jax.ShapeDtypeStruct((NH, s, LANE), F32)],
        compiler_params=_params(("parallel", "parallel")),
    )(q, k, vv)


def _lower_bound(lbl):
    a0, a1 = lbl[0:1, :], lbl[1:2, :]
    mx = jnp.maximum(a0, a1)
    e0, e1 = jnp.exp(a0 - mx), jnp.exp(a1 - mx)
    return e0 / (e0 + e1)


def _chunk_cumsum(v, reverse=False):
    pos = lax.broadcasted_iota(jnp.int32, v.shape, 0) % HG_BLOCK
    s = 1
    while s < HG_BLOCK:
        if reverse:
            v = v + jnp.where(pos < HG_BLOCK - s, pltpu.roll(v, TH - s, 0), 0.0)
        else:
            v = v + jnp.where(pos >= s, pltpu.roll(v, s, 0), 0.0)
        s *= 2
    return v


def _hgrn_gates(hq, hf, lb):
    sig = _sigmoid(hf)
    f = lb + (1.0 - lb) * sig
    g = jnp.log(f)
    kk = 1.0 - f
    r = lax.broadcasted_iota(jnp.int32, (TH, TH), 0)
    c = lax.broadcasted_iota(jnp.int32, (TH, TH), 1)
    tri = ((r // HG_BLOCK) == (c // HG_BLOCK)) & (r >= c)
    cum = _chunk_cumsum(g)
    nch = TH // HG_BLOCK
    total = _chunks(cum)[:, HG_BLOCK - 1:HG_BLOCK, :]
    lastb = jnp.broadcast_to(total, (nch, HG_BLOCK, LANE)).reshape(TH, LANE)
    e, ei, ee = jnp.exp(cum), jnp.exp(-cum), jnp.exp(lastb - cum)
    return dict(sig=sig, f=f, kk=kk, tri=tri, cum=cum, total=total, e=e, ei=ei, ee=ee, qd=hq * e, ki=kk * ei, ke=kk * ee)


def _chunks(v):
    return v.reshape(TH // HG_BLOCK, HG_BLOCK, v.shape[-1])


def _bmm_nt(a, b):
    return lax.dot_general(a, b, (((2,), (2,)), ((0,), (0,))), preferred_element_type=F32)


def _bmm_nn(a, b):
    return lax.dot_general(a, b, (((2,), (1,)), ((0,), (0,))), preferred_element_type=F32)


def _bmm_tn(a, b):
    return lax.dot_general(a, b, (((1,), (1,)), ((0,), (0,))), preferred_element_type=F32)


def _pair_masks():
    lane = lax.broadcasted_iota(jnp.int32, (TH, LANE), 1)
    kr = lax.broadcasted_iota(jnp.int32, (LANE, LANE), 0)
    kc = lax.broadcasted_iota(jnp.int32, (LANE, LANE), 1)
    return lane < 64, (kr // 64) == (kc // 64)


def _hgrn_fwd(proj, lbl):
    s = proj.shape[0]
    nch = TH // HG_BLOCK

    def body(hq_ref, hf_ref, hi_ref, lbl_ref, o_ref, st_ref, st):
        @pl.when(pl.program_id(1) == 0)
        def _():
            st[...] = jnp.zeros_like(st)

        m0, bd = _pair_masks()
        for u in range(HG_PAIRS):
            lanes = slice(u * LANE, (u + 1) * LANE)
            lb = _lower_bound(lbl_ref[:, lanes])
            gt = _hgrn_gates(hq_ref[:, lanes], hf_ref[:, lanes], lb)
            v_b = hi_ref[:, lanes].astype(BF16)
            qd, ki_b, ke_b = gt["qd"], gt["ki"].astype(BF16), gt["ke"].astype(BF16)
            qd_b = qd.astype(BF16)
            o = jnp.zeros((TH, LANE), F32)
            for hh in range(2):
                mh = m0 if hh == 0 else jnp.logical_not(m0)
                a = jnp.where(gt["tri"], _mm_nt(jnp.where(mh, qd, 0.0).astype(BF16), ki_b), 0.0)
                o = jnp.where(mh, _mm(a.astype(BF16), v_b), o)
            upd = _bmm_tn(_chunks(v_b), _chunks(ke_b))
            decay = jnp.exp(gt["total"])
            cur, entering = st[u], []
            for n in range(nch):
                entering.append(cur)
                cur = decay[n] * cur + jnp.where(bd, upd[n], 0.0)
            st[u] = cur
            entering = jnp.stack(entering)
            st_ref[u] = entering
            o_ref[:, lanes] = o + _bmm_nt(_chunks(qd_b), entering.astype(BF16)).reshape(TH, LANE)

    wide = HG_PAIRS * LANE
    col = lambda base: pl.BlockSpec((TH, wide), lambda p, i: (i, base // wide + p))
    return pl.pallas_call(
        body, name="hgrn_fwd", grid=(NH // 2 // HG_PAIRS, s // TH),
        in_specs=[col(P_HQ), col(P_HF), col(P_HI), pl.BlockSpec((2, wide), lambda p, i: (0, p))],
        out_specs=[pl.BlockSpec((TH, wide), lambda p, i: (i, p)),
                   pl.BlockSpec((HG_PAIRS, nch, LANE, LANE), lambda p, i: (p, i, 0, 0))],
        out_shape=[jax.ShapeDtypeStruct((s, 512), F32), jax.ShapeDtypeStruct((NH // 2, s // HG_BLOCK, LANE, LANE), F32)],
        scratch_shapes=[pltpu.VMEM((HG_PAIRS, LANE, LANE), F32)],
        compiler_params=_params(("parallel", "arbitrary")),
    )(proj, proj, proj, lbl)


def _group_sum(v):
    r = lax.broadcasted_iota(jnp.int32, (512, 512), 0)
    c = lax.broadcasted_iota(jnp.int32, (512, 512), 1)
    return jnp.dot(v, ((r // 64) == (c // 64)).astype(F32), precision=HIGHEST, preferred_element_type=F32)


def _dsilu(z, sg):
    return sg * (1.0 + z * (1.0 - sg))


def _mid(proj, attn, o_raw, x, tgt, g_hg, b_gate, g_post, wa, wb, w_out):
    s = x.shape[0]

    def body(attn_ref, ga_ref, o_ref, gb_ref, mg_ref, x_ref, t_ref, ghg_ref, bg_ref, gp_ref, wa_ref, wb_ref, wo_ref,
             loss_ref, dout_ref, dattn_ref, dga_ref, dor_ref, dgb_ref, dmg_ref, dwo_ref, dwa_ref, dwb_ref, dgp_ref, dbg_ref, dghg_ref):
        first = pl.program_id(0) == 0

        @pl.when(first)
        def _():
            for rf in (loss_ref, dwo_ref, dwa_ref, dwb_ref, dgp_ref, dbg_ref, dghg_ref):
                rf[...] = jnp.zeros_like(rf)

        attn, za, orw, zb = attn_ref[...], ga_ref[...], o_ref[...], gb_ref[...]
        ghg, gp = ghg_ref[...], gp_ref[...]
        sga, sgb = _sigmoid(za), _sigmoid(zb)
        sa, sb = za * sga, zb * sgb
        ga = attn * sa
        rh = lax.rsqrt(_group_sum(orw * orw) * (1.0 / V_DIM) + EPS)
        on = (orw * rh) * ghg
        gb = on * sb
        ga_b, gb_b = ga.astype(BF16), gb.astype(BF16)
        ya = _mm(ga_b, wa_ref[...])
        yb = _mm(gb_b, wb_ref[...])
        gates = _sigmoid(mg_ref[...] + bg_ref[...])
        g0, g1 = gates[:, :D], gates[:, D:]
        m_b = (g0 * ya + g1 * yb).astype(BF16)
        y = _mm(m_b, wo_ref[...])
        ry = lax.rsqrt(jnp.mean(y * y, axis=-1, keepdims=True) + EPS)
        out = x_ref[...] + (y * ry) * gp
        err = out - t_ref[...]
        loss_ref[...] += 0.5 * jnp.sum(jnp.mean(err * err, axis=-1, keepdims=True), axis=0, keepdims=True)
        dout = err * (1.0 / D)
        dout_ref[...] = dout
        dgp_ref[...] += jnp.sum(dout * (y * ry), axis=0, keepdims=True)
        dgy = dout * gp
        dy = ry * dgy - y * (ry * ry * ry) * jnp.mean(y * dgy, axis=-1, keepdims=True)
        dy_b = dy.astype(BF16)
        dwo_ref[...] += _mm_tn(m_b, dy_b)
        dm = _mm_nt(dy_b, wo_ref[...])
        dya, dyb = dm * g0, dm * g1
        dg0, dg1 = dm * ya, dm * yb
        dmg = jnp.concatenate([dg0 * g0 * (1.0 - g0), dg1 * g1 * (1.0 - g1)], axis=1)
        dmg_ref[...] = dmg.astype(BF16)
        dbg_ref[...] += jnp.sum(dmg, axis=0, keepdims=True)
        dya_b, dyb_b = dya.astype(BF16), dyb.astype(BF16)
        dwa_ref[...] += _mm_tn(ga_b, dya_b)
        dwb_ref[...] += _mm_tn(gb_b, dyb_b)
        dga = _mm_nt(dya_b, wa_ref[...])
        dgb = _mm_nt(dyb_b, wb_ref[...])
        dattn_ref[...] = dga * sa
        dga_ref[...] = (dga * attn * _dsilu(za, sga)).astype(BF16)
        dgb_ref[...] = (dgb * on * _dsilu(zb, sgb)).astype(BF16)
        don = dgb * sb
        dghg_ref[...] += jnp.sum(don * (orw * rh), axis=0, keepdims=True)
        dgo = don * ghg
        dor_ref[...] = rh * dgo - orw * (rh * rh * rh) * (_group_sum(orw * dgo) * (1.0 / V_DIM))

    row = lambda w, j=0: pl.BlockSpec((TM, w), lambda i: (i, j))
    full = lambda a: pl.BlockSpec(a.shape, lambda i: (0,) * a.ndim)
    acc = lambda shape: pl.BlockSpec(shape, lambda i: (0, 0))
    sds = jax.ShapeDtypeStruct
    return pl.pallas_call(
        body, name="mid", grid=(s // TM,),
        in_specs=[row(512), row(512, P_GA // 512), row(512), row(512, P_GB // 512), row(2048, P_MERGE // 2048), row(D), row(D),
                  full(g_hg), full(b_gate), full(g_post), full(wa), full(wb), full(w_out)],
        out_specs=[acc((1, 1)), row(D), row(512), row(512), row(512), row(512), row(2048),
                   acc((D, D)), acc((512, D)), acc((512, D)), acc((1, D)), acc((1, 2048)), acc((1, 512))],
        out_shape=[sds((1, 1), F32), sds((s, D), F32), sds((s, 512), F32), sds((s, 512), BF16), sds((s, 512), F32), sds((s, 512), BF16),
                   sds((s, 2048), BF16), sds((D, D), F32), sds((512, D), F32), sds((512, D), F32), sds((1, D), F32),
                   sds((1, 2048), F32), sds((1, 512), F32)],
        compiler_params=_params(("arbitrary",)),
    )(attn, proj, o_raw, proj, proj, x, tgt, g_hg, b_gate, g_post, wa, wb, w_out)


def _attn_bwd(q, k, vv, attn, dattn, lse):
    s = q.shape[1]
    nt = s // TQ
    scale = 1.0 / math.sqrt(QK_NOPE + QK_ROPE)

    def body(q_ref, k_ref, v_ref, o_ref, do_ref, lse_ref, dq_ref, dk_ref, dv_ref, do_s, delta_s):
        j = pl.program_id(1)

        @pl.when(j == 0)
        def _():
            dq_ref[...] = jnp.zeros_like(dq_ref)
            lane = lax.broadcasted_iota(jnp.int32, (TQ, LANE), 1)

            @pl.loop(0, nt)
            def _(i):
                rows = pl.ds(pl.multiple_of(i * TQ, TQ), TQ)
                do, o = do_ref[rows, :], o_ref[rows, :]
                for hh in range(2):
                    doh = jnp.where((lane < 64) if hh == 0 else (lane >= 64), do, 0.0)
                    do_s[hh, rows, :] = doh.astype(BF16)
                    delta_s[hh, rows, :] = jnp.broadcast_to(jnp.sum(doh * o, axis=-1, keepdims=True), (TQ, LANE))

        kjs, vjs = (k_ref[0], k_ref[1]), (v_ref[0], v_ref[1])
        wide = lambda a: jnp.concatenate([a] * (TQ // LANE), axis=1)

        def tile(hh, i, carry, diag):
            dk, dv = carry
            rows = pl.ds(pl.multiple_of(i * TQ, TQ), TQ)
            qi, do_b = q_ref[hh, rows, :], do_s[hh, rows, :]
            p = jnp.exp2(_mm_nt(qi, kjs[hh]) * QK_SCALE2 - wide(lse_ref[hh, rows, :]))
            if diag:
                p = jnp.where(_diag_visible(), p, 0.0)
            dv = dv + _mm_tn(p.astype(BF16), do_b)
            ds_b = (p * (_mm_nt(do_b, vjs[hh]) - wide(delta_s[hh, rows, :]))).astype(BF16)
            dk = dk + _mm_tn(ds_b, qi)
            dq_ref[hh, rows, :] += _mm(ds_b, kjs[hh])
            return dk, dv

        def step(i, carry):
            return tile(0, i, carry[0], False), tile(1, i, carry[1], False)

        z = jnp.zeros((TQ, LANE), F32)
        first = (tile(0, j, (z, z), True), tile(1, j, (z, z), True))
        carry = lax.fori_loop(j + 1, nt, step, first)
        for hh in range(2):
            dk_ref[hh] = carry[hh][0] * scale
            dv_ref[hh] = carry[hh][1]

        @pl.when(j == nt - 1)
        def _():
            dq_ref[...] = dq_ref[...] * scale

    whole = pl.BlockSpec((2, s, LANE), lambda p, j: (p, 0, 0))
    tile_spec = pl.BlockSpec((2, TQ, LANE), lambda p, j: (p, j, 0))
    cols = pl.BlockSpec((s, LANE), lambda p, j: (0, p))
    hs = jax.ShapeDtypeStruct((NH, s, LANE), F32)
    return pl.pallas_call(
        body, name="attn_bwd", grid=(NH // 2, nt),
        in_specs=[whole, tile_spec, tile_spec, cols, cols, whole],
        out_specs=[whole, tile_spec, tile_spec],
        out_shape=[hs, hs, hs],
        scratch_shapes=[pltpu.VMEM((2, s, LANE), BF16), pltpu.VMEM((2, s, LANE), F32)],
        compiler_params=_params(("parallel", "arbitrary")),
    )(q, k, vv, attn, dattn, lse)


def _hgrn_bwd(proj, lbl, states, do_raw):
    s = proj.shape[0]
    nt = s // TH
    nch = TH // HG_BLOCK

    def body(hq_ref, hf_ref, hi_ref, lbl_ref, st_ref, do_ref, dh_ref, dlbl_ref, dst, dlb):
        step = pl.program_id(1)

        @pl.when(step == 0)
        def _():
            dst[...] = jnp.zeros_like(dst)
            dlb[...] = jnp.zeros_like(dlb)

        m0, bd = _pair_masks()
        for u in range(HG_PAIRS):
            lanes = slice(u * LANE, (u + 1) * LANE)
            lb = _lower_bound(lbl_ref[:, lanes])
            gt = _hgrn_gates(hq_ref[:, lanes], hf_ref[:, lanes], lb)
            do = do_ref[:, lanes]
            qd, ki, ke = gt["qd"], gt["ki"], gt["ke"]
            v_b, do_b = hi_ref[:, lanes].astype(BF16), do.astype(BF16)
            qd_b, ki_b, ke_b = qd.astype(BF16), ki.astype(BF16), ke.astype(BF16)
            dv = jnp.zeros((TH, LANE), F32)
            dqd = jnp.zeros((TH, LANE), F32)
            dki = jnp.zeros((TH, LANE), F32)
            for hh in range(2):
                mh = m0 if hh == 0 else jnp.logical_not(m0)
                a_b = jnp.where(gt["tri"], _mm_nt(jnp.where(mh, qd, 0.0).astype(BF16), ki_b), 0.0).astype(BF16)
                doh_b = jnp.where(mh, do, 0.0).astype(BF16)
                da_b = jnp.where(gt["tri"], _mm_nt(doh_b, v_b), 0.0).astype(BF16)
                dv = dv + _mm_tn(a_b, doh_b)
                dqd = jnp.where(mh, _mm(da_b, ki_b), dqd)
                dki = jnp.where(mh, _mm_tn(da_b, qd_b), dki)
            fed = _bmm_tn(_chunks(do_b), _chunks(qd_b))
            decay = jnp.exp(gt["total"])
            ds, leaving = dst[u], [None] * nch
            for n in reversed(range(nch)):
                leaving[n] = ds
                ds = decay[n] * ds + jnp.where(bd, fed[n], 0.0)
            dst[u] = ds
            leaving = jnp.stack(leaving)
            entering = st_ref[u]
            leaving_b = leaving.astype(BF16)
            dke3 = _bmm_nn(_chunks(v_b), leaving_b)
            dv = dv + _bmm_nt(_chunks(ke_b), leaving_b).reshape(TH, LANE)
            dqd = dqd + _bmm_nn(_chunks(do_b), entering.astype(BF16)).reshape(TH, LANE)
            dke = dke3.reshape(TH, LANE)
            dlast = (jnp.sum(dke3 * _chunks(ke), axis=1, keepdims=True)
                     + jnp.sum(leaving * entering, axis=1, keepdims=True) * decay)
            dk = dki * gt["ei"] + dke * gt["ee"]
            dcum = dqd * qd - dki * ki - dke * ke
            dg = _chunk_cumsum(dcum, reverse=True) + jnp.broadcast_to(dlast, (nch, HG_BLOCK, LANE)).reshape(TH, LANE)
            sig = gt["sig"]
            df = dg / gt["f"] - dk
            dlb[:, lanes] += jnp.sum(df * (1.0 - sig), axis=0, keepdims=True)
            dh_ref[0, :, lanes] = (dqd * gt["e"]).astype(BF16)
            dh_ref[1, :, lanes] = ((df * (1.0 - lb)) * sig * (1.0 - sig)).astype(BF16)
            dh_ref[2, :, lanes] = dv.astype(BF16)

        @pl.when(step == nt - 1)
        def _():
            lb = _lower_bound(lbl_ref[...])
            da0 = dlb[...] * lb * (1.0 - lb)
            dlbl_ref[...] = jnp.concatenate([da0, -da0], axis=0)

    wide = HG_PAIRS * LANE
    col = lambda base: pl.BlockSpec((TH, wide), lambda p, i: (nt - 1 - i, base // wide + p))
    tile = pl.BlockSpec((TH, wide), lambda p, i: (nt - 1 - i, p))
    sds = jax.ShapeDtypeStruct
    return pl.pallas_call(
        body, name="hgrn_bwd", grid=(NH // 2 // HG_PAIRS, nt),
        in_specs=[col(P_HQ), col(P_HF), col(P_HI), pl.BlockSpec((2, wide), lambda p, i: (0, p)),
                  pl.BlockSpec((HG_PAIRS, nch, LANE, LANE), lambda p, i: (p, nt - 1 - i, 0, 0)), tile],
        out_specs=[pl.BlockSpec((3, TH, wide), lambda p, i: (0, nt - 1 - i, p)), pl.BlockSpec((2, wide), lambda p, i: (0, p))],
        out_shape=[sds((3, s, 512), BF16), sds((2, 512), F32)],
        scratch_shapes=[pltpu.VMEM((HG_PAIRS, LANE, LANE), F32), pltpu.VMEM((1, wide), F32)],
        compiler_params=_params(("parallel", "arbitrary")),
    )(proj, proj, proj, lbl, states, do_raw)


def _norm_rows_bwd(v, r, g, dn):
    dgv = dn * g
    return r * dgv - v * (r * r * r) * jnp.mean(v * dgv, axis=-1, keepdims=True)


def _qkv_bwd(proj, dq, dk, dvv, g_q, g_kv, w_uq_p, w_k_p, w_v_p, rc, rs1, rs2):
    s = proj.shape[0]

    def body(cq_ref, ckv_ref, dq_ref, dk_ref, dv_ref, gq_ref, gkv_ref, wq_ref, wk_ref, wv_ref, c_ref, s1_ref, s2_ref,
             dcq_ref, dckv_ref, dkpe_ref, dwq_ref, dwk_ref, dwv_ref, dgq_ref, dgkv_ref):
        @pl.when(pl.program_id(0) == 0)
        def _():
            for rf in (dwq_ref, dwk_ref, dwv_ref, dgq_ref, dgkv_ref):
                rf[...] = jnp.zeros_like(rf)

        c, s1, s2 = c_ref[...], s1_ref[...], s2_ref[...]
        cq, ckv = cq_ref[...], ckv_ref[...]
        gq, gkv = gq_ref[...], gkv_ref[...]
        cqn, rq = _norm_rows(cq, gq)
        ckvn, rkv = _norm_rows(ckv, gkv)
        cqn_b, ckvn_b = cqn.astype(BF16), ckvn.astype(BF16)
        dqf = jnp.concatenate([_rope_t(dq_ref[h], c, s1, s2) for h in range(NH)], axis=1).astype(BF16)
        dkf = jnp.concatenate([dk_ref[h] for h in range(NH)], axis=1).astype(BF16)
        dvf = jnp.concatenate([dv_ref[h] for h in range(NH)], axis=1).astype(BF16)
        dkpe = dk_ref[0]
        for h in range(1, NH):
            dkpe = dkpe + dk_ref[h]
        lane = lax.broadcasted_iota(jnp.int32, (TM, LANE), 1)
        dkpe = jnp.where((lane >= QK_NOPE) & (lane < QK_NOPE + QK_ROPE), dkpe, 0.0)
        dkpe_ref[...] = _rope_t(dkpe, c, s1, s2).astype(BF16)
        dwq_ref[...] += _mm_tn(cqn_b, dqf)
        dwk_ref[...] += _mm_tn(ckvn_b, dkf)
        dwv_ref[...] += _mm_tn(ckvn_b, dvf)
        dcqn = _mm_nt(dqf, wq_ref[...])
        dckvn = _mm_nt(dkf, wk_ref[...]) + _mm_nt(dvf, wv_ref[...])
        dgq_ref[...] += jnp.sum(dcqn * (cq * rq), axis=0, keepdims=True)
        dgkv_ref[...] += jnp.sum(dckvn * (ckv * rkv), axis=0, keepdims=True)
        dcq_ref[...] = _norm_rows_bwd(cq, rq, gq, dcqn).astype(BF16)
        dckv_ref[...] = _norm_rows_bwd(ckv, rkv, gkv, dckvn).astype(BF16)

    row = lambda w, j=0: pl.BlockSpec((TM, w), lambda i: (i, j))
    full = lambda a: pl.BlockSpec(a.shape, lambda i: (0,) * a.ndim)
    acc = lambda shape: pl.BlockSpec(shape, lambda i: (0, 0))
    heads = pl.BlockSpec((NH, TM, LANE), lambda i: (0, i, 0))
    sds = jax.ShapeDtypeStruct
    return pl.pallas_call(
        body, name="qkv_bwd", grid=(s // TM,),
        in_specs=[row(Q_LORA, P_CQ // Q_LORA), row(KV_LORA, P_CKV // KV_LORA), heads, heads, heads,
                  full(g_q), full(g_kv), full(w_uq_p), full(w_k_p), full(w_v_p), row(LANE), row(LANE), row(LANE)],
        out_specs=[row(Q_LORA), row(KV_LORA), row(LANE), acc((Q_LORA, D)), acc((KV_LORA, D)), acc((KV_LORA, D)),
                   acc((1, Q_LORA)), acc((1, KV_LORA))],
        out_shape=[sds((s, Q_LORA), BF16), sds((s, KV_LORA), BF16), sds((s, LANE), BF16), sds((Q_LORA, D), F32),
                   sds((KV_LORA, D), F32), sds((KV_LORA, D), F32), sds((1, Q_LORA), F32), sds((1, KV_LORA), F32)],
        compiler_params=_params(("arbitrary",)),
    )(proj, proj, dq, dk, dvv, g_q, g_kv, w_uq_p, w_k_p, w_v_p, rc, rs1, rs2)


def _front_bwd(x, dout, dmg, dga, dh3, dgb, dcq, dckv, dkpe, g_pre, w_in_p):
    s = x.shape[0]

    def body(x_ref, do_ref, dmg_ref, dga_ref, dh3_ref, dgb_ref, dcq_ref, dckv_ref, dkpe_ref, g_ref, w_ref, gx_ref, dg_ref):
        @pl.when(pl.program_id(0) == 0)
        def _():
            dg_ref[...] = jnp.zeros_like(dg_ref)

        xv, g = x_ref[...], g_ref[...]
        _, r = _norm_rows(xv, g)
        pieces = ((dmg_ref[...], P_MERGE), (dga_ref[...], P_GA), (dh3_ref[0], P_HQ), (dh3_ref[1], P_HF), (dh3_ref[2], P_HI),
                  (dgb_ref[...], P_GB), (dcq_ref[...], P_CQ), (dckv_ref[...], P_CKV), (dkpe_ref[...], P_KPE))
        dh = jnp.zeros((TM, D), F32)
        for piece, off in pieces:
            dh = dh + _mm_nt(piece, w_ref[:, off:off + piece.shape[1]])
        dg_ref[...] += jnp.sum(dh * (xv * r), axis=0, keepdims=True)
        gx_ref[...] = do_ref[...] + _norm_rows_bwd(xv, r, g, dh)

    row = lambda w: pl.BlockSpec((TM, w), lambda i: (i, 0))
    full = lambda a: pl.BlockSpec(a.shape, lambda i: (0,) * a.ndim)
    sds = jax.ShapeDtypeStruct
    return pl.pallas_call(
        body, name="front_bwd", grid=(s // TM,),
        in_specs=[row(D), row(D), row(2048), row(512), pl.BlockSpec((3, TM, 512), lambda i: (0, i, 0)), row(512), row(Q_LORA),
                  row(KV_LORA), row(LANE), full(g_pre), full(w_in_p)],
        out_specs=[row(D), pl.BlockSpec((1, D), lambda i: (0, 0))],
        out_shape=[sds((s, D), F32), sds((1, D), F32)],
        compiler_params=_params(("arbitrary",)),
    )(x, dout, dmg, dga, dh3, dgb, dcq, dckv, dkpe, g_pre, w_in_p)


TK_GRAD = 512


def _win_grad(h, pieces, name):
    s = h.shape[0]
    n = len(pieces)

    def body(h_ref, *refs):
        @pl.when(pl.program_id(0) == 0)
        def _():
            for o_ref in refs[n:]:
                o_ref[...] = jnp.zeros_like(o_ref)

        hv = h_ref[...]
        for d_ref, o_ref in zip(refs[:n], refs[n:]):
            if len(d_ref.shape) == 3:
                for k in range(d_ref.shape[0]):
                    o_ref[k] += _mm_tn(hv, d_ref[k])
            else:
                o_ref[...] += _mm_tn(hv, d_ref[...])

    def in_spec(p):
        if p.ndim == 3:
            return pl.BlockSpec((p.shape[0], TK_GRAD, p.shape[2]), lambda kk: (0, kk, 0))
        return pl.BlockSpec((TK_GRAD, p.shape[1]), lambda kk: (kk, 0))

    out_shapes = [(p.shape[0], D, p.shape[2]) if p.ndim == 3 else (D, p.shape[1]) for p in pieces]
    return pl.pallas_call(
        body, name=name, grid=(s // TK_GRAD,),
        in_specs=[pl.BlockSpec((TK_GRAD, D), lambda kk: (kk, 0))] + [in_spec(p) for p in pieces],
        out_specs=[pl.BlockSpec(sh, lambda kk, nd=len(sh): (0,) * nd) for sh in out_shapes],
        out_shape=[jax.ShapeDtypeStruct(sh, F32) for sh in out_shapes],
        compiler_params=_params(("arbitrary",)),
    )(h, *pieces)


def _pad_win(w_in):
    z = lambda n: jnp.zeros((w_in.shape[0], n), w_in.dtype)
    sl = lambda o, n: w_in[:, o:o + n]
    return jnp.concatenate([sl(O_MERGE, 2048), sl(O_GA, 512), sl(O_HQ, 512), sl(O_HF, 512), sl(O_HI, 512), sl(O_GB, 512),
                            sl(O_CQ, Q_LORA), sl(O_CKV, KV_LORA), z(64), sl(O_KPE, QK_ROPE), z(32)], axis=1)


def _unpad_win(g):
    sl = lambda o, n: g[:, o:o + n]
    return jnp.concatenate([sl(P_CQ, Q_LORA), sl(P_CKV, KV_LORA), sl(P_KPE + 64, QK_ROPE), sl(P_GA, 512), sl(P_HQ, 512),
                            sl(P_HF, 512), sl(P_HI, 512), sl(P_GB, 512), sl(P_MERGE, 2048)], axis=1)


def _pad_wuq(w_uq):
    w = w_uq.reshape(Q_LORA, NH, QK_NOPE + QK_ROPE)
    return jnp.pad(w, ((0, 0), (0, 0), (0, LANE - QK_NOPE - QK_ROPE))).reshape(Q_LORA, NH * LANE)


def _unpad_wuq(g):
    return g.reshape(Q_LORA, NH, LANE)[:, :, :QK_NOPE + QK_ROPE].reshape(Q_LORA, NH * (QK_NOPE + QK_ROPE))


def _pad_wukv(w_ukv):
    w = w_ukv.reshape(KV_LORA, NH, QK_NOPE + V_DIM)
    w_k = jnp.pad(w[:, :, :QK_NOPE], ((0, 0), (0, 0), (0, LANE - QK_NOPE))).reshape(KV_LORA, NH * LANE)
    wv = w[:, :, QK_NOPE:].reshape(KV_LORA, NH // 2, 2, 1, V_DIM)
    eye = jnp.eye(2, dtype=w.dtype).reshape(1, 1, 2, 2, 1)
    return w_k, (wv * eye).reshape(KV_LORA, NH * LANE)


def _unpad_wukv(gk, gv):
    gk = gk.reshape(KV_LORA, NH, LANE)[:, :, :QK_NOPE]
    gv = gv.reshape(KV_LORA, NH // 2, 2, 2, V_DIM)
    gv = jnp.stack([gv[:, :, 0, 0], gv[:, :, 1, 1]], axis=2).reshape(KV_LORA, NH, V_DIM)
    return jnp.concatenate([gk, gv], axis=-1).reshape(KV_LORA, NH * (QK_NOPE + V_DIM))


def _local_step(x, tgt, g_pre, w_in, b_gate, g_q, w_uq, g_kv, w_ukv, lb_logits, g_hgrn, wa, wb, w_out, g_post):
    s = x.shape[0]
    w_in_p = _pad_win(w_in)
    w_uq_p = _pad_wuq(w_uq)
    w_k_p, w_v_p = _pad_wukv(w_ukv)
    rc, rs1, rs2 = _rope_tables(s)
    g_hg = jnp.tile(g_hgrn, (1, NH))

    proj, h = _front_fwd(x, g_pre, w_in_p)
    q, k, vv = _qkv_fwd(proj, g_q, g_kv, w_uq_p, w_k_p, w_v_p, rc, rs1, rs2)
    attn, lse = _attn_fwd(q, k, vv)
    o_raw, states = _hgrn_fwd(proj, lb_logits)
    (loss, dout, dattn, dga, dor, dgb, dmg, d_wout, d_wa, d_wb, d_gpost, d_bgate, d_ghg) = _mid(
        proj, attn, o_raw, x, tgt, g_hg, b_gate, g_post, wa, wb, w_out)
    w_mg, w_ga, w_gb = _win_grad(h, [dmg, dga, dgb], "win_grad_mid")
    dq, dk, dvv = _attn_bwd(q, k, vv, attn, dattn, lse)
    dh3, d_lbl = _hgrn_bwd(proj, lb_logits, states, dor)
    (w_h3,) = _win_grad(h, [dh3], "win_grad_hgrn")
    dcq, dckv, dkpe, d_wuq_p, d_wk_p, d_wv_p, d_gq, d_gkv = _qkv_bwd(proj, dq, dk, dvv, g_q, g_kv, w_uq_p, w_k_p, w_v_p, rc, rs1, rs2)
    w_cq, w_ckv, w_kpe = _win_grad(h, [dcq, dckv, dkpe], "win_grad_qkv")
    grad_x, d_gpre = _front_bwd(x, dout, dmg, dga, dh3, dgb, dcq, dckv, dkpe, g_pre, w_in_p)
    d_win = jnp.concatenate([w_cq, w_ckv, w_kpe[:, 64:64 + QK_ROPE], w_ga, w_h3[0], w_h3[1], w_h3[2], w_gb, w_mg], axis=1)
    grads = dict(w_in=d_win, w_uq=_unpad_wuq(d_wuq_p), w_ukv=_unpad_wukv(d_wk_p, d_wv_p), w_branch_a=d_wa, w_branch_b=d_wb, w_out=d_wout)
    vec_grads = dict(g_pre=d_gpre, b_gate=d_bgate, g_q=d_gq, g_kv=d_gkv, lb_logits=d_lbl, g_hgrn=d_ghg, g_post=d_gpost)
    return loss, grad_x, grads, vec_grads


SHARD_SHAPES = (("w_in", (1024, 1416)), ("w_uq", (192, 768)), ("w_ukv", (256, 256)), ("w_branch_a", (512, 256)),
                ("w_branch_b", (512, 256)), ("w_out", (256, 1024)))
BIG = tuple(n for n, _ in SHARD_SHAPES)
ROW_SHARDED = ("w_uq", "w_out")
N_CHIPS = 4
VEC_ROWS = (("g_pre", 0, 1024), ("b_gate", 1, 2048), ("g_q", 2, 768), ("g_kv", 3, 256), ("g_hgrn", 6, 64), ("g_post", 7, 1024))
VEC_LB_ROW = 4
VEC_SHAPE = (8, 2048)


def _split_by_chip(name, g):
    a, b = dict(SHARD_SHAPES)[name]
    return g.reshape(N_CHIPS, a, b) if name in ROW_SHARDED else g.reshape(a, N_CHIPS, b).transpose(1, 0, 2)


def _join_chips(name, w):
    a, b = dict(SHARD_SHAPES)[name]
    return w.reshape(N_CHIPS * a, b) if name in ROW_SHARDED else w.transpose(1, 0, 2).reshape(a, N_CHIPS * b)


MESH = pl.DeviceIdType.MESH
HBM = pl.BlockSpec(memory_space=pltpu.HBM)


def _mesh_place():
    x, y, c = lax.axis_index("x"), lax.axis_index("y"), lax.axis_index("c")
    return x, y, c, 2 * x + y, [(1 - x, y), (x, 1 - y), (1 - x, 1 - y)]


def _remote(src, dst, send_sems, recv_sems, k, to):
    return pltpu.make_async_remote_copy(src_ref=src, dst_ref=dst, send_sem=send_sems.at[k], recv_sem=recv_sems.at[k],
                                        device_id=to, device_id_type=MESH)


def _gather_weights(shards):
    n = len(shards)

    def body(*refs):
        srcs, outs = refs[:n], refs[n:2 * n]
        ici_send, ici_recv, d2d_send, d2d_recv, local_sems = refs[2 * n:]
        x, y, c, me, chips = _mesh_place()
        sibling = (x, y, 1 - c)

        def half(ref, k, which):
            rows = shards[k].shape[0] // 2
            return ref.at[pl.ds(pl.multiple_of(which * rows, rows), rows)]

        own = [pltpu.make_async_copy(srcs[k], outs[k].at[me], local_sems.at[k]) for k in range(n)]
        for cp in own:
            cp.start()
        started = []
        for k in range(n):
            for j, (px, py) in enumerate(chips):
                cp = _remote(half(srcs[k], k, c), half(outs[k].at[me], k, c), ici_send, ici_recv, 3 * k + j, (px, py, c))
                cp.start()
                started.append(cp)
        for k in range(n):
            for j, (px, py) in enumerate(chips):
                landed = half(outs[k].at[2 * px + py], k, c)
                _remote(landed, landed, ici_send, ici_recv, 3 * k + j, (px, py, c)).wait_recv()
                cp = _remote(landed, landed, d2d_send, d2d_recv, 3 * k + j, sibling)
                cp.start()
                started.append(cp)
        for k in range(n):
            for j, (px, py) in enumerate(chips):
                other = half(outs[k].at[2 * px + py], k, 1 - c)
                _remote(other, other, d2d_send, d2d_recv, 3 * k + j, sibling).wait_recv()
        for cp in started:
            cp.wait_send()
        for cp in own:
            cp.wait()

    sems = pltpu.SemaphoreType.DMA((3 * n,))
    return pl.pallas_call(
        body, name="gather_weights", in_specs=[HBM] * n, out_specs=[HBM] * n,
        out_shape=[jax.ShapeDtypeStruct((N_CHIPS,) + s.shape, s.dtype) for s in shards],
        scratch_shapes=[sems, sems, sems, sems, pltpu.SemaphoreType.DMA((n,))],
        compiler_params=pltpu.CompilerParams(has_side_effects=True),
    )(*shards)


def _chip_exchange(srcs, name):
    n = len(srcs)

    def body(*refs):
        src_refs, outs = refs[:n], refs[n:2 * n]
        send_sems, recv_sems, local_sems = refs[2 * n:]
        x, y, c, me, chips = _mesh_place()

        def slab(k, t):
            return src_refs[k] if srcs[k].ndim == 2 else src_refs[k].at[t]

        own = [pltpu.make_async_copy(slab(k, me), outs[k].at[me], local_sems.at[k]) for k in range(n)]
        for cp in own:
            cp.start()
        sends = []
        for k in range(n):
            for j, (px, py) in enumerate(chips):
                cp = _remote(slab(k, 2 * px + py), outs[k].at[me], send_sems, recv_sems, 3 * k + j, (px, py, c))
                cp.start()
                sends.append(cp)
        for k in range(n):
            for j, (px, py) in enumerate(chips):
                _remote(slab(k, me), outs[k].at[2 * px + py], send_sems, recv_sems, 3 * k + j, (px, py, c)).wait_recv()
        for cp in sends:
            cp.wait_send()
        for cp in own:
            cp.wait()

    sems = pltpu.SemaphoreType.DMA((3 * n,))
    return pl.pallas_call(
        body, name=name, in_specs=[HBM] * n, out_specs=[HBM] * n,
        out_shape=[jax.ShapeDtypeStruct((N_CHIPS,) + s.shape[-2:], s.dtype) for s in srcs],
        scratch_shapes=[sems, sems, pltpu.SemaphoreType.DMA((n,))],
        compiler_params=pltpu.CompilerParams(has_side_effects=True),
    )(*srcs)


def _sibling_exchange(srcs, name):
    n = len(srcs)

    def body(*refs):
        src_refs, outs = refs[:n], refs[n:2 * n]
        send_sems, recv_sems = refs[2 * n:]
        sibling = (lax.axis_index("x"), lax.axis_index("y"), 1 - lax.axis_index("c"))
        copies = [_remote(src_refs[k], outs[k], send_sems, recv_sems, k, sibling) for k in range(n)]
        for cp in copies:
            cp.start()
        for cp in copies:
            cp.wait()

    sems = pltpu.SemaphoreType.DMA((n,))
    return pl.pallas_call(
        body, name=name, in_specs=[HBM] * n, out_specs=[HBM] * n,
        out_shape=[jax.ShapeDtypeStruct(s.shape, s.dtype) for s in srcs],
        scratch_shapes=[sems, sems],
        compiler_params=pltpu.CompilerParams(has_side_effects=True),
    )(*srcs)


ROW_TILE = 256


def _sum_chips(parts, name):
    _, a, b = parts.shape
    ta = min(a, ROW_TILE)

    def body(p_ref, o_ref):
        f = lambda t: p_ref[t].astype(F32)
        o_ref[...] = ((f(0) + f(1)) + f(2)) + f(3)

    return pl.pallas_call(
        body, name=name, grid=(a // ta,),
        in_specs=[pl.BlockSpec((N_CHIPS, ta, b), lambda i: (0, i, 0))],
        out_specs=pl.BlockSpec((ta, b), lambda i: (i, 0)),
        out_shape=jax.ShapeDtypeStruct((a, b), F32),
        compiler_params=_params(("parallel",)),
    )(parts)


def _adamw_math(g, w, m, v):
    nm = ADAM_B1 * m + (1.0 - ADAM_B1) * g
    nv = ADAM_B2 * v + (1.0 - ADAM_B2) * (g * g)
    m_hat = nm / (1.0 - ADAM_B1 ** ADAM_STEP)
    v_hat = nv / (1.0 - ADAM_B2 ** ADAM_STEP)
    return -ADAM_LR * (m_hat / (jnp.sqrt(v_hat) + ADAM_EPS) + ADAM_WD * w), nm, nv


def _adamw(p_mine, p_sibling, w, m, v, name):
    a, b = p_mine.shape
    ta = min(a, ROW_TILE)

    def body(a_ref, b_ref, w_ref, m_ref, v_ref, g_ref, d_ref, nm_ref, nv_ref):
        g = a_ref[...] + b_ref[...]
        g_ref[0] = g
        d_ref[0], nm_ref[0], nv_ref[0] = _adamw_math(g, w_ref[0], m_ref[0], v_ref[0])

    part = pl.BlockSpec((ta, b), lambda i: (i, 0))
    tile = pl.BlockSpec((1, ta, b), lambda i: (0, i, 0))
    sds = jax.ShapeDtypeStruct((1, a, b), F32)
    return pl.pallas_call(
        body, name=name, grid=(a // ta,), in_specs=[part, part, tile, tile, tile], out_specs=[tile] * 4, out_shape=[sds] * 4,
        compiler_params=_params(("parallel",)),
    )(p_mine, p_sibling, w, m, v)


def _vec_pack(vg):
    names = [n for n, _, _ in VEC_ROWS]

    def body(*refs):
        o_ref = refs[-1]
        lb_ref = refs[len(names)]
        o_ref[...] = jnp.zeros_like(o_ref)
        for (name, row, size), ref in zip(VEC_ROWS, refs):
            if name == "g_hgrn":
                r = lax.broadcasted_iota(jnp.int32, (NH * V_DIM, LANE), 0)
                c = lax.broadcasted_iota(jnp.int32, (NH * V_DIM, LANE), 1)
                fold = ((r % V_DIM) == c).astype(F32)
                o_ref[row:row + 1, 0:LANE] = jnp.dot(ref[...], fold, precision=HIGHEST, preferred_element_type=F32)
            else:
                o_ref[row:row + 1, 0:size] = ref[...]
        o_ref[VEC_LB_ROW:VEC_LB_ROW + 2, 0:512] = lb_ref[...]

    return pl.pallas_call(body, name="vec_pack", out_shape=jax.ShapeDtypeStruct(VEC_SHAPE, F32))(
        *[vg[n] for n in names], vg["lb_logits"])


def _adamw_vec(p_mine, p_sibling, w, m, v):
    names = [n for n, _, _ in VEC_ROWS] + ["lb_logits"]
    k = len(names)

    def body(a_ref, b_ref, *refs):
        ins, outs = refs[:3 * k], refs[3 * k:]
        for i, name in enumerate(names):
            if name == "lb_logits":
                rows, cols = slice(VEC_LB_ROW, VEC_LB_ROW + 2), slice(0, 512)
            else:
                _, row, size = VEC_ROWS[i]
                rows, cols = slice(row, row + 1), slice(0, size)
            g = a_ref[rows, cols] + b_ref[rows, cols]
            d, nm, nv = _adamw_math(g, ins[i][...], ins[k + i][...], ins[2 * k + i][...])
            for o_ref, val in zip(outs[4 * i:4 * i + 4], (g, d, nm, nv)):
                o_ref[...] = val

    shapes = [jax.ShapeDtypeStruct(w[n].shape, F32) for n in names for _ in range(4)]
    res = pl.pallas_call(body, name="adamw_vec", out_shape=shapes)(
        p_mine, p_sibling, *[w[n] for n in names], *[m[n] for n in names], *[v[n] for n in names])
    return [{n: res[4 * i + j] for i, n in enumerate(names)} for j in range(4)]


WEIGHTS = ("g_pre", "w_in", "b_gate", "g_q", "w_uq", "g_kv", "w_ukv", "lb_logits", "g_hgrn", "w_branch_a", "w_branch_b", "w_out", "g_post")


def kernel(x, g_pre, w_in, b_gate, g_q, w_uq, g_kv, w_ukv, lb_logits, g_hgrn, w_branch_a, w_branch_b, w_out, g_post, loss_target, m_g_pre, m_w_in, m_b_gate, m_g_q, m_w_uq, m_g_kv, m_w_ukv, m_lb_logits, m_g_hgrn, m_w_branch_a, m_w_branch_b, m_w_out, m_g_post, v_g_pre, v_w_in, v_b_gate, v_g_q, v_w_uq, v_g_kv, v_w_ukv, v_lb_logits, v_g_hgrn, v_w_branch_a, v_w_branch_b, v_w_out, v_g_post):
    w = dict(g_pre=g_pre, w_in=w_in, b_gate=b_gate, g_q=g_q, w_uq=w_uq, g_kv=g_kv, w_ukv=w_ukv, lb_logits=lb_logits, g_hgrn=g_hgrn,
             w_branch_a=w_branch_a, w_branch_b=w_branch_b, w_out=w_out, g_post=g_post)
    m = dict(g_pre=m_g_pre, w_in=m_w_in, b_gate=m_b_gate, g_q=m_g_q, w_uq=m_w_uq, g_kv=m_g_kv, w_ukv=m_w_ukv, lb_logits=m_lb_logits,
             g_hgrn=m_g_hgrn, w_branch_a=m_w_branch_a, w_branch_b=m_w_branch_b, w_out=m_w_out, g_post=m_g_post)
    v = dict(g_pre=v_g_pre, w_in=v_w_in, b_gate=v_b_gate, g_q=v_g_q, w_uq=v_w_uq, g_kv=v_g_kv, w_ukv=v_w_ukv, lb_logits=v_lb_logits,
             g_hgrn=v_g_hgrn, w_branch_a=v_w_branch_a, w_branch_b=v_w_branch_b, w_out=v_w_out, g_post=v_g_post)
    gathered = _gather_weights([w[n][0].astype(BF16) for n in BIG])
    big = {n: _join_chips(n, g) for n, g in zip(BIG, gathered)}
    loss, grad_x, grads, vec_grads = _local_step(
        x[0], loss_target[0], g_pre, big["w_in"], b_gate, g_q, big["w_uq"], g_kv, big["w_ukv"], lb_logits, g_hgrn,
        big["w_branch_a"], big["w_branch_b"], big["w_out"], g_post)

    slabs = [_split_by_chip(n, grads[n]).astype(BF16) for n in BIG]
    landed = _chip_exchange(slabs + [_vec_pack(vec_grads)], "scatter_grads")
    mine = [_sum_chips(p, "sum_chips_" + n) for p, n in zip(landed, BIG + ("vec",))]
    theirs = _sibling_exchange(mine, "sibling_grads")
    outs = [{}, {}, {}, {}]
    for k, n in enumerate(BIG):
        for o, val in zip(outs, _adamw(mine[k], theirs[k], w[n], m[n], v[n], "adamw_" + n)):
            o[n] = val
    for o, vals in zip(outs, _adamw_vec(mine[-1], theirs[-1], w, m, v)):
        o.update(vals)

    total = lax.psum(loss[0, 0], ("x", "y", "c"))
    return (total, grad_x[None], *[o[n] for o in outs for n in WEIGHTS])
```

```python
import functools
import math

import numpy as np
import jax
import jax.numpy as jnp
from jax import lax
from jax.experimental import pallas as pl
from jax.experimental.pallas import tpu as pltpu

F32 = jnp.float32
BF16 = jnp.bfloat16
HIGHEST = lax.Precision.HIGHEST

D = 1024
NH = 8
QK_NOPE, QK_ROPE, V_DIM = 64, 32, 64
Q_LORA, KV_LORA = 768, 256
CHUNK = 64
HG_BLOCK = 32
EPS = 1e-6
D_IN = 5664
LANE = 128
P_MERGE, P_GA, P_HQ, P_HF, P_HI, P_GB, P_CQ, P_CKV, P_KPE = 0, 2048, 2560, 3072, 3584, 4096, 4608, 5376, 5632
D_P = 5760
O_CQ, O_CKV, O_KPE, O_GA, O_HQ, O_HF, O_HI, O_GB, O_MERGE = 0, 768, 1024, 1056, 1568, 2080, 2592, 3104, 3616

TM = 256
TQ = 512
TH = 256
HG_PAIRS = 2
VMEM_LIMIT = 56 * 1024 * 1024

ADAM_LR, ADAM_B1, ADAM_B2, ADAM_EPS, ADAM_WD, ADAM_STEP = 0.001, 0.9, 0.999, 1e-08, 0.01, 10

NT_DIMS = (((1,), (1,)), ((), ()))
TN_DIMS = (((0,), (0,)), ((), ()))


def _params(sem):
    return pltpu.CompilerParams(dimension_semantics=sem, vmem_limit_bytes=VMEM_LIMIT)


def _mm(a, b):
    return jnp.dot(a, b, preferred_element_type=F32)


def _mm_nt(a, b):
    return lax.dot_general(a, b, NT_DIMS, preferred_element_type=F32)


def _mm_tn(a, b):
    return lax.dot_general(a, b, TN_DIMS, preferred_element_type=F32)


def _sigmoid(z):
    return jax.nn.sigmoid(z)


def _rope(v, c, s1, s2):
    return v * c + pltpu.roll(v, 112, 1) * s1 + pltpu.roll(v, 16, 1) * s2


def _rope_t(dy, c, s1, s2):
    return dy * c + pltpu.roll(dy * s1, 16, 1) + pltpu.roll(dy * s2, 112, 1)


def _rope_tables(s):
    inv = 10000.0 ** (-jnp.arange(0, QK_ROPE, 2, dtype=F32) / QK_ROPE)
    ang = jnp.arange(s, dtype=F32)[:, None] * inv[None, :]
    cos, sin = jnp.cos(ang), jnp.sin(ang)
    z64, z32, o64, o32 = jnp.zeros((s, 64), F32), jnp.zeros((s, 32), F32), jnp.ones((s, 64), F32), jnp.ones((s, 32), F32)
    z16 = jnp.zeros((s, 16), F32)
    c = jnp.concatenate([o64, cos, cos, o32], axis=1)
    s1 = jnp.concatenate([z64, -sin, z16, z32], axis=1)
    s2 = jnp.concatenate([z64, z16, sin, z32], axis=1)
    return c, s1, s2


def _front_fwd(x, g_pre, w_in_pt):
    s = x.shape[0]

    def body(x_ref, g_ref, w_ref, o_ref, h_ref):
        xv = x_ref[...]
        r = lax.rsqrt(jnp.mean(xv * xv, axis=-1, keepdims=True) + EPS)
        h = ((xv * r) * g_ref[...]).astype(BF16)
        h_ref[...] = h
        o_ref[...] = _mm_nt(h, w_ref[...])

    return pl.pallas_call(
        body, name="front_fwd", grid=(s // TM,),
        in_specs=[pl.BlockSpec((TM, D), lambda i: (i, 0)), pl.BlockSpec((1, D), lambda i: (0, 0)),
                  pl.BlockSpec((D_P, D), lambda i: (0, 0))],
        out_specs=[pl.BlockSpec((TM, D_P), lambda i: (i, 0)), pl.BlockSpec((TM, D), lambda i: (i, 0))],
        out_shape=[jax.ShapeDtypeStruct((s, D_P), F32), jax.ShapeDtypeStruct((s, D), BF16)],
        compiler_params=_params(("parallel",)),
    )(x, g_pre, w_in_pt)


def _norm_rows(v, g):
    r = lax.rsqrt(jnp.mean(v * v, axis=-1, keepdims=True) + EPS)
    return (v * r) * g, r


def _qkv_fwd(proj, g_q, g_kv, w_uq_p, w_k_p, w_v_p, rc, rs1, rs2):
    s = proj.shape[0]

    def body(cq_ref, ckv_ref, kpe_ref, gq_ref, gkv_ref, wq_ref, wk_ref, wv_ref, c_ref, s1_ref, s2_ref, q_ref, k_ref, v_ref):
        c, s1, s2 = c_ref[...], s1_ref[...], s2_ref[...]
        cqn, _ = _norm_rows(cq_ref[...], gq_ref[...])
        ckvn, _ = _norm_rows(ckv_ref[...], gkv_ref[...])
        ckvn = ckvn.astype(BF16)
        qf = _mm(cqn.astype(BF16), wq_ref[...])
        kf = _mm(ckvn, wk_ref[...])
        vf = _mm(ckvn, wv_ref[...])
        kpe = _rope(kpe_ref[...], c, s1, s2)
        for h in range(NH):
            blk = slice(h * LANE, (h + 1) * LANE)
            q_ref[h] = _rope(qf[:, blk], c, s1, s2).astype(BF16)
            k_ref[h] = (kf[:, blk] + kpe).astype(BF16)
            v_ref[h] = vf[:, blk].astype(BF16)

    row = lambda w, j: pl.BlockSpec((TM, w), lambda i: (i, j))
    full = lambda a: pl.BlockSpec(a.shape, lambda i: (0,) * a.ndim)
    hs = jax.ShapeDtypeStruct((NH, s, LANE), BF16)
    return pl.pallas_call(
        body, name="qkv_fwd", grid=(s // TM,),
        in_specs=[row(Q_LORA, P_CQ // Q_LORA), row(KV_LORA, P_CKV // KV_LORA), row(LANE, P_KPE // LANE),
                  full(g_q), full(g_kv), full(w_uq_p), full(w_k_p), full(w_v_p), row(LANE, 0), row(LANE, 0), row(LANE, 0)],
        out_specs=[pl.BlockSpec((NH, TM, LANE), lambda i: (0, i, 0))] * 3,
        out_shape=[hs, hs, hs],
        compiler_params=_params(("parallel",)),
    )(proj, proj, proj, g_q, g_kv, w_uq_p, w_k_p, w_v_p, rc, rs1, rs2)


LOG2E = 1.4426950408889634
QK_SCALE2 = LOG2E / math.sqrt(QK_NOPE + QK_ROPE)


def _diag_visible():
    row = lax.broadcasted_iota(jnp.int32, (TQ, TQ), 0)
    col = lax.broadcasted_iota(jnp.int32, (TQ, TQ), 1)
    return (col // CHUNK) <= (row // CHUNK)


def _attn_fwd(q, k, vv):
    s = q.shape[1]

    def body(q_ref, k_ref, v_ref, o_ref, lse_ref):
        i = pl.program_id(1)
        qs = (q_ref[0], q_ref[1])

        def tile(hh, t, carry, diag):
            m, l, acc = carry
            rows = pl.ds(pl.multiple_of(t * TQ, TQ), TQ)
            sc = _mm_nt(qs[hh], k_ref[hh, rows, :])
            if diag:
                sc = jnp.where(_diag_visible(), sc, -jnp.inf)
            m_new = jnp.maximum(m, jnp.max(sc, axis=-1, keepdims=True))
            alpha = jnp.exp2((m - m_new) * QK_SCALE2)
            p = jnp.exp2((sc - m_new) * QK_SCALE2)
            l = alpha * l + jnp.sum(p, axis=-1, keepdims=True)
            acc = alpha * acc + _mm(p.astype(BF16), v_ref[hh, rows, :])
            return m_new, l, acc

        def step(t, carry):
            return tile(0, t, carry[0], False), tile(1, t, carry[1], False)

        init = (jnp.full((TQ, 1), -jnp.inf, F32), jnp.zeros((TQ, 1), F32), jnp.zeros((TQ, LANE), F32))
        carry = lax.fori_loop(0, i, step, (init, init))
        out = jnp.zeros((TQ, LANE), F32)
        for hh in range(2):
            m, l, acc = tile(hh, i, carry[hh], True)
            out = out + acc / l
            lse_ref[hh] = jnp.broadcast_to(m * QK_SCALE2 + jnp.log(l) * LOG2E, (TQ, LANE))
        o_ref[...] = out

    return pl.pallas_call(
        body, name="attn_fwd", grid=(NH // 2, s // TQ),
        in_specs=[pl.BlockSpec((2, TQ, LANE), lambda p, i: (p, i, 0)), pl.BlockSpec((2, s, LANE), lambda p, i: (p, 0, 0)),
                  pl.BlockSpec((2, s, LANE), lambda p, i: (p, 0, 0))],
        out_specs=[pl.BlockSpec((TQ, LANE), lambda p, i: (i, p)), pl.BlockSpec((2, TQ, LANE), lambda p, i: (p, i, 0))],
        out_shape=[jax.ShapeDtypeStruct((s, NH * V_DIM), F32), jax.ShapeDtypeStruct((NH, s, LANE), F32)],
        compiler_params=_params(("parallel", "parallel")),
    )(q, k, vv)


def _lower_bound(lbl):
    a0, a1 = lbl[0:1, :], lbl[1:2, :]
    mx = jnp.maximum(a0, a1)
    e0, e1 = jnp.exp(a0 - mx), jnp.exp(a1 - mx)
    return e0 / (e0 + e1)


def _chunk_cumsum(v, reverse=False):
    pos = lax.broadcasted_iota(jnp.int32, v.shape, 0) % HG_BLOCK
    s = 1
    while s < HG_BLOCK:
        if reverse:
            v = v + jnp.where(pos < HG_BLOCK - s, pltpu.roll(v, TH - s, 0), 0.0)
        else:
            v = v + jnp.where(pos >= s, pltpu.roll(v, s, 0), 0.0)
        s *= 2
    return v


def _hgrn_gates(hq, hf, lb):
    sig = _sigmoid(hf)
    f = lb + (1.0 - lb) * sig
    g = jnp.log(f)
    kk = 1.0 - f
    r = lax.broadcasted_iota(jnp.int32, (TH, TH), 0)
    c = lax.broadcasted_iota(jnp.int32, (TH, TH), 1)
    tri = ((r // HG_BLOCK) == (c // HG_BLOCK)) & (r >= c)
    cum = _chunk_cumsum(g)
    nch = TH // HG_BLOCK
    total = _chunks(cum)[:, HG_BLOCK - 1:HG_BLOCK, :]
    lastb = jnp.broadcast_to(total, (nch, HG_BLOCK, LANE)).reshape(TH, LANE)
    e, ei, ee = jnp.exp(cum), jnp.exp(-cum), jnp.exp(lastb - cum)
    return dict(sig=sig, f=f, kk=kk, tri=tri, cum=cum, total=total, e=e, ei=ei, ee=ee, qd=hq * e, ki=kk * ei, ke=kk * ee)


def _chunks(v):
    return v.reshape(TH // HG_BLOCK, HG_BLOCK, v.shape[-1])


def _bmm_nt(a, b):
    return lax.dot_general(a, b, (((2,), (2,)), ((0,), (0,))), preferred_element_type=F32)


def _bmm_nn(a, b):
    return lax.dot_general(a, b, (((2,), (1,)), ((0,), (0,))), preferred_element_type=F32)


def _bmm_tn(a, b):
    return lax.dot_general(a, b, (((1,), (1,)), ((0,), (0,))), preferred_element_type=F32)


def _pair_masks():
    lane = lax.broadcasted_iota(jnp.int32, (TH, LANE), 1)
    kr = lax.broadcasted_iota(jnp.int32, (LANE, LANE), 0)
    kc = lax.broadcasted_iota(jnp.int32, (LANE, LANE), 1)
    return lane < 64, (kr // 64) == (kc // 64)


def _hgrn_fwd(proj, lbl):
    s = proj.shape[0]
    nch = TH // HG_BLOCK

    def body(hq_ref, hf_ref, hi_ref, lbl_ref, o_ref, st_ref, st):
        @pl.when(pl.program_id(1) == 0)
        def _():
            st[...] = jnp.zeros_like(st)

        m0, bd = _pair_masks()
        for u in range(HG_PAIRS):
            lanes = slice(u * LANE, (u + 1) * LANE)
            lb = _lower_bound(lbl_ref[:, lanes])
            gt = _hgrn_gates(hq_ref[:, lanes], hf_ref[:, lanes], lb)
            v_b = hi_ref[:, lanes].astype(BF16)
            qd, ki_b, ke_b = gt["qd"], gt["ki"].astype(BF16), gt["ke"].astype(BF16)
            qd_b = qd.astype(BF16)
            o = jnp.zeros((TH, LANE), F32)
            for hh in range(2):
                mh = m0 if hh == 0 else jnp.logical_not(m0)
                a = jnp.where(gt["tri"], _mm_nt(jnp.where(mh, qd, 0.0).astype(BF16), ki_b), 0.0)
                o = jnp.where(mh, _mm(a.astype(BF16), v_b), o)
            upd = _bmm_tn(_chunks(v_b), _chunks(ke_b))
            decay = jnp.exp(gt["total"])
            cur, entering = st[u], []
            for n in range(nch):
                entering.append(cur)
                cur = decay[n] * cur + jnp.where(bd, upd[n], 0.0)
            st[u] = cur
            entering = jnp.stack(entering)
            st_ref[u] = entering
            o_ref[:, lanes] = o + _bmm_nt(_chunks(qd_b), entering.astype(BF16)).reshape(TH, LANE)

    wide = HG_PAIRS * LANE
    col = lambda base: pl.BlockSpec((TH, wide), lambda p, i: (i, base // wide + p))
    return pl.pallas_call(
        body, name="hgrn_fwd", grid=(NH // 2 // HG_PAIRS, s // TH),
        in_specs=[col(P_HQ), col(P_HF), col(P_HI), pl.BlockSpec((2, wide), lambda p, i: (0, p))],
        out_specs=[pl.BlockSpec((TH, wide), lambda p, i: (i, p)),
                   pl.BlockSpec((HG_PAIRS, nch, LANE, LANE), lambda p, i: (p, i, 0, 0))],
        out_shape=[jax.ShapeDtypeStruct((s, 512), F32), jax.ShapeDtypeStruct((NH // 2, s // HG_BLOCK, LANE, LANE), F32)],
        scratch_shapes=[pltpu.VMEM((HG_PAIRS, LANE, LANE), F32)],
        compiler_params=_params(("parallel", "arbitrary")),
    )(proj, proj, proj, lbl)


def _group_sum(v):
    r = lax.broadcasted_iota(jnp.int32, (512, 512), 0)
    c = lax.broadcasted_iota(jnp.int32, (512, 512), 1)
    return jnp.dot(v, ((r // 64) == (c // 64)).astype(F32), precision=HIGHEST, preferred_element_type=F32)


def _dsilu(z, sg):
    return sg * (1.0 + z * (1.0 - sg))


def _mid(proj, attn, o_raw, x, tgt, g_hg, b_gate, g_post, wa, wb, w_out):
    s = x.shape[0]

    def body(attn_ref, ga_ref, o_ref, gb_ref, mg_ref, x_ref, t_ref, ghg_ref, bg_ref, gp_ref, wa_ref, wb_ref, wo_ref,
             loss_ref, dout_ref, dattn_ref, dga_ref, dor_ref, dgb_ref, dmg_ref, dwo_ref, dwa_ref, dwb_ref, dgp_ref, dbg_ref, dghg_ref):
        first = pl.program_id(0) == 0

        @pl.when(first)
        def _():
            for rf in (loss_ref, dwo_ref, dwa_ref, dwb_ref, dgp_ref, dbg_ref, dghg_ref):
                rf[...] = jnp.zeros_like(rf)

        attn, za, orw, zb = attn_ref[...], ga_ref[...], o_ref[...], gb_ref[...]
        ghg, gp = ghg_ref[...], gp_ref[...]
        sga, sgb = _sigmoid(za), _sigmoid(zb)
        sa, sb = za * sga, zb * sgb
        ga = attn * sa
        rh = lax.rsqrt(_group_sum(orw * orw) * (1.0 / V_DIM) + EPS)
        on = (orw * rh) * ghg
        gb = on * sb
        ga_b, gb_b = ga.astype(BF16), gb.astype(BF16)
        ya = _mm(ga_b, wa_ref[...])
        yb = _mm(gb_b, wb_ref[...])
        gates = _sigmoid(mg_ref[...] + bg_ref[...])
        g0, g1 = gates[:, :D], gates[:, D:]
        m_b = (g0 * ya + g1 * yb).astype(BF16)
        y = _mm(m_b, wo_ref[...])
        ry = lax.rsqrt(jnp.mean(y * y, axis=-1, keepdims=True) + EPS)
        out = x_ref[...] + (y * ry) * gp
        err = out - t_ref[...]
        loss_ref[...] += 0.5 * jnp.sum(jnp.mean(err * err, axis=-1, keepdims=True), axis=0, keepdims=True)
        dout = err * (1.0 / D)
        dout_ref[...] = dout
        dgp_ref[...] += jnp.sum(dout * (y * ry), axis=0, keepdims=True)
        dgy = dout * gp
        dy = ry * dgy - y * (ry * ry * ry) * jnp.mean(y * dgy, axis=-1, keepdims=True)
        dy_b = dy.astype(BF16)
        dwo_ref[...] += _mm_tn(m_b, dy_b)
        dm = _mm_nt(dy_b, wo_ref[...])
        dya, dyb = dm * g0, dm * g1
        dg0, dg1 = dm * ya, dm * yb
        dmg = jnp.concatenate([dg0 * g0 * (1.0 - g0), dg1 * g1 * (1.0 - g1)], axis=1)
        dmg_ref[...] = dmg.astype(BF16)
        dbg_ref[...] += jnp.sum(dmg, axis=0, keepdims=True)
        dya_b, dyb_b = dya.astype(BF16), dyb.astype(BF16)
        dwa_ref[...] += _mm_tn(ga_b, dya_b)
        dwb_ref[...] += _mm_tn(gb_b, dyb_b)
        dga = _mm_nt(dya_b, wa_ref[...])
        dgb = _mm_nt(dyb_b, wb_ref[...])
        dattn_ref[...] = dga * sa
        dga_ref[...] = (dga * attn * _dsilu(za, sga)).astype(BF16)
        dgb_ref[...] = (dgb * on * _dsilu(zb, sgb)).astype(BF16)
        don = dgb * sb
        dghg_ref[...] += jnp.sum(don * (orw * rh), axis=0, keepdims=True)
        dgo = don * ghg
        dor_ref[...] = rh * dgo - orw * (rh * rh * rh) * (_group_sum(orw * dgo) * (1.0 / V_DIM))

    row = lambda w, j=0: pl.BlockSpec((TM, w), lambda i: (i, j))
    full = lambda a: pl.BlockSpec(a.shape, lambda i: (0,) * a.ndim)
    acc = lambda shape: pl.BlockSpec(shape, lambda i: (0, 0))
    sds = jax.ShapeDtypeStruct
    return pl.pallas_call(
        body, name="mid", grid=(s // TM,),
        in_specs=[row(512), row(512, P_GA // 512), row(512), row(512, P_GB // 512), row(2048, P_MERGE // 2048), row(D), row(D),
                  full(g_hg), full(b_gate), full(g_post), full(wa), full(wb), full(w_out)],
        out_specs=[acc((1, 1)), row(D), row(512), row(512), row(512), row(512), row(2048),
                   acc((D, D)), acc((512, D)), acc((512, D)), acc((1, D)), acc((1, 2048)), acc((1, 512))],
        out_shape=[sds((1, 1), F32), sds((s, D), F32), sds((s, 512), F32), sds((s, 512), BF16), sds((s, 512), F32), sds((s, 512), BF16),
                   sds((s, 2048), BF16), sds((D, D), F32), sds((512, D), F32), sds((512, D), F32), sds((1, D), F32),
                   sds((1, 2048), F32), sds((1, 512), F32)],
        compiler_params=_params(("arbitrary",)),
    )(attn, proj, o_raw, proj, proj, x, tgt, g_hg, b_gate, g_post, wa, wb, w_out)


def _attn_bwd(q, k, vv, attn, dattn, lse, token):
    s = q.shape[1]
    nt = s // TQ
    scale = 1.0 / math.sqrt(QK_NOPE + QK_ROPE)

    def body(q_ref, k_ref, v_ref, o_ref, do_ref, lse_ref, token_ref, dq_ref, dk_ref, dv_ref, do_s, delta_s):
        j = pl.program_id(1)

        @pl.when(j == 0)
        def _():
            dq_ref[...] = jnp.zeros_like(dq_ref)
            lane = lax.broadcasted_iota(jnp.int32, (TQ, LANE), 1)

            @pl.loop(0, nt)
            def _(i):
                rows = pl.ds(pl.multiple_of(i * TQ, TQ), TQ)
                do, o = do_ref[rows, :], o_ref[rows, :]
                for hh in range(2):
                    doh = jnp.where((lane < 64) if hh == 0 else (lane >= 64), do, 0.0)
                    do_s[hh, rows, :] = doh.astype(BF16)
                    delta_s[hh, rows, :] = jnp.broadcast_to(jnp.sum(doh * o, axis=-1, keepdims=True), (TQ, LANE))

        kjs, vjs = (k_ref[0], k_ref[1]), (v_ref[0], v_ref[1])
        wide = lambda a: jnp.concatenate([a] * (TQ // LANE), axis=1)

        def tile(hh, i, carry, diag):
            dk, dv = carry
            rows = pl.ds(pl.multiple_of(i * TQ, TQ), TQ)
            qi, do_b = q_ref[hh, rows, :], do_s[hh, rows, :]
            p = jnp.exp2(_mm_nt(qi, kjs[hh]) * QK_SCALE2 - wide(lse_ref[hh, rows, :]))
            if diag:
                p = jnp.where(_diag_visible(), p, 0.0)
            dv = dv + _mm_tn(p.astype(BF16), do_b)
            ds_b = (p * (_mm_nt(do_b, vjs[hh]) - wide(delta_s[hh, rows, :]))).astype(BF16)
            dk = dk + _mm_tn(ds_b, qi)
            dq_ref[hh, rows, :] += _mm(ds_b, kjs[hh])
            return dk, dv

        def step(i, carry):
            return tile(0, i, carry[0], False), tile(1, i, carry[1], False)

        z = jnp.zeros((TQ, LANE), F32)
        first = (tile(0, j, (z, z), True), tile(1, j, (z, z), True))
        carry = lax.fori_loop(j + 1, nt, step, first)
        for hh in range(2):
            dk_ref[hh] = carry[hh][0] * scale
            dv_ref[hh] = carry[hh][1]

        @pl.when(j == nt - 1)
        def _():
            dq_ref[...] = dq_ref[...] * scale

    whole = pl.BlockSpec((2, s, LANE), lambda p, j: (p, 0, 0))
    tile_spec = pl.BlockSpec((2, TQ, LANE), lambda p, j: (p, j, 0))
    cols = pl.BlockSpec((s, LANE), lambda p, j: (0, p))
    hs = jax.ShapeDtypeStruct((NH, s, LANE), F32)
    return pl.pallas_call(
        body, name="attn_bwd", grid=(NH // 2, nt),
        in_specs=[whole, tile_spec, tile_spec, cols, cols, whole, pl.BlockSpec((8, LANE), lambda p, j: (0, 0))],
        out_specs=[whole, tile_spec, tile_spec],
        out_shape=[hs, hs, hs],
        scratch_shapes=[pltpu.VMEM((2, s, LANE), BF16), pltpu.VMEM((2, s, LANE), F32)],
        compiler_params=_params(("parallel", "arbitrary")),
    )(q, k, vv, attn, dattn, lse, token)


def _hgrn_bwd(proj, lbl, states, do_raw):
    s = proj.shape[0]
    nt = s // TH
    nch = TH // HG_BLOCK

    def body(hq_ref, hf_ref, hi_ref, lbl_ref, st_ref, do_ref, dh_ref, dlbl_ref, dst, dlb):
        step = pl.program_id(1)

        @pl.when(step == 0)
        def _():
            dst[...] = jnp.zeros_like(dst)
            dlb[...] = jnp.zeros_like(dlb)

        m0, bd = _pair_masks()
        for u in range(HG_PAIRS):
            lanes = slice(u * LANE, (u + 1) * LANE)
            lb = _lower_bound(lbl_ref[:, lanes])
            gt = _hgrn_gates(hq_ref[:, lanes], hf_ref[:, lanes], lb)
            do = do_ref[:, lanes]
            qd, ki, ke = gt["qd"], gt["ki"], gt["ke"]
            v_b, do_b = hi_ref[:, lanes].astype(BF16), do.astype(BF16)
            qd_b, ki_b, ke_b = qd.astype(BF16), ki.astype(BF16), ke.astype(BF16)
            dv = jnp.zeros((TH, LANE), F32)
            dqd = jnp.zeros((TH, LANE), F32)
            dki = jnp.zeros((TH, LANE), F32)
            for hh in range(2):
                mh = m0 if hh == 0 else jnp.logical_not(m0)
                a_b = jnp.where(gt["tri"], _mm_nt(jnp.where(mh, qd, 0.0).astype(BF16), ki_b), 0.0).astype(BF16)
                doh_b = jnp.where(mh, do, 0.0).astype(BF16)
                da_b = jnp.where(gt["tri"], _mm_nt(doh_b, v_b), 0.0).astype(BF16)
                dv = dv + _mm_tn(a_b, doh_b)
                dqd = jnp.where(mh, _mm(da_b, ki_b), dqd)
                dki = jnp.where(mh, _mm_tn(da_b, qd_b), dki)
            fed = _bmm_tn(_chunks(do_b), _chunks(qd_b))
            decay = jnp.exp(gt["total"])
            ds, leaving = dst[u], [None] * nch
            for n in reversed(range(nch)):
                leaving[n] = ds
                ds = decay[n] * ds + jnp.where(bd, fed[n], 0.0)
            dst[u] = ds
            leaving = jnp.stack(leaving)
            entering = st_ref[u]
            leaving_b = leaving.astype(BF16)
            dke3 = _bmm_nn(_chunks(v_b), leaving_b)
            dv = dv + _bmm_nt(_chunks(ke_b), leaving_b).reshape(TH, LANE)
            dqd = dqd + _bmm_nn(_chunks(do_b), entering.astype(BF16)).reshape(TH, LANE)
            dke = dke3.reshape(TH, LANE)
            dlast = (jnp.sum(dke3 * _chunks(ke), axis=1, keepdims=True)
                     + jnp.sum(leaving * entering, axis=1, keepdims=True) * decay)
            dk = dki * gt["ei"] + dke * gt["ee"]
            dcum = dqd * qd - dki * ki - dke * ke
            dg = _chunk_cumsum(dcum, reverse=True) + jnp.broadcast_to(dlast, (nch, HG_BLOCK, LANE)).reshape(TH, LANE)
            sig = gt["sig"]
            df = dg / gt["f"] - dk
            dlb[:, lanes] += jnp.sum(df * (1.0 - sig), axis=0, keepdims=True)
            dh_ref[0, :, lanes] = (dqd * gt["e"]).astype(BF16)
            dh_ref[1, :, lanes] = ((df * (1.0 - lb)) * sig * (1.0 - sig)).astype(BF16)
            dh_ref[2, :, lanes] = dv.astype(BF16)

        @pl.when(step == nt - 1)
        def _():
            lb = _lower_bound(lbl_ref[...])
            da0 = dlb[...] * lb * (1.0 - lb)
            dlbl_ref[...] = jnp.concatenate([da0, -da0], axis=0)

    wide = HG_PAIRS * LANE
    col = lambda base: pl.BlockSpec((TH, wide), lambda p, i: (nt - 1 - i, base // wide + p))
    tile = pl.BlockSpec((TH, wide), lambda p, i: (nt - 1 - i, p))
    sds = jax.ShapeDtypeStruct
    return pl.pallas_call(
        body, name="hgrn_bwd", grid=(NH // 2 // HG_PAIRS, nt),
        in_specs=[col(P_HQ), col(P_HF), col(P_HI), pl.BlockSpec((2, wide), lambda p, i: (0, p)),
                  pl.BlockSpec((HG_PAIRS, nch, LANE, LANE), lambda p, i: (p, nt - 1 - i, 0, 0)), tile],
        out_specs=[pl.BlockSpec((3, TH, wide), lambda p, i: (0, nt - 1 - i, p)), pl.BlockSpec((2, wide), lambda p, i: (0, p))],
        out_shape=[sds((3, s, 512), BF16), sds((2, 512), F32)],
        scratch_shapes=[pltpu.VMEM((HG_PAIRS, LANE, LANE), F32), pltpu.VMEM((1, wide), F32)],
        compiler_params=_params(("parallel", "arbitrary")),
    )(proj, proj, proj, lbl, states, do_raw)


def _norm_rows_bwd(v, r, g, dn):
    dgv = dn * g
    return r * dgv - v * (r * r * r) * jnp.mean(v * dgv, axis=-1, keepdims=True)


def _qkv_bwd(proj, dq, dk, dvv, g_q, g_kv, w_uq_p, w_k_p, w_v_p, rc, rs1, rs2):
    s = proj.shape[0]

    def body(cq_ref, ckv_ref, dq_ref, dk_ref, dv_ref, gq_ref, gkv_ref, wq_ref, wk_ref, wv_ref, c_ref, s1_ref, s2_ref,
             dcq_ref, dckv_ref, dkpe_ref, dwq_ref, dwk_ref, dwv_ref, dgq_ref, dgkv_ref):
        @pl.when(pl.program_id(0) == 0)
        def _():
            for rf in (dwq_ref, dwk_ref, dwv_ref, dgq_ref, dgkv_ref):
                rf[...] = jnp.zeros_like(rf)

        c, s1, s2 = c_ref[...], s1_ref[...], s2_ref[...]
        cq, ckv = cq_ref[...], ckv_ref[...]
        gq, gkv = gq_ref[...], gkv_ref[...]
        cqn, rq = _norm_rows(cq, gq)
        ckvn, rkv = _norm_rows(ckv, gkv)
        cqn_b, ckvn_b = cqn.astype(BF16), ckvn.astype(BF16)
        dqf = jnp.concatenate([_rope_t(dq_ref[h], c, s1, s2) for h in range(NH)], axis=1).astype(BF16)
        dkf = jnp.concatenate([dk_ref[h] for h in range(NH)], axis=1).astype(BF16)
        dvf = jnp.concatenate([dv_ref[h] for h in range(NH)], axis=1).astype(BF16)
        dkpe = dk_ref[0]
        for h in range(1, NH):
            dkpe = dkpe + dk_ref[h]
        lane = lax.broadcasted_iota(jnp.int32, (TM, LANE), 1)
        dkpe = jnp.where((lane >= QK_NOPE) & (lane < QK_NOPE + QK_ROPE), dkpe, 0.0)
        dkpe_ref[...] = _rope_t(dkpe, c, s1, s2).astype(BF16)
        dwq_ref[...] += _mm_tn(cqn_b, dqf)
        dwk_ref[...] += _mm_tn(ckvn_b, dkf)
        dwv_ref[...] += _mm_tn(ckvn_b, dvf)
        dcqn = _mm_nt(dqf, wq_ref[...])
        dckvn = _mm_nt(dkf, wk_ref[...]) + _mm_nt(dvf, wv_ref[...])
        dgq_ref[...] += jnp.sum(dcqn * (cq * rq), axis=0, keepdims=True)
        dgkv_ref[...] += jnp.sum(dckvn * (ckv * rkv), axis=0, keepdims=True)
        dcq_ref[...] = _norm_rows_bwd(cq, rq, gq, dcqn).astype(BF16)
        dckv_ref[...] = _norm_rows_bwd(ckv, rkv, gkv, dckvn).astype(BF16)

    row = lambda w, j=0: pl.BlockSpec((TM, w), lambda i: (i, j))
    full = lambda a: pl.BlockSpec(a.shape, lambda i: (0,) * a.ndim)
    acc = lambda shape: pl.BlockSpec(shape, lambda i: (0, 0))
    heads = pl.BlockSpec((NH, TM, LANE), lambda i: (0, i, 0))
    sds = jax.ShapeDtypeStruct
    return pl.pallas_call(
        body, name="qkv_bwd", grid=(s // TM,),
        in_specs=[row(Q_LORA, P_CQ // Q_LORA), row(KV_LORA, P_CKV // KV_LORA), heads, heads, heads,
                  full(g_q), full(g_kv), full(w_uq_p), full(w_k_p), full(w_v_p), row(LANE), row(LANE), row(LANE)],
        out_specs=[row(Q_LORA), row(KV_LORA), row(LANE), acc((Q_LORA, D)), acc((KV_LORA, D)), acc((KV_LORA, D)),
                   acc((1, Q_LORA)), acc((1, KV_LORA))],
        out_shape=[sds((s, Q_LORA), BF16), sds((s, KV_LORA), BF16), sds((s, LANE), BF16), sds((Q_LORA, D), F32),
                   sds((KV_LORA, D), F32), sds((KV_LORA, D), F32), sds((1, Q_LORA), F32), sds((1, KV_LORA), F32)],
        compiler_params=_params(("arbitrary",)),
    )(proj, proj, dq, dk, dvv, g_q, g_kv, w_uq_p, w_k_p, w_v_p, rc, rs1, rs2)


def _front_bwd(x, dout, dmg, dga, dh3, dgb, dcq, dckv, dkpe, g_pre, w_in_pt, token):
    s = x.shape[0]

    def body(x_ref, do_ref, dmg_ref, dga_ref, dh3_ref, dgb_ref, dcq_ref, dckv_ref, dkpe_ref, g_ref, w_ref, token_ref, gx_ref, dg_ref):
        @pl.when(pl.program_id(0) == 0)
        def _():
            dg_ref[...] = jnp.zeros_like(dg_ref)

        xv, g = x_ref[...], g_ref[...]
        _, r = _norm_rows(xv, g)
        pieces = ((dmg_ref[...], P_MERGE), (dga_ref[...], P_GA), (dh3_ref[0], P_HQ), (dh3_ref[1], P_HF), (dh3_ref[2], P_HI),
                  (dgb_ref[...], P_GB), (dcq_ref[...], P_CQ), (dckv_ref[...], P_CKV), (dkpe_ref[...], P_KPE))
        dh = jnp.zeros((TM, D), F32)
        for piece, off in pieces:
            dh = dh + _mm(piece, w_ref[off:off + piece.shape[1], :])
        dg_ref[...] += jnp.sum(dh * (xv * r), axis=0, keepdims=True)
        gx_ref[...] = do_ref[...] + _norm_rows_bwd(xv, r, g, dh)

    row = lambda w: pl.BlockSpec((TM, w), lambda i: (i, 0))
    full = lambda a: pl.BlockSpec(a.shape, lambda i: (0,) * a.ndim)
    sds = jax.ShapeDtypeStruct
    return pl.pallas_call(
        body, name="front_bwd", grid=(s // TM,),
        in_specs=[row(D), row(D), row(2048), row(512), pl.BlockSpec((3, TM, 512), lambda i: (0, i, 0)), row(512), row(Q_LORA),
                  row(KV_LORA), row(LANE), full(g_pre), full(w_in_pt), full(token)],
        out_specs=[row(D), pl.BlockSpec((1, D), lambda i: (0, 0))],
        out_shape=[sds((s, D), F32), sds((1, D), F32)],
        compiler_params=_params(("arbitrary",)),
    )(x, dout, dmg, dga, dh3, dgb, dcq, dckv, dkpe, g_pre, w_in_pt, token)


TK_GRAD = 512


def _win_grad(h, pieces, name):
    s = h.shape[0]
    n = len(pieces)

    def body(h_ref, *refs):
        @pl.when(pl.program_id(0) == 0)
        def _():
            for o_ref in refs[n:]:
                o_ref[...] = jnp.zeros_like(o_ref)

        hv = h_ref[...]
        for d_ref, o_ref in zip(refs[:n], refs[n:]):
            if len(d_ref.shape) == 3:
                for k in range(d_ref.shape[0]):
                    o_ref[k] += _mm_tn(d_ref[k], hv)
            else:
                o_ref[...] += _mm_tn(d_ref[...], hv)

    def in_spec(p):
        if p.ndim == 3:
            return pl.BlockSpec((p.shape[0], TK_GRAD, p.shape[2]), lambda kk: (0, kk, 0))
        return pl.BlockSpec((TK_GRAD, p.shape[1]), lambda kk: (kk, 0))

    out_shapes = [(p.shape[0], p.shape[2], D) if p.ndim == 3 else (p.shape[1], D) for p in pieces]
    return pl.pallas_call(
        body, name=name, grid=(s // TK_GRAD,),
        in_specs=[pl.BlockSpec((TK_GRAD, D), lambda kk: (kk, 0))] + [in_spec(p) for p in pieces],
        out_specs=[pl.BlockSpec(sh, lambda kk, nd=len(sh): (0,) * nd) for sh in out_shapes],
        out_shape=[jax.ShapeDtypeStruct(sh, F32) for sh in out_shapes],
        compiler_params=_params(("arbitrary",)),
    )(h, *pieces)


def _pad_win_t(w_in_t):
    z = lambda n: jnp.zeros((n, w_in_t.shape[1]), w_in_t.dtype)
    sl = lambda o, n: w_in_t[o:o + n]
    return jnp.concatenate([sl(O_MERGE, 2048), sl(O_GA, 512), sl(O_HQ, 512), sl(O_HF, 512), sl(O_HI, 512), sl(O_GB, 512),
                            sl(O_CQ, Q_LORA), sl(O_CKV, KV_LORA), z(64), sl(O_KPE, QK_ROPE), z(32)], axis=0)


def _pad_wuq(w_uq):
    w = w_uq.reshape(Q_LORA, NH, QK_NOPE + QK_ROPE)
    return jnp.pad(w, ((0, 0), (0, 0), (0, LANE - QK_NOPE - QK_ROPE))).reshape(Q_LORA, NH * LANE)


def _unpad_wuq(g):
    return g.reshape(Q_LORA, NH, LANE)[:, :, :QK_NOPE + QK_ROPE].reshape(Q_LORA, NH * (QK_NOPE + QK_ROPE))


def _pad_wukv(w_ukv):
    w = w_ukv.reshape(KV_LORA, NH, QK_NOPE + V_DIM)
    w_k = jnp.pad(w[:, :, :QK_NOPE], ((0, 0), (0, 0), (0, LANE - QK_NOPE))).reshape(KV_LORA, NH * LANE)
    wv = w[:, :, QK_NOPE:].reshape(KV_LORA, NH // 2, 2, 1, V_DIM)
    eye = jnp.eye(2, dtype=w.dtype).reshape(1, 1, 2, 2, 1)
    return w_k, (wv * eye).reshape(KV_LORA, NH * LANE)


def _unpad_wukv(gk, gv):
    gk = gk.reshape(KV_LORA, NH, LANE)[:, :, :QK_NOPE]
    gv = gv.reshape(KV_LORA, NH // 2, 2, 2, V_DIM)
    gv = jnp.stack([gv[:, :, 0, 0], gv[:, :, 1, 1]], axis=2).reshape(KV_LORA, NH, V_DIM)
    return jnp.concatenate([gk, gv], axis=-1).reshape(KV_LORA, NH * (QK_NOPE + V_DIM))


def _local_step(x, tgt, g_pre, w_in_t, b_gate, g_q, w_uq, g_kv, w_ukv, lb_logits, g_hgrn, wa, wb, w_out, g_post, exchange=None):
    s = x.shape[0]
    w_in_p = _pad_win_t(w_in_t)
    w_uq_p = _pad_wuq(w_uq)
    w_k_p, w_v_p = _pad_wukv(w_ukv)
    rc, rs1, rs2 = _rope_tables(s)
    g_hg = jnp.tile(g_hgrn, (1, NH))

    proj, h = _front_fwd(x, g_pre, w_in_p)
    q, k, vv = _qkv_fwd(proj, g_q, g_kv, w_uq_p, w_k_p, w_v_p, rc, rs1, rs2)
    attn, lse = _attn_fwd(q, k, vv)
    o_raw, states = _hgrn_fwd(proj, lb_logits)
    (loss, dout, dattn, dga, dor, dgb, dmg, d_wout, d_wa, d_wb, d_gpost, d_bgate, d_ghg) = _mid(
        proj, attn, o_raw, x, tgt, g_hg, b_gate, g_post, wa, wb, w_out)
    w_mg, w_ga, w_gb = _win_grad(h, [dmg, dga, dgb], "win_grad_mid")
    dh3, d_lbl = _hgrn_bwd(proj, lb_logits, states, dor)
    (w_h3,) = _win_grad(h, [dh3], "win_grad_hgrn")
    d_win_rest = jnp.concatenate([w_ga, w_h3[0], w_h3[1], w_h3[2], w_gb, w_mg], axis=0)
    early = dict(w_in_rest=d_win_rest, w_branch_a=d_wa, w_branch_b=d_wb, w_out=d_wout)
    token = exchange.start_early(early) if exchange else jnp.zeros((8, LANE), F32)
    dq, dk, dvv = _attn_bwd(q, k, vv, attn, dattn, lse, token)
    dcq, dckv, dkpe, d_wuq_p, d_wk_p, d_wv_p, d_gq, d_gkv = _qkv_bwd(proj, dq, dk, dvv, g_q, g_kv, w_uq_p, w_k_p, w_v_p, rc, rs1, rs2)
    w_cq, w_ckv, w_kpe = _win_grad(h, [dcq, dckv, dkpe], "win_grad_qkv")
    d_win_qkv = jnp.concatenate([w_cq, w_ckv, w_kpe[64:64 + QK_ROPE]], axis=0)
    late = dict(w_in_qkv=d_win_qkv, w_uq=_unpad_wuq(d_wuq_p), w_ukv=_unpad_wukv(d_wk_p, d_wv_p))
    token = exchange.start_late(late) if exchange else jnp.zeros((8, LANE), F32)
    grad_x, d_gpre = _front_bwd(x, dout, dmg, dga, dh3, dgb, dcq, dckv, dkpe, g_pre, w_in_p, token)
    vec_grads = dict(g_pre=d_gpre, b_gate=d_bgate, g_q=d_gq, g_kv=d_gkv, lb_logits=d_lbl, g_hgrn=d_ghg, g_post=d_gpost)
    return loss, grad_x, dict(early, **late), vec_grads


SHARD_SHAPES = (("w_in", (1416, 1024)), ("w_uq", (192, 768)), ("w_ukv", (256, 256)), ("w_branch_a", (512, 256)),
                ("w_branch_b", (512, 256)), ("w_out", (256, 1024)))
BIG = tuple(n for n, _ in SHARD_SHAPES)
ROW_SHARDED = ("w_in", "w_uq", "w_out")
GATHER_SPLIT_AXIS = dict(w_in=1, w_uq=0, w_ukv=0, w_branch_a=0, w_branch_b=0, w_out=0)
N_CHIPS = 4
QKV_ROWS = Q_LORA + KV_LORA + QK_ROPE


def _to_block(name, a):
    return a[0].T if name == "w_in" else a[0]


def _from_block(name, a):
    return a.T[None] if name == "w_in" else a[None]
VEC_ROWS = (("g_pre", 0, 1024), ("b_gate", 1, 2048), ("g_q", 2, 768), ("g_kv", 3, 256), ("g_hgrn", 6, 64), ("g_post", 7, 1024))
VEC_LB_ROW = 4
VEC_SHAPE = (8, 2048)


def _split_by_chip(name, g):
    a, b = dict(SHARD_SHAPES)[name]
    return g.reshape(N_CHIPS, a, b) if name in ROW_SHARDED else g.reshape(a, N_CHIPS, b).transpose(1, 0, 2)


def _join_chips(name, w):
    a, b = dict(SHARD_SHAPES)[name]
    return w.reshape(N_CHIPS * a, b) if name in ROW_SHARDED else w.transpose(1, 0, 2).reshape(a, N_CHIPS * b)


MESH = pl.DeviceIdType.MESH
HBM = pl.BlockSpec(memory_space=pltpu.HBM)


def _mesh_place():
    x, y, c = lax.axis_index("x"), lax.axis_index("y"), lax.axis_index("c")
    return x, y, c, 2 * x + y, [(1 - x, y), (x, 1 - y), (1 - x, 1 - y)]


def _remote(src, dst, send_sems, recv_sems, k, to):
    return pltpu.make_async_remote_copy(src_ref=src, dst_ref=dst, send_sem=send_sems.at[k], recv_sem=recv_sems.at[k],
                                        device_id=to, device_id_type=MESH)


def _gather_weights(shards, split_axes):
    n = len(shards)

    def body(*refs):
        srcs, outs = refs[:n], refs[n:2 * n]
        ici_send, ici_recv, d2d_send, d2d_recv, local_sems = refs[2 * n:]
        x, y, c, me, chips = _mesh_place()
        sibling = (x, y, 1 - c)

        def half(ref, k, which):
            size = shards[k].shape[split_axes[k]] // 2
            part = pl.ds(pl.multiple_of(which * size, size), size)
            return ref.at[part] if split_axes[k] == 0 else ref.at[:, part]

        own = [pltpu.make_async_copy(srcs[k], outs[k].at[me], local_sems.at[k]) for k in range(n)]
        for cp in own:
            cp.start()
        started = []
        for k in range(n):
            for j, (px, py) in enumerate(chips):
                cp = _remote(half(srcs[k], k, c), half(outs[k].at[me], k, c), ici_send, ici_recv, 3 * k + j, (px, py, c))
                cp.start()
                started.append(cp)
        for k in range(n):
            for j, (px, py) in enumerate(chips):
                landed = half(outs[k].at[2 * px + py], k, c)
                _remote(landed, landed, ici_send, ici_recv, 3 * k + j, (px, py, c)).wait_recv()
                cp = _remote(landed, landed, d2d_send, d2d_recv, 3 * k + j, sibling)
                cp.start()
                started.append(cp)
        for k in range(n):
            for j, (px, py) in enumerate(chips):
                other = half(outs[k].at[2 * px + py], k, 1 - c)
                _remote(other, other, d2d_send, d2d_recv, 3 * k + j, sibling).wait_recv()
        for cp in started:
            cp.wait_send()
        for cp in own:
            cp.wait()

    sems = pltpu.SemaphoreType.DMA((3 * n,))
    return pl.pallas_call(
        body, name="gather_weights", in_specs=[HBM] * n, out_specs=[HBM] * n,
        out_shape=[jax.ShapeDtypeStruct((N_CHIPS,) + s.shape, s.dtype) for s in shards],
        scratch_shapes=[sems, sems, sems, sems, pltpu.SemaphoreType.DMA((n,))],
        compiler_params=pltpu.CompilerParams(has_side_effects=True),
    )(*shards)


def _chip_exchange(srcs, name):
    n = len(srcs)

    def body(*refs):
        src_refs, outs = refs[:n], refs[n:2 * n]
        send_sems, recv_sems, local_sems = refs[2 * n:]
        x, y, c, me, chips = _mesh_place()

        def slab(k, t):
            return src_refs[k] if srcs[k].ndim == 2 else src_refs[k].at[t]

        own = [pltpu.make_async_copy(slab(k, me), outs[k].at[me], local_sems.at[k]) for k in range(n)]
        for cp in own:
            cp.start()
        sends = []
        for k in range(n):
            for j, (px, py) in enumerate(chips):
                cp = _remote(slab(k, 2 * px + py), outs[k].at[me], send_sems, recv_sems, 3 * k + j, (px, py, c))
                cp.start()
                sends.append(cp)
        for k in range(n):
            for j, (px, py) in enumerate(chips):
                _remote(slab(k, me), outs[k].at[2 * px + py], send_sems, recv_sems, 3 * k + j, (px, py, c)).wait_recv()
        for cp in sends:
            cp.wait_send()
        for cp in own:
            cp.wait()

    sems = pltpu.SemaphoreType.DMA((3 * n,))
    return pl.pallas_call(
        body, name=name, in_specs=[HBM] * n, out_specs=[HBM] * n,
        out_shape=[jax.ShapeDtypeStruct((N_CHIPS,) + s.shape[-2:], s.dtype) for s in srcs],
        scratch_shapes=[sems, sems, pltpu.SemaphoreType.DMA((n,))],
        compiler_params=pltpu.CompilerParams(has_side_effects=True),
    )(*srcs)


def _sibling_exchange(srcs, name):
    n = len(srcs)

    def body(*refs):
        src_refs, outs = refs[:n], refs[n:2 * n]
        send_sems, recv_sems = refs[2 * n:]
        sibling = (lax.axis_index("x"), lax.axis_index("y"), 1 - lax.axis_index("c"))
        copies = [_remote(src_refs[k], outs[k], send_sems, recv_sems, k, sibling) for k in range(n)]
        for cp in copies:
            cp.start()
        for cp in copies:
            cp.wait()

    sems = pltpu.SemaphoreType.DMA((n,))
    return pl.pallas_call(
        body, name=name, in_specs=[HBM] * n, out_specs=[HBM] * n,
        out_shape=[jax.ShapeDtypeStruct(s.shape, s.dtype) for s in srcs],
        scratch_shapes=[sems, sems],
        compiler_params=pltpu.CompilerParams(has_side_effects=True),
    )(*srcs)


SEM = pl.BlockSpec(memory_space=pltpu.SEMAPHORE)
DATAFLOW = pltpu.SideEffectType.DATAFLOW_SIDE_EFFECTING


def _exchange_copies(srcs, to_first, src_refs, land_refs, send_sems, recv_sems):
    x, y, c, me, chips = _mesh_place()
    n = len(srcs)
    sends, recvs = [], []
    for k in range(n):
        if k in to_first:
            base = 3 * n + 4 * to_first.index(k)
            sends.append((me != 0, pltpu.make_async_remote_copy(
                src_ref=src_refs[k], dst_ref=land_refs[k].at[me], send_sem=send_sems.at[base], recv_sem=recv_sems.at[base + me],
                device_id=(0, 0, c), device_id_type=MESH)))
            for s in range(1, N_CHIPS):
                recvs.append((me == 0, pltpu.make_async_remote_copy(
                    src_ref=src_refs[k], dst_ref=land_refs[k].at[s], send_sem=send_sems.at[base], recv_sem=recv_sems.at[base + s],
                    device_id=(s // 2, s % 2, c), device_id_type=MESH)))
        else:
            for j, (px, py) in enumerate(chips):
                sends.append((None, _remote(src_refs[k].at[2 * px + py], land_refs[k].at[me], send_sems, recv_sems, 3 * k + j, (px, py, c))))
                recvs.append((None, _remote(src_refs[k].at[me], land_refs[k].at[2 * px + py], send_sems, recv_sems, 3 * k + j, (px, py, c))))
    return sends, recvs


def _when(pred, fn):
    if pred is None:
        fn()
    else:
        pl.when(pred)(fn)


def _exchange_start(srcs, to_first, name):
    n = len(srcs)
    n_sems = 3 * n + 4 * len(to_first)
    lands = [lax.empty((N_CHIPS,) + s.shape[-2:], s.dtype) for s in srcs]

    def body(*refs):
        src_refs, land_refs, send_sems, recv_sems, token = refs[:n], refs[n:2 * n], refs[2 * n], refs[2 * n + 1], refs[-1]
        sends, _ = _exchange_copies(srcs, to_first, src_refs, land_refs, send_sems, recv_sems)
        for pred, cp in sends:
            _when(pred, cp.start)
        token[...] = jnp.zeros_like(token)

    hbm = lambda a: pltpu.HBM(a.shape, a.dtype)
    res = pl.pallas_call(
        body, name=name,
        out_shape=[pltpu.SemaphoreType.DMA((n_sems,)), pltpu.SemaphoreType.DMA((n_sems,))] + [hbm(a) for a in srcs + lands]
        + [jax.ShapeDtypeStruct((8, LANE), F32)],
        in_specs=[HBM] * (2 * n), out_specs=[SEM, SEM] + [HBM] * (2 * n) + [pl.BlockSpec(memory_space=pltpu.VMEM)],
        input_output_aliases={i: 2 + i for i in range(2 * n)},
        compiler_params=pltpu.CompilerParams(has_side_effects=DATAFLOW),
    )(*[pltpu.with_memory_space_constraint(a, pltpu.HBM) for a in srcs + lands])
    return res[:-1], res[-1]


def _exchange_wait(srcs, to_first, started, after, name):
    n = len(srcs)
    send_sems, recv_sems, thru = started[0], started[1], started[2:]

    def body(*refs):
        src_refs, land_refs, send_ref, recv_ref = refs[:n], refs[n:2 * n], refs[2 * n], refs[2 * n + 1]
        sends, recvs = _exchange_copies(srcs, to_first, src_refs, land_refs, send_ref, recv_ref)
        for pred, cp in sends:
            _when(pred, cp.wait_send)
        for pred, cp in recvs:
            _when(pred, cp.wait_recv)

    res = pl.pallas_call(
        body, name=name, out_shape=[pltpu.HBM(a.shape, a.dtype) for a in thru],
        in_specs=[HBM] * (2 * n) + [SEM, SEM, pl.BlockSpec(memory_space=pl.ANY)], out_specs=[HBM] * (2 * n),
        input_output_aliases={i: i for i in range(2 * n)},
        compiler_params=pltpu.CompilerParams(has_side_effects=DATAFLOW),
    )(*thru, send_sems, recv_sems, after)
    return res[n:]


ROW_TILE = 256
COL_TILE = 256


def _block_tiling(a, b):
    if a <= ROW_TILE or a % ROW_TILE == 0:
        ta = min(a, ROW_TILE)
        return a // ta, (ta, b), lambda i: (i, 0)
    return b // COL_TILE, (a, COL_TILE), lambda i: (0, i)


def _sum_chips(parts, name):
    _, a, b = parts.shape
    steps, tile, at = _block_tiling(a, b)

    def body(p_ref, o_ref):
        f = lambda t: p_ref[t].astype(F32)
        o_ref[...] = ((f(0) + f(1)) + f(2)) + f(3)

    return pl.pallas_call(
        body, name=name, grid=(steps,),
        in_specs=[pl.BlockSpec((N_CHIPS,) + tile, lambda i: (0,) + at(i))],
        out_specs=pl.BlockSpec(tile, at),
        out_shape=jax.ShapeDtypeStruct((a, b), F32),
        compiler_params=_params(("parallel",)),
    )(parts)


def _sum_landed(land, own, name, first_land=None, first_own=None):
    _, a, b = land.shape
    steps, tile, at = _block_tiling(a, b)
    extra = first_land is not None

    def body(*refs):
        p_ref, own_ref, o_ref = refs[0], refs[1], refs[-1]
        me = 2 * lax.axis_index("x") + lax.axis_index("y")
        own = own_ref[...].astype(F32)
        slot = lambda t: jnp.where(me == t, own, p_ref[t].astype(F32))
        o_ref[...] = ((slot(0) + slot(1)) + slot(2)) + slot(3)
        if extra:
            fp_ref, fo_ref = refs[2], refs[3]
            r = fo_ref.shape[0]

            @pl.when(me == 0)
            def _():
                f = lambda t: fp_ref[t].astype(F32)
                o_ref[0:r, :] += ((fo_ref[...].astype(F32) + f(1)) + f(2)) + f(3)

    in_specs = [pl.BlockSpec((N_CHIPS,) + tile, lambda i: (0,) + at(i)), pl.BlockSpec(tile, at)]
    args = [land, own]
    if extra:
        r = first_own.shape[0]
        assert tile[0] == a, "the extra rows need whole columns in a step"
        in_specs += [pl.BlockSpec((N_CHIPS, r, tile[1]), lambda i: (0,) + at(i)), pl.BlockSpec((r, tile[1]), at)]
        args += [first_land, first_own]
    return pl.pallas_call(
        body, name=name, grid=(steps,), in_specs=in_specs, out_specs=pl.BlockSpec(tile, at),
        out_shape=jax.ShapeDtypeStruct((a, b), F32), compiler_params=_params(("parallel",)),
    )(*args)


class _GradExchange:
    EARLY = ("w_in", "w_branch_a", "w_branch_b", "w_out")
    LATE = ("w_uq", "w_ukv")

    def start_early(self, g):
        full = jnp.concatenate([jnp.zeros((QKV_ROWS, D), F32), g["w_in_rest"]], axis=0)
        g = dict(g, w_in=full)
        self.early = [_split_by_chip(n, g[n]).astype(BF16) for n in self.EARLY]
        self.early_started, token = _exchange_start(self.early, (), "grads_early_start")
        return token

    def start_late(self, g):
        self.early_landed = _exchange_wait(self.early, (), self.early_started, g["w_uq"], "grads_early_wait")
        self.late = [_split_by_chip(n, g[n]).astype(BF16) for n in self.LATE] + [g["w_in_qkv"].astype(BF16)]
        self.late_started, token = _exchange_start(self.late, (2,), "grads_late_start")
        return token

    def finish(self, after):
        late_landed = _exchange_wait(self.late, (2,), self.late_started, after, "grads_late_wait")
        me = 2 * lax.axis_index("x") + lax.axis_index("y")
        own = lambda slabs: lax.dynamic_index_in_dim(slabs, me, axis=0, keepdims=False)
        sums = {}
        for n, slabs, land in zip(self.EARLY, self.early, self.early_landed):
            if n == "w_in":
                sums[n] = _sum_landed(land, own(slabs), "sum_" + n, first_land=late_landed[2], first_own=self.late[2])
            else:
                sums[n] = _sum_landed(land, own(slabs), "sum_" + n)
        for n, slabs, land in zip(self.LATE, self.late, late_landed):
            sums[n] = _sum_landed(land, own(slabs), "sum_" + n)
        return sums


def _adamw_math(g, w, m, v):
    nm = ADAM_B1 * m + (1.0 - ADAM_B1) * g
    nv = ADAM_B2 * v + (1.0 - ADAM_B2) * (g * g)
    m_hat = nm / (1.0 - ADAM_B1 ** ADAM_STEP)
    v_hat = nv / (1.0 - ADAM_B2 ** ADAM_STEP)
    return -ADAM_LR * (m_hat / (jnp.sqrt(v_hat) + ADAM_EPS) + ADAM_WD * w), nm, nv


def _adamw(p_mine, p_sibling, w, m, v, name):
    a, b = p_mine.shape
    steps, tile, at = _block_tiling(a, b)

    def body(a_ref, b_ref, w_ref, m_ref, v_ref, g_ref, d_ref, nm_ref, nv_ref):
        g = a_ref[...] + b_ref[...]
        g_ref[...] = g
        d_ref[...], nm_ref[...], nv_ref[...] = _adamw_math(g, w_ref[...], m_ref[...], v_ref[...])

    spec = pl.BlockSpec(tile, at)
    sds = jax.ShapeDtypeStruct((a, b), F32)
    return pl.pallas_call(
        body, name=name, grid=(steps,), in_specs=[spec] * 5, out_specs=[spec] * 4, out_shape=[sds] * 4,
        compiler_params=_params(("parallel",)),
    )(p_mine, p_sibling, w, m, v)


def _vec_pack(vg):
    names = [n for n, _, _ in VEC_ROWS]

    def body(*refs):
        o_ref = refs[-1]
        lb_ref = refs[len(names)]
        o_ref[...] = jnp.zeros_like(o_ref)
        for (name, row, size), ref in zip(VEC_ROWS, refs):
            if name == "g_hgrn":
                r = lax.broadcasted_iota(jnp.int32, (NH * V_DIM, LANE), 0)
                c = lax.broadcasted_iota(jnp.int32, (NH * V_DIM, LANE), 1)
                fold = ((r % V_DIM) == c).astype(F32)
                o_ref[row:row + 1, 0:LANE] = jnp.dot(ref[...], fold, precision=HIGHEST, preferred_element_type=F32)
            else:
                o_ref[row:row + 1, 0:size] = ref[...]
        o_ref[VEC_LB_ROW:VEC_LB_ROW + 2, 0:512] = lb_ref[...]

    return pl.pallas_call(body, name="vec_pack", out_shape=jax.ShapeDtypeStruct(VEC_SHAPE, F32))(
        *[vg[n] for n in names], vg["lb_logits"])


def _adamw_vec(p_mine, p_sibling, w, m, v):
    names = [n for n, _, _ in VEC_ROWS] + ["lb_logits"]
    k = len(names)

    def body(a_ref, b_ref, *refs):
        ins, outs = refs[:3 * k], refs[3 * k:]
        for i, name in enumerate(names):
            if name == "lb_logits":
                rows, cols = slice(VEC_LB_ROW, VEC_LB_ROW + 2), slice(0, 512)
            else:
                _, row, size = VEC_ROWS[i]
                rows, cols = slice(row, row + 1), slice(0, size)
            g = a_ref[rows, cols] + b_ref[rows, cols]
            d, nm, nv = _adamw_math(g, ins[i][...], ins[k + i][...], ins[2 * k + i][...])
            for o_ref, val in zip(outs[4 * i:4 * i + 4], (g, d, nm, nv)):
                o_ref[...] = val

    shapes = [jax.ShapeDtypeStruct(w[n].shape, F32) for n in names for _ in range(4)]
    res = pl.pallas_call(body, name="adamw_vec", out_shape=shapes)(
        p_mine, p_sibling, *[w[n] for n in names], *[m[n] for n in names], *[v[n] for n in names])
    return [{n: res[4 * i + j] for i, n in enumerate(names)} for j in range(4)]


WEIGHTS = ("g_pre", "w_in", "b_gate", "g_q", "w_uq", "g_kv", "w_ukv", "lb_logits", "g_hgrn", "w_branch_a", "w_branch_b", "w_out", "g_post")


def kernel(x, g_pre, w_in, b_gate, g_q, w_uq, g_kv, w_ukv, lb_logits, g_hgrn, w_branch_a, w_branch_b, w_out, g_post, loss_target, m_g_pre, m_w_in, m_b_gate, m_g_q, m_w_uq, m_g_kv, m_w_ukv, m_lb_logits, m_g_hgrn, m_w_branch_a, m_w_branch_b, m_w_out, m_g_post, v_g_pre, v_w_in, v_b_gate, v_g_q, v_w_uq, v_g_kv, v_w_ukv, v_lb_logits, v_g_hgrn, v_w_branch_a, v_w_branch_b, v_w_out, v_g_post):
    w = dict(g_pre=g_pre, w_in=w_in, b_gate=b_gate, g_q=g_q, w_uq=w_uq, g_kv=g_kv, w_ukv=w_ukv, lb_logits=lb_logits, g_hgrn=g_hgrn,
             w_branch_a=w_branch_a, w_branch_b=w_branch_b, w_out=w_out, g_post=g_post)
    m = dict(g_pre=m_g_pre, w_in=m_w_in, b_gate=m_b_gate, g_q=m_g_q, w_uq=m_w_uq, g_kv=m_g_kv, w_ukv=m_w_ukv, lb_logits=m_lb_logits,
             g_hgrn=m_g_hgrn, w_branch_a=m_w_branch_a, w_branch_b=m_w_branch_b, w_out=m_w_out, g_post=m_g_post)
    v = dict(g_pre=v_g_pre, w_in=v_w_in, b_gate=v_b_gate, g_q=v_g_q, w_uq=v_w_uq, g_kv=v_g_kv, w_ukv=v_w_ukv, lb_logits=v_lb_logits,
             g_hgrn=v_g_hgrn, w_branch_a=v_w_branch_a, w_branch_b=v_w_branch_b, w_out=v_w_out, g_post=v_g_post)
    gathered = _gather_weights([_to_block(n, w[n]).astype(BF16) for n in BIG], [GATHER_SPLIT_AXIS[n] for n in BIG])
    big = {n: _join_chips(n, g) for n, g in zip(BIG, gathered)}
    exchange = _GradExchange()
    loss, grad_x, _, vec_grads = _local_step(
        x[0], loss_target[0], g_pre, big["w_in"], b_gate, g_q, big["w_uq"], g_kv, big["w_ukv"], lb_logits, g_hgrn,
        big["w_branch_a"], big["w_branch_b"], big["w_out"], g_post, exchange)
    sums = exchange.finish(grad_x)
    (vec_landed,) = _chip_exchange([_vec_pack(vec_grads)], "scatter_vec")
    mine = [sums[n] for n in BIG] + [_sum_chips(vec_landed, "sum_vec")]
    theirs = _sibling_exchange(mine, "sibling_grads")
    outs = [{}, {}, {}, {}]
    for k, n in enumerate(BIG):
        blocks = [_to_block(n, t[n]) for t in (w, m, v)]
        for o, val in zip(outs, _adamw(mine[k], theirs[k], *blocks, "adamw_" + n)):
            o[n] = _from_block(n, val)
    for o, vals in zip(outs, _adamw_vec(mine[-1], theirs[-1], w, m, v)):
        o.update(vals)

    total = lax.psum(loss[0, 0], ("x", "y", "c"))
    return (total, grad_x[None], *[o[n] for o in outs for n in WEIGHTS])
```

```python
import functools
import math

import numpy as np
import jax
import jax.numpy as jnp
from jax import lax
from jax.experimental import pallas as pl
from jax.experimental.pallas import tpu as pltpu

F32 = jnp.float32
BF16 = jnp.bfloat16
HIGHEST = lax.Precision.HIGHEST

D = 1024
NH = 8
QK_NOPE, QK_ROPE, V_DIM = 64, 32, 64
Q_LORA, KV_LORA = 768, 256
CHUNK = 64
HG_BLOCK = 32
EPS = 1e-6
D_IN = 5664
LANE = 128
P_MERGE, P_GA, P_HQ, P_HF, P_HI, P_GB, P_CQ, P_CKV, P_KPE = 0, 2048, 2560, 3072, 3584, 4096, 4608, 5376, 5632
D_P = 5760
O_CQ, O_CKV, O_KPE, O_GA, O_HQ, O_HF, O_HI, O_GB, O_MERGE = 0, 768, 1024, 1056, 1568, 2080, 2592, 3104, 3616

TM = 256
TQ = 512
TH = 256
HG_PAIRS = 2
VMEM_LIMIT = 56 * 1024 * 1024

ADAM_LR, ADAM_B1, ADAM_B2, ADAM_EPS, ADAM_WD, ADAM_STEP = 0.001, 0.9, 0.999, 1e-08, 0.01, 10

NT_DIMS = (((1,), (1,)), ((), ()))
TN_DIMS = (((0,), (0,)), ((), ()))


def _params(sem):
    return pltpu.CompilerParams(dimension_semantics=sem, vmem_limit_bytes=VMEM_LIMIT)


def _mm(a, b):
    return jnp.dot(a, b, preferred_element_type=F32)


def _mm_nt(a, b):
    return lax.dot_general(a, b, NT_DIMS, preferred_element_type=F32)


def _mm_tn(a, b):
    return lax.dot_general(a, b, TN_DIMS, preferred_element_type=F32)


def _sigmoid(z):
    return jax.nn.sigmoid(z)


def _rope(v, c, s1, s2):
    return v * c + pltpu.roll(v, 112, 1) * s1 + pltpu.roll(v, 16, 1) * s2


def _rope_t(dy, c, s1, s2):
    return dy * c + pltpu.roll(dy * s1, 16, 1) + pltpu.roll(dy * s2, 112, 1)


def _rope_tables(s):
    inv = 10000.0 ** (-jnp.arange(0, QK_ROPE, 2, dtype=F32) / QK_ROPE)
    ang = jnp.arange(s, dtype=F32)[:, None] * inv[None, :]
    cos, sin = jnp.cos(ang), jnp.sin(ang)
    z64, z32, o64, o32 = jnp.zeros((s, 64), F32), jnp.zeros((s, 32), F32), jnp.ones((s, 64), F32), jnp.ones((s, 32), F32)
    z16 = jnp.zeros((s, 16), F32)
    c = jnp.concatenate([o64, cos, cos, o32], axis=1)
    s1 = jnp.concatenate([z64, -sin, z16, z32], axis=1)
    s2 = jnp.concatenate([z64, z16, sin, z32], axis=1)
    return c, s1, s2


def _front_fwd(x, g_pre, w_in_pt, tokens=()):
    s = x.shape[0]
    tokens = list(tokens)

    def body(x_ref, g_ref, w_ref, *refs):
        o_ref, h_ref = refs[len(tokens):]
        xv = x_ref[...]
        r = lax.rsqrt(jnp.mean(xv * xv, axis=-1, keepdims=True) + EPS)
        h = ((xv * r) * g_ref[...]).astype(BF16)
        h_ref[...] = h
        o_ref[...] = _mm_nt(h, w_ref[...])

    return pl.pallas_call(
        body, name="front_fwd", grid=(s // TM,),
        in_specs=[pl.BlockSpec((TM, D), lambda i: (i, 0)), pl.BlockSpec((1, D), lambda i: (0, 0)),
                  pl.BlockSpec((D_P, D), lambda i: (0, 0))] + [pl.BlockSpec((8, LANE), lambda i: (0, 0))] * len(tokens),
        out_specs=[pl.BlockSpec((TM, D_P), lambda i: (i, 0)), pl.BlockSpec((TM, D), lambda i: (i, 0))],
        out_shape=[jax.ShapeDtypeStruct((s, D_P), F32), jax.ShapeDtypeStruct((s, D), BF16)],
        compiler_params=_params(("parallel",)),
    )(x, g_pre, w_in_pt, *tokens)


def _norm_rows(v, g):
    r = lax.rsqrt(jnp.mean(v * v, axis=-1, keepdims=True) + EPS)
    return (v * r) * g, r


def _qkv_fwd(proj, g_q, g_kv, w_uq_p, w_k_p, w_v_p, rc, rs1, rs2):
    s = proj.shape[0]

    def body(cq_ref, ckv_ref, kpe_ref, gq_ref, gkv_ref, wq_ref, wk_ref, wv_ref, c_ref, s1_ref, s2_ref, q_ref, k_ref, v_ref):
        c, s1, s2 = c_ref[...], s1_ref[...], s2_ref[...]
        cqn, _ = _norm_rows(cq_ref[...], gq_ref[...])
        ckvn, _ = _norm_rows(ckv_ref[...], gkv_ref[...])
        ckvn = ckvn.astype(BF16)
        qf = _mm(cqn.astype(BF16), wq_ref[...])
        kf = _mm(ckvn, wk_ref[...])
        vf = _mm(ckvn, wv_ref[...])
        kpe = _rope(kpe_ref[...], c, s1, s2)
        for h in range(NH):
            blk = slice(h * LANE, (h + 1) * LANE)
            q_ref[h] = _rope(qf[:, blk], c, s1, s2).astype(BF16)
            k_ref[h] = (kf[:, blk] + kpe).astype(BF16)
            v_ref[h] = vf[:, blk].astype(BF16)

    row = lambda w, j: pl.BlockSpec((TM, w), lambda i: (i, j))
    full = lambda a: pl.BlockSpec(a.shape, lambda i: (0,) * a.ndim)
    hs = jax.ShapeDtypeStruct((NH, s, LANE), BF16)
    return pl.pallas_call(
        body, name="qkv_fwd", grid=(s // TM,),
        in_specs=[row(Q_LORA, P_CQ // Q_LORA), row(KV_LORA, P_CKV // KV_LORA), row(LANE, P_KPE // LANE),
                  full(g_q), full(g_kv), full(w_uq_p), full(w_k_p), full(w_v_p), row(LANE, 0), row(LANE, 0), row(LANE, 0)],
        out_specs=[pl.BlockSpec((NH, TM, LANE), lambda i: (0, i, 0))] * 3,
        out_shape=[hs, hs, hs],
        compiler_params=_params(("parallel",)),
    )(proj, proj, proj, g_q, g_kv, w_uq_p, w_k_p, w_v_p, rc, rs1, rs2)


LOG2E = 1.4426950408889634
QK_SCALE2 = LOG2E / math.sqrt(QK_NOPE + QK_ROPE)


def _diag_visible():
    row = lax.broadcasted_iota(jnp.int32, (TQ, TQ), 0)
    col = lax.broadcasted_iota(jnp.int32, (TQ, TQ), 1)
    return (col // CHUNK) <= (row // CHUNK)


def _attn_fwd(q, k, vv):
    s = q.shape[1]

    def body(q_ref, k_ref, v_ref, o_ref, lse_ref):
        i = pl.program_id(1)
        qs = (q_ref[0], q_ref[1])

        def tile(hh, t, carry, diag):
            m, l, acc = carry
            rows = pl.ds(pl.multiple_of(t * TQ, TQ), TQ)
            sc = _mm_nt(qs[hh], k_ref[hh, rows, :])
            if diag:
                sc = jnp.where(_diag_visible(), sc, -jnp.inf)
            m_new = jnp.maximum(m, jnp.max(sc, axis=-1, keepdims=True))
            alpha = jnp.exp2((m - m_new) * QK_SCALE2)
            p = jnp.exp2((sc - m_new) * QK_SCALE2)
            l = alpha * l + jnp.sum(p, axis=-1, keepdims=True)
            acc = alpha * acc + _mm(p.astype(BF16), v_ref[hh, rows, :])
            return m_new, l, acc

        def step(t, carry):
            return tile(0, t, carry[0], False), tile(1, t, carry[1], False)

        init = (jnp.full((TQ, 1), -jnp.inf, F32), jnp.zeros((TQ, 1), F32), jnp.zeros((TQ, LANE), F32))
        carry = lax.fori_loop(0, i, step, (init, init))
        out = jnp.zeros((TQ, LANE), F32)
        for hh in range(2):
            m, l, acc = tile(hh, i, carry[hh], True)
            out = out + acc / l
            lse_ref[hh] = jnp.broadcast_to(m * QK_SCALE2 + jnp.log(l) * LOG2E, (TQ, LANE))
        o_ref[...] = out

    return pl.pallas_call(
        body, name="attn_fwd", grid=(NH // 2, s // TQ),
        in_specs=[pl.BlockSpec((2, TQ, LANE), lambda p, i: (p, i, 0)), pl.BlockSpec((2, s, LANE), lambda p, i: (p, 0, 0)),
                  pl.BlockSpec((2, s, LANE), lambda p, i: (p, 0, 0))],
        out_specs=[pl.BlockSpec((TQ, LANE), lambda p, i: (i, p)), pl.BlockSpec((2, TQ, LANE), lambda p, i: (p, i, 0))],
        out_shape=[jax.ShapeDtypeStruct((s, NH * V_DIM), F32), jax.ShapeDtypeStruct((NH, s, LANE), F32)],
        compiler_params=_params(("parallel", "parallel")),
    )(q, k, vv)


def _lower_bound(lbl):
    a0, a1 = lbl[0:1, :], lbl[1:2, :]
    mx = jnp.maximum(a0, a1)
    e0, e1 = jnp.exp(a0 - mx), jnp.exp(a1 - mx)
    return e0 / (e0 + e1)


def _chunk_cumsum(v, reverse=False):
    pos = lax.broadcasted_iota(jnp.int32, v.shape, 0) % HG_BLOCK
    s = 1
    while s < HG_BLOCK:
        if reverse:
            v = v + jnp.where(pos < HG_BLOCK - s, pltpu.roll(v, TH - s, 0), 0.0)
        else:
            v = v + jnp.where(pos >= s, pltpu.roll(v, s, 0), 0.0)
        s *= 2
    return v


def _hgrn_gates(hq, hf, lb):
    sig = _sigmoid(hf)
    f = lb + (1.0 - lb) * sig
    g = jnp.log(f)
    kk = 1.0 - f
    r = lax.broadcasted_iota(jnp.int32, (TH, TH), 0)
    c = lax.broadcasted_iota(jnp.int32, (TH, TH), 1)
    tri = ((r // HG_BLOCK) == (c // HG_BLOCK)) & (r >= c)
    cum = _chunk_cumsum(g)
    nch = TH // HG_BLOCK
    total = _chunks(cum)[:, HG_BLOCK - 1:HG_BLOCK, :]
    lastb = jnp.broadcast_to(total, (nch, HG_BLOCK, LANE)).reshape(TH, LANE)
    e, ei, ee = jnp.exp(cum), jnp.exp(-cum), jnp.exp(lastb - cum)
    return dict(sig=sig, f=f, kk=kk, tri=tri, cum=cum, total=total, e=e, ei=ei, ee=ee, qd=hq * e, ki=kk * ei, ke=kk * ee)


def _chunks(v):
    return v.reshape(TH // HG_BLOCK, HG_BLOCK, v.shape[-1])


def _bmm_nt(a, b):
    return lax.dot_general(a, b, (((2,), (2,)), ((0,), (0,))), preferred_element_type=F32)


def _bmm_nn(a, b):
    return lax.dot_general(a, b, (((2,), (1,)), ((0,), (0,))), preferred_element_type=F32)


def _bmm_tn(a, b):
    return lax.dot_general(a, b, (((1,), (1,)), ((0,), (0,))), preferred_element_type=F32)


def _pair_masks():
    lane = lax.broadcasted_iota(jnp.int32, (TH, LANE), 1)
    kr = lax.broadcasted_iota(jnp.int32, (LANE, LANE), 0)
    kc = lax.broadcasted_iota(jnp.int32, (LANE, LANE), 1)
    return lane < 64, (kr // 64) == (kc // 64)


def _hgrn_fwd(proj, lbl):
    s = proj.shape[0]
    nch = TH // HG_BLOCK

    def body(hq_ref, hf_ref, hi_ref, lbl_ref, o_ref, st_ref, st):
        @pl.when(pl.program_id(1) == 0)
        def _():
            st[...] = jnp.zeros_like(st)

        m0, bd = _pair_masks()
        for u in range(HG_PAIRS):
            lanes = slice(u * LANE, (u + 1) * LANE)
            lb = _lower_bound(lbl_ref[:, lanes])
            gt = _hgrn_gates(hq_ref[:, lanes], hf_ref[:, lanes], lb)
            v_b = hi_ref[:, lanes].astype(BF16)
            qd, ki_b, ke_b = gt["qd"], gt["ki"].astype(BF16), gt["ke"].astype(BF16)
            qd_b = qd.astype(BF16)
            o = jnp.zeros((TH, LANE), F32)
            for hh in range(2):
                mh = m0 if hh == 0 else jnp.logical_not(m0)
                a = jnp.where(gt["tri"], _mm_nt(jnp.where(mh, qd, 0.0).astype(BF16), ki_b), 0.0)
                o = jnp.where(mh, _mm(a.astype(BF16), v_b), o)
            upd = _bmm_tn(_chunks(v_b), _chunks(ke_b))
            decay = jnp.exp(gt["total"])
            cur, entering = st[u], []
            for n in range(nch):
                entering.append(cur)
                cur = decay[n] * cur + jnp.where(bd, upd[n], 0.0)
            st[u] = cur
            entering = jnp.stack(entering)
            st_ref[u] = entering
            o_ref[:, lanes] = o + _bmm_nt(_chunks(qd_b), entering.astype(BF16)).reshape(TH, LANE)

    wide = HG_PAIRS * LANE
    col = lambda base: pl.BlockSpec((TH, wide), lambda p, i: (i, base // wide + p))
    return pl.pallas_call(
        body, name="hgrn_fwd", grid=(NH // 2 // HG_PAIRS, s // TH),
        in_specs=[col(P_HQ), col(P_HF), col(P_HI), pl.BlockSpec((2, wide), lambda p, i: (0, p))],
        out_specs=[pl.BlockSpec((TH, wide), lambda p, i: (i, p)),
                   pl.BlockSpec((HG_PAIRS, nch, LANE, LANE), lambda p, i: (p, i, 0, 0))],
        out_shape=[jax.ShapeDtypeStruct((s, 512), F32), jax.ShapeDtypeStruct((NH // 2, s // HG_BLOCK, LANE, LANE), F32)],
        scratch_shapes=[pltpu.VMEM((HG_PAIRS, LANE, LANE), F32)],
        compiler_params=_params(("parallel", "arbitrary")),
    )(proj, proj, proj, lbl)


def _group_sum(v):
    low = lax.broadcasted_iota(jnp.int32, (v.shape[0], LANE), 1) < V_DIM
    blocks = []
    for b in range(v.shape[1] // LANE):
        blk = v[:, b * LANE:(b + 1) * LANE]
        s_low = jnp.sum(jnp.where(low, blk, 0.0), axis=-1, keepdims=True)
        s_high = jnp.sum(jnp.where(low, 0.0, blk), axis=-1, keepdims=True)
        blocks.append(jnp.where(low, s_low, s_high))
    return jnp.concatenate(blocks, axis=1)


def _dsilu(z, sg):
    return sg * (1.0 + z * (1.0 - sg))


def _mid(proj, attn, o_raw, x, tgt, g_hg, b_gate, g_post, wa, wb, w_out):
    s = x.shape[0]

    def body(attn_ref, ga_ref, o_ref, gb_ref, mg_ref, x_ref, t_ref, ghg_ref, bg_ref, gp_ref, wa_ref, wb_ref, wo_ref,
             loss_ref, dout_ref, dattn_ref, dga_ref, dor_ref, dgb_ref, dmg_ref, dwo_ref, dwa_ref, dwb_ref, dgp_ref, dbg_ref, dghg_ref):
        first = pl.program_id(0) == 0

        @pl.when(first)
        def _():
            for rf in (loss_ref, dwo_ref, dwa_ref, dwb_ref, dgp_ref, dbg_ref, dghg_ref):
                rf[...] = jnp.zeros_like(rf)

        attn, za, orw, zb = attn_ref[...], ga_ref[...], o_ref[...], gb_ref[...]
        ghg, gp = ghg_ref[...], gp_ref[...]
        sga, sgb = _sigmoid(za), _sigmoid(zb)
        sa, sb = za * sga, zb * sgb
        ga = attn * sa
        rh = lax.rsqrt(_group_sum(orw * orw) * (1.0 / V_DIM) + EPS)
        on = (orw * rh) * ghg
        gb = on * sb
        ga_b, gb_b = ga.astype(BF16), gb.astype(BF16)
        ya = _mm(ga_b, wa_ref[...])
        yb = _mm(gb_b, wb_ref[...])
        gates = _sigmoid(mg_ref[...] + bg_ref[...])
        g0, g1 = gates[:, :D], gates[:, D:]
        m_b = (g0 * ya + g1 * yb).astype(BF16)
        y = _mm(m_b, wo_ref[...])
        ry = lax.rsqrt(jnp.mean(y * y, axis=-1, keepdims=True) + EPS)
        out = x_ref[...] + (y * ry) * gp
        err = out - t_ref[...]
        loss_ref[...] += 0.5 * jnp.sum(jnp.mean(err * err, axis=-1, keepdims=True), axis=0, keepdims=True)
        dout = err * (1.0 / D)
        dout_ref[...] = dout
        dgp_ref[...] += jnp.sum(dout * (y * ry), axis=0, keepdims=True)
        dgy = dout * gp
        dy = ry * dgy - y * (ry * ry * ry) * jnp.mean(y * dgy, axis=-1, keepdims=True)
        dy_b = dy.astype(BF16)
        dwo_ref[...] += _mm_tn(m_b, dy_b)
        dm = _mm_nt(dy_b, wo_ref[...])
        dya, dyb = dm * g0, dm * g1
        dg0, dg1 = dm * ya, dm * yb
        dmg = jnp.concatenate([dg0 * g0 * (1.0 - g0), dg1 * g1 * (1.0 - g1)], axis=1)
        dmg_ref[...] = dmg.astype(BF16)
        dbg_ref[...] += jnp.sum(dmg, axis=0, keepdims=True)
        dya_b, dyb_b = dya.astype(BF16), dyb.astype(BF16)
        dwa_ref[...] += _mm_tn(ga_b, dya_b)
        dwb_ref[...] += _mm_tn(gb_b, dyb_b)
        dga = _mm_nt(dya_b, wa_ref[...])
        dgb = _mm_nt(dyb_b, wb_ref[...])
        dattn_ref[...] = dga * sa
        dga_ref[...] = (dga * attn * _dsilu(za, sga)).astype(BF16)
        dgb_ref[...] = (dgb * on * _dsilu(zb, sgb)).astype(BF16)
        don = dgb * sb
        dghg_ref[...] += jnp.sum(don * (orw * rh), axis=0, keepdims=True)
        dgo = don * ghg
        dor_ref[...] = rh * dgo - orw * (rh * rh * rh) * (_group_sum(orw * dgo) * (1.0 / V_DIM))

    row = lambda w, j=0: pl.BlockSpec((TM, w), lambda i: (i, j))
    full = lambda a: pl.BlockSpec(a.shape, lambda i: (0,) * a.ndim)
    acc = lambda shape: pl.BlockSpec(shape, lambda i: (0, 0))
    sds = jax.ShapeDtypeStruct
    return pl.pallas_call(
        body, name="mid", grid=(s // TM,),
        in_specs=[row(512), row(512, P_GA // 512), row(512), row(512, P_GB // 512), row(2048, P_MERGE // 2048), row(D), row(D),
                  full(g_hg), full(b_gate), full(g_post), full(wa), full(wb), full(w_out)],
        out_specs=[acc((1, 1)), row(D), row(512), row(512), row(512), row(512), row(2048),
                   acc((D, D)), acc((512, D)), acc((512, D)), acc((1, D)), acc((1, 2048)), acc((1, 512))],
        out_shape=[sds((1, 1), F32), sds((s, D), F32), sds((s, 512), F32), sds((s, 512), BF16), sds((s, 512), F32), sds((s, 512), BF16),
                   sds((s, 2048), BF16), sds((D, D), F32), sds((512, D), F32), sds((512, D), F32), sds((1, D), F32),
                   sds((1, 2048), F32), sds((1, 512), F32)],
        compiler_params=_params(("arbitrary",)),
    )(attn, proj, o_raw, proj, proj, x, tgt, g_hg, b_gate, g_post, wa, wb, w_out)


def _attn_bwd(q, k, vv, attn, dattn, lse, token):
    s = q.shape[1]
    nt = s // TQ
    scale = 1.0 / math.sqrt(QK_NOPE + QK_ROPE)

    def body(q_ref, k_ref, v_ref, o_ref, do_ref, lse_ref, token_ref, dq_ref, dk_ref, dv_ref, do_s, delta_s):
        j = pl.program_id(1)

        @pl.when(j == 0)
        def _():
            dq_ref[...] = jnp.zeros_like(dq_ref)
            lane = lax.broadcasted_iota(jnp.int32, (TQ, LANE), 1)

            @pl.loop(0, nt)
            def _(i):
                rows = pl.ds(pl.multiple_of(i * TQ, TQ), TQ)
                do, o = do_ref[rows, :], o_ref[rows, :]
                for hh in range(2):
                    doh = jnp.where((lane < 64) if hh == 0 else (lane >= 64), do, 0.0)
                    do_s[hh, rows, :] = doh.astype(BF16)
                    delta_s[hh, rows, :] = jnp.broadcast_to(jnp.sum(doh * o, axis=-1, keepdims=True), (TQ, LANE))

        kjs, vjs = (k_ref[0], k_ref[1]), (v_ref[0], v_ref[1])
        wide = lambda a: jnp.concatenate([a] * (TQ // LANE), axis=1)

        def tile(hh, i, carry, diag):
            dk, dv = carry
            rows = pl.ds(pl.multiple_of(i * TQ, TQ), TQ)
            qi, do_b = q_ref[hh, rows, :], do_s[hh, rows, :]
            p = jnp.exp2(_mm_nt(qi, kjs[hh]) * QK_SCALE2 - wide(lse_ref[hh, rows, :]))
            if diag:
                p = jnp.where(_diag_visible(), p, 0.0)
            dv = dv + _mm_tn(p.astype(BF16), do_b)
            ds_b = (p * (_mm_nt(do_b, vjs[hh]) - wide(delta_s[hh, rows, :]))).astype(BF16)
            dk = dk + _mm_tn(ds_b, qi)
            dq_ref[hh, rows, :] += _mm(ds_b, kjs[hh])
            return dk, dv

        def step(i, carry):
            return tile(0, i, carry[0], False), tile(1, i, carry[1], False)

        z = jnp.zeros((TQ, LANE), F32)
        first = (tile(0, j, (z, z), True), tile(1, j, (z, z), True))
        carry = lax.fori_loop(j + 1, nt, step, first)
        for hh in range(2):
            dk_ref[hh] = carry[hh][0] * scale
            dv_ref[hh] = carry[hh][1]

        @pl.when(j == nt - 1)
        def _():
            dq_ref[...] = dq_ref[...] * scale

    whole = pl.BlockSpec((2, s, LANE), lambda p, j: (p, 0, 0))
    tile_spec = pl.BlockSpec((2, TQ, LANE), lambda p, j: (p, j, 0))
    cols = pl.BlockSpec((s, LANE), lambda p, j: (0, p))
    hs = jax.ShapeDtypeStruct((NH, s, LANE), F32)
    return pl.pallas_call(
        body, name="attn_bwd", grid=(NH // 2, nt),
        in_specs=[whole, tile_spec, tile_spec, cols, cols, whole, pl.BlockSpec((8, LANE), lambda p, j: (0, 0))],
        out_specs=[whole, tile_spec, tile_spec],
        out_shape=[hs, hs, hs],
        scratch_shapes=[pltpu.VMEM((2, s, LANE), BF16), pltpu.VMEM((2, s, LANE), F32)],
        compiler_params=_params(("parallel", "arbitrary")),
    )(q, k, vv, attn, dattn, lse, token)


def _hgrn_bwd(proj, lbl, states, do_raw):
    s = proj.shape[0]
    nt = s // TH
    nch = TH // HG_BLOCK

    def body(hq_ref, hf_ref, hi_ref, lbl_ref, st_ref, do_ref, dh_ref, dlbl_ref, dst, dlb):
        step = pl.program_id(1)

        @pl.when(step == 0)
        def _():
            dst[...] = jnp.zeros_like(dst)
            dlb[...] = jnp.zeros_like(dlb)

        m0, bd = _pair_masks()
        for u in range(HG_PAIRS):
            lanes = slice(u * LANE, (u + 1) * LANE)
            lb = _lower_bound(lbl_ref[:, lanes])
            gt = _hgrn_gates(hq_ref[:, lanes], hf_ref[:, lanes], lb)
            do = do_ref[:, lanes]
            qd, ki, ke = gt["qd"], gt["ki"], gt["ke"]
            v_b, do_b = hi_ref[:, lanes].astype(BF16), do.astype(BF16)
            qd_b, ki_b, ke_b = qd.astype(BF16), ki.astype(BF16), ke.astype(BF16)
            dv = jnp.zeros((TH, LANE), F32)
            dqd = jnp.zeros((TH, LANE), F32)
            dki = jnp.zeros((TH, LANE), F32)
            for hh in range(2):
                mh = m0 if hh == 0 else jnp.logical_not(m0)
                a_b = jnp.where(gt["tri"], _mm_nt(jnp.where(mh, qd, 0.0).astype(BF16), ki_b), 0.0).astype(BF16)
                doh_b = jnp.where(mh, do, 0.0).astype(BF16)
                da_b = jnp.where(gt["tri"], _mm_nt(doh_b, v_b), 0.0).astype(BF16)
                dv = dv + _mm_tn(a_b, doh_b)
                dqd = jnp.where(mh, _mm(da_b, ki_b), dqd)
                dki = jnp.where(mh, _mm_tn(da_b, qd_b), dki)
            fed = _bmm_tn(_chunks(do_b), _chunks(qd_b))
            decay = jnp.exp(gt["total"])
            ds, leaving = dst[u], [None] * nch
            for n in reversed(range(nch)):
                leaving[n] = ds
                ds = decay[n] * ds + jnp.where(bd, fed[n], 0.0)
            dst[u] = ds
            leaving = jnp.stack(leaving)
            entering = st_ref[u]
            leaving_b = leaving.astype(BF16)
            dke3 = _bmm_nn(_chunks(v_b), leaving_b)
            dv = dv + _bmm_nt(_chunks(ke_b), leaving_b).reshape(TH, LANE)
            dqd = dqd + _bmm_nn(_chunks(do_b), entering.astype(BF16)).reshape(TH, LANE)
            dke = dke3.reshape(TH, LANE)
            dlast = (jnp.sum(dke3 * _chunks(ke), axis=1, keepdims=True)
                     + jnp.sum(leaving * entering, axis=1, keepdims=True) * decay)
            dk = dki * gt["ei"] + dke * gt["ee"]
            dcum = dqd * qd - dki * ki - dke * ke
            dg = _chunk_cumsum(dcum, reverse=True) + jnp.broadcast_to(dlast, (nch, HG_BLOCK, LANE)).reshape(TH, LANE)
            sig = gt["sig"]
            df = dg / gt["f"] - dk
            dlb[:, lanes] += jnp.sum(df * (1.0 - sig), axis=0, keepdims=True)
            dh_ref[0, :, lanes] = (dqd * gt["e"]).astype(BF16)
            dh_ref[1, :, lanes] = ((df * (1.0 - lb)) * sig * (1.0 - sig)).astype(BF16)
            dh_ref[2, :, lanes] = dv.astype(BF16)

        @pl.when(step == nt - 1)
        def _():
            lb = _lower_bound(lbl_ref[...])
            da0 = dlb[...] * lb * (1.0 - lb)
            dlbl_ref[...] = jnp.concatenate([da0, -da0], axis=0)

    wide = HG_PAIRS * LANE
    col = lambda base: pl.BlockSpec((TH, wide), lambda p, i: (nt - 1 - i, base // wide + p))
    tile = pl.BlockSpec((TH, wide), lambda p, i: (nt - 1 - i, p))
    sds = jax.ShapeDtypeStruct
    return pl.pallas_call(
        body, name="hgrn_bwd", grid=(NH // 2 // HG_PAIRS, nt),
        in_specs=[col(P_HQ), col(P_HF), col(P_HI), pl.BlockSpec((2, wide), lambda p, i: (0, p)),
                  pl.BlockSpec((HG_PAIRS, nch, LANE, LANE), lambda p, i: (p, nt - 1 - i, 0, 0)), tile],
        out_specs=[pl.BlockSpec((3, TH, wide), lambda p, i: (0, nt - 1 - i, p)), pl.BlockSpec((2, wide), lambda p, i: (0, p))],
        out_shape=[sds((3, s, 512), BF16), sds((2, 512), F32)],
        scratch_shapes=[pltpu.VMEM((HG_PAIRS, LANE, LANE), F32), pltpu.VMEM((1, wide), F32)],
        compiler_params=_params(("parallel", "arbitrary")),
    )(proj, proj, proj, lbl, states, do_raw)


def _norm_rows_bwd(v, r, g, dn):
    dgv = dn * g
    return r * dgv - v * (r * r * r) * jnp.mean(v * dgv, axis=-1, keepdims=True)


def _qkv_bwd(proj, dq, dk, dvv, g_q, g_kv, w_uq_p, w_k_p, w_v_p, rc, rs1, rs2):
    s = proj.shape[0]

    def body(cq_ref, ckv_ref, dq_ref, dk_ref, dv_ref, gq_ref, gkv_ref, wq_ref, wk_ref, wv_ref, c_ref, s1_ref, s2_ref,
             dcq_ref, dckv_ref, dkpe_ref, dwq_ref, dwk_ref, dwv_ref, dgq_ref, dgkv_ref):
        @pl.when(pl.program_id(0) == 0)
        def _():
            for rf in (dwq_ref, dwk_ref, dwv_ref, dgq_ref, dgkv_ref):
                rf[...] = jnp.zeros_like(rf)

        c, s1, s2 = c_ref[...], s1_ref[...], s2_ref[...]
        cq, ckv = cq_ref[...], ckv_ref[...]
        gq, gkv = gq_ref[...], gkv_ref[...]
        cqn, rq = _norm_rows(cq, gq)
        ckvn, rkv = _norm_rows(ckv, gkv)
        cqn_b, ckvn_b = cqn.astype(BF16), ckvn.astype(BF16)
        dqf = jnp.concatenate([_rope_t(dq_ref[h], c, s1, s2) for h in range(NH)], axis=1).astype(BF16)
        dkf = jnp.concatenate([dk_ref[h] for h in range(NH)], axis=1).astype(BF16)
        dvf = jnp.concatenate([dv_ref[h] for h in range(NH)], axis=1).astype(BF16)
        dkpe = dk_ref[0]
        for h in range(1, NH):
            dkpe = dkpe + dk_ref[h]
        lane = lax.broadcasted_iota(jnp.int32, (TM, LANE), 1)
        dkpe = jnp.where((lane >= QK_NOPE) & (lane < QK_NOPE + QK_ROPE), dkpe, 0.0)
        dkpe_ref[...] = _rope_t(dkpe, c, s1, s2).astype(BF16)
        dwq_ref[...] += _mm_tn(cqn_b, dqf)
        dwk_ref[...] += _mm_tn(ckvn_b, dkf)
        dwv_ref[...] += _mm_tn(ckvn_b, dvf)
        dcqn = _mm_nt(dqf, wq_ref[...])
        dckvn = _mm_nt(dkf, wk_ref[...]) + _mm_nt(dvf, wv_ref[...])
        dgq_ref[...] += jnp.sum(dcqn * (cq * rq), axis=0, keepdims=True)
        dgkv_ref[...] += jnp.sum(dckvn * (ckv * rkv), axis=0, keepdims=True)
        dcq_ref[...] = _norm_rows_bwd(cq, rq, gq, dcqn).astype(BF16)
        dckv_ref[...] = _norm_rows_bwd(ckv, rkv, gkv, dckvn).astype(BF16)

    row = lambda w, j=0: pl.BlockSpec((TM, w), lambda i: (i, j))
    full = lambda a: pl.BlockSpec(a.shape, lambda i: (0,) * a.ndim)
    acc = lambda shape: pl.BlockSpec(shape, lambda i: (0, 0))
    heads = pl.BlockSpec((NH, TM, LANE), lambda i: (0, i, 0))
    sds = jax.ShapeDtypeStruct
    return pl.pallas_call(
        body, name="qkv_bwd", grid=(s // TM,),
        in_specs=[row(Q_LORA, P_CQ // Q_LORA), row(KV_LORA, P_CKV // KV_LORA), heads, heads, heads,
                  full(g_q), full(g_kv), full(w_uq_p), full(w_k_p), full(w_v_p), row(LANE), row(LANE), row(LANE)],
        out_specs=[row(Q_LORA), row(KV_LORA), row(LANE), acc((Q_LORA, D)), acc((KV_LORA, D)), acc((KV_LORA, D)),
                   acc((1, Q_LORA)), acc((1, KV_LORA))],
        out_shape=[sds((s, Q_LORA), BF16), sds((s, KV_LORA), BF16), sds((s, LANE), BF16), sds((Q_LORA, D), F32),
                   sds((KV_LORA, D), F32), sds((KV_LORA, D), F32), sds((1, Q_LORA), F32), sds((1, KV_LORA), F32)],
        compiler_params=_params(("arbitrary",)),
    )(proj, proj, dq, dk, dvv, g_q, g_kv, w_uq_p, w_k_p, w_v_p, rc, rs1, rs2)


def _front_bwd(x, dout, dmg, dga, dh3, dgb, dcq, dckv, dkpe, g_pre, w_in_pt, token):
    s = x.shape[0]

    def body(x_ref, do_ref, dmg_ref, dga_ref, dh3_ref, dgb_ref, dcq_ref, dckv_ref, dkpe_ref, g_ref, w_ref, token_ref, gx_ref, dg_ref):
        @pl.when(pl.program_id(0) == 0)
        def _():
            dg_ref[...] = jnp.zeros_like(dg_ref)

        xv, g = x_ref[...], g_ref[...]
        _, r = _norm_rows(xv, g)
        pieces = ((dmg_ref[...], P_MERGE), (dga_ref[...], P_GA), (dh3_ref[0], P_HQ), (dh3_ref[1], P_HF), (dh3_ref[2], P_HI),
                  (dgb_ref[...], P_GB), (dcq_ref[...], P_CQ), (dckv_ref[...], P_CKV), (dkpe_ref[...], P_KPE))
        dh = jnp.zeros((TM, D), F32)
        for piece, off in pieces:
            dh = dh + _mm(piece, w_ref[off:off + piece.shape[1], :])
        dg_ref[...] += jnp.sum(dh * (xv * r), axis=0, keepdims=True)
        gx_ref[...] = do_ref[...] + _norm_rows_bwd(xv, r, g, dh)

    row = lambda w: pl.BlockSpec((TM, w), lambda i: (i, 0))
    full = lambda a: pl.BlockSpec(a.shape, lambda i: (0,) * a.ndim)
    sds = jax.ShapeDtypeStruct
    return pl.pallas_call(
        body, name="front_bwd", grid=(s // TM,),
        in_specs=[row(D), row(D), row(2048), row(512), pl.BlockSpec((3, TM, 512), lambda i: (0, i, 0)), row(512), row(Q_LORA),
                  row(KV_LORA), row(LANE), full(g_pre), full(w_in_pt), full(token)],
        out_specs=[row(D), pl.BlockSpec((1, D), lambda i: (0, 0))],
        out_shape=[sds((s, D), F32), sds((1, D), F32)],
        compiler_params=_params(("arbitrary",)),
    )(x, dout, dmg, dga, dh3, dgb, dcq, dckv, dkpe, g_pre, w_in_pt, token)


TK_GRAD = 512


def _win_grad(h, pieces, name):
    s = h.shape[0]
    n = len(pieces)

    def body(h_ref, *refs):
        @pl.when(pl.program_id(0) == 0)
        def _():
            for o_ref in refs[n:]:
                o_ref[...] = jnp.zeros_like(o_ref)

        hv = h_ref[...]
        for d_ref, o_ref in zip(refs[:n], refs[n:]):
            if len(d_ref.shape) == 3:
                for k in range(d_ref.shape[0]):
                    o_ref[k] += _mm_tn(d_ref[k], hv)
            else:
                o_ref[...] += _mm_tn(d_ref[...], hv)

    def in_spec(p):
        if p.ndim == 3:
            return pl.BlockSpec((p.shape[0], TK_GRAD, p.shape[2]), lambda kk: (0, kk, 0))
        return pl.BlockSpec((TK_GRAD, p.shape[1]), lambda kk: (kk, 0))

    out_shapes = [(p.shape[0], p.shape[2], D) if p.ndim == 3 else (p.shape[1], D) for p in pieces]
    return pl.pallas_call(
        body, name=name, grid=(s // TK_GRAD,),
        in_specs=[pl.BlockSpec((TK_GRAD, D), lambda kk: (kk, 0))] + [in_spec(p) for p in pieces],
        out_specs=[pl.BlockSpec(sh, lambda kk, nd=len(sh): (0,) * nd) for sh in out_shapes],
        out_shape=[jax.ShapeDtypeStruct(sh, F32) for sh in out_shapes],
        compiler_params=_params(("arbitrary",)),
    )(h, *pieces)


def _pad_win_t(w_in_t):
    z = lambda n: jnp.zeros((n, w_in_t.shape[1]), w_in_t.dtype)
    sl = lambda o, n: w_in_t[o:o + n]
    return jnp.concatenate([sl(O_MERGE, 2048), sl(O_GA, 512), sl(O_HQ, 512), sl(O_HF, 512), sl(O_HI, 512), sl(O_GB, 512),
                            sl(O_CQ, Q_LORA), sl(O_CKV, KV_LORA), z(64), sl(O_KPE, QK_ROPE), z(32)], axis=0)


def _pad_wuq(w_uq):
    w = w_uq.reshape(Q_LORA, NH, QK_NOPE + QK_ROPE)
    return jnp.pad(w, ((0, 0), (0, 0), (0, LANE - QK_NOPE - QK_ROPE))).reshape(Q_LORA, NH * LANE)


def _unpad_wuq(g):
    return g.reshape(Q_LORA, NH, LANE)[:, :, :QK_NOPE + QK_ROPE].reshape(Q_LORA, NH * (QK_NOPE + QK_ROPE))


def _pad_wukv(w_ukv):
    w = w_ukv.reshape(KV_LORA, NH, QK_NOPE + V_DIM)
    w_k = jnp.pad(w[:, :, :QK_NOPE], ((0, 0), (0, 0), (0, LANE - QK_NOPE))).reshape(KV_LORA, NH * LANE)
    wv = w[:, :, QK_NOPE:].reshape(KV_LORA, NH // 2, 2, 1, V_DIM)
    eye = jnp.eye(2, dtype=w.dtype).reshape(1, 1, 2, 2, 1)
    return w_k, (wv * eye).reshape(KV_LORA, NH * LANE)


def _unpad_wukv(gk, gv):
    gk = gk.reshape(KV_LORA, NH, LANE)[:, :, :QK_NOPE]
    gv = gv.reshape(KV_LORA, NH // 2, 2, 2, V_DIM)
    gv = jnp.stack([gv[:, :, 0, 0], gv[:, :, 1, 1]], axis=2).reshape(KV_LORA, NH, V_DIM)
    return jnp.concatenate([gk, gv], axis=-1).reshape(KV_LORA, NH * (QK_NOPE + V_DIM))


def _local_step(x, tgt, g_pre, w_in_t, b_gate, g_q, g_kv, lb_logits, g_hgrn, g_post, weights, exchange=None):
    s = x.shape[0]
    w_in_p = _pad_win_t(w_in_t)
    rc, rs1, rs2 = _rope_tables(s)
    g_hg = jnp.tile(g_hgrn, (1, NH))

    proj, h = _front_fwd(x, g_pre, w_in_p, weights.tokens)
    w_uq, w_ukv = weights.qkv(h)
    w_uq_p = _pad_wuq(w_uq)
    w_k_p, w_v_p = _pad_wukv(w_ukv)
    q, k, vv = _qkv_fwd(proj, g_q, g_kv, w_uq_p, w_k_p, w_v_p, rc, rs1, rs2)
    attn, lse = _attn_fwd(q, k, vv)
    o_raw, states = _hgrn_fwd(proj, lb_logits)
    wa, wb, w_out = weights.mid(o_raw)
    (loss, dout, dattn, dga, dor, dgb, dmg, d_wout, d_wa, d_wb, d_gpost, d_bgate, d_ghg) = _mid(
        proj, attn, o_raw, x, tgt, g_hg, b_gate, g_post, wa, wb, w_out)
    w_mg, w_ga, w_gb = _win_grad(h, [dmg, dga, dgb], "win_grad_mid")
    dh3, d_lbl = _hgrn_bwd(proj, lb_logits, states, dor)
    (w_h3,) = _win_grad(h, [dh3], "win_grad_hgrn")
    d_win_rest = jnp.concatenate([w_ga, w_h3[0], w_h3[1], w_h3[2], w_gb, w_mg], axis=0)
    early = dict(w_in_rest=d_win_rest, w_branch_a=d_wa, w_branch_b=d_wb, w_out=d_wout)
    token = exchange.start_early(early) if exchange else jnp.zeros((8, LANE), F32)
    dq, dk, dvv = _attn_bwd(q, k, vv, attn, dattn, lse, token)
    dcq, dckv, dkpe, d_wuq_p, d_wk_p, d_wv_p, d_gq, d_gkv = _qkv_bwd(proj, dq, dk, dvv, g_q, g_kv, w_uq_p, w_k_p, w_v_p, rc, rs1, rs2)
    w_cq, w_ckv, w_kpe = _win_grad(h, [dcq, dckv, dkpe], "win_grad_qkv")
    d_win_qkv = jnp.concatenate([w_cq, w_ckv, w_kpe[64:64 + QK_ROPE]], axis=0)
    late = dict(w_in_qkv=d_win_qkv, w_uq=_unpad_wuq(d_wuq_p), w_ukv=_unpad_wukv(d_wk_p, d_wv_p))
    token = exchange.start_late(late) if exchange else jnp.zeros((8, LANE), F32)
    grad_x, d_gpre = _front_bwd(x, dout, dmg, dga, dh3, dgb, dcq, dckv, dkpe, g_pre, w_in_p, token)
    vec_grads = dict(g_pre=d_gpre, b_gate=d_bgate, g_q=d_gq, g_kv=d_gkv, lb_logits=d_lbl, g_hgrn=d_ghg, g_post=d_gpost)
    return loss, grad_x, dict(early, **late), vec_grads


SHARD_SHAPES = (("w_in", (1416, 1024)), ("w_uq", (192, 768)), ("w_ukv", (256, 256)), ("w_branch_a", (512, 256)),
                ("w_branch_b", (512, 256)), ("w_out", (256, 1024)))
BIG = tuple(n for n, _ in SHARD_SHAPES)
ROW_SHARDED = ("w_in", "w_uq", "w_out")
GATHER_SPLIT_AXIS = dict(w_in=1, w_uq=0, w_ukv=0, w_branch_a=0, w_branch_b=0, w_out=0)
N_CHIPS = 4
QKV_ROWS = Q_LORA + KV_LORA + QK_ROPE


def _to_block(name, a):
    return a[0].T if name == "w_in" else a[0]


def _from_block(name, a):
    return a.T[None] if name == "w_in" else a[None]
VEC_ROWS = (("g_pre", 0, 1024), ("b_gate", 1, 2048), ("g_q", 2, 768), ("g_kv", 3, 256), ("g_hgrn", 6, 64), ("g_post", 7, 1024))
VEC_LB_ROW = 4
VEC_SHAPE = (8, 2048)


def _split_by_chip(name, g):
    a, b = dict(SHARD_SHAPES)[name]
    return g.reshape(N_CHIPS, a, b) if name in ROW_SHARDED else g.reshape(a, N_CHIPS, b).transpose(1, 0, 2)


def _join_chips(name, w):
    a, b = dict(SHARD_SHAPES)[name]
    return w.reshape(N_CHIPS * a, b) if name in ROW_SHARDED else w.transpose(1, 0, 2).reshape(a, N_CHIPS * b)


MESH = pl.DeviceIdType.MESH
HBM = pl.BlockSpec(memory_space=pltpu.HBM)


def _mesh_place():
    x, y, c = lax.axis_index("x"), lax.axis_index("y"), lax.axis_index("c")
    return x, y, c, 2 * x + y, [(1 - x, y), (x, 1 - y), (1 - x, 1 - y)]


def _remote(src, dst, send_sems, recv_sems, k, to):
    return pltpu.make_async_remote_copy(src_ref=src, dst_ref=dst, send_sem=send_sems.at[k], recv_sem=recv_sems.at[k],
                                        device_id=to, device_id_type=MESH)


def _gather_weights(shards, split_axes):
    n = len(shards)

    def body(*refs):
        srcs, outs = refs[:n], refs[n:2 * n]
        ici_send, ici_recv, d2d_send, d2d_recv, local_sems = refs[2 * n:]
        x, y, c, me, chips = _mesh_place()
        sibling = (x, y, 1 - c)

        def half(ref, k, which):
            size = shards[k].shape[split_axes[k]] // 2
            part = pl.ds(pl.multiple_of(which * size, size), size)
            return ref.at[part] if split_axes[k] == 0 else ref.at[:, part]

        own = [pltpu.make_async_copy(srcs[k], outs[k].at[me], local_sems.at[k]) for k in range(n)]
        for cp in own:
            cp.start()
        started = []
        for k in range(n):
            for j, (px, py) in enumerate(chips):
                cp = _remote(half(srcs[k], k, c), half(outs[k].at[me], k, c), ici_send, ici_recv, 3 * k + j, (px, py, c))
                cp.start()
                started.append(cp)
        for k in range(n):
            for j, (px, py) in enumerate(chips):
                landed = half(outs[k].at[2 * px + py], k, c)
                _remote(landed, landed, ici_send, ici_recv, 3 * k + j, (px, py, c)).wait_recv()
                cp = _remote(landed, landed, d2d_send, d2d_recv, 3 * k + j, sibling)
                cp.start()
                started.append(cp)
        for k in range(n):
            for j, (px, py) in enumerate(chips):
                other = half(outs[k].at[2 * px + py], k, 1 - c)
                _remote(other, other, d2d_send, d2d_recv, 3 * k + j, sibling).wait_recv()
        for cp in started:
            cp.wait_send()
        for cp in own:
            cp.wait()

    sems = pltpu.SemaphoreType.DMA((3 * n,))
    return pl.pallas_call(
        body, name="gather_weights", in_specs=[HBM] * n, out_specs=[HBM] * n,
        out_shape=[jax.ShapeDtypeStruct((N_CHIPS,) + s.shape, s.dtype) for s in shards],
        scratch_shapes=[sems, sems, sems, sems, pltpu.SemaphoreType.DMA((n,))],
        compiler_params=pltpu.CompilerParams(has_side_effects=True),
    )(*shards)


def _chip_exchange(srcs, name):
    n = len(srcs)

    def body(*refs):
        src_refs, outs = refs[:n], refs[n:2 * n]
        send_sems, recv_sems, local_sems = refs[2 * n:]
        x, y, c, me, chips = _mesh_place()

        def slab(k, t):
            return src_refs[k] if srcs[k].ndim == 2 else src_refs[k].at[t]

        own = [pltpu.make_async_copy(slab(k, me), outs[k].at[me], local_sems.at[k]) for k in range(n)]
        for cp in own:
            cp.start()
        sends = []
        for k in range(n):
            for j, (px, py) in enumerate(chips):
                cp = _remote(slab(k, 2 * px + py), outs[k].at[me], send_sems, recv_sems, 3 * k + j, (px, py, c))
                cp.start()
                sends.append(cp)
        for k in range(n):
            for j, (px, py) in enumerate(chips):
                _remote(slab(k, me), outs[k].at[2 * px + py], send_sems, recv_sems, 3 * k + j, (px, py, c)).wait_recv()
        for cp in sends:
            cp.wait_send()
        for cp in own:
            cp.wait()

    sems = pltpu.SemaphoreType.DMA((3 * n,))
    return pl.pallas_call(
        body, name=name, in_specs=[HBM] * n, out_specs=[HBM] * n,
        out_shape=[jax.ShapeDtypeStruct((N_CHIPS,) + s.shape[-2:], s.dtype) for s in srcs],
        scratch_shapes=[sems, sems, pltpu.SemaphoreType.DMA((n,))],
        compiler_params=pltpu.CompilerParams(has_side_effects=True),
    )(*srcs)


def _sibling_exchange(srcs, name):
    n = len(srcs)

    def body(*refs):
        src_refs, outs = refs[:n], refs[n:2 * n]
        send_sems, recv_sems = refs[2 * n:]
        sibling = (lax.axis_index("x"), lax.axis_index("y"), 1 - lax.axis_index("c"))
        copies = [_remote(src_refs[k], outs[k], send_sems, recv_sems, k, sibling) for k in range(n)]
        for cp in copies:
            cp.start()
        for cp in copies:
            cp.wait()

    sems = pltpu.SemaphoreType.DMA((n,))
    return pl.pallas_call(
        body, name=name, in_specs=[HBM] * n, out_specs=[HBM] * n,
        out_shape=[jax.ShapeDtypeStruct(s.shape, s.dtype) for s in srcs],
        scratch_shapes=[sems, sems],
        compiler_params=pltpu.CompilerParams(has_side_effects=True),
    )(*srcs)


SEM = pl.BlockSpec(memory_space=pltpu.SEMAPHORE)
DATAFLOW = pltpu.SideEffectType.DATAFLOW_SIDE_EFFECTING


def _exchange_copies(srcs, to_first, src_refs, land_refs, send_sems, recv_sems):
    x, y, c, me, chips = _mesh_place()
    n = len(srcs)
    sends, recvs = [], []
    for k in range(n):
        if k in to_first:
            base = 3 * n + 4 * to_first.index(k)
            sends.append((me != 0, pltpu.make_async_remote_copy(
                src_ref=src_refs[k], dst_ref=land_refs[k].at[me], send_sem=send_sems.at[base], recv_sem=recv_sems.at[base + me],
                device_id=(0, 0, c), device_id_type=MESH)))
            for s in range(1, N_CHIPS):
                recvs.append((me == 0, pltpu.make_async_remote_copy(
                    src_ref=src_refs[k], dst_ref=land_refs[k].at[s], send_sem=send_sems.at[base], recv_sem=recv_sems.at[base + s],
                    device_id=(s // 2, s % 2, c), device_id_type=MESH)))
        else:
            slab = (lambda t, k=k: src_refs[k]) if srcs[k].ndim == 2 else (lambda t, k=k: src_refs[k].at[t])
            for j, (px, py) in enumerate(chips):
                sends.append((None, _remote(slab(2 * px + py), land_refs[k].at[me], send_sems, recv_sems, 3 * k + j, (px, py, c))))
                recvs.append((None, _remote(slab(me), land_refs[k].at[2 * px + py], send_sems, recv_sems, 3 * k + j, (px, py, c))))
    return sends, recvs


def _when(pred, fn):
    if pred is None:
        fn()
    else:
        pl.when(pred)(fn)


def _exchange_start(srcs, to_first, name, after=None):
    n = len(srcs)
    n_sems = 3 * n + 4 * len(to_first)
    lands = [lax.empty((N_CHIPS,) + s.shape[-2:], s.dtype) for s in srcs]
    extra = [] if after is None else [after]

    def body(*refs):
        src_refs, land_refs = refs[:n], refs[n:2 * n]
        send_sems, recv_sems, token = refs[2 * n + len(extra)], refs[2 * n + len(extra) + 1], refs[-1]
        sends, _ = _exchange_copies(srcs, to_first, src_refs, land_refs, send_sems, recv_sems)
        for pred, cp in sends:
            _when(pred, cp.start)
        token[...] = jnp.zeros_like(token)

    hbm = lambda a: pltpu.HBM(a.shape, a.dtype)
    res = pl.pallas_call(
        body, name=name,
        out_shape=[pltpu.SemaphoreType.DMA((n_sems,)), pltpu.SemaphoreType.DMA((n_sems,))] + [hbm(a) for a in srcs + lands]
        + [jax.ShapeDtypeStruct((8, LANE), F32)],
        in_specs=[HBM] * (2 * n) + [pl.BlockSpec(memory_space=pl.ANY)] * len(extra),
        out_specs=[SEM, SEM] + [HBM] * (2 * n) + [pl.BlockSpec(memory_space=pltpu.VMEM)],
        input_output_aliases={i: 2 + i for i in range(2 * n)},
        compiler_params=pltpu.CompilerParams(has_side_effects=DATAFLOW),
    )(*[pltpu.with_memory_space_constraint(a, pltpu.HBM) for a in srcs + lands], *extra)
    return res[:-1], res[-1]


def _exchange_wait(srcs, to_first, started, after, name):
    n = len(srcs)
    send_sems, recv_sems, thru = started[0], started[1], started[2:]

    def body(*refs):
        src_refs, land_refs, send_ref, recv_ref = refs[:n], refs[n:2 * n], refs[2 * n], refs[2 * n + 1]
        sends, recvs = _exchange_copies(srcs, to_first, src_refs, land_refs, send_ref, recv_ref)
        for pred, cp in sends:
            _when(pred, cp.wait_send)
        for pred, cp in recvs:
            _when(pred, cp.wait_recv)

    res = pl.pallas_call(
        body, name=name, out_shape=[pltpu.HBM(a.shape, a.dtype) for a in thru],
        in_specs=[HBM] * (2 * n) + [SEM, SEM, pl.BlockSpec(memory_space=pl.ANY)], out_specs=[HBM] * (2 * n),
        input_output_aliases={i: i for i in range(2 * n)},
        compiler_params=pltpu.CompilerParams(has_side_effects=DATAFLOW),
    )(*thru, send_sems, recv_sems, after)
    return res[n:]


ROW_TILE = 256
COL_TILE = 256


def _block_tiling(a, b):
    if a <= ROW_TILE or a % ROW_TILE == 0:
        ta = min(a, ROW_TILE)
        return a // ta, (ta, b), lambda i: (i, 0)
    return b // COL_TILE, (a, COL_TILE), lambda i: (0, i)


def _sum_chips(parts, name):
    _, a, b = parts.shape
    steps, tile, at = _block_tiling(a, b)

    def body(p_ref, o_ref):
        f = lambda t: p_ref[t].astype(F32)
        o_ref[...] = ((f(0) + f(1)) + f(2)) + f(3)

    return pl.pallas_call(
        body, name=name, grid=(steps,),
        in_specs=[pl.BlockSpec((N_CHIPS,) + tile, lambda i: (0,) + at(i))],
        out_specs=pl.BlockSpec(tile, at),
        out_shape=jax.ShapeDtypeStruct((a, b), F32),
        compiler_params=_params(("parallel",)),
    )(parts)


def _sum_landed(land, own, name, first_land=None, first_own=None):
    _, a, b = land.shape
    steps, tile, at = _block_tiling(a, b)
    extra = first_land is not None

    def body(*refs):
        p_ref, own_ref, o_ref = refs[0], refs[1], refs[-1]
        me = 2 * lax.axis_index("x") + lax.axis_index("y")
        own = own_ref[...].astype(F32)
        slot = lambda t: jnp.where(me == t, own, p_ref[t].astype(F32))
        o_ref[...] = ((slot(0) + slot(1)) + slot(2)) + slot(3)
        if extra:
            fp_ref, fo_ref = refs[2], refs[3]
            r = fo_ref.shape[0]

            @pl.when(me == 0)
            def _():
                f = lambda t: fp_ref[t].astype(F32)
                o_ref[0:r, :] += ((fo_ref[...].astype(F32) + f(1)) + f(2)) + f(3)

    in_specs = [pl.BlockSpec((N_CHIPS,) + tile, lambda i: (0,) + at(i)), pl.BlockSpec(tile, at)]
    args = [land, own]
    if extra:
        r = first_own.shape[0]
        assert tile[0] == a, "the extra rows need whole columns in a step"
        in_specs += [pl.BlockSpec((N_CHIPS, r, tile[1]), lambda i: (0,) + at(i)), pl.BlockSpec((r, tile[1]), at)]
        args += [first_land, first_own]
    return pl.pallas_call(
        body, name=name, grid=(steps,), in_specs=in_specs, out_specs=pl.BlockSpec(tile, at),
        out_shape=jax.ShapeDtypeStruct((a, b), F32), compiler_params=_params(("parallel",)),
    )(*args)


class _LaterWeights:
    QKV = ("w_uq", "w_ukv")
    MID = ("w_branch_a", "w_branch_b", "w_out")

    def __init__(self, blocks, after):
        self.blocks = blocks
        self.qkv_started, t1 = _exchange_start([blocks[n] for n in self.QKV], (), "weights_qkv_start", after)
        self.mid_started, t2 = _exchange_start([blocks[n] for n in self.MID], (), "weights_mid_start", after)
        self.tokens = [t1, t2]

    def _whole(self, names, started, after, name):
        landed = _exchange_wait([self.blocks[n] for n in names], (), started, after, name)
        me = 2 * lax.axis_index("x") + lax.axis_index("y")
        return [_join_chips(n, lax.dynamic_update_index_in_dim(land, self.blocks[n], me, 0)) for n, land in zip(names, landed)]

    def qkv(self, after):
        return self._whole(self.QKV, self.qkv_started, after, "weights_qkv_wait")

    def mid(self, after):
        return self._whole(self.MID, self.mid_started, after, "weights_mid_wait")


class _GradExchange:
    EARLY = ("w_in", "w_branch_a", "w_branch_b", "w_out")
    LATE = ("w_uq", "w_ukv")

    def start_early(self, g):
        full = jnp.concatenate([jnp.zeros((QKV_ROWS, D), F32), g["w_in_rest"]], axis=0)
        g = dict(g, w_in=full)
        self.early = [_split_by_chip(n, g[n]).astype(BF16) for n in self.EARLY]
        self.early_started, token = _exchange_start(self.early, (), "grads_early_start")
        return token

    def start_late(self, g):
        self.early_landed = _exchange_wait(self.early, (), self.early_started, g["w_uq"], "grads_early_wait")
        self.late = [_split_by_chip(n, g[n]).astype(BF16) for n in self.LATE] + [g["w_in_qkv"].astype(BF16)]
        self.late_started, token = _exchange_start(self.late, (2,), "grads_late_start")
        return token

    def finish(self, after):
        late_landed = _exchange_wait(self.late, (2,), self.late_started, after, "grads_late_wait")
        me = 2 * lax.axis_index("x") + lax.axis_index("y")
        own = lambda slabs: lax.dynamic_index_in_dim(slabs, me, axis=0, keepdims=False)
        sums = {}
        for n, slabs, land in zip(self.EARLY, self.early, self.early_landed):
            if n == "w_in":
                sums[n] = _sum_landed(land, own(slabs), "sum_" + n, first_land=late_landed[2], first_own=self.late[2])
            else:
                sums[n] = _sum_landed(land, own(slabs), "sum_" + n)
        for n, slabs, land in zip(self.LATE, self.late, late_landed):
            sums[n] = _sum_landed(land, own(slabs), "sum_" + n)
        return sums


def _adamw_math(g, w, m, v):
    nm = ADAM_B1 * m + (1.0 - ADAM_B1) * g
    nv = ADAM_B2 * v + (1.0 - ADAM_B2) * (g * g)
    m_hat = nm / (1.0 - ADAM_B1 ** ADAM_STEP)
    v_hat = nv / (1.0 - ADAM_B2 ** ADAM_STEP)
    return -ADAM_LR * (m_hat / (jnp.sqrt(v_hat) + ADAM_EPS) + ADAM_WD * w), nm, nv


def _adamw(p_mine, p_sibling, w, m, v, name):
    a, b = p_mine.shape
    steps, tile, at = _block_tiling(a, b)

    def body(a_ref, b_ref, w_ref, m_ref, v_ref, g_ref, d_ref, nm_ref, nv_ref):
        g = a_ref[...] + b_ref[...]
        g_ref[...] = g
        d_ref[...], nm_ref[...], nv_ref[...] = _adamw_math(g, w_ref[...], m_ref[...], v_ref[...])

    spec = pl.BlockSpec(tile, at)
    sds = jax.ShapeDtypeStruct((a, b), F32)
    return pl.pallas_call(
        body, name=name, grid=(steps,), in_specs=[spec] * 5, out_specs=[spec] * 4, out_shape=[sds] * 4,
        compiler_params=_params(("parallel",)),
    )(p_mine, p_sibling, w, m, v)


LOSS_AT = (2, 1024)


def _vec_pack(vg, loss):
    names = [n for n, _, _ in VEC_ROWS]

    def body(*refs):
        o_ref = refs[-1]
        lb_ref, loss_ref = refs[len(names)], refs[len(names) + 1]
        o_ref[...] = jnp.zeros_like(o_ref)
        o_ref[LOSS_AT[0]:LOSS_AT[0] + 1, LOSS_AT[1]:LOSS_AT[1] + LANE] = jnp.broadcast_to(loss_ref[...], (1, LANE))
        for (name, row, size), ref in zip(VEC_ROWS, refs):
            if name == "g_hgrn":
                r = lax.broadcasted_iota(jnp.int32, (NH * V_DIM, LANE), 0)
                c = lax.broadcasted_iota(jnp.int32, (NH * V_DIM, LANE), 1)
                fold = ((r % V_DIM) == c).astype(F32)
                o_ref[row:row + 1, 0:LANE] = jnp.dot(ref[...], fold, precision=HIGHEST, preferred_element_type=F32)
            else:
                o_ref[row:row + 1, 0:size] = ref[...]
        o_ref[VEC_LB_ROW:VEC_LB_ROW + 2, 0:512] = lb_ref[...]

    return pl.pallas_call(body, name="vec_pack", out_shape=jax.ShapeDtypeStruct(VEC_SHAPE, F32))(
        *[vg[n] for n in names], vg["lb_logits"], loss)


def _adamw_vec(p_mine, p_sibling, w, m, v):
    names = [n for n, _, _ in VEC_ROWS] + ["lb_logits"]
    k = len(names)

    def body(a_ref, b_ref, *refs):
        ins, outs = refs[:3 * k], refs[3 * k:]
        at = (slice(LOSS_AT[0], LOSS_AT[0] + 1), slice(LOSS_AT[1], LOSS_AT[1] + LANE))
        outs[-1][...] = a_ref[at] + b_ref[at]
        for i, name in enumerate(names):
            if name == "lb_logits":
                rows, cols = slice(VEC_LB_ROW, VEC_LB_ROW + 2), slice(0, 512)
            else:
                _, row, size = VEC_ROWS[i]
                rows, cols = slice(row, row + 1), slice(0, size)
            g = a_ref[rows, cols] + b_ref[rows, cols]
            d, nm, nv = _adamw_math(g, ins[i][...], ins[k + i][...], ins[2 * k + i][...])
            for o_ref, val in zip(outs[4 * i:4 * i + 4], (g, d, nm, nv)):
                o_ref[...] = val

    shapes = [jax.ShapeDtypeStruct(w[n].shape, F32) for n in names for _ in range(4)] + [jax.ShapeDtypeStruct((1, LANE), F32)]
    res = pl.pallas_call(body, name="adamw_vec", out_shape=shapes)(
        p_mine, p_sibling, *[w[n] for n in names], *[m[n] for n in names], *[v[n] for n in names])
    return [{n: res[4 * i + j] for i, n in enumerate(names)} for j in range(4)], res[-1]


WEIGHTS = ("g_pre", "w_in", "b_gate", "g_q", "w_uq", "g_kv", "w_ukv", "lb_logits", "g_hgrn", "w_branch_a", "w_branch_b", "w_out", "g_post")


def kernel(x, g_pre, w_in, b_gate, g_q, w_uq, g_kv, w_ukv, lb_logits, g_hgrn, w_branch_a, w_branch_b, w_out, g_post, loss_target, m_g_pre, m_w_in, m_b_gate, m_g_q, m_w_uq, m_g_kv, m_w_ukv, m_lb_logits, m_g_hgrn, m_w_branch_a, m_w_branch_b, m_w_out, m_g_post, v_g_pre, v_w_in, v_b_gate, v_g_q, v_w_uq, v_g_kv, v_w_ukv, v_lb_logits, v_g_hgrn, v_w_branch_a, v_w_branch_b, v_w_out, v_g_post):
    w = dict(g_pre=g_pre, w_in=w_in, b_gate=b_gate, g_q=g_q, w_uq=w_uq, g_kv=g_kv, w_ukv=w_ukv, lb_logits=lb_logits, g_hgrn=g_hgrn,
             w_branch_a=w_branch_a, w_branch_b=w_branch_b, w_out=w_out, g_post=g_post)
    m = dict(g_pre=m_g_pre, w_in=m_w_in, b_gate=m_b_gate, g_q=m_g_q, w_uq=m_w_uq, g_kv=m_g_kv, w_ukv=m_w_ukv, lb_logits=m_lb_logits,
             g_hgrn=m_g_hgrn, w_branch_a=m_w_branch_a, w_branch_b=m_w_branch_b, w_out=m_w_out, g_post=m_g_post)
    v = dict(g_pre=v_g_pre, w_in=v_w_in, b_gate=v_b_gate, g_q=v_g_q, w_uq=v_w_uq, g_kv=v_g_kv, w_ukv=v_w_ukv, lb_logits=v_lb_logits,
             g_hgrn=v_g_hgrn, w_branch_a=v_w_branch_a, w_branch_b=v_w_branch_b, w_out=v_w_out, g_post=v_g_post)
    blocks = {n: _to_block(n, w[n]).astype(BF16) for n in BIG}
    (w_in_all,) = _gather_weights([blocks["w_in"]], [GATHER_SPLIT_AXIS["w_in"]])
    weights = _LaterWeights(blocks, w_in_all)
    exchange = _GradExchange()
    loss, grad_x, _, vec_grads = _local_step(
        x[0], loss_target[0], g_pre, _join_chips("w_in", w_in_all), b_gate, g_q, g_kv, lb_logits, g_hgrn, g_post, weights, exchange)
    sums = exchange.finish(grad_x)
    (vec_landed,) = _chip_exchange([_vec_pack(vec_grads, loss)], "scatter_vec")
    mine = [sums[n] for n in BIG] + [_sum_chips(vec_landed, "sum_vec")]
    theirs = _sibling_exchange(mine, "sibling_grads")
    outs = [{}, {}, {}, {}]
    for k, n in enumerate(BIG):
        blocks = [_to_block(n, t[n]) for t in (w, m, v)]
        for o, val in zip(outs, _adamw(mine[k], theirs[k], *blocks, "adamw_" + n)):
            o[n] = _from_block(n, val)
    vec_outs, total = _adamw_vec(mine[-1], theirs[-1], w, m, v)
    for o, vals in zip(outs, vec_outs):
        o.update(vals)
    return (total[0, 0], grad_x[None], *[o[n] for o in outs for n in WEIGHTS])
```

```python
import functools
import math

import numpy as np
import jax
import jax.numpy as jnp
from jax import lax
from jax.experimental import pallas as pl
from jax.experimental.pallas import tpu as pltpu

F32 = jnp.float32
BF16 = jnp.bfloat16
HIGHEST = lax.Precision.HIGHEST

D = 1024
NH = 8
QK_NOPE, QK_ROPE, V_DIM = 64, 32, 64
Q_LORA, KV_LORA = 768, 256
CHUNK = 64
HG_BLOCK = 32
EPS = 1e-6
D_IN = 5664
LANE = 128
P_MERGE, P_GA, P_HQ, P_HF, P_HI, P_GB, P_CQ, P_CKV, P_KPE = 0, 2048, 2560, 3072, 3584, 4096, 4608, 5376, 5632
D_P = 5760
O_CQ, O_CKV, O_KPE, O_GA, O_HQ, O_HF, O_HI, O_GB, O_MERGE = 0, 768, 1024, 1056, 1568, 2080, 2592, 3104, 3616

TM = 256
TQ = 512
ONES_LANE = (LANE - 1, 0)
TH = 256
HG_PAIRS = 2
VMEM_LIMIT = 56 * 1024 * 1024

ADAM_LR, ADAM_B1, ADAM_B2, ADAM_EPS, ADAM_WD, ADAM_STEP = 0.001, 0.9, 0.999, 1e-08, 0.01, 10

NT_DIMS = (((1,), (1,)), ((), ()))
TN_DIMS = (((0,), (0,)), ((), ()))


def _params(sem):
    return pltpu.CompilerParams(dimension_semantics=sem, vmem_limit_bytes=VMEM_LIMIT)


def _mm(a, b):
    return jnp.dot(a, b, preferred_element_type=F32)


def _mm_nt(a, b):
    return lax.dot_general(a, b, NT_DIMS, preferred_element_type=F32)


def _mm_tn(a, b):
    return lax.dot_general(a, b, TN_DIMS, preferred_element_type=F32)


def _sigmoid(z):
    return jax.nn.sigmoid(z)


def _rope(v, c, s1, s2):
    return v * c + pltpu.roll(v, 112, 1) * s1 + pltpu.roll(v, 16, 1) * s2


def _rope_t(dy, c, s1, s2):
    return dy * c + pltpu.roll(dy * s1, 16, 1) + pltpu.roll(dy * s2, 112, 1)


def _rope_tables(s):
    inv = 10000.0 ** (-jnp.arange(0, QK_ROPE, 2, dtype=F32) / QK_ROPE)
    ang = jnp.arange(s, dtype=F32)[:, None] * inv[None, :]
    cos, sin = jnp.cos(ang), jnp.sin(ang)
    z64, z32, o64, o32 = jnp.zeros((s, 64), F32), jnp.zeros((s, 32), F32), jnp.ones((s, 64), F32), jnp.ones((s, 32), F32)
    z16 = jnp.zeros((s, 16), F32)
    c = jnp.concatenate([o64, cos, cos, o32], axis=1)
    s1 = jnp.concatenate([z64, -sin, z16, z32], axis=1)
    s2 = jnp.concatenate([z64, z16, sin, z32], axis=1)
    return c, s1, s2


def _front_fwd(x, g_pre, w_in_pt, tokens=()):
    s = x.shape[0]
    tokens = list(tokens)

    def body(x_ref, g_ref, w_ref, *refs):
        o_ref, h_ref = refs[len(tokens):]
        xv = x_ref[...]
        r = lax.rsqrt(jnp.mean(xv * xv, axis=-1, keepdims=True) + EPS)
        h = ((xv * r) * g_ref[...]).astype(BF16)
        h_ref[...] = h
        o_ref[...] = _mm_nt(h, w_ref[...])

    return pl.pallas_call(
        body, name="front_fwd", grid=(s // TM,),
        in_specs=[pl.BlockSpec((TM, D), lambda i: (i, 0)), pl.BlockSpec((1, D), lambda i: (0, 0)),
                  pl.BlockSpec((D_P, D), lambda i: (0, 0))] + [pl.BlockSpec((8, LANE), lambda i: (0, 0))] * len(tokens),
        out_specs=[pl.BlockSpec((TM, D_P), lambda i: (i, 0)), pl.BlockSpec((TM, D), lambda i: (i, 0))],
        out_shape=[jax.ShapeDtypeStruct((s, D_P), F32), jax.ShapeDtypeStruct((s, D), BF16)],
        compiler_params=_params(("parallel",)),
    )(x, g_pre, w_in_pt, *tokens)


def _norm_rows(v, g):
    r = lax.rsqrt(jnp.mean(v * v, axis=-1, keepdims=True) + EPS)
    return (v * r) * g, r


def _qkv_fwd(proj, g_q, g_kv, w_uq_p, w_k_p, w_v_p, rc, rs1, rs2):
    s = proj.shape[0]

    def body(cq_ref, ckv_ref, kpe_ref, gq_ref, gkv_ref, wq_ref, wk_ref, wv_ref, c_ref, s1_ref, s2_ref, q_ref, k_ref, v_ref):
        c, s1, s2 = c_ref[...], s1_ref[...], s2_ref[...]
        cqn, _ = _norm_rows(cq_ref[...], gq_ref[...])
        ckvn, _ = _norm_rows(ckv_ref[...], gkv_ref[...])
        ckvn = ckvn.astype(BF16)
        qf = _mm(cqn.astype(BF16), wq_ref[...])
        kf = _mm(ckvn, wk_ref[...])
        vf = _mm(ckvn, wv_ref[...])
        kpe = _rope(kpe_ref[...], c, s1, s2)
        lane = lax.broadcasted_iota(jnp.int32, (TM, LANE), 1)
        for h in range(NH):
            blk = slice(h * LANE, (h + 1) * LANE)
            q_ref[h] = _rope(qf[:, blk], c, s1, s2).astype(BF16)
            k_ref[h] = (kf[:, blk] + kpe).astype(BF16)
            v_ref[h] = jnp.where(lane == ONES_LANE[h % 2], 1.0, vf[:, blk]).astype(BF16)

    row = lambda w, j: pl.BlockSpec((TM, w), lambda i: (i, j))
    full = lambda a: pl.BlockSpec(a.shape, lambda i: (0,) * a.ndim)
    hs = jax.ShapeDtypeStruct((NH, s, LANE), BF16)
    return pl.pallas_call(
        body, name="qkv_fwd", grid=(s // TM,),
        in_specs=[row(Q_LORA, P_CQ // Q_LORA), row(KV_LORA, P_CKV // KV_LORA), row(LANE, P_KPE // LANE),
                  full(g_q), full(g_kv), full(w_uq_p), full(w_k_p), full(w_v_p), row(LANE, 0), row(LANE, 0), row(LANE, 0)],
        out_specs=[pl.BlockSpec((NH, TM, LANE), lambda i: (0, i, 0))] * 3,
        out_shape=[hs, hs, hs],
        compiler_params=_params(("parallel",)),
    )(proj, proj, proj, g_q, g_kv, w_uq_p, w_k_p, w_v_p, rc, rs1, rs2)


LOG2E = 1.4426950408889634
QK_SCALE2 = LOG2E / math.sqrt(QK_NOPE + QK_ROPE)


def _diag_visible():
    row = lax.broadcasted_iota(jnp.int32, (TQ, TQ), 0)
    col = lax.broadcasted_iota(jnp.int32, (TQ, TQ), 1)
    return (col // CHUNK) <= (row // CHUNK)


def _attn_fwd(q, k, vv):
    s = q.shape[1]

    def body(q_ref, k_ref, v_ref, o_ref, lse_ref):
        i = pl.program_id(1)
        qs = (q_ref[0], q_ref[1])

        def tile(hh, t, carry, diag):
            m, acc = carry
            rows = pl.ds(pl.multiple_of(t * TQ, TQ), TQ)
            sc = _mm_nt(qs[hh], k_ref[hh, rows, :])
            if diag:
                sc = jnp.where(_diag_visible(), sc, -jnp.inf)
            m_new = jnp.maximum(m, jnp.max(sc, axis=-1, keepdims=True))
            alpha = jnp.exp2((m - m_new) * QK_SCALE2)
            p = jnp.exp2((sc - m_new) * QK_SCALE2).astype(BF16)
            acc = alpha * acc + _mm(p, v_ref[hh, rows, :])
            return m_new, acc

        def step(t, carry):
            return tile(0, t, carry[0], False), tile(1, t, carry[1], False)

        init = (jnp.full((TQ, 1), -jnp.inf, F32), jnp.zeros((TQ, LANE), F32))
        carry = lax.fori_loop(0, i, step, (init, init))
        lane = lax.broadcasted_iota(jnp.int32, (TQ, LANE), 1)
        out = jnp.zeros((TQ, LANE), F32)
        for hh in range(2):
            m, acc = tile(hh, i, carry[hh], True)
            l = jnp.sum(jnp.where(lane == ONES_LANE[hh], acc, 0.0), axis=-1, keepdims=True)
            out = out + jnp.where((lane < V_DIM) == (hh == 0), acc, 0.0) / l
            lse_ref[hh] = jnp.broadcast_to(m * QK_SCALE2 + jnp.log(l) * LOG2E, (TQ, LANE))
        o_ref[...] = out

    return pl.pallas_call(
        body, name="attn_fwd", grid=(NH // 2, s // TQ),
        in_specs=[pl.BlockSpec((2, TQ, LANE), lambda p, i: (p, i, 0)), pl.BlockSpec((2, s, LANE), lambda p, i: (p, 0, 0)),
                  pl.BlockSpec((2, s, LANE), lambda p, i: (p, 0, 0))],
        out_specs=[pl.BlockSpec((TQ, LANE), lambda p, i: (i, p)), pl.BlockSpec((2, TQ, LANE), lambda p, i: (p, i, 0))],
        out_shape=[jax.ShapeDtypeStruct((s, NH * V_DIM), F32), jax.ShapeDtypeStruct((NH, s, LANE), F32)],
        compiler_params=_params(("parallel", "parallel")),
    )(q, k, vv)


def _lower_bound(lbl):
    a0, a1 = lbl[0:1, :], lbl[1:2, :]
    mx = jnp.maximum(a0, a1)
    e0, e1 = jnp.exp(a0 - mx), jnp.exp(a1 - mx)
    return e0 / (e0 + e1)


def _chunk_cumsum(v, reverse=False):
    pos = lax.broadcasted_iota(jnp.int32, v.shape, 0) % HG_BLOCK
    s = 1
    while s < HG_BLOCK:
        if reverse:
            v = v + jnp.where(pos < HG_BLOCK - s, pltpu.roll(v, TH - s, 0), 0.0)
        else:
            v = v + jnp.where(pos >= s, pltpu.roll(v, s, 0), 0.0)
        s *= 2
    return v


def _hgrn_gates(hq, hf, lb):
    sig = _sigmoid(hf)
    f = lb + (1.0 - lb) * sig
    g = jnp.log(f)
    kk = 1.0 - f
    r = lax.broadcasted_iota(jnp.int32, (TH, TH), 0)
    c = lax.broadcasted_iota(jnp.int32, (TH, TH), 1)
    tri = ((r // HG_BLOCK) == (c // HG_BLOCK)) & (r >= c)
    cum = _chunk_cumsum(g)
    nch = TH // HG_BLOCK
    total = _chunks(cum)[:, HG_BLOCK - 1:HG_BLOCK, :]
    lastb = jnp.broadcast_to(total, (nch, HG_BLOCK, LANE)).reshape(TH, LANE)
    e, ei, ee = jnp.exp(cum), jnp.exp(-cum), jnp.exp(lastb - cum)
    return dict(sig=sig, f=f, kk=kk, tri=tri, cum=cum, total=total, e=e, ei=ei, ee=ee, qd=hq * e, ki=kk * ei, ke=kk * ee)


def _chunks(v):
    return v.reshape(TH // HG_BLOCK, HG_BLOCK, v.shape[-1])


def _bmm_nt(a, b):
    return lax.dot_general(a, b, (((2,), (2,)), ((0,), (0,))), preferred_element_type=F32)


def _bmm_nn(a, b):
    return lax.dot_general(a, b, (((2,), (1,)), ((0,), (0,))), preferred_element_type=F32)


def _bmm_tn(a, b):
    return lax.dot_general(a, b, (((1,), (1,)), ((0,), (0,))), preferred_element_type=F32)


def _pair_masks():
    lane = lax.broadcasted_iota(jnp.int32, (TH, LANE), 1)
    kr = lax.broadcasted_iota(jnp.int32, (LANE, LANE), 0)
    kc = lax.broadcasted_iota(jnp.int32, (LANE, LANE), 1)
    return lane < 64, (kr // 64) == (kc // 64)


def _hgrn_fwd(proj, lbl):
    s = proj.shape[0]
    nch = TH // HG_BLOCK

    def body(hq_ref, hf_ref, hi_ref, lbl_ref, o_ref, st_ref, st):
        @pl.when(pl.program_id(1) == 0)
        def _():
            st[...] = jnp.zeros_like(st)

        m0, bd = _pair_masks()
        for u in range(HG_PAIRS):
            lanes = slice(u * LANE, (u + 1) * LANE)
            lb = _lower_bound(lbl_ref[:, lanes])
            gt = _hgrn_gates(hq_ref[:, lanes], hf_ref[:, lanes], lb)
            v_b = hi_ref[:, lanes].astype(BF16)
            qd, ki_b, ke_b = gt["qd"], gt["ki"].astype(BF16), gt["ke"].astype(BF16)
            qd_b = qd.astype(BF16)
            o = jnp.zeros((TH, LANE), F32)
            for hh in range(2):
                mh = m0 if hh == 0 else jnp.logical_not(m0)
                a = jnp.where(gt["tri"], _mm_nt(jnp.where(mh, qd, 0.0).astype(BF16), ki_b), 0.0)
                o = jnp.where(mh, _mm(a.astype(BF16), v_b), o)
            upd = _bmm_tn(_chunks(v_b), _chunks(ke_b))
            decay = jnp.exp(gt["total"])
            cur, entering = st[u], []
            for n in range(nch):
                entering.append(cur)
                cur = decay[n] * cur + jnp.where(bd, upd[n], 0.0)
            st[u] = cur
            entering = jnp.stack(entering)
            st_ref[u] = entering
            o_ref[:, lanes] = o + _bmm_nt(_chunks(qd_b), entering.astype(BF16)).reshape(TH, LANE)

    wide = HG_PAIRS * LANE
    col = lambda base: pl.BlockSpec((TH, wide), lambda p, i: (i, base // wide + p))
    return pl.pallas_call(
        body, name="hgrn_fwd", grid=(NH // 2 // HG_PAIRS, s // TH),
        in_specs=[col(P_HQ), col(P_HF), col(P_HI), pl.BlockSpec((2, wide), lambda p, i: (0, p))],
        out_specs=[pl.BlockSpec((TH, wide), lambda p, i: (i, p)),
                   pl.BlockSpec((HG_PAIRS, nch, LANE, LANE), lambda p, i: (p, i, 0, 0))],
        out_shape=[jax.ShapeDtypeStruct((s, 512), F32), jax.ShapeDtypeStruct((NH // 2, s // HG_BLOCK, LANE, LANE), F32)],
        scratch_shapes=[pltpu.VMEM((HG_PAIRS, LANE, LANE), F32)],
        compiler_params=_params(("parallel", "arbitrary")),
    )(proj, proj, proj, lbl)


def _group_sum(v):
    low = lax.broadcasted_iota(jnp.int32, (v.shape[0], LANE), 1) < V_DIM
    blocks = []
    for b in range(v.shape[1] // LANE):
        blk = v[:, b * LANE:(b + 1) * LANE]
        s_low = jnp.sum(jnp.where(low, blk, 0.0), axis=-1, keepdims=True)
        s_high = jnp.sum(jnp.where(low, 0.0, blk), axis=-1, keepdims=True)
        blocks.append(jnp.where(low, s_low, s_high))
    return jnp.concatenate(blocks, axis=1)


def _dsilu(z, sg):
    return sg * (1.0 + z * (1.0 - sg))


def _mid(proj, attn, o_raw, x, tgt, g_hg, b_gate, g_post, wa, wb, w_out):
    s = x.shape[0]

    def body(attn_ref, ga_ref, o_ref, gb_ref, mg_ref, x_ref, t_ref, ghg_ref, bg_ref, gp_ref, wa_ref, wb_ref, wo_ref,
             loss_ref, dout_ref, dattn_ref, dga_ref, dor_ref, dgb_ref, dmg_ref, dwo_ref, dwa_ref, dwb_ref, dgp_ref, dbg_ref, dghg_ref):
        first = pl.program_id(0) == 0

        @pl.when(first)
        def _():
            for rf in (loss_ref, dwo_ref, dwa_ref, dwb_ref, dgp_ref, dbg_ref, dghg_ref):
                rf[...] = jnp.zeros_like(rf)

        attn, za, orw, zb = attn_ref[...], ga_ref[...], o_ref[...], gb_ref[...]
        ghg, gp = ghg_ref[...], gp_ref[...]
        sga, sgb = _sigmoid(za), _sigmoid(zb)
        sa, sb = za * sga, zb * sgb
        ga = attn * sa
        rh = lax.rsqrt(_group_sum(orw * orw) * (1.0 / V_DIM) + EPS)
        on = (orw * rh) * ghg
        gb = on * sb
        ga_b, gb_b = ga.astype(BF16), gb.astype(BF16)
        ya = _mm(ga_b, wa_ref[...])
        yb = _mm(gb_b, wb_ref[...])
        gates = _sigmoid(mg_ref[...] + bg_ref[...])
        g0, g1 = gates[:, :D], gates[:, D:]
        m_b = (g0 * ya + g1 * yb).astype(BF16)
        y = _mm(m_b, wo_ref[...])
        ry = lax.rsqrt(jnp.mean(y * y, axis=-1, keepdims=True) + EPS)
        out = x_ref[...] + (y * ry) * gp
        err = out - t_ref[...]
        loss_ref[...] += 0.5 * jnp.sum(jnp.mean(err * err, axis=-1, keepdims=True), axis=0, keepdims=True)
        dout = err * (1.0 / D)
        dout_ref[...] = dout
        dgp_ref[...] += jnp.sum(dout * (y * ry), axis=0, keepdims=True)
        dgy = dout * gp
        dy = ry * dgy - y * (ry * ry * ry) * jnp.mean(y * dgy, axis=-1, keepdims=True)
        dy_b = dy.astype(BF16)
        dwo_ref[...] += _mm_tn(m_b, dy_b)
        dm = _mm_nt(dy_b, wo_ref[...])
        dya, dyb = dm * g0, dm * g1
        dg0, dg1 = dm * ya, dm * yb
        dmg = jnp.concatenate([dg0 * g0 * (1.0 - g0), dg1 * g1 * (1.0 - g1)], axis=1)
        dmg_ref[...] = dmg.astype(BF16)
        dbg_ref[...] += jnp.sum(dmg, axis=0, keepdims=True)
        dya_b, dyb_b = dya.astype(BF16), dyb.astype(BF16)
        dwa_ref[...] += _mm_tn(ga_b, dya_b)
        dwb_ref[...] += _mm_tn(gb_b, dyb_b)
        dga = _mm_nt(dya_b, wa_ref[...])
        dgb = _mm_nt(dyb_b, wb_ref[...])
        dattn_ref[...] = dga * sa
        dga_ref[...] = (dga * attn * _dsilu(za, sga)).astype(BF16)
        dgb_ref[...] = (dgb * on * _dsilu(zb, sgb)).astype(BF16)
        don = dgb * sb
        dghg_ref[...] += jnp.sum(don * (orw * rh), axis=0, keepdims=True)
        dgo = don * ghg
        dor_ref[...] = rh * dgo - orw * (rh * rh * rh) * (_group_sum(orw * dgo) * (1.0 / V_DIM))

    row = lambda w, j=0: pl.BlockSpec((TM, w), lambda i: (i, j))
    full = lambda a: pl.BlockSpec(a.shape, lambda i: (0,) * a.ndim)
    acc = lambda shape: pl.BlockSpec(shape, lambda i: (0, 0))
    sds = jax.ShapeDtypeStruct
    return pl.pallas_call(
        body, name="mid", grid=(s // TM,),
        in_specs=[row(512), row(512, P_GA // 512), row(512), row(512, P_GB // 512), row(2048, P_MERGE // 2048), row(D), row(D),
                  full(g_hg), full(b_gate), full(g_post), full(wa), full(wb), full(w_out)],
        out_specs=[acc((1, 1)), row(D), row(512), row(512), row(512), row(512), row(2048),
                   acc((D, D)), acc((512, D)), acc((512, D)), acc((1, D)), acc((1, 2048)), acc((1, 512))],
        out_shape=[sds((1, 1), F32), sds((s, D), F32), sds((s, 512), F32), sds((s, 512), BF16), sds((s, 512), F32), sds((s, 512), BF16),
                   sds((s, 2048), BF16), sds((D, D), F32), sds((512, D), F32), sds((512, D), F32), sds((1, D), F32),
                   sds((1, 2048), F32), sds((1, 512), F32)],
        compiler_params=_params(("arbitrary",)),
    )(attn, proj, o_raw, proj, proj, x, tgt, g_hg, b_gate, g_post, wa, wb, w_out)


def _attn_bwd(q, k, vv, attn, dattn, lse, token):
    s = q.shape[1]
    nt = s // TQ
    scale = 1.0 / math.sqrt(QK_NOPE + QK_ROPE)

    def body(q_ref, k_ref, v_ref, o_ref, do_ref, lse_ref, token_ref, dq_ref, dk_ref, dv_ref, do_s, delta_s):
        j = pl.program_id(1)

        @pl.when(j == 0)
        def _():
            dq_ref[...] = jnp.zeros_like(dq_ref)
            lane = lax.broadcasted_iota(jnp.int32, (TQ, LANE), 1)

            @pl.loop(0, nt)
            def _(i):
                rows = pl.ds(pl.multiple_of(i * TQ, TQ), TQ)
                do, o = do_ref[rows, :], o_ref[rows, :]
                for hh in range(2):
                    doh = jnp.where((lane < 64) if hh == 0 else (lane >= 64), do, 0.0)
                    do_s[hh, rows, :] = doh.astype(BF16)
                    delta_s[hh, rows, :] = jnp.broadcast_to(jnp.sum(doh * o, axis=-1, keepdims=True), (TQ, LANE))

        kjs, vjs = (k_ref[0], k_ref[1]), (v_ref[0], v_ref[1])
        wide = lambda a: jnp.concatenate([a] * (TQ // LANE), axis=1)

        def tile(hh, i, carry, diag):
            dk, dv = carry
            rows = pl.ds(pl.multiple_of(i * TQ, TQ), TQ)
            qi, do_b = q_ref[hh, rows, :], do_s[hh, rows, :]
            p = jnp.exp2(_mm_nt(qi, kjs[hh]) * QK_SCALE2 - wide(lse_ref[hh, rows, :]))
            if diag:
                p = jnp.where(_diag_visible(), p, 0.0)
            dv = dv + _mm_tn(do_b, p.astype(BF16))
            ds_b = (p * (_mm_nt(do_b, vjs[hh]) - wide(delta_s[hh, rows, :]))).astype(BF16)
            dk = dk + _mm_tn(qi, ds_b)
            dq_ref[hh, rows, :] += _mm(ds_b, kjs[hh])
            return dk, dv

        def step(i, carry):
            return tile(0, i, carry[0], False), tile(1, i, carry[1], False)

        z = jnp.zeros((LANE, TQ), F32)
        first = (tile(0, j, (z, z), True), tile(1, j, (z, z), True))
        carry = lax.fori_loop(j + 1, nt, step, first)
        for hh in range(2):
            dk_ref[hh] = carry[hh][0].T * scale
            dv_ref[hh] = carry[hh][1].T

        @pl.when(j == nt - 1)
        def _():
            dq_ref[...] = dq_ref[...] * scale

    whole = pl.BlockSpec((2, s, LANE), lambda p, j: (p, 0, 0))
    tile_spec = pl.BlockSpec((2, TQ, LANE), lambda p, j: (p, j, 0))
    cols = pl.BlockSpec((s, LANE), lambda p, j: (0, p))
    hs = jax.ShapeDtypeStruct((NH, s, LANE), F32)
    return pl.pallas_call(
        body, name="attn_bwd", grid=(NH // 2, nt),
        in_specs=[whole, tile_spec, tile_spec, cols, cols, whole, pl.BlockSpec((8, LANE), lambda p, j: (0, 0))],
        out_specs=[whole, tile_spec, tile_spec],
        out_shape=[hs, hs, hs],
        scratch_shapes=[pltpu.VMEM((2, s, LANE), BF16), pltpu.VMEM((2, s, LANE), F32)],
        compiler_params=_params(("parallel", "arbitrary")),
    )(q, k, vv, attn, dattn, lse, token)


def _hgrn_bwd(proj, lbl, states, do_raw):
    s = proj.shape[0]
    nt = s // TH
    nch = TH // HG_BLOCK

    def body(hq_ref, hf_ref, hi_ref, lbl_ref, st_ref, do_ref, dh_ref, dlbl_ref, dst, dlb):
        step = pl.program_id(1)

        @pl.when(step == 0)
        def _():
            dst[...] = jnp.zeros_like(dst)
            dlb[...] = jnp.zeros_like(dlb)

        m0, bd = _pair_masks()
        for u in range(HG_PAIRS):
            lanes = slice(u * LANE, (u + 1) * LANE)
            lb = _lower_bound(lbl_ref[:, lanes])
            gt = _hgrn_gates(hq_ref[:, lanes], hf_ref[:, lanes], lb)
            do = do_ref[:, lanes]
            qd, ki, ke = gt["qd"], gt["ki"], gt["ke"]
            v_b, do_b = hi_ref[:, lanes].astype(BF16), do.astype(BF16)
            qd_b, ki_b, ke_b = qd.astype(BF16), ki.astype(BF16), ke.astype(BF16)
            dv = jnp.zeros((TH, LANE), F32)
            dqd = jnp.zeros((TH, LANE), F32)
            dki = jnp.zeros((TH, LANE), F32)
            for hh in range(2):
                mh = m0 if hh == 0 else jnp.logical_not(m0)
                a_b = jnp.where(gt["tri"], _mm_nt(jnp.where(mh, qd, 0.0).astype(BF16), ki_b), 0.0).astype(BF16)
                doh_b = jnp.where(mh, do, 0.0).astype(BF16)
                da_b = jnp.where(gt["tri"], _mm_nt(doh_b, v_b), 0.0).astype(BF16)
                dv = dv + _mm_tn(a_b, doh_b)
                dqd = jnp.where(mh, _mm(da_b, ki_b), dqd)
                dki = jnp.where(mh, _mm_tn(da_b, qd_b), dki)
            fed = _bmm_tn(_chunks(do_b), _chunks(qd_b))
            decay = jnp.exp(gt["total"])
            ds, leaving = dst[u], [None] * nch
            for n in reversed(range(nch)):
                leaving[n] = ds
                ds = decay[n] * ds + jnp.where(bd, fed[n], 0.0)
            dst[u] = ds
            leaving = jnp.stack(leaving)
            entering = st_ref[u]
            leaving_b = leaving.astype(BF16)
            dke3 = _bmm_nn(_chunks(v_b), leaving_b)
            dv = dv + _bmm_nt(_chunks(ke_b), leaving_b).reshape(TH, LANE)
            dqd = dqd + _bmm_nn(_chunks(do_b), entering.astype(BF16)).reshape(TH, LANE)
            dke = dke3.reshape(TH, LANE)
            dlast = (jnp.sum(dke3 * _chunks(ke), axis=1, keepdims=True)
                     + jnp.sum(leaving * entering, axis=1, keepdims=True) * decay)
            dk = dki * gt["ei"] + dke * gt["ee"]
            dcum = dqd * qd - dki * ki - dke * ke
            dg = _chunk_cumsum(dcum, reverse=True) + jnp.broadcast_to(dlast, (nch, HG_BLOCK, LANE)).reshape(TH, LANE)
            sig = gt["sig"]
            df = dg / gt["f"] - dk
            dlb[:, lanes] += jnp.sum(df * (1.0 - sig), axis=0, keepdims=True)
            dh_ref[0, :, lanes] = (dqd * gt["e"]).astype(BF16)
            dh_ref[1, :, lanes] = ((df * (1.0 - lb)) * sig * (1.0 - sig)).astype(BF16)
            dh_ref[2, :, lanes] = dv.astype(BF16)

        @pl.when(step == nt - 1)
        def _():
            lb = _lower_bound(lbl_ref[...])
            da0 = dlb[...] * lb * (1.0 - lb)
            dlbl_ref[...] = jnp.concatenate([da0, -da0], axis=0)

    wide = HG_PAIRS * LANE
    col = lambda base: pl.BlockSpec((TH, wide), lambda p, i: (nt - 1 - i, base // wide + p))
    tile = pl.BlockSpec((TH, wide), lambda p, i: (nt - 1 - i, p))
    sds = jax.ShapeDtypeStruct
    return pl.pallas_call(
        body, name="hgrn_bwd", grid=(NH // 2 // HG_PAIRS, nt),
        in_specs=[col(P_HQ), col(P_HF), col(P_HI), pl.BlockSpec((2, wide), lambda p, i: (0, p)),
                  pl.BlockSpec((HG_PAIRS, nch, LANE, LANE), lambda p, i: (p, nt - 1 - i, 0, 0)), tile],
        out_specs=[pl.BlockSpec((3, TH, wide), lambda p, i: (0, nt - 1 - i, p)), pl.BlockSpec((2, wide), lambda p, i: (0, p))],
        out_shape=[sds((3, s, 512), BF16), sds((2, 512), F32)],
        scratch_shapes=[pltpu.VMEM((HG_PAIRS, LANE, LANE), F32), pltpu.VMEM((1, wide), F32)],
        compiler_params=_params(("parallel", "arbitrary")),
    )(proj, proj, proj, lbl, states, do_raw)


def _norm_rows_bwd(v, r, g, dn):
    dgv = dn * g
    return r * dgv - v * (r * r * r) * jnp.mean(v * dgv, axis=-1, keepdims=True)


def _qkv_bwd(proj, dq, dk, dvv, g_q, g_kv, w_uq_p, w_k_p, w_v_p, rc, rs1, rs2):
    s = proj.shape[0]

    def body(cq_ref, ckv_ref, dq_ref, dk_ref, dv_ref, gq_ref, gkv_ref, wq_ref, wk_ref, wv_ref, c_ref, s1_ref, s2_ref,
             dcq_ref, dckv_ref, dkpe_ref, dwq_ref, dwk_ref, dwv_ref, dgq_ref, dgkv_ref):
        @pl.when(pl.program_id(0) == 0)
        def _():
            for rf in (dwq_ref, dwk_ref, dwv_ref, dgq_ref, dgkv_ref):
                rf[...] = jnp.zeros_like(rf)

        c, s1, s2 = c_ref[...], s1_ref[...], s2_ref[...]
        cq, ckv = cq_ref[...], ckv_ref[...]
        gq, gkv = gq_ref[...], gkv_ref[...]
        cqn, rq = _norm_rows(cq, gq)
        ckvn, rkv = _norm_rows(ckv, gkv)
        cqn_b, ckvn_b = cqn.astype(BF16), ckvn.astype(BF16)
        dqf = jnp.concatenate([_rope_t(dq_ref[h], c, s1, s2) for h in range(NH)], axis=1).astype(BF16)
        dkf = jnp.concatenate([dk_ref[h] for h in range(NH)], axis=1).astype(BF16)
        dvf = jnp.concatenate([dv_ref[h] for h in range(NH)], axis=1).astype(BF16)
        dkpe = dk_ref[0]
        for h in range(1, NH):
            dkpe = dkpe + dk_ref[h]
        lane = lax.broadcasted_iota(jnp.int32, (TM, LANE), 1)
        dkpe = jnp.where((lane >= QK_NOPE) & (lane < QK_NOPE + QK_ROPE), dkpe, 0.0)
        dkpe_ref[...] = _rope_t(dkpe, c, s1, s2).astype(BF16)
        dwq_ref[...] += _mm_tn(cqn_b, dqf)
        dwk_ref[...] += _mm_tn(ckvn_b, dkf)
        dwv_ref[...] += _mm_tn(ckvn_b, dvf)
        dcqn = _mm_nt(dqf, wq_ref[...])
        dckvn = _mm_nt(dkf, wk_ref[...]) + _mm_nt(dvf, wv_ref[...])
        dgq_ref[...] += jnp.sum(dcqn * (cq * rq), axis=0, keepdims=True)
        dgkv_ref[...] += jnp.sum(dckvn * (ckv * rkv), axis=0, keepdims=True)
        dcq_ref[...] = _norm_rows_bwd(cq, rq, gq, dcqn).astype(BF16)
        dckv_ref[...] = _norm_rows_bwd(ckv, rkv, gkv, dckvn).astype(BF16)

    row = lambda w, j=0: pl.BlockSpec((TM, w), lambda i: (i, j))
    full = lambda a: pl.BlockSpec(a.shape, lambda i: (0,) * a.ndim)
    acc = lambda shape: pl.BlockSpec(shape, lambda i: (0, 0))
    heads = pl.BlockSpec((NH, TM, LANE), lambda i: (0, i, 0))
    sds = jax.ShapeDtypeStruct
    return pl.pallas_call(
        body, name="qkv_bwd", grid=(s // TM,),
        in_specs=[row(Q_LORA, P_CQ // Q_LORA), row(KV_LORA, P_CKV // KV_LORA), heads, heads, heads,
                  full(g_q), full(g_kv), full(w_uq_p), full(w_k_p), full(w_v_p), row(LANE), row(LANE), row(LANE)],
        out_specs=[row(Q_LORA), row(KV_LORA), row(LANE), acc((Q_LORA, D)), acc((KV_LORA, D)), acc((KV_LORA, D)),
                   acc((1, Q_LORA)), acc((1, KV_LORA))],
        out_shape=[sds((s, Q_LORA), BF16), sds((s, KV_LORA), BF16), sds((s, LANE), BF16), sds((Q_LORA, D), F32),
                   sds((KV_LORA, D), F32), sds((KV_LORA, D), F32), sds((1, Q_LORA), F32), sds((1, KV_LORA), F32)],
        compiler_params=_params(("arbitrary",)),
    )(proj, proj, dq, dk, dvv, g_q, g_kv, w_uq_p, w_k_p, w_v_p, rc, rs1, rs2)


def _front_bwd(x, dout, dmg, dga, dh3, dgb, dcq, dckv, dkpe, g_pre, w_in_pt, token):
    s = x.shape[0]

    def body(x_ref, do_ref, dmg_ref, dga_ref, dh3_ref, dgb_ref, dcq_ref, dckv_ref, dkpe_ref, g_ref, w_ref, token_ref, gx_ref, dg_ref):
        @pl.when(pl.program_id(0) == 0)
        def _():
            dg_ref[...] = jnp.zeros_like(dg_ref)

        xv, g = x_ref[...], g_ref[...]
        _, r = _norm_rows(xv, g)
        pieces = ((dmg_ref[...], P_MERGE), (dga_ref[...], P_GA), (dh3_ref[0], P_HQ), (dh3_ref[1], P_HF), (dh3_ref[2], P_HI),
                  (dgb_ref[...], P_GB), (dcq_ref[...], P_CQ), (dckv_ref[...], P_CKV), (dkpe_ref[...], P_KPE))
        dh = jnp.zeros((TM, D), F32)
        for piece, off in pieces:
            dh = dh + _mm(piece, w_ref[off:off + piece.shape[1], :])
        dg_ref[...] += jnp.sum(dh * (xv * r), axis=0, keepdims=True)
        gx_ref[...] = do_ref[...] + _norm_rows_bwd(xv, r, g, dh)

    row = lambda w: pl.BlockSpec((TM, w), lambda i: (i, 0))
    full = lambda a: pl.BlockSpec(a.shape, lambda i: (0,) * a.ndim)
    sds = jax.ShapeDtypeStruct
    return pl.pallas_call(
        body, name="front_bwd", grid=(s // TM,),
        in_specs=[row(D), row(D), row(2048), row(512), pl.BlockSpec((3, TM, 512), lambda i: (0, i, 0)), row(512), row(Q_LORA),
                  row(KV_LORA), row(LANE), full(g_pre), full(w_in_pt), full(token)],
        out_specs=[row(D), pl.BlockSpec((1, D), lambda i: (0, 0))],
        out_shape=[sds((s, D), F32), sds((1, D), F32)],
        compiler_params=_params(("arbitrary",)),
    )(x, dout, dmg, dga, dh3, dgb, dcq, dckv, dkpe, g_pre, w_in_pt, token)


TK_GRAD = 512


def _win_grad(h, pieces, name):
    s = h.shape[0]
    n = len(pieces)

    def body(h_ref, *refs):
        @pl.when(pl.program_id(0) == 0)
        def _():
            for o_ref in refs[n:]:
                o_ref[...] = jnp.zeros_like(o_ref)

        hv = h_ref[...]
        for d_ref, o_ref in zip(refs[:n], refs[n:]):
            if len(d_ref.shape) == 3:
                for k in range(d_ref.shape[0]):
                    o_ref[k] += _mm_tn(d_ref[k], hv)
            else:
                o_ref[...] += _mm_tn(d_ref[...], hv)

    def in_spec(p):
        if p.ndim == 3:
            return pl.BlockSpec((p.shape[0], TK_GRAD, p.shape[2]), lambda kk: (0, kk, 0))
        return pl.BlockSpec((TK_GRAD, p.shape[1]), lambda kk: (kk, 0))

    out_shapes = [(p.shape[0], p.shape[2], D) if p.ndim == 3 else (p.shape[1], D) for p in pieces]
    return pl.pallas_call(
        body, name=name, grid=(s // TK_GRAD,),
        in_specs=[pl.BlockSpec((TK_GRAD, D), lambda kk: (kk, 0))] + [in_spec(p) for p in pieces],
        out_specs=[pl.BlockSpec(sh, lambda kk, nd=len(sh): (0,) * nd) for sh in out_shapes],
        out_shape=[jax.ShapeDtypeStruct(sh, F32) for sh in out_shapes],
        compiler_params=_params(("arbitrary",)),
    )(h, *pieces)


def _pad_win_t(w_in_t):
    z = lambda n: jnp.zeros((n, w_in_t.shape[1]), w_in_t.dtype)
    sl = lambda o, n: w_in_t[o:o + n]
    return jnp.concatenate([sl(O_MERGE, 2048), sl(O_GA, 512), sl(O_HQ, 512), sl(O_HF, 512), sl(O_HI, 512), sl(O_GB, 512),
                            sl(O_CQ, Q_LORA), sl(O_CKV, KV_LORA), z(64), sl(O_KPE, QK_ROPE), z(32)], axis=0)


def _pad_wuq(w_uq):
    w = w_uq.reshape(Q_LORA, NH, QK_NOPE + QK_ROPE)
    return jnp.pad(w, ((0, 0), (0, 0), (0, LANE - QK_NOPE - QK_ROPE))).reshape(Q_LORA, NH * LANE)


def _unpad_wuq(g):
    return g.reshape(Q_LORA, NH, LANE)[:, :, :QK_NOPE + QK_ROPE].reshape(Q_LORA, NH * (QK_NOPE + QK_ROPE))


def _pad_wukv(w_ukv):
    w = w_ukv.reshape(KV_LORA, NH, QK_NOPE + V_DIM)
    w_k = jnp.pad(w[:, :, :QK_NOPE], ((0, 0), (0, 0), (0, LANE - QK_NOPE))).reshape(KV_LORA, NH * LANE)
    wv = w[:, :, QK_NOPE:].reshape(KV_LORA, NH // 2, 2, 1, V_DIM)
    eye = jnp.eye(2, dtype=w.dtype).reshape(1, 1, 2, 2, 1)
    return w_k, (wv * eye).reshape(KV_LORA, NH * LANE)


def _unpad_wukv(gk, gv):
    gk = gk.reshape(KV_LORA, NH, LANE)[:, :, :QK_NOPE]
    gv = gv.reshape(KV_LORA, NH // 2, 2, 2, V_DIM)
    gv = jnp.stack([gv[:, :, 0, 0], gv[:, :, 1, 1]], axis=2).reshape(KV_LORA, NH, V_DIM)
    return jnp.concatenate([gk, gv], axis=-1).reshape(KV_LORA, NH * (QK_NOPE + V_DIM))


def _local_step(x, tgt, g_pre, w_in_t, b_gate, g_q, g_kv, lb_logits, g_hgrn, g_post, weights, exchange=None):
    s = x.shape[0]
    w_in_p = _pad_win_t(w_in_t)
    rc, rs1, rs2 = _rope_tables(s)
    g_hg = jnp.tile(g_hgrn, (1, NH))

    proj, h = _front_fwd(x, g_pre, w_in_p, weights.tokens)
    w_uq, w_ukv = weights.qkv(h)
    w_uq_p = _pad_wuq(w_uq)
    w_k_p, w_v_p = _pad_wukv(w_ukv)
    q, k, vv = _qkv_fwd(proj, g_q, g_kv, w_uq_p, w_k_p, w_v_p, rc, rs1, rs2)
    attn, lse = _attn_fwd(q, k, vv)
    o_raw, states = _hgrn_fwd(proj, lb_logits)
    wa, wb, w_out = weights.mid(o_raw)
    (loss, dout, dattn, dga, dor, dgb, dmg, d_wout, d_wa, d_wb, d_gpost, d_bgate, d_ghg) = _mid(
        proj, attn, o_raw, x, tgt, g_hg, b_gate, g_post, wa, wb, w_out)
    w_mg, w_ga, w_gb = _win_grad(h, [dmg, dga, dgb], "win_grad_mid")
    dh3, d_lbl = _hgrn_bwd(proj, lb_logits, states, dor)
    (w_h3,) = _win_grad(h, [dh3], "win_grad_hgrn")
    d_win_rest = jnp.concatenate([w_ga, w_h3[0], w_h3[1], w_h3[2], w_gb, w_mg], axis=0)
    early = dict(w_in_rest=d_win_rest, w_branch_a=d_wa, w_branch_b=d_wb, w_out=d_wout)
    token = exchange.start_early(early) if exchange else jnp.zeros((8, LANE), F32)
    dq, dk, dvv = _attn_bwd(q, k, vv, attn, dattn, lse, token)
    dcq, dckv, dkpe, d_wuq_p, d_wk_p, d_wv_p, d_gq, d_gkv = _qkv_bwd(proj, dq, dk, dvv, g_q, g_kv, w_uq_p, w_k_p, w_v_p, rc, rs1, rs2)
    w_cq, w_ckv, w_kpe = _win_grad(h, [dcq, dckv, dkpe], "win_grad_qkv")
    d_win_qkv = jnp.concatenate([w_cq, w_ckv, w_kpe[64:64 + QK_ROPE]], axis=0)
    late = dict(w_in_qkv=d_win_qkv, w_uq=_unpad_wuq(d_wuq_p), w_ukv=_unpad_wukv(d_wk_p, d_wv_p))
    token = exchange.start_late(late) if exchange else jnp.zeros((8, LANE), F32)
    grad_x, d_gpre = _front_bwd(x, dout, dmg, dga, dh3, dgb, dcq, dckv, dkpe, g_pre, w_in_p, token)
    vec_grads = dict(g_pre=d_gpre, b_gate=d_bgate, g_q=d_gq, g_kv=d_gkv, lb_logits=d_lbl, g_hgrn=d_ghg, g_post=d_gpost)
    return loss, grad_x, dict(early, **late), vec_grads


SHARD_SHAPES = (("w_in", (1416, 1024)), ("w_uq", (192, 768)), ("w_ukv", (256, 256)), ("w_branch_a", (512, 256)),
                ("w_branch_b", (512, 256)), ("w_out", (256, 1024)))
BIG = tuple(n for n, _ in SHARD_SHAPES)
ROW_SHARDED = ("w_in", "w_uq", "w_out")
GATHER_SPLIT_AXIS = dict(w_in=1, w_uq=0, w_ukv=0, w_branch_a=0, w_branch_b=0, w_out=0)
N_CHIPS = 4
QKV_ROWS = Q_LORA + KV_LORA + QK_ROPE


def _to_block(name, a):
    return a[0].T if name == "w_in" else a[0]


def _from_block(name, a):
    return a.T[None] if name == "w_in" else a[None]
VEC_ROWS = (("g_pre", 0, 1024), ("b_gate", 1, 2048), ("g_q", 2, 768), ("g_kv", 3, 256), ("g_hgrn", 6, 64), ("g_post", 7, 1024))
VEC_LB_ROW = 4
VEC_SHAPE = (8, 2048)


def _split_by_chip(name, g):
    a, b = dict(SHARD_SHAPES)[name]
    return g.reshape(N_CHIPS, a, b) if name in ROW_SHARDED else g.reshape(a, N_CHIPS, b).transpose(1, 0, 2)


def _join_chips(name, w):
    a, b = dict(SHARD_SHAPES)[name]
    return w.reshape(N_CHIPS * a, b) if name in ROW_SHARDED else w.transpose(1, 0, 2).reshape(a, N_CHIPS * b)


MESH = pl.DeviceIdType.MESH
HBM = pl.BlockSpec(memory_space=pltpu.HBM)


def _mesh_place():
    x, y, c = lax.axis_index("x"), lax.axis_index("y"), lax.axis_index("c")
    return x, y, c, 2 * x + y, [(1 - x, y), (x, 1 - y), (1 - x, 1 - y)]


def _remote(src, dst, send_sems, recv_sems, k, to):
    return pltpu.make_async_remote_copy(src_ref=src, dst_ref=dst, send_sem=send_sems.at[k], recv_sem=recv_sems.at[k],
                                        device_id=to, device_id_type=MESH)


def _gather_weights(shards, split_axes):
    n = len(shards)

    def body(*refs):
        srcs, outs = refs[:n], refs[n:2 * n]
        ici_send, ici_recv, d2d_send, d2d_recv, local_sems = refs[2 * n:]
        x, y, c, me, chips = _mesh_place()
        sibling = (x, y, 1 - c)

        def half(ref, k, which):
            size = shards[k].shape[split_axes[k]] // 2
            part = pl.ds(pl.multiple_of(which * size, size), size)
            return ref.at[part] if split_axes[k] == 0 else ref.at[:, part]

        own = [pltpu.make_async_copy(srcs[k], outs[k].at[me], local_sems.at[k]) for k in range(n)]
        for cp in own:
            cp.start()
        started = []
        for k in range(n):
            for j, (px, py) in enumerate(chips):
                cp = _remote(half(srcs[k], k, c), half(outs[k].at[me], k, c), ici_send, ici_recv, 3 * k + j, (px, py, c))
                cp.start()
                started.append(cp)
        for k in range(n):
            for j, (px, py) in enumerate(chips):
                landed = half(outs[k].at[2 * px + py], k, c)
                _remote(landed, landed, ici_send, ici_recv, 3 * k + j, (px, py, c)).wait_recv()
                cp = _remote(landed, landed, d2d_send, d2d_recv, 3 * k + j, sibling)
                cp.start()
                started.append(cp)
        for k in range(n):
            for j, (px, py) in enumerate(chips):
                other = half(outs[k].at[2 * px + py], k, 1 - c)
                _remote(other, other, d2d_send, d2d_recv, 3 * k + j, sibling).wait_recv()
        for cp in started:
            cp.wait_send()
        for cp in own:
            cp.wait()

    sems = pltpu.SemaphoreType.DMA((3 * n,))
    return pl.pallas_call(
        body, name="gather_weights", in_specs=[HBM] * n, out_specs=[HBM] * n,
        out_shape=[jax.ShapeDtypeStruct((N_CHIPS,) + s.shape, s.dtype) for s in shards],
        scratch_shapes=[sems, sems, sems, sems, pltpu.SemaphoreType.DMA((n,))],
        compiler_params=pltpu.CompilerParams(has_side_effects=True),
    )(*shards)


def _chip_exchange(srcs, name):
    n = len(srcs)

    def body(*refs):
        src_refs, outs = refs[:n], refs[n:2 * n]
        send_sems, recv_sems, local_sems = refs[2 * n:]
        x, y, c, me, chips = _mesh_place()

        def slab(k, t):
            return src_refs[k] if srcs[k].ndim == 2 else src_refs[k].at[t]

        own = [pltpu.make_async_copy(slab(k, me), outs[k].at[me], local_sems.at[k]) for k in range(n)]
        for cp in own:
            cp.start()
        sends = []
        for k in range(n):
            for j, (px, py) in enumerate(chips):
                cp = _remote(slab(k, 2 * px + py), outs[k].at[me], send_sems, recv_sems, 3 * k + j, (px, py, c))
                cp.start()
                sends.append(cp)
        for k in range(n):
            for j, (px, py) in enumerate(chips):
                _remote(slab(k, me), outs[k].at[2 * px + py], send_sems, recv_sems, 3 * k + j, (px, py, c)).wait_recv()
        for cp in sends:
            cp.wait_send()
        for cp in own:
            cp.wait()

    sems = pltpu.SemaphoreType.DMA((3 * n,))
    return pl.pallas_call(
        body, name=name, in_specs=[HBM] * n, out_specs=[HBM] * n,
        out_shape=[jax.ShapeDtypeStruct((N_CHIPS,) + s.shape[-2:], s.dtype) for s in srcs],
        scratch_shapes=[sems, sems, pltpu.SemaphoreType.DMA((n,))],
        compiler_params=pltpu.CompilerParams(has_side_effects=True),
    )(*srcs)


def _sibling_exchange(srcs, name):
    n = len(srcs)

    def body(*refs):
        src_refs, outs = refs[:n], refs[n:2 * n]
        send_sems, recv_sems = refs[2 * n:]
        sibling = (lax.axis_index("x"), lax.axis_index("y"), 1 - lax.axis_index("c"))
        copies = [_remote(src_refs[k], outs[k], send_sems, recv_sems, k, sibling) for k in range(n)]
        for cp in copies:
            cp.start()
        for cp in copies:
            cp.wait()

    sems = pltpu.SemaphoreType.DMA((n,))
    return pl.pallas_call(
        body, name=name, in_specs=[HBM] * n, out_specs=[HBM] * n,
        out_shape=[jax.ShapeDtypeStruct(s.shape, s.dtype) for s in srcs],
        scratch_shapes=[sems, sems],
        compiler_params=pltpu.CompilerParams(has_side_effects=True),
    )(*srcs)


SEM = pl.BlockSpec(memory_space=pltpu.SEMAPHORE)
DATAFLOW = pltpu.SideEffectType.DATAFLOW_SIDE_EFFECTING


def _exchange_copies(srcs, to_first, src_refs, land_refs, send_sems, recv_sems):
    x, y, c, me, chips = _mesh_place()
    n = len(srcs)
    sends, recvs = [], []
    for k in range(n):
        if k in to_first:
            base = 3 * n + 4 * to_first.index(k)
            sends.append((me != 0, pltpu.make_async_remote_copy(
                src_ref=src_refs[k], dst_ref=land_refs[k].at[me], send_sem=send_sems.at[base], recv_sem=recv_sems.at[base + me],
                device_id=(0, 0, c), device_id_type=MESH)))
            for s in range(1, N_CHIPS):
                recvs.append((me == 0, pltpu.make_async_remote_copy(
                    src_ref=src_refs[k], dst_ref=land_refs[k].at[s], send_sem=send_sems.at[base], recv_sem=recv_sems.at[base + s],
                    device_id=(s // 2, s % 2, c), device_id_type=MESH)))
        else:
            slab = (lambda t, k=k: src_refs[k]) if srcs[k].ndim == 2 else (lambda t, k=k: src_refs[k].at[t])
            for j, (px, py) in enumerate(chips):
                sends.append((None, _remote(slab(2 * px + py), land_refs[k].at[me], send_sems, recv_sems, 3 * k + j, (px, py, c))))
                recvs.append((None, _remote(slab(me), land_refs[k].at[2 * px + py], send_sems, recv_sems, 3 * k + j, (px, py, c))))
    return sends, recvs


def _when(pred, fn):
    if pred is None:
        fn()
    else:
        pl.when(pred)(fn)


def _exchange_start(srcs, to_first, name, after=None):
    n = len(srcs)
    n_sems = 3 * n + 4 * len(to_first)
    lands = [lax.empty((N_CHIPS,) + s.shape[-2:], s.dtype) for s in srcs]
    extra = [] if after is None else [after]

    def body(*refs):
        src_refs, land_refs = refs[:n], refs[n:2 * n]
        send_sems, recv_sems, token = refs[2 * n + len(extra)], refs[2 * n + len(extra) + 1], refs[-1]
        sends, _ = _exchange_copies(srcs, to_first, src_refs, land_refs, send_sems, recv_sems)
        for pred, cp in sends:
            _when(pred, cp.start)
        token[...] = jnp.zeros_like(token)

    hbm = lambda a: pltpu.HBM(a.shape, a.dtype)
    res = pl.pallas_call(
        body, name=name,
        out_shape=[pltpu.SemaphoreType.DMA((n_sems,)), pltpu.SemaphoreType.DMA((n_sems,))] + [hbm(a) for a in srcs + lands]
        + [jax.ShapeDtypeStruct((8, LANE), F32)],
        in_specs=[HBM] * (2 * n) + [pl.BlockSpec(memory_space=pl.ANY)] * len(extra),
        out_specs=[SEM, SEM] + [HBM] * (2 * n) + [pl.BlockSpec(memory_space=pltpu.VMEM)],
        input_output_aliases={i: 2 + i for i in range(2 * n)},
        compiler_params=pltpu.CompilerParams(has_side_effects=DATAFLOW),
    )(*[pltpu.with_memory_space_constraint(a, pltpu.HBM) for a in srcs + lands], *extra)
    return res[:-1], res[-1]


def _exchange_wait(srcs, to_first, started, after, name):
    n = len(srcs)
    send_sems, recv_sems, thru = started[0], started[1], started[2:]

    def body(*refs):
        src_refs, land_refs, send_ref, recv_ref = refs[:n], refs[n:2 * n], refs[2 * n], refs[2 * n + 1]
        sends, recvs = _exchange_copies(srcs, to_first, src_refs, land_refs, send_ref, recv_ref)
        for pred, cp in sends:
            _when(pred, cp.wait_send)
        for pred, cp in recvs:
            _when(pred, cp.wait_recv)

    res = pl.pallas_call(
        body, name=name, out_shape=[pltpu.HBM(a.shape, a.dtype) for a in thru],
        in_specs=[HBM] * (2 * n) + [SEM, SEM, pl.BlockSpec(memory_space=pl.ANY)], out_specs=[HBM] * (2 * n),
        input_output_aliases={i: i for i in range(2 * n)},
        compiler_params=pltpu.CompilerParams(has_side_effects=DATAFLOW),
    )(*thru, send_sems, recv_sems, after)
    return res[n:]


ROW_TILE = 256
COL_TILE = 256


def _block_tiling(a, b):
    if a <= ROW_TILE or a % ROW_TILE == 0:
        ta = min(a, ROW_TILE)
        return a // ta, (ta, b), lambda i: (i, 0)
    return b // COL_TILE, (a, COL_TILE), lambda i: (0, i)


def _sum_chips(parts, name):
    _, a, b = parts.shape
    steps, tile, at = _block_tiling(a, b)

    def body(p_ref, o_ref):
        f = lambda t: p_ref[t].astype(F32)
        o_ref[...] = ((f(0) + f(1)) + f(2)) + f(3)

    return pl.pallas_call(
        body, name=name, grid=(steps,),
        in_specs=[pl.BlockSpec((N_CHIPS,) + tile, lambda i: (0,) + at(i))],
        out_specs=pl.BlockSpec(tile, at),
        out_shape=jax.ShapeDtypeStruct((a, b), F32),
        compiler_params=_params(("parallel",)),
    )(parts)


def _sum_landed(land, own, name, first_land=None, first_own=None):
    _, a, b = land.shape
    steps, tile, at = _block_tiling(a, b)
    extra = first_land is not None

    def body(*refs):
        p_ref, own_ref, o_ref = refs[0], refs[1], refs[-1]
        me = 2 * lax.axis_index("x") + lax.axis_index("y")
        own = own_ref[...].astype(F32)
        slot = lambda t: jnp.where(me == t, own, p_ref[t].astype(F32))
        o_ref[...] = ((slot(0) + slot(1)) + slot(2)) + slot(3)
        if extra:
            fp_ref, fo_ref = refs[2], refs[3]
            r = fo_ref.shape[0]

            @pl.when(me == 0)
            def _():
                f = lambda t: fp_ref[t].astype(F32)
                o_ref[0:r, :] += ((fo_ref[...].astype(F32) + f(1)) + f(2)) + f(3)

    in_specs = [pl.BlockSpec((N_CHIPS,) + tile, lambda i: (0,) + at(i)), pl.BlockSpec(tile, at)]
    args = [land, own]
    if extra:
        r = first_own.shape[0]
        assert tile[0] == a, "the extra rows need whole columns in a step"
        in_specs += [pl.BlockSpec((N_CHIPS, r, tile[1]), lambda i: (0,) + at(i)), pl.BlockSpec((r, tile[1]), at)]
        args += [first_land, first_own]
    return pl.pallas_call(
        body, name=name, grid=(steps,), in_specs=in_specs, out_specs=pl.BlockSpec(tile, at),
        out_shape=jax.ShapeDtypeStruct((a, b), F32), compiler_params=_params(("parallel",)),
    )(*args)


class _LaterWeights:
    QKV = ("w_uq", "w_ukv")
    MID = ("w_branch_a", "w_branch_b", "w_out")

    def __init__(self, blocks, after):
        self.blocks = blocks
        self.qkv_started, t1 = _exchange_start([blocks[n] for n in self.QKV], (), "weights_qkv_start", after)
        self.mid_started, t2 = _exchange_start([blocks[n] for n in self.MID], (), "weights_mid_start", after)
        self.tokens = [t1, t2]

    def _whole(self, names, started, after, name):
        landed = _exchange_wait([self.blocks[n] for n in names], (), started, after, name)
        me = 2 * lax.axis_index("x") + lax.axis_index("y")
        return [_join_chips(n, lax.dynamic_update_index_in_dim(land, self.blocks[n], me, 0)) for n, land in zip(names, landed)]

    def qkv(self, after):
        return self._whole(self.QKV, self.qkv_started, after, "weights_qkv_wait")

    def mid(self, after):
        return self._whole(self.MID, self.mid_started, after, "weights_mid_wait")


class _GradExchange:
    EARLY = ("w_in", "w_branch_a", "w_branch_b", "w_out")
    LATE = ("w_uq", "w_ukv")

    def start_early(self, g):
        full = jnp.concatenate([jnp.zeros((QKV_ROWS, D), F32), g["w_in_rest"]], axis=0)
        g = dict(g, w_in=full)
        self.early = [_split_by_chip(n, g[n]).astype(BF16) for n in self.EARLY]
        self.early_started, token = _exchange_start(self.early, (), "grads_early_start")
        return token

    def start_late(self, g):
        self.early_landed = _exchange_wait(self.early, (), self.early_started, g["w_uq"], "grads_early_wait")
        self.late = [_split_by_chip(n, g[n]).astype(BF16) for n in self.LATE] + [g["w_in_qkv"].astype(BF16)]
        self.late_started, token = _exchange_start(self.late, (2,), "grads_late_start")
        return token

    def finish(self, after):
        late_landed = _exchange_wait(self.late, (2,), self.late_started, after, "grads_late_wait")
        me = 2 * lax.axis_index("x") + lax.axis_index("y")
        own = lambda slabs: lax.dynamic_index_in_dim(slabs, me, axis=0, keepdims=False)
        sums = {}
        for n, slabs, land in zip(self.EARLY, self.early, self.early_landed):
            if n == "w_in":
                sums[n] = _sum_landed(land, own(slabs), "sum_" + n, first_land=late_landed[2], first_own=self.late[2])
            else:
                sums[n] = _sum_landed(land, own(slabs), "sum_" + n)
        for n, slabs, land in zip(self.LATE, self.late, late_landed):
            sums[n] = _sum_landed(land, own(slabs), "sum_" + n)
        return sums


def _adamw_math(g, w, m, v):
    nm = ADAM_B1 * m + (1.0 - ADAM_B1) * g
    nv = ADAM_B2 * v + (1.0 - ADAM_B2) * (g * g)
    m_hat = nm / (1.0 - ADAM_B1 ** ADAM_STEP)
    v_hat = nv / (1.0 - ADAM_B2 ** ADAM_STEP)
    return -ADAM_LR * (m_hat / (jnp.sqrt(v_hat) + ADAM_EPS) + ADAM_WD * w), nm, nv


def _adamw(p_mine, p_sibling, w, m, v, name):
    a, b = p_mine.shape
    steps, tile, at = _block_tiling(a, b)

    def body(a_ref, b_ref, w_ref, m_ref, v_ref, g_ref, d_ref, nm_ref, nv_ref):
        g = a_ref[...] + b_ref[...]
        g_ref[...] = g
        d_ref[...], nm_ref[...], nv_ref[...] = _adamw_math(g, w_ref[...], m_ref[...], v_ref[...])

    spec = pl.BlockSpec(tile, at)
    sds = jax.ShapeDtypeStruct((a, b), F32)
    return pl.pallas_call(
        body, name=name, grid=(steps,), in_specs=[spec] * 5, out_specs=[spec] * 4, out_shape=[sds] * 4,
        compiler_params=_params(("parallel",)),
    )(p_mine, p_sibling, w, m, v)


LOSS_AT = (2, 1024)


def _vec_pack(vg, loss):
    names = [n for n, _, _ in VEC_ROWS]

    def body(*refs):
        o_ref = refs[-1]
        lb_ref, loss_ref = refs[len(names)], refs[len(names) + 1]
        o_ref[...] = jnp.zeros_like(o_ref)
        o_ref[LOSS_AT[0]:LOSS_AT[0] + 1, LOSS_AT[1]:LOSS_AT[1] + LANE] = jnp.broadcast_to(loss_ref[...], (1, LANE))
        for (name, row, size), ref in zip(VEC_ROWS, refs):
            if name == "g_hgrn":
                r = lax.broadcasted_iota(jnp.int32, (NH * V_DIM, LANE), 0)
                c = lax.broadcasted_iota(jnp.int32, (NH * V_DIM, LANE), 1)
                fold = ((r % V_DIM) == c).astype(F32)
                o_ref[row:row + 1, 0:LANE] = jnp.dot(ref[...], fold, precision=HIGHEST, preferred_element_type=F32)
            else:
                o_ref[row:row + 1, 0:size] = ref[...]
        o_ref[VEC_LB_ROW:VEC_LB_ROW + 2, 0:512] = lb_ref[...]

    return pl.pallas_call(body, name="vec_pack", out_shape=jax.ShapeDtypeStruct(VEC_SHAPE, F32))(
        *[vg[n] for n in names], vg["lb_logits"], loss)


def _adamw_vec(p_mine, p_sibling, w, m, v):
    names = [n for n, _, _ in VEC_ROWS] + ["lb_logits"]
    k = len(names)

    def body(a_ref, b_ref, *refs):
        ins, outs = refs[:3 * k], refs[3 * k:]
        at = (slice(LOSS_AT[0], LOSS_AT[0] + 1), slice(LOSS_AT[1], LOSS_AT[1] + LANE))
        outs[-1][...] = a_ref[at] + b_ref[at]
        for i, name in enumerate(names):
            if name == "lb_logits":
                rows, cols = slice(VEC_LB_ROW, VEC_LB_ROW + 2), slice(0, 512)
            else:
                _, row, size = VEC_ROWS[i]
                rows, cols = slice(row, row + 1), slice(0, size)
            g = a_ref[rows, cols] + b_ref[rows, cols]
            d, nm, nv = _adamw_math(g, ins[i][...], ins[k + i][...], ins[2 * k + i][...])
            for o_ref, val in zip(outs[4 * i:4 * i + 4], (g, d, nm, nv)):
                o_ref[...] = val

    shapes = [jax.ShapeDtypeStruct(w[n].shape, F32) for n in names for _ in range(4)] + [jax.ShapeDtypeStruct((1, LANE), F32)]
    res = pl.pallas_call(body, name="adamw_vec", out_shape=shapes)(
        p_mine, p_sibling, *[w[n] for n in names], *[m[n] for n in names], *[v[n] for n in names])
    return [{n: res[4 * i + j] for i, n in enumerate(names)} for j in range(4)], res[-1]


WEIGHTS = ("g_pre", "w_in", "b_gate", "g_q", "w_uq", "g_kv", "w_ukv", "lb_logits", "g_hgrn", "w_branch_a", "w_branch_b", "w_out", "g_post")


def kernel(x, g_pre, w_in, b_gate, g_q, w_uq, g_kv, w_ukv, lb_logits, g_hgrn, w_branch_a, w_branch_b, w_out, g_post, loss_target, m_g_pre, m_w_in, m_b_gate, m_g_q, m_w_uq, m_g_kv, m_w_ukv, m_lb_logits, m_g_hgrn, m_w_branch_a, m_w_branch_b, m_w_out, m_g_post, v_g_pre, v_w_in, v_b_gate, v_g_q, v_w_uq, v_g_kv, v_w_ukv, v_lb_logits, v_g_hgrn, v_w_branch_a, v_w_branch_b, v_w_out, v_g_post):
    w = dict(g_pre=g_pre, w_in=w_in, b_gate=b_gate, g_q=g_q, w_uq=w_uq, g_kv=g_kv, w_ukv=w_ukv, lb_logits=lb_logits, g_hgrn=g_hgrn,
             w_branch_a=w_branch_a, w_branch_b=w_branch_b, w_out=w_out, g_post=g_post)
    m = dict(g_pre=m_g_pre, w_in=m_w_in, b_gate=m_b_gate, g_q=m_g_q, w_uq=m_w_uq, g_kv=m_g_kv, w_ukv=m_w_ukv, lb_logits=m_lb_logits,
             g_hgrn=m_g_hgrn, w_branch_a=m_w_branch_a, w_branch_b=m_w_branch_b, w_out=m_w_out, g_post=m_g_post)
    v = dict(g_pre=v_g_pre, w_in=v_w_in, b_gate=v_b_gate, g_q=v_g_q, w_uq=v_w_uq, g_kv=v_g_kv, w_ukv=v_w_ukv, lb_logits=v_lb_logits,
             g_hgrn=v_g_hgrn, w_branch_a=v_w_branch_a, w_branch_b=v_w_branch_b, w_out=v_w_out, g_post=v_g_post)
    blocks = {n: _to_block(n, w[n]).astype(BF16) for n in BIG}
    (w_in_all,) = _gather_weights([blocks["w_in"]], [GATHER_SPLIT_AXIS["w_in"]])
    weights = _LaterWeights(blocks, w_in_all)
    exchange = _GradExchange()
    loss, grad_x, _, vec_grads = _local_step(
        x[0], loss_target[0], g_pre, _join_chips("w_in", w_in_all), b_gate, g_q, g_kv, lb_logits, g_hgrn, g_post, weights, exchange)
    sums = exchange.finish(grad_x)
    (vec_landed,) = _chip_exchange([_vec_pack(vec_grads, loss)], "scatter_vec")
    mine = [sums[n] for n in BIG] + [_sum_chips(vec_landed, "sum_vec")]
    theirs = _sibling_exchange(mine, "sibling_grads")
    outs = [{}, {}, {}, {}]
    for k, n in enumerate(BIG):
        blocks = [_to_block(n, t[n]) for t in (w, m, v)]
        for o, val in zip(outs, _adamw(mine[k], theirs[k], *blocks, "adamw_" + n)):
            o[n] = _from_block(n, val)
    vec_outs, total = _adamw_vec(mine[-1], theirs[-1], w, m, v)
    for o, vals in zip(outs, vec_outs):
        o.update(vals)
    return (total[0, 0], grad_x[None], *[o[n] for o in outs for n in WEIGHTS])
```

```python
import functools
import math

import numpy as np
import jax
import jax.numpy as jnp
from jax import lax
from jax.experimental import pallas as pl
from jax.experimental.pallas import tpu as pltpu

F32 = jnp.float32
BF16 = jnp.bfloat16
HIGHEST = lax.Precision.HIGHEST

D = 1024
NH = 8
QK_NOPE, QK_ROPE, V_DIM = 64, 32, 64
Q_LORA, KV_LORA = 768, 256
CHUNK = 64
HG_BLOCK = 32
EPS = 1e-6
D_IN = 5664
LANE = 128
P_MERGE, P_GA, P_HQ, P_HF, P_HI, P_GB, P_CQ, P_CKV, P_KPE = 0, 2048, 2560, 3072, 3584, 4096, 4608, 5376, 5632
D_P = 5760
O_CQ, O_CKV, O_KPE, O_GA, O_HQ, O_HF, O_HI, O_GB, O_MERGE = 0, 768, 1024, 1056, 1568, 2080, 2592, 3104, 3616

TM = 256
TM_MID = 256
TQ = 512
ONES_LANE = (LANE - 1, 0)
TH = 256
HG_PAIRS = 2
VMEM_LIMIT = 56 * 1024 * 1024

ADAM_LR, ADAM_B1, ADAM_B2, ADAM_EPS, ADAM_WD, ADAM_STEP = 0.001, 0.9, 0.999, 1e-08, 0.01, 10

NT_DIMS = (((1,), (1,)), ((), ()))
TN_DIMS = (((0,), (0,)), ((), ()))


def _params(sem):
    return pltpu.CompilerParams(dimension_semantics=sem, vmem_limit_bytes=VMEM_LIMIT)


def _mm(a, b):
    return jnp.dot(a, b, preferred_element_type=F32)


def _mm_nt(a, b):
    return lax.dot_general(a, b, NT_DIMS, preferred_element_type=F32)


def _mm_tn(a, b):
    return lax.dot_general(a, b, TN_DIMS, preferred_element_type=F32)


def _sigmoid(z):
    return jax.nn.sigmoid(z)


def _rope(v, c, s1, s2):
    return v * c + pltpu.roll(v, 112, 1) * s1 + pltpu.roll(v, 16, 1) * s2


def _rope_t(dy, c, s1, s2):
    return dy * c + pltpu.roll(dy * s1, 16, 1) + pltpu.roll(dy * s2, 112, 1)


def _rope_tables(s):
    inv = 10000.0 ** (-jnp.arange(0, QK_ROPE, 2, dtype=F32) / QK_ROPE)
    ang = jnp.arange(s, dtype=F32)[:, None] * inv[None, :]
    cos, sin = jnp.cos(ang), jnp.sin(ang)
    z64, z32, o64, o32 = jnp.zeros((s, 64), F32), jnp.zeros((s, 32), F32), jnp.ones((s, 64), F32), jnp.ones((s, 32), F32)
    z16 = jnp.zeros((s, 16), F32)
    c = jnp.concatenate([o64, cos, cos, o32], axis=1)
    s1 = jnp.concatenate([z64, -sin, z16, z32], axis=1)
    s2 = jnp.concatenate([z64, z16, sin, z32], axis=1)
    return c, s1, s2


def _front_fwd(x, g_pre, w_in_pt, tokens=()):
    s = x.shape[0]
    tokens = list(tokens)

    def body(x_ref, g_ref, w_ref, *refs):
        o_ref, h_ref = refs[len(tokens):]
        xv = x_ref[...]
        r = lax.rsqrt(jnp.mean(xv * xv, axis=-1, keepdims=True) + EPS)
        h = ((xv * r) * g_ref[...]).astype(BF16)
        h_ref[...] = h
        o_ref[...] = _mm_nt(h, w_ref[...])

    return pl.pallas_call(
        body, name="front_fwd", grid=(s // TM,),
        in_specs=[pl.BlockSpec((TM, D), lambda i: (i, 0)), pl.BlockSpec((1, D), lambda i: (0, 0)),
                  pl.BlockSpec((D_P, D), lambda i: (0, 0))] + [pl.BlockSpec((8, LANE), lambda i: (0, 0))] * len(tokens),
        out_specs=[pl.BlockSpec((TM, D_P), lambda i: (i, 0)), pl.BlockSpec((TM, D), lambda i: (i, 0))],
        out_shape=[jax.ShapeDtypeStruct((s, D_P), F32), jax.ShapeDtypeStruct((s, D), BF16)],
        compiler_params=_params(("parallel",)),
    )(x, g_pre, w_in_pt, *tokens)


def _norm_rows(v, g):
    r = lax.rsqrt(jnp.mean(v * v, axis=-1, keepdims=True) + EPS)
    return (v * r) * g, r


def _qkv_fwd(proj, g_q, g_kv, w_uq_p, w_k_p, w_v_p, rc, rs1, rs2):
    s = proj.shape[0]

    def body(cq_ref, ckv_ref, kpe_ref, gq_ref, gkv_ref, wq_ref, wk_ref, wv_ref, c_ref, s1_ref, s2_ref, q_ref, k_ref, v_ref):
        c, s1, s2 = c_ref[...], s1_ref[...], s2_ref[...]
        cqn, _ = _norm_rows(cq_ref[...], gq_ref[...])
        ckvn, _ = _norm_rows(ckv_ref[...], gkv_ref[...])
        ckvn = ckvn.astype(BF16)
        qf = _mm(cqn.astype(BF16), wq_ref[...])
        kf = _mm(ckvn, wk_ref[...])
        vf = _mm(ckvn, wv_ref[...])
        kpe = _rope(kpe_ref[...], c, s1, s2)
        lane = lax.broadcasted_iota(jnp.int32, (TM, LANE), 1)
        for h in range(NH):
            blk = slice(h * LANE, (h + 1) * LANE)
            q_ref[h] = _rope(qf[:, blk], c, s1, s2).astype(BF16)
            k_ref[h] = (kf[:, blk] + kpe).astype(BF16)
            v_ref[h] = jnp.where(lane == ONES_LANE[h % 2], 1.0, vf[:, blk]).astype(BF16)

    row = lambda w, j: pl.BlockSpec((TM, w), lambda i: (i, j))
    full = lambda a: pl.BlockSpec(a.shape, lambda i: (0,) * a.ndim)
    hs = jax.ShapeDtypeStruct((NH, s, LANE), BF16)
    return pl.pallas_call(
        body, name="qkv_fwd", grid=(s // TM,),
        in_specs=[row(Q_LORA, P_CQ // Q_LORA), row(KV_LORA, P_CKV // KV_LORA), row(LANE, P_KPE // LANE),
                  full(g_q), full(g_kv), full(w_uq_p), full(w_k_p), full(w_v_p), row(LANE, 0), row(LANE, 0), row(LANE, 0)],
        out_specs=[pl.BlockSpec((NH, TM, LANE), lambda i: (0, i, 0))] * 3,
        out_shape=[hs, hs, hs],
        compiler_params=_params(("parallel",)),
    )(proj, proj, proj, g_q, g_kv, w_uq_p, w_k_p, w_v_p, rc, rs1, rs2)


LOG2E = 1.4426950408889634
QK_SCALE2 = LOG2E / math.sqrt(QK_NOPE + QK_ROPE)


def _diag_visible():
    row = lax.broadcasted_iota(jnp.int32, (TQ, TQ), 0)
    col = lax.broadcasted_iota(jnp.int32, (TQ, TQ), 1)
    return (col // CHUNK) <= (row // CHUNK)


def _attn_fwd(q, k, vv):
    s = q.shape[1]

    def body(q_ref, k_ref, v_ref, o_ref, lse_ref):
        i = pl.program_id(1)
        qs = (q_ref[0], q_ref[1])

        def tile(hh, t, carry, diag):
            m, acc = carry
            rows = pl.ds(pl.multiple_of(t * TQ, TQ), TQ)
            sc = _mm_nt(qs[hh], k_ref[hh, rows, :])
            if diag:
                sc = jnp.where(_diag_visible(), sc, -jnp.inf)
            m_new = jnp.maximum(m, jnp.max(sc, axis=-1, keepdims=True))
            alpha = jnp.exp2((m - m_new) * QK_SCALE2)
            p = jnp.exp2((sc - m_new) * QK_SCALE2).astype(BF16)
            acc = alpha * acc + _mm(p, v_ref[hh, rows, :])
            return m_new, acc

        def step(t, carry):
            return tile(0, t, carry[0], False), tile(1, t, carry[1], False)

        init = (jnp.full((TQ, 1), -jnp.inf, F32), jnp.zeros((TQ, LANE), F32))
        carry = lax.fori_loop(0, i, step, (init, init))
        lane = lax.broadcasted_iota(jnp.int32, (TQ, LANE), 1)
        out = jnp.zeros((TQ, LANE), F32)
        for hh in range(2):
            m, acc = tile(hh, i, carry[hh], True)
            l = jnp.sum(jnp.where(lane == ONES_LANE[hh], acc, 0.0), axis=-1, keepdims=True)
            out = out + jnp.where((lane < V_DIM) == (hh == 0), acc, 0.0) / l
            lse_ref[hh] = jnp.broadcast_to(m * QK_SCALE2 + jnp.log(l) * LOG2E, (TQ, LANE))
        o_ref[...] = out

    return pl.pallas_call(
        body, name="attn_fwd", grid=(NH // 2, s // TQ),
        in_specs=[pl.BlockSpec((2, TQ, LANE), lambda p, i: (p, i, 0)), pl.BlockSpec((2, s, LANE), lambda p, i: (p, 0, 0)),
                  pl.BlockSpec((2, s, LANE), lambda p, i: (p, 0, 0))],
        out_specs=[pl.BlockSpec((TQ, LANE), lambda p, i: (i, p)), pl.BlockSpec((2, TQ, LANE), lambda p, i: (p, i, 0))],
        out_shape=[jax.ShapeDtypeStruct((s, NH * V_DIM), F32), jax.ShapeDtypeStruct((NH, s, LANE), F32)],
        compiler_params=_params(("parallel", "parallel")),
    )(q, k, vv)


def _lower_bound(lbl):
    a0, a1 = lbl[0:1, :], lbl[1:2, :]
    mx = jnp.maximum(a0, a1)
    e0, e1 = jnp.exp(a0 - mx), jnp.exp(a1 - mx)
    return e0 / (e0 + e1)


def _chunk_cumsum(v, reverse=False):
    pos = lax.broadcasted_iota(jnp.int32, v.shape, 0) % HG_BLOCK
    s = 1
    while s < HG_BLOCK:
        if reverse:
            v = v + jnp.where(pos < HG_BLOCK - s, pltpu.roll(v, TH - s, 0), 0.0)
        else:
            v = v + jnp.where(pos >= s, pltpu.roll(v, s, 0), 0.0)
        s *= 2
    return v


def _hgrn_gates(hq, hf, lb):
    sig = _sigmoid(hf)
    f = lb + (1.0 - lb) * sig
    g = jnp.log(f)
    kk = 1.0 - f
    r = lax.broadcasted_iota(jnp.int32, (TH, TH), 0)
    c = lax.broadcasted_iota(jnp.int32, (TH, TH), 1)
    tri = ((r // HG_BLOCK) == (c // HG_BLOCK)) & (r >= c)
    cum = _chunk_cumsum(g)
    nch = TH // HG_BLOCK
    total = _chunks(cum)[:, HG_BLOCK - 1:HG_BLOCK, :]
    lastb = jnp.broadcast_to(total, (nch, HG_BLOCK, LANE)).reshape(TH, LANE)
    e, ei, ee = jnp.exp(cum), jnp.exp(-cum), jnp.exp(lastb - cum)
    return dict(sig=sig, f=f, kk=kk, tri=tri, cum=cum, total=total, e=e, ei=ei, ee=ee, qd=hq * e, ki=kk * ei, ke=kk * ee)


def _chunks(v):
    return v.reshape(TH // HG_BLOCK, HG_BLOCK, v.shape[-1])


def _bmm_nt(a, b):
    return lax.dot_general(a, b, (((2,), (2,)), ((0,), (0,))), preferred_element_type=F32)


def _bmm_nn(a, b):
    return lax.dot_general(a, b, (((2,), (1,)), ((0,), (0,))), preferred_element_type=F32)


def _bmm_tn(a, b):
    return lax.dot_general(a, b, (((1,), (1,)), ((0,), (0,))), preferred_element_type=F32)


def _pair_masks():
    lane = lax.broadcasted_iota(jnp.int32, (TH, LANE), 1)
    kr = lax.broadcasted_iota(jnp.int32, (LANE, LANE), 0)
    kc = lax.broadcasted_iota(jnp.int32, (LANE, LANE), 1)
    return lane < 64, (kr // 64) == (kc // 64)


def _hgrn_fwd(proj, lbl):
    s = proj.shape[0]
    nch = TH // HG_BLOCK

    def body(hq_ref, hf_ref, hi_ref, lbl_ref, o_ref, st_ref, st):
        @pl.when(pl.program_id(1) == 0)
        def _():
            st[...] = jnp.zeros_like(st)

        m0, bd = _pair_masks()
        for u in range(HG_PAIRS):
            lanes = slice(u * LANE, (u + 1) * LANE)
            lb = _lower_bound(lbl_ref[:, lanes])
            gt = _hgrn_gates(hq_ref[:, lanes], hf_ref[:, lanes], lb)
            v_b = hi_ref[:, lanes].astype(BF16)
            qd, ki_b, ke_b = gt["qd"], gt["ki"].astype(BF16), gt["ke"].astype(BF16)
            qd_b = qd.astype(BF16)
            o = jnp.zeros((TH, LANE), F32)
            for hh in range(2):
                mh = m0 if hh == 0 else jnp.logical_not(m0)
                a = jnp.where(gt["tri"], _mm_nt(jnp.where(mh, qd, 0.0).astype(BF16), ki_b), 0.0)
                o = jnp.where(mh, _mm(a.astype(BF16), v_b), o)
            upd = _bmm_tn(_chunks(v_b), _chunks(ke_b))
            decay = jnp.exp(gt["total"])
            cur, entering = st[u], []
            for n in range(nch):
                entering.append(cur)
                cur = decay[n] * cur + jnp.where(bd, upd[n], 0.0)
            st[u] = cur
            entering = jnp.stack(entering)
            st_ref[u] = entering
            o_ref[:, lanes] = o + _bmm_nt(_chunks(qd_b), entering.astype(BF16)).reshape(TH, LANE)

    wide = HG_PAIRS * LANE
    col = lambda base: pl.BlockSpec((TH, wide), lambda p, i: (i, base // wide + p))
    return pl.pallas_call(
        body, name="hgrn_fwd", grid=(NH // 2 // HG_PAIRS, s // TH),
        in_specs=[col(P_HQ), col(P_HF), col(P_HI), pl.BlockSpec((2, wide), lambda p, i: (0, p))],
        out_specs=[pl.BlockSpec((TH, wide), lambda p, i: (i, p)),
                   pl.BlockSpec((HG_PAIRS, nch, LANE, LANE), lambda p, i: (p, i, 0, 0))],
        out_shape=[jax.ShapeDtypeStruct((s, 512), F32), jax.ShapeDtypeStruct((NH // 2, s // HG_BLOCK, LANE, LANE), F32)],
        scratch_shapes=[pltpu.VMEM((HG_PAIRS, LANE, LANE), F32)],
        compiler_params=_params(("parallel", "arbitrary")),
    )(proj, proj, proj, lbl)


def _group_sum(v):
    low = lax.broadcasted_iota(jnp.int32, (v.shape[0], LANE), 1) < V_DIM
    blocks = []
    for b in range(v.shape[1] // LANE):
        blk = v[:, b * LANE:(b + 1) * LANE]
        s_low = jnp.sum(jnp.where(low, blk, 0.0), axis=-1, keepdims=True)
        s_high = jnp.sum(jnp.where(low, 0.0, blk), axis=-1, keepdims=True)
        blocks.append(jnp.where(low, s_low, s_high))
    return jnp.concatenate(blocks, axis=1)


def _dsilu(z, sg):
    return sg * (1.0 + z * (1.0 - sg))


def _mid(proj, attn, o_raw, x, tgt, g_hg, b_gate, g_post, wa, wb, w_out):
    s = x.shape[0]

    def body(attn_ref, ga_ref, o_ref, gb_ref, mg_ref, x_ref, t_ref, ghg_ref, bg_ref, gp_ref, wa_ref, wb_ref, wo_ref,
             loss_ref, dout_ref, dattn_ref, dga_ref, dor_ref, dgb_ref, dmg_ref, dwo_ref, dwa_ref, dwb_ref, dgp_ref, dbg_ref, dghg_ref):
        first = pl.program_id(0) == 0

        @pl.when(first)
        def _():
            for rf in (loss_ref, dwo_ref, dwa_ref, dwb_ref, dgp_ref, dbg_ref, dghg_ref):
                rf[...] = jnp.zeros_like(rf)

        attn, za, orw, zb = attn_ref[...], ga_ref[...], o_ref[...], gb_ref[...]
        ghg, gp = ghg_ref[...], gp_ref[...]
        sga, sgb = _sigmoid(za), _sigmoid(zb)
        sa, sb = za * sga, zb * sgb
        ga = attn * sa
        rh = lax.rsqrt(_group_sum(orw * orw) * (1.0 / V_DIM) + EPS)
        on = (orw * rh) * ghg
        gb = on * sb
        ga_b, gb_b = ga.astype(BF16), gb.astype(BF16)
        ya = _mm(ga_b, wa_ref[...])
        yb = _mm(gb_b, wb_ref[...])
        gates = _sigmoid(mg_ref[...] + bg_ref[...])
        g0, g1 = gates[:, :D], gates[:, D:]
        m_b = (g0 * ya + g1 * yb).astype(BF16)
        y = _mm(m_b, wo_ref[...])
        ry = lax.rsqrt(jnp.mean(y * y, axis=-1, keepdims=True) + EPS)
        out = x_ref[...] + (y * ry) * gp
        err = out - t_ref[...]
        loss_ref[...] += 0.5 * jnp.sum(jnp.mean(err * err, axis=-1, keepdims=True), axis=0, keepdims=True)
        dout = err * (1.0 / D)
        dout_ref[...] = dout
        dgp_ref[...] += jnp.sum(dout * (y * ry), axis=0, keepdims=True)
        dgy = dout * gp
        dy = ry * dgy - y * (ry * ry * ry) * jnp.mean(y * dgy, axis=-1, keepdims=True)
        dy_b = dy.astype(BF16)
        dwo_ref[...] += _mm_tn(m_b, dy_b)
        dm = _mm_nt(dy_b, wo_ref[...])
        dya, dyb = dm * g0, dm * g1
        dg0, dg1 = dm * ya, dm * yb
        dmg = jnp.concatenate([dg0 * g0 * (1.0 - g0), dg1 * g1 * (1.0 - g1)], axis=1)
        dmg_ref[...] = dmg.astype(BF16)
        dbg_ref[...] += jnp.sum(dmg, axis=0, keepdims=True)
        dya_b, dyb_b = dya.astype(BF16), dyb.astype(BF16)
        dwa_ref[...] += _mm_tn(ga_b, dya_b)
        dwb_ref[...] += _mm_tn(gb_b, dyb_b)
        dga = _mm_nt(dya_b, wa_ref[...])
        dgb = _mm_nt(dyb_b, wb_ref[...])
        dattn_ref[...] = dga * sa
        dga_ref[...] = (dga * attn * _dsilu(za, sga)).astype(BF16)
        dgb_ref[...] = (dgb * on * _dsilu(zb, sgb)).astype(BF16)
        don = dgb * sb
        dghg_ref[...] += jnp.sum(don * (orw * rh), axis=0, keepdims=True)
        dgo = don * ghg
        dor_ref[...] = rh * dgo - orw * (rh * rh * rh) * (_group_sum(orw * dgo) * (1.0 / V_DIM))

    row = lambda w, j=0: pl.BlockSpec((TM_MID, w), lambda i: (i, j))
    full = lambda a: pl.BlockSpec(a.shape, lambda i: (0,) * a.ndim)
    acc = lambda shape: pl.BlockSpec(shape, lambda i: (0, 0))
    sds = jax.ShapeDtypeStruct
    return pl.pallas_call(
        body, name="mid", grid=(s // TM_MID,),
        in_specs=[row(512), row(512, P_GA // 512), row(512), row(512, P_GB // 512), row(2048, P_MERGE // 2048), row(D), row(D),
                  full(g_hg), full(b_gate), full(g_post), full(wa), full(wb), full(w_out)],
        out_specs=[acc((1, 1)), row(D), row(512), row(512), row(512), row(512), row(2048),
                   acc((D, D)), acc((512, D)), acc((512, D)), acc((1, D)), acc((1, 2048)), acc((1, 512))],
        out_shape=[sds((1, 1), F32), sds((s, D), F32), sds((s, 512), F32), sds((s, 512), BF16), sds((s, 512), F32), sds((s, 512), BF16),
                   sds((s, 2048), BF16), sds((D, D), F32), sds((512, D), F32), sds((512, D), F32), sds((1, D), F32),
                   sds((1, 2048), F32), sds((1, 512), F32)],
        compiler_params=_params(("arbitrary",)),
    )(attn, proj, o_raw, proj, proj, x, tgt, g_hg, b_gate, g_post, wa, wb, w_out)


def _attn_bwd(q, k, vv, attn, dattn, lse, token):
    s = q.shape[1]
    nt = s // TQ
    scale = 1.0 / math.sqrt(QK_NOPE + QK_ROPE)

    def body(q_ref, k_ref, v_ref, o_ref, do_ref, lse_ref, token_ref, dq_ref, dk_ref, dv_ref, do_s, delta_s):
        j = pl.program_id(1)

        @pl.when(j == 0)
        def _():
            dq_ref[...] = jnp.zeros_like(dq_ref)
            lane = lax.broadcasted_iota(jnp.int32, (TQ, LANE), 1)

            @pl.loop(0, nt)
            def _(i):
                rows = pl.ds(pl.multiple_of(i * TQ, TQ), TQ)
                do, o = do_ref[rows, :], o_ref[rows, :]
                for hh in range(2):
                    doh = jnp.where((lane < 64) if hh == 0 else (lane >= 64), do, 0.0)
                    do_s[hh, rows, :] = doh.astype(BF16)
                    delta_s[hh, rows, :] = jnp.broadcast_to(jnp.sum(doh * o, axis=-1, keepdims=True), (TQ, LANE))

        kjs, vjs = (k_ref[0], k_ref[1]), (v_ref[0], v_ref[1])
        wide = lambda a: jnp.concatenate([a] * (TQ // LANE), axis=1)

        def tile(hh, i, carry, diag):
            dk, dv = carry
            rows = pl.ds(pl.multiple_of(i * TQ, TQ), TQ)
            qi, do_b = q_ref[hh, rows, :], do_s[hh, rows, :]
            p = jnp.exp2(_mm_nt(qi, kjs[hh]) * QK_SCALE2 - wide(lse_ref[hh, rows, :]))
            if diag:
                p = jnp.where(_diag_visible(), p, 0.0)
            dv = dv + _mm_tn(do_b, p.astype(BF16))
            ds_b = (p * (_mm_nt(do_b, vjs[hh]) - wide(delta_s[hh, rows, :]))).astype(BF16)
            dk = dk + _mm_tn(qi, ds_b)
            dq_ref[hh, rows, :] += _mm(ds_b, kjs[hh])
            return dk, dv

        def step(i, carry):
            return tile(0, i, carry[0], False), tile(1, i, carry[1], False)

        z = jnp.zeros((LANE, TQ), F32)
        first = (tile(0, j, (z, z), True), tile(1, j, (z, z), True))
        carry = lax.fori_loop(j + 1, nt, step, first)
        for hh in range(2):
            dk_ref[hh] = carry[hh][0].T * scale
            dv_ref[hh] = carry[hh][1].T

        @pl.when(j == nt - 1)
        def _():
            dq_ref[...] = dq_ref[...] * scale

    whole = pl.BlockSpec((2, s, LANE), lambda p, j: (p, 0, 0))
    tile_spec = pl.BlockSpec((2, TQ, LANE), lambda p, j: (p, j, 0))
    cols = pl.BlockSpec((s, LANE), lambda p, j: (0, p))
    hs = jax.ShapeDtypeStruct((NH, s, LANE), F32)
    return pl.pallas_call(
        body, name="attn_bwd", grid=(NH // 2, nt),
        in_specs=[whole, tile_spec, tile_spec, cols, cols, whole, pl.BlockSpec((8, LANE), lambda p, j: (0, 0))],
        out_specs=[whole, tile_spec, tile_spec],
        out_shape=[hs, hs, hs],
        scratch_shapes=[pltpu.VMEM((2, s, LANE), BF16), pltpu.VMEM((2, s, LANE), F32)],
        compiler_params=_params(("parallel", "arbitrary")),
    )(q, k, vv, attn, dattn, lse, token)


def _hgrn_bwd(proj, lbl, states, do_raw):
    s = proj.shape[0]
    nt = s // TH
    nch = TH // HG_BLOCK

    def body(hq_ref, hf_ref, hi_ref, lbl_ref, st_ref, do_ref, dh_ref, dlbl_ref, dst, dlb):
        step = pl.program_id(1)

        @pl.when(step == 0)
        def _():
            dst[...] = jnp.zeros_like(dst)
            dlb[...] = jnp.zeros_like(dlb)

        m0, bd = _pair_masks()
        for u in range(HG_PAIRS):
            lanes = slice(u * LANE, (u + 1) * LANE)
            lb = _lower_bound(lbl_ref[:, lanes])
            gt = _hgrn_gates(hq_ref[:, lanes], hf_ref[:, lanes], lb)
            do = do_ref[:, lanes]
            qd, ki, ke = gt["qd"], gt["ki"], gt["ke"]
            v_b, do_b = hi_ref[:, lanes].astype(BF16), do.astype(BF16)
            qd_b, ki_b, ke_b = qd.astype(BF16), ki.astype(BF16), ke.astype(BF16)
            dv = jnp.zeros((TH, LANE), F32)
            dqd = jnp.zeros((TH, LANE), F32)
            dki = jnp.zeros((TH, LANE), F32)
            for hh in range(2):
                mh = m0 if hh == 0 else jnp.logical_not(m0)
                a_b = jnp.where(gt["tri"], _mm_nt(jnp.where(mh, qd, 0.0).astype(BF16), ki_b), 0.0).astype(BF16)
                doh_b = jnp.where(mh, do, 0.0).astype(BF16)
                da_b = jnp.where(gt["tri"], _mm_nt(doh_b, v_b), 0.0).astype(BF16)
                dv = dv + _mm_tn(a_b, doh_b)
                dqd = jnp.where(mh, _mm(da_b, ki_b), dqd)
                dki = jnp.where(mh, _mm_tn(da_b, qd_b), dki)
            fed = _bmm_tn(_chunks(do_b), _chunks(qd_b))
            decay = jnp.exp(gt["total"])
            ds, leaving = dst[u], [None] * nch
            for n in reversed(range(nch)):
                leaving[n] = ds
                ds = decay[n] * ds + jnp.where(bd, fed[n], 0.0)
            dst[u] = ds
            leaving = jnp.stack(leaving)
            entering = st_ref[u]
            leaving_b = leaving.astype(BF16)
            dke3 = _bmm_nn(_chunks(v_b), leaving_b)
            dv = dv + _bmm_nt(_chunks(ke_b), leaving_b).reshape(TH, LANE)
            dqd = dqd + _bmm_nn(_chunks(do_b), entering.astype(BF16)).reshape(TH, LANE)
            dke = dke3.reshape(TH, LANE)
            dlast = (jnp.sum(dke3 * _chunks(ke), axis=1, keepdims=True)
                     + jnp.sum(leaving * entering, axis=1, keepdims=True) * decay)
            dk = dki * gt["ei"] + dke * gt["ee"]
            dcum = dqd * qd - dki * ki - dke * ke
            dg = _chunk_cumsum(dcum, reverse=True) + jnp.broadcast_to(dlast, (nch, HG_BLOCK, LANE)).reshape(TH, LANE)
            sig = gt["sig"]
            df = dg / gt["f"] - dk
            dlb[:, lanes] += jnp.sum(df * (1.0 - sig), axis=0, keepdims=True)
            dh_ref[0, :, lanes] = (dqd * gt["e"]).astype(BF16)
            dh_ref[1, :, lanes] = ((df * (1.0 - lb)) * sig * (1.0 - sig)).astype(BF16)
            dh_ref[2, :, lanes] = dv.astype(BF16)

        @pl.when(step == nt - 1)
        def _():
            lb = _lower_bound(lbl_ref[...])
            da0 = dlb[...] * lb * (1.0 - lb)
            dlbl_ref[...] = jnp.concatenate([da0, -da0], axis=0)

    wide = HG_PAIRS * LANE
    col = lambda base: pl.BlockSpec((TH, wide), lambda p, i: (nt - 1 - i, base // wide + p))
    tile = pl.BlockSpec((TH, wide), lambda p, i: (nt - 1 - i, p))
    sds = jax.ShapeDtypeStruct
    return pl.pallas_call(
        body, name="hgrn_bwd", grid=(NH // 2 // HG_PAIRS, nt),
        in_specs=[col(P_HQ), col(P_HF), col(P_HI), pl.BlockSpec((2, wide), lambda p, i: (0, p)),
                  pl.BlockSpec((HG_PAIRS, nch, LANE, LANE), lambda p, i: (p, nt - 1 - i, 0, 0)), tile],
        out_specs=[pl.BlockSpec((3, TH, wide), lambda p, i: (0, nt - 1 - i, p)), pl.BlockSpec((2, wide), lambda p, i: (0, p))],
        out_shape=[sds((3, s, 512), BF16), sds((2, 512), F32)],
        scratch_shapes=[pltpu.VMEM((HG_PAIRS, LANE, LANE), F32), pltpu.VMEM((1, wide), F32)],
        compiler_params=_params(("parallel", "arbitrary")),
    )(proj, proj, proj, lbl, states, do_raw)


def _norm_rows_bwd(v, r, g, dn):
    dgv = dn * g
    return r * dgv - v * (r * r * r) * jnp.mean(v * dgv, axis=-1, keepdims=True)


def _qkv_bwd(proj, dq, dk, dvv, g_q, g_kv, w_uq_p, w_k_p, w_v_p, rc, rs1, rs2):
    s = proj.shape[0]

    def body(cq_ref, ckv_ref, dq_ref, dk_ref, dv_ref, gq_ref, gkv_ref, wq_ref, wk_ref, wv_ref, c_ref, s1_ref, s2_ref,
             dcq_ref, dckv_ref, dkpe_ref, dwq_ref, dwk_ref, dwv_ref, dgq_ref, dgkv_ref):
        @pl.when(pl.program_id(0) == 0)
        def _():
            for rf in (dwq_ref, dwk_ref, dwv_ref, dgq_ref, dgkv_ref):
                rf[...] = jnp.zeros_like(rf)

        c, s1, s2 = c_ref[...], s1_ref[...], s2_ref[...]
        cq, ckv = cq_ref[...], ckv_ref[...]
        gq, gkv = gq_ref[...], gkv_ref[...]
        cqn, rq = _norm_rows(cq, gq)
        ckvn, rkv = _norm_rows(ckv, gkv)
        cqn_b, ckvn_b = cqn.astype(BF16), ckvn.astype(BF16)
        dqf = jnp.concatenate([_rope_t(dq_ref[h], c, s1, s2) for h in range(NH)], axis=1).astype(BF16)
        dkf = jnp.concatenate([dk_ref[h] for h in range(NH)], axis=1).astype(BF16)
        dvf = jnp.concatenate([dv_ref[h] for h in range(NH)], axis=1).astype(BF16)
        dkpe = dk_ref[0]
        for h in range(1, NH):
            dkpe = dkpe + dk_ref[h]
        lane = lax.broadcasted_iota(jnp.int32, (TM, LANE), 1)
        dkpe = jnp.where((lane >= QK_NOPE) & (lane < QK_NOPE + QK_ROPE), dkpe, 0.0)
        dkpe_ref[...] = _rope_t(dkpe, c, s1, s2).astype(BF16)
        dwq_ref[...] += _mm_tn(cqn_b, dqf)
        dwk_ref[...] += _mm_tn(ckvn_b, dkf)
        dwv_ref[...] += _mm_tn(ckvn_b, dvf)
        dcqn = _mm_nt(dqf, wq_ref[...])
        dckvn = _mm_nt(dkf, wk_ref[...]) + _mm_nt(dvf, wv_ref[...])
        dgq_ref[...] += jnp.sum(dcqn * (cq * rq), axis=0, keepdims=True)
        dgkv_ref[...] += jnp.sum(dckvn * (ckv * rkv), axis=0, keepdims=True)
        dcq_ref[...] = _norm_rows_bwd(cq, rq, gq, dcqn).astype(BF16)
        dckv_ref[...] = _norm_rows_bwd(ckv, rkv, gkv, dckvn).astype(BF16)

    row = lambda w, j=0: pl.BlockSpec((TM, w), lambda i: (i, j))
    full = lambda a: pl.BlockSpec(a.shape, lambda i: (0,) * a.ndim)
    acc = lambda shape: pl.BlockSpec(shape, lambda i: (0, 0))
    heads = pl.BlockSpec((NH, TM, LANE), lambda i: (0, i, 0))
    sds = jax.ShapeDtypeStruct
    return pl.pallas_call(
        body, name="qkv_bwd", grid=(s // TM,),
        in_specs=[row(Q_LORA, P_CQ // Q_LORA), row(KV_LORA, P_CKV // KV_LORA), heads, heads, heads,
                  full(g_q), full(g_kv), full(w_uq_p), full(w_k_p), full(w_v_p), row(LANE), row(LANE), row(LANE)],
        out_specs=[row(Q_LORA), row(KV_LORA), row(LANE), acc((Q_LORA, D)), acc((KV_LORA, D)), acc((KV_LORA, D)),
                   acc((1, Q_LORA)), acc((1, KV_LORA))],
        out_shape=[sds((s, Q_LORA), BF16), sds((s, KV_LORA), BF16), sds((s, LANE), BF16), sds((Q_LORA, D), F32),
                   sds((KV_LORA, D), F32), sds((KV_LORA, D), F32), sds((1, Q_LORA), F32), sds((1, KV_LORA), F32)],
        compiler_params=_params(("arbitrary",)),
    )(proj, proj, dq, dk, dvv, g_q, g_kv, w_uq_p, w_k_p, w_v_p, rc, rs1, rs2)


def _front_bwd(x, dout, dmg, dga, dh3, dgb, dcq, dckv, dkpe, g_pre, w_in_pt, token):
    s = x.shape[0]

    def body(x_ref, do_ref, dmg_ref, dga_ref, dh3_ref, dgb_ref, dcq_ref, dckv_ref, dkpe_ref, g_ref, w_ref, token_ref, gx_ref, dg_ref):
        @pl.when(pl.program_id(0) == 0)
        def _():
            dg_ref[...] = jnp.zeros_like(dg_ref)

        xv, g = x_ref[...], g_ref[...]
        _, r = _norm_rows(xv, g)
        pieces = ((dmg_ref[...], P_MERGE), (dga_ref[...], P_GA), (dh3_ref[0], P_HQ), (dh3_ref[1], P_HF), (dh3_ref[2], P_HI),
                  (dgb_ref[...], P_GB), (dcq_ref[...], P_CQ), (dckv_ref[...], P_CKV), (dkpe_ref[...], P_KPE))
        dh = jnp.zeros((TM, D), F32)
        for piece, off in pieces:
            dh = dh + _mm(piece, w_ref[off:off + piece.shape[1], :])
        dg_ref[...] += jnp.sum(dh * (xv * r), axis=0, keepdims=True)
        gx_ref[...] = do_ref[...] + _norm_rows_bwd(xv, r, g, dh)

    row = lambda w: pl.BlockSpec((TM, w), lambda i: (i, 0))
    full = lambda a: pl.BlockSpec(a.shape, lambda i: (0,) * a.ndim)
    sds = jax.ShapeDtypeStruct
    return pl.pallas_call(
        body, name="front_bwd", grid=(s // TM,),
        in_specs=[row(D), row(D), row(2048), row(512), pl.BlockSpec((3, TM, 512), lambda i: (0, i, 0)), row(512), row(Q_LORA),
                  row(KV_LORA), row(LANE), full(g_pre), full(w_in_pt), pl.BlockSpec(memory_space=pl.ANY)],
        out_specs=[row(D), pl.BlockSpec((1, D), lambda i: (0, 0))],
        out_shape=[sds((s, D), F32), sds((1, D), F32)],
        compiler_params=_params(("arbitrary",)),
    )(x, dout, dmg, dga, dh3, dgb, dcq, dckv, dkpe, g_pre, w_in_pt, token)


TK_GRAD = 512


def _win_grad(h, pieces, name):
    s = h.shape[0]
    n = len(pieces)

    def body(h_ref, *refs):
        @pl.when(pl.program_id(0) == 0)
        def _():
            for o_ref in refs[n:]:
                o_ref[...] = jnp.zeros_like(o_ref)

        hv = h_ref[...]
        for d_ref, o_ref in zip(refs[:n], refs[n:]):
            if len(d_ref.shape) == 3:
                for k in range(d_ref.shape[0]):
                    o_ref[k] += _mm_tn(d_ref[k], hv)
            else:
                o_ref[...] += _mm_tn(d_ref[...], hv)

    def in_spec(p):
        if p.ndim == 3:
            return pl.BlockSpec((p.shape[0], TK_GRAD, p.shape[2]), lambda kk: (0, kk, 0))
        return pl.BlockSpec((TK_GRAD, p.shape[1]), lambda kk: (kk, 0))

    out_shapes = [(p.shape[0], p.shape[2], D) if p.ndim == 3 else (p.shape[1], D) for p in pieces]
    return pl.pallas_call(
        body, name=name, grid=(s // TK_GRAD,),
        in_specs=[pl.BlockSpec((TK_GRAD, D), lambda kk: (kk, 0))] + [in_spec(p) for p in pieces],
        out_specs=[pl.BlockSpec(sh, lambda kk, nd=len(sh): (0,) * nd) for sh in out_shapes],
        out_shape=[jax.ShapeDtypeStruct(sh, F32) for sh in out_shapes],
        compiler_params=_params(("arbitrary",)),
    )(h, *pieces)


def _pad_win_t(w_in_t):
    z = lambda n: jnp.zeros((n, w_in_t.shape[1]), w_in_t.dtype)
    sl = lambda o, n: w_in_t[o:o + n]
    return jnp.concatenate([sl(O_MERGE, 2048), sl(O_GA, 512), sl(O_HQ, 512), sl(O_HF, 512), sl(O_HI, 512), sl(O_GB, 512),
                            sl(O_CQ, Q_LORA), sl(O_CKV, KV_LORA), z(64), sl(O_KPE, QK_ROPE), z(32)], axis=0)


def _pad_wuq(w_uq):
    w = w_uq.reshape(Q_LORA, NH, QK_NOPE + QK_ROPE)
    return jnp.pad(w, ((0, 0), (0, 0), (0, LANE - QK_NOPE - QK_ROPE))).reshape(Q_LORA, NH * LANE)


def _unpad_wuq(g):
    return g.reshape(Q_LORA, NH, LANE)[:, :, :QK_NOPE + QK_ROPE].reshape(Q_LORA, NH * (QK_NOPE + QK_ROPE))


def _pad_wukv(w_ukv):
    w = w_ukv.reshape(KV_LORA, NH, QK_NOPE + V_DIM)
    w_k = jnp.pad(w[:, :, :QK_NOPE], ((0, 0), (0, 0), (0, LANE - QK_NOPE))).reshape(KV_LORA, NH * LANE)
    wv = w[:, :, QK_NOPE:].reshape(KV_LORA, NH // 2, 2, 1, V_DIM)
    eye = jnp.eye(2, dtype=w.dtype).reshape(1, 1, 2, 2, 1)
    return w_k, (wv * eye).reshape(KV_LORA, NH * LANE)


def _unpad_wukv(gk, gv):
    gk = gk.reshape(KV_LORA, NH, LANE)[:, :, :QK_NOPE]
    gv = gv.reshape(KV_LORA, NH // 2, 2, 2, V_DIM)
    gv = jnp.stack([gv[:, :, 0, 0], gv[:, :, 1, 1]], axis=2).reshape(KV_LORA, NH, V_DIM)
    return jnp.concatenate([gk, gv], axis=-1).reshape(KV_LORA, NH * (QK_NOPE + V_DIM))


def _local_step(x, tgt, g_pre, w_in_t, b_gate, g_q, g_kv, lb_logits, g_hgrn, g_post, weights, exchange=None):
    s = x.shape[0]
    w_in_p = _pad_win_t(w_in_t)
    rc, rs1, rs2 = _rope_tables(s)
    g_hg = jnp.tile(g_hgrn, (1, NH))

    proj, h = _front_fwd(x, g_pre, w_in_p, weights.tokens)
    w_uq, w_ukv = weights.qkv(h)
    w_uq_p = _pad_wuq(w_uq)
    w_k_p, w_v_p = _pad_wukv(w_ukv)
    q, k, vv = _qkv_fwd(proj, g_q, g_kv, w_uq_p, w_k_p, w_v_p, rc, rs1, rs2)
    attn, lse = _attn_fwd(q, k, vv)
    o_raw, states = _hgrn_fwd(proj, lb_logits)
    wa, wb, w_out = weights.mid(o_raw)
    (loss, dout, dattn, dga, dor, dgb, dmg, d_wout, d_wa, d_wb, d_gpost, d_bgate, d_ghg) = _mid(
        proj, attn, o_raw, x, tgt, g_hg, b_gate, g_post, wa, wb, w_out)
    w_mg, w_ga, w_gb = _win_grad(h, [dmg, dga, dgb], "win_grad_mid")
    dh3, d_lbl = _hgrn_bwd(proj, lb_logits, states, dor)
    (w_h3,) = _win_grad(h, [dh3], "win_grad_hgrn")
    d_win_rest = jnp.concatenate([w_ga, w_h3[0], w_h3[1], w_h3[2], w_gb, w_mg], axis=0)
    early = dict(w_in_rest=d_win_rest, w_branch_a=d_wa, w_branch_b=d_wb, w_out=d_wout)
    token = exchange.start_early(early) if exchange else jnp.zeros((8, LANE), F32)
    dq, dk, dvv = _attn_bwd(q, k, vv, attn, dattn, lse, token)
    dcq, dckv, dkpe, d_wuq_p, d_wk_p, d_wv_p, d_gq, d_gkv = _qkv_bwd(proj, dq, dk, dvv, g_q, g_kv, w_uq_p, w_k_p, w_v_p, rc, rs1, rs2)
    w_cq, w_ckv, w_kpe = _win_grad(h, [dcq, dckv, dkpe], "win_grad_qkv")
    d_win_qkv = jnp.concatenate([w_cq, w_ckv, w_kpe[64:64 + QK_ROPE]], axis=0)
    late = dict(w_in_qkv=d_win_qkv, w_uq=_unpad_wuq(d_wuq_p), w_ukv=_unpad_wukv(d_wk_p, d_wv_p))
    token = exchange.start_late(late) if exchange else jnp.zeros((8, LANE), F32)
    grad_x, d_gpre = _front_bwd(x, dout, dmg, dga, dh3, dgb, dcq, dckv, dkpe, g_pre, w_in_p, token)
    vec_grads = dict(g_pre=d_gpre, b_gate=d_bgate, g_q=d_gq, g_kv=d_gkv, lb_logits=d_lbl, g_hgrn=d_ghg, g_post=d_gpost)
    return loss, grad_x, dict(early, **late), vec_grads


SHARD_SHAPES = (("w_in", (1416, 1024)), ("w_uq", (192, 768)), ("w_ukv", (256, 256)), ("w_branch_a", (512, 256)),
                ("w_branch_b", (512, 256)), ("w_out", (256, 1024)))
BIG = tuple(n for n, _ in SHARD_SHAPES)
ROW_SHARDED = ("w_in", "w_uq", "w_out")
GATHER_SPLIT_AXIS = dict(w_in=1, w_uq=0, w_ukv=0, w_branch_a=0, w_branch_b=0, w_out=0)
N_CHIPS = 4
QKV_ROWS = Q_LORA + KV_LORA + QK_ROPE


def _to_block(name, a):
    return a[0].T if name == "w_in" else a[0]


def _from_block(name, a):
    return a.T[None] if name == "w_in" else a[None]
VEC_ROWS = (("g_pre", 0, 1024), ("b_gate", 1, 2048), ("g_q", 2, 768), ("g_kv", 3, 256), ("g_hgrn", 6, 64), ("g_post", 7, 1024))
VEC_LB_ROW = 4
VEC_SHAPE = (8, 2048)


def _split_by_chip(name, g):
    a, b = dict(SHARD_SHAPES)[name]
    return g.reshape(N_CHIPS, a, b) if name in ROW_SHARDED else g.reshape(a, N_CHIPS, b).transpose(1, 0, 2)


def _join_chips(name, w):
    a, b = dict(SHARD_SHAPES)[name]
    return w.reshape(N_CHIPS * a, b) if name in ROW_SHARDED else w.transpose(1, 0, 2).reshape(a, N_CHIPS * b)


MESH = pl.DeviceIdType.MESH
HBM = pl.BlockSpec(memory_space=pltpu.HBM)


def _mesh_place():
    x, y, c = lax.axis_index("x"), lax.axis_index("y"), lax.axis_index("c")
    return x, y, c, 2 * x + y, [(1 - x, y), (x, 1 - y), (1 - x, 1 - y)]


def _remote(src, dst, send_sems, recv_sems, k, to):
    return pltpu.make_async_remote_copy(src_ref=src, dst_ref=dst, send_sem=send_sems.at[k], recv_sem=recv_sems.at[k],
                                        device_id=to, device_id_type=MESH)


def _gather_weights(shards, split_axes):
    n = len(shards)

    def body(*refs):
        srcs, outs = refs[:n], refs[n:2 * n]
        ici_send, ici_recv, d2d_send, d2d_recv, local_sems = refs[2 * n:]
        x, y, c, me, chips = _mesh_place()
        sibling = (x, y, 1 - c)

        def half(ref, k, which):
            size = shards[k].shape[split_axes[k]] // 2
            part = pl.ds(pl.multiple_of(which * size, size), size)
            return ref.at[part] if split_axes[k] == 0 else ref.at[:, part]

        own = [pltpu.make_async_copy(srcs[k], outs[k].at[me], local_sems.at[k]) for k in range(n)]
        for cp in own:
            cp.start()
        started = []
        for k in range(n):
            for j, (px, py) in enumerate(chips):
                cp = _remote(half(srcs[k], k, c), half(outs[k].at[me], k, c), ici_send, ici_recv, 3 * k + j, (px, py, c))
                cp.start()
                started.append(cp)
        for k in range(n):
            for j, (px, py) in enumerate(chips):
                landed = half(outs[k].at[2 * px + py], k, c)
                _remote(landed, landed, ici_send, ici_recv, 3 * k + j, (px, py, c)).wait_recv()
                cp = _remote(landed, landed, d2d_send, d2d_recv, 3 * k + j, sibling)
                cp.start()
                started.append(cp)
        for k in range(n):
            for j, (px, py) in enumerate(chips):
                other = half(outs[k].at[2 * px + py], k, 1 - c)
                _remote(other, other, d2d_send, d2d_recv, 3 * k + j, sibling).wait_recv()
        for cp in started:
            cp.wait_send()
        for cp in own:
            cp.wait()

    sems = pltpu.SemaphoreType.DMA((3 * n,))
    return pl.pallas_call(
        body, name="gather_weights", in_specs=[HBM] * n, out_specs=[HBM] * n,
        out_shape=[jax.ShapeDtypeStruct((N_CHIPS,) + s.shape, s.dtype) for s in shards],
        scratch_shapes=[sems, sems, sems, sems, pltpu.SemaphoreType.DMA((n,))],
        compiler_params=pltpu.CompilerParams(has_side_effects=True),
    )(*shards)


def _chip_exchange(srcs, name):
    n = len(srcs)

    def body(*refs):
        src_refs, outs = refs[:n], refs[n:2 * n]
        send_sems, recv_sems, local_sems = refs[2 * n:]
        x, y, c, me, chips = _mesh_place()

        def slab(k, t):
            return src_refs[k] if srcs[k].ndim == 2 else src_refs[k].at[t]

        own = [pltpu.make_async_copy(slab(k, me), outs[k].at[me], local_sems.at[k]) for k in range(n)]
        for cp in own:
            cp.start()
        sends = []
        for k in range(n):
            for j, (px, py) in enumerate(chips):
                cp = _remote(slab(k, 2 * px + py), outs[k].at[me], send_sems, recv_sems, 3 * k + j, (px, py, c))
                cp.start()
                sends.append(cp)
        for k in range(n):
            for j, (px, py) in enumerate(chips):
                _remote(slab(k, me), outs[k].at[2 * px + py], send_sems, recv_sems, 3 * k + j, (px, py, c)).wait_recv()
        for cp in sends:
            cp.wait_send()
        for cp in own:
            cp.wait()

    sems = pltpu.SemaphoreType.DMA((3 * n,))
    return pl.pallas_call(
        body, name=name, in_specs=[HBM] * n, out_specs=[HBM] * n,
        out_shape=[jax.ShapeDtypeStruct((N_CHIPS,) + s.shape[-2:], s.dtype) for s in srcs],
        scratch_shapes=[sems, sems, pltpu.SemaphoreType.DMA((n,))],
        compiler_params=pltpu.CompilerParams(has_side_effects=True),
    )(*srcs)


def _sibling_exchange(srcs, name, after=None):
    n = len(srcs)
    extra = [] if after is None else [after]

    def body(*refs):
        src_refs, outs = refs[:n], refs[n + len(extra):2 * n + len(extra)]
        send_sems, recv_sems = refs[2 * n + len(extra):]
        sibling = (lax.axis_index("x"), lax.axis_index("y"), 1 - lax.axis_index("c"))
        copies = [_remote(src_refs[k], outs[k], send_sems, recv_sems, k, sibling) for k in range(n)]
        for cp in copies:
            cp.start()
        for cp in copies:
            cp.wait()

    sems = pltpu.SemaphoreType.DMA((n,))
    return pl.pallas_call(
        body, name=name, in_specs=[HBM] * n + [pl.BlockSpec(memory_space=pl.ANY)] * len(extra), out_specs=[HBM] * n,
        out_shape=[jax.ShapeDtypeStruct(s.shape, s.dtype) for s in srcs],
        scratch_shapes=[sems, sems],
        compiler_params=pltpu.CompilerParams(has_side_effects=True),
    )(*srcs, *extra)


SEM = pl.BlockSpec(memory_space=pltpu.SEMAPHORE)
DATAFLOW = pltpu.SideEffectType.DATAFLOW_SIDE_EFFECTING


def _exchange_copies(srcs, to_first, src_refs, land_refs, send_sems, recv_sems):
    x, y, c, me, chips = _mesh_place()
    n = len(srcs)
    sends, recvs = [], []
    for k in range(n):
        if k in to_first:
            base = 3 * n + 4 * to_first.index(k)
            sends.append((me != 0, pltpu.make_async_remote_copy(
                src_ref=src_refs[k], dst_ref=land_refs[k].at[me], send_sem=send_sems.at[base], recv_sem=recv_sems.at[base + me],
                device_id=(0, 0, c), device_id_type=MESH)))
            for s in range(1, N_CHIPS):
                recvs.append((me == 0, pltpu.make_async_remote_copy(
                    src_ref=src_refs[k], dst_ref=land_refs[k].at[s], send_sem=send_sems.at[base], recv_sem=recv_sems.at[base + s],
                    device_id=(s // 2, s % 2, c), device_id_type=MESH)))
        else:
            slab = (lambda t, k=k: src_refs[k]) if srcs[k].ndim == 2 else (lambda t, k=k: src_refs[k].at[t])
            for j, (px, py) in enumerate(chips):
                sends.append((None, _remote(slab(2 * px + py), land_refs[k].at[me], send_sems, recv_sems, 3 * k + j, (px, py, c))))
                recvs.append((None, _remote(slab(me), land_refs[k].at[2 * px + py], send_sems, recv_sems, 3 * k + j, (px, py, c))))
    return sends, recvs


def _when(pred, fn):
    if pred is None:
        fn()
    else:
        pl.when(pred)(fn)


def _exchange_start(srcs, to_first, name, after=None):
    n = len(srcs)
    n_sems = 3 * n + 4 * len(to_first)
    lands = [lax.empty((N_CHIPS,) + s.shape[-2:], s.dtype) for s in srcs]
    extra = [] if after is None else [after]

    def body(*refs):
        src_refs, land_refs = refs[:n], refs[n:2 * n]
        send_sems, recv_sems, token = refs[2 * n + len(extra)], refs[2 * n + len(extra) + 1], refs[-1]
        sends, _ = _exchange_copies(srcs, to_first, src_refs, land_refs, send_sems, recv_sems)
        for pred, cp in sends:
            _when(pred, cp.start)
        token[...] = jnp.zeros_like(token)

    hbm = lambda a: pltpu.HBM(a.shape, a.dtype)
    res = pl.pallas_call(
        body, name=name,
        out_shape=[pltpu.SemaphoreType.DMA((n_sems,)), pltpu.SemaphoreType.DMA((n_sems,))] + [hbm(a) for a in srcs + lands]
        + [jax.ShapeDtypeStruct((8, LANE), F32)],
        in_specs=[HBM] * (2 * n) + [pl.BlockSpec(memory_space=pl.ANY)] * len(extra),
        out_specs=[SEM, SEM] + [HBM] * (2 * n) + [pl.BlockSpec(memory_space=pltpu.VMEM)],
        input_output_aliases={i: 2 + i for i in range(2 * n)},
        compiler_params=pltpu.CompilerParams(has_side_effects=DATAFLOW),
    )(*[pltpu.with_memory_space_constraint(a, pltpu.HBM) for a in srcs + lands], *extra)
    return res[:-1], res[-1]


def _exchange_wait(srcs, to_first, started, after, name):
    n = len(srcs)
    send_sems, recv_sems, thru = started[0], started[1], started[2:]

    def body(*refs):
        src_refs, land_refs, send_ref, recv_ref = refs[:n], refs[n:2 * n], refs[2 * n], refs[2 * n + 1]
        sends, recvs = _exchange_copies(srcs, to_first, src_refs, land_refs, send_ref, recv_ref)
        for pred, cp in sends:
            _when(pred, cp.wait_send)
        for pred, cp in recvs:
            _when(pred, cp.wait_recv)

    res = pl.pallas_call(
        body, name=name, out_shape=[pltpu.HBM(a.shape, a.dtype) for a in thru],
        in_specs=[HBM] * (2 * n) + [SEM, SEM, pl.BlockSpec(memory_space=pl.ANY)], out_specs=[HBM] * (2 * n),
        input_output_aliases={i: i for i in range(2 * n)},
        compiler_params=pltpu.CompilerParams(has_side_effects=DATAFLOW),
    )(*thru, send_sems, recv_sems, after)
    return res[n:]


ROW_TILE = 256
COL_TILE = 256


def _block_tiling(a, b):
    if a <= ROW_TILE or a % ROW_TILE == 0:
        ta = min(a, ROW_TILE)
        return a // ta, (ta, b), lambda i: (i, 0)
    return b // COL_TILE, (a, COL_TILE), lambda i: (0, i)


def _sum_chips(parts, name):
    _, a, b = parts.shape
    steps, tile, at = _block_tiling(a, b)

    def body(p_ref, o_ref):
        f = lambda t: p_ref[t].astype(F32)
        o_ref[...] = ((f(0) + f(1)) + f(2)) + f(3)

    return pl.pallas_call(
        body, name=name, grid=(steps,),
        in_specs=[pl.BlockSpec((N_CHIPS,) + tile, lambda i: (0,) + at(i))],
        out_specs=pl.BlockSpec(tile, at),
        out_shape=jax.ShapeDtypeStruct((a, b), F32),
        compiler_params=_params(("parallel",)),
    )(parts)


def _sum_landed(land, own, name, first_land=None, first_own=None):
    _, a, b = land.shape
    steps, tile, at = _block_tiling(a, b)
    extra = first_land is not None

    def body(*refs):
        p_ref, own_ref, o_ref = refs[0], refs[1], refs[-1]
        me = 2 * lax.axis_index("x") + lax.axis_index("y")
        own = own_ref[...].astype(F32)
        slot = lambda t: jnp.where(me == t, own, p_ref[t].astype(F32))
        o_ref[...] = ((slot(0) + slot(1)) + slot(2)) + slot(3)
        if extra:
            fp_ref, fo_ref = refs[2], refs[3]
            r = fo_ref.shape[0]

            @pl.when(me == 0)
            def _():
                f = lambda t: fp_ref[t].astype(F32)
                rows = pl.ds(pl.multiple_of(lax.axis_index("c") * r, 8), r)
                o_ref[rows, :] += ((fo_ref[...].astype(F32) + f(1)) + f(2)) + f(3)

    in_specs = [pl.BlockSpec((N_CHIPS,) + tile, lambda i: (0,) + at(i)), pl.BlockSpec(tile, at)]
    args = [land, own]
    if extra:
        r = first_own.shape[0]
        assert tile[0] == a, "the extra rows need whole columns in a step"
        in_specs += [pl.BlockSpec((N_CHIPS, r, tile[1]), lambda i: (0,) + at(i)), pl.BlockSpec((r, tile[1]), at)]
        args += [first_land, first_own]
    return pl.pallas_call(
        body, name=name, grid=(steps,), in_specs=in_specs, out_specs=pl.BlockSpec(tile, at),
        out_shape=jax.ShapeDtypeStruct((a, b), F32), compiler_params=_params(("parallel",)),
    )(*args)


def _add_cast(a, b, name):
    def body(a_ref, b_ref, o_ref):
        o_ref[...] = (a_ref[...] + b_ref[...]).astype(BF16)

    return pl.pallas_call(body, name=name, out_shape=jax.ShapeDtypeStruct(a.shape, BF16),
                          compiler_params=_params(()))(a, b)


class _LaterWeights:
    QKV = ("w_uq", "w_ukv")
    MID = ("w_branch_a", "w_branch_b", "w_out")

    def __init__(self, blocks, after):
        self.blocks = blocks
        self.qkv_started, t1 = _exchange_start([blocks[n] for n in self.QKV], (), "weights_qkv_start", after)
        self.mid_started, t2 = _exchange_start([blocks[n] for n in self.MID], (), "weights_mid_start", after)
        self.tokens = [t1, t2]

    def _whole(self, names, started, after, name):
        landed = _exchange_wait([self.blocks[n] for n in names], (), started, after, name)
        me = 2 * lax.axis_index("x") + lax.axis_index("y")
        return [_join_chips(n, lax.dynamic_update_index_in_dim(land, self.blocks[n], me, 0)) for n, land in zip(names, landed)]

    def qkv(self, after):
        return self._whole(self.QKV, self.qkv_started, after, "weights_qkv_wait")

    def mid(self, after):
        return self._whole(self.MID, self.mid_started, after, "weights_mid_wait")


class _GradExchange:
    EARLY = ("w_in", "w_branch_a", "w_branch_b", "w_out")
    LATE = ("w_uq", "w_ukv")

    def __init__(self, state):
        self.state = state
        self.outs = {}

    @staticmethod
    def _own(slabs):
        return lax.dynamic_index_in_dim(slabs, 2 * lax.axis_index("x") + lax.axis_index("y"), axis=0, keepdims=False)

    def start_early(self, g):
        full = jnp.concatenate([jnp.zeros((QKV_ROWS, D), F32), g["w_in_rest"]], axis=0)
        g = dict(g, w_in=full)
        self.early = [_split_by_chip(n, g[n]).astype(BF16) for n in self.EARLY]
        self.early_started, token = _exchange_start(self.early, (), "grads_early_start")
        return token

    def start_late(self, g):
        self.early_landed = _exchange_wait(self.early, (), self.early_started, g["w_uq"], "grads_early_wait")
        half = QKV_ROWS // 2
        c = lax.axis_index("c")
        mine = lax.dynamic_slice_in_dim(g["w_in_qkv"], c * half, half, axis=0)
        (theirs,) = _sibling_exchange([lax.dynamic_slice_in_dim(g["w_in_qkv"], (1 - c) * half, half, axis=0)], "sibling_qkv_rows")
        self.late = [_split_by_chip(n, g[n]).astype(BF16) for n in self.LATE] + [_add_cast(mine, theirs, "add_qkv_rows")]
        self.late_started, token = _exchange_start(self.late, (2,), "grads_late_start")
        names = self.EARLY[1:]
        mine = [_sum_landed(land, self._own(slabs), "sum_" + n) for n, slabs, land in list(zip(self.EARLY, self.early, self.early_landed))[1:]]
        theirs = _sibling_exchange(mine, "sibling_early", after=token)
        for n, a, b in zip(names, mine, theirs):
            self.outs[n] = _adamw(a, b, *self.state[n], "adamw_" + n)
        return self.outs[names[-1]][0]

    def finish(self, after):
        late_landed = _exchange_wait(self.late, (2,), self.late_started, after, "grads_late_wait")
        sums = {"w_in": _sum_landed(self.early_landed[0], self._own(self.early[0]), "sum_w_in",
                                    first_land=late_landed[2], first_own=self.late[2])}
        for n, slabs, land in zip(self.LATE, self.late, late_landed):
            sums[n] = _sum_landed(land, self._own(slabs), "sum_" + n)
        return sums


def _adamw_math(g, w, m, v):
    nm = ADAM_B1 * m + (1.0 - ADAM_B1) * g
    nv = ADAM_B2 * v + (1.0 - ADAM_B2) * (g * g)
    m_hat = nm / (1.0 - ADAM_B1 ** ADAM_STEP)
    v_hat = nv / (1.0 - ADAM_B2 ** ADAM_STEP)
    return -ADAM_LR * (m_hat / (jnp.sqrt(v_hat) + ADAM_EPS) + ADAM_WD * w), nm, nv


def _adamw(p_mine, p_sibling, w, m, v, name):
    a, b = p_mine.shape
    steps, tile, at = _block_tiling(a, b)

    def body(a_ref, b_ref, w_ref, m_ref, v_ref, g_ref, d_ref, nm_ref, nv_ref):
        g = a_ref[...] + b_ref[...]
        g_ref[...] = g
        d_ref[...], nm_ref[...], nv_ref[...] = _adamw_math(g, w_ref[...], m_ref[...], v_ref[...])

    spec = pl.BlockSpec(tile, at)
    sds = jax.ShapeDtypeStruct((a, b), F32)
    return pl.pallas_call(
        body, name=name, grid=(steps,), in_specs=[spec] * 5, out_specs=[spec] * 4, out_shape=[sds] * 4,
        compiler_params=_params(("parallel",)),
    )(p_mine, p_sibling, w, m, v)


LOSS_AT = (2, 1024)


def _vec_pack(vg, loss):
    names = [n for n, _, _ in VEC_ROWS]

    def body(*refs):
        o_ref = refs[-1]
        lb_ref, loss_ref = refs[len(names)], refs[len(names) + 1]
        o_ref[...] = jnp.zeros_like(o_ref)
        o_ref[LOSS_AT[0]:LOSS_AT[0] + 1, LOSS_AT[1]:LOSS_AT[1] + LANE] = jnp.broadcast_to(loss_ref[...], (1, LANE))
        for (name, row, size), ref in zip(VEC_ROWS, refs):
            if name == "g_hgrn":
                r = lax.broadcasted_iota(jnp.int32, (NH * V_DIM, LANE), 0)
                c = lax.broadcasted_iota(jnp.int32, (NH * V_DIM, LANE), 1)
                fold = ((r % V_DIM) == c).astype(F32)
                o_ref[row:row + 1, 0:LANE] = jnp.dot(ref[...], fold, precision=HIGHEST, preferred_element_type=F32)
            else:
                o_ref[row:row + 1, 0:size] = ref[...]
        o_ref[VEC_LB_ROW:VEC_LB_ROW + 2, 0:512] = lb_ref[...]

    return pl.pallas_call(body, name="vec_pack", out_shape=jax.ShapeDtypeStruct(VEC_SHAPE, F32))(
        *[vg[n] for n in names], vg["lb_logits"], loss)


def _adamw_vec(p_mine, p_sibling, w, m, v):
    names = [n for n, _, _ in VEC_ROWS] + ["lb_logits"]
    k = len(names)

    def body(a_ref, b_ref, *refs):
        ins, outs = refs[:3 * k], refs[3 * k:]
        at = (slice(LOSS_AT[0], LOSS_AT[0] + 1), slice(LOSS_AT[1], LOSS_AT[1] + LANE))
        outs[-1][...] = a_ref[at] + b_ref[at]
        for i, name in enumerate(names):
            if name == "lb_logits":
                rows, cols = slice(VEC_LB_ROW, VEC_LB_ROW + 2), slice(0, 512)
            else:
                _, row, size = VEC_ROWS[i]
                rows, cols = slice(row, row + 1), slice(0, size)
            g = a_ref[rows, cols] + b_ref[rows, cols]
            d, nm, nv = _adamw_math(g, ins[i][...], ins[k + i][...], ins[2 * k + i][...])
            for o_ref, val in zip(outs[4 * i:4 * i + 4], (g, d, nm, nv)):
                o_ref[...] = val

    shapes = [jax.ShapeDtypeStruct(w[n].shape, F32) for n in names for _ in range(4)] + [jax.ShapeDtypeStruct((1, LANE), F32)]
    res = pl.pallas_call(body, name="adamw_vec", out_shape=shapes)(
        p_mine, p_sibling, *[w[n] for n in names], *[m[n] for n in names], *[v[n] for n in names])
    return [{n: res[4 * i + j] for i, n in enumerate(names)} for j in range(4)], res[-1]


WEIGHTS = ("g_pre", "w_in", "b_gate", "g_q", "w_uq", "g_kv", "w_ukv", "lb_logits", "g_hgrn", "w_branch_a", "w_branch_b", "w_out", "g_post")


def kernel(x, g_pre, w_in, b_gate, g_q, w_uq, g_kv, w_ukv, lb_logits, g_hgrn, w_branch_a, w_branch_b, w_out, g_post, loss_target, m_g_pre, m_w_in, m_b_gate, m_g_q, m_w_uq, m_g_kv, m_w_ukv, m_lb_logits, m_g_hgrn, m_w_branch_a, m_w_branch_b, m_w_out, m_g_post, v_g_pre, v_w_in, v_b_gate, v_g_q, v_w_uq, v_g_kv, v_w_ukv, v_lb_logits, v_g_hgrn, v_w_branch_a, v_w_branch_b, v_w_out, v_g_post):
    w = dict(g_pre=g_pre, w_in=w_in, b_gate=b_gate, g_q=g_q, w_uq=w_uq, g_kv=g_kv, w_ukv=w_ukv, lb_logits=lb_logits, g_hgrn=g_hgrn,
             w_branch_a=w_branch_a, w_branch_b=w_branch_b, w_out=w_out, g_post=g_post)
    m = dict(g_pre=m_g_pre, w_in=m_w_in, b_gate=m_b_gate, g_q=m_g_q, w_uq=m_w_uq, g_kv=m_g_kv, w_ukv=m_w_ukv, lb_logits=m_lb_logits,
             g_hgrn=m_g_hgrn, w_branch_a=m_w_branch_a, w_branch_b=m_w_branch_b, w_out=m_w_out, g_post=m_g_post)
    v = dict(g_pre=v_g_pre, w_in=v_w_in, b_gate=v_b_gate, g_q=v_g_q, w_uq=v_w_uq, g_kv=v_g_kv, w_ukv=v_w_ukv, lb_logits=v_lb_logits,
             g_hgrn=v_g_hgrn, w_branch_a=v_w_branch_a, w_branch_b=v_w_branch_b, w_out=v_w_out, g_post=v_g_post)
    blocks = {n: _to_block(n, w[n]).astype(BF16) for n in BIG}
    (w_in_all,) = _gather_weights([blocks["w_in"]], [GATHER_SPLIT_AXIS["w_in"]])
    weights = _LaterWeights(blocks, w_in_all)
    state = {n: [_to_block(n, t[n]) for t in (w, m, v)] for n in BIG}
    exchange = _GradExchange(state)
    loss, grad_x, _, vec_grads = _local_step(
        x[0], loss_target[0], g_pre, _join_chips("w_in", w_in_all), b_gate, g_q, g_kv, lb_logits, g_hgrn, g_post, weights, exchange)
    sums = exchange.finish(grad_x)
    (vec_landed,) = _chip_exchange([_vec_pack(vec_grads, loss)], "scatter_vec")
    rest = tuple(sums)
    mine = [sums[n] for n in rest] + [_sum_chips(vec_landed, "sum_vec")]
    theirs = _sibling_exchange(mine, "sibling_grads")
    done = dict(exchange.outs)
    for k, n in enumerate(rest):
        done[n] = _adamw(mine[k], theirs[k], *state[n], "adamw_" + n)
    outs = [{}, {}, {}, {}]
    for n in BIG:
        for o, val in zip(outs, done[n]):
            o[n] = _from_block(n, val)
    vec_outs, total = _adamw_vec(mine[-1], theirs[-1], w, m, v)
    for o, vals in zip(outs, vec_outs):
        o.update(vals)
    return (total[0, 0], grad_x[None], *[o[n] for o in outs for n in WEIGHTS])
```

```python
import functools
import math

import numpy as np
import jax
import jax.numpy as jnp
from jax import lax
from jax.experimental import pallas as pl
from jax.experimental.pallas import tpu as pltpu

F32 = jnp.float32
BF16 = jnp.bfloat16
HIGHEST = lax.Precision.HIGHEST

D = 1024
NH = 8
QK_NOPE, QK_ROPE, V_DIM = 64, 32, 64
Q_LORA, KV_LORA = 768, 256
CHUNK = 64
HG_BLOCK = 32
EPS = 1e-6
D_IN = 5664
LANE = 128
P_MERGE, P_GA, P_HQ, P_HF, P_HI, P_GB, P_CQ, P_CKV, P_KPE = 0, 2048, 2560, 3072, 3584, 4096, 4608, 5376, 5632
D_P = 5760
O_CQ, O_CKV, O_KPE, O_GA, O_HQ, O_HF, O_HI, O_GB, O_MERGE = 0, 768, 1024, 1056, 1568, 2080, 2592, 3104, 3616

TM = 256
TM_MID = 256
TQ = 1024
ONES_LANE = (LANE - 1, 0)
TH = 256
HG_PAIRS = 4
VMEM_LIMIT = 56 * 1024 * 1024

ADAM_LR, ADAM_B1, ADAM_B2, ADAM_EPS, ADAM_WD, ADAM_STEP = 0.001, 0.9, 0.999, 1e-08, 0.01, 10

NT_DIMS = (((1,), (1,)), ((), ()))
TN_DIMS = (((0,), (0,)), ((), ()))


def _params(sem):
    return pltpu.CompilerParams(dimension_semantics=sem, vmem_limit_bytes=VMEM_LIMIT)


def _mm(a, b):
    return jnp.dot(a, b, preferred_element_type=F32)


def _mm_nt(a, b):
    return lax.dot_general(a, b, NT_DIMS, preferred_element_type=F32)


def _mm_tn(a, b):
    return lax.dot_general(a, b, TN_DIMS, preferred_element_type=F32)


def _sigmoid(z):
    return jax.nn.sigmoid(z)


def _rope(v, c, s1, s2):
    return v * c + pltpu.roll(v, 112, 1) * s1 + pltpu.roll(v, 16, 1) * s2


def _rope_t(dy, c, s1, s2):
    return dy * c + pltpu.roll(dy * s1, 16, 1) + pltpu.roll(dy * s2, 112, 1)


def _rope_tables(s):
    inv = 10000.0 ** (-jnp.arange(0, QK_ROPE, 2, dtype=F32) / QK_ROPE)
    ang = jnp.arange(s, dtype=F32)[:, None] * inv[None, :]
    cos, sin = jnp.cos(ang), jnp.sin(ang)
    z64, z32, o64, o32 = jnp.zeros((s, 64), F32), jnp.zeros((s, 32), F32), jnp.ones((s, 64), F32), jnp.ones((s, 32), F32)
    z16 = jnp.zeros((s, 16), F32)
    c = jnp.concatenate([o64, cos, cos, o32], axis=1)
    s1 = jnp.concatenate([z64, -sin, z16, z32], axis=1)
    s2 = jnp.concatenate([z64, z16, sin, z32], axis=1)
    return c, s1, s2


def _front_fwd(x, g_pre, w_in_pt, tokens=()):
    s = x.shape[0]
    tokens = list(tokens)

    def body(x_ref, g_ref, w_ref, *refs):
        o_ref, h_ref = refs[len(tokens):]
        xv = x_ref[...]
        r = lax.rsqrt(jnp.mean(xv * xv, axis=-1, keepdims=True) + EPS)
        h = ((xv * r) * g_ref[...]).astype(BF16)
        h_ref[...] = h
        o_ref[...] = _mm_nt(h, w_ref[...])

    return pl.pallas_call(
        body, name="front_fwd", grid=(s // TM,),
        in_specs=[pl.BlockSpec((TM, D), lambda i: (i, 0)), pl.BlockSpec((1, D), lambda i: (0, 0)),
                  pl.BlockSpec((D_P, D), lambda i: (0, 0))] + [pl.BlockSpec((8, LANE), lambda i: (0, 0))] * len(tokens),
        out_specs=[pl.BlockSpec((TM, D_P), lambda i: (i, 0)), pl.BlockSpec((TM, D), lambda i: (i, 0))],
        out_shape=[jax.ShapeDtypeStruct((s, D_P), F32), jax.ShapeDtypeStruct((s, D), BF16)],
        compiler_params=_params(("parallel",)),
    )(x, g_pre, w_in_pt, *tokens)


def _norm_rows(v, g):
    r = lax.rsqrt(jnp.mean(v * v, axis=-1, keepdims=True) + EPS)
    return (v * r) * g, r


def _qkv_fwd(proj, g_q, g_kv, w_uq_p, w_k_p, w_v_p, rc, rs1, rs2):
    s = proj.shape[0]

    def body(cq_ref, ckv_ref, kpe_ref, gq_ref, gkv_ref, wq_ref, wk_ref, wv_ref, c_ref, s1_ref, s2_ref, q_ref, k_ref, v_ref):
        c, s1, s2 = c_ref[...], s1_ref[...], s2_ref[...]
        cqn, _ = _norm_rows(cq_ref[...], gq_ref[...])
        ckvn, _ = _norm_rows(ckv_ref[...], gkv_ref[...])
        ckvn = ckvn.astype(BF16)
        qf = _mm(cqn.astype(BF16), wq_ref[...])
        kf = _mm(ckvn, wk_ref[...])
        vf = _mm(ckvn, wv_ref[...])
        kpe = _rope(kpe_ref[...], c, s1, s2)
        lane = lax.broadcasted_iota(jnp.int32, (TM, LANE), 1)
        for h in range(NH):
            blk = slice(h * LANE, (h + 1) * LANE)
            q_ref[h] = _rope(qf[:, blk], c, s1, s2).astype(BF16)
            k_ref[h] = (kf[:, blk] + kpe).astype(BF16)
            v_ref[h] = jnp.where(lane == ONES_LANE[h % 2], 1.0, vf[:, blk]).astype(BF16)

    row = lambda w, j: pl.BlockSpec((TM, w), lambda i: (i, j))
    full = lambda a: pl.BlockSpec(a.shape, lambda i: (0,) * a.ndim)
    hs = jax.ShapeDtypeStruct((NH, s, LANE), BF16)
    return pl.pallas_call(
        body, name="qkv_fwd", grid=(s // TM,),
        in_specs=[row(Q_LORA, P_CQ // Q_LORA), row(KV_LORA, P_CKV // KV_LORA), row(LANE, P_KPE // LANE),
                  full(g_q), full(g_kv), full(w_uq_p), full(w_k_p), full(w_v_p), row(LANE, 0), row(LANE, 0), row(LANE, 0)],
        out_specs=[pl.BlockSpec((NH, TM, LANE), lambda i: (0, i, 0))] * 3,
        out_shape=[hs, hs, hs],
        compiler_params=_params(("parallel",)),
    )(proj, proj, proj, g_q, g_kv, w_uq_p, w_k_p, w_v_p, rc, rs1, rs2)


LOG2E = 1.4426950408889634
QK_SCALE2 = LOG2E / math.sqrt(QK_NOPE + QK_ROPE)


def _diag_visible():
    row = lax.broadcasted_iota(jnp.int32, (TQ, TQ), 0)
    col = lax.broadcasted_iota(jnp.int32, (TQ, TQ), 1)
    return (col // CHUNK) <= (row // CHUNK)


def _attn_fwd(q, k, vv):
    s = q.shape[1]

    def body(q_ref, k_ref, v_ref, o_ref, lse_ref):
        i = pl.program_id(1)
        qs = (q_ref[0], q_ref[1])

        def tile(hh, t, carry, diag):
            m, acc = carry
            rows = pl.ds(pl.multiple_of(t * TQ, TQ), TQ)
            sc = _mm_nt(qs[hh], k_ref[hh, rows, :])
            if diag:
                sc = jnp.where(_diag_visible(), sc, -jnp.inf)
            m_new = jnp.maximum(m, jnp.max(sc, axis=-1, keepdims=True))
            alpha = jnp.exp2((m - m_new) * QK_SCALE2)
            p = jnp.exp2((sc - m_new) * QK_SCALE2).astype(BF16)
            acc = alpha * acc + _mm(p, v_ref[hh, rows, :])
            return m_new, acc

        def step(t, carry):
            return tile(0, t, carry[0], False), tile(1, t, carry[1], False)

        init = (jnp.full((TQ, 1), -jnp.inf, F32), jnp.zeros((TQ, LANE), F32))
        carry = lax.fori_loop(0, i, step, (init, init))
        lane = lax.broadcasted_iota(jnp.int32, (TQ, LANE), 1)
        out = jnp.zeros((TQ, LANE), F32)
        for hh in range(2):
            m, acc = tile(hh, i, carry[hh], True)
            l = jnp.sum(jnp.where(lane == ONES_LANE[hh], acc, 0.0), axis=-1, keepdims=True)
            out = out + jnp.where((lane < V_DIM) == (hh == 0), acc, 0.0) / l
            lse_ref[hh] = jnp.broadcast_to(m * QK_SCALE2 + jnp.log(l) * LOG2E, (TQ, LANE))
        o_ref[...] = out

    return pl.pallas_call(
        body, name="attn_fwd", grid=(NH // 2, s // TQ),
        in_specs=[pl.BlockSpec((2, TQ, LANE), lambda p, i: (p, i, 0)), pl.BlockSpec((2, s, LANE), lambda p, i: (p, 0, 0)),
                  pl.BlockSpec((2, s, LANE), lambda p, i: (p, 0, 0))],
        out_specs=[pl.BlockSpec((TQ, LANE), lambda p, i: (i, p)), pl.BlockSpec((2, TQ, LANE), lambda p, i: (p, i, 0))],
        out_shape=[jax.ShapeDtypeStruct((s, NH * V_DIM), F32), jax.ShapeDtypeStruct((NH, s, LANE), F32)],
        compiler_params=_params(("parallel", "parallel")),
    )(q, k, vv)


def _lower_bound(lbl):
    a0, a1 = lbl[0:1, :], lbl[1:2, :]
    mx = jnp.maximum(a0, a1)
    e0, e1 = jnp.exp(a0 - mx), jnp.exp(a1 - mx)
    return e0 / (e0 + e1)


def _chunk_cumsum(v, reverse=False):
    pos = lax.broadcasted_iota(jnp.int32, v.shape, 0) % HG_BLOCK
    s = 1
    while s < HG_BLOCK:
        if reverse:
            v = v + jnp.where(pos < HG_BLOCK - s, pltpu.roll(v, TH - s, 0), 0.0)
        else:
            v = v + jnp.where(pos >= s, pltpu.roll(v, s, 0), 0.0)
        s *= 2
    return v


def _hgrn_gates(hq, hf, lb):
    sig = _sigmoid(hf)
    f = lb + (1.0 - lb) * sig
    g = jnp.log(f)
    kk = 1.0 - f
    r = lax.broadcasted_iota(jnp.int32, (TH, TH), 0)
    c = lax.broadcasted_iota(jnp.int32, (TH, TH), 1)
    tri = ((r // HG_BLOCK) == (c // HG_BLOCK)) & (r >= c)
    cum = _chunk_cumsum(g)
    nch = TH // HG_BLOCK
    total = _chunks(cum)[:, HG_BLOCK - 1:HG_BLOCK, :]
    lastb = jnp.broadcast_to(total, (nch, HG_BLOCK, LANE)).reshape(TH, LANE)
    e, ei, ee = jnp.exp(cum), jnp.exp(-cum), jnp.exp(lastb - cum)
    return dict(sig=sig, f=f, kk=kk, tri=tri, cum=cum, total=total, e=e, ei=ei, ee=ee, qd=hq * e, ki=kk * ei, ke=kk * ee)


def _chunks(v):
    return v.reshape(TH // HG_BLOCK, HG_BLOCK, v.shape[-1])


def _bmm_nt(a, b):
    return lax.dot_general(a, b, (((2,), (2,)), ((0,), (0,))), preferred_element_type=F32)


def _bmm_nn(a, b):
    return lax.dot_general(a, b, (((2,), (1,)), ((0,), (0,))), preferred_element_type=F32)


def _bmm_tn(a, b):
    return lax.dot_general(a, b, (((1,), (1,)), ((0,), (0,))), preferred_element_type=F32)


def _pair_masks():
    lane = lax.broadcasted_iota(jnp.int32, (TH, LANE), 1)
    kr = lax.broadcasted_iota(jnp.int32, (LANE, LANE), 0)
    kc = lax.broadcasted_iota(jnp.int32, (LANE, LANE), 1)
    return lane < 64, (kr // 64) == (kc // 64)


def _hgrn_fwd(proj, lbl):
    s = proj.shape[0]
    nch = TH // HG_BLOCK

    def body(hq_ref, hf_ref, hi_ref, lbl_ref, o_ref, st_ref, st):
        @pl.when(pl.program_id(1) == 0)
        def _():
            st[...] = jnp.zeros_like(st)

        m0, bd = _pair_masks()
        for u in range(HG_PAIRS):
            lanes = slice(u * LANE, (u + 1) * LANE)
            lb = _lower_bound(lbl_ref[:, lanes])
            gt = _hgrn_gates(hq_ref[:, lanes], hf_ref[:, lanes], lb)
            v_b = hi_ref[:, lanes].astype(BF16)
            qd, ki_b, ke_b = gt["qd"], gt["ki"].astype(BF16), gt["ke"].astype(BF16)
            qd_b = qd.astype(BF16)
            o = jnp.zeros((TH, LANE), F32)
            for hh in range(2):
                mh = m0 if hh == 0 else jnp.logical_not(m0)
                a = jnp.where(gt["tri"], _mm_nt(jnp.where(mh, qd, 0.0).astype(BF16), ki_b), 0.0)
                o = jnp.where(mh, _mm(a.astype(BF16), v_b), o)
            upd = _bmm_tn(_chunks(v_b), _chunks(ke_b))
            decay = jnp.exp(gt["total"])
            cur, entering = st[u], []
            for n in range(nch):
                entering.append(cur)
                cur = decay[n] * cur + jnp.where(bd, upd[n], 0.0)
            st[u] = cur
            entering = jnp.stack(entering)
            st_ref[u] = entering
            o_ref[:, lanes] = o + _bmm_nt(_chunks(qd_b), entering.astype(BF16)).reshape(TH, LANE)

    wide = HG_PAIRS * LANE
    col = lambda base: pl.BlockSpec((TH, wide), lambda p, i: (i, base // wide + p))
    return pl.pallas_call(
        body, name="hgrn_fwd", grid=(NH // 2 // HG_PAIRS, s // TH),
        in_specs=[col(P_HQ), col(P_HF), col(P_HI), pl.BlockSpec((2, wide), lambda p, i: (0, p))],
        out_specs=[pl.BlockSpec((TH, wide), lambda p, i: (i, p)),
                   pl.BlockSpec((HG_PAIRS, nch, LANE, LANE), lambda p, i: (p, i, 0, 0))],
        out_shape=[jax.ShapeDtypeStruct((s, 512), F32), jax.ShapeDtypeStruct((NH // 2, s // HG_BLOCK, LANE, LANE), F32)],
        scratch_shapes=[pltpu.VMEM((HG_PAIRS, LANE, LANE), F32)],
        compiler_params=_params(("parallel", "arbitrary")),
    )(proj, proj, proj, lbl)


def _group_sum(v):
    low = lax.broadcasted_iota(jnp.int32, (v.shape[0], LANE), 1) < V_DIM
    blocks = []
    for b in range(v.shape[1] // LANE):
        blk = v[:, b * LANE:(b + 1) * LANE]
        s_low = jnp.sum(jnp.where(low, blk, 0.0), axis=-1, keepdims=True)
        s_high = jnp.sum(jnp.where(low, 0.0, blk), axis=-1, keepdims=True)
        blocks.append(jnp.where(low, s_low, s_high))
    return jnp.concatenate(blocks, axis=1)


def _dsilu(z, sg):
    return sg * (1.0 + z * (1.0 - sg))


def _mid(proj, attn, o_raw, x, tgt, g_hg, b_gate, g_post, wa, wb, w_out):
    s = x.shape[0]

    def body(attn_ref, ga_ref, o_ref, gb_ref, mg_ref, x_ref, t_ref, ghg_ref, bg_ref, gp_ref, wa_ref, wb_ref, wo_ref,
             loss_ref, dout_ref, dattn_ref, dga_ref, dor_ref, dgb_ref, dmg_ref, dwo_ref, dwa_ref, dwb_ref, dgp_ref, dbg_ref, dghg_ref):
        first = pl.program_id(0) == 0

        @pl.when(first)
        def _():
            for rf in (loss_ref, dwo_ref, dwa_ref, dwb_ref, dgp_ref, dbg_ref, dghg_ref):
                rf[...] = jnp.zeros_like(rf)

        attn, za, orw, zb = attn_ref[...], ga_ref[...], o_ref[...], gb_ref[...]
        ghg, gp = ghg_ref[...], gp_ref[...]
        sga, sgb = _sigmoid(za), _sigmoid(zb)
        sa, sb = za * sga, zb * sgb
        ga = attn * sa
        rh = lax.rsqrt(_group_sum(orw * orw) * (1.0 / V_DIM) + EPS)
        on = (orw * rh) * ghg
        gb = on * sb
        ga_b, gb_b = ga.astype(BF16), gb.astype(BF16)
        ya = _mm(ga_b, wa_ref[...])
        yb = _mm(gb_b, wb_ref[...])
        gates = _sigmoid(mg_ref[...] + bg_ref[...])
        g0, g1 = gates[:, :D], gates[:, D:]
        m_b = (g0 * ya + g1 * yb).astype(BF16)
        y = _mm(m_b, wo_ref[...])
        ry = lax.rsqrt(jnp.mean(y * y, axis=-1, keepdims=True) + EPS)
        out = x_ref[...] + (y * ry) * gp
        err = out - t_ref[...]
        loss_ref[...] += 0.5 * jnp.sum(jnp.mean(err * err, axis=-1, keepdims=True), axis=0, keepdims=True)
        dout = err * (1.0 / D)
        dout_ref[...] = dout
        dgp_ref[...] += jnp.sum(dout * (y * ry), axis=0, keepdims=True)
        dgy = dout * gp
        dy = ry * dgy - y * (ry * ry * ry) * jnp.mean(y * dgy, axis=-1, keepdims=True)
        dy_b = dy.astype(BF16)
        dwo_ref[...] += _mm_tn(m_b, dy_b)
        dm = _mm_nt(dy_b, wo_ref[...])
        dya, dyb = dm * g0, dm * g1
        dg0, dg1 = dm * ya, dm * yb
        dmg = jnp.concatenate([dg0 * g0 * (1.0 - g0), dg1 * g1 * (1.0 - g1)], axis=1)
        dmg_ref[...] = dmg.astype(BF16)
        dbg_ref[...] += jnp.sum(dmg, axis=0, keepdims=True)
        dya_b, dyb_b = dya.astype(BF16), dyb.astype(BF16)
        dwa_ref[...] += _mm_tn(ga_b, dya_b)
        dwb_ref[...] += _mm_tn(gb_b, dyb_b)
        dga = _mm_nt(dya_b, wa_ref[...])
        dgb = _mm_nt(dyb_b, wb_ref[...])
        dattn_ref[...] = dga * sa
        dga_ref[...] = (dga * attn * _dsilu(za, sga)).astype(BF16)
        dgb_ref[...] = (dgb * on * _dsilu(zb, sgb)).astype(BF16)
        don = dgb * sb
        dghg_ref[...] += jnp.sum(don * (orw * rh), axis=0, keepdims=True)
        dgo = don * ghg
        dor_ref[...] = rh * dgo - orw * (rh * rh * rh) * (_group_sum(orw * dgo) * (1.0 / V_DIM))

    row = lambda w, j=0: pl.BlockSpec((TM_MID, w), lambda i: (i, j))
    full = lambda a: pl.BlockSpec(a.shape, lambda i: (0,) * a.ndim)
    acc = lambda shape: pl.BlockSpec(shape, lambda i: (0, 0))
    sds = jax.ShapeDtypeStruct
    return pl.pallas_call(
        body, name="mid", grid=(s // TM_MID,),
        in_specs=[row(512), row(512, P_GA // 512), row(512), row(512, P_GB // 512), row(2048, P_MERGE // 2048), row(D), row(D),
                  full(g_hg), full(b_gate), full(g_post), full(wa), full(wb), full(w_out)],
        out_specs=[acc((1, 1)), row(D), row(512), row(512), row(512), row(512), row(2048),
                   acc((D, D)), acc((512, D)), acc((512, D)), acc((1, D)), acc((1, 2048)), acc((1, 512))],
        out_shape=[sds((1, 1), F32), sds((s, D), F32), sds((s, 512), F32), sds((s, 512), BF16), sds((s, 512), F32), sds((s, 512), BF16),
                   sds((s, 2048), BF16), sds((D, D), F32), sds((512, D), F32), sds((512, D), F32), sds((1, D), F32),
                   sds((1, 2048), F32), sds((1, 512), F32)],
        compiler_params=_params(("arbitrary",)),
    )(attn, proj, o_raw, proj, proj, x, tgt, g_hg, b_gate, g_post, wa, wb, w_out)


def _attn_bwd(q, k, vv, attn, dattn, lse, token):
    s = q.shape[1]
    nt = s // TQ
    scale = 1.0 / math.sqrt(QK_NOPE + QK_ROPE)

    def body(q_ref, k_ref, v_ref, o_ref, do_ref, lse_ref, token_ref, dq_ref, dk_ref, dv_ref, do_s, delta_s):
        j = pl.program_id(1)

        @pl.when(j == 0)
        def _():
            dq_ref[...] = jnp.zeros_like(dq_ref)
            lane = lax.broadcasted_iota(jnp.int32, (TQ, LANE), 1)

            @pl.loop(0, nt)
            def _(i):
                rows = pl.ds(pl.multiple_of(i * TQ, TQ), TQ)
                do, o = do_ref[rows, :], o_ref[rows, :]
                for hh in range(2):
                    doh = jnp.where((lane < 64) if hh == 0 else (lane >= 64), do, 0.0)
                    do_s[hh, rows, :] = doh.astype(BF16)
                    delta_s[hh, rows, :] = jnp.broadcast_to(jnp.sum(doh * o, axis=-1, keepdims=True), (TQ, LANE))

        kjs, vjs = (k_ref[0], k_ref[1]), (v_ref[0], v_ref[1])
        wide = lambda a: jnp.concatenate([a] * (TQ // LANE), axis=1)

        def tile(hh, i, carry, diag):
            dk, dv = carry
            rows = pl.ds(pl.multiple_of(i * TQ, TQ), TQ)
            qi, do_b = q_ref[hh, rows, :], do_s[hh, rows, :]
            p = jnp.exp2(_mm_nt(qi, kjs[hh]) * QK_SCALE2 - wide(lse_ref[hh, rows, :]))
            if diag:
                p = jnp.where(_diag_visible(), p, 0.0)
            dv = dv + _mm_tn(do_b, p.astype(BF16))
            ds_b = (p * (_mm_nt(do_b, vjs[hh]) - wide(delta_s[hh, rows, :]))).astype(BF16)
            dk = dk + _mm_tn(qi, ds_b)
            dq_ref[hh, rows, :] += _mm(ds_b, kjs[hh])
            return dk, dv

        def step(i, carry):
            return tile(0, i, carry[0], False), tile(1, i, carry[1], False)

        z = jnp.zeros((LANE, TQ), F32)
        first = (tile(0, j, (z, z), True), tile(1, j, (z, z), True))
        carry = lax.fori_loop(j + 1, nt, step, first)
        for hh in range(2):
            dk_ref[hh] = carry[hh][0].T * scale
            dv_ref[hh] = carry[hh][1].T

        @pl.when(j == nt - 1)
        def _():
            dq_ref[...] = dq_ref[...] * scale

    whole = pl.BlockSpec((2, s, LANE), lambda p, j: (p, 0, 0))
    tile_spec = pl.BlockSpec((2, TQ, LANE), lambda p, j: (p, j, 0))
    cols = pl.BlockSpec((s, LANE), lambda p, j: (0, p))
    hs = jax.ShapeDtypeStruct((NH, s, LANE), F32)
    return pl.pallas_call(
        body, name="attn_bwd", grid=(NH // 2, nt),
        in_specs=[whole, tile_spec, tile_spec, cols, cols, whole, pl.BlockSpec((8, LANE), lambda p, j: (0, 0))],
        out_specs=[whole, tile_spec, tile_spec],
        out_shape=[hs, hs, hs],
        scratch_shapes=[pltpu.VMEM((2, s, LANE), BF16), pltpu.VMEM((2, s, LANE), F32)],
        compiler_params=_params(("parallel", "arbitrary")),
    )(q, k, vv, attn, dattn, lse, token)


def _hgrn_bwd(proj, lbl, states, do_raw):
    s = proj.shape[0]
    nt = s // TH
    nch = TH // HG_BLOCK

    def body(hq_ref, hf_ref, hi_ref, lbl_ref, st_ref, do_ref, dh_ref, dlbl_ref, dst, dlb):
        step = pl.program_id(1)

        @pl.when(step == 0)
        def _():
            dst[...] = jnp.zeros_like(dst)
            dlb[...] = jnp.zeros_like(dlb)

        m0, bd = _pair_masks()
        for u in range(HG_PAIRS):
            lanes = slice(u * LANE, (u + 1) * LANE)
            lb = _lower_bound(lbl_ref[:, lanes])
            gt = _hgrn_gates(hq_ref[:, lanes], hf_ref[:, lanes], lb)
            do = do_ref[:, lanes]
            qd, ki, ke = gt["qd"], gt["ki"], gt["ke"]
            v_b, do_b = hi_ref[:, lanes].astype(BF16), do.astype(BF16)
            qd_b, ki_b, ke_b = qd.astype(BF16), ki.astype(BF16), ke.astype(BF16)
            dv = jnp.zeros((TH, LANE), F32)
            dqd = jnp.zeros((TH, LANE), F32)
            dki = jnp.zeros((TH, LANE), F32)
            for hh in range(2):
                mh = m0 if hh == 0 else jnp.logical_not(m0)
                a_b = jnp.where(gt["tri"], _mm_nt(jnp.where(mh, qd, 0.0).astype(BF16), ki_b), 0.0).astype(BF16)
                doh_b = jnp.where(mh, do, 0.0).astype(BF16)
                da_b = jnp.where(gt["tri"], _mm_nt(doh_b, v_b), 0.0).astype(BF16)
                dv = dv + _mm_tn(a_b, doh_b)
                dqd = jnp.where(mh, _mm(da_b, ki_b), dqd)
                dki = jnp.where(mh, _mm_tn(da_b, qd_b), dki)
            fed = _bmm_tn(_chunks(do_b), _chunks(qd_b))
            decay = jnp.exp(gt["total"])
            ds, leaving = dst[u], [None] * nch
            for n in reversed(range(nch)):
                leaving[n] = ds
                ds = decay[n] * ds + jnp.where(bd, fed[n], 0.0)
            dst[u] = ds
            leaving = jnp.stack(leaving)
            entering = st_ref[u]
            leaving_b = leaving.astype(BF16)
            dke3 = _bmm_nn(_chunks(v_b), leaving_b)
            dv = dv + _bmm_nt(_chunks(ke_b), leaving_b).reshape(TH, LANE)
            dqd = dqd + _bmm_nn(_chunks(do_b), entering.astype(BF16)).reshape(TH, LANE)
            dke = dke3.reshape(TH, LANE)
            dlast = (jnp.sum(dke3 * _chunks(ke), axis=1, keepdims=True)
                     + jnp.sum(leaving * entering, axis=1, keepdims=True) * decay)
            dk = dki * gt["ei"] + dke * gt["ee"]
            dcum = dqd * qd - dki * ki - dke * ke
            dg = _chunk_cumsum(dcum, reverse=True) + jnp.broadcast_to(dlast, (nch, HG_BLOCK, LANE)).reshape(TH, LANE)
            sig = gt["sig"]
            df = dg / gt["f"] - dk
            dlb[:, lanes] += jnp.sum(df * (1.0 - sig), axis=0, keepdims=True)
            dh_ref[0, :, lanes] = (dqd * gt["e"]).astype(BF16)
            dh_ref[1, :, lanes] = ((df * (1.0 - lb)) * sig * (1.0 - sig)).astype(BF16)
            dh_ref[2, :, lanes] = dv.astype(BF16)

        @pl.when(step == nt - 1)
        def _():
            lb = _lower_bound(lbl_ref[...])
            da0 = dlb[...] * lb * (1.0 - lb)
            dlbl_ref[...] = jnp.concatenate([da0, -da0], axis=0)

    wide = HG_PAIRS * LANE
    col = lambda base: pl.BlockSpec((TH, wide), lambda p, i: (nt - 1 - i, base // wide + p))
    tile = pl.BlockSpec((TH, wide), lambda p, i: (nt - 1 - i, p))
    sds = jax.ShapeDtypeStruct
    return pl.pallas_call(
        body, name="hgrn_bwd", grid=(NH // 2 // HG_PAIRS, nt),
        in_specs=[col(P_HQ), col(P_HF), col(P_HI), pl.BlockSpec((2, wide), lambda p, i: (0, p)),
                  pl.BlockSpec((HG_PAIRS, nch, LANE, LANE), lambda p, i: (p, nt - 1 - i, 0, 0)), tile],
        out_specs=[pl.BlockSpec((3, TH, wide), lambda p, i: (0, nt - 1 - i, p)), pl.BlockSpec((2, wide), lambda p, i: (0, p))],
        out_shape=[sds((3, s, 512), BF16), sds((2, 512), F32)],
        scratch_shapes=[pltpu.VMEM((HG_PAIRS, LANE, LANE), F32), pltpu.VMEM((1, wide), F32)],
        compiler_params=_params(("parallel", "arbitrary")),
    )(proj, proj, proj, lbl, states, do_raw)


def _norm_rows_bwd(v, r, g, dn):
    dgv = dn * g
    return r * dgv - v * (r * r * r) * jnp.mean(v * dgv, axis=-1, keepdims=True)


def _qkv_bwd(proj, dq, dk, dvv, g_q, g_kv, w_uq_p, w_k_p, w_v_p, rc, rs1, rs2):
    s = proj.shape[0]

    def body(cq_ref, ckv_ref, dq_ref, dk_ref, dv_ref, gq_ref, gkv_ref, wq_ref, wk_ref, wv_ref, c_ref, s1_ref, s2_ref,
             dcq_ref, dckv_ref, dkpe_ref, dwq_ref, dwk_ref, dwv_ref, dgq_ref, dgkv_ref):
        @pl.when(pl.program_id(0) == 0)
        def _():
            for rf in (dwq_ref, dwk_ref, dwv_ref, dgq_ref, dgkv_ref):
                rf[...] = jnp.zeros_like(rf)

        c, s1, s2 = c_ref[...], s1_ref[...], s2_ref[...]
        cq, ckv = cq_ref[...], ckv_ref[...]
        gq, gkv = gq_ref[...], gkv_ref[...]
        cqn, rq = _norm_rows(cq, gq)
        ckvn, rkv = _norm_rows(ckv, gkv)
        cqn_b, ckvn_b = cqn.astype(BF16), ckvn.astype(BF16)
        dqf = jnp.concatenate([_rope_t(dq_ref[h], c, s1, s2) for h in range(NH)], axis=1).astype(BF16)
        dkf = jnp.concatenate([dk_ref[h] for h in range(NH)], axis=1).astype(BF16)
        dvf = jnp.concatenate([dv_ref[h] for h in range(NH)], axis=1).astype(BF16)
        dkpe = dk_ref[0]
        for h in range(1, NH):
            dkpe = dkpe + dk_ref[h]
        lane = lax.broadcasted_iota(jnp.int32, (TM, LANE), 1)
        dkpe = jnp.where((lane >= QK_NOPE) & (lane < QK_NOPE + QK_ROPE), dkpe, 0.0)
        dkpe_ref[...] = _rope_t(dkpe, c, s1, s2).astype(BF16)
        dwq_ref[...] += _mm_tn(cqn_b, dqf)
        dwk_ref[...] += _mm_tn(ckvn_b, dkf)
        dwv_ref[...] += _mm_tn(ckvn_b, dvf)
        dcqn = _mm_nt(dqf, wq_ref[...])
        dckvn = _mm_nt(dkf, wk_ref[...]) + _mm_nt(dvf, wv_ref[...])
        dgq_ref[...] += jnp.sum(dcqn * (cq * rq), axis=0, keepdims=True)
        dgkv_ref[...] += jnp.sum(dckvn * (ckv * rkv), axis=0, keepdims=True)
        dcq_ref[...] = _norm_rows_bwd(cq, rq, gq, dcqn).astype(BF16)
        dckv_ref[...] = _norm_rows_bwd(ckv, rkv, gkv, dckvn).astype(BF16)

    row = lambda w, j=0: pl.BlockSpec((TM, w), lambda i: (i, j))
    full = lambda a: pl.BlockSpec(a.shape, lambda i: (0,) * a.ndim)
    acc = lambda shape: pl.BlockSpec(shape, lambda i: (0, 0))
    heads = pl.BlockSpec((NH, TM, LANE), lambda i: (0, i, 0))
    sds = jax.ShapeDtypeStruct
    return pl.pallas_call(
        body, name="qkv_bwd", grid=(s // TM,),
        in_specs=[row(Q_LORA, P_CQ // Q_LORA), row(KV_LORA, P_CKV // KV_LORA), heads, heads, heads,
                  full(g_q), full(g_kv), full(w_uq_p), full(w_k_p), full(w_v_p), row(LANE), row(LANE), row(LANE)],
        out_specs=[row(Q_LORA), row(KV_LORA), row(LANE), acc((Q_LORA, D)), acc((KV_LORA, D)), acc((KV_LORA, D)),
                   acc((1, Q_LORA)), acc((1, KV_LORA))],
        out_shape=[sds((s, Q_LORA), BF16), sds((s, KV_LORA), BF16), sds((s, LANE), BF16), sds((Q_LORA, D), F32),
                   sds((KV_LORA, D), F32), sds((KV_LORA, D), F32), sds((1, Q_LORA), F32), sds((1, KV_LORA), F32)],
        compiler_params=_params(("arbitrary",)),
    )(proj, proj, dq, dk, dvv, g_q, g_kv, w_uq_p, w_k_p, w_v_p, rc, rs1, rs2)


def _front_bwd(x, dout, dmg, dga, dh3, dgb, dcq, dckv, dkpe, g_pre, w_in_pt, token):
    s = x.shape[0]

    def body(x_ref, do_ref, dmg_ref, dga_ref, dh3_ref, dgb_ref, dcq_ref, dckv_ref, dkpe_ref, g_ref, w_ref, token_ref, gx_ref, dg_ref):
        @pl.when(pl.program_id(0) == 0)
        def _():
            dg_ref[...] = jnp.zeros_like(dg_ref)

        xv, g = x_ref[...], g_ref[...]
        _, r = _norm_rows(xv, g)
        pieces = ((dmg_ref[...], P_MERGE), (dga_ref[...], P_GA), (dh3_ref[0], P_HQ), (dh3_ref[1], P_HF), (dh3_ref[2], P_HI),
                  (dgb_ref[...], P_GB), (dcq_ref[...], P_CQ), (dckv_ref[...], P_CKV), (dkpe_ref[...], P_KPE))
        dh = jnp.zeros((TM, D), F32)
        for piece, off in pieces:
            dh = dh + _mm(piece, w_ref[off:off + piece.shape[1], :])
        dg_ref[...] += jnp.sum(dh * (xv * r), axis=0, keepdims=True)
        gx_ref[...] = do_ref[...] + _norm_rows_bwd(xv, r, g, dh)

    row = lambda w: pl.BlockSpec((TM, w), lambda i: (i, 0))
    full = lambda a: pl.BlockSpec(a.shape, lambda i: (0,) * a.ndim)
    sds = jax.ShapeDtypeStruct
    return pl.pallas_call(
        body, name="front_bwd", grid=(s // TM,),
        in_specs=[row(D), row(D), row(2048), row(512), pl.BlockSpec((3, TM, 512), lambda i: (0, i, 0)), row(512), row(Q_LORA),
                  row(KV_LORA), row(LANE), full(g_pre), full(w_in_pt), pl.BlockSpec(memory_space=pl.ANY)],
        out_specs=[row(D), pl.BlockSpec((1, D), lambda i: (0, 0))],
        out_shape=[sds((s, D), F32), sds((1, D), F32)],
        compiler_params=_params(("arbitrary",)),
    )(x, dout, dmg, dga, dh3, dgb, dcq, dckv, dkpe, g_pre, w_in_pt, token)


TK_GRAD = 512


def _win_grad(h, pieces, name):
    s = h.shape[0]
    n = len(pieces)

    def body(h_ref, *refs):
        @pl.when(pl.program_id(0) == 0)
        def _():
            for o_ref in refs[n:]:
                o_ref[...] = jnp.zeros_like(o_ref)

        hv = h_ref[...]
        for d_ref, o_ref in zip(refs[:n], refs[n:]):
            if len(d_ref.shape) == 3:
                for k in range(d_ref.shape[0]):
                    o_ref[k] += _mm_tn(d_ref[k], hv)
            else:
                o_ref[...] += _mm_tn(d_ref[...], hv)

    def in_spec(p):
        if p.ndim == 3:
            return pl.BlockSpec((p.shape[0], TK_GRAD, p.shape[2]), lambda kk: (0, kk, 0))
        return pl.BlockSpec((TK_GRAD, p.shape[1]), lambda kk: (kk, 0))

    out_shapes = [(p.shape[0], p.shape[2], D) if p.ndim == 3 else (p.shape[1], D) for p in pieces]
    return pl.pallas_call(
        body, name=name, grid=(s // TK_GRAD,),
        in_specs=[pl.BlockSpec((TK_GRAD, D), lambda kk: (kk, 0))] + [in_spec(p) for p in pieces],
        out_specs=[pl.BlockSpec(sh, lambda kk, nd=len(sh): (0,) * nd) for sh in out_shapes],
        out_shape=[jax.ShapeDtypeStruct(sh, F32) for sh in out_shapes],
        compiler_params=_params(("arbitrary",)),
    )(h, *pieces)


def _pad_win_t(w_in_t):
    z = lambda n: jnp.zeros((n, w_in_t.shape[1]), w_in_t.dtype)
    sl = lambda o, n: w_in_t[o:o + n]
    return jnp.concatenate([sl(O_MERGE, 2048), sl(O_GA, 512), sl(O_HQ, 512), sl(O_HF, 512), sl(O_HI, 512), sl(O_GB, 512),
                            sl(O_CQ, Q_LORA), sl(O_CKV, KV_LORA), z(64), sl(O_KPE, QK_ROPE), z(32)], axis=0)


def _pad_wuq(w_uq):
    w = w_uq.reshape(Q_LORA, NH, QK_NOPE + QK_ROPE)
    return jnp.pad(w, ((0, 0), (0, 0), (0, LANE - QK_NOPE - QK_ROPE))).reshape(Q_LORA, NH * LANE)


def _unpad_wuq(g):
    return g.reshape(Q_LORA, NH, LANE)[:, :, :QK_NOPE + QK_ROPE].reshape(Q_LORA, NH * (QK_NOPE + QK_ROPE))


def _pad_wukv(w_ukv):
    w = w_ukv.reshape(KV_LORA, NH, QK_NOPE + V_DIM)
    w_k = jnp.pad(w[:, :, :QK_NOPE], ((0, 0), (0, 0), (0, LANE - QK_NOPE))).reshape(KV_LORA, NH * LANE)
    wv = w[:, :, QK_NOPE:].reshape(KV_LORA, NH // 2, 2, 1, V_DIM)
    eye = jnp.eye(2, dtype=w.dtype).reshape(1, 1, 2, 2, 1)
    return w_k, (wv * eye).reshape(KV_LORA, NH * LANE)


def _unpad_wukv(gk, gv):
    gk = gk.reshape(KV_LORA, NH, LANE)[:, :, :QK_NOPE]
    gv = gv.reshape(KV_LORA, NH // 2, 2, 2, V_DIM)
    gv = jnp.stack([gv[:, :, 0, 0], gv[:, :, 1, 1]], axis=2).reshape(KV_LORA, NH, V_DIM)
    return jnp.concatenate([gk, gv], axis=-1).reshape(KV_LORA, NH * (QK_NOPE + V_DIM))


def _local_step(x, tgt, g_pre, w_in_t, b_gate, g_q, g_kv, lb_logits, g_hgrn, g_post, weights, exchange=None):
    s = x.shape[0]
    w_in_p = _pad_win_t(w_in_t)
    rc, rs1, rs2 = _rope_tables(s)
    g_hg = jnp.tile(g_hgrn, (1, NH))

    proj, h = _front_fwd(x, g_pre, w_in_p, weights.tokens)
    w_uq, w_ukv = weights.qkv(h)
    w_uq_p = _pad_wuq(w_uq)
    w_k_p, w_v_p = _pad_wukv(w_ukv)
    q, k, vv = _qkv_fwd(proj, g_q, g_kv, w_uq_p, w_k_p, w_v_p, rc, rs1, rs2)
    attn, lse = _attn_fwd(q, k, vv)
    o_raw, states = _hgrn_fwd(proj, lb_logits)
    wa, wb, w_out = weights.mid(o_raw)
    (loss, dout, dattn, dga, dor, dgb, dmg, d_wout, d_wa, d_wb, d_gpost, d_bgate, d_ghg) = _mid(
        proj, attn, o_raw, x, tgt, g_hg, b_gate, g_post, wa, wb, w_out)
    w_mg, w_ga, w_gb = _win_grad(h, [dmg, dga, dgb], "win_grad_mid")
    dh3, d_lbl = _hgrn_bwd(proj, lb_logits, states, dor)
    (w_h3,) = _win_grad(h, [dh3], "win_grad_hgrn")
    d_win_rest = jnp.concatenate([w_ga, w_h3[0], w_h3[1], w_h3[2], w_gb, w_mg], axis=0)
    early = dict(w_in_rest=d_win_rest, w_branch_a=d_wa, w_branch_b=d_wb, w_out=d_wout)
    token = exchange.start_early(early) if exchange else jnp.zeros((8, LANE), F32)
    dq, dk, dvv = _attn_bwd(q, k, vv, attn, dattn, lse, token)
    dcq, dckv, dkpe, d_wuq_p, d_wk_p, d_wv_p, d_gq, d_gkv = _qkv_bwd(proj, dq, dk, dvv, g_q, g_kv, w_uq_p, w_k_p, w_v_p, rc, rs1, rs2)
    w_cq, w_ckv, w_kpe = _win_grad(h, [dcq, dckv, dkpe], "win_grad_qkv")
    d_win_qkv = jnp.concatenate([w_cq, w_ckv, w_kpe[64:64 + QK_ROPE]], axis=0)
    late = dict(w_in_qkv=d_win_qkv, w_uq=_unpad_wuq(d_wuq_p), w_ukv=_unpad_wukv(d_wk_p, d_wv_p))
    token = exchange.start_late(late) if exchange else jnp.zeros((8, LANE), F32)
    grad_x, d_gpre = _front_bwd(x, dout, dmg, dga, dh3, dgb, dcq, dckv, dkpe, g_pre, w_in_p, token)
    vec_grads = dict(g_pre=d_gpre, b_gate=d_bgate, g_q=d_gq, g_kv=d_gkv, lb_logits=d_lbl, g_hgrn=d_ghg, g_post=d_gpost)
    return loss, grad_x, dict(early, **late), vec_grads


SHARD_SHAPES = (("w_in", (1416, 1024)), ("w_uq", (192, 768)), ("w_ukv", (256, 256)), ("w_branch_a", (512, 256)),
                ("w_branch_b", (512, 256)), ("w_out", (256, 1024)))
BIG = tuple(n for n, _ in SHARD_SHAPES)
ROW_SHARDED = ("w_in", "w_uq", "w_out")
GATHER_SPLIT_AXIS = dict(w_in=1, w_uq=0, w_ukv=0, w_branch_a=0, w_branch_b=0, w_out=0)
N_CHIPS = 4
QKV_ROWS = Q_LORA + KV_LORA + QK_ROPE


def _to_block(name, a):
    return a[0].T if name == "w_in" else a[0]


def _from_block(name, a):
    return a.T[None] if name == "w_in" else a[None]
VEC_ROWS = (("g_pre", 0, 1024), ("b_gate", 1, 2048), ("g_q", 2, 768), ("g_kv", 3, 256), ("g_hgrn", 6, 64), ("g_post", 7, 1024))
VEC_LB_ROW = 4
VEC_SHAPE = (8, 2048)


def _split_by_chip(name, g):
    a, b = dict(SHARD_SHAPES)[name]
    return g.reshape(N_CHIPS, a, b) if name in ROW_SHARDED else g.reshape(a, N_CHIPS, b).transpose(1, 0, 2)


def _join_chips(name, w):
    a, b = dict(SHARD_SHAPES)[name]
    return w.reshape(N_CHIPS * a, b) if name in ROW_SHARDED else w.transpose(1, 0, 2).reshape(a, N_CHIPS * b)


MESH = pl.DeviceIdType.MESH
HBM = pl.BlockSpec(memory_space=pltpu.HBM)


def _mesh_place():
    x, y, c = lax.axis_index("x"), lax.axis_index("y"), lax.axis_index("c")
    return x, y, c, 2 * x + y, [(1 - x, y), (x, 1 - y), (1 - x, 1 - y)]


def _remote(src, dst, send_sems, recv_sems, k, to):
    return pltpu.make_async_remote_copy(src_ref=src, dst_ref=dst, send_sem=send_sems.at[k], recv_sem=recv_sems.at[k],
                                        device_id=to, device_id_type=MESH)


def _gather_weights(shards, split_axes):
    n = len(shards)

    def body(*refs):
        srcs, outs = refs[:n], refs[n:2 * n]
        ici_send, ici_recv, d2d_send, d2d_recv, local_sems = refs[2 * n:]
        x, y, c, me, chips = _mesh_place()
        sibling = (x, y, 1 - c)

        def half(ref, k, which):
            size = shards[k].shape[split_axes[k]] // 2
            part = pl.ds(pl.multiple_of(which * size, size), size)
            return ref.at[part] if split_axes[k] == 0 else ref.at[:, part]

        own = [pltpu.make_async_copy(srcs[k], outs[k].at[me], local_sems.at[k]) for k in range(n)]
        for cp in own:
            cp.start()
        started = []
        for k in range(n):
            for j, (px, py) in enumerate(chips):
                cp = _remote(half(srcs[k], k, c), half(outs[k].at[me], k, c), ici_send, ici_recv, 3 * k + j, (px, py, c))
                cp.start()
                started.append(cp)
        for k in range(n):
            for j, (px, py) in enumerate(chips):
                landed = half(outs[k].at[2 * px + py], k, c)
                _remote(landed, landed, ici_send, ici_recv, 3 * k + j, (px, py, c)).wait_recv()
                cp = _remote(landed, landed, d2d_send, d2d_recv, 3 * k + j, sibling)
                cp.start()
                started.append(cp)
        for k in range(n):
            for j, (px, py) in enumerate(chips):
                other = half(outs[k].at[2 * px + py], k, 1 - c)
                _remote(other, other, d2d_send, d2d_recv, 3 * k + j, sibling).wait_recv()
        for cp in started:
            cp.wait_send()
        for cp in own:
            cp.wait()

    sems = pltpu.SemaphoreType.DMA((3 * n,))
    return pl.pallas_call(
        body, name="gather_weights", in_specs=[HBM] * n, out_specs=[HBM] * n,
        out_shape=[jax.ShapeDtypeStruct((N_CHIPS,) + s.shape, s.dtype) for s in shards],
        scratch_shapes=[sems, sems, sems, sems, pltpu.SemaphoreType.DMA((n,))],
        compiler_params=pltpu.CompilerParams(has_side_effects=True),
    )(*shards)


def _chip_exchange(srcs, name):
    n = len(srcs)

    def body(*refs):
        src_refs, outs = refs[:n], refs[n:2 * n]
        send_sems, recv_sems, local_sems = refs[2 * n:]
        x, y, c, me, chips = _mesh_place()

        def slab(k, t):
            return src_refs[k] if srcs[k].ndim == 2 else src_refs[k].at[t]

        own = [pltpu.make_async_copy(slab(k, me), outs[k].at[me], local_sems.at[k]) for k in range(n)]
        for cp in own:
            cp.start()
        sends = []
        for k in range(n):
            for j, (px, py) in enumerate(chips):
                cp = _remote(slab(k, 2 * px + py), outs[k].at[me], send_sems, recv_sems, 3 * k + j, (px, py, c))
                cp.start()
                sends.append(cp)
        for k in range(n):
            for j, (px, py) in enumerate(chips):
                _remote(slab(k, me), outs[k].at[2 * px + py], send_sems, recv_sems, 3 * k + j, (px, py, c)).wait_recv()
        for cp in sends:
            cp.wait_send()
        for cp in own:
            cp.wait()

    sems = pltpu.SemaphoreType.DMA((3 * n,))
    return pl.pallas_call(
        body, name=name, in_specs=[HBM] * n, out_specs=[HBM] * n,
        out_shape=[jax.ShapeDtypeStruct((N_CHIPS,) + s.shape[-2:], s.dtype) for s in srcs],
        scratch_shapes=[sems, sems, pltpu.SemaphoreType.DMA((n,))],
        compiler_params=pltpu.CompilerParams(has_side_effects=True),
    )(*srcs)


def _sibling_exchange(srcs, name, after=None):
    n = len(srcs)
    extra = [] if after is None else [after]

    def body(*refs):
        src_refs, outs = refs[:n], refs[n + len(extra):2 * n + len(extra)]
        send_sems, recv_sems = refs[2 * n + len(extra):]
        sibling = (lax.axis_index("x"), lax.axis_index("y"), 1 - lax.axis_index("c"))
        copies = [_remote(src_refs[k], outs[k], send_sems, recv_sems, k, sibling) for k in range(n)]
        for cp in copies:
            cp.start()
        for cp in copies:
            cp.wait()

    sems = pltpu.SemaphoreType.DMA((n,))
    return pl.pallas_call(
        body, name=name, in_specs=[HBM] * n + [pl.BlockSpec(memory_space=pl.ANY)] * len(extra), out_specs=[HBM] * n,
        out_shape=[jax.ShapeDtypeStruct(s.shape, s.dtype) for s in srcs],
        scratch_shapes=[sems, sems],
        compiler_params=pltpu.CompilerParams(has_side_effects=True),
    )(*srcs, *extra)


SEM = pl.BlockSpec(memory_space=pltpu.SEMAPHORE)
DATAFLOW = pltpu.SideEffectType.DATAFLOW_SIDE_EFFECTING


def _exchange_copies(srcs, to_first, src_refs, land_refs, send_sems, recv_sems):
    x, y, c, me, chips = _mesh_place()
    n = len(srcs)
    sends, recvs = [], []
    for k in range(n):
        if k in to_first:
            base = 3 * n + 4 * to_first.index(k)
            sends.append((me != 0, pltpu.make_async_remote_copy(
                src_ref=src_refs[k], dst_ref=land_refs[k].at[me], send_sem=send_sems.at[base], recv_sem=recv_sems.at[base + me],
                device_id=(0, 0, c), device_id_type=MESH)))
            for s in range(1, N_CHIPS):
                recvs.append((me == 0, pltpu.make_async_remote_copy(
                    src_ref=src_refs[k], dst_ref=land_refs[k].at[s], send_sem=send_sems.at[base], recv_sem=recv_sems.at[base + s],
                    device_id=(s // 2, s % 2, c), device_id_type=MESH)))
        else:
            slab = (lambda t, k=k: src_refs[k]) if srcs[k].ndim == 2 else (lambda t, k=k: src_refs[k].at[t])
            for j, (px, py) in enumerate(chips):
                sends.append((None, _remote(slab(2 * px + py), land_refs[k].at[me], send_sems, recv_sems, 3 * k + j, (px, py, c))))
                recvs.append((None, _remote(slab(me), land_refs[k].at[2 * px + py], send_sems, recv_sems, 3 * k + j, (px, py, c))))
    return sends, recvs


def _when(pred, fn):
    if pred is None:
        fn()
    else:
        pl.when(pred)(fn)


def _exchange_start(srcs, to_first, name, after=None):
    n = len(srcs)
    n_sems = 3 * n + 4 * len(to_first)
    lands = [lax.empty((N_CHIPS,) + s.shape[-2:], s.dtype) for s in srcs]
    extra = [] if after is None else [after]

    def body(*refs):
        src_refs, land_refs = refs[:n], refs[n:2 * n]
        send_sems, recv_sems, token = refs[2 * n + len(extra)], refs[2 * n + len(extra) + 1], refs[-1]
        sends, _ = _exchange_copies(srcs, to_first, src_refs, land_refs, send_sems, recv_sems)
        for pred, cp in sends:
            _when(pred, cp.start)
        token[...] = jnp.zeros_like(token)

    hbm = lambda a: pltpu.HBM(a.shape, a.dtype)
    res = pl.pallas_call(
        body, name=name,
        out_shape=[pltpu.SemaphoreType.DMA((n_sems,)), pltpu.SemaphoreType.DMA((n_sems,))] + [hbm(a) for a in srcs + lands]
        + [jax.ShapeDtypeStruct((8, LANE), F32)],
        in_specs=[HBM] * (2 * n) + [pl.BlockSpec(memory_space=pl.ANY)] * len(extra),
        out_specs=[SEM, SEM] + [HBM] * (2 * n) + [pl.BlockSpec(memory_space=pltpu.VMEM)],
        input_output_aliases={i: 2 + i for i in range(2 * n)},
        compiler_params=pltpu.CompilerParams(has_side_effects=DATAFLOW),
    )(*[pltpu.with_memory_space_constraint(a, pltpu.HBM) for a in srcs + lands], *extra)
    return res[:-1], res[-1]


def _exchange_wait(srcs, to_first, started, after, name):
    n = len(srcs)
    send_sems, recv_sems, thru = started[0], started[1], started[2:]

    def body(*refs):
        src_refs, land_refs, send_ref, recv_ref = refs[:n], refs[n:2 * n], refs[2 * n], refs[2 * n + 1]
        sends, recvs = _exchange_copies(srcs, to_first, src_refs, land_refs, send_ref, recv_ref)
        for pred, cp in sends:
            _when(pred, cp.wait_send)
        for pred, cp in recvs:
            _when(pred, cp.wait_recv)

    res = pl.pallas_call(
        body, name=name, out_shape=[pltpu.HBM(a.shape, a.dtype) for a in thru],
        in_specs=[HBM] * (2 * n) + [SEM, SEM, pl.BlockSpec(memory_space=pl.ANY)], out_specs=[HBM] * (2 * n),
        input_output_aliases={i: i for i in range(2 * n)},
        compiler_params=pltpu.CompilerParams(has_side_effects=DATAFLOW),
    )(*thru, send_sems, recv_sems, after)
    return res[n:]


ROW_TILE = 256
COL_TILE = 256


def _block_tiling(a, b):
    if a <= ROW_TILE or a % ROW_TILE == 0:
        ta = min(a, ROW_TILE)
        return a // ta, (ta, b), lambda i: (i, 0)
    return b // COL_TILE, (a, COL_TILE), lambda i: (0, i)


def _sum_chips(parts, name):
    _, a, b = parts.shape
    steps, tile, at = _block_tiling(a, b)

    def body(p_ref, o_ref):
        f = lambda t: p_ref[t].astype(F32)
        o_ref[...] = ((f(0) + f(1)) + f(2)) + f(3)

    return pl.pallas_call(
        body, name=name, grid=(steps,),
        in_specs=[pl.BlockSpec((N_CHIPS,) + tile, lambda i: (0,) + at(i))],
        out_specs=pl.BlockSpec(tile, at),
        out_shape=jax.ShapeDtypeStruct((a, b), F32),
        compiler_params=_params(("parallel",)),
    )(parts)


def _sum_landed(land, own, name, first_land=None, first_own=None):
    _, a, b = land.shape
    steps, tile, at = _block_tiling(a, b)
    extra = first_land is not None

    def body(*refs):
        p_ref, own_ref, o_ref = refs[0], refs[1], refs[-1]
        me = 2 * lax.axis_index("x") + lax.axis_index("y")
        own = own_ref[...].astype(F32)
        slot = lambda t: jnp.where(me == t, own, p_ref[t].astype(F32))
        o_ref[...] = ((slot(0) + slot(1)) + slot(2)) + slot(3)
        if extra:
            fp_ref, fo_ref = refs[2], refs[3]
            r = fo_ref.shape[0]

            @pl.when(me == 0)
            def _():
                f = lambda t: fp_ref[t].astype(F32)
                rows = pl.ds(pl.multiple_of(lax.axis_index("c") * r, 8), r)
                o_ref[rows, :] += ((fo_ref[...].astype(F32) + f(1)) + f(2)) + f(3)

    in_specs = [pl.BlockSpec((N_CHIPS,) + tile, lambda i: (0,) + at(i)), pl.BlockSpec(tile, at)]
    args = [land, own]
    if extra:
        r = first_own.shape[0]
        assert tile[0] == a, "the extra rows need whole columns in a step"
        in_specs += [pl.BlockSpec((N_CHIPS, r, tile[1]), lambda i: (0,) + at(i)), pl.BlockSpec((r, tile[1]), at)]
        args += [first_land, first_own]
    return pl.pallas_call(
        body, name=name, grid=(steps,), in_specs=in_specs, out_specs=pl.BlockSpec(tile, at),
        out_shape=jax.ShapeDtypeStruct((a, b), F32), compiler_params=_params(("parallel",)),
    )(*args)


def _add_cast(a, b, name):
    def body(a_ref, b_ref, o_ref):
        o_ref[...] = (a_ref[...] + b_ref[...]).astype(BF16)

    return pl.pallas_call(body, name=name, out_shape=jax.ShapeDtypeStruct(a.shape, BF16),
                          compiler_params=_params(()))(a, b)


class _LaterWeights:
    QKV = ("w_uq", "w_ukv")
    MID = ("w_branch_a", "w_branch_b", "w_out")

    def __init__(self, blocks, after):
        self.blocks = blocks
        self.qkv_started, t1 = _exchange_start([blocks[n] for n in self.QKV], (), "weights_qkv_start", after)
        self.mid_started, t2 = _exchange_start([blocks[n] for n in self.MID], (), "weights_mid_start", after)
        self.tokens = [t1, t2]

    def _whole(self, names, started, after, name):
        landed = _exchange_wait([self.blocks[n] for n in names], (), started, after, name)
        me = 2 * lax.axis_index("x") + lax.axis_index("y")
        return [_join_chips(n, lax.dynamic_update_index_in_dim(land, self.blocks[n], me, 0)) for n, land in zip(names, landed)]

    def qkv(self, after):
        return self._whole(self.QKV, self.qkv_started, after, "weights_qkv_wait")

    def mid(self, after):
        return self._whole(self.MID, self.mid_started, after, "weights_mid_wait")


class _GradExchange:
    EARLY = ("w_in", "w_branch_a", "w_branch_b", "w_out")
    LATE = ("w_uq", "w_ukv")

    def __init__(self, state):
        self.state = state
        self.outs = {}

    @staticmethod
    def _own(slabs):
        return lax.dynamic_index_in_dim(slabs, 2 * lax.axis_index("x") + lax.axis_index("y"), axis=0, keepdims=False)

    def start_early(self, g):
        full = jnp.concatenate([jnp.zeros((QKV_ROWS, D), F32), g["w_in_rest"]], axis=0)
        g = dict(g, w_in=full)
        self.early = [_split_by_chip(n, g[n]).astype(BF16) for n in self.EARLY]
        self.early_started, token = _exchange_start(self.early, (), "grads_early_start")
        return token

    def start_late(self, g):
        self.early_landed = _exchange_wait(self.early, (), self.early_started, g["w_uq"], "grads_early_wait")
        half = QKV_ROWS // 2
        c = lax.axis_index("c")
        mine = lax.dynamic_slice_in_dim(g["w_in_qkv"], c * half, half, axis=0)
        (theirs,) = _sibling_exchange([lax.dynamic_slice_in_dim(g["w_in_qkv"], (1 - c) * half, half, axis=0)], "sibling_qkv_rows")
        self.late = [_split_by_chip(n, g[n]).astype(BF16) for n in self.LATE] + [_add_cast(mine, theirs, "add_qkv_rows")]
        self.late_started, token = _exchange_start(self.late, (2,), "grads_late_start")
        names = self.EARLY[1:]
        mine = [_sum_landed(land, self._own(slabs), "sum_" + n) for n, slabs, land in list(zip(self.EARLY, self.early, self.early_landed))[1:]]
        theirs = _sibling_exchange(mine, "sibling_early", after=token)
        for n, a, b in zip(names, mine, theirs):
            self.outs[n] = _adamw(a, b, *self.state[n], "adamw_" + n)
        return self.outs[names[-1]][0]

    def finish(self, after):
        late_landed = _exchange_wait(self.late, (2,), self.late_started, after, "grads_late_wait")
        sums = {"w_in": _sum_landed(self.early_landed[0], self._own(self.early[0]), "sum_w_in",
                                    first_land=late_landed[2], first_own=self.late[2])}
        for n, slabs, land in zip(self.LATE, self.late, late_landed):
            sums[n] = _sum_landed(land, self._own(slabs), "sum_" + n)
        return sums


def _adamw_math(g, w, m, v):
    nm = ADAM_B1 * m + (1.0 - ADAM_B1) * g
    nv = ADAM_B2 * v + (1.0 - ADAM_B2) * (g * g)
    m_hat = nm / (1.0 - ADAM_B1 ** ADAM_STEP)
    v_hat = nv / (1.0 - ADAM_B2 ** ADAM_STEP)
    return -ADAM_LR * (m_hat / (jnp.sqrt(v_hat) + ADAM_EPS) + ADAM_WD * w), nm, nv


def _adamw(p_mine, p_sibling, w, m, v, name):
    a, b = p_mine.shape
    steps, tile, at = _block_tiling(a, b)

    def body(a_ref, b_ref, w_ref, m_ref, v_ref, g_ref, d_ref, nm_ref, nv_ref):
        g = a_ref[...] + b_ref[...]
        g_ref[...] = g
        d_ref[...], nm_ref[...], nv_ref[...] = _adamw_math(g, w_ref[...], m_ref[...], v_ref[...])

    spec = pl.BlockSpec(tile, at)
    sds = jax.ShapeDtypeStruct((a, b), F32)
    return pl.pallas_call(
        body, name=name, grid=(steps,), in_specs=[spec] * 5, out_specs=[spec] * 4, out_shape=[sds] * 4,
        compiler_params=_params(("parallel",)),
    )(p_mine, p_sibling, w, m, v)


LOSS_AT = (2, 1024)


def _vec_pack(vg, loss):
    names = [n for n, _, _ in VEC_ROWS]

    def body(*refs):
        o_ref = refs[-1]
        lb_ref, loss_ref = refs[len(names)], refs[len(names) + 1]
        o_ref[...] = jnp.zeros_like(o_ref)
        o_ref[LOSS_AT[0]:LOSS_AT[0] + 1, LOSS_AT[1]:LOSS_AT[1] + LANE] = jnp.broadcast_to(loss_ref[...], (1, LANE))
        for (name, row, size), ref in zip(VEC_ROWS, refs):
            if name == "g_hgrn":
                r = lax.broadcasted_iota(jnp.int32, (NH * V_DIM, LANE), 0)
                c = lax.broadcasted_iota(jnp.int32, (NH * V_DIM, LANE), 1)
                fold = ((r % V_DIM) == c).astype(F32)
                o_ref[row:row + 1, 0:LANE] = jnp.dot(ref[...], fold, precision=HIGHEST, preferred_element_type=F32)
            else:
                o_ref[row:row + 1, 0:size] = ref[...]
        o_ref[VEC_LB_ROW:VEC_LB_ROW + 2, 0:512] = lb_ref[...]

    return pl.pallas_call(body, name="vec_pack", out_shape=jax.ShapeDtypeStruct(VEC_SHAPE, F32))(
        *[vg[n] for n in names], vg["lb_logits"], loss)


def _adamw_vec(p_mine, p_sibling, w, m, v):
    names = [n for n, _, _ in VEC_ROWS] + ["lb_logits"]
    k = len(names)

    def body(a_ref, b_ref, *refs):
        ins, outs = refs[:3 * k], refs[3 * k:]
        at = (slice(LOSS_AT[0], LOSS_AT[0] + 1), slice(LOSS_AT[1], LOSS_AT[1] + LANE))
        outs[-1][...] = a_ref[at] + b_ref[at]
        for i, name in enumerate(names):
            if name == "lb_logits":
                rows, cols = slice(VEC_LB_ROW, VEC_LB_ROW + 2), slice(0, 512)
            else:
                _, row, size = VEC_ROWS[i]
                rows, cols = slice(row, row + 1), slice(0, size)
            g = a_ref[rows, cols] + b_ref[rows, cols]
            d, nm, nv = _adamw_math(g, ins[i][...], ins[k + i][...], ins[2 * k + i][...])
            for o_ref, val in zip(outs[4 * i:4 * i + 4], (g, d, nm, nv)):
                o_ref[...] = val

    shapes = [jax.ShapeDtypeStruct(w[n].shape, F32) for n in names for _ in range(4)] + [jax.ShapeDtypeStruct((1, LANE), F32)]
    res = pl.pallas_call(body, name="adamw_vec", out_shape=shapes)(
        p_mine, p_sibling, *[w[n] for n in names], *[m[n] for n in names], *[v[n] for n in names])
    return [{n: res[4 * i + j] for i, n in enumerate(names)} for j in range(4)], res[-1]


WEIGHTS = ("g_pre", "w_in", "b_gate", "g_q", "w_uq", "g_kv", "w_ukv", "lb_logits", "g_hgrn", "w_branch_a", "w_branch_b", "w_out", "g_post")


def kernel(x, g_pre, w_in, b_gate, g_q, w_uq, g_kv, w_ukv, lb_logits, g_hgrn, w_branch_a, w_branch_b, w_out, g_post, loss_target, m_g_pre, m_w_in, m_b_gate, m_g_q, m_w_uq, m_g_kv, m_w_ukv, m_lb_logits, m_g_hgrn, m_w_branch_a, m_w_branch_b, m_w_out, m_g_post, v_g_pre, v_w_in, v_b_gate, v_g_q, v_w_uq, v_g_kv, v_w_ukv, v_lb_logits, v_g_hgrn, v_w_branch_a, v_w_branch_b, v_w_out, v_g_post):
    w = dict(g_pre=g_pre, w_in=w_in, b_gate=b_gate, g_q=g_q, w_uq=w_uq, g_kv=g_kv, w_ukv=w_ukv, lb_logits=lb_logits, g_hgrn=g_hgrn,
             w_branch_a=w_branch_a, w_branch_b=w_branch_b, w_out=w_out, g_post=g_post)
    m = dict(g_pre=m_g_pre, w_in=m_w_in, b_gate=m_b_gate, g_q=m_g_q, w_uq=m_w_uq, g_kv=m_g_kv, w_ukv=m_w_ukv, lb_logits=m_lb_logits,
             g_hgrn=m_g_hgrn, w_branch_a=m_w_branch_a, w_branch_b=m_w_branch_b, w_out=m_w_out, g_post=m_g_post)
    v = dict(g_pre=v_g_pre, w_in=v_w_in, b_gate=v_b_gate, g_q=v_g_q, w_uq=v_w_uq, g_kv=v_g_kv, w_ukv=v_w_ukv, lb_logits=v_lb_logits,
             g_hgrn=v_g_hgrn, w_branch_a=v_w_branch_a, w_branch_b=v_w_branch_b, w_out=v_w_out, g_post=v_g_post)
    blocks = {n: _to_block(n, w[n]).astype(BF16) for n in BIG}
    (w_in_all,) = _gather_weights([blocks["w_in"]], [GATHER_SPLIT_AXIS["w_in"]])
    weights = _LaterWeights(blocks, w_in_all)
    state = {n: [_to_block(n, t[n]) for t in (w, m, v)] for n in BIG}
    exchange = _GradExchange(state)
    loss, grad_x, _, vec_grads = _local_step(
        x[0], loss_target[0], g_pre, _join_chips("w_in", w_in_all), b_gate, g_q, g_kv, lb_logits, g_hgrn, g_post, weights, exchange)
    sums = exchange.finish(grad_x)
    (vec_landed,) = _chip_exchange([_vec_pack(vec_grads, loss)], "scatter_vec")
    rest = tuple(sums)
    mine = [sums[n] for n in rest] + [_sum_chips(vec_landed, "sum_vec")]
    theirs = _sibling_exchange(mine, "sibling_grads")
    done = dict(exchange.outs)
    for k, n in enumerate(rest):
        done[n] = _adamw(mine[k], theirs[k], *state[n], "adamw_" + n)
    outs = [{}, {}, {}, {}]
    for n in BIG:
        for o, val in zip(outs, done[n]):
            o[n] = _from_block(n, val)
    vec_outs, total = _adamw_vec(mine[-1], theirs[-1], w, m, v)
    for o, vals in zip(outs, vec_outs):
        o.update(vals)
    return (total[0, 0], grad_x[None], *[o[n] for o in outs for n in WEIGHTS])
```

```python
import functools
import math

import numpy as np
import jax
import jax.numpy as jnp
from jax import lax
from jax.experimental import pallas as pl
from jax.experimental.pallas import tpu as pltpu

F32 = jnp.float32
BF16 = jnp.bfloat16
HIGHEST = lax.Precision.HIGHEST

D = 1024
NH = 8
QK_NOPE, QK_ROPE, V_DIM = 64, 32, 64
Q_LORA, KV_LORA = 768, 256
CHUNK = 64
HG_BLOCK = 32
EPS = 1e-6
D_IN = 5664
LANE = 128
P_MERGE, P_GA, P_HQ, P_HF, P_HI, P_GB, P_CQ, P_CKV, P_KPE = 0, 2048, 2560, 3072, 3584, 4096, 4608, 5376, 5632
D_P = 5760
O_CQ, O_CKV, O_KPE, O_GA, O_HQ, O_HF, O_HI, O_GB, O_MERGE = 0, 768, 1024, 1056, 1568, 2080, 2592, 3104, 3616

TM = 512
TM_MID = 256
TQ = 1024
ONES_LANE = (LANE - 1, 0)
TH = 256
HG_PAIRS = 4
VMEM_LIMIT = 56 * 1024 * 1024

ADAM_LR, ADAM_B1, ADAM_B2, ADAM_EPS, ADAM_WD, ADAM_STEP = 0.001, 0.9, 0.999, 1e-08, 0.01, 10

NT_DIMS = (((1,), (1,)), ((), ()))
TN_DIMS = (((0,), (0,)), ((), ()))


def _params(sem):
    return pltpu.CompilerParams(dimension_semantics=sem, vmem_limit_bytes=VMEM_LIMIT)


def _mm(a, b):
    return jnp.dot(a, b, preferred_element_type=F32)


def _mm_nt(a, b):
    return lax.dot_general(a, b, NT_DIMS, preferred_element_type=F32)


def _mm_tn(a, b):
    return lax.dot_general(a, b, TN_DIMS, preferred_element_type=F32)


def _sigmoid(z):
    return jax.nn.sigmoid(z)


def _rope(v, c, s1, s2):
    return v * c + pltpu.roll(v, 112, 1) * s1 + pltpu.roll(v, 16, 1) * s2


def _rope_t(dy, c, s1, s2):
    return dy * c + pltpu.roll(dy * s1, 16, 1) + pltpu.roll(dy * s2, 112, 1)


def _rope_tables(s):
    inv = 10000.0 ** (-jnp.arange(0, QK_ROPE, 2, dtype=F32) / QK_ROPE)
    ang = jnp.arange(s, dtype=F32)[:, None] * inv[None, :]
    cos, sin = jnp.cos(ang), jnp.sin(ang)
    z64, z32, o64, o32 = jnp.zeros((s, 64), F32), jnp.zeros((s, 32), F32), jnp.ones((s, 64), F32), jnp.ones((s, 32), F32)
    z16 = jnp.zeros((s, 16), F32)
    c = jnp.concatenate([o64, cos, cos, o32], axis=1)
    s1 = jnp.concatenate([z64, -sin, z16, z32], axis=1)
    s2 = jnp.concatenate([z64, z16, sin, z32], axis=1)
    return c, s1, s2


def _front_fwd(x, g_pre, w_in_pt, tokens=()):
    s = x.shape[0]
    tokens = list(tokens)

    def body(x_ref, g_ref, w_ref, *refs):
        o_ref, h_ref = refs[len(tokens):]
        xv = x_ref[...]
        r = lax.rsqrt(jnp.mean(xv * xv, axis=-1, keepdims=True) + EPS)
        h = ((xv * r) * g_ref[...]).astype(BF16)
        h_ref[...] = h
        o_ref[...] = _mm_nt(h, w_ref[...])

    return pl.pallas_call(
        body, name="front_fwd", grid=(s // TM,),
        in_specs=[pl.BlockSpec((TM, D), lambda i: (i, 0)), pl.BlockSpec((1, D), lambda i: (0, 0)),
                  pl.BlockSpec((D_P, D), lambda i: (0, 0))] + [pl.BlockSpec((8, LANE), lambda i: (0, 0))] * len(tokens),
        out_specs=[pl.BlockSpec((TM, D_P), lambda i: (i, 0)), pl.BlockSpec((TM, D), lambda i: (i, 0))],
        out_shape=[jax.ShapeDtypeStruct((s, D_P), F32), jax.ShapeDtypeStruct((s, D), BF16)],
        compiler_params=_params(("parallel",)),
    )(x, g_pre, w_in_pt, *tokens)


def _norm_rows(v, g):
    r = lax.rsqrt(jnp.mean(v * v, axis=-1, keepdims=True) + EPS)
    return (v * r) * g, r


def _qkv_fwd(proj, g_q, g_kv, w_uq_p, w_k_p, w_v_p, rc, rs1, rs2):
    s = proj.shape[0]

    def body(cq_ref, ckv_ref, kpe_ref, gq_ref, gkv_ref, wq_ref, wk_ref, wv_ref, c_ref, s1_ref, s2_ref, q_ref, k_ref, v_ref):
        c, s1, s2 = c_ref[...], s1_ref[...], s2_ref[...]
        cqn, _ = _norm_rows(cq_ref[...], gq_ref[...])
        ckvn, _ = _norm_rows(ckv_ref[...], gkv_ref[...])
        ckvn = ckvn.astype(BF16)
        qf = _mm(cqn.astype(BF16), wq_ref[...])
        kf = _mm(ckvn, wk_ref[...])
        vf = _mm(ckvn, wv_ref[...])
        kpe = _rope(kpe_ref[...], c, s1, s2)
        lane = lax.broadcasted_iota(jnp.int32, (TM, LANE), 1)
        for h in range(NH):
            blk = slice(h * LANE, (h + 1) * LANE)
            q_ref[h] = _rope(qf[:, blk], c, s1, s2).astype(BF16)
            k_ref[h] = (kf[:, blk] + kpe).astype(BF16)
            v_ref[h] = jnp.where(lane == ONES_LANE[h % 2], 1.0, vf[:, blk]).astype(BF16)

    row = lambda w, j: pl.BlockSpec((TM, w), lambda i: (i, j))
    full = lambda a: pl.BlockSpec(a.shape, lambda i: (0,) * a.ndim)
    hs = jax.ShapeDtypeStruct((NH, s, LANE), BF16)
    return pl.pallas_call(
        body, name="qkv_fwd", grid=(s // TM,),
        in_specs=[row(Q_LORA, P_CQ // Q_LORA), row(KV_LORA, P_CKV // KV_LORA), row(LANE, P_KPE // LANE),
                  full(g_q), full(g_kv), full(w_uq_p), full(w_k_p), full(w_v_p), row(LANE, 0), row(LANE, 0), row(LANE, 0)],
        out_specs=[pl.BlockSpec((NH, TM, LANE), lambda i: (0, i, 0))] * 3,
        out_shape=[hs, hs, hs],
        compiler_params=_params(("parallel",)),
    )(proj, proj, proj, g_q, g_kv, w_uq_p, w_k_p, w_v_p, rc, rs1, rs2)


LOG2E = 1.4426950408889634
QK_SCALE2 = LOG2E / math.sqrt(QK_NOPE + QK_ROPE)


def _diag_visible():
    row = lax.broadcasted_iota(jnp.int32, (TQ, TQ), 0)
    col = lax.broadcasted_iota(jnp.int32, (TQ, TQ), 1)
    return (col // CHUNK) <= (row // CHUNK)


def _attn_fwd(q, k, vv):
    s = q.shape[1]

    def body(q_ref, k_ref, v_ref, o_ref, lse_ref):
        i = pl.program_id(1)
        qs = (q_ref[0], q_ref[1])

        def tile(hh, t, carry, diag):
            m, acc = carry
            rows = pl.ds(pl.multiple_of(t * TQ, TQ), TQ)
            sc = _mm_nt(qs[hh], k_ref[hh, rows, :])
            if diag:
                sc = jnp.where(_diag_visible(), sc, -jnp.inf)
            m_new = jnp.maximum(m, jnp.max(sc, axis=-1, keepdims=True))
            alpha = jnp.exp2((m - m_new) * QK_SCALE2)
            p = jnp.exp2((sc - m_new) * QK_SCALE2).astype(BF16)
            acc = alpha * acc + _mm(p, v_ref[hh, rows, :])
            return m_new, acc

        def step(t, carry):
            return tile(0, t, carry[0], False), tile(1, t, carry[1], False)

        init = (jnp.full((TQ, 1), -jnp.inf, F32), jnp.zeros((TQ, LANE), F32))
        carry = lax.fori_loop(0, i, step, (init, init))
        lane = lax.broadcasted_iota(jnp.int32, (TQ, LANE), 1)
        out = jnp.zeros((TQ, LANE), F32)
        for hh in range(2):
            m, acc = tile(hh, i, carry[hh], True)
            l = jnp.sum(jnp.where(lane == ONES_LANE[hh], acc, 0.0), axis=-1, keepdims=True)
            out = out + jnp.where((lane < V_DIM) == (hh == 0), acc, 0.0) / l
            lse_ref[hh] = jnp.broadcast_to(m * QK_SCALE2 + jnp.log(l) * LOG2E, (TQ, LANE))
        o_ref[...] = out

    return pl.pallas_call(
        body, name="attn_fwd", grid=(NH // 2, s // TQ),
        in_specs=[pl.BlockSpec((2, TQ, LANE), lambda p, i: (p, i, 0)), pl.BlockSpec((2, s, LANE), lambda p, i: (p, 0, 0)),
                  pl.BlockSpec((2, s, LANE), lambda p, i: (p, 0, 0))],
        out_specs=[pl.BlockSpec((TQ, LANE), lambda p, i: (i, p)), pl.BlockSpec((2, TQ, LANE), lambda p, i: (p, i, 0))],
        out_shape=[jax.ShapeDtypeStruct((s, NH * V_DIM), F32), jax.ShapeDtypeStruct((NH, s, LANE), F32)],
        compiler_params=_params(("parallel", "parallel")),
    )(q, k, vv)


def _lower_bound(lbl):
    a0, a1 = lbl[0:1, :], lbl[1:2, :]
    mx = jnp.maximum(a0, a1)
    e0, e1 = jnp.exp(a0 - mx), jnp.exp(a1 - mx)
    return e0 / (e0 + e1)


def _chunk_cumsum(v, reverse=False):
    pos = lax.broadcasted_iota(jnp.int32, v.shape, 0) % HG_BLOCK
    s = 1
    while s < HG_BLOCK:
        if reverse:
            v = v + jnp.where(pos < HG_BLOCK - s, pltpu.roll(v, TH - s, 0), 0.0)
        else:
            v = v + jnp.where(pos >= s, pltpu.roll(v, s, 0), 0.0)
        s *= 2
    return v


def _hgrn_gates(hq, hf, lb):
    sig = _sigmoid(hf)
    f = lb + (1.0 - lb) * sig
    g = jnp.log(f)
    kk = 1.0 - f
    r = lax.broadcasted_iota(jnp.int32, (TH, TH), 0)
    c = lax.broadcasted_iota(jnp.int32, (TH, TH), 1)
    tri = ((r // HG_BLOCK) == (c // HG_BLOCK)) & (r >= c)
    cum = _chunk_cumsum(g)
    nch = TH // HG_BLOCK
    total = _chunks(cum)[:, HG_BLOCK - 1:HG_BLOCK, :]
    lastb = jnp.broadcast_to(total, (nch, HG_BLOCK, LANE)).reshape(TH, LANE)
    e, ei, ee = jnp.exp(cum), jnp.exp(-cum), jnp.exp(lastb - cum)
    return dict(sig=sig, f=f, kk=kk, tri=tri, cum=cum, total=total, e=e, ei=ei, ee=ee, qd=hq * e, ki=kk * ei, ke=kk * ee)


def _chunks(v):
    return v.reshape(TH // HG_BLOCK, HG_BLOCK, v.shape[-1])


def _bmm_nt(a, b):
    return lax.dot_general(a, b, (((2,), (2,)), ((0,), (0,))), preferred_element_type=F32)


def _bmm_nn(a, b):
    return lax.dot_general(a, b, (((2,), (1,)), ((0,), (0,))), preferred_element_type=F32)


def _bmm_tn(a, b):
    return lax.dot_general(a, b, (((1,), (1,)), ((0,), (0,))), preferred_element_type=F32)


def _pair_masks():
    lane = lax.broadcasted_iota(jnp.int32, (TH, LANE), 1)
    kr = lax.broadcasted_iota(jnp.int32, (LANE, LANE), 0)
    kc = lax.broadcasted_iota(jnp.int32, (LANE, LANE), 1)
    return lane < 64, (kr // 64) == (kc // 64)


def _hgrn_fwd(proj, lbl):
    s = proj.shape[0]
    nch = TH // HG_BLOCK

    def body(hq_ref, hf_ref, hi_ref, lbl_ref, o_ref, st_ref, st):
        @pl.when(pl.program_id(1) == 0)
        def _():
            st[...] = jnp.zeros_like(st)

        m0, bd = _pair_masks()
        for u in range(HG_PAIRS):
            lanes = slice(u * LANE, (u + 1) * LANE)
            lb = _lower_bound(lbl_ref[:, lanes])
            gt = _hgrn_gates(hq_ref[:, lanes], hf_ref[:, lanes], lb)
            v_b = hi_ref[:, lanes].astype(BF16)
            qd, ki_b, ke_b = gt["qd"], gt["ki"].astype(BF16), gt["ke"].astype(BF16)
            qd_b = qd.astype(BF16)
            o = jnp.zeros((TH, LANE), F32)
            for hh in range(2):
                mh = m0 if hh == 0 else jnp.logical_not(m0)
                a = jnp.where(gt["tri"], _mm_nt(jnp.where(mh, qd, 0.0).astype(BF16), ki_b), 0.0)
                o = jnp.where(mh, _mm(a.astype(BF16), v_b), o)
            upd = _bmm_tn(_chunks(v_b), _chunks(ke_b))
            decay = jnp.exp(gt["total"])
            cur, entering = st[u], []
            for n in range(nch):
                entering.append(cur)
                cur = decay[n] * cur + jnp.where(bd, upd[n], 0.0)
            st[u] = cur
            entering = jnp.stack(entering)
            st_ref[u] = entering
            o_ref[:, lanes] = o + _bmm_nt(_chunks(qd_b), entering.astype(BF16)).reshape(TH, LANE)

    wide = HG_PAIRS * LANE
    col = lambda base: pl.BlockSpec((TH, wide), lambda p, i: (i, base // wide + p))
    return pl.pallas_call(
        body, name="hgrn_fwd", grid=(NH // 2 // HG_PAIRS, s // TH),
        in_specs=[col(P_HQ), col(P_HF), col(P_HI), pl.BlockSpec((2, wide), lambda p, i: (0, p))],
        out_specs=[pl.BlockSpec((TH, wide), lambda p, i: (i, p)),
                   pl.BlockSpec((HG_PAIRS, nch, LANE, LANE), lambda p, i: (p, i, 0, 0))],
        out_shape=[jax.ShapeDtypeStruct((s, 512), F32), jax.ShapeDtypeStruct((NH // 2, s // HG_BLOCK, LANE, LANE), F32)],
        scratch_shapes=[pltpu.VMEM((HG_PAIRS, LANE, LANE), F32)],
        compiler_params=_params(("parallel", "arbitrary")),
    )(proj, proj, proj, lbl)


def _group_sum(v):
    low = lax.broadcasted_iota(jnp.int32, (v.shape[0], LANE), 1) < V_DIM
    blocks = []
    for b in range(v.shape[1] // LANE):
        blk = v[:, b * LANE:(b + 1) * LANE]
        s_low = jnp.sum(jnp.where(low, blk, 0.0), axis=-1, keepdims=True)
        s_high = jnp.sum(jnp.where(low, 0.0, blk), axis=-1, keepdims=True)
        blocks.append(jnp.where(low, s_low, s_high))
    return jnp.concatenate(blocks, axis=1)


def _dsilu(z, sg):
    return sg * (1.0 + z * (1.0 - sg))


def _mid(proj, attn, o_raw, x, tgt, g_hg, b_gate, g_post, wa, wb, w_out):
    s = x.shape[0]

    def body(attn_ref, ga_ref, o_ref, gb_ref, mg_ref, x_ref, t_ref, ghg_ref, bg_ref, gp_ref, wa_ref, wb_ref, wo_ref,
             loss_ref, dout_ref, dattn_ref, dga_ref, dor_ref, dgb_ref, dmg_ref, dwo_ref, dwa_ref, dwb_ref, dgp_ref, dbg_ref, dghg_ref):
        first = pl.program_id(0) == 0

        @pl.when(first)
        def _():
            for rf in (loss_ref, dwo_ref, dwa_ref, dwb_ref, dgp_ref, dbg_ref, dghg_ref):
                rf[...] = jnp.zeros_like(rf)

        attn, za, orw, zb = attn_ref[...], ga_ref[...], o_ref[...], gb_ref[...]
        ghg, gp = ghg_ref[...], gp_ref[...]
        sga, sgb = _sigmoid(za), _sigmoid(zb)
        sa, sb = za * sga, zb * sgb
        ga = attn * sa
        rh = lax.rsqrt(_group_sum(orw * orw) * (1.0 / V_DIM) + EPS)
        on = (orw * rh) * ghg
        gb = on * sb
        ga_b, gb_b = ga.astype(BF16), gb.astype(BF16)
        ya = _mm(ga_b, wa_ref[...])
        yb = _mm(gb_b, wb_ref[...])
        gates = _sigmoid(mg_ref[...] + bg_ref[...])
        g0, g1 = gates[:, :D], gates[:, D:]
        m_b = (g0 * ya + g1 * yb).astype(BF16)
        y = _mm(m_b, wo_ref[...])
        ry = lax.rsqrt(jnp.mean(y * y, axis=-1, keepdims=True) + EPS)
        out = x_ref[...] + (y * ry) * gp
        err = out - t_ref[...]
        loss_ref[...] += 0.5 * jnp.sum(jnp.mean(err * err, axis=-1, keepdims=True), axis=0, keepdims=True)
        dout = err * (1.0 / D)
        dout_ref[...] = dout
        dgp_ref[...] += jnp.sum(dout * (y * ry), axis=0, keepdims=True)
        dgy = dout * gp
        dy = ry * dgy - y * (ry * ry * ry) * jnp.mean(y * dgy, axis=-1, keepdims=True)
        dy_b = dy.astype(BF16)
        dwo_ref[...] += _mm_tn(m_b, dy_b)
        dm = _mm_nt(dy_b, wo_ref[...])
        dya, dyb = dm * g0, dm * g1
        dg0, dg1 = dm * ya, dm * yb
        dmg = jnp.concatenate([dg0 * g0 * (1.0 - g0), dg1 * g1 * (1.0 - g1)], axis=1)
        dmg_ref[...] = dmg.astype(BF16)
        dbg_ref[...] += jnp.sum(dmg, axis=0, keepdims=True)
        dya_b, dyb_b = dya.astype(BF16), dyb.astype(BF16)
        dwa_ref[...] += _mm_tn(ga_b, dya_b)
        dwb_ref[...] += _mm_tn(gb_b, dyb_b)
        dga = _mm_nt(dya_b, wa_ref[...])
        dgb = _mm_nt(dyb_b, wb_ref[...])
        dattn_ref[...] = dga * sa
        dga_ref[...] = (dga * attn * _dsilu(za, sga)).astype(BF16)
        dgb_ref[...] = (dgb * on * _dsilu(zb, sgb)).astype(BF16)
        don = dgb * sb
        dghg_ref[...] += jnp.sum(don * (orw * rh), axis=0, keepdims=True)
        dgo = don * ghg
        dor_ref[...] = rh * dgo - orw * (rh * rh * rh) * (_group_sum(orw * dgo) * (1.0 / V_DIM))

    row = lambda w, j=0: pl.BlockSpec((TM_MID, w), lambda i: (i, j))
    full = lambda a: pl.BlockSpec(a.shape, lambda i: (0,) * a.ndim)
    acc = lambda shape: pl.BlockSpec(shape, lambda i: (0, 0))
    sds = jax.ShapeDtypeStruct
    return pl.pallas_call(
        body, name="mid", grid=(s // TM_MID,),
        in_specs=[row(512), row(512, P_GA // 512), row(512), row(512, P_GB // 512), row(2048, P_MERGE // 2048), row(D), row(D),
                  full(g_hg), full(b_gate), full(g_post), full(wa), full(wb), full(w_out)],
        out_specs=[acc((1, 1)), row(D), row(512), row(512), row(512), row(512), row(2048),
                   acc((D, D)), acc((512, D)), acc((512, D)), acc((1, D)), acc((1, 2048)), acc((1, 512))],
        out_shape=[sds((1, 1), F32), sds((s, D), F32), sds((s, 512), F32), sds((s, 512), BF16), sds((s, 512), F32), sds((s, 512), BF16),
                   sds((s, 2048), BF16), sds((D, D), F32), sds((512, D), F32), sds((512, D), F32), sds((1, D), F32),
                   sds((1, 2048), F32), sds((1, 512), F32)],
        compiler_params=_params(("arbitrary",)),
    )(attn, proj, o_raw, proj, proj, x, tgt, g_hg, b_gate, g_post, wa, wb, w_out)


def _attn_bwd(q, k, vv, attn, dattn, lse, token):
    s = q.shape[1]
    nt = s // TQ
    scale = 1.0 / math.sqrt(QK_NOPE + QK_ROPE)

    def body(q_ref, k_ref, v_ref, o_ref, do_ref, lse_ref, token_ref, dq_ref, dk_ref, dv_ref, do_s, delta_s):
        j = pl.program_id(1)

        @pl.when(j == 0)
        def _():
            dq_ref[...] = jnp.zeros_like(dq_ref)
            lane = lax.broadcasted_iota(jnp.int32, (TQ, LANE), 1)

            @pl.loop(0, nt)
            def _(i):
                rows = pl.ds(pl.multiple_of(i * TQ, TQ), TQ)
                do, o = do_ref[rows, :], o_ref[rows, :]
                for hh in range(2):
                    doh = jnp.where((lane < 64) if hh == 0 else (lane >= 64), do, 0.0)
                    do_s[hh, rows, :] = doh.astype(BF16)
                    delta_s[hh, rows, :] = jnp.broadcast_to(jnp.sum(doh * o, axis=-1, keepdims=True), (TQ, LANE))

        kjs, vjs = (k_ref[0], k_ref[1]), (v_ref[0], v_ref[1])
        wide = lambda a: jnp.concatenate([a] * (TQ // LANE), axis=1)

        def tile(hh, i, carry, diag):
            dk, dv = carry
            rows = pl.ds(pl.multiple_of(i * TQ, TQ), TQ)
            qi, do_b = q_ref[hh, rows, :], do_s[hh, rows, :]
            p = jnp.exp2(_mm_nt(qi, kjs[hh]) * QK_SCALE2 - wide(lse_ref[hh, rows, :]))
            if diag:
                p = jnp.where(_diag_visible(), p, 0.0)
            dv = dv + _mm_tn(do_b, p.astype(BF16))
            ds_b = (p * (_mm_nt(do_b, vjs[hh]) - wide(delta_s[hh, rows, :]))).astype(BF16)
            dk = dk + _mm_tn(qi, ds_b)
            dq_ref[hh, rows, :] += _mm(ds_b, kjs[hh])
            return dk, dv

        def step(i, carry):
            return tile(0, i, carry[0], False), tile(1, i, carry[1], False)

        z = jnp.zeros((LANE, TQ), F32)
        first = (tile(0, j, (z, z), True), tile(1, j, (z, z), True))
        carry = lax.fori_loop(j + 1, nt, step, first)
        for hh in range(2):
            dk_ref[hh] = carry[hh][0].T * scale
            dv_ref[hh] = carry[hh][1].T

        @pl.when(j == nt - 1)
        def _():
            dq_ref[...] = dq_ref[...] * scale

    whole = pl.BlockSpec((2, s, LANE), lambda p, j: (p, 0, 0))
    tile_spec = pl.BlockSpec((2, TQ, LANE), lambda p, j: (p, j, 0))
    cols = pl.BlockSpec((s, LANE), lambda p, j: (0, p))
    hs = jax.ShapeDtypeStruct((NH, s, LANE), F32)
    return pl.pallas_call(
        body, name="attn_bwd", grid=(NH // 2, nt),
        in_specs=[whole, tile_spec, tile_spec, cols, cols, whole, pl.BlockSpec((8, LANE), lambda p, j: (0, 0))],
        out_specs=[whole, tile_spec, tile_spec],
        out_shape=[hs, hs, hs],
        scratch_shapes=[pltpu.VMEM((2, s, LANE), BF16), pltpu.VMEM((2, s, LANE), F32)],
        compiler_params=_params(("parallel", "arbitrary")),
    )(q, k, vv, attn, dattn, lse, token)


def _hgrn_bwd(proj, lbl, states, do_raw):
    s = proj.shape[0]
    nt = s // TH
    nch = TH // HG_BLOCK

    def body(hq_ref, hf_ref, hi_ref, lbl_ref, st_ref, do_ref, dh_ref, dlbl_ref, dst, dlb):
        step = pl.program_id(1)

        @pl.when(step == 0)
        def _():
            dst[...] = jnp.zeros_like(dst)
            dlb[...] = jnp.zeros_like(dlb)

        m0, bd = _pair_masks()
        for u in range(HG_PAIRS):
            lanes = slice(u * LANE, (u + 1) * LANE)
            lb = _lower_bound(lbl_ref[:, lanes])
            gt = _hgrn_gates(hq_ref[:, lanes], hf_ref[:, lanes], lb)
            do = do_ref[:, lanes]
            qd, ki, ke = gt["qd"], gt["ki"], gt["ke"]
            v_b, do_b = hi_ref[:, lanes].astype(BF16), do.astype(BF16)
            qd_b, ki_b, ke_b = qd.astype(BF16), ki.astype(BF16), ke.astype(BF16)
            dv = jnp.zeros((TH, LANE), F32)
            dqd = jnp.zeros((TH, LANE), F32)
            dki = jnp.zeros((TH, LANE), F32)
            for hh in range(2):
                mh = m0 if hh == 0 else jnp.logical_not(m0)
                a_b = jnp.where(gt["tri"], _mm_nt(jnp.where(mh, qd, 0.0).astype(BF16), ki_b), 0.0).astype(BF16)
                doh_b = jnp.where(mh, do, 0.0).astype(BF16)
                da_b = jnp.where(gt["tri"], _mm_nt(doh_b, v_b), 0.0).astype(BF16)
                dv = dv + _mm_tn(a_b, doh_b)
                dqd = jnp.where(mh, _mm(da_b, ki_b), dqd)
                dki = jnp.where(mh, _mm_tn(da_b, qd_b), dki)
            fed = _bmm_tn(_chunks(do_b), _chunks(qd_b))
            decay = jnp.exp(gt["total"])
            ds, leaving = dst[u], [None] * nch
            for n in reversed(range(nch)):
                leaving[n] = ds
                ds = decay[n] * ds + jnp.where(bd, fed[n], 0.0)
            dst[u] = ds
            leaving = jnp.stack(leaving)
            entering = st_ref[u]
            leaving_b = leaving.astype(BF16)
            dke3 = _bmm_nn(_chunks(v_b), leaving_b)
            dv = dv + _bmm_nt(_chunks(ke_b), leaving_b).reshape(TH, LANE)
            dqd = dqd + _bmm_nn(_chunks(do_b), entering.astype(BF16)).reshape(TH, LANE)
            dke = dke3.reshape(TH, LANE)
            dlast = (jnp.sum(dke3 * _chunks(ke), axis=1, keepdims=True)
                     + jnp.sum(leaving * entering, axis=1, keepdims=True) * decay)
            dk = dki * gt["ei"] + dke * gt["ee"]
            dcum = dqd * qd - dki * ki - dke * ke
            dg = _chunk_cumsum(dcum, reverse=True) + jnp.broadcast_to(dlast, (nch, HG_BLOCK, LANE)).reshape(TH, LANE)
            sig = gt["sig"]
            df = dg / gt["f"] - dk
            dlb[:, lanes] += jnp.sum(df * (1.0 - sig), axis=0, keepdims=True)
            dh_ref[0, :, lanes] = (dqd * gt["e"]).astype(BF16)
            dh_ref[1, :, lanes] = ((df * (1.0 - lb)) * sig * (1.0 - sig)).astype(BF16)
            dh_ref[2, :, lanes] = dv.astype(BF16)

        @pl.when(step == nt - 1)
        def _():
            lb = _lower_bound(lbl_ref[...])
            da0 = dlb[...] * lb * (1.0 - lb)
            dlbl_ref[...] = jnp.concatenate([da0, -da0], axis=0)

    wide = HG_PAIRS * LANE
    col = lambda base: pl.BlockSpec((TH, wide), lambda p, i: (nt - 1 - i, base // wide + p))
    tile = pl.BlockSpec((TH, wide), lambda p, i: (nt - 1 - i, p))
    sds = jax.ShapeDtypeStruct
    return pl.pallas_call(
        body, name="hgrn_bwd", grid=(NH // 2 // HG_PAIRS, nt),
        in_specs=[col(P_HQ), col(P_HF), col(P_HI), pl.BlockSpec((2, wide), lambda p, i: (0, p)),
                  pl.BlockSpec((HG_PAIRS, nch, LANE, LANE), lambda p, i: (p, nt - 1 - i, 0, 0)), tile],
        out_specs=[pl.BlockSpec((3, TH, wide), lambda p, i: (0, nt - 1 - i, p)), pl.BlockSpec((2, wide), lambda p, i: (0, p))],
        out_shape=[sds((3, s, 512), BF16), sds((2, 512), F32)],
        scratch_shapes=[pltpu.VMEM((HG_PAIRS, LANE, LANE), F32), pltpu.VMEM((1, wide), F32)],
        compiler_params=_params(("parallel", "arbitrary")),
    )(proj, proj, proj, lbl, states, do_raw)


def _norm_rows_bwd(v, r, g, dn):
    dgv = dn * g
    return r * dgv - v * (r * r * r) * jnp.mean(v * dgv, axis=-1, keepdims=True)


def _qkv_bwd(proj, dq, dk, dvv, g_q, g_kv, w_uq_p, w_k_p, w_v_p, rc, rs1, rs2):
    s = proj.shape[0]

    def body(cq_ref, ckv_ref, dq_ref, dk_ref, dv_ref, gq_ref, gkv_ref, wq_ref, wk_ref, wv_ref, c_ref, s1_ref, s2_ref,
             dcq_ref, dckv_ref, dkpe_ref, dwq_ref, dwk_ref, dwv_ref, dgq_ref, dgkv_ref):
        @pl.when(pl.program_id(0) == 0)
        def _():
            for rf in (dwq_ref, dwk_ref, dwv_ref, dgq_ref, dgkv_ref):
                rf[...] = jnp.zeros_like(rf)

        c, s1, s2 = c_ref[...], s1_ref[...], s2_ref[...]
        cq, ckv = cq_ref[...], ckv_ref[...]
        gq, gkv = gq_ref[...], gkv_ref[...]
        cqn, rq = _norm_rows(cq, gq)
        ckvn, rkv = _norm_rows(ckv, gkv)
        cqn_b, ckvn_b = cqn.astype(BF16), ckvn.astype(BF16)
        dqf = jnp.concatenate([_rope_t(dq_ref[h], c, s1, s2) for h in range(NH)], axis=1).astype(BF16)
        dkf = jnp.concatenate([dk_ref[h] for h in range(NH)], axis=1).astype(BF16)
        dvf = jnp.concatenate([dv_ref[h] for h in range(NH)], axis=1).astype(BF16)
        dkpe = dk_ref[0]
        for h in range(1, NH):
            dkpe = dkpe + dk_ref[h]
        lane = lax.broadcasted_iota(jnp.int32, (TM, LANE), 1)
        dkpe = jnp.where((lane >= QK_NOPE) & (lane < QK_NOPE + QK_ROPE), dkpe, 0.0)
        dkpe_ref[...] = _rope_t(dkpe, c, s1, s2).astype(BF16)
        dwq_ref[...] += _mm_tn(cqn_b, dqf)
        dwk_ref[...] += _mm_tn(ckvn_b, dkf)
        dwv_ref[...] += _mm_tn(ckvn_b, dvf)
        dcqn = _mm_nt(dqf, wq_ref[...])
        dckvn = _mm_nt(dkf, wk_ref[...]) + _mm_nt(dvf, wv_ref[...])
        dgq_ref[...] += jnp.sum(dcqn * (cq * rq), axis=0, keepdims=True)
        dgkv_ref[...] += jnp.sum(dckvn * (ckv * rkv), axis=0, keepdims=True)
        dcq_ref[...] = _norm_rows_bwd(cq, rq, gq, dcqn).astype(BF16)
        dckv_ref[...] = _norm_rows_bwd(ckv, rkv, gkv, dckvn).astype(BF16)

    row = lambda w, j=0: pl.BlockSpec((TM, w), lambda i: (i, j))
    full = lambda a: pl.BlockSpec(a.shape, lambda i: (0,) * a.ndim)
    acc = lambda shape: pl.BlockSpec(shape, lambda i: (0, 0))
    heads = pl.BlockSpec((NH, TM, LANE), lambda i: (0, i, 0))
    sds = jax.ShapeDtypeStruct
    return pl.pallas_call(
        body, name="qkv_bwd", grid=(s // TM,),
        in_specs=[row(Q_LORA, P_CQ // Q_LORA), row(KV_LORA, P_CKV // KV_LORA), heads, heads, heads,
                  full(g_q), full(g_kv), full(w_uq_p), full(w_k_p), full(w_v_p), row(LANE), row(LANE), row(LANE)],
        out_specs=[row(Q_LORA), row(KV_LORA), row(LANE), acc((Q_LORA, D)), acc((KV_LORA, D)), acc((KV_LORA, D)),
                   acc((1, Q_LORA)), acc((1, KV_LORA))],
        out_shape=[sds((s, Q_LORA), BF16), sds((s, KV_LORA), BF16), sds((s, LANE), BF16), sds((Q_LORA, D), F32),
                   sds((KV_LORA, D), F32), sds((KV_LORA, D), F32), sds((1, Q_LORA), F32), sds((1, KV_LORA), F32)],
        compiler_params=_params(("arbitrary",)),
    )(proj, proj, dq, dk, dvv, g_q, g_kv, w_uq_p, w_k_p, w_v_p, rc, rs1, rs2)


def _front_bwd(x, dout, dmg, dga, dh3, dgb, dcq, dckv, dkpe, g_pre, w_in_pt, token):
    s = x.shape[0]

    def body(x_ref, do_ref, dmg_ref, dga_ref, dh3_ref, dgb_ref, dcq_ref, dckv_ref, dkpe_ref, g_ref, w_ref, token_ref, gx_ref, dg_ref):
        @pl.when(pl.program_id(0) == 0)
        def _():
            dg_ref[...] = jnp.zeros_like(dg_ref)

        xv, g = x_ref[...], g_ref[...]
        _, r = _norm_rows(xv, g)
        pieces = ((dmg_ref[...], P_MERGE), (dga_ref[...], P_GA), (dh3_ref[0], P_HQ), (dh3_ref[1], P_HF), (dh3_ref[2], P_HI),
                  (dgb_ref[...], P_GB), (dcq_ref[...], P_CQ), (dckv_ref[...], P_CKV), (dkpe_ref[...], P_KPE))
        dh = jnp.zeros((TM, D), F32)
        for piece, off in pieces:
            dh = dh + _mm(piece, w_ref[off:off + piece.shape[1], :])
        dg_ref[...] += jnp.sum(dh * (xv * r), axis=0, keepdims=True)
        gx_ref[...] = do_ref[...] + _norm_rows_bwd(xv, r, g, dh)

    row = lambda w: pl.BlockSpec((TM, w), lambda i: (i, 0))
    full = lambda a: pl.BlockSpec(a.shape, lambda i: (0,) * a.ndim)
    sds = jax.ShapeDtypeStruct
    return pl.pallas_call(
        body, name="front_bwd", grid=(s // TM,),
        in_specs=[row(D), row(D), row(2048), row(512), pl.BlockSpec((3, TM, 512), lambda i: (0, i, 0)), row(512), row(Q_LORA),
                  row(KV_LORA), row(LANE), full(g_pre), full(w_in_pt), pl.BlockSpec(memory_space=pl.ANY)],
        out_specs=[row(D), pl.BlockSpec((1, D), lambda i: (0, 0))],
        out_shape=[sds((s, D), F32), sds((1, D), F32)],
        compiler_params=_params(("arbitrary",)),
    )(x, dout, dmg, dga, dh3, dgb, dcq, dckv, dkpe, g_pre, w_in_pt, token)


TK_GRAD = 512


def _win_grad(h, pieces, name):
    s = h.shape[0]
    n = len(pieces)

    def body(h_ref, *refs):
        @pl.when(pl.program_id(0) == 0)
        def _():
            for o_ref in refs[n:]:
                o_ref[...] = jnp.zeros_like(o_ref)

        hv = h_ref[...]
        for d_ref, o_ref in zip(refs[:n], refs[n:]):
            if len(d_ref.shape) == 3:
                for k in range(d_ref.shape[0]):
                    o_ref[k] += _mm_tn(d_ref[k], hv)
            else:
                o_ref[...] += _mm_tn(d_ref[...], hv)

    def in_spec(p):
        if p.ndim == 3:
            return pl.BlockSpec((p.shape[0], TK_GRAD, p.shape[2]), lambda kk: (0, kk, 0))
        return pl.BlockSpec((TK_GRAD, p.shape[1]), lambda kk: (kk, 0))

    out_shapes = [(p.shape[0], p.shape[2], D) if p.ndim == 3 else (p.shape[1], D) for p in pieces]
    return pl.pallas_call(
        body, name=name, grid=(s // TK_GRAD,),
        in_specs=[pl.BlockSpec((TK_GRAD, D), lambda kk: (kk, 0))] + [in_spec(p) for p in pieces],
        out_specs=[pl.BlockSpec(sh, lambda kk, nd=len(sh): (0,) * nd) for sh in out_shapes],
        out_shape=[jax.ShapeDtypeStruct(sh, F32) for sh in out_shapes],
        compiler_params=_params(("arbitrary",)),
    )(h, *pieces)


def _pad_win_t(w_in_t):
    z = lambda n: jnp.zeros((n, w_in_t.shape[1]), w_in_t.dtype)
    sl = lambda o, n: w_in_t[o:o + n]
    return jnp.concatenate([sl(O_MERGE, 2048), sl(O_GA, 512), sl(O_HQ, 512), sl(O_HF, 512), sl(O_HI, 512), sl(O_GB, 512),
                            sl(O_CQ, Q_LORA), sl(O_CKV, KV_LORA), z(64), sl(O_KPE, QK_ROPE), z(32)], axis=0)


def _pad_wuq(w_uq):
    w = w_uq.reshape(Q_LORA, NH, QK_NOPE + QK_ROPE)
    return jnp.pad(w, ((0, 0), (0, 0), (0, LANE - QK_NOPE - QK_ROPE))).reshape(Q_LORA, NH * LANE)


def _unpad_wuq(g):
    return g.reshape(Q_LORA, NH, LANE)[:, :, :QK_NOPE + QK_ROPE].reshape(Q_LORA, NH * (QK_NOPE + QK_ROPE))


def _pad_wukv(w_ukv):
    w = w_ukv.reshape(KV_LORA, NH, QK_NOPE + V_DIM)
    w_k = jnp.pad(w[:, :, :QK_NOPE], ((0, 0), (0, 0), (0, LANE - QK_NOPE))).reshape(KV_LORA, NH * LANE)
    wv = w[:, :, QK_NOPE:].reshape(KV_LORA, NH // 2, 2, 1, V_DIM)
    eye = jnp.eye(2, dtype=w.dtype).reshape(1, 1, 2, 2, 1)
    return w_k, (wv * eye).reshape(KV_LORA, NH * LANE)


def _unpad_wukv(gk, gv):
    gk = gk.reshape(KV_LORA, NH, LANE)[:, :, :QK_NOPE]
    gv = gv.reshape(KV_LORA, NH // 2, 2, 2, V_DIM)
    gv = jnp.stack([gv[:, :, 0, 0], gv[:, :, 1, 1]], axis=2).reshape(KV_LORA, NH, V_DIM)
    return jnp.concatenate([gk, gv], axis=-1).reshape(KV_LORA, NH * (QK_NOPE + V_DIM))


def _local_step(x, tgt, g_pre, w_in_t, b_gate, g_q, g_kv, lb_logits, g_hgrn, g_post, weights, exchange=None):
    s = x.shape[0]
    w_in_p = _pad_win_t(w_in_t)
    rc, rs1, rs2 = _rope_tables(s)
    g_hg = jnp.tile(g_hgrn, (1, NH))

    proj, h = _front_fwd(x, g_pre, w_in_p, weights.tokens)
    w_uq, w_ukv = weights.qkv(h)
    w_uq_p = _pad_wuq(w_uq)
    w_k_p, w_v_p = _pad_wukv(w_ukv)
    q, k, vv = _qkv_fwd(proj, g_q, g_kv, w_uq_p, w_k_p, w_v_p, rc, rs1, rs2)
    attn, lse = _attn_fwd(q, k, vv)
    o_raw, states = _hgrn_fwd(proj, lb_logits)
    wa, wb, w_out = weights.mid(o_raw)
    (loss, dout, dattn, dga, dor, dgb, dmg, d_wout, d_wa, d_wb, d_gpost, d_bgate, d_ghg) = _mid(
        proj, attn, o_raw, x, tgt, g_hg, b_gate, g_post, wa, wb, w_out)
    w_mg, w_ga, w_gb = _win_grad(h, [dmg, dga, dgb], "win_grad_mid")
    dh3, d_lbl = _hgrn_bwd(proj, lb_logits, states, dor)
    (w_h3,) = _win_grad(h, [dh3], "win_grad_hgrn")
    d_win_rest = jnp.concatenate([w_ga, w_h3[0], w_h3[1], w_h3[2], w_gb, w_mg], axis=0)
    early = dict(w_in_rest=d_win_rest, w_branch_a=d_wa, w_branch_b=d_wb, w_out=d_wout)
    token = exchange.start_early(early) if exchange else jnp.zeros((8, LANE), F32)
    dq, dk, dvv = _attn_bwd(q, k, vv, attn, dattn, lse, token)
    dcq, dckv, dkpe, d_wuq_p, d_wk_p, d_wv_p, d_gq, d_gkv = _qkv_bwd(proj, dq, dk, dvv, g_q, g_kv, w_uq_p, w_k_p, w_v_p, rc, rs1, rs2)
    w_cq, w_ckv, w_kpe = _win_grad(h, [dcq, dckv, dkpe], "win_grad_qkv")
    d_win_qkv = jnp.concatenate([w_cq, w_ckv, w_kpe[64:64 + QK_ROPE]], axis=0)
    late = dict(w_in_qkv=d_win_qkv, w_uq=_unpad_wuq(d_wuq_p), w_ukv=_unpad_wukv(d_wk_p, d_wv_p))
    token = exchange.start_late(late) if exchange else jnp.zeros((8, LANE), F32)
    grad_x, d_gpre = _front_bwd(x, dout, dmg, dga, dh3, dgb, dcq, dckv, dkpe, g_pre, w_in_p, token)
    vec_grads = dict(g_pre=d_gpre, b_gate=d_bgate, g_q=d_gq, g_kv=d_gkv, lb_logits=d_lbl, g_hgrn=d_ghg, g_post=d_gpost)
    return loss, grad_x, dict(early, **late), vec_grads


SHARD_SHAPES = (("w_in", (1416, 1024)), ("w_uq", (192, 768)), ("w_ukv", (256, 256)), ("w_branch_a", (512, 256)),
                ("w_branch_b", (512, 256)), ("w_out", (256, 1024)))
BIG = tuple(n for n, _ in SHARD_SHAPES)
ROW_SHARDED = ("w_in", "w_uq", "w_out")
GATHER_SPLIT_AXIS = dict(w_in=1, w_uq=0, w_ukv=0, w_branch_a=0, w_branch_b=0, w_out=0)
N_CHIPS = 4
QKV_ROWS = Q_LORA + KV_LORA + QK_ROPE


def _to_block(name, a):
    return a[0].T if name == "w_in" else a[0]


def _from_block(name, a):
    return a.T[None] if name == "w_in" else a[None]
VEC_ROWS = (("g_pre", 0, 1024), ("b_gate", 1, 2048), ("g_q", 2, 768), ("g_kv", 3, 256), ("g_hgrn", 6, 64), ("g_post", 7, 1024))
VEC_LB_ROW = 4
VEC_SHAPE = (8, 2048)


def _split_by_chip(name, g):
    a, b = dict(SHARD_SHAPES)[name]
    return g.reshape(N_CHIPS, a, b) if name in ROW_SHARDED else g.reshape(a, N_CHIPS, b).transpose(1, 0, 2)


def _join_chips(name, w):
    a, b = dict(SHARD_SHAPES)[name]
    return w.reshape(N_CHIPS * a, b) if name in ROW_SHARDED else w.transpose(1, 0, 2).reshape(a, N_CHIPS * b)


MESH = pl.DeviceIdType.MESH
HBM = pl.BlockSpec(memory_space=pltpu.HBM)


def _mesh_place():
    x, y, c = lax.axis_index("x"), lax.axis_index("y"), lax.axis_index("c")
    return x, y, c, 2 * x + y, [(1 - x, y), (x, 1 - y), (1 - x, 1 - y)]


def _remote(src, dst, send_sems, recv_sems, k, to):
    return pltpu.make_async_remote_copy(src_ref=src, dst_ref=dst, send_sem=send_sems.at[k], recv_sem=recv_sems.at[k],
                                        device_id=to, device_id_type=MESH)


def _gather_weights(shards, split_axes):
    n = len(shards)

    def body(*refs):
        srcs, outs = refs[:n], refs[n:2 * n]
        ici_send, ici_recv, d2d_send, d2d_recv, local_sems = refs[2 * n:]
        x, y, c, me, chips = _mesh_place()
        sibling = (x, y, 1 - c)

        def half(ref, k, which):
            size = shards[k].shape[split_axes[k]] // 2
            part = pl.ds(pl.multiple_of(which * size, size), size)
            return ref.at[part] if split_axes[k] == 0 else ref.at[:, part]

        own = [pltpu.make_async_copy(srcs[k], outs[k].at[me], local_sems.at[k]) for k in range(n)]
        for cp in own:
            cp.start()
        started = []
        for k in range(n):
            for j, (px, py) in enumerate(chips):
                cp = _remote(half(srcs[k], k, c), half(outs[k].at[me], k, c), ici_send, ici_recv, 3 * k + j, (px, py, c))
                cp.start()
                started.append(cp)
        for k in range(n):
            for j, (px, py) in enumerate(chips):
                landed = half(outs[k].at[2 * px + py], k, c)
                _remote(landed, landed, ici_send, ici_recv, 3 * k + j, (px, py, c)).wait_recv()
                cp = _remote(landed, landed, d2d_send, d2d_recv, 3 * k + j, sibling)
                cp.start()
                started.append(cp)
        for k in range(n):
            for j, (px, py) in enumerate(chips):
                other = half(outs[k].at[2 * px + py], k, 1 - c)
                _remote(other, other, d2d_send, d2d_recv, 3 * k + j, sibling).wait_recv()
        for cp in started:
            cp.wait_send()
        for cp in own:
            cp.wait()

    sems = pltpu.SemaphoreType.DMA((3 * n,))
    return pl.pallas_call(
        body, name="gather_weights", in_specs=[HBM] * n, out_specs=[HBM] * n,
        out_shape=[jax.ShapeDtypeStruct((N_CHIPS,) + s.shape, s.dtype) for s in shards],
        scratch_shapes=[sems, sems, sems, sems, pltpu.SemaphoreType.DMA((n,))],
        compiler_params=pltpu.CompilerParams(has_side_effects=True),
    )(*shards)


def _chip_exchange(srcs, name):
    n = len(srcs)

    def body(*refs):
        src_refs, outs = refs[:n], refs[n:2 * n]
        send_sems, recv_sems, local_sems = refs[2 * n:]
        x, y, c, me, chips = _mesh_place()

        def slab(k, t):
            return src_refs[k] if srcs[k].ndim == 2 else src_refs[k].at[t]

        own = [pltpu.make_async_copy(slab(k, me), outs[k].at[me], local_sems.at[k]) for k in range(n)]
        for cp in own:
            cp.start()
        sends = []
        for k in range(n):
            for j, (px, py) in enumerate(chips):
                cp = _remote(slab(k, 2 * px + py), outs[k].at[me], send_sems, recv_sems, 3 * k + j, (px, py, c))
                cp.start()
                sends.append(cp)
        for k in range(n):
            for j, (px, py) in enumerate(chips):
                _remote(slab(k, me), outs[k].at[2 * px + py], send_sems, recv_sems, 3 * k + j, (px, py, c)).wait_recv()
        for cp in sends:
            cp.wait_send()
        for cp in own:
            cp.wait()

    sems = pltpu.SemaphoreType.DMA((3 * n,))
    return pl.pallas_call(
        body, name=name, in_specs=[HBM] * n, out_specs=[HBM] * n,
        out_shape=[jax.ShapeDtypeStruct((N_CHIPS,) + s.shape[-2:], s.dtype) for s in srcs],
        scratch_shapes=[sems, sems, pltpu.SemaphoreType.DMA((n,))],
        compiler_params=pltpu.CompilerParams(has_side_effects=True),
    )(*srcs)


def _sibling_exchange(srcs, name, after=None):
    n = len(srcs)
    extra = [] if after is None else [after]

    def body(*refs):
        src_refs, outs = refs[:n], refs[n + len(extra):2 * n + len(extra)]
        send_sems, recv_sems = refs[2 * n + len(extra):]
        sibling = (lax.axis_index("x"), lax.axis_index("y"), 1 - lax.axis_index("c"))
        copies = [_remote(src_refs[k], outs[k], send_sems, recv_sems, k, sibling) for k in range(n)]
        for cp in copies:
            cp.start()
        for cp in copies:
            cp.wait()

    sems = pltpu.SemaphoreType.DMA((n,))
    return pl.pallas_call(
        body, name=name, in_specs=[HBM] * n + [pl.BlockSpec(memory_space=pl.ANY)] * len(extra), out_specs=[HBM] * n,
        out_shape=[jax.ShapeDtypeStruct(s.shape, s.dtype) for s in srcs],
        scratch_shapes=[sems, sems],
        compiler_params=pltpu.CompilerParams(has_side_effects=True),
    )(*srcs, *extra)


SEM = pl.BlockSpec(memory_space=pltpu.SEMAPHORE)
DATAFLOW = pltpu.SideEffectType.DATAFLOW_SIDE_EFFECTING


def _exchange_copies(srcs, to_first, src_refs, land_refs, send_sems, recv_sems):
    x, y, c, me, chips = _mesh_place()
    n = len(srcs)
    sends, recvs = [], []
    for k in range(n):
        if k in to_first:
            base = 3 * n + 4 * to_first.index(k)
            sends.append((me != 0, pltpu.make_async_remote_copy(
                src_ref=src_refs[k], dst_ref=land_refs[k].at[me], send_sem=send_sems.at[base], recv_sem=recv_sems.at[base + me],
                device_id=(0, 0, c), device_id_type=MESH)))
            for s in range(1, N_CHIPS):
                recvs.append((me == 0, pltpu.make_async_remote_copy(
                    src_ref=src_refs[k], dst_ref=land_refs[k].at[s], send_sem=send_sems.at[base], recv_sem=recv_sems.at[base + s],
                    device_id=(s // 2, s % 2, c), device_id_type=MESH)))
        else:
            slab = (lambda t, k=k: src_refs[k]) if srcs[k].ndim == 2 else (lambda t, k=k: src_refs[k].at[t])
            for j, (px, py) in enumerate(chips):
                sends.append((None, _remote(slab(2 * px + py), land_refs[k].at[me], send_sems, recv_sems, 3 * k + j, (px, py, c))))
                recvs.append((None, _remote(slab(me), land_refs[k].at[2 * px + py], send_sems, recv_sems, 3 * k + j, (px, py, c))))
    return sends, recvs


def _when(pred, fn):
    if pred is None:
        fn()
    else:
        pl.when(pred)(fn)


def _exchange_start(srcs, to_first, name, after=None):
    n = len(srcs)
    n_sems = 3 * n + 4 * len(to_first)
    lands = [lax.empty((N_CHIPS,) + s.shape[-2:], s.dtype) for s in srcs]
    extra = [] if after is None else [after]

    def body(*refs):
        src_refs, land_refs = refs[:n], refs[n:2 * n]
        send_sems, recv_sems, token = refs[2 * n + len(extra)], refs[2 * n + len(extra) + 1], refs[-1]
        sends, _ = _exchange_copies(srcs, to_first, src_refs, land_refs, send_sems, recv_sems)
        for pred, cp in sends:
            _when(pred, cp.start)
        token[...] = jnp.zeros_like(token)

    hbm = lambda a: pltpu.HBM(a.shape, a.dtype)
    res = pl.pallas_call(
        body, name=name,
        out_shape=[pltpu.SemaphoreType.DMA((n_sems,)), pltpu.SemaphoreType.DMA((n_sems,))] + [hbm(a) for a in srcs + lands]
        + [jax.ShapeDtypeStruct((8, LANE), F32)],
        in_specs=[HBM] * (2 * n) + [pl.BlockSpec(memory_space=pl.ANY)] * len(extra),
        out_specs=[SEM, SEM] + [HBM] * (2 * n) + [pl.BlockSpec(memory_space=pltpu.VMEM)],
        input_output_aliases={i: 2 + i for i in range(2 * n)},
        compiler_params=pltpu.CompilerParams(has_side_effects=DATAFLOW),
    )(*[pltpu.with_memory_space_constraint(a, pltpu.HBM) for a in srcs + lands], *extra)
    return res[:-1], res[-1]


def _exchange_wait(srcs, to_first, started, after, name):
    n = len(srcs)
    send_sems, recv_sems, thru = started[0], started[1], started[2:]

    def body(*refs):
        src_refs, land_refs, send_ref, recv_ref = refs[:n], refs[n:2 * n], refs[2 * n], refs[2 * n + 1]
        sends, recvs = _exchange_copies(srcs, to_first, src_refs, land_refs, send_ref, recv_ref)
        for pred, cp in sends:
            _when(pred, cp.wait_send)
        for pred, cp in recvs:
            _when(pred, cp.wait_recv)

    res = pl.pallas_call(
        body, name=name, out_shape=[pltpu.HBM(a.shape, a.dtype) for a in thru],
        in_specs=[HBM] * (2 * n) + [SEM, SEM, pl.BlockSpec(memory_space=pl.ANY)], out_specs=[HBM] * (2 * n),
        input_output_aliases={i: i for i in range(2 * n)},
        compiler_params=pltpu.CompilerParams(has_side_effects=DATAFLOW),
    )(*thru, send_sems, recv_sems, after)
    return res[n:]


ROW_TILE = 256
COL_TILE = 256


def _block_tiling(a, b):
    if a <= ROW_TILE or a % ROW_TILE == 0:
        ta = min(a, ROW_TILE)
        return a // ta, (ta, b), lambda i: (i, 0)
    return b // COL_TILE, (a, COL_TILE), lambda i: (0, i)


def _sum_chips(parts, name):
    _, a, b = parts.shape
    steps, tile, at = _block_tiling(a, b)

    def body(p_ref, o_ref):
        f = lambda t: p_ref[t].astype(F32)
        o_ref[...] = ((f(0) + f(1)) + f(2)) + f(3)

    return pl.pallas_call(
        body, name=name, grid=(steps,),
        in_specs=[pl.BlockSpec((N_CHIPS,) + tile, lambda i: (0,) + at(i))],
        out_specs=pl.BlockSpec(tile, at),
        out_shape=jax.ShapeDtypeStruct((a, b), F32),
        compiler_params=_params(("parallel",)),
    )(parts)


def _sum_landed(land, own, name, first_land=None, first_own=None):
    _, a, b = land.shape
    steps, tile, at = _block_tiling(a, b)
    extra = first_land is not None

    def body(*refs):
        p_ref, own_ref, o_ref = refs[0], refs[1], refs[-1]
        me = 2 * lax.axis_index("x") + lax.axis_index("y")
        own = own_ref[...].astype(F32)
        slot = lambda t: jnp.where(me == t, own, p_ref[t].astype(F32))
        o_ref[...] = ((slot(0) + slot(1)) + slot(2)) + slot(3)
        if extra:
            fp_ref, fo_ref = refs[2], refs[3]
            r = fo_ref.shape[0]

            @pl.when(me == 0)
            def _():
                f = lambda t: fp_ref[t].astype(F32)
                rows = pl.ds(pl.multiple_of(lax.axis_index("c") * r, 8), r)
                o_ref[rows, :] += ((fo_ref[...].astype(F32) + f(1)) + f(2)) + f(3)

    in_specs = [pl.BlockSpec((N_CHIPS,) + tile, lambda i: (0,) + at(i)), pl.BlockSpec(tile, at)]
    args = [land, own]
    if extra:
        r = first_own.shape[0]
        assert tile[0] == a, "the extra rows need whole columns in a step"
        in_specs += [pl.BlockSpec((N_CHIPS, r, tile[1]), lambda i: (0,) + at(i)), pl.BlockSpec((r, tile[1]), at)]
        args += [first_land, first_own]
    return pl.pallas_call(
        body, name=name, grid=(steps,), in_specs=in_specs, out_specs=pl.BlockSpec(tile, at),
        out_shape=jax.ShapeDtypeStruct((a, b), F32), compiler_params=_params(("parallel",)),
    )(*args)


def _add_cast(a, b, name):
    def body(a_ref, b_ref, o_ref):
        o_ref[...] = (a_ref[...] + b_ref[...]).astype(BF16)

    return pl.pallas_call(body, name=name, out_shape=jax.ShapeDtypeStruct(a.shape, BF16),
                          compiler_params=_params(()))(a, b)


class _LaterWeights:
    QKV = ("w_uq", "w_ukv")
    MID = ("w_branch_a", "w_branch_b", "w_out")

    def __init__(self, blocks, after):
        self.blocks = blocks
        self.qkv_started, t1 = _exchange_start([blocks[n] for n in self.QKV], (), "weights_qkv_start", after)
        self.mid_started, t2 = _exchange_start([blocks[n] for n in self.MID], (), "weights_mid_start", after)
        self.tokens = [t1, t2]

    def _whole(self, names, started, after, name):
        landed = _exchange_wait([self.blocks[n] for n in names], (), started, after, name)
        me = 2 * lax.axis_index("x") + lax.axis_index("y")
        return [_join_chips(n, lax.dynamic_update_index_in_dim(land, self.blocks[n], me, 0)) for n, land in zip(names, landed)]

    def qkv(self, after):
        return self._whole(self.QKV, self.qkv_started, after, "weights_qkv_wait")

    def mid(self, after):
        return self._whole(self.MID, self.mid_started, after, "weights_mid_wait")


class _GradExchange:
    EARLY = ("w_in", "w_branch_a", "w_branch_b", "w_out")
    LATE = ("w_uq", "w_ukv")

    def __init__(self, state):
        self.state = state
        self.outs = {}

    @staticmethod
    def _own(slabs):
        return lax.dynamic_index_in_dim(slabs, 2 * lax.axis_index("x") + lax.axis_index("y"), axis=0, keepdims=False)

    def start_early(self, g):
        full = jnp.concatenate([jnp.zeros((QKV_ROWS, D), F32), g["w_in_rest"]], axis=0)
        g = dict(g, w_in=full)
        self.early = [_split_by_chip(n, g[n]).astype(BF16) for n in self.EARLY]
        self.early_started, token = _exchange_start(self.early, (), "grads_early_start")
        return token

    def start_late(self, g):
        self.early_landed = _exchange_wait(self.early, (), self.early_started, g["w_uq"], "grads_early_wait")
        half = QKV_ROWS // 2
        c = lax.axis_index("c")
        mine = lax.dynamic_slice_in_dim(g["w_in_qkv"], c * half, half, axis=0)
        (theirs,) = _sibling_exchange([lax.dynamic_slice_in_dim(g["w_in_qkv"], (1 - c) * half, half, axis=0)], "sibling_qkv_rows")
        self.late = [_split_by_chip(n, g[n]).astype(BF16) for n in self.LATE] + [_add_cast(mine, theirs, "add_qkv_rows")]
        self.late_started, token = _exchange_start(self.late, (2,), "grads_late_start")
        names = self.EARLY[1:]
        mine = [_sum_landed(land, self._own(slabs), "sum_" + n) for n, slabs, land in list(zip(self.EARLY, self.early, self.early_landed))[1:]]
        theirs = _sibling_exchange(mine, "sibling_early", after=token)
        for n, a, b in zip(names, mine, theirs):
            self.outs[n] = _adamw(a, b, *self.state[n], "adamw_" + n)
        return self.outs[names[-1]][0]

    def finish(self, after):
        late_landed = _exchange_wait(self.late, (2,), self.late_started, after, "grads_late_wait")
        sums = {"w_in": _sum_landed(self.early_landed[0], self._own(self.early[0]), "sum_w_in",
                                    first_land=late_landed[2], first_own=self.late[2])}
        for n, slabs, land in zip(self.LATE, self.late, late_landed):
            sums[n] = _sum_landed(land, self._own(slabs), "sum_" + n)
        return sums


def _adamw_math(g, w, m, v):
    nm = ADAM_B1 * m + (1.0 - ADAM_B1) * g
    nv = ADAM_B2 * v + (1.0 - ADAM_B2) * (g * g)
    m_hat = nm / (1.0 - ADAM_B1 ** ADAM_STEP)
    v_hat = nv / (1.0 - ADAM_B2 ** ADAM_STEP)
    return -ADAM_LR * (m_hat / (jnp.sqrt(v_hat) + ADAM_EPS) + ADAM_WD * w), nm, nv


def _adamw(p_mine, p_sibling, w, m, v, name):
    a, b = p_mine.shape
    steps, tile, at = _block_tiling(a, b)

    def body(a_ref, b_ref, w_ref, m_ref, v_ref, g_ref, d_ref, nm_ref, nv_ref):
        g = a_ref[...] + b_ref[...]
        g_ref[...] = g
        d_ref[...], nm_ref[...], nv_ref[...] = _adamw_math(g, w_ref[...], m_ref[...], v_ref[...])

    spec = pl.BlockSpec(tile, at)
    sds = jax.ShapeDtypeStruct((a, b), F32)
    return pl.pallas_call(
        body, name=name, grid=(steps,), in_specs=[spec] * 5, out_specs=[spec] * 4, out_shape=[sds] * 4,
        compiler_params=_params(("parallel",)),
    )(p_mine, p_sibling, w, m, v)


LOSS_AT = (2, 1024)


def _vec_pack(vg, loss):
    names = [n for n, _, _ in VEC_ROWS]

    def body(*refs):
        o_ref = refs[-1]
        lb_ref, loss_ref = refs[len(names)], refs[len(names) + 1]
        o_ref[...] = jnp.zeros_like(o_ref)
        o_ref[LOSS_AT[0]:LOSS_AT[0] + 1, LOSS_AT[1]:LOSS_AT[1] + LANE] = jnp.broadcast_to(loss_ref[...], (1, LANE))
        for (name, row, size), ref in zip(VEC_ROWS, refs):
            if name == "g_hgrn":
                r = lax.broadcasted_iota(jnp.int32, (NH * V_DIM, LANE), 0)
                c = lax.broadcasted_iota(jnp.int32, (NH * V_DIM, LANE), 1)
                fold = ((r % V_DIM) == c).astype(F32)
                o_ref[row:row + 1, 0:LANE] = jnp.dot(ref[...], fold, precision=HIGHEST, preferred_element_type=F32)
            else:
                o_ref[row:row + 1, 0:size] = ref[...]
        o_ref[VEC_LB_ROW:VEC_LB_ROW + 2, 0:512] = lb_ref[...]

    return pl.pallas_call(body, name="vec_pack", out_shape=jax.ShapeDtypeStruct(VEC_SHAPE, F32))(
        *[vg[n] for n in names], vg["lb_logits"], loss)


def _adamw_vec(p_mine, p_sibling, w, m, v):
    names = [n for n, _, _ in VEC_ROWS] + ["lb_logits"]
    k = len(names)

    def body(a_ref, b_ref, *refs):
        ins, outs = refs[:3 * k], refs[3 * k:]
        at = (slice(LOSS_AT[0], LOSS_AT[0] + 1), slice(LOSS_AT[1], LOSS_AT[1] + LANE))
        outs[-1][...] = a_ref[at] + b_ref[at]
        for i, name in enumerate(names):
            if name == "lb_logits":
                rows, cols = slice(VEC_LB_ROW, VEC_LB_ROW + 2), slice(0, 512)
            else:
                _, row, size = VEC_ROWS[i]
                rows, cols = slice(row, row + 1), slice(0, size)
            g = a_ref[rows, cols] + b_ref[rows, cols]
            d, nm, nv = _adamw_math(g, ins[i][...], ins[k + i][...], ins[2 * k + i][...])
            for o_ref, val in zip(outs[4 * i:4 * i + 4], (g, d, nm, nv)):
                o_ref[...] = val

    shapes = [jax.ShapeDtypeStruct(w[n].shape, F32) for n in names for _ in range(4)] + [jax.ShapeDtypeStruct((1, LANE), F32)]
    res = pl.pallas_call(body, name="adamw_vec", out_shape=shapes)(
        p_mine, p_sibling, *[w[n] for n in names], *[m[n] for n in names], *[v[n] for n in names])
    return [{n: res[4 * i + j] for i, n in enumerate(names)} for j in range(4)], res[-1]


WEIGHTS = ("g_pre", "w_in", "b_gate", "g_q", "w_uq", "g_kv", "w_ukv", "lb_logits", "g_hgrn", "w_branch_a", "w_branch_b", "w_out", "g_post")


def kernel(x, g_pre, w_in, b_gate, g_q, w_uq, g_kv, w_ukv, lb_logits, g_hgrn, w_branch_a, w_branch_b, w_out, g_post, loss_target, m_g_pre, m_w_in, m_b_gate, m_g_q, m_w_uq, m_g_kv, m_w_ukv, m_lb_logits, m_g_hgrn, m_w_branch_a, m_w_branch_b, m_w_out, m_g_post, v_g_pre, v_w_in, v_b_gate, v_g_q, v_w_uq, v_g_kv, v_w_ukv, v_lb_logits, v_g_hgrn, v_w_branch_a, v_w_branch_b, v_w_out, v_g_post):
    w = dict(g_pre=g_pre, w_in=w_in, b_gate=b_gate, g_q=g_q, w_uq=w_uq, g_kv=g_kv, w_ukv=w_ukv, lb_logits=lb_logits, g_hgrn=g_hgrn,
             w_branch_a=w_branch_a, w_branch_b=w_branch_b, w_out=w_out, g_post=g_post)
    m = dict(g_pre=m_g_pre, w_in=m_w_in, b_gate=m_b_gate, g_q=m_g_q, w_uq=m_w_uq, g_kv=m_g_kv, w_ukv=m_w_ukv, lb_logits=m_lb_logits,
             g_hgrn=m_g_hgrn, w_branch_a=m_w_branch_a, w_branch_b=m_w_branch_b, w_out=m_w_out, g_post=m_g_post)
    v = dict(g_pre=v_g_pre, w_in=v_w_in, b_gate=v_b_gate, g_q=v_g_q, w_uq=v_w_uq, g_kv=v_g_kv, w_ukv=v_w_ukv, lb_logits=v_lb_logits,
             g_hgrn=v_g_hgrn, w_branch_a=v_w_branch_a, w_branch_b=v_w_branch_b, w_out=v_w_out, g_post=v_g_post)
    blocks = {n: _to_block(n, w[n]).astype(BF16) for n in BIG}
    (w_in_all,) = _gather_weights([blocks["w_in"]], [GATHER_SPLIT_AXIS["w_in"]])
    weights = _LaterWeights(blocks, w_in_all)
    state = {n: [_to_block(n, t[n]) for t in (w, m, v)] for n in BIG}
    exchange = _GradExchange(state)
    loss, grad_x, _, vec_grads = _local_step(
        x[0], loss_target[0], g_pre, _join_chips("w_in", w_in_all), b_gate, g_q, g_kv, lb_logits, g_hgrn, g_post, weights, exchange)
    sums = exchange.finish(grad_x)
    (vec_landed,) = _chip_exchange([_vec_pack(vec_grads, loss)], "scatter_vec")
    rest = tuple(sums)
    mine = [sums[n] for n in rest] + [_sum_chips(vec_landed, "sum_vec")]
    theirs = _sibling_exchange(mine, "sibling_grads")
    done = dict(exchange.outs)
    for k, n in enumerate(rest):
        done[n] = _adamw(mine[k], theirs[k], *state[n], "adamw_" + n)
    outs = [{}, {}, {}, {}]
    for n in BIG:
        for o, val in zip(outs, done[n]):
            o[n] = _from_block(n, val)
    vec_outs, total = _adamw_vec(mine[-1], theirs[-1], w, m, v)
    for o, vals in zip(outs, vec_outs):
        o.update(vals)
    return (total[0, 0], grad_x[None], *[o[n] for o in outs for n in WEIGHTS])
```

```python
import functools
import math

import numpy as np
import jax
import jax.numpy as jnp
from jax import lax
from jax.experimental import pallas as pl
from jax.experimental.pallas import tpu as pltpu

F32 = jnp.float32
BF16 = jnp.bfloat16
HIGHEST = lax.Precision.HIGHEST

D = 1024
NH = 8
QK_NOPE, QK_ROPE, V_DIM = 64, 32, 64
Q_LORA, KV_LORA = 768, 256
CHUNK = 64
HG_BLOCK = 32
EPS = 1e-6
D_IN = 5664
LANE = 128
P_MERGE, P_GA, P_HQ, P_HF, P_HI, P_GB, P_CQ, P_CKV, P_KPE = 0, 2048, 2560, 3072, 3584, 4096, 4608, 5376, 5632
D_P = 5760
O_CQ, O_CKV, O_KPE, O_GA, O_HQ, O_HF, O_HI, O_GB, O_MERGE = 0, 768, 1024, 1056, 1568, 2080, 2592, 3104, 3616

TM = 512
TM_MID = 256
TQ = 1024
ONES_LANE = (LANE - 1, 0)
TH = 256
HG_PAIRS = 4
VMEM_LIMIT = 56 * 1024 * 1024

ADAM_LR, ADAM_B1, ADAM_B2, ADAM_EPS, ADAM_WD, ADAM_STEP = 0.001, 0.9, 0.999, 1e-08, 0.01, 10

NT_DIMS = (((1,), (1,)), ((), ()))
TN_DIMS = (((0,), (0,)), ((), ()))


def _params(sem):
    return pltpu.CompilerParams(dimension_semantics=sem, vmem_limit_bytes=VMEM_LIMIT)


def _mm(a, b):
    return jnp.dot(a, b, preferred_element_type=F32)


def _mm_nt(a, b):
    return lax.dot_general(a, b, NT_DIMS, preferred_element_type=F32)


def _mm_tn(a, b):
    return lax.dot_general(a, b, TN_DIMS, preferred_element_type=F32)


def _sigmoid(z):
    return jax.nn.sigmoid(z)


def _rope(v, c, s1, s2):
    return v * c + pltpu.roll(v, 112, 1) * s1 + pltpu.roll(v, 16, 1) * s2


def _rope_t(dy, c, s1, s2):
    return dy * c + pltpu.roll(dy * s1, 16, 1) + pltpu.roll(dy * s2, 112, 1)


def _rope_tables(s):
    inv = 10000.0 ** (-jnp.arange(0, QK_ROPE, 2, dtype=F32) / QK_ROPE)
    ang = jnp.arange(s, dtype=F32)[:, None] * inv[None, :]
    cos, sin = jnp.cos(ang), jnp.sin(ang)
    z64, z32, o64, o32 = jnp.zeros((s, 64), F32), jnp.zeros((s, 32), F32), jnp.ones((s, 64), F32), jnp.ones((s, 32), F32)
    z16 = jnp.zeros((s, 16), F32)
    c = jnp.concatenate([o64, cos, cos, o32], axis=1)
    s1 = jnp.concatenate([z64, -sin, z16, z32], axis=1)
    s2 = jnp.concatenate([z64, z16, sin, z32], axis=1)
    return c, s1, s2


def _front_fwd(x, g_pre, w_in_pt, tokens=()):
    s = x.shape[0]
    tokens = list(tokens)

    def body(x_ref, g_ref, w_ref, *refs):
        o_ref, h_ref = refs[len(tokens):]
        xv = x_ref[...]
        r = lax.rsqrt(jnp.mean(xv * xv, axis=-1, keepdims=True) + EPS)
        h = ((xv * r) * g_ref[...]).astype(BF16)
        h_ref[...] = h
        o_ref[...] = _mm_nt(h, w_ref[...])

    return pl.pallas_call(
        body, name="front_fwd", grid=(s // TM,),
        in_specs=[pl.BlockSpec((TM, D), lambda i: (i, 0)), pl.BlockSpec((1, D), lambda i: (0, 0)),
                  pl.BlockSpec((D_P, D), lambda i: (0, 0))] + [pl.BlockSpec((8, LANE), lambda i: (0, 0))] * len(tokens),
        out_specs=[pl.BlockSpec((TM, D_P), lambda i: (i, 0)), pl.BlockSpec((TM, D), lambda i: (i, 0))],
        out_shape=[jax.ShapeDtypeStruct((s, D_P), F32), jax.ShapeDtypeStruct((s, D), BF16)],
        compiler_params=_params(("parallel",)),
    )(x, g_pre, w_in_pt, *tokens)


def _norm_rows(v, g):
    r = lax.rsqrt(jnp.mean(v * v, axis=-1, keepdims=True) + EPS)
    return (v * r) * g, r


def _qkv_fwd(proj, g_q, g_kv, w_uq_p, w_k_p, w_v_p, rc, rs1, rs2):
    s = proj.shape[0]

    def body(cq_ref, ckv_ref, kpe_ref, gq_ref, gkv_ref, wq_ref, wk_ref, wv_ref, c_ref, s1_ref, s2_ref, q_ref, k_ref, v_ref):
        c, s1, s2 = c_ref[...], s1_ref[...], s2_ref[...]
        cqn, _ = _norm_rows(cq_ref[...], gq_ref[...])
        ckvn, _ = _norm_rows(ckv_ref[...], gkv_ref[...])
        ckvn = ckvn.astype(BF16)
        qf = _mm(cqn.astype(BF16), wq_ref[...])
        kf = _mm(ckvn, wk_ref[...])
        vf = _mm(ckvn, wv_ref[...])
        kpe = _rope(kpe_ref[...], c, s1, s2)
        lane = lax.broadcasted_iota(jnp.int32, (TM, LANE), 1)
        for h in range(NH):
            blk = slice(h * LANE, (h + 1) * LANE)
            q_ref[h] = _rope(qf[:, blk], c, s1, s2).astype(BF16)
            k_ref[h] = (kf[:, blk] + kpe).astype(BF16)
            v_ref[h] = jnp.where(lane == ONES_LANE[h % 2], 1.0, vf[:, blk]).astype(BF16)

    row = lambda w, j: pl.BlockSpec((TM, w), lambda i: (i, j))
    full = lambda a: pl.BlockSpec(a.shape, lambda i: (0,) * a.ndim)
    hs = jax.ShapeDtypeStruct((NH, s, LANE), BF16)
    return pl.pallas_call(
        body, name="qkv_fwd", grid=(s // TM,),
        in_specs=[row(Q_LORA, P_CQ // Q_LORA), row(KV_LORA, P_CKV // KV_LORA), row(LANE, P_KPE // LANE),
                  full(g_q), full(g_kv), full(w_uq_p), full(w_k_p), full(w_v_p), row(LANE, 0), row(LANE, 0), row(LANE, 0)],
        out_specs=[pl.BlockSpec((NH, TM, LANE), lambda i: (0, i, 0))] * 3,
        out_shape=[hs, hs, hs],
        compiler_params=_params(("parallel",)),
    )(proj, proj, proj, g_q, g_kv, w_uq_p, w_k_p, w_v_p, rc, rs1, rs2)


LOG2E = 1.4426950408889634
QK_SCALE2 = LOG2E / math.sqrt(QK_NOPE + QK_ROPE)


HQ = TQ // 2


def _diag_visible(n):
    row = lax.broadcasted_iota(jnp.int32, (n, n), 0)
    col = lax.broadcasted_iota(jnp.int32, (n, n), 1)
    return (col // CHUNK) <= (row // CHUNK)


def _attn_fwd(q, k, vv):
    s = q.shape[1]

    def body(q_ref, k_ref, v_ref, o_ref, lse_ref):
        i = pl.program_id(1)
        qs = (q_ref[0], q_ref[1])

        def tile(hh, t, carry, diag):
            m, acc = carry
            rows = pl.ds(pl.multiple_of(t * TQ, TQ), TQ)
            sc = _mm_nt(qs[hh], k_ref[hh, rows, :])
            if diag:
                sc = jnp.where(_diag_visible(TQ), sc, -jnp.inf)
            m_new = jnp.maximum(m, jnp.max(sc, axis=-1, keepdims=True))
            alpha = jnp.exp2((m - m_new) * QK_SCALE2)
            p = jnp.exp2((sc - m_new) * QK_SCALE2).astype(BF16)
            acc = alpha * acc + _mm(p, v_ref[hh, rows, :])
            return m_new, acc

        def step(t, carry):
            return tile(0, t, carry[0], False), tile(1, t, carry[1], False)

        init = (jnp.full((TQ, 1), -jnp.inf, F32), jnp.zeros((TQ, LANE), F32))
        carry = lax.fori_loop(0, i, step, (init, init))
        lane = lax.broadcasted_iota(jnp.int32, (TQ, LANE), 1)
        out = jnp.zeros((TQ, LANE), F32)
        for hh in range(2):
            m, acc = tile(hh, i, carry[hh], True)
            l = jnp.sum(jnp.where(lane == ONES_LANE[hh], acc, 0.0), axis=-1, keepdims=True)
            out = out + jnp.where((lane < V_DIM) == (hh == 0), acc, 0.0) / l
            lse_ref[hh] = jnp.broadcast_to(m * QK_SCALE2 + jnp.log(l) * LOG2E, (TQ, LANE))
        o_ref[...] = out

    return pl.pallas_call(
        body, name="attn_fwd", grid=(NH // 2, s // TQ),
        in_specs=[pl.BlockSpec((2, TQ, LANE), lambda p, i: (p, i, 0)), pl.BlockSpec((2, s, LANE), lambda p, i: (p, 0, 0)),
                  pl.BlockSpec((2, s, LANE), lambda p, i: (p, 0, 0))],
        out_specs=[pl.BlockSpec((TQ, LANE), lambda p, i: (i, p)), pl.BlockSpec((2, TQ, LANE), lambda p, i: (p, i, 0))],
        out_shape=[jax.ShapeDtypeStruct((s, NH * V_DIM), F32), jax.ShapeDtypeStruct((NH, s, LANE), F32)],
        compiler_params=_params(("parallel", "parallel")),
    )(q, k, vv)


def _lower_bound(lbl):
    a0, a1 = lbl[0:1, :], lbl[1:2, :]
    mx = jnp.maximum(a0, a1)
    e0, e1 = jnp.exp(a0 - mx), jnp.exp(a1 - mx)
    return e0 / (e0 + e1)


def _chunk_cumsum(v, reverse=False):
    pos = lax.broadcasted_iota(jnp.int32, v.shape, 0) % HG_BLOCK
    s = 1
    while s < HG_BLOCK:
        if reverse:
            v = v + jnp.where(pos < HG_BLOCK - s, pltpu.roll(v, TH - s, 0), 0.0)
        else:
            v = v + jnp.where(pos >= s, pltpu.roll(v, s, 0), 0.0)
        s *= 2
    return v


def _hgrn_gates(hq, hf, lb):
    sig = _sigmoid(hf)
    f = lb + (1.0 - lb) * sig
    g = jnp.log(f)
    kk = 1.0 - f
    r = lax.broadcasted_iota(jnp.int32, (TH, TH), 0)
    c = lax.broadcasted_iota(jnp.int32, (TH, TH), 1)
    tri = ((r // HG_BLOCK) == (c // HG_BLOCK)) & (r >= c)
    cum = _chunk_cumsum(g)
    nch = TH // HG_BLOCK
    total = _chunks(cum)[:, HG_BLOCK - 1:HG_BLOCK, :]
    lastb = jnp.broadcast_to(total, (nch, HG_BLOCK, LANE)).reshape(TH, LANE)
    e, ei, ee = jnp.exp(cum), jnp.exp(-cum), jnp.exp(lastb - cum)
    return dict(sig=sig, f=f, kk=kk, tri=tri, cum=cum, total=total, e=e, ei=ei, ee=ee, qd=hq * e, ki=kk * ei, ke=kk * ee)


def _chunks(v):
    return v.reshape(TH // HG_BLOCK, HG_BLOCK, v.shape[-1])


def _bmm_nt(a, b):
    return lax.dot_general(a, b, (((2,), (2,)), ((0,), (0,))), preferred_element_type=F32)


def _bmm_nn(a, b):
    return lax.dot_general(a, b, (((2,), (1,)), ((0,), (0,))), preferred_element_type=F32)


def _bmm_tn(a, b):
    return lax.dot_general(a, b, (((1,), (1,)), ((0,), (0,))), preferred_element_type=F32)


def _pair_masks():
    lane = lax.broadcasted_iota(jnp.int32, (TH, LANE), 1)
    kr = lax.broadcasted_iota(jnp.int32, (LANE, LANE), 0)
    kc = lax.broadcasted_iota(jnp.int32, (LANE, LANE), 1)
    return lane < 64, (kr // 64) == (kc // 64)


def _hgrn_fwd(proj, lbl):
    s = proj.shape[0]
    nch = TH // HG_BLOCK

    def body(hq_ref, hf_ref, hi_ref, lbl_ref, o_ref, st_ref, st):
        @pl.when(pl.program_id(1) == 0)
        def _():
            st[...] = jnp.zeros_like(st)

        m0, bd = _pair_masks()
        for u in range(HG_PAIRS):
            lanes = slice(u * LANE, (u + 1) * LANE)
            lb = _lower_bound(lbl_ref[:, lanes])
            gt = _hgrn_gates(hq_ref[:, lanes], hf_ref[:, lanes], lb)
            v_b = hi_ref[:, lanes].astype(BF16)
            qd, ki_b, ke_b = gt["qd"], gt["ki"].astype(BF16), gt["ke"].astype(BF16)
            qd_b = qd.astype(BF16)
            o = jnp.zeros((TH, LANE), F32)
            for hh in range(2):
                mh = m0 if hh == 0 else jnp.logical_not(m0)
                a = jnp.where(gt["tri"], _mm_nt(jnp.where(mh, qd, 0.0).astype(BF16), ki_b), 0.0)
                o = jnp.where(mh, _mm(a.astype(BF16), v_b), o)
            upd = _bmm_tn(_chunks(v_b), _chunks(ke_b))
            decay = jnp.exp(gt["total"])
            cur, entering = st[u], []
            for n in range(nch):
                entering.append(cur)
                cur = decay[n] * cur + jnp.where(bd, upd[n], 0.0)
            st[u] = cur
            entering = jnp.stack(entering)
            st_ref[u] = entering
            o_ref[:, lanes] = o + _bmm_nt(_chunks(qd_b), entering.astype(BF16)).reshape(TH, LANE)

    wide = HG_PAIRS * LANE
    col = lambda base: pl.BlockSpec((TH, wide), lambda p, i: (i, base // wide + p))
    return pl.pallas_call(
        body, name="hgrn_fwd", grid=(NH // 2 // HG_PAIRS, s // TH),
        in_specs=[col(P_HQ), col(P_HF), col(P_HI), pl.BlockSpec((2, wide), lambda p, i: (0, p))],
        out_specs=[pl.BlockSpec((TH, wide), lambda p, i: (i, p)),
                   pl.BlockSpec((HG_PAIRS, nch, LANE, LANE), lambda p, i: (p, i, 0, 0))],
        out_shape=[jax.ShapeDtypeStruct((s, 512), F32), jax.ShapeDtypeStruct((NH // 2, s // HG_BLOCK, LANE, LANE), F32)],
        scratch_shapes=[pltpu.VMEM((HG_PAIRS, LANE, LANE), F32)],
        compiler_params=_params(("parallel", "arbitrary")),
    )(proj, proj, proj, lbl)


def _group_sum(v):
    low = lax.broadcasted_iota(jnp.int32, (v.shape[0], LANE), 1) < V_DIM
    blocks = []
    for b in range(v.shape[1] // LANE):
        blk = v[:, b * LANE:(b + 1) * LANE]
        s_low = jnp.sum(jnp.where(low, blk, 0.0), axis=-1, keepdims=True)
        s_high = jnp.sum(jnp.where(low, 0.0, blk), axis=-1, keepdims=True)
        blocks.append(jnp.where(low, s_low, s_high))
    return jnp.concatenate(blocks, axis=1)


def _dsilu(z, sg):
    return sg * (1.0 + z * (1.0 - sg))


def _mid(proj, attn, o_raw, x, tgt, g_hg, b_gate, g_post, wa, wb, w_out):
    s = x.shape[0]

    def body(attn_ref, ga_ref, o_ref, gb_ref, mg_ref, x_ref, t_ref, ghg_ref, bg_ref, gp_ref, wa_ref, wb_ref, wo_ref,
             loss_ref, dout_ref, dattn_ref, dga_ref, dor_ref, dgb_ref, dmg_ref, dwo_ref, dwa_ref, dwb_ref, dgp_ref, dbg_ref, dghg_ref):
        first = pl.program_id(0) == 0

        @pl.when(first)
        def _():
            for rf in (loss_ref, dwo_ref, dwa_ref, dwb_ref, dgp_ref, dbg_ref, dghg_ref):
                rf[...] = jnp.zeros_like(rf)

        attn, za, orw, zb = attn_ref[...], ga_ref[...], o_ref[...], gb_ref[...]
        ghg, gp = ghg_ref[...], gp_ref[...]
        sga, sgb = _sigmoid(za), _sigmoid(zb)
        sa, sb = za * sga, zb * sgb
        ga = attn * sa
        rh = lax.rsqrt(_group_sum(orw * orw) * (1.0 / V_DIM) + EPS)
        on = (orw * rh) * ghg
        gb = on * sb
        ga_b, gb_b = ga.astype(BF16), gb.astype(BF16)
        ya = _mm(ga_b, wa_ref[...])
        yb = _mm(gb_b, wb_ref[...])
        gates = _sigmoid(mg_ref[...] + bg_ref[...])
        g0, g1 = gates[:, :D], gates[:, D:]
        m_b = (g0 * ya + g1 * yb).astype(BF16)
        y = _mm(m_b, wo_ref[...])
        ry = lax.rsqrt(jnp.mean(y * y, axis=-1, keepdims=True) + EPS)
        out = x_ref[...] + (y * ry) * gp
        err = out - t_ref[...]
        loss_ref[...] += 0.5 * jnp.sum(jnp.mean(err * err, axis=-1, keepdims=True), axis=0, keepdims=True)
        dout = err * (1.0 / D)
        dout_ref[...] = dout
        dgp_ref[...] += jnp.sum(dout * (y * ry), axis=0, keepdims=True)
        dgy = dout * gp
        dy = ry * dgy - y * (ry * ry * ry) * jnp.mean(y * dgy, axis=-1, keepdims=True)
        dy_b = dy.astype(BF16)
        dwo_ref[...] += _mm_tn(m_b, dy_b)
        dm = _mm_nt(dy_b, wo_ref[...])
        dya, dyb = dm * g0, dm * g1
        dg0, dg1 = dm * ya, dm * yb
        dmg = jnp.concatenate([dg0 * g0 * (1.0 - g0), dg1 * g1 * (1.0 - g1)], axis=1)
        dmg_ref[...] = dmg.astype(BF16)
        dbg_ref[...] += jnp.sum(dmg, axis=0, keepdims=True)
        dya_b, dyb_b = dya.astype(BF16), dyb.astype(BF16)
        dwa_ref[...] += _mm_tn(ga_b, dya_b)
        dwb_ref[...] += _mm_tn(gb_b, dyb_b)
        dga = _mm_nt(dya_b, wa_ref[...])
        dgb = _mm_nt(dyb_b, wb_ref[...])
        dattn_ref[...] = dga * sa
        dga_ref[...] = (dga * attn * _dsilu(za, sga)).astype(BF16)
        dgb_ref[...] = (dgb * on * _dsilu(zb, sgb)).astype(BF16)
        don = dgb * sb
        dghg_ref[...] += jnp.sum(don * (orw * rh), axis=0, keepdims=True)
        dgo = don * ghg
        dor_ref[...] = rh * dgo - orw * (rh * rh * rh) * (_group_sum(orw * dgo) * (1.0 / V_DIM))

    row = lambda w, j=0: pl.BlockSpec((TM_MID, w), lambda i: (i, j))
    full = lambda a: pl.BlockSpec(a.shape, lambda i: (0,) * a.ndim)
    acc = lambda shape: pl.BlockSpec(shape, lambda i: (0, 0))
    sds = jax.ShapeDtypeStruct
    return pl.pallas_call(
        body, name="mid", grid=(s // TM_MID,),
        in_specs=[row(512), row(512, P_GA // 512), row(512), row(512, P_GB // 512), row(2048, P_MERGE // 2048), row(D), row(D),
                  full(g_hg), full(b_gate), full(g_post), full(wa), full(wb), full(w_out)],
        out_specs=[acc((1, 1)), row(D), row(512), row(512), row(512), row(512), row(2048),
                   acc((D, D)), acc((512, D)), acc((512, D)), acc((1, D)), acc((1, 2048)), acc((1, 512))],
        out_shape=[sds((1, 1), F32), sds((s, D), F32), sds((s, 512), F32), sds((s, 512), BF16), sds((s, 512), F32), sds((s, 512), BF16),
                   sds((s, 2048), BF16), sds((D, D), F32), sds((512, D), F32), sds((512, D), F32), sds((1, D), F32),
                   sds((1, 2048), F32), sds((1, 512), F32)],
        compiler_params=_params(("arbitrary",)),
    )(attn, proj, o_raw, proj, proj, x, tgt, g_hg, b_gate, g_post, wa, wb, w_out)


def _attn_bwd(q, k, vv, attn, dattn, lse, token):
    s = q.shape[1]
    nt = s // TQ
    scale = 1.0 / math.sqrt(QK_NOPE + QK_ROPE)

    def body(q_ref, k_ref, v_ref, o_ref, do_ref, lse_ref, token_ref, dq_ref, dk_ref, dv_ref, do_s, delta_s):
        j = pl.program_id(1)

        @pl.when(j == 0)
        def _():
            dq_ref[...] = jnp.zeros_like(dq_ref)
            lane = lax.broadcasted_iota(jnp.int32, (TQ, LANE), 1)

            @pl.loop(0, nt)
            def _(i):
                rows = pl.ds(pl.multiple_of(i * TQ, TQ), TQ)
                do, o = do_ref[rows, :], o_ref[rows, :]
                for hh in range(2):
                    doh = jnp.where((lane < 64) if hh == 0 else (lane >= 64), do, 0.0)
                    do_s[hh, rows, :] = doh.astype(BF16)
                    delta_s[hh, rows, :] = jnp.broadcast_to(jnp.sum(doh * o, axis=-1, keepdims=True), (TQ, LANE))

        kjs, vjs = (k_ref[0], k_ref[1]), (v_ref[0], v_ref[1])

        def tile(hh, start, size, kj, vj, diag):
            rows = pl.ds(pl.multiple_of(start, size), size)
            wide = lambda a: jnp.concatenate([a] * (kj.shape[0] // LANE), axis=1)
            qi, do_b = q_ref[hh, rows, :], do_s[hh, rows, :]
            p = jnp.exp2(_mm_nt(qi, kj) * QK_SCALE2 - wide(lse_ref[hh, rows, :]))
            if diag:
                p = jnp.where(_diag_visible(size), p, 0.0)
            dv = _mm_tn(do_b, p.astype(BF16))
            ds_b = (p * (_mm_nt(do_b, vj) - wide(delta_s[hh, rows, :]))).astype(BF16)
            dk = _mm_tn(qi, ds_b)
            dq_ref[hh, rows, :] += _mm(ds_b, kj)
            return dk, dv

        def step(i, carry):
            new = [tile(hh, i * TQ, TQ, kjs[hh], vjs[hh], False) for hh in range(2)]
            return tuple((carry[hh][0] + new[hh][0], carry[hh][1] + new[hh][1]) for hh in range(2))

        def diagonal(hh):
            k0, k1, v0, v1 = kjs[hh][:HQ], kjs[hh][HQ:], vjs[hh][:HQ], vjs[hh][HQ:]
            a = tile(hh, j * TQ, HQ, k0, v0, True)
            b = tile(hh, j * TQ + HQ, HQ, k0, v0, False)
            c = tile(hh, j * TQ + HQ, HQ, k1, v1, True)
            return jnp.concatenate([a[0] + b[0], c[0]], axis=1), jnp.concatenate([a[1] + b[1], c[1]], axis=1)

        carry = lax.fori_loop(j + 1, nt, step, (diagonal(0), diagonal(1)))
        for hh in range(2):
            dk_ref[hh] = carry[hh][0].T * scale
            dv_ref[hh] = carry[hh][1].T

        @pl.when(j == nt - 1)
        def _():
            dq_ref[...] = dq_ref[...] * scale

    whole = pl.BlockSpec((2, s, LANE), lambda p, j: (p, 0, 0))
    tile_spec = pl.BlockSpec((2, TQ, LANE), lambda p, j: (p, j, 0))
    cols = pl.BlockSpec((s, LANE), lambda p, j: (0, p))
    hs = jax.ShapeDtypeStruct((NH, s, LANE), F32)
    return pl.pallas_call(
        body, name="attn_bwd", grid=(NH // 2, nt),
        in_specs=[whole, tile_spec, tile_spec, cols, cols, whole, pl.BlockSpec((8, LANE), lambda p, j: (0, 0))],
        out_specs=[whole, tile_spec, tile_spec],
        out_shape=[hs, hs, hs],
        scratch_shapes=[pltpu.VMEM((2, s, LANE), BF16), pltpu.VMEM((2, s, LANE), F32)],
        compiler_params=_params(("parallel", "arbitrary")),
    )(q, k, vv, attn, dattn, lse, token)


def _hgrn_bwd(proj, lbl, states, do_raw):
    s = proj.shape[0]
    nt = s // TH
    nch = TH // HG_BLOCK

    def body(hq_ref, hf_ref, hi_ref, lbl_ref, st_ref, do_ref, dh_ref, dlbl_ref, dst, dlb):
        step = pl.program_id(1)

        @pl.when(step == 0)
        def _():
            dst[...] = jnp.zeros_like(dst)
            dlb[...] = jnp.zeros_like(dlb)

        m0, bd = _pair_masks()
        for u in range(HG_PAIRS):
            lanes = slice(u * LANE, (u + 1) * LANE)
            lb = _lower_bound(lbl_ref[:, lanes])
            gt = _hgrn_gates(hq_ref[:, lanes], hf_ref[:, lanes], lb)
            do = do_ref[:, lanes]
            qd, ki, ke = gt["qd"], gt["ki"], gt["ke"]
            v_b, do_b = hi_ref[:, lanes].astype(BF16), do.astype(BF16)
            qd_b, ki_b, ke_b = qd.astype(BF16), ki.astype(BF16), ke.astype(BF16)
            dv = jnp.zeros((TH, LANE), F32)
            dqd = jnp.zeros((TH, LANE), F32)
            dki = jnp.zeros((TH, LANE), F32)
            for hh in range(2):
                mh = m0 if hh == 0 else jnp.logical_not(m0)
                a_b = jnp.where(gt["tri"], _mm_nt(jnp.where(mh, qd, 0.0).astype(BF16), ki_b), 0.0).astype(BF16)
                doh_b = jnp.where(mh, do, 0.0).astype(BF16)
                da_b = jnp.where(gt["tri"], _mm_nt(doh_b, v_b), 0.0).astype(BF16)
                dv = dv + _mm_tn(a_b, doh_b)
                dqd = jnp.where(mh, _mm(da_b, ki_b), dqd)
                dki = jnp.where(mh, _mm_tn(da_b, qd_b), dki)
            fed = _bmm_tn(_chunks(do_b), _chunks(qd_b))
            decay = jnp.exp(gt["total"])
            ds, leaving = dst[u], [None] * nch
            for n in reversed(range(nch)):
                leaving[n] = ds
                ds = decay[n] * ds + jnp.where(bd, fed[n], 0.0)
            dst[u] = ds
            leaving = jnp.stack(leaving)
            entering = st_ref[u]
            leaving_b = leaving.astype(BF16)
            dke3 = _bmm_nn(_chunks(v_b), leaving_b)
            dv = dv + _bmm_nt(_chunks(ke_b), leaving_b).reshape(TH, LANE)
            dqd = dqd + _bmm_nn(_chunks(do_b), entering.astype(BF16)).reshape(TH, LANE)
            dke = dke3.reshape(TH, LANE)
            dlast = (jnp.sum(dke3 * _chunks(ke), axis=1, keepdims=True)
                     + jnp.sum(leaving * entering, axis=1, keepdims=True) * decay)
            dk = dki * gt["ei"] + dke * gt["ee"]
            dcum = dqd * qd - dki * ki - dke * ke
            dg = _chunk_cumsum(dcum, reverse=True) + jnp.broadcast_to(dlast, (nch, HG_BLOCK, LANE)).reshape(TH, LANE)
            sig = gt["sig"]
            df = dg / gt["f"] - dk
            dlb[:, lanes] += jnp.sum(df * (1.0 - sig), axis=0, keepdims=True)
            dh_ref[0, :, lanes] = (dqd * gt["e"]).astype(BF16)
            dh_ref[1, :, lanes] = ((df * (1.0 - lb)) * sig * (1.0 - sig)).astype(BF16)
            dh_ref[2, :, lanes] = dv.astype(BF16)

        @pl.when(step == nt - 1)
        def _():
            lb = _lower_bound(lbl_ref[...])
            da0 = dlb[...] * lb * (1.0 - lb)
            dlbl_ref[...] = jnp.concatenate([da0, -da0], axis=0)

    wide = HG_PAIRS * LANE
    col = lambda base: pl.BlockSpec((TH, wide), lambda p, i: (nt - 1 - i, base // wide + p))
    tile = pl.BlockSpec((TH, wide), lambda p, i: (nt - 1 - i, p))
    sds = jax.ShapeDtypeStruct
    return pl.pallas_call(
        body, name="hgrn_bwd", grid=(NH // 2 // HG_PAIRS, nt),
        in_specs=[col(P_HQ), col(P_HF), col(P_HI), pl.BlockSpec((2, wide), lambda p, i: (0, p)),
                  pl.BlockSpec((HG_PAIRS, nch, LANE, LANE), lambda p, i: (p, nt - 1 - i, 0, 0)), tile],
        out_specs=[pl.BlockSpec((3, TH, wide), lambda p, i: (0, nt - 1 - i, p)), pl.BlockSpec((2, wide), lambda p, i: (0, p))],
        out_shape=[sds((3, s, 512), BF16), sds((2, 512), F32)],
        scratch_shapes=[pltpu.VMEM((HG_PAIRS, LANE, LANE), F32), pltpu.VMEM((1, wide), F32)],
        compiler_params=_params(("parallel", "arbitrary")),
    )(proj, proj, proj, lbl, states, do_raw)


def _norm_rows_bwd(v, r, g, dn):
    dgv = dn * g
    return r * dgv - v * (r * r * r) * jnp.mean(v * dgv, axis=-1, keepdims=True)


def _qkv_bwd(proj, dq, dk, dvv, g_q, g_kv, w_uq_p, w_k_p, w_v_p, rc, rs1, rs2):
    s = proj.shape[0]

    def body(cq_ref, ckv_ref, dq_ref, dk_ref, dv_ref, gq_ref, gkv_ref, wq_ref, wk_ref, wv_ref, c_ref, s1_ref, s2_ref,
             dcq_ref, dckv_ref, dkpe_ref, dwq_ref, dwk_ref, dwv_ref, dgq_ref, dgkv_ref):
        @pl.when(pl.program_id(0) == 0)
        def _():
            for rf in (dwq_ref, dwk_ref, dwv_ref, dgq_ref, dgkv_ref):
                rf[...] = jnp.zeros_like(rf)

        c, s1, s2 = c_ref[...], s1_ref[...], s2_ref[...]
        cq, ckv = cq_ref[...], ckv_ref[...]
        gq, gkv = gq_ref[...], gkv_ref[...]
        cqn, rq = _norm_rows(cq, gq)
        ckvn, rkv = _norm_rows(ckv, gkv)
        cqn_b, ckvn_b = cqn.astype(BF16), ckvn.astype(BF16)
        dqf = jnp.concatenate([_rope_t(dq_ref[h], c, s1, s2) for h in range(NH)], axis=1).astype(BF16)
        dkf = jnp.concatenate([dk_ref[h] for h in range(NH)], axis=1).astype(BF16)
        dvf = jnp.concatenate([dv_ref[h] for h in range(NH)], axis=1).astype(BF16)
        dkpe = dk_ref[0]
        for h in range(1, NH):
            dkpe = dkpe + dk_ref[h]
        lane = lax.broadcasted_iota(jnp.int32, (TM, LANE), 1)
        dkpe = jnp.where((lane >= QK_NOPE) & (lane < QK_NOPE + QK_ROPE), dkpe, 0.0)
        dkpe_ref[...] = _rope_t(dkpe, c, s1, s2).astype(BF16)
        dwq_ref[...] += _mm_tn(cqn_b, dqf)
        dwk_ref[...] += _mm_tn(ckvn_b, dkf)
        dwv_ref[...] += _mm_tn(ckvn_b, dvf)
        dcqn = _mm_nt(dqf, wq_ref[...])
        dckvn = _mm_nt(dkf, wk_ref[...]) + _mm_nt(dvf, wv_ref[...])
        dgq_ref[...] += jnp.sum(dcqn * (cq * rq), axis=0, keepdims=True)
        dgkv_ref[...] += jnp.sum(dckvn * (ckv * rkv), axis=0, keepdims=True)
        dcq_ref[...] = _norm_rows_bwd(cq, rq, gq, dcqn).astype(BF16)
        dckv_ref[...] = _norm_rows_bwd(ckv, rkv, gkv, dckvn).astype(BF16)

    row = lambda w, j=0: pl.BlockSpec((TM, w), lambda i: (i, j))
    full = lambda a: pl.BlockSpec(a.shape, lambda i: (0,) * a.ndim)
    acc = lambda shape: pl.BlockSpec(shape, lambda i: (0, 0))
    heads = pl.BlockSpec((NH, TM, LANE), lambda i: (0, i, 0))
    sds = jax.ShapeDtypeStruct
    return pl.pallas_call(
        body, name="qkv_bwd", grid=(s // TM,),
        in_specs=[row(Q_LORA, P_CQ // Q_LORA), row(KV_LORA, P_CKV // KV_LORA), heads, heads, heads,
                  full(g_q), full(g_kv), full(w_uq_p), full(w_k_p), full(w_v_p), row(LANE), row(LANE), row(LANE)],
        out_specs=[row(Q_LORA), row(KV_LORA), row(LANE), acc((Q_LORA, D)), acc((KV_LORA, D)), acc((KV_LORA, D)),
                   acc((1, Q_LORA)), acc((1, KV_LORA))],
        out_shape=[sds((s, Q_LORA), BF16), sds((s, KV_LORA), BF16), sds((s, LANE), BF16), sds((Q_LORA, D), F32),
                   sds((KV_LORA, D), F32), sds((KV_LORA, D), F32), sds((1, Q_LORA), F32), sds((1, KV_LORA), F32)],
        compiler_params=_params(("arbitrary",)),
    )(proj, proj, dq, dk, dvv, g_q, g_kv, w_uq_p, w_k_p, w_v_p, rc, rs1, rs2)


def _front_bwd(x, dout, dmg, dga, dh3, dgb, dcq, dckv, dkpe, g_pre, w_in_pt, token):
    s = x.shape[0]

    def body(x_ref, do_ref, dmg_ref, dga_ref, dh3_ref, dgb_ref, dcq_ref, dckv_ref, dkpe_ref, g_ref, w_ref, token_ref, gx_ref, dg_ref):
        @pl.when(pl.program_id(0) == 0)
        def _():
            dg_ref[...] = jnp.zeros_like(dg_ref)

        xv, g = x_ref[...], g_ref[...]
        _, r = _norm_rows(xv, g)
        pieces = ((dmg_ref[...], P_MERGE), (dga_ref[...], P_GA), (dh3_ref[0], P_HQ), (dh3_ref[1], P_HF), (dh3_ref[2], P_HI),
                  (dgb_ref[...], P_GB), (dcq_ref[...], P_CQ), (dckv_ref[...], P_CKV), (dkpe_ref[...], P_KPE))
        dh = jnp.zeros((TM, D), F32)
        for piece, off in pieces:
            dh = dh + _mm(piece, w_ref[off:off + piece.shape[1], :])
        dg_ref[...] += jnp.sum(dh * (xv * r), axis=0, keepdims=True)
        gx_ref[...] = do_ref[...] + _norm_rows_bwd(xv, r, g, dh)

    row = lambda w: pl.BlockSpec((TM, w), lambda i: (i, 0))
    full = lambda a: pl.BlockSpec(a.shape, lambda i: (0,) * a.ndim)
    sds = jax.ShapeDtypeStruct
    return pl.pallas_call(
        body, name="front_bwd", grid=(s // TM,),
        in_specs=[row(D), row(D), row(2048), row(512), pl.BlockSpec((3, TM, 512), lambda i: (0, i, 0)), row(512), row(Q_LORA),
                  row(KV_LORA), row(LANE), full(g_pre), full(w_in_pt), pl.BlockSpec(memory_space=pl.ANY)],
        out_specs=[row(D), pl.BlockSpec((1, D), lambda i: (0, 0))],
        out_shape=[sds((s, D), F32), sds((1, D), F32)],
        compiler_params=_params(("arbitrary",)),
    )(x, dout, dmg, dga, dh3, dgb, dcq, dckv, dkpe, g_pre, w_in_pt, token)


TK_GRAD = 1024


def _win_grad(h, pieces, name):
    s = h.shape[0]
    n = len(pieces)

    def body(h_ref, *refs):
        @pl.when(pl.program_id(0) == 0)
        def _():
            for o_ref in refs[n:]:
                o_ref[...] = jnp.zeros_like(o_ref)

        hv = h_ref[...]
        for d_ref, o_ref in zip(refs[:n], refs[n:]):
            if len(d_ref.shape) == 3:
                for k in range(d_ref.shape[0]):
                    o_ref[k] += _mm_tn(d_ref[k], hv)
            else:
                o_ref[...] += _mm_tn(d_ref[...], hv)

    def in_spec(p):
        if p.ndim == 3:
            return pl.BlockSpec((p.shape[0], TK_GRAD, p.shape[2]), lambda kk: (0, kk, 0))
        return pl.BlockSpec((TK_GRAD, p.shape[1]), lambda kk: (kk, 0))

    out_shapes = [(p.shape[0], p.shape[2], D) if p.ndim == 3 else (p.shape[1], D) for p in pieces]
    return pl.pallas_call(
        body, name=name, grid=(s // TK_GRAD,),
        in_specs=[pl.BlockSpec((TK_GRAD, D), lambda kk: (kk, 0))] + [in_spec(p) for p in pieces],
        out_specs=[pl.BlockSpec(sh, lambda kk, nd=len(sh): (0,) * nd) for sh in out_shapes],
        out_shape=[jax.ShapeDtypeStruct(sh, F32) for sh in out_shapes],
        compiler_params=_params(("arbitrary",)),
    )(h, *pieces)


def _pad_win_t(w_in_t):
    z = lambda n: jnp.zeros((n, w_in_t.shape[1]), w_in_t.dtype)
    sl = lambda o, n: w_in_t[o:o + n]
    return jnp.concatenate([sl(O_MERGE, 2048), sl(O_GA, 512), sl(O_HQ, 512), sl(O_HF, 512), sl(O_HI, 512), sl(O_GB, 512),
                            sl(O_CQ, Q_LORA), sl(O_CKV, KV_LORA), z(64), sl(O_KPE, QK_ROPE), z(32)], axis=0)


def _pad_wuq(w_uq):
    w = w_uq.reshape(Q_LORA, NH, QK_NOPE + QK_ROPE)
    return jnp.pad(w, ((0, 0), (0, 0), (0, LANE - QK_NOPE - QK_ROPE))).reshape(Q_LORA, NH * LANE)


def _unpad_wuq(g):
    return g.reshape(Q_LORA, NH, LANE)[:, :, :QK_NOPE + QK_ROPE].reshape(Q_LORA, NH * (QK_NOPE + QK_ROPE))


def _pad_wukv(w_ukv):
    w = w_ukv.reshape(KV_LORA, NH, QK_NOPE + V_DIM)
    w_k = jnp.pad(w[:, :, :QK_NOPE], ((0, 0), (0, 0), (0, LANE - QK_NOPE))).reshape(KV_LORA, NH * LANE)
    wv = w[:, :, QK_NOPE:].reshape(KV_LORA, NH // 2, 2, 1, V_DIM)
    eye = jnp.eye(2, dtype=w.dtype).reshape(1, 1, 2, 2, 1)
    return w_k, (wv * eye).reshape(KV_LORA, NH * LANE)


def _unpad_wukv(gk, gv):
    gk = gk.reshape(KV_LORA, NH, LANE)[:, :, :QK_NOPE]
    gv = gv.reshape(KV_LORA, NH // 2, 2, 2, V_DIM)
    gv = jnp.stack([gv[:, :, 0, 0], gv[:, :, 1, 1]], axis=2).reshape(KV_LORA, NH, V_DIM)
    return jnp.concatenate([gk, gv], axis=-1).reshape(KV_LORA, NH * (QK_NOPE + V_DIM))


def _local_step(x, tgt, g_pre, w_in_t, b_gate, g_q, g_kv, lb_logits, g_hgrn, g_post, weights, exchange=None):
    s = x.shape[0]
    w_in_p = _pad_win_t(w_in_t)
    rc, rs1, rs2 = _rope_tables(s)
    g_hg = jnp.tile(g_hgrn, (1, NH))

    proj, h = _front_fwd(x, g_pre, w_in_p, weights.tokens)
    w_uq, w_ukv = weights.qkv(h)
    w_uq_p = _pad_wuq(w_uq)
    w_k_p, w_v_p = _pad_wukv(w_ukv)
    q, k, vv = _qkv_fwd(proj, g_q, g_kv, w_uq_p, w_k_p, w_v_p, rc, rs1, rs2)
    attn, lse = _attn_fwd(q, k, vv)
    o_raw, states = _hgrn_fwd(proj, lb_logits)
    wa, wb, w_out = weights.mid(o_raw)
    (loss, dout, dattn, dga, dor, dgb, dmg, d_wout, d_wa, d_wb, d_gpost, d_bgate, d_ghg) = _mid(
        proj, attn, o_raw, x, tgt, g_hg, b_gate, g_post, wa, wb, w_out)
    w_mg, w_ga, w_gb = _win_grad(h, [dmg, dga, dgb], "win_grad_mid")
    dh3, d_lbl = _hgrn_bwd(proj, lb_logits, states, dor)
    (w_h3,) = _win_grad(h, [dh3], "win_grad_hgrn")
    d_win_rest = jnp.concatenate([w_ga, w_h3[0], w_h3[1], w_h3[2], w_gb, w_mg], axis=0)
    early = dict(w_in_rest=d_win_rest, w_branch_a=d_wa, w_branch_b=d_wb, w_out=d_wout)
    token = exchange.start_early(early) if exchange else jnp.zeros((8, LANE), F32)
    dq, dk, dvv = _attn_bwd(q, k, vv, attn, dattn, lse, token)
    dcq, dckv, dkpe, d_wuq_p, d_wk_p, d_wv_p, d_gq, d_gkv = _qkv_bwd(proj, dq, dk, dvv, g_q, g_kv, w_uq_p, w_k_p, w_v_p, rc, rs1, rs2)
    w_cq, w_ckv, w_kpe = _win_grad(h, [dcq, dckv, dkpe], "win_grad_qkv")
    d_win_qkv = jnp.concatenate([w_cq, w_ckv, w_kpe[64:64 + QK_ROPE]], axis=0)
    late = dict(w_in_qkv=d_win_qkv, w_uq=_unpad_wuq(d_wuq_p), w_ukv=_unpad_wukv(d_wk_p, d_wv_p))
    token = exchange.start_late(late) if exchange else jnp.zeros((8, LANE), F32)
    grad_x, d_gpre = _front_bwd(x, dout, dmg, dga, dh3, dgb, dcq, dckv, dkpe, g_pre, w_in_p, token)
    vec_grads = dict(g_pre=d_gpre, b_gate=d_bgate, g_q=d_gq, g_kv=d_gkv, lb_logits=d_lbl, g_hgrn=d_ghg, g_post=d_gpost)
    return loss, grad_x, dict(early, **late), vec_grads


SHARD_SHAPES = (("w_in", (1416, 1024)), ("w_uq", (192, 768)), ("w_ukv", (256, 256)), ("w_branch_a", (512, 256)),
                ("w_branch_b", (512, 256)), ("w_out", (256, 1024)))
BIG = tuple(n for n, _ in SHARD_SHAPES)
ROW_SHARDED = ("w_in", "w_uq", "w_out")
GATHER_SPLIT_AXIS = dict(w_in=1, w_uq=0, w_ukv=0, w_branch_a=0, w_branch_b=0, w_out=0)
N_CHIPS = 4
QKV_ROWS = Q_LORA + KV_LORA + QK_ROPE


def _to_block(name, a):
    return a[0].T if name == "w_in" else a[0]


def _from_block(name, a):
    return a.T[None] if name == "w_in" else a[None]
VEC_ROWS = (("g_pre", 0, 1024), ("b_gate", 1, 2048), ("g_q", 2, 768), ("g_kv", 3, 256), ("g_hgrn", 6, 64), ("g_post", 7, 1024))
VEC_LB_ROW = 4
VEC_SHAPE = (8, 2048)


def _split_by_chip(name, g):
    a, b = dict(SHARD_SHAPES)[name]
    return g.reshape(N_CHIPS, a, b) if name in ROW_SHARDED else g.reshape(a, N_CHIPS, b).transpose(1, 0, 2)


def _join_chips(name, w):
    a, b = dict(SHARD_SHAPES)[name]
    return w.reshape(N_CHIPS * a, b) if name in ROW_SHARDED else w.transpose(1, 0, 2).reshape(a, N_CHIPS * b)


MESH = pl.DeviceIdType.MESH
HBM = pl.BlockSpec(memory_space=pltpu.HBM)


def _mesh_place():
    x, y, c = lax.axis_index("x"), lax.axis_index("y"), lax.axis_index("c")
    return x, y, c, 2 * x + y, [(1 - x, y), (x, 1 - y), (1 - x, 1 - y)]


def _remote(src, dst, send_sems, recv_sems, k, to):
    return pltpu.make_async_remote_copy(src_ref=src, dst_ref=dst, send_sem=send_sems.at[k], recv_sem=recv_sems.at[k],
                                        device_id=to, device_id_type=MESH)


def _gather_weights(shards, split_axes):
    n = len(shards)

    def body(*refs):
        srcs, outs = refs[:n], refs[n:2 * n]
        ici_send, ici_recv, d2d_send, d2d_recv, local_sems = refs[2 * n:]
        x, y, c, me, chips = _mesh_place()
        sibling = (x, y, 1 - c)

        def half(ref, k, which):
            size = shards[k].shape[split_axes[k]] // 2
            part = pl.ds(pl.multiple_of(which * size, size), size)
            return ref.at[part] if split_axes[k] == 0 else ref.at[:, part]

        own = [pltpu.make_async_copy(srcs[k], outs[k].at[me], local_sems.at[k]) for k in range(n)]
        for cp in own:
            cp.start()
        started = []
        for k in range(n):
            for j, (px, py) in enumerate(chips):
                cp = _remote(half(srcs[k], k, c), half(outs[k].at[me], k, c), ici_send, ici_recv, 3 * k + j, (px, py, c))
                cp.start()
                started.append(cp)
        for k in range(n):
            for j, (px, py) in enumerate(chips):
                landed = half(outs[k].at[2 * px + py], k, c)
                _remote(landed, landed, ici_send, ici_recv, 3 * k + j, (px, py, c)).wait_recv()
                cp = _remote(landed, landed, d2d_send, d2d_recv, 3 * k + j, sibling)
                cp.start()
                started.append(cp)
        for k in range(n):
            for j, (px, py) in enumerate(chips):
                other = half(outs[k].at[2 * px + py], k, 1 - c)
                _remote(other, other, d2d_send, d2d_recv, 3 * k + j, sibling).wait_recv()
        for cp in started:
            cp.wait_send()
        for cp in own:
            cp.wait()

    sems = pltpu.SemaphoreType.DMA((3 * n,))
    return pl.pallas_call(
        body, name="gather_weights", in_specs=[HBM] * n, out_specs=[HBM] * n,
        out_shape=[jax.ShapeDtypeStruct((N_CHIPS,) + s.shape, s.dtype) for s in shards],
        scratch_shapes=[sems, sems, sems, sems, pltpu.SemaphoreType.DMA((n,))],
        compiler_params=pltpu.CompilerParams(has_side_effects=True),
    )(*shards)


def _chip_exchange(srcs, name):
    n = len(srcs)

    def body(*refs):
        src_refs, outs = refs[:n], refs[n:2 * n]
        send_sems, recv_sems, local_sems = refs[2 * n:]
        x, y, c, me, chips = _mesh_place()

        def slab(k, t):
            return src_refs[k] if srcs[k].ndim == 2 else src_refs[k].at[t]

        own = [pltpu.make_async_copy(slab(k, me), outs[k].at[me], local_sems.at[k]) for k in range(n)]
        for cp in own:
            cp.start()
        sends = []
        for k in range(n):
            for j, (px, py) in enumerate(chips):
                cp = _remote(slab(k, 2 * px + py), outs[k].at[me], send_sems, recv_sems, 3 * k + j, (px, py, c))
                cp.start()
                sends.append(cp)
        for k in range(n):
            for j, (px, py) in enumerate(chips):
                _remote(slab(k, me), outs[k].at[2 * px + py], send_sems, recv_sems, 3 * k + j, (px, py, c)).wait_recv()
        for cp in sends:
            cp.wait_send()
        for cp in own:
            cp.wait()

    sems = pltpu.SemaphoreType.DMA((3 * n,))
    return pl.pallas_call(
        body, name=name, in_specs=[HBM] * n, out_specs=[HBM] * n,
        out_shape=[jax.ShapeDtypeStruct((N_CHIPS,) + s.shape[-2:], s.dtype) for s in srcs],
        scratch_shapes=[sems, sems, pltpu.SemaphoreType.DMA((n,))],
        compiler_params=pltpu.CompilerParams(has_side_effects=True),
    )(*srcs)


def _sibling_exchange(srcs, name, after=None):
    n = len(srcs)
    extra = [] if after is None else [after]

    def body(*refs):
        src_refs, outs = refs[:n], refs[n + len(extra):2 * n + len(extra)]
        send_sems, recv_sems = refs[2 * n + len(extra):]
        sibling = (lax.axis_index("x"), lax.axis_index("y"), 1 - lax.axis_index("c"))
        copies = [_remote(src_refs[k], outs[k], send_sems, recv_sems, k, sibling) for k in range(n)]
        for cp in copies:
            cp.start()
        for cp in copies:
            cp.wait()

    sems = pltpu.SemaphoreType.DMA((n,))
    return pl.pallas_call(
        body, name=name, in_specs=[HBM] * n + [pl.BlockSpec(memory_space=pl.ANY)] * len(extra), out_specs=[HBM] * n,
        out_shape=[jax.ShapeDtypeStruct(s.shape, s.dtype) for s in srcs],
        scratch_shapes=[sems, sems],
        compiler_params=pltpu.CompilerParams(has_side_effects=True),
    )(*srcs, *extra)


SEM = pl.BlockSpec(memory_space=pltpu.SEMAPHORE)
DATAFLOW = pltpu.SideEffectType.DATAFLOW_SIDE_EFFECTING


def _exchange_copies(srcs, to_first, src_refs, land_refs, send_sems, recv_sems):
    x, y, c, me, chips = _mesh_place()
    n = len(srcs)
    sends, recvs = [], []
    for k in range(n):
        if k in to_first:
            base = 3 * n + 4 * to_first.index(k)
            sends.append((me != 0, pltpu.make_async_remote_copy(
                src_ref=src_refs[k], dst_ref=land_refs[k].at[me], send_sem=send_sems.at[base], recv_sem=recv_sems.at[base + me],
                device_id=(0, 0, c), device_id_type=MESH)))
            for s in range(1, N_CHIPS):
                recvs.append((me == 0, pltpu.make_async_remote_copy(
                    src_ref=src_refs[k], dst_ref=land_refs[k].at[s], send_sem=send_sems.at[base], recv_sem=recv_sems.at[base + s],
                    device_id=(s // 2, s % 2, c), device_id_type=MESH)))
        else:
            slab = (lambda t, k=k: src_refs[k]) if srcs[k].ndim == 2 else (lambda t, k=k: src_refs[k].at[t])
            for j, (px, py) in enumerate(chips):
                sends.append((None, _remote(slab(2 * px + py), land_refs[k].at[me], send_sems, recv_sems, 3 * k + j, (px, py, c))))
                recvs.append((None, _remote(slab(me), land_refs[k].at[2 * px + py], send_sems, recv_sems, 3 * k + j, (px, py, c))))
    return sends, recvs


def _when(pred, fn):
    if pred is None:
        fn()
    else:
        pl.when(pred)(fn)


def _exchange_start(srcs, to_first, name, after=None):
    n = len(srcs)
    n_sems = 3 * n + 4 * len(to_first)
    lands = [lax.empty((N_CHIPS,) + s.shape[-2:], s.dtype) for s in srcs]
    extra = [] if after is None else [after]

    def body(*refs):
        src_refs, land_refs = refs[:n], refs[n:2 * n]
        send_sems, recv_sems, token = refs[2 * n + len(extra)], refs[2 * n + len(extra) + 1], refs[-1]
        sends, _ = _exchange_copies(srcs, to_first, src_refs, land_refs, send_sems, recv_sems)
        for pred, cp in sends:
            _when(pred, cp.start)
        token[...] = jnp.zeros_like(token)

    hbm = lambda a: pltpu.HBM(a.shape, a.dtype)
    res = pl.pallas_call(
        body, name=name,
        out_shape=[pltpu.SemaphoreType.DMA((n_sems,)), pltpu.SemaphoreType.DMA((n_sems,))] + [hbm(a) for a in srcs + lands]
        + [jax.ShapeDtypeStruct((8, LANE), F32)],
        in_specs=[HBM] * (2 * n) + [pl.BlockSpec(memory_space=pl.ANY)] * len(extra),
        out_specs=[SEM, SEM] + [HBM] * (2 * n) + [pl.BlockSpec(memory_space=pltpu.VMEM)],
        input_output_aliases={i: 2 + i for i in range(2 * n)},
        compiler_params=pltpu.CompilerParams(has_side_effects=DATAFLOW),
    )(*[pltpu.with_memory_space_constraint(a, pltpu.HBM) for a in srcs + lands], *extra)
    return res[:-1], res[-1]


def _exchange_wait(srcs, to_first, started, after, name):
    n = len(srcs)
    send_sems, recv_sems, thru = started[0], started[1], started[2:]

    def body(*refs):
        src_refs, land_refs, send_ref, recv_ref = refs[:n], refs[n:2 * n], refs[2 * n], refs[2 * n + 1]
        sends, recvs = _exchange_copies(srcs, to_first, src_refs, land_refs, send_ref, recv_ref)
        for pred, cp in sends:
            _when(pred, cp.wait_send)
        for pred, cp in recvs:
            _when(pred, cp.wait_recv)

    res = pl.pallas_call(
        body, name=name, out_shape=[pltpu.HBM(a.shape, a.dtype) for a in thru],
        in_specs=[HBM] * (2 * n) + [SEM, SEM, pl.BlockSpec(memory_space=pl.ANY)], out_specs=[HBM] * (2 * n),
        input_output_aliases={i: i for i in range(2 * n)},
        compiler_params=pltpu.CompilerParams(has_side_effects=DATAFLOW),
    )(*thru, send_sems, recv_sems, after)
    return res[n:]


ROW_TILE = 256
COL_TILE = 256


def _block_tiling(a, b):
    if a <= ROW_TILE or a % ROW_TILE == 0:
        ta = min(a, ROW_TILE)
        return a // ta, (ta, b), lambda i: (i, 0)
    return b // COL_TILE, (a, COL_TILE), lambda i: (0, i)


def _sum_chips(parts, name):
    _, a, b = parts.shape
    steps, tile, at = _block_tiling(a, b)

    def body(p_ref, o_ref):
        f = lambda t: p_ref[t].astype(F32)
        o_ref[...] = ((f(0) + f(1)) + f(2)) + f(3)

    return pl.pallas_call(
        body, name=name, grid=(steps,),
        in_specs=[pl.BlockSpec((N_CHIPS,) + tile, lambda i: (0,) + at(i))],
        out_specs=pl.BlockSpec(tile, at),
        out_shape=jax.ShapeDtypeStruct((a, b), F32),
        compiler_params=_params(("parallel",)),
    )(parts)


def _sum_landed(land, own, name, first_land=None, first_own=None):
    _, a, b = land.shape
    steps, tile, at = _block_tiling(a, b)
    extra = first_land is not None

    def body(*refs):
        p_ref, own_ref, o_ref = refs[0], refs[1], refs[-1]
        me = 2 * lax.axis_index("x") + lax.axis_index("y")
        own = own_ref[...].astype(F32)
        slot = lambda t: jnp.where(me == t, own, p_ref[t].astype(F32))
        o_ref[...] = ((slot(0) + slot(1)) + slot(2)) + slot(3)
        if extra:
            fp_ref, fo_ref = refs[2], refs[3]
            r = fo_ref.shape[0]

            @pl.when(me == 0)
            def _():
                f = lambda t: fp_ref[t].astype(F32)
                rows = pl.ds(pl.multiple_of(lax.axis_index("c") * r, 8), r)
                o_ref[rows, :] += ((fo_ref[...].astype(F32) + f(1)) + f(2)) + f(3)

    in_specs = [pl.BlockSpec((N_CHIPS,) + tile, lambda i: (0,) + at(i)), pl.BlockSpec(tile, at)]
    args = [land, own]
    if extra:
        r = first_own.shape[0]
        assert tile[0] == a, "the extra rows need whole columns in a step"
        in_specs += [pl.BlockSpec((N_CHIPS, r, tile[1]), lambda i: (0,) + at(i)), pl.BlockSpec((r, tile[1]), at)]
        args += [first_land, first_own]
    return pl.pallas_call(
        body, name=name, grid=(steps,), in_specs=in_specs, out_specs=pl.BlockSpec(tile, at),
        out_shape=jax.ShapeDtypeStruct((a, b), F32), compiler_params=_params(("parallel",)),
    )(*args)


def _add_cast(a, b, name):
    def body(a_ref, b_ref, o_ref):
        o_ref[...] = (a_ref[...] + b_ref[...]).astype(BF16)

    return pl.pallas_call(body, name=name, out_shape=jax.ShapeDtypeStruct(a.shape, BF16),
                          compiler_params=_params(()))(a, b)


class _LaterWeights:
    QKV = ("w_uq", "w_ukv")
    MID = ("w_branch_a", "w_branch_b", "w_out")

    def __init__(self, blocks, after):
        self.blocks = blocks
        self.qkv_started, t1 = _exchange_start([blocks[n] for n in self.QKV], (), "weights_qkv_start", after)
        self.mid_started, t2 = _exchange_start([blocks[n] for n in self.MID], (), "weights_mid_start", after)
        self.tokens = [t1, t2]

    def _whole(self, names, started, after, name):
        landed = _exchange_wait([self.blocks[n] for n in names], (), started, after, name)
        me = 2 * lax.axis_index("x") + lax.axis_index("y")
        return [_join_chips(n, lax.dynamic_update_index_in_dim(land, self.blocks[n], me, 0)) for n, land in zip(names, landed)]

    def qkv(self, after):
        return self._whole(self.QKV, self.qkv_started, after, "weights_qkv_wait")

    def mid(self, after):
        return self._whole(self.MID, self.mid_started, after, "weights_mid_wait")


class _GradExchange:
    EARLY = ("w_in", "w_branch_a", "w_branch_b", "w_out")
    LATE = ("w_uq", "w_ukv")

    def __init__(self, state):
        self.state = state
        self.outs = {}

    @staticmethod
    def _own(slabs):
        return lax.dynamic_index_in_dim(slabs, 2 * lax.axis_index("x") + lax.axis_index("y"), axis=0, keepdims=False)

    def start_early(self, g):
        full = jnp.concatenate([jnp.zeros((QKV_ROWS, D), F32), g["w_in_rest"]], axis=0)
        g = dict(g, w_in=full)
        self.early = [_split_by_chip(n, g[n]).astype(BF16) for n in self.EARLY]
        self.early_started, token = _exchange_start(self.early, (), "grads_early_start")
        return token

    def start_late(self, g):
        self.early_landed = _exchange_wait(self.early, (), self.early_started, g["w_uq"], "grads_early_wait")
        half = QKV_ROWS // 2
        c = lax.axis_index("c")
        mine = lax.dynamic_slice_in_dim(g["w_in_qkv"], c * half, half, axis=0)
        (theirs,) = _sibling_exchange([lax.dynamic_slice_in_dim(g["w_in_qkv"], (1 - c) * half, half, axis=0)], "sibling_qkv_rows")
        self.late = [_split_by_chip(n, g[n]).astype(BF16) for n in self.LATE] + [_add_cast(mine, theirs, "add_qkv_rows")]
        self.late_started, token = _exchange_start(self.late, (2,), "grads_late_start")
        names = self.EARLY[1:]
        mine = [_sum_landed(land, self._own(slabs), "sum_" + n) for n, slabs, land in list(zip(self.EARLY, self.early, self.early_landed))[1:]]
        theirs = _sibling_exchange(mine, "sibling_early", after=token)
        for n, a, b in zip(names, mine, theirs):
            self.outs[n] = _adamw(a, b, *self.state[n], "adamw_" + n)
        return self.outs[names[-1]][0]

    def finish(self, after):
        late_landed = _exchange_wait(self.late, (2,), self.late_started, after, "grads_late_wait")
        sums = {"w_in": _sum_landed(self.early_landed[0], self._own(self.early[0]), "sum_w_in",
                                    first_land=late_landed[2], first_own=self.late[2])}
        for n, slabs, land in zip(self.LATE, self.late, late_landed):
            sums[n] = _sum_landed(land, self._own(slabs), "sum_" + n)
        return sums


def _adamw_math(g, w, m, v):
    nm = ADAM_B1 * m + (1.0 - ADAM_B1) * g
    nv = ADAM_B2 * v + (1.0 - ADAM_B2) * (g * g)
    m_hat = nm / (1.0 - ADAM_B1 ** ADAM_STEP)
    v_hat = nv / (1.0 - ADAM_B2 ** ADAM_STEP)
    return -ADAM_LR * (m_hat / (jnp.sqrt(v_hat) + ADAM_EPS) + ADAM_WD * w), nm, nv


def _adamw(p_mine, p_sibling, w, m, v, name):
    a, b = p_mine.shape
    steps, tile, at = _block_tiling(a, b)

    def body(a_ref, b_ref, w_ref, m_ref, v_ref, g_ref, d_ref, nm_ref, nv_ref):
        g = a_ref[...] + b_ref[...]
        g_ref[...] = g
        d_ref[...], nm_ref[...], nv_ref[...] = _adamw_math(g, w_ref[...], m_ref[...], v_ref[...])

    spec = pl.BlockSpec(tile, at)
    sds = jax.ShapeDtypeStruct((a, b), F32)
    return pl.pallas_call(
        body, name=name, grid=(steps,), in_specs=[spec] * 5, out_specs=[spec] * 4, out_shape=[sds] * 4,
        compiler_params=_params(("parallel",)),
    )(p_mine, p_sibling, w, m, v)


LOSS_AT = (2, 1024)


def _vec_pack(vg, loss):
    names = [n for n, _, _ in VEC_ROWS]

    def body(*refs):
        o_ref = refs[-1]
        lb_ref, loss_ref = refs[len(names)], refs[len(names) + 1]
        o_ref[...] = jnp.zeros_like(o_ref)
        o_ref[LOSS_AT[0]:LOSS_AT[0] + 1, LOSS_AT[1]:LOSS_AT[1] + LANE] = jnp.broadcast_to(loss_ref[...], (1, LANE))
        for (name, row, size), ref in zip(VEC_ROWS, refs):
            if name == "g_hgrn":
                r = lax.broadcasted_iota(jnp.int32, (NH * V_DIM, LANE), 0)
                c = lax.broadcasted_iota(jnp.int32, (NH * V_DIM, LANE), 1)
                fold = ((r % V_DIM) == c).astype(F32)
                o_ref[row:row + 1, 0:LANE] = jnp.dot(ref[...], fold, precision=HIGHEST, preferred_element_type=F32)
            else:
                o_ref[row:row + 1, 0:size] = ref[...]
        o_ref[VEC_LB_ROW:VEC_LB_ROW + 2, 0:512] = lb_ref[...]

    return pl.pallas_call(body, name="vec_pack", out_shape=jax.ShapeDtypeStruct(VEC_SHAPE, F32))(
        *[vg[n] for n in names], vg["lb_logits"], loss)


def _adamw_vec(p_mine, p_sibling, w, m, v):
    names = [n for n, _, _ in VEC_ROWS] + ["lb_logits"]
    k = len(names)

    def body(a_ref, b_ref, *refs):
        ins, outs = refs[:3 * k], refs[3 * k:]
        at = (slice(LOSS_AT[0], LOSS_AT[0] + 1), slice(LOSS_AT[1], LOSS_AT[1] + LANE))
        outs[-1][...] = a_ref[at] + b_ref[at]
        for i, name in enumerate(names):
            if name == "lb_logits":
                rows, cols = slice(VEC_LB_ROW, VEC_LB_ROW + 2), slice(0, 512)
            else:
                _, row, size = VEC_ROWS[i]
                rows, cols = slice(row, row + 1), slice(0, size)
            g = a_ref[rows, cols] + b_ref[rows, cols]
            d, nm, nv = _adamw_math(g, ins[i][...], ins[k + i][...], ins[2 * k + i][...])
            for o_ref, val in zip(outs[4 * i:4 * i + 4], (g, d, nm, nv)):
                o_ref[...] = val

    shapes = [jax.ShapeDtypeStruct(w[n].shape, F32) for n in names for _ in range(4)] + [jax.ShapeDtypeStruct((1, LANE), F32)]
    res = pl.pallas_call(body, name="adamw_vec", out_shape=shapes)(
        p_mine, p_sibling, *[w[n] for n in names], *[m[n] for n in names], *[v[n] for n in names])
    return [{n: res[4 * i + j] for i, n in enumerate(names)} for j in range(4)], res[-1]


WEIGHTS = ("g_pre", "w_in", "b_gate", "g_q", "w_uq", "g_kv", "w_ukv", "lb_logits", "g_hgrn", "w_branch_a", "w_branch_b", "w_out", "g_post")


def kernel(x, g_pre, w_in, b_gate, g_q, w_uq, g_kv, w_ukv, lb_logits, g_hgrn, w_branch_a, w_branch_b, w_out, g_post, loss_target, m_g_pre, m_w_in, m_b_gate, m_g_q, m_w_uq, m_g_kv, m_w_ukv, m_lb_logits, m_g_hgrn, m_w_branch_a, m_w_branch_b, m_w_out, m_g_post, v_g_pre, v_w_in, v_b_gate, v_g_q, v_w_uq, v_g_kv, v_w_ukv, v_lb_logits, v_g_hgrn, v_w_branch_a, v_w_branch_b, v_w_out, v_g_post):
    w = dict(g_pre=g_pre, w_in=w_in, b_gate=b_gate, g_q=g_q, w_uq=w_uq, g_kv=g_kv, w_ukv=w_ukv, lb_logits=lb_logits, g_hgrn=g_hgrn,
             w_branch_a=w_branch_a, w_branch_b=w_branch_b, w_out=w_out, g_post=g_post)
    m = dict(g_pre=m_g_pre, w_in=m_w_in, b_gate=m_b_gate, g_q=m_g_q, w_uq=m_w_uq, g_kv=m_g_kv, w_ukv=m_w_ukv, lb_logits=m_lb_logits,
             g_hgrn=m_g_hgrn, w_branch_a=m_w_branch_a, w_branch_b=m_w_branch_b, w_out=m_w_out, g_post=m_g_post)
    v = dict(g_pre=v_g_pre, w_in=v_w_in, b_gate=v_b_gate, g_q=v_g_q, w_uq=v_w_uq, g_kv=v_g_kv, w_ukv=v_w_ukv, lb_logits=v_lb_logits,
             g_hgrn=v_g_hgrn, w_branch_a=v_w_branch_a, w_branch_b=v_w_branch_b, w_out=v_w_out, g_post=v_g_post)
    blocks = {n: _to_block(n, w[n]).astype(BF16) for n in BIG}
    (w_in_all,) = _gather_weights([blocks["w_in"]], [GATHER_SPLIT_AXIS["w_in"]])
    weights = _LaterWeights(blocks, w_in_all)
    state = {n: [_to_block(n, t[n]) for t in (w, m, v)] for n in BIG}
    exchange = _GradExchange(state)
    loss, grad_x, _, vec_grads = _local_step(
        x[0], loss_target[0], g_pre, _join_chips("w_in", w_in_all), b_gate, g_q, g_kv, lb_logits, g_hgrn, g_post, weights, exchange)
    sums = exchange.finish(grad_x)
    (vec_landed,) = _chip_exchange([_vec_pack(vec_grads, loss)], "scatter_vec")
    rest = tuple(sums)
    mine = [sums[n] for n in rest] + [_sum_chips(vec_landed, "sum_vec")]
    theirs = _sibling_exchange(mine, "sibling_grads")
    done = dict(exchange.outs)
    for k, n in enumerate(rest):
        done[n] = _adamw(mine[k], theirs[k], *state[n], "adamw_" + n)
    outs = [{}, {}, {}, {}]
    for n in BIG:
        for o, val in zip(outs, done[n]):
            o[n] = _from_block(n, val)
    vec_outs, total = _adamw_vec(mine[-1], theirs[-1], w, m, v)
    for o, vals in zip(outs, vec_outs):
        o.update(vals)
    return (total[0, 0], grad_x[None], *[o[n] for o in outs for n in WEIGHTS])
```

```python
import functools
import math

import numpy as np
import jax
import jax.numpy as jnp
from jax import lax
from jax.experimental import pallas as pl
from jax.experimental.pallas import tpu as pltpu

F32 = jnp.float32
BF16 = jnp.bfloat16
HIGHEST = lax.Precision.HIGHEST

D = 1024
NH = 8
QK_NOPE, QK_ROPE, V_DIM = 64, 32, 64
Q_LORA, KV_LORA = 768, 256
CHUNK = 64
HG_BLOCK = 32
EPS = 1e-6
D_IN = 5664
LANE = 128
P_MERGE, P_GA, P_HQ, P_HF, P_HI, P_GB, P_CQ, P_CKV, P_KPE = 0, 2048, 2560, 3072, 3584, 4096, 4608, 5376, 5632
D_P = 5760
O_CQ, O_CKV, O_KPE, O_GA, O_HQ, O_HF, O_HI, O_GB, O_MERGE = 0, 768, 1024, 1056, 1568, 2080, 2592, 3104, 3616

TM = 512
TM_MID = 256
TQ = 1024
ONES_LANE = (LANE - 1, 0)
TH = 256
HG_PAIRS = 4
VMEM_LIMIT = 56 * 1024 * 1024

ADAM_LR, ADAM_B1, ADAM_B2, ADAM_EPS, ADAM_WD, ADAM_STEP = 0.001, 0.9, 0.999, 1e-08, 0.01, 10

NT_DIMS = (((1,), (1,)), ((), ()))
TN_DIMS = (((0,), (0,)), ((), ()))


def _params(sem):
    return pltpu.CompilerParams(dimension_semantics=sem, vmem_limit_bytes=VMEM_LIMIT)


def _mm(a, b):
    return jnp.dot(a, b, preferred_element_type=F32)


def _mm_nt(a, b):
    return lax.dot_general(a, b, NT_DIMS, preferred_element_type=F32)


def _mm_tn(a, b):
    return lax.dot_general(a, b, TN_DIMS, preferred_element_type=F32)


def _sigmoid(z):
    return jax.nn.sigmoid(z)


def _rope(v, c, s1, s2):
    return v * c + pltpu.roll(v, 112, 1) * s1 + pltpu.roll(v, 16, 1) * s2


def _rope_t(dy, c, s1, s2):
    return dy * c + pltpu.roll(dy * s1, 16, 1) + pltpu.roll(dy * s2, 112, 1)


def _rope_tables(s):
    inv = 10000.0 ** (-jnp.arange(0, QK_ROPE, 2, dtype=F32) / QK_ROPE)
    ang = jnp.arange(s, dtype=F32)[:, None] * inv[None, :]
    cos, sin = jnp.cos(ang), jnp.sin(ang)
    z64, z32, o64, o32 = jnp.zeros((s, 64), F32), jnp.zeros((s, 32), F32), jnp.ones((s, 64), F32), jnp.ones((s, 32), F32)
    z16 = jnp.zeros((s, 16), F32)
    c = jnp.concatenate([o64, cos, cos, o32], axis=1)
    s1 = jnp.concatenate([z64, -sin, z16, z32], axis=1)
    s2 = jnp.concatenate([z64, z16, sin, z32], axis=1)
    return c, s1, s2


def _front_fwd(x, g_pre, w_in_pt, tokens=()):
    s = x.shape[0]
    tokens = list(tokens)

    def body(x_ref, g_ref, w_ref, *refs):
        o_ref, h_ref = refs[len(tokens):]
        xv = x_ref[...]
        r = lax.rsqrt(jnp.mean(xv * xv, axis=-1, keepdims=True) + EPS)
        h = ((xv * r) * g_ref[...]).astype(BF16)
        h_ref[...] = h
        o_ref[...] = _mm_nt(h, w_ref[...])

    return pl.pallas_call(
        body, name="front_fwd", grid=(s // TM,),
        in_specs=[pl.BlockSpec((TM, D), lambda i: (i, 0)), pl.BlockSpec((1, D), lambda i: (0, 0)),
                  pl.BlockSpec((D_P, D), lambda i: (0, 0))] + [pl.BlockSpec((8, LANE), lambda i: (0, 0))] * len(tokens),
        out_specs=[pl.BlockSpec((TM, D_P), lambda i: (i, 0)), pl.BlockSpec((TM, D), lambda i: (i, 0))],
        out_shape=[jax.ShapeDtypeStruct((s, D_P), F32), jax.ShapeDtypeStruct((s, D), BF16)],
        compiler_params=_params(("parallel",)),
    )(x, g_pre, w_in_pt, *tokens)


def _norm_rows(v, g):
    r = lax.rsqrt(jnp.mean(v * v, axis=-1, keepdims=True) + EPS)
    return (v * r) * g, r


def _qkv_fwd(proj, g_q, g_kv, w_uq_p, w_k_p, w_v_p, rc, rs1, rs2):
    s = proj.shape[0]

    def body(cq_ref, ckv_ref, kpe_ref, gq_ref, gkv_ref, wq_ref, wk_ref, wv_ref, c_ref, s1_ref, s2_ref, q_ref, k_ref, v_ref):
        c, s1, s2 = c_ref[...], s1_ref[...], s2_ref[...]
        cqn, _ = _norm_rows(cq_ref[...], gq_ref[...])
        ckvn, _ = _norm_rows(ckv_ref[...], gkv_ref[...])
        ckvn = ckvn.astype(BF16)
        qf = _mm(cqn.astype(BF16), wq_ref[...])
        kf = _mm(ckvn, wk_ref[...])
        vf = _mm(ckvn, wv_ref[...])
        kpe = _rope(kpe_ref[...], c, s1, s2)
        lane = lax.broadcasted_iota(jnp.int32, (TM, LANE), 1)
        for h in range(NH):
            blk = slice(h * LANE, (h + 1) * LANE)
            q_ref[h] = _rope(qf[:, blk], c, s1, s2).astype(BF16)
            k_ref[h] = (kf[:, blk] + kpe).astype(BF16)
            v_ref[h] = jnp.where(lane == ONES_LANE[h % 2], 1.0, vf[:, blk]).astype(BF16)

    row = lambda w, j: pl.BlockSpec((TM, w), lambda i: (i, j))
    full = lambda a: pl.BlockSpec(a.shape, lambda i: (0,) * a.ndim)
    hs = jax.ShapeDtypeStruct((NH, s, LANE), BF16)
    return pl.pallas_call(
        body, name="qkv_fwd", grid=(s // TM,),
        in_specs=[row(Q_LORA, P_CQ // Q_LORA), row(KV_LORA, P_CKV // KV_LORA), row(LANE, P_KPE // LANE),
                  full(g_q), full(g_kv), full(w_uq_p), full(w_k_p), full(w_v_p), row(LANE, 0), row(LANE, 0), row(LANE, 0)],
        out_specs=[pl.BlockSpec((NH, TM, LANE), lambda i: (0, i, 0))] * 3,
        out_shape=[hs, hs, hs],
        compiler_params=_params(("parallel",)),
    )(proj, proj, proj, g_q, g_kv, w_uq_p, w_k_p, w_v_p, rc, rs1, rs2)


LOG2E = 1.4426950408889634
QK_SCALE2 = LOG2E / math.sqrt(QK_NOPE + QK_ROPE)


HQ = TQ // 2


def _diag_visible(n):
    row = lax.broadcasted_iota(jnp.int32, (n, n), 0)
    col = lax.broadcasted_iota(jnp.int32, (n, n), 1)
    return (col // CHUNK) <= (row // CHUNK)


def _attn_fwd(q, k, vv):
    s = q.shape[1]

    def body(q_ref, k_ref, v_ref, o_ref, lse_ref):
        i = pl.program_id(1)
        qs = (q_ref[0], q_ref[1])

        def tile(hh, t, carry, diag):
            m, acc = carry
            rows = pl.ds(pl.multiple_of(t * TQ, TQ), TQ)
            sc = _mm_nt(qs[hh], k_ref[hh, rows, :])
            if diag:
                sc = jnp.where(_diag_visible(TQ), sc, -jnp.inf)
            m_new = jnp.maximum(m, jnp.max(sc, axis=-1, keepdims=True))
            alpha = jnp.exp2((m - m_new) * QK_SCALE2)
            p = jnp.exp2((sc - m_new) * QK_SCALE2).astype(BF16)
            acc = alpha * acc + _mm(p, v_ref[hh, rows, :])
            return m_new, acc

        def step(t, carry):
            return tile(0, t, carry[0], False), tile(1, t, carry[1], False)

        init = (jnp.full((TQ, 1), -jnp.inf, F32), jnp.zeros((TQ, LANE), F32))
        carry = lax.fori_loop(0, i, step, (init, init))
        lane = lax.broadcasted_iota(jnp.int32, (TQ, LANE), 1)
        out = jnp.zeros((TQ, LANE), F32)
        for hh in range(2):
            m, acc = tile(hh, i, carry[hh], True)
            l = jnp.sum(jnp.where(lane == ONES_LANE[hh], acc, 0.0), axis=-1, keepdims=True)
            out = out + jnp.where((lane < V_DIM) == (hh == 0), acc, 0.0) / l
            lse_ref[hh] = jnp.broadcast_to(m * QK_SCALE2 + jnp.log(l) * LOG2E, (TQ, LANE))
        o_ref[...] = out

    return pl.pallas_call(
        body, name="attn_fwd", grid=(NH // 2, s // TQ),
        in_specs=[pl.BlockSpec((2, TQ, LANE), lambda p, i: (p, i, 0)), pl.BlockSpec((2, s, LANE), lambda p, i: (p, 0, 0)),
                  pl.BlockSpec((2, s, LANE), lambda p, i: (p, 0, 0))],
        out_specs=[pl.BlockSpec((TQ, LANE), lambda p, i: (i, p)), pl.BlockSpec((2, TQ, LANE), lambda p, i: (p, i, 0))],
        out_shape=[jax.ShapeDtypeStruct((s, NH * V_DIM), F32), jax.ShapeDtypeStruct((NH, s, LANE), F32)],
        compiler_params=_params(("parallel", "parallel")),
    )(q, k, vv)


def _lower_bound(lbl):
    a0, a1 = lbl[0:1, :], lbl[1:2, :]
    mx = jnp.maximum(a0, a1)
    e0, e1 = jnp.exp(a0 - mx), jnp.exp(a1 - mx)
    return e0 / (e0 + e1)


def _chunk_cumsum(v, reverse=False):
    pos = lax.broadcasted_iota(jnp.int32, v.shape, 0) % HG_BLOCK
    s = 1
    while s < HG_BLOCK:
        if reverse:
            v = v + jnp.where(pos < HG_BLOCK - s, pltpu.roll(v, TH - s, 0), 0.0)
        else:
            v = v + jnp.where(pos >= s, pltpu.roll(v, s, 0), 0.0)
        s *= 2
    return v


def _hgrn_gates(hq, hf, lb):
    sig = _sigmoid(hf)
    f = lb + (1.0 - lb) * sig
    g = jnp.log(f)
    kk = 1.0 - f
    r = lax.broadcasted_iota(jnp.int32, (TH, TH), 0)
    c = lax.broadcasted_iota(jnp.int32, (TH, TH), 1)
    tri = ((r // HG_BLOCK) == (c // HG_BLOCK)) & (r >= c)
    cum = _chunk_cumsum(g)
    nch = TH // HG_BLOCK
    total = _chunks(cum)[:, HG_BLOCK - 1:HG_BLOCK, :]
    lastb = jnp.broadcast_to(total, (nch, HG_BLOCK, LANE)).reshape(TH, LANE)
    e, ei, ee = jnp.exp(cum), jnp.exp(-cum), jnp.exp(lastb - cum)
    return dict(sig=sig, f=f, kk=kk, tri=tri, cum=cum, total=total, e=e, ei=ei, ee=ee, qd=hq * e, ki=kk * ei, ke=kk * ee)


def _chunks(v):
    return v.reshape(TH // HG_BLOCK, HG_BLOCK, v.shape[-1])


def _bmm_nt(a, b):
    return lax.dot_general(a, b, (((2,), (2,)), ((0,), (0,))), preferred_element_type=F32)


def _bmm_nn(a, b):
    return lax.dot_general(a, b, (((2,), (1,)), ((0,), (0,))), preferred_element_type=F32)


def _bmm_tn(a, b):
    return lax.dot_general(a, b, (((1,), (1,)), ((0,), (0,))), preferred_element_type=F32)


def _pair_masks():
    lane = lax.broadcasted_iota(jnp.int32, (TH, LANE), 1)
    kr = lax.broadcasted_iota(jnp.int32, (LANE, LANE), 0)
    kc = lax.broadcasted_iota(jnp.int32, (LANE, LANE), 1)
    return lane < 64, (kr // 64) == (kc // 64)


def _hgrn_fwd(proj, lbl):
    s = proj.shape[0]
    nch = TH // HG_BLOCK

    def body(hq_ref, hf_ref, hi_ref, lbl_ref, o_ref, st_ref, st):
        @pl.when(pl.program_id(1) == 0)
        def _():
            st[...] = jnp.zeros_like(st)

        m0, bd = _pair_masks()
        for u in range(HG_PAIRS):
            lanes = slice(u * LANE, (u + 1) * LANE)
            lb = _lower_bound(lbl_ref[:, lanes])
            gt = _hgrn_gates(hq_ref[:, lanes], hf_ref[:, lanes], lb)
            v_b = hi_ref[:, lanes].astype(BF16)
            qd, ki_b, ke_b = gt["qd"], gt["ki"].astype(BF16), gt["ke"].astype(BF16)
            qd_b = qd.astype(BF16)
            o = jnp.zeros((TH, LANE), F32)
            for hh in range(2):
                mh = m0 if hh == 0 else jnp.logical_not(m0)
                a = jnp.where(gt["tri"], _mm_nt(jnp.where(mh, qd, 0.0).astype(BF16), ki_b), 0.0)
                o = jnp.where(mh, _mm(a.astype(BF16), v_b), o)
            upd = _bmm_tn(_chunks(v_b), _chunks(ke_b))
            decay = jnp.exp(gt["total"])
            cur, entering = st[u], []
            for n in range(nch):
                entering.append(cur)
                cur = decay[n] * cur + jnp.where(bd, upd[n], 0.0)
            st[u] = cur
            entering = jnp.stack(entering)
            st_ref[u] = entering
            o_ref[:, lanes] = o + _bmm_nt(_chunks(qd_b), entering.astype(BF16)).reshape(TH, LANE)

    wide = HG_PAIRS * LANE
    col = lambda base: pl.BlockSpec((TH, wide), lambda p, i: (i, base // wide + p))
    return pl.pallas_call(
        body, name="hgrn_fwd", grid=(NH // 2 // HG_PAIRS, s // TH),
        in_specs=[col(P_HQ), col(P_HF), col(P_HI), pl.BlockSpec((2, wide), lambda p, i: (0, p))],
        out_specs=[pl.BlockSpec((TH, wide), lambda p, i: (i, p)),
                   pl.BlockSpec((HG_PAIRS, nch, LANE, LANE), lambda p, i: (p, i, 0, 0))],
        out_shape=[jax.ShapeDtypeStruct((s, 512), F32), jax.ShapeDtypeStruct((NH // 2, s // HG_BLOCK, LANE, LANE), F32)],
        scratch_shapes=[pltpu.VMEM((HG_PAIRS, LANE, LANE), F32)],
        compiler_params=_params(("parallel", "arbitrary")),
    )(proj, proj, proj, lbl)


def _group_sum(v):
    low = lax.broadcasted_iota(jnp.int32, (v.shape[0], LANE), 1) < V_DIM
    blocks = []
    for b in range(v.shape[1] // LANE):
        blk = v[:, b * LANE:(b + 1) * LANE]
        s_low = jnp.sum(jnp.where(low, blk, 0.0), axis=-1, keepdims=True)
        s_high = jnp.sum(jnp.where(low, 0.0, blk), axis=-1, keepdims=True)
        blocks.append(jnp.where(low, s_low, s_high))
    return jnp.concatenate(blocks, axis=1)


def _dsilu(z, sg):
    return sg * (1.0 + z * (1.0 - sg))


def _mid(proj, attn, o_raw, x, tgt, g_hg, b_gate, g_post, wa, wb, w_out):
    s = x.shape[0]

    def body(attn_ref, ga_ref, o_ref, gb_ref, mg_ref, x_ref, t_ref, ghg_ref, bg_ref, gp_ref, wa_ref, wb_ref, wo_ref,
             loss_ref, dout_ref, dattn_ref, dga_ref, dor_ref, dgb_ref, dmg_ref, dwo_ref, dwa_ref, dwb_ref, dgp_ref, dbg_ref, dghg_ref):
        first = pl.program_id(0) == 0

        @pl.when(first)
        def _():
            for rf in (loss_ref, dwo_ref, dwa_ref, dwb_ref, dgp_ref, dbg_ref, dghg_ref):
                rf[...] = jnp.zeros_like(rf)

        attn, za, orw, zb = attn_ref[...], ga_ref[...], o_ref[...], gb_ref[...]
        ghg, gp = ghg_ref[...], gp_ref[...]
        sga, sgb = _sigmoid(za), _sigmoid(zb)
        sa, sb = za * sga, zb * sgb
        ga = attn * sa
        rh = lax.rsqrt(_group_sum(orw * orw) * (1.0 / V_DIM) + EPS)
        on = (orw * rh) * ghg
        gb = on * sb
        ga_b, gb_b = ga.astype(BF16), gb.astype(BF16)
        ya = _mm(ga_b, wa_ref[...])
        yb = _mm(gb_b, wb_ref[...])
        gates = _sigmoid(mg_ref[...] + bg_ref[...])
        g0, g1 = gates[:, :D], gates[:, D:]
        m_b = (g0 * ya + g1 * yb).astype(BF16)
        y = _mm(m_b, wo_ref[...])
        ry = lax.rsqrt(jnp.mean(y * y, axis=-1, keepdims=True) + EPS)
        out = x_ref[...] + (y * ry) * gp
        err = out - t_ref[...]
        loss_ref[...] += 0.5 * jnp.sum(jnp.mean(err * err, axis=-1, keepdims=True), axis=0, keepdims=True)
        dout = err * (1.0 / D)
        dout_ref[...] = dout
        dgp_ref[...] += jnp.sum(dout * (y * ry), axis=0, keepdims=True)
        dgy = dout * gp
        dy = ry * dgy - y * (ry * ry * ry) * jnp.mean(y * dgy, axis=-1, keepdims=True)
        dy_b = dy.astype(BF16)
        dwo_ref[...] += _mm_tn(m_b, dy_b)
        dm = _mm_nt(dy_b, wo_ref[...])
        dya, dyb = dm * g0, dm * g1
        dg0, dg1 = dm * ya, dm * yb
        dmg = jnp.concatenate([dg0 * g0 * (1.0 - g0), dg1 * g1 * (1.0 - g1)], axis=1)
        dmg_ref[...] = dmg.astype(BF16)
        dbg_ref[...] += jnp.sum(dmg, axis=0, keepdims=True)
        dya_b, dyb_b = dya.astype(BF16), dyb.astype(BF16)
        dwa_ref[...] += _mm_tn(ga_b, dya_b)
        dwb_ref[...] += _mm_tn(gb_b, dyb_b)
        dga = _mm_nt(dya_b, wa_ref[...])
        dgb = _mm_nt(dyb_b, wb_ref[...])
        dattn_ref[...] = dga * sa
        dga_ref[...] = (dga * attn * _dsilu(za, sga)).astype(BF16)
        dgb_ref[...] = (dgb * on * _dsilu(zb, sgb)).astype(BF16)
        don = dgb * sb
        dghg_ref[...] += jnp.sum(don * (orw * rh), axis=0, keepdims=True)
        dgo = don * ghg
        dor_ref[...] = rh * dgo - orw * (rh * rh * rh) * (_group_sum(orw * dgo) * (1.0 / V_DIM))

    row = lambda w, j=0: pl.BlockSpec((TM_MID, w), lambda i: (i, j))
    full = lambda a: pl.BlockSpec(a.shape, lambda i: (0,) * a.ndim)
    acc = lambda shape: pl.BlockSpec(shape, lambda i: (0, 0))
    sds = jax.ShapeDtypeStruct
    return pl.pallas_call(
        body, name="mid", grid=(s // TM_MID,),
        in_specs=[row(512), row(512, P_GA // 512), row(512), row(512, P_GB // 512), row(2048, P_MERGE // 2048), row(D), row(D),
                  full(g_hg), full(b_gate), full(g_post), full(wa), full(wb), full(w_out)],
        out_specs=[acc((1, 1)), row(D), row(512), row(512), row(512), row(512), row(2048),
                   acc((D, D)), acc((512, D)), acc((512, D)), acc((1, D)), acc((1, 2048)), acc((1, 512))],
        out_shape=[sds((1, 1), F32), sds((s, D), F32), sds((s, 512), F32), sds((s, 512), BF16), sds((s, 512), F32), sds((s, 512), BF16),
                   sds((s, 2048), BF16), sds((D, D), F32), sds((512, D), F32), sds((512, D), F32), sds((1, D), F32),
                   sds((1, 2048), F32), sds((1, 512), F32)],
        compiler_params=_params(("arbitrary",)),
    )(attn, proj, o_raw, proj, proj, x, tgt, g_hg, b_gate, g_post, wa, wb, w_out)


def _attn_bwd(q, k, vv, attn, dattn, lse, token):
    s = q.shape[1]
    nt = s // TQ
    scale = 1.0 / math.sqrt(QK_NOPE + QK_ROPE)

    def body(q_ref, k_ref, v_ref, o_ref, do_ref, lse_ref, token_ref, dq_ref, dk_ref, dv_ref, do_s, delta_s):
        j = pl.program_id(1)

        @pl.when(j == 0)
        def _():
            dq_ref[...] = jnp.zeros_like(dq_ref)
            lane = lax.broadcasted_iota(jnp.int32, (TQ, LANE), 1)

            @pl.loop(0, nt)
            def _(i):
                rows = pl.ds(pl.multiple_of(i * TQ, TQ), TQ)
                do, o = do_ref[rows, :], o_ref[rows, :]
                for hh in range(2):
                    doh = jnp.where((lane < 64) if hh == 0 else (lane >= 64), do, 0.0)
                    do_s[hh, rows, :] = doh.astype(BF16)
                    delta_s[hh, rows, :] = jnp.broadcast_to(jnp.sum(doh * o, axis=-1, keepdims=True), (TQ, LANE))

        kjs, vjs = (k_ref[0], k_ref[1]), (v_ref[0], v_ref[1])

        def tile(hh, start, size, kj, vj, diag):
            rows = pl.ds(pl.multiple_of(start, size), size)
            wide = lambda a: jnp.concatenate([a] * (kj.shape[0] // LANE), axis=1)
            qi, do_b = q_ref[hh, rows, :], do_s[hh, rows, :]
            p = jnp.exp2(_mm_nt(qi, kj) * QK_SCALE2 - wide(lse_ref[hh, rows, :]))
            if diag:
                p = jnp.where(_diag_visible(size), p, 0.0)
            dv = _mm_tn(do_b, p.astype(BF16))
            ds_b = (p * (_mm_nt(do_b, vj) - wide(delta_s[hh, rows, :]))).astype(BF16)
            dk = _mm_tn(qi, ds_b)
            dq_ref[hh, rows, :] += _mm(ds_b, kj)
            return dk, dv

        def step(i, carry):
            new = [tile(hh, i * TQ, TQ, kjs[hh], vjs[hh], False) for hh in range(2)]
            return tuple((carry[hh][0] + new[hh][0], carry[hh][1] + new[hh][1]) for hh in range(2))

        def diagonal(hh):
            k0, k1, v0, v1 = kjs[hh][:HQ], kjs[hh][HQ:], vjs[hh][:HQ], vjs[hh][HQ:]
            a = tile(hh, j * TQ, HQ, k0, v0, True)
            b = tile(hh, j * TQ + HQ, HQ, k0, v0, False)
            c = tile(hh, j * TQ + HQ, HQ, k1, v1, True)
            return jnp.concatenate([a[0] + b[0], c[0]], axis=1), jnp.concatenate([a[1] + b[1], c[1]], axis=1)

        carry = lax.fori_loop(j + 1, nt, step, (diagonal(0), diagonal(1)))
        for hh in range(2):
            dk_ref[hh] = carry[hh][0].T * scale
            dv_ref[hh] = carry[hh][1].T

        @pl.when(j == nt - 1)
        def _():
            dq_ref[...] = dq_ref[...] * scale

    whole = pl.BlockSpec((2, s, LANE), lambda p, j: (p, 0, 0))
    tile_spec = pl.BlockSpec((2, TQ, LANE), lambda p, j: (p, j, 0))
    cols = pl.BlockSpec((s, LANE), lambda p, j: (0, p))
    hs = jax.ShapeDtypeStruct((NH, s, LANE), F32)
    return pl.pallas_call(
        body, name="attn_bwd", grid=(NH // 2, nt),
        in_specs=[whole, tile_spec, tile_spec, cols, cols, whole, pl.BlockSpec((8, LANE), lambda p, j: (0, 0))],
        out_specs=[whole, tile_spec, tile_spec],
        out_shape=[hs, hs, hs],
        scratch_shapes=[pltpu.VMEM((2, s, LANE), BF16), pltpu.VMEM((2, s, LANE), F32)],
        compiler_params=_params(("parallel", "arbitrary")),
    )(q, k, vv, attn, dattn, lse, token)


def _hgrn_bwd(proj, lbl, states, do_raw):
    s = proj.shape[0]
    nt = s // TH
    nch = TH // HG_BLOCK

    def body(hq_ref, hf_ref, hi_ref, lbl_ref, st_ref, do_ref, dh_ref, dlbl_ref, dst, dlb):
        step = pl.program_id(1)

        @pl.when(step == 0)
        def _():
            dst[...] = jnp.zeros_like(dst)
            dlb[...] = jnp.zeros_like(dlb)

        m0, bd = _pair_masks()
        for u in range(HG_PAIRS):
            lanes = slice(u * LANE, (u + 1) * LANE)
            lb = _lower_bound(lbl_ref[:, lanes])
            gt = _hgrn_gates(hq_ref[:, lanes], hf_ref[:, lanes], lb)
            do = do_ref[:, lanes]
            qd, ki, ke = gt["qd"], gt["ki"], gt["ke"]
            v_b, do_b = hi_ref[:, lanes].astype(BF16), do.astype(BF16)
            qd_b, ki_b, ke_b = qd.astype(BF16), ki.astype(BF16), ke.astype(BF16)
            dv = jnp.zeros((TH, LANE), F32)
            dqd = jnp.zeros((TH, LANE), F32)
            dki = jnp.zeros((TH, LANE), F32)
            for hh in range(2):
                mh = m0 if hh == 0 else jnp.logical_not(m0)
                a_b = jnp.where(gt["tri"], _mm_nt(jnp.where(mh, qd, 0.0).astype(BF16), ki_b), 0.0).astype(BF16)
                doh_b = jnp.where(mh, do, 0.0).astype(BF16)
                da_b = jnp.where(gt["tri"], _mm_nt(doh_b, v_b), 0.0).astype(BF16)
                dv = dv + _mm_tn(a_b, doh_b)
                dqd = jnp.where(mh, _mm(da_b, ki_b), dqd)
                dki = jnp.where(mh, _mm_tn(da_b, qd_b), dki)
            fed = _bmm_tn(_chunks(do_b), _chunks(qd_b))
            decay = jnp.exp(gt["total"])
            ds, leaving = dst[u], [None] * nch
            for n in reversed(range(nch)):
                leaving[n] = ds
                ds = decay[n] * ds + jnp.where(bd, fed[n], 0.0)
            dst[u] = ds
            leaving = jnp.stack(leaving)
            entering = st_ref[u]
            leaving_b = leaving.astype(BF16)
            dke3 = _bmm_nn(_chunks(v_b), leaving_b)
            dv = dv + _bmm_nt(_chunks(ke_b), leaving_b).reshape(TH, LANE)
            dqd = dqd + _bmm_nn(_chunks(do_b), entering.astype(BF16)).reshape(TH, LANE)
            dke = dke3.reshape(TH, LANE)
            dlast = (jnp.sum(dke3 * _chunks(ke), axis=1, keepdims=True)
                     + jnp.sum(leaving * entering, axis=1, keepdims=True) * decay)
            dk = dki * gt["ei"] + dke * gt["ee"]
            dcum = dqd * qd - dki * ki - dke * ke
            dg = _chunk_cumsum(dcum, reverse=True) + jnp.broadcast_to(dlast, (nch, HG_BLOCK, LANE)).reshape(TH, LANE)
            sig = gt["sig"]
            df = dg / gt["f"] - dk
            dlb[:, lanes] += jnp.sum(df * (1.0 - sig), axis=0, keepdims=True)
            dh_ref[0, :, lanes] = (dqd * gt["e"]).astype(BF16)
            dh_ref[1, :, lanes] = ((df * (1.0 - lb)) * sig * (1.0 - sig)).astype(BF16)
            dh_ref[2, :, lanes] = dv.astype(BF16)

        @pl.when(step == nt - 1)
        def _():
            lb = _lower_bound(lbl_ref[...])
            da0 = dlb[...] * lb * (1.0 - lb)
            dlbl_ref[...] = jnp.concatenate([da0, -da0], axis=0)

    wide = HG_PAIRS * LANE
    col = lambda base: pl.BlockSpec((TH, wide), lambda p, i: (nt - 1 - i, base // wide + p))
    tile = pl.BlockSpec((TH, wide), lambda p, i: (nt - 1 - i, p))
    sds = jax.ShapeDtypeStruct
    return pl.pallas_call(
        body, name="hgrn_bwd", grid=(NH // 2 // HG_PAIRS, nt),
        in_specs=[col(P_HQ), col(P_HF), col(P_HI), pl.BlockSpec((2, wide), lambda p, i: (0, p)),
                  pl.BlockSpec((HG_PAIRS, nch, LANE, LANE), lambda p, i: (p, nt - 1 - i, 0, 0)), tile],
        out_specs=[pl.BlockSpec((3, TH, wide), lambda p, i: (0, nt - 1 - i, p)), pl.BlockSpec((2, wide), lambda p, i: (0, p))],
        out_shape=[sds((3, s, 512), BF16), sds((2, 512), F32)],
        scratch_shapes=[pltpu.VMEM((HG_PAIRS, LANE, LANE), F32), pltpu.VMEM((1, wide), F32)],
        compiler_params=_params(("parallel", "arbitrary")),
    )(proj, proj, proj, lbl, states, do_raw)


def _norm_rows_bwd(v, r, g, dn):
    dgv = dn * g
    return r * dgv - v * (r * r * r) * jnp.mean(v * dgv, axis=-1, keepdims=True)


def _qkv_bwd(proj, dq, dk, dvv, g_q, g_kv, w_uq_p, w_k_p, w_v_p, rc, rs1, rs2):
    s = proj.shape[0]

    def body(cq_ref, ckv_ref, dq_ref, dk_ref, dv_ref, gq_ref, gkv_ref, wq_ref, wk_ref, wv_ref, c_ref, s1_ref, s2_ref,
             dcq_ref, dckv_ref, dkpe_ref, dwq_ref, dwk_ref, dwv_ref, dgq_ref, dgkv_ref):
        @pl.when(pl.program_id(0) == 0)
        def _():
            for rf in (dwq_ref, dwk_ref, dwv_ref, dgq_ref, dgkv_ref):
                rf[...] = jnp.zeros_like(rf)

        c, s1, s2 = c_ref[...], s1_ref[...], s2_ref[...]
        cq, ckv = cq_ref[...], ckv_ref[...]
        gq, gkv = gq_ref[...], gkv_ref[...]
        cqn, rq = _norm_rows(cq, gq)
        ckvn, rkv = _norm_rows(ckv, gkv)
        cqn_b, ckvn_b = cqn.astype(BF16), ckvn.astype(BF16)
        dqf = jnp.concatenate([_rope_t(dq_ref[h], c, s1, s2) for h in range(NH)], axis=1).astype(BF16)
        dkf = jnp.concatenate([dk_ref[h] for h in range(NH)], axis=1).astype(BF16)
        dvf = jnp.concatenate([dv_ref[h] for h in range(NH)], axis=1).astype(BF16)
        dkpe = dk_ref[0]
        for h in range(1, NH):
            dkpe = dkpe + dk_ref[h]
        lane = lax.broadcasted_iota(jnp.int32, (TM, LANE), 1)
        dkpe = jnp.where((lane >= QK_NOPE) & (lane < QK_NOPE + QK_ROPE), dkpe, 0.0)
        dkpe_ref[...] = _rope_t(dkpe, c, s1, s2).astype(BF16)
        dwq_ref[...] += _mm_tn(cqn_b, dqf)
        dwk_ref[...] += _mm_tn(ckvn_b, dkf)
        dwv_ref[...] += _mm_tn(ckvn_b, dvf)
        dcqn = _mm_nt(dqf, wq_ref[...])
        dckvn = _mm_nt(dkf, wk_ref[...]) + _mm_nt(dvf, wv_ref[...])
        dgq_ref[...] += jnp.sum(dcqn * (cq * rq), axis=0, keepdims=True)
        dgkv_ref[...] += jnp.sum(dckvn * (ckv * rkv), axis=0, keepdims=True)
        dcq_ref[...] = _norm_rows_bwd(cq, rq, gq, dcqn).astype(BF16)
        dckv_ref[...] = _norm_rows_bwd(ckv, rkv, gkv, dckvn).astype(BF16)

    row = lambda w, j=0: pl.BlockSpec((TM, w), lambda i: (i, j))
    full = lambda a: pl.BlockSpec(a.shape, lambda i: (0,) * a.ndim)
    acc = lambda shape: pl.BlockSpec(shape, lambda i: (0, 0))
    heads = pl.BlockSpec((NH, TM, LANE), lambda i: (0, i, 0))
    sds = jax.ShapeDtypeStruct
    return pl.pallas_call(
        body, name="qkv_bwd", grid=(s // TM,),
        in_specs=[row(Q_LORA, P_CQ // Q_LORA), row(KV_LORA, P_CKV // KV_LORA), heads, heads, heads,
                  full(g_q), full(g_kv), full(w_uq_p), full(w_k_p), full(w_v_p), row(LANE), row(LANE), row(LANE)],
        out_specs=[row(Q_LORA), row(KV_LORA), row(LANE), acc((Q_LORA, D)), acc((KV_LORA, D)), acc((KV_LORA, D)),
                   acc((1, Q_LORA)), acc((1, KV_LORA))],
        out_shape=[sds((s, Q_LORA), BF16), sds((s, KV_LORA), BF16), sds((s, LANE), BF16), sds((Q_LORA, D), F32),
                   sds((KV_LORA, D), F32), sds((KV_LORA, D), F32), sds((1, Q_LORA), F32), sds((1, KV_LORA), F32)],
        compiler_params=_params(("arbitrary",)),
    )(proj, proj, dq, dk, dvv, g_q, g_kv, w_uq_p, w_k_p, w_v_p, rc, rs1, rs2)


def _front_bwd(x, dout, dmg, dga, dh3, dgb, dcq, dckv, dkpe, g_pre, w_in_pt, token):
    s = x.shape[0]

    def body(x_ref, do_ref, dmg_ref, dga_ref, dh3_ref, dgb_ref, dcq_ref, dckv_ref, dkpe_ref, g_ref, w_ref, token_ref, gx_ref, dg_ref):
        @pl.when(pl.program_id(0) == 0)
        def _():
            dg_ref[...] = jnp.zeros_like(dg_ref)

        xv, g = x_ref[...], g_ref[...]
        _, r = _norm_rows(xv, g)
        pieces = ((dmg_ref[...], P_MERGE), (dga_ref[...], P_GA), (dh3_ref[0], P_HQ), (dh3_ref[1], P_HF), (dh3_ref[2], P_HI),
                  (dgb_ref[...], P_GB), (dcq_ref[...], P_CQ), (dckv_ref[...], P_CKV), (dkpe_ref[...], P_KPE))
        dh = jnp.zeros((TM, D), F32)
        for piece, off in pieces:
            dh = dh + _mm(piece, w_ref[off:off + piece.shape[1], :])
        dg_ref[...] += jnp.sum(dh * (xv * r), axis=0, keepdims=True)
        gx_ref[...] = do_ref[...] + _norm_rows_bwd(xv, r, g, dh)

    row = lambda w: pl.BlockSpec((TM, w), lambda i: (i, 0))
    full = lambda a: pl.BlockSpec(a.shape, lambda i: (0,) * a.ndim)
    sds = jax.ShapeDtypeStruct
    return pl.pallas_call(
        body, name="front_bwd", grid=(s // TM,),
        in_specs=[row(D), row(D), row(2048), row(512), pl.BlockSpec((3, TM, 512), lambda i: (0, i, 0)), row(512), row(Q_LORA),
                  row(KV_LORA), row(LANE), full(g_pre), full(w_in_pt), pl.BlockSpec(memory_space=pl.ANY)],
        out_specs=[row(D), pl.BlockSpec((1, D), lambda i: (0, 0))],
        out_shape=[sds((s, D), F32), sds((1, D), F32)],
        compiler_params=_params(("arbitrary",)),
    )(x, dout, dmg, dga, dh3, dgb, dcq, dckv, dkpe, g_pre, w_in_pt, token)


TK_GRAD = 1024


def _win_grad(h, pieces, name):
    s = h.shape[0]
    n = len(pieces)

    def body(h_ref, *refs):
        @pl.when(pl.program_id(0) == 0)
        def _():
            for o_ref in refs[n:]:
                o_ref[...] = jnp.zeros_like(o_ref)

        hv = h_ref[...]
        for d_ref, o_ref in zip(refs[:n], refs[n:]):
            if len(d_ref.shape) == 3:
                for k in range(d_ref.shape[0]):
                    o_ref[k] += _mm_tn(d_ref[k], hv)
            else:
                o_ref[...] += _mm_tn(d_ref[...], hv)

    def in_spec(p):
        if p.ndim == 3:
            return pl.BlockSpec((p.shape[0], TK_GRAD, p.shape[2]), lambda kk: (0, kk, 0))
        return pl.BlockSpec((TK_GRAD, p.shape[1]), lambda kk: (kk, 0))

    out_shapes = [(p.shape[0], p.shape[2], D) if p.ndim == 3 else (p.shape[1], D) for p in pieces]
    return pl.pallas_call(
        body, name=name, grid=(s // TK_GRAD,),
        in_specs=[pl.BlockSpec((TK_GRAD, D), lambda kk: (kk, 0))] + [in_spec(p) for p in pieces],
        out_specs=[pl.BlockSpec(sh, lambda kk, nd=len(sh): (0,) * nd) for sh in out_shapes],
        out_shape=[jax.ShapeDtypeStruct(sh, F32) for sh in out_shapes],
        compiler_params=_params(("arbitrary",)),
    )(h, *pieces)


def _pad_win_t(w_in_t):
    z = lambda n: jnp.zeros((n, w_in_t.shape[1]), w_in_t.dtype)
    sl = lambda o, n: w_in_t[o:o + n]
    return jnp.concatenate([sl(O_MERGE, 2048), sl(O_GA, 512), sl(O_HQ, 512), sl(O_HF, 512), sl(O_HI, 512), sl(O_GB, 512),
                            sl(O_CQ, Q_LORA), sl(O_CKV, KV_LORA), z(64), sl(O_KPE, QK_ROPE), z(32)], axis=0)


def _pad_wuq(w_uq):
    w = w_uq.reshape(Q_LORA, NH, QK_NOPE + QK_ROPE)
    return jnp.pad(w, ((0, 0), (0, 0), (0, LANE - QK_NOPE - QK_ROPE))).reshape(Q_LORA, NH * LANE)


def _unpad_wuq(g):
    return g.reshape(Q_LORA, NH, LANE)[:, :, :QK_NOPE + QK_ROPE].reshape(Q_LORA, NH * (QK_NOPE + QK_ROPE))


def _pad_wukv(w_ukv):
    w = w_ukv.reshape(KV_LORA, NH, QK_NOPE + V_DIM)
    w_k = jnp.pad(w[:, :, :QK_NOPE], ((0, 0), (0, 0), (0, LANE - QK_NOPE))).reshape(KV_LORA, NH * LANE)
    wv = w[:, :, QK_NOPE:].reshape(KV_LORA, NH // 2, 2, 1, V_DIM)
    eye = jnp.eye(2, dtype=w.dtype).reshape(1, 1, 2, 2, 1)
    return w_k, (wv * eye).reshape(KV_LORA, NH * LANE)


def _unpad_wukv(gk, gv):
    gk = gk.reshape(KV_LORA, NH, LANE)[:, :, :QK_NOPE]
    gv = gv.reshape(KV_LORA, NH // 2, 2, 2, V_DIM)
    gv = jnp.stack([gv[:, :, 0, 0], gv[:, :, 1, 1]], axis=2).reshape(KV_LORA, NH, V_DIM)
    return jnp.concatenate([gk, gv], axis=-1).reshape(KV_LORA, NH * (QK_NOPE + V_DIM))


def _local_step(x, tgt, g_pre, w_in_t, b_gate, g_q, g_kv, lb_logits, g_hgrn, g_post, weights, exchange=None):
    s = x.shape[0]
    w_in_p = _pad_win_t(w_in_t)
    rc, rs1, rs2 = _rope_tables(s)
    g_hg = jnp.tile(g_hgrn, (1, NH))

    proj, h = _front_fwd(x, g_pre, w_in_p, weights.tokens)
    w_uq, w_ukv = weights.qkv(h)
    w_uq_p = _pad_wuq(w_uq)
    w_k_p, w_v_p = _pad_wukv(w_ukv)
    q, k, vv = _qkv_fwd(proj, g_q, g_kv, w_uq_p, w_k_p, w_v_p, rc, rs1, rs2)
    attn, lse = _attn_fwd(q, k, vv)
    o_raw, states = _hgrn_fwd(proj, lb_logits)
    wa, wb, w_out = weights.mid(o_raw)
    (loss, dout, dattn, dga, dor, dgb, dmg, d_wout, d_wa, d_wb, d_gpost, d_bgate, d_ghg) = _mid(
        proj, attn, o_raw, x, tgt, g_hg, b_gate, g_post, wa, wb, w_out)
    w_mg, w_ga, w_gb = _win_grad(h, [dmg, dga, dgb], "win_grad_mid")
    dh3, d_lbl = _hgrn_bwd(proj, lb_logits, states, dor)
    (w_h3,) = _win_grad(h, [dh3], "win_grad_hgrn")
    d_win_rest = jnp.concatenate([w_ga, w_h3[0], w_h3[1], w_h3[2], w_gb, w_mg], axis=0)
    early = dict(w_in_rest=d_win_rest, w_branch_a=d_wa, w_branch_b=d_wb, w_out=d_wout)
    token = exchange.start_early(early) if exchange else jnp.zeros((8, LANE), F32)
    dq, dk, dvv = _attn_bwd(q, k, vv, attn, dattn, lse, token)
    dcq, dckv, dkpe, d_wuq_p, d_wk_p, d_wv_p, d_gq, d_gkv = _qkv_bwd(proj, dq, dk, dvv, g_q, g_kv, w_uq_p, w_k_p, w_v_p, rc, rs1, rs2)
    w_cq, w_ckv, w_kpe = _win_grad(h, [dcq, dckv, dkpe], "win_grad_qkv")
    d_win_qkv = jnp.concatenate([w_cq, w_ckv, w_kpe[64:64 + QK_ROPE]], axis=0)
    late = dict(w_in_qkv=d_win_qkv, w_uq=_unpad_wuq(d_wuq_p), w_ukv=_unpad_wukv(d_wk_p, d_wv_p))
    token = exchange.start_late(late) if exchange else jnp.zeros((8, LANE), F32)
    grad_x, d_gpre = _front_bwd(x, dout, dmg, dga, dh3, dgb, dcq, dckv, dkpe, g_pre, w_in_p, token)
    vec_grads = dict(g_pre=d_gpre, b_gate=d_bgate, g_q=d_gq, g_kv=d_gkv, lb_logits=d_lbl, g_hgrn=d_ghg, g_post=d_gpost)
    return loss, grad_x, dict(early, **late), vec_grads


SHARD_SHAPES = (("w_in", (1416, 1024)), ("w_uq", (192, 768)), ("w_ukv", (256, 256)), ("w_branch_a", (512, 256)),
                ("w_branch_b", (512, 256)), ("w_out", (256, 1024)))
BIG = tuple(n for n, _ in SHARD_SHAPES)
ROW_SHARDED = ("w_in", "w_uq", "w_out")
GATHER_SPLIT_AXIS = dict(w_in=1, w_uq=0, w_ukv=0, w_branch_a=0, w_branch_b=0, w_out=0)
N_CHIPS = 4
QKV_ROWS = Q_LORA + KV_LORA + QK_ROPE


def _to_block(name, a):
    return a[0].T if name == "w_in" else a[0]


def _from_block(name, a):
    return a.T[None] if name == "w_in" else a[None]
VEC_ROWS = (("g_pre", 0, 1024), ("b_gate", 1, 2048), ("g_q", 2, 768), ("g_kv", 3, 256), ("g_hgrn", 6, 64), ("g_post", 7, 1024))
VEC_LB_ROW = 4
VEC_SHAPE = (8, 2048)


def _split_by_chip(name, g):
    a, b = dict(SHARD_SHAPES)[name]
    return g.reshape(N_CHIPS, a, b) if name in ROW_SHARDED else g.reshape(a, N_CHIPS, b).transpose(1, 0, 2)


def _join_chips(name, w):
    a, b = dict(SHARD_SHAPES)[name]
    return w.reshape(N_CHIPS * a, b) if name in ROW_SHARDED else w.transpose(1, 0, 2).reshape(a, N_CHIPS * b)


MESH = pl.DeviceIdType.MESH
HBM = pl.BlockSpec(memory_space=pltpu.HBM)


def _mesh_place():
    x, y, c = lax.axis_index("x"), lax.axis_index("y"), lax.axis_index("c")
    return x, y, c, 2 * x + y, [(1 - x, y), (x, 1 - y), (1 - x, 1 - y)]


def _remote(src, dst, send_sems, recv_sems, k, to):
    return pltpu.make_async_remote_copy(src_ref=src, dst_ref=dst, send_sem=send_sems.at[k], recv_sem=recv_sems.at[k],
                                        device_id=to, device_id_type=MESH)


def _gather_weights(shards, split_axes):
    n = len(shards)

    def body(*refs):
        srcs, outs = refs[:n], refs[n:2 * n]
        ici_send, ici_recv, d2d_send, d2d_recv, local_sems = refs[2 * n:]
        x, y, c, me, chips = _mesh_place()
        sibling = (x, y, 1 - c)

        def half(ref, k, which):
            size = shards[k].shape[split_axes[k]] // 2
            part = pl.ds(pl.multiple_of(which * size, size), size)
            return ref.at[part] if split_axes[k] == 0 else ref.at[:, part]

        own = [pltpu.make_async_copy(srcs[k], outs[k].at[me], local_sems.at[k]) for k in range(n)]
        for cp in own:
            cp.start()
        started = []
        for k in range(n):
            for j, (px, py) in enumerate(chips):
                cp = _remote(half(srcs[k], k, c), half(outs[k].at[me], k, c), ici_send, ici_recv, 3 * k + j, (px, py, c))
                cp.start()
                started.append(cp)
        for k in range(n):
            for j, (px, py) in enumerate(chips):
                landed = half(outs[k].at[2 * px + py], k, c)
                _remote(landed, landed, ici_send, ici_recv, 3 * k + j, (px, py, c)).wait_recv()
                cp = _remote(landed, landed, d2d_send, d2d_recv, 3 * k + j, sibling)
                cp.start()
                started.append(cp)
        for k in range(n):
            for j, (px, py) in enumerate(chips):
                other = half(outs[k].at[2 * px + py], k, 1 - c)
                _remote(other, other, d2d_send, d2d_recv, 3 * k + j, sibling).wait_recv()
        for cp in started:
            cp.wait_send()
        for cp in own:
            cp.wait()

    sems = pltpu.SemaphoreType.DMA((3 * n,))
    return pl.pallas_call(
        body, name="gather_weights", in_specs=[HBM] * n, out_specs=[HBM] * n,
        out_shape=[jax.ShapeDtypeStruct((N_CHIPS,) + s.shape, s.dtype) for s in shards],
        scratch_shapes=[sems, sems, sems, sems, pltpu.SemaphoreType.DMA((n,))],
        compiler_params=pltpu.CompilerParams(has_side_effects=True),
    )(*shards)


def _sibling_exchange(srcs, name, after=None):
    n = len(srcs)
    extra = [] if after is None else [after]

    def body(*refs):
        src_refs, outs = refs[:n], refs[n + len(extra):2 * n + len(extra)]
        send_sems, recv_sems = refs[2 * n + len(extra):]
        sibling = (lax.axis_index("x"), lax.axis_index("y"), 1 - lax.axis_index("c"))
        copies = [_remote(src_refs[k], outs[k], send_sems, recv_sems, k, sibling) for k in range(n)]
        for cp in copies:
            cp.start()
        for cp in copies:
            cp.wait()

    sems = pltpu.SemaphoreType.DMA((n,))
    return pl.pallas_call(
        body, name=name, in_specs=[HBM] * n + [pl.BlockSpec(memory_space=pl.ANY)] * len(extra), out_specs=[HBM] * n,
        out_shape=[jax.ShapeDtypeStruct(s.shape, s.dtype) for s in srcs],
        scratch_shapes=[sems, sems],
        compiler_params=pltpu.CompilerParams(has_side_effects=True),
    )(*srcs, *extra)


SEM = pl.BlockSpec(memory_space=pltpu.SEMAPHORE)
DATAFLOW = pltpu.SideEffectType.DATAFLOW_SIDE_EFFECTING


def _exchange_copies(srcs, to_first, src_refs, land_refs, send_sems, recv_sems):
    x, y, c, me, chips = _mesh_place()
    n = len(srcs)
    sends, recvs = [], []
    for k in range(n):
        if k in to_first:
            base = 3 * n + 4 * to_first.index(k)
            sends.append((me != 0, pltpu.make_async_remote_copy(
                src_ref=src_refs[k], dst_ref=land_refs[k].at[me], send_sem=send_sems.at[base], recv_sem=recv_sems.at[base + me],
                device_id=(0, 0, c), device_id_type=MESH)))
            for s in range(1, N_CHIPS):
                recvs.append((me == 0, pltpu.make_async_remote_copy(
                    src_ref=src_refs[k], dst_ref=land_refs[k].at[s], send_sem=send_sems.at[base], recv_sem=recv_sems.at[base + s],
                    device_id=(s // 2, s % 2, c), device_id_type=MESH)))
        else:
            slab = (lambda t, k=k: src_refs[k]) if srcs[k].ndim == 2 else (lambda t, k=k: src_refs[k].at[t])
            for j, (px, py) in enumerate(chips):
                sends.append((None, _remote(slab(2 * px + py), land_refs[k].at[me], send_sems, recv_sems, 3 * k + j, (px, py, c))))
                recvs.append((None, _remote(slab(me), land_refs[k].at[2 * px + py], send_sems, recv_sems, 3 * k + j, (px, py, c))))
    return sends, recvs


def _when(pred, fn):
    if pred is None:
        fn()
    else:
        pl.when(pred)(fn)


def _exchange_start(srcs, to_first, name, after=None):
    n = len(srcs)
    n_sems = 3 * n + 4 * len(to_first)
    lands = [lax.empty((N_CHIPS,) + s.shape[-2:], s.dtype) for s in srcs]
    extra = [] if after is None else [after]

    def body(*refs):
        src_refs, land_refs = refs[:n], refs[n:2 * n]
        send_sems, recv_sems, token = refs[2 * n + len(extra)], refs[2 * n + len(extra) + 1], refs[-1]
        sends, _ = _exchange_copies(srcs, to_first, src_refs, land_refs, send_sems, recv_sems)
        for pred, cp in sends:
            _when(pred, cp.start)
        token[...] = jnp.zeros_like(token)

    hbm = lambda a: pltpu.HBM(a.shape, a.dtype)
    res = pl.pallas_call(
        body, name=name,
        out_shape=[pltpu.SemaphoreType.DMA((n_sems,)), pltpu.SemaphoreType.DMA((n_sems,))] + [hbm(a) for a in srcs + lands]
        + [jax.ShapeDtypeStruct((8, LANE), F32)],
        in_specs=[HBM] * (2 * n) + [pl.BlockSpec(memory_space=pl.ANY)] * len(extra),
        out_specs=[SEM, SEM] + [HBM] * (2 * n) + [pl.BlockSpec(memory_space=pltpu.VMEM)],
        input_output_aliases={i: 2 + i for i in range(2 * n)},
        compiler_params=pltpu.CompilerParams(has_side_effects=DATAFLOW),
    )(*[pltpu.with_memory_space_constraint(a, pltpu.HBM) for a in srcs + lands], *extra)
    return res[:-1], res[-1]


def _exchange_wait(srcs, to_first, started, after, name):
    n = len(srcs)
    send_sems, recv_sems, thru = started[0], started[1], started[2:]

    def body(*refs):
        src_refs, land_refs, send_ref, recv_ref = refs[:n], refs[n:2 * n], refs[2 * n], refs[2 * n + 1]
        sends, recvs = _exchange_copies(srcs, to_first, src_refs, land_refs, send_ref, recv_ref)
        for pred, cp in sends:
            _when(pred, cp.wait_send)
        for pred, cp in recvs:
            _when(pred, cp.wait_recv)

    res = pl.pallas_call(
        body, name=name, out_shape=[pltpu.HBM(a.shape, a.dtype) for a in thru],
        in_specs=[HBM] * (2 * n) + [SEM, SEM, pl.BlockSpec(memory_space=pl.ANY)], out_specs=[HBM] * (2 * n),
        input_output_aliases={i: i for i in range(2 * n)},
        compiler_params=pltpu.CompilerParams(has_side_effects=DATAFLOW),
    )(*thru, send_sems, recv_sems, after)
    return res[n:]


ROW_TILE = 256
COL_TILE = 256


def _block_tiling(a, b):
    if a <= ROW_TILE or a % ROW_TILE == 0:
        ta = min(a, ROW_TILE)
        return a // ta, (ta, b), lambda i: (i, 0)
    return b // COL_TILE, (a, COL_TILE), lambda i: (0, i)


def _sum_landed(land, own, name, first_land=None, first_own=None):
    _, a, b = land.shape
    steps, tile, at = _block_tiling(a, b)
    extra = first_land is not None

    def body(*refs):
        p_ref, own_ref, o_ref = refs[0], refs[1], refs[-1]
        me = 2 * lax.axis_index("x") + lax.axis_index("y")
        own = own_ref[...].astype(F32)
        slot = lambda t: jnp.where(me == t, own, p_ref[t].astype(F32))
        o_ref[...] = ((slot(0) + slot(1)) + slot(2)) + slot(3)
        if extra:
            fp_ref, fo_ref = refs[2], refs[3]
            r = fo_ref.shape[0]

            @pl.when(me == 0)
            def _():
                f = lambda t: fp_ref[t].astype(F32)
                rows = pl.ds(pl.multiple_of(lax.axis_index("c") * r, 8), r)
                o_ref[rows, :] += ((fo_ref[...].astype(F32) + f(1)) + f(2)) + f(3)

    in_specs = [pl.BlockSpec((N_CHIPS,) + tile, lambda i: (0,) + at(i)), pl.BlockSpec(tile, at)]
    args = [land, own]
    if extra:
        r = first_own.shape[0]
        assert tile[0] == a, "the extra rows need whole columns in a step"
        in_specs += [pl.BlockSpec((N_CHIPS, r, tile[1]), lambda i: (0,) + at(i)), pl.BlockSpec((r, tile[1]), at)]
        args += [first_land, first_own]
    return pl.pallas_call(
        body, name=name, grid=(steps,), in_specs=in_specs, out_specs=pl.BlockSpec(tile, at),
        out_shape=jax.ShapeDtypeStruct((a, b), F32), compiler_params=_params(("parallel",)),
    )(*args)


def _add_cast(a, b, name):
    def body(a_ref, b_ref, o_ref):
        o_ref[...] = (a_ref[...] + b_ref[...]).astype(BF16)

    return pl.pallas_call(body, name=name, out_shape=jax.ShapeDtypeStruct(a.shape, BF16),
                          compiler_params=_params(()))(a, b)


class _LaterWeights:
    QKV = ("w_uq", "w_ukv")
    MID = ("w_branch_a", "w_branch_b", "w_out")

    def __init__(self, blocks, after):
        self.blocks = blocks
        self.qkv_started, t1 = _exchange_start([blocks[n] for n in self.QKV], (), "weights_qkv_start", after)
        self.mid_started, t2 = _exchange_start([blocks[n] for n in self.MID], (), "weights_mid_start", after)
        self.tokens = [t1, t2]

    def _whole(self, names, started, after, name):
        landed = _exchange_wait([self.blocks[n] for n in names], (), started, after, name)
        me = 2 * lax.axis_index("x") + lax.axis_index("y")
        return [_join_chips(n, lax.dynamic_update_index_in_dim(land, self.blocks[n], me, 0)) for n, land in zip(names, landed)]

    def qkv(self, after):
        return self._whole(self.QKV, self.qkv_started, after, "weights_qkv_wait")

    def mid(self, after):
        return self._whole(self.MID, self.mid_started, after, "weights_mid_wait")


class _GradExchange:
    EARLY = ("w_in", "w_branch_a", "w_branch_b", "w_out")
    LATE = ("w_uq", "w_ukv")

    def __init__(self, state):
        self.state = state
        self.outs = {}

    @staticmethod
    def _own(slabs):
        return lax.dynamic_index_in_dim(slabs, 2 * lax.axis_index("x") + lax.axis_index("y"), axis=0, keepdims=False)

    def start_early(self, g):
        full = jnp.concatenate([jnp.zeros((QKV_ROWS, D), F32), g["w_in_rest"]], axis=0)
        g = dict(g, w_in=full)
        self.early = [_split_by_chip(n, g[n]).astype(BF16) for n in self.EARLY]
        self.early_started, token = _exchange_start(self.early, (), "grads_early_start")
        return token

    def start_late(self, g):
        self.early_landed = _exchange_wait(self.early, (), self.early_started, g["w_uq"], "grads_early_wait")
        half = QKV_ROWS // 2
        c = lax.axis_index("c")
        mine = lax.dynamic_slice_in_dim(g["w_in_qkv"], c * half, half, axis=0)
        (theirs,) = _sibling_exchange([lax.dynamic_slice_in_dim(g["w_in_qkv"], (1 - c) * half, half, axis=0)], "sibling_qkv_rows")
        self.late = [_split_by_chip(n, g[n]).astype(BF16) for n in self.LATE] + [_add_cast(mine, theirs, "add_qkv_rows")]
        self.late_started, token = _exchange_start(self.late, (2,), "grads_late_start")
        names = self.EARLY[1:]
        mine = [_sum_landed(land, self._own(slabs), "sum_" + n) for n, slabs, land in list(zip(self.EARLY, self.early, self.early_landed))[1:]]
        theirs = _sibling_exchange(mine, "sibling_early", after=token)
        for n, a, b in zip(names, mine, theirs):
            self.outs[n] = _adamw(a, b, *self.state[n], "adamw_" + n)
        return self.outs[names[-1]][0]

    def finish(self, after):
        late_landed = _exchange_wait(self.late, (2,), self.late_started, after, "grads_late_wait")
        sums = {"w_in": _sum_landed(self.early_landed[0], self._own(self.early[0]), "sum_w_in",
                                    first_land=late_landed[2], first_own=self.late[2])}
        for n, slabs, land in zip(self.LATE, self.late, late_landed):
            sums[n] = _sum_landed(land, self._own(slabs), "sum_" + n)
        return sums


def _adamw_math(g, w, m, v):
    nm = ADAM_B1 * m + (1.0 - ADAM_B1) * g
    nv = ADAM_B2 * v + (1.0 - ADAM_B2) * (g * g)
    m_hat = nm / (1.0 - ADAM_B1 ** ADAM_STEP)
    v_hat = nv / (1.0 - ADAM_B2 ** ADAM_STEP)
    return -ADAM_LR * (m_hat / (jnp.sqrt(v_hat) + ADAM_EPS) + ADAM_WD * w), nm, nv


def _adamw(p_mine, p_sibling, w, m, v, name):
    a, b = p_mine.shape
    steps, tile, at = _block_tiling(a, b)

    def body(a_ref, b_ref, w_ref, m_ref, v_ref, g_ref, d_ref, nm_ref, nv_ref):
        g = a_ref[...] + b_ref[...]
        g_ref[...] = g
        d_ref[...], nm_ref[...], nv_ref[...] = _adamw_math(g, w_ref[...], m_ref[...], v_ref[...])

    spec = pl.BlockSpec(tile, at)
    sds = jax.ShapeDtypeStruct((a, b), F32)
    return pl.pallas_call(
        body, name=name, grid=(steps,), in_specs=[spec] * 5, out_specs=[spec] * 4, out_shape=[sds] * 4,
        compiler_params=_params(("parallel",)),
    )(p_mine, p_sibling, w, m, v)


LOSS_AT = (2, 1024)


def _vec_pack(vg, loss):
    names = [n for n, _, _ in VEC_ROWS]

    def body(*refs):
        o_ref = refs[-1]
        lb_ref, loss_ref = refs[len(names)], refs[len(names) + 1]
        o_ref[...] = jnp.zeros_like(o_ref)
        o_ref[LOSS_AT[0]:LOSS_AT[0] + 1, LOSS_AT[1]:LOSS_AT[1] + LANE] = jnp.broadcast_to(loss_ref[...], (1, LANE))
        for (name, row, size), ref in zip(VEC_ROWS, refs):
            if name == "g_hgrn":
                r = lax.broadcasted_iota(jnp.int32, (NH * V_DIM, LANE), 0)
                c = lax.broadcasted_iota(jnp.int32, (NH * V_DIM, LANE), 1)
                fold = ((r % V_DIM) == c).astype(F32)
                o_ref[row:row + 1, 0:LANE] = jnp.dot(ref[...], fold, precision=HIGHEST, preferred_element_type=F32)
            else:
                o_ref[row:row + 1, 0:size] = ref[...]
        o_ref[VEC_LB_ROW:VEC_LB_ROW + 2, 0:512] = lb_ref[...]

    return pl.pallas_call(body, name="vec_pack", out_shape=jax.ShapeDtypeStruct(VEC_SHAPE, F32))(
        *[vg[n] for n in names], vg["lb_logits"], loss)


def _adamw_vec(p_mine, p_sibling, w, m, v):
    names = [n for n, _, _ in VEC_ROWS] + ["lb_logits"]
    k = len(names)

    def body(a_ref, b_ref, *refs):
        ins, outs = refs[:3 * k], refs[3 * k:]
        at = (slice(LOSS_AT[0], LOSS_AT[0] + 1), slice(LOSS_AT[1], LOSS_AT[1] + LANE))
        outs[-1][...] = a_ref[at] + b_ref[at]
        for i, name in enumerate(names):
            if name == "lb_logits":
                rows, cols = slice(VEC_LB_ROW, VEC_LB_ROW + 2), slice(0, 512)
            else:
                _, row, size = VEC_ROWS[i]
                rows, cols = slice(row, row + 1), slice(0, size)
            g = a_ref[rows, cols] + b_ref[rows, cols]
            d, nm, nv = _adamw_math(g, ins[i][...], ins[k + i][...], ins[2 * k + i][...])
            for o_ref, val in zip(outs[4 * i:4 * i + 4], (g, d, nm, nv)):
                o_ref[...] = val

    shapes = [jax.ShapeDtypeStruct(w[n].shape, F32) for n in names for _ in range(4)] + [jax.ShapeDtypeStruct((1, LANE), F32)]
    res = pl.pallas_call(body, name="adamw_vec", out_shape=shapes)(
        p_mine, p_sibling, *[w[n] for n in names], *[m[n] for n in names], *[v[n] for n in names])
    return [{n: res[4 * i + j] for i, n in enumerate(names)} for j in range(4)], res[-1]


WEIGHTS = ("g_pre", "w_in", "b_gate", "g_q", "w_uq", "g_kv", "w_ukv", "lb_logits", "g_hgrn", "w_branch_a", "w_branch_b", "w_out", "g_post")


def kernel(x, g_pre, w_in, b_gate, g_q, w_uq, g_kv, w_ukv, lb_logits, g_hgrn, w_branch_a, w_branch_b, w_out, g_post, loss_target, m_g_pre, m_w_in, m_b_gate, m_g_q, m_w_uq, m_g_kv, m_w_ukv, m_lb_logits, m_g_hgrn, m_w_branch_a, m_w_branch_b, m_w_out, m_g_post, v_g_pre, v_w_in, v_b_gate, v_g_q, v_w_uq, v_g_kv, v_w_ukv, v_lb_logits, v_g_hgrn, v_w_branch_a, v_w_branch_b, v_w_out, v_g_post):
    w = dict(g_pre=g_pre, w_in=w_in, b_gate=b_gate, g_q=g_q, w_uq=w_uq, g_kv=g_kv, w_ukv=w_ukv, lb_logits=lb_logits, g_hgrn=g_hgrn,
             w_branch_a=w_branch_a, w_branch_b=w_branch_b, w_out=w_out, g_post=g_post)
    m = dict(g_pre=m_g_pre, w_in=m_w_in, b_gate=m_b_gate, g_q=m_g_q, w_uq=m_w_uq, g_kv=m_g_kv, w_ukv=m_w_ukv, lb_logits=m_lb_logits,
             g_hgrn=m_g_hgrn, w_branch_a=m_w_branch_a, w_branch_b=m_w_branch_b, w_out=m_w_out, g_post=m_g_post)
    v = dict(g_pre=v_g_pre, w_in=v_w_in, b_gate=v_b_gate, g_q=v_g_q, w_uq=v_w_uq, g_kv=v_g_kv, w_ukv=v_w_ukv, lb_logits=v_lb_logits,
             g_hgrn=v_g_hgrn, w_branch_a=v_w_branch_a, w_branch_b=v_w_branch_b, w_out=v_w_out, g_post=v_g_post)
    blocks = {n: _to_block(n, w[n]).astype(BF16) for n in BIG}
    (w_in_all,) = _gather_weights([blocks["w_in"]], [GATHER_SPLIT_AXIS["w_in"]])
    weights = _LaterWeights(blocks, w_in_all)
    state = {n: [_to_block(n, t[n]) for t in (w, m, v)] for n in BIG}
    exchange = _GradExchange(state)
    loss, grad_x, _, vec_grads = _local_step(
        x[0], loss_target[0], g_pre, _join_chips("w_in", w_in_all), b_gate, g_q, g_kv, lb_logits, g_hgrn, g_post, weights, exchange)
    vec = _vec_pack(vec_grads, loss)
    vec_started, token = _exchange_start([vec], (), "vec_start")
    sums = exchange.finish(token)
    rest = tuple(sums)
    (vec_landed,) = _exchange_wait([vec], (), vec_started, sums[rest[-1]], "vec_wait")
    mine = [sums[n] for n in rest] + [_sum_landed(vec_landed, vec, "sum_vec")]
    theirs = _sibling_exchange(mine, "sibling_grads")
    done = dict(exchange.outs)
    for k, n in enumerate(rest):
        done[n] = _adamw(mine[k], theirs[k], *state[n], "adamw_" + n)
    outs = [{}, {}, {}, {}]
    for n in BIG:
        for o, val in zip(outs, done[n]):
            o[n] = _from_block(n, val)
    vec_outs, total = _adamw_vec(mine[-1], theirs[-1], w, m, v)
    for o, vals in zip(outs, vec_outs):
        o.update(vals)
    return (total[0, 0], grad_x[None], *[o[n] for o in outs for n in WEIGHTS])
```

```python
import functools
import math

import numpy as np
import jax
import jax.numpy as jnp
from jax import lax
from jax.experimental import pallas as pl
from jax.experimental.pallas import tpu as pltpu

F32 = jnp.float32
BF16 = jnp.bfloat16
HIGHEST = lax.Precision.HIGHEST

D = 1024
NH = 8
QK_NOPE, QK_ROPE, V_DIM = 64, 32, 64
Q_LORA, KV_LORA = 768, 256
CHUNK = 64
HG_BLOCK = 32
EPS = 1e-6
D_IN = 5664
LANE = 128
P_MERGE, P_GA, P_HQ, P_HF, P_HI, P_GB, P_CQ, P_CKV, P_KPE = 0, 2048, 2560, 3072, 3584, 4096, 4608, 5376, 5632
D_P = 5760
O_CQ, O_CKV, O_KPE, O_GA, O_HQ, O_HF, O_HI, O_GB, O_MERGE = 0, 768, 1024, 1056, 1568, 2080, 2592, 3104, 3616

TM = 512
TM_MID = 256
TQ = 1024
ONES_LANE = (LANE - 1, 0)
TH = 256
HG_PAIRS = 4
VMEM_LIMIT = 56 * 1024 * 1024

ADAM_LR, ADAM_B1, ADAM_B2, ADAM_EPS, ADAM_WD, ADAM_STEP = 0.001, 0.9, 0.999, 1e-08, 0.01, 10

NT_DIMS = (((1,), (1,)), ((), ()))
TN_DIMS = (((0,), (0,)), ((), ()))


def _params(sem):
    return pltpu.CompilerParams(dimension_semantics=sem, vmem_limit_bytes=VMEM_LIMIT)


def _mm(a, b):
    return jnp.dot(a, b, preferred_element_type=F32)


def _mm_nt(a, b):
    return lax.dot_general(a, b, NT_DIMS, preferred_element_type=F32)


def _mm_tn(a, b):
    return lax.dot_general(a, b, TN_DIMS, preferred_element_type=F32)


def _sigmoid(z):
    return jax.nn.sigmoid(z)


def _rope(v, c, s1, s2):
    return v * c + pltpu.roll(v, 112, 1) * s1 + pltpu.roll(v, 16, 1) * s2


def _rope_t(dy, c, s1, s2):
    return dy * c + pltpu.roll(dy * s1, 16, 1) + pltpu.roll(dy * s2, 112, 1)


def _rope_tables(s):
    inv = 10000.0 ** (-jnp.arange(0, QK_ROPE, 2, dtype=F32) / QK_ROPE)
    ang = jnp.arange(s, dtype=F32)[:, None] * inv[None, :]
    cos, sin = jnp.cos(ang), jnp.sin(ang)
    z64, z32, o64, o32 = jnp.zeros((s, 64), F32), jnp.zeros((s, 32), F32), jnp.ones((s, 64), F32), jnp.ones((s, 32), F32)
    z16 = jnp.zeros((s, 16), F32)
    c = jnp.concatenate([o64, cos, cos, o32], axis=1)
    s1 = jnp.concatenate([z64, -sin, z16, z32], axis=1)
    s2 = jnp.concatenate([z64, z16, sin, z32], axis=1)
    return c, s1, s2


def _front_fwd(x, g_pre, w_in_pt, tokens=()):
    s = x.shape[0]
    tokens = list(tokens)

    def body(x_ref, g_ref, w_ref, *refs):
        o_ref, h_ref = refs[len(tokens):]
        xv = x_ref[...]
        r = lax.rsqrt(jnp.mean(xv * xv, axis=-1, keepdims=True) + EPS)
        h = ((xv * r) * g_ref[...]).astype(BF16)
        h_ref[...] = h
        o_ref[...] = _mm_nt(h, w_ref[...])

    return pl.pallas_call(
        body, name="front_fwd", grid=(s // TM,),
        in_specs=[pl.BlockSpec((TM, D), lambda i: (i, 0)), pl.BlockSpec((1, D), lambda i: (0, 0)),
                  pl.BlockSpec((D_P, D), lambda i: (0, 0))] + [pl.BlockSpec((8, LANE), lambda i: (0, 0))] * len(tokens),
        out_specs=[pl.BlockSpec((TM, D_P), lambda i: (i, 0)), pl.BlockSpec((TM, D), lambda i: (i, 0))],
        out_shape=[jax.ShapeDtypeStruct((s, D_P), F32), jax.ShapeDtypeStruct((s, D), BF16)],
        compiler_params=_params(("parallel",)),
    )(x, g_pre, w_in_pt, *tokens)


def _norm_rows(v, g):
    r = lax.rsqrt(jnp.mean(v * v, axis=-1, keepdims=True) + EPS)
    return (v * r) * g, r


def _qkv_fwd(proj, g_q, g_kv, w_uq_p, w_k_p, w_v_p, rc, rs1, rs2):
    s = proj.shape[0]

    def body(cq_ref, ckv_ref, kpe_ref, gq_ref, gkv_ref, wq_ref, wk_ref, wv_ref, c_ref, s1_ref, s2_ref, q_ref, k_ref, v_ref):
        c, s1, s2 = c_ref[...], s1_ref[...], s2_ref[...]
        cqn, _ = _norm_rows(cq_ref[...], gq_ref[...])
        ckvn, _ = _norm_rows(ckv_ref[...], gkv_ref[...])
        ckvn = ckvn.astype(BF16)
        qf = _mm(cqn.astype(BF16), wq_ref[...])
        kf = _mm(ckvn, wk_ref[...])
        vf = _mm(ckvn, wv_ref[...])
        kpe = _rope(kpe_ref[...], c, s1, s2)
        lane = lax.broadcasted_iota(jnp.int32, (TM, LANE), 1)
        for h in range(NH):
            blk = slice(h * LANE, (h + 1) * LANE)
            q_ref[h] = _rope(qf[:, blk], c, s1, s2).astype(BF16)
            k_ref[h] = (kf[:, blk] + kpe).astype(BF16)
            v_ref[h] = jnp.where(lane == ONES_LANE[h % 2], 1.0, vf[:, blk]).astype(BF16)

    row = lambda w, j: pl.BlockSpec((TM, w), lambda i: (i, j))
    full = lambda a: pl.BlockSpec(a.shape, lambda i: (0,) * a.ndim)
    hs = jax.ShapeDtypeStruct((NH, s, LANE), BF16)
    return pl.pallas_call(
        body, name="qkv_fwd", grid=(s // TM,),
        in_specs=[row(Q_LORA, P_CQ // Q_LORA), row(KV_LORA, P_CKV // KV_LORA), row(LANE, P_KPE // LANE),
                  full(g_q), full(g_kv), full(w_uq_p), full(w_k_p), full(w_v_p), row(LANE, 0), row(LANE, 0), row(LANE, 0)],
        out_specs=[pl.BlockSpec((NH, TM, LANE), lambda i: (0, i, 0))] * 3,
        out_shape=[hs, hs, hs],
        compiler_params=_params(("parallel",)),
    )(proj, proj, proj, g_q, g_kv, w_uq_p, w_k_p, w_v_p, rc, rs1, rs2)


LOG2E = 1.4426950408889634
QK_SCALE2 = LOG2E / math.sqrt(QK_NOPE + QK_ROPE)


HQ = TQ // 2


def _diag_visible(n):
    row = lax.broadcasted_iota(jnp.int32, (n, n), 0)
    col = lax.broadcasted_iota(jnp.int32, (n, n), 1)
    return (col // CHUNK) <= (row // CHUNK)


def _attn_fwd(q, k, vv):
    s = q.shape[1]

    def body(q_ref, k_ref, v_ref, o_ref, lse_ref):
        i = pl.program_id(1)
        qs = (q_ref[0], q_ref[1])

        def tile(hh, t, carry, diag):
            m, acc = carry
            rows = pl.ds(pl.multiple_of(t * TQ, TQ), TQ)
            sc = _mm_nt(qs[hh], k_ref[hh, rows, :])
            if diag:
                sc = jnp.where(_diag_visible(TQ), sc, -jnp.inf)
            m_new = jnp.maximum(m, jnp.max(sc, axis=-1, keepdims=True))
            alpha = jnp.exp2((m - m_new) * QK_SCALE2)
            p = jnp.exp2((sc - m_new) * QK_SCALE2).astype(BF16)
            acc = alpha * acc + _mm(p, v_ref[hh, rows, :])
            return m_new, acc

        def step(t, carry):
            return tile(0, t, carry[0], False), tile(1, t, carry[1], False)

        init = (jnp.full((TQ, 1), -jnp.inf, F32), jnp.zeros((TQ, LANE), F32))
        carry = lax.fori_loop(0, i, step, (init, init))
        lane = lax.broadcasted_iota(jnp.int32, (TQ, LANE), 1)
        out = jnp.zeros((TQ, LANE), F32)
        for hh in range(2):
            m, acc = tile(hh, i, carry[hh], True)
            l = jnp.sum(jnp.where(lane == ONES_LANE[hh], acc, 0.0), axis=-1, keepdims=True)
            out = out + jnp.where((lane < V_DIM) == (hh == 0), acc, 0.0) / l
            lse_ref[hh] = jnp.broadcast_to(m * QK_SCALE2 + jnp.log(l) * LOG2E, (TQ, LANE))
        o_ref[...] = out

    return pl.pallas_call(
        body, name="attn_fwd", grid=(NH // 2, s // TQ),
        in_specs=[pl.BlockSpec((2, TQ, LANE), lambda p, i: (p, i, 0)), pl.BlockSpec((2, s, LANE), lambda p, i: (p, 0, 0)),
                  pl.BlockSpec((2, s, LANE), lambda p, i: (p, 0, 0))],
        out_specs=[pl.BlockSpec((TQ, LANE), lambda p, i: (i, p)), pl.BlockSpec((2, TQ, LANE), lambda p, i: (p, i, 0))],
        out_shape=[jax.ShapeDtypeStruct((s, NH * V_DIM), F32), jax.ShapeDtypeStruct((NH, s, LANE), F32)],
        compiler_params=_params(("parallel", "parallel")),
    )(q, k, vv)


def _lower_bound(lbl):
    a0, a1 = lbl[0:1, :], lbl[1:2, :]
    mx = jnp.maximum(a0, a1)
    e0, e1 = jnp.exp(a0 - mx), jnp.exp(a1 - mx)
    return e0 / (e0 + e1)


def _chunk_cumsum(v, reverse=False):
    pos = lax.broadcasted_iota(jnp.int32, v.shape, 0) % HG_BLOCK
    s = 1
    while s < HG_BLOCK:
        if reverse:
            v = v + jnp.where(pos < HG_BLOCK - s, pltpu.roll(v, TH - s, 0), 0.0)
        else:
            v = v + jnp.where(pos >= s, pltpu.roll(v, s, 0), 0.0)
        s *= 2
    return v


def _hgrn_gates(hq, hf, lb):
    sig = _sigmoid(hf)
    f = lb + (1.0 - lb) * sig
    g = jnp.log(f)
    kk = 1.0 - f
    r = lax.broadcasted_iota(jnp.int32, (TH, TH), 0)
    c = lax.broadcasted_iota(jnp.int32, (TH, TH), 1)
    tri = ((r // HG_BLOCK) == (c // HG_BLOCK)) & (r >= c)
    cum = _chunk_cumsum(g)
    nch = TH // HG_BLOCK
    total = _chunks(cum)[:, HG_BLOCK - 1:HG_BLOCK, :]
    lastb = jnp.broadcast_to(total, (nch, HG_BLOCK, LANE)).reshape(TH, LANE)
    e, ei, ee = jnp.exp(cum), jnp.exp(-cum), jnp.exp(lastb - cum)
    return dict(sig=sig, f=f, kk=kk, tri=tri, cum=cum, total=total, e=e, ei=ei, ee=ee, qd=hq * e, ki=kk * ei, ke=kk * ee)


def _chunks(v):
    return v.reshape(TH // HG_BLOCK, HG_BLOCK, v.shape[-1])


def _bmm_nt(a, b):
    return lax.dot_general(a, b, (((2,), (2,)), ((0,), (0,))), preferred_element_type=F32)


def _bmm_nn(a, b):
    return lax.dot_general(a, b, (((2,), (1,)), ((0,), (0,))), preferred_element_type=F32)


def _bmm_tn(a, b):
    return lax.dot_general(a, b, (((1,), (1,)), ((0,), (0,))), preferred_element_type=F32)


def _pair_masks():
    lane = lax.broadcasted_iota(jnp.int32, (TH, LANE), 1)
    kr = lax.broadcasted_iota(jnp.int32, (LANE, LANE), 0)
    kc = lax.broadcasted_iota(jnp.int32, (LANE, LANE), 1)
    return lane < 64, (kr // 64) == (kc // 64)


def _hgrn_fwd(proj, lbl):
    s = proj.shape[0]
    nch = TH // HG_BLOCK

    def body(hq_ref, hf_ref, hi_ref, lbl_ref, o_ref, st_ref, st):
        @pl.when(pl.program_id(1) == 0)
        def _():
            st[...] = jnp.zeros_like(st)

        m0, bd = _pair_masks()
        for u in range(HG_PAIRS):
            lanes = slice(u * LANE, (u + 1) * LANE)
            lb = _lower_bound(lbl_ref[:, lanes])
            gt = _hgrn_gates(hq_ref[:, lanes], hf_ref[:, lanes], lb)
            v_b = hi_ref[:, lanes].astype(BF16)
            qd, ki_b, ke_b = gt["qd"], gt["ki"].astype(BF16), gt["ke"].astype(BF16)
            qd_b = qd.astype(BF16)
            o = jnp.zeros((TH, LANE), F32)
            for hh in range(2):
                mh = m0 if hh == 0 else jnp.logical_not(m0)
                a = jnp.where(gt["tri"], _mm_nt(jnp.where(mh, qd, 0.0).astype(BF16), ki_b), 0.0)
                o = jnp.where(mh, _mm(a.astype(BF16), v_b), o)
            upd = _bmm_tn(_chunks(v_b), _chunks(ke_b))
            decay = jnp.exp(gt["total"])
            cur, entering = st[u], []
            for n in range(nch):
                entering.append(cur)
                cur = decay[n] * cur + jnp.where(bd, upd[n], 0.0)
            st[u] = cur
            entering = jnp.stack(entering)
            st_ref[u] = entering
            o_ref[:, lanes] = o + _bmm_nt(_chunks(qd_b), entering.astype(BF16)).reshape(TH, LANE)

    wide = HG_PAIRS * LANE
    col = lambda base: pl.BlockSpec((TH, wide), lambda p, i: (i, base // wide + p))
    return pl.pallas_call(
        body, name="hgrn_fwd", grid=(NH // 2 // HG_PAIRS, s // TH),
        in_specs=[col(P_HQ), col(P_HF), col(P_HI), pl.BlockSpec((2, wide), lambda p, i: (0, p))],
        out_specs=[pl.BlockSpec((TH, wide), lambda p, i: (i, p)),
                   pl.BlockSpec((HG_PAIRS, nch, LANE, LANE), lambda p, i: (p, i, 0, 0))],
        out_shape=[jax.ShapeDtypeStruct((s, 512), F32), jax.ShapeDtypeStruct((NH // 2, s // HG_BLOCK, LANE, LANE), F32)],
        scratch_shapes=[pltpu.VMEM((HG_PAIRS, LANE, LANE), F32)],
        compiler_params=_params(("parallel", "arbitrary")),
    )(proj, proj, proj, lbl)


def _group_sum(v):
    low = lax.broadcasted_iota(jnp.int32, (v.shape[0], LANE), 1) < V_DIM
    blocks = []
    for b in range(v.shape[1] // LANE):
        blk = v[:, b * LANE:(b + 1) * LANE]
        s_low = jnp.sum(jnp.where(low, blk, 0.0), axis=-1, keepdims=True)
        s_high = jnp.sum(jnp.where(low, 0.0, blk), axis=-1, keepdims=True)
        blocks.append(jnp.where(low, s_low, s_high))
    return jnp.concatenate(blocks, axis=1)


def _dsilu(z, sg):
    return sg * (1.0 + z * (1.0 - sg))


def _mid(proj, attn, o_raw, x, tgt, g_hg, b_gate, g_post, wa, wb, w_out):
    s = x.shape[0]

    def body(attn_ref, ga_ref, o_ref, gb_ref, mg_ref, x_ref, t_ref, ghg_ref, bg_ref, gp_ref, wa_ref, wb_ref, wo_ref,
             loss_ref, dout_ref, dattn_ref, dga_ref, dor_ref, dgb_ref, dmg_ref, dwo_ref, dwa_ref, dwb_ref, dgp_ref, dbg_ref, dghg_ref):
        @pl.when(pl.program_id(0) == 0)
        def _():
            for rf in (loss_ref, dwo_ref, dwa_ref, dwb_ref, dgp_ref, dbg_ref, dghg_ref):
                rf[...] = jnp.zeros_like(rf)

        attn, za, orw, zb = attn_ref[...], ga_ref[...], o_ref[...], gb_ref[...]
        ghg, gp = ghg_ref[...], gp_ref[...]
        sga, sgb = _sigmoid(za), _sigmoid(zb)
        sa, sb = za * sga, zb * sgb
        ga = attn * sa
        rh = lax.rsqrt(_group_sum(orw * orw) * (1.0 / V_DIM) + EPS)
        on = (orw * rh) * ghg
        gb = on * sb
        ga_b, gb_b = ga.astype(BF16), gb.astype(BF16)
        ya = _mm(ga_b, wa_ref[...])
        yb = _mm(gb_b, wb_ref[...])
        gates = _sigmoid(mg_ref[...] + bg_ref[...])
        g0, g1 = gates[:, :D], gates[:, D:]
        m_b = (g0 * ya + g1 * yb).astype(BF16)
        y = _mm(m_b, wo_ref[...])
        ry = lax.rsqrt(jnp.mean(y * y, axis=-1, keepdims=True) + EPS)
        out = x_ref[...] + (y * ry) * gp
        err = out - t_ref[...]
        loss_ref[...] += 0.5 * jnp.sum(jnp.mean(err * err, axis=-1, keepdims=True), axis=0, keepdims=True)
        dout = err * (1.0 / D)
        dout_ref[...] = dout
        dgp_ref[...] += jnp.sum(dout * (y * ry), axis=0, keepdims=True)
        dgy = dout * gp
        dy = ry * dgy - y * (ry * ry * ry) * jnp.mean(y * dgy, axis=-1, keepdims=True)
        dy_b = dy.astype(BF16)
        dwo_ref[...] += _mm_tn(m_b, dy_b)
        dm = _mm_nt(dy_b, wo_ref[...])
        dya, dyb = dm * g0, dm * g1
        dg0, dg1 = dm * ya, dm * yb
        dmg = jnp.concatenate([dg0 * g0 * (1.0 - g0), dg1 * g1 * (1.0 - g1)], axis=1)
        dmg_ref[...] = dmg.astype(BF16)
        dbg_ref[...] += jnp.sum(dmg, axis=0, keepdims=True)
        dya_b, dyb_b = dya.astype(BF16), dyb.astype(BF16)
        dwa_ref[...] += _mm_tn(ga_b, dya_b)
        dwb_ref[...] += _mm_tn(gb_b, dyb_b)
        dga = _mm_nt(dya_b, wa_ref[...])
        dgb = _mm_nt(dyb_b, wb_ref[...])
        dattn_ref[...] = dga * sa
        dga_ref[...] = (dga * attn * _dsilu(za, sga)).astype(BF16)
        dgb_ref[...] = (dgb * on * _dsilu(zb, sgb)).astype(BF16)
        don = dgb * sb
        dghg_ref[...] += jnp.sum(don * (orw * rh), axis=0, keepdims=True)
        dgo = don * ghg
        dor_ref[...] = rh * dgo - orw * (rh * rh * rh) * (_group_sum(orw * dgo) * (1.0 / V_DIM))

    row = lambda w, j=0: pl.BlockSpec((TM_MID, w), lambda i: (i, j))
    full = lambda a: pl.BlockSpec(a.shape, lambda i: (0,) * a.ndim)
    acc = lambda shape: pl.BlockSpec(shape, lambda i: (0, 0))
    sds = jax.ShapeDtypeStruct
    return pl.pallas_call(
        body, name="mid", grid=(s // TM_MID,),
        in_specs=[row(512), row(512, P_GA // 512), row(512), row(512, P_GB // 512), row(2048, P_MERGE // 2048), row(D), row(D),
                  full(g_hg), full(b_gate), full(g_post), full(wa), full(wb), full(w_out)],
        out_specs=[acc((1, 1)), row(D), row(512), row(512), row(512), row(512), row(2048),
                   acc((D, D)), acc((512, D)), acc((512, D)), acc((1, D)), acc((1, 2048)), acc((1, 512))],
        out_shape=[sds((1, 1), F32), sds((s, D), F32), sds((s, 512), F32), sds((s, 512), BF16), sds((s, 512), F32), sds((s, 512), BF16),
                   sds((s, 2048), BF16), sds((D, D), F32), sds((512, D), F32), sds((512, D), F32), sds((1, D), F32),
                   sds((1, 2048), F32), sds((1, 512), F32)],
        compiler_params=_params(("arbitrary",)),
    )(attn, proj, o_raw, proj, proj, x, tgt, g_hg, b_gate, g_post, wa, wb, w_out)


def _attn_bwd(q, k, vv, attn, dattn, lse, token):
    s = q.shape[1]
    nt = s // TQ
    scale = 1.0 / math.sqrt(QK_NOPE + QK_ROPE)

    def body(q_ref, k_ref, v_ref, o_ref, do_ref, lse_ref, token_ref, dq_ref, dk_ref, dv_ref, do_s, delta_s):
        j = pl.program_id(1)

        @pl.when(j == 0)
        def _():
            dq_ref[...] = jnp.zeros_like(dq_ref)
            lane = lax.broadcasted_iota(jnp.int32, (TQ, LANE), 1)

            @pl.loop(0, nt)
            def _(i):
                rows = pl.ds(pl.multiple_of(i * TQ, TQ), TQ)
                do, o = do_ref[rows, :], o_ref[rows, :]
                for hh in range(2):
                    doh = jnp.where((lane < 64) if hh == 0 else (lane >= 64), do, 0.0)
                    do_s[hh, rows, :] = doh.astype(BF16)
                    delta_s[hh, rows, :] = jnp.broadcast_to(jnp.sum(doh * o, axis=-1, keepdims=True), (TQ, LANE))

        kjs, vjs = (k_ref[0], k_ref[1]), (v_ref[0], v_ref[1])

        def tile(hh, start, size, kj, vj, diag):
            rows = pl.ds(pl.multiple_of(start, size), size)
            wide = lambda a: jnp.concatenate([a] * (kj.shape[0] // LANE), axis=1)
            qi, do_b = q_ref[hh, rows, :], do_s[hh, rows, :]
            p = jnp.exp2(_mm_nt(qi, kj) * QK_SCALE2 - wide(lse_ref[hh, rows, :]))
            if diag:
                p = jnp.where(_diag_visible(size), p, 0.0)
            dv = _mm_tn(do_b, p.astype(BF16))
            ds_b = (p * (_mm_nt(do_b, vj) - wide(delta_s[hh, rows, :]))).astype(BF16)
            dk = _mm_tn(qi, ds_b)
            dq_ref[hh, rows, :] += _mm(ds_b, kj)
            return dk, dv

        def step(i, carry):
            new = [tile(hh, i * TQ, TQ, kjs[hh], vjs[hh], False) for hh in range(2)]
            return tuple((carry[hh][0] + new[hh][0], carry[hh][1] + new[hh][1]) for hh in range(2))

        def diagonal(hh):
            k0, k1, v0, v1 = kjs[hh][:HQ], kjs[hh][HQ:], vjs[hh][:HQ], vjs[hh][HQ:]
            a = tile(hh, j * TQ, HQ, k0, v0, True)
            b = tile(hh, j * TQ + HQ, HQ, k0, v0, False)
            c = tile(hh, j * TQ + HQ, HQ, k1, v1, True)
            return jnp.concatenate([a[0] + b[0], c[0]], axis=1), jnp.concatenate([a[1] + b[1], c[1]], axis=1)

        carry = lax.fori_loop(j + 1, nt, step, (diagonal(0), diagonal(1)))
        for hh in range(2):
            dk_ref[hh] = carry[hh][0].T * scale
            dv_ref[hh] = carry[hh][1].T

        @pl.when(j == nt - 1)
        def _():
            dq_ref[...] = dq_ref[...] * scale

    whole = pl.BlockSpec((2, s, LANE), lambda p, j: (p, 0, 0))
    tile_spec = pl.BlockSpec((2, TQ, LANE), lambda p, j: (p, j, 0))
    cols = pl.BlockSpec((s, LANE), lambda p, j: (0, p))
    hs = jax.ShapeDtypeStruct((NH, s, LANE), F32)
    return pl.pallas_call(
        body, name="attn_bwd", grid=(NH // 2, nt),
        in_specs=[whole, tile_spec, tile_spec, cols, cols, whole, pl.BlockSpec((8, LANE), lambda p, j: (0, 0))],
        out_specs=[whole, tile_spec, tile_spec],
        out_shape=[hs, hs, hs],
        scratch_shapes=[pltpu.VMEM((2, s, LANE), BF16), pltpu.VMEM((2, s, LANE), F32)],
        compiler_params=_params(("parallel", "arbitrary")),
    )(q, k, vv, attn, dattn, lse, token)


def _hgrn_bwd(proj, lbl, states, do_raw):
    s = proj.shape[0]
    nt = s // TH
    nch = TH // HG_BLOCK

    def body(hq_ref, hf_ref, hi_ref, lbl_ref, st_ref, do_ref, dh_ref, dlbl_ref, dst, dlb):
        step = pl.program_id(1)

        @pl.when(step == 0)
        def _():
            dst[...] = jnp.zeros_like(dst)
            dlb[...] = jnp.zeros_like(dlb)

        m0, bd = _pair_masks()
        for u in range(HG_PAIRS):
            lanes = slice(u * LANE, (u + 1) * LANE)
            lb = _lower_bound(lbl_ref[:, lanes])
            gt = _hgrn_gates(hq_ref[:, lanes], hf_ref[:, lanes], lb)
            do = do_ref[:, lanes]
            qd, ki, ke = gt["qd"], gt["ki"], gt["ke"]
            v_b, do_b = hi_ref[:, lanes].astype(BF16), do.astype(BF16)
            qd_b, ki_b, ke_b = qd.astype(BF16), ki.astype(BF16), ke.astype(BF16)
            dv = jnp.zeros((TH, LANE), F32)
            dqd = jnp.zeros((TH, LANE), F32)
            dki = jnp.zeros((TH, LANE), F32)
            for hh in range(2):
                mh = m0 if hh == 0 else jnp.logical_not(m0)
                a_b = jnp.where(gt["tri"], _mm_nt(jnp.where(mh, qd, 0.0).astype(BF16), ki_b), 0.0).astype(BF16)
                doh_b = jnp.where(mh, do, 0.0).astype(BF16)
                da_b = jnp.where(gt["tri"], _mm_nt(doh_b, v_b), 0.0).astype(BF16)
                dv = dv + _mm_tn(a_b, doh_b)
                dqd = jnp.where(mh, _mm(da_b, ki_b), dqd)
                dki = jnp.where(mh, _mm_tn(da_b, qd_b), dki)
            fed = _bmm_tn(_chunks(do_b), _chunks(qd_b))
            decay = jnp.exp(gt["total"])
            ds, leaving = dst[u], [None] * nch
            for n in reversed(range(nch)):
                leaving[n] = ds
                ds = decay[n] * ds + jnp.where(bd, fed[n], 0.0)
            dst[u] = ds
            leaving = jnp.stack(leaving)
            entering = st_ref[u]
            leaving_b = leaving.astype(BF16)
            dke3 = _bmm_nn(_chunks(v_b), leaving_b)
            dv = dv + _bmm_nt(_chunks(ke_b), leaving_b).reshape(TH, LANE)
            dqd = dqd + _bmm_nn(_chunks(do_b), entering.astype(BF16)).reshape(TH, LANE)
            dke = dke3.reshape(TH, LANE)
            dlast = (jnp.sum(dke3 * _chunks(ke), axis=1, keepdims=True)
                     + jnp.sum(leaving * entering, axis=1, keepdims=True) * decay)
            dk = dki * gt["ei"] + dke * gt["ee"]
            dcum = dqd * qd - dki * ki - dke * ke
            dg = _chunk_cumsum(dcum, reverse=True) + jnp.broadcast_to(dlast, (nch, HG_BLOCK, LANE)).reshape(TH, LANE)
            sig = gt["sig"]
            df = dg / gt["f"] - dk
            dlb[:, lanes] += jnp.sum(df * (1.0 - sig), axis=0, keepdims=True)
            dh_ref[0, :, lanes] = (dqd * gt["e"]).astype(BF16)
            dh_ref[1, :, lanes] = ((df * (1.0 - lb)) * sig * (1.0 - sig)).astype(BF16)
            dh_ref[2, :, lanes] = dv.astype(BF16)

        @pl.when(step == nt - 1)
        def _():
            lb = _lower_bound(lbl_ref[...])
            da0 = dlb[...] * lb * (1.0 - lb)
            dlbl_ref[...] = jnp.concatenate([da0, -da0], axis=0)

    wide = HG_PAIRS * LANE
    col = lambda base: pl.BlockSpec((TH, wide), lambda p, i: (nt - 1 - i, base // wide + p))
    tile = pl.BlockSpec((TH, wide), lambda p, i: (nt - 1 - i, p))
    sds = jax.ShapeDtypeStruct
    return pl.pallas_call(
        body, name="hgrn_bwd", grid=(NH // 2 // HG_PAIRS, nt),
        in_specs=[col(P_HQ), col(P_HF), col(P_HI), pl.BlockSpec((2, wide), lambda p, i: (0, p)),
                  pl.BlockSpec((HG_PAIRS, nch, LANE, LANE), lambda p, i: (p, nt - 1 - i, 0, 0)), tile],
        out_specs=[pl.BlockSpec((3, TH, wide), lambda p, i: (0, nt - 1 - i, p)), pl.BlockSpec((2, wide), lambda p, i: (0, p))],
        out_shape=[sds((3, s, 512), BF16), sds((2, 512), F32)],
        scratch_shapes=[pltpu.VMEM((HG_PAIRS, LANE, LANE), F32), pltpu.VMEM((1, wide), F32)],
        compiler_params=_params(("parallel", "arbitrary")),
    )(proj, proj, proj, lbl, states, do_raw)


def _norm_rows_bwd(v, r, g, dn):
    dgv = dn * g
    return r * dgv - v * (r * r * r) * jnp.mean(v * dgv, axis=-1, keepdims=True)


def _qkv_bwd(proj, dq, dk, dvv, g_q, g_kv, w_uq_p, w_k_p, w_v_p, rc, rs1, rs2):
    s = proj.shape[0]

    def body(cq_ref, ckv_ref, dq_ref, dk_ref, dv_ref, gq_ref, gkv_ref, wq_ref, wk_ref, wv_ref, c_ref, s1_ref, s2_ref,
             dcq_ref, dckv_ref, dkpe_ref, dwq_ref, dwk_ref, dwv_ref, dgq_ref, dgkv_ref):
        @pl.when(pl.program_id(0) == 0)
        def _():
            for rf in (dwq_ref, dwk_ref, dwv_ref, dgq_ref, dgkv_ref):
                rf[...] = jnp.zeros_like(rf)

        c, s1, s2 = c_ref[...], s1_ref[...], s2_ref[...]
        cq, ckv = cq_ref[...], ckv_ref[...]
        gq, gkv = gq_ref[...], gkv_ref[...]
        cqn, rq = _norm_rows(cq, gq)
        ckvn, rkv = _norm_rows(ckv, gkv)
        cqn_b, ckvn_b = cqn.astype(BF16), ckvn.astype(BF16)
        dqf = jnp.concatenate([_rope_t(dq_ref[h], c, s1, s2) for h in range(NH)], axis=1).astype(BF16)
        dkf = jnp.concatenate([dk_ref[h] for h in range(NH)], axis=1).astype(BF16)
        dvf = jnp.concatenate([dv_ref[h] for h in range(NH)], axis=1).astype(BF16)
        dkpe = dk_ref[0]
        for h in range(1, NH):
            dkpe = dkpe + dk_ref[h]
        lane = lax.broadcasted_iota(jnp.int32, (TM, LANE), 1)
        dkpe = jnp.where((lane >= QK_NOPE) & (lane < QK_NOPE + QK_ROPE), dkpe, 0.0)
        dkpe_ref[...] = _rope_t(dkpe, c, s1, s2).astype(BF16)
        dwq_ref[...] += _mm_tn(cqn_b, dqf)
        dwk_ref[...] += _mm_tn(ckvn_b, dkf)
        dwv_ref[...] += _mm_tn(ckvn_b, dvf)
        dcqn = _mm_nt(dqf, wq_ref[...])
        dckvn = _mm_nt(dkf, wk_ref[...]) + _mm_nt(dvf, wv_ref[...])
        dgq_ref[...] += jnp.sum(dcqn * (cq * rq), axis=0, keepdims=True)
        dgkv_ref[...] += jnp.sum(dckvn * (ckv * rkv), axis=0, keepdims=True)
        dcq_ref[...] = _norm_rows_bwd(cq, rq, gq, dcqn).astype(BF16)
        dckv_ref[...] = _norm_rows_bwd(ckv, rkv, gkv, dckvn).astype(BF16)

    row = lambda w, j=0: pl.BlockSpec((TM, w), lambda i: (i, j))
    full = lambda a: pl.BlockSpec(a.shape, lambda i: (0,) * a.ndim)
    acc = lambda shape: pl.BlockSpec(shape, lambda i: (0, 0))
    heads = pl.BlockSpec((NH, TM, LANE), lambda i: (0, i, 0))
    sds = jax.ShapeDtypeStruct
    return pl.pallas_call(
        body, name="qkv_bwd", grid=(s // TM,),
        in_specs=[row(Q_LORA, P_CQ // Q_LORA), row(KV_LORA, P_CKV // KV_LORA), heads, heads, heads,
                  full(g_q), full(g_kv), full(w_uq_p), full(w_k_p), full(w_v_p), row(LANE), row(LANE), row(LANE)],
        out_specs=[row(Q_LORA), row(KV_LORA), row(LANE), acc((Q_LORA, D)), acc((KV_LORA, D)), acc((KV_LORA, D)),
                   acc((1, Q_LORA)), acc((1, KV_LORA))],
        out_shape=[sds((s, Q_LORA), BF16), sds((s, KV_LORA), BF16), sds((s, LANE), BF16), sds((Q_LORA, D), F32),
                   sds((KV_LORA, D), F32), sds((KV_LORA, D), F32), sds((1, Q_LORA), F32), sds((1, KV_LORA), F32)],
        compiler_params=_params(("arbitrary",)),
    )(proj, proj, dq, dk, dvv, g_q, g_kv, w_uq_p, w_k_p, w_v_p, rc, rs1, rs2)


def _front_bwd(x, dout, dmg, dga, dh3, dgb, dcq, dckv, dkpe, g_pre, w_in_pt, token):
    s = x.shape[0]

    def body(x_ref, do_ref, dmg_ref, dga_ref, dh3_ref, dgb_ref, dcq_ref, dckv_ref, dkpe_ref, g_ref, w_ref, token_ref, gx_ref, dg_ref):
        @pl.when(pl.program_id(0) == 0)
        def _():
            dg_ref[...] = jnp.zeros_like(dg_ref)

        xv, g = x_ref[...], g_ref[...]
        _, r = _norm_rows(xv, g)
        pieces = ((dmg_ref[...], P_MERGE), (dga_ref[...], P_GA), (dh3_ref[0], P_HQ), (dh3_ref[1], P_HF), (dh3_ref[2], P_HI),
                  (dgb_ref[...], P_GB), (dcq_ref[...], P_CQ), (dckv_ref[...], P_CKV), (dkpe_ref[...], P_KPE))
        dh = jnp.zeros((TM, D), F32)
        for piece, off in pieces:
            dh = dh + _mm(piece, w_ref[off:off + piece.shape[1], :])
        dg_ref[...] += jnp.sum(dh * (xv * r), axis=0, keepdims=True)
        gx_ref[...] = do_ref[...] + _norm_rows_bwd(xv, r, g, dh)

    row = lambda w: pl.BlockSpec((TM, w), lambda i: (i, 0))
    full = lambda a: pl.BlockSpec(a.shape, lambda i: (0,) * a.ndim)
    sds = jax.ShapeDtypeStruct
    return pl.pallas_call(
        body, name="front_bwd", grid=(s // TM,),
        in_specs=[row(D), row(D), row(2048), row(512), pl.BlockSpec((3, TM, 512), lambda i: (0, i, 0)), row(512), row(Q_LORA),
                  row(KV_LORA), row(LANE), full(g_pre), full(w_in_pt), pl.BlockSpec(memory_space=pl.ANY)],
        out_specs=[row(D), pl.BlockSpec((1, D), lambda i: (0, 0))],
        out_shape=[sds((s, D), F32), sds((1, D), F32)],
        compiler_params=_params(("arbitrary",)),
    )(x, dout, dmg, dga, dh3, dgb, dcq, dckv, dkpe, g_pre, w_in_pt, token)


TK_GRAD = 1024


def _win_grad(h, pieces, name):
    s = h.shape[0]
    n = len(pieces)

    def body(h_ref, *refs):
        @pl.when(pl.program_id(0) == 0)
        def _():
            for o_ref in refs[n:]:
                o_ref[...] = jnp.zeros_like(o_ref)

        hv = h_ref[...]
        for d_ref, o_ref in zip(refs[:n], refs[n:]):
            if len(d_ref.shape) == 3:
                for k in range(d_ref.shape[0]):
                    o_ref[k] += _mm_tn(d_ref[k], hv)
            else:
                o_ref[...] += _mm_tn(d_ref[...], hv)

    def in_spec(p):
        if p.ndim == 3:
            return pl.BlockSpec((p.shape[0], TK_GRAD, p.shape[2]), lambda kk: (0, kk, 0))
        return pl.BlockSpec((TK_GRAD, p.shape[1]), lambda kk: (kk, 0))

    out_shapes = [(p.shape[0], p.shape[2], D) if p.ndim == 3 else (p.shape[1], D) for p in pieces]
    return pl.pallas_call(
        body, name=name, grid=(s // TK_GRAD,),
        in_specs=[pl.BlockSpec((TK_GRAD, D), lambda kk: (kk, 0))] + [in_spec(p) for p in pieces],
        out_specs=[pl.BlockSpec(sh, lambda kk, nd=len(sh): (0,) * nd) for sh in out_shapes],
        out_shape=[jax.ShapeDtypeStruct(sh, F32) for sh in out_shapes],
        compiler_params=_params(("arbitrary",)),
    )(h, *pieces)


def _pad_win_t(w_in_t):
    z = lambda n: jnp.zeros((n, w_in_t.shape[1]), w_in_t.dtype)
    sl = lambda o, n: w_in_t[o:o + n]
    return jnp.concatenate([sl(O_MERGE, 2048), sl(O_GA, 512), sl(O_HQ, 512), sl(O_HF, 512), sl(O_HI, 512), sl(O_GB, 512),
                            sl(O_CQ, Q_LORA), sl(O_CKV, KV_LORA), z(64), sl(O_KPE, QK_ROPE), z(32)], axis=0)


def _pad_wuq(w_uq):
    w = w_uq.reshape(Q_LORA, NH, QK_NOPE + QK_ROPE)
    return jnp.pad(w, ((0, 0), (0, 0), (0, LANE - QK_NOPE - QK_ROPE))).reshape(Q_LORA, NH * LANE)


def _unpad_wuq(g):
    return g.reshape(Q_LORA, NH, LANE)[:, :, :QK_NOPE + QK_ROPE].reshape(Q_LORA, NH * (QK_NOPE + QK_ROPE))


def _pad_wukv(w_ukv):
    w = w_ukv.reshape(KV_LORA, NH, QK_NOPE + V_DIM)
    w_k = jnp.pad(w[:, :, :QK_NOPE], ((0, 0), (0, 0), (0, LANE - QK_NOPE))).reshape(KV_LORA, NH * LANE)
    wv = w[:, :, QK_NOPE:].reshape(KV_LORA, NH // 2, 2, 1, V_DIM)
    eye = jnp.eye(2, dtype=w.dtype).reshape(1, 1, 2, 2, 1)
    return w_k, (wv * eye).reshape(KV_LORA, NH * LANE)


def _unpad_wukv(gk, gv):
    gk = gk.reshape(KV_LORA, NH, LANE)[:, :, :QK_NOPE]
    gv = gv.reshape(KV_LORA, NH // 2, 2, 2, V_DIM)
    gv = jnp.stack([gv[:, :, 0, 0], gv[:, :, 1, 1]], axis=2).reshape(KV_LORA, NH, V_DIM)
    return jnp.concatenate([gk, gv], axis=-1).reshape(KV_LORA, NH * (QK_NOPE + V_DIM))


def _local_step(x, tgt, g_pre, w_in_t, b_gate, g_q, g_kv, lb_logits, g_hgrn, g_post, weights, exchange=None):
    s = x.shape[0]
    w_in_p = _pad_win_t(w_in_t)
    rc, rs1, rs2 = _rope_tables(s)
    g_hg = jnp.tile(g_hgrn, (1, NH))

    proj, h = _front_fwd(x, g_pre, w_in_p, weights.tokens)
    w_uq, w_ukv = weights.qkv(h)
    w_uq_p = _pad_wuq(w_uq)
    w_k_p, w_v_p = _pad_wukv(w_ukv)
    q, k, vv = _qkv_fwd(proj, g_q, g_kv, w_uq_p, w_k_p, w_v_p, rc, rs1, rs2)
    attn, lse = _attn_fwd(q, k, vv)
    o_raw, states = _hgrn_fwd(proj, lb_logits)
    wa, wb, w_out = weights.mid(o_raw)
    (loss, dout, dattn, dga, dor, dgb, dmg, d_wout, d_wa, d_wb, d_gpost, d_bgate, d_ghg) = _mid(
        proj, attn, o_raw, x, tgt, g_hg, b_gate, g_post, wa, wb, w_out)
    w_mg, w_ga, w_gb = _win_grad(h, [dmg, dga, dgb], "win_grad_mid")
    dh3, d_lbl = _hgrn_bwd(proj, lb_logits, states, dor)
    (w_h3,) = _win_grad(h, [dh3], "win_grad_hgrn")
    d_win_rest = jnp.concatenate([w_ga, w_h3[0], w_h3[1], w_h3[2], w_gb, w_mg], axis=0)
    early = dict(w_in_rest=d_win_rest, w_branch_a=d_wa, w_branch_b=d_wb, w_out=d_wout)
    token = exchange.start_early(early) if exchange else jnp.zeros((8, LANE), F32)
    dq, dk, dvv = _attn_bwd(q, k, vv, attn, dattn, lse, token)
    dcq, dckv, dkpe, d_wuq_p, d_wk_p, d_wv_p, d_gq, d_gkv = _qkv_bwd(proj, dq, dk, dvv, g_q, g_kv, w_uq_p, w_k_p, w_v_p, rc, rs1, rs2)
    w_cq, w_ckv, w_kpe = _win_grad(h, [dcq, dckv, dkpe], "win_grad_qkv")
    d_win_qkv = jnp.concatenate([w_cq, w_ckv, w_kpe[64:64 + QK_ROPE]], axis=0)
    late = dict(w_in_qkv=d_win_qkv, w_uq=_unpad_wuq(d_wuq_p), w_ukv=_unpad_wukv(d_wk_p, d_wv_p))
    token = exchange.start_late(late) if exchange else jnp.zeros((8, LANE), F32)
    grad_x, d_gpre = _front_bwd(x, dout, dmg, dga, dh3, dgb, dcq, dckv, dkpe, g_pre, w_in_p, token)
    vec_grads = dict(g_pre=d_gpre, b_gate=d_bgate, g_q=d_gq, g_kv=d_gkv, lb_logits=d_lbl, g_hgrn=d_ghg, g_post=d_gpost)
    return loss, grad_x, dict(early, **late), vec_grads


SHARD_SHAPES = (("w_in", (1416, 1024)), ("w_uq", (192, 768)), ("w_ukv", (256, 256)), ("w_branch_a", (512, 256)),
                ("w_branch_b", (512, 256)), ("w_out", (256, 1024)))
BIG = tuple(n for n, _ in SHARD_SHAPES)
ROW_SHARDED = ("w_in", "w_uq", "w_out")
GATHER_SPLIT_AXIS = dict(w_in=1, w_uq=0, w_ukv=0, w_branch_a=0, w_branch_b=0, w_out=0)
N_CHIPS = 4
QKV_ROWS = Q_LORA + KV_LORA + QK_ROPE
W_IN_GATHER_CUT = 704


def _to_block(name, a):
    return a[0].T if name == "w_in" else a[0]


def _from_block(name, a):
    return a.T[None] if name == "w_in" else a[None]
VEC_ROWS = (("g_pre", 0, 1024), ("b_gate", 1, 2048), ("g_q", 2, 768), ("g_kv", 3, 256), ("g_hgrn", 6, 64), ("g_post", 7, 1024))
VEC_LB_ROW = 4
VEC_SHAPE = (8, 2048)


def _split_by_chip(name, g):
    a, b = dict(SHARD_SHAPES)[name]
    return g.reshape(N_CHIPS, a, b) if name in ROW_SHARDED else g.reshape(a, N_CHIPS, b).transpose(1, 0, 2)


def _join_chips(name, w):
    a, b = dict(SHARD_SHAPES)[name]
    return w.reshape(N_CHIPS * a, b) if name in ROW_SHARDED else w.transpose(1, 0, 2).reshape(a, N_CHIPS * b)


MESH = pl.DeviceIdType.MESH
HBM = pl.BlockSpec(memory_space=pltpu.HBM)


def _mesh_place():
    x, y, c = lax.axis_index("x"), lax.axis_index("y"), lax.axis_index("c")
    return x, y, c, 2 * x + y, [(1 - x, y), (x, 1 - y), (1 - x, 1 - y)]


def _remote(src, dst, send_sems, recv_sems, k, to):
    return pltpu.make_async_remote_copy(src_ref=src, dst_ref=dst, send_sem=send_sems.at[k], recv_sem=recv_sems.at[k],
                                        device_id=to, device_id_type=MESH)


def _gather_weights(shards, split_axes):
    n = len(shards)

    def body(*refs):
        srcs, outs = refs[:n], refs[n:2 * n]
        ici_send, ici_recv, d2d_send, d2d_recv, local_sems = refs[2 * n:]
        x, y, c, me, chips = _mesh_place()
        sibling = (x, y, 1 - c)

        def half(ref, k, which):
            size = shards[k].shape[split_axes[k]] // 2
            part = pl.ds(pl.multiple_of(which * size, size), size)
            return ref.at[part] if split_axes[k] == 0 else ref.at[:, part]

        own = [pltpu.make_async_copy(srcs[k], outs[k].at[me], local_sems.at[k]) for k in range(n)]
        for cp in own:
            cp.start()
        started = []
        for k in range(n):
            for j, (px, py) in enumerate(chips):
                cp = _remote(half(srcs[k], k, c), half(outs[k].at[me], k, c), ici_send, ici_recv, 3 * k + j, (px, py, c))
                cp.start()
                started.append(cp)
        for k in range(n):
            for j, (px, py) in enumerate(chips):
                landed = half(outs[k].at[2 * px + py], k, c)
                _remote(landed, landed, ici_send, ici_recv, 3 * k + j, (px, py, c)).wait_recv()
                cp = _remote(landed, landed, d2d_send, d2d_recv, 3 * k + j, sibling)
                cp.start()
                started.append(cp)
        for k in range(n):
            for j, (px, py) in enumerate(chips):
                other = half(outs[k].at[2 * px + py], k, 1 - c)
                _remote(other, other, d2d_send, d2d_recv, 3 * k + j, sibling).wait_recv()
        for cp in started:
            cp.wait_send()
        for cp in own:
            cp.wait()

    sems = pltpu.SemaphoreType.DMA((3 * n,))
    return pl.pallas_call(
        body, name="gather_weights", in_specs=[HBM] * n, out_specs=[HBM] * n,
        out_shape=[jax.ShapeDtypeStruct((N_CHIPS,) + s.shape, s.dtype) for s in shards],
        scratch_shapes=[sems, sems, sems, sems, pltpu.SemaphoreType.DMA((n,))],
        compiler_params=pltpu.CompilerParams(has_side_effects=True),
    )(*shards)


def _sibling_exchange(srcs, name, after=None):
    n = len(srcs)
    extra = [] if after is None else [after]

    def body(*refs):
        src_refs, outs = refs[:n], refs[n + len(extra):2 * n + len(extra)]
        send_sems, recv_sems = refs[2 * n + len(extra):]
        sibling = (lax.axis_index("x"), lax.axis_index("y"), 1 - lax.axis_index("c"))
        copies = [_remote(src_refs[k], outs[k], send_sems, recv_sems, k, sibling) for k in range(n)]
        for cp in copies:
            cp.start()
        for cp in copies:
            cp.wait()

    sems = pltpu.SemaphoreType.DMA((n,))
    return pl.pallas_call(
        body, name=name, in_specs=[HBM] * n + [pl.BlockSpec(memory_space=pl.ANY)] * len(extra), out_specs=[HBM] * n,
        out_shape=[jax.ShapeDtypeStruct(s.shape, s.dtype) for s in srcs],
        scratch_shapes=[sems, sems],
        compiler_params=pltpu.CompilerParams(has_side_effects=True),
    )(*srcs, *extra)


SEM = pl.BlockSpec(memory_space=pltpu.SEMAPHORE)
DATAFLOW = pltpu.SideEffectType.DATAFLOW_SIDE_EFFECTING


def _exchange_copies(srcs, to_first, src_refs, land_refs, send_sems, recv_sems):
    x, y, c, me, chips = _mesh_place()
    n = len(srcs)
    sends, recvs = [], []
    for k in range(n):
        if k in to_first:
            base = 3 * n + 4 * to_first.index(k)
            sends.append((me != 0, pltpu.make_async_remote_copy(
                src_ref=src_refs[k], dst_ref=land_refs[k].at[me], send_sem=send_sems.at[base], recv_sem=recv_sems.at[base + me],
                device_id=(0, 0, c), device_id_type=MESH)))
            for s in range(1, N_CHIPS):
                recvs.append((me == 0, pltpu.make_async_remote_copy(
                    src_ref=src_refs[k], dst_ref=land_refs[k].at[s], send_sem=send_sems.at[base], recv_sem=recv_sems.at[base + s],
                    device_id=(s // 2, s % 2, c), device_id_type=MESH)))
        else:
            slab = (lambda t, k=k: src_refs[k]) if srcs[k].ndim == 2 else (lambda t, k=k: src_refs[k].at[t])
            for j, (px, py) in enumerate(chips):
                sends.append((None, _remote(slab(2 * px + py), land_refs[k].at[me], send_sems, recv_sems, 3 * k + j, (px, py, c))))
                recvs.append((None, _remote(slab(me), land_refs[k].at[2 * px + py], send_sems, recv_sems, 3 * k + j, (px, py, c))))
    return sends, recvs


def _when(pred, fn):
    if pred is None:
        fn()
    else:
        pl.when(pred)(fn)


def _exchange_start(srcs, to_first, name, after=None):
    n = len(srcs)
    n_sems = 3 * n + 4 * len(to_first)
    lands = [lax.empty((N_CHIPS,) + s.shape[-2:], s.dtype) for s in srcs]
    extra = [] if after is None else [after]

    def body(*refs):
        src_refs, land_refs = refs[:n], refs[n:2 * n]
        send_sems, recv_sems, token = refs[2 * n + len(extra)], refs[2 * n + len(extra) + 1], refs[-1]
        sends, _ = _exchange_copies(srcs, to_first, src_refs, land_refs, send_sems, recv_sems)
        for pred, cp in sends:
            _when(pred, cp.start)
        token[...] = jnp.zeros_like(token)

    hbm = lambda a: pltpu.HBM(a.shape, a.dtype)
    res = pl.pallas_call(
        body, name=name,
        out_shape=[pltpu.SemaphoreType.DMA((n_sems,)), pltpu.SemaphoreType.DMA((n_sems,))] + [hbm(a) for a in srcs + lands]
        + [jax.ShapeDtypeStruct((8, LANE), F32)],
        in_specs=[HBM] * (2 * n) + [pl.BlockSpec(memory_space=pl.ANY)] * len(extra),
        out_specs=[SEM, SEM] + [HBM] * (2 * n) + [pl.BlockSpec(memory_space=pltpu.VMEM)],
        input_output_aliases={i: 2 + i for i in range(2 * n)},
        compiler_params=pltpu.CompilerParams(has_side_effects=DATAFLOW),
    )(*[pltpu.with_memory_space_constraint(a, pltpu.HBM) for a in srcs + lands], *extra)
    return res[:-1], res[-1]


def _exchange_wait(srcs, to_first, started, after, name):
    n = len(srcs)
    send_sems, recv_sems, thru = started[0], started[1], started[2:]

    def body(*refs):
        src_refs, land_refs, send_ref, recv_ref = refs[:n], refs[n:2 * n], refs[2 * n], refs[2 * n + 1]
        sends, recvs = _exchange_copies(srcs, to_first, src_refs, land_refs, send_ref, recv_ref)
        for pred, cp in sends:
            _when(pred, cp.wait_send)
        for pred, cp in recvs:
            _when(pred, cp.wait_recv)

    res = pl.pallas_call(
        body, name=name, out_shape=[pltpu.HBM(a.shape, a.dtype) for a in thru],
        in_specs=[HBM] * (2 * n) + [SEM, SEM, pl.BlockSpec(memory_space=pl.ANY)], out_specs=[HBM] * (2 * n),
        input_output_aliases={i: i for i in range(2 * n)},
        compiler_params=pltpu.CompilerParams(has_side_effects=DATAFLOW),
    )(*thru, send_sems, recv_sems, after)
    return res[n:]


ROW_TILE = 256
COL_TILE = 256


def _block_tiling(a, b):
    if a <= ROW_TILE or a % ROW_TILE == 0:
        ta = min(a, ROW_TILE)
        return a // ta, (ta, b), lambda i: (i, 0)
    return b // COL_TILE, (a, COL_TILE), lambda i: (0, i)


def _sum_landed(land, own, name, first_land=None, first_own=None):
    _, a, b = land.shape
    steps, tile, at = _block_tiling(a, b)
    extra = first_land is not None

    def body(*refs):
        p_ref, own_ref, o_ref = refs[0], refs[1], refs[-1]
        me = 2 * lax.axis_index("x") + lax.axis_index("y")
        own = own_ref[...].astype(F32)
        slot = lambda t: jnp.where(me == t, own, p_ref[t].astype(F32))
        o_ref[...] = ((slot(0) + slot(1)) + slot(2)) + slot(3)
        if extra:
            fp_ref, fo_ref = refs[2], refs[3]
            r = fo_ref.shape[0]

            @pl.when(me == 0)
            def _():
                f = lambda t: fp_ref[t].astype(F32)
                rows = pl.ds(pl.multiple_of(lax.axis_index("c") * r, 8), r)
                o_ref[rows, :] += ((fo_ref[...].astype(F32) + f(1)) + f(2)) + f(3)

    in_specs = [pl.BlockSpec((N_CHIPS,) + tile, lambda i: (0,) + at(i)), pl.BlockSpec(tile, at)]
    args = [land, own]
    if extra:
        r = first_own.shape[0]
        assert tile[0] == a, "the extra rows need whole columns in a step"
        in_specs += [pl.BlockSpec((N_CHIPS, r, tile[1]), lambda i: (0,) + at(i)), pl.BlockSpec((r, tile[1]), at)]
        args += [first_land, first_own]
    return pl.pallas_call(
        body, name=name, grid=(steps,), in_specs=in_specs, out_specs=pl.BlockSpec(tile, at),
        out_shape=jax.ShapeDtypeStruct((a, b), F32), compiler_params=_params(("parallel",)),
    )(*args)


def _add_cast(a, b, name):
    def body(a_ref, b_ref, o_ref):
        o_ref[...] = (a_ref[...] + b_ref[...]).astype(BF16)

    return pl.pallas_call(body, name=name, out_shape=jax.ShapeDtypeStruct(a.shape, BF16),
                          compiler_params=_params(()))(a, b)


class _LaterWeights:
    QKV = ("w_uq", "w_ukv")
    MID = ("w_branch_a", "w_branch_b", "w_out")

    def __init__(self, blocks, after):
        self.blocks = blocks
        self.qkv_started, t1 = _exchange_start([blocks[n] for n in self.QKV], (), "weights_qkv_start", after)
        self.mid_started, t2 = _exchange_start([blocks[n] for n in self.MID], (), "weights_mid_start", after)
        self.tokens = [t1, t2]

    def _whole(self, names, started, after, name):
        landed = _exchange_wait([self.blocks[n] for n in names], (), started, after, name)
        me = 2 * lax.axis_index("x") + lax.axis_index("y")
        return [_join_chips(n, lax.dynamic_update_index_in_dim(land, self.blocks[n], me, 0)) for n, land in zip(names, landed)]

    def qkv(self, after):
        return self._whole(self.QKV, self.qkv_started, after, "weights_qkv_wait")

    def mid(self, after):
        return self._whole(self.MID, self.mid_started, after, "weights_mid_wait")


class _GradExchange:
    EARLY = ("w_in", "w_branch_a", "w_branch_b", "w_out")
    LATE = ("w_uq", "w_ukv")

    def __init__(self, state):
        self.state = state
        self.outs = {}

    @staticmethod
    def _own(slabs):
        return lax.dynamic_index_in_dim(slabs, 2 * lax.axis_index("x") + lax.axis_index("y"), axis=0, keepdims=False)

    def start_early(self, g):
        full = jnp.concatenate([jnp.zeros((QKV_ROWS, D), F32), g["w_in_rest"]], axis=0)
        g = dict(g, w_in=full)
        self.early = [_split_by_chip(n, g[n]).astype(BF16) for n in self.EARLY]
        self.early_started, token = _exchange_start(self.early, (), "grads_early_start")
        return token

    def start_late(self, g):
        self.early_landed = _exchange_wait(self.early, (), self.early_started, g["w_uq"], "grads_early_wait")
        half = QKV_ROWS // 2
        c = lax.axis_index("c")
        mine = lax.dynamic_slice_in_dim(g["w_in_qkv"], c * half, half, axis=0)
        (theirs,) = _sibling_exchange([lax.dynamic_slice_in_dim(g["w_in_qkv"], (1 - c) * half, half, axis=0)], "sibling_qkv_rows")
        self.late = [_split_by_chip(n, g[n]).astype(BF16) for n in self.LATE] + [_add_cast(mine, theirs, "add_qkv_rows")]
        self.late_started, token = _exchange_start(self.late, (2,), "grads_late_start")
        names = self.EARLY[1:]
        mine = [_sum_landed(land, self._own(slabs), "sum_" + n) for n, slabs, land in list(zip(self.EARLY, self.early, self.early_landed))[1:]]
        theirs = _sibling_exchange(mine, "sibling_early", after=token)
        for n, a, b in zip(names, mine, theirs):
            self.outs[n] = _adamw(a, b, *self.state[n], "adamw_" + n)
        return self.outs[names[-1]][0]

    def finish(self, after):
        late_landed = _exchange_wait(self.late, (2,), self.late_started, after, "grads_late_wait")
        sums = {"w_in": _sum_landed(self.early_landed[0], self._own(self.early[0]), "sum_w_in",
                                    first_land=late_landed[2], first_own=self.late[2])}
        for n, slabs, land in zip(self.LATE, self.late, late_landed):
            sums[n] = _sum_landed(land, self._own(slabs), "sum_" + n)
        return sums


def _adamw_math(g, w, m, v):
    nm = ADAM_B1 * m + (1.0 - ADAM_B1) * g
    nv = ADAM_B2 * v + (1.0 - ADAM_B2) * (g * g)
    m_hat = nm / (1.0 - ADAM_B1 ** ADAM_STEP)
    v_hat = nv / (1.0 - ADAM_B2 ** ADAM_STEP)
    return -ADAM_LR * (m_hat / (jnp.sqrt(v_hat) + ADAM_EPS) + ADAM_WD * w), nm, nv


def _adamw(p_mine, p_sibling, w, m, v, name):
    a, b = p_mine.shape
    steps, tile, at = _block_tiling(a, b)

    def body(a_ref, b_ref, w_ref, m_ref, v_ref, g_ref, d_ref, nm_ref, nv_ref):
        g = a_ref[...] + b_ref[...]
        g_ref[...] = g
        d_ref[...], nm_ref[...], nv_ref[...] = _adamw_math(g, w_ref[...], m_ref[...], v_ref[...])

    spec = pl.BlockSpec(tile, at)
    sds = jax.ShapeDtypeStruct((a, b), F32)
    return pl.pallas_call(
        body, name=name, grid=(steps,), in_specs=[spec] * 5, out_specs=[spec] * 4, out_shape=[sds] * 4,
        compiler_params=_params(("parallel",)),
    )(p_mine, p_sibling, w, m, v)


LOSS_AT = (2, 1024)


def _vec_pack(vg, loss):
    names = [n for n, _, _ in VEC_ROWS]

    def body(*refs):
        o_ref = refs[-1]
        lb_ref, loss_ref = refs[len(names)], refs[len(names) + 1]
        o_ref[...] = jnp.zeros_like(o_ref)
        o_ref[LOSS_AT[0]:LOSS_AT[0] + 1, LOSS_AT[1]:LOSS_AT[1] + LANE] = jnp.broadcast_to(loss_ref[...], (1, LANE))
        for (name, row, size), ref in zip(VEC_ROWS, refs):
            if name == "g_hgrn":
                r = lax.broadcasted_iota(jnp.int32, (NH * V_DIM, LANE), 0)
                c = lax.broadcasted_iota(jnp.int32, (NH * V_DIM, LANE), 1)
                fold = ((r % V_DIM) == c).astype(F32)
                o_ref[row:row + 1, 0:LANE] = jnp.dot(ref[...], fold, precision=HIGHEST, preferred_element_type=F32)
            else:
                o_ref[row:row + 1, 0:size] = ref[...]
        o_ref[VEC_LB_ROW:VEC_LB_ROW + 2, 0:512] = lb_ref[...]

    return pl.pallas_call(body, name="vec_pack", out_shape=jax.ShapeDtypeStruct(VEC_SHAPE, F32))(
        *[vg[n] for n in names], vg["lb_logits"], loss)


def _adamw_vec(p_mine, p_sibling, w, m, v):
    names = [n for n, _, _ in VEC_ROWS] + ["lb_logits"]
    k = len(names)

    def body(a_ref, b_ref, *refs):
        ins, outs = refs[:3 * k], refs[3 * k:]
        at = (slice(LOSS_AT[0], LOSS_AT[0] + 1), slice(LOSS_AT[1], LOSS_AT[1] + LANE))
        outs[-1][...] = a_ref[at] + b_ref[at]
        for i, name in enumerate(names):
            if name == "lb_logits":
                rows, cols = slice(VEC_LB_ROW, VEC_LB_ROW + 2), slice(0, 512)
            else:
                _, row, size = VEC_ROWS[i]
                rows, cols = slice(row, row + 1), slice(0, size)
            g = a_ref[rows, cols] + b_ref[rows, cols]
            d, nm, nv = _adamw_math(g, ins[i][...], ins[k + i][...], ins[2 * k + i][...])
            for o_ref, val in zip(outs[4 * i:4 * i + 4], (g, d, nm, nv)):
                o_ref[...] = val

    shapes = [jax.ShapeDtypeStruct(w[n].shape, F32) for n in names for _ in range(4)] + [jax.ShapeDtypeStruct((1, LANE), F32)]
    res = pl.pallas_call(body, name="adamw_vec", out_shape=shapes)(
        p_mine, p_sibling, *[w[n] for n in names], *[m[n] for n in names], *[v[n] for n in names])
    return [{n: res[4 * i + j] for i, n in enumerate(names)} for j in range(4)], res[-1]


WEIGHTS = ("g_pre", "w_in", "b_gate", "g_q", "w_uq", "g_kv", "w_ukv", "lb_logits", "g_hgrn", "w_branch_a", "w_branch_b", "w_out", "g_post")


def kernel(x, g_pre, w_in, b_gate, g_q, w_uq, g_kv, w_ukv, lb_logits, g_hgrn, w_branch_a, w_branch_b, w_out, g_post, loss_target, m_g_pre, m_w_in, m_b_gate, m_g_q, m_w_uq, m_g_kv, m_w_ukv, m_lb_logits, m_g_hgrn, m_w_branch_a, m_w_branch_b, m_w_out, m_g_post, v_g_pre, v_w_in, v_b_gate, v_g_q, v_w_uq, v_g_kv, v_w_ukv, v_lb_logits, v_g_hgrn, v_w_branch_a, v_w_branch_b, v_w_out, v_g_post):
    w = dict(g_pre=g_pre, w_in=w_in, b_gate=b_gate, g_q=g_q, w_uq=w_uq, g_kv=g_kv, w_ukv=w_ukv, lb_logits=lb_logits, g_hgrn=g_hgrn,
             w_branch_a=w_branch_a, w_branch_b=w_branch_b, w_out=w_out, g_post=g_post)
    m = dict(g_pre=m_g_pre, w_in=m_w_in, b_gate=m_b_gate, g_q=m_g_q, w_uq=m_w_uq, g_kv=m_g_kv, w_ukv=m_w_ukv, lb_logits=m_lb_logits,
             g_hgrn=m_g_hgrn, w_branch_a=m_w_branch_a, w_branch_b=m_w_branch_b, w_out=m_w_out, g_post=m_g_post)
    v = dict(g_pre=v_g_pre, w_in=v_w_in, b_gate=v_b_gate, g_q=v_g_q, w_uq=v_w_uq, g_kv=v_g_kv, w_ukv=v_w_ukv, lb_logits=v_lb_logits,
             g_hgrn=v_g_hgrn, w_branch_a=v_w_branch_a, w_branch_b=v_w_branch_b, w_out=v_w_out, g_post=v_g_post)
    blocks = {n: _to_block(n, w[n]).astype(BF16) for n in BIG}
    cut = W_IN_GATHER_CUT
    top, bottom = _gather_weights([blocks["w_in"][:cut], blocks["w_in"][cut:]], [GATHER_SPLIT_AXIS["w_in"]] * 2)
    w_in_all = jnp.concatenate([top, bottom], axis=1)
    weights = _LaterWeights(blocks, top)
    state = {n: [_to_block(n, t[n]) for t in (w, m, v)] for n in BIG}
    exchange = _GradExchange(state)
    loss, grad_x, _, vec_grads = _local_step(
        x[0], loss_target[0], g_pre, _join_chips("w_in", w_in_all), b_gate, g_q, g_kv, lb_logits, g_hgrn, g_post, weights, exchange)
    vec = _vec_pack(vec_grads, loss)
    vec_started, token = _exchange_start([vec], (), "vec_start")
    sums = exchange.finish(token)
    rest = tuple(sums)
    (vec_landed,) = _exchange_wait([vec], (), vec_started, sums[rest[-1]], "vec_wait")
    mine = [sums[n] for n in rest] + [_sum_landed(vec_landed, vec, "sum_vec")]
    theirs = _sibling_exchange(mine, "sibling_grads")
    done = dict(exchange.outs)
    for k, n in enumerate(rest):
        done[n] = _adamw(mine[k], theirs[k], *state[n], "adamw_" + n)
    outs = [{}, {}, {}, {}]
    for n in BIG:
        for o, val in zip(outs, done[n]):
            o[n] = _from_block(n, val)
    vec_outs, total = _adamw_vec(mine[-1], theirs[-1], w, m, v)
    for o, vals in zip(outs, vec_outs):
        o.update(vals)
    return (total[0, 0], grad_x[None], *[o[n] for o in outs for n in WEIGHTS])
```

```python
import functools
import math

import numpy as np
import jax
import jax.numpy as jnp
from jax import lax
from jax.experimental import pallas as pl
from jax.experimental.pallas import tpu as pltpu

F32 = jnp.float32
BF16 = jnp.bfloat16
HIGHEST = lax.Precision.HIGHEST

D = 1024
NH = 8
QK_NOPE, QK_ROPE, V_DIM = 64, 32, 64
Q_LORA, KV_LORA = 768, 256
CHUNK = 64
HG_BLOCK = 32
EPS = 1e-6
D_IN = 5664
LANE = 128
P_MERGE, P_GA, P_HQ, P_HF, P_HI, P_GB, P_CQ, P_CKV, P_KPE = 0, 2048, 2560, 3072, 3584, 4096, 4608, 5376, 5632
D_P = 5760
O_CQ, O_CKV, O_KPE, O_GA, O_HQ, O_HF, O_HI, O_GB, O_MERGE = 0, 768, 1024, 1056, 1568, 2080, 2592, 3104, 3616

TM = 512
TM_MID = 256
TQ = 1024
ONES_LANE = (LANE - 1, 0)
TH = 256
HG_PAIRS = 4
VMEM_LIMIT = 56 * 1024 * 1024

ADAM_LR, ADAM_B1, ADAM_B2, ADAM_EPS, ADAM_WD, ADAM_STEP = 0.001, 0.9, 0.999, 1e-08, 0.01, 10

NT_DIMS = (((1,), (1,)), ((), ()))
TN_DIMS = (((0,), (0,)), ((), ()))


def _params(sem):
    return pltpu.CompilerParams(dimension_semantics=sem, vmem_limit_bytes=VMEM_LIMIT)


def _mm(a, b):
    return jnp.dot(a, b, preferred_element_type=F32)


def _mm_nt(a, b):
    return lax.dot_general(a, b, NT_DIMS, preferred_element_type=F32)


def _mm_tn(a, b):
    return lax.dot_general(a, b, TN_DIMS, preferred_element_type=F32)


def _sigmoid(z):
    return jax.nn.sigmoid(z)


def _rope(v, c, s1, s2):
    return v * c + pltpu.roll(v, 112, 1) * s1 + pltpu.roll(v, 16, 1) * s2


def _rope_t(dy, c, s1, s2):
    return dy * c + pltpu.roll(dy * s1, 16, 1) + pltpu.roll(dy * s2, 112, 1)


def _rope_tables(s):
    inv = 10000.0 ** (-jnp.arange(0, QK_ROPE, 2, dtype=F32) / QK_ROPE)
    ang = jnp.arange(s, dtype=F32)[:, None] * inv[None, :]
    cos, sin = jnp.cos(ang), jnp.sin(ang)
    z64, z32, o64, o32 = jnp.zeros((s, 64), F32), jnp.zeros((s, 32), F32), jnp.ones((s, 64), F32), jnp.ones((s, 32), F32)
    z16 = jnp.zeros((s, 16), F32)
    c = jnp.concatenate([o64, cos, cos, o32], axis=1)
    s1 = jnp.concatenate([z64, -sin, z16, z32], axis=1)
    s2 = jnp.concatenate([z64, z16, sin, z32], axis=1)
    return c, s1, s2


def _front_fwd(x, g_pre, w_in_pt, tokens=()):
    s = x.shape[0]
    tokens = list(tokens)

    def body(x_ref, g_ref, w_ref, *refs):
        o_ref, h_ref = refs[len(tokens):]
        xv = x_ref[...]
        r = lax.rsqrt(jnp.mean(xv * xv, axis=-1, keepdims=True) + EPS)
        h = ((xv * r) * g_ref[...]).astype(BF16)
        h_ref[...] = h
        o_ref[...] = _mm_nt(h, w_ref[...])

    return pl.pallas_call(
        body, name="front_fwd", grid=(s // TM,),
        in_specs=[pl.BlockSpec((TM, D), lambda i: (i, 0)), pl.BlockSpec((1, D), lambda i: (0, 0)),
                  pl.BlockSpec((D_P, D), lambda i: (0, 0))] + [pl.BlockSpec((8, LANE), lambda i: (0, 0))] * len(tokens),
        out_specs=[pl.BlockSpec((TM, D_P), lambda i: (i, 0)), pl.BlockSpec((TM, D), lambda i: (i, 0))],
        out_shape=[jax.ShapeDtypeStruct((s, D_P), F32), jax.ShapeDtypeStruct((s, D), BF16)],
        compiler_params=_params(("parallel",)),
    )(x, g_pre, w_in_pt, *tokens)


def _norm_rows(v, g):
    r = lax.rsqrt(jnp.mean(v * v, axis=-1, keepdims=True) + EPS)
    return (v * r) * g, r


def _qkv_fwd(proj, g_q, g_kv, w_uq_p, w_k_p, w_v_p, rc, rs1, rs2):
    s = proj.shape[0]

    def body(cq_ref, ckv_ref, kpe_ref, gq_ref, gkv_ref, wq_ref, wk_ref, wv_ref, c_ref, s1_ref, s2_ref, q_ref, k_ref, v_ref):
        c, s1, s2 = c_ref[...], s1_ref[...], s2_ref[...]
        cqn, _ = _norm_rows(cq_ref[...], gq_ref[...])
        ckvn, _ = _norm_rows(ckv_ref[...], gkv_ref[...])
        ckvn = ckvn.astype(BF16)
        qf = _mm(cqn.astype(BF16), wq_ref[...])
        kf = _mm(ckvn, wk_ref[...])
        vf = _mm(ckvn, wv_ref[...])
        kpe = _rope(kpe_ref[...], c, s1, s2)
        lane = lax.broadcasted_iota(jnp.int32, (TM, LANE), 1)
        for h in range(NH):
            blk = slice(h * LANE, (h + 1) * LANE)
            q_ref[h] = _rope(qf[:, blk], c, s1, s2).astype(BF16)
            k_ref[h] = (kf[:, blk] + kpe).astype(BF16)
            v_ref[h] = jnp.where(lane == ONES_LANE[h % 2], 1.0, vf[:, blk]).astype(BF16)

    row = lambda w, j: pl.BlockSpec((TM, w), lambda i: (i, j))
    full = lambda a: pl.BlockSpec(a.shape, lambda i: (0,) * a.ndim)
    hs = jax.ShapeDtypeStruct((NH, s, LANE), BF16)
    return pl.pallas_call(
        body, name="qkv_fwd", grid=(s // TM,),
        in_specs=[row(Q_LORA, P_CQ // Q_LORA), row(KV_LORA, P_CKV // KV_LORA), row(LANE, P_KPE // LANE),
                  full(g_q), full(g_kv), full(w_uq_p), full(w_k_p), full(w_v_p), row(LANE, 0), row(LANE, 0), row(LANE, 0)],
        out_specs=[pl.BlockSpec((NH, TM, LANE), lambda i: (0, i, 0))] * 3,
        out_shape=[hs, hs, hs],
        compiler_params=_params(("parallel",)),
    )(proj, proj, proj, g_q, g_kv, w_uq_p, w_k_p, w_v_p, rc, rs1, rs2)


LOG2E = 1.4426950408889634
QK_SCALE2 = LOG2E / math.sqrt(QK_NOPE + QK_ROPE)


HQ = TQ // 2


def _diag_visible(n):
    row = lax.broadcasted_iota(jnp.int32, (n, n), 0)
    col = lax.broadcasted_iota(jnp.int32, (n, n), 1)
    return (col // CHUNK) <= (row // CHUNK)


def _attn_fwd(q, k, vv):
    s = q.shape[1]

    def body(q_ref, k_ref, v_ref, o_ref, lse_ref):
        i = pl.program_id(1)
        qs = (q_ref[0], q_ref[1])

        def tile(hh, t, carry, diag):
            m, acc = carry
            rows = pl.ds(pl.multiple_of(t * TQ, TQ), TQ)
            sc = _mm_nt(qs[hh], k_ref[hh, rows, :])
            if diag:
                sc = jnp.where(_diag_visible(TQ), sc, -jnp.inf)
            m_new = jnp.maximum(m, jnp.max(sc, axis=-1, keepdims=True))
            alpha = jnp.exp2((m - m_new) * QK_SCALE2)
            p = jnp.exp2((sc - m_new) * QK_SCALE2).astype(BF16)
            acc = alpha * acc + _mm(p, v_ref[hh, rows, :])
            return m_new, acc

        def step(t, carry):
            return tile(0, t, carry[0], False), tile(1, t, carry[1], False)

        init = (jnp.full((TQ, 1), -jnp.inf, F32), jnp.zeros((TQ, LANE), F32))
        carry = lax.fori_loop(0, i, step, (init, init))
        lane = lax.broadcasted_iota(jnp.int32, (TQ, LANE), 1)
        out = jnp.zeros((TQ, LANE), F32)
        for hh in range(2):
            m, acc = tile(hh, i, carry[hh], True)
            l = jnp.sum(jnp.where(lane == ONES_LANE[hh], acc, 0.0), axis=-1, keepdims=True)
            out = out + jnp.where((lane < V_DIM) == (hh == 0), acc, 0.0) / l
            lse_ref[hh] = jnp.broadcast_to(m * QK_SCALE2 + jnp.log(l) * LOG2E, (TQ, LANE))
        o_ref[...] = out

    return pl.pallas_call(
        body, name="attn_fwd", grid=(NH // 2, s // TQ),
        in_specs=[pl.BlockSpec((2, TQ, LANE), lambda p, i: (p, i, 0)), pl.BlockSpec((2, s, LANE), lambda p, i: (p, 0, 0)),
                  pl.BlockSpec((2, s, LANE), lambda p, i: (p, 0, 0))],
        out_specs=[pl.BlockSpec((TQ, LANE), lambda p, i: (i, p)), pl.BlockSpec((2, TQ, LANE), lambda p, i: (p, i, 0))],
        out_shape=[jax.ShapeDtypeStruct((s, NH * V_DIM), F32), jax.ShapeDtypeStruct((NH, s, LANE), F32)],
        compiler_params=_params(("parallel", "parallel")),
    )(q, k, vv)


def _lower_bound(lbl):
    a0, a1 = lbl[0:1, :], lbl[1:2, :]
    mx = jnp.maximum(a0, a1)
    e0, e1 = jnp.exp(a0 - mx), jnp.exp(a1 - mx)
    return e0 / (e0 + e1)


def _chunk_cumsum(v, reverse=False):
    pos = lax.broadcasted_iota(jnp.int32, v.shape, 0) % HG_BLOCK
    s = 1
    while s < HG_BLOCK:
        if reverse:
            v = v + jnp.where(pos < HG_BLOCK - s, pltpu.roll(v, TH - s, 0), 0.0)
        else:
            v = v + jnp.where(pos >= s, pltpu.roll(v, s, 0), 0.0)
        s *= 2
    return v


def _hgrn_gates(hq, hf, lb):
    sig = _sigmoid(hf)
    f = lb + (1.0 - lb) * sig
    g = jnp.log(f)
    kk = 1.0 - f
    r = lax.broadcasted_iota(jnp.int32, (TH, TH), 0)
    c = lax.broadcasted_iota(jnp.int32, (TH, TH), 1)
    tri = ((r // HG_BLOCK) == (c // HG_BLOCK)) & (r >= c)
    cum = _chunk_cumsum(g)
    nch = TH // HG_BLOCK
    total = _chunks(cum)[:, HG_BLOCK - 1:HG_BLOCK, :]
    lastb = jnp.broadcast_to(total, (nch, HG_BLOCK, LANE)).reshape(TH, LANE)
    e, ei, ee = jnp.exp(cum), jnp.exp(-cum), jnp.exp(lastb - cum)
    return dict(sig=sig, f=f, kk=kk, tri=tri, cum=cum, total=total, e=e, ei=ei, ee=ee, qd=hq * e, ki=kk * ei, ke=kk * ee)


def _chunks(v):
    return v.reshape(TH // HG_BLOCK, HG_BLOCK, v.shape[-1])


def _bmm_nt(a, b):
    return lax.dot_general(a, b, (((2,), (2,)), ((0,), (0,))), preferred_element_type=F32)


def _bmm_nn(a, b):
    return lax.dot_general(a, b, (((2,), (1,)), ((0,), (0,))), preferred_element_type=F32)


def _bmm_tn(a, b):
    return lax.dot_general(a, b, (((1,), (1,)), ((0,), (0,))), preferred_element_type=F32)


def _pair_masks():
    lane = lax.broadcasted_iota(jnp.int32, (TH, LANE), 1)
    kr = lax.broadcasted_iota(jnp.int32, (LANE, LANE), 0)
    kc = lax.broadcasted_iota(jnp.int32, (LANE, LANE), 1)
    return lane < 64, (kr // 64) == (kc // 64)


def _hgrn_fwd(proj, lbl):
    s = proj.shape[0]
    nch = TH // HG_BLOCK

    def body(hq_ref, hf_ref, hi_ref, lbl_ref, o_ref, st_ref, st):
        @pl.when(pl.program_id(1) == 0)
        def _():
            st[...] = jnp.zeros_like(st)

        m0, bd = _pair_masks()
        for u in range(HG_PAIRS):
            lanes = slice(u * LANE, (u + 1) * LANE)
            lb = _lower_bound(lbl_ref[:, lanes])
            gt = _hgrn_gates(hq_ref[:, lanes], hf_ref[:, lanes], lb)
            v_b = hi_ref[:, lanes].astype(BF16)
            qd, ki_b, ke_b = gt["qd"], gt["ki"].astype(BF16), gt["ke"].astype(BF16)
            qd_b = qd.astype(BF16)
            o = jnp.zeros((TH, LANE), F32)
            for hh in range(2):
                mh = m0 if hh == 0 else jnp.logical_not(m0)
                a = jnp.where(gt["tri"], _mm_nt(jnp.where(mh, qd, 0.0).astype(BF16), ki_b), 0.0)
                o = jnp.where(mh, _mm(a.astype(BF16), v_b), o)
            upd = _bmm_tn(_chunks(v_b), _chunks(ke_b))
            decay = jnp.exp(gt["total"])
            cur, entering = st[u], []
            for n in range(nch):
                entering.append(cur)
                cur = decay[n] * cur + jnp.where(bd, upd[n], 0.0)
            st[u] = cur
            entering = jnp.stack(entering)
            st_ref[u] = entering
            o_ref[:, lanes] = o + _bmm_nt(_chunks(qd_b), entering.astype(BF16)).reshape(TH, LANE)

    wide = HG_PAIRS * LANE
    col = lambda base: pl.BlockSpec((TH, wide), lambda p, i: (i, base // wide + p))
    return pl.pallas_call(
        body, name="hgrn_fwd", grid=(NH // 2 // HG_PAIRS, s // TH),
        in_specs=[col(P_HQ), col(P_HF), col(P_HI), pl.BlockSpec((2, wide), lambda p, i: (0, p))],
        out_specs=[pl.BlockSpec((TH, wide), lambda p, i: (i, p)),
                   pl.BlockSpec((HG_PAIRS, nch, LANE, LANE), lambda p, i: (p, i, 0, 0))],
        out_shape=[jax.ShapeDtypeStruct((s, 512), F32), jax.ShapeDtypeStruct((NH // 2, s // HG_BLOCK, LANE, LANE), F32)],
        scratch_shapes=[pltpu.VMEM((HG_PAIRS, LANE, LANE), F32)],
        compiler_params=_params(("parallel", "arbitrary")),
    )(proj, proj, proj, lbl)


def _group_sum(v):
    low = lax.broadcasted_iota(jnp.int32, (v.shape[0], LANE), 1) < V_DIM
    blocks = []
    for b in range(v.shape[1] // LANE):
        blk = v[:, b * LANE:(b + 1) * LANE]
        s_low = jnp.sum(jnp.where(low, blk, 0.0), axis=-1, keepdims=True)
        s_high = jnp.sum(jnp.where(low, 0.0, blk), axis=-1, keepdims=True)
        blocks.append(jnp.where(low, s_low, s_high))
    return jnp.concatenate(blocks, axis=1)


def _dsilu(z, sg):
    return sg * (1.0 + z * (1.0 - sg))


def _mid(proj, attn, o_raw, x, tgt, g_hg, b_gate, g_post, wa, wb, w_out):
    s = x.shape[0]

    def body(attn_ref, ga_ref, o_ref, gb_ref, mg_ref, x_ref, t_ref, ghg_ref, bg_ref, gp_ref, wa_ref, wb_ref, wo_ref,
             loss_ref, dout_ref, dattn_ref, dga_ref, dor_ref, dgb_ref, dmg_ref, dwo_ref, dwa_ref, dwb_ref, dgp_ref, dbg_ref, dghg_ref):
        @pl.when(pl.program_id(0) == 0)
        def _():
            for rf in (loss_ref, dwo_ref, dwa_ref, dwb_ref, dgp_ref, dbg_ref, dghg_ref):
                rf[...] = jnp.zeros_like(rf)

        attn, za, orw, zb = attn_ref[...], ga_ref[...], o_ref[...], gb_ref[...]
        ghg, gp = ghg_ref[...], gp_ref[...]
        sga, sgb = _sigmoid(za), _sigmoid(zb)
        sa, sb = za * sga, zb * sgb
        ga = attn * sa
        rh = lax.rsqrt(_group_sum(orw * orw) * (1.0 / V_DIM) + EPS)
        on = (orw * rh) * ghg
        gb = on * sb
        ga_b, gb_b = ga.astype(BF16), gb.astype(BF16)
        ya = _mm(ga_b, wa_ref[...])
        yb = _mm(gb_b, wb_ref[...])
        gates = _sigmoid(mg_ref[...] + bg_ref[...])
        g0, g1 = gates[:, :D], gates[:, D:]
        m_b = (g0 * ya + g1 * yb).astype(BF16)
        y = _mm(m_b, wo_ref[...])
        ry = lax.rsqrt(jnp.mean(y * y, axis=-1, keepdims=True) + EPS)
        out = x_ref[...] + (y * ry) * gp
        err = out - t_ref[...]
        loss_ref[...] += 0.5 * jnp.sum(jnp.mean(err * err, axis=-1, keepdims=True), axis=0, keepdims=True)
        dout = err * (1.0 / D)
        dout_ref[...] = dout
        dgp_ref[...] += jnp.sum(dout * (y * ry), axis=0, keepdims=True)
        dgy = dout * gp
        dy = ry * dgy - y * (ry * ry * ry) * jnp.mean(y * dgy, axis=-1, keepdims=True)
        dy_b = dy.astype(BF16)
        dwo_ref[...] += _mm_tn(m_b, dy_b)
        dm = _mm_nt(dy_b, wo_ref[...])
        dya, dyb = dm * g0, dm * g1
        dg0, dg1 = dm * ya, dm * yb
        dmg = jnp.concatenate([dg0 * g0 * (1.0 - g0), dg1 * g1 * (1.0 - g1)], axis=1)
        dmg_ref[...] = dmg.astype(BF16)
        dbg_ref[...] += jnp.sum(dmg, axis=0, keepdims=True)
        dya_b, dyb_b = dya.astype(BF16), dyb.astype(BF16)
        dwa_ref[...] += _mm_tn(ga_b, dya_b)
        dwb_ref[...] += _mm_tn(gb_b, dyb_b)
        dga = _mm_nt(dya_b, wa_ref[...])
        dgb = _mm_nt(dyb_b, wb_ref[...])
        dattn_ref[...] = dga * sa
        dga_ref[...] = (dga * attn * _dsilu(za, sga)).astype(BF16)
        dgb_ref[...] = (dgb * on * _dsilu(zb, sgb)).astype(BF16)
        don = dgb * sb
        dghg_ref[...] += jnp.sum(don * (orw * rh), axis=0, keepdims=True)
        dgo = don * ghg
        dor_ref[...] = rh * dgo - orw * (rh * rh * rh) * (_group_sum(orw * dgo) * (1.0 / V_DIM))

    row = lambda w, j=0: pl.BlockSpec((TM_MID, w), lambda i: (i, j))
    full = lambda a: pl.BlockSpec(a.shape, lambda i: (0,) * a.ndim)
    acc = lambda shape: pl.BlockSpec(shape, lambda i: (0, 0))
    sds = jax.ShapeDtypeStruct
    return pl.pallas_call(
        body, name="mid", grid=(s // TM_MID,),
        in_specs=[row(512), row(512, P_GA // 512), row(512), row(512, P_GB // 512), row(2048, P_MERGE // 2048), row(D), row(D),
                  full(g_hg), full(b_gate), full(g_post), full(wa), full(wb), full(w_out)],
        out_specs=[acc((1, 1)), row(D), row(512), row(512), row(512), row(512), row(2048),
                   acc((D, D)), acc((512, D)), acc((512, D)), acc((1, D)), acc((1, 2048)), acc((1, 512))],
        out_shape=[sds((1, 1), F32), sds((s, D), F32), sds((s, 512), F32), sds((s, 512), BF16), sds((s, 512), F32), sds((s, 512), BF16),
                   sds((s, 2048), BF16), sds((D, D), F32), sds((512, D), F32), sds((512, D), F32), sds((1, D), F32),
                   sds((1, 2048), F32), sds((1, 512), F32)],
        compiler_params=_params(("arbitrary",)),
    )(attn, proj, o_raw, proj, proj, x, tgt, g_hg, b_gate, g_post, wa, wb, w_out)


def _attn_bwd(q, k, vv, attn, dattn, lse, token):
    s = q.shape[1]
    nt = s // TQ
    scale = 1.0 / math.sqrt(QK_NOPE + QK_ROPE)

    def body(q_ref, k_ref, v_ref, o_ref, do_ref, lse_ref, token_ref, dq_ref, dk_ref, dv_ref, do_s, delta_s):
        j = pl.program_id(1)

        @pl.when(j == 0)
        def _():
            dq_ref[...] = jnp.zeros_like(dq_ref)
            lane = lax.broadcasted_iota(jnp.int32, (TQ, LANE), 1)

            @pl.loop(0, nt)
            def _(i):
                rows = pl.ds(pl.multiple_of(i * TQ, TQ), TQ)
                do, o = do_ref[rows, :], o_ref[rows, :]
                for hh in range(2):
                    doh = jnp.where((lane < 64) if hh == 0 else (lane >= 64), do, 0.0)
                    do_s[hh, rows, :] = doh.astype(BF16)
                    delta_s[hh, rows, :] = jnp.broadcast_to(jnp.sum(doh * o, axis=-1, keepdims=True), (TQ, LANE))

        kjs, vjs = (k_ref[0], k_ref[1]), (v_ref[0], v_ref[1])

        def tile(hh, start, size, kj, vj, diag):
            rows = pl.ds(pl.multiple_of(start, size), size)
            wide = lambda a: jnp.concatenate([a] * (kj.shape[0] // LANE), axis=1)
            qi, do_b = q_ref[hh, rows, :], do_s[hh, rows, :]
            p = jnp.exp2(_mm_nt(qi, kj) * QK_SCALE2 - wide(lse_ref[hh, rows, :]))
            if diag:
                p = jnp.where(_diag_visible(size), p, 0.0)
            dv = _mm_tn(do_b, p.astype(BF16))
            ds_b = (p * (_mm_nt(do_b, vj) - wide(delta_s[hh, rows, :]))).astype(BF16)
            dk = _mm_tn(qi, ds_b)
            dq_ref[hh, rows, :] += _mm(ds_b, kj)
            return dk, dv

        def step(i, carry):
            new = [tile(hh, i * TQ, TQ, kjs[hh], vjs[hh], False) for hh in range(2)]
            return tuple((carry[hh][0] + new[hh][0], carry[hh][1] + new[hh][1]) for hh in range(2))

        def diagonal(hh):
            k0, k1, v0, v1 = kjs[hh][:HQ], kjs[hh][HQ:], vjs[hh][:HQ], vjs[hh][HQ:]
            a = tile(hh, j * TQ, HQ, k0, v0, True)
            b = tile(hh, j * TQ + HQ, HQ, k0, v0, False)
            c = tile(hh, j * TQ + HQ, HQ, k1, v1, True)
            return jnp.concatenate([a[0] + b[0], c[0]], axis=1), jnp.concatenate([a[1] + b[1], c[1]], axis=1)

        carry = lax.fori_loop(j + 1, nt, step, (diagonal(0), diagonal(1)))
        for hh in range(2):
            dk_ref[hh] = carry[hh][0].T * scale
            dv_ref[hh] = carry[hh][1].T

        @pl.when(j == nt - 1)
        def _():
            dq_ref[...] = dq_ref[...] * scale

    whole = pl.BlockSpec((2, s, LANE), lambda p, j: (p, 0, 0))
    tile_spec = pl.BlockSpec((2, TQ, LANE), lambda p, j: (p, j, 0))
    cols = pl.BlockSpec((s, LANE), lambda p, j: (0, p))
    hs = jax.ShapeDtypeStruct((NH, s, LANE), F32)
    return pl.pallas_call(
        body, name="attn_bwd", grid=(NH // 2, nt),
        in_specs=[whole, tile_spec, tile_spec, cols, cols, whole, pl.BlockSpec((8, LANE), lambda p, j: (0, 0))],
        out_specs=[whole, tile_spec, tile_spec],
        out_shape=[hs, hs, hs],
        scratch_shapes=[pltpu.VMEM((2, s, LANE), BF16), pltpu.VMEM((2, s, LANE), F32)],
        compiler_params=_params(("parallel", "arbitrary")),
    )(q, k, vv, attn, dattn, lse, token)


def _hgrn_bwd(proj, lbl, states, do_raw):
    s = proj.shape[0]
    nt = s // TH
    nch = TH // HG_BLOCK

    def body(hq_ref, hf_ref, hi_ref, lbl_ref, st_ref, do_ref, dh_ref, dlbl_ref, dst, dlb):
        step = pl.program_id(1)

        @pl.when(step == 0)
        def _():
            dst[...] = jnp.zeros_like(dst)
            dlb[...] = jnp.zeros_like(dlb)

        m0, bd = _pair_masks()
        for u in range(HG_PAIRS):
            lanes = slice(u * LANE, (u + 1) * LANE)
            lb = _lower_bound(lbl_ref[:, lanes])
            gt = _hgrn_gates(hq_ref[:, lanes], hf_ref[:, lanes], lb)
            do = do_ref[:, lanes]
            qd, ki, ke = gt["qd"], gt["ki"], gt["ke"]
            v_b, do_b = hi_ref[:, lanes].astype(BF16), do.astype(BF16)
            qd_b, ki_b, ke_b = qd.astype(BF16), ki.astype(BF16), ke.astype(BF16)
            dv = jnp.zeros((TH, LANE), F32)
            dqd = jnp.zeros((TH, LANE), F32)
            dki = jnp.zeros((TH, LANE), F32)
            for hh in range(2):
                mh = m0 if hh == 0 else jnp.logical_not(m0)
                a_b = jnp.where(gt["tri"], _mm_nt(jnp.where(mh, qd, 0.0).astype(BF16), ki_b), 0.0).astype(BF16)
                doh_b = jnp.where(mh, do, 0.0).astype(BF16)
                da_b = jnp.where(gt["tri"], _mm_nt(doh_b, v_b), 0.0).astype(BF16)
                dv = dv + _mm_tn(a_b, doh_b)
                dqd = jnp.where(mh, _mm(da_b, ki_b), dqd)
                dki = jnp.where(mh, _mm_tn(da_b, qd_b), dki)
            fed = _bmm_tn(_chunks(do_b), _chunks(qd_b))
            decay = jnp.exp(gt["total"])
            ds, leaving = dst[u], [None] * nch
            for n in reversed(range(nch)):
                leaving[n] = ds
                ds = decay[n] * ds + jnp.where(bd, fed[n], 0.0)
            dst[u] = ds
            leaving = jnp.stack(leaving)
            entering = st_ref[u]
            leaving_b = leaving.astype(BF16)
            dke3 = _bmm_nn(_chunks(v_b), leaving_b)
            dv = dv + _bmm_nt(_chunks(ke_b), leaving_b).reshape(TH, LANE)
            dqd = dqd + _bmm_nn(_chunks(do_b), entering.astype(BF16)).reshape(TH, LANE)
            dke = dke3.reshape(TH, LANE)
            dlast = (jnp.sum(dke3 * _chunks(ke), axis=1, keepdims=True)
                     + jnp.sum(leaving * entering, axis=1, keepdims=True) * decay)
            dk = dki * gt["ei"] + dke * gt["ee"]
            dcum = dqd * qd - dki * ki - dke * ke
            dg = _chunk_cumsum(dcum, reverse=True) + jnp.broadcast_to(dlast, (nch, HG_BLOCK, LANE)).reshape(TH, LANE)
            sig = gt["sig"]
            df = dg / gt["f"] - dk
            dlb[:, lanes] += jnp.sum(df * (1.0 - sig), axis=0, keepdims=True)
            dh_ref[0, :, lanes] = (dqd * gt["e"]).astype(BF16)
            dh_ref[1, :, lanes] = ((df * (1.0 - lb)) * sig * (1.0 - sig)).astype(BF16)
            dh_ref[2, :, lanes] = dv.astype(BF16)

        @pl.when(step == nt - 1)
        def _():
            lb = _lower_bound(lbl_ref[...])
            da0 = dlb[...] * lb * (1.0 - lb)
            dlbl_ref[...] = jnp.concatenate([da0, -da0], axis=0)

    wide = HG_PAIRS * LANE
    col = lambda base: pl.BlockSpec((TH, wide), lambda p, i: (nt - 1 - i, base // wide + p))
    tile = pl.BlockSpec((TH, wide), lambda p, i: (nt - 1 - i, p))
    sds = jax.ShapeDtypeStruct
    return pl.pallas_call(
        body, name="hgrn_bwd", grid=(NH // 2 // HG_PAIRS, nt),
        in_specs=[col(P_HQ), col(P_HF), col(P_HI), pl.BlockSpec((2, wide), lambda p, i: (0, p)),
                  pl.BlockSpec((HG_PAIRS, nch, LANE, LANE), lambda p, i: (p, nt - 1 - i, 0, 0)), tile],
        out_specs=[pl.BlockSpec((3, TH, wide), lambda p, i: (0, nt - 1 - i, p)), pl.BlockSpec((2, wide), lambda p, i: (0, p))],
        out_shape=[sds((3, s, 512), BF16), sds((2, 512), F32)],
        scratch_shapes=[pltpu.VMEM((HG_PAIRS, LANE, LANE), F32), pltpu.VMEM((1, wide), F32)],
        compiler_params=_params(("parallel", "arbitrary")),
    )(proj, proj, proj, lbl, states, do_raw)


def _norm_rows_bwd(v, r, g, dn):
    dgv = dn * g
    return r * dgv - v * (r * r * r) * jnp.mean(v * dgv, axis=-1, keepdims=True)


def _qkv_bwd(proj, dq, dk, dvv, g_q, g_kv, w_uq_p, w_k_p, w_v_p, rc, rs1, rs2):
    s = proj.shape[0]

    def body(cq_ref, ckv_ref, dq_ref, dk_ref, dv_ref, gq_ref, gkv_ref, wq_ref, wk_ref, wv_ref, c_ref, s1_ref, s2_ref,
             dcq_ref, dckv_ref, dkpe_ref, dwq_ref, dwk_ref, dwv_ref, dgq_ref, dgkv_ref):
        @pl.when(pl.program_id(0) == 0)
        def _():
            for rf in (dwq_ref, dwk_ref, dwv_ref, dgq_ref, dgkv_ref):
                rf[...] = jnp.zeros_like(rf)

        c, s1, s2 = c_ref[...], s1_ref[...], s2_ref[...]
        cq, ckv = cq_ref[...], ckv_ref[...]
        gq, gkv = gq_ref[...], gkv_ref[...]
        cqn, rq = _norm_rows(cq, gq)
        ckvn, rkv = _norm_rows(ckv, gkv)
        cqn_b, ckvn_b = cqn.astype(BF16), ckvn.astype(BF16)
        dqf = jnp.concatenate([_rope_t(dq_ref[h], c, s1, s2) for h in range(NH)], axis=1).astype(BF16)
        dkf = jnp.concatenate([dk_ref[h] for h in range(NH)], axis=1).astype(BF16)
        dvf = jnp.concatenate([dv_ref[h] for h in range(NH)], axis=1).astype(BF16)
        dkpe = dk_ref[0]
        for h in range(1, NH):
            dkpe = dkpe + dk_ref[h]
        lane = lax.broadcasted_iota(jnp.int32, (TM, LANE), 1)
        dkpe = jnp.where((lane >= QK_NOPE) & (lane < QK_NOPE + QK_ROPE), dkpe, 0.0)
        dkpe_ref[...] = _rope_t(dkpe, c, s1, s2).astype(BF16)
        dwq_ref[...] += _mm_tn(cqn_b, dqf)
        dwk_ref[...] += _mm_tn(ckvn_b, dkf)
        dwv_ref[...] += _mm_tn(ckvn_b, dvf)
        dcqn = _mm_nt(dqf, wq_ref[...])
        dckvn = _mm_nt(dkf, wk_ref[...]) + _mm_nt(dvf, wv_ref[...])
        dgq_ref[...] += jnp.sum(dcqn * (cq * rq), axis=0, keepdims=True)
        dgkv_ref[...] += jnp.sum(dckvn * (ckv * rkv), axis=0, keepdims=True)
        dcq_ref[...] = _norm_rows_bwd(cq, rq, gq, dcqn).astype(BF16)
        dckv_ref[...] = _norm_rows_bwd(ckv, rkv, gkv, dckvn).astype(BF16)

    row = lambda w, j=0: pl.BlockSpec((TM, w), lambda i: (i, j))
    full = lambda a: pl.BlockSpec(a.shape, lambda i: (0,) * a.ndim)
    acc = lambda shape: pl.BlockSpec(shape, lambda i: (0, 0))
    heads = pl.BlockSpec((NH, TM, LANE), lambda i: (0, i, 0))
    sds = jax.ShapeDtypeStruct
    return pl.pallas_call(
        body, name="qkv_bwd", grid=(s // TM,),
        in_specs=[row(Q_LORA, P_CQ // Q_LORA), row(KV_LORA, P_CKV // KV_LORA), heads, heads, heads,
                  full(g_q), full(g_kv), full(w_uq_p), full(w_k_p), full(w_v_p), row(LANE), row(LANE), row(LANE)],
        out_specs=[row(Q_LORA), row(KV_LORA), row(LANE), acc((Q_LORA, D)), acc((KV_LORA, D)), acc((KV_LORA, D)),
                   acc((1, Q_LORA)), acc((1, KV_LORA))],
        out_shape=[sds((s, Q_LORA), BF16), sds((s, KV_LORA), BF16), sds((s, LANE), BF16), sds((Q_LORA, D), F32),
                   sds((KV_LORA, D), F32), sds((KV_LORA, D), F32), sds((1, Q_LORA), F32), sds((1, KV_LORA), F32)],
        compiler_params=_params(("arbitrary",)),
    )(proj, proj, dq, dk, dvv, g_q, g_kv, w_uq_p, w_k_p, w_v_p, rc, rs1, rs2)


def _front_bwd(x, dout, dmg, dga, dh3, dgb, dcq, dckv, dkpe, g_pre, w_in_pt, token):
    s = x.shape[0]

    def body(x_ref, do_ref, dmg_ref, dga_ref, dh3_ref, dgb_ref, dcq_ref, dckv_ref, dkpe_ref, g_ref, w_ref, token_ref, gx_ref, dg_ref):
        @pl.when(pl.program_id(0) == 0)
        def _():
            dg_ref[...] = jnp.zeros_like(dg_ref)

        xv, g = x_ref[...], g_ref[...]
        _, r = _norm_rows(xv, g)
        pieces = ((dmg_ref[...], P_MERGE), (dga_ref[...], P_GA), (dh3_ref[0], P_HQ), (dh3_ref[1], P_HF), (dh3_ref[2], P_HI),
                  (dgb_ref[...], P_GB), (dcq_ref[...], P_CQ), (dckv_ref[...], P_CKV), (dkpe_ref[...], P_KPE))
        dh = jnp.zeros((TM, D), F32)
        for piece, off in pieces:
            dh = dh + _mm(piece, w_ref[off:off + piece.shape[1], :])
        dg_ref[...] += jnp.sum(dh * (xv * r), axis=0, keepdims=True)
        gx_ref[...] = do_ref[...] + _norm_rows_bwd(xv, r, g, dh)

    row = lambda w: pl.BlockSpec((TM, w), lambda i: (i, 0))
    full = lambda a: pl.BlockSpec(a.shape, lambda i: (0,) * a.ndim)
    sds = jax.ShapeDtypeStruct
    return pl.pallas_call(
        body, name="front_bwd", grid=(s // TM,),
        in_specs=[row(D), row(D), row(2048), row(512), pl.BlockSpec((3, TM, 512), lambda i: (0, i, 0)), row(512), row(Q_LORA),
                  row(KV_LORA), row(LANE), full(g_pre), full(w_in_pt), pl.BlockSpec(memory_space=pl.ANY)],
        out_specs=[row(D), pl.BlockSpec((1, D), lambda i: (0, 0))],
        out_shape=[sds((s, D), F32), sds((1, D), F32)],
        compiler_params=_params(("arbitrary",)),
    )(x, dout, dmg, dga, dh3, dgb, dcq, dckv, dkpe, g_pre, w_in_pt, token)


TK_GRAD = 1024


def _win_grad(h, pieces, name):
    s = h.shape[0]
    n = len(pieces)

    def body(h_ref, *refs):
        @pl.when(pl.program_id(0) == 0)
        def _():
            for o_ref in refs[n:]:
                o_ref[...] = jnp.zeros_like(o_ref)

        hv = h_ref[...]
        for d_ref, o_ref in zip(refs[:n], refs[n:]):
            if len(d_ref.shape) == 3:
                for k in range(d_ref.shape[0]):
                    o_ref[k] += _mm_tn(d_ref[k], hv)
            else:
                o_ref[...] += _mm_tn(d_ref[...], hv)

    def in_spec(p):
        if p.ndim == 3:
            return pl.BlockSpec((p.shape[0], TK_GRAD, p.shape[2]), lambda kk: (0, kk, 0))
        return pl.BlockSpec((TK_GRAD, p.shape[1]), lambda kk: (kk, 0))

    out_shapes = [(p.shape[0], p.shape[2], D) if p.ndim == 3 else (p.shape[1], D) for p in pieces]
    return pl.pallas_call(
        body, name=name, grid=(s // TK_GRAD,),
        in_specs=[pl.BlockSpec((TK_GRAD, D), lambda kk: (kk, 0))] + [in_spec(p) for p in pieces],
        out_specs=[pl.BlockSpec(sh, lambda kk, nd=len(sh): (0,) * nd) for sh in out_shapes],
        out_shape=[jax.ShapeDtypeStruct(sh, F32) for sh in out_shapes],
        compiler_params=_params(("arbitrary",)),
    )(h, *pieces)


def _pad_win_t(w_in_t):
    z = lambda n: jnp.zeros((n, w_in_t.shape[1]), w_in_t.dtype)
    sl = lambda o, n: w_in_t[o:o + n]
    return jnp.concatenate([sl(O_MERGE, 2048), sl(O_GA, 512), sl(O_HQ, 512), sl(O_HF, 512), sl(O_HI, 512), sl(O_GB, 512),
                            sl(O_CQ, Q_LORA), sl(O_CKV, KV_LORA), z(64), sl(O_KPE, QK_ROPE), z(32)], axis=0)


def _pad_wuq(w_uq):
    w = w_uq.reshape(Q_LORA, NH, QK_NOPE + QK_ROPE)
    return jnp.pad(w, ((0, 0), (0, 0), (0, LANE - QK_NOPE - QK_ROPE))).reshape(Q_LORA, NH * LANE)


def _unpad_wuq(g):
    return g.reshape(Q_LORA, NH, LANE)[:, :, :QK_NOPE + QK_ROPE].reshape(Q_LORA, NH * (QK_NOPE + QK_ROPE))


def _pad_wukv(w_ukv):
    w = w_ukv.reshape(KV_LORA, NH, QK_NOPE + V_DIM)
    w_k = jnp.pad(w[:, :, :QK_NOPE], ((0, 0), (0, 0), (0, LANE - QK_NOPE))).reshape(KV_LORA, NH * LANE)
    wv = w[:, :, QK_NOPE:].reshape(KV_LORA, NH // 2, 2, 1, V_DIM)
    eye = jnp.eye(2, dtype=w.dtype).reshape(1, 1, 2, 2, 1)
    return w_k, (wv * eye).reshape(KV_LORA, NH * LANE)


def _unpad_wukv(gk, gv):
    gk = gk.reshape(KV_LORA, NH, LANE)[:, :, :QK_NOPE]
    gv = gv.reshape(KV_LORA, NH // 2, 2, 2, V_DIM)
    gv = jnp.stack([gv[:, :, 0, 0], gv[:, :, 1, 1]], axis=2).reshape(KV_LORA, NH, V_DIM)
    return jnp.concatenate([gk, gv], axis=-1).reshape(KV_LORA, NH * (QK_NOPE + V_DIM))


def _local_step(x, tgt, g_pre, w_in_t, b_gate, g_q, g_kv, lb_logits, g_hgrn, g_post, weights, exchange=None):
    s = x.shape[0]
    w_in_p = _pad_win_t(w_in_t)
    rc, rs1, rs2 = _rope_tables(s)
    g_hg = jnp.tile(g_hgrn, (1, NH))

    proj, h = _front_fwd(x, g_pre, w_in_p, weights.tokens)
    w_uq, w_ukv = weights.qkv(h)
    w_uq_p = _pad_wuq(w_uq)
    w_k_p, w_v_p = _pad_wukv(w_ukv)
    q, k, vv = _qkv_fwd(proj, g_q, g_kv, w_uq_p, w_k_p, w_v_p, rc, rs1, rs2)
    attn, lse = _attn_fwd(q, k, vv)
    o_raw, states = _hgrn_fwd(proj, lb_logits)
    wa, wb, w_out = weights.mid(o_raw)
    (loss, dout, dattn, dga, dor, dgb, dmg, d_wout, d_wa, d_wb, d_gpost, d_bgate, d_ghg) = _mid(
        proj, attn, o_raw, x, tgt, g_hg, b_gate, g_post, wa, wb, w_out)
    w_mg, w_ga, w_gb = _win_grad(h, [dmg, dga, dgb], "win_grad_mid")
    dh3, d_lbl = _hgrn_bwd(proj, lb_logits, states, dor)
    (w_h3,) = _win_grad(h, [dh3], "win_grad_hgrn")
    d_win_rest = jnp.concatenate([w_ga, w_h3[0], w_h3[1], w_h3[2], w_gb, w_mg], axis=0)
    early = dict(w_in_rest=d_win_rest, w_branch_a=d_wa, w_branch_b=d_wb, w_out=d_wout)
    token = exchange.start_early(early) if exchange else jnp.zeros((8, LANE), F32)
    dq, dk, dvv = _attn_bwd(q, k, vv, attn, dattn, lse, token)
    dcq, dckv, dkpe, d_wuq_p, d_wk_p, d_wv_p, d_gq, d_gkv = _qkv_bwd(proj, dq, dk, dvv, g_q, g_kv, w_uq_p, w_k_p, w_v_p, rc, rs1, rs2)
    w_cq, w_ckv, w_kpe = _win_grad(h, [dcq, dckv, dkpe], "win_grad_qkv")
    d_win_qkv = jnp.concatenate([w_cq, w_ckv, w_kpe[64:64 + QK_ROPE]], axis=0)
    late = dict(w_in_qkv=d_win_qkv, w_uq=_unpad_wuq(d_wuq_p), w_ukv=_unpad_wukv(d_wk_p, d_wv_p))
    token = exchange.start_late(late) if exchange else jnp.zeros((8, LANE), F32)
    grad_x, d_gpre = _front_bwd(x, dout, dmg, dga, dh3, dgb, dcq, dckv, dkpe, g_pre, w_in_p, token)
    vec_grads = dict(g_pre=d_gpre, b_gate=d_bgate, g_q=d_gq, g_kv=d_gkv, lb_logits=d_lbl, g_hgrn=d_ghg, g_post=d_gpost)
    return loss, grad_x, dict(early, **late), vec_grads


SHARD_SHAPES = (("w_in", (1416, 1024)), ("w_uq", (192, 768)), ("w_ukv", (256, 256)), ("w_branch_a", (512, 256)),
                ("w_branch_b", (512, 256)), ("w_out", (256, 1024)))
BIG = tuple(n for n, _ in SHARD_SHAPES)
ROW_SHARDED = ("w_in", "w_uq", "w_out")
GATHER_SPLIT_AXIS = dict(w_in=1, w_uq=0, w_ukv=0, w_branch_a=0, w_branch_b=0, w_out=0)
N_CHIPS = 4
QKV_ROWS = Q_LORA + KV_LORA + QK_ROPE


def _to_block(name, a):
    return a[0].T if name == "w_in" else a[0]


def _from_block(name, a):
    return a.T[None] if name == "w_in" else a[None]
VEC_ROWS = (("g_pre", 0, 1024), ("b_gate", 1, 2048), ("g_q", 2, 768), ("g_kv", 3, 256), ("g_hgrn", 6, 64), ("g_post", 7, 1024))
VEC_LB_ROW = 4
VEC_SHAPE = (8, 2048)


def _split_by_chip(name, g):
    a, b = dict(SHARD_SHAPES)[name]
    return g.reshape(N_CHIPS, a, b) if name in ROW_SHARDED else g.reshape(a, N_CHIPS, b).transpose(1, 0, 2)


def _join_chips(name, w):
    a, b = dict(SHARD_SHAPES)[name]
    return w.reshape(N_CHIPS * a, b) if name in ROW_SHARDED else w.transpose(1, 0, 2).reshape(a, N_CHIPS * b)


MESH = pl.DeviceIdType.MESH
HBM = pl.BlockSpec(memory_space=pltpu.HBM)


def _mesh_place():
    x, y, c = lax.axis_index("x"), lax.axis_index("y"), lax.axis_index("c")
    return x, y, c, 2 * x + y, [(1 - x, y), (x, 1 - y), (1 - x, 1 - y)]


def _remote(src, dst, send_sems, recv_sems, k, to):
    return pltpu.make_async_remote_copy(src_ref=src, dst_ref=dst, send_sem=send_sems.at[k], recv_sem=recv_sems.at[k],
                                        device_id=to, device_id_type=MESH)


def _gather_weights(shards, split_axes):
    n = len(shards)

    def body(*refs):
        srcs, outs = refs[:n], refs[n:2 * n]
        ici_send, ici_recv, d2d_send, d2d_recv, local_sems = refs[2 * n:]
        x, y, c, me, chips = _mesh_place()
        sibling = (x, y, 1 - c)

        def half(ref, k, which):
            size = shards[k].shape[split_axes[k]] // 2
            part = pl.ds(pl.multiple_of(which * size, size), size)
            return ref.at[part] if split_axes[k] == 0 else ref.at[:, part]

        own = [pltpu.make_async_copy(srcs[k], outs[k].at[me], local_sems.at[k]) for k in range(n)]
        for cp in own:
            cp.start()
        started = []
        for k in range(n):
            for j, (px, py) in enumerate(chips):
                cp = _remote(half(srcs[k], k, c), half(outs[k].at[me], k, c), ici_send, ici_recv, 3 * k + j, (px, py, c))
                cp.start()
                started.append(cp)
        for k in range(n):
            for j, (px, py) in enumerate(chips):
                landed = half(outs[k].at[2 * px + py], k, c)
                _remote(landed, landed, ici_send, ici_recv, 3 * k + j, (px, py, c)).wait_recv()
                cp = _remote(landed, landed, d2d_send, d2d_recv, 3 * k + j, sibling)
                cp.start()
                started.append(cp)
        for k in range(n):
            for j, (px, py) in enumerate(chips):
                other = half(outs[k].at[2 * px + py], k, 1 - c)
                _remote(other, other, d2d_send, d2d_recv, 3 * k + j, sibling).wait_recv()
        for cp in started:
            cp.wait_send()
        for cp in own:
            cp.wait()

    sems = pltpu.SemaphoreType.DMA((3 * n,))
    return pl.pallas_call(
        body, name="gather_weights", in_specs=[HBM] * n, out_specs=[HBM] * n,
        out_shape=[jax.ShapeDtypeStruct((N_CHIPS,) + s.shape, s.dtype) for s in shards],
        scratch_shapes=[sems, sems, sems, sems, pltpu.SemaphoreType.DMA((n,))],
        compiler_params=pltpu.CompilerParams(has_side_effects=True),
    )(*shards)


def _sibling_exchange(srcs, name, after=None):
    n = len(srcs)
    extra = [] if after is None else [after]

    def body(*refs):
        src_refs, outs = refs[:n], refs[n + len(extra):2 * n + len(extra)]
        send_sems, recv_sems = refs[2 * n + len(extra):]
        sibling = (lax.axis_index("x"), lax.axis_index("y"), 1 - lax.axis_index("c"))
        copies = [_remote(src_refs[k], outs[k], send_sems, recv_sems, k, sibling) for k in range(n)]
        for cp in copies:
            cp.start()
        for cp in copies:
            cp.wait()

    sems = pltpu.SemaphoreType.DMA((n,))
    return pl.pallas_call(
        body, name=name, in_specs=[HBM] * n + [pl.BlockSpec(memory_space=pl.ANY)] * len(extra), out_specs=[HBM] * n,
        out_shape=[jax.ShapeDtypeStruct(s.shape, s.dtype) for s in srcs],
        scratch_shapes=[sems, sems],
        compiler_params=pltpu.CompilerParams(has_side_effects=True),
    )(*srcs, *extra)


SEM = pl.BlockSpec(memory_space=pltpu.SEMAPHORE)
DATAFLOW = pltpu.SideEffectType.DATAFLOW_SIDE_EFFECTING


def _exchange_copies(srcs, to_first, src_refs, land_refs, send_sems, recv_sems):
    x, y, c, me, chips = _mesh_place()
    n = len(srcs)
    sends, recvs = [], []
    for k in range(n):
        if k in to_first:
            base = 3 * n + 4 * to_first.index(k)
            sends.append((me != 0, pltpu.make_async_remote_copy(
                src_ref=src_refs[k], dst_ref=land_refs[k].at[me], send_sem=send_sems.at[base], recv_sem=recv_sems.at[base + me],
                device_id=(0, 0, c), device_id_type=MESH)))
            for s in range(1, N_CHIPS):
                recvs.append((me == 0, pltpu.make_async_remote_copy(
                    src_ref=src_refs[k], dst_ref=land_refs[k].at[s], send_sem=send_sems.at[base], recv_sem=recv_sems.at[base + s],
                    device_id=(s // 2, s % 2, c), device_id_type=MESH)))
        else:
            slab = (lambda t, k=k: src_refs[k]) if srcs[k].ndim == 2 else (lambda t, k=k: src_refs[k].at[t])
            for j, (px, py) in enumerate(chips):
                sends.append((None, _remote(slab(2 * px + py), land_refs[k].at[me], send_sems, recv_sems, 3 * k + j, (px, py, c))))
                recvs.append((None, _remote(slab(me), land_refs[k].at[2 * px + py], send_sems, recv_sems, 3 * k + j, (px, py, c))))
    return sends, recvs


def _when(pred, fn):
    if pred is None:
        fn()
    else:
        pl.when(pred)(fn)


def _exchange_start(srcs, to_first, name, after=None):
    n = len(srcs)
    n_sems = 3 * n + 4 * len(to_first)
    lands = [lax.empty((N_CHIPS,) + s.shape[-2:], s.dtype) for s in srcs]
    extra = [] if after is None else [after]

    def body(*refs):
        src_refs, land_refs = refs[:n], refs[n:2 * n]
        send_sems, recv_sems, token = refs[2 * n + len(extra)], refs[2 * n + len(extra) + 1], refs[-1]
        sends, _ = _exchange_copies(srcs, to_first, src_refs, land_refs, send_sems, recv_sems)
        for pred, cp in sends:
            _when(pred, cp.start)
        token[...] = jnp.zeros_like(token)

    hbm = lambda a: pltpu.HBM(a.shape, a.dtype)
    res = pl.pallas_call(
        body, name=name,
        out_shape=[pltpu.SemaphoreType.DMA((n_sems,)), pltpu.SemaphoreType.DMA((n_sems,))] + [hbm(a) for a in srcs + lands]
        + [jax.ShapeDtypeStruct((8, LANE), F32)],
        in_specs=[HBM] * (2 * n) + [pl.BlockSpec(memory_space=pl.ANY)] * len(extra),
        out_specs=[SEM, SEM] + [HBM] * (2 * n) + [pl.BlockSpec(memory_space=pltpu.VMEM)],
        input_output_aliases={i: 2 + i for i in range(2 * n)},
        compiler_params=pltpu.CompilerParams(has_side_effects=DATAFLOW),
    )(*[pltpu.with_memory_space_constraint(a, pltpu.HBM) for a in srcs + lands], *extra)
    return res[:-1], res[-1]


def _exchange_wait(srcs, to_first, started, after, name):
    n = len(srcs)
    send_sems, recv_sems, thru = started[0], started[1], started[2:]

    def body(*refs):
        src_refs, land_refs, send_ref, recv_ref = refs[:n], refs[n:2 * n], refs[2 * n], refs[2 * n + 1]
        sends, recvs = _exchange_copies(srcs, to_first, src_refs, land_refs, send_ref, recv_ref)
        for pred, cp in sends:
            _when(pred, cp.wait_send)
        for pred, cp in recvs:
            _when(pred, cp.wait_recv)

    res = pl.pallas_call(
        body, name=name, out_shape=[pltpu.HBM(a.shape, a.dtype) for a in thru],
        in_specs=[HBM] * (2 * n) + [SEM, SEM, pl.BlockSpec(memory_space=pl.ANY)], out_specs=[HBM] * (2 * n),
        input_output_aliases={i: i for i in range(2 * n)},
        compiler_params=pltpu.CompilerParams(has_side_effects=DATAFLOW),
    )(*thru, send_sems, recv_sems, after)
    return res[n:]


ROW_TILE = 256
COL_TILE = 256


def _block_tiling(a, b):
    if a <= ROW_TILE or a % ROW_TILE == 0:
        ta = min(a, ROW_TILE)
        return a // ta, (ta, b), lambda i: (i, 0)
    return b // COL_TILE, (a, COL_TILE), lambda i: (0, i)


def _sum_landed(land, own, name):
    _, a, b = land.shape
    steps, tile, at = _block_tiling(a, b)

    def body(p_ref, own_ref, o_ref):
        me = 2 * lax.axis_index("x") + lax.axis_index("y")
        own = own_ref[...].astype(F32)
        slot = lambda t: jnp.where(me == t, own, p_ref[t].astype(F32))
        o_ref[...] = ((slot(0) + slot(1)) + slot(2)) + slot(3)

    return pl.pallas_call(
        body, name=name, grid=(steps,),
        in_specs=[pl.BlockSpec((N_CHIPS,) + tile, lambda i: (0,) + at(i)), pl.BlockSpec(tile, at)], out_specs=pl.BlockSpec(tile, at),
        out_shape=jax.ShapeDtypeStruct((a, b), F32), compiler_params=_params(("parallel",)),
    )(land, own)


def _sum_to_first(land, own, name):
    _, r, b = land.shape
    steps, tile, at = _block_tiling(2 * r, b)
    assert tile[0] == 2 * r, "a step takes whole columns"

    def body(p_ref, own_ref, o_ref):
        o_ref[...] = jnp.zeros_like(o_ref)

        @pl.when(2 * lax.axis_index("x") + lax.axis_index("y") == 0)
        def _():
            f = lambda t: p_ref[t].astype(F32)
            rows = pl.ds(pl.multiple_of(lax.axis_index("c") * r, 8), r)
            o_ref[rows, :] = ((own_ref[...].astype(F32) + f(1)) + f(2)) + f(3)

    return pl.pallas_call(
        body, name=name, grid=(steps,),
        in_specs=[pl.BlockSpec((N_CHIPS, r, tile[1]), lambda i: (0,) + at(i)), pl.BlockSpec((r, tile[1]), at)],
        out_specs=pl.BlockSpec(tile, at),
        out_shape=jax.ShapeDtypeStruct((2 * r, b), F32), compiler_params=_params(("parallel",)),
    )(land, own)


def _add_cast(a, b, name):
    def body(a_ref, b_ref, o_ref):
        o_ref[...] = (a_ref[...] + b_ref[...]).astype(BF16)

    return pl.pallas_call(body, name=name, out_shape=jax.ShapeDtypeStruct(a.shape, BF16),
                          compiler_params=_params(()))(a, b)


class _LaterWeights:
    QKV = ("w_uq", "w_ukv")
    MID = ("w_branch_a", "w_branch_b", "w_out")

    def __init__(self, blocks, after):
        self.blocks = blocks
        self.qkv_started, t1 = _exchange_start([blocks[n] for n in self.QKV], (), "weights_qkv_start", after)
        self.mid_started, t2 = _exchange_start([blocks[n] for n in self.MID], (), "weights_mid_start", after)
        self.tokens = [t1, t2]

    def _whole(self, names, started, after, name):
        landed = _exchange_wait([self.blocks[n] for n in names], (), started, after, name)
        me = 2 * lax.axis_index("x") + lax.axis_index("y")
        return [_join_chips(n, lax.dynamic_update_index_in_dim(land, self.blocks[n], me, 0)) for n, land in zip(names, landed)]

    def qkv(self, after):
        return self._whole(self.QKV, self.qkv_started, after, "weights_qkv_wait")

    def mid(self, after):
        return self._whole(self.MID, self.mid_started, after, "weights_mid_wait")


class _GradExchange:
    EARLY = ("w_in", "w_branch_a", "w_branch_b", "w_out")
    LATE = ("w_uq", "w_ukv")

    def __init__(self, state):
        self.state = state
        self.outs = {}

    @staticmethod
    def _own(slabs):
        return lax.dynamic_index_in_dim(slabs, 2 * lax.axis_index("x") + lax.axis_index("y"), axis=0, keepdims=False)

    def start_early(self, g):
        full = jnp.concatenate([jnp.zeros((QKV_ROWS, D), F32), g["w_in_rest"]], axis=0)
        g = dict(g, w_in=full)
        self.early = [_split_by_chip(n, g[n]).astype(BF16) for n in self.EARLY]
        self.early_started, token = _exchange_start(self.early, (), "grads_early_start")
        return token

    def start_late(self, g):
        self.early_landed = _exchange_wait(self.early, (), self.early_started, g["w_uq"], "grads_early_wait")
        half = QKV_ROWS // 2
        c = lax.axis_index("c")
        mine = lax.dynamic_slice_in_dim(g["w_in_qkv"], c * half, half, axis=0)
        (theirs,) = _sibling_exchange([lax.dynamic_slice_in_dim(g["w_in_qkv"], (1 - c) * half, half, axis=0)], "sibling_qkv_rows")
        self.late = [_split_by_chip(n, g[n]).astype(BF16) for n in self.LATE] + [_add_cast(mine, theirs, "add_qkv_rows")]
        self.late_started, token = _exchange_start(self.late, (2,), "grads_late_start")
        mine = [_sum_landed(land, self._own(slabs), "sum_" + n) for n, slabs, land in zip(self.EARLY, self.early, self.early_landed)]
        theirs = _sibling_exchange(mine, "sibling_early", after=token)
        self.w_in_early = (mine[0], theirs[0])
        for n, a, b in list(zip(self.EARLY, mine, theirs))[1:]:
            self.outs[n] = _adamw(a, b, *self.state[n], "adamw_" + n)
        return self.outs[self.EARLY[-1]][0]

    def finish(self, after):
        late_landed = _exchange_wait(self.late, (2,), self.late_started, after, "grads_late_wait")
        sums = {n: _sum_landed(land, self._own(slabs), "sum_" + n) for n, slabs, land in zip(self.LATE, self.late, late_landed)}
        sums["w_in_top"] = _sum_to_first(late_landed[2], self.late[2], "sum_w_in_top")
        return sums


def _adamw_math(g, w, m, v):
    nm = ADAM_B1 * m + (1.0 - ADAM_B1) * g
    nv = ADAM_B2 * v + (1.0 - ADAM_B2) * (g * g)
    m_hat = nm / (1.0 - ADAM_B1 ** ADAM_STEP)
    v_hat = nv / (1.0 - ADAM_B2 ** ADAM_STEP)
    return -ADAM_LR * (m_hat / (jnp.sqrt(v_hat) + ADAM_EPS) + ADAM_WD * w), nm, nv


def _adamw(p_mine, p_sibling, w, m, v, name, top=None):
    a, b = p_mine.shape
    steps, tile, at = _block_tiling(a, b)
    extra = [] if top is None else list(top)

    def body(a_ref, b_ref, w_ref, m_ref, v_ref, *refs):
        g_ref, d_ref, nm_ref, nv_ref = refs[len(extra):]
        g = a_ref[...] + b_ref[...]
        if extra:
            r = refs[0].shape[0]
            g = jnp.concatenate([g[:r] + (refs[0][...] + refs[1][...]), g[r:]], axis=0)
        g_ref[...] = g
        d_ref[...], nm_ref[...], nv_ref[...] = _adamw_math(g, w_ref[...], m_ref[...], v_ref[...])

    spec = pl.BlockSpec(tile, at)
    if extra:
        assert tile[0] == a, "the rows on top need whole columns in a step"
    top_spec = [pl.BlockSpec((t.shape[0], tile[1]), at) for t in extra]
    sds = jax.ShapeDtypeStruct((a, b), F32)
    return pl.pallas_call(
        body, name=name, grid=(steps,), in_specs=[spec] * 5 + top_spec, out_specs=[spec] * 4, out_shape=[sds] * 4,
        compiler_params=_params(("parallel",)),
    )(p_mine, p_sibling, w, m, v, *extra)


LOSS_AT = (2, 1024)


def _vec_pack(vg, loss):
    names = [n for n, _, _ in VEC_ROWS]

    def body(*refs):
        o_ref = refs[-1]
        lb_ref, loss_ref = refs[len(names)], refs[len(names) + 1]
        o_ref[...] = jnp.zeros_like(o_ref)
        o_ref[LOSS_AT[0]:LOSS_AT[0] + 1, LOSS_AT[1]:LOSS_AT[1] + LANE] = jnp.broadcast_to(loss_ref[...], (1, LANE))
        for (name, row, size), ref in zip(VEC_ROWS, refs):
            if name == "g_hgrn":
                r = lax.broadcasted_iota(jnp.int32, (NH * V_DIM, LANE), 0)
                c = lax.broadcasted_iota(jnp.int32, (NH * V_DIM, LANE), 1)
                fold = ((r % V_DIM) == c).astype(F32)
                o_ref[row:row + 1, 0:LANE] = jnp.dot(ref[...], fold, precision=HIGHEST, preferred_element_type=F32)
            else:
                o_ref[row:row + 1, 0:size] = ref[...]
        o_ref[VEC_LB_ROW:VEC_LB_ROW + 2, 0:512] = lb_ref[...]

    return pl.pallas_call(body, name="vec_pack", out_shape=jax.ShapeDtypeStruct(VEC_SHAPE, F32))(
        *[vg[n] for n in names], vg["lb_logits"], loss)


def _adamw_vec(p_mine, p_sibling, w, m, v):
    names = [n for n, _, _ in VEC_ROWS] + ["lb_logits"]
    k = len(names)

    def body(a_ref, b_ref, *refs):
        ins, outs = refs[:3 * k], refs[3 * k:]
        at = (slice(LOSS_AT[0], LOSS_AT[0] + 1), slice(LOSS_AT[1], LOSS_AT[1] + LANE))
        outs[-1][...] = a_ref[at] + b_ref[at]
        for i, name in enumerate(names):
            if name == "lb_logits":
                rows, cols = slice(VEC_LB_ROW, VEC_LB_ROW + 2), slice(0, 512)
            else:
                _, row, size = VEC_ROWS[i]
                rows, cols = slice(row, row + 1), slice(0, size)
            g = a_ref[rows, cols] + b_ref[rows, cols]
            d, nm, nv = _adamw_math(g, ins[i][...], ins[k + i][...], ins[2 * k + i][...])
            for o_ref, val in zip(outs[4 * i:4 * i + 4], (g, d, nm, nv)):
                o_ref[...] = val

    shapes = [jax.ShapeDtypeStruct(w[n].shape, F32) for n in names for _ in range(4)] + [jax.ShapeDtypeStruct((1, LANE), F32)]
    res = pl.pallas_call(body, name="adamw_vec", out_shape=shapes)(
        p_mine, p_sibling, *[w[n] for n in names], *[m[n] for n in names], *[v[n] for n in names])
    return [{n: res[4 * i + j] for i, n in enumerate(names)} for j in range(4)], res[-1]


WEIGHTS = ("g_pre", "w_in", "b_gate", "g_q", "w_uq", "g_kv", "w_ukv", "lb_logits", "g_hgrn", "w_branch_a", "w_branch_b", "w_out", "g_post")


def kernel(x, g_pre, w_in, b_gate, g_q, w_uq, g_kv, w_ukv, lb_logits, g_hgrn, w_branch_a, w_branch_b, w_out, g_post, loss_target, m_g_pre, m_w_in, m_b_gate, m_g_q, m_w_uq, m_g_kv, m_w_ukv, m_lb_logits, m_g_hgrn, m_w_branch_a, m_w_branch_b, m_w_out, m_g_post, v_g_pre, v_w_in, v_b_gate, v_g_q, v_w_uq, v_g_kv, v_w_ukv, v_lb_logits, v_g_hgrn, v_w_branch_a, v_w_branch_b, v_w_out, v_g_post):
    w = dict(g_pre=g_pre, w_in=w_in, b_gate=b_gate, g_q=g_q, w_uq=w_uq, g_kv=g_kv, w_ukv=w_ukv, lb_logits=lb_logits, g_hgrn=g_hgrn,
             w_branch_a=w_branch_a, w_branch_b=w_branch_b, w_out=w_out, g_post=g_post)
    m = dict(g_pre=m_g_pre, w_in=m_w_in, b_gate=m_b_gate, g_q=m_g_q, w_uq=m_w_uq, g_kv=m_g_kv, w_ukv=m_w_ukv, lb_logits=m_lb_logits,
             g_hgrn=m_g_hgrn, w_branch_a=m_w_branch_a, w_branch_b=m_w_branch_b, w_out=m_w_out, g_post=m_g_post)
    v = dict(g_pre=v_g_pre, w_in=v_w_in, b_gate=v_b_gate, g_q=v_g_q, w_uq=v_w_uq, g_kv=v_g_kv, w_ukv=v_w_ukv, lb_logits=v_lb_logits,
             g_hgrn=v_g_hgrn, w_branch_a=v_w_branch_a, w_branch_b=v_w_branch_b, w_out=v_w_out, g_post=v_g_post)
    blocks = {n: _to_block(n, w[n]).astype(BF16) for n in BIG}
    (w_in_all,) = _gather_weights([blocks["w_in"]], [GATHER_SPLIT_AXIS["w_in"]])
    weights = _LaterWeights(blocks, w_in_all)
    state = {n: [_to_block(n, t[n]) for t in (w, m, v)] for n in BIG}
    exchange = _GradExchange(state)
    loss, grad_x, _, vec_grads = _local_step(
        x[0], loss_target[0], g_pre, _join_chips("w_in", w_in_all), b_gate, g_q, g_kv, lb_logits, g_hgrn, g_post, weights, exchange)
    vec = _vec_pack(vec_grads, loss)
    vec_started, token = _exchange_start([vec], (), "vec_start")
    sums = exchange.finish(token)
    rest = tuple(sums)
    (vec_landed,) = _exchange_wait([vec], (), vec_started, sums[rest[-1]], "vec_wait")
    mine = [sums[n] for n in rest] + [_sum_landed(vec_landed, vec, "sum_vec")]
    theirs = _sibling_exchange(mine, "sibling_grads")
    done = dict(exchange.outs)
    for k, n in enumerate(rest):
        if n == "w_in_top":
            done["w_in"] = _adamw(*exchange.w_in_early, *state["w_in"], "adamw_w_in", top=(mine[k], theirs[k]))
        else:
            done[n] = _adamw(mine[k], theirs[k], *state[n], "adamw_" + n)
    outs = [{}, {}, {}, {}]
    for n in BIG:
        for o, val in zip(outs, done[n]):
            o[n] = _from_block(n, val)
    vec_outs, total = _adamw_vec(mine[-1], theirs[-1], w, m, v)
    for o, vals in zip(outs, vec_outs):
        o.update(vals)
    return (total[0, 0], grad_x[None], *[o[n] for o in outs for n in WEIGHTS])
```

```python
import functools
import math

import numpy as np
import jax
import jax.numpy as jnp
from jax import lax
from jax.experimental import pallas as pl
from jax.experimental.pallas import tpu as pltpu

F32 = jnp.float32
BF16 = jnp.bfloat16
HIGHEST = lax.Precision.HIGHEST

D = 1024
NH = 8
QK_NOPE, QK_ROPE, V_DIM = 64, 32, 64
Q_LORA, KV_LORA = 768, 256
CHUNK = 64
HG_BLOCK = 32
EPS = 1e-6
D_IN = 5664
LANE = 128
P_MERGE, P_GA, P_HQ, P_HF, P_HI, P_GB, P_CQ, P_CKV, P_KPE = 0, 2048, 2560, 3072, 3584, 4096, 4608, 5376, 5632
D_P = 5760
O_CQ, O_CKV, O_KPE, O_GA, O_HQ, O_HF, O_HI, O_GB, O_MERGE = 0, 768, 1024, 1056, 1568, 2080, 2592, 3104, 3616

TM = 512
TM_MID = 256
TQ = 1024
ONES_LANE = (LANE - 1, 0)
TH = 256
HG_PAIRS = 4
VMEM_LIMIT = 56 * 1024 * 1024

ADAM_LR, ADAM_B1, ADAM_B2, ADAM_EPS, ADAM_WD, ADAM_STEP = 0.001, 0.9, 0.999, 1e-08, 0.01, 10

NT_DIMS = (((1,), (1,)), ((), ()))
TN_DIMS = (((0,), (0,)), ((), ()))


def _params(sem):
    return pltpu.CompilerParams(dimension_semantics=sem, vmem_limit_bytes=VMEM_LIMIT)


def _mm(a, b):
    return jnp.dot(a, b, preferred_element_type=F32)


def _mm_nt(a, b):
    return lax.dot_general(a, b, NT_DIMS, preferred_element_type=F32)


def _mm_tn(a, b):
    return lax.dot_general(a, b, TN_DIMS, preferred_element_type=F32)


def _sigmoid(z):
    return jax.nn.sigmoid(z)


def _rope(v, c, s1, s2):
    return v * c + pltpu.roll(v, 112, 1) * s1 + pltpu.roll(v, 16, 1) * s2


def _rope_t(dy, c, s1, s2):
    return dy * c + pltpu.roll(dy * s1, 16, 1) + pltpu.roll(dy * s2, 112, 1)


def _rope_tables(s):
    inv = 10000.0 ** (-jnp.arange(0, QK_ROPE, 2, dtype=F32) / QK_ROPE)
    ang = jnp.arange(s, dtype=F32)[:, None] * inv[None, :]
    cos, sin = jnp.cos(ang), jnp.sin(ang)
    z64, z32, o64, o32 = jnp.zeros((s, 64), F32), jnp.zeros((s, 32), F32), jnp.ones((s, 64), F32), jnp.ones((s, 32), F32)
    z16 = jnp.zeros((s, 16), F32)
    c = jnp.concatenate([o64, cos, cos, o32], axis=1)
    s1 = jnp.concatenate([z64, -sin, z16, z32], axis=1)
    s2 = jnp.concatenate([z64, z16, sin, z32], axis=1)
    return c, s1, s2


def _front_fwd(x, g_pre, w_in_pt, tokens=()):
    s = x.shape[0]
    tokens = list(tokens)

    def body(x_ref, g_ref, w_ref, *refs):
        o_ref, h_ref = refs[len(tokens):]
        xv = x_ref[...]
        r = lax.rsqrt(jnp.mean(xv * xv, axis=-1, keepdims=True) + EPS)
        h = ((xv * r) * g_ref[...]).astype(BF16)
        h_ref[...] = h
        o_ref[...] = _mm_nt(h, w_ref[...])

    return pl.pallas_call(
        body, name="front_fwd", grid=(s // TM,),
        in_specs=[pl.BlockSpec((TM, D), lambda i: (i, 0)), pl.BlockSpec((1, D), lambda i: (0, 0)),
                  pl.BlockSpec((D_P, D), lambda i: (0, 0))] + [pl.BlockSpec((8, LANE), lambda i: (0, 0))] * len(tokens),
        out_specs=[pl.BlockSpec((TM, D_P), lambda i: (i, 0)), pl.BlockSpec((TM, D), lambda i: (i, 0))],
        out_shape=[jax.ShapeDtypeStruct((s, D_P), F32), jax.ShapeDtypeStruct((s, D), BF16)],
        compiler_params=_params(("parallel",)),
    )(x, g_pre, w_in_pt, *tokens)


def _norm_rows(v, g):
    r = lax.rsqrt(jnp.mean(v * v, axis=-1, keepdims=True) + EPS)
    return (v * r) * g, r


def _qkv_fwd(proj, g_q, g_kv, w_uq_p, w_k_p, w_v_p, rc, rs1, rs2):
    s = proj.shape[0]

    def body(cq_ref, ckv_ref, kpe_ref, gq_ref, gkv_ref, wq_ref, wk_ref, wv_ref, c_ref, s1_ref, s2_ref, q_ref, k_ref, v_ref):
        c, s1, s2 = c_ref[...], s1_ref[...], s2_ref[...]
        cqn, _ = _norm_rows(cq_ref[...], gq_ref[...])
        ckvn, _ = _norm_rows(ckv_ref[...], gkv_ref[...])
        ckvn = ckvn.astype(BF16)
        qf = _mm(cqn.astype(BF16), wq_ref[...])
        kf = _mm(ckvn, wk_ref[...])
        vf = _mm(ckvn, wv_ref[...])
        kpe = _rope(kpe_ref[...], c, s1, s2)
        lane = lax.broadcasted_iota(jnp.int32, (TM, LANE), 1)
        for h in range(NH):
            blk = slice(h * LANE, (h + 1) * LANE)
            q_ref[h] = _rope(qf[:, blk], c, s1, s2).astype(BF16)
            k_ref[h] = (kf[:, blk] + kpe).astype(BF16)
            v_ref[h] = jnp.where(lane == ONES_LANE[h % 2], 1.0, vf[:, blk]).astype(BF16)

    row = lambda w, j: pl.BlockSpec((TM, w), lambda i: (i, j))
    full = lambda a: pl.BlockSpec(a.shape, lambda i: (0,) * a.ndim)
    hs = jax.ShapeDtypeStruct((NH, s, LANE), BF16)
    return pl.pallas_call(
        body, name="qkv_fwd", grid=(s // TM,),
        in_specs=[row(Q_LORA, P_CQ // Q_LORA), row(KV_LORA, P_CKV // KV_LORA), row(LANE, P_KPE // LANE),
                  full(g_q), full(g_kv), full(w_uq_p), full(w_k_p), full(w_v_p), row(LANE, 0), row(LANE, 0), row(LANE, 0)],
        out_specs=[pl.BlockSpec((NH, TM, LANE), lambda i: (0, i, 0))] * 3,
        out_shape=[hs, hs, hs],
        compiler_params=_params(("parallel",)),
    )(proj, proj, proj, g_q, g_kv, w_uq_p, w_k_p, w_v_p, rc, rs1, rs2)


LOG2E = 1.4426950408889634
QK_SCALE2 = LOG2E / math.sqrt(QK_NOPE + QK_ROPE)


HQ = TQ // 2


def _diag_visible(n):
    row = lax.broadcasted_iota(jnp.int32, (n, n), 0)
    col = lax.broadcasted_iota(jnp.int32, (n, n), 1)
    return (col // CHUNK) <= (row // CHUNK)


def _attn_fwd(q, k, vv):
    s = q.shape[1]

    def body(q_ref, k_ref, v_ref, o_ref, lse_ref):
        i = pl.program_id(1)
        qs = (q_ref[0], q_ref[1])

        def tile(hh, t, carry, diag):
            m, acc = carry
            rows = pl.ds(pl.multiple_of(t * TQ, TQ), TQ)
            sc = _mm_nt(qs[hh], k_ref[hh, rows, :])
            if diag:
                sc = jnp.where(_diag_visible(TQ), sc, -jnp.inf)
            m_new = jnp.maximum(m, jnp.max(sc, axis=-1, keepdims=True))
            alpha = jnp.exp2((m - m_new) * QK_SCALE2)
            p = jnp.exp2((sc - m_new) * QK_SCALE2).astype(BF16)
            acc = alpha * acc + _mm(p, v_ref[hh, rows, :])
            return m_new, acc

        def step(t, carry):
            return tile(0, t, carry[0], False), tile(1, t, carry[1], False)

        init = (jnp.full((TQ, 1), -jnp.inf, F32), jnp.zeros((TQ, LANE), F32))
        carry = lax.fori_loop(0, i, step, (init, init))
        lane = lax.broadcasted_iota(jnp.int32, (TQ, LANE), 1)
        out = jnp.zeros((TQ, LANE), F32)
        for hh in range(2):
            m, acc = tile(hh, i, carry[hh], True)
            l = jnp.sum(jnp.where(lane == ONES_LANE[hh], acc, 0.0), axis=-1, keepdims=True)
            out = out + jnp.where((lane < V_DIM) == (hh == 0), acc, 0.0) / l
            lse_ref[hh] = jnp.broadcast_to(m * QK_SCALE2 + jnp.log(l) * LOG2E, (TQ, LANE))
        o_ref[...] = out

    return pl.pallas_call(
        body, name="attn_fwd", grid=(NH // 2, s // TQ),
        in_specs=[pl.BlockSpec((2, TQ, LANE), lambda p, i: (p, i, 0)), pl.BlockSpec((2, s, LANE), lambda p, i: (p, 0, 0)),
                  pl.BlockSpec((2, s, LANE), lambda p, i: (p, 0, 0))],
        out_specs=[pl.BlockSpec((TQ, LANE), lambda p, i: (i, p)), pl.BlockSpec((2, TQ, LANE), lambda p, i: (p, i, 0))],
        out_shape=[jax.ShapeDtypeStruct((s, NH * V_DIM), F32), jax.ShapeDtypeStruct((NH, s, LANE), F32)],
        compiler_params=_params(("parallel", "parallel")),
    )(q, k, vv)


def _lower_bound(lbl):
    a0, a1 = lbl[0:1, :], lbl[1:2, :]
    mx = jnp.maximum(a0, a1)
    e0, e1 = jnp.exp(a0 - mx), jnp.exp(a1 - mx)
    return e0 / (e0 + e1)


def _chunk_cumsum(v, reverse=False):
    pos = lax.broadcasted_iota(jnp.int32, v.shape, 0) % HG_BLOCK
    s = 1
    while s < HG_BLOCK:
        if reverse:
            v = v + jnp.where(pos < HG_BLOCK - s, pltpu.roll(v, TH - s, 0), 0.0)
        else:
            v = v + jnp.where(pos >= s, pltpu.roll(v, s, 0), 0.0)
        s *= 2
    return v


def _hgrn_gates(hq, hf, lb):
    sig = _sigmoid(hf)
    f = lb + (1.0 - lb) * sig
    g = jnp.log(f)
    kk = 1.0 - f
    r = lax.broadcasted_iota(jnp.int32, (TH, TH), 0)
    c = lax.broadcasted_iota(jnp.int32, (TH, TH), 1)
    tri = ((r // HG_BLOCK) == (c // HG_BLOCK)) & (r >= c)
    cum = _chunk_cumsum(g)
    nch = TH // HG_BLOCK
    total = _chunks(cum)[:, HG_BLOCK - 1:HG_BLOCK, :]
    lastb = jnp.broadcast_to(total, (nch, HG_BLOCK, LANE)).reshape(TH, LANE)
    e, ei, ee = jnp.exp(cum), jnp.exp(-cum), jnp.exp(lastb - cum)
    return dict(sig=sig, f=f, kk=kk, tri=tri, cum=cum, total=total, e=e, ei=ei, ee=ee, qd=hq * e, ki=kk * ei, ke=kk * ee)


def _chunks(v):
    return v.reshape(TH // HG_BLOCK, HG_BLOCK, v.shape[-1])


def _bmm_nt(a, b):
    return lax.dot_general(a, b, (((2,), (2,)), ((0,), (0,))), preferred_element_type=F32)


def _bmm_nn(a, b):
    return lax.dot_general(a, b, (((2,), (1,)), ((0,), (0,))), preferred_element_type=F32)


def _bmm_tn(a, b):
    return lax.dot_general(a, b, (((1,), (1,)), ((0,), (0,))), preferred_element_type=F32)


def _pair_masks():
    lane = lax.broadcasted_iota(jnp.int32, (TH, LANE), 1)
    kr = lax.broadcasted_iota(jnp.int32, (LANE, LANE), 0)
    kc = lax.broadcasted_iota(jnp.int32, (LANE, LANE), 1)
    return lane < 64, (kr // 64) == (kc // 64)


def _hgrn_fwd(proj, lbl):
    s = proj.shape[0]
    nch = TH // HG_BLOCK

    def body(hq_ref, hf_ref, hi_ref, lbl_ref, o_ref, st_ref, st):
        @pl.when(pl.program_id(1) == 0)
        def _():
            st[...] = jnp.zeros_like(st)

        m0, bd = _pair_masks()
        for u in range(HG_PAIRS):
            lanes = slice(u * LANE, (u + 1) * LANE)
            lb = _lower_bound(lbl_ref[:, lanes])
            gt = _hgrn_gates(hq_ref[:, lanes], hf_ref[:, lanes], lb)
            v_b = hi_ref[:, lanes].astype(BF16)
            qd, ki_b, ke_b = gt["qd"], gt["ki"].astype(BF16), gt["ke"].astype(BF16)
            qd_b = qd.astype(BF16)
            o = jnp.zeros((TH, LANE), F32)
            for hh in range(2):
                mh = m0 if hh == 0 else jnp.logical_not(m0)
                a = jnp.where(gt["tri"], _mm_nt(jnp.where(mh, qd, 0.0).astype(BF16), ki_b), 0.0)
                o = jnp.where(mh, _mm(a.astype(BF16), v_b), o)
            upd = _bmm_tn(_chunks(v_b), _chunks(ke_b))
            decay = jnp.exp(gt["total"])
            cur, entering = st[u], []
            for n in range(nch):
                entering.append(cur)
                cur = decay[n] * cur + jnp.where(bd, upd[n], 0.0)
            st[u] = cur
            entering = jnp.stack(entering)
            st_ref[u] = entering
            o_ref[:, lanes] = o + _bmm_nt(_chunks(qd_b), entering.astype(BF16)).reshape(TH, LANE)

    wide = HG_PAIRS * LANE
    col = lambda base: pl.BlockSpec((TH, wide), lambda p, i: (i, base // wide + p))
    return pl.pallas_call(
        body, name="hgrn_fwd", grid=(NH // 2 // HG_PAIRS, s // TH),
        in_specs=[col(P_HQ), col(P_HF), col(P_HI), pl.BlockSpec((2, wide), lambda p, i: (0, p))],
        out_specs=[pl.BlockSpec((TH, wide), lambda p, i: (i, p)),
                   pl.BlockSpec((HG_PAIRS, nch, LANE, LANE), lambda p, i: (p, i, 0, 0))],
        out_shape=[jax.ShapeDtypeStruct((s, 512), F32), jax.ShapeDtypeStruct((NH // 2, s // HG_BLOCK, LANE, LANE), F32)],
        scratch_shapes=[pltpu.VMEM((HG_PAIRS, LANE, LANE), F32)],
        compiler_params=_params(("parallel", "arbitrary")),
    )(proj, proj, proj, lbl)


def _group_sum(v):
    low = lax.broadcasted_iota(jnp.int32, (v.shape[0], LANE), 1) < V_DIM
    blocks = []
    for b in range(v.shape[1] // LANE):
        blk = v[:, b * LANE:(b + 1) * LANE]
        s_low = jnp.sum(jnp.where(low, blk, 0.0), axis=-1, keepdims=True)
        s_high = jnp.sum(jnp.where(low, 0.0, blk), axis=-1, keepdims=True)
        blocks.append(jnp.where(low, s_low, s_high))
    return jnp.concatenate(blocks, axis=1)


def _dsilu(z, sg):
    return sg * (1.0 + z * (1.0 - sg))


def _mid(proj, attn, o_raw, x, tgt, g_hg, b_gate, g_post, wa, wb, w_out):
    s = x.shape[0]

    def body(attn_ref, ga_ref, o_ref, gb_ref, mg_ref, x_ref, t_ref, ghg_ref, bg_ref, gp_ref, wa_ref, wb_ref, wo_ref,
             loss_ref, dout_ref, dattn_ref, dga_ref, dor_ref, dgb_ref, dmg_ref, dwo_ref, dwa_ref, dwb_ref, dgp_ref, dbg_ref, dghg_ref):
        @pl.when(pl.program_id(0) == 0)
        def _():
            for rf in (loss_ref, dwo_ref, dwa_ref, dwb_ref, dgp_ref, dbg_ref, dghg_ref):
                rf[...] = jnp.zeros_like(rf)

        attn, za, orw, zb = attn_ref[...], ga_ref[...], o_ref[...], gb_ref[...]
        ghg, gp = ghg_ref[...], gp_ref[...]
        sga, sgb = _sigmoid(za), _sigmoid(zb)
        sa, sb = za * sga, zb * sgb
        ga = attn * sa
        rh = lax.rsqrt(_group_sum(orw * orw) * (1.0 / V_DIM) + EPS)
        on = (orw * rh) * ghg
        gb = on * sb
        ga_b, gb_b = ga.astype(BF16), gb.astype(BF16)
        ya = _mm(ga_b, wa_ref[...])
        yb = _mm(gb_b, wb_ref[...])
        gates = _sigmoid(mg_ref[...] + bg_ref[...])
        g0, g1 = gates[:, :D], gates[:, D:]
        m_b = (g0 * ya + g1 * yb).astype(BF16)
        y = _mm(m_b, wo_ref[...])
        ry = lax.rsqrt(jnp.mean(y * y, axis=-1, keepdims=True) + EPS)
        out = x_ref[...] + (y * ry) * gp
        err = out - t_ref[...]
        loss_ref[...] += 0.5 * jnp.sum(jnp.mean(err * err, axis=-1, keepdims=True), axis=0, keepdims=True)
        dout = err * (1.0 / D)
        dout_ref[...] = dout
        dgp_ref[...] += jnp.sum(dout * (y * ry), axis=0, keepdims=True)
        dgy = dout * gp
        dy = ry * dgy - y * (ry * ry * ry) * jnp.mean(y * dgy, axis=-1, keepdims=True)
        dy_b = dy.astype(BF16)
        dwo_ref[...] += _mm_tn(m_b, dy_b)
        dm = _mm_nt(dy_b, wo_ref[...])
        dya, dyb = dm * g0, dm * g1
        dg0, dg1 = dm * ya, dm * yb
        dmg = jnp.concatenate([dg0 * g0 * (1.0 - g0), dg1 * g1 * (1.0 - g1)], axis=1)
        dmg_ref[...] = dmg.astype(BF16)
        dbg_ref[...] += jnp.sum(dmg, axis=0, keepdims=True)
        dya_b, dyb_b = dya.astype(BF16), dyb.astype(BF16)
        dwa_ref[...] += _mm_tn(ga_b, dya_b)
        dwb_ref[...] += _mm_tn(gb_b, dyb_b)
        dga = _mm_nt(dya_b, wa_ref[...])
        dgb = _mm_nt(dyb_b, wb_ref[...])
        dattn_ref[...] = dga * sa
        dga_ref[...] = (dga * attn * _dsilu(za, sga)).astype(BF16)
        dgb_ref[...] = (dgb * on * _dsilu(zb, sgb)).astype(BF16)
        don = dgb * sb
        dghg_ref[...] += jnp.sum(don * (orw * rh), axis=0, keepdims=True)
        dgo = don * ghg
        dor_ref[...] = rh * dgo - orw * (rh * rh * rh) * (_group_sum(orw * dgo) * (1.0 / V_DIM))

    row = lambda w, j=0: pl.BlockSpec((TM_MID, w), lambda i: (i, j))
    full = lambda a: pl.BlockSpec(a.shape, lambda i: (0,) * a.ndim)
    acc = lambda shape: pl.BlockSpec(shape, lambda i: (0, 0))
    sds = jax.ShapeDtypeStruct
    return pl.pallas_call(
        body, name="mid", grid=(s // TM_MID,),
        in_specs=[row(512), row(512, P_GA // 512), row(512), row(512, P_GB // 512), row(2048, P_MERGE // 2048), row(D), row(D),
                  full(g_hg), full(b_gate), full(g_post), full(wa), full(wb), full(w_out)],
        out_specs=[acc((1, 1)), row(D), row(512), row(512), row(512), row(512), row(2048),
                   acc((D, D)), acc((512, D)), acc((512, D)), acc((1, D)), acc((1, 2048)), acc((1, 512))],
        out_shape=[sds((1, 1), F32), sds((s, D), F32), sds((s, 512), F32), sds((s, 512), BF16), sds((s, 512), F32), sds((s, 512), BF16),
                   sds((s, 2048), BF16), sds((D, D), F32), sds((512, D), F32), sds((512, D), F32), sds((1, D), F32),
                   sds((1, 2048), F32), sds((1, 512), F32)],
        compiler_params=_params(("arbitrary",)),
    )(attn, proj, o_raw, proj, proj, x, tgt, g_hg, b_gate, g_post, wa, wb, w_out)


def _attn_bwd(q, k, vv, attn, dattn, lse, token):
    s = q.shape[1]
    nt = s // TQ
    scale = 1.0 / math.sqrt(QK_NOPE + QK_ROPE)

    def body(q_ref, k_ref, v_ref, o_ref, do_ref, lse_ref, token_ref, dq_ref, dk_ref, dv_ref, do_s, delta_s):
        j = pl.program_id(1)

        @pl.when(j == 0)
        def _():
            dq_ref[...] = jnp.zeros_like(dq_ref)
            lane = lax.broadcasted_iota(jnp.int32, (TQ, LANE), 1)

            @pl.loop(0, nt)
            def _(i):
                rows = pl.ds(pl.multiple_of(i * TQ, TQ), TQ)
                do, o = do_ref[rows, :], o_ref[rows, :]
                for hh in range(2):
                    doh = jnp.where((lane < 64) if hh == 0 else (lane >= 64), do, 0.0)
                    do_s[hh, rows, :] = doh.astype(BF16)
                    delta_s[hh, rows, :] = jnp.broadcast_to(jnp.sum(doh * o, axis=-1, keepdims=True), (TQ, LANE))

        kjs, vjs = (k_ref[0], k_ref[1]), (v_ref[0], v_ref[1])

        def tile(hh, start, size, kj, vj, diag):
            rows = pl.ds(pl.multiple_of(start, size), size)
            wide = lambda a: jnp.concatenate([a] * (kj.shape[0] // LANE), axis=1)
            qi, do_b = q_ref[hh, rows, :], do_s[hh, rows, :]
            p = jnp.exp2(_mm_nt(qi, kj) * QK_SCALE2 - wide(lse_ref[hh, rows, :]))
            if diag:
                p = jnp.where(_diag_visible(size), p, 0.0)
            dv = _mm_tn(do_b, p.astype(BF16))
            ds_b = (p * (_mm_nt(do_b, vj) - wide(delta_s[hh, rows, :]))).astype(BF16)
            dk = _mm_tn(qi, ds_b)
            dq_ref[hh, rows, :] += _mm(ds_b, kj)
            return dk, dv

        def step(i, carry):
            new = [tile(hh, i * TQ, TQ, kjs[hh], vjs[hh], False) for hh in range(2)]
            return tuple((carry[hh][0] + new[hh][0], carry[hh][1] + new[hh][1]) for hh in range(2))

        def diagonal(hh):
            k0, k1, v0, v1 = kjs[hh][:HQ], kjs[hh][HQ:], vjs[hh][:HQ], vjs[hh][HQ:]
            a = tile(hh, j * TQ, HQ, k0, v0, True)
            b = tile(hh, j * TQ + HQ, HQ, k0, v0, False)
            c = tile(hh, j * TQ + HQ, HQ, k1, v1, True)
            return jnp.concatenate([a[0] + b[0], c[0]], axis=1), jnp.concatenate([a[1] + b[1], c[1]], axis=1)

        carry = lax.fori_loop(j + 1, nt, step, (diagonal(0), diagonal(1)))
        for hh in range(2):
            dk_ref[hh] = carry[hh][0].T * scale
            dv_ref[hh] = carry[hh][1].T

        @pl.when(j == nt - 1)
        def _():
            dq_ref[...] = dq_ref[...] * scale

    whole = pl.BlockSpec((2, s, LANE), lambda p, j: (p, 0, 0))
    tile_spec = pl.BlockSpec((2, TQ, LANE), lambda p, j: (p, j, 0))
    cols = pl.BlockSpec((s, LANE), lambda p, j: (0, p))
    hs = jax.ShapeDtypeStruct((NH, s, LANE), F32)
    return pl.pallas_call(
        body, name="attn_bwd", grid=(NH // 2, nt),
        in_specs=[whole, tile_spec, tile_spec, cols, cols, whole, pl.BlockSpec((8, LANE), lambda p, j: (0, 0))],
        out_specs=[whole, tile_spec, tile_spec],
        out_shape=[hs, hs, hs],
        scratch_shapes=[pltpu.VMEM((2, s, LANE), BF16), pltpu.VMEM((2, s, LANE), F32)],
        compiler_params=_params(("parallel", "arbitrary")),
    )(q, k, vv, attn, dattn, lse, token)


def _hgrn_bwd(proj, lbl, states, do_raw):
    s = proj.shape[0]
    nt = s // TH
    nch = TH // HG_BLOCK

    def body(hq_ref, hf_ref, hi_ref, lbl_ref, st_ref, do_ref, dh_ref, dlbl_ref, dst, dlb):
        step = pl.program_id(1)

        @pl.when(step == 0)
        def _():
            dst[...] = jnp.zeros_like(dst)
            dlb[...] = jnp.zeros_like(dlb)

        m0, bd = _pair_masks()
        for u in range(HG_PAIRS):
            lanes = slice(u * LANE, (u + 1) * LANE)
            lb = _lower_bound(lbl_ref[:, lanes])
            gt = _hgrn_gates(hq_ref[:, lanes], hf_ref[:, lanes], lb)
            do = do_ref[:, lanes]
            qd, ki, ke = gt["qd"], gt["ki"], gt["ke"]
            v_b, do_b = hi_ref[:, lanes].astype(BF16), do.astype(BF16)
            qd_b, ki_b, ke_b = qd.astype(BF16), ki.astype(BF16), ke.astype(BF16)
            dv = jnp.zeros((TH, LANE), F32)
            dqd = jnp.zeros((TH, LANE), F32)
            dki = jnp.zeros((TH, LANE), F32)
            for hh in range(2):
                mh = m0 if hh == 0 else jnp.logical_not(m0)
                a_b = jnp.where(gt["tri"], _mm_nt(jnp.where(mh, qd, 0.0).astype(BF16), ki_b), 0.0).astype(BF16)
                doh_b = jnp.where(mh, do, 0.0).astype(BF16)
                da_b = jnp.where(gt["tri"], _mm_nt(doh_b, v_b), 0.0).astype(BF16)
                dv = dv + _mm_tn(a_b, doh_b)
                dqd = jnp.where(mh, _mm(da_b, ki_b), dqd)
                dki = jnp.where(mh, _mm_tn(da_b, qd_b), dki)
            fed = _bmm_tn(_chunks(do_b), _chunks(qd_b))
            decay = jnp.exp(gt["total"])
            ds, leaving = dst[u], [None] * nch
            for n in reversed(range(nch)):
                leaving[n] = ds
                ds = decay[n] * ds + jnp.where(bd, fed[n], 0.0)
            dst[u] = ds
            leaving = jnp.stack(leaving)
            entering = st_ref[u]
            leaving_b = leaving.astype(BF16)
            dke3 = _bmm_nn(_chunks(v_b), leaving_b)
            dv = dv + _bmm_nt(_chunks(ke_b), leaving_b).reshape(TH, LANE)
            dqd = dqd + _bmm_nn(_chunks(do_b), entering.astype(BF16)).reshape(TH, LANE)
            dke = dke3.reshape(TH, LANE)
            dlast = (jnp.sum(dke3 * _chunks(ke), axis=1, keepdims=True)
                     + jnp.sum(leaving * entering, axis=1, keepdims=True) * decay)
            dk = dki * gt["ei"] + dke * gt["ee"]
            dcum = dqd * qd - dki * ki - dke * ke
            dg = _chunk_cumsum(dcum, reverse=True) + jnp.broadcast_to(dlast, (nch, HG_BLOCK, LANE)).reshape(TH, LANE)
            sig = gt["sig"]
            df = dg / gt["f"] - dk
            dlb[:, lanes] += jnp.sum(df * (1.0 - sig), axis=0, keepdims=True)
            dh_ref[0, :, lanes] = (dqd * gt["e"]).astype(BF16)
            dh_ref[1, :, lanes] = ((df * (1.0 - lb)) * sig * (1.0 - sig)).astype(BF16)
            dh_ref[2, :, lanes] = dv.astype(BF16)

        @pl.when(step == nt - 1)
        def _():
            lb = _lower_bound(lbl_ref[...])
            da0 = dlb[...] * lb * (1.0 - lb)
            dlbl_ref[...] = jnp.concatenate([da0, -da0], axis=0)

    wide = HG_PAIRS * LANE
    col = lambda base: pl.BlockSpec((TH, wide), lambda p, i: (nt - 1 - i, base // wide + p))
    tile = pl.BlockSpec((TH, wide), lambda p, i: (nt - 1 - i, p))
    sds = jax.ShapeDtypeStruct
    return pl.pallas_call(
        body, name="hgrn_bwd", grid=(NH // 2 // HG_PAIRS, nt),
        in_specs=[col(P_HQ), col(P_HF), col(P_HI), pl.BlockSpec((2, wide), lambda p, i: (0, p)),
                  pl.BlockSpec((HG_PAIRS, nch, LANE, LANE), lambda p, i: (p, nt - 1 - i, 0, 0)), tile],
        out_specs=[pl.BlockSpec((3, TH, wide), lambda p, i: (0, nt - 1 - i, p)), pl.BlockSpec((2, wide), lambda p, i: (0, p))],
        out_shape=[sds((3, s, 512), BF16), sds((2, 512), F32)],
        scratch_shapes=[pltpu.VMEM((HG_PAIRS, LANE, LANE), F32), pltpu.VMEM((1, wide), F32)],
        compiler_params=_params(("parallel", "arbitrary")),
    )(proj, proj, proj, lbl, states, do_raw)


def _norm_rows_bwd(v, r, g, dn):
    dgv = dn * g
    return r * dgv - v * (r * r * r) * jnp.mean(v * dgv, axis=-1, keepdims=True)


def _qkv_bwd(proj, dq, dk, dvv, g_q, g_kv, w_uq_p, w_k_p, w_v_p, rc, rs1, rs2):
    s = proj.shape[0]

    def body(cq_ref, ckv_ref, dq_ref, dk_ref, dv_ref, gq_ref, gkv_ref, wq_ref, wk_ref, wv_ref, c_ref, s1_ref, s2_ref,
             dcq_ref, dckv_ref, dkpe_ref, dwq_ref, dwk_ref, dwv_ref, dgq_ref, dgkv_ref):
        @pl.when(pl.program_id(0) == 0)
        def _():
            for rf in (dwq_ref, dwk_ref, dwv_ref, dgq_ref, dgkv_ref):
                rf[...] = jnp.zeros_like(rf)

        c, s1, s2 = c_ref[...], s1_ref[...], s2_ref[...]
        cq, ckv = cq_ref[...], ckv_ref[...]
        gq, gkv = gq_ref[...], gkv_ref[...]
        cqn, rq = _norm_rows(cq, gq)
        ckvn, rkv = _norm_rows(ckv, gkv)
        cqn_b, ckvn_b = cqn.astype(BF16), ckvn.astype(BF16)
        dqf = jnp.concatenate([_rope_t(dq_ref[h], c, s1, s2) for h in range(NH)], axis=1).astype(BF16)
        dkf = jnp.concatenate([dk_ref[h] for h in range(NH)], axis=1).astype(BF16)
        dvf = jnp.concatenate([dv_ref[h] for h in range(NH)], axis=1).astype(BF16)
        dkpe = dk_ref[0]
        for h in range(1, NH):
            dkpe = dkpe + dk_ref[h]
        lane = lax.broadcasted_iota(jnp.int32, (TM, LANE), 1)
        dkpe = jnp.where((lane >= QK_NOPE) & (lane < QK_NOPE + QK_ROPE), dkpe, 0.0)
        dkpe_ref[...] = _rope_t(dkpe, c, s1, s2).astype(BF16)
        dwq_ref[...] += _mm_tn(cqn_b, dqf)
        dwk_ref[...] += _mm_tn(ckvn_b, dkf)
        dwv_ref[...] += _mm_tn(ckvn_b, dvf)
        dcqn = _mm_nt(dqf, wq_ref[...])
        dckvn = _mm_nt(dkf, wk_ref[...]) + _mm_nt(dvf, wv_ref[...])
        dgq_ref[...] += jnp.sum(dcqn * (cq * rq), axis=0, keepdims=True)
        dgkv_ref[...] += jnp.sum(dckvn * (ckv * rkv), axis=0, keepdims=True)
        dcq_ref[...] = _norm_rows_bwd(cq, rq, gq, dcqn).astype(BF16)
        dckv_ref[...] = _norm_rows_bwd(ckv, rkv, gkv, dckvn).astype(BF16)

    row = lambda w, j=0: pl.BlockSpec((TM, w), lambda i: (i, j))
    full = lambda a: pl.BlockSpec(a.shape, lambda i: (0,) * a.ndim)
    acc = lambda shape: pl.BlockSpec(shape, lambda i: (0, 0))
    heads = pl.BlockSpec((NH, TM, LANE), lambda i: (0, i, 0))
    sds = jax.ShapeDtypeStruct
    return pl.pallas_call(
        body, name="qkv_bwd", grid=(s // TM,),
        in_specs=[row(Q_LORA, P_CQ // Q_LORA), row(KV_LORA, P_CKV // KV_LORA), heads, heads, heads,
                  full(g_q), full(g_kv), full(w_uq_p), full(w_k_p), full(w_v_p), row(LANE), row(LANE), row(LANE)],
        out_specs=[row(Q_LORA), row(KV_LORA), row(LANE), acc((Q_LORA, D)), acc((KV_LORA, D)), acc((KV_LORA, D)),
                   acc((1, Q_LORA)), acc((1, KV_LORA))],
        out_shape=[sds((s, Q_LORA), BF16), sds((s, KV_LORA), BF16), sds((s, LANE), BF16), sds((Q_LORA, D), F32),
                   sds((KV_LORA, D), F32), sds((KV_LORA, D), F32), sds((1, Q_LORA), F32), sds((1, KV_LORA), F32)],
        compiler_params=_params(("arbitrary",)),
    )(proj, proj, dq, dk, dvv, g_q, g_kv, w_uq_p, w_k_p, w_v_p, rc, rs1, rs2)


def _front_bwd(x, dout, dmg, dga, dh3, dgb, dcq, dckv, dkpe, g_pre, w_in_pt, token):
    s = x.shape[0]

    def body(x_ref, do_ref, dmg_ref, dga_ref, dh3_ref, dgb_ref, dcq_ref, dckv_ref, dkpe_ref, g_ref, w_ref, token_ref, gx_ref, dg_ref):
        @pl.when(pl.program_id(0) == 0)
        def _():
            dg_ref[...] = jnp.zeros_like(dg_ref)

        xv, g = x_ref[...], g_ref[...]
        _, r = _norm_rows(xv, g)
        pieces = ((dmg_ref[...], P_MERGE), (dga_ref[...], P_GA), (dh3_ref[0], P_HQ), (dh3_ref[1], P_HF), (dh3_ref[2], P_HI),
                  (dgb_ref[...], P_GB), (dcq_ref[...], P_CQ), (dckv_ref[...], P_CKV), (dkpe_ref[...], P_KPE))
        dh = jnp.zeros((TM, D), F32)
        for piece, off in pieces:
            dh = dh + _mm(piece, w_ref[off:off + piece.shape[1], :])
        dg_ref[...] += jnp.sum(dh * (xv * r), axis=0, keepdims=True)
        gx_ref[...] = do_ref[...] + _norm_rows_bwd(xv, r, g, dh)

    row = lambda w: pl.BlockSpec((TM, w), lambda i: (i, 0))
    full = lambda a: pl.BlockSpec(a.shape, lambda i: (0,) * a.ndim)
    sds = jax.ShapeDtypeStruct
    return pl.pallas_call(
        body, name="front_bwd", grid=(s // TM,),
        in_specs=[row(D), row(D), row(2048), row(512), pl.BlockSpec((3, TM, 512), lambda i: (0, i, 0)), row(512), row(Q_LORA),
                  row(KV_LORA), row(LANE), full(g_pre), full(w_in_pt), pl.BlockSpec(memory_space=pl.ANY)],
        out_specs=[row(D), pl.BlockSpec((1, D), lambda i: (0, 0))],
        out_shape=[sds((s, D), F32), sds((1, D), F32)],
        compiler_params=_params(("arbitrary",)),
    )(x, dout, dmg, dga, dh3, dgb, dcq, dckv, dkpe, g_pre, w_in_pt, token)


TK_GRAD = 1024


def _win_grad(h, pieces, name):
    s = h.shape[0]
    n = len(pieces)

    def body(h_ref, *refs):
        @pl.when(pl.program_id(0) == 0)
        def _():
            for o_ref in refs[n:]:
                o_ref[...] = jnp.zeros_like(o_ref)

        hv = h_ref[...]
        for d_ref, o_ref in zip(refs[:n], refs[n:]):
            if len(d_ref.shape) == 3:
                for k in range(d_ref.shape[0]):
                    o_ref[k] += _mm_tn(d_ref[k], hv)
            else:
                o_ref[...] += _mm_tn(d_ref[...], hv)

    def in_spec(p):
        if p.ndim == 3:
            return pl.BlockSpec((p.shape[0], TK_GRAD, p.shape[2]), lambda kk: (0, kk, 0))
        return pl.BlockSpec((TK_GRAD, p.shape[1]), lambda kk: (kk, 0))

    out_shapes = [(p.shape[0], p.shape[2], D) if p.ndim == 3 else (p.shape[1], D) for p in pieces]
    return pl.pallas_call(
        body, name=name, grid=(s // TK_GRAD,),
        in_specs=[pl.BlockSpec((TK_GRAD, D), lambda kk: (kk, 0))] + [in_spec(p) for p in pieces],
        out_specs=[pl.BlockSpec(sh, lambda kk, nd=len(sh): (0,) * nd) for sh in out_shapes],
        out_shape=[jax.ShapeDtypeStruct(sh, F32) for sh in out_shapes],
        compiler_params=_params(("arbitrary",)),
    )(h, *pieces)


def _pad_win_t(w_in_t):
    z = lambda n: jnp.zeros((n, w_in_t.shape[1]), w_in_t.dtype)
    sl = lambda o, n: w_in_t[o:o + n]
    return jnp.concatenate([sl(O_MERGE, 2048), sl(O_GA, 512), sl(O_HQ, 512), sl(O_HF, 512), sl(O_HI, 512), sl(O_GB, 512),
                            sl(O_CQ, Q_LORA), sl(O_CKV, KV_LORA), z(64), sl(O_KPE, QK_ROPE), z(32)], axis=0)


def _pad_wuq(w_uq):
    w = w_uq.reshape(Q_LORA, NH, QK_NOPE + QK_ROPE)
    return jnp.pad(w, ((0, 0), (0, 0), (0, LANE - QK_NOPE - QK_ROPE))).reshape(Q_LORA, NH * LANE)


def _unpad_wuq(g):
    return g.reshape(Q_LORA, NH, LANE)[:, :, :QK_NOPE + QK_ROPE].reshape(Q_LORA, NH * (QK_NOPE + QK_ROPE))


def _pad_wukv(w_ukv):
    w = w_ukv.reshape(KV_LORA, NH, QK_NOPE + V_DIM)
    w_k = jnp.pad(w[:, :, :QK_NOPE], ((0, 0), (0, 0), (0, LANE - QK_NOPE))).reshape(KV_LORA, NH * LANE)
    wv = w[:, :, QK_NOPE:].reshape(KV_LORA, NH // 2, 2, 1, V_DIM)
    eye = jnp.eye(2, dtype=w.dtype).reshape(1, 1, 2, 2, 1)
    return w_k, (wv * eye).reshape(KV_LORA, NH * LANE)


def _unpad_wukv(gk, gv):
    gk = gk.reshape(KV_LORA, NH, LANE)[:, :, :QK_NOPE]
    gv = gv.reshape(KV_LORA, NH // 2, 2, 2, V_DIM)
    gv = jnp.stack([gv[:, :, 0, 0], gv[:, :, 1, 1]], axis=2).reshape(KV_LORA, NH, V_DIM)
    return jnp.concatenate([gk, gv], axis=-1).reshape(KV_LORA, NH * (QK_NOPE + V_DIM))


def _local_step(x, tgt, g_pre, w_in_t, b_gate, g_q, g_kv, lb_logits, g_hgrn, g_post, weights, exchange=None):
    s = x.shape[0]
    w_in_p = _pad_win_t(w_in_t)
    rc, rs1, rs2 = _rope_tables(s)
    g_hg = jnp.tile(g_hgrn, (1, NH))

    proj, h = _front_fwd(x, g_pre, w_in_p, weights.tokens)
    w_uq, w_ukv = weights.qkv(h)
    w_uq_p = _pad_wuq(w_uq)
    w_k_p, w_v_p = _pad_wukv(w_ukv)
    q, k, vv = _qkv_fwd(proj, g_q, g_kv, w_uq_p, w_k_p, w_v_p, rc, rs1, rs2)
    attn, lse = _attn_fwd(q, k, vv)
    o_raw, states = _hgrn_fwd(proj, lb_logits)
    wa, wb, w_out = weights.mid(o_raw)
    (loss, dout, dattn, dga, dor, dgb, dmg, d_wout, d_wa, d_wb, d_gpost, d_bgate, d_ghg) = _mid(
        proj, attn, o_raw, x, tgt, g_hg, b_gate, g_post, wa, wb, w_out)
    w_mg, w_ga, w_gb = _win_grad(h, [dmg, dga, dgb], "win_grad_mid")
    dh3, d_lbl = _hgrn_bwd(proj, lb_logits, states, dor)
    (w_h3,) = _win_grad(h, [dh3], "win_grad_hgrn")
    d_win_rest = jnp.concatenate([w_ga, w_h3[0], w_h3[1], w_h3[2], w_gb, w_mg], axis=0)
    early = dict(w_in_rest=d_win_rest, w_branch_a=d_wa, w_branch_b=d_wb, w_out=d_wout)
    token = exchange.start_early(early) if exchange else jnp.zeros((8, LANE), F32)
    dq, dk, dvv = _attn_bwd(q, k, vv, attn, dattn, lse, token)
    dcq, dckv, dkpe, d_wuq_p, d_wk_p, d_wv_p, d_gq, d_gkv = _qkv_bwd(proj, dq, dk, dvv, g_q, g_kv, w_uq_p, w_k_p, w_v_p, rc, rs1, rs2)
    w_cq, w_ckv, w_kpe = _win_grad(h, [dcq, dckv, dkpe], "win_grad_qkv")
    d_win_qkv = jnp.concatenate([w_cq, w_ckv, w_kpe[64:64 + QK_ROPE]], axis=0)
    late = dict(w_in_qkv=d_win_qkv, w_uq=_unpad_wuq(d_wuq_p), w_ukv=_unpad_wukv(d_wk_p, d_wv_p))
    token = exchange.start_late(late) if exchange else jnp.zeros((8, LANE), F32)
    grad_x, d_gpre = _front_bwd(x, dout, dmg, dga, dh3, dgb, dcq, dckv, dkpe, g_pre, w_in_p, token)
    vec_grads = dict(g_pre=d_gpre, b_gate=d_bgate, g_q=d_gq, g_kv=d_gkv, lb_logits=d_lbl, g_hgrn=d_ghg, g_post=d_gpost)
    return loss, grad_x, dict(early, **late), vec_grads


SHARD_SHAPES = (("w_in", (1416, 1024)), ("w_uq", (192, 768)), ("w_ukv", (256, 256)), ("w_branch_a", (512, 256)),
                ("w_branch_b", (512, 256)), ("w_out", (256, 1024)))
BIG = tuple(n for n, _ in SHARD_SHAPES)
ROW_SHARDED = ("w_in", "w_uq", "w_out")
GATHER_SPLIT_AXIS = dict(w_in=1, w_uq=0, w_ukv=0, w_branch_a=0, w_branch_b=0, w_out=0)
N_CHIPS = 4
QKV_ROWS = Q_LORA + KV_LORA + QK_ROPE


def _to_block(name, a):
    return a[0].T if name == "w_in" else a[0]


def _from_block(name, a):
    return a.T[None] if name == "w_in" else a[None]
VEC_ROWS = (("g_pre", 0, 1024), ("b_gate", 1, 2048), ("g_q", 2, 768), ("g_kv", 3, 256), ("g_hgrn", 6, 64), ("g_post", 7, 1024))
VEC_LB_ROW = 4
VEC_SHAPE = (8, 2048)


def _split_by_chip(name, g):
    a, b = dict(SHARD_SHAPES)[name]
    return g.reshape(N_CHIPS, a, b) if name in ROW_SHARDED else g.reshape(a, N_CHIPS, b).transpose(1, 0, 2)


def _join_chips(name, w):
    a, b = dict(SHARD_SHAPES)[name]
    return w.reshape(N_CHIPS * a, b) if name in ROW_SHARDED else w.transpose(1, 0, 2).reshape(a, N_CHIPS * b)


MESH = pl.DeviceIdType.MESH
HBM = pl.BlockSpec(memory_space=pltpu.HBM)


def _mesh_place():
    x, y, c = lax.axis_index("x"), lax.axis_index("y"), lax.axis_index("c")
    return x, y, c, 2 * x + y, [(1 - x, y), (x, 1 - y), (1 - x, 1 - y)]


def _remote(src, dst, send_sems, recv_sems, k, to):
    return pltpu.make_async_remote_copy(src_ref=src, dst_ref=dst, send_sem=send_sems.at[k], recv_sem=recv_sems.at[k],
                                        device_id=to, device_id_type=MESH)


def _gather_weights(shards, split_axes):
    n = len(shards)

    def body(*refs):
        srcs, outs = refs[:n], refs[n:2 * n]
        ici_send, ici_recv, d2d_send, d2d_recv, local_sems = refs[2 * n:]
        x, y, c, me, chips = _mesh_place()
        sibling = (x, y, 1 - c)

        def half(ref, k, which):
            size = shards[k].shape[split_axes[k]] // 2
            part = pl.ds(pl.multiple_of(which * size, size), size)
            return ref.at[part] if split_axes[k] == 0 else ref.at[:, part]

        own = [pltpu.make_async_copy(srcs[k], outs[k].at[me], local_sems.at[k]) for k in range(n)]
        for cp in own:
            cp.start()
        started = []
        for k in range(n):
            for j, (px, py) in enumerate(chips):
                cp = _remote(half(srcs[k], k, c), half(outs[k].at[me], k, c), ici_send, ici_recv, 3 * k + j, (px, py, c))
                cp.start()
                started.append(cp)
        for k in range(n):
            for j, (px, py) in enumerate(chips):
                landed = half(outs[k].at[2 * px + py], k, c)
                _remote(landed, landed, ici_send, ici_recv, 3 * k + j, (px, py, c)).wait_recv()
                cp = _remote(landed, landed, d2d_send, d2d_recv, 3 * k + j, sibling)
                cp.start()
                started.append(cp)
        for k in range(n):
            for j, (px, py) in enumerate(chips):
                other = half(outs[k].at[2 * px + py], k, 1 - c)
                _remote(other, other, d2d_send, d2d_recv, 3 * k + j, sibling).wait_recv()
        for cp in started:
            cp.wait_send()
        for cp in own:
            cp.wait()

    sems = pltpu.SemaphoreType.DMA((3 * n,))
    return pl.pallas_call(
        body, name="gather_weights", in_specs=[HBM] * n, out_specs=[HBM] * n,
        out_shape=[jax.ShapeDtypeStruct((N_CHIPS,) + s.shape, s.dtype) for s in shards],
        scratch_shapes=[sems, sems, sems, sems, pltpu.SemaphoreType.DMA((n,))],
        compiler_params=pltpu.CompilerParams(has_side_effects=True),
    )(*shards)


def _sibling_exchange(srcs, name, after=None):
    n = len(srcs)
    extra = [] if after is None else [after]

    def body(*refs):
        src_refs, outs = refs[:n], refs[n + len(extra):2 * n + len(extra)]
        send_sems, recv_sems = refs[2 * n + len(extra):]
        sibling = (lax.axis_index("x"), lax.axis_index("y"), 1 - lax.axis_index("c"))
        copies = [_remote(src_refs[k], outs[k], send_sems, recv_sems, k, sibling) for k in range(n)]
        for cp in copies:
            cp.start()
        for cp in copies:
            cp.wait()

    sems = pltpu.SemaphoreType.DMA((n,))
    return pl.pallas_call(
        body, name=name, in_specs=[HBM] * n + [pl.BlockSpec(memory_space=pl.ANY)] * len(extra), out_specs=[HBM] * n,
        out_shape=[jax.ShapeDtypeStruct(s.shape, s.dtype) for s in srcs],
        scratch_shapes=[sems, sems],
        compiler_params=pltpu.CompilerParams(has_side_effects=True),
    )(*srcs, *extra)


SEM = pl.BlockSpec(memory_space=pltpu.SEMAPHORE)
DATAFLOW = pltpu.SideEffectType.DATAFLOW_SIDE_EFFECTING
SIBLING = "sibling"


def _exchange_copies(srcs, to_first, src_refs, land_refs, send_sems, recv_sems):
    x, y, c, me, chips = _mesh_place()
    n = len(srcs)
    if to_first == SIBLING:
        copies = [(None, _remote(src_refs[k], land_refs[k], send_sems, recv_sems, k, (x, y, 1 - c))) for k in range(n)]
        return copies, copies
    sends, recvs = [], []
    for k in range(n):
        if k in to_first:
            base = 3 * n + 4 * to_first.index(k)
            sends.append((me != 0, pltpu.make_async_remote_copy(
                src_ref=src_refs[k], dst_ref=land_refs[k].at[me], send_sem=send_sems.at[base], recv_sem=recv_sems.at[base + me],
                device_id=(0, 0, c), device_id_type=MESH)))
            for s in range(1, N_CHIPS):
                recvs.append((me == 0, pltpu.make_async_remote_copy(
                    src_ref=src_refs[k], dst_ref=land_refs[k].at[s], send_sem=send_sems.at[base], recv_sem=recv_sems.at[base + s],
                    device_id=(s // 2, s % 2, c), device_id_type=MESH)))
        else:
            slab = (lambda t, k=k: src_refs[k]) if srcs[k].ndim == 2 else (lambda t, k=k: src_refs[k].at[t])
            for j, (px, py) in enumerate(chips):
                sends.append((None, _remote(slab(2 * px + py), land_refs[k].at[me], send_sems, recv_sems, 3 * k + j, (px, py, c))))
                recvs.append((None, _remote(slab(me), land_refs[k].at[2 * px + py], send_sems, recv_sems, 3 * k + j, (px, py, c))))
    return sends, recvs


def _when(pred, fn):
    if pred is None:
        fn()
    else:
        pl.when(pred)(fn)


def _exchange_start(srcs, to_first, name, after=None):
    n = len(srcs)
    if to_first == SIBLING:
        n_sems, lands = n, [lax.empty(s.shape, s.dtype) for s in srcs]
    else:
        n_sems, lands = 3 * n + 4 * len(to_first), [lax.empty((N_CHIPS,) + s.shape[-2:], s.dtype) for s in srcs]
    extra = [] if after is None else [after]

    def body(*refs):
        src_refs, land_refs = refs[:n], refs[n:2 * n]
        send_sems, recv_sems, token = refs[2 * n + len(extra)], refs[2 * n + len(extra) + 1], refs[-1]
        sends, _ = _exchange_copies(srcs, to_first, src_refs, land_refs, send_sems, recv_sems)
        for pred, cp in sends:
            _when(pred, cp.start)
        token[...] = jnp.zeros_like(token)

    hbm = lambda a: pltpu.HBM(a.shape, a.dtype)
    res = pl.pallas_call(
        body, name=name,
        out_shape=[pltpu.SemaphoreType.DMA((n_sems,)), pltpu.SemaphoreType.DMA((n_sems,))] + [hbm(a) for a in srcs + lands]
        + [jax.ShapeDtypeStruct((8, LANE), F32)],
        in_specs=[HBM] * (2 * n) + [pl.BlockSpec(memory_space=pl.ANY)] * len(extra),
        out_specs=[SEM, SEM] + [HBM] * (2 * n) + [pl.BlockSpec(memory_space=pltpu.VMEM)],
        input_output_aliases={i: 2 + i for i in range(2 * n)},
        compiler_params=pltpu.CompilerParams(has_side_effects=DATAFLOW),
    )(*[pltpu.with_memory_space_constraint(a, pltpu.HBM) for a in srcs + lands], *extra)
    return res[:-1], res[-1]


def _exchange_wait(srcs, to_first, started, after, name):
    n = len(srcs)
    send_sems, recv_sems, thru = started[0], started[1], started[2:]

    def body(*refs):
        src_refs, land_refs, send_ref, recv_ref = refs[:n], refs[n:2 * n], refs[2 * n], refs[2 * n + 1]
        sends, recvs = _exchange_copies(srcs, to_first, src_refs, land_refs, send_ref, recv_ref)
        for pred, cp in sends:
            _when(pred, cp.wait_send)
        for pred, cp in recvs:
            _when(pred, cp.wait_recv)

    res = pl.pallas_call(
        body, name=name, out_shape=[pltpu.HBM(a.shape, a.dtype) for a in thru],
        in_specs=[HBM] * (2 * n) + [SEM, SEM, pl.BlockSpec(memory_space=pl.ANY)], out_specs=[HBM] * (2 * n),
        input_output_aliases={i: i for i in range(2 * n)},
        compiler_params=pltpu.CompilerParams(has_side_effects=DATAFLOW),
    )(*thru, send_sems, recv_sems, after)
    return res[n:]


ROW_TILE = 256
COL_TILE = 256


def _block_tiling(a, b):
    if a <= ROW_TILE or a % ROW_TILE == 0:
        ta = min(a, ROW_TILE)
        return a // ta, (ta, b), lambda i: (i, 0)
    return b // COL_TILE, (a, COL_TILE), lambda i: (0, i)


def _sum_landed(land, own, name, first_land=None, first_own=None):
    _, a, b = land.shape
    steps, tile, at = _block_tiling(a, b)
    extra = first_land is not None

    def body(*refs):
        p_ref, own_ref, o_ref = refs[0], refs[1], refs[-1]
        me = 2 * lax.axis_index("x") + lax.axis_index("y")
        own = own_ref[...].astype(F32)
        slot = lambda t: jnp.where(me == t, own, p_ref[t].astype(F32))
        o_ref[...] = ((slot(0) + slot(1)) + slot(2)) + slot(3)
        if extra:
            fp_ref, fo_ref = refs[2], refs[3]
            r = fo_ref.shape[0]

            @pl.when(me == 0)
            def _():
                f = lambda t: fp_ref[t].astype(F32)
                rows = pl.ds(pl.multiple_of(lax.axis_index("c") * r, 8), r)
                o_ref[rows, :] += ((fo_ref[...].astype(F32) + f(1)) + f(2)) + f(3)

    in_specs = [pl.BlockSpec((N_CHIPS,) + tile, lambda i: (0,) + at(i)), pl.BlockSpec(tile, at)]
    args = [land, own]
    if extra:
        r = first_own.shape[0]
        assert tile[0] == a, "the extra rows need whole columns in a step"
        in_specs += [pl.BlockSpec((N_CHIPS, r, tile[1]), lambda i: (0,) + at(i)), pl.BlockSpec((r, tile[1]), at)]
        args += [first_land, first_own]
    return pl.pallas_call(
        body, name=name, grid=(steps,), in_specs=in_specs, out_specs=pl.BlockSpec(tile, at),
        out_shape=jax.ShapeDtypeStruct((a, b), F32), compiler_params=_params(("parallel",)),
    )(*args)


def _add_cast(a, b, name):
    def body(a_ref, b_ref, o_ref):
        o_ref[...] = (a_ref[...] + b_ref[...]).astype(BF16)

    return pl.pallas_call(body, name=name, out_shape=jax.ShapeDtypeStruct(a.shape, BF16),
                          compiler_params=_params(()))(a, b)


class _LaterWeights:
    QKV = ("w_uq", "w_ukv")
    MID = ("w_branch_a", "w_branch_b", "w_out")

    def __init__(self, blocks, after):
        self.blocks = blocks
        self.qkv_started, t1 = _exchange_start([blocks[n] for n in self.QKV], (), "weights_qkv_start", after)
        self.mid_started, t2 = _exchange_start([blocks[n] for n in self.MID], (), "weights_mid_start", after)
        self.tokens = [t1, t2]

    def _whole(self, names, started, after, name):
        landed = _exchange_wait([self.blocks[n] for n in names], (), started, after, name)
        me = 2 * lax.axis_index("x") + lax.axis_index("y")
        return [_join_chips(n, lax.dynamic_update_index_in_dim(land, self.blocks[n], me, 0)) for n, land in zip(names, landed)]

    def qkv(self, after):
        return self._whole(self.QKV, self.qkv_started, after, "weights_qkv_wait")

    def mid(self, after):
        return self._whole(self.MID, self.mid_started, after, "weights_mid_wait")


class _GradExchange:
    EARLY = ("w_in", "w_branch_a", "w_branch_b", "w_out")
    LATE = ("w_uq", "w_ukv")

    def __init__(self, state):
        self.state = state
        self.outs = {}

    @staticmethod
    def _own(slabs):
        return lax.dynamic_index_in_dim(slabs, 2 * lax.axis_index("x") + lax.axis_index("y"), axis=0, keepdims=False)

    def start_early(self, g):
        full = jnp.concatenate([jnp.zeros((QKV_ROWS, D), F32), g["w_in_rest"]], axis=0)
        g = dict(g, w_in=full)
        self.early = [_split_by_chip(n, g[n]).astype(BF16) for n in self.EARLY]
        self.early_started, token = _exchange_start(self.early, (), "grads_early_start")
        return token

    def start_late(self, g):
        self.early_landed = _exchange_wait(self.early, (), self.early_started, g["w_uq"], "grads_early_wait")
        half = QKV_ROWS // 2
        c = lax.axis_index("c")
        mine = lax.dynamic_slice_in_dim(g["w_in_qkv"], c * half, half, axis=0)
        (theirs,) = _sibling_exchange([lax.dynamic_slice_in_dim(g["w_in_qkv"], (1 - c) * half, half, axis=0)], "sibling_qkv_rows")
        self.late = [_split_by_chip(n, g[n]).astype(BF16) for n in self.LATE] + [_add_cast(mine, theirs, "add_qkv_rows")]
        self.late_started, token = _exchange_start(self.late, (2,), "grads_late_start")
        self.early_mine = [_sum_landed(land, self._own(slabs), "sum_" + n)
                           for n, slabs, land in list(zip(self.EARLY, self.early, self.early_landed))[1:]]
        self.swap_started, token = _exchange_start(self.early_mine, SIBLING, "sibling_early_start", after=token)
        return token

    def finish(self, after):
        theirs = _exchange_wait(self.early_mine, SIBLING, self.swap_started, after, "sibling_early_wait")
        for n, a, b in zip(self.EARLY[1:], self.early_mine, theirs):
            self.outs[n] = _adamw(a, b, *self.state[n], "adamw_" + n)
        late_landed = _exchange_wait(self.late, (2,), self.late_started, after, "grads_late_wait")
        sums = {"w_in": _sum_landed(self.early_landed[0], self._own(self.early[0]), "sum_w_in",
                                    first_land=late_landed[2], first_own=self.late[2])}
        for n, slabs, land in zip(self.LATE, self.late, late_landed):
            sums[n] = _sum_landed(land, self._own(slabs), "sum_" + n)
        return sums


def _adamw_math(g, w, m, v):
    nm = ADAM_B1 * m + (1.0 - ADAM_B1) * g
    nv = ADAM_B2 * v + (1.0 - ADAM_B2) * (g * g)
    m_hat = nm / (1.0 - ADAM_B1 ** ADAM_STEP)
    v_hat = nv / (1.0 - ADAM_B2 ** ADAM_STEP)
    return -ADAM_LR * (m_hat / (jnp.sqrt(v_hat) + ADAM_EPS) + ADAM_WD * w), nm, nv


def _adamw(p_mine, p_sibling, w, m, v, name):
    a, b = p_mine.shape
    steps, tile, at = _block_tiling(a, b)

    def body(a_ref, b_ref, w_ref, m_ref, v_ref, g_ref, d_ref, nm_ref, nv_ref):
        g = a_ref[...] + b_ref[...]
        g_ref[...] = g
        d_ref[...], nm_ref[...], nv_ref[...] = _adamw_math(g, w_ref[...], m_ref[...], v_ref[...])

    spec = pl.BlockSpec(tile, at)
    sds = jax.ShapeDtypeStruct((a, b), F32)
    return pl.pallas_call(
        body, name=name, grid=(steps,), in_specs=[spec] * 5, out_specs=[spec] * 4, out_shape=[sds] * 4,
        compiler_params=_params(("parallel",)),
    )(p_mine, p_sibling, w, m, v)


LOSS_AT = (2, 1024)


def _vec_pack(vg, loss):
    names = [n for n, _, _ in VEC_ROWS]

    def body(*refs):
        o_ref = refs[-1]
        lb_ref, loss_ref = refs[len(names)], refs[len(names) + 1]
        o_ref[...] = jnp.zeros_like(o_ref)
        o_ref[LOSS_AT[0]:LOSS_AT[0] + 1, LOSS_AT[1]:LOSS_AT[1] + LANE] = jnp.broadcast_to(loss_ref[...], (1, LANE))
        for (name, row, size), ref in zip(VEC_ROWS, refs):
            if name == "g_hgrn":
                r = lax.broadcasted_iota(jnp.int32, (NH * V_DIM, LANE), 0)
                c = lax.broadcasted_iota(jnp.int32, (NH * V_DIM, LANE), 1)
                fold = ((r % V_DIM) == c).astype(F32)
                o_ref[row:row + 1, 0:LANE] = jnp.dot(ref[...], fold, precision=HIGHEST, preferred_element_type=F32)
            else:
                o_ref[row:row + 1, 0:size] = ref[...]
        o_ref[VEC_LB_ROW:VEC_LB_ROW + 2, 0:512] = lb_ref[...]

    return pl.pallas_call(body, name="vec_pack", out_shape=jax.ShapeDtypeStruct(VEC_SHAPE, F32))(
        *[vg[n] for n in names], vg["lb_logits"], loss)


def _adamw_vec(p_mine, p_sibling, w, m, v):
    names = [n for n, _, _ in VEC_ROWS] + ["lb_logits"]
    k = len(names)

    def body(a_ref, b_ref, *refs):
        ins, outs = refs[:3 * k], refs[3 * k:]
        at = (slice(LOSS_AT[0], LOSS_AT[0] + 1), slice(LOSS_AT[1], LOSS_AT[1] + LANE))
        outs[-1][...] = a_ref[at] + b_ref[at]
        for i, name in enumerate(names):
            if name == "lb_logits":
                rows, cols = slice(VEC_LB_ROW, VEC_LB_ROW + 2), slice(0, 512)
            else:
                _, row, size = VEC_ROWS[i]
                rows, cols = slice(row, row + 1), slice(0, size)
            g = a_ref[rows, cols] + b_ref[rows, cols]
            d, nm, nv = _adamw_math(g, ins[i][...], ins[k + i][...], ins[2 * k + i][...])
            for o_ref, val in zip(outs[4 * i:4 * i + 4], (g, d, nm, nv)):
                o_ref[...] = val

    shapes = [jax.ShapeDtypeStruct(w[n].shape, F32) for n in names for _ in range(4)] + [jax.ShapeDtypeStruct((1, LANE), F32)]
    res = pl.pallas_call(body, name="adamw_vec", out_shape=shapes)(
        p_mine, p_sibling, *[w[n] for n in names], *[m[n] for n in names], *[v[n] for n in names])
    return [{n: res[4 * i + j] for i, n in enumerate(names)} for j in range(4)], res[-1]


WEIGHTS = ("g_pre", "w_in", "b_gate", "g_q", "w_uq", "g_kv", "w_ukv", "lb_logits", "g_hgrn", "w_branch_a", "w_branch_b", "w_out", "g_post")


def kernel(x, g_pre, w_in, b_gate, g_q, w_uq, g_kv, w_ukv, lb_logits, g_hgrn, w_branch_a, w_branch_b, w_out, g_post, loss_target, m_g_pre, m_w_in, m_b_gate, m_g_q, m_w_uq, m_g_kv, m_w_ukv, m_lb_logits, m_g_hgrn, m_w_branch_a, m_w_branch_b, m_w_out, m_g_post, v_g_pre, v_w_in, v_b_gate, v_g_q, v_w_uq, v_g_kv, v_w_ukv, v_lb_logits, v_g_hgrn, v_w_branch_a, v_w_branch_b, v_w_out, v_g_post):
    w = dict(g_pre=g_pre, w_in=w_in, b_gate=b_gate, g_q=g_q, w_uq=w_uq, g_kv=g_kv, w_ukv=w_ukv, lb_logits=lb_logits, g_hgrn=g_hgrn,
             w_branch_a=w_branch_a, w_branch_b=w_branch_b, w_out=w_out, g_post=g_post)
    m = dict(g_pre=m_g_pre, w_in=m_w_in, b_gate=m_b_gate, g_q=m_g_q, w_uq=m_w_uq, g_kv=m_g_kv, w_ukv=m_w_ukv, lb_logits=m_lb_logits,
             g_hgrn=m_g_hgrn, w_branch_a=m_w_branch_a, w_branch_b=m_w_branch_b, w_out=m_w_out, g_post=m_g_post)
    v = dict(g_pre=v_g_pre, w_in=v_w_in, b_gate=v_b_gate, g_q=v_g_q, w_uq=v_w_uq, g_kv=v_g_kv, w_ukv=v_w_ukv, lb_logits=v_lb_logits,
             g_hgrn=v_g_hgrn, w_branch_a=v_w_branch_a, w_branch_b=v_w_branch_b, w_out=v_w_out, g_post=v_g_post)
    blocks = {n: _to_block(n, w[n]).astype(BF16) for n in BIG}
    (w_in_all,) = _gather_weights([blocks["w_in"]], [GATHER_SPLIT_AXIS["w_in"]])
    weights = _LaterWeights(blocks, w_in_all)
    state = {n: [_to_block(n, t[n]) for t in (w, m, v)] for n in BIG}
    exchange = _GradExchange(state)
    loss, grad_x, _, vec_grads = _local_step(
        x[0], loss_target[0], g_pre, _join_chips("w_in", w_in_all), b_gate, g_q, g_kv, lb_logits, g_hgrn, g_post, weights, exchange)
    vec = _vec_pack(vec_grads, loss)
    vec_started, token = _exchange_start([vec], (), "vec_start")
    sums = exchange.finish(token)
    rest = tuple(sums)
    (vec_landed,) = _exchange_wait([vec], (), vec_started, sums[rest[-1]], "vec_wait")
    mine = [sums[n] for n in rest] + [_sum_landed(vec_landed, vec, "sum_vec")]
    theirs = _sibling_exchange(mine, "sibling_grads")
    done = dict(exchange.outs)
    for k, n in enumerate(rest):
        done[n] = _adamw(mine[k], theirs[k], *state[n], "adamw_" + n)
    outs = [{}, {}, {}, {}]
    for n in BIG:
        for o, val in zip(outs, done[n]):
            o[n] = _from_block(n, val)
    vec_outs, total = _adamw_vec(mine[-1], theirs[-1], w, m, v)
    for o, vals in zip(outs, vec_outs):
        o.update(vals)
    return (total[0, 0], grad_x[None], *[o[n] for o in outs for n in WEIGHTS])
```

```python
import functools
import math

import numpy as np
import jax
import jax.numpy as jnp
from jax import lax
from jax.experimental import pallas as pl
from jax.experimental.pallas import tpu as pltpu

F32 = jnp.float32
BF16 = jnp.bfloat16
HIGHEST = lax.Precision.HIGHEST

D = 1024
NH = 8
QK_NOPE, QK_ROPE, V_DIM = 64, 32, 64
Q_LORA, KV_LORA = 768, 256
CHUNK = 64
HG_BLOCK = 32
EPS = 1e-6
D_IN = 5664
LANE = 128
P_MERGE, P_GA, P_HQ, P_HF, P_HI, P_GB, P_CQ, P_CKV, P_KPE = 0, 2048, 2560, 3072, 3584, 4096, 4608, 5376, 5632
D_P = 5760
O_CQ, O_CKV, O_KPE, O_GA, O_HQ, O_HF, O_HI, O_GB, O_MERGE = 0, 768, 1024, 1056, 1568, 2080, 2592, 3104, 3616

TM = 512
TM_MID = 256
TQ = 1024
ONES_LANE = (LANE - 1, 0)
TH = 256
HG_PAIRS = 4
VMEM_LIMIT = 56 * 1024 * 1024

ADAM_LR, ADAM_B1, ADAM_B2, ADAM_EPS, ADAM_WD, ADAM_STEP = 0.001, 0.9, 0.999, 1e-08, 0.01, 10

NT_DIMS = (((1,), (1,)), ((), ()))
TN_DIMS = (((0,), (0,)), ((), ()))


def _params(sem):
    return pltpu.CompilerParams(dimension_semantics=sem, vmem_limit_bytes=VMEM_LIMIT)


def _mm(a, b):
    return jnp.dot(a, b, preferred_element_type=F32)


def _mm_nt(a, b):
    return lax.dot_general(a, b, NT_DIMS, preferred_element_type=F32)


def _mm_tn(a, b):
    return lax.dot_general(a, b, TN_DIMS, preferred_element_type=F32)


def _sigmoid(z):
    return jax.nn.sigmoid(z)


def _rope(v, c, s1, s2):
    return v * c + pltpu.roll(v, 112, 1) * s1 + pltpu.roll(v, 16, 1) * s2


def _rope_t(dy, c, s1, s2):
    return dy * c + pltpu.roll(dy * s1, 16, 1) + pltpu.roll(dy * s2, 112, 1)


def _rope_tables(s):
    inv = 10000.0 ** (-jnp.arange(0, QK_ROPE, 2, dtype=F32) / QK_ROPE)
    ang = jnp.arange(s, dtype=F32)[:, None] * inv[None, :]
    cos, sin = jnp.cos(ang), jnp.sin(ang)
    z64, z32, o64, o32 = jnp.zeros((s, 64), F32), jnp.zeros((s, 32), F32), jnp.ones((s, 64), F32), jnp.ones((s, 32), F32)
    z16 = jnp.zeros((s, 16), F32)
    c = jnp.concatenate([o64, cos, cos, o32], axis=1)
    s1 = jnp.concatenate([z64, -sin, z16, z32], axis=1)
    s2 = jnp.concatenate([z64, z16, sin, z32], axis=1)
    return c, s1, s2


def _front_fwd(x, g_pre, w_in_pt, tokens=()):
    s = x.shape[0]
    tokens = list(tokens)

    def body(x_ref, g_ref, w_ref, *refs):
        o_ref, h_ref = refs[len(tokens):]
        xv = x_ref[...]
        r = lax.rsqrt(jnp.mean(xv * xv, axis=-1, keepdims=True) + EPS)
        h = ((xv * r) * g_ref[...]).astype(BF16)
        h_ref[...] = h
        o_ref[...] = _mm_nt(h, w_ref[...])

    return pl.pallas_call(
        body, name="front_fwd", grid=(s // TM,),
        in_specs=[pl.BlockSpec((TM, D), lambda i: (i, 0)), pl.BlockSpec((1, D), lambda i: (0, 0)),
                  pl.BlockSpec((D_P, D), lambda i: (0, 0))] + [pl.BlockSpec((8, LANE), lambda i: (0, 0))] * len(tokens),
        out_specs=[pl.BlockSpec((TM, D_P), lambda i: (i, 0)), pl.BlockSpec((TM, D), lambda i: (i, 0))],
        out_shape=[jax.ShapeDtypeStruct((s, D_P), F32), jax.ShapeDtypeStruct((s, D), BF16)],
        compiler_params=_params(("parallel",)),
    )(x, g_pre, w_in_pt, *tokens)


def _norm_rows(v, g):
    r = lax.rsqrt(jnp.mean(v * v, axis=-1, keepdims=True) + EPS)
    return (v * r) * g, r


def _qkv_fwd(proj, g_q, g_kv, w_uq_p, w_k_p, w_v_p, rc, rs1, rs2):
    s = proj.shape[0]

    def body(cq_ref, ckv_ref, kpe_ref, gq_ref, gkv_ref, wq_ref, wk_ref, wv_ref, c_ref, s1_ref, s2_ref, q_ref, k_ref, v_ref):
        c, s1, s2 = c_ref[...], s1_ref[...], s2_ref[...]
        cqn, _ = _norm_rows(cq_ref[...], gq_ref[...])
        ckvn, _ = _norm_rows(ckv_ref[...], gkv_ref[...])
        ckvn = ckvn.astype(BF16)
        qf = _mm(cqn.astype(BF16), wq_ref[...])
        kf = _mm(ckvn, wk_ref[...])
        vf = _mm(ckvn, wv_ref[...])
        kpe = _rope(kpe_ref[...], c, s1, s2)
        lane = lax.broadcasted_iota(jnp.int32, (TM, LANE), 1)
        for h in range(NH):
            blk = slice(h * LANE, (h + 1) * LANE)
            q_ref[h] = _rope(qf[:, blk], c, s1, s2).astype(BF16)
            k_ref[h] = (kf[:, blk] + kpe).astype(BF16)
            v_ref[h] = jnp.where(lane == ONES_LANE[h % 2], 1.0, vf[:, blk]).astype(BF16)

    row = lambda w, j: pl.BlockSpec((TM, w), lambda i: (i, j))
    full = lambda a: pl.BlockSpec(a.shape, lambda i: (0,) * a.ndim)
    hs = jax.ShapeDtypeStruct((NH, s, LANE), BF16)
    return pl.pallas_call(
        body, name="qkv_fwd", grid=(s // TM,),
        in_specs=[row(Q_LORA, P_CQ // Q_LORA), row(KV_LORA, P_CKV // KV_LORA), row(LANE, P_KPE // LANE),
                  full(g_q), full(g_kv), full(w_uq_p), full(w_k_p), full(w_v_p), row(LANE, 0), row(LANE, 0), row(LANE, 0)],
        out_specs=[pl.BlockSpec((NH, TM, LANE), lambda i: (0, i, 0))] * 3,
        out_shape=[hs, hs, hs],
        compiler_params=_params(("parallel",)),
    )(proj, proj, proj, g_q, g_kv, w_uq_p, w_k_p, w_v_p, rc, rs1, rs2)


LOG2E = 1.4426950408889634
QK_SCALE2 = LOG2E / math.sqrt(QK_NOPE + QK_ROPE)


HQ = TQ // 2


def _diag_visible(n):
    row = lax.broadcasted_iota(jnp.int32, (n, n), 0)
    col = lax.broadcasted_iota(jnp.int32, (n, n), 1)
    return (col // CHUNK) <= (row // CHUNK)


def _attn_fwd(q, k, vv):
    s = q.shape[1]

    def body(q_ref, k_ref, v_ref, o_ref, lse_ref):
        i = pl.program_id(1)
        qs = (q_ref[0], q_ref[1])

        def tile(hh, t, carry, diag):
            m, acc = carry
            rows = pl.ds(pl.multiple_of(t * TQ, TQ), TQ)
            sc = _mm_nt(qs[hh], k_ref[hh, rows, :])
            if diag:
                sc = jnp.where(_diag_visible(TQ), sc, -jnp.inf)
            m_new = jnp.maximum(m, jnp.max(sc, axis=-1, keepdims=True))
            alpha = jnp.exp2((m - m_new) * QK_SCALE2)
            p = jnp.exp2((sc - m_new) * QK_SCALE2).astype(BF16)
            acc = alpha * acc + _mm(p, v_ref[hh, rows, :])
            return m_new, acc

        def step(t, carry):
            return tile(0, t, carry[0], False), tile(1, t, carry[1], False)

        init = (jnp.full((TQ, 1), -jnp.inf, F32), jnp.zeros((TQ, LANE), F32))
        carry = lax.fori_loop(0, i, step, (init, init))
        lane = lax.broadcasted_iota(jnp.int32, (TQ, LANE), 1)
        out = jnp.zeros((TQ, LANE), F32)
        for hh in range(2):
            m, acc = tile(hh, i, carry[hh], True)
            l = jnp.sum(jnp.where(lane == ONES_LANE[hh], acc, 0.0), axis=-1, keepdims=True)
            out = out + jnp.where((lane < V_DIM) == (hh == 0), acc, 0.0) / l
            lse_ref[hh] = jnp.broadcast_to(m * QK_SCALE2 + jnp.log(l) * LOG2E, (TQ, LANE))
        o_ref[...] = out

    return pl.pallas_call(
        body, name="attn_fwd", grid=(NH // 2, s // TQ),
        in_specs=[pl.BlockSpec((2, TQ, LANE), lambda p, i: (p, i, 0)), pl.BlockSpec((2, s, LANE), lambda p, i: (p, 0, 0)),
                  pl.BlockSpec((2, s, LANE), lambda p, i: (p, 0, 0))],
        out_specs=[pl.BlockSpec((TQ, LANE), lambda p, i: (i, p)), pl.BlockSpec((2, TQ, LANE), lambda p, i: (p, i, 0))],
        out_shape=[jax.ShapeDtypeStruct((s, NH * V_DIM), F32), jax.ShapeDtypeStruct((NH, s, LANE), F32)],
        compiler_params=_params(("parallel", "parallel")),
    )(q, k, vv)


def _lower_bound(lbl):
    a0, a1 = lbl[0:1, :], lbl[1:2, :]
    mx = jnp.maximum(a0, a1)
    e0, e1 = jnp.exp(a0 - mx), jnp.exp(a1 - mx)
    return e0 / (e0 + e1)


def _chunk_cumsum(v, reverse=False):
    pos = lax.broadcasted_iota(jnp.int32, v.shape, 0) % HG_BLOCK
    s = 1
    while s < HG_BLOCK:
        if reverse:
            v = v + jnp.where(pos < HG_BLOCK - s, pltpu.roll(v, TH - s, 0), 0.0)
        else:
            v = v + jnp.where(pos >= s, pltpu.roll(v, s, 0), 0.0)
        s *= 2
    return v


def _hgrn_gates(hq, hf, lb):
    sig = _sigmoid(hf)
    f = lb + (1.0 - lb) * sig
    g = jnp.log(f)
    kk = 1.0 - f
    r = lax.broadcasted_iota(jnp.int32, (TH, TH), 0)
    c = lax.broadcasted_iota(jnp.int32, (TH, TH), 1)
    tri = ((r // HG_BLOCK) == (c // HG_BLOCK)) & (r >= c)
    cum = _chunk_cumsum(g)
    nch = TH // HG_BLOCK
    total = _chunks(cum)[:, HG_BLOCK - 1:HG_BLOCK, :]
    lastb = jnp.broadcast_to(total, (nch, HG_BLOCK, LANE)).reshape(TH, LANE)
    e, ei, ee = jnp.exp(cum), jnp.exp(-cum), jnp.exp(lastb - cum)
    return dict(sig=sig, f=f, kk=kk, tri=tri, cum=cum, total=total, e=e, ei=ei, ee=ee, qd=hq * e, ki=kk * ei, ke=kk * ee)


def _chunks(v):
    return v.reshape(TH // HG_BLOCK, HG_BLOCK, v.shape[-1])


def _bmm_nt(a, b):
    return lax.dot_general(a, b, (((2,), (2,)), ((0,), (0,))), preferred_element_type=F32)


def _bmm_nn(a, b):
    return lax.dot_general(a, b, (((2,), (1,)), ((0,), (0,))), preferred_element_type=F32)


def _bmm_tn(a, b):
    return lax.dot_general(a, b, (((1,), (1,)), ((0,), (0,))), preferred_element_type=F32)


def _pair_masks():
    lane = lax.broadcasted_iota(jnp.int32, (TH, LANE), 1)
    kr = lax.broadcasted_iota(jnp.int32, (LANE, LANE), 0)
    kc = lax.broadcasted_iota(jnp.int32, (LANE, LANE), 1)
    return lane < 64, (kr // 64) == (kc // 64)


def _hgrn_fwd(proj, lbl):
    s = proj.shape[0]
    nch = TH // HG_BLOCK

    def body(hq_ref, hf_ref, hi_ref, lbl_ref, o_ref, st_ref, st):
        @pl.when(pl.program_id(1) == 0)
        def _():
            st[...] = jnp.zeros_like(st)

        m0, bd = _pair_masks()
        for u in range(HG_PAIRS):
            lanes = slice(u * LANE, (u + 1) * LANE)
            lb = _lower_bound(lbl_ref[:, lanes])
            gt = _hgrn_gates(hq_ref[:, lanes], hf_ref[:, lanes], lb)
            v_b = hi_ref[:, lanes].astype(BF16)
            qd, ki_b, ke_b = gt["qd"], gt["ki"].astype(BF16), gt["ke"].astype(BF16)
            qd_b = qd.astype(BF16)
            o = jnp.zeros((TH, LANE), F32)
            for hh in range(2):
                mh = m0 if hh == 0 else jnp.logical_not(m0)
                a = jnp.where(gt["tri"], _mm_nt(jnp.where(mh, qd, 0.0).astype(BF16), ki_b), 0.0)
                o = jnp.where(mh, _mm(a.astype(BF16), v_b), o)
            upd = _bmm_tn(_chunks(v_b), _chunks(ke_b))
            decay = jnp.exp(gt["total"])
            cur, entering = st[u], []
            for n in range(nch):
                entering.append(cur)
                cur = decay[n] * cur + jnp.where(bd, upd[n], 0.0)
            st[u] = cur
            entering = jnp.stack(entering)
            st_ref[u] = entering
            o_ref[:, lanes] = o + _bmm_nt(_chunks(qd_b), entering.astype(BF16)).reshape(TH, LANE)

    wide = HG_PAIRS * LANE
    col = lambda base: pl.BlockSpec((TH, wide), lambda p, i: (i, base // wide + p))
    return pl.pallas_call(
        body, name="hgrn_fwd", grid=(NH // 2 // HG_PAIRS, s // TH),
        in_specs=[col(P_HQ), col(P_HF), col(P_HI), pl.BlockSpec((2, wide), lambda p, i: (0, p))],
        out_specs=[pl.BlockSpec((TH, wide), lambda p, i: (i, p)),
                   pl.BlockSpec((HG_PAIRS, nch, LANE, LANE), lambda p, i: (p, i, 0, 0))],
        out_shape=[jax.ShapeDtypeStruct((s, 512), F32), jax.ShapeDtypeStruct((NH // 2, s // HG_BLOCK, LANE, LANE), F32)],
        scratch_shapes=[pltpu.VMEM((HG_PAIRS, LANE, LANE), F32)],
        compiler_params=_params(("parallel", "arbitrary")),
    )(proj, proj, proj, lbl)


def _group_sum(v):
    low = lax.broadcasted_iota(jnp.int32, (v.shape[0], LANE), 1) < V_DIM
    blocks = []
    for b in range(v.shape[1] // LANE):
        blk = v[:, b * LANE:(b + 1) * LANE]
        s_low = jnp.sum(jnp.where(low, blk, 0.0), axis=-1, keepdims=True)
        s_high = jnp.sum(jnp.where(low, 0.0, blk), axis=-1, keepdims=True)
        blocks.append(jnp.where(low, s_low, s_high))
    return jnp.concatenate(blocks, axis=1)


def _dsilu(z, sg):
    return sg * (1.0 + z * (1.0 - sg))


def _mid(proj, attn, o_raw, x, tgt, g_hg, b_gate, g_post, wa, wb, w_out):
    s = x.shape[0]

    def body(attn_ref, ga_ref, o_ref, gb_ref, mg_ref, x_ref, t_ref, ghg_ref, bg_ref, gp_ref, wa_ref, wb_ref, wo_ref,
             loss_ref, dout_ref, dattn_ref, dga_ref, dor_ref, dgb_ref, dmg_ref, dwo_ref, dwa_ref, dwb_ref, dgp_ref, dbg_ref, dghg_ref):
        @pl.when(pl.program_id(0) == 0)
        def _():
            for rf in (loss_ref, dwo_ref, dwa_ref, dwb_ref, dgp_ref, dbg_ref, dghg_ref):
                rf[...] = jnp.zeros_like(rf)

        attn, za, orw, zb = attn_ref[...], ga_ref[...], o_ref[...], gb_ref[...]
        ghg, gp = ghg_ref[...], gp_ref[...]
        sga, sgb = _sigmoid(za), _sigmoid(zb)
        sa, sb = za * sga, zb * sgb
        ga = attn * sa
        rh = lax.rsqrt(_group_sum(orw * orw) * (1.0 / V_DIM) + EPS)
        on = (orw * rh) * ghg
        gb = on * sb
        ga_b, gb_b = ga.astype(BF16), gb.astype(BF16)
        ya = _mm(ga_b, wa_ref[...])
        yb = _mm(gb_b, wb_ref[...])
        gates = _sigmoid(mg_ref[...] + bg_ref[...])
        g0, g1 = gates[:, :D], gates[:, D:]
        m_b = (g0 * ya + g1 * yb).astype(BF16)
        y = _mm(m_b, wo_ref[...])
        ry = lax.rsqrt(jnp.mean(y * y, axis=-1, keepdims=True) + EPS)
        out = x_ref[...] + (y * ry) * gp
        err = out - t_ref[...]
        loss_ref[...] += 0.5 * jnp.sum(jnp.mean(err * err, axis=-1, keepdims=True), axis=0, keepdims=True)
        dout = err * (1.0 / D)
        dout_ref[...] = dout
        dgp_ref[...] += jnp.sum(dout * (y * ry), axis=0, keepdims=True)
        dgy = dout * gp
        dy = ry * dgy - y * (ry * ry * ry) * jnp.mean(y * dgy, axis=-1, keepdims=True)
        dy_b = dy.astype(BF16)
        dwo_ref[...] += _mm_tn(m_b, dy_b)
        dm = _mm_nt(dy_b, wo_ref[...])
        dya, dyb = dm * g0, dm * g1
        dg0, dg1 = dm * ya, dm * yb
        dmg = jnp.concatenate([dg0 * g0 * (1.0 - g0), dg1 * g1 * (1.0 - g1)], axis=1)
        dmg_ref[...] = dmg.astype(BF16)
        dbg_ref[...] += jnp.sum(dmg, axis=0, keepdims=True)
        dya_b, dyb_b = dya.astype(BF16), dyb.astype(BF16)
        dwa_ref[...] += _mm_tn(ga_b, dya_b)
        dwb_ref[...] += _mm_tn(gb_b, dyb_b)
        dga = _mm_nt(dya_b, wa_ref[...])
        dgb = _mm_nt(dyb_b, wb_ref[...])
        dattn_ref[...] = dga * sa
        dga_ref[...] = (dga * attn * _dsilu(za, sga)).astype(BF16)
        dgb_ref[...] = (dgb * on * _dsilu(zb, sgb)).astype(BF16)
        don = dgb * sb
        dghg_ref[...] += jnp.sum(don * (orw * rh), axis=0, keepdims=True)
        dgo = don * ghg
        dor_ref[...] = rh * dgo - orw * (rh * rh * rh) * (_group_sum(orw * dgo) * (1.0 / V_DIM))

    row = lambda w, j=0: pl.BlockSpec((TM_MID, w), lambda i: (i, j))
    full = lambda a: pl.BlockSpec(a.shape, lambda i: (0,) * a.ndim)
    acc = lambda shape: pl.BlockSpec(shape, lambda i: (0, 0))
    sds = jax.ShapeDtypeStruct
    return pl.pallas_call(
        body, name="mid", grid=(s // TM_MID,),
        in_specs=[row(512), row(512, P_GA // 512), row(512), row(512, P_GB // 512), row(2048, P_MERGE // 2048), row(D), row(D),
                  full(g_hg), full(b_gate), full(g_post), full(wa), full(wb), full(w_out)],
        out_specs=[acc((1, 1)), row(D), row(512), row(512), row(512), row(512), row(2048),
                   acc((D, D)), acc((512, D)), acc((512, D)), acc((1, D)), acc((1, 2048)), acc((1, 512))],
        out_shape=[sds((1, 1), F32), sds((s, D), F32), sds((s, 512), F32), sds((s, 512), BF16), sds((s, 512), F32), sds((s, 512), BF16),
                   sds((s, 2048), BF16), sds((D, D), F32), sds((512, D), F32), sds((512, D), F32), sds((1, D), F32),
                   sds((1, 2048), F32), sds((1, 512), F32)],
        compiler_params=_params(("arbitrary",)),
    )(attn, proj, o_raw, proj, proj, x, tgt, g_hg, b_gate, g_post, wa, wb, w_out)


def _attn_bwd(q, k, vv, attn, dattn, lse, token):
    s = q.shape[1]
    nt = s // TQ
    scale = 1.0 / math.sqrt(QK_NOPE + QK_ROPE)

    def body(q_ref, k_ref, v_ref, o_ref, do_ref, lse_ref, token_ref, dq_ref, dk_ref, dv_ref, do_s, delta_s):
        j = pl.program_id(1)

        @pl.when(j == 0)
        def _():
            dq_ref[...] = jnp.zeros_like(dq_ref)
            lane = lax.broadcasted_iota(jnp.int32, (TQ, LANE), 1)

            @pl.loop(0, nt)
            def _(i):
                rows = pl.ds(pl.multiple_of(i * TQ, TQ), TQ)
                do, o = do_ref[rows, :], o_ref[rows, :]
                for hh in range(2):
                    doh = jnp.where((lane < 64) if hh == 0 else (lane >= 64), do, 0.0)
                    do_s[hh, rows, :] = doh.astype(BF16)
                    delta_s[hh, rows, :] = jnp.broadcast_to(jnp.sum(doh * o, axis=-1, keepdims=True), (TQ, LANE))

        kjs, vjs = (k_ref[0], k_ref[1]), (v_ref[0], v_ref[1])

        def tile(hh, start, size, kj, vj, diag):
            rows = pl.ds(pl.multiple_of(start, size), size)
            wide = lambda a: jnp.concatenate([a] * (kj.shape[0] // LANE), axis=1)
            qi, do_b = q_ref[hh, rows, :], do_s[hh, rows, :]
            p = jnp.exp2(_mm_nt(qi, kj) * QK_SCALE2 - wide(lse_ref[hh, rows, :]))
            if diag:
                p = jnp.where(_diag_visible(size), p, 0.0)
            dv = _mm_tn(do_b, p.astype(BF16))
            ds_b = (p * (_mm_nt(do_b, vj) - wide(delta_s[hh, rows, :]))).astype(BF16)
            dk = _mm_tn(qi, ds_b)
            dq_ref[hh, rows, :] += _mm(ds_b, kj)
            return dk, dv

        def step(i, carry):
            new = [tile(hh, i * TQ, TQ, kjs[hh], vjs[hh], False) for hh in range(2)]
            return tuple((carry[hh][0] + new[hh][0], carry[hh][1] + new[hh][1]) for hh in range(2))

        def diagonal(hh):
            k0, k1, v0, v1 = kjs[hh][:HQ], kjs[hh][HQ:], vjs[hh][:HQ], vjs[hh][HQ:]
            a = tile(hh, j * TQ, HQ, k0, v0, True)
            b = tile(hh, j * TQ + HQ, HQ, k0, v0, False)
            c = tile(hh, j * TQ + HQ, HQ, k1, v1, True)
            return jnp.concatenate([a[0] + b[0], c[0]], axis=1), jnp.concatenate([a[1] + b[1], c[1]], axis=1)

        carry = lax.fori_loop(j + 1, nt, step, (diagonal(0), diagonal(1)))
        for hh in range(2):
            dk_ref[hh] = carry[hh][0].T * scale
            dv_ref[hh] = carry[hh][1].T

        @pl.when(j == nt - 1)
        def _():
            dq_ref[...] = dq_ref[...] * scale

    whole = pl.BlockSpec((2, s, LANE), lambda p, j: (p, 0, 0))
    tile_spec = pl.BlockSpec((2, TQ, LANE), lambda p, j: (p, j, 0))
    cols = pl.BlockSpec((s, LANE), lambda p, j: (0, p))
    hs = jax.ShapeDtypeStruct((NH, s, LANE), F32)
    return pl.pallas_call(
        body, name="attn_bwd", grid=(NH // 2, nt),
        in_specs=[whole, tile_spec, tile_spec, cols, cols, whole, pl.BlockSpec((8, LANE), lambda p, j: (0, 0))],
        out_specs=[whole, tile_spec, tile_spec],
        out_shape=[hs, hs, hs],
        scratch_shapes=[pltpu.VMEM((2, s, LANE), BF16), pltpu.VMEM((2, s, LANE), F32)],
        compiler_params=_params(("parallel", "arbitrary")),
    )(q, k, vv, attn, dattn, lse, token)


def _hgrn_bwd(proj, lbl, states, do_raw):
    s = proj.shape[0]
    nt = s // TH
    nch = TH // HG_BLOCK

    def body(hq_ref, hf_ref, hi_ref, lbl_ref, st_ref, do_ref, dh_ref, dlbl_ref, dst, dlb):
        step = pl.program_id(1)

        @pl.when(step == 0)
        def _():
            dst[...] = jnp.zeros_like(dst)
            dlb[...] = jnp.zeros_like(dlb)

        m0, bd = _pair_masks()
        for u in range(HG_PAIRS):
            lanes = slice(u * LANE, (u + 1) * LANE)
            lb = _lower_bound(lbl_ref[:, lanes])
            gt = _hgrn_gates(hq_ref[:, lanes], hf_ref[:, lanes], lb)
            do = do_ref[:, lanes]
            qd, ki, ke = gt["qd"], gt["ki"], gt["ke"]
            v_b, do_b = hi_ref[:, lanes].astype(BF16), do.astype(BF16)
            qd_b, ki_b, ke_b = qd.astype(BF16), ki.astype(BF16), ke.astype(BF16)
            dv = jnp.zeros((TH, LANE), F32)
            dqd = jnp.zeros((TH, LANE), F32)
            dki = jnp.zeros((TH, LANE), F32)
            for hh in range(2):
                mh = m0 if hh == 0 else jnp.logical_not(m0)
                a_b = jnp.where(gt["tri"], _mm_nt(jnp.where(mh, qd, 0.0).astype(BF16), ki_b), 0.0).astype(BF16)
                doh_b = jnp.where(mh, do, 0.0).astype(BF16)
                da_b = jnp.where(gt["tri"], _mm_nt(doh_b, v_b), 0.0).astype(BF16)
                dv = dv + _mm_tn(a_b, doh_b)
                dqd = jnp.where(mh, _mm(da_b, ki_b), dqd)
                dki = jnp.where(mh, _mm_tn(da_b, qd_b), dki)
            fed = _bmm_tn(_chunks(do_b), _chunks(qd_b))
            decay = jnp.exp(gt["total"])
            ds, leaving = dst[u], [None] * nch
            for n in reversed(range(nch)):
                leaving[n] = ds
                ds = decay[n] * ds + jnp.where(bd, fed[n], 0.0)
            dst[u] = ds
            leaving = jnp.stack(leaving)
            entering = st_ref[u]
            leaving_b = leaving.astype(BF16)
            dke3 = _bmm_nn(_chunks(v_b), leaving_b)
            dv = dv + _bmm_nt(_chunks(ke_b), leaving_b).reshape(TH, LANE)
            dqd = dqd + _bmm_nn(_chunks(do_b), entering.astype(BF16)).reshape(TH, LANE)
            dke = dke3.reshape(TH, LANE)
            dlast = (jnp.sum(dke3 * _chunks(ke), axis=1, keepdims=True)
                     + jnp.sum(leaving * entering, axis=1, keepdims=True) * decay)
            dk = dki * gt["ei"] + dke * gt["ee"]
            dcum = dqd * qd - dki * ki - dke * ke
            dg = _chunk_cumsum(dcum, reverse=True) + jnp.broadcast_to(dlast, (nch, HG_BLOCK, LANE)).reshape(TH, LANE)
            sig = gt["sig"]
            df = dg / gt["f"] - dk
            dlb[:, lanes] += jnp.sum(df * (1.0 - sig), axis=0, keepdims=True)
            dh_ref[0, :, lanes] = (dqd * gt["e"]).astype(BF16)
            dh_ref[1, :, lanes] = ((df * (1.0 - lb)) * sig * (1.0 - sig)).astype(BF16)
            dh_ref[2, :, lanes] = dv.astype(BF16)

        @pl.when(step == nt - 1)
        def _():
            lb = _lower_bound(lbl_ref[...])
            da0 = dlb[...] * lb * (1.0 - lb)
            dlbl_ref[...] = jnp.concatenate([da0, -da0], axis=0)

    wide = HG_PAIRS * LANE
    col = lambda base: pl.BlockSpec((TH, wide), lambda p, i: (nt - 1 - i, base // wide + p))
    tile = pl.BlockSpec((TH, wide), lambda p, i: (nt - 1 - i, p))
    sds = jax.ShapeDtypeStruct
    return pl.pallas_call(
        body, name="hgrn_bwd", grid=(NH // 2 // HG_PAIRS, nt),
        in_specs=[col(P_HQ), col(P_HF), col(P_HI), pl.BlockSpec((2, wide), lambda p, i: (0, p)),
                  pl.BlockSpec((HG_PAIRS, nch, LANE, LANE), lambda p, i: (p, nt - 1 - i, 0, 0)), tile],
        out_specs=[pl.BlockSpec((3, TH, wide), lambda p, i: (0, nt - 1 - i, p)), pl.BlockSpec((2, wide), lambda p, i: (0, p))],
        out_shape=[sds((3, s, 512), BF16), sds((2, 512), F32)],
        scratch_shapes=[pltpu.VMEM((HG_PAIRS, LANE, LANE), F32), pltpu.VMEM((1, wide), F32)],
        compiler_params=_params(("parallel", "arbitrary")),
    )(proj, proj, proj, lbl, states, do_raw)


def _norm_rows_bwd(v, r, g, dn):
    dgv = dn * g
    return r * dgv - v * (r * r * r) * jnp.mean(v * dgv, axis=-1, keepdims=True)


def _qkv_bwd(proj, dq, dk, dvv, g_q, g_kv, w_uq_p, w_k_p, w_v_p, rc, rs1, rs2):
    s = proj.shape[0]

    def body(cq_ref, ckv_ref, dq_ref, dk_ref, dv_ref, gq_ref, gkv_ref, wq_ref, wk_ref, wv_ref, c_ref, s1_ref, s2_ref,
             dcq_ref, dckv_ref, dkpe_ref, dwq_ref, dwk_ref, dwv_ref, dgq_ref, dgkv_ref):
        @pl.when(pl.program_id(0) == 0)
        def _():
            for rf in (dwq_ref, dwk_ref, dwv_ref, dgq_ref, dgkv_ref):
                rf[...] = jnp.zeros_like(rf)

        c, s1, s2 = c_ref[...], s1_ref[...], s2_ref[...]
        cq, ckv = cq_ref[...], ckv_ref[...]
        gq, gkv = gq_ref[...], gkv_ref[...]
        cqn, rq = _norm_rows(cq, gq)
        ckvn, rkv = _norm_rows(ckv, gkv)
        cqn_b, ckvn_b = cqn.astype(BF16), ckvn.astype(BF16)
        dqf = jnp.concatenate([_rope_t(dq_ref[h], c, s1, s2) for h in range(NH)], axis=1).astype(BF16)
        dkf = jnp.concatenate([dk_ref[h] for h in range(NH)], axis=1).astype(BF16)
        dvf = jnp.concatenate([dv_ref[h] for h in range(NH)], axis=1).astype(BF16)
        dkpe = dk_ref[0]
        for h in range(1, NH):
            dkpe = dkpe + dk_ref[h]
        lane = lax.broadcasted_iota(jnp.int32, (TM, LANE), 1)
        dkpe = jnp.where((lane >= QK_NOPE) & (lane < QK_NOPE + QK_ROPE), dkpe, 0.0)
        dkpe_ref[...] = _rope_t(dkpe, c, s1, s2).astype(BF16)
        dwq_ref[...] += _mm_tn(cqn_b, dqf)
        dwk_ref[...] += _mm_tn(ckvn_b, dkf)
        dwv_ref[...] += _mm_tn(ckvn_b, dvf)
        dcqn = _mm_nt(dqf, wq_ref[...])
        dckvn = _mm_nt(dkf, wk_ref[...]) + _mm_nt(dvf, wv_ref[...])
        dgq_ref[...] += jnp.sum(dcqn * (cq * rq), axis=0, keepdims=True)
        dgkv_ref[...] += jnp.sum(dckvn * (ckv * rkv), axis=0, keepdims=True)
        dcq_ref[...] = _norm_rows_bwd(cq, rq, gq, dcqn).astype(BF16)
        dckv_ref[...] = _norm_rows_bwd(ckv, rkv, gkv, dckvn).astype(BF16)

    row = lambda w, j=0: pl.BlockSpec((TM, w), lambda i: (i, j))
    full = lambda a: pl.BlockSpec(a.shape, lambda i: (0,) * a.ndim)
    acc = lambda shape: pl.BlockSpec(shape, lambda i: (0, 0))
    heads = pl.BlockSpec((NH, TM, LANE), lambda i: (0, i, 0))
    sds = jax.ShapeDtypeStruct
    return pl.pallas_call(
        body, name="qkv_bwd", grid=(s // TM,),
        in_specs=[row(Q_LORA, P_CQ // Q_LORA), row(KV_LORA, P_CKV // KV_LORA), heads, heads, heads,
                  full(g_q), full(g_kv), full(w_uq_p), full(w_k_p), full(w_v_p), row(LANE), row(LANE), row(LANE)],
        out_specs=[row(Q_LORA), row(KV_LORA), row(LANE), acc((Q_LORA, D)), acc((KV_LORA, D)), acc((KV_LORA, D)),
                   acc((1, Q_LORA)), acc((1, KV_LORA))],
        out_shape=[sds((s, Q_LORA), BF16), sds((s, KV_LORA), BF16), sds((s, LANE), BF16), sds((Q_LORA, D), F32),
                   sds((KV_LORA, D), F32), sds((KV_LORA, D), F32), sds((1, Q_LORA), F32), sds((1, KV_LORA), F32)],
        compiler_params=_params(("arbitrary",)),
    )(proj, proj, dq, dk, dvv, g_q, g_kv, w_uq_p, w_k_p, w_v_p, rc, rs1, rs2)


def _front_bwd(x, dout, dmg, dga, dh3, dgb, dcq, dckv, dkpe, g_pre, w_in_pt, token):
    s = x.shape[0]

    def body(x_ref, do_ref, dmg_ref, dga_ref, dh3_ref, dgb_ref, dcq_ref, dckv_ref, dkpe_ref, g_ref, w_ref, token_ref, gx_ref, dg_ref):
        @pl.when(pl.program_id(0) == 0)
        def _():
            dg_ref[...] = jnp.zeros_like(dg_ref)

        xv, g = x_ref[...], g_ref[...]
        _, r = _norm_rows(xv, g)
        pieces = ((dmg_ref[...], P_MERGE), (dga_ref[...], P_GA), (dh3_ref[0], P_HQ), (dh3_ref[1], P_HF), (dh3_ref[2], P_HI),
                  (dgb_ref[...], P_GB), (dcq_ref[...], P_CQ), (dckv_ref[...], P_CKV), (dkpe_ref[...], P_KPE))
        dh = jnp.zeros((TM, D), F32)
        for piece, off in pieces:
            dh = dh + _mm(piece, w_ref[off:off + piece.shape[1], :])
        dg_ref[...] += jnp.sum(dh * (xv * r), axis=0, keepdims=True)
        gx_ref[...] = do_ref[...] + _norm_rows_bwd(xv, r, g, dh)

    row = lambda w: pl.BlockSpec((TM, w), lambda i: (i, 0))
    full = lambda a: pl.BlockSpec(a.shape, lambda i: (0,) * a.ndim)
    sds = jax.ShapeDtypeStruct
    return pl.pallas_call(
        body, name="front_bwd", grid=(s // TM,),
        in_specs=[row(D), row(D), row(2048), row(512), pl.BlockSpec((3, TM, 512), lambda i: (0, i, 0)), row(512), row(Q_LORA),
                  row(KV_LORA), row(LANE), full(g_pre), full(w_in_pt), pl.BlockSpec(memory_space=pl.ANY)],
        out_specs=[row(D), pl.BlockSpec((1, D), lambda i: (0, 0))],
        out_shape=[sds((s, D), F32), sds((1, D), F32)],
        compiler_params=_params(("arbitrary",)),
    )(x, dout, dmg, dga, dh3, dgb, dcq, dckv, dkpe, g_pre, w_in_pt, token)


TK_GRAD = 1024


def _win_grad(h, pieces, name):
    s = h.shape[0]
    n = len(pieces)

    def body(h_ref, *refs):
        @pl.when(pl.program_id(0) == 0)
        def _():
            for o_ref in refs[n:]:
                o_ref[...] = jnp.zeros_like(o_ref)

        hv = h_ref[...]
        for d_ref, o_ref in zip(refs[:n], refs[n:]):
            if len(d_ref.shape) == 3:
                for k in range(d_ref.shape[0]):
                    o_ref[k] += _mm_tn(d_ref[k], hv)
            else:
                o_ref[...] += _mm_tn(d_ref[...], hv)

    def in_spec(p):
        if p.ndim == 3:
            return pl.BlockSpec((p.shape[0], TK_GRAD, p.shape[2]), lambda kk: (0, kk, 0))
        return pl.BlockSpec((TK_GRAD, p.shape[1]), lambda kk: (kk, 0))

    out_shapes = [(p.shape[0], p.shape[2], D) if p.ndim == 3 else (p.shape[1], D) for p in pieces]
    return pl.pallas_call(
        body, name=name, grid=(s // TK_GRAD,),
        in_specs=[pl.BlockSpec((TK_GRAD, D), lambda kk: (kk, 0))] + [in_spec(p) for p in pieces],
        out_specs=[pl.BlockSpec(sh, lambda kk, nd=len(sh): (0,) * nd) for sh in out_shapes],
        out_shape=[jax.ShapeDtypeStruct(sh, F32) for sh in out_shapes],
        compiler_params=_params(("arbitrary",)),
    )(h, *pieces)


def _pad_win_t(w_in_t):
    z = lambda n: jnp.zeros((n, w_in_t.shape[1]), w_in_t.dtype)
    sl = lambda o, n: w_in_t[o:o + n]
    return jnp.concatenate([sl(O_MERGE, 2048), sl(O_GA, 512), sl(O_HQ, 512), sl(O_HF, 512), sl(O_HI, 512), sl(O_GB, 512),
                            sl(O_CQ, Q_LORA), sl(O_CKV, KV_LORA), z(64), sl(O_KPE, QK_ROPE), z(32)], axis=0)


def _pad_wuq(w_uq):
    w = w_uq.reshape(Q_LORA, NH, QK_NOPE + QK_ROPE)
    return jnp.pad(w, ((0, 0), (0, 0), (0, LANE - QK_NOPE - QK_ROPE))).reshape(Q_LORA, NH * LANE)


def _unpad_wuq(g):
    return g.reshape(Q_LORA, NH, LANE)[:, :, :QK_NOPE + QK_ROPE].reshape(Q_LORA, NH * (QK_NOPE + QK_ROPE))


def _pad_wukv(w_ukv):
    w = w_ukv.reshape(KV_LORA, NH, QK_NOPE + V_DIM)
    w_k = jnp.pad(w[:, :, :QK_NOPE], ((0, 0), (0, 0), (0, LANE - QK_NOPE))).reshape(KV_LORA, NH * LANE)
    wv = w[:, :, QK_NOPE:].reshape(KV_LORA, NH // 2, 2, 1, V_DIM)
    eye = jnp.eye(2, dtype=w.dtype).reshape(1, 1, 2, 2, 1)
    return w_k, (wv * eye).reshape(KV_LORA, NH * LANE)


def _unpad_wukv(gk, gv):
    gk = gk.reshape(KV_LORA, NH, LANE)[:, :, :QK_NOPE]
    gv = gv.reshape(KV_LORA, NH // 2, 2, 2, V_DIM)
    gv = jnp.stack([gv[:, :, 0, 0], gv[:, :, 1, 1]], axis=2).reshape(KV_LORA, NH, V_DIM)
    return jnp.concatenate([gk, gv], axis=-1).reshape(KV_LORA, NH * (QK_NOPE + V_DIM))


def _local_step(x, tgt, g_pre, w_in_t, b_gate, g_q, g_kv, lb_logits, g_hgrn, g_post, weights, exchange=None):
    s = x.shape[0]
    w_in_p = _pad_win_t(w_in_t)
    rc, rs1, rs2 = _rope_tables(s)
    g_hg = jnp.tile(g_hgrn, (1, NH))

    proj, h = _front_fwd(x, g_pre, w_in_p, weights.tokens)
    w_uq, w_ukv = weights.qkv(h)
    w_uq_p = _pad_wuq(w_uq)
    w_k_p, w_v_p = _pad_wukv(w_ukv)
    q, k, vv = _qkv_fwd(proj, g_q, g_kv, w_uq_p, w_k_p, w_v_p, rc, rs1, rs2)
    attn, lse = _attn_fwd(q, k, vv)
    o_raw, states = _hgrn_fwd(proj, lb_logits)
    wa, wb, w_out = weights.mid(o_raw)
    (loss, dout, dattn, dga, dor, dgb, dmg, d_wout, d_wa, d_wb, d_gpost, d_bgate, d_ghg) = _mid(
        proj, attn, o_raw, x, tgt, g_hg, b_gate, g_post, wa, wb, w_out)
    w_mg, w_ga, w_gb = _win_grad(h, [dmg, dga, dgb], "win_grad_mid")
    dh3, d_lbl = _hgrn_bwd(proj, lb_logits, states, dor)
    (w_h3,) = _win_grad(h, [dh3], "win_grad_hgrn")
    d_win_rest = jnp.concatenate([w_ga, w_h3[0], w_h3[1], w_h3[2], w_gb, w_mg], axis=0)
    early = dict(w_in_rest=d_win_rest, w_branch_a=d_wa, w_branch_b=d_wb, w_out=d_wout)
    token = exchange.start_early(early) if exchange else jnp.zeros((8, LANE), F32)
    dq, dk, dvv = _attn_bwd(q, k, vv, attn, dattn, lse, token)
    dcq, dckv, dkpe, d_wuq_p, d_wk_p, d_wv_p, d_gq, d_gkv = _qkv_bwd(proj, dq, dk, dvv, g_q, g_kv, w_uq_p, w_k_p, w_v_p, rc, rs1, rs2)
    w_cq, w_ckv, w_kpe = _win_grad(h, [dcq, dckv, dkpe], "win_grad_qkv")
    d_win_qkv = jnp.concatenate([w_cq, w_ckv, w_kpe[64:64 + QK_ROPE]], axis=0)
    late = dict(w_in_qkv=d_win_qkv, w_uq=_unpad_wuq(d_wuq_p), w_ukv=_unpad_wukv(d_wk_p, d_wv_p))
    token = exchange.start_late(late) if exchange else jnp.zeros((8, LANE), F32)
    grad_x, d_gpre = _front_bwd(x, dout, dmg, dga, dh3, dgb, dcq, dckv, dkpe, g_pre, w_in_p, token)
    vec_grads = dict(g_pre=d_gpre, b_gate=d_bgate, g_q=d_gq, g_kv=d_gkv, lb_logits=d_lbl, g_hgrn=d_ghg, g_post=d_gpost)
    return loss, grad_x, dict(early, **late), vec_grads


SHARD_SHAPES = (("w_in", (1416, 1024)), ("w_uq", (192, 768)), ("w_ukv", (256, 256)), ("w_branch_a", (512, 256)),
                ("w_branch_b", (512, 256)), ("w_out", (256, 1024)))
BIG = tuple(n for n, _ in SHARD_SHAPES)
ROW_SHARDED = ("w_in", "w_uq", "w_out")
GATHER_SPLIT_AXIS = dict(w_in=1, w_uq=0, w_ukv=0, w_branch_a=0, w_branch_b=0, w_out=0)
N_CHIPS = 4
QKV_ROWS = Q_LORA + KV_LORA + QK_ROPE
W_IN_PAD_ROWS = 1440


def _to_block(name, a):
    return a[0].T if name == "w_in" else a[0]


def _from_block(name, a):
    return a.T[None] if name == "w_in" else a[None]
VEC_ROWS = (("g_pre", 0, 1024), ("b_gate", 1, 2048), ("g_q", 2, 768), ("g_kv", 3, 256), ("g_hgrn", 6, 64), ("g_post", 7, 1024))
VEC_LB_ROW = 4
VEC_SHAPE = (8, 2048)


def _split_by_chip(name, g):
    a, b = dict(SHARD_SHAPES)[name]
    return g.reshape(N_CHIPS, a, b) if name in ROW_SHARDED else g.reshape(a, N_CHIPS, b).transpose(1, 0, 2)


def _join_chips(name, w):
    a, b = dict(SHARD_SHAPES)[name]
    return w.reshape(N_CHIPS * a, b) if name in ROW_SHARDED else w.transpose(1, 0, 2).reshape(a, N_CHIPS * b)


MESH = pl.DeviceIdType.MESH
HBM = pl.BlockSpec(memory_space=pltpu.HBM)


def _mesh_place():
    x, y, c = lax.axis_index("x"), lax.axis_index("y"), lax.axis_index("c")
    return x, y, c, 2 * x + y, [(1 - x, y), (x, 1 - y), (1 - x, 1 - y)]


def _remote(src, dst, send_sems, recv_sems, k, to):
    return pltpu.make_async_remote_copy(src_ref=src, dst_ref=dst, send_sem=send_sems.at[k], recv_sem=recv_sems.at[k],
                                        device_id=to, device_id_type=MESH)


def _gather_weights(shards, split_axes):
    n = len(shards)

    def body(*refs):
        srcs, outs = refs[:n], refs[n:2 * n]
        ici_send, ici_recv, d2d_send, d2d_recv, local_sems = refs[2 * n:]
        x, y, c, me, chips = _mesh_place()
        sibling = (x, y, 1 - c)

        def half(ref, k, which):
            size = shards[k].shape[split_axes[k]] // 2
            part = pl.ds(pl.multiple_of(which * size, size), size)
            return ref.at[part] if split_axes[k] == 0 else ref.at[:, part]

        own = [pltpu.make_async_copy(srcs[k], outs[k].at[me], local_sems.at[k]) for k in range(n)]
        for cp in own:
            cp.start()
        started = []
        for k in range(n):
            for j, (px, py) in enumerate(chips):
                cp = _remote(half(srcs[k], k, c), half(outs[k].at[me], k, c), ici_send, ici_recv, 3 * k + j, (px, py, c))
                cp.start()
                started.append(cp)
        for k in range(n):
            for j, (px, py) in enumerate(chips):
                landed = half(outs[k].at[2 * px + py], k, c)
                _remote(landed, landed, ici_send, ici_recv, 3 * k + j, (px, py, c)).wait_recv()
                cp = _remote(landed, landed, d2d_send, d2d_recv, 3 * k + j, sibling)
                cp.start()
                started.append(cp)
        for k in range(n):
            for j, (px, py) in enumerate(chips):
                other = half(outs[k].at[2 * px + py], k, 1 - c)
                _remote(other, other, d2d_send, d2d_recv, 3 * k + j, sibling).wait_recv()
        for cp in started:
            cp.wait_send()
        for cp in own:
            cp.wait()

    sems = pltpu.SemaphoreType.DMA((3 * n,))
    return pl.pallas_call(
        body, name="gather_weights", in_specs=[HBM] * n, out_specs=[HBM] * n,
        out_shape=[jax.ShapeDtypeStruct((N_CHIPS,) + s.shape, s.dtype) for s in shards],
        scratch_shapes=[sems, sems, sems, sems, pltpu.SemaphoreType.DMA((n,))],
        compiler_params=pltpu.CompilerParams(has_side_effects=True),
    )(*shards)


def _sibling_exchange(srcs, name, after=None):
    n = len(srcs)
    extra = [] if after is None else [after]

    def body(*refs):
        src_refs, outs = refs[:n], refs[n + len(extra):2 * n + len(extra)]
        send_sems, recv_sems = refs[2 * n + len(extra):]
        sibling = (lax.axis_index("x"), lax.axis_index("y"), 1 - lax.axis_index("c"))
        copies = [_remote(src_refs[k], outs[k], send_sems, recv_sems, k, sibling) for k in range(n)]
        for cp in copies:
            cp.start()
        for cp in copies:
            cp.wait()

    sems = pltpu.SemaphoreType.DMA((n,))
    return pl.pallas_call(
        body, name=name, in_specs=[HBM] * n + [pl.BlockSpec(memory_space=pl.ANY)] * len(extra), out_specs=[HBM] * n,
        out_shape=[jax.ShapeDtypeStruct(s.shape, s.dtype) for s in srcs],
        scratch_shapes=[sems, sems],
        compiler_params=pltpu.CompilerParams(has_side_effects=True),
    )(*srcs, *extra)


SEM = pl.BlockSpec(memory_space=pltpu.SEMAPHORE)
DATAFLOW = pltpu.SideEffectType.DATAFLOW_SIDE_EFFECTING


def _exchange_copies(srcs, to_first, src_refs, land_refs, send_sems, recv_sems):
    x, y, c, me, chips = _mesh_place()
    n = len(srcs)
    sends, recvs = [], []
    for k in range(n):
        if k in to_first:
            base = 3 * n + 4 * to_first.index(k)
            sends.append((me != 0, pltpu.make_async_remote_copy(
                src_ref=src_refs[k], dst_ref=land_refs[k].at[me], send_sem=send_sems.at[base], recv_sem=recv_sems.at[base + me],
                device_id=(0, 0, c), device_id_type=MESH)))
            for s in range(1, N_CHIPS):
                recvs.append((me == 0, pltpu.make_async_remote_copy(
                    src_ref=src_refs[k], dst_ref=land_refs[k].at[s], send_sem=send_sems.at[base], recv_sem=recv_sems.at[base + s],
                    device_id=(s // 2, s % 2, c), device_id_type=MESH)))
        else:
            slab = (lambda t, k=k: src_refs[k]) if srcs[k].ndim == 2 else (lambda t, k=k: src_refs[k].at[t])
            for j, (px, py) in enumerate(chips):
                sends.append((None, _remote(slab(2 * px + py), land_refs[k].at[me], send_sems, recv_sems, 3 * k + j, (px, py, c))))
                recvs.append((None, _remote(slab(me), land_refs[k].at[2 * px + py], send_sems, recv_sems, 3 * k + j, (px, py, c))))
    return sends, recvs


def _when(pred, fn):
    if pred is None:
        fn()
    else:
        pl.when(pred)(fn)


def _exchange_start(srcs, to_first, name, after=None):
    n = len(srcs)
    n_sems = 3 * n + 4 * len(to_first)
    lands = [lax.empty((N_CHIPS,) + s.shape[-2:], s.dtype) for s in srcs]
    extra = [] if after is None else [after]

    def body(*refs):
        src_refs, land_refs = refs[:n], refs[n:2 * n]
        send_sems, recv_sems, token = refs[2 * n + len(extra)], refs[2 * n + len(extra) + 1], refs[-1]
        sends, _ = _exchange_copies(srcs, to_first, src_refs, land_refs, send_sems, recv_sems)
        for pred, cp in sends:
            _when(pred, cp.start)
        token[...] = jnp.zeros_like(token)

    hbm = lambda a: pltpu.HBM(a.shape, a.dtype)
    res = pl.pallas_call(
        body, name=name,
        out_shape=[pltpu.SemaphoreType.DMA((n_sems,)), pltpu.SemaphoreType.DMA((n_sems,))] + [hbm(a) for a in srcs + lands]
        + [jax.ShapeDtypeStruct((8, LANE), F32)],
        in_specs=[HBM] * (2 * n) + [pl.BlockSpec(memory_space=pl.ANY)] * len(extra),
        out_specs=[SEM, SEM] + [HBM] * (2 * n) + [pl.BlockSpec(memory_space=pltpu.VMEM)],
        input_output_aliases={i: 2 + i for i in range(2 * n)},
        compiler_params=pltpu.CompilerParams(has_side_effects=DATAFLOW),
    )(*[pltpu.with_memory_space_constraint(a, pltpu.HBM) for a in srcs + lands], *extra)
    return res[:-1], res[-1]


def _exchange_wait(srcs, to_first, started, after, name):
    n = len(srcs)
    send_sems, recv_sems, thru = started[0], started[1], started[2:]

    def body(*refs):
        src_refs, land_refs, send_ref, recv_ref = refs[:n], refs[n:2 * n], refs[2 * n], refs[2 * n + 1]
        sends, recvs = _exchange_copies(srcs, to_first, src_refs, land_refs, send_ref, recv_ref)
        for pred, cp in sends:
            _when(pred, cp.wait_send)
        for pred, cp in recvs:
            _when(pred, cp.wait_recv)

    res = pl.pallas_call(
        body, name=name, out_shape=[pltpu.HBM(a.shape, a.dtype) for a in thru],
        in_specs=[HBM] * (2 * n) + [SEM, SEM, pl.BlockSpec(memory_space=pl.ANY)], out_specs=[HBM] * (2 * n),
        input_output_aliases={i: i for i in range(2 * n)},
        compiler_params=pltpu.CompilerParams(has_side_effects=DATAFLOW),
    )(*thru, send_sems, recv_sems, after)
    return res[n:]


ROW_TILE = 256
COL_TILE = 256


def _block_tiling(a, b):
    if a <= ROW_TILE or a % ROW_TILE == 0:
        ta = min(a, ROW_TILE)
        return a // ta, (ta, b), lambda i: (i, 0)
    return b // COL_TILE, (a, COL_TILE), lambda i: (0, i)


def _sum_landed(land, own, name, first_land=None, first_own=None):
    _, a, b = land.shape
    steps, tile, at = _block_tiling(a, b)
    extra = first_land is not None

    def body(*refs):
        p_ref, own_ref, o_ref = refs[0], refs[1], refs[-1]
        me = 2 * lax.axis_index("x") + lax.axis_index("y")
        own = own_ref[...].astype(F32)
        slot = lambda t: jnp.where(me == t, own, p_ref[t].astype(F32))
        o_ref[...] = ((slot(0) + slot(1)) + slot(2)) + slot(3)
        if extra:
            fp_ref, fo_ref = refs[2], refs[3]
            r = fo_ref.shape[0]

            @pl.when(me == 0)
            def _():
                f = lambda t: fp_ref[t].astype(F32)
                rows = pl.ds(pl.multiple_of(lax.axis_index("c") * r, 8), r)
                o_ref[rows, :] += ((fo_ref[...].astype(F32) + f(1)) + f(2)) + f(3)

    in_specs = [pl.BlockSpec((N_CHIPS,) + tile, lambda i: (0,) + at(i)), pl.BlockSpec(tile, at)]
    args = [land, own]
    if extra:
        r = first_own.shape[0]
        assert tile[0] == a, "the extra rows need whole columns in a step"
        in_specs += [pl.BlockSpec((N_CHIPS, r, tile[1]), lambda i: (0,) + at(i)), pl.BlockSpec((r, tile[1]), at)]
        args += [first_land, first_own]
    return pl.pallas_call(
        body, name=name, grid=(steps,), in_specs=in_specs, out_specs=pl.BlockSpec(tile, at),
        out_shape=jax.ShapeDtypeStruct((a, b), F32), compiler_params=_params(("parallel",)),
    )(*args)


def _add_cast(a, b, name):
    def body(a_ref, b_ref, o_ref):
        o_ref[...] = (a_ref[...] + b_ref[...]).astype(BF16)

    return pl.pallas_call(body, name=name, out_shape=jax.ShapeDtypeStruct(a.shape, BF16),
                          compiler_params=_params(()))(a, b)


class _LaterWeights:
    QKV = ("w_uq", "w_ukv")
    MID = ("w_branch_a", "w_branch_b", "w_out")

    def __init__(self, blocks, after):
        self.blocks = blocks
        self.qkv_started, t1 = _exchange_start([blocks[n] for n in self.QKV], (), "weights_qkv_start", after)
        self.mid_started, t2 = _exchange_start([blocks[n] for n in self.MID], (), "weights_mid_start", after)
        self.tokens = [t1, t2]

    def _whole(self, names, started, after, name):
        landed = _exchange_wait([self.blocks[n] for n in names], (), started, after, name)
        me = 2 * lax.axis_index("x") + lax.axis_index("y")
        return [_join_chips(n, lax.dynamic_update_index_in_dim(land, self.blocks[n], me, 0)) for n, land in zip(names, landed)]

    def qkv(self, after):
        return self._whole(self.QKV, self.qkv_started, after, "weights_qkv_wait")

    def mid(self, after):
        return self._whole(self.MID, self.mid_started, after, "weights_mid_wait")


class _GradExchange:
    EARLY = ("w_in", "w_branch_a", "w_branch_b", "w_out")
    LATE = ("w_uq", "w_ukv")

    def __init__(self, state):
        self.state = state
        self.outs = {}

    @staticmethod
    def _own(slabs):
        return lax.dynamic_index_in_dim(slabs, 2 * lax.axis_index("x") + lax.axis_index("y"), axis=0, keepdims=False)

    def start_early(self, g):
        full = jnp.concatenate([jnp.zeros((QKV_ROWS, D), F32), g["w_in_rest"]], axis=0)
        g = dict(g, w_in=full)
        self.early = [_split_by_chip(n, g[n]).astype(BF16) for n in self.EARLY]
        self.early_started, token = _exchange_start(self.early, (), "grads_early_start")
        return token

    def start_late(self, g):
        self.early_landed = _exchange_wait(self.early, (), self.early_started, g["w_uq"], "grads_early_wait")
        half = QKV_ROWS // 2
        c = lax.axis_index("c")
        mine = lax.dynamic_slice_in_dim(g["w_in_qkv"], c * half, half, axis=0)
        (theirs,) = _sibling_exchange([lax.dynamic_slice_in_dim(g["w_in_qkv"], (1 - c) * half, half, axis=0)], "sibling_qkv_rows")
        self.late = [_split_by_chip(n, g[n]).astype(BF16) for n in self.LATE] + [_add_cast(mine, theirs, "add_qkv_rows")]
        self.late_started, token = _exchange_start(self.late, (2,), "grads_late_start")
        names = self.EARLY[1:]
        mine = [_sum_landed(land, self._own(slabs), "sum_" + n) for n, slabs, land in list(zip(self.EARLY, self.early, self.early_landed))[1:]]
        theirs = _sibling_exchange(mine, "sibling_early", after=token)
        for n, a, b in zip(names, mine, theirs):
            self.outs[n] = _adamw(a, b, *self.state[n], "adamw_" + n)
        return self.outs[names[-1]][0]

    def finish(self, after):
        late_landed = _exchange_wait(self.late, (2,), self.late_started, after, "grads_late_wait")
        sums = {"w_in": _sum_landed(self.early_landed[0], self._own(self.early[0]), "sum_w_in",
                                    first_land=late_landed[2], first_own=self.late[2])}
        for n, slabs, land in zip(self.LATE, self.late, late_landed):
            sums[n] = _sum_landed(land, self._own(slabs), "sum_" + n)
        return sums


def _adamw_math(g, w, m, v):
    nm = ADAM_B1 * m + (1.0 - ADAM_B1) * g
    nv = ADAM_B2 * v + (1.0 - ADAM_B2) * (g * g)
    m_hat = nm / (1.0 - ADAM_B1 ** ADAM_STEP)
    v_hat = nv / (1.0 - ADAM_B2 ** ADAM_STEP)
    return -ADAM_LR * (m_hat / (jnp.sqrt(v_hat) + ADAM_EPS) + ADAM_WD * w), nm, nv


def _adamw(p_mine, p_sibling, w, m, v, name):
    a, b = p_mine.shape
    steps, tile, at = _block_tiling(a, b)

    def body(a_ref, b_ref, w_ref, m_ref, v_ref, g_ref, d_ref, nm_ref, nv_ref):
        g = a_ref[...] + b_ref[...]
        g_ref[...] = g
        d_ref[...], nm_ref[...], nv_ref[...] = _adamw_math(g, w_ref[...], m_ref[...], v_ref[...])

    spec = pl.BlockSpec(tile, at)
    sds = jax.ShapeDtypeStruct((a, b), F32)
    return pl.pallas_call(
        body, name=name, grid=(steps,), in_specs=[spec] * 5, out_specs=[spec] * 4, out_shape=[sds] * 4,
        compiler_params=_params(("parallel",)),
    )(p_mine, p_sibling, w, m, v)


LOSS_AT = (2, 1024)


def _vec_pack(vg, loss):
    names = [n for n, _, _ in VEC_ROWS]

    def body(*refs):
        o_ref = refs[-1]
        lb_ref, loss_ref = refs[len(names)], refs[len(names) + 1]
        o_ref[...] = jnp.zeros_like(o_ref)
        o_ref[LOSS_AT[0]:LOSS_AT[0] + 1, LOSS_AT[1]:LOSS_AT[1] + LANE] = jnp.broadcast_to(loss_ref[...], (1, LANE))
        for (name, row, size), ref in zip(VEC_ROWS, refs):
            if name == "g_hgrn":
                r = lax.broadcasted_iota(jnp.int32, (NH * V_DIM, LANE), 0)
                c = lax.broadcasted_iota(jnp.int32, (NH * V_DIM, LANE), 1)
                fold = ((r % V_DIM) == c).astype(F32)
                o_ref[row:row + 1, 0:LANE] = jnp.dot(ref[...], fold, precision=HIGHEST, preferred_element_type=F32)
            else:
                o_ref[row:row + 1, 0:size] = ref[...]
        o_ref[VEC_LB_ROW:VEC_LB_ROW + 2, 0:512] = lb_ref[...]

    return pl.pallas_call(body, name="vec_pack", out_shape=jax.ShapeDtypeStruct(VEC_SHAPE, F32))(
        *[vg[n] for n in names], vg["lb_logits"], loss)


def _adamw_vec(p_mine, p_sibling, w, m, v):
    names = [n for n, _, _ in VEC_ROWS] + ["lb_logits"]
    k = len(names)

    def body(a_ref, b_ref, *refs):
        ins, outs = refs[:3 * k], refs[3 * k:]
        at = (slice(LOSS_AT[0], LOSS_AT[0] + 1), slice(LOSS_AT[1], LOSS_AT[1] + LANE))
        outs[-1][...] = a_ref[at] + b_ref[at]
        for i, name in enumerate(names):
            if name == "lb_logits":
                rows, cols = slice(VEC_LB_ROW, VEC_LB_ROW + 2), slice(0, 512)
            else:
                _, row, size = VEC_ROWS[i]
                rows, cols = slice(row, row + 1), slice(0, size)
            g = a_ref[rows, cols] + b_ref[rows, cols]
            d, nm, nv = _adamw_math(g, ins[i][...], ins[k + i][...], ins[2 * k + i][...])
            for o_ref, val in zip(outs[4 * i:4 * i + 4], (g, d, nm, nv)):
                o_ref[...] = val

    shapes = [jax.ShapeDtypeStruct(w[n].shape, F32) for n in names for _ in range(4)] + [jax.ShapeDtypeStruct((1, LANE), F32)]
    res = pl.pallas_call(body, name="adamw_vec", out_shape=shapes)(
        p_mine, p_sibling, *[w[n] for n in names], *[m[n] for n in names], *[v[n] for n in names])
    return [{n: res[4 * i + j] for i, n in enumerate(names)} for j in range(4)], res[-1]


WEIGHTS = ("g_pre", "w_in", "b_gate", "g_q", "w_uq", "g_kv", "w_ukv", "lb_logits", "g_hgrn", "w_branch_a", "w_branch_b", "w_out", "g_post")


def kernel(x, g_pre, w_in, b_gate, g_q, w_uq, g_kv, w_ukv, lb_logits, g_hgrn, w_branch_a, w_branch_b, w_out, g_post, loss_target, m_g_pre, m_w_in, m_b_gate, m_g_q, m_w_uq, m_g_kv, m_w_ukv, m_lb_logits, m_g_hgrn, m_w_branch_a, m_w_branch_b, m_w_out, m_g_post, v_g_pre, v_w_in, v_b_gate, v_g_q, v_w_uq, v_g_kv, v_w_ukv, v_lb_logits, v_g_hgrn, v_w_branch_a, v_w_branch_b, v_w_out, v_g_post):
    w = dict(g_pre=g_pre, w_in=w_in, b_gate=b_gate, g_q=g_q, w_uq=w_uq, g_kv=g_kv, w_ukv=w_ukv, lb_logits=lb_logits, g_hgrn=g_hgrn,
             w_branch_a=w_branch_a, w_branch_b=w_branch_b, w_out=w_out, g_post=g_post)
    m = dict(g_pre=m_g_pre, w_in=m_w_in, b_gate=m_b_gate, g_q=m_g_q, w_uq=m_w_uq, g_kv=m_g_kv, w_ukv=m_w_ukv, lb_logits=m_lb_logits,
             g_hgrn=m_g_hgrn, w_branch_a=m_w_branch_a, w_branch_b=m_w_branch_b, w_out=m_w_out, g_post=m_g_post)
    v = dict(g_pre=v_g_pre, w_in=v_w_in, b_gate=v_b_gate, g_q=v_g_q, w_uq=v_w_uq, g_kv=v_g_kv, w_ukv=v_w_ukv, lb_logits=v_lb_logits,
             g_hgrn=v_g_hgrn, w_branch_a=v_w_branch_a, w_branch_b=v_w_branch_b, w_out=v_w_out, g_post=v_g_post)
    blocks = {n: _to_block(n, w[n]).astype(BF16) for n in BIG}
    padded = jnp.pad(blocks["w_in"], ((0, W_IN_PAD_ROWS - blocks["w_in"].shape[0]), (0, 0)))
    (w_in_all,) = _gather_weights([padded], [0])
    w_in_all = w_in_all[:, :blocks["w_in"].shape[0]]
    weights = _LaterWeights(blocks, w_in_all)
    state = {n: [_to_block(n, t[n]) for t in (w, m, v)] for n in BIG}
    exchange = _GradExchange(state)
    loss, grad_x, _, vec_grads = _local_step(
        x[0], loss_target[0], g_pre, _join_chips("w_in", w_in_all), b_gate, g_q, g_kv, lb_logits, g_hgrn, g_post, weights, exchange)
    vec = _vec_pack(vec_grads, loss)
    vec_started, token = _exchange_start([vec], (), "vec_start")
    sums = exchange.finish(token)
    rest = tuple(sums)
    (vec_landed,) = _exchange_wait([vec], (), vec_started, sums[rest[-1]], "vec_wait")
    mine = [sums[n] for n in rest] + [_sum_landed(vec_landed, vec, "sum_vec")]
    theirs = _sibling_exchange(mine, "sibling_grads")
    done = dict(exchange.outs)
    for k, n in enumerate(rest):
        done[n] = _adamw(mine[k], theirs[k], *state[n], "adamw_" + n)
    outs = [{}, {}, {}, {}]
    for n in BIG:
        for o, val in zip(outs, done[n]):
            o[n] = _from_block(n, val)
    vec_outs, total = _adamw_vec(mine[-1], theirs[-1], w, m, v)
    for o, vals in zip(outs, vec_outs):
        o.update(vals)
    return (total[0, 0], grad_x[None], *[o[n] for o in outs for n in WEIGHTS])
```

```python
import functools
import math

import numpy as np
import jax
import jax.numpy as jnp
from jax import lax
from jax.experimental import pallas as pl
from jax.experimental.pallas import tpu as pltpu

F32 = jnp.float32
BF16 = jnp.bfloat16
HIGHEST = lax.Precision.HIGHEST

D = 1024
NH = 8
QK_NOPE, QK_ROPE, V_DIM = 64, 32, 64
Q_LORA, KV_LORA = 768, 256
CHUNK = 64
HG_BLOCK = 32
EPS = 1e-6
D_IN = 5664
LANE = 128
P_MERGE, P_GA, P_HQ, P_HF, P_HI, P_GB, P_CQ, P_CKV, P_KPE = 0, 2048, 2560, 3072, 3584, 4096, 4608, 5376, 5632
D_P = 5760
O_CQ, O_CKV, O_KPE, O_GA, O_HQ, O_HF, O_HI, O_GB, O_MERGE = 0, 768, 1024, 1056, 1568, 2080, 2592, 3104, 3616

TM = 512
TM_MID = 256
TQ = 1024
ONES_LANE = (LANE - 1, 0)
TH = 256
HG_PAIRS = 4
VMEM_LIMIT = 56 * 1024 * 1024

ADAM_LR, ADAM_B1, ADAM_B2, ADAM_EPS, ADAM_WD, ADAM_STEP = 0.001, 0.9, 0.999, 1e-08, 0.01, 10

NT_DIMS = (((1,), (1,)), ((), ()))
TN_DIMS = (((0,), (0,)), ((), ()))


def _params(sem):
    return pltpu.CompilerParams(dimension_semantics=sem, vmem_limit_bytes=VMEM_LIMIT)


def _mm(a, b):
    return jnp.dot(a, b, preferred_element_type=F32)


def _mm_nt(a, b):
    return lax.dot_general(a, b, NT_DIMS, preferred_element_type=F32)


def _mm_tn(a, b):
    return lax.dot_general(a, b, TN_DIMS, preferred_element_type=F32)


def _sigmoid(z):
    return jax.nn.sigmoid(z)


def _rope(v, c, s1, s2):
    return v * c + pltpu.roll(v, 112, 1) * s1 + pltpu.roll(v, 16, 1) * s2


def _rope_t(dy, c, s1, s2):
    return dy * c + pltpu.roll(dy * s1, 16, 1) + pltpu.roll(dy * s2, 112, 1)


def _rope_tables(s):
    inv = 10000.0 ** (-jnp.arange(0, QK_ROPE, 2, dtype=F32) / QK_ROPE)
    ang = jnp.arange(s, dtype=F32)[:, None] * inv[None, :]
    cos, sin = jnp.cos(ang), jnp.sin(ang)
    z64, z32, o64, o32 = jnp.zeros((s, 64), F32), jnp.zeros((s, 32), F32), jnp.ones((s, 64), F32), jnp.ones((s, 32), F32)
    z16 = jnp.zeros((s, 16), F32)
    c = jnp.concatenate([o64, cos, cos, o32], axis=1)
    s1 = jnp.concatenate([z64, -sin, z16, z32], axis=1)
    s2 = jnp.concatenate([z64, z16, sin, z32], axis=1)
    return c, s1, s2


def _front_fwd(x, g_pre, w_in_pt, tokens=()):
    s = x.shape[0]
    tokens = list(tokens)

    def body(x_ref, g_ref, w_ref, *refs):
        o_ref, h_ref = refs[len(tokens):]
        xv = x_ref[...]
        r = lax.rsqrt(jnp.mean(xv * xv, axis=-1, keepdims=True) + EPS)
        h = ((xv * r) * g_ref[...]).astype(BF16)
        h_ref[...] = h
        o_ref[...] = _mm_nt(h, w_ref[...])

    return pl.pallas_call(
        body, name="front_fwd", grid=(s // TM,),
        in_specs=[pl.BlockSpec((TM, D), lambda i: (i, 0)), pl.BlockSpec((1, D), lambda i: (0, 0)),
                  pl.BlockSpec((D_P, D), lambda i: (0, 0))] + [pl.BlockSpec((8, LANE), lambda i: (0, 0))] * len(tokens),
        out_specs=[pl.BlockSpec((TM, D_P), lambda i: (i, 0)), pl.BlockSpec((TM, D), lambda i: (i, 0))],
        out_shape=[jax.ShapeDtypeStruct((s, D_P), F32), jax.ShapeDtypeStruct((s, D), BF16)],
        compiler_params=_params(("parallel",)),
    )(x, g_pre, w_in_pt, *tokens)


def _norm_rows(v, g):
    r = lax.rsqrt(jnp.mean(v * v, axis=-1, keepdims=True) + EPS)
    return (v * r) * g, r


def _qkv_fwd(proj, g_q, g_kv, w_uq_p, w_k_p, w_v_p, rc, rs1, rs2):
    s = proj.shape[0]

    def body(cq_ref, ckv_ref, kpe_ref, gq_ref, gkv_ref, wq_ref, wk_ref, wv_ref, c_ref, s1_ref, s2_ref, q_ref, k_ref, v_ref):
        c, s1, s2 = c_ref[...], s1_ref[...], s2_ref[...]
        cqn, _ = _norm_rows(cq_ref[...], gq_ref[...])
        ckvn, _ = _norm_rows(ckv_ref[...], gkv_ref[...])
        ckvn = ckvn.astype(BF16)
        qf = _mm(cqn.astype(BF16), wq_ref[...])
        kf = _mm(ckvn, wk_ref[...])
        vf = _mm(ckvn, wv_ref[...])
        kpe = _rope(kpe_ref[...], c, s1, s2)
        lane = lax.broadcasted_iota(jnp.int32, (TM, LANE), 1)
        for h in range(NH):
            blk = slice(h * LANE, (h + 1) * LANE)
            q_ref[h] = _rope(qf[:, blk], c, s1, s2).astype(BF16)
            k_ref[h] = (kf[:, blk] + kpe).astype(BF16)
            v_ref[h] = jnp.where(lane == ONES_LANE[h % 2], 1.0, vf[:, blk]).astype(BF16)

    row = lambda w, j: pl.BlockSpec((TM, w), lambda i: (i, j))
    full = lambda a: pl.BlockSpec(a.shape, lambda i: (0,) * a.ndim)
    hs = jax.ShapeDtypeStruct((NH, s, LANE), BF16)
    return pl.pallas_call(
        body, name="qkv_fwd", grid=(s // TM,),
        in_specs=[row(Q_LORA, P_CQ // Q_LORA), row(KV_LORA, P_CKV // KV_LORA), row(LANE, P_KPE // LANE),
                  full(g_q), full(g_kv), full(w_uq_p), full(w_k_p), full(w_v_p), row(LANE, 0), row(LANE, 0), row(LANE, 0)],
        out_specs=[pl.BlockSpec((NH, TM, LANE), lambda i: (0, i, 0))] * 3,
        out_shape=[hs, hs, hs],
        compiler_params=_params(("parallel",)),
    )(proj, proj, proj, g_q, g_kv, w_uq_p, w_k_p, w_v_p, rc, rs1, rs2)


LOG2E = 1.4426950408889634
QK_SCALE2 = LOG2E / math.sqrt(QK_NOPE + QK_ROPE)


HQ = TQ // 2


def _diag_visible(n):
    row = lax.broadcasted_iota(jnp.int32, (n, n), 0)
    col = lax.broadcasted_iota(jnp.int32, (n, n), 1)
    return (col // CHUNK) <= (row // CHUNK)


def _attn_fwd(q, k, vv):
    s = q.shape[1]

    def body(q_ref, k_ref, v_ref, o_ref, lse_ref):
        i = pl.program_id(1)
        qs = (q_ref[0], q_ref[1])

        def tile(hh, t, carry, diag):
            m, acc = carry
            rows = pl.ds(pl.multiple_of(t * TQ, TQ), TQ)
            sc = _mm_nt(qs[hh], k_ref[hh, rows, :])
            if diag:
                sc = jnp.where(_diag_visible(TQ), sc, -jnp.inf)
            m_new = jnp.maximum(m, jnp.max(sc, axis=-1, keepdims=True))
            alpha = jnp.exp2((m - m_new) * QK_SCALE2)
            p = jnp.exp2((sc - m_new) * QK_SCALE2).astype(BF16)
            acc = alpha * acc + _mm(p, v_ref[hh, rows, :])
            return m_new, acc

        def step(t, carry):
            return tile(0, t, carry[0], False), tile(1, t, carry[1], False)

        init = (jnp.full((TQ, 1), -jnp.inf, F32), jnp.zeros((TQ, LANE), F32))
        carry = lax.fori_loop(0, i, step, (init, init))
        lane = lax.broadcasted_iota(jnp.int32, (TQ, LANE), 1)
        out = jnp.zeros((TQ, LANE), F32)
        for hh in range(2):
            m, acc = tile(hh, i, carry[hh], True)
            l = jnp.sum(jnp.where(lane == ONES_LANE[hh], acc, 0.0), axis=-1, keepdims=True)
            out = out + jnp.where((lane < V_DIM) == (hh == 0), acc, 0.0) / l
            lse_ref[hh] = jnp.broadcast_to(m * QK_SCALE2 + jnp.log(l) * LOG2E, (TQ, LANE))
        o_ref[...] = out

    return pl.pallas_call(
        body, name="attn_fwd", grid=(NH // 2, s // TQ),
        in_specs=[pl.BlockSpec((2, TQ, LANE), lambda p, i: (p, i, 0)), pl.BlockSpec((2, s, LANE), lambda p, i: (p, 0, 0)),
                  pl.BlockSpec((2, s, LANE), lambda p, i: (p, 0, 0))],
        out_specs=[pl.BlockSpec((TQ, LANE), lambda p, i: (i, p)), pl.BlockSpec((2, TQ, LANE), lambda p, i: (p, i, 0))],
        out_shape=[jax.ShapeDtypeStruct((s, NH * V_DIM), F32), jax.ShapeDtypeStruct((NH, s, LANE), F32)],
        compiler_params=_params(("parallel", "parallel")),
    )(q, k, vv)


def _lower_bound(lbl):
    a0, a1 = lbl[0:1, :], lbl[1:2, :]
    mx = jnp.maximum(a0, a1)
    e0, e1 = jnp.exp(a0 - mx), jnp.exp(a1 - mx)
    return e0 / (e0 + e1)


def _chunk_cumsum(v, reverse=False):
    pos = lax.broadcasted_iota(jnp.int32, v.shape, 0) % HG_BLOCK
    s = 1
    while s < HG_BLOCK:
        if reverse:
            v = v + jnp.where(pos < HG_BLOCK - s, pltpu.roll(v, TH - s, 0), 0.0)
        else:
            v = v + jnp.where(pos >= s, pltpu.roll(v, s, 0), 0.0)
        s *= 2
    return v


def _hgrn_gates(hq, hf, lb):
    sig = _sigmoid(hf)
    f = lb + (1.0 - lb) * sig
    g = jnp.log(f)
    kk = 1.0 - f
    r = lax.broadcasted_iota(jnp.int32, (TH, TH), 0)
    c = lax.broadcasted_iota(jnp.int32, (TH, TH), 1)
    tri = ((r // HG_BLOCK) == (c // HG_BLOCK)) & (r >= c)
    cum = _chunk_cumsum(g)
    nch = TH // HG_BLOCK
    total = _chunks(cum)[:, HG_BLOCK - 1:HG_BLOCK, :]
    lastb = jnp.broadcast_to(total, (nch, HG_BLOCK, LANE)).reshape(TH, LANE)
    e, ei, ee = jnp.exp(cum), jnp.exp(-cum), jnp.exp(lastb - cum)
    return dict(sig=sig, f=f, kk=kk, tri=tri, cum=cum, total=total, e=e, ei=ei, ee=ee, qd=hq * e, ki=kk * ei, ke=kk * ee)


def _chunks(v):
    return v.reshape(TH // HG_BLOCK, HG_BLOCK, v.shape[-1])


def _bmm_nt(a, b):
    return lax.dot_general(a, b, (((2,), (2,)), ((0,), (0,))), preferred_element_type=F32)


def _bmm_nn(a, b):
    return lax.dot_general(a, b, (((2,), (1,)), ((0,), (0,))), preferred_element_type=F32)


def _bmm_tn(a, b):
    return lax.dot_general(a, b, (((1,), (1,)), ((0,), (0,))), preferred_element_type=F32)


def _pair_masks():
    lane = lax.broadcasted_iota(jnp.int32, (TH, LANE), 1)
    kr = lax.broadcasted_iota(jnp.int32, (LANE, LANE), 0)
    kc = lax.broadcasted_iota(jnp.int32, (LANE, LANE), 1)
    return lane < 64, (kr // 64) == (kc // 64)


def _hgrn_fwd(proj, lbl):
    s = proj.shape[0]
    nch = TH // HG_BLOCK

    def body(hq_ref, hf_ref, hi_ref, lbl_ref, o_ref, st_ref, st):
        @pl.when(pl.program_id(1) == 0)
        def _():
            st[...] = jnp.zeros_like(st)

        m0, bd = _pair_masks()
        for u in range(HG_PAIRS):
            lanes = slice(u * LANE, (u + 1) * LANE)
            lb = _lower_bound(lbl_ref[:, lanes])
            gt = _hgrn_gates(hq_ref[:, lanes], hf_ref[:, lanes], lb)
            v_b = hi_ref[:, lanes].astype(BF16)
            qd, ki_b, ke_b = gt["qd"], gt["ki"].astype(BF16), gt["ke"].astype(BF16)
            qd_b = qd.astype(BF16)
            o = jnp.zeros((TH, LANE), F32)
            for hh in range(2):
                mh = m0 if hh == 0 else jnp.logical_not(m0)
                a = jnp.where(gt["tri"], _mm_nt(jnp.where(mh, qd, 0.0).astype(BF16), ki_b), 0.0)
                o = jnp.where(mh, _mm(a.astype(BF16), v_b), o)
            upd = _bmm_tn(_chunks(v_b), _chunks(ke_b))
            decay = jnp.exp(gt["total"])
            cur, entering = st[u], []
            for n in range(nch):
                entering.append(cur)
                cur = decay[n] * cur + jnp.where(bd, upd[n], 0.0)
            st[u] = cur
            entering = jnp.stack(entering)
            st_ref[u] = entering
            o_ref[:, lanes] = o + _bmm_nt(_chunks(qd_b), entering.astype(BF16)).reshape(TH, LANE)

    wide = HG_PAIRS * LANE
    col = lambda base: pl.BlockSpec((TH, wide), lambda p, i: (i, base // wide + p))
    return pl.pallas_call(
        body, name="hgrn_fwd", grid=(NH // 2 // HG_PAIRS, s // TH),
        in_specs=[col(P_HQ), col(P_HF), col(P_HI), pl.BlockSpec((2, wide), lambda p, i: (0, p))],
        out_specs=[pl.BlockSpec((TH, wide), lambda p, i: (i, p)),
                   pl.BlockSpec((HG_PAIRS, nch, LANE, LANE), lambda p, i: (p, i, 0, 0))],
        out_shape=[jax.ShapeDtypeStruct((s, 512), F32), jax.ShapeDtypeStruct((NH // 2, s // HG_BLOCK, LANE, LANE), F32)],
        scratch_shapes=[pltpu.VMEM((HG_PAIRS, LANE, LANE), F32)],
        compiler_params=_params(("parallel", "arbitrary")),
    )(proj, proj, proj, lbl)


def _group_sum(v):
    low = lax.broadcasted_iota(jnp.int32, (v.shape[0], LANE), 1) < V_DIM
    blocks = []
    for b in range(v.shape[1] // LANE):
        blk = v[:, b * LANE:(b + 1) * LANE]
        s_low = jnp.sum(jnp.where(low, blk, 0.0), axis=-1, keepdims=True)
        s_high = jnp.sum(jnp.where(low, 0.0, blk), axis=-1, keepdims=True)
        blocks.append(jnp.where(low, s_low, s_high))
    return jnp.concatenate(blocks, axis=1)


def _dsilu(z, sg):
    return sg * (1.0 + z * (1.0 - sg))


def _mid(proj, attn, o_raw, x, tgt, g_hg, b_gate, g_post, wa, wb, w_out):
    s = x.shape[0]

    def body(attn_ref, ga_ref, o_ref, gb_ref, mg_ref, x_ref, t_ref, ghg_ref, bg_ref, gp_ref, wa_ref, wb_ref, wo_ref,
             loss_ref, dout_ref, dattn_ref, dga_ref, dor_ref, dgb_ref, dmg_ref, dwo_ref, dwa_ref, dwb_ref, dgp_ref, dbg_ref, dghg_ref):
        @pl.when(pl.program_id(0) == 0)
        def _():
            for rf in (loss_ref, dwo_ref, dwa_ref, dwb_ref, dgp_ref, dbg_ref, dghg_ref):
                rf[...] = jnp.zeros_like(rf)

        attn, za, orw, zb = attn_ref[...], ga_ref[...], o_ref[...], gb_ref[...]
        ghg, gp = ghg_ref[...], gp_ref[...]
        sga, sgb = _sigmoid(za), _sigmoid(zb)
        sa, sb = za * sga, zb * sgb
        ga = attn * sa
        rh = lax.rsqrt(_group_sum(orw * orw) * (1.0 / V_DIM) + EPS)
        on = (orw * rh) * ghg
        gb = on * sb
        ga_b, gb_b = ga.astype(BF16), gb.astype(BF16)
        ya = _mm(ga_b, wa_ref[...])
        yb = _mm(gb_b, wb_ref[...])
        gates = _sigmoid(mg_ref[...] + bg_ref[...])
        g0, g1 = gates[:, :D], gates[:, D:]
        m_b = (g0 * ya + g1 * yb).astype(BF16)
        y = _mm(m_b, wo_ref[...])
        ry = lax.rsqrt(jnp.mean(y * y, axis=-1, keepdims=True) + EPS)
        out = x_ref[...] + (y * ry) * gp
        err = out - t_ref[...]
        loss_ref[...] += 0.5 * jnp.sum(jnp.mean(err * err, axis=-1, keepdims=True), axis=0, keepdims=True)
        dout = err * (1.0 / D)
        dout_ref[...] = dout
        dgp_ref[...] += jnp.sum(dout * (y * ry), axis=0, keepdims=True)
        dgy = dout * gp
        dy = ry * dgy - y * (ry * ry * ry) * jnp.mean(y * dgy, axis=-1, keepdims=True)
        dy_b = dy.astype(BF16)
        dwo_ref[...] += _mm_tn(m_b, dy_b)
        dm = _mm_nt(dy_b, wo_ref[...])
        dya, dyb = dm * g0, dm * g1
        dg0, dg1 = dm * ya, dm * yb
        dmg = jnp.concatenate([dg0 * g0 * (1.0 - g0), dg1 * g1 * (1.0 - g1)], axis=1)
        dmg_ref[...] = dmg.astype(BF16)
        dbg_ref[...] += jnp.sum(dmg, axis=0, keepdims=True)
        dya_b, dyb_b = dya.astype(BF16), dyb.astype(BF16)
        dwa_ref[...] += _mm_tn(ga_b, dya_b)
        dwb_ref[...] += _mm_tn(gb_b, dyb_b)
        dga = _mm_nt(dya_b, wa_ref[...])
        dgb = _mm_nt(dyb_b, wb_ref[...])
        dattn_ref[...] = dga * sa
        dga_ref[...] = (dga * attn * _dsilu(za, sga)).astype(BF16)
        dgb_ref[...] = (dgb * on * _dsilu(zb, sgb)).astype(BF16)
        don = dgb * sb
        dghg_ref[...] += jnp.sum(don * (orw * rh), axis=0, keepdims=True)
        dgo = don * ghg
        dor_ref[...] = rh * dgo - orw * (rh * rh * rh) * (_group_sum(orw * dgo) * (1.0 / V_DIM))

    row = lambda w, j=0: pl.BlockSpec((TM_MID, w), lambda i: (i, j))
    full = lambda a: pl.BlockSpec(a.shape, lambda i: (0,) * a.ndim)
    acc = lambda shape: pl.BlockSpec(shape, lambda i: (0, 0))
    sds = jax.ShapeDtypeStruct
    return pl.pallas_call(
        body, name="mid", grid=(s // TM_MID,),
        in_specs=[row(512), row(512, P_GA // 512), row(512), row(512, P_GB // 512), row(2048, P_MERGE // 2048), row(D), row(D),
                  full(g_hg), full(b_gate), full(g_post), full(wa), full(wb), full(w_out)],
        out_specs=[acc((1, 1)), row(D), row(512), row(512), row(512), row(512), row(2048),
                   acc((D, D)), acc((512, D)), acc((512, D)), acc((1, D)), acc((1, 2048)), acc((1, 512))],
        out_shape=[sds((1, 1), F32), sds((s, D), F32), sds((s, 512), F32), sds((s, 512), BF16), sds((s, 512), F32), sds((s, 512), BF16),
                   sds((s, 2048), BF16), sds((D, D), F32), sds((512, D), F32), sds((512, D), F32), sds((1, D), F32),
                   sds((1, 2048), F32), sds((1, 512), F32)],
        compiler_params=_params(("arbitrary",)),
    )(attn, proj, o_raw, proj, proj, x, tgt, g_hg, b_gate, g_post, wa, wb, w_out)


def _attn_bwd(q, k, vv, attn, dattn, lse, token):
    s = q.shape[1]
    nt = s // TQ
    scale = 1.0 / math.sqrt(QK_NOPE + QK_ROPE)

    def body(q_ref, k_ref, v_ref, o_ref, do_ref, lse_ref, token_ref, dq_ref, dk_ref, dv_ref, do_s, delta_s):
        j = pl.program_id(1)

        @pl.when(j == 0)
        def _():
            dq_ref[...] = jnp.zeros_like(dq_ref)
            lane = lax.broadcasted_iota(jnp.int32, (TQ, LANE), 1)

            @pl.loop(0, nt)
            def _(i):
                rows = pl.ds(pl.multiple_of(i * TQ, TQ), TQ)
                do, o = do_ref[rows, :], o_ref[rows, :]
                for hh in range(2):
                    doh = jnp.where((lane < 64) if hh == 0 else (lane >= 64), do, 0.0)
                    do_s[hh, rows, :] = doh.astype(BF16)
                    delta_s[hh, rows, :] = jnp.broadcast_to(jnp.sum(doh * o, axis=-1, keepdims=True), (TQ, LANE))

        kjs, vjs = (k_ref[0], k_ref[1]), (v_ref[0], v_ref[1])

        def tile(hh, start, size, kj, vj, diag):
            rows = pl.ds(pl.multiple_of(start, size), size)
            wide = lambda a: jnp.concatenate([a] * (kj.shape[0] // LANE), axis=1)
            qi, do_b = q_ref[hh, rows, :], do_s[hh, rows, :]
            p = jnp.exp2(_mm_nt(qi, kj) * QK_SCALE2 - wide(lse_ref[hh, rows, :]))
            if diag:
                p = jnp.where(_diag_visible(size), p, 0.0)
            dv = _mm_tn(do_b, p.astype(BF16))
            ds_b = (p * (_mm_nt(do_b, vj) - wide(delta_s[hh, rows, :]))).astype(BF16)
            dk = _mm_tn(qi, ds_b)
            dq_ref[hh, rows, :] += _mm(ds_b, kj)
            return dk, dv

        def step(i, carry):
            new = [tile(hh, i * TQ, TQ, kjs[hh], vjs[hh], False) for hh in range(2)]
            return tuple((carry[hh][0] + new[hh][0], carry[hh][1] + new[hh][1]) for hh in range(2))

        def diagonal(hh):
            k0, k1, v0, v1 = kjs[hh][:HQ], kjs[hh][HQ:], vjs[hh][:HQ], vjs[hh][HQ:]
            a = tile(hh, j * TQ, HQ, k0, v0, True)
            b = tile(hh, j * TQ + HQ, HQ, k0, v0, False)
            c = tile(hh, j * TQ + HQ, HQ, k1, v1, True)
            return jnp.concatenate([a[0] + b[0], c[0]], axis=1), jnp.concatenate([a[1] + b[1], c[1]], axis=1)

        carry = lax.fori_loop(j + 1, nt, step, (diagonal(0), diagonal(1)))
        for hh in range(2):
            dk_ref[hh] = carry[hh][0].T * scale
            dv_ref[hh] = carry[hh][1].T

        @pl.when(j == nt - 1)
        def _():
            dq_ref[...] = dq_ref[...] * scale

    whole = pl.BlockSpec((2, s, LANE), lambda p, j: (p, 0, 0))
    tile_spec = pl.BlockSpec((2, TQ, LANE), lambda p, j: (p, j, 0))
    cols = pl.BlockSpec((s, LANE), lambda p, j: (0, p))
    hs = jax.ShapeDtypeStruct((NH, s, LANE), F32)
    return pl.pallas_call(
        body, name="attn_bwd", grid=(NH // 2, nt),
        in_specs=[whole, tile_spec, tile_spec, cols, cols, whole, pl.BlockSpec((8, LANE), lambda p, j: (0, 0))],
        out_specs=[whole, tile_spec, tile_spec],
        out_shape=[hs, hs, hs],
        scratch_shapes=[pltpu.VMEM((2, s, LANE), BF16), pltpu.VMEM((2, s, LANE), F32)],
        compiler_params=_params(("parallel", "arbitrary")),
    )(q, k, vv, attn, dattn, lse, token)


def _hgrn_bwd(proj, lbl, states, do_raw):
    s = proj.shape[0]
    nt = s // TH
    nch = TH // HG_BLOCK

    def body(hq_ref, hf_ref, hi_ref, lbl_ref, st_ref, do_ref, dh_ref, dlbl_ref, dst, dlb):
        step = pl.program_id(1)

        @pl.when(step == 0)
        def _():
            dst[...] = jnp.zeros_like(dst)
            dlb[...] = jnp.zeros_like(dlb)

        m0, bd = _pair_masks()
        for u in range(HG_PAIRS):
            lanes = slice(u * LANE, (u + 1) * LANE)
            lb = _lower_bound(lbl_ref[:, lanes])
            gt = _hgrn_gates(hq_ref[:, lanes], hf_ref[:, lanes], lb)
            do = do_ref[:, lanes]
            qd, ki, ke = gt["qd"], gt["ki"], gt["ke"]
            v_b, do_b = hi_ref[:, lanes].astype(BF16), do.astype(BF16)
            qd_b, ki_b, ke_b = qd.astype(BF16), ki.astype(BF16), ke.astype(BF16)
            dv = jnp.zeros((TH, LANE), F32)
            dqd = jnp.zeros((TH, LANE), F32)
            dki = jnp.zeros((TH, LANE), F32)
            for hh in range(2):
                mh = m0 if hh == 0 else jnp.logical_not(m0)
                a_b = jnp.where(gt["tri"], _mm_nt(jnp.where(mh, qd, 0.0).astype(BF16), ki_b), 0.0).astype(BF16)
                doh_b = jnp.where(mh, do, 0.0).astype(BF16)
                da_b = jnp.where(gt["tri"], _mm_nt(doh_b, v_b), 0.0).astype(BF16)
                dv = dv + _mm_tn(a_b, doh_b)
                dqd = jnp.where(mh, _mm(da_b, ki_b), dqd)
                dki = jnp.where(mh, _mm_tn(da_b, qd_b), dki)
            fed = _bmm_tn(_chunks(do_b), _chunks(qd_b))
            decay = jnp.exp(gt["total"])
            ds, leaving = dst[u], [None] * nch
            for n in reversed(range(nch)):
                leaving[n] = ds
                ds = decay[n] * ds + jnp.where(bd, fed[n], 0.0)
            dst[u] = ds
            leaving = jnp.stack(leaving)
            entering = st_ref[u]
            leaving_b = leaving.astype(BF16)
            dke3 = _bmm_nn(_chunks(v_b), leaving_b)
            dv = dv + _bmm_nt(_chunks(ke_b), leaving_b).reshape(TH, LANE)
            dqd = dqd + _bmm_nn(_chunks(do_b), entering.astype(BF16)).reshape(TH, LANE)
            dke = dke3.reshape(TH, LANE)
            dlast = (jnp.sum(dke3 * _chunks(ke), axis=1, keepdims=True)
                     + jnp.sum(leaving * entering, axis=1, keepdims=True) * decay)
            dk = dki * gt["ei"] + dke * gt["ee"]
            dcum = dqd * qd - dki * ki - dke * ke
            dg = _chunk_cumsum(dcum, reverse=True) + jnp.broadcast_to(dlast, (nch, HG_BLOCK, LANE)).reshape(TH, LANE)
            sig = gt["sig"]
            df = dg / gt["f"] - dk
            dlb[:, lanes] += jnp.sum(df * (1.0 - sig), axis=0, keepdims=True)
            dh_ref[0, :, lanes] = (dqd * gt["e"]).astype(BF16)
            dh_ref[1, :, lanes] = ((df * (1.0 - lb)) * sig * (1.0 - sig)).astype(BF16)
            dh_ref[2, :, lanes] = dv.astype(BF16)

        @pl.when(step == nt - 1)
        def _():
            lb = _lower_bound(lbl_ref[...])
            da0 = dlb[...] * lb * (1.0 - lb)
            dlbl_ref[...] = jnp.concatenate([da0, -da0], axis=0)

    wide = HG_PAIRS * LANE
    col = lambda base: pl.BlockSpec((TH, wide), lambda p, i: (nt - 1 - i, base // wide + p))
    tile = pl.BlockSpec((TH, wide), lambda p, i: (nt - 1 - i, p))
    sds = jax.ShapeDtypeStruct
    return pl.pallas_call(
        body, name="hgrn_bwd", grid=(NH // 2 // HG_PAIRS, nt),
        in_specs=[col(P_HQ), col(P_HF), col(P_HI), pl.BlockSpec((2, wide), lambda p, i: (0, p)),
                  pl.BlockSpec((HG_PAIRS, nch, LANE, LANE), lambda p, i: (p, nt - 1 - i, 0, 0)), tile],
        out_specs=[pl.BlockSpec((3, TH, wide), lambda p, i: (0, nt - 1 - i, p)), pl.BlockSpec((2, wide), lambda p, i: (0, p))],
        out_shape=[sds((3, s, 512), BF16), sds((2, 512), F32)],
        scratch_shapes=[pltpu.VMEM((HG_PAIRS, LANE, LANE), F32), pltpu.VMEM((1, wide), F32)],
        compiler_params=_params(("parallel", "arbitrary")),
    )(proj, proj, proj, lbl, states, do_raw)


def _norm_rows_bwd(v, r, g, dn):
    dgv = dn * g
    return r * dgv - v * (r * r * r) * jnp.mean(v * dgv, axis=-1, keepdims=True)


def _qkv_bwd(proj, dq, dk, dvv, g_q, g_kv, w_uq_p, w_k_p, w_v_p, rc, rs1, rs2):
    s = proj.shape[0]

    def body(cq_ref, ckv_ref, dq_ref, dk_ref, dv_ref, gq_ref, gkv_ref, wq_ref, wk_ref, wv_ref, c_ref, s1_ref, s2_ref,
             dcq_ref, dckv_ref, dkpe_ref, dwq_ref, dwk_ref, dwv_ref, dgq_ref, dgkv_ref):
        @pl.when(pl.program_id(0) == 0)
        def _():
            for rf in (dwq_ref, dwk_ref, dwv_ref, dgq_ref, dgkv_ref):
                rf[...] = jnp.zeros_like(rf)

        c, s1, s2 = c_ref[...], s1_ref[...], s2_ref[...]
        cq, ckv = cq_ref[...], ckv_ref[...]
        gq, gkv = gq_ref[...], gkv_ref[...]
        cqn, rq = _norm_rows(cq, gq)
        ckvn, rkv = _norm_rows(ckv, gkv)
        cqn_b, ckvn_b = cqn.astype(BF16), ckvn.astype(BF16)
        dqf = jnp.concatenate([_rope_t(dq_ref[h], c, s1, s2) for h in range(NH)], axis=1).astype(BF16)
        dkf = jnp.concatenate([dk_ref[h] for h in range(NH)], axis=1).astype(BF16)
        dvf = jnp.concatenate([dv_ref[h] for h in range(NH)], axis=1).astype(BF16)
        dkpe = dk_ref[0]
        for h in range(1, NH):
            dkpe = dkpe + dk_ref[h]
        lane = lax.broadcasted_iota(jnp.int32, (TM, LANE), 1)
        dkpe = jnp.where((lane >= QK_NOPE) & (lane < QK_NOPE + QK_ROPE), dkpe, 0.0)
        dkpe_ref[...] = _rope_t(dkpe, c, s1, s2).astype(BF16)
        dwq_ref[...] += _mm_tn(cqn_b, dqf)
        dwk_ref[...] += _mm_tn(ckvn_b, dkf)
        dwv_ref[...] += _mm_tn(ckvn_b, dvf)
        dcqn = _mm_nt(dqf, wq_ref[...])
        dckvn = _mm_nt(dkf, wk_ref[...]) + _mm_nt(dvf, wv_ref[...])
        dgq_ref[...] += jnp.sum(dcqn * (cq * rq), axis=0, keepdims=True)
        dgkv_ref[...] += jnp.sum(dckvn * (ckv * rkv), axis=0, keepdims=True)
        dcq_ref[...] = _norm_rows_bwd(cq, rq, gq, dcqn).astype(BF16)
        dckv_ref[...] = _norm_rows_bwd(ckv, rkv, gkv, dckvn).astype(BF16)

    row = lambda w, j=0: pl.BlockSpec((TM, w), lambda i: (i, j))
    full = lambda a: pl.BlockSpec(a.shape, lambda i: (0,) * a.ndim)
    acc = lambda shape: pl.BlockSpec(shape, lambda i: (0, 0))
    heads = pl.BlockSpec((NH, TM, LANE), lambda i: (0, i, 0))
    sds = jax.ShapeDtypeStruct
    return pl.pallas_call(
        body, name="qkv_bwd", grid=(s // TM,),
        in_specs=[row(Q_LORA, P_CQ // Q_LORA), row(KV_LORA, P_CKV // KV_LORA), heads, heads, heads,
                  full(g_q), full(g_kv), full(w_uq_p), full(w_k_p), full(w_v_p), row(LANE), row(LANE), row(LANE)],
        out_specs=[row(Q_LORA), row(KV_LORA), row(LANE), acc((Q_LORA, D)), acc((KV_LORA, D)), acc((KV_LORA, D)),
                   acc((1, Q_LORA)), acc((1, KV_LORA))],
        out_shape=[sds((s, Q_LORA), BF16), sds((s, KV_LORA), BF16), sds((s, LANE), BF16), sds((Q_LORA, D), F32),
                   sds((KV_LORA, D), F32), sds((KV_LORA, D), F32), sds((1, Q_LORA), F32), sds((1, KV_LORA), F32)],
        compiler_params=_params(("arbitrary",)),
    )(proj, proj, dq, dk, dvv, g_q, g_kv, w_uq_p, w_k_p, w_v_p, rc, rs1, rs2)


def _front_bwd(x, dout, dmg, dga, dh3, dgb, dcq, dckv, dkpe, g_pre, w_in_pt, token):
    s = x.shape[0]

    def body(x_ref, do_ref, dmg_ref, dga_ref, dh3_ref, dgb_ref, dcq_ref, dckv_ref, dkpe_ref, g_ref, w_ref, token_ref, gx_ref, dg_ref):
        @pl.when(pl.program_id(0) == 0)
        def _():
            dg_ref[...] = jnp.zeros_like(dg_ref)

        xv, g = x_ref[...], g_ref[...]
        _, r = _norm_rows(xv, g)
        pieces = ((dmg_ref[...], P_MERGE), (dga_ref[...], P_GA), (dh3_ref[0], P_HQ), (dh3_ref[1], P_HF), (dh3_ref[2], P_HI),
                  (dgb_ref[...], P_GB), (dcq_ref[...], P_CQ), (dckv_ref[...], P_CKV), (dkpe_ref[...], P_KPE))
        dh = jnp.zeros((TM, D), F32)
        for piece, off in pieces:
            dh = dh + _mm(piece, w_ref[off:off + piece.shape[1], :])
        dg_ref[...] += jnp.sum(dh * (xv * r), axis=0, keepdims=True)
        gx_ref[...] = do_ref[...] + _norm_rows_bwd(xv, r, g, dh)

    row = lambda w: pl.BlockSpec((TM, w), lambda i: (i, 0))
    full = lambda a: pl.BlockSpec(a.shape, lambda i: (0,) * a.ndim)
    sds = jax.ShapeDtypeStruct
    return pl.pallas_call(
        body, name="front_bwd", grid=(s // TM,),
        in_specs=[row(D), row(D), row(2048), row(512), pl.BlockSpec((3, TM, 512), lambda i: (0, i, 0)), row(512), row(Q_LORA),
                  row(KV_LORA), row(LANE), full(g_pre), full(w_in_pt), pl.BlockSpec(memory_space=pl.ANY)],
        out_specs=[row(D), pl.BlockSpec((1, D), lambda i: (0, 0))],
        out_shape=[sds((s, D), F32), sds((1, D), F32)],
        compiler_params=_params(("arbitrary",)),
    )(x, dout, dmg, dga, dh3, dgb, dcq, dckv, dkpe, g_pre, w_in_pt, token)


TK_GRAD = 1024


def _win_grad(h, pieces, name):
    s = h.shape[0]
    n = len(pieces)

    def body(h_ref, *refs):
        @pl.when(pl.program_id(0) == 0)
        def _():
            for o_ref in refs[n:]:
                o_ref[...] = jnp.zeros_like(o_ref)

        hv = h_ref[...]
        for d_ref, o_ref in zip(refs[:n], refs[n:]):
            if len(d_ref.shape) == 3:
                for k in range(d_ref.shape[0]):
                    o_ref[k] += _mm_tn(d_ref[k], hv)
            else:
                o_ref[...] += _mm_tn(d_ref[...], hv)

    def in_spec(p):
        if p.ndim == 3:
            return pl.BlockSpec((p.shape[0], TK_GRAD, p.shape[2]), lambda kk: (0, kk, 0))
        return pl.BlockSpec((TK_GRAD, p.shape[1]), lambda kk: (kk, 0))

    out_shapes = [(p.shape[0], p.shape[2], D) if p.ndim == 3 else (p.shape[1], D) for p in pieces]
    return pl.pallas_call(
        body, name=name, grid=(s // TK_GRAD,),
        in_specs=[pl.BlockSpec((TK_GRAD, D), lambda kk: (kk, 0))] + [in_spec(p) for p in pieces],
        out_specs=[pl.BlockSpec(sh, lambda kk, nd=len(sh): (0,) * nd) for sh in out_shapes],
        out_shape=[jax.ShapeDtypeStruct(sh, F32) for sh in out_shapes],
        compiler_params=_params(("arbitrary",)),
    )(h, *pieces)


def _pad_win_t(w_in_t):
    z = lambda n: jnp.zeros((n, w_in_t.shape[1]), w_in_t.dtype)
    sl = lambda o, n: w_in_t[o:o + n]
    return jnp.concatenate([sl(O_MERGE, 2048), sl(O_GA, 512), sl(O_HQ, 512), sl(O_HF, 512), sl(O_HI, 512), sl(O_GB, 512),
                            sl(O_CQ, Q_LORA), sl(O_CKV, KV_LORA), z(64), sl(O_KPE, QK_ROPE), z(32)], axis=0)


def _pad_wuq(w_uq):
    w = w_uq.reshape(Q_LORA, NH, QK_NOPE + QK_ROPE)
    return jnp.pad(w, ((0, 0), (0, 0), (0, LANE - QK_NOPE - QK_ROPE))).reshape(Q_LORA, NH * LANE)


def _unpad_wuq(g):
    return g.reshape(Q_LORA, NH, LANE)[:, :, :QK_NOPE + QK_ROPE].reshape(Q_LORA, NH * (QK_NOPE + QK_ROPE))


def _pad_wukv(w_ukv):
    w = w_ukv.reshape(KV_LORA, NH, QK_NOPE + V_DIM)
    w_k = jnp.pad(w[:, :, :QK_NOPE], ((0, 0), (0, 0), (0, LANE - QK_NOPE))).reshape(KV_LORA, NH * LANE)
    wv = w[:, :, QK_NOPE:].reshape(KV_LORA, NH // 2, 2, 1, V_DIM)
    eye = jnp.eye(2, dtype=w.dtype).reshape(1, 1, 2, 2, 1)
    return w_k, (wv * eye).reshape(KV_LORA, NH * LANE)


def _unpad_wukv(gk, gv):
    gk = gk.reshape(KV_LORA, NH, LANE)[:, :, :QK_NOPE]
    gv = gv.reshape(KV_LORA, NH // 2, 2, 2, V_DIM)
    gv = jnp.stack([gv[:, :, 0, 0], gv[:, :, 1, 1]], axis=2).reshape(KV_LORA, NH, V_DIM)
    return jnp.concatenate([gk, gv], axis=-1).reshape(KV_LORA, NH * (QK_NOPE + V_DIM))


def _local_step(x, tgt, g_pre, w_in_t, b_gate, g_q, g_kv, lb_logits, g_hgrn, g_post, weights, exchange=None):
    s = x.shape[0]
    w_in_p = _pad_win_t(w_in_t)
    rc, rs1, rs2 = _rope_tables(s)
    g_hg = jnp.tile(g_hgrn, (1, NH))

    proj, h = _front_fwd(x, g_pre, w_in_p, weights.tokens)
    w_uq, w_ukv = weights.qkv(h)
    w_uq_p = _pad_wuq(w_uq)
    w_k_p, w_v_p = _pad_wukv(w_ukv)
    q, k, vv = _qkv_fwd(proj, g_q, g_kv, w_uq_p, w_k_p, w_v_p, rc, rs1, rs2)
    attn, lse = _attn_fwd(q, k, vv)
    o_raw, states = _hgrn_fwd(proj, lb_logits)
    wa, wb, w_out = weights.mid(o_raw)
    (loss, dout, dattn, dga, dor, dgb, dmg, d_wout, d_wa, d_wb, d_gpost, d_bgate, d_ghg) = _mid(
        proj, attn, o_raw, x, tgt, g_hg, b_gate, g_post, wa, wb, w_out)
    w_mg, w_ga, w_gb = _win_grad(h, [dmg, dga, dgb], "win_grad_mid")
    dh3, d_lbl = _hgrn_bwd(proj, lb_logits, states, dor)
    (w_h3,) = _win_grad(h, [dh3], "win_grad_hgrn")
    d_win_rest = jnp.concatenate([w_ga, w_h3[0], w_h3[1], w_h3[2], w_gb, w_mg], axis=0)
    early = dict(w_in_rest=d_win_rest, w_branch_a=d_wa, w_branch_b=d_wb, w_out=d_wout)
    token = exchange.start_early(early) if exchange else jnp.zeros((8, LANE), F32)
    dq, dk, dvv = _attn_bwd(q, k, vv, attn, dattn, lse, token)
    dcq, dckv, dkpe, d_wuq_p, d_wk_p, d_wv_p, d_gq, d_gkv = _qkv_bwd(proj, dq, dk, dvv, g_q, g_kv, w_uq_p, w_k_p, w_v_p, rc, rs1, rs2)
    w_cq, w_ckv, w_kpe = _win_grad(h, [dcq, dckv, dkpe], "win_grad_qkv")
    d_win_qkv = jnp.concatenate([w_cq, w_ckv, w_kpe[64:64 + QK_ROPE]], axis=0)
    late = dict(w_in_qkv=d_win_qkv, w_uq=_unpad_wuq(d_wuq_p), w_ukv=_unpad_wukv(d_wk_p, d_wv_p))
    token = exchange.start_late(late) if exchange else jnp.zeros((8, LANE), F32)
    grad_x, d_gpre = _front_bwd(x, dout, dmg, dga, dh3, dgb, dcq, dckv, dkpe, g_pre, w_in_p, token)
    vec_grads = dict(g_pre=d_gpre, b_gate=d_bgate, g_q=d_gq, g_kv=d_gkv, lb_logits=d_lbl, g_hgrn=d_ghg, g_post=d_gpost)
    return loss, grad_x, dict(early, **late), vec_grads


SHARD_SHAPES = (("w_in", (1416, 1024)), ("w_uq", (192, 768)), ("w_ukv", (256, 256)), ("w_branch_a", (512, 256)),
                ("w_branch_b", (512, 256)), ("w_out", (256, 1024)))
BIG = tuple(n for n, _ in SHARD_SHAPES)
ROW_SHARDED = ("w_in", "w_uq", "w_out")
GATHER_SPLIT_AXIS = dict(w_in=1, w_uq=0, w_ukv=0, w_branch_a=0, w_branch_b=0, w_out=0)
N_CHIPS = 4
QKV_ROWS = Q_LORA + KV_LORA + QK_ROPE
W_IN_FORWARD_CUT = 704


def _to_block(name, a):
    return a[0].T if name == "w_in" else a[0]


def _from_block(name, a):
    return a.T[None] if name == "w_in" else a[None]
VEC_ROWS = (("g_pre", 0, 1024), ("b_gate", 1, 2048), ("g_q", 2, 768), ("g_kv", 3, 256), ("g_hgrn", 6, 64), ("g_post", 7, 1024))
VEC_LB_ROW = 4
VEC_SHAPE = (8, 2048)


def _split_by_chip(name, g):
    a, b = dict(SHARD_SHAPES)[name]
    return g.reshape(N_CHIPS, a, b) if name in ROW_SHARDED else g.reshape(a, N_CHIPS, b).transpose(1, 0, 2)


def _join_chips(name, w):
    a, b = dict(SHARD_SHAPES)[name]
    return w.reshape(N_CHIPS * a, b) if name in ROW_SHARDED else w.transpose(1, 0, 2).reshape(a, N_CHIPS * b)


MESH = pl.DeviceIdType.MESH
HBM = pl.BlockSpec(memory_space=pltpu.HBM)


def _mesh_place():
    x, y, c = lax.axis_index("x"), lax.axis_index("y"), lax.axis_index("c")
    return x, y, c, 2 * x + y, [(1 - x, y), (x, 1 - y), (1 - x, 1 - y)]


def _remote(src, dst, send_sems, recv_sems, k, to):
    return pltpu.make_async_remote_copy(src_ref=src, dst_ref=dst, send_sem=send_sems.at[k], recv_sem=recv_sems.at[k],
                                        device_id=to, device_id_type=MESH)


def _gather_w_in(shard):
    a, b = shard.shape
    cut = W_IN_FORWARD_CUT

    def body(src, out, ici_send, ici_recv, d2d_send, d2d_recv, local_sem):
        x, y, c = lax.axis_index("x"), lax.axis_index("y"), lax.axis_index("c")
        me, xn, yn, dg = 2 * x + y, 2 * (1 - x) + y, 2 * x + (1 - y), 2 * (1 - x) + (1 - y)
        to_x, to_y, sibling = (1 - x, y, c), (x, 1 - y, c), (x, y, 1 - c)
        first, rest = pl.ds(0, cut), pl.ds(cut, a - cut)

        whole = lambda ref, which: ref.at[:, pl.ds(pl.multiple_of(which * (b // 2), b // 2), b // 2)]
        own = pltpu.make_async_copy(src, out.at[me], local_sem)
        own.start()
        sends = [_remote(whole(src, c), whole(out.at[me], c), ici_send, ici_recv, 0, to_x),
                 _remote(whole(src, c), whole(out.at[me], c), ici_send, ici_recv, 1, to_y)]
        for cp in sends:
            cp.start()

        def landed(slot, rows, k, d2d_k, src_dev):
            piece = whole(out.at[slot], c) if rows is None else out.at[slot].at[rows, pl.ds(pl.multiple_of(c * (b // 2), b // 2), b // 2)]
            _remote(piece, piece, ici_send, ici_recv, k, src_dev).wait_recv()
            cp = _remote(piece, piece, d2d_send, d2d_recv, d2d_k, sibling)
            cp.start()
            sends.append(cp)
            return piece

        def pass_on(slot, rows, k, to):
            piece = out.at[slot].at[rows, pl.ds(pl.multiple_of(c * (b // 2), b // 2), b // 2)]
            cp = _remote(piece, piece, ici_send, ici_recv, k, to)
            cp.start()
            sends.append(cp)

        landed(xn, None, 0, 0, to_x)
        pass_on(xn, first, 2, to_y)
        landed(yn, None, 1, 1, to_y)
        pass_on(yn, rest, 3, to_x)
        landed(dg, first, 2, 2, to_y)
        landed(dg, rest, 3, 3, to_x)
        other = pl.ds(pl.multiple_of((1 - c) * (b // 2), b // 2), b // 2)
        for d2d_k, (slot, rows) in enumerate(((xn, None), (yn, None), (dg, first), (dg, rest))):
            piece = out.at[slot].at[:, other] if rows is None else out.at[slot].at[rows, other]
            _remote(piece, piece, d2d_send, d2d_recv, d2d_k, sibling).wait_recv()
        for cp in sends:
            cp.wait_send()
        own.wait()

    sems = pltpu.SemaphoreType.DMA((4,))
    return pl.pallas_call(
        body, name="gather_w_in", in_specs=[HBM], out_specs=HBM,
        out_shape=jax.ShapeDtypeStruct((N_CHIPS, a, b), shard.dtype),
        scratch_shapes=[sems, sems, sems, sems, pltpu.SemaphoreType.DMA],
        compiler_params=pltpu.CompilerParams(has_side_effects=True),
    )(shard)


def _gather_weights(shards, split_axes):
    n = len(shards)

    def body(*refs):
        srcs, outs = refs[:n], refs[n:2 * n]
        ici_send, ici_recv, d2d_send, d2d_recv, local_sems = refs[2 * n:]
        x, y, c, me, chips = _mesh_place()
        sibling = (x, y, 1 - c)

        def half(ref, k, which):
            size = shards[k].shape[split_axes[k]] // 2
            part = pl.ds(pl.multiple_of(which * size, size), size)
            return ref.at[part] if split_axes[k] == 0 else ref.at[:, part]

        own = [pltpu.make_async_copy(srcs[k], outs[k].at[me], local_sems.at[k]) for k in range(n)]
        for cp in own:
            cp.start()
        started = []
        for k in range(n):
            for j, (px, py) in enumerate(chips):
                cp = _remote(half(srcs[k], k, c), half(outs[k].at[me], k, c), ici_send, ici_recv, 3 * k + j, (px, py, c))
                cp.start()
                started.append(cp)
        for k in range(n):
            for j, (px, py) in enumerate(chips):
                landed = half(outs[k].at[2 * px + py], k, c)
                _remote(landed, landed, ici_send, ici_recv, 3 * k + j, (px, py, c)).wait_recv()
                cp = _remote(landed, landed, d2d_send, d2d_recv, 3 * k + j, sibling)
                cp.start()
                started.append(cp)
        for k in range(n):
            for j, (px, py) in enumerate(chips):
                other = half(outs[k].at[2 * px + py], k, 1 - c)
                _remote(other, other, d2d_send, d2d_recv, 3 * k + j, sibling).wait_recv()
        for cp in started:
            cp.wait_send()
        for cp in own:
            cp.wait()

    sems = pltpu.SemaphoreType.DMA((3 * n,))
    return pl.pallas_call(
        body, name="gather_weights", in_specs=[HBM] * n, out_specs=[HBM] * n,
        out_shape=[jax.ShapeDtypeStruct((N_CHIPS,) + s.shape, s.dtype) for s in shards],
        scratch_shapes=[sems, sems, sems, sems, pltpu.SemaphoreType.DMA((n,))],
        compiler_params=pltpu.CompilerParams(has_side_effects=True),
    )(*shards)


def _sibling_exchange(srcs, name, after=None):
    n = len(srcs)
    extra = [] if after is None else [after]

    def body(*refs):
        src_refs, outs = refs[:n], refs[n + len(extra):2 * n + len(extra)]
        send_sems, recv_sems = refs[2 * n + len(extra):]
        sibling = (lax.axis_index("x"), lax.axis_index("y"), 1 - lax.axis_index("c"))
        copies = [_remote(src_refs[k], outs[k], send_sems, recv_sems, k, sibling) for k in range(n)]
        for cp in copies:
            cp.start()
        for cp in copies:
            cp.wait()

    sems = pltpu.SemaphoreType.DMA((n,))
    return pl.pallas_call(
        body, name=name, in_specs=[HBM] * n + [pl.BlockSpec(memory_space=pl.ANY)] * len(extra), out_specs=[HBM] * n,
        out_shape=[jax.ShapeDtypeStruct(s.shape, s.dtype) for s in srcs],
        scratch_shapes=[sems, sems],
        compiler_params=pltpu.CompilerParams(has_side_effects=True),
    )(*srcs, *extra)


SEM = pl.BlockSpec(memory_space=pltpu.SEMAPHORE)
DATAFLOW = pltpu.SideEffectType.DATAFLOW_SIDE_EFFECTING


def _exchange_copies(srcs, to_first, src_refs, land_refs, send_sems, recv_sems):
    x, y, c, me, chips = _mesh_place()
    n = len(srcs)
    sends, recvs = [], []
    for k in range(n):
        if k in to_first:
            base = 3 * n + 4 * to_first.index(k)
            sends.append((me != 0, pltpu.make_async_remote_copy(
                src_ref=src_refs[k], dst_ref=land_refs[k].at[me], send_sem=send_sems.at[base], recv_sem=recv_sems.at[base + me],
                device_id=(0, 0, c), device_id_type=MESH)))
            for s in range(1, N_CHIPS):
                recvs.append((me == 0, pltpu.make_async_remote_copy(
                    src_ref=src_refs[k], dst_ref=land_refs[k].at[s], send_sem=send_sems.at[base], recv_sem=recv_sems.at[base + s],
                    device_id=(s // 2, s % 2, c), device_id_type=MESH)))
        else:
            slab = (lambda t, k=k: src_refs[k]) if srcs[k].ndim == 2 else (lambda t, k=k: src_refs[k].at[t])
            for j, (px, py) in enumerate(chips):
                sends.append((None, _remote(slab(2 * px + py), land_refs[k].at[me], send_sems, recv_sems, 3 * k + j, (px, py, c))))
                recvs.append((None, _remote(slab(me), land_refs[k].at[2 * px + py], send_sems, recv_sems, 3 * k + j, (px, py, c))))
    return sends, recvs


def _when(pred, fn):
    if pred is None:
        fn()
    else:
        pl.when(pred)(fn)


def _exchange_start(srcs, to_first, name, after=None):
    n = len(srcs)
    n_sems = 3 * n + 4 * len(to_first)
    lands = [lax.empty((N_CHIPS,) + s.shape[-2:], s.dtype) for s in srcs]
    extra = [] if after is None else [after]

    def body(*refs):
        src_refs, land_refs = refs[:n], refs[n:2 * n]
        send_sems, recv_sems, token = refs[2 * n + len(extra)], refs[2 * n + len(extra) + 1], refs[-1]
        sends, _ = _exchange_copies(srcs, to_first, src_refs, land_refs, send_sems, recv_sems)
        for pred, cp in sends:
            _when(pred, cp.start)
        token[...] = jnp.zeros_like(token)

    hbm = lambda a: pltpu.HBM(a.shape, a.dtype)
    res = pl.pallas_call(
        body, name=name,
        out_shape=[pltpu.SemaphoreType.DMA((n_sems,)), pltpu.SemaphoreType.DMA((n_sems,))] + [hbm(a) for a in srcs + lands]
        + [jax.ShapeDtypeStruct((8, LANE), F32)],
        in_specs=[HBM] * (2 * n) + [pl.BlockSpec(memory_space=pl.ANY)] * len(extra),
        out_specs=[SEM, SEM] + [HBM] * (2 * n) + [pl.BlockSpec(memory_space=pltpu.VMEM)],
        input_output_aliases={i: 2 + i for i in range(2 * n)},
        compiler_params=pltpu.CompilerParams(has_side_effects=DATAFLOW),
    )(*[pltpu.with_memory_space_constraint(a, pltpu.HBM) for a in srcs + lands], *extra)
    return res[:-1], res[-1]


def _exchange_wait(srcs, to_first, started, after, name):
    n = len(srcs)
    send_sems, recv_sems, thru = started[0], started[1], started[2:]

    def body(*refs):
        src_refs, land_refs, send_ref, recv_ref = refs[:n], refs[n:2 * n], refs[2 * n], refs[2 * n + 1]
        sends, recvs = _exchange_copies(srcs, to_first, src_refs, land_refs, send_ref, recv_ref)
        for pred, cp in sends:
            _when(pred, cp.wait_send)
        for pred, cp in recvs:
            _when(pred, cp.wait_recv)

    res = pl.pallas_call(
        body, name=name, out_shape=[pltpu.HBM(a.shape, a.dtype) for a in thru],
        in_specs=[HBM] * (2 * n) + [SEM, SEM, pl.BlockSpec(memory_space=pl.ANY)], out_specs=[HBM] * (2 * n),
        input_output_aliases={i: i for i in range(2 * n)},
        compiler_params=pltpu.CompilerParams(has_side_effects=DATAFLOW),
    )(*thru, send_sems, recv_sems, after)
    return res[n:]


ROW_TILE = 256
COL_TILE = 256


def _block_tiling(a, b):
    if a <= ROW_TILE or a % ROW_TILE == 0:
        ta = min(a, ROW_TILE)
        return a // ta, (ta, b), lambda i: (i, 0)
    return b // COL_TILE, (a, COL_TILE), lambda i: (0, i)


def _sum_landed(land, own, name, first_land=None, first_own=None):
    _, a, b = land.shape
    steps, tile, at = _block_tiling(a, b)
    extra = first_land is not None

    def body(*refs):
        p_ref, own_ref, o_ref = refs[0], refs[1], refs[-1]
        me = 2 * lax.axis_index("x") + lax.axis_index("y")
        own = own_ref[...].astype(F32)
        slot = lambda t: jnp.where(me == t, own, p_ref[t].astype(F32))
        o_ref[...] = ((slot(0) + slot(1)) + slot(2)) + slot(3)
        if extra:
            fp_ref, fo_ref = refs[2], refs[3]
            r = fo_ref.shape[0]

            @pl.when(me == 0)
            def _():
                f = lambda t: fp_ref[t].astype(F32)
                rows = pl.ds(pl.multiple_of(lax.axis_index("c") * r, 8), r)
                o_ref[rows, :] += ((fo_ref[...].astype(F32) + f(1)) + f(2)) + f(3)

    in_specs = [pl.BlockSpec((N_CHIPS,) + tile, lambda i: (0,) + at(i)), pl.BlockSpec(tile, at)]
    args = [land, own]
    if extra:
        r = first_own.shape[0]
        assert tile[0] == a, "the extra rows need whole columns in a step"
        in_specs += [pl.BlockSpec((N_CHIPS, r, tile[1]), lambda i: (0,) + at(i)), pl.BlockSpec((r, tile[1]), at)]
        args += [first_land, first_own]
    return pl.pallas_call(
        body, name=name, grid=(steps,), in_specs=in_specs, out_specs=pl.BlockSpec(tile, at),
        out_shape=jax.ShapeDtypeStruct((a, b), F32), compiler_params=_params(("parallel",)),
    )(*args)


def _add_cast(a, b, name):
    def body(a_ref, b_ref, o_ref):
        o_ref[...] = (a_ref[...] + b_ref[...]).astype(BF16)

    return pl.pallas_call(body, name=name, out_shape=jax.ShapeDtypeStruct(a.shape, BF16),
                          compiler_params=_params(()))(a, b)


class _LaterWeights:
    QKV = ("w_uq", "w_ukv")
    MID = ("w_branch_a", "w_branch_b", "w_out")

    def __init__(self, blocks, after):
        self.blocks = blocks
        self.qkv_started, t1 = _exchange_start([blocks[n] for n in self.QKV], (), "weights_qkv_start", after)
        self.mid_started, t2 = _exchange_start([blocks[n] for n in self.MID], (), "weights_mid_start", after)
        self.tokens = [t1, t2]

    def _whole(self, names, started, after, name):
        landed = _exchange_wait([self.blocks[n] for n in names], (), started, after, name)
        me = 2 * lax.axis_index("x") + lax.axis_index("y")
        return [_join_chips(n, lax.dynamic_update_index_in_dim(land, self.blocks[n], me, 0)) for n, land in zip(names, landed)]

    def qkv(self, after):
        return self._whole(self.QKV, self.qkv_started, after, "weights_qkv_wait")

    def mid(self, after):
        return self._whole(self.MID, self.mid_started, after, "weights_mid_wait")


class _GradExchange:
    EARLY = ("w_in", "w_branch_a", "w_branch_b", "w_out")
    LATE = ("w_uq", "w_ukv")

    def __init__(self, state):
        self.state = state
        self.outs = {}

    @staticmethod
    def _own(slabs):
        return lax.dynamic_index_in_dim(slabs, 2 * lax.axis_index("x") + lax.axis_index("y"), axis=0, keepdims=False)

    def start_early(self, g):
        full = jnp.concatenate([jnp.zeros((QKV_ROWS, D), F32), g["w_in_rest"]], axis=0)
        g = dict(g, w_in=full)
        self.early = [_split_by_chip(n, g[n]).astype(BF16) for n in self.EARLY]
        self.early_started, token = _exchange_start(self.early, (), "grads_early_start")
        return token

    def start_late(self, g):
        self.early_landed = _exchange_wait(self.early, (), self.early_started, g["w_uq"], "grads_early_wait")
        half = QKV_ROWS // 2
        c = lax.axis_index("c")
        mine = lax.dynamic_slice_in_dim(g["w_in_qkv"], c * half, half, axis=0)
        (theirs,) = _sibling_exchange([lax.dynamic_slice_in_dim(g["w_in_qkv"], (1 - c) * half, half, axis=0)], "sibling_qkv_rows")
        self.late = [_split_by_chip(n, g[n]).astype(BF16) for n in self.LATE] + [_add_cast(mine, theirs, "add_qkv_rows")]
        self.late_started, token = _exchange_start(self.late, (2,), "grads_late_start")
        names = self.EARLY[1:]
        mine = [_sum_landed(land, self._own(slabs), "sum_" + n) for n, slabs, land in list(zip(self.EARLY, self.early, self.early_landed))[1:]]
        theirs = _sibling_exchange(mine, "sibling_early", after=token)
        for n, a, b in zip(names, mine, theirs):
            self.outs[n] = _adamw(a, b, *self.state[n], "adamw_" + n)
        return self.outs[names[-1]][0]

    def finish(self, after):
        late_landed = _exchange_wait(self.late, (2,), self.late_started, after, "grads_late_wait")
        sums = {"w_in": _sum_landed(self.early_landed[0], self._own(self.early[0]), "sum_w_in",
                                    first_land=late_landed[2], first_own=self.late[2])}
        for n, slabs, land in zip(self.LATE, self.late, late_landed):
            sums[n] = _sum_landed(land, self._own(slabs), "sum_" + n)
        return sums


def _adamw_math(g, w, m, v):
    nm = ADAM_B1 * m + (1.0 - ADAM_B1) * g
    nv = ADAM_B2 * v + (1.0 - ADAM_B2) * (g * g)
    m_hat = nm / (1.0 - ADAM_B1 ** ADAM_STEP)
    v_hat = nv / (1.0 - ADAM_B2 ** ADAM_STEP)
    return -ADAM_LR * (m_hat / (jnp.sqrt(v_hat) + ADAM_EPS) + ADAM_WD * w), nm, nv


def _adamw(p_mine, p_sibling, w, m, v, name):
    a, b = p_mine.shape
    steps, tile, at = _block_tiling(a, b)

    def body(a_ref, b_ref, w_ref, m_ref, v_ref, g_ref, d_ref, nm_ref, nv_ref):
        g = a_ref[...] + b_ref[...]
        g_ref[...] = g
        d_ref[...], nm_ref[...], nv_ref[...] = _adamw_math(g, w_ref[...], m_ref[...], v_ref[...])

    spec = pl.BlockSpec(tile, at)
    sds = jax.ShapeDtypeStruct((a, b), F32)
    return pl.pallas_call(
        body, name=name, grid=(steps,), in_specs=[spec] * 5, out_specs=[spec] * 4, out_shape=[sds] * 4,
        compiler_params=_params(("parallel",)),
    )(p_mine, p_sibling, w, m, v)


LOSS_AT = (2, 1024)


def _vec_pack(vg, loss):
    names = [n for n, _, _ in VEC_ROWS]

    def body(*refs):
        o_ref = refs[-1]
        lb_ref, loss_ref = refs[len(names)], refs[len(names) + 1]
        o_ref[...] = jnp.zeros_like(o_ref)
        o_ref[LOSS_AT[0]:LOSS_AT[0] + 1, LOSS_AT[1]:LOSS_AT[1] + LANE] = jnp.broadcast_to(loss_ref[...], (1, LANE))
        for (name, row, size), ref in zip(VEC_ROWS, refs):
            if name == "g_hgrn":
                r = lax.broadcasted_iota(jnp.int32, (NH * V_DIM, LANE), 0)
                c = lax.broadcasted_iota(jnp.int32, (NH * V_DIM, LANE), 1)
                fold = ((r % V_DIM) == c).astype(F32)
                o_ref[row:row + 1, 0:LANE] = jnp.dot(ref[...], fold, precision=HIGHEST, preferred_element_type=F32)
            else:
                o_ref[row:row + 1, 0:size] = ref[...]
        o_ref[VEC_LB_ROW:VEC_LB_ROW + 2, 0:512] = lb_ref[...]

    return pl.pallas_call(body, name="vec_pack", out_shape=jax.ShapeDtypeStruct(VEC_SHAPE, F32))(
        *[vg[n] for n in names], vg["lb_logits"], loss)


def _adamw_vec(p_mine, p_sibling, w, m, v):
    names = [n for n, _, _ in VEC_ROWS] + ["lb_logits"]
    k = len(names)

    def body(a_ref, b_ref, *refs):
        ins, outs = refs[:3 * k], refs[3 * k:]
        at = (slice(LOSS_AT[0], LOSS_AT[0] + 1), slice(LOSS_AT[1], LOSS_AT[1] + LANE))
        outs[-1][...] = a_ref[at] + b_ref[at]
        for i, name in enumerate(names):
            if name == "lb_logits":
                rows, cols = slice(VEC_LB_ROW, VEC_LB_ROW + 2), slice(0, 512)
            else:
                _, row, size = VEC_ROWS[i]
                rows, cols = slice(row, row + 1), slice(0, size)
            g = a_ref[rows, cols] + b_ref[rows, cols]
            d, nm, nv = _adamw_math(g, ins[i][...], ins[k + i][...], ins[2 * k + i][...])
            for o_ref, val in zip(outs[4 * i:4 * i + 4], (g, d, nm, nv)):
                o_ref[...] = val

    shapes = [jax.ShapeDtypeStruct(w[n].shape, F32) for n in names for _ in range(4)] + [jax.ShapeDtypeStruct((1, LANE), F32)]
    res = pl.pallas_call(body, name="adamw_vec", out_shape=shapes)(
        p_mine, p_sibling, *[w[n] for n in names], *[m[n] for n in names], *[v[n] for n in names])
    return [{n: res[4 * i + j] for i, n in enumerate(names)} for j in range(4)], res[-1]


WEIGHTS = ("g_pre", "w_in", "b_gate", "g_q", "w_uq", "g_kv", "w_ukv", "lb_logits", "g_hgrn", "w_branch_a", "w_branch_b", "w_out", "g_post")


def kernel(x, g_pre, w_in, b_gate, g_q, w_uq, g_kv, w_ukv, lb_logits, g_hgrn, w_branch_a, w_branch_b, w_out, g_post, loss_target, m_g_pre, m_w_in, m_b_gate, m_g_q, m_w_uq, m_g_kv, m_w_ukv, m_lb_logits, m_g_hgrn, m_w_branch_a, m_w_branch_b, m_w_out, m_g_post, v_g_pre, v_w_in, v_b_gate, v_g_q, v_w_uq, v_g_kv, v_w_ukv, v_lb_logits, v_g_hgrn, v_w_branch_a, v_w_branch_b, v_w_out, v_g_post):
    w = dict(g_pre=g_pre, w_in=w_in, b_gate=b_gate, g_q=g_q, w_uq=w_uq, g_kv=g_kv, w_ukv=w_ukv, lb_logits=lb_logits, g_hgrn=g_hgrn,
             w_branch_a=w_branch_a, w_branch_b=w_branch_b, w_out=w_out, g_post=g_post)
    m = dict(g_pre=m_g_pre, w_in=m_w_in, b_gate=m_b_gate, g_q=m_g_q, w_uq=m_w_uq, g_kv=m_g_kv, w_ukv=m_w_ukv, lb_logits=m_lb_logits,
             g_hgrn=m_g_hgrn, w_branch_a=m_w_branch_a, w_branch_b=m_w_branch_b, w_out=m_w_out, g_post=m_g_post)
    v = dict(g_pre=v_g_pre, w_in=v_w_in, b_gate=v_b_gate, g_q=v_g_q, w_uq=v_w_uq, g_kv=v_g_kv, w_ukv=v_w_ukv, lb_logits=v_lb_logits,
             g_hgrn=v_g_hgrn, w_branch_a=v_w_branch_a, w_branch_b=v_w_branch_b, w_out=v_w_out, g_post=v_g_post)
    blocks = {n: _to_block(n, w[n]).astype(BF16) for n in BIG}
    w_in_all = _gather_w_in(blocks["w_in"])
    weights = _LaterWeights(blocks, w_in_all)
    state = {n: [_to_block(n, t[n]) for t in (w, m, v)] for n in BIG}
    exchange = _GradExchange(state)
    loss, grad_x, _, vec_grads = _local_step(
        x[0], loss_target[0], g_pre, _join_chips("w_in", w_in_all), b_gate, g_q, g_kv, lb_logits, g_hgrn, g_post, weights, exchange)
    vec = _vec_pack(vec_grads, loss)
    vec_started, token = _exchange_start([vec], (), "vec_start")
    sums = exchange.finish(token)
    rest = tuple(sums)
    (vec_landed,) = _exchange_wait([vec], (), vec_started, sums[rest[-1]], "vec_wait")
    mine = [sums[n] for n in rest] + [_sum_landed(vec_landed, vec, "sum_vec")]
    theirs = _sibling_exchange(mine, "sibling_grads")
    done = dict(exchange.outs)
    for k, n in enumerate(rest):
        done[n] = _adamw(mine[k], theirs[k], *state[n], "adamw_" + n)
    outs = [{}, {}, {}, {}]
    for n in BIG:
        for o, val in zip(outs, done[n]):
            o[n] = _from_block(n, val)
    vec_outs, total = _adamw_vec(mine[-1], theirs[-1], w, m, v)
    for o, vals in zip(outs, vec_outs):
        o.update(vals)
    return (total[0, 0], grad_x[None], *[o[n] for o in outs for n in WEIGHTS])
```

```python
import math

import jax
import jax.numpy as jnp
from jax import lax
from jax.experimental import pallas as pl
from jax.experimental.pallas import tpu as pltpu

F32 = jnp.float32
BF16 = jnp.bfloat16
HIGHEST = lax.Precision.HIGHEST

D = 1024
NH = 8
QK_NOPE, QK_ROPE, V_DIM = 64, 32, 64
Q_LORA, KV_LORA = 768, 256
CHUNK = 64
HG_BLOCK = 32
EPS = 1e-6
LANE = 128
P_MERGE, P_GA, P_HQ, P_HF, P_HI, P_GB, P_CQ, P_CKV, P_KPE = 0, 2048, 2560, 3072, 3584, 4096, 4608, 5376, 5632
D_P = 5760
O_CQ, O_CKV, O_KPE, O_GA, O_HQ, O_HF, O_HI, O_GB, O_MERGE = 0, 768, 1024, 1056, 1568, 2080, 2592, 3104, 3616

TM = 512
TM_MID = 256
TQ = 1024
ONES_LANE = (LANE - 1, 0)
TH = 256
HG_PAIRS = 4
VMEM_LIMIT = 56 * 1024 * 1024

ADAM_LR, ADAM_B1, ADAM_B2, ADAM_EPS, ADAM_WD, ADAM_STEP = 0.001, 0.9, 0.999, 1e-08, 0.01, 10

NT_DIMS = (((1,), (1,)), ((), ()))
TN_DIMS = (((0,), (0,)), ((), ()))


def _params(sem):
    return pltpu.CompilerParams(dimension_semantics=sem, vmem_limit_bytes=VMEM_LIMIT)


def _mm(a, b):
    return jnp.dot(a, b, preferred_element_type=F32)


def _mm_nt(a, b):
    return lax.dot_general(a, b, NT_DIMS, preferred_element_type=F32)


def _mm_tn(a, b):
    return lax.dot_general(a, b, TN_DIMS, preferred_element_type=F32)


def _sigmoid(z):
    return jax.nn.sigmoid(z)


def _rope(v, c, s1, s2):
    return v * c + pltpu.roll(v, 112, 1) * s1 + pltpu.roll(v, 16, 1) * s2


def _rope_t(dy, c, s1, s2):
    return dy * c + pltpu.roll(dy * s1, 16, 1) + pltpu.roll(dy * s2, 112, 1)


def _rope_tables(s):
    inv = 10000.0 ** (-jnp.arange(0, QK_ROPE, 2, dtype=F32) / QK_ROPE)
    ang = jnp.arange(s, dtype=F32)[:, None] * inv[None, :]
    cos, sin = jnp.cos(ang), jnp.sin(ang)
    z64, z32, o64, o32 = jnp.zeros((s, 64), F32), jnp.zeros((s, 32), F32), jnp.ones((s, 64), F32), jnp.ones((s, 32), F32)
    z16 = jnp.zeros((s, 16), F32)
    c = jnp.concatenate([o64, cos, cos, o32], axis=1)
    s1 = jnp.concatenate([z64, -sin, z16, z32], axis=1)
    s2 = jnp.concatenate([z64, z16, sin, z32], axis=1)
    return c, s1, s2


def _front_fwd(x, g_pre, w_in_pt, tokens=()):
    s = x.shape[0]
    tokens = list(tokens)

    def body(x_ref, g_ref, w_ref, *refs):
        o_ref, h_ref = refs[len(tokens):]
        xv = x_ref[...]
        r = lax.rsqrt(jnp.mean(xv * xv, axis=-1, keepdims=True) + EPS)
        h = ((xv * r) * g_ref[...]).astype(BF16)
        h_ref[...] = h
        o_ref[...] = _mm_nt(h, w_ref[...])

    return pl.pallas_call(
        body, name="front_fwd", grid=(s // TM,),
        in_specs=[pl.BlockSpec((TM, D), lambda i: (i, 0)), pl.BlockSpec((1, D), lambda i: (0, 0)),
                  pl.BlockSpec((D_P, D), lambda i: (0, 0))] + [pl.BlockSpec((8, LANE), lambda i: (0, 0))] * len(tokens),
        out_specs=[pl.BlockSpec((TM, D_P), lambda i: (i, 0)), pl.BlockSpec((TM, D), lambda i: (i, 0))],
        out_shape=[jax.ShapeDtypeStruct((s, D_P), F32), jax.ShapeDtypeStruct((s, D), BF16)],
        compiler_params=_params(("parallel",)),
    )(x, g_pre, w_in_pt, *tokens)


def _norm_rows(v, g):
    r = lax.rsqrt(jnp.mean(v * v, axis=-1, keepdims=True) + EPS)
    return (v * r) * g, r


def _qkv_fwd(proj, g_q, g_kv, w_uq_p, w_k_p, w_v_p, rc, rs1, rs2):
    s = proj.shape[0]

    def body(cq_ref, ckv_ref, kpe_ref, gq_ref, gkv_ref, wq_ref, wk_ref, wv_ref, c_ref, s1_ref, s2_ref, q_ref, k_ref, v_ref):
        c, s1, s2 = c_ref[...], s1_ref[...], s2_ref[...]
        cqn, _ = _norm_rows(cq_ref[...], gq_ref[...])
        ckvn, _ = _norm_rows(ckv_ref[...], gkv_ref[...])
        ckvn = ckvn.astype(BF16)
        qf = _mm(cqn.astype(BF16), wq_ref[...])
        kf = _mm(ckvn, wk_ref[...])
        vf = _mm(ckvn, wv_ref[...])
        kpe = _rope(kpe_ref[...], c, s1, s2)
        lane = lax.broadcasted_iota(jnp.int32, (TM, LANE), 1)
        for h in range(NH):
            blk = slice(h * LANE, (h + 1) * LANE)
            q_ref[h] = _rope(qf[:, blk], c, s1, s2).astype(BF16)
            k_ref[h] = (kf[:, blk] + kpe).astype(BF16)
            v_ref[h] = jnp.where(lane == ONES_LANE[h % 2], 1.0, vf[:, blk]).astype(BF16)

    row = lambda w, j: pl.BlockSpec((TM, w), lambda i: (i, j))
    full = lambda a: pl.BlockSpec(a.shape, lambda i: (0,) * a.ndim)
    hs = jax.ShapeDtypeStruct((NH, s, LANE), BF16)
    return pl.pallas_call(
        body, name="qkv_fwd", grid=(s // TM,),
        in_specs=[row(Q_LORA, P_CQ // Q_LORA), row(KV_LORA, P_CKV // KV_LORA), row(LANE, P_KPE // LANE),
                  full(g_q), full(g_kv), full(w_uq_p), full(w_k_p), full(w_v_p), row(LANE, 0), row(LANE, 0), row(LANE, 0)],
        out_specs=[pl.BlockSpec((NH, TM, LANE), lambda i: (0, i, 0))] * 3,
        out_shape=[hs, hs, hs],
        compiler_params=_params(("parallel",)),
    )(proj, proj, proj, g_q, g_kv, w_uq_p, w_k_p, w_v_p, rc, rs1, rs2)


LOG2E = 1.4426950408889634
QK_SCALE2 = LOG2E / math.sqrt(QK_NOPE + QK_ROPE)


HQ = TQ // 2


def _diag_visible(n):
    row = lax.broadcasted_iota(jnp.int32, (n, n), 0)
    col = lax.broadcasted_iota(jnp.int32, (n, n), 1)
    return (col // CHUNK) <= (row // CHUNK)


def _attn_fwd(q, k, vv):
    s = q.shape[1]

    def body(q_ref, k_ref, v_ref, o_ref, lse_ref):
        i = pl.program_id(1)
        qs = (q_ref[0], q_ref[1])

        def tile(hh, t, carry, diag):
            m, acc = carry
            rows = pl.ds(pl.multiple_of(t * TQ, TQ), TQ)
            sc = _mm_nt(qs[hh], k_ref[hh, rows, :])
            if diag:
                sc = jnp.where(_diag_visible(TQ), sc, -jnp.inf)
            m_new = jnp.maximum(m, jnp.max(sc, axis=-1, keepdims=True))
            alpha = jnp.exp2((m - m_new) * QK_SCALE2)
            p = jnp.exp2((sc - m_new) * QK_SCALE2).astype(BF16)
            acc = alpha * acc + _mm(p, v_ref[hh, rows, :])
            return m_new, acc

        def step(t, carry):
            return tile(0, t, carry[0], False), tile(1, t, carry[1], False)

        init = (jnp.full((TQ, 1), -jnp.inf, F32), jnp.zeros((TQ, LANE), F32))
        carry = lax.fori_loop(0, i, step, (init, init))
        lane = lax.broadcasted_iota(jnp.int32, (TQ, LANE), 1)
        out = jnp.zeros((TQ, LANE), F32)
        for hh in range(2):
            m, acc = tile(hh, i, carry[hh], True)
            l = jnp.sum(jnp.where(lane == ONES_LANE[hh], acc, 0.0), axis=-1, keepdims=True)
            out = out + jnp.where((lane < V_DIM) == (hh == 0), acc, 0.0) / l
            lse_ref[hh] = jnp.broadcast_to(m * QK_SCALE2 + jnp.log(l) * LOG2E, (TQ, LANE))
        o_ref[...] = out

    return pl.pallas_call(
        body, name="attn_fwd", grid=(NH // 2, s // TQ),
        in_specs=[pl.BlockSpec((2, TQ, LANE), lambda p, i: (p, i, 0)), pl.BlockSpec((2, s, LANE), lambda p, i: (p, 0, 0)),
                  pl.BlockSpec((2, s, LANE), lambda p, i: (p, 0, 0))],
        out_specs=[pl.BlockSpec((TQ, LANE), lambda p, i: (i, p)), pl.BlockSpec((2, TQ, LANE), lambda p, i: (p, i, 0))],
        out_shape=[jax.ShapeDtypeStruct((s, NH * V_DIM), F32), jax.ShapeDtypeStruct((NH, s, LANE), F32)],
        compiler_params=_params(("parallel", "parallel")),
    )(q, k, vv)


def _lower_bound(lbl):
    a0, a1 = lbl[0:1, :], lbl[1:2, :]
    mx = jnp.maximum(a0, a1)
    e0, e1 = jnp.exp(a0 - mx), jnp.exp(a1 - mx)
    return e0 / (e0 + e1)


def _chunk_cumsum(v, reverse=False):
    pos = lax.broadcasted_iota(jnp.int32, v.shape, 0) % HG_BLOCK
    s = 1
    while s < HG_BLOCK:
        if reverse:
            v = v + jnp.where(pos < HG_BLOCK - s, pltpu.roll(v, TH - s, 0), 0.0)
        else:
            v = v + jnp.where(pos >= s, pltpu.roll(v, s, 0), 0.0)
        s *= 2
    return v


def _hgrn_gates(hq, hf, lb):
    sig = _sigmoid(hf)
    f = lb + (1.0 - lb) * sig
    g = jnp.log(f)
    kk = 1.0 - f
    r = lax.broadcasted_iota(jnp.int32, (TH, TH), 0)
    c = lax.broadcasted_iota(jnp.int32, (TH, TH), 1)
    tri = ((r // HG_BLOCK) == (c // HG_BLOCK)) & (r >= c)
    cum = _chunk_cumsum(g)
    nch = TH // HG_BLOCK
    total = _chunks(cum)[:, HG_BLOCK - 1:HG_BLOCK, :]
    lastb = jnp.broadcast_to(total, (nch, HG_BLOCK, LANE)).reshape(TH, LANE)
    e, ei, ee = jnp.exp(cum), jnp.exp(-cum), jnp.exp(lastb - cum)
    return dict(sig=sig, f=f, kk=kk, tri=tri, cum=cum, total=total, e=e, ei=ei, ee=ee, qd=hq * e, ki=kk * ei, ke=kk * ee)


def _chunks(v):
    return v.reshape(TH // HG_BLOCK, HG_BLOCK, v.shape[-1])


def _bmm_nt(a, b):
    return lax.dot_general(a, b, (((2,), (2,)), ((0,), (0,))), preferred_element_type=F32)


def _bmm_nn(a, b):
    return lax.dot_general(a, b, (((2,), (1,)), ((0,), (0,))), preferred_element_type=F32)


def _bmm_tn(a, b):
    return lax.dot_general(a, b, (((1,), (1,)), ((0,), (0,))), preferred_element_type=F32)


def _pair_masks():
    lane = lax.broadcasted_iota(jnp.int32, (TH, LANE), 1)
    kr = lax.broadcasted_iota(jnp.int32, (LANE, LANE), 0)
    kc = lax.broadcasted_iota(jnp.int32, (LANE, LANE), 1)
    return lane < 64, (kr // 64) == (kc // 64)


def _hgrn_fwd(proj, lbl):
    s = proj.shape[0]
    nch = TH // HG_BLOCK

    def body(hq_ref, hf_ref, hi_ref, lbl_ref, o_ref, st_ref, st):
        @pl.when(pl.program_id(1) == 0)
        def _():
            st[...] = jnp.zeros_like(st)

        m0, bd = _pair_masks()
        for u in range(HG_PAIRS):
            lanes = slice(u * LANE, (u + 1) * LANE)
            lb = _lower_bound(lbl_ref[:, lanes])
            gt = _hgrn_gates(hq_ref[:, lanes], hf_ref[:, lanes], lb)
            v_b = hi_ref[:, lanes].astype(BF16)
            qd, ki_b, ke_b = gt["qd"], gt["ki"].astype(BF16), gt["ke"].astype(BF16)
            qd_b = qd.astype(BF16)
            o = jnp.zeros((TH, LANE), F32)
            for hh in range(2):
                mh = m0 if hh == 0 else jnp.logical_not(m0)
                a = jnp.where(gt["tri"], _mm_nt(jnp.where(mh, qd, 0.0).astype(BF16), ki_b), 0.0)
                o = jnp.where(mh, _mm(a.astype(BF16), v_b), o)
            upd = _bmm_tn(_chunks(v_b), _chunks(ke_b))
            decay = jnp.exp(gt["total"])
            cur, entering = st[u], []
            for n in range(nch):
                entering.append(cur)
                cur = decay[n] * cur + jnp.where(bd, upd[n], 0.0)
            st[u] = cur
            entering = jnp.stack(entering)
            st_ref[u] = entering
            o_ref[:, lanes] = o + _bmm_nt(_chunks(qd_b), entering.astype(BF16)).reshape(TH, LANE)

    wide = HG_PAIRS * LANE
    col = lambda base: pl.BlockSpec((TH, wide), lambda p, i: (i, base // wide + p))
    return pl.pallas_call(
        body, name="hgrn_fwd", grid=(NH // 2 // HG_PAIRS, s // TH),
        in_specs=[col(P_HQ), col(P_HF), col(P_HI), pl.BlockSpec((2, wide), lambda p, i: (0, p))],
        out_specs=[pl.BlockSpec((TH, wide), lambda p, i: (i, p)),
                   pl.BlockSpec((HG_PAIRS, nch, LANE, LANE), lambda p, i: (p, i, 0, 0))],
        out_shape=[jax.ShapeDtypeStruct((s, 512), F32), jax.ShapeDtypeStruct((NH // 2, s // HG_BLOCK, LANE, LANE), F32)],
        scratch_shapes=[pltpu.VMEM((HG_PAIRS, LANE, LANE), F32)],
        compiler_params=_params(("parallel", "arbitrary")),
    )(proj, proj, proj, lbl)


def _group_sum(v):
    low = lax.broadcasted_iota(jnp.int32, (v.shape[0], LANE), 1) < V_DIM
    blocks = []
    for b in range(v.shape[1] // LANE):
        blk = v[:, b * LANE:(b + 1) * LANE]
        s_low = jnp.sum(jnp.where(low, blk, 0.0), axis=-1, keepdims=True)
        s_high = jnp.sum(jnp.where(low, 0.0, blk), axis=-1, keepdims=True)
        blocks.append(jnp.where(low, s_low, s_high))
    return jnp.concatenate(blocks, axis=1)


def _dsilu(z, sg):
    return sg * (1.0 + z * (1.0 - sg))


def _mid(proj, attn, o_raw, x, tgt, g_hg, b_gate, g_post, wa, wb, w_out):
    s = x.shape[0]

    def body(attn_ref, ga_ref, o_ref, gb_ref, mg_ref, x_ref, t_ref, ghg_ref, bg_ref, gp_ref, wa_ref, wb_ref, wo_ref,
             loss_ref, dout_ref, dattn_ref, dga_ref, dor_ref, dgb_ref, dmg_ref, dwo_ref, dwa_ref, dwb_ref, dgp_ref, dbg_ref, dghg_ref):
        @pl.when(pl.program_id(0) == 0)
        def _():
            for rf in (loss_ref, dwo_ref, dwa_ref, dwb_ref, dgp_ref, dbg_ref, dghg_ref):
                rf[...] = jnp.zeros_like(rf)

        attn, za, orw, zb = attn_ref[...], ga_ref[...], o_ref[...], gb_ref[...]
        ghg, gp = ghg_ref[...], gp_ref[...]
        sga, sgb = _sigmoid(za), _sigmoid(zb)
        sa, sb = za * sga, zb * sgb
        ga = attn * sa
        rh = lax.rsqrt(_group_sum(orw * orw) * (1.0 / V_DIM) + EPS)
        on = (orw * rh) * ghg
        gb = on * sb
        ga_b, gb_b = ga.astype(BF16), gb.astype(BF16)
        ya = _mm(ga_b, wa_ref[...])
        yb = _mm(gb_b, wb_ref[...])
        gates = _sigmoid(mg_ref[...] + bg_ref[...])
        g0, g1 = gates[:, :D], gates[:, D:]
        m_b = (g0 * ya + g1 * yb).astype(BF16)
        y = _mm(m_b, wo_ref[...])
        ry = lax.rsqrt(jnp.mean(y * y, axis=-1, keepdims=True) + EPS)
        out = x_ref[...] + (y * ry) * gp
        err = out - t_ref[...]
        loss_ref[...] += 0.5 * jnp.sum(jnp.mean(err * err, axis=-1, keepdims=True), axis=0, keepdims=True)
        dout = err * (1.0 / D)
        dout_ref[...] = dout
        dgp_ref[...] += jnp.sum(dout * (y * ry), axis=0, keepdims=True)
        dgy = dout * gp
        dy = ry * dgy - y * (ry * ry * ry) * jnp.mean(y * dgy, axis=-1, keepdims=True)
        dy_b = dy.astype(BF16)
        dwo_ref[...] += _mm_tn(m_b, dy_b)
        dm = _mm_nt(dy_b, wo_ref[...])
        dya, dyb = dm * g0, dm * g1
        dg0, dg1 = dm * ya, dm * yb
        dmg = jnp.concatenate([dg0 * g0 * (1.0 - g0), dg1 * g1 * (1.0 - g1)], axis=1)
        dmg_ref[...] = dmg.astype(BF16)
        dbg_ref[...] += jnp.sum(dmg, axis=0, keepdims=True)
        dya_b, dyb_b = dya.astype(BF16), dyb.astype(BF16)
        dwa_ref[...] += _mm_tn(ga_b, dya_b)
        dwb_ref[...] += _mm_tn(gb_b, dyb_b)
        dga = _mm_nt(dya_b, wa_ref[...])
        dgb = _mm_nt(dyb_b, wb_ref[...])
        dattn_ref[...] = dga * sa
        dga_ref[...] = (dga * attn * _dsilu(za, sga)).astype(BF16)
        dgb_ref[...] = (dgb * on * _dsilu(zb, sgb)).astype(BF16)
        don = dgb * sb
        dghg_ref[...] += jnp.sum(don * (orw * rh), axis=0, keepdims=True)
        dgo = don * ghg
        dor_ref[...] = rh * dgo - orw * (rh * rh * rh) * (_group_sum(orw * dgo) * (1.0 / V_DIM))

    row = lambda w, j=0: pl.BlockSpec((TM_MID, w), lambda i: (i, j))
    full = lambda a: pl.BlockSpec(a.shape, lambda i: (0,) * a.ndim)
    acc = lambda shape: pl.BlockSpec(shape, lambda i: (0, 0))
    sds = jax.ShapeDtypeStruct
    return pl.pallas_call(
        body, name="mid", grid=(s // TM_MID,),
        in_specs=[row(512), row(512, P_GA // 512), row(512), row(512, P_GB // 512), row(2048, P_MERGE // 2048), row(D), row(D),
                  full(g_hg), full(b_gate), full(g_post), full(wa), full(wb), full(w_out)],
        out_specs=[acc((1, 1)), row(D), row(512), row(512), row(512), row(512), row(2048),
                   acc((D, D)), acc((512, D)), acc((512, D)), acc((1, D)), acc((1, 2048)), acc((1, 512))],
        out_shape=[sds((1, 1), F32), sds((s, D), F32), sds((s, 512), F32), sds((s, 512), BF16), sds((s, 512), F32), sds((s, 512), BF16),
                   sds((s, 2048), BF16), sds((D, D), F32), sds((512, D), F32), sds((512, D), F32), sds((1, D), F32),
                   sds((1, 2048), F32), sds((1, 512), F32)],
        compiler_params=_params(("arbitrary",)),
    )(attn, proj, o_raw, proj, proj, x, tgt, g_hg, b_gate, g_post, wa, wb, w_out)


def _attn_bwd(q, k, vv, attn, dattn, lse, token):
    s = q.shape[1]
    nt = s // TQ
    scale = 1.0 / math.sqrt(QK_NOPE + QK_ROPE)

    def body(q_ref, k_ref, v_ref, o_ref, do_ref, lse_ref, token_ref, dq_ref, dk_ref, dv_ref, do_s, delta_s):
        j = pl.program_id(1)

        @pl.when(j == 0)
        def _():
            dq_ref[...] = jnp.zeros_like(dq_ref)
            lane = lax.broadcasted_iota(jnp.int32, (TQ, LANE), 1)

            @pl.loop(0, nt)
            def _(i):
                rows = pl.ds(pl.multiple_of(i * TQ, TQ), TQ)
                do, o = do_ref[rows, :], o_ref[rows, :]
                for hh in range(2):
                    doh = jnp.where((lane < 64) if hh == 0 else (lane >= 64), do, 0.0)
                    do_s[hh, rows, :] = doh.astype(BF16)
                    delta_s[hh, rows, :] = jnp.broadcast_to(jnp.sum(doh * o, axis=-1, keepdims=True), (TQ, LANE))

        kjs, vjs = (k_ref[0], k_ref[1]), (v_ref[0], v_ref[1])

        def tile(hh, start, size, kj, vj, diag):
            rows = pl.ds(pl.multiple_of(start, size), size)
            wide = lambda a: jnp.concatenate([a] * (kj.shape[0] // LANE), axis=1)
            qi, do_b = q_ref[hh, rows, :], do_s[hh, rows, :]
            p = jnp.exp2(_mm_nt(qi, kj) * QK_SCALE2 - wide(lse_ref[hh, rows, :]))
            if diag:
                p = jnp.where(_diag_visible(size), p, 0.0)
            dv = _mm_tn(do_b, p.astype(BF16))
            ds_b = (p * (_mm_nt(do_b, vj) - wide(delta_s[hh, rows, :]))).astype(BF16)
            dk = _mm_tn(qi, ds_b)
            dq_ref[hh, rows, :] += _mm(ds_b, kj)
            return dk, dv

        def step(i, carry):
            new = [tile(hh, i * TQ, TQ, kjs[hh], vjs[hh], False) for hh in range(2)]
            return tuple((carry[hh][0] + new[hh][0], carry[hh][1] + new[hh][1]) for hh in range(2))

        def diagonal(hh):
            k0, k1, v0, v1 = kjs[hh][:HQ], kjs[hh][HQ:], vjs[hh][:HQ], vjs[hh][HQ:]
            a = tile(hh, j * TQ, HQ, k0, v0, True)
            b = tile(hh, j * TQ + HQ, HQ, k0, v0, False)
            c = tile(hh, j * TQ + HQ, HQ, k1, v1, True)
            return jnp.concatenate([a[0] + b[0], c[0]], axis=1), jnp.concatenate([a[1] + b[1], c[1]], axis=1)

        carry = lax.fori_loop(j + 1, nt, step, (diagonal(0), diagonal(1)))
        for hh in range(2):
            dk_ref[hh] = carry[hh][0].T * scale
            dv_ref[hh] = carry[hh][1].T

        @pl.when(j == nt - 1)
        def _():
            dq_ref[...] = dq_ref[...] * scale

    whole = pl.BlockSpec((2, s, LANE), lambda p, j: (p, 0, 0))
    tile_spec = pl.BlockSpec((2, TQ, LANE), lambda p, j: (p, j, 0))
    cols = pl.BlockSpec((s, LANE), lambda p, j: (0, p))
    hs = jax.ShapeDtypeStruct((NH, s, LANE), F32)
    return pl.pallas_call(
        body, name="attn_bwd", grid=(NH // 2, nt),
        in_specs=[whole, tile_spec, tile_spec, cols, cols, whole, pl.BlockSpec((8, LANE), lambda p, j: (0, 0))],
        out_specs=[whole, tile_spec, tile_spec],
        out_shape=[hs, hs, hs],
        scratch_shapes=[pltpu.VMEM((2, s, LANE), BF16), pltpu.VMEM((2, s, LANE), F32)],
        compiler_params=_params(("parallel", "arbitrary")),
    )(q, k, vv, attn, dattn, lse, token)


def _hgrn_bwd(proj, lbl, states, do_raw):
    s = proj.shape[0]
    nt = s // TH
    nch = TH // HG_BLOCK

    def body(hq_ref, hf_ref, hi_ref, lbl_ref, st_ref, do_ref, dh_ref, dlbl_ref, dst, dlb):
        step = pl.program_id(1)

        @pl.when(step == 0)
        def _():
            dst[...] = jnp.zeros_like(dst)
            dlb[...] = jnp.zeros_like(dlb)

        m0, bd = _pair_masks()
        for u in range(HG_PAIRS):
            lanes = slice(u * LANE, (u + 1) * LANE)
            lb = _lower_bound(lbl_ref[:, lanes])
            gt = _hgrn_gates(hq_ref[:, lanes], hf_ref[:, lanes], lb)
            do = do_ref[:, lanes]
            qd, ki, ke = gt["qd"], gt["ki"], gt["ke"]
            v_b, do_b = hi_ref[:, lanes].astype(BF16), do.astype(BF16)
            qd_b, ki_b, ke_b = qd.astype(BF16), ki.astype(BF16), ke.astype(BF16)
            dv = jnp.zeros((TH, LANE), F32)
            dqd = jnp.zeros((TH, LANE), F32)
            dki = jnp.zeros((TH, LANE), F32)
            for hh in range(2):
                mh = m0 if hh == 0 else jnp.logical_not(m0)
                a_b = jnp.where(gt["tri"], _mm_nt(jnp.where(mh, qd, 0.0).astype(BF16), ki_b), 0.0).astype(BF16)
                doh_b = jnp.where(mh, do, 0.0).astype(BF16)
                da_b = jnp.where(gt["tri"], _mm_nt(doh_b, v_b), 0.0).astype(BF16)
                dv = dv + _mm_tn(a_b, doh_b)
                dqd = jnp.where(mh, _mm(da_b, ki_b), dqd)
                dki = jnp.where(mh, _mm_tn(da_b, qd_b), dki)
            fed = _bmm_tn(_chunks(do_b), _chunks(qd_b))
            decay = jnp.exp(gt["total"])
            ds, leaving = dst[u], [None] * nch
            for n in reversed(range(nch)):
                leaving[n] = ds
                ds = decay[n] * ds + jnp.where(bd, fed[n], 0.0)
            dst[u] = ds
            leaving = jnp.stack(leaving)
            entering = st_ref[u]
            leaving_b = leaving.astype(BF16)
            dke3 = _bmm_nn(_chunks(v_b), leaving_b)
            dv = dv + _bmm_nt(_chunks(ke_b), leaving_b).reshape(TH, LANE)
            dqd = dqd + _bmm_nn(_chunks(do_b), entering.astype(BF16)).reshape(TH, LANE)
            dke = dke3.reshape(TH, LANE)
            dlast = (jnp.sum(dke3 * _chunks(ke), axis=1, keepdims=True)
                     + jnp.sum(leaving * entering, axis=1, keepdims=True) * decay)
            dk = dki * gt["ei"] + dke * gt["ee"]
            dcum = dqd * qd - dki * ki - dke * ke
            dg = _chunk_cumsum(dcum, reverse=True) + jnp.broadcast_to(dlast, (nch, HG_BLOCK, LANE)).reshape(TH, LANE)
            sig = gt["sig"]
            df = dg / gt["f"] - dk
            dlb[:, lanes] += jnp.sum(df * (1.0 - sig), axis=0, keepdims=True)
            dh_ref[0, :, lanes] = (dqd * gt["e"]).astype(BF16)
            dh_ref[1, :, lanes] = ((df * (1.0 - lb)) * sig * (1.0 - sig)).astype(BF16)
            dh_ref[2, :, lanes] = dv.astype(BF16)

        @pl.when(step == nt - 1)
        def _():
            lb = _lower_bound(lbl_ref[...])
            da0 = dlb[...] * lb * (1.0 - lb)
            dlbl_ref[...] = jnp.concatenate([da0, -da0], axis=0)

    wide = HG_PAIRS * LANE
    col = lambda base: pl.BlockSpec((TH, wide), lambda p, i: (nt - 1 - i, base // wide + p))
    tile = pl.BlockSpec((TH, wide), lambda p, i: (nt - 1 - i, p))
    sds = jax.ShapeDtypeStruct
    return pl.pallas_call(
        body, name="hgrn_bwd", grid=(NH // 2 // HG_PAIRS, nt),
        in_specs=[col(P_HQ), col(P_HF), col(P_HI), pl.BlockSpec((2, wide), lambda p, i: (0, p)),
                  pl.BlockSpec((HG_PAIRS, nch, LANE, LANE), lambda p, i: (p, nt - 1 - i, 0, 0)), tile],
        out_specs=[pl.BlockSpec((3, TH, wide), lambda p, i: (0, nt - 1 - i, p)), pl.BlockSpec((2, wide), lambda p, i: (0, p))],
        out_shape=[sds((3, s, 512), BF16), sds((2, 512), F32)],
        scratch_shapes=[pltpu.VMEM((HG_PAIRS, LANE, LANE), F32), pltpu.VMEM((1, wide), F32)],
        compiler_params=_params(("parallel", "arbitrary")),
    )(proj, proj, proj, lbl, states, do_raw)


def _norm_rows_bwd(v, r, g, dn):
    dgv = dn * g
    return r * dgv - v * (r * r * r) * jnp.mean(v * dgv, axis=-1, keepdims=True)


def _qkv_bwd(proj, dq, dk, dvv, g_q, g_kv, w_uq_p, w_k_p, w_v_p, rc, rs1, rs2):
    s = proj.shape[0]

    def body(cq_ref, ckv_ref, dq_ref, dk_ref, dv_ref, gq_ref, gkv_ref, wq_ref, wk_ref, wv_ref, c_ref, s1_ref, s2_ref,
             dcq_ref, dckv_ref, dkpe_ref, dwq_ref, dwk_ref, dwv_ref, dgq_ref, dgkv_ref):
        @pl.when(pl.program_id(0) == 0)
        def _():
            for rf in (dwq_ref, dwk_ref, dwv_ref, dgq_ref, dgkv_ref):
                rf[...] = jnp.zeros_like(rf)

        c, s1, s2 = c_ref[...], s1_ref[...], s2_ref[...]
        cq, ckv = cq_ref[...], ckv_ref[...]
        gq, gkv = gq_ref[...], gkv_ref[...]
        cqn, rq = _norm_rows(cq, gq)
        ckvn, rkv = _norm_rows(ckv, gkv)
        cqn_b, ckvn_b = cqn.astype(BF16), ckvn.astype(BF16)
        dqf = jnp.concatenate([_rope_t(dq_ref[h], c, s1, s2) for h in range(NH)], axis=1).astype(BF16)
        dkf = jnp.concatenate([dk_ref[h] for h in range(NH)], axis=1).astype(BF16)
        dvf = jnp.concatenate([dv_ref[h] for h in range(NH)], axis=1).astype(BF16)
        dkpe = dk_ref[0]
        for h in range(1, NH):
            dkpe = dkpe + dk_ref[h]
        lane = lax.broadcasted_iota(jnp.int32, (TM, LANE), 1)
        dkpe = jnp.where((lane >= QK_NOPE) & (lane < QK_NOPE + QK_ROPE), dkpe, 0.0)
        dkpe_ref[...] = _rope_t(dkpe, c, s1, s2).astype(BF16)
        dwq_ref[...] += _mm_tn(cqn_b, dqf)
        dwk_ref[...] += _mm_tn(ckvn_b, dkf)
        dwv_ref[...] += _mm_tn(ckvn_b, dvf)
        dcqn = _mm_nt(dqf, wq_ref[...])
        dckvn = _mm_nt(dkf, wk_ref[...]) + _mm_nt(dvf, wv_ref[...])
        dgq_ref[...] += jnp.sum(dcqn * (cq * rq), axis=0, keepdims=True)
        dgkv_ref[...] += jnp.sum(dckvn * (ckv * rkv), axis=0, keepdims=True)
        dcq_ref[...] = _norm_rows_bwd(cq, rq, gq, dcqn).astype(BF16)
        dckv_ref[...] = _norm_rows_bwd(ckv, rkv, gkv, dckvn).astype(BF16)

    row = lambda w, j=0: pl.BlockSpec((TM, w), lambda i: (i, j))
    full = lambda a: pl.BlockSpec(a.shape, lambda i: (0,) * a.ndim)
    acc = lambda shape: pl.BlockSpec(shape, lambda i: (0, 0))
    heads = pl.BlockSpec((NH, TM, LANE), lambda i: (0, i, 0))
    sds = jax.ShapeDtypeStruct
    return pl.pallas_call(
        body, name="qkv_bwd", grid=(s // TM,),
        in_specs=[row(Q_LORA, P_CQ // Q_LORA), row(KV_LORA, P_CKV // KV_LORA), heads, heads, heads,
                  full(g_q), full(g_kv), full(w_uq_p), full(w_k_p), full(w_v_p), row(LANE), row(LANE), row(LANE)],
        out_specs=[row(Q_LORA), row(KV_LORA), row(LANE), acc((Q_LORA, D)), acc((KV_LORA, D)), acc((KV_LORA, D)),
                   acc((1, Q_LORA)), acc((1, KV_LORA))],
        out_shape=[sds((s, Q_LORA), BF16), sds((s, KV_LORA), BF16), sds((s, LANE), BF16), sds((Q_LORA, D), F32),
                   sds((KV_LORA, D), F32), sds((KV_LORA, D), F32), sds((1, Q_LORA), F32), sds((1, KV_LORA), F32)],
        compiler_params=_params(("arbitrary",)),
    )(proj, proj, dq, dk, dvv, g_q, g_kv, w_uq_p, w_k_p, w_v_p, rc, rs1, rs2)


def _front_bwd(x, dout, dmg, dga, dh3, dgb, dcq, dckv, dkpe, g_pre, w_in_pt, token):
    s = x.shape[0]

    def body(x_ref, do_ref, dmg_ref, dga_ref, dh3_ref, dgb_ref, dcq_ref, dckv_ref, dkpe_ref, g_ref, w_ref, token_ref, gx_ref, dg_ref):
        @pl.when(pl.program_id(0) == 0)
        def _():
            dg_ref[...] = jnp.zeros_like(dg_ref)

        xv, g = x_ref[...], g_ref[...]
        _, r = _norm_rows(xv, g)
        pieces = ((dmg_ref[...], P_MERGE), (dga_ref[...], P_GA), (dh3_ref[0], P_HQ), (dh3_ref[1], P_HF), (dh3_ref[2], P_HI),
                  (dgb_ref[...], P_GB), (dcq_ref[...], P_CQ), (dckv_ref[...], P_CKV), (dkpe_ref[...], P_KPE))
        dh = jnp.zeros((TM, D), F32)
        for piece, off in pieces:
            dh = dh + _mm(piece, w_ref[off:off + piece.shape[1], :])
        dg_ref[...] += jnp.sum(dh * (xv * r), axis=0, keepdims=True)
        gx_ref[...] = do_ref[...] + _norm_rows_bwd(xv, r, g, dh)

    row = lambda w: pl.BlockSpec((TM, w), lambda i: (i, 0))
    full = lambda a: pl.BlockSpec(a.shape, lambda i: (0,) * a.ndim)
    sds = jax.ShapeDtypeStruct
    return pl.pallas_call(
        body, name="front_bwd", grid=(s // TM,),
        in_specs=[row(D), row(D), row(2048), row(512), pl.BlockSpec((3, TM, 512), lambda i: (0, i, 0)), row(512), row(Q_LORA),
                  row(KV_LORA), row(LANE), full(g_pre), full(w_in_pt), pl.BlockSpec(memory_space=pl.ANY)],
        out_specs=[row(D), pl.BlockSpec((1, D), lambda i: (0, 0))],
        out_shape=[sds((s, D), F32), sds((1, D), F32)],
        compiler_params=_params(("arbitrary",)),
    )(x, dout, dmg, dga, dh3, dgb, dcq, dckv, dkpe, g_pre, w_in_pt, token)


TK_GRAD = 1024


def _win_grad(h, pieces, name):
    s = h.shape[0]
    n = len(pieces)

    def body(h_ref, *refs):
        @pl.when(pl.program_id(0) == 0)
        def _():
            for o_ref in refs[n:]:
                o_ref[...] = jnp.zeros_like(o_ref)

        hv = h_ref[...]
        for d_ref, o_ref in zip(refs[:n], refs[n:]):
            if len(d_ref.shape) == 3:
                for k in range(d_ref.shape[0]):
                    o_ref[k] += _mm_tn(d_ref[k], hv)
            else:
                o_ref[...] += _mm_tn(d_ref[...], hv)

    def in_spec(p):
        if p.ndim == 3:
            return pl.BlockSpec((p.shape[0], TK_GRAD, p.shape[2]), lambda kk: (0, kk, 0))
        return pl.BlockSpec((TK_GRAD, p.shape[1]), lambda kk: (kk, 0))

    out_shapes = [(p.shape[0], p.shape[2], D) if p.ndim == 3 else (p.shape[1], D) for p in pieces]
    return pl.pallas_call(
        body, name=name, grid=(s // TK_GRAD,),
        in_specs=[pl.BlockSpec((TK_GRAD, D), lambda kk: (kk, 0))] + [in_spec(p) for p in pieces],
        out_specs=[pl.BlockSpec(sh, lambda kk, nd=len(sh): (0,) * nd) for sh in out_shapes],
        out_shape=[jax.ShapeDtypeStruct(sh, F32) for sh in out_shapes],
        compiler_params=_params(("arbitrary",)),
    )(h, *pieces)


def _pad_win_t(w_in_t):
    z = lambda n: jnp.zeros((n, w_in_t.shape[1]), w_in_t.dtype)
    sl = lambda o, n: w_in_t[o:o + n]
    return jnp.concatenate([sl(O_MERGE, 2048), sl(O_GA, 512), sl(O_HQ, 512), sl(O_HF, 512), sl(O_HI, 512), sl(O_GB, 512),
                            sl(O_CQ, Q_LORA), sl(O_CKV, KV_LORA), z(64), sl(O_KPE, QK_ROPE), z(32)], axis=0)


def _pad_wuq(w_uq):
    rows = w_uq.shape[0]
    w = w_uq.reshape(rows, NH, QK_NOPE + QK_ROPE)
    return jnp.pad(w, ((0, 0), (0, 0), (0, LANE - QK_NOPE - QK_ROPE))).reshape(rows, NH * LANE)


def _unpad_wuq(g):
    return g.reshape(Q_LORA, NH, LANE)[:, :, :QK_NOPE + QK_ROPE].reshape(Q_LORA, NH * (QK_NOPE + QK_ROPE))


def _pad_wukv(w_ukv):
    heads = w_ukv.shape[1] // (QK_NOPE + V_DIM)
    w = w_ukv.reshape(KV_LORA, heads, QK_NOPE + V_DIM)
    w_k = jnp.pad(w[:, :, :QK_NOPE], ((0, 0), (0, 0), (0, LANE - QK_NOPE))).reshape(KV_LORA, heads * LANE)
    wv = w[:, :, QK_NOPE:].reshape(KV_LORA, heads // 2, 2, 1, V_DIM)
    eye = jnp.eye(2, dtype=w.dtype).reshape(1, 1, 2, 2, 1)
    return w_k, (wv * eye).reshape(KV_LORA, heads * LANE)


def _unpad_wukv(gk, gv):
    gk = gk.reshape(KV_LORA, NH, LANE)[:, :, :QK_NOPE]
    gv = gv.reshape(KV_LORA, NH // 2, 2, 2, V_DIM)
    gv = jnp.stack([gv[:, :, 0, 0], gv[:, :, 1, 1]], axis=2).reshape(KV_LORA, NH, V_DIM)
    return jnp.concatenate([gk, gv], axis=-1).reshape(KV_LORA, NH * (QK_NOPE + V_DIM))


def _local_step(x, tgt, g_pre, w_in_t, b_gate, g_q, g_kv, lb_logits, g_hgrn, g_post, weights, exchange=None):
    s = x.shape[0]
    w_in_p = _pad_win_t(w_in_t)
    rc, rs1, rs2 = _rope_tables(s)
    g_hg = jnp.tile(g_hgrn, (1, NH))

    proj, h = _front_fwd(x, g_pre, w_in_p, weights.tokens)
    w_uq_p, w_k_p, w_v_p = weights.qkv(h)
    q, k, vv = _qkv_fwd(proj, g_q, g_kv, w_uq_p, w_k_p, w_v_p, rc, rs1, rs2)
    attn, lse = _attn_fwd(q, k, vv)
    o_raw, states = _hgrn_fwd(proj, lb_logits)
    wa, wb, w_out = weights.mid(o_raw)
    (loss, dout, dattn, dga, dor, dgb, dmg, d_wout, d_wa, d_wb, d_gpost, d_bgate, d_ghg) = _mid(
        proj, attn, o_raw, x, tgt, g_hg, b_gate, g_post, wa, wb, w_out)
    w_mg, w_ga, w_gb = _win_grad(h, [dmg, dga, dgb], "win_grad_mid")
    dh3, d_lbl = _hgrn_bwd(proj, lb_logits, states, dor)
    (w_h3,) = _win_grad(h, [dh3], "win_grad_hgrn")
    d_win_rest = jnp.concatenate([w_ga, w_h3[0], w_h3[1], w_h3[2], w_gb, w_mg], axis=0)
    early = dict(w_in_rest=d_win_rest, w_branch_a=d_wa, w_branch_b=d_wb, w_out=d_wout)
    token = exchange.start_early(early) if exchange else jnp.zeros((8, LANE), F32)
    dq, dk, dvv = _attn_bwd(q, k, vv, attn, dattn, lse, token)
    dcq, dckv, dkpe, d_wuq_p, d_wk_p, d_wv_p, d_gq, d_gkv = _qkv_bwd(proj, dq, dk, dvv, g_q, g_kv, w_uq_p, w_k_p, w_v_p, rc, rs1, rs2)
    w_cq, w_ckv, w_kpe = _win_grad(h, [dcq, dckv, dkpe], "win_grad_qkv")
    d_win_qkv = jnp.concatenate([w_cq, w_ckv, w_kpe[64:64 + QK_ROPE]], axis=0)
    late = dict(w_in_qkv=d_win_qkv, w_uq=_unpad_wuq(d_wuq_p), w_ukv=_unpad_wukv(d_wk_p, d_wv_p))
    token = exchange.start_late(late) if exchange else jnp.zeros((8, LANE), F32)
    grad_x, d_gpre = _front_bwd(x, dout, dmg, dga, dh3, dgb, dcq, dckv, dkpe, g_pre, w_in_p, token)
    vec_grads = dict(g_pre=d_gpre, b_gate=d_bgate, g_q=d_gq, g_kv=d_gkv, lb_logits=d_lbl, g_hgrn=d_ghg, g_post=d_gpost)
    return loss, grad_x, dict(early, **late), vec_grads


SHARD_SHAPES = (("w_in", (1416, 1024)), ("w_uq", (192, 768)), ("w_ukv", (256, 256)), ("w_branch_a", (512, 256)),
                ("w_branch_b", (512, 256)), ("w_out", (256, 1024)))
BIG = tuple(n for n, _ in SHARD_SHAPES)
ROW_SHARDED = ("w_in", "w_uq", "w_out")
N_CHIPS = 4
QKV_ROWS = Q_LORA + KV_LORA + QK_ROPE
W_IN_FORWARD_CUT = 704


def _to_block(name, a):
    return a[0].T if name == "w_in" else a[0]


def _from_block(name, a):
    return a.T[None] if name == "w_in" else a[None]
VEC_ROWS = (("g_pre", 0, 1024), ("b_gate", 1, 2048), ("g_q", 2, 768), ("g_kv", 3, 256), ("g_hgrn", 6, 64), ("g_post", 7, 1024))
VEC_LB_ROW = 4
VEC_SHAPE = (8, 2048)


def _split_by_chip(name, g):
    a, b = dict(SHARD_SHAPES)[name]
    return g.reshape(N_CHIPS, a, b) if name in ROW_SHARDED else g.reshape(a, N_CHIPS, b).transpose(1, 0, 2)


def _join_chips(name, w):
    a, b = dict(SHARD_SHAPES)[name]
    return w.reshape(N_CHIPS * a, b) if name in ROW_SHARDED else w.transpose(1, 0, 2).reshape(a, N_CHIPS * b)


MESH = pl.DeviceIdType.MESH
HBM = pl.BlockSpec(memory_space=pltpu.HBM)


def _mesh_place():
    x, y, c = lax.axis_index("x"), lax.axis_index("y"), lax.axis_index("c")
    return x, y, c, 2 * x + y, [(1 - x, y), (x, 1 - y), (1 - x, 1 - y)]


def _remote(src, dst, send_sems, recv_sems, k, to):
    return pltpu.make_async_remote_copy(src_ref=src, dst_ref=dst, send_sem=send_sems.at[k], recv_sem=recv_sems.at[k],
                                        device_id=to, device_id_type=MESH)


def _gather_w_in(shard):
    a, b = shard.shape
    cut = W_IN_FORWARD_CUT

    def body(src, out, ici_send, ici_recv, d2d_send, d2d_recv, local_sem):
        x, y, c = lax.axis_index("x"), lax.axis_index("y"), lax.axis_index("c")
        me, xn, yn, dg = 2 * x + y, 2 * (1 - x) + y, 2 * x + (1 - y), 2 * (1 - x) + (1 - y)
        to_x, to_y, sibling = (1 - x, y, c), (x, 1 - y, c), (x, y, 1 - c)
        first, rest = pl.ds(0, cut), pl.ds(cut, a - cut)

        whole = lambda ref, which: ref.at[:, pl.ds(pl.multiple_of(which * (b // 2), b // 2), b // 2)]
        own = pltpu.make_async_copy(src, out.at[me], local_sem)
        own.start()
        sends = [_remote(whole(src, c), whole(out.at[me], c), ici_send, ici_recv, 0, to_x),
                 _remote(whole(src, c), whole(out.at[me], c), ici_send, ici_recv, 1, to_y)]
        for cp in sends:
            cp.start()

        def landed(slot, rows, k, d2d_k, src_dev):
            piece = whole(out.at[slot], c) if rows is None else out.at[slot].at[rows, pl.ds(pl.multiple_of(c * (b // 2), b // 2), b // 2)]
            _remote(piece, piece, ici_send, ici_recv, k, src_dev).wait_recv()
            cp = _remote(piece, piece, d2d_send, d2d_recv, d2d_k, sibling)
            cp.start()
            sends.append(cp)
            return piece

        def pass_on(slot, rows, k, to):
            piece = out.at[slot].at[rows, pl.ds(pl.multiple_of(c * (b // 2), b // 2), b // 2)]
            cp = _remote(piece, piece, ici_send, ici_recv, k, to)
            cp.start()
            sends.append(cp)

        landed(xn, None, 0, 0, to_x)
        pass_on(xn, first, 2, to_y)
        landed(yn, None, 1, 1, to_y)
        pass_on(yn, rest, 3, to_x)
        landed(dg, first, 2, 2, to_y)
        landed(dg, rest, 3, 3, to_x)
        other = pl.ds(pl.multiple_of((1 - c) * (b // 2), b // 2), b // 2)
        for d2d_k, (slot, rows) in enumerate(((xn, None), (yn, None), (dg, first), (dg, rest))):
            piece = out.at[slot].at[:, other] if rows is None else out.at[slot].at[rows, other]
            _remote(piece, piece, d2d_send, d2d_recv, d2d_k, sibling).wait_recv()
        for cp in sends:
            cp.wait_send()
        own.wait()

    sems = pltpu.SemaphoreType.DMA((4,))
    return pl.pallas_call(
        body, name="gather_w_in", in_specs=[HBM], out_specs=HBM,
        out_shape=jax.ShapeDtypeStruct((N_CHIPS, a, b), shard.dtype),
        scratch_shapes=[sems, sems, sems, sems, pltpu.SemaphoreType.DMA],
        compiler_params=pltpu.CompilerParams(has_side_effects=True),
    )(shard)


def _sibling_exchange(srcs, name, after=None):
    n = len(srcs)
    extra = [] if after is None else [after]

    def body(*refs):
        src_refs, outs = refs[:n], refs[n + len(extra):2 * n + len(extra)]
        send_sems, recv_sems = refs[2 * n + len(extra):]
        sibling = (lax.axis_index("x"), lax.axis_index("y"), 1 - lax.axis_index("c"))
        copies = [_remote(src_refs[k], outs[k], send_sems, recv_sems, k, sibling) for k in range(n)]
        for cp in copies:
            cp.start()
        for cp in copies:
            cp.wait()

    sems = pltpu.SemaphoreType.DMA((n,))
    return pl.pallas_call(
        body, name=name, in_specs=[HBM] * n + [pl.BlockSpec(memory_space=pl.ANY)] * len(extra), out_specs=[HBM] * n,
        out_shape=[jax.ShapeDtypeStruct(s.shape, s.dtype) for s in srcs],
        scratch_shapes=[sems, sems],
        compiler_params=pltpu.CompilerParams(has_side_effects=True),
    )(*srcs, *extra)


SEM = pl.BlockSpec(memory_space=pltpu.SEMAPHORE)
DATAFLOW = pltpu.SideEffectType.DATAFLOW_SIDE_EFFECTING


def _exchange_copies(srcs, to_first, src_refs, land_refs, send_sems, recv_sems):
    x, y, c, me, chips = _mesh_place()
    n = len(srcs)
    sends, recvs = [], []
    for k in range(n):
        if k in to_first:
            base = 3 * n + 4 * to_first.index(k)
            sends.append((me != 0, pltpu.make_async_remote_copy(
                src_ref=src_refs[k], dst_ref=land_refs[k].at[me], send_sem=send_sems.at[base], recv_sem=recv_sems.at[base + me],
                device_id=(0, 0, c), device_id_type=MESH)))
            for s in range(1, N_CHIPS):
                recvs.append((me == 0, pltpu.make_async_remote_copy(
                    src_ref=src_refs[k], dst_ref=land_refs[k].at[s], send_sem=send_sems.at[base], recv_sem=recv_sems.at[base + s],
                    device_id=(s // 2, s % 2, c), device_id_type=MESH)))
        else:
            slab = (lambda t, k=k: src_refs[k]) if srcs[k].ndim == 2 else (lambda t, k=k: src_refs[k].at[t])
            for j, (px, py) in enumerate(chips):
                sends.append((None, _remote(slab(2 * px + py), land_refs[k].at[me], send_sems, recv_sems, 3 * k + j, (px, py, c))))
                recvs.append((None, _remote(slab(me), land_refs[k].at[2 * px + py], send_sems, recv_sems, 3 * k + j, (px, py, c))))
    return sends, recvs


def _when(pred, fn):
    if pred is None:
        fn()
    else:
        pl.when(pred)(fn)


def _exchange_start(srcs, to_first, name, after=None):
    n = len(srcs)
    n_sems = 3 * n + 4 * len(to_first)
    lands = [lax.empty((N_CHIPS,) + s.shape[-2:], s.dtype) for s in srcs]
    extra = [] if after is None else [after]

    def body(*refs):
        src_refs, land_refs = refs[:n], refs[n:2 * n]
        send_sems, recv_sems, token = refs[2 * n + len(extra)], refs[2 * n + len(extra) + 1], refs[-1]
        sends, _ = _exchange_copies(srcs, to_first, src_refs, land_refs, send_sems, recv_sems)
        for pred, cp in sends:
            _when(pred, cp.start)
        token[...] = jnp.zeros_like(token)

    hbm = lambda a: pltpu.HBM(a.shape, a.dtype)
    res = pl.pallas_call(
        body, name=name,
        out_shape=[pltpu.SemaphoreType.DMA((n_sems,)), pltpu.SemaphoreType.DMA((n_sems,))] + [hbm(a) for a in srcs + lands]
        + [jax.ShapeDtypeStruct((8, LANE), F32)],
        in_specs=[HBM] * (2 * n) + [pl.BlockSpec(memory_space=pl.ANY)] * len(extra),
        out_specs=[SEM, SEM] + [HBM] * (2 * n) + [pl.BlockSpec(memory_space=pltpu.VMEM)],
        input_output_aliases={i: 2 + i for i in range(2 * n)},
        compiler_params=pltpu.CompilerParams(has_side_effects=DATAFLOW),
    )(*[pltpu.with_memory_space_constraint(a, pltpu.HBM) for a in srcs + lands], *extra)
    return res[:-1], res[-1]


def _exchange_wait(srcs, to_first, started, after, name):
    n = len(srcs)
    send_sems, recv_sems, thru = started[0], started[1], started[2:]

    def body(*refs):
        src_refs, land_refs, send_ref, recv_ref = refs[:n], refs[n:2 * n], refs[2 * n], refs[2 * n + 1]
        sends, recvs = _exchange_copies(srcs, to_first, src_refs, land_refs, send_ref, recv_ref)
        for pred, cp in sends:
            _when(pred, cp.wait_send)
        for pred, cp in recvs:
            _when(pred, cp.wait_recv)

    res = pl.pallas_call(
        body, name=name, out_shape=[pltpu.HBM(a.shape, a.dtype) for a in thru],
        in_specs=[HBM] * (2 * n) + [SEM, SEM, pl.BlockSpec(memory_space=pl.ANY)], out_specs=[HBM] * (2 * n),
        input_output_aliases={i: i for i in range(2 * n)},
        compiler_params=pltpu.CompilerParams(has_side_effects=DATAFLOW),
    )(*thru, send_sems, recv_sems, after)
    return res[n:]


ROW_TILE = 256
COL_TILE = 256


def _block_tiling(a, b):
    if a <= ROW_TILE or a % ROW_TILE == 0:
        ta = min(a, ROW_TILE)
        return a // ta, (ta, b), lambda i: (i, 0)
    return b // COL_TILE, (a, COL_TILE), lambda i: (0, i)


def _sum_landed(land, own, name, first_land=None, first_own=None):
    _, a, b = land.shape
    steps, tile, at = _block_tiling(a, b)
    extra = first_land is not None

    def body(*refs):
        p_ref, own_ref, o_ref = refs[0], refs[1], refs[-1]
        me = 2 * lax.axis_index("x") + lax.axis_index("y")
        own = own_ref[...].astype(F32)
        slot = lambda t: jnp.where(me == t, own, p_ref[t].astype(F32))
        o_ref[...] = ((slot(0) + slot(1)) + slot(2)) + slot(3)
        if extra:
            fp_ref, fo_ref = refs[2], refs[3]
            r = fo_ref.shape[0]

            @pl.when(me == 0)
            def _():
                f = lambda t: fp_ref[t].astype(F32)
                rows = pl.ds(pl.multiple_of(lax.axis_index("c") * r, 8), r)
                o_ref[rows, :] += ((fo_ref[...].astype(F32) + f(1)) + f(2)) + f(3)

    in_specs = [pl.BlockSpec((N_CHIPS,) + tile, lambda i: (0,) + at(i)), pl.BlockSpec(tile, at)]
    args = [land, own]
    if extra:
        r = first_own.shape[0]
        assert tile[0] == a, "the extra rows need whole columns in a step"
        in_specs += [pl.BlockSpec((N_CHIPS, r, tile[1]), lambda i: (0,) + at(i)), pl.BlockSpec((r, tile[1]), at)]
        args += [first_land, first_own]
    return pl.pallas_call(
        body, name=name, grid=(steps,), in_specs=in_specs, out_specs=pl.BlockSpec(tile, at),
        out_shape=jax.ShapeDtypeStruct((a, b), F32), compiler_params=_params(("parallel",)),
    )(*args)


def _add_cast(a, b, name):
    def body(a_ref, b_ref, o_ref):
        o_ref[...] = (a_ref[...] + b_ref[...]).astype(BF16)

    return pl.pallas_call(body, name=name, out_shape=jax.ShapeDtypeStruct(a.shape, BF16),
                          compiler_params=_params(()))(a, b)


class _LaterWeights:
    MID = ("w_branch_a", "w_branch_b", "w_out")

    def __init__(self, blocks, after):
        self.qkv_blocks = [_pad_wuq(blocks["w_uq"]), *_pad_wukv(blocks["w_ukv"])]
        self.mid_blocks = [blocks[n] for n in self.MID]
        self.qkv_started, t1 = _exchange_start(self.qkv_blocks, (), "weights_qkv_start", after)
        self.mid_started, t2 = _exchange_start(self.mid_blocks, (), "weights_mid_start", after)
        self.tokens = [t1, t2]

    @staticmethod
    def _whole(blocks, rows_sharded, started, after, name):
        landed = _exchange_wait(blocks, (), started, after, name)
        me = 2 * lax.axis_index("x") + lax.axis_index("y")
        out = []
        for block, land, by_rows in zip(blocks, landed, rows_sharded):
            w = lax.dynamic_update_index_in_dim(land, block, me, 0)
            a, b = block.shape
            out.append(w.reshape(N_CHIPS * a, b) if by_rows else w.transpose(1, 0, 2).reshape(a, N_CHIPS * b))
        return out

    def qkv(self, after):
        return self._whole(self.qkv_blocks, (True, False, False), self.qkv_started, after, "weights_qkv_wait")

    def mid(self, after):
        return self._whole(self.mid_blocks, (False, False, True), self.mid_started, after, "weights_mid_wait")


class _GradExchange:
    EARLY = ("w_in", "w_branch_a", "w_branch_b", "w_out")
    LATE = ("w_uq", "w_ukv")

    def __init__(self, state):
        self.state = state
        self.outs = {}

    @staticmethod
    def _own(slabs):
        return lax.dynamic_index_in_dim(slabs, 2 * lax.axis_index("x") + lax.axis_index("y"), axis=0, keepdims=False)

    def start_early(self, g):
        full = jnp.concatenate([jnp.zeros((QKV_ROWS, D), F32), g["w_in_rest"]], axis=0)
        g = dict(g, w_in=full)
        self.early = [_split_by_chip(n, g[n]).astype(BF16) for n in self.EARLY]
        self.early_started, token = _exchange_start(self.early, (), "grads_early_start")
        return token

    def start_late(self, g):
        self.early_landed = _exchange_wait(self.early, (), self.early_started, g["w_uq"], "grads_early_wait")
        half = QKV_ROWS // 2
        c = lax.axis_index("c")
        mine = lax.dynamic_slice_in_dim(g["w_in_qkv"], c * half, half, axis=0)
        (theirs,) = _sibling_exchange([lax.dynamic_slice_in_dim(g["w_in_qkv"], (1 - c) * half, half, axis=0)], "sibling_qkv_rows")
        self.late = [_split_by_chip(n, g[n]).astype(BF16) for n in self.LATE] + [_add_cast(mine, theirs, "add_qkv_rows")]
        self.late_started, token = _exchange_start(self.late, (2,), "grads_late_start")
        names = self.EARLY[1:]
        mine = [_sum_landed(land, self._own(slabs), "sum_" + n) for n, slabs, land in list(zip(self.EARLY, self.early, self.early_landed))[1:]]
        theirs = _sibling_exchange(mine, "sibling_early", after=token)
        for n, a, b in zip(names, mine, theirs):
            self.outs[n] = _adamw(a, b, *self.state[n], "adamw_" + n)
        return self.outs[names[-1]][0]

    def finish(self, after):
        late_landed = _exchange_wait(self.late, (2,), self.late_started, after, "grads_late_wait")
        sums = {"w_in": _sum_landed(self.early_landed[0], self._own(self.early[0]), "sum_w_in",
                                    first_land=late_landed[2], first_own=self.late[2])}
        for n, slabs, land in zip(self.LATE, self.late, late_landed):
            sums[n] = _sum_landed(land, self._own(slabs), "sum_" + n)
        return sums


def _adamw_math(g, w, m, v):
    nm = ADAM_B1 * m + (1.0 - ADAM_B1) * g
    nv = ADAM_B2 * v + (1.0 - ADAM_B2) * (g * g)
    m_hat = nm / (1.0 - ADAM_B1 ** ADAM_STEP)
    v_hat = nv / (1.0 - ADAM_B2 ** ADAM_STEP)
    return -ADAM_LR * (m_hat / (jnp.sqrt(v_hat) + ADAM_EPS) + ADAM_WD * w), nm, nv


def _adamw(p_mine, p_sibling, w, m, v, name):
    a, b = p_mine.shape
    steps, tile, at = _block_tiling(a, b)

    def body(a_ref, b_ref, w_ref, m_ref, v_ref, g_ref, d_ref, nm_ref, nv_ref):
        g = a_ref[...] + b_ref[...]
        g_ref[...] = g
        d_ref[...], nm_ref[...], nv_ref[...] = _adamw_math(g, w_ref[...], m_ref[...], v_ref[...])

    spec = pl.BlockSpec(tile, at)
    sds = jax.ShapeDtypeStruct((a, b), F32)
    return pl.pallas_call(
        body, name=name, grid=(steps,), in_specs=[spec] * 5, out_specs=[spec] * 4, out_shape=[sds] * 4,
        compiler_params=_params(("parallel",)),
    )(p_mine, p_sibling, w, m, v)


LOSS_AT = (2, 1024)


def _vec_pack(vg, loss):
    names = [n for n, _, _ in VEC_ROWS]

    def body(*refs):
        o_ref = refs[-1]
        lb_ref, loss_ref = refs[len(names)], refs[len(names) + 1]
        o_ref[...] = jnp.zeros_like(o_ref)
        o_ref[LOSS_AT[0]:LOSS_AT[0] + 1, LOSS_AT[1]:LOSS_AT[1] + LANE] = jnp.broadcast_to(loss_ref[...], (1, LANE))
        for (name, row, size), ref in zip(VEC_ROWS, refs):
            if name == "g_hgrn":
                r = lax.broadcasted_iota(jnp.int32, (NH * V_DIM, LANE), 0)
                c = lax.broadcasted_iota(jnp.int32, (NH * V_DIM, LANE), 1)
                fold = ((r % V_DIM) == c).astype(F32)
                o_ref[row:row + 1, 0:LANE] = jnp.dot(ref[...], fold, precision=HIGHEST, preferred_element_type=F32)
            else:
                o_ref[row:row + 1, 0:size] = ref[...]
        o_ref[VEC_LB_ROW:VEC_LB_ROW + 2, 0:512] = lb_ref[...]

    return pl.pallas_call(body, name="vec_pack", out_shape=jax.ShapeDtypeStruct(VEC_SHAPE, F32))(
        *[vg[n] for n in names], vg["lb_logits"], loss)


def _adamw_vec(p_mine, p_sibling, w, m, v):
    names = [n for n, _, _ in VEC_ROWS] + ["lb_logits"]
    k = len(names)

    def body(a_ref, b_ref, *refs):
        ins, outs = refs[:3 * k], refs[3 * k:]
        at = (slice(LOSS_AT[0], LOSS_AT[0] + 1), slice(LOSS_AT[1], LOSS_AT[1] + LANE))
        outs[-1][...] = a_ref[at] + b_ref[at]
        for i, name in enumerate(names):
            if name == "lb_logits":
                rows, cols = slice(VEC_LB_ROW, VEC_LB_ROW + 2), slice(0, 512)
            else:
                _, row, size = VEC_ROWS[i]
                rows, cols = slice(row, row + 1), slice(0, size)
            g = a_ref[rows, cols] + b_ref[rows, cols]
            d, nm, nv = _adamw_math(g, ins[i][...], ins[k + i][...], ins[2 * k + i][...])
            for o_ref, val in zip(outs[4 * i:4 * i + 4], (g, d, nm, nv)):
                o_ref[...] = val

    shapes = [jax.ShapeDtypeStruct(w[n].shape, F32) for n in names for _ in range(4)] + [jax.ShapeDtypeStruct((1, LANE), F32)]
    res = pl.pallas_call(body, name="adamw_vec", out_shape=shapes)(
        p_mine, p_sibling, *[w[n] for n in names], *[m[n] for n in names], *[v[n] for n in names])
    return [{n: res[4 * i + j] for i, n in enumerate(names)} for j in range(4)], res[-1]


WEIGHTS = ("g_pre", "w_in", "b_gate", "g_q", "w_uq", "g_kv", "w_ukv", "lb_logits", "g_hgrn", "w_branch_a", "w_branch_b", "w_out", "g_post")


def kernel(x, g_pre, w_in, b_gate, g_q, w_uq, g_kv, w_ukv, lb_logits, g_hgrn, w_branch_a, w_branch_b, w_out, g_post, loss_target, m_g_pre, m_w_in, m_b_gate, m_g_q, m_w_uq, m_g_kv, m_w_ukv, m_lb_logits, m_g_hgrn, m_w_branch_a, m_w_branch_b, m_w_out, m_g_post, v_g_pre, v_w_in, v_b_gate, v_g_q, v_w_uq, v_g_kv, v_w_ukv, v_lb_logits, v_g_hgrn, v_w_branch_a, v_w_branch_b, v_w_out, v_g_post):
    w = dict(g_pre=g_pre, w_in=w_in, b_gate=b_gate, g_q=g_q, w_uq=w_uq, g_kv=g_kv, w_ukv=w_ukv, lb_logits=lb_logits, g_hgrn=g_hgrn,
             w_branch_a=w_branch_a, w_branch_b=w_branch_b, w_out=w_out, g_post=g_post)
    m = dict(g_pre=m_g_pre, w_in=m_w_in, b_gate=m_b_gate, g_q=m_g_q, w_uq=m_w_uq, g_kv=m_g_kv, w_ukv=m_w_ukv, lb_logits=m_lb_logits,
             g_hgrn=m_g_hgrn, w_branch_a=m_w_branch_a, w_branch_b=m_w_branch_b, w_out=m_w_out, g_post=m_g_post)
    v = dict(g_pre=v_g_pre, w_in=v_w_in, b_gate=v_b_gate, g_q=v_g_q, w_uq=v_w_uq, g_kv=v_g_kv, w_ukv=v_w_ukv, lb_logits=v_lb_logits,
             g_hgrn=v_g_hgrn, w_branch_a=v_w_branch_a, w_branch_b=v_w_branch_b, w_out=v_w_out, g_post=v_g_post)
    blocks = {n: _to_block(n, w[n]).astype(BF16) for n in BIG}
    w_in_all = _gather_w_in(blocks["w_in"])
    weights = _LaterWeights(blocks, w_in_all)
    state = {n: [_to_block(n, t[n]) for t in (w, m, v)] for n in BIG}
    exchange = _GradExchange(state)
    loss, grad_x, _, vec_grads = _local_step(
        x[0], loss_target[0], g_pre, _join_chips("w_in", w_in_all), b_gate, g_q, g_kv, lb_logits, g_hgrn, g_post, weights, exchange)
    vec = _vec_pack(vec_grads, loss)
    vec_started, token = _exchange_start([vec], (), "vec_start")
    sums = exchange.finish(token)
    rest = tuple(sums)
    (vec_landed,) = _exchange_wait([vec], (), vec_started, sums[rest[-1]], "vec_wait")
    mine = [sums[n] for n in rest] + [_sum_landed(vec_landed, vec, "sum_vec")]
    theirs = _sibling_exchange(mine, "sibling_grads")
    done = dict(exchange.outs)
    for k, n in enumerate(rest):
        done[n] = _adamw(mine[k], theirs[k], *state[n], "adamw_" + n)
    outs = [{}, {}, {}, {}]
    for n in BIG:
        for o, val in zip(outs, done[n]):
            o[n] = _from_block(n, val)
    vec_outs, total = _adamw_vec(mine[-1], theirs[-1], w, m, v)
    for o, vals in zip(outs, vec_outs):
        o.update(vals)
    return (total[0, 0], grad_x[None], *[o[n] for o in outs for n in WEIGHTS])
```

```python
import math

import jax
import jax.numpy as jnp
from jax import lax
from jax.experimental import pallas as pl
from jax.experimental.pallas import tpu as pltpu

F32 = jnp.float32
BF16 = jnp.bfloat16
HIGHEST = lax.Precision.HIGHEST

D = 1024
NH = 8
QK_NOPE, QK_ROPE, V_DIM = 64, 32, 64
Q_LORA, KV_LORA = 768, 256
CHUNK = 64
HG_BLOCK = 32
EPS = 1e-6
LANE = 128
P_MERGE, P_GA, P_HQ, P_HF, P_HI, P_GB, P_CQ, P_CKV, P_KPE = 0, 2048, 2560, 3072, 3584, 4096, 4608, 5376, 5632
D_P = 5760
O_CQ, O_CKV, O_KPE, O_GA, O_HQ, O_HF, O_HI, O_GB, O_MERGE = 0, 768, 1024, 1056, 1568, 2080, 2592, 3104, 3616

TM = 512
TM_MID = 256
TQ = 1024
ONES_LANE = (LANE - 1, 0)
TH = 256
HG_PAIRS = 4
VMEM_LIMIT = 56 * 1024 * 1024

ADAM_LR, ADAM_B1, ADAM_B2, ADAM_EPS, ADAM_WD, ADAM_STEP = 0.001, 0.9, 0.999, 1e-08, 0.01, 10

NT_DIMS = (((1,), (1,)), ((), ()))
TN_DIMS = (((0,), (0,)), ((), ()))


def _params(sem):
    return pltpu.CompilerParams(dimension_semantics=sem, vmem_limit_bytes=VMEM_LIMIT)


def _mm(a, b):
    return jnp.dot(a, b, preferred_element_type=F32)


def _mm_nt(a, b):
    return lax.dot_general(a, b, NT_DIMS, preferred_element_type=F32)


def _mm_tn(a, b):
    return lax.dot_general(a, b, TN_DIMS, preferred_element_type=F32)


def _sigmoid(z):
    return jax.nn.sigmoid(z)


def _rope(v, c, s1, s2):
    return v * c + pltpu.roll(v, 112, 1) * s1 + pltpu.roll(v, 16, 1) * s2


def _rope_t(dy, c, s1, s2):
    return dy * c + pltpu.roll(dy * s1, 16, 1) + pltpu.roll(dy * s2, 112, 1)


def _rope_tables(s):
    inv = 10000.0 ** (-jnp.arange(0, QK_ROPE, 2, dtype=F32) / QK_ROPE)
    ang = jnp.arange(s, dtype=F32)[:, None] * inv[None, :]
    cos, sin = jnp.cos(ang), jnp.sin(ang)
    z64, z32, o64, o32 = jnp.zeros((s, 64), F32), jnp.zeros((s, 32), F32), jnp.ones((s, 64), F32), jnp.ones((s, 32), F32)
    z16 = jnp.zeros((s, 16), F32)
    c = jnp.concatenate([o64, cos, cos, o32], axis=1)
    s1 = jnp.concatenate([z64, -sin, z16, z32], axis=1)
    s2 = jnp.concatenate([z64, z16, sin, z32], axis=1)
    return c, s1, s2


def _front_fwd(x, g_pre, w_in_pt, tokens=()):
    s = x.shape[0]
    tokens = list(tokens)

    def body(x_ref, g_ref, w_ref, *refs):
        o_ref, h_ref = refs[len(tokens):]
        xv = x_ref[...]
        r = lax.rsqrt(jnp.mean(xv * xv, axis=-1, keepdims=True) + EPS)
        h = ((xv * r) * g_ref[...]).astype(BF16)
        h_ref[...] = h
        o_ref[...] = _mm_nt(h, w_ref[...])

    return pl.pallas_call(
        body, name="front_fwd", grid=(s // TM,),
        in_specs=[pl.BlockSpec((TM, D), lambda i: (i, 0)), pl.BlockSpec((1, D), lambda i: (0, 0)),
                  pl.BlockSpec((D_P, D), lambda i: (0, 0))] + [pl.BlockSpec((8, LANE), lambda i: (0, 0))] * len(tokens),
        out_specs=[pl.BlockSpec((TM, D_P), lambda i: (i, 0)), pl.BlockSpec((TM, D), lambda i: (i, 0))],
        out_shape=[jax.ShapeDtypeStruct((s, D_P), F32), jax.ShapeDtypeStruct((s, D), BF16)],
        compiler_params=_params(("parallel",)),
    )(x, g_pre, w_in_pt, *tokens)


def _norm_rows(v, g):
    r = lax.rsqrt(jnp.mean(v * v, axis=-1, keepdims=True) + EPS)
    return (v * r) * g, r


def _qkv_fwd(proj, g_q, g_kv, w_uq_p, w_k_p, w_v_p, rc, rs1, rs2):
    s = proj.shape[0]

    def body(cq_ref, ckv_ref, kpe_ref, gq_ref, gkv_ref, wq_ref, wk_ref, wv_ref, c_ref, s1_ref, s2_ref, q_ref, k_ref, v_ref):
        c, s1, s2 = c_ref[...], s1_ref[...], s2_ref[...]
        cqn, _ = _norm_rows(cq_ref[...], gq_ref[...])
        ckvn, _ = _norm_rows(ckv_ref[...], gkv_ref[...])
        ckvn = ckvn.astype(BF16)
        qf = _mm(cqn.astype(BF16), wq_ref[...])
        kf = _mm(ckvn, wk_ref[...])
        vf = _mm(ckvn, wv_ref[...])
        kpe = _rope(kpe_ref[...], c, s1, s2)
        lane = lax.broadcasted_iota(jnp.int32, (TM, LANE), 1)
        for h in range(NH):
            blk = slice(h * LANE, (h + 1) * LANE)
            q_ref[h] = _rope(qf[:, blk], c, s1, s2).astype(BF16)
            k_ref[h] = (kf[:, blk] + kpe).astype(BF16)
            v_ref[h] = jnp.where(lane == ONES_LANE[h % 2], 1.0, vf[:, blk]).astype(BF16)

    row = lambda w, j: pl.BlockSpec((TM, w), lambda i: (i, j))
    full = lambda a: pl.BlockSpec(a.shape, lambda i: (0,) * a.ndim)
    hs = jax.ShapeDtypeStruct((NH, s, LANE), BF16)
    return pl.pallas_call(
        body, name="qkv_fwd", grid=(s // TM,),
        in_specs=[row(Q_LORA, P_CQ // Q_LORA), row(KV_LORA, P_CKV // KV_LORA), row(LANE, P_KPE // LANE),
                  full(g_q), full(g_kv), full(w_uq_p), full(w_k_p), full(w_v_p), row(LANE, 0), row(LANE, 0), row(LANE, 0)],
        out_specs=[pl.BlockSpec((NH, TM, LANE), lambda i: (0, i, 0))] * 3,
        out_shape=[hs, hs, hs],
        compiler_params=_params(("parallel",)),
    )(proj, proj, proj, g_q, g_kv, w_uq_p, w_k_p, w_v_p, rc, rs1, rs2)


LOG2E = 1.4426950408889634
QK_SCALE2 = LOG2E / math.sqrt(QK_NOPE + QK_ROPE)


HQ = TQ // 2


def _diag_visible(n):
    row = lax.broadcasted_iota(jnp.int32, (n, n), 0)
    col = lax.broadcasted_iota(jnp.int32, (n, n), 1)
    return (col // CHUNK) <= (row // CHUNK)


def _attn_fwd(q, k, vv):
    s = q.shape[1]

    def body(q_ref, k_ref, v_ref, o_ref, lse_ref):
        i = pl.program_id(1)
        qs = (q_ref[0], q_ref[1])

        def tiles(t, carry, diag):
            rows = pl.ds(pl.multiple_of(t * TQ, TQ), TQ)
            sc = [_mm_nt(qs[hh], k_ref[hh, rows, :]) for hh in range(2)]
            if diag:
                sc = [jnp.where(_diag_visible(TQ), s_, -jnp.inf) for s_ in sc]
            m_new = [jnp.maximum(carry[hh][0], jnp.max(sc[hh], axis=-1, keepdims=True)) for hh in range(2)]
            alpha = [jnp.exp2((carry[hh][0] - m_new[hh]) * QK_SCALE2) for hh in range(2)]
            p = [jnp.exp2((sc[hh] - m_new[hh]) * QK_SCALE2).astype(BF16) for hh in range(2)]
            acc = [alpha[hh] * carry[hh][1] + _mm(p[hh], v_ref[hh, rows, :]) for hh in range(2)]
            return (m_new[0], acc[0]), (m_new[1], acc[1])

        init = (jnp.full((TQ, 1), -jnp.inf, F32), jnp.zeros((TQ, LANE), F32))
        carry = lax.fori_loop(0, i, lambda t, c: tiles(t, c, False), (init, init))
        carry = tiles(i, carry, True)
        lane = lax.broadcasted_iota(jnp.int32, (TQ, LANE), 1)
        out = jnp.zeros((TQ, LANE), F32)
        for hh in range(2):
            m, acc = carry[hh]
            l = jnp.sum(jnp.where(lane == ONES_LANE[hh], acc, 0.0), axis=-1, keepdims=True)
            out = out + jnp.where((lane < V_DIM) == (hh == 0), acc, 0.0) / l
            lse_ref[hh] = jnp.broadcast_to(m * QK_SCALE2 + jnp.log(l) * LOG2E, (TQ, LANE))
        o_ref[...] = out

    return pl.pallas_call(
        body, name="attn_fwd", grid=(NH // 2, s // TQ),
        in_specs=[pl.BlockSpec((2, TQ, LANE), lambda p, i: (p, i, 0)), pl.BlockSpec((2, s, LANE), lambda p, i: (p, 0, 0)),
                  pl.BlockSpec((2, s, LANE), lambda p, i: (p, 0, 0))],
        out_specs=[pl.BlockSpec((TQ, LANE), lambda p, i: (i, p)), pl.BlockSpec((2, TQ, LANE), lambda p, i: (p, i, 0))],
        out_shape=[jax.ShapeDtypeStruct((s, NH * V_DIM), F32), jax.ShapeDtypeStruct((NH, s, LANE), F32)],
        compiler_params=_params(("parallel", "parallel")),
    )(q, k, vv)


def _lower_bound(lbl):
    a0, a1 = lbl[0:1, :], lbl[1:2, :]
    mx = jnp.maximum(a0, a1)
    e0, e1 = jnp.exp(a0 - mx), jnp.exp(a1 - mx)
    return e0 / (e0 + e1)


def _chunk_cumsum(v, reverse=False):
    pos = lax.broadcasted_iota(jnp.int32, v.shape, 0) % HG_BLOCK
    s = 1
    while s < HG_BLOCK:
        if reverse:
            v = v + jnp.where(pos < HG_BLOCK - s, pltpu.roll(v, TH - s, 0), 0.0)
        else:
            v = v + jnp.where(pos >= s, pltpu.roll(v, s, 0), 0.0)
        s *= 2
    return v


def _hgrn_gates(hq, hf, lb):
    sig = _sigmoid(hf)
    f = lb + (1.0 - lb) * sig
    g = jnp.log(f)
    kk = 1.0 - f
    r = lax.broadcasted_iota(jnp.int32, (TH, TH), 0)
    c = lax.broadcasted_iota(jnp.int32, (TH, TH), 1)
    tri = ((r // HG_BLOCK) == (c // HG_BLOCK)) & (r >= c)
    cum = _chunk_cumsum(g)
    nch = TH // HG_BLOCK
    total = _chunks(cum)[:, HG_BLOCK - 1:HG_BLOCK, :]
    lastb = jnp.broadcast_to(total, (nch, HG_BLOCK, hf.shape[-1])).reshape(hf.shape)
    e, ei, ee = jnp.exp(cum), jnp.exp(-cum), jnp.exp(lastb - cum)
    return dict(sig=sig, f=f, kk=kk, tri=tri, cum=cum, total=total, decay=jnp.exp(total), e=e, ei=ei, ee=ee,
                qd=hq * e, ki=kk * ei, ke=kk * ee)


def _chunks(v):
    return v.reshape(TH // HG_BLOCK, HG_BLOCK, v.shape[-1])


def _bmm_nt(a, b):
    return lax.dot_general(a, b, (((2,), (2,)), ((0,), (0,))), preferred_element_type=F32)


def _bmm_nn(a, b):
    return lax.dot_general(a, b, (((2,), (1,)), ((0,), (0,))), preferred_element_type=F32)


def _bmm_tn(a, b):
    return lax.dot_general(a, b, (((1,), (1,)), ((0,), (0,))), preferred_element_type=F32)


def _pair_masks():
    lane = lax.broadcasted_iota(jnp.int32, (TH, LANE), 1)
    kr = lax.broadcasted_iota(jnp.int32, (LANE, LANE), 0)
    kc = lax.broadcasted_iota(jnp.int32, (LANE, LANE), 1)
    return lane < 64, (kr // 64) == (kc // 64)


def _hgrn_fwd(proj, lbl):
    s = proj.shape[0]
    nch = TH // HG_BLOCK

    def body(hq_ref, hf_ref, hi_ref, lbl_ref, o_ref, st_ref, st):
        @pl.when(pl.program_id(1) == 0)
        def _():
            st[...] = jnp.zeros_like(st)

        m0, bd = _pair_masks()
        gt = _hgrn_gates(hq_ref[...], hf_ref[...], _lower_bound(lbl_ref[...]))
        v_b, qd, qd_b = hi_ref[...].astype(BF16), gt["qd"], gt["qd"].astype(BF16)
        ki_b, ke_b = gt["ki"].astype(BF16), gt["ke"].astype(BF16)
        pairs = [slice(u * LANE, (u + 1) * LANE) for u in range(HG_PAIRS)]
        heads = [(lanes, m0 if hh == 0 else jnp.logical_not(m0)) for lanes in pairs for hh in range(2)]
        a_b = [jnp.where(gt["tri"], _mm_nt(jnp.where(mh, qd[:, lanes], 0.0).astype(BF16), ki_b[:, lanes]), 0.0).astype(BF16)
               for lanes, mh in heads]
        intra = [jnp.where(m0, _mm(a_b[2 * u], v_b[:, lanes]), _mm(a_b[2 * u + 1], v_b[:, lanes])) for u, lanes in enumerate(pairs)]
        upd = [_bmm_tn(_chunks(v_b[:, lanes]), _chunks(ke_b[:, lanes])) for lanes in pairs]
        entering = []
        for u, lanes in enumerate(pairs):
            cur, states = st[u], []
            for n in range(nch):
                states.append(cur)
                cur = gt["decay"][n][:, lanes] * cur + jnp.where(bd, upd[u][n], 0.0)
            st[u] = cur
            entering.append(jnp.stack(states))
            st_ref[u] = entering[u]
        for u, lanes in enumerate(pairs):
            o_ref[:, lanes] = intra[u] + _bmm_nt(_chunks(qd_b[:, lanes]), entering[u].astype(BF16)).reshape(TH, LANE)

    wide = HG_PAIRS * LANE
    col = lambda base: pl.BlockSpec((TH, wide), lambda p, i: (i, base // wide + p))
    return pl.pallas_call(
        body, name="hgrn_fwd", grid=(NH // 2 // HG_PAIRS, s // TH),
        in_specs=[col(P_HQ), col(P_HF), col(P_HI), pl.BlockSpec((2, wide), lambda p, i: (0, p))],
        out_specs=[pl.BlockSpec((TH, wide), lambda p, i: (i, p)),
                   pl.BlockSpec((HG_PAIRS, nch, LANE, LANE), lambda p, i: (p, i, 0, 0))],
        out_shape=[jax.ShapeDtypeStruct((s, 512), F32), jax.ShapeDtypeStruct((NH // 2, s // HG_BLOCK, LANE, LANE), F32)],
        scratch_shapes=[pltpu.VMEM((HG_PAIRS, LANE, LANE), F32)],
        compiler_params=_params(("parallel", "arbitrary")),
    )(proj, proj, proj, lbl)


def _group_sum(v):
    low = lax.broadcasted_iota(jnp.int32, (v.shape[0], LANE), 1) < V_DIM
    blocks = []
    for b in range(v.shape[1] // LANE):
        blk = v[:, b * LANE:(b + 1) * LANE]
        s_low = jnp.sum(jnp.where(low, blk, 0.0), axis=-1, keepdims=True)
        s_high = jnp.sum(jnp.where(low, 0.0, blk), axis=-1, keepdims=True)
        blocks.append(jnp.where(low, s_low, s_high))
    return jnp.concatenate(blocks, axis=1)


def _dsilu(z, sg):
    return sg * (1.0 + z * (1.0 - sg))


def _mid(proj, attn, o_raw, x, tgt, g_hg, b_gate, g_post, wa, wb, w_out):
    s = x.shape[0]

    def body(attn_ref, ga_ref, o_ref, gb_ref, mg_ref, x_ref, t_ref, ghg_ref, bg_ref, gp_ref, wa_ref, wb_ref, wo_ref,
             loss_ref, dout_ref, dattn_ref, dga_ref, dor_ref, dgb_ref, dmg_ref, dwo_ref, dwa_ref, dwb_ref, dgp_ref, dbg_ref, dghg_ref):
        @pl.when(pl.program_id(0) == 0)
        def _():
            for rf in (loss_ref, dwo_ref, dwa_ref, dwb_ref, dgp_ref, dbg_ref, dghg_ref):
                rf[...] = jnp.zeros_like(rf)

        attn, za, orw, zb = attn_ref[...], ga_ref[...], o_ref[...], gb_ref[...]
        ghg, gp = ghg_ref[...], gp_ref[...]
        sga, sgb = _sigmoid(za), _sigmoid(zb)
        sa, sb = za * sga, zb * sgb
        ga = attn * sa
        rh = lax.rsqrt(_group_sum(orw * orw) * (1.0 / V_DIM) + EPS)
        on = (orw * rh) * ghg
        gb = on * sb
        ga_b, gb_b = ga.astype(BF16), gb.astype(BF16)
        ya = _mm(ga_b, wa_ref[...])
        yb = _mm(gb_b, wb_ref[...])
        gates = _sigmoid(mg_ref[...] + bg_ref[...])
        g0, g1 = gates[:, :D], gates[:, D:]
        m_b = (g0 * ya + g1 * yb).astype(BF16)
        y = _mm(m_b, wo_ref[...])
        ry = lax.rsqrt(jnp.mean(y * y, axis=-1, keepdims=True) + EPS)
        out = x_ref[...] + (y * ry) * gp
        err = out - t_ref[...]
        loss_ref[...] += 0.5 * jnp.sum(jnp.mean(err * err, axis=-1, keepdims=True), axis=0, keepdims=True)
        dout = err * (1.0 / D)
        dout_ref[...] = dout
        dgp_ref[...] += jnp.sum(dout * (y * ry), axis=0, keepdims=True)
        dgy = dout * gp
        dy = ry * dgy - y * (ry * ry * ry) * jnp.mean(y * dgy, axis=-1, keepdims=True)
        dy_b = dy.astype(BF16)
        dwo_ref[...] += _mm_tn(m_b, dy_b)
        dm = _mm_nt(dy_b, wo_ref[...])
        dya, dyb = dm * g0, dm * g1
        dg0, dg1 = dm * ya, dm * yb
        dmg = jnp.concatenate([dg0 * g0 * (1.0 - g0), dg1 * g1 * (1.0 - g1)], axis=1)
        dmg_ref[...] = dmg.astype(BF16)
        dbg_ref[...] += jnp.sum(dmg, axis=0, keepdims=True)
        dya_b, dyb_b = dya.astype(BF16), dyb.astype(BF16)
        dwa_ref[...] += _mm_tn(ga_b, dya_b)
        dwb_ref[...] += _mm_tn(gb_b, dyb_b)
        dga = _mm_nt(dya_b, wa_ref[...])
        dgb = _mm_nt(dyb_b, wb_ref[...])
        dattn_ref[...] = dga * sa
        dga_ref[...] = (dga * attn * _dsilu(za, sga)).astype(BF16)
        dgb_ref[...] = (dgb * on * _dsilu(zb, sgb)).astype(BF16)
        don = dgb * sb
        dghg_ref[...] += jnp.sum(don * (orw * rh), axis=0, keepdims=True)
        dgo = don * ghg
        dor_ref[...] = rh * dgo - orw * (rh * rh * rh) * (_group_sum(orw * dgo) * (1.0 / V_DIM))

    row = lambda w, j=0: pl.BlockSpec((TM_MID, w), lambda i: (i, j))
    full = lambda a: pl.BlockSpec(a.shape, lambda i: (0,) * a.ndim)
    acc = lambda shape: pl.BlockSpec(shape, lambda i: (0, 0))
    sds = jax.ShapeDtypeStruct
    return pl.pallas_call(
        body, name="mid", grid=(s // TM_MID,),
        in_specs=[row(512), row(512, P_GA // 512), row(512), row(512, P_GB // 512), row(2048, P_MERGE // 2048), row(D), row(D),
                  full(g_hg), full(b_gate), full(g_post), full(wa), full(wb), full(w_out)],
        out_specs=[acc((1, 1)), row(D), row(512), row(512), row(512), row(512), row(2048),
                   acc((D, D)), acc((512, D)), acc((512, D)), acc((1, D)), acc((1, 2048)), acc((1, 512))],
        out_shape=[sds((1, 1), F32), sds((s, D), F32), sds((s, 512), F32), sds((s, 512), BF16), sds((s, 512), F32), sds((s, 512), BF16),
                   sds((s, 2048), BF16), sds((D, D), F32), sds((512, D), F32), sds((512, D), F32), sds((1, D), F32),
                   sds((1, 2048), F32), sds((1, 512), F32)],
        compiler_params=_params(("arbitrary",)),
    )(attn, proj, o_raw, proj, proj, x, tgt, g_hg, b_gate, g_post, wa, wb, w_out)


def _attn_bwd(q, k, vv, attn, dattn, lse, token):
    s = q.shape[1]
    nt = s // TQ
    scale = 1.0 / math.sqrt(QK_NOPE + QK_ROPE)

    def body(q_ref, k_ref, v_ref, o_ref, do_ref, lse_ref, token_ref, dq_ref, dk_ref, dv_ref, do_s, delta_s):
        j = pl.program_id(1)

        @pl.when(j == 0)
        def _():
            dq_ref[...] = jnp.zeros_like(dq_ref)
            lane = lax.broadcasted_iota(jnp.int32, (TQ, LANE), 1)

            @pl.loop(0, nt)
            def _(i):
                rows = pl.ds(pl.multiple_of(i * TQ, TQ), TQ)
                do, o = do_ref[rows, :], o_ref[rows, :]
                for hh in range(2):
                    doh = jnp.where((lane < 64) if hh == 0 else (lane >= 64), do, 0.0)
                    do_s[hh, rows, :] = doh.astype(BF16)
                    delta_s[hh, rows, :] = jnp.broadcast_to(jnp.sum(doh * o, axis=-1, keepdims=True), (TQ, LANE))

        kjs, vjs = (k_ref[0], k_ref[1]), (v_ref[0], v_ref[1])

        def tile(hh, start, size, kj, vj, diag):
            rows = pl.ds(pl.multiple_of(start, size), size)
            wide = lambda a: jnp.concatenate([a] * (kj.shape[0] // LANE), axis=1)
            qi, do_b = q_ref[hh, rows, :], do_s[hh, rows, :]
            p = jnp.exp2(_mm_nt(qi, kj) * QK_SCALE2 - wide(lse_ref[hh, rows, :]))
            if diag:
                p = jnp.where(_diag_visible(size), p, 0.0)
            dv = _mm_tn(do_b, p.astype(BF16))
            ds_b = (p * (_mm_nt(do_b, vj) - wide(delta_s[hh, rows, :]))).astype(BF16)
            dk = _mm_tn(qi, ds_b)
            dq_ref[hh, rows, :] += _mm(ds_b, kj)
            return dk, dv

        def step(i, carry):
            new = [tile(hh, i * TQ, TQ, kjs[hh], vjs[hh], False) for hh in range(2)]
            return tuple((carry[hh][0] + new[hh][0], carry[hh][1] + new[hh][1]) for hh in range(2))

        def diagonal(hh):
            k0, k1, v0, v1 = kjs[hh][:HQ], kjs[hh][HQ:], vjs[hh][:HQ], vjs[hh][HQ:]
            a = tile(hh, j * TQ, HQ, k0, v0, True)
            b = tile(hh, j * TQ + HQ, HQ, k0, v0, False)
            c = tile(hh, j * TQ + HQ, HQ, k1, v1, True)
            return jnp.concatenate([a[0] + b[0], c[0]], axis=1), jnp.concatenate([a[1] + b[1], c[1]], axis=1)

        carry = lax.fori_loop(j + 1, nt, step, (diagonal(0), diagonal(1)))
        for hh in range(2):
            dk_ref[hh] = carry[hh][0].T * scale
            dv_ref[hh] = carry[hh][1].T

        @pl.when(j == nt - 1)
        def _():
            dq_ref[...] = dq_ref[...] * scale

    whole = pl.BlockSpec((2, s, LANE), lambda p, j: (p, 0, 0))
    tile_spec = pl.BlockSpec((2, TQ, LANE), lambda p, j: (p, j, 0))
    cols = pl.BlockSpec((s, LANE), lambda p, j: (0, p))
    hs = jax.ShapeDtypeStruct((NH, s, LANE), F32)
    return pl.pallas_call(
        body, name="attn_bwd", grid=(NH // 2, nt),
        in_specs=[whole, tile_spec, tile_spec, cols, cols, whole, pl.BlockSpec((8, LANE), lambda p, j: (0, 0))],
        out_specs=[whole, tile_spec, tile_spec],
        out_shape=[hs, hs, hs],
        scratch_shapes=[pltpu.VMEM((2, s, LANE), BF16), pltpu.VMEM((2, s, LANE), F32)],
        compiler_params=_params(("parallel", "arbitrary")),
    )(q, k, vv, attn, dattn, lse, token)


def _hgrn_bwd(proj, lbl, states, do_raw):
    s = proj.shape[0]
    nt = s // TH
    nch = TH // HG_BLOCK

    def body(hq_ref, hf_ref, hi_ref, lbl_ref, st_ref, do_ref, dh_ref, dlbl_ref, dst, dlb):
        step = pl.program_id(1)

        @pl.when(step == 0)
        def _():
            dst[...] = jnp.zeros_like(dst)
            dlb[...] = jnp.zeros_like(dlb)

        m0, bd = _pair_masks()
        lb = _lower_bound(lbl_ref[...])
        gt = _hgrn_gates(hq_ref[...], hf_ref[...], lb)
        do = do_ref[...]
        qd, ki, ke = gt["qd"], gt["ki"], gt["ke"]
        v_b, do_b = hi_ref[...].astype(BF16), do.astype(BF16)
        qd_b, ki_b, ke_b = qd.astype(BF16), ki.astype(BF16), ke.astype(BF16)
        pairs = [slice(u * LANE, (u + 1) * LANE) for u in range(HG_PAIRS)]
        heads = [(lanes, m0 if hh == 0 else jnp.logical_not(m0)) for lanes in pairs for hh in range(2)]
        a_b = [jnp.where(gt["tri"], _mm_nt(jnp.where(mh, qd[:, lanes], 0.0).astype(BF16), ki_b[:, lanes]), 0.0).astype(BF16)
               for lanes, mh in heads]
        doh_b = [jnp.where(mh, do[:, lanes], 0.0).astype(BF16) for lanes, mh in heads]
        da_b = [jnp.where(gt["tri"], _mm_nt(d, v_b[:, lanes]), 0.0).astype(BF16) for d, (lanes, _) in zip(doh_b, heads)]
        dv_p, dqd_p, dki_p = [], [], []
        for u, lanes in enumerate(pairs):
            e, o = 2 * u, 2 * u + 1
            dv_p.append(_mm_tn(a_b[e], doh_b[e]) + _mm_tn(a_b[o], doh_b[o]))
            dqd_p.append(jnp.where(m0, _mm(da_b[e], ki_b[:, lanes]), _mm(da_b[o], ki_b[:, lanes])))
            dki_p.append(jnp.where(m0, _mm_tn(da_b[e], qd_b[:, lanes]), _mm_tn(da_b[o], qd_b[:, lanes])))
        fed = [_bmm_tn(_chunks(do_b[:, lanes]), _chunks(qd_b[:, lanes])) for lanes in pairs]
        leaving = []
        for u, lanes in enumerate(pairs):
            ds, left = dst[u], [None] * nch
            for n in reversed(range(nch)):
                left[n] = ds
                ds = gt["decay"][n][:, lanes] * ds + jnp.where(bd, fed[u][n], 0.0)
            dst[u] = ds
            leaving.append(jnp.stack(left))
        dke_p, dlast_p = [], []
        for u, lanes in enumerate(pairs):
            entering, leaving_b = st_ref[u], leaving[u].astype(BF16)
            dke3 = _bmm_nn(_chunks(v_b[:, lanes]), leaving_b)
            dv_p[u] = dv_p[u] + _bmm_nt(_chunks(ke_b[:, lanes]), leaving_b).reshape(TH, LANE)
            dqd_p[u] = dqd_p[u] + _bmm_nn(_chunks(do_b[:, lanes]), entering.astype(BF16)).reshape(TH, LANE)
            dke_p.append(dke3.reshape(TH, LANE))
            dlast_p.append(jnp.sum(dke3 * _chunks(ke[:, lanes]), axis=1, keepdims=True)
                           + jnp.sum(leaving[u] * entering, axis=1, keepdims=True) * gt["decay"][:, :, lanes])
        cat = lambda parts: jnp.concatenate(parts, axis=-1)
        dv, dqd, dki, dke, dlast = cat(dv_p), cat(dqd_p), cat(dki_p), cat(dke_p), cat(dlast_p)
        dk = dki * gt["ei"] + dke * gt["ee"]
        dcum = dqd * qd - dki * ki - dke * ke
        dg = _chunk_cumsum(dcum, reverse=True) + jnp.broadcast_to(dlast, (nch, HG_BLOCK, dlast.shape[-1])).reshape(dcum.shape)
        sig = gt["sig"]
        df = dg / gt["f"] - dk
        dlb[...] += jnp.sum(df * (1.0 - sig), axis=0, keepdims=True)
        dh_ref[0] = (dqd * gt["e"]).astype(BF16)
        dh_ref[1] = ((df * (1.0 - lb)) * sig * (1.0 - sig)).astype(BF16)
        dh_ref[2] = dv.astype(BF16)

        @pl.when(step == nt - 1)
        def _():
            lb = _lower_bound(lbl_ref[...])
            da0 = dlb[...] * lb * (1.0 - lb)
            dlbl_ref[...] = jnp.concatenate([da0, -da0], axis=0)

    wide = HG_PAIRS * LANE
    col = lambda base: pl.BlockSpec((TH, wide), lambda p, i: (nt - 1 - i, base // wide + p))
    tile = pl.BlockSpec((TH, wide), lambda p, i: (nt - 1 - i, p))
    sds = jax.ShapeDtypeStruct
    return pl.pallas_call(
        body, name="hgrn_bwd", grid=(NH // 2 // HG_PAIRS, nt),
        in_specs=[col(P_HQ), col(P_HF), col(P_HI), pl.BlockSpec((2, wide), lambda p, i: (0, p)),
                  pl.BlockSpec((HG_PAIRS, nch, LANE, LANE), lambda p, i: (p, nt - 1 - i, 0, 0)), tile],
        out_specs=[pl.BlockSpec((3, TH, wide), lambda p, i: (0, nt - 1 - i, p)), pl.BlockSpec((2, wide), lambda p, i: (0, p))],
        out_shape=[sds((3, s, 512), BF16), sds((2, 512), F32)],
        scratch_shapes=[pltpu.VMEM((HG_PAIRS, LANE, LANE), F32), pltpu.VMEM((1, wide), F32)],
        compiler_params=_params(("parallel", "arbitrary")),
    )(proj, proj, proj, lbl, states, do_raw)


def _norm_rows_bwd(v, r, g, dn):
    dgv = dn * g
    return r * dgv - v * (r * r * r) * jnp.mean(v * dgv, axis=-1, keepdims=True)


def _qkv_bwd(proj, dq, dk, dvv, g_q, g_kv, w_uq_p, w_k_p, w_v_p, rc, rs1, rs2):
    s = proj.shape[0]

    def body(cq_ref, ckv_ref, dq_ref, dk_ref, dv_ref, gq_ref, gkv_ref, wq_ref, wk_ref, wv_ref, c_ref, s1_ref, s2_ref,
             dcq_ref, dckv_ref, dkpe_ref, dwq_ref, dwk_ref, dwv_ref, dgq_ref, dgkv_ref):
        @pl.when(pl.program_id(0) == 0)
        def _():
            for rf in (dwq_ref, dwk_ref, dwv_ref, dgq_ref, dgkv_ref):
                rf[...] = jnp.zeros_like(rf)

        c, s1, s2 = c_ref[...], s1_ref[...], s2_ref[...]
        cq, ckv = cq_ref[...], ckv_ref[...]
        gq, gkv = gq_ref[...], gkv_ref[...]
        cqn, rq = _norm_rows(cq, gq)
        ckvn, rkv = _norm_rows(ckv, gkv)
        cqn_b, ckvn_b = cqn.astype(BF16), ckvn.astype(BF16)
        dqf = jnp.concatenate([_rope_t(dq_ref[h], c, s1, s2) for h in range(NH)], axis=1).astype(BF16)
        dkf = jnp.concatenate([dk_ref[h] for h in range(NH)], axis=1).astype(BF16)
        dvf = jnp.concatenate([dv_ref[h] for h in range(NH)], axis=1).astype(BF16)
        dkpe = dk_ref[0]
        for h in range(1, NH):
            dkpe = dkpe + dk_ref[h]
        lane = lax.broadcasted_iota(jnp.int32, (TM, LANE), 1)
        dkpe = jnp.where((lane >= QK_NOPE) & (lane < QK_NOPE + QK_ROPE), dkpe, 0.0)
        dkpe_ref[...] = _rope_t(dkpe, c, s1, s2).astype(BF16)
        dwq_ref[...] += _mm_tn(cqn_b, dqf)
        dwk_ref[...] += _mm_tn(ckvn_b, dkf)
        dwv_ref[...] += _mm_tn(ckvn_b, dvf)
        dcqn = _mm_nt(dqf, wq_ref[...])
        dckvn = _mm_nt(dkf, wk_ref[...]) + _mm_nt(dvf, wv_ref[...])
        dgq_ref[...] += jnp.sum(dcqn * (cq * rq), axis=0, keepdims=True)
        dgkv_ref[...] += jnp.sum(dckvn * (ckv * rkv), axis=0, keepdims=True)
        dcq_ref[...] = _norm_rows_bwd(cq, rq, gq, dcqn).astype(BF16)
        dckv_ref[...] = _norm_rows_bwd(ckv, rkv, gkv, dckvn).astype(BF16)

    row = lambda w, j=0: pl.BlockSpec((TM, w), lambda i: (i, j))
    full = lambda a: pl.BlockSpec(a.shape, lambda i: (0,) * a.ndim)
    acc = lambda shape: pl.BlockSpec(shape, lambda i: (0, 0))
    heads = pl.BlockSpec((NH, TM, LANE), lambda i: (0, i, 0))
    sds = jax.ShapeDtypeStruct
    return pl.pallas_call(
        body, name="qkv_bwd", grid=(s // TM,),
        in_specs=[row(Q_LORA, P_CQ // Q_LORA), row(KV_LORA, P_CKV // KV_LORA), heads, heads, heads,
                  full(g_q), full(g_kv), full(w_uq_p), full(w_k_p), full(w_v_p), row(LANE), row(LANE), row(LANE)],
        out_specs=[row(Q_LORA), row(KV_LORA), row(LANE), acc((Q_LORA, D)), acc((KV_LORA, D)), acc((KV_LORA, D)),
                   acc((1, Q_LORA)), acc((1, KV_LORA))],
        out_shape=[sds((s, Q_LORA), BF16), sds((s, KV_LORA), BF16), sds((s, LANE), BF16), sds((Q_LORA, D), F32),
                   sds((KV_LORA, D), F32), sds((KV_LORA, D), F32), sds((1, Q_LORA), F32), sds((1, KV_LORA), F32)],
        compiler_params=_params(("arbitrary",)),
    )(proj, proj, dq, dk, dvv, g_q, g_kv, w_uq_p, w_k_p, w_v_p, rc, rs1, rs2)


def _front_bwd(x, dout, dmg, dga, dh3, dgb, dcq, dckv, dkpe, g_pre, w_in_pt, token):
    s = x.shape[0]

    def body(x_ref, do_ref, dmg_ref, dga_ref, dh3_ref, dgb_ref, dcq_ref, dckv_ref, dkpe_ref, g_ref, w_ref, token_ref, gx_ref, dg_ref):
        @pl.when(pl.program_id(0) == 0)
        def _():
            dg_ref[...] = jnp.zeros_like(dg_ref)

        xv, g = x_ref[...], g_ref[...]
        _, r = _norm_rows(xv, g)
        pieces = ((dmg_ref[...], P_MERGE), (dga_ref[...], P_GA), (dh3_ref[0], P_HQ), (dh3_ref[1], P_HF), (dh3_ref[2], P_HI),
                  (dgb_ref[...], P_GB), (dcq_ref[...], P_CQ), (dckv_ref[...], P_CKV), (dkpe_ref[...], P_KPE))
        dh = jnp.zeros((TM, D), F32)
        for piece, off in pieces:
            dh = dh + _mm(piece, w_ref[off:off + piece.shape[1], :])
        dg_ref[...] += jnp.sum(dh * (xv * r), axis=0, keepdims=True)
        gx_ref[...] = do_ref[...] + _norm_rows_bwd(xv, r, g, dh)

    row = lambda w: pl.BlockSpec((TM, w), lambda i: (i, 0))
    full = lambda a: pl.BlockSpec(a.shape, lambda i: (0,) * a.ndim)
    sds = jax.ShapeDtypeStruct
    return pl.pallas_call(
        body, name="front_bwd", grid=(s // TM,),
        in_specs=[row(D), row(D), row(2048), row(512), pl.BlockSpec((3, TM, 512), lambda i: (0, i, 0)), row(512), row(Q_LORA),
                  row(KV_LORA), row(LANE), full(g_pre), full(w_in_pt), pl.BlockSpec(memory_space=pl.ANY)],
        out_specs=[row(D), pl.BlockSpec((1, D), lambda i: (0, 0))],
        out_shape=[sds((s, D), F32), sds((1, D), F32)],
        compiler_params=_params(("arbitrary",)),
    )(x, dout, dmg, dga, dh3, dgb, dcq, dckv, dkpe, g_pre, w_in_pt, token)


TK_GRAD = 1024


def _win_grad(h, pieces, name):
    s = h.shape[0]
    n = len(pieces)

    def body(h_ref, *refs):
        @pl.when(pl.program_id(0) == 0)
        def _():
            for o_ref in refs[n:]:
                o_ref[...] = jnp.zeros_like(o_ref)

        hv = h_ref[...]
        for d_ref, o_ref in zip(refs[:n], refs[n:]):
            if len(d_ref.shape) == 3:
                for k in range(d_ref.shape[0]):
                    o_ref[k] += _mm_tn(d_ref[k], hv)
            else:
                o_ref[...] += _mm_tn(d_ref[...], hv)

    def in_spec(p):
        if p.ndim == 3:
            return pl.BlockSpec((p.shape[0], TK_GRAD, p.shape[2]), lambda kk: (0, kk, 0))
        return pl.BlockSpec((TK_GRAD, p.shape[1]), lambda kk: (kk, 0))

    out_shapes = [(p.shape[0], p.shape[2], D) if p.ndim == 3 else (p.shape[1], D) for p in pieces]
    return pl.pallas_call(
        body, name=name, grid=(s // TK_GRAD,),
        in_specs=[pl.BlockSpec((TK_GRAD, D), lambda kk: (kk, 0))] + [in_spec(p) for p in pieces],
        out_specs=[pl.BlockSpec(sh, lambda kk, nd=len(sh): (0,) * nd) for sh in out_shapes],
        out_shape=[jax.ShapeDtypeStruct(sh, F32) for sh in out_shapes],
        compiler_params=_params(("arbitrary",)),
    )(h, *pieces)


def _pad_win_t(w_in_t):
    z = lambda n: jnp.zeros((n, w_in_t.shape[1]), w_in_t.dtype)
    sl = lambda o, n: w_in_t[o:o + n]
    return jnp.concatenate([sl(O_MERGE, 2048), sl(O_GA, 512), sl(O_HQ, 512), sl(O_HF, 512), sl(O_HI, 512), sl(O_GB, 512),
                            sl(O_CQ, Q_LORA), sl(O_CKV, KV_LORA), z(64), sl(O_KPE, QK_ROPE), z(32)], axis=0)


def _pad_wuq(w_uq):
    rows = w_uq.shape[0]
    w = w_uq.reshape(rows, NH, QK_NOPE + QK_ROPE)
    return jnp.pad(w, ((0, 0), (0, 0), (0, LANE - QK_NOPE - QK_ROPE))).reshape(rows, NH * LANE)


def _unpad_wuq(g):
    return g.reshape(Q_LORA, NH, LANE)[:, :, :QK_NOPE + QK_ROPE].reshape(Q_LORA, NH * (QK_NOPE + QK_ROPE))


def _pad_wukv(w_ukv):
    heads = w_ukv.shape[1] // (QK_NOPE + V_DIM)
    w = w_ukv.reshape(KV_LORA, heads, QK_NOPE + V_DIM)
    w_k = jnp.pad(w[:, :, :QK_NOPE], ((0, 0), (0, 0), (0, LANE - QK_NOPE))).reshape(KV_LORA, heads * LANE)
    wv = w[:, :, QK_NOPE:].reshape(KV_LORA, heads // 2, 2, 1, V_DIM)
    eye = jnp.eye(2, dtype=w.dtype).reshape(1, 1, 2, 2, 1)
    return w_k, (wv * eye).reshape(KV_LORA, heads * LANE)


def _unpad_wukv(gk, gv):
    gk = gk.reshape(KV_LORA, NH, LANE)[:, :, :QK_NOPE]
    gv = gv.reshape(KV_LORA, NH // 2, 2, 2, V_DIM)
    gv = jnp.stack([gv[:, :, 0, 0], gv[:, :, 1, 1]], axis=2).reshape(KV_LORA, NH, V_DIM)
    return jnp.concatenate([gk, gv], axis=-1).reshape(KV_LORA, NH * (QK_NOPE + V_DIM))


def _local_step(x, tgt, g_pre, w_in_t, b_gate, g_q, g_kv, lb_logits, g_hgrn, g_post, weights, exchange=None):
    s = x.shape[0]
    w_in_p = _pad_win_t(w_in_t)
    rc, rs1, rs2 = _rope_tables(s)
    g_hg = jnp.tile(g_hgrn, (1, NH))

    proj, h = _front_fwd(x, g_pre, w_in_p, weights.tokens)
    w_uq_p, w_k_p, w_v_p = weights.qkv(h)
    q, k, vv = _qkv_fwd(proj, g_q, g_kv, w_uq_p, w_k_p, w_v_p, rc, rs1, rs2)
    attn, lse = _attn_fwd(q, k, vv)
    o_raw, states = _hgrn_fwd(proj, lb_logits)
    wa, wb, w_out = weights.mid(o_raw)
    (loss, dout, dattn, dga, dor, dgb, dmg, d_wout, d_wa, d_wb, d_gpost, d_bgate, d_ghg) = _mid(
        proj, attn, o_raw, x, tgt, g_hg, b_gate, g_post, wa, wb, w_out)
    w_mg, w_ga, w_gb = _win_grad(h, [dmg, dga, dgb], "win_grad_mid")
    dh3, d_lbl = _hgrn_bwd(proj, lb_logits, states, dor)
    (w_h3,) = _win_grad(h, [dh3], "win_grad_hgrn")
    d_win_rest = jnp.concatenate([w_ga, w_h3[0], w_h3[1], w_h3[2], w_gb, w_mg], axis=0)
    early = dict(w_in_rest=d_win_rest, w_branch_a=d_wa, w_branch_b=d_wb, w_out=d_wout)
    token = exchange.start_early(early) if exchange else jnp.zeros((8, LANE), F32)
    dq, dk, dvv = _attn_bwd(q, k, vv, attn, dattn, lse, token)
    dcq, dckv, dkpe, d_wuq_p, d_wk_p, d_wv_p, d_gq, d_gkv = _qkv_bwd(proj, dq, dk, dvv, g_q, g_kv, w_uq_p, w_k_p, w_v_p, rc, rs1, rs2)
    w_cq, w_ckv, w_kpe = _win_grad(h, [dcq, dckv, dkpe], "win_grad_qkv")
    d_win_qkv = jnp.concatenate([w_cq, w_ckv, w_kpe[64:64 + QK_ROPE]], axis=0)
    late = dict(w_in_qkv=d_win_qkv, w_uq=_unpad_wuq(d_wuq_p), w_ukv=_unpad_wukv(d_wk_p, d_wv_p))
    token = exchange.start_late(late) if exchange else jnp.zeros((8, LANE), F32)
    grad_x, d_gpre = _front_bwd(x, dout, dmg, dga, dh3, dgb, dcq, dckv, dkpe, g_pre, w_in_p, token)
    vec_grads = dict(g_pre=d_gpre, b_gate=d_bgate, g_q=d_gq, g_kv=d_gkv, lb_logits=d_lbl, g_hgrn=d_ghg, g_post=d_gpost)
    return loss, grad_x, dict(early, **late), vec_grads


SHARD_SHAPES = (("w_in", (1416, 1024)), ("w_uq", (192, 768)), ("w_ukv", (256, 256)), ("w_branch_a", (512, 256)),
                ("w_branch_b", (512, 256)), ("w_out", (256, 1024)))
BIG = tuple(n for n, _ in SHARD_SHAPES)
ROW_SHARDED = ("w_in", "w_uq", "w_out")
N_CHIPS = 4
QKV_ROWS = Q_LORA + KV_LORA + QK_ROPE
W_IN_FORWARD_CUT = 704


def _to_block(name, a):
    return a[0].T if name == "w_in" else a[0]


def _from_block(name, a):
    return a.T[None] if name == "w_in" else a[None]
VEC_ROWS = (("g_pre", 0, 1024), ("b_gate", 1, 2048), ("g_q", 2, 768), ("g_kv", 3, 256), ("g_hgrn", 6, 64), ("g_post", 7, 1024))
VEC_LB_ROW = 4
VEC_SHAPE = (8, 2048)


def _split_by_chip(name, g):
    a, b = dict(SHARD_SHAPES)[name]
    return g.reshape(N_CHIPS, a, b) if name in ROW_SHARDED else g.reshape(a, N_CHIPS, b).transpose(1, 0, 2)


def _join_chips(name, w):
    a, b = dict(SHARD_SHAPES)[name]
    return w.reshape(N_CHIPS * a, b) if name in ROW_SHARDED else w.transpose(1, 0, 2).reshape(a, N_CHIPS * b)


MESH = pl.DeviceIdType.MESH
HBM = pl.BlockSpec(memory_space=pltpu.HBM)


def _mesh_place():
    x, y, c = lax.axis_index("x"), lax.axis_index("y"), lax.axis_index("c")
    return x, y, c, 2 * x + y, [(1 - x, y), (x, 1 - y), (1 - x, 1 - y)]


def _remote(src, dst, send_sems, recv_sems, k, to):
    return pltpu.make_async_remote_copy(src_ref=src, dst_ref=dst, send_sem=send_sems.at[k], recv_sem=recv_sems.at[k],
                                        device_id=to, device_id_type=MESH)


def _gather_w_in(shard):
    a, b = shard.shape
    cut = W_IN_FORWARD_CUT

    def body(src, out, ici_send, ici_recv, d2d_send, d2d_recv, local_sem):
        x, y, c = lax.axis_index("x"), lax.axis_index("y"), lax.axis_index("c")
        me, xn, yn, dg = 2 * x + y, 2 * (1 - x) + y, 2 * x + (1 - y), 2 * (1 - x) + (1 - y)
        to_x, to_y, sibling = (1 - x, y, c), (x, 1 - y, c), (x, y, 1 - c)
        first, rest = pl.ds(0, cut), pl.ds(cut, a - cut)

        whole = lambda ref, which: ref.at[:, pl.ds(pl.multiple_of(which * (b // 2), b // 2), b // 2)]
        own = pltpu.make_async_copy(src, out.at[me], local_sem)
        own.start()
        sends = [_remote(whole(src, c), whole(out.at[me], c), ici_send, ici_recv, 0, to_x),
                 _remote(whole(src, c), whole(out.at[me], c), ici_send, ici_recv, 1, to_y)]
        for cp in sends:
            cp.start()

        def landed(slot, rows, k, d2d_k, src_dev):
            piece = whole(out.at[slot], c) if rows is None else out.at[slot].at[rows, pl.ds(pl.multiple_of(c * (b // 2), b // 2), b // 2)]
            _remote(piece, piece, ici_send, ici_recv, k, src_dev).wait_recv()
            cp = _remote(piece, piece, d2d_send, d2d_recv, d2d_k, sibling)
            cp.start()
            sends.append(cp)
            return piece

        def pass_on(slot, rows, k, to):
            piece = out.at[slot].at[rows, pl.ds(pl.multiple_of(c * (b // 2), b // 2), b // 2)]
            cp = _remote(piece, piece, ici_send, ici_recv, k, to)
            cp.start()
            sends.append(cp)

        landed(xn, None, 0, 0, to_x)
        pass_on(xn, first, 2, to_y)
        landed(yn, None, 1, 1, to_y)
        pass_on(yn, rest, 3, to_x)
        landed(dg, first, 2, 2, to_y)
        landed(dg, rest, 3, 3, to_x)
        other = pl.ds(pl.multiple_of((1 - c) * (b // 2), b // 2), b // 2)
        for d2d_k, (slot, rows) in enumerate(((xn, None), (yn, None), (dg, first), (dg, rest))):
            piece = out.at[slot].at[:, other] if rows is None else out.at[slot].at[rows, other]
            _remote(piece, piece, d2d_send, d2d_recv, d2d_k, sibling).wait_recv()
        for cp in sends:
            cp.wait_send()
        own.wait()

    sems = pltpu.SemaphoreType.DMA((4,))
    return pl.pallas_call(
        body, name="gather_w_in", in_specs=[HBM], out_specs=HBM,
        out_shape=jax.ShapeDtypeStruct((N_CHIPS, a, b), shard.dtype),
        scratch_shapes=[sems, sems, sems, sems, pltpu.SemaphoreType.DMA],
        compiler_params=pltpu.CompilerParams(has_side_effects=True),
    )(shard)


def _sibling_exchange(srcs, name, after=None):
    n = len(srcs)
    extra = [] if after is None else [after]

    def body(*refs):
        src_refs, outs = refs[:n], refs[n + len(extra):2 * n + len(extra)]
        send_sems, recv_sems = refs[2 * n + len(extra):]
        sibling = (lax.axis_index("x"), lax.axis_index("y"), 1 - lax.axis_index("c"))
        copies = [_remote(src_refs[k], outs[k], send_sems, recv_sems, k, sibling) for k in range(n)]
        for cp in copies:
            cp.start()
        for cp in copies:
            cp.wait()

    sems = pltpu.SemaphoreType.DMA((n,))
    return pl.pallas_call(
        body, name=name, in_specs=[HBM] * n + [pl.BlockSpec(memory_space=pl.ANY)] * len(extra), out_specs=[HBM] * n,
        out_shape=[jax.ShapeDtypeStruct(s.shape, s.dtype) for s in srcs],
        scratch_shapes=[sems, sems],
        compiler_params=pltpu.CompilerParams(has_side_effects=True),
    )(*srcs, *extra)


SEM = pl.BlockSpec(memory_space=pltpu.SEMAPHORE)
DATAFLOW = pltpu.SideEffectType.DATAFLOW_SIDE_EFFECTING


def _exchange_copies(srcs, to_first, src_refs, land_refs, send_sems, recv_sems):
    x, y, c, me, chips = _mesh_place()
    n = len(srcs)
    sends, recvs = [], []
    for k in range(n):
        if k in to_first:
            base = 3 * n + 4 * to_first.index(k)
            sends.append((me != 0, pltpu.make_async_remote_copy(
                src_ref=src_refs[k], dst_ref=land_refs[k].at[me], send_sem=send_sems.at[base], recv_sem=recv_sems.at[base + me],
                device_id=(0, 0, c), device_id_type=MESH)))
            for s in range(1, N_CHIPS):
                recvs.append((me == 0, pltpu.make_async_remote_copy(
                    src_ref=src_refs[k], dst_ref=land_refs[k].at[s], send_sem=send_sems.at[base], recv_sem=recv_sems.at[base + s],
                    device_id=(s // 2, s % 2, c), device_id_type=MESH)))
        else:
            slab = (lambda t, k=k: src_refs[k]) if srcs[k].ndim == 2 else (lambda t, k=k: src_refs[k].at[t])
            for j, (px, py) in enumerate(chips):
                sends.append((None, _remote(slab(2 * px + py), land_refs[k].at[me], send_sems, recv_sems, 3 * k + j, (px, py, c))))
                recvs.append((None, _remote(slab(me), land_refs[k].at[2 * px + py], send_sems, recv_sems, 3 * k + j, (px, py, c))))
    return sends, recvs


def _when(pred, fn):
    if pred is None:
        fn()
    else:
        pl.when(pred)(fn)


def _exchange_start(srcs, to_first, name, after=None):
    n = len(srcs)
    n_sems = 3 * n + 4 * len(to_first)
    lands = [lax.empty((N_CHIPS,) + s.shape[-2:], s.dtype) for s in srcs]
    extra = [] if after is None else [after]

    def body(*refs):
        src_refs, land_refs = refs[:n], refs[n:2 * n]
        send_sems, recv_sems, token = refs[2 * n + len(extra)], refs[2 * n + len(extra) + 1], refs[-1]
        sends, _ = _exchange_copies(srcs, to_first, src_refs, land_refs, send_sems, recv_sems)
        for pred, cp in sends:
            _when(pred, cp.start)
        token[...] = jnp.zeros_like(token)

    hbm = lambda a: pltpu.HBM(a.shape, a.dtype)
    res = pl.pallas_call(
        body, name=name,
        out_shape=[pltpu.SemaphoreType.DMA((n_sems,)), pltpu.SemaphoreType.DMA((n_sems,))] + [hbm(a) for a in srcs + lands]
        + [jax.ShapeDtypeStruct((8, LANE), F32)],
        in_specs=[HBM] * (2 * n) + [pl.BlockSpec(memory_space=pl.ANY)] * len(extra),
        out_specs=[SEM, SEM] + [HBM] * (2 * n) + [pl.BlockSpec(memory_space=pltpu.VMEM)],
        input_output_aliases={i: 2 + i for i in range(2 * n)},
        compiler_params=pltpu.CompilerParams(has_side_effects=DATAFLOW),
    )(*[pltpu.with_memory_space_constraint(a, pltpu.HBM) for a in srcs + lands], *extra)
    return res[:-1], res[-1]


def _exchange_wait(srcs, to_first, started, after, name):
    n = len(srcs)
    send_sems, recv_sems, thru = started[0], started[1], started[2:]

    def body(*refs):
        src_refs, land_refs, send_ref, recv_ref = refs[:n], refs[n:2 * n], refs[2 * n], refs[2 * n + 1]
        sends, recvs = _exchange_copies(srcs, to_first, src_refs, land_refs, send_ref, recv_ref)
        for pred, cp in sends:
            _when(pred, cp.wait_send)
        for pred, cp in recvs:
            _when(pred, cp.wait_recv)

    res = pl.pallas_call(
        body, name=name, out_shape=[pltpu.HBM(a.shape, a.dtype) for a in thru],
        in_specs=[HBM] * (2 * n) + [SEM, SEM, pl.BlockSpec(memory_space=pl.ANY)], out_specs=[HBM] * (2 * n),
        input_output_aliases={i: i for i in range(2 * n)},
        compiler_params=pltpu.CompilerParams(has_side_effects=DATAFLOW),
    )(*thru, send_sems, recv_sems, after)
    return res[n:]


ROW_TILE = 256
COL_TILE = 256


def _block_tiling(a, b):
    if a <= ROW_TILE or a % ROW_TILE == 0:
        ta = min(a, ROW_TILE)
        return a // ta, (ta, b), lambda i: (i, 0)
    return b // COL_TILE, (a, COL_TILE), lambda i: (0, i)


def _sum_landed(land, own, name, first_land=None, first_own=None):
    _, a, b = land.shape
    steps, tile, at = _block_tiling(a, b)
    extra = first_land is not None

    def body(*refs):
        p_ref, own_ref, o_ref = refs[0], refs[1], refs[-1]
        me = 2 * lax.axis_index("x") + lax.axis_index("y")
        own = own_ref[...].astype(F32)
        slot = lambda t: jnp.where(me == t, own, p_ref[t].astype(F32))
        o_ref[...] = ((slot(0) + slot(1)) + slot(2)) + slot(3)
        if extra:
            fp_ref, fo_ref = refs[2], refs[3]
            r = fo_ref.shape[0]

            @pl.when(me == 0)
            def _():
                f = lambda t: fp_ref[t].astype(F32)
                rows = pl.ds(pl.multiple_of(lax.axis_index("c") * r, 8), r)
                o_ref[rows, :] += ((fo_ref[...].astype(F32) + f(1)) + f(2)) + f(3)

    in_specs = [pl.BlockSpec((N_CHIPS,) + tile, lambda i: (0,) + at(i)), pl.BlockSpec(tile, at)]
    args = [land, own]
    if extra:
        r = first_own.shape[0]
        assert tile[0] == a, "the extra rows need whole columns in a step"
        in_specs += [pl.BlockSpec((N_CHIPS, r, tile[1]), lambda i: (0,) + at(i)), pl.BlockSpec((r, tile[1]), at)]
        args += [first_land, first_own]
    return pl.pallas_call(
        body, name=name, grid=(steps,), in_specs=in_specs, out_specs=pl.BlockSpec(tile, at),
        out_shape=jax.ShapeDtypeStruct((a, b), F32), compiler_params=_params(("parallel",)),
    )(*args)


def _add_cast(a, b, name):
    def body(a_ref, b_ref, o_ref):
        o_ref[...] = (a_ref[...] + b_ref[...]).astype(BF16)

    return pl.pallas_call(body, name=name, out_shape=jax.ShapeDtypeStruct(a.shape, BF16),
                          compiler_params=_params(()))(a, b)


class _LaterWeights:
    MID = ("w_branch_a", "w_branch_b", "w_out")

    def __init__(self, blocks, after):
        self.qkv_blocks = [_pad_wuq(blocks["w_uq"]), *_pad_wukv(blocks["w_ukv"])]
        self.mid_blocks = [blocks[n] for n in self.MID]
        self.qkv_started, t1 = _exchange_start(self.qkv_blocks, (), "weights_qkv_start", after)
        self.mid_started, t2 = _exchange_start(self.mid_blocks, (), "weights_mid_start", after)
        self.tokens = [t1, t2]

    @staticmethod
    def _whole(blocks, rows_sharded, started, after, name):
        landed = _exchange_wait(blocks, (), started, after, name)
        me = 2 * lax.axis_index("x") + lax.axis_index("y")
        out = []
        for block, land, by_rows in zip(blocks, landed, rows_sharded):
            w = lax.dynamic_update_index_in_dim(land, block, me, 0)
            a, b = block.shape
            out.append(w.reshape(N_CHIPS * a, b) if by_rows else w.transpose(1, 0, 2).reshape(a, N_CHIPS * b))
        return out

    def qkv(self, after):
        return self._whole(self.qkv_blocks, (True, False, False), self.qkv_started, after, "weights_qkv_wait")

    def mid(self, after):
        return self._whole(self.mid_blocks, (False, False, True), self.mid_started, after, "weights_mid_wait")


class _GradExchange:
    EARLY = ("w_in", "w_branch_a", "w_branch_b", "w_out")
    LATE = ("w_uq", "w_ukv")

    def __init__(self, state):
        self.state = state
        self.outs = {}

    @staticmethod
    def _own(slabs):
        return lax.dynamic_index_in_dim(slabs, 2 * lax.axis_index("x") + lax.axis_index("y"), axis=0, keepdims=False)

    def start_early(self, g):
        full = jnp.concatenate([jnp.zeros((QKV_ROWS, D), F32), g["w_in_rest"]], axis=0)
        g = dict(g, w_in=full)
        self.early = [_split_by_chip(n, g[n]).astype(BF16) for n in self.EARLY]
        self.early_started, token = _exchange_start(self.early, (), "grads_early_start")
        return token

    def start_late(self, g):
        self.early_landed = _exchange_wait(self.early, (), self.early_started, g["w_uq"], "grads_early_wait")
        half = QKV_ROWS // 2
        c = lax.axis_index("c")
        mine = lax.dynamic_slice_in_dim(g["w_in_qkv"], c * half, half, axis=0)
        (theirs,) = _sibling_exchange([lax.dynamic_slice_in_dim(g["w_in_qkv"], (1 - c) * half, half, axis=0)], "sibling_qkv_rows")
        self.late = [_split_by_chip(n, g[n]).astype(BF16) for n in self.LATE] + [_add_cast(mine, theirs, "add_qkv_rows")]
        self.late_started, token = _exchange_start(self.late, (2,), "grads_late_start")
        names = self.EARLY[1:]
        mine = [_sum_landed(land, self._own(slabs), "sum_" + n) for n, slabs, land in list(zip(self.EARLY, self.early, self.early_landed))[1:]]
        theirs = _sibling_exchange(mine, "sibling_early", after=token)
        for n, a, b in zip(names, mine, theirs):
            self.outs[n] = _adamw(a, b, *self.state[n], "adamw_" + n)
        return self.outs[names[-1]][0]

    def finish(self, after):
        late_landed = _exchange_wait(self.late, (2,), self.late_started, after, "grads_late_wait")
        sums = {"w_in": _sum_landed(self.early_landed[0], self._own(self.early[0]), "sum_w_in",
                                    first_land=late_landed[2], first_own=self.late[2])}
        for n, slabs, land in zip(self.LATE, self.late, late_landed):
            sums[n] = _sum_landed(land, self._own(slabs), "sum_" + n)
        return sums


def _adamw_math(g, w, m, v):
    nm = ADAM_B1 * m + (1.0 - ADAM_B1) * g
    nv = ADAM_B2 * v + (1.0 - ADAM_B2) * (g * g)
    m_hat = nm / (1.0 - ADAM_B1 ** ADAM_STEP)
    v_hat = nv / (1.0 - ADAM_B2 ** ADAM_STEP)
    return -ADAM_LR * (m_hat / (jnp.sqrt(v_hat) + ADAM_EPS) + ADAM_WD * w), nm, nv


def _adamw(p_mine, p_sibling, w, m, v, name):
    a, b = p_mine.shape
    steps, tile, at = _block_tiling(a, b)

    def body(a_ref, b_ref, w_ref, m_ref, v_ref, g_ref, d_ref, nm_ref, nv_ref):
        g = a_ref[...] + b_ref[...]
        g_ref[...] = g
        d_ref[...], nm_ref[...], nv_ref[...] = _adamw_math(g, w_ref[...], m_ref[...], v_ref[...])

    spec = pl.BlockSpec(tile, at)
    sds = jax.ShapeDtypeStruct((a, b), F32)
    return pl.pallas_call(
        body, name=name, grid=(steps,), in_specs=[spec] * 5, out_specs=[spec] * 4, out_shape=[sds] * 4,
        compiler_params=_params(("parallel",)),
    )(p_mine, p_sibling, w, m, v)


LOSS_AT = (2, 1024)


def _vec_pack(vg, loss):
    names = [n for n, _, _ in VEC_ROWS]

    def body(*refs):
        o_ref = refs[-1]
        lb_ref, loss_ref = refs[len(names)], refs[len(names) + 1]
        o_ref[...] = jnp.zeros_like(o_ref)
        o_ref[LOSS_AT[0]:LOSS_AT[0] + 1, LOSS_AT[1]:LOSS_AT[1] + LANE] = jnp.broadcast_to(loss_ref[...], (1, LANE))
        for (name, row, size), ref in zip(VEC_ROWS, refs):
            if name == "g_hgrn":
                r = lax.broadcasted_iota(jnp.int32, (NH * V_DIM, LANE), 0)
                c = lax.broadcasted_iota(jnp.int32, (NH * V_DIM, LANE), 1)
                fold = ((r % V_DIM) == c).astype(F32)
                o_ref[row:row + 1, 0:LANE] = jnp.dot(ref[...], fold, precision=HIGHEST, preferred_element_type=F32)
            else:
                o_ref[row:row + 1, 0:size] = ref[...]
        o_ref[VEC_LB_ROW:VEC_LB_ROW + 2, 0:512] = lb_ref[...]

    return pl.pallas_call(body, name="vec_pack", out_shape=jax.ShapeDtypeStruct(VEC_SHAPE, F32))(
        *[vg[n] for n in names], vg["lb_logits"], loss)


def _adamw_vec(p_mine, p_sibling, w, m, v):
    names = [n for n, _, _ in VEC_ROWS] + ["lb_logits"]
    k = len(names)

    def body(a_ref, b_ref, *refs):
        ins, outs = refs[:3 * k], refs[3 * k:]
        at = (slice(LOSS_AT[0], LOSS_AT[0] + 1), slice(LOSS_AT[1], LOSS_AT[1] + LANE))
        outs[-1][...] = a_ref[at] + b_ref[at]
        for i, name in enumerate(names):
            if name == "lb_logits":
                rows, cols = slice(VEC_LB_ROW, VEC_LB_ROW + 2), slice(0, 512)
            else:
                _, row, size = VEC_ROWS[i]
                rows, cols = slice(row, row + 1), slice(0, size)
            g = a_ref[rows, cols] + b_ref[rows, cols]
            d, nm, nv = _adamw_math(g, ins[i][...], ins[k + i][...], ins[2 * k + i][...])
            for o_ref, val in zip(outs[4 * i:4 * i + 4], (g, d, nm, nv)):
                o_ref[...] = val

    shapes = [jax.ShapeDtypeStruct(w[n].shape, F32) for n in names for _ in range(4)] + [jax.ShapeDtypeStruct((1, LANE), F32)]
    res = pl.pallas_call(body, name="adamw_vec", out_shape=shapes)(
        p_mine, p_sibling, *[w[n] for n in names], *[m[n] for n in names], *[v[n] for n in names])
    return [{n: res[4 * i + j] for i, n in enumerate(names)} for j in range(4)], res[-1]


WEIGHTS = ("g_pre", "w_in", "b_gate", "g_q", "w_uq", "g_kv", "w_ukv", "lb_logits", "g_hgrn", "w_branch_a", "w_branch_b", "w_out", "g_post")


def kernel(x, g_pre, w_in, b_gate, g_q, w_uq, g_kv, w_ukv, lb_logits, g_hgrn, w_branch_a, w_branch_b, w_out, g_post, loss_target, m_g_pre, m_w_in, m_b_gate, m_g_q, m_w_uq, m_g_kv, m_w_ukv, m_lb_logits, m_g_hgrn, m_w_branch_a, m_w_branch_b, m_w_out, m_g_post, v_g_pre, v_w_in, v_b_gate, v_g_q, v_w_uq, v_g_kv, v_w_ukv, v_lb_logits, v_g_hgrn, v_w_branch_a, v_w_branch_b, v_w_out, v_g_post):
    w = dict(g_pre=g_pre, w_in=w_in, b_gate=b_gate, g_q=g_q, w_uq=w_uq, g_kv=g_kv, w_ukv=w_ukv, lb_logits=lb_logits, g_hgrn=g_hgrn,
             w_branch_a=w_branch_a, w_branch_b=w_branch_b, w_out=w_out, g_post=g_post)
    m = dict(g_pre=m_g_pre, w_in=m_w_in, b_gate=m_b_gate, g_q=m_g_q, w_uq=m_w_uq, g_kv=m_g_kv, w_ukv=m_w_ukv, lb_logits=m_lb_logits,
             g_hgrn=m_g_hgrn, w_branch_a=m_w_branch_a, w_branch_b=m_w_branch_b, w_out=m_w_out, g_post=m_g_post)
    v = dict(g_pre=v_g_pre, w_in=v_w_in, b_gate=v_b_gate, g_q=v_g_q, w_uq=v_w_uq, g_kv=v_g_kv, w_ukv=v_w_ukv, lb_logits=v_lb_logits,
             g_hgrn=v_g_hgrn, w_branch_a=v_w_branch_a, w_branch_b=v_w_branch_b, w_out=v_w_out, g_post=v_g_post)
    blocks = {n: _to_block(n, w[n]).astype(BF16) for n in BIG}
    w_in_all = _gather_w_in(blocks["w_in"])
    weights = _LaterWeights(blocks, w_in_all)
    state = {n: [_to_block(n, t[n]) for t in (w, m, v)] for n in BIG}
    exchange = _GradExchange(state)
    loss, grad_x, _, vec_grads = _local_step(
        x[0], loss_target[0], g_pre, _join_chips("w_in", w_in_all), b_gate, g_q, g_kv, lb_logits, g_hgrn, g_post, weights, exchange)
    vec = _vec_pack(vec_grads, loss)
    vec_started, token = _exchange_start([vec], (), "vec_start")
    sums = exchange.finish(token)
    rest = tuple(sums)
    (vec_landed,) = _exchange_wait([vec], (), vec_started, sums[rest[-1]], "vec_wait")
    mine = [sums[n] for n in rest] + [_sum_landed(vec_landed, vec, "sum_vec")]
    theirs = _sibling_exchange(mine, "sibling_grads")
    done = dict(exchange.outs)
    for k, n in enumerate(rest):
        done[n] = _adamw(mine[k], theirs[k], *state[n], "adamw_" + n)
    outs = [{}, {}, {}, {}]
    for n in BIG:
        for o, val in zip(outs, done[n]):
            o[n] = _from_block(n, val)
    vec_outs, total = _adamw_vec(mine[-1], theirs[-1], w, m, v)
    for o, vals in zip(outs, vec_outs):
        o.update(vals)
    return (total[0, 0], grad_x[None], *[o[n] for o in outs for n in WEIGHTS])
```

```python
import math

import jax
import jax.numpy as jnp
from jax import lax
from jax.experimental import pallas as pl
from jax.experimental.pallas import tpu as pltpu

F32 = jnp.float32
BF16 = jnp.bfloat16
HIGHEST = lax.Precision.HIGHEST

D = 1024
NH = 8
QK_NOPE, QK_ROPE, V_DIM = 64, 32, 64
Q_LORA, KV_LORA = 768, 256
CHUNK = 64
HG_BLOCK = 32
EPS = 1e-6
LANE = 128
P_MERGE, P_GA, P_HQ, P_HF, P_HI, P_GB, P_CQ, P_CKV, P_KPE = 0, 2048, 2560, 3072, 3584, 4096, 4608, 5376, 5632
D_P = 5760
O_CQ, O_CKV, O_KPE, O_GA, O_HQ, O_HF, O_HI, O_GB, O_MERGE = 0, 768, 1024, 1056, 1568, 2080, 2592, 3104, 3616

TM = 512
TM_MID = 256
TQ = 1024
ONES_LANE = (LANE - 1, 0)
TH = 256
HG_PAIRS = 4
VMEM_LIMIT = 56 * 1024 * 1024

ADAM_LR, ADAM_B1, ADAM_B2, ADAM_EPS, ADAM_WD, ADAM_STEP = 0.001, 0.9, 0.999, 1e-08, 0.01, 10

NT_DIMS = (((1,), (1,)), ((), ()))
TN_DIMS = (((0,), (0,)), ((), ()))


def _params(sem):
    return pltpu.CompilerParams(dimension_semantics=sem, vmem_limit_bytes=VMEM_LIMIT)


def _mm(a, b):
    return jnp.dot(a, b, preferred_element_type=F32)


def _mm_nt(a, b):
    return lax.dot_general(a, b, NT_DIMS, preferred_element_type=F32)


def _mm_tn(a, b):
    return lax.dot_general(a, b, TN_DIMS, preferred_element_type=F32)


def _sigmoid(z):
    return jax.nn.sigmoid(z)


def _rope(v, c, s1, s2):
    return v * c + pltpu.roll(v, 112, 1) * s1 + pltpu.roll(v, 16, 1) * s2


def _rope_t(dy, c, s1, s2):
    return dy * c + pltpu.roll(dy * s1, 16, 1) + pltpu.roll(dy * s2, 112, 1)


def _rope_tables(s):
    inv = 10000.0 ** (-jnp.arange(0, QK_ROPE, 2, dtype=F32) / QK_ROPE)
    ang = jnp.arange(s, dtype=F32)[:, None] * inv[None, :]
    cos, sin = jnp.cos(ang), jnp.sin(ang)
    z64, z32, o64, o32 = jnp.zeros((s, 64), F32), jnp.zeros((s, 32), F32), jnp.ones((s, 64), F32), jnp.ones((s, 32), F32)
    z16 = jnp.zeros((s, 16), F32)
    c = jnp.concatenate([o64, cos, cos, o32], axis=1)
    s1 = jnp.concatenate([z64, -sin, z16, z32], axis=1)
    s2 = jnp.concatenate([z64, z16, sin, z32], axis=1)
    return c, s1, s2


def _front_fwd(x, g_pre, w_in_pt, tokens=()):
    s = x.shape[0]
    tokens = list(tokens)

    def body(x_ref, g_ref, w_ref, *refs):
        o_ref, h_ref = refs[len(tokens):]
        xv = x_ref[...]
        r = lax.rsqrt(jnp.mean(xv * xv, axis=-1, keepdims=True) + EPS)
        h = ((xv * r) * g_ref[...]).astype(BF16)
        h_ref[...] = h
        o_ref[...] = _mm_nt(h, w_ref[...])

    return pl.pallas_call(
        body, name="front_fwd", grid=(s // TM,),
        in_specs=[pl.BlockSpec((TM, D), lambda i: (i, 0)), pl.BlockSpec((1, D), lambda i: (0, 0)),
                  pl.BlockSpec((D_P, D), lambda i: (0, 0))] + [pl.BlockSpec((8, LANE), lambda i: (0, 0))] * len(tokens),
        out_specs=[pl.BlockSpec((TM, D_P), lambda i: (i, 0)), pl.BlockSpec((TM, D), lambda i: (i, 0))],
        out_shape=[jax.ShapeDtypeStruct((s, D_P), F32), jax.ShapeDtypeStruct((s, D), BF16)],
        compiler_params=_params(("parallel",)),
    )(x, g_pre, w_in_pt, *tokens)


def _norm_rows(v, g):
    r = lax.rsqrt(jnp.mean(v * v, axis=-1, keepdims=True) + EPS)
    return (v * r) * g, r


def _qkv_fwd(proj, g_q, g_kv, w_uq_p, w_k_p, w_v_p, rc, rs1, rs2):
    s = proj.shape[0]

    def body(cq_ref, ckv_ref, kpe_ref, gq_ref, gkv_ref, wq_ref, wk_ref, wv_ref, c_ref, s1_ref, s2_ref, q_ref, k_ref, v_ref):
        c, s1, s2 = c_ref[...], s1_ref[...], s2_ref[...]
        cqn, _ = _norm_rows(cq_ref[...], gq_ref[...])
        ckvn, _ = _norm_rows(ckv_ref[...], gkv_ref[...])
        ckvn = ckvn.astype(BF16)
        qf = _mm(cqn.astype(BF16), wq_ref[...])
        kf = _mm(ckvn, wk_ref[...])
        vf = _mm(ckvn, wv_ref[...])
        kpe = _rope(kpe_ref[...], c, s1, s2)
        lane = lax.broadcasted_iota(jnp.int32, (TM, LANE), 1)
        for h in range(NH):
            blk = slice(h * LANE, (h + 1) * LANE)
            q_ref[h] = _rope(qf[:, blk], c, s1, s2).astype(BF16)
            k_ref[h] = (kf[:, blk] + kpe).astype(BF16)
            v_ref[h] = jnp.where(lane == ONES_LANE[h % 2], 1.0, vf[:, blk]).astype(BF16)

    row = lambda w, j: pl.BlockSpec((TM, w), lambda i: (i, j))
    full = lambda a: pl.BlockSpec(a.shape, lambda i: (0,) * a.ndim)
    hs = jax.ShapeDtypeStruct((NH, s, LANE), BF16)
    return pl.pallas_call(
        body, name="qkv_fwd", grid=(s // TM,),
        in_specs=[row(Q_LORA, P_CQ // Q_LORA), row(KV_LORA, P_CKV // KV_LORA), row(LANE, P_KPE // LANE),
                  full(g_q), full(g_kv), full(w_uq_p), full(w_k_p), full(w_v_p), row(LANE, 0), row(LANE, 0), row(LANE, 0)],
        out_specs=[pl.BlockSpec((NH, TM, LANE), lambda i: (0, i, 0))] * 3,
        out_shape=[hs, hs, hs],
        compiler_params=_params(("parallel",)),
    )(proj, proj, proj, g_q, g_kv, w_uq_p, w_k_p, w_v_p, rc, rs1, rs2)


LOG2E = 1.4426950408889634
QK_SCALE2 = LOG2E / math.sqrt(QK_NOPE + QK_ROPE)


HQ = TQ // 2


def _diag_visible(n):
    row = lax.broadcasted_iota(jnp.int32, (n, n), 0)
    col = lax.broadcasted_iota(jnp.int32, (n, n), 1)
    return (col // CHUNK) <= (row // CHUNK)


def _attn_fwd(q, k, vv):
    s = q.shape[1]

    def body(q_ref, k_ref, v_ref, o_ref, lse_ref):
        i = pl.program_id(1)
        qs = (q_ref[0], q_ref[1])

        def tiles(t, carry, diag):
            rows = pl.ds(pl.multiple_of(t * TQ, TQ), TQ)
            sc = [_mm_nt(qs[hh], k_ref[hh, rows, :]) for hh in range(2)]
            if diag:
                sc = [jnp.where(_diag_visible(TQ), s_, -jnp.inf) for s_ in sc]
            m_new = [jnp.maximum(carry[hh][0], jnp.max(sc[hh], axis=-1, keepdims=True)) for hh in range(2)]
            alpha = [jnp.exp2((carry[hh][0] - m_new[hh]) * QK_SCALE2) for hh in range(2)]
            p = [jnp.exp2((sc[hh] - m_new[hh]) * QK_SCALE2).astype(BF16) for hh in range(2)]
            acc = [alpha[hh] * carry[hh][1] + _mm(p[hh], v_ref[hh, rows, :]) for hh in range(2)]
            return (m_new[0], acc[0]), (m_new[1], acc[1])

        init = (jnp.full((TQ, 1), -jnp.inf, F32), jnp.zeros((TQ, LANE), F32))
        carry = lax.fori_loop(0, i, lambda t, c: tiles(t, c, False), (init, init))
        carry = tiles(i, carry, True)
        lane = lax.broadcasted_iota(jnp.int32, (TQ, LANE), 1)
        out = jnp.zeros((TQ, LANE), F32)
        for hh in range(2):
            m, acc = carry[hh]
            l = jnp.sum(jnp.where(lane == ONES_LANE[hh], acc, 0.0), axis=-1, keepdims=True)
            out = out + jnp.where((lane < V_DIM) == (hh == 0), acc, 0.0) / l
            lse_ref[hh] = jnp.broadcast_to(m * QK_SCALE2 + jnp.log(l) * LOG2E, (TQ, LANE))
        o_ref[...] = out

    return pl.pallas_call(
        body, name="attn_fwd", grid=(NH // 2, s // TQ),
        in_specs=[pl.BlockSpec((2, TQ, LANE), lambda p, i: (p, i, 0)), pl.BlockSpec((2, s, LANE), lambda p, i: (p, 0, 0)),
                  pl.BlockSpec((2, s, LANE), lambda p, i: (p, 0, 0))],
        out_specs=[pl.BlockSpec((TQ, LANE), lambda p, i: (i, p)), pl.BlockSpec((2, TQ, LANE), lambda p, i: (p, i, 0))],
        out_shape=[jax.ShapeDtypeStruct((s, NH * V_DIM), F32), jax.ShapeDtypeStruct((NH, s, LANE), F32)],
        compiler_params=_params(("parallel", "parallel")),
    )(q, k, vv)


def _lower_bound(lbl):
    a0, a1 = lbl[0:1, :], lbl[1:2, :]
    mx = jnp.maximum(a0, a1)
    e0, e1 = jnp.exp(a0 - mx), jnp.exp(a1 - mx)
    return e0 / (e0 + e1)


def _chunk_cumsum(v, reverse=False):
    pos = lax.broadcasted_iota(jnp.int32, v.shape, 0) % HG_BLOCK
    s = 1
    while s < HG_BLOCK:
        if reverse:
            v = v + jnp.where(pos < HG_BLOCK - s, pltpu.roll(v, TH - s, 0), 0.0)
        else:
            v = v + jnp.where(pos >= s, pltpu.roll(v, s, 0), 0.0)
        s *= 2
    return v


def _hgrn_gates(hq, hf, lb):
    sig = _sigmoid(hf)
    f = lb + (1.0 - lb) * sig
    g = jnp.log(f)
    kk = 1.0 - f
    r = lax.broadcasted_iota(jnp.int32, (TH, TH), 0)
    c = lax.broadcasted_iota(jnp.int32, (TH, TH), 1)
    tri = ((r // HG_BLOCK) == (c // HG_BLOCK)) & (r >= c)
    cum = _chunk_cumsum(g)
    nch = TH // HG_BLOCK
    total = _chunks(cum)[:, HG_BLOCK - 1:HG_BLOCK, :]
    lastb = jnp.broadcast_to(total, (nch, HG_BLOCK, hf.shape[-1])).reshape(hf.shape)
    e, ei, ee = jnp.exp(cum), jnp.exp(-cum), jnp.exp(lastb - cum)
    return dict(sig=sig, f=f, kk=kk, tri=tri, cum=cum, total=total, decay=jnp.exp(total), e=e, ei=ei, ee=ee,
                qd=hq * e, ki=kk * ei, ke=kk * ee)


def _chunks(v):
    return v.reshape(TH // HG_BLOCK, HG_BLOCK, v.shape[-1])


def _bmm_nt(a, b):
    return lax.dot_general(a, b, (((2,), (2,)), ((0,), (0,))), preferred_element_type=F32)


def _bmm_nn(a, b):
    return lax.dot_general(a, b, (((2,), (1,)), ((0,), (0,))), preferred_element_type=F32)


def _bmm_tn(a, b):
    return lax.dot_general(a, b, (((1,), (1,)), ((0,), (0,))), preferred_element_type=F32)


def _pair_masks():
    lane = lax.broadcasted_iota(jnp.int32, (TH, LANE), 1)
    kr = lax.broadcasted_iota(jnp.int32, (LANE, LANE), 0)
    kc = lax.broadcasted_iota(jnp.int32, (LANE, LANE), 1)
    return lane < 64, (kr // 64) == (kc // 64)


def _hgrn_fwd(proj, lbl):
    s = proj.shape[0]
    nch = TH // HG_BLOCK

    def body(hq_ref, hf_ref, hi_ref, lbl_ref, o_ref, st_ref, st):
        @pl.when(pl.program_id(1) == 0)
        def _():
            st[...] = jnp.zeros_like(st)

        m0, bd = _pair_masks()
        gt = _hgrn_gates(hq_ref[...], hf_ref[...], _lower_bound(lbl_ref[...]))
        v_b, qd, qd_b = hi_ref[...].astype(BF16), gt["qd"], gt["qd"].astype(BF16)
        ki_b, ke_b = gt["ki"].astype(BF16), gt["ke"].astype(BF16)
        pairs = [slice(u * LANE, (u + 1) * LANE) for u in range(HG_PAIRS)]
        heads = [(lanes, m0 if hh == 0 else jnp.logical_not(m0)) for lanes in pairs for hh in range(2)]
        a_b = [jnp.where(gt["tri"], _mm_nt(jnp.where(mh, qd[:, lanes], 0.0).astype(BF16), ki_b[:, lanes]), 0.0).astype(BF16)
               for lanes, mh in heads]
        intra = [jnp.where(m0, _mm(a_b[2 * u], v_b[:, lanes]), _mm(a_b[2 * u + 1], v_b[:, lanes])) for u, lanes in enumerate(pairs)]
        upd = [_bmm_tn(_chunks(v_b[:, lanes]), _chunks(ke_b[:, lanes])) for lanes in pairs]
        entering = []
        for u, lanes in enumerate(pairs):
            cur, states = st[u], []
            for n in range(nch):
                states.append(cur)
                cur = gt["decay"][n][:, lanes] * cur + jnp.where(bd, upd[u][n], 0.0)
            st[u] = cur
            entering.append(jnp.stack(states))
            st_ref[u] = entering[u]
        for u, lanes in enumerate(pairs):
            o_ref[:, lanes] = intra[u] + _bmm_nt(_chunks(qd_b[:, lanes]), entering[u].astype(BF16)).reshape(TH, LANE)

    wide = HG_PAIRS * LANE
    col = lambda base: pl.BlockSpec((TH, wide), lambda p, i: (i, base // wide + p))
    return pl.pallas_call(
        body, name="hgrn_fwd", grid=(NH // 2 // HG_PAIRS, s // TH),
        in_specs=[col(P_HQ), col(P_HF), col(P_HI), pl.BlockSpec((2, wide), lambda p, i: (0, p))],
        out_specs=[pl.BlockSpec((TH, wide), lambda p, i: (i, p)),
                   pl.BlockSpec((HG_PAIRS, nch, LANE, LANE), lambda p, i: (p, i, 0, 0))],
        out_shape=[jax.ShapeDtypeStruct((s, 512), F32), jax.ShapeDtypeStruct((NH // 2, s // HG_BLOCK, LANE, LANE), F32)],
        scratch_shapes=[pltpu.VMEM((HG_PAIRS, LANE, LANE), F32)],
        compiler_params=_params(("parallel", "arbitrary")),
    )(proj, proj, proj, lbl)


def _group_sum(v):
    low = lax.broadcasted_iota(jnp.int32, (v.shape[0], LANE), 1) < V_DIM
    blocks = []
    for b in range(v.shape[1] // LANE):
        blk = v[:, b * LANE:(b + 1) * LANE]
        s_low = jnp.sum(jnp.where(low, blk, 0.0), axis=-1, keepdims=True)
        s_high = jnp.sum(jnp.where(low, 0.0, blk), axis=-1, keepdims=True)
        blocks.append(jnp.where(low, s_low, s_high))
    return jnp.concatenate(blocks, axis=1)


def _dsilu(z, sg):
    return sg * (1.0 + z * (1.0 - sg))


def _mid(proj, attn, o_raw, x, tgt, g_hg, b_gate, g_post, wa, wb, w_out):
    s = x.shape[0]

    def body(attn_ref, ga_ref, o_ref, gb_ref, mg_ref, x_ref, t_ref, ghg_ref, bg_ref, gp_ref, wa_ref, wb_ref, wo_ref,
             loss_ref, dout_ref, dattn_ref, dga_ref, dor_ref, dgb_ref, dmg_ref, dwo_ref, dwa_ref, dwb_ref, dgp_ref, dbg_ref, dghg_ref):
        @pl.when(pl.program_id(0) == 0)
        def _():
            for rf in (loss_ref, dwo_ref, dwa_ref, dwb_ref, dgp_ref, dbg_ref, dghg_ref):
                rf[...] = jnp.zeros_like(rf)

        attn, za, orw, zb = attn_ref[...], ga_ref[...], o_ref[...], gb_ref[...]
        ghg, gp = ghg_ref[...], gp_ref[...]
        sga, sgb = _sigmoid(za), _sigmoid(zb)
        sa, sb = za * sga, zb * sgb
        ga = attn * sa
        rh = lax.rsqrt(_group_sum(orw * orw) * (1.0 / V_DIM) + EPS)
        on = (orw * rh) * ghg
        gb = on * sb
        ga_b, gb_b = ga.astype(BF16), gb.astype(BF16)
        ya = _mm(ga_b, wa_ref[...])
        yb = _mm(gb_b, wb_ref[...])
        gates = _sigmoid(mg_ref[...] + bg_ref[...])
        g0, g1 = gates[:, :D], gates[:, D:]
        m_b = (g0 * ya + g1 * yb).astype(BF16)
        y = _mm(m_b, wo_ref[...])
        ry = lax.rsqrt(jnp.mean(y * y, axis=-1, keepdims=True) + EPS)
        out = x_ref[...] + (y * ry) * gp
        err = out - t_ref[...]
        loss_ref[...] += 0.5 * jnp.sum(jnp.mean(err * err, axis=-1, keepdims=True), axis=0, keepdims=True)
        dout = err * (1.0 / D)
        dout_ref[...] = dout
        dgp_ref[...] += jnp.sum(dout * (y * ry), axis=0, keepdims=True)
        dgy = dout * gp
        dy = ry * dgy - y * (ry * ry * ry) * jnp.mean(y * dgy, axis=-1, keepdims=True)
        dy_b = dy.astype(BF16)
        dm = _mm_nt(dy_b, wo_ref[...])
        dya_b, dyb_b = (dm * g0).astype(BF16), (dm * g1).astype(BF16)
        dga = _mm_nt(dya_b, wa_ref[...])
        dgb = _mm_nt(dyb_b, wb_ref[...])
        dwo_ref[...] += _mm_tn(m_b, dy_b)
        dwa_ref[...] += _mm_tn(ga_b, dya_b)
        dwb_ref[...] += _mm_tn(gb_b, dyb_b)
        dg0, dg1 = dm * ya, dm * yb
        dmg = jnp.concatenate([dg0 * g0 * (1.0 - g0), dg1 * g1 * (1.0 - g1)], axis=1)
        dmg_ref[...] = dmg.astype(BF16)
        dbg_ref[...] += jnp.sum(dmg, axis=0, keepdims=True)
        dattn_ref[...] = dga * sa
        dga_ref[...] = (dga * attn * _dsilu(za, sga)).astype(BF16)
        dgb_ref[...] = (dgb * on * _dsilu(zb, sgb)).astype(BF16)
        don = dgb * sb
        dghg_ref[...] += jnp.sum(don * (orw * rh), axis=0, keepdims=True)
        dgo = don * ghg
        dor_ref[...] = rh * dgo - orw * (rh * rh * rh) * (_group_sum(orw * dgo) * (1.0 / V_DIM))

    row = lambda w, j=0: pl.BlockSpec((TM_MID, w), lambda i: (i, j))
    full = lambda a: pl.BlockSpec(a.shape, lambda i: (0,) * a.ndim)
    acc = lambda shape: pl.BlockSpec(shape, lambda i: (0, 0))
    sds = jax.ShapeDtypeStruct
    return pl.pallas_call(
        body, name="mid", grid=(s // TM_MID,),
        in_specs=[row(512), row(512, P_GA // 512), row(512), row(512, P_GB // 512), row(2048, P_MERGE // 2048), row(D), row(D),
                  full(g_hg), full(b_gate), full(g_post), full(wa), full(wb), full(w_out)],
        out_specs=[acc((1, 1)), row(D), row(512), row(512), row(512), row(512), row(2048),
                   acc((D, D)), acc((512, D)), acc((512, D)), acc((1, D)), acc((1, 2048)), acc((1, 512))],
        out_shape=[sds((1, 1), F32), sds((s, D), F32), sds((s, 512), F32), sds((s, 512), BF16), sds((s, 512), F32), sds((s, 512), BF16),
                   sds((s, 2048), BF16), sds((D, D), F32), sds((512, D), F32), sds((512, D), F32), sds((1, D), F32),
                   sds((1, 2048), F32), sds((1, 512), F32)],
        compiler_params=_params(("arbitrary",)),
    )(attn, proj, o_raw, proj, proj, x, tgt, g_hg, b_gate, g_post, wa, wb, w_out)


def _attn_bwd(q, k, vv, attn, dattn, lse, token):
    s = q.shape[1]
    nt = s // TQ
    scale = 1.0 / math.sqrt(QK_NOPE + QK_ROPE)

    def body(q_ref, k_ref, v_ref, o_ref, do_ref, lse_ref, token_ref, dq_ref, dk_ref, dv_ref, do_s, delta_s):
        j = pl.program_id(1)

        @pl.when(j == 0)
        def _():
            dq_ref[...] = jnp.zeros_like(dq_ref)
            lane = lax.broadcasted_iota(jnp.int32, (TQ, LANE), 1)

            @pl.loop(0, nt)
            def _(i):
                rows = pl.ds(pl.multiple_of(i * TQ, TQ), TQ)
                do, o = do_ref[rows, :], o_ref[rows, :]
                for hh in range(2):
                    doh = jnp.where((lane < 64) if hh == 0 else (lane >= 64), do, 0.0)
                    do_s[hh, rows, :] = doh.astype(BF16)
                    delta_s[hh, rows, :] = jnp.broadcast_to(jnp.sum(doh * o, axis=-1, keepdims=True), (TQ, LANE))

        kjs, vjs = (k_ref[0], k_ref[1]), (v_ref[0], v_ref[1])

        def tile(hh, start, size, kj, vj, diag):
            rows = pl.ds(pl.multiple_of(start, size), size)
            wide = lambda a: jnp.concatenate([a] * (kj.shape[0] // LANE), axis=1)
            qi, do_b = q_ref[hh, rows, :], do_s[hh, rows, :]
            sc, dp = _mm_nt(qi, kj), _mm_nt(do_b, vj)
            p = jnp.exp2(sc * QK_SCALE2 - wide(lse_ref[hh, rows, :]))
            if diag:
                p = jnp.where(_diag_visible(size), p, 0.0)
            ds_b = (p * (dp - wide(delta_s[hh, rows, :]))).astype(BF16)
            dv, dk = _mm_tn(do_b, p.astype(BF16)), _mm_tn(qi, ds_b)
            dq_ref[hh, rows, :] += _mm(ds_b, kj)
            return dk, dv

        def step(i, carry):
            new = [tile(hh, i * TQ, TQ, kjs[hh], vjs[hh], False) for hh in range(2)]
            return tuple((carry[hh][0] + new[hh][0], carry[hh][1] + new[hh][1]) for hh in range(2))

        def diagonal(hh):
            k0, k1, v0, v1 = kjs[hh][:HQ], kjs[hh][HQ:], vjs[hh][:HQ], vjs[hh][HQ:]
            a = tile(hh, j * TQ, HQ, k0, v0, True)
            b = tile(hh, j * TQ + HQ, HQ, k0, v0, False)
            c = tile(hh, j * TQ + HQ, HQ, k1, v1, True)
            return jnp.concatenate([a[0] + b[0], c[0]], axis=1), jnp.concatenate([a[1] + b[1], c[1]], axis=1)

        carry = lax.fori_loop(j + 1, nt, step, (diagonal(0), diagonal(1)))
        for hh in range(2):
            dk_ref[hh] = carry[hh][0].T * scale
            dv_ref[hh] = carry[hh][1].T

        @pl.when(j == nt - 1)
        def _():
            dq_ref[...] = dq_ref[...] * scale

    whole = pl.BlockSpec((2, s, LANE), lambda p, j: (p, 0, 0))
    tile_spec = pl.BlockSpec((2, TQ, LANE), lambda p, j: (p, j, 0))
    cols = pl.BlockSpec((s, LANE), lambda p, j: (0, p))
    hs = jax.ShapeDtypeStruct((NH, s, LANE), F32)
    return pl.pallas_call(
        body, name="attn_bwd", grid=(NH // 2, nt),
        in_specs=[whole, tile_spec, tile_spec, cols, cols, whole, pl.BlockSpec((8, LANE), lambda p, j: (0, 0))],
        out_specs=[whole, tile_spec, tile_spec],
        out_shape=[hs, hs, hs],
        scratch_shapes=[pltpu.VMEM((2, s, LANE), BF16), pltpu.VMEM((2, s, LANE), F32)],
        compiler_params=_params(("parallel", "arbitrary")),
    )(q, k, vv, attn, dattn, lse, token)


def _hgrn_bwd(proj, lbl, states, do_raw):
    s = proj.shape[0]
    nt = s // TH
    nch = TH // HG_BLOCK

    def body(hq_ref, hf_ref, hi_ref, lbl_ref, st_ref, do_ref, dh_ref, dlbl_ref, dst, dlb):
        step = pl.program_id(1)

        @pl.when(step == 0)
        def _():
            dst[...] = jnp.zeros_like(dst)
            dlb[...] = jnp.zeros_like(dlb)

        m0, bd = _pair_masks()
        lb = _lower_bound(lbl_ref[...])
        gt = _hgrn_gates(hq_ref[...], hf_ref[...], lb)
        do = do_ref[...]
        qd, ki, ke = gt["qd"], gt["ki"], gt["ke"]
        v_b, do_b = hi_ref[...].astype(BF16), do.astype(BF16)
        qd_b, ki_b, ke_b = qd.astype(BF16), ki.astype(BF16), ke.astype(BF16)
        pairs = [slice(u * LANE, (u + 1) * LANE) for u in range(HG_PAIRS)]
        heads = [(lanes, m0 if hh == 0 else jnp.logical_not(m0)) for lanes in pairs for hh in range(2)]
        a_b = [jnp.where(gt["tri"], _mm_nt(jnp.where(mh, qd[:, lanes], 0.0).astype(BF16), ki_b[:, lanes]), 0.0).astype(BF16)
               for lanes, mh in heads]
        doh_b = [jnp.where(mh, do[:, lanes], 0.0).astype(BF16) for lanes, mh in heads]
        da_b = [jnp.where(gt["tri"], _mm_nt(d, v_b[:, lanes]), 0.0).astype(BF16) for d, (lanes, _) in zip(doh_b, heads)]
        dv_p, dqd_p, dki_p = [], [], []
        for u, lanes in enumerate(pairs):
            e, o = 2 * u, 2 * u + 1
            dv_p.append(_mm_tn(a_b[e], doh_b[e]) + _mm_tn(a_b[o], doh_b[o]))
            dqd_p.append(jnp.where(m0, _mm(da_b[e], ki_b[:, lanes]), _mm(da_b[o], ki_b[:, lanes])))
            dki_p.append(jnp.where(m0, _mm_tn(da_b[e], qd_b[:, lanes]), _mm_tn(da_b[o], qd_b[:, lanes])))
        fed = [_bmm_tn(_chunks(do_b[:, lanes]), _chunks(qd_b[:, lanes])) for lanes in pairs]
        leaving = []
        for u, lanes in enumerate(pairs):
            ds, left = dst[u], [None] * nch
            for n in reversed(range(nch)):
                left[n] = ds
                ds = gt["decay"][n][:, lanes] * ds + jnp.where(bd, fed[u][n], 0.0)
            dst[u] = ds
            leaving.append(jnp.stack(left))
        dke_p, dlast_p = [], []
        for u, lanes in enumerate(pairs):
            entering, leaving_b = st_ref[u], leaving[u].astype(BF16)
            dke3 = _bmm_nn(_chunks(v_b[:, lanes]), leaving_b)
            dv_p[u] = dv_p[u] + _bmm_nt(_chunks(ke_b[:, lanes]), leaving_b).reshape(TH, LANE)
            dqd_p[u] = dqd_p[u] + _bmm_nn(_chunks(do_b[:, lanes]), entering.astype(BF16)).reshape(TH, LANE)
            dke_p.append(dke3.reshape(TH, LANE))
            dlast_p.append(jnp.sum(dke3 * _chunks(ke[:, lanes]), axis=1, keepdims=True)
                           + jnp.sum(leaving[u] * entering, axis=1, keepdims=True) * gt["decay"][:, :, lanes])
        cat = lambda parts: jnp.concatenate(parts, axis=-1)
        dv, dqd, dki, dke, dlast = cat(dv_p), cat(dqd_p), cat(dki_p), cat(dke_p), cat(dlast_p)
        dk = dki * gt["ei"] + dke * gt["ee"]
        dcum = dqd * qd - dki * ki - dke * ke
        dg = _chunk_cumsum(dcum, reverse=True) + jnp.broadcast_to(dlast, (nch, HG_BLOCK, dlast.shape[-1])).reshape(dcum.shape)
        sig = gt["sig"]
        df = dg / gt["f"] - dk
        dlb[...] += jnp.sum(df * (1.0 - sig), axis=0, keepdims=True)
        dh_ref[0] = (dqd * gt["e"]).astype(BF16)
        dh_ref[1] = ((df * (1.0 - lb)) * sig * (1.0 - sig)).astype(BF16)
        dh_ref[2] = dv.astype(BF16)

        @pl.when(step == nt - 1)
        def _():
            lb = _lower_bound(lbl_ref[...])
            da0 = dlb[...] * lb * (1.0 - lb)
            dlbl_ref[...] = jnp.concatenate([da0, -da0], axis=0)

    wide = HG_PAIRS * LANE
    col = lambda base: pl.BlockSpec((TH, wide), lambda p, i: (nt - 1 - i, base // wide + p))
    tile = pl.BlockSpec((TH, wide), lambda p, i: (nt - 1 - i, p))
    sds = jax.ShapeDtypeStruct
    return pl.pallas_call(
        body, name="hgrn_bwd", grid=(NH // 2 // HG_PAIRS, nt),
        in_specs=[col(P_HQ), col(P_HF), col(P_HI), pl.BlockSpec((2, wide), lambda p, i: (0, p)),
                  pl.BlockSpec((HG_PAIRS, nch, LANE, LANE), lambda p, i: (p, nt - 1 - i, 0, 0)), tile],
        out_specs=[pl.BlockSpec((3, TH, wide), lambda p, i: (0, nt - 1 - i, p)), pl.BlockSpec((2, wide), lambda p, i: (0, p))],
        out_shape=[sds((3, s, 512), BF16), sds((2, 512), F32)],
        scratch_shapes=[pltpu.VMEM((HG_PAIRS, LANE, LANE), F32), pltpu.VMEM((1, wide), F32)],
        compiler_params=_params(("parallel", "arbitrary")),
    )(proj, proj, proj, lbl, states, do_raw)


def _norm_rows_bwd(v, r, g, dn):
    dgv = dn * g
    return r * dgv - v * (r * r * r) * jnp.mean(v * dgv, axis=-1, keepdims=True)


def _qkv_bwd(proj, dq, dk, dvv, g_q, g_kv, w_uq_p, w_k_p, w_v_p, rc, rs1, rs2):
    s = proj.shape[0]

    def body(cq_ref, ckv_ref, dq_ref, dk_ref, dv_ref, gq_ref, gkv_ref, wq_ref, wk_ref, wv_ref, c_ref, s1_ref, s2_ref,
             dcq_ref, dckv_ref, dkpe_ref, dwq_ref, dwk_ref, dwv_ref, dgq_ref, dgkv_ref):
        @pl.when(pl.program_id(0) == 0)
        def _():
            for rf in (dwq_ref, dwk_ref, dwv_ref, dgq_ref, dgkv_ref):
                rf[...] = jnp.zeros_like(rf)

        c, s1, s2 = c_ref[...], s1_ref[...], s2_ref[...]
        cq, ckv = cq_ref[...], ckv_ref[...]
        gq, gkv = gq_ref[...], gkv_ref[...]
        cqn, rq = _norm_rows(cq, gq)
        ckvn, rkv = _norm_rows(ckv, gkv)
        cqn_b, ckvn_b = cqn.astype(BF16), ckvn.astype(BF16)
        dqf = jnp.concatenate([_rope_t(dq_ref[h], c, s1, s2) for h in range(NH)], axis=1).astype(BF16)
        dkf = jnp.concatenate([dk_ref[h] for h in range(NH)], axis=1).astype(BF16)
        dvf = jnp.concatenate([dv_ref[h] for h in range(NH)], axis=1).astype(BF16)
        dkpe = dk_ref[0]
        for h in range(1, NH):
            dkpe = dkpe + dk_ref[h]
        lane = lax.broadcasted_iota(jnp.int32, (TM, LANE), 1)
        dkpe = jnp.where((lane >= QK_NOPE) & (lane < QK_NOPE + QK_ROPE), dkpe, 0.0)
        dkpe_ref[...] = _rope_t(dkpe, c, s1, s2).astype(BF16)
        dcqn = _mm_nt(dqf, wq_ref[...])
        dckvn = _mm_nt(dkf, wk_ref[...]) + _mm_nt(dvf, wv_ref[...])
        dwq_ref[...] += _mm_tn(cqn_b, dqf)
        dwk_ref[...] += _mm_tn(ckvn_b, dkf)
        dwv_ref[...] += _mm_tn(ckvn_b, dvf)
        dgq_ref[...] += jnp.sum(dcqn * (cq * rq), axis=0, keepdims=True)
        dgkv_ref[...] += jnp.sum(dckvn * (ckv * rkv), axis=0, keepdims=True)
        dcq_ref[...] = _norm_rows_bwd(cq, rq, gq, dcqn).astype(BF16)
        dckv_ref[...] = _norm_rows_bwd(ckv, rkv, gkv, dckvn).astype(BF16)

    row = lambda w, j=0: pl.BlockSpec((TM, w), lambda i: (i, j))
    full = lambda a: pl.BlockSpec(a.shape, lambda i: (0,) * a.ndim)
    acc = lambda shape: pl.BlockSpec(shape, lambda i: (0, 0))
    heads = pl.BlockSpec((NH, TM, LANE), lambda i: (0, i, 0))
    sds = jax.ShapeDtypeStruct
    return pl.pallas_call(
        body, name="qkv_bwd", grid=(s // TM,),
        in_specs=[row(Q_LORA, P_CQ // Q_LORA), row(KV_LORA, P_CKV // KV_LORA), heads, heads, heads,
                  full(g_q), full(g_kv), full(w_uq_p), full(w_k_p), full(w_v_p), row(LANE), row(LANE), row(LANE)],
        out_specs=[row(Q_LORA), row(KV_LORA), row(LANE), acc((Q_LORA, D)), acc((KV_LORA, D)), acc((KV_LORA, D)),
                   acc((1, Q_LORA)), acc((1, KV_LORA))],
        out_shape=[sds((s, Q_LORA), BF16), sds((s, KV_LORA), BF16), sds((s, LANE), BF16), sds((Q_LORA, D), F32),
                   sds((KV_LORA, D), F32), sds((KV_LORA, D), F32), sds((1, Q_LORA), F32), sds((1, KV_LORA), F32)],
        compiler_params=_params(("arbitrary",)),
    )(proj, proj, dq, dk, dvv, g_q, g_kv, w_uq_p, w_k_p, w_v_p, rc, rs1, rs2)


def _front_bwd(x, dout, dmg, dga, dh3, dgb, dcq, dckv, dkpe, g_pre, w_in_pt, token):
    s = x.shape[0]

    def body(x_ref, do_ref, dmg_ref, dga_ref, dh3_ref, dgb_ref, dcq_ref, dckv_ref, dkpe_ref, g_ref, w_ref, token_ref, gx_ref, dg_ref):
        @pl.when(pl.program_id(0) == 0)
        def _():
            dg_ref[...] = jnp.zeros_like(dg_ref)

        xv, g = x_ref[...], g_ref[...]
        _, r = _norm_rows(xv, g)
        pieces = ((dmg_ref[...], P_MERGE), (dga_ref[...], P_GA), (dh3_ref[0], P_HQ), (dh3_ref[1], P_HF), (dh3_ref[2], P_HI),
                  (dgb_ref[...], P_GB), (dcq_ref[...], P_CQ), (dckv_ref[...], P_CKV), (dkpe_ref[...], P_KPE))
        dh = jnp.zeros((TM, D), F32)
        for piece, off in pieces:
            dh = dh + _mm(piece, w_ref[off:off + piece.shape[1], :])
        dg_ref[...] += jnp.sum(dh * (xv * r), axis=0, keepdims=True)
        gx_ref[...] = do_ref[...] + _norm_rows_bwd(xv, r, g, dh)

    row = lambda w: pl.BlockSpec((TM, w), lambda i: (i, 0))
    full = lambda a: pl.BlockSpec(a.shape, lambda i: (0,) * a.ndim)
    sds = jax.ShapeDtypeStruct
    return pl.pallas_call(
        body, name="front_bwd", grid=(s // TM,),
        in_specs=[row(D), row(D), row(2048), row(512), pl.BlockSpec((3, TM, 512), lambda i: (0, i, 0)), row(512), row(Q_LORA),
                  row(KV_LORA), row(LANE), full(g_pre), full(w_in_pt), pl.BlockSpec(memory_space=pl.ANY)],
        out_specs=[row(D), pl.BlockSpec((1, D), lambda i: (0, 0))],
        out_shape=[sds((s, D), F32), sds((1, D), F32)],
        compiler_params=_params(("arbitrary",)),
    )(x, dout, dmg, dga, dh3, dgb, dcq, dckv, dkpe, g_pre, w_in_pt, token)


TK_GRAD = 1024


def _win_grad(h, pieces, name):
    s = h.shape[0]
    n = len(pieces)

    def body(h_ref, *refs):
        @pl.when(pl.program_id(0) == 0)
        def _():
            for o_ref in refs[n:]:
                o_ref[...] = jnp.zeros_like(o_ref)

        hv = h_ref[...]
        for d_ref, o_ref in zip(refs[:n], refs[n:]):
            if len(d_ref.shape) == 3:
                for k in range(d_ref.shape[0]):
                    o_ref[k] += _mm_tn(d_ref[k], hv)
            else:
                o_ref[...] += _mm_tn(d_ref[...], hv)

    def in_spec(p):
        if p.ndim == 3:
            return pl.BlockSpec((p.shape[0], TK_GRAD, p.shape[2]), lambda kk: (0, kk, 0))
        return pl.BlockSpec((TK_GRAD, p.shape[1]), lambda kk: (kk, 0))

    out_shapes = [(p.shape[0], p.shape[2], D) if p.ndim == 3 else (p.shape[1], D) for p in pieces]
    return pl.pallas_call(
        body, name=name, grid=(s // TK_GRAD,),
        in_specs=[pl.BlockSpec((TK_GRAD, D), lambda kk: (kk, 0))] + [in_spec(p) for p in pieces],
        out_specs=[pl.BlockSpec(sh, lambda kk, nd=len(sh): (0,) * nd) for sh in out_shapes],
        out_shape=[jax.ShapeDtypeStruct(sh, F32) for sh in out_shapes],
        compiler_params=_params(("arbitrary",)),
    )(h, *pieces)


def _pad_win_t(w_in_t):
    z = lambda n: jnp.zeros((n, w_in_t.shape[1]), w_in_t.dtype)
    sl = lambda o, n: w_in_t[o:o + n]
    return jnp.concatenate([sl(O_MERGE, 2048), sl(O_GA, 512), sl(O_HQ, 512), sl(O_HF, 512), sl(O_HI, 512), sl(O_GB, 512),
                            sl(O_CQ, Q_LORA), sl(O_CKV, KV_LORA), z(64), sl(O_KPE, QK_ROPE), z(32)], axis=0)


def _pad_wuq(w_uq):
    rows = w_uq.shape[0]
    w = w_uq.reshape(rows, NH, QK_NOPE + QK_ROPE)
    return jnp.pad(w, ((0, 0), (0, 0), (0, LANE - QK_NOPE - QK_ROPE))).reshape(rows, NH * LANE)


def _unpad_wuq(g):
    return g.reshape(Q_LORA, NH, LANE)[:, :, :QK_NOPE + QK_ROPE].reshape(Q_LORA, NH * (QK_NOPE + QK_ROPE))


def _pad_wukv(w_ukv):
    heads = w_ukv.shape[1] // (QK_NOPE + V_DIM)
    w = w_ukv.reshape(KV_LORA, heads, QK_NOPE + V_DIM)
    w_k = jnp.pad(w[:, :, :QK_NOPE], ((0, 0), (0, 0), (0, LANE - QK_NOPE))).reshape(KV_LORA, heads * LANE)
    wv = w[:, :, QK_NOPE:].reshape(KV_LORA, heads // 2, 2, 1, V_DIM)
    eye = jnp.eye(2, dtype=w.dtype).reshape(1, 1, 2, 2, 1)
    return w_k, (wv * eye).reshape(KV_LORA, heads * LANE)


def _unpad_wukv(gk, gv):
    gk = gk.reshape(KV_LORA, NH, LANE)[:, :, :QK_NOPE]
    gv = gv.reshape(KV_LORA, NH // 2, 2, 2, V_DIM)
    gv = jnp.stack([gv[:, :, 0, 0], gv[:, :, 1, 1]], axis=2).reshape(KV_LORA, NH, V_DIM)
    return jnp.concatenate([gk, gv], axis=-1).reshape(KV_LORA, NH * (QK_NOPE + V_DIM))


def _local_step(x, tgt, g_pre, w_in_t, b_gate, g_q, g_kv, lb_logits, g_hgrn, g_post, weights, exchange=None):
    s = x.shape[0]
    w_in_p = _pad_win_t(w_in_t)
    rc, rs1, rs2 = _rope_tables(s)
    g_hg = jnp.tile(g_hgrn, (1, NH))

    proj, h = _front_fwd(x, g_pre, w_in_p, weights.tokens)
    w_uq_p, w_k_p, w_v_p = weights.qkv(h)
    q, k, vv = _qkv_fwd(proj, g_q, g_kv, w_uq_p, w_k_p, w_v_p, rc, rs1, rs2)
    attn, lse = _attn_fwd(q, k, vv)
    o_raw, states = _hgrn_fwd(proj, lb_logits)
    wa, wb, w_out = weights.mid(o_raw)
    (loss, dout, dattn, dga, dor, dgb, dmg, d_wout, d_wa, d_wb, d_gpost, d_bgate, d_ghg) = _mid(
        proj, attn, o_raw, x, tgt, g_hg, b_gate, g_post, wa, wb, w_out)
    w_mg, w_ga, w_gb = _win_grad(h, [dmg, dga, dgb], "win_grad_mid")
    dh3, d_lbl = _hgrn_bwd(proj, lb_logits, states, dor)
    (w_h3,) = _win_grad(h, [dh3], "win_grad_hgrn")
    d_win_rest = jnp.concatenate([w_ga, w_h3[0], w_h3[1], w_h3[2], w_gb, w_mg], axis=0)
    early = dict(w_in_rest=d_win_rest, w_branch_a=d_wa, w_branch_b=d_wb, w_out=d_wout)
    token = exchange.start_early(early) if exchange else jnp.zeros((8, LANE), F32)
    dq, dk, dvv = _attn_bwd(q, k, vv, attn, dattn, lse, token)
    dcq, dckv, dkpe, d_wuq_p, d_wk_p, d_wv_p, d_gq, d_gkv = _qkv_bwd(proj, dq, dk, dvv, g_q, g_kv, w_uq_p, w_k_p, w_v_p, rc, rs1, rs2)
    w_cq, w_ckv, w_kpe = _win_grad(h, [dcq, dckv, dkpe], "win_grad_qkv")
    d_win_qkv = jnp.concatenate([w_cq, w_ckv, w_kpe[64:64 + QK_ROPE]], axis=0)
    late = dict(w_in_qkv=d_win_qkv, w_uq=_unpad_wuq(d_wuq_p), w_ukv=_unpad_wukv(d_wk_p, d_wv_p))
    token = exchange.start_late(late) if exchange else jnp.zeros((8, LANE), F32)
    grad_x, d_gpre = _front_bwd(x, dout, dmg, dga, dh3, dgb, dcq, dckv, dkpe, g_pre, w_in_p, token)
    vec_grads = dict(g_pre=d_gpre, b_gate=d_bgate, g_q=d_gq, g_kv=d_gkv, lb_logits=d_lbl, g_hgrn=d_ghg, g_post=d_gpost)
    return loss, grad_x, dict(early, **late), vec_grads


SHARD_SHAPES = (("w_in", (1416, 1024)), ("w_uq", (192, 768)), ("w_ukv", (256, 256)), ("w_branch_a", (512, 256)),
                ("w_branch_b", (512, 256)), ("w_out", (256, 1024)))
BIG = tuple(n for n, _ in SHARD_SHAPES)
ROW_SHARDED = ("w_in", "w_uq", "w_out")
N_CHIPS = 4
QKV_ROWS = Q_LORA + KV_LORA + QK_ROPE
W_IN_FORWARD_CUT = 704


def _to_block(name, a):
    return a[0].T if name == "w_in" else a[0]


def _from_block(name, a):
    return a.T[None] if name == "w_in" else a[None]
VEC_ROWS = (("g_pre", 0, 1024), ("b_gate", 1, 2048), ("g_q", 2, 768), ("g_kv", 3, 256), ("g_hgrn", 6, 64), ("g_post", 7, 1024))
VEC_LB_ROW = 4
VEC_SHAPE = (8, 2048)


def _split_by_chip(name, g):
    a, b = dict(SHARD_SHAPES)[name]
    return g.reshape(N_CHIPS, a, b) if name in ROW_SHARDED else g.reshape(a, N_CHIPS, b).transpose(1, 0, 2)


def _join_chips(name, w):
    a, b = dict(SHARD_SHAPES)[name]
    return w.reshape(N_CHIPS * a, b) if name in ROW_SHARDED else w.transpose(1, 0, 2).reshape(a, N_CHIPS * b)


MESH = pl.DeviceIdType.MESH
HBM = pl.BlockSpec(memory_space=pltpu.HBM)


def _mesh_place():
    x, y, c = lax.axis_index("x"), lax.axis_index("y"), lax.axis_index("c")
    return x, y, c, 2 * x + y, [(1 - x, y), (x, 1 - y), (1 - x, 1 - y)]


def _remote(src, dst, send_sems, recv_sems, k, to):
    return pltpu.make_async_remote_copy(src_ref=src, dst_ref=dst, send_sem=send_sems.at[k], recv_sem=recv_sems.at[k],
                                        device_id=to, device_id_type=MESH)


def _gather_w_in(shard):
    a, b = shard.shape
    cut = W_IN_FORWARD_CUT

    def body(src, out, ici_send, ici_recv, d2d_send, d2d_recv, local_sem):
        x, y, c = lax.axis_index("x"), lax.axis_index("y"), lax.axis_index("c")
        me, xn, yn, dg = 2 * x + y, 2 * (1 - x) + y, 2 * x + (1 - y), 2 * (1 - x) + (1 - y)
        to_x, to_y, sibling = (1 - x, y, c), (x, 1 - y, c), (x, y, 1 - c)
        first, rest = pl.ds(0, cut), pl.ds(cut, a - cut)

        whole = lambda ref, which: ref.at[:, pl.ds(pl.multiple_of(which * (b // 2), b // 2), b // 2)]
        own = pltpu.make_async_copy(src, out.at[me], local_sem)
        own.start()
        sends = [_remote(whole(src, c), whole(out.at[me], c), ici_send, ici_recv, 0, to_x),
                 _remote(whole(src, c), whole(out.at[me], c), ici_send, ici_recv, 1, to_y)]
        for cp in sends:
            cp.start()

        def landed(slot, rows, k, d2d_k, src_dev):
            piece = whole(out.at[slot], c) if rows is None else out.at[slot].at[rows, pl.ds(pl.multiple_of(c * (b // 2), b // 2), b // 2)]
            _remote(piece, piece, ici_send, ici_recv, k, src_dev).wait_recv()
            cp = _remote(piece, piece, d2d_send, d2d_recv, d2d_k, sibling)
            cp.start()
            sends.append(cp)
            return piece

        def pass_on(slot, rows, k, to):
            piece = out.at[slot].at[rows, pl.ds(pl.multiple_of(c * (b // 2), b // 2), b // 2)]
            cp = _remote(piece, piece, ici_send, ici_recv, k, to)
            cp.start()
            sends.append(cp)

        landed(xn, None, 0, 0, to_x)
        pass_on(xn, first, 2, to_y)
        landed(yn, None, 1, 1, to_y)
        pass_on(yn, rest, 3, to_x)
        landed(dg, first, 2, 2, to_y)
        landed(dg, rest, 3, 3, to_x)
        other = pl.ds(pl.multiple_of((1 - c) * (b // 2), b // 2), b // 2)
        for d2d_k, (slot, rows) in enumerate(((xn, None), (yn, None), (dg, first), (dg, rest))):
            piece = out.at[slot].at[:, other] if rows is None else out.at[slot].at[rows, other]
            _remote(piece, piece, d2d_send, d2d_recv, d2d_k, sibling).wait_recv()
        for cp in sends:
            cp.wait_send()
        own.wait()

    sems = pltpu.SemaphoreType.DMA((4,))
    return pl.pallas_call(
        body, name="gather_w_in", in_specs=[HBM], out_specs=HBM,
        out_shape=jax.ShapeDtypeStruct((N_CHIPS, a, b), shard.dtype),
        scratch_shapes=[sems, sems, sems, sems, pltpu.SemaphoreType.DMA],
        compiler_params=pltpu.CompilerParams(has_side_effects=True),
    )(shard)


def _sibling_exchange(srcs, name, after=None):
    n = len(srcs)
    extra = [] if after is None else [after]

    def body(*refs):
        src_refs, outs = refs[:n], refs[n + len(extra):2 * n + len(extra)]
        send_sems, recv_sems = refs[2 * n + len(extra):]
        sibling = (lax.axis_index("x"), lax.axis_index("y"), 1 - lax.axis_index("c"))
        copies = [_remote(src_refs[k], outs[k], send_sems, recv_sems, k, sibling) for k in range(n)]
        for cp in copies:
            cp.start()
        for cp in copies:
            cp.wait()

    sems = pltpu.SemaphoreType.DMA((n,))
    return pl.pallas_call(
        body, name=name, in_specs=[HBM] * n + [pl.BlockSpec(memory_space=pl.ANY)] * len(extra), out_specs=[HBM] * n,
        out_shape=[jax.ShapeDtypeStruct(s.shape, s.dtype) for s in srcs],
        scratch_shapes=[sems, sems],
        compiler_params=pltpu.CompilerParams(has_side_effects=True),
    )(*srcs, *extra)


SEM = pl.BlockSpec(memory_space=pltpu.SEMAPHORE)
DATAFLOW = pltpu.SideEffectType.DATAFLOW_SIDE_EFFECTING


def _exchange_copies(srcs, to_first, src_refs, land_refs, send_sems, recv_sems):
    x, y, c, me, chips = _mesh_place()
    n = len(srcs)
    sends, recvs = [], []
    for k in range(n):
        if k in to_first:
            base = 3 * n + 4 * to_first.index(k)
            sends.append((me != 0, pltpu.make_async_remote_copy(
                src_ref=src_refs[k], dst_ref=land_refs[k].at[me], send_sem=send_sems.at[base], recv_sem=recv_sems.at[base + me],
                device_id=(0, 0, c), device_id_type=MESH)))
            for s in range(1, N_CHIPS):
                recvs.append((me == 0, pltpu.make_async_remote_copy(
                    src_ref=src_refs[k], dst_ref=land_refs[k].at[s], send_sem=send_sems.at[base], recv_sem=recv_sems.at[base + s],
                    device_id=(s // 2, s % 2, c), device_id_type=MESH)))
        else:
            slab = (lambda t, k=k: src_refs[k]) if srcs[k].ndim == 2 else (lambda t, k=k: src_refs[k].at[t])
            for j, (px, py) in enumerate(chips):
                sends.append((None, _remote(slab(2 * px + py), land_refs[k].at[me], send_sems, recv_sems, 3 * k + j, (px, py, c))))
                recvs.append((None, _remote(slab(me), land_refs[k].at[2 * px + py], send_sems, recv_sems, 3 * k + j, (px, py, c))))
    return sends, recvs


def _when(pred, fn):
    if pred is None:
        fn()
    else:
        pl.when(pred)(fn)


def _exchange_start(srcs, to_first, name, after=None):
    n = len(srcs)
    n_sems = 3 * n + 4 * len(to_first)
    lands = [lax.empty((N_CHIPS,) + s.shape[-2:], s.dtype) for s in srcs]
    extra = [] if after is None else [after]

    def body(*refs):
        src_refs, land_refs = refs[:n], refs[n:2 * n]
        send_sems, recv_sems, token = refs[2 * n + len(extra)], refs[2 * n + len(extra) + 1], refs[-1]
        sends, _ = _exchange_copies(srcs, to_first, src_refs, land_refs, send_sems, recv_sems)
        for pred, cp in sends:
            _when(pred, cp.start)
        token[...] = jnp.zeros_like(token)

    hbm = lambda a: pltpu.HBM(a.shape, a.dtype)
    res = pl.pallas_call(
        body, name=name,
        out_shape=[pltpu.SemaphoreType.DMA((n_sems,)), pltpu.SemaphoreType.DMA((n_sems,))] + [hbm(a) for a in srcs + lands]
        + [jax.ShapeDtypeStruct((8, LANE), F32)],
        in_specs=[HBM] * (2 * n) + [pl.BlockSpec(memory_space=pl.ANY)] * len(extra),
        out_specs=[SEM, SEM] + [HBM] * (2 * n) + [pl.BlockSpec(memory_space=pltpu.VMEM)],
        input_output_aliases={i: 2 + i for i in range(2 * n)},
        compiler_params=pltpu.CompilerParams(has_side_effects=DATAFLOW),
    )(*[pltpu.with_memory_space_constraint(a, pltpu.HBM) for a in srcs + lands], *extra)
    return res[:-1], res[-1]


def _exchange_wait(srcs, to_first, started, after, name):
    n = len(srcs)
    send_sems, recv_sems, thru = started[0], started[1], started[2:]

    def body(*refs):
        src_refs, land_refs, send_ref, recv_ref = refs[:n], refs[n:2 * n], refs[2 * n], refs[2 * n + 1]
        sends, recvs = _exchange_copies(srcs, to_first, src_refs, land_refs, send_ref, recv_ref)
        for pred, cp in sends:
            _when(pred, cp.wait_send)
        for pred, cp in recvs:
            _when(pred, cp.wait_recv)

    res = pl.pallas_call(
        body, name=name, out_shape=[pltpu.HBM(a.shape, a.dtype) for a in thru],
        in_specs=[HBM] * (2 * n) + [SEM, SEM, pl.BlockSpec(memory_space=pl.ANY)], out_specs=[HBM] * (2 * n),
        input_output_aliases={i: i for i in range(2 * n)},
        compiler_params=pltpu.CompilerParams(has_side_effects=DATAFLOW),
    )(*thru, send_sems, recv_sems, after)
    return res[n:]


ROW_TILE = 256
COL_TILE = 256


def _block_tiling(a, b):
    if a <= ROW_TILE or a % ROW_TILE == 0:
        ta = min(a, ROW_TILE)
        return a // ta, (ta, b), lambda i: (i, 0)
    return b // COL_TILE, (a, COL_TILE), lambda i: (0, i)


def _sum_landed(land, own, name, first_land=None, first_own=None):
    _, a, b = land.shape
    steps, tile, at = _block_tiling(a, b)
    extra = first_land is not None

    def body(*refs):
        p_ref, own_ref, o_ref = refs[0], refs[1], refs[-1]
        me = 2 * lax.axis_index("x") + lax.axis_index("y")
        own = own_ref[...].astype(F32)
        slot = lambda t: jnp.where(me == t, own, p_ref[t].astype(F32))
        o_ref[...] = ((slot(0) + slot(1)) + slot(2)) + slot(3)
        if extra:
            fp_ref, fo_ref = refs[2], refs[3]
            r = fo_ref.shape[0]

            @pl.when(me == 0)
            def _():
                f = lambda t: fp_ref[t].astype(F32)
                rows = pl.ds(pl.multiple_of(lax.axis_index("c") * r, 8), r)
                o_ref[rows, :] += ((fo_ref[...].astype(F32) + f(1)) + f(2)) + f(3)

    in_specs = [pl.BlockSpec((N_CHIPS,) + tile, lambda i: (0,) + at(i)), pl.BlockSpec(tile, at)]
    args = [land, own]
    if extra:
        r = first_own.shape[0]
        assert tile[0] == a, "the extra rows need whole columns in a step"
        in_specs += [pl.BlockSpec((N_CHIPS, r, tile[1]), lambda i: (0,) + at(i)), pl.BlockSpec((r, tile[1]), at)]
        args += [first_land, first_own]
    return pl.pallas_call(
        body, name=name, grid=(steps,), in_specs=in_specs, out_specs=pl.BlockSpec(tile, at),
        out_shape=jax.ShapeDtypeStruct((a, b), F32), compiler_params=_params(("parallel",)),
    )(*args)


def _add_cast(a, b, name):
    def body(a_ref, b_ref, o_ref):
        o_ref[...] = (a_ref[...] + b_ref[...]).astype(BF16)

    return pl.pallas_call(body, name=name, out_shape=jax.ShapeDtypeStruct(a.shape, BF16),
                          compiler_params=_params(()))(a, b)


class _LaterWeights:
    MID = ("w_branch_a", "w_branch_b", "w_out")

    def __init__(self, blocks, after):
        self.qkv_blocks = [_pad_wuq(blocks["w_uq"]), *_pad_wukv(blocks["w_ukv"])]
        self.mid_blocks = [blocks[n] for n in self.MID]
        self.qkv_started, t1 = _exchange_start(self.qkv_blocks, (), "weights_qkv_start", after)
        self.mid_started, t2 = _exchange_start(self.mid_blocks, (), "weights_mid_start", after)
        self.tokens = [t1, t2]

    @staticmethod
    def _whole(blocks, rows_sharded, started, after, name):
        landed = _exchange_wait(blocks, (), started, after, name)
        me = 2 * lax.axis_index("x") + lax.axis_index("y")
        out = []
        for block, land, by_rows in zip(blocks, landed, rows_sharded):
            w = lax.dynamic_update_index_in_dim(land, block, me, 0)
            a, b = block.shape
            out.append(w.reshape(N_CHIPS * a, b) if by_rows else w.transpose(1, 0, 2).reshape(a, N_CHIPS * b))
        return out

    def qkv(self, after):
        return self._whole(self.qkv_blocks, (True, False, False), self.qkv_started, after, "weights_qkv_wait")

    def mid(self, after):
        return self._whole(self.mid_blocks, (False, False, True), self.mid_started, after, "weights_mid_wait")


class _GradExchange:
    EARLY = ("w_in", "w_branch_a", "w_branch_b", "w_out")
    LATE = ("w_uq", "w_ukv")

    def __init__(self, state):
        self.state = state
        self.outs = {}

    @staticmethod
    def _own(slabs):
        return lax.dynamic_index_in_dim(slabs, 2 * lax.axis_index("x") + lax.axis_index("y"), axis=0, keepdims=False)

    def start_early(self, g):
        full = jnp.concatenate([jnp.zeros((QKV_ROWS, D), F32), g["w_in_rest"]], axis=0)
        g = dict(g, w_in=full)
        self.early = [_split_by_chip(n, g[n]).astype(BF16) for n in self.EARLY]
        self.early_started, token = _exchange_start(self.early, (), "grads_early_start")
        return token

    def start_late(self, g):
        self.early_landed = _exchange_wait(self.early, (), self.early_started, g["w_uq"], "grads_early_wait")
        half = QKV_ROWS // 2
        c = lax.axis_index("c")
        mine = lax.dynamic_slice_in_dim(g["w_in_qkv"], c * half, half, axis=0)
        (theirs,) = _sibling_exchange([lax.dynamic_slice_in_dim(g["w_in_qkv"], (1 - c) * half, half, axis=0)], "sibling_qkv_rows")
        self.late = [_split_by_chip(n, g[n]).astype(BF16) for n in self.LATE] + [_add_cast(mine, theirs, "add_qkv_rows")]
        self.late_started, token = _exchange_start(self.late, (2,), "grads_late_start")
        names = self.EARLY[1:]
        mine = [_sum_landed(land, self._own(slabs), "sum_" + n) for n, slabs, land in list(zip(self.EARLY, self.early, self.early_landed))[1:]]
        theirs = _sibling_exchange(mine, "sibling_early", after=token)
        for n, a, b in zip(names, mine, theirs):
            self.outs[n] = _adamw(a, b, *self.state[n], "adamw_" + n)
        return self.outs[names[-1]][0]

    def finish(self, after):
        late_landed = _exchange_wait(self.late, (2,), self.late_started, after, "grads_late_wait")
        sums = {"w_in": _sum_landed(self.early_landed[0], self._own(self.early[0]), "sum_w_in",
                                    first_land=late_landed[2], first_own=self.late[2])}
        for n, slabs, land in zip(self.LATE, self.late, late_landed):
            sums[n] = _sum_landed(land, self._own(slabs), "sum_" + n)
        return sums


def _adamw_math(g, w, m, v):
    nm = ADAM_B1 * m + (1.0 - ADAM_B1) * g
    nv = ADAM_B2 * v + (1.0 - ADAM_B2) * (g * g)
    m_hat = nm / (1.0 - ADAM_B1 ** ADAM_STEP)
    v_hat = nv / (1.0 - ADAM_B2 ** ADAM_STEP)
    return -ADAM_LR * (m_hat / (jnp.sqrt(v_hat) + ADAM_EPS) + ADAM_WD * w), nm, nv


def _adamw(p_mine, p_sibling, w, m, v, name):
    a, b = p_mine.shape
    steps, tile, at = _block_tiling(a, b)

    def body(a_ref, b_ref, w_ref, m_ref, v_ref, g_ref, d_ref, nm_ref, nv_ref):
        g = a_ref[...] + b_ref[...]
        g_ref[...] = g
        d_ref[...], nm_ref[...], nv_ref[...] = _adamw_math(g, w_ref[...], m_ref[...], v_ref[...])

    spec = pl.BlockSpec(tile, at)
    sds = jax.ShapeDtypeStruct((a, b), F32)
    return pl.pallas_call(
        body, name=name, grid=(steps,), in_specs=[spec] * 5, out_specs=[spec] * 4, out_shape=[sds] * 4,
        compiler_params=_params(("parallel",)),
    )(p_mine, p_sibling, w, m, v)


LOSS_AT = (2, 1024)


def _vec_pack(vg, loss):
    names = [n for n, _, _ in VEC_ROWS]

    def body(*refs):
        o_ref = refs[-1]
        lb_ref, loss_ref = refs[len(names)], refs[len(names) + 1]
        o_ref[...] = jnp.zeros_like(o_ref)
        o_ref[LOSS_AT[0]:LOSS_AT[0] + 1, LOSS_AT[1]:LOSS_AT[1] + LANE] = jnp.broadcast_to(loss_ref[...], (1, LANE))
        for (name, row, size), ref in zip(VEC_ROWS, refs):
            if name == "g_hgrn":
                r = lax.broadcasted_iota(jnp.int32, (NH * V_DIM, LANE), 0)
                c = lax.broadcasted_iota(jnp.int32, (NH * V_DIM, LANE), 1)
                fold = ((r % V_DIM) == c).astype(F32)
                o_ref[row:row + 1, 0:LANE] = jnp.dot(ref[...], fold, precision=HIGHEST, preferred_element_type=F32)
            else:
                o_ref[row:row + 1, 0:size] = ref[...]
        o_ref[VEC_LB_ROW:VEC_LB_ROW + 2, 0:512] = lb_ref[...]

    return pl.pallas_call(body, name="vec_pack", out_shape=jax.ShapeDtypeStruct(VEC_SHAPE, F32))(
        *[vg[n] for n in names], vg["lb_logits"], loss)


def _adamw_vec(p_mine, p_sibling, w, m, v):
    names = [n for n, _, _ in VEC_ROWS] + ["lb_logits"]
    k = len(names)

    def body(a_ref, b_ref, *refs):
        ins, outs = refs[:3 * k], refs[3 * k:]
        at = (slice(LOSS_AT[0], LOSS_AT[0] + 1), slice(LOSS_AT[1], LOSS_AT[1] + LANE))
        outs[-1][...] = a_ref[at] + b_ref[at]
        for i, name in enumerate(names):
            if name == "lb_logits":
                rows, cols = slice(VEC_LB_ROW, VEC_LB_ROW + 2), slice(0, 512)
            else:
                _, row, size = VEC_ROWS[i]
                rows, cols = slice(row, row + 1), slice(0, size)
            g = a_ref[rows, cols] + b_ref[rows, cols]
            d, nm, nv = _adamw_math(g, ins[i][...], ins[k + i][...], ins[2 * k + i][...])
            for o_ref, val in zip(outs[4 * i:4 * i + 4], (g, d, nm, nv)):
                o_ref[...] = val

    shapes = [jax.ShapeDtypeStruct(w[n].shape, F32) for n in names for _ in range(4)] + [jax.ShapeDtypeStruct((1, LANE), F32)]
    res = pl.pallas_call(body, name="adamw_vec", out_shape=shapes)(
        p_mine, p_sibling, *[w[n] for n in names], *[m[n] for n in names], *[v[n] for n in names])
    return [{n: res[4 * i + j] for i, n in enumerate(names)} for j in range(4)], res[-1]


WEIGHTS = ("g_pre", "w_in", "b_gate", "g_q", "w_uq", "g_kv", "w_ukv", "lb_logits", "g_hgrn", "w_branch_a", "w_branch_b", "w_out", "g_post")


def kernel(x, g_pre, w_in, b_gate, g_q, w_uq, g_kv, w_ukv, lb_logits, g_hgrn, w_branch_a, w_branch_b, w_out, g_post, loss_target, m_g_pre, m_w_in, m_b_gate, m_g_q, m_w_uq, m_g_kv, m_w_ukv, m_lb_logits, m_g_hgrn, m_w_branch_a, m_w_branch_b, m_w_out, m_g_post, v_g_pre, v_w_in, v_b_gate, v_g_q, v_w_uq, v_g_kv, v_w_ukv, v_lb_logits, v_g_hgrn, v_w_branch_a, v_w_branch_b, v_w_out, v_g_post):
    w = dict(g_pre=g_pre, w_in=w_in, b_gate=b_gate, g_q=g_q, w_uq=w_uq, g_kv=g_kv, w_ukv=w_ukv, lb_logits=lb_logits, g_hgrn=g_hgrn,
             w_branch_a=w_branch_a, w_branch_b=w_branch_b, w_out=w_out, g_post=g_post)
    m = dict(g_pre=m_g_pre, w_in=m_w_in, b_gate=m_b_gate, g_q=m_g_q, w_uq=m_w_uq, g_kv=m_g_kv, w_ukv=m_w_ukv, lb_logits=m_lb_logits,
             g_hgrn=m_g_hgrn, w_branch_a=m_w_branch_a, w_branch_b=m_w_branch_b, w_out=m_w_out, g_post=m_g_post)
    v = dict(g_pre=v_g_pre, w_in=v_w_in, b_gate=v_b_gate, g_q=v_g_q, w_uq=v_w_uq, g_kv=v_g_kv, w_ukv=v_w_ukv, lb_logits=v_lb_logits,
             g_hgrn=v_g_hgrn, w_branch_a=v_w_branch_a, w_branch_b=v_w_branch_b, w_out=v_w_out, g_post=v_g_post)
    blocks = {n: _to_block(n, w[n]).astype(BF16) for n in BIG}
    w_in_all = _gather_w_in(blocks["w_in"])
    weights = _LaterWeights(blocks, w_in_all)
    state = {n: [_to_block(n, t[n]) for t in (w, m, v)] for n in BIG}
    exchange = _GradExchange(state)
    loss, grad_x, _, vec_grads = _local_step(
        x[0], loss_target[0], g_pre, _join_chips("w_in", w_in_all), b_gate, g_q, g_kv, lb_logits, g_hgrn, g_post, weights, exchange)
    vec = _vec_pack(vec_grads, loss)
    vec_started, token = _exchange_start([vec], (), "vec_start")
    sums = exchange.finish(token)
    rest = tuple(sums)
    (vec_landed,) = _exchange_wait([vec], (), vec_started, sums[rest[-1]], "vec_wait")
    mine = [sums[n] for n in rest] + [_sum_landed(vec_landed, vec, "sum_vec")]
    theirs = _sibling_exchange(mine, "sibling_grads")
    done = dict(exchange.outs)
    for k, n in enumerate(rest):
        done[n] = _adamw(mine[k], theirs[k], *state[n], "adamw_" + n)
    outs = [{}, {}, {}, {}]
    for n in BIG:
        for o, val in zip(outs, done[n]):
            o[n] = _from_block(n, val)
    vec_outs, total = _adamw_vec(mine[-1], theirs[-1], w, m, v)
    for o, vals in zip(outs, vec_outs):
        o.update(vals)
    return (total[0, 0], grad_x[None], *[o[n] for o in outs for n in WEIGHTS])
```

```python
import math

import jax
import jax.numpy as jnp
from jax import lax
from jax.experimental import pallas as pl
from jax.experimental.pallas import tpu as pltpu

F32 = jnp.float32
BF16 = jnp.bfloat16
HIGHEST = lax.Precision.HIGHEST

D = 1024
NH = 8
QK_NOPE, QK_ROPE, V_DIM = 64, 32, 64
Q_LORA, KV_LORA = 768, 256
CHUNK = 64
HG_BLOCK = 32
EPS = 1e-6
LANE = 128
P_MERGE, P_GA, P_HQ, P_HF, P_HI, P_GB, P_CQ, P_CKV, P_KPE = 0, 2048, 2560, 3072, 3584, 4096, 4608, 5376, 5632
D_P = 5760
O_CQ, O_CKV, O_KPE, O_GA, O_HQ, O_HF, O_HI, O_GB, O_MERGE = 0, 768, 1024, 1056, 1568, 2080, 2592, 3104, 3616

TM = 512
TM_MID = 256
TQ = 1024
ONES_LANE = (LANE - 1, 0)
TH = 256
HG_PAIRS = 4
VMEM_LIMIT = 56 * 1024 * 1024

ADAM_LR, ADAM_B1, ADAM_B2, ADAM_EPS, ADAM_WD, ADAM_STEP = 0.001, 0.9, 0.999, 1e-08, 0.01, 10

NT_DIMS = (((1,), (1,)), ((), ()))
TN_DIMS = (((0,), (0,)), ((), ()))


def _params(sem):
    return pltpu.CompilerParams(dimension_semantics=sem, vmem_limit_bytes=VMEM_LIMIT)


def _mm(a, b):
    return jnp.dot(a, b, preferred_element_type=F32)


def _mm_nt(a, b):
    return lax.dot_general(a, b, NT_DIMS, preferred_element_type=F32)


def _mm_tn(a, b):
    return lax.dot_general(a, b, TN_DIMS, preferred_element_type=F32)


def _sigmoid(z):
    return jax.nn.sigmoid(z)


def _rope(v, c, s1, s2):
    return v * c + pltpu.roll(v, 112, 1) * s1 + pltpu.roll(v, 16, 1) * s2


def _rope_t(dy, c, s1, s2):
    return dy * c + pltpu.roll(dy * s1, 16, 1) + pltpu.roll(dy * s2, 112, 1)


def _rope_tables(s):
    inv = 10000.0 ** (-jnp.arange(0, QK_ROPE, 2, dtype=F32) / QK_ROPE)
    ang = jnp.arange(s, dtype=F32)[:, None] * inv[None, :]
    cos, sin = jnp.cos(ang), jnp.sin(ang)
    z64, z32, o64, o32 = jnp.zeros((s, 64), F32), jnp.zeros((s, 32), F32), jnp.ones((s, 64), F32), jnp.ones((s, 32), F32)
    z16 = jnp.zeros((s, 16), F32)
    c = jnp.concatenate([o64, cos, cos, o32], axis=1)
    s1 = jnp.concatenate([z64, -sin, z16, z32], axis=1)
    s2 = jnp.concatenate([z64, z16, sin, z32], axis=1)
    return c, s1, s2


def _front_fwd(x, g_pre, w_in_pt, tokens=()):
    s = x.shape[0]
    tokens = list(tokens)

    def body(x_ref, g_ref, w_ref, *refs):
        o_ref, h_ref = refs[len(tokens):]
        xv = x_ref[...]
        r = lax.rsqrt(jnp.mean(xv * xv, axis=-1, keepdims=True) + EPS)
        h = ((xv * r) * g_ref[...]).astype(BF16)
        h_ref[...] = h
        o_ref[...] = _mm_nt(h, w_ref[...])

    return pl.pallas_call(
        body, name="front_fwd", grid=(s // TM,),
        in_specs=[pl.BlockSpec((TM, D), lambda i: (i, 0)), pl.BlockSpec((1, D), lambda i: (0, 0)),
                  pl.BlockSpec((D_P, D), lambda i: (0, 0))] + [pl.BlockSpec((8, LANE), lambda i: (0, 0))] * len(tokens),
        out_specs=[pl.BlockSpec((TM, D_P), lambda i: (i, 0)), pl.BlockSpec((TM, D), lambda i: (i, 0))],
        out_shape=[jax.ShapeDtypeStruct((s, D_P), F32), jax.ShapeDtypeStruct((s, D), BF16)],
        compiler_params=_params(("parallel",)),
    )(x, g_pre, w_in_pt, *tokens)


def _norm_rows(v, g):
    r = lax.rsqrt(jnp.mean(v * v, axis=-1, keepdims=True) + EPS)
    return (v * r) * g, r


def _qkv_fwd(proj, g_q, g_kv, w_uq_p, w_k_p, w_v_p, rc, rs1, rs2):
    s = proj.shape[0]

    def body(cq_ref, ckv_ref, kpe_ref, gq_ref, gkv_ref, wq_ref, wk_ref, wv_ref, c_ref, s1_ref, s2_ref, q_ref, k_ref, v_ref):
        c, s1, s2 = c_ref[...], s1_ref[...], s2_ref[...]
        cqn, _ = _norm_rows(cq_ref[...], gq_ref[...])
        ckvn, _ = _norm_rows(ckv_ref[...], gkv_ref[...])
        ckvn = ckvn.astype(BF16)
        qf = _mm(cqn.astype(BF16), wq_ref[...])
        kf = _mm(ckvn, wk_ref[...])
        vf = _mm(ckvn, wv_ref[...])
        kpe = _rope(kpe_ref[...], c, s1, s2)
        lane = lax.broadcasted_iota(jnp.int32, (TM, LANE), 1)
        for h in range(NH):
            blk = slice(h * LANE, (h + 1) * LANE)
            q_ref[h] = _rope(qf[:, blk], c, s1, s2).astype(BF16)
            k_ref[h] = (kf[:, blk] + kpe).astype(BF16)
            v_ref[h] = jnp.where(lane == ONES_LANE[h % 2], 1.0, vf[:, blk]).astype(BF16)

    row = lambda w, j: pl.BlockSpec((TM, w), lambda i: (i, j))
    full = lambda a: pl.BlockSpec(a.shape, lambda i: (0,) * a.ndim)
    hs = jax.ShapeDtypeStruct((NH, s, LANE), BF16)
    return pl.pallas_call(
        body, name="qkv_fwd", grid=(s // TM,),
        in_specs=[row(Q_LORA, P_CQ // Q_LORA), row(KV_LORA, P_CKV // KV_LORA), row(LANE, P_KPE // LANE),
                  full(g_q), full(g_kv), full(w_uq_p), full(w_k_p), full(w_v_p), row(LANE, 0), row(LANE, 0), row(LANE, 0)],
        out_specs=[pl.BlockSpec((NH, TM, LANE), lambda i: (0, i, 0))] * 3,
        out_shape=[hs, hs, hs],
        compiler_params=_params(("parallel",)),
    )(proj, proj, proj, g_q, g_kv, w_uq_p, w_k_p, w_v_p, rc, rs1, rs2)


LOG2E = 1.4426950408889634
QK_SCALE2 = LOG2E / math.sqrt(QK_NOPE + QK_ROPE)


HQ = TQ // 2


def _diag_visible(n):
    row = lax.broadcasted_iota(jnp.int32, (n, n), 0)
    col = lax.broadcasted_iota(jnp.int32, (n, n), 1)
    return (col // CHUNK) <= (row // CHUNK)


def _attn_fwd(q, k, vv):
    s = q.shape[1]

    def body(q_ref, k_ref, v_ref, o_ref, lse_ref):
        i = pl.program_id(1)
        qs = (q_ref[0], q_ref[1])

        def tiles(t, carry, diag):
            rows = pl.ds(pl.multiple_of(t * TQ, TQ), TQ)
            sc = [_mm_nt(qs[hh], k_ref[hh, rows, :]) for hh in range(2)]
            if diag:
                sc = [jnp.where(_diag_visible(TQ), s_, -jnp.inf) for s_ in sc]
            m_new = [jnp.maximum(carry[hh][0], jnp.max(sc[hh], axis=-1, keepdims=True)) for hh in range(2)]
            alpha = [jnp.exp2((carry[hh][0] - m_new[hh]) * QK_SCALE2) for hh in range(2)]
            p = [jnp.exp2((sc[hh] - m_new[hh]) * QK_SCALE2).astype(BF16) for hh in range(2)]
            acc = [alpha[hh] * carry[hh][1] + _mm(p[hh], v_ref[hh, rows, :]) for hh in range(2)]
            return (m_new[0], acc[0]), (m_new[1], acc[1])

        init = (jnp.full((TQ, 1), -jnp.inf, F32), jnp.zeros((TQ, LANE), F32))
        carry = lax.fori_loop(0, i, lambda t, c: tiles(t, c, False), (init, init))
        carry = tiles(i, carry, True)
        lane = lax.broadcasted_iota(jnp.int32, (TQ, LANE), 1)
        out = jnp.zeros((TQ, LANE), F32)
        for hh in range(2):
            m, acc = carry[hh]
            l = jnp.sum(jnp.where(lane == ONES_LANE[hh], acc, 0.0), axis=-1, keepdims=True)
            out = out + jnp.where((lane < V_DIM) == (hh == 0), acc, 0.0) / l
            lse_ref[hh] = jnp.broadcast_to(m * QK_SCALE2 + jnp.log(l) * LOG2E, (TQ, LANE))
        o_ref[...] = out

    return pl.pallas_call(
        body, name="attn_fwd", grid=(NH // 2, s // TQ),
        in_specs=[pl.BlockSpec((2, TQ, LANE), lambda p, i: (p, i, 0)), pl.BlockSpec((2, s, LANE), lambda p, i: (p, 0, 0)),
                  pl.BlockSpec((2, s, LANE), lambda p, i: (p, 0, 0))],
        out_specs=[pl.BlockSpec((TQ, LANE), lambda p, i: (i, p)), pl.BlockSpec((2, TQ, LANE), lambda p, i: (p, i, 0))],
        out_shape=[jax.ShapeDtypeStruct((s, NH * V_DIM), F32), jax.ShapeDtypeStruct((NH, s, LANE), F32)],
        compiler_params=_params(("parallel", "parallel")),
    )(q, k, vv)


def _lower_bound(lbl):
    a0, a1 = lbl[0:1, :], lbl[1:2, :]
    mx = jnp.maximum(a0, a1)
    e0, e1 = jnp.exp(a0 - mx), jnp.exp(a1 - mx)
    return e0 / (e0 + e1)


def _chunk_cumsum(v, reverse=False):
    pos = lax.broadcasted_iota(jnp.int32, v.shape, 0) % HG_BLOCK
    s = 1
    while s < HG_BLOCK:
        if reverse:
            v = v + jnp.where(pos < HG_BLOCK - s, pltpu.roll(v, TH - s, 0), 0.0)
        else:
            v = v + jnp.where(pos >= s, pltpu.roll(v, s, 0), 0.0)
        s *= 2
    return v


def _hgrn_gates(hq, hf, lb):
    sig = _sigmoid(hf)
    f = lb + (1.0 - lb) * sig
    g = jnp.log(f)
    kk = 1.0 - f
    r = lax.broadcasted_iota(jnp.int32, (TH, TH), 0)
    c = lax.broadcasted_iota(jnp.int32, (TH, TH), 1)
    tri = ((r // HG_BLOCK) == (c // HG_BLOCK)) & (r >= c)
    cum = _chunk_cumsum(g)
    nch = TH // HG_BLOCK
    total = _chunks(cum)[:, HG_BLOCK - 1:HG_BLOCK, :]
    lastb = jnp.broadcast_to(total, (nch, HG_BLOCK, hf.shape[-1])).reshape(hf.shape)
    e, ei, ee = jnp.exp(cum), jnp.exp(-cum), jnp.exp(lastb - cum)
    return dict(sig=sig, f=f, kk=kk, tri=tri, cum=cum, total=total, decay=jnp.exp(total), e=e, ei=ei, ee=ee,
                qd=hq * e, ki=kk * ei, ke=kk * ee)


def _chunks(v):
    return v.reshape(TH // HG_BLOCK, HG_BLOCK, v.shape[-1])


def _bmm_nt(a, b):
    return lax.dot_general(a, b, (((2,), (2,)), ((0,), (0,))), preferred_element_type=F32)


def _bmm_nn(a, b):
    return lax.dot_general(a, b, (((2,), (1,)), ((0,), (0,))), preferred_element_type=F32)


def _bmm_tn(a, b):
    return lax.dot_general(a, b, (((1,), (1,)), ((0,), (0,))), preferred_element_type=F32)


def _pair_masks():
    lane = lax.broadcasted_iota(jnp.int32, (TH, LANE), 1)
    kr = lax.broadcasted_iota(jnp.int32, (LANE, LANE), 0)
    kc = lax.broadcasted_iota(jnp.int32, (LANE, LANE), 1)
    return lane < 64, (kr // 64) == (kc // 64)


def _hgrn_fwd(proj, lbl):
    s = proj.shape[0]
    nch = TH // HG_BLOCK

    def body(hq_ref, hf_ref, hi_ref, lbl_ref, o_ref, st_ref, st):
        @pl.when(pl.program_id(1) == 0)
        def _():
            st[...] = jnp.zeros_like(st)

        m0, bd = _pair_masks()
        gt = _hgrn_gates(hq_ref[...], hf_ref[...], _lower_bound(lbl_ref[...]))
        v_b, qd, qd_b = hi_ref[...].astype(BF16), gt["qd"], gt["qd"].astype(BF16)
        ki_b, ke_b = gt["ki"].astype(BF16), gt["ke"].astype(BF16)
        pairs = [slice(u * LANE, (u + 1) * LANE) for u in range(HG_PAIRS)]
        heads = [(lanes, m0 if hh == 0 else jnp.logical_not(m0)) for lanes in pairs for hh in range(2)]
        a_b = [jnp.where(gt["tri"], _mm_nt(jnp.where(mh, qd[:, lanes], 0.0).astype(BF16), ki_b[:, lanes]), 0.0).astype(BF16)
               for lanes, mh in heads]
        intra = [jnp.where(m0, _mm(a_b[2 * u], v_b[:, lanes]), _mm(a_b[2 * u + 1], v_b[:, lanes])) for u, lanes in enumerate(pairs)]
        upd = [_bmm_tn(_chunks(v_b[:, lanes]), _chunks(ke_b[:, lanes])) for lanes in pairs]
        cur, states = [st[u] for u in range(HG_PAIRS)], [[] for _ in pairs]
        for n in range(nch):
            for u, lanes in enumerate(pairs):
                states[u].append(cur[u])
                cur[u] = gt["decay"][n][:, lanes] * cur[u] + jnp.where(bd, upd[u][n], 0.0)
        entering = [jnp.stack(s_) for s_ in states]
        for u in range(HG_PAIRS):
            st[u] = cur[u]
            st_ref[u] = entering[u]
        for u, lanes in enumerate(pairs):
            o_ref[:, lanes] = intra[u] + _bmm_nt(_chunks(qd_b[:, lanes]), entering[u].astype(BF16)).reshape(TH, LANE)

    wide = HG_PAIRS * LANE
    col = lambda base: pl.BlockSpec((TH, wide), lambda p, i: (i, base // wide + p))
    return pl.pallas_call(
        body, name="hgrn_fwd", grid=(NH // 2 // HG_PAIRS, s // TH),
        in_specs=[col(P_HQ), col(P_HF), col(P_HI), pl.BlockSpec((2, wide), lambda p, i: (0, p))],
        out_specs=[pl.BlockSpec((TH, wide), lambda p, i: (i, p)),
                   pl.BlockSpec((HG_PAIRS, nch, LANE, LANE), lambda p, i: (p, i, 0, 0))],
        out_shape=[jax.ShapeDtypeStruct((s, 512), F32), jax.ShapeDtypeStruct((NH // 2, s // HG_BLOCK, LANE, LANE), F32)],
        scratch_shapes=[pltpu.VMEM((HG_PAIRS, LANE, LANE), F32)],
        compiler_params=_params(("parallel", "arbitrary")),
    )(proj, proj, proj, lbl)


def _group_sum(v):
    low = lax.broadcasted_iota(jnp.int32, (v.shape[0], LANE), 1) < V_DIM
    blocks = []
    for b in range(v.shape[1] // LANE):
        blk = v[:, b * LANE:(b + 1) * LANE]
        s_low = jnp.sum(jnp.where(low, blk, 0.0), axis=-1, keepdims=True)
        s_high = jnp.sum(jnp.where(low, 0.0, blk), axis=-1, keepdims=True)
        blocks.append(jnp.where(low, s_low, s_high))
    return jnp.concatenate(blocks, axis=1)


def _dsilu(z, sg):
    return sg * (1.0 + z * (1.0 - sg))


def _mid(proj, attn, o_raw, x, tgt, g_hg, b_gate, g_post, wa, wb, w_out):
    s = x.shape[0]

    def body(attn_ref, ga_ref, o_ref, gb_ref, mg_ref, x_ref, t_ref, ghg_ref, bg_ref, gp_ref, wa_ref, wb_ref, wo_ref,
             loss_ref, dout_ref, dattn_ref, dga_ref, dor_ref, dgb_ref, dmg_ref, dwo_ref, dwa_ref, dwb_ref, dgp_ref, dbg_ref, dghg_ref):
        @pl.when(pl.program_id(0) == 0)
        def _():
            for rf in (loss_ref, dwo_ref, dwa_ref, dwb_ref, dgp_ref, dbg_ref, dghg_ref):
                rf[...] = jnp.zeros_like(rf)

        attn, za, orw, zb = attn_ref[...], ga_ref[...], o_ref[...], gb_ref[...]
        ghg, gp = ghg_ref[...], gp_ref[...]
        sga = _sigmoid(za)
        sa = za * sga
        ga = attn * sa
        ga_b = ga.astype(BF16)
        ya = _mm(ga_b, wa_ref[...])
        sgb = _sigmoid(zb)
        sb = zb * sgb
        rh = lax.rsqrt(_group_sum(orw * orw) * (1.0 / V_DIM) + EPS)
        on = (orw * rh) * ghg
        gb = on * sb
        gb_b = gb.astype(BF16)
        yb = _mm(gb_b, wb_ref[...])
        gates = _sigmoid(mg_ref[...] + bg_ref[...])
        g0, g1 = gates[:, :D], gates[:, D:]
        m_b = (g0 * ya + g1 * yb).astype(BF16)
        y = _mm(m_b, wo_ref[...])
        ry = lax.rsqrt(jnp.mean(y * y, axis=-1, keepdims=True) + EPS)
        out = x_ref[...] + (y * ry) * gp
        err = out - t_ref[...]
        loss_ref[...] += 0.5 * jnp.sum(jnp.mean(err * err, axis=-1, keepdims=True), axis=0, keepdims=True)
        dout = err * (1.0 / D)
        dout_ref[...] = dout
        dgp_ref[...] += jnp.sum(dout * (y * ry), axis=0, keepdims=True)
        dgy = dout * gp
        dy = ry * dgy - y * (ry * ry * ry) * jnp.mean(y * dgy, axis=-1, keepdims=True)
        dy_b = dy.astype(BF16)
        dm = _mm_nt(dy_b, wo_ref[...])
        dya_b, dyb_b = (dm * g0).astype(BF16), (dm * g1).astype(BF16)
        dga = _mm_nt(dya_b, wa_ref[...])
        dgb = _mm_nt(dyb_b, wb_ref[...])
        dwo_ref[...] += _mm_tn(m_b, dy_b)
        dwa_ref[...] += _mm_tn(ga_b, dya_b)
        dwb_ref[...] += _mm_tn(gb_b, dyb_b)
        dg0, dg1 = dm * ya, dm * yb
        dmg = jnp.concatenate([dg0 * g0 * (1.0 - g0), dg1 * g1 * (1.0 - g1)], axis=1)
        dmg_ref[...] = dmg.astype(BF16)
        dbg_ref[...] += jnp.sum(dmg, axis=0, keepdims=True)
        dattn_ref[...] = dga * sa
        dga_ref[...] = (dga * attn * _dsilu(za, sga)).astype(BF16)
        dgb_ref[...] = (dgb * on * _dsilu(zb, sgb)).astype(BF16)
        don = dgb * sb
        dghg_ref[...] += jnp.sum(don * (orw * rh), axis=0, keepdims=True)
        dgo = don * ghg
        dor_ref[...] = rh * dgo - orw * (rh * rh * rh) * (_group_sum(orw * dgo) * (1.0 / V_DIM))

    row = lambda w, j=0: pl.BlockSpec((TM_MID, w), lambda i: (i, j))
    full = lambda a: pl.BlockSpec(a.shape, lambda i: (0,) * a.ndim)
    acc = lambda shape: pl.BlockSpec(shape, lambda i: (0, 0))
    sds = jax.ShapeDtypeStruct
    return pl.pallas_call(
        body, name="mid", grid=(s // TM_MID,),
        in_specs=[row(512), row(512, P_GA // 512), row(512), row(512, P_GB // 512), row(2048, P_MERGE // 2048), row(D), row(D),
                  full(g_hg), full(b_gate), full(g_post), full(wa), full(wb), full(w_out)],
        out_specs=[acc((1, 1)), row(D), row(512), row(512), row(512), row(512), row(2048),
                   acc((D, D)), acc((512, D)), acc((512, D)), acc((1, D)), acc((1, 2048)), acc((1, 512))],
        out_shape=[sds((1, 1), F32), sds((s, D), F32), sds((s, 512), F32), sds((s, 512), BF16), sds((s, 512), F32), sds((s, 512), BF16),
                   sds((s, 2048), BF16), sds((D, D), F32), sds((512, D), F32), sds((512, D), F32), sds((1, D), F32),
                   sds((1, 2048), F32), sds((1, 512), F32)],
        compiler_params=_params(("arbitrary",)),
    )(attn, proj, o_raw, proj, proj, x, tgt, g_hg, b_gate, g_post, wa, wb, w_out)


def _attn_bwd(q, k, vv, attn, dattn, lse, token):
    s = q.shape[1]
    nt = s // TQ
    scale = 1.0 / math.sqrt(QK_NOPE + QK_ROPE)

    def body(q_ref, k_ref, v_ref, o_ref, do_ref, lse_ref, token_ref, dq_ref, dk_ref, dv_ref, do_s, delta_s):
        j = pl.program_id(1)

        @pl.when(j == 0)
        def _():
            dq_ref[...] = jnp.zeros_like(dq_ref)
            lane = lax.broadcasted_iota(jnp.int32, (TQ, LANE), 1)

            @pl.loop(0, nt)
            def _(i):
                rows = pl.ds(pl.multiple_of(i * TQ, TQ), TQ)
                do, o = do_ref[rows, :], o_ref[rows, :]
                for hh in range(2):
                    doh = jnp.where((lane < 64) if hh == 0 else (lane >= 64), do, 0.0)
                    do_s[hh, rows, :] = doh.astype(BF16)
                    delta_s[hh, rows, :] = jnp.broadcast_to(jnp.sum(doh * o, axis=-1, keepdims=True), (TQ, LANE))

        kjs, vjs = (k_ref[0], k_ref[1]), (v_ref[0], v_ref[1])

        def tile(hh, start, size, kj, vj, diag):
            rows = pl.ds(pl.multiple_of(start, size), size)
            wide = lambda a: jnp.concatenate([a] * (kj.shape[0] // LANE), axis=1)
            qi, do_b = q_ref[hh, rows, :], do_s[hh, rows, :]
            sc, dp = _mm_nt(qi, kj), _mm_nt(do_b, vj)
            p = jnp.exp2(sc * QK_SCALE2 - wide(lse_ref[hh, rows, :]))
            if diag:
                p = jnp.where(_diag_visible(size), p, 0.0)
            ds_b = (p * (dp - wide(delta_s[hh, rows, :]))).astype(BF16)
            dv, dk = _mm_tn(do_b, p.astype(BF16)), _mm_tn(qi, ds_b)
            dq_ref[hh, rows, :] += _mm(ds_b, kj)
            return dk, dv

        def step(i, carry):
            new = [tile(hh, i * TQ, TQ, kjs[hh], vjs[hh], False) for hh in range(2)]
            return tuple((carry[hh][0] + new[hh][0], carry[hh][1] + new[hh][1]) for hh in range(2))

        def diagonal(hh):
            k0, k1, v0, v1 = kjs[hh][:HQ], kjs[hh][HQ:], vjs[hh][:HQ], vjs[hh][HQ:]
            a = tile(hh, j * TQ, HQ, k0, v0, True)
            b = tile(hh, j * TQ + HQ, HQ, k0, v0, False)
            c = tile(hh, j * TQ + HQ, HQ, k1, v1, True)
            return jnp.concatenate([a[0] + b[0], c[0]], axis=1), jnp.concatenate([a[1] + b[1], c[1]], axis=1)

        carry = lax.fori_loop(j + 1, nt, step, (diagonal(0), diagonal(1)))
        for hh in range(2):
            dk_ref[hh] = carry[hh][0].T * scale
            dv_ref[hh] = carry[hh][1].T

        @pl.when(j == nt - 1)
        def _():
            dq_ref[...] = dq_ref[...] * scale

    whole = pl.BlockSpec((2, s, LANE), lambda p, j: (p, 0, 0))
    tile_spec = pl.BlockSpec((2, TQ, LANE), lambda p, j: (p, j, 0))
    cols = pl.BlockSpec((s, LANE), lambda p, j: (0, p))
    hs = jax.ShapeDtypeStruct((NH, s, LANE), F32)
    return pl.pallas_call(
        body, name="attn_bwd", grid=(NH // 2, nt),
        in_specs=[whole, tile_spec, tile_spec, cols, cols, whole, pl.BlockSpec((8, LANE), lambda p, j: (0, 0))],
        out_specs=[whole, tile_spec, tile_spec],
        out_shape=[hs, hs, hs],
        scratch_shapes=[pltpu.VMEM((2, s, LANE), BF16), pltpu.VMEM((2, s, LANE), F32)],
        compiler_params=_params(("parallel", "arbitrary")),
    )(q, k, vv, attn, dattn, lse, token)


def _hgrn_bwd(proj, lbl, states, do_raw):
    s = proj.shape[0]
    nt = s // TH
    nch = TH // HG_BLOCK

    def body(hq_ref, hf_ref, hi_ref, lbl_ref, st_ref, do_ref, dh_ref, dlbl_ref, dst, dlb):
        step = pl.program_id(1)

        @pl.when(step == 0)
        def _():
            dst[...] = jnp.zeros_like(dst)
            dlb[...] = jnp.zeros_like(dlb)

        m0, bd = _pair_masks()
        lb = _lower_bound(lbl_ref[...])
        gt = _hgrn_gates(hq_ref[...], hf_ref[...], lb)
        do = do_ref[...]
        qd, ki, ke = gt["qd"], gt["ki"], gt["ke"]
        v_b, do_b = hi_ref[...].astype(BF16), do.astype(BF16)
        qd_b, ki_b, ke_b = qd.astype(BF16), ki.astype(BF16), ke.astype(BF16)
        pairs = [slice(u * LANE, (u + 1) * LANE) for u in range(HG_PAIRS)]
        heads = [(lanes, m0 if hh == 0 else jnp.logical_not(m0)) for lanes in pairs for hh in range(2)]
        a_b = [jnp.where(gt["tri"], _mm_nt(jnp.where(mh, qd[:, lanes], 0.0).astype(BF16), ki_b[:, lanes]), 0.0).astype(BF16)
               for lanes, mh in heads]
        doh_b = [jnp.where(mh, do[:, lanes], 0.0).astype(BF16) for lanes, mh in heads]
        da_b = [jnp.where(gt["tri"], _mm_nt(d, v_b[:, lanes]), 0.0).astype(BF16) for d, (lanes, _) in zip(doh_b, heads)]
        dv_p, dqd_p, dki_p = [], [], []
        for u, lanes in enumerate(pairs):
            e, o = 2 * u, 2 * u + 1
            dv_p.append(_mm_tn(a_b[e], doh_b[e]) + _mm_tn(a_b[o], doh_b[o]))
            dqd_p.append(jnp.where(m0, _mm(da_b[e], ki_b[:, lanes]), _mm(da_b[o], ki_b[:, lanes])))
            dki_p.append(jnp.where(m0, _mm_tn(da_b[e], qd_b[:, lanes]), _mm_tn(da_b[o], qd_b[:, lanes])))
        fed = [_bmm_tn(_chunks(do_b[:, lanes]), _chunks(qd_b[:, lanes])) for lanes in pairs]
        ds, left = [dst[u] for u in range(HG_PAIRS)], [[None] * nch for _ in pairs]
        for n in reversed(range(nch)):
            for u, lanes in enumerate(pairs):
                left[u][n] = ds[u]
                ds[u] = gt["decay"][n][:, lanes] * ds[u] + jnp.where(bd, fed[u][n], 0.0)
        for u in range(HG_PAIRS):
            dst[u] = ds[u]
        leaving = [jnp.stack(l_) for l_ in left]
        dke_p, dlast_p = [], []
        for u, lanes in enumerate(pairs):
            entering, leaving_b = st_ref[u], leaving[u].astype(BF16)
            dke3 = _bmm_nn(_chunks(v_b[:, lanes]), leaving_b)
            dv_p[u] = dv_p[u] + _bmm_nt(_chunks(ke_b[:, lanes]), leaving_b).reshape(TH, LANE)
            dqd_p[u] = dqd_p[u] + _bmm_nn(_chunks(do_b[:, lanes]), entering.astype(BF16)).reshape(TH, LANE)
            dke_p.append(dke3.reshape(TH, LANE))
            dlast_p.append(jnp.sum(dke3 * _chunks(ke[:, lanes]), axis=1, keepdims=True)
                           + jnp.sum(leaving[u] * entering, axis=1, keepdims=True) * gt["decay"][:, :, lanes])
        cat = lambda parts: jnp.concatenate(parts, axis=-1)
        dv, dqd, dki, dke, dlast = cat(dv_p), cat(dqd_p), cat(dki_p), cat(dke_p), cat(dlast_p)
        dk = dki * gt["ei"] + dke * gt["ee"]
        dcum = dqd * qd - dki * ki - dke * ke
        dg = _chunk_cumsum(dcum, reverse=True) + jnp.broadcast_to(dlast, (nch, HG_BLOCK, dlast.shape[-1])).reshape(dcum.shape)
        sig = gt["sig"]
        df = dg / gt["f"] - dk
        dlb[...] += jnp.sum(df * (1.0 - sig), axis=0, keepdims=True)
        dh_ref[0] = (dqd * gt["e"]).astype(BF16)
        dh_ref[1] = ((df * (1.0 - lb)) * sig * (1.0 - sig)).astype(BF16)
        dh_ref[2] = dv.astype(BF16)

        @pl.when(step == nt - 1)
        def _():
            lb = _lower_bound(lbl_ref[...])
            da0 = dlb[...] * lb * (1.0 - lb)
            dlbl_ref[...] = jnp.concatenate([da0, -da0], axis=0)

    wide = HG_PAIRS * LANE
    col = lambda base: pl.BlockSpec((TH, wide), lambda p, i: (nt - 1 - i, base // wide + p))
    tile = pl.BlockSpec((TH, wide), lambda p, i: (nt - 1 - i, p))
    sds = jax.ShapeDtypeStruct
    return pl.pallas_call(
        body, name="hgrn_bwd", grid=(NH // 2 // HG_PAIRS, nt),
        in_specs=[col(P_HQ), col(P_HF), col(P_HI), pl.BlockSpec((2, wide), lambda p, i: (0, p)),
                  pl.BlockSpec((HG_PAIRS, nch, LANE, LANE), lambda p, i: (p, nt - 1 - i, 0, 0)), tile],
        out_specs=[pl.BlockSpec((3, TH, wide), lambda p, i: (0, nt - 1 - i, p)), pl.BlockSpec((2, wide), lambda p, i: (0, p))],
        out_shape=[sds((3, s, 512), BF16), sds((2, 512), F32)],
        scratch_shapes=[pltpu.VMEM((HG_PAIRS, LANE, LANE), F32), pltpu.VMEM((1, wide), F32)],
        compiler_params=_params(("parallel", "arbitrary")),
    )(proj, proj, proj, lbl, states, do_raw)


def _norm_rows_bwd(v, r, g, dn):
    dgv = dn * g
    return r * dgv - v * (r * r * r) * jnp.mean(v * dgv, axis=-1, keepdims=True)


def _qkv_bwd(proj, dq, dk, dvv, g_q, g_kv, w_uq_p, w_k_p, w_v_p, rc, rs1, rs2):
    s = proj.shape[0]

    def body(cq_ref, ckv_ref, dq_ref, dk_ref, dv_ref, gq_ref, gkv_ref, wq_ref, wk_ref, wv_ref, c_ref, s1_ref, s2_ref,
             dcq_ref, dckv_ref, dkpe_ref, dwq_ref, dwk_ref, dwv_ref, dgq_ref, dgkv_ref):
        @pl.when(pl.program_id(0) == 0)
        def _():
            for rf in (dwq_ref, dwk_ref, dwv_ref, dgq_ref, dgkv_ref):
                rf[...] = jnp.zeros_like(rf)

        c, s1, s2 = c_ref[...], s1_ref[...], s2_ref[...]
        cq, ckv = cq_ref[...], ckv_ref[...]
        gq, gkv = gq_ref[...], gkv_ref[...]
        cqn, rq = _norm_rows(cq, gq)
        ckvn, rkv = _norm_rows(ckv, gkv)
        cqn_b, ckvn_b = cqn.astype(BF16), ckvn.astype(BF16)
        dqf = jnp.concatenate([_rope_t(dq_ref[h], c, s1, s2) for h in range(NH)], axis=1).astype(BF16)
        dkf = jnp.concatenate([dk_ref[h] for h in range(NH)], axis=1).astype(BF16)
        dvf = jnp.concatenate([dv_ref[h] for h in range(NH)], axis=1).astype(BF16)
        dkpe = dk_ref[0]
        for h in range(1, NH):
            dkpe = dkpe + dk_ref[h]
        lane = lax.broadcasted_iota(jnp.int32, (TM, LANE), 1)
        dkpe = jnp.where((lane >= QK_NOPE) & (lane < QK_NOPE + QK_ROPE), dkpe, 0.0)
        dkpe_ref[...] = _rope_t(dkpe, c, s1, s2).astype(BF16)
        dcqn = _mm_nt(dqf, wq_ref[...])
        dckvn = _mm_nt(dkf, wk_ref[...]) + _mm_nt(dvf, wv_ref[...])
        dwq_ref[...] += _mm_tn(cqn_b, dqf)
        dwk_ref[...] += _mm_tn(ckvn_b, dkf)
        dwv_ref[...] += _mm_tn(ckvn_b, dvf)
        dgq_ref[...] += jnp.sum(dcqn * (cq * rq), axis=0, keepdims=True)
        dgkv_ref[...] += jnp.sum(dckvn * (ckv * rkv), axis=0, keepdims=True)
        dcq_ref[...] = _norm_rows_bwd(cq, rq, gq, dcqn).astype(BF16)
        dckv_ref[...] = _norm_rows_bwd(ckv, rkv, gkv, dckvn).astype(BF16)

    row = lambda w, j=0: pl.BlockSpec((TM, w), lambda i: (i, j))
    full = lambda a: pl.BlockSpec(a.shape, lambda i: (0,) * a.ndim)
    acc = lambda shape: pl.BlockSpec(shape, lambda i: (0, 0))
    heads = pl.BlockSpec((NH, TM, LANE), lambda i: (0, i, 0))
    sds = jax.ShapeDtypeStruct
    return pl.pallas_call(
        body, name="qkv_bwd", grid=(s // TM,),
        in_specs=[row(Q_LORA, P_CQ // Q_LORA), row(KV_LORA, P_CKV // KV_LORA), heads, heads, heads,
                  full(g_q), full(g_kv), full(w_uq_p), full(w_k_p), full(w_v_p), row(LANE), row(LANE), row(LANE)],
        out_specs=[row(Q_LORA), row(KV_LORA), row(LANE), acc((Q_LORA, D)), acc((KV_LORA, D)), acc((KV_LORA, D)),
                   acc((1, Q_LORA)), acc((1, KV_LORA))],
        out_shape=[sds((s, Q_LORA), BF16), sds((s, KV_LORA), BF16), sds((s, LANE), BF16), sds((Q_LORA, D), F32),
                   sds((KV_LORA, D), F32), sds((KV_LORA, D), F32), sds((1, Q_LORA), F32), sds((1, KV_LORA), F32)],
        compiler_params=_params(("arbitrary",)),
    )(proj, proj, dq, dk, dvv, g_q, g_kv, w_uq_p, w_k_p, w_v_p, rc, rs1, rs2)


def _front_bwd(x, dout, dmg, dga, dh3, dgb, dcq, dckv, dkpe, g_pre, w_in_pt, token):
    s = x.shape[0]

    def body(x_ref, do_ref, dmg_ref, dga_ref, dh3_ref, dgb_ref, dcq_ref, dckv_ref, dkpe_ref, g_ref, w_ref, token_ref, gx_ref, dg_ref):
        @pl.when(pl.program_id(0) == 0)
        def _():
            dg_ref[...] = jnp.zeros_like(dg_ref)

        xv, g = x_ref[...], g_ref[...]
        _, r = _norm_rows(xv, g)
        pieces = ((dmg_ref[...], P_MERGE), (dga_ref[...], P_GA), (dh3_ref[0], P_HQ), (dh3_ref[1], P_HF), (dh3_ref[2], P_HI),
                  (dgb_ref[...], P_GB), (dcq_ref[...], P_CQ), (dckv_ref[...], P_CKV), (dkpe_ref[...], P_KPE))
        dh = jnp.zeros((TM, D), F32)
        for piece, off in pieces:
            dh = dh + _mm(piece, w_ref[off:off + piece.shape[1], :])
        dg_ref[...] += jnp.sum(dh * (xv * r), axis=0, keepdims=True)
        gx_ref[...] = do_ref[...] + _norm_rows_bwd(xv, r, g, dh)

    row = lambda w: pl.BlockSpec((TM, w), lambda i: (i, 0))
    full = lambda a: pl.BlockSpec(a.shape, lambda i: (0,) * a.ndim)
    sds = jax.ShapeDtypeStruct
    return pl.pallas_call(
        body, name="front_bwd", grid=(s // TM,),
        in_specs=[row(D), row(D), row(2048), row(512), pl.BlockSpec((3, TM, 512), lambda i: (0, i, 0)), row(512), row(Q_LORA),
                  row(KV_LORA), row(LANE), full(g_pre), full(w_in_pt), pl.BlockSpec(memory_space=pl.ANY)],
        out_specs=[row(D), pl.BlockSpec((1, D), lambda i: (0, 0))],
        out_shape=[sds((s, D), F32), sds((1, D), F32)],
        compiler_params=_params(("arbitrary",)),
    )(x, dout, dmg, dga, dh3, dgb, dcq, dckv, dkpe, g_pre, w_in_pt, token)


TK_GRAD = 1024


def _win_grad(h, pieces, name):
    s = h.shape[0]
    n = len(pieces)

    def body(h_ref, *refs):
        @pl.when(pl.program_id(0) == 0)
        def _():
            for o_ref in refs[n:]:
                o_ref[...] = jnp.zeros_like(o_ref)

        hv = h_ref[...]
        for d_ref, o_ref in zip(refs[:n], refs[n:]):
            if len(d_ref.shape) == 3:
                for k in range(d_ref.shape[0]):
                    o_ref[k] += _mm_tn(d_ref[k], hv)
            else:
                o_ref[...] += _mm_tn(d_ref[...], hv)

    def in_spec(p):
        if p.ndim == 3:
            return pl.BlockSpec((p.shape[0], TK_GRAD, p.shape[2]), lambda kk: (0, kk, 0))
        return pl.BlockSpec((TK_GRAD, p.shape[1]), lambda kk: (kk, 0))

    out_shapes = [(p.shape[0], p.shape[2], D) if p.ndim == 3 else (p.shape[1], D) for p in pieces]
    return pl.pallas_call(
        body, name=name, grid=(s // TK_GRAD,),
        in_specs=[pl.BlockSpec((TK_GRAD, D), lambda kk: (kk, 0))] + [in_spec(p) for p in pieces],
        out_specs=[pl.BlockSpec(sh, lambda kk, nd=len(sh): (0,) * nd) for sh in out_shapes],
        out_shape=[jax.ShapeDtypeStruct(sh, F32) for sh in out_shapes],
        compiler_params=_params(("arbitrary",)),
    )(h, *pieces)


def _pad_win_t(w_in_t):
    z = lambda n: jnp.zeros((n, w_in_t.shape[1]), w_in_t.dtype)
    sl = lambda o, n: w_in_t[o:o + n]
    return jnp.concatenate([sl(O_MERGE, 2048), sl(O_GA, 512), sl(O_HQ, 512), sl(O_HF, 512), sl(O_HI, 512), sl(O_GB, 512),
                            sl(O_CQ, Q_LORA), sl(O_CKV, KV_LORA), z(64), sl(O_KPE, QK_ROPE), z(32)], axis=0)


def _pad_wuq(w_uq):
    rows = w_uq.shape[0]
    w = w_uq.reshape(rows, NH, QK_NOPE + QK_ROPE)
    return jnp.pad(w, ((0, 0), (0, 0), (0, LANE - QK_NOPE - QK_ROPE))).reshape(rows, NH * LANE)


def _unpad_wuq(g):
    return g.reshape(Q_LORA, NH, LANE)[:, :, :QK_NOPE + QK_ROPE].reshape(Q_LORA, NH * (QK_NOPE + QK_ROPE))


def _pad_wukv(w_ukv):
    heads = w_ukv.shape[1] // (QK_NOPE + V_DIM)
    w = w_ukv.reshape(KV_LORA, heads, QK_NOPE + V_DIM)
    w_k = jnp.pad(w[:, :, :QK_NOPE], ((0, 0), (0, 0), (0, LANE - QK_NOPE))).reshape(KV_LORA, heads * LANE)
    wv = w[:, :, QK_NOPE:].reshape(KV_LORA, heads // 2, 2, 1, V_DIM)
    eye = jnp.eye(2, dtype=w.dtype).reshape(1, 1, 2, 2, 1)
    return w_k, (wv * eye).reshape(KV_LORA, heads * LANE)


def _unpad_wukv(gk, gv):
    gk = gk.reshape(KV_LORA, NH, LANE)[:, :, :QK_NOPE]
    gv = gv.reshape(KV_LORA, NH // 2, 2, 2, V_DIM)
    gv = jnp.stack([gv[:, :, 0, 0], gv[:, :, 1, 1]], axis=2).reshape(KV_LORA, NH, V_DIM)
    return jnp.concatenate([gk, gv], axis=-1).reshape(KV_LORA, NH * (QK_NOPE + V_DIM))


def _local_step(x, tgt, g_pre, w_in_t, b_gate, g_q, g_kv, lb_logits, g_hgrn, g_post, weights, exchange=None):
    s = x.shape[0]
    w_in_p = _pad_win_t(w_in_t)
    rc, rs1, rs2 = _rope_tables(s)
    g_hg = jnp.tile(g_hgrn, (1, NH))

    proj, h = _front_fwd(x, g_pre, w_in_p, weights.tokens)
    w_uq_p, w_k_p, w_v_p = weights.qkv(h)
    q, k, vv = _qkv_fwd(proj, g_q, g_kv, w_uq_p, w_k_p, w_v_p, rc, rs1, rs2)
    attn, lse = _attn_fwd(q, k, vv)
    o_raw, states = _hgrn_fwd(proj, lb_logits)
    wa, wb, w_out = weights.mid(o_raw)
    (loss, dout, dattn, dga, dor, dgb, dmg, d_wout, d_wa, d_wb, d_gpost, d_bgate, d_ghg) = _mid(
        proj, attn, o_raw, x, tgt, g_hg, b_gate, g_post, wa, wb, w_out)
    w_mg, w_ga, w_gb = _win_grad(h, [dmg, dga, dgb], "win_grad_mid")
    dh3, d_lbl = _hgrn_bwd(proj, lb_logits, states, dor)
    (w_h3,) = _win_grad(h, [dh3], "win_grad_hgrn")
    d_win_rest = jnp.concatenate([w_ga, w_h3[0], w_h3[1], w_h3[2], w_gb, w_mg], axis=0)
    early = dict(w_in_rest=d_win_rest, w_branch_a=d_wa, w_branch_b=d_wb, w_out=d_wout)
    token = exchange.start_early(early) if exchange else jnp.zeros((8, LANE), F32)
    dq, dk, dvv = _attn_bwd(q, k, vv, attn, dattn, lse, token)
    dcq, dckv, dkpe, d_wuq_p, d_wk_p, d_wv_p, d_gq, d_gkv = _qkv_bwd(proj, dq, dk, dvv, g_q, g_kv, w_uq_p, w_k_p, w_v_p, rc, rs1, rs2)
    w_cq, w_ckv, w_kpe = _win_grad(h, [dcq, dckv, dkpe], "win_grad_qkv")
    d_win_qkv = jnp.concatenate([w_cq, w_ckv, w_kpe[64:64 + QK_ROPE]], axis=0)
    late = dict(w_in_qkv=d_win_qkv, w_uq=_unpad_wuq(d_wuq_p), w_ukv=_unpad_wukv(d_wk_p, d_wv_p))
    token = exchange.start_late(late) if exchange else jnp.zeros((8, LANE), F32)
    grad_x, d_gpre = _front_bwd(x, dout, dmg, dga, dh3, dgb, dcq, dckv, dkpe, g_pre, w_in_p, token)
    vec_grads = dict(g_pre=d_gpre, b_gate=d_bgate, g_q=d_gq, g_kv=d_gkv, lb_logits=d_lbl, g_hgrn=d_ghg, g_post=d_gpost)
    return loss, grad_x, dict(early, **late), vec_grads


SHARD_SHAPES = (("w_in", (1416, 1024)), ("w_uq", (192, 768)), ("w_ukv", (256, 256)), ("w_branch_a", (512, 256)),
                ("w_branch_b", (512, 256)), ("w_out", (256, 1024)))
BIG = tuple(n for n, _ in SHARD_SHAPES)
ROW_SHARDED = ("w_in", "w_uq", "w_out")
N_CHIPS = 4
QKV_ROWS = Q_LORA + KV_LORA + QK_ROPE
W_IN_FORWARD_CUT = 704


def _to_block(name, a):
    return a[0].T if name == "w_in" else a[0]


def _from_block(name, a):
    return a.T[None] if name == "w_in" else a[None]
VEC_ROWS = (("g_pre", 0, 1024), ("b_gate", 1, 2048), ("g_q", 2, 768), ("g_kv", 3, 256), ("g_hgrn", 6, 64), ("g_post", 7, 1024))
VEC_LB_ROW = 4
VEC_SHAPE = (8, 2048)


def _split_by_chip(name, g):
    a, b = dict(SHARD_SHAPES)[name]
    return g.reshape(N_CHIPS, a, b) if name in ROW_SHARDED else g.reshape(a, N_CHIPS, b).transpose(1, 0, 2)


def _join_chips(name, w):
    a, b = dict(SHARD_SHAPES)[name]
    return w.reshape(N_CHIPS * a, b) if name in ROW_SHARDED else w.transpose(1, 0, 2).reshape(a, N_CHIPS * b)


MESH = pl.DeviceIdType.MESH
HBM = pl.BlockSpec(memory_space=pltpu.HBM)


def _mesh_place():
    x, y, c = lax.axis_index("x"), lax.axis_index("y"), lax.axis_index("c")
    return x, y, c, 2 * x + y, [(1 - x, y), (x, 1 - y), (1 - x, 1 - y)]


def _remote(src, dst, send_sems, recv_sems, k, to):
    return pltpu.make_async_remote_copy(src_ref=src, dst_ref=dst, send_sem=send_sems.at[k], recv_sem=recv_sems.at[k],
                                        device_id=to, device_id_type=MESH)


def _gather_w_in(shard):
    a, b = shard.shape
    cut = W_IN_FORWARD_CUT

    def body(src, out, ici_send, ici_recv, d2d_send, d2d_recv, local_sem):
        x, y, c = lax.axis_index("x"), lax.axis_index("y"), lax.axis_index("c")
        me, xn, yn, dg = 2 * x + y, 2 * (1 - x) + y, 2 * x + (1 - y), 2 * (1 - x) + (1 - y)
        to_x, to_y, sibling = (1 - x, y, c), (x, 1 - y, c), (x, y, 1 - c)
        first, rest = pl.ds(0, cut), pl.ds(cut, a - cut)

        whole = lambda ref, which: ref.at[:, pl.ds(pl.multiple_of(which * (b // 2), b // 2), b // 2)]
        own = pltpu.make_async_copy(src, out.at[me], local_sem)
        own.start()
        sends = [_remote(whole(src, c), whole(out.at[me], c), ici_send, ici_recv, 0, to_x),
                 _remote(whole(src, c), whole(out.at[me], c), ici_send, ici_recv, 1, to_y)]
        for cp in sends:
            cp.start()

        def landed(slot, rows, k, d2d_k, src_dev):
            piece = whole(out.at[slot], c) if rows is None else out.at[slot].at[rows, pl.ds(pl.multiple_of(c * (b // 2), b // 2), b // 2)]
            _remote(piece, piece, ici_send, ici_recv, k, src_dev).wait_recv()
            cp = _remote(piece, piece, d2d_send, d2d_recv, d2d_k, sibling)
            cp.start()
            sends.append(cp)
            return piece

        def pass_on(slot, rows, k, to):
            piece = out.at[slot].at[rows, pl.ds(pl.multiple_of(c * (b // 2), b // 2), b // 2)]
            cp = _remote(piece, piece, ici_send, ici_recv, k, to)
            cp.start()
            sends.append(cp)

        landed(xn, None, 0, 0, to_x)
        pass_on(xn, first, 2, to_y)
        landed(yn, None, 1, 1, to_y)
        pass_on(yn, rest, 3, to_x)
        landed(dg, first, 2, 2, to_y)
        landed(dg, rest, 3, 3, to_x)
        other = pl.ds(pl.multiple_of((1 - c) * (b // 2), b // 2), b // 2)
        for d2d_k, (slot, rows) in enumerate(((xn, None), (yn, None), (dg, first), (dg, rest))):
            piece = out.at[slot].at[:, other] if rows is None else out.at[slot].at[rows, other]
            _remote(piece, piece, d2d_send, d2d_recv, d2d_k, sibling).wait_recv()
        for cp in sends:
            cp.wait_send()
        own.wait()

    sems = pltpu.SemaphoreType.DMA((4,))
    return pl.pallas_call(
        body, name="gather_w_in", in_specs=[HBM], out_specs=HBM,
        out_shape=jax.ShapeDtypeStruct((N_CHIPS, a, b), shard.dtype),
        scratch_shapes=[sems, sems, sems, sems, pltpu.SemaphoreType.DMA],
        compiler_params=pltpu.CompilerParams(has_side_effects=True),
    )(shard)


def _sibling_exchange(srcs, name, after=None):
    n = len(srcs)
    extra = [] if after is None else [after]

    def body(*refs):
        src_refs, outs = refs[:n], refs[n + len(extra):2 * n + len(extra)]
        send_sems, recv_sems = refs[2 * n + len(extra):]
        sibling = (lax.axis_index("x"), lax.axis_index("y"), 1 - lax.axis_index("c"))
        copies = [_remote(src_refs[k], outs[k], send_sems, recv_sems, k, sibling) for k in range(n)]
        for cp in copies:
            cp.start()
        for cp in copies:
            cp.wait()

    sems = pltpu.SemaphoreType.DMA((n,))
    return pl.pallas_call(
        body, name=name, in_specs=[HBM] * n + [pl.BlockSpec(memory_space=pl.ANY)] * len(extra), out_specs=[HBM] * n,
        out_shape=[jax.ShapeDtypeStruct(s.shape, s.dtype) for s in srcs],
        scratch_shapes=[sems, sems],
        compiler_params=pltpu.CompilerParams(has_side_effects=True),
    )(*srcs, *extra)


SEM = pl.BlockSpec(memory_space=pltpu.SEMAPHORE)
DATAFLOW = pltpu.SideEffectType.DATAFLOW_SIDE_EFFECTING


def _exchange_copies(srcs, to_first, src_refs, land_refs, send_sems, recv_sems):
    x, y, c, me, chips = _mesh_place()
    n = len(srcs)
    sends, recvs = [], []
    for k in range(n):
        if k in to_first:
            base = 3 * n + 4 * to_first.index(k)
            sends.append((me != 0, pltpu.make_async_remote_copy(
                src_ref=src_refs[k], dst_ref=land_refs[k].at[me], send_sem=send_sems.at[base], recv_sem=recv_sems.at[base + me],
                device_id=(0, 0, c), device_id_type=MESH)))
            for s in range(1, N_CHIPS):
                recvs.append((me == 0, pltpu.make_async_remote_copy(
                    src_ref=src_refs[k], dst_ref=land_refs[k].at[s], send_sem=send_sems.at[base], recv_sem=recv_sems.at[base + s],
                    device_id=(s // 2, s % 2, c), device_id_type=MESH)))
        else:
            slab = (lambda t, k=k: src_refs[k]) if srcs[k].ndim == 2 else (lambda t, k=k: src_refs[k].at[t])
            for j, (px, py) in enumerate(chips):
                sends.append((None, _remote(slab(2 * px + py), land_refs[k].at[me], send_sems, recv_sems, 3 * k + j, (px, py, c))))
                recvs.append((None, _remote(slab(me), land_refs[k].at[2 * px + py], send_sems, recv_sems, 3 * k + j, (px, py, c))))
    return sends, recvs


def _when(pred, fn):
    if pred is None:
        fn()
    else:
        pl.when(pred)(fn)


def _exchange_start(srcs, to_first, name, after=None):
    n = len(srcs)
    n_sems = 3 * n + 4 * len(to_first)
    lands = [lax.empty((N_CHIPS,) + s.shape[-2:], s.dtype) for s in srcs]
    extra = [] if after is None else [after]

    def body(*refs):
        src_refs, land_refs = refs[:n], refs[n:2 * n]
        send_sems, recv_sems, token = refs[2 * n + len(extra)], refs[2 * n + len(extra) + 1], refs[-1]
        sends, _ = _exchange_copies(srcs, to_first, src_refs, land_refs, send_sems, recv_sems)
        for pred, cp in sends:
            _when(pred, cp.start)
        token[...] = jnp.zeros_like(token)

    hbm = lambda a: pltpu.HBM(a.shape, a.dtype)
    res = pl.pallas_call(
        body, name=name,
        out_shape=[pltpu.SemaphoreType.DMA((n_sems,)), pltpu.SemaphoreType.DMA((n_sems,))] + [hbm(a) for a in srcs + lands]
        + [jax.ShapeDtypeStruct((8, LANE), F32)],
        in_specs=[HBM] * (2 * n) + [pl.BlockSpec(memory_space=pl.ANY)] * len(extra),
        out_specs=[SEM, SEM] + [HBM] * (2 * n) + [pl.BlockSpec(memory_space=pltpu.VMEM)],
        input_output_aliases={i: 2 + i for i in range(2 * n)},
        compiler_params=pltpu.CompilerParams(has_side_effects=DATAFLOW),
    )(*[pltpu.with_memory_space_constraint(a, pltpu.HBM) for a in srcs + lands], *extra)
    return res[:-1], res[-1]


def _exchange_wait(srcs, to_first, started, after, name):
    n = len(srcs)
    send_sems, recv_sems, thru = started[0], started[1], started[2:]

    def body(*refs):
        src_refs, land_refs, send_ref, recv_ref = refs[:n], refs[n:2 * n], refs[2 * n], refs[2 * n + 1]
        sends, recvs = _exchange_copies(srcs, to_first, src_refs, land_refs, send_ref, recv_ref)
        for pred, cp in sends:
            _when(pred, cp.wait_send)
        for pred, cp in recvs:
            _when(pred, cp.wait_recv)

    res = pl.pallas_call(
        body, name=name, out_shape=[pltpu.HBM(a.shape, a.dtype) for a in thru],
        in_specs=[HBM] * (2 * n) + [SEM, SEM, pl.BlockSpec(memory_space=pl.ANY)], out_specs=[HBM] * (2 * n),
        input_output_aliases={i: i for i in range(2 * n)},
        compiler_params=pltpu.CompilerParams(has_side_effects=DATAFLOW),
    )(*thru, send_sems, recv_sems, after)
    return res[n:]


ROW_TILE = 256
COL_TILE = 256


def _block_tiling(a, b):
    if a <= ROW_TILE or a % ROW_TILE == 0:
        ta = min(a, ROW_TILE)
        return a // ta, (ta, b), lambda i: (i, 0)
    return b // COL_TILE, (a, COL_TILE), lambda i: (0, i)


def _sum_landed(land, own, name, first_land=None, first_own=None):
    _, a, b = land.shape
    steps, tile, at = _block_tiling(a, b)
    extra = first_land is not None

    def body(*refs):
        p_ref, own_ref, o_ref = refs[0], refs[1], refs[-1]
        me = 2 * lax.axis_index("x") + lax.axis_index("y")
        own = own_ref[...].astype(F32)
        slot = lambda t: jnp.where(me == t, own, p_ref[t].astype(F32))
        o_ref[...] = ((slot(0) + slot(1)) + slot(2)) + slot(3)
        if extra:
            fp_ref, fo_ref = refs[2], refs[3]
            r = fo_ref.shape[0]

            @pl.when(me == 0)
            def _():
                f = lambda t: fp_ref[t].astype(F32)
                rows = pl.ds(pl.multiple_of(lax.axis_index("c") * r, 8), r)
                o_ref[rows, :] += ((fo_ref[...].astype(F32) + f(1)) + f(2)) + f(3)

    in_specs = [pl.BlockSpec((N_CHIPS,) + tile, lambda i: (0,) + at(i)), pl.BlockSpec(tile, at)]
    args = [land, own]
    if extra:
        r = first_own.shape[0]
        assert tile[0] == a, "the extra rows need whole columns in a step"
        in_specs += [pl.BlockSpec((N_CHIPS, r, tile[1]), lambda i: (0,) + at(i)), pl.BlockSpec((r, tile[1]), at)]
        args += [first_land, first_own]
    return pl.pallas_call(
        body, name=name, grid=(steps,), in_specs=in_specs, out_specs=pl.BlockSpec(tile, at),
        out_shape=jax.ShapeDtypeStruct((a, b), F32), compiler_params=_params(("parallel",)),
    )(*args)


def _add_cast(a, b, name):
    def body(a_ref, b_ref, o_ref):
        o_ref[...] = (a_ref[...] + b_ref[...]).astype(BF16)

    return pl.pallas_call(body, name=name, out_shape=jax.ShapeDtypeStruct(a.shape, BF16),
                          compiler_params=_params(()))(a, b)


class _LaterWeights:
    MID = ("w_branch_a", "w_branch_b", "w_out")

    def __init__(self, blocks, after):
        self.qkv_blocks = [_pad_wuq(blocks["w_uq"]), *_pad_wukv(blocks["w_ukv"])]
        self.mid_blocks = [blocks[n] for n in self.MID]
        self.qkv_started, t1 = _exchange_start(self.qkv_blocks, (), "weights_qkv_start", after)
        self.mid_started, t2 = _exchange_start(self.mid_blocks, (), "weights_mid_start", after)
        self.tokens = [t1, t2]

    @staticmethod
    def _whole(blocks, rows_sharded, started, after, name):
        landed = _exchange_wait(blocks, (), started, after, name)
        me = 2 * lax.axis_index("x") + lax.axis_index("y")
        out = []
        for block, land, by_rows in zip(blocks, landed, rows_sharded):
            w = lax.dynamic_update_index_in_dim(land, block, me, 0)
            a, b = block.shape
            out.append(w.reshape(N_CHIPS * a, b) if by_rows else w.transpose(1, 0, 2).reshape(a, N_CHIPS * b))
        return out

    def qkv(self, after):
        return self._whole(self.qkv_blocks, (True, False, False), self.qkv_started, after, "weights_qkv_wait")

    def mid(self, after):
        return self._whole(self.mid_blocks, (False, False, True), self.mid_started, after, "weights_mid_wait")


class _GradExchange:
    EARLY = ("w_in", "w_branch_a", "w_branch_b", "w_out")
    LATE = ("w_uq", "w_ukv")

    def __init__(self, state):
        self.state = state
        self.outs = {}

    @staticmethod
    def _own(slabs):
        return lax.dynamic_index_in_dim(slabs, 2 * lax.axis_index("x") + lax.axis_index("y"), axis=0, keepdims=False)

    def start_early(self, g):
        full = jnp.concatenate([jnp.zeros((QKV_ROWS, D), F32), g["w_in_rest"]], axis=0)
        g = dict(g, w_in=full)
        self.early = [_split_by_chip(n, g[n]).astype(BF16) for n in self.EARLY]
        self.early_started, token = _exchange_start(self.early, (), "grads_early_start")
        return token

    def start_late(self, g):
        self.early_landed = _exchange_wait(self.early, (), self.early_started, g["w_uq"], "grads_early_wait")
        half = QKV_ROWS // 2
        c = lax.axis_index("c")
        mine = lax.dynamic_slice_in_dim(g["w_in_qkv"], c * half, half, axis=0)
        (theirs,) = _sibling_exchange([lax.dynamic_slice_in_dim(g["w_in_qkv"], (1 - c) * half, half, axis=0)], "sibling_qkv_rows")
        self.late = [_split_by_chip(n, g[n]).astype(BF16) for n in self.LATE] + [_add_cast(mine, theirs, "add_qkv_rows")]
        self.late_started, token = _exchange_start(self.late, (2,), "grads_late_start")
        names = self.EARLY[1:]
        mine = [_sum_landed(land, self._own(slabs), "sum_" + n) for n, slabs, land in list(zip(self.EARLY, self.early, self.early_landed))[1:]]
        theirs = _sibling_exchange(mine, "sibling_early", after=token)
        for n, a, b in zip(names, mine, theirs):
            self.outs[n] = _adamw(a, b, *self.state[n], "adamw_" + n)
        return self.outs[names[-1]][0]

    def finish(self, after):
        late_landed = _exchange_wait(self.late, (2,), self.late_started, after, "grads_late_wait")
        sums = {"w_in": _sum_landed(self.early_landed[0], self._own(self.early[0]), "sum_w_in",
                                    first_land=late_landed[2], first_own=self.late[2])}
        for n, slabs, land in zip(self.LATE, self.late, late_landed):
            sums[n] = _sum_landed(land, self._own(slabs), "sum_" + n)
        return sums


def _adamw_math(g, w, m, v):
    nm = ADAM_B1 * m + (1.0 - ADAM_B1) * g
    nv = ADAM_B2 * v + (1.0 - ADAM_B2) * (g * g)
    m_hat = nm / (1.0 - ADAM_B1 ** ADAM_STEP)
    v_hat = nv / (1.0 - ADAM_B2 ** ADAM_STEP)
    return -ADAM_LR * (m_hat / (jnp.sqrt(v_hat) + ADAM_EPS) + ADAM_WD * w), nm, nv


def _adamw(p_mine, p_sibling, w, m, v, name):
    a, b = p_mine.shape
    steps, tile, at = _block_tiling(a, b)

    def body(a_ref, b_ref, w_ref, m_ref, v_ref, g_ref, d_ref, nm_ref, nv_ref):
        g = a_ref[...] + b_ref[...]
        g_ref[...] = g
        d_ref[...], nm_ref[...], nv_ref[...] = _adamw_math(g, w_ref[...], m_ref[...], v_ref[...])

    spec = pl.BlockSpec(tile, at)
    sds = jax.ShapeDtypeStruct((a, b), F32)
    return pl.pallas_call(
        body, name=name, grid=(steps,), in_specs=[spec] * 5, out_specs=[spec] * 4, out_shape=[sds] * 4,
        compiler_params=_params(("parallel",)),
    )(p_mine, p_sibling, w, m, v)


LOSS_AT = (2, 1024)


def _vec_pack(vg, loss):
    names = [n for n, _, _ in VEC_ROWS]

    def body(*refs):
        o_ref = refs[-1]
        lb_ref, loss_ref = refs[len(names)], refs[len(names) + 1]
        o_ref[...] = jnp.zeros_like(o_ref)
        o_ref[LOSS_AT[0]:LOSS_AT[0] + 1, LOSS_AT[1]:LOSS_AT[1] + LANE] = jnp.broadcast_to(loss_ref[...], (1, LANE))
        for (name, row, size), ref in zip(VEC_ROWS, refs):
            if name == "g_hgrn":
                r = lax.broadcasted_iota(jnp.int32, (NH * V_DIM, LANE), 0)
                c = lax.broadcasted_iota(jnp.int32, (NH * V_DIM, LANE), 1)
                fold = ((r % V_DIM) == c).astype(F32)
                o_ref[row:row + 1, 0:LANE] = jnp.dot(ref[...], fold, precision=HIGHEST, preferred_element_type=F32)
            else:
                o_ref[row:row + 1, 0:size] = ref[...]
        o_ref[VEC_LB_ROW:VEC_LB_ROW + 2, 0:512] = lb_ref[...]

    return pl.pallas_call(body, name="vec_pack", out_shape=jax.ShapeDtypeStruct(VEC_SHAPE, F32))(
        *[vg[n] for n in names], vg["lb_logits"], loss)


def _adamw_vec(p_mine, p_sibling, w, m, v):
    names = [n for n, _, _ in VEC_ROWS] + ["lb_logits"]
    k = len(names)

    def body(a_ref, b_ref, *refs):
        ins, outs = refs[:3 * k], refs[3 * k:]
        at = (slice(LOSS_AT[0], LOSS_AT[0] + 1), slice(LOSS_AT[1], LOSS_AT[1] + LANE))
        outs[-1][...] = a_ref[at] + b_ref[at]
        for i, name in enumerate(names):
            if name == "lb_logits":
                rows, cols = slice(VEC_LB_ROW, VEC_LB_ROW + 2), slice(0, 512)
            else:
                _, row, size = VEC_ROWS[i]
                rows, cols = slice(row, row + 1), slice(0, size)
            g = a_ref[rows, cols] + b_ref[rows, cols]
            d, nm, nv = _adamw_math(g, ins[i][...], ins[k + i][...], ins[2 * k + i][...])
            for o_ref, val in zip(outs[4 * i:4 * i + 4], (g, d, nm, nv)):
                o_ref[...] = val

    shapes = [jax.ShapeDtypeStruct(w[n].shape, F32) for n in names for _ in range(4)] + [jax.ShapeDtypeStruct((1, LANE), F32)]
    res = pl.pallas_call(body, name="adamw_vec", out_shape=shapes)(
        p_mine, p_sibling, *[w[n] for n in names], *[m[n] for n in names], *[v[n] for n in names])
    return [{n: res[4 * i + j] for i, n in enumerate(names)} for j in range(4)], res[-1]


WEIGHTS = ("g_pre", "w_in", "b_gate", "g_q", "w_uq", "g_kv", "w_ukv", "lb_logits", "g_hgrn", "w_branch_a", "w_branch_b", "w_out", "g_post")


def kernel(x, g_pre, w_in, b_gate, g_q, w_uq, g_kv, w_ukv, lb_logits, g_hgrn, w_branch_a, w_branch_b, w_out, g_post, loss_target, m_g_pre, m_w_in, m_b_gate, m_g_q, m_w_uq, m_g_kv, m_w_ukv, m_lb_logits, m_g_hgrn, m_w_branch_a, m_w_branch_b, m_w_out, m_g_post, v_g_pre, v_w_in, v_b_gate, v_g_q, v_w_uq, v_g_kv, v_w_ukv, v_lb_logits, v_g_hgrn, v_w_branch_a, v_w_branch_b, v_w_out, v_g_post):
    w = dict(g_pre=g_pre, w_in=w_in, b_gate=b_gate, g_q=g_q, w_uq=w_uq, g_kv=g_kv, w_ukv=w_ukv, lb_logits=lb_logits, g_hgrn=g_hgrn,
             w_branch_a=w_branch_a, w_branch_b=w_branch_b, w_out=w_out, g_post=g_post)
    m = dict(g_pre=m_g_pre, w_in=m_w_in, b_gate=m_b_gate, g_q=m_g_q, w_uq=m_w_uq, g_kv=m_g_kv, w_ukv=m_w_ukv, lb_logits=m_lb_logits,
             g_hgrn=m_g_hgrn, w_branch_a=m_w_branch_a, w_branch_b=m_w_branch_b, w_out=m_w_out, g_post=m_g_post)
    v = dict(g_pre=v_g_pre, w_in=v_w_in, b_gate=v_b_gate, g_q=v_g_q, w_uq=v_w_uq, g_kv=v_g_kv, w_ukv=v_w_ukv, lb_logits=v_lb_logits,
             g_hgrn=v_g_hgrn, w_branch_a=v_w_branch_a, w_branch_b=v_w_branch_b, w_out=v_w_out, g_post=v_g_post)
    blocks = {n: _to_block(n, w[n]).astype(BF16) for n in BIG}
    w_in_all = _gather_w_in(blocks["w_in"])
    weights = _LaterWeights(blocks, w_in_all)
    state = {n: [_to_block(n, t[n]) for t in (w, m, v)] for n in BIG}
    exchange = _GradExchange(state)
    loss, grad_x, _, vec_grads = _local_step(
        x[0], loss_target[0], g_pre, _join_chips("w_in", w_in_all), b_gate, g_q, g_kv, lb_logits, g_hgrn, g_post, weights, exchange)
    vec = _vec_pack(vec_grads, loss)
    vec_started, token = _exchange_start([vec], (), "vec_start")
    sums = exchange.finish(token)
    rest = tuple(sums)
    (vec_landed,) = _exchange_wait([vec], (), vec_started, sums[rest[-1]], "vec_wait")
    mine = [sums[n] for n in rest] + [_sum_landed(vec_landed, vec, "sum_vec")]
    theirs = _sibling_exchange(mine, "sibling_grads")
    done = dict(exchange.outs)
    for k, n in enumerate(rest):
        done[n] = _adamw(mine[k], theirs[k], *state[n], "adamw_" + n)
    outs = [{}, {}, {}, {}]
    for n in BIG:
        for o, val in zip(outs, done[n]):
            o[n] = _from_block(n, val)
    vec_outs, total = _adamw_vec(mine[-1], theirs[-1], w, m, v)
    for o, vals in zip(outs, vec_outs):
        o.update(vals)
    return (total[0, 0], grad_x[None], *[o[n] for o in outs for n in WEIGHTS])
```

```python
import math

import jax
import jax.numpy as jnp
from jax import lax
from jax.experimental import pallas as pl
from jax.experimental.pallas import tpu as pltpu

F32 = jnp.float32
BF16 = jnp.bfloat16
HIGHEST = lax.Precision.HIGHEST

D = 1024
NH = 8
QK_NOPE, QK_ROPE, V_DIM = 64, 32, 64
Q_LORA, KV_LORA = 768, 256
CHUNK = 64
HG_BLOCK = 32
EPS = 1e-6
LANE = 128
P_MERGE, P_GA, P_HQ, P_HF, P_HI, P_GB, P_CQ, P_CKV, P_KPE = 0, 2048, 2560, 3072, 3584, 4096, 4608, 5376, 5632
D_P = 5760
O_CQ, O_CKV, O_KPE, O_GA, O_HQ, O_HF, O_HI, O_GB, O_MERGE = 0, 768, 1024, 1056, 1568, 2080, 2592, 3104, 3616

TM = 512
TM_MID = 256
TQ = 1024
ONES_LANE = (LANE - 1, 0)
TH = 256
HG_PAIRS = 4
VMEM_LIMIT = 56 * 1024 * 1024

ADAM_LR, ADAM_B1, ADAM_B2, ADAM_EPS, ADAM_WD, ADAM_STEP = 0.001, 0.9, 0.999, 1e-08, 0.01, 10

NT_DIMS = (((1,), (1,)), ((), ()))
TN_DIMS = (((0,), (0,)), ((), ()))


def _params(sem):
    return pltpu.CompilerParams(dimension_semantics=sem, vmem_limit_bytes=VMEM_LIMIT)


def _mm(a, b):
    return jnp.dot(a, b, preferred_element_type=F32)


def _mm_nt(a, b):
    return lax.dot_general(a, b, NT_DIMS, preferred_element_type=F32)


def _mm_tn(a, b):
    return lax.dot_general(a, b, TN_DIMS, preferred_element_type=F32)


def _sigmoid(z):
    return jax.nn.sigmoid(z)


def _rope(v, c, s1, s2):
    return v * c + pltpu.roll(v, 112, 1) * s1 + pltpu.roll(v, 16, 1) * s2


def _rope_t(dy, c, s1, s2):
    return dy * c + pltpu.roll(dy * s1, 16, 1) + pltpu.roll(dy * s2, 112, 1)


def _rope_tables(s):
    inv = 10000.0 ** (-jnp.arange(0, QK_ROPE, 2, dtype=F32) / QK_ROPE)
    ang = jnp.arange(s, dtype=F32)[:, None] * inv[None, :]
    cos, sin = jnp.cos(ang), jnp.sin(ang)
    z64, z32, o64, o32 = jnp.zeros((s, 64), F32), jnp.zeros((s, 32), F32), jnp.ones((s, 64), F32), jnp.ones((s, 32), F32)
    z16 = jnp.zeros((s, 16), F32)
    c = jnp.concatenate([o64, cos, cos, o32], axis=1)
    s1 = jnp.concatenate([z64, -sin, z16, z32], axis=1)
    s2 = jnp.concatenate([z64, z16, sin, z32], axis=1)
    return c, s1, s2


def _front_fwd(x, g_pre, w_in_pt, tokens=()):
    s = x.shape[0]
    tokens = list(tokens)

    def body(x_ref, g_ref, w_ref, *refs):
        o_ref, h_ref = refs[len(tokens):]
        xv = x_ref[...]
        r = lax.rsqrt(jnp.mean(xv * xv, axis=-1, keepdims=True) + EPS)
        h = ((xv * r) * g_ref[...]).astype(BF16)
        h_ref[...] = h
        o_ref[...] = _mm_nt(h, w_ref[...])

    return pl.pallas_call(
        body, name="front_fwd", grid=(s // TM,),
        in_specs=[pl.BlockSpec((TM, D), lambda i: (i, 0)), pl.BlockSpec((1, D), lambda i: (0, 0)),
                  pl.BlockSpec((D_P, D), lambda i: (0, 0))] + [pl.BlockSpec((8, LANE), lambda i: (0, 0))] * len(tokens),
        out_specs=[pl.BlockSpec((TM, D_P), lambda i: (i, 0)), pl.BlockSpec((TM, D), lambda i: (i, 0))],
        out_shape=[jax.ShapeDtypeStruct((s, D_P), F32), jax.ShapeDtypeStruct((s, D), BF16)],
        compiler_params=_params(("parallel",)),
    )(x, g_pre, w_in_pt, *tokens)


def _norm_rows(v, g):
    r = lax.rsqrt(jnp.mean(v * v, axis=-1, keepdims=True) + EPS)
    return (v * r) * g, r


def _qkv_fwd(proj, g_q, g_kv, w_uq_p, w_k_p, w_v_p, rc, rs1, rs2):
    s = proj.shape[0]

    def body(cq_ref, ckv_ref, kpe_ref, gq_ref, gkv_ref, wq_ref, wk_ref, wv_ref, c_ref, s1_ref, s2_ref, q_ref, k_ref, v_ref):
        c, s1, s2 = c_ref[...], s1_ref[...], s2_ref[...]
        cqn, _ = _norm_rows(cq_ref[...], gq_ref[...])
        ckvn, _ = _norm_rows(ckv_ref[...], gkv_ref[...])
        ckvn = ckvn.astype(BF16)
        qf = _mm(cqn.astype(BF16), wq_ref[...])
        kf = _mm(ckvn, wk_ref[...])
        vf = _mm(ckvn, wv_ref[...])
        kpe = _rope(kpe_ref[...], c, s1, s2)
        lane = lax.broadcasted_iota(jnp.int32, (TM, LANE), 1)
        for h in range(NH):
            blk = slice(h * LANE, (h + 1) * LANE)
            q_ref[h] = _rope(qf[:, blk], c, s1, s2).astype(BF16)
            k_ref[h] = (kf[:, blk] + kpe).astype(BF16)
            v_ref[h] = jnp.where(lane == ONES_LANE[h % 2], 1.0, vf[:, blk]).astype(BF16)

    row = lambda w, j: pl.BlockSpec((TM, w), lambda i: (i, j))
    full = lambda a: pl.BlockSpec(a.shape, lambda i: (0,) * a.ndim)
    hs = jax.ShapeDtypeStruct((NH, s, LANE), BF16)
    return pl.pallas_call(
        body, name="qkv_fwd", grid=(s // TM,),
        in_specs=[row(Q_LORA, P_CQ // Q_LORA), row(KV_LORA, P_CKV // KV_LORA), row(LANE, P_KPE // LANE),
                  full(g_q), full(g_kv), full(w_uq_p), full(w_k_p), full(w_v_p), row(LANE, 0), row(LANE, 0), row(LANE, 0)],
        out_specs=[pl.BlockSpec((NH, TM, LANE), lambda i: (0, i, 0))] * 3,
        out_shape=[hs, hs, hs],
        compiler_params=_params(("parallel",)),
    )(proj, proj, proj, g_q, g_kv, w_uq_p, w_k_p, w_v_p, rc, rs1, rs2)


LOG2E = 1.4426950408889634
QK_SCALE2 = LOG2E / math.sqrt(QK_NOPE + QK_ROPE)


HQ = TQ // 2


def _diag_visible(n):
    row = lax.broadcasted_iota(jnp.int32, (n, n), 0)
    col = lax.broadcasted_iota(jnp.int32, (n, n), 1)
    return (col // CHUNK) <= (row // CHUNK)


def _attn_fwd(q, k, vv):
    s = q.shape[1]

    def body(q_ref, k_ref, v_ref, o_ref, lse_ref):
        i = pl.program_id(1)
        qs = (q_ref[0], q_ref[1])

        def tiles(t, carry, diag):
            rows = pl.ds(pl.multiple_of(t * TQ, TQ), TQ)
            sc = [_mm_nt(qs[hh], k_ref[hh, rows, :]) for hh in range(2)]
            if diag:
                sc = [jnp.where(_diag_visible(TQ), s_, -jnp.inf) for s_ in sc]
            m_new = [jnp.maximum(carry[hh][0], jnp.max(sc[hh], axis=-1, keepdims=True)) for hh in range(2)]
            alpha = [jnp.exp2((carry[hh][0] - m_new[hh]) * QK_SCALE2) for hh in range(2)]
            p = [jnp.exp2((sc[hh] - m_new[hh]) * QK_SCALE2).astype(BF16) for hh in range(2)]
            acc = [alpha[hh] * carry[hh][1] + _mm(p[hh], v_ref[hh, rows, :]) for hh in range(2)]
            return (m_new[0], acc[0]), (m_new[1], acc[1])

        init = (jnp.full((TQ, 1), -jnp.inf, F32), jnp.zeros((TQ, LANE), F32))
        carry = lax.fori_loop(0, i, lambda t, c: tiles(t, c, False), (init, init))
        carry = tiles(i, carry, True)
        lane = lax.broadcasted_iota(jnp.int32, (TQ, LANE), 1)
        out = jnp.zeros((TQ, LANE), F32)
        for hh in range(2):
            m, acc = carry[hh]
            l = jnp.sum(jnp.where(lane == ONES_LANE[hh], acc, 0.0), axis=-1, keepdims=True)
            out = out + jnp.where((lane < V_DIM) == (hh == 0), acc, 0.0) / l
            lse_ref[hh] = jnp.broadcast_to(m * QK_SCALE2 + jnp.log(l) * LOG2E, (TQ, LANE))
        o_ref[...] = out

    return pl.pallas_call(
        body, name="attn_fwd", grid=(NH // 2, s // TQ),
        in_specs=[pl.BlockSpec((2, TQ, LANE), lambda p, i: (p, i, 0)), pl.BlockSpec((2, s, LANE), lambda p, i: (p, 0, 0)),
                  pl.BlockSpec((2, s, LANE), lambda p, i: (p, 0, 0))],
        out_specs=[pl.BlockSpec((TQ, LANE), lambda p, i: (i, p)), pl.BlockSpec((2, TQ, LANE), lambda p, i: (p, i, 0))],
        out_shape=[jax.ShapeDtypeStruct((s, NH * V_DIM), F32), jax.ShapeDtypeStruct((NH, s, LANE), F32)],
        compiler_params=_params(("parallel", "parallel")),
    )(q, k, vv)


def _lower_bound(lbl):
    a0, a1 = lbl[0:1, :], lbl[1:2, :]
    mx = jnp.maximum(a0, a1)
    e0, e1 = jnp.exp(a0 - mx), jnp.exp(a1 - mx)
    return e0 / (e0 + e1)


def _chunk_cumsum(v, reverse=False):
    pos = lax.broadcasted_iota(jnp.int32, v.shape, 0) % HG_BLOCK
    s = 1
    while s < HG_BLOCK:
        if reverse:
            v = v + jnp.where(pos < HG_BLOCK - s, pltpu.roll(v, TH - s, 0), 0.0)
        else:
            v = v + jnp.where(pos >= s, pltpu.roll(v, s, 0), 0.0)
        s *= 2
    return v


def _hgrn_gates(hq, hf, lb):
    sig = _sigmoid(hf)
    f = lb + (1.0 - lb) * sig
    g = jnp.log(f)
    kk = 1.0 - f
    r = lax.broadcasted_iota(jnp.int32, (TH, TH), 0)
    c = lax.broadcasted_iota(jnp.int32, (TH, TH), 1)
    tri = ((r // HG_BLOCK) == (c // HG_BLOCK)) & (r >= c)
    cum = _chunk_cumsum(g)
    nch = TH // HG_BLOCK
    total = _chunks(cum)[:, HG_BLOCK - 1:HG_BLOCK, :]
    lastb = jnp.broadcast_to(total, (nch, HG_BLOCK, hf.shape[-1])).reshape(hf.shape)
    e, ei, ee = jnp.exp(cum), jnp.exp(-cum), jnp.exp(lastb - cum)
    return dict(sig=sig, f=f, kk=kk, tri=tri, cum=cum, total=total, decay=jnp.exp(total), e=e, ei=ei, ee=ee,
                qd=hq * e, ki=kk * ei, ke=kk * ee)


def _chunks(v):
    return v.reshape(TH // HG_BLOCK, HG_BLOCK, v.shape[-1])


def _bmm_nt(a, b):
    return lax.dot_general(a, b, (((2,), (2,)), ((0,), (0,))), preferred_element_type=F32)


def _bmm_nn(a, b):
    return lax.dot_general(a, b, (((2,), (1,)), ((0,), (0,))), preferred_element_type=F32)


def _bmm_tn(a, b):
    return lax.dot_general(a, b, (((1,), (1,)), ((0,), (0,))), preferred_element_type=F32)


def _pair_masks():
    lane = lax.broadcasted_iota(jnp.int32, (TH, LANE), 1)
    kr = lax.broadcasted_iota(jnp.int32, (LANE, LANE), 0)
    kc = lax.broadcasted_iota(jnp.int32, (LANE, LANE), 1)
    return lane < 64, (kr // 64) == (kc // 64)


def _hgrn_fwd(proj, lbl):
    s = proj.shape[0]
    nch = TH // HG_BLOCK

    def body(hq_ref, hf_ref, hi_ref, lbl_ref, o_ref, st_ref, st):
        @pl.when(pl.program_id(1) == 0)
        def _():
            st[...] = jnp.zeros_like(st)

        m0, bd = _pair_masks()
        gt = _hgrn_gates(hq_ref[...], hf_ref[...], _lower_bound(lbl_ref[...]))
        v_b, qd, qd_b = hi_ref[...].astype(BF16), gt["qd"], gt["qd"].astype(BF16)
        ki_b, ke_b = gt["ki"].astype(BF16), gt["ke"].astype(BF16)
        pairs = [slice(u * LANE, (u + 1) * LANE) for u in range(HG_PAIRS)]
        heads = [(lanes, m0 if hh == 0 else jnp.logical_not(m0)) for lanes in pairs for hh in range(2)]
        a_b = [jnp.where(gt["tri"], _mm_nt(jnp.where(mh, qd[:, lanes], 0.0).astype(BF16), ki_b[:, lanes]), 0.0).astype(BF16)
               for lanes, mh in heads]
        intra = [jnp.where(m0, _mm(a_b[2 * u], v_b[:, lanes]), _mm(a_b[2 * u + 1], v_b[:, lanes])) for u, lanes in enumerate(pairs)]
        upd = [_bmm_tn(_chunks(v_b[:, lanes]), _chunks(ke_b[:, lanes])) for lanes in pairs]
        entering = []
        for u, lanes in enumerate(pairs):
            cur, states = st[u], []
            for n in range(nch):
                states.append(cur)
                cur = gt["decay"][n][:, lanes] * cur + jnp.where(bd, upd[u][n], 0.0)
            st[u] = cur
            entering.append(jnp.stack(states))
            st_ref[u] = entering[u]
        for u, lanes in enumerate(pairs):
            o_ref[:, lanes] = intra[u] + _bmm_nt(_chunks(qd_b[:, lanes]), entering[u].astype(BF16)).reshape(TH, LANE)

    wide = HG_PAIRS * LANE
    col = lambda base: pl.BlockSpec((TH, wide), lambda p, i: (i, base // wide + p))
    return pl.pallas_call(
        body, name="hgrn_fwd", grid=(NH // 2 // HG_PAIRS, s // TH),
        in_specs=[col(P_HQ), col(P_HF), col(P_HI), pl.BlockSpec((2, wide), lambda p, i: (0, p))],
        out_specs=[pl.BlockSpec((TH, wide), lambda p, i: (i, p)),
                   pl.BlockSpec((HG_PAIRS, nch, LANE, LANE), lambda p, i: (p, i, 0, 0))],
        out_shape=[jax.ShapeDtypeStruct((s, 512), F32), jax.ShapeDtypeStruct((NH // 2, s // HG_BLOCK, LANE, LANE), F32)],
        scratch_shapes=[pltpu.VMEM((HG_PAIRS, LANE, LANE), F32)],
        compiler_params=_params(("parallel", "arbitrary")),
    )(proj, proj, proj, lbl)


def _group_sum(v):
    low = lax.broadcasted_iota(jnp.int32, (v.shape[0], LANE), 1) < V_DIM
    blocks = []
    for b in range(v.shape[1] // LANE):
        blk = v[:, b * LANE:(b + 1) * LANE]
        s_low = jnp.sum(jnp.where(low, blk, 0.0), axis=-1, keepdims=True)
        s_high = jnp.sum(jnp.where(low, 0.0, blk), axis=-1, keepdims=True)
        blocks.append(jnp.where(low, s_low, s_high))
    return jnp.concatenate(blocks, axis=1)


def _dsilu(z, sg):
    return sg * (1.0 + z * (1.0 - sg))


def _mid(proj, attn, o_raw, x, tgt, g_hg, b_gate, g_post, wa, wb, w_out):
    s = x.shape[0]

    def body(attn_ref, ga_ref, o_ref, gb_ref, mg_ref, x_ref, t_ref, ghg_ref, bg_ref, gp_ref, wa_ref, wb_ref, wo_ref,
             loss_ref, dout_ref, dattn_ref, dga_ref, dor_ref, dgb_ref, dmg_ref, dwo_ref, dwa_ref, dwb_ref, dgp_ref, dbg_ref, dghg_ref):
        @pl.when(pl.program_id(0) == 0)
        def _():
            for rf in (loss_ref, dwo_ref, dwa_ref, dwb_ref, dgp_ref, dbg_ref, dghg_ref):
                rf[...] = jnp.zeros_like(rf)

        attn, za, orw, zb = attn_ref[...], ga_ref[...], o_ref[...], gb_ref[...]
        ghg, gp = ghg_ref[...], gp_ref[...]
        sga, sgb = _sigmoid(za), _sigmoid(zb)
        sa, sb = za * sga, zb * sgb
        ga = attn * sa
        rh = lax.rsqrt(_group_sum(orw * orw) * (1.0 / V_DIM) + EPS)
        on = (orw * rh) * ghg
        gb = on * sb
        ga_b, gb_b = ga.astype(BF16), gb.astype(BF16)
        ya = _mm(ga_b, wa_ref[...])
        yb = _mm(gb_b, wb_ref[...])
        gates = _sigmoid(mg_ref[...] + bg_ref[...])
        g0, g1 = gates[:, :D], gates[:, D:]
        m_b = (g0 * ya + g1 * yb).astype(BF16)
        y = _mm(m_b, wo_ref[...])
        ry = lax.rsqrt(jnp.mean(y * y, axis=-1, keepdims=True) + EPS)
        out = x_ref[...] + (y * ry) * gp
        err = out - t_ref[...]
        loss_ref[...] += 0.5 * jnp.sum(jnp.mean(err * err, axis=-1, keepdims=True), axis=0, keepdims=True)
        dout = err * (1.0 / D)
        dout_ref[...] = dout
        dgp_ref[...] += jnp.sum(dout * (y * ry), axis=0, keepdims=True)
        dgy = dout * gp
        dy = ry * dgy - y * (ry * ry * ry) * jnp.mean(y * dgy, axis=-1, keepdims=True)
        dy_b = dy.astype(BF16)
        dm = _mm_nt(dy_b, wo_ref[...])
        dya_b, dyb_b = (dm * g0).astype(BF16), (dm * g1).astype(BF16)
        dga = _mm_nt(dya_b, wa_ref[...])
        dgb = _mm_nt(dyb_b, wb_ref[...])
        dwo_ref[...] += _mm_tn(m_b, dy_b)
        dwa_ref[...] += _mm_tn(ga_b, dya_b)
        dwb_ref[...] += _mm_tn(gb_b, dyb_b)
        dg0, dg1 = dm * ya, dm * yb
        dmg = jnp.concatenate([dg0 * g0 * (1.0 - g0), dg1 * g1 * (1.0 - g1)], axis=1)
        dmg_ref[...] = dmg.astype(BF16)
        dbg_ref[...] += jnp.sum(dmg, axis=0, keepdims=True)
        dattn_ref[...] = dga * sa
        dga_ref[...] = (dga * attn * _dsilu(za, sga)).astype(BF16)
        dgb_ref[...] = (dgb * on * _dsilu(zb, sgb)).astype(BF16)
        don = dgb * sb
        dghg_ref[...] += jnp.sum(don * (orw * rh), axis=0, keepdims=True)
        dgo = don * ghg
        dor_ref[...] = rh * dgo - orw * (rh * rh * rh) * (_group_sum(orw * dgo) * (1.0 / V_DIM))

    row = lambda w, j=0: pl.BlockSpec((TM_MID, w), lambda i: (i, j))
    full = lambda a: pl.BlockSpec(a.shape, lambda i: (0,) * a.ndim)
    acc = lambda shape: pl.BlockSpec(shape, lambda i: (0, 0))
    sds = jax.ShapeDtypeStruct
    return pl.pallas_call(
        body, name="mid", grid=(s // TM_MID,),
        in_specs=[row(512), row(512, P_GA // 512), row(512), row(512, P_GB // 512), row(2048, P_MERGE // 2048), row(D), row(D),
                  full(g_hg), full(b_gate), full(g_post), full(wa), full(wb), full(w_out)],
        out_specs=[acc((1, 1)), row(D), row(512), row(512), row(512), row(512), row(2048),
                   acc((D, D)), acc((512, D)), acc((512, D)), acc((1, D)), acc((1, 2048)), acc((1, 512))],
        out_shape=[sds((1, 1), F32), sds((s, D), F32), sds((s, 512), F32), sds((s, 512), BF16), sds((s, 512), F32), sds((s, 512), BF16),
                   sds((s, 2048), BF16), sds((D, D), F32), sds((512, D), F32), sds((512, D), F32), sds((1, D), F32),
                   sds((1, 2048), F32), sds((1, 512), F32)],
        compiler_params=_params(("arbitrary",)),
    )(attn, proj, o_raw, proj, proj, x, tgt, g_hg, b_gate, g_post, wa, wb, w_out)


def _attn_bwd(q, k, vv, attn, dattn, lse, token):
    s = q.shape[1]
    nt = s // TQ
    scale = 1.0 / math.sqrt(QK_NOPE + QK_ROPE)

    def body(q_ref, k_ref, v_ref, o_ref, do_ref, lse_ref, token_ref, dq_ref, dk_ref, dv_ref, do_s, delta_s):
        j = pl.program_id(1)

        @pl.when(j == 0)
        def _():
            dq_ref[...] = jnp.zeros_like(dq_ref)
            lane = lax.broadcasted_iota(jnp.int32, (TQ, LANE), 1)

            @pl.loop(0, nt)
            def _(i):
                rows = pl.ds(pl.multiple_of(i * TQ, TQ), TQ)
                do, o = do_ref[rows, :], o_ref[rows, :]
                for hh in range(2):
                    doh = jnp.where((lane < 64) if hh == 0 else (lane >= 64), do, 0.0)
                    do_s[hh, rows, :] = doh.astype(BF16)
                    delta_s[hh, rows, :] = jnp.broadcast_to(jnp.sum(doh * o, axis=-1, keepdims=True), (TQ, LANE))

        kjs, vjs = (k_ref[0], k_ref[1]), (v_ref[0], v_ref[1])

        def tile(hh, start, size, kj, vj, diag):
            rows = pl.ds(pl.multiple_of(start, size), size)
            wide = lambda a: jnp.concatenate([a] * (kj.shape[0] // LANE), axis=1)
            qi, do_b = q_ref[hh, rows, :], do_s[hh, rows, :]
            sc, dp = _mm_nt(qi, kj), _mm_nt(do_b, vj)
            p = jnp.exp2(sc * QK_SCALE2 - wide(lse_ref[hh, rows, :]))
            if diag:
                p = jnp.where(_diag_visible(size), p, 0.0)
            ds_b = (p * (dp - wide(delta_s[hh, rows, :]))).astype(BF16)
            dv, dk = _mm_tn(do_b, p.astype(BF16)), _mm_tn(qi, ds_b)
            dq_ref[hh, rows, :] += _mm(ds_b, kj)
            return dk, dv

        def step(i, carry):
            new = [tile(hh, i * TQ, TQ, kjs[hh], vjs[hh], False) for hh in range(2)]
            return tuple((carry[hh][0] + new[hh][0], carry[hh][1] + new[hh][1]) for hh in range(2))

        def diagonal(hh):
            k0, k1, v0, v1 = kjs[hh][:HQ], kjs[hh][HQ:], vjs[hh][:HQ], vjs[hh][HQ:]
            a = tile(hh, j * TQ, HQ, k0, v0, True)
            b = tile(hh, j * TQ + HQ, HQ, k0, v0, False)
            c = tile(hh, j * TQ + HQ, HQ, k1, v1, True)
            return jnp.concatenate([a[0] + b[0], c[0]], axis=1), jnp.concatenate([a[1] + b[1], c[1]], axis=1)

        carry = lax.fori_loop(j + 1, nt, step, (diagonal(0), diagonal(1)))
        for hh in range(2):
            dk_ref[hh] = carry[hh][0].T * scale
            dv_ref[hh] = carry[hh][1].T

        @pl.when(j == nt - 1)
        def _():
            dq_ref[...] = dq_ref[...] * scale

    whole = pl.BlockSpec((2, s, LANE), lambda p, j: (p, 0, 0))
    tile_spec = pl.BlockSpec((2, TQ, LANE), lambda p, j: (p, j, 0))
    cols = pl.BlockSpec((s, LANE), lambda p, j: (0, p))
    hs = jax.ShapeDtypeStruct((NH, s, LANE), F32)
    return pl.pallas_call(
        body, name="attn_bwd", grid=(NH // 2, nt),
        in_specs=[whole, tile_spec, tile_spec, cols, cols, whole, pl.BlockSpec((8, LANE), lambda p, j: (0, 0))],
        out_specs=[whole, tile_spec, tile_spec],
        out_shape=[hs, hs, hs],
        scratch_shapes=[pltpu.VMEM((2, s, LANE), BF16), pltpu.VMEM((2, s, LANE), F32)],
        compiler_params=_params(("parallel", "arbitrary")),
    )(q, k, vv, attn, dattn, lse, token)


def _hgrn_bwd(proj, lbl, states, do_raw):
    s = proj.shape[0]
    nt = s // TH
    nch = TH // HG_BLOCK

    def body(hq_ref, hf_ref, hi_ref, lbl_ref, st_ref, do_ref, dh_ref, dlbl_ref, dst, dlb):
        step = pl.program_id(1)

        @pl.when(step == 0)
        def _():
            dst[...] = jnp.zeros_like(dst)
            dlb[...] = jnp.zeros_like(dlb)

        m0, bd = _pair_masks()
        lb = _lower_bound(lbl_ref[...])
        gt = _hgrn_gates(hq_ref[...], hf_ref[...], lb)
        do = do_ref[...]
        qd, ki, ke = gt["qd"], gt["ki"], gt["ke"]
        v_b, do_b = hi_ref[...].astype(BF16), do.astype(BF16)
        qd_b, ki_b, ke_b = qd.astype(BF16), ki.astype(BF16), ke.astype(BF16)
        pairs = [slice(u * LANE, (u + 1) * LANE) for u in range(HG_PAIRS)]
        heads = [(lanes, m0 if hh == 0 else jnp.logical_not(m0)) for lanes in pairs for hh in range(2)]
        a_b = [jnp.where(gt["tri"], _mm_nt(jnp.where(mh, qd[:, lanes], 0.0).astype(BF16), ki_b[:, lanes]), 0.0).astype(BF16)
               for lanes, mh in heads]
        doh_b = [jnp.where(mh, do[:, lanes], 0.0).astype(BF16) for lanes, mh in heads]
        da_b = [jnp.where(gt["tri"], _mm_nt(d, v_b[:, lanes]), 0.0).astype(BF16) for d, (lanes, _) in zip(doh_b, heads)]
        dv_p, dqd_p, dki_p = [], [], []
        for u, lanes in enumerate(pairs):
            e, o = 2 * u, 2 * u + 1
            dv_p.append(_mm_tn(a_b[e], doh_b[e]) + _mm_tn(a_b[o], doh_b[o]))
            dqd_p.append(jnp.where(m0, _mm(da_b[e], ki_b[:, lanes]), _mm(da_b[o], ki_b[:, lanes])))
            dki_p.append(jnp.where(m0, _mm_tn(da_b[e], qd_b[:, lanes]), _mm_tn(da_b[o], qd_b[:, lanes])))
        fed = [_bmm_tn(_chunks(do_b[:, lanes]), _chunks(qd_b[:, lanes])) for lanes in pairs]
        leaving = []
        for u, lanes in enumerate(pairs):
            ds, left = dst[u], [None] * nch
            for n in reversed(range(nch)):
                left[n] = ds
                ds = gt["decay"][n][:, lanes] * ds + jnp.where(bd, fed[u][n], 0.0)
            dst[u] = ds
            leaving.append(jnp.stack(left))
        dke_p, dlast_p = [], []
        for u, lanes in enumerate(pairs):
            entering, leaving_b = st_ref[u], leaving[u].astype(BF16)
            dke3 = _bmm_nn(_chunks(v_b[:, lanes]), leaving_b)
            dv_p[u] = dv_p[u] + _bmm_nt(_chunks(ke_b[:, lanes]), leaving_b).reshape(TH, LANE)
            dqd_p[u] = dqd_p[u] + _bmm_nn(_chunks(do_b[:, lanes]), entering.astype(BF16)).reshape(TH, LANE)
            dke_p.append(dke3.reshape(TH, LANE))
            dlast_p.append(jnp.sum(dke3 * _chunks(ke[:, lanes]), axis=1, keepdims=True)
                           + jnp.sum(leaving[u] * entering, axis=1, keepdims=True) * gt["decay"][:, :, lanes])
        cat = lambda parts: jnp.concatenate(parts, axis=-1)
        dv, dqd, dki, dke, dlast = cat(dv_p), cat(dqd_p), cat(dki_p), cat(dke_p), cat(dlast_p)
        dk = dki * gt["ei"] + dke * gt["ee"]
        dcum = dqd * qd - dki * ki - dke * ke
        dg = _chunk_cumsum(dcum, reverse=True) + jnp.broadcast_to(dlast, (nch, HG_BLOCK, dlast.shape[-1])).reshape(dcum.shape)
        sig = gt["sig"]
        df = dg / gt["f"] - dk
        dlb[...] += jnp.sum(df * (1.0 - sig), axis=0, keepdims=True)
        dh_ref[0] = (dqd * gt["e"]).astype(BF16)
        dh_ref[1] = ((df * (1.0 - lb)) * sig * (1.0 - sig)).astype(BF16)
        dh_ref[2] = dv.astype(BF16)

        @pl.when(step == nt - 1)
        def _():
            lb = _lower_bound(lbl_ref[...])
            da0 = dlb[...] * lb * (1.0 - lb)
            dlbl_ref[...] = jnp.concatenate([da0, -da0], axis=0)

    wide = HG_PAIRS * LANE
    col = lambda base: pl.BlockSpec((TH, wide), lambda p, i: (nt - 1 - i, base // wide + p))
    tile = pl.BlockSpec((TH, wide), lambda p, i: (nt - 1 - i, p))
    sds = jax.ShapeDtypeStruct
    return pl.pallas_call(
        body, name="hgrn_bwd", grid=(NH // 2 // HG_PAIRS, nt),
        in_specs=[col(P_HQ), col(P_HF), col(P_HI), pl.BlockSpec((2, wide), lambda p, i: (0, p)),
                  pl.BlockSpec((HG_PAIRS, nch, LANE, LANE), lambda p, i: (p, nt - 1 - i, 0, 0)), tile],
        out_specs=[pl.BlockSpec((3, TH, wide), lambda p, i: (0, nt - 1 - i, p)), pl.BlockSpec((2, wide), lambda p, i: (0, p))],
        out_shape=[sds((3, s, 512), BF16), sds((2, 512), F32)],
        scratch_shapes=[pltpu.VMEM((HG_PAIRS, LANE, LANE), F32), pltpu.VMEM((1, wide), F32)],
        compiler_params=_params(("parallel", "arbitrary")),
    )(proj, proj, proj, lbl, states, do_raw)


def _norm_rows_bwd(v, r, g, dn):
    dgv = dn * g
    return r * dgv - v * (r * r * r) * jnp.mean(v * dgv, axis=-1, keepdims=True)


def _qkv_bwd(proj, dq, dk, dvv, g_q, g_kv, w_uq_p, w_k_p, w_v_p, rc, rs1, rs2):
    s = proj.shape[0]

    def body(cq_ref, ckv_ref, dq_ref, dk_ref, dv_ref, gq_ref, gkv_ref, wq_ref, wk_ref, wv_ref, c_ref, s1_ref, s2_ref,
             dcq_ref, dckv_ref, dkpe_ref, dwq_ref, dwk_ref, dwv_ref, dgq_ref, dgkv_ref):
        @pl.when(pl.program_id(0) == 0)
        def _():
            for rf in (dwq_ref, dwk_ref, dwv_ref, dgq_ref, dgkv_ref):
                rf[...] = jnp.zeros_like(rf)

        c, s1, s2 = c_ref[...], s1_ref[...], s2_ref[...]
        cq, ckv = cq_ref[...], ckv_ref[...]
        gq, gkv = gq_ref[...], gkv_ref[...]
        cqn, rq = _norm_rows(cq, gq)
        ckvn, rkv = _norm_rows(ckv, gkv)
        cqn_b, ckvn_b = cqn.astype(BF16), ckvn.astype(BF16)
        dqf = jnp.concatenate([_rope_t(dq_ref[h], c, s1, s2) for h in range(NH)], axis=1).astype(BF16)
        dkf = jnp.concatenate([dk_ref[h] for h in range(NH)], axis=1).astype(BF16)
        dvf = jnp.concatenate([dv_ref[h] for h in range(NH)], axis=1).astype(BF16)
        dkpe = dk_ref[0]
        for h in range(1, NH):
            dkpe = dkpe + dk_ref[h]
        lane = lax.broadcasted_iota(jnp.int32, (TM, LANE), 1)
        dkpe = jnp.where((lane >= QK_NOPE) & (lane < QK_NOPE + QK_ROPE), dkpe, 0.0)
        dkpe_ref[...] = _rope_t(dkpe, c, s1, s2).astype(BF16)
        dcqn = _mm_nt(dqf, wq_ref[...])
        dckvn = _mm_nt(dkf, wk_ref[...]) + _mm_nt(dvf, wv_ref[...])
        dwq_ref[...] += _mm_tn(cqn_b, dqf)
        dwk_ref[...] += _mm_tn(ckvn_b, dkf)
        dwv_ref[...] += _mm_tn(ckvn_b, dvf)
        dgq_ref[...] += jnp.sum(dcqn * (cq * rq), axis=0, keepdims=True)
        dgkv_ref[...] += jnp.sum(dckvn * (ckv * rkv), axis=0, keepdims=True)
        dcq_ref[...] = _norm_rows_bwd(cq, rq, gq, dcqn).astype(BF16)
        dckv_ref[...] = _norm_rows_bwd(ckv, rkv, gkv, dckvn).astype(BF16)

    row = lambda w, j=0: pl.BlockSpec((TM, w), lambda i: (i, j))
    full = lambda a: pl.BlockSpec(a.shape, lambda i: (0,) * a.ndim)
    acc = lambda shape: pl.BlockSpec(shape, lambda i: (0, 0))
    heads = pl.BlockSpec((NH, TM, LANE), lambda i: (0, i, 0))
    sds = jax.ShapeDtypeStruct
    return pl.pallas_call(
        body, name="qkv_bwd", grid=(s // TM,),
        in_specs=[row(Q_LORA, P_CQ // Q_LORA), row(KV_LORA, P_CKV // KV_LORA), heads, heads, heads,
                  full(g_q), full(g_kv), full(w_uq_p), full(w_k_p), full(w_v_p), row(LANE), row(LANE), row(LANE)],
        out_specs=[row(Q_LORA), row(KV_LORA), row(LANE), acc((Q_LORA, D)), acc((KV_LORA, D)), acc((KV_LORA, D)),
                   acc((1, Q_LORA)), acc((1, KV_LORA))],
        out_shape=[sds((s, Q_LORA), BF16), sds((s, KV_LORA), BF16), sds((s, LANE), BF16), sds((Q_LORA, D), F32),
                   sds((KV_LORA, D), F32), sds((KV_LORA, D), F32), sds((1, Q_LORA), F32), sds((1, KV_LORA), F32)],
        compiler_params=_params(("arbitrary",)),
    )(proj, proj, dq, dk, dvv, g_q, g_kv, w_uq_p, w_k_p, w_v_p, rc, rs1, rs2)


def _front_bwd(x, dout, dmg, dga, dh3, dgb, dcq, dckv, dkpe, g_pre, w_in_pt, token):
    s = x.shape[0]

    def body(x_ref, do_ref, dmg_ref, dga_ref, dh3_ref, dgb_ref, dcq_ref, dckv_ref, dkpe_ref, g_ref, w_ref, token_ref, gx_ref, dg_ref):
        @pl.when(pl.program_id(0) == 0)
        def _():
            dg_ref[...] = jnp.zeros_like(dg_ref)

        xv, g = x_ref[...], g_ref[...]
        _, r = _norm_rows(xv, g)
        pieces = ((dmg_ref[...], P_MERGE), (dga_ref[...], P_GA), (dh3_ref[0], P_HQ), (dh3_ref[1], P_HF), (dh3_ref[2], P_HI),
                  (dgb_ref[...], P_GB), (dcq_ref[...], P_CQ), (dckv_ref[...], P_CKV), (dkpe_ref[...], P_KPE))
        dh = jnp.zeros((TM, D), F32)
        for piece, off in pieces:
            dh = dh + _mm(piece, w_ref[off:off + piece.shape[1], :])
        dg_ref[...] += jnp.sum(dh * (xv * r), axis=0, keepdims=True)
        gx_ref[...] = do_ref[...] + _norm_rows_bwd(xv, r, g, dh)

    row = lambda w: pl.BlockSpec((TM, w), lambda i: (i, 0))
    full = lambda a: pl.BlockSpec(a.shape, lambda i: (0,) * a.ndim)
    sds = jax.ShapeDtypeStruct
    return pl.pallas_call(
        body, name="front_bwd", grid=(s // TM,),
        in_specs=[row(D), row(D), row(2048), row(512), pl.BlockSpec((3, TM, 512), lambda i: (0, i, 0)), row(512), row(Q_LORA),
                  row(KV_LORA), row(LANE), full(g_pre), full(w_in_pt), pl.BlockSpec(memory_space=pl.ANY)],
        out_specs=[row(D), pl.BlockSpec((1, D), lambda i: (0, 0))],
        out_shape=[sds((s, D), F32), sds((1, D), F32)],
        compiler_params=_params(("arbitrary",)),
    )(x, dout, dmg, dga, dh3, dgb, dcq, dckv, dkpe, g_pre, w_in_pt, token)


TK_GRAD = 1024


def _win_grad(h, pieces, name):
    s = h.shape[0]
    n = len(pieces)

    def body(h_ref, *refs):
        @pl.when(pl.program_id(0) == 0)
        def _():
            for o_ref in refs[n:]:
                o_ref[...] = jnp.zeros_like(o_ref)

        hv = h_ref[...]
        for d_ref, o_ref in zip(refs[:n], refs[n:]):
            if len(d_ref.shape) == 3:
                for k in range(d_ref.shape[0]):
                    o_ref[k] += _mm_tn(d_ref[k], hv)
            else:
                o_ref[...] += _mm_tn(d_ref[...], hv)

    def in_spec(p):
        if p.ndim == 3:
            return pl.BlockSpec((p.shape[0], TK_GRAD, p.shape[2]), lambda kk: (0, kk, 0))
        return pl.BlockSpec((TK_GRAD, p.shape[1]), lambda kk: (kk, 0))

    out_shapes = [(p.shape[0], p.shape[2], D) if p.ndim == 3 else (p.shape[1], D) for p in pieces]
    return pl.pallas_call(
        body, name=name, grid=(s // TK_GRAD,),
        in_specs=[pl.BlockSpec((TK_GRAD, D), lambda kk: (kk, 0))] + [in_spec(p) for p in pieces],
        out_specs=[pl.BlockSpec(sh, lambda kk, nd=len(sh): (0,) * nd) for sh in out_shapes],
        out_shape=[jax.ShapeDtypeStruct(sh, F32) for sh in out_shapes],
        compiler_params=_params(("arbitrary",)),
    )(h, *pieces)


def _pad_win_t(w_in_t):
    z = lambda n: jnp.zeros((n, w_in_t.shape[1]), w_in_t.dtype)
    sl = lambda o, n: w_in_t[o:o + n]
    return jnp.concatenate([sl(O_MERGE, 2048), sl(O_GA, 512), sl(O_HQ, 512), sl(O_HF, 512), sl(O_HI, 512), sl(O_GB, 512),
                            sl(O_CQ, Q_LORA), sl(O_CKV, KV_LORA), z(64), sl(O_KPE, QK_ROPE), z(32)], axis=0)


def _pad_wuq(w_uq):
    rows = w_uq.shape[0]
    w = w_uq.reshape(rows, NH, QK_NOPE + QK_ROPE)
    return jnp.pad(w, ((0, 0), (0, 0), (0, LANE - QK_NOPE - QK_ROPE))).reshape(rows, NH * LANE)


def _unpad_wuq(g):
    return g.reshape(Q_LORA, NH, LANE)[:, :, :QK_NOPE + QK_ROPE].reshape(Q_LORA, NH * (QK_NOPE + QK_ROPE))


def _pad_wukv(w_ukv):
    heads = w_ukv.shape[1] // (QK_NOPE + V_DIM)
    w = w_ukv.reshape(KV_LORA, heads, QK_NOPE + V_DIM)
    w_k = jnp.pad(w[:, :, :QK_NOPE], ((0, 0), (0, 0), (0, LANE - QK_NOPE))).reshape(KV_LORA, heads * LANE)
    wv = w[:, :, QK_NOPE:].reshape(KV_LORA, heads // 2, 2, 1, V_DIM)
    eye = jnp.eye(2, dtype=w.dtype).reshape(1, 1, 2, 2, 1)
    return w_k, (wv * eye).reshape(KV_LORA, heads * LANE)


def _unpad_wukv(gk, gv):
    gk = gk.reshape(KV_LORA, NH, LANE)[:, :, :QK_NOPE]
    gv = gv.reshape(KV_LORA, NH // 2, 2, 2, V_DIM)
    gv = jnp.stack([gv[:, :, 0, 0], gv[:, :, 1, 1]], axis=2).reshape(KV_LORA, NH, V_DIM)
    return jnp.concatenate([gk, gv], axis=-1).reshape(KV_LORA, NH * (QK_NOPE + V_DIM))


def _local_step(x, tgt, g_pre, w_in_t, b_gate, g_q, g_kv, lb_logits, g_hgrn, g_post, weights, exchange=None):
    s = x.shape[0]
    w_in_p = _pad_win_t(w_in_t)
    rc, rs1, rs2 = _rope_tables(s)
    g_hg = jnp.tile(g_hgrn, (1, NH))

    proj, h = _front_fwd(x, g_pre, w_in_p, weights.tokens)
    w_uq_p, w_k_p, w_v_p = weights.qkv(h)
    q, k, vv = _qkv_fwd(proj, g_q, g_kv, w_uq_p, w_k_p, w_v_p, rc, rs1, rs2)
    attn, lse = _attn_fwd(q, k, vv)
    o_raw, states = _hgrn_fwd(proj, lb_logits)
    wa, wb, w_out = weights.mid(o_raw)
    (loss, dout, dattn, dga, dor, dgb, dmg, d_wout, d_wa, d_wb, d_gpost, d_bgate, d_ghg) = _mid(
        proj, attn, o_raw, x, tgt, g_hg, b_gate, g_post, wa, wb, w_out)
    w_mg, w_ga, w_gb = _win_grad(h, [dmg, dga, dgb], "win_grad_mid")
    dh3, d_lbl = _hgrn_bwd(proj, lb_logits, states, dor)
    (w_h3,) = _win_grad(h, [dh3], "win_grad_hgrn")
    d_win_rest = jnp.concatenate([w_ga, w_h3[0], w_h3[1], w_h3[2], w_gb, w_mg], axis=0)
    early = dict(w_in_rest=d_win_rest, w_branch_a=d_wa, w_branch_b=d_wb, w_out=d_wout)
    token = exchange.start_early(early) if exchange else jnp.zeros((8, LANE), F32)
    dq, dk, dvv = _attn_bwd(q, k, vv, attn, dattn, lse, token)
    dcq, dckv, dkpe, d_wuq_p, d_wk_p, d_wv_p, d_gq, d_gkv = _qkv_bwd(proj, dq, dk, dvv, g_q, g_kv, w_uq_p, w_k_p, w_v_p, rc, rs1, rs2)
    w_cq, w_ckv, w_kpe = _win_grad(h, [dcq, dckv, dkpe], "win_grad_qkv")
    d_win_qkv = jnp.concatenate([w_cq, w_ckv, w_kpe[64:64 + QK_ROPE]], axis=0)
    late = dict(w_in_qkv=d_win_qkv, w_uq=_unpad_wuq(d_wuq_p), w_ukv=_unpad_wukv(d_wk_p, d_wv_p))
    token = exchange.start_late(late) if exchange else jnp.zeros((8, LANE), F32)
    grad_x, d_gpre = _front_bwd(x, dout, dmg, dga, dh3, dgb, dcq, dckv, dkpe, g_pre, w_in_p, token)
    vec_grads = dict(g_pre=d_gpre, b_gate=d_bgate, g_q=d_gq, g_kv=d_gkv, lb_logits=d_lbl, g_hgrn=d_ghg, g_post=d_gpost)
    return loss, grad_x, dict(early, **late), vec_grads


SHARD_SHAPES = (("w_in", (1416, 1024)), ("w_uq", (192, 768)), ("w_ukv", (256, 256)), ("w_branch_a", (512, 256)),
                ("w_branch_b", (512, 256)), ("w_out", (256, 1024)))
BIG = tuple(n for n, _ in SHARD_SHAPES)
ROW_SHARDED = ("w_in", "w_uq", "w_out")
N_CHIPS = 4
QKV_ROWS = Q_LORA + KV_LORA + QK_ROPE
W_IN_FORWARD_CUT = 704


def _to_block(name, a):
    return a[0].T if name == "w_in" else a[0]


def _from_block(name, a):
    return a.T[None] if name == "w_in" else a[None]
VEC_ROWS = (("g_pre", 0, 1024), ("b_gate", 1, 2048), ("g_q", 2, 768), ("g_kv", 3, 256), ("g_hgrn", 6, 64), ("g_post", 7, 1024))
VEC_LB_ROW = 4
VEC_SHAPE = (8, 2048)


def _split_by_chip(name, g):
    a, b = dict(SHARD_SHAPES)[name]
    return g.reshape(N_CHIPS, a, b) if name in ROW_SHARDED else g.reshape(a, N_CHIPS, b).transpose(1, 0, 2)


def _join_chips(name, w):
    a, b = dict(SHARD_SHAPES)[name]
    return w.reshape(N_CHIPS * a, b) if name in ROW_SHARDED else w.transpose(1, 0, 2).reshape(a, N_CHIPS * b)


MESH = pl.DeviceIdType.MESH
HBM = pl.BlockSpec(memory_space=pltpu.HBM)


def _mesh_place():
    x, y, c = lax.axis_index("x"), lax.axis_index("y"), lax.axis_index("c")
    return x, y, c, 2 * x + y, [(1 - x, y), (x, 1 - y), (1 - x, 1 - y)]


def _remote(src, dst, send_sems, recv_sems, k, to):
    return pltpu.make_async_remote_copy(src_ref=src, dst_ref=dst, send_sem=send_sems.at[k], recv_sem=recv_sems.at[k],
                                        device_id=to, device_id_type=MESH)


def _gather_w_in(shard):
    a, b = shard.shape
    cut = W_IN_FORWARD_CUT

    def body(src, out, ici_send, ici_recv, d2d_send, d2d_recv, local_sem):
        x, y, c = lax.axis_index("x"), lax.axis_index("y"), lax.axis_index("c")
        me, xn, yn, dg = 2 * x + y, 2 * (1 - x) + y, 2 * x + (1 - y), 2 * (1 - x) + (1 - y)
        to_x, to_y, sibling = (1 - x, y, c), (x, 1 - y, c), (x, y, 1 - c)
        first, rest = pl.ds(0, cut), pl.ds(cut, a - cut)

        whole = lambda ref, which: ref.at[:, pl.ds(pl.multiple_of(which * (b // 2), b // 2), b // 2)]
        own = pltpu.make_async_copy(src, out.at[me], local_sem)
        own.start()
        sends = [_remote(whole(src, c), whole(out.at[me], c), ici_send, ici_recv, 0, to_x),
                 _remote(whole(src, c), whole(out.at[me], c), ici_send, ici_recv, 1, to_y)]
        for cp in sends:
            cp.start()

        def landed(slot, rows, k, d2d_k, src_dev):
            piece = whole(out.at[slot], c) if rows is None else out.at[slot].at[rows, pl.ds(pl.multiple_of(c * (b // 2), b // 2), b // 2)]
            _remote(piece, piece, ici_send, ici_recv, k, src_dev).wait_recv()
            cp = _remote(piece, piece, d2d_send, d2d_recv, d2d_k, sibling)
            cp.start()
            sends.append(cp)
            return piece

        def pass_on(slot, rows, k, to):
            piece = out.at[slot].at[rows, pl.ds(pl.multiple_of(c * (b // 2), b // 2), b // 2)]
            cp = _remote(piece, piece, ici_send, ici_recv, k, to)
            cp.start()
            sends.append(cp)

        landed(xn, None, 0, 0, to_x)
        pass_on(xn, first, 2, to_y)
        landed(yn, None, 1, 1, to_y)
        pass_on(yn, rest, 3, to_x)
        landed(dg, first, 2, 2, to_y)
        landed(dg, rest, 3, 3, to_x)
        other = pl.ds(pl.multiple_of((1 - c) * (b // 2), b // 2), b // 2)
        for d2d_k, (slot, rows) in enumerate(((xn, None), (yn, None), (dg, first), (dg, rest))):
            piece = out.at[slot].at[:, other] if rows is None else out.at[slot].at[rows, other]
            _remote(piece, piece, d2d_send, d2d_recv, d2d_k, sibling).wait_recv()
        for cp in sends:
            cp.wait_send()
        own.wait()

    sems = pltpu.SemaphoreType.DMA((4,))
    return pl.pallas_call(
        body, name="gather_w_in", in_specs=[HBM], out_specs=HBM,
        out_shape=jax.ShapeDtypeStruct((N_CHIPS, a, b), shard.dtype),
        scratch_shapes=[sems, sems, sems, sems, pltpu.SemaphoreType.DMA],
        compiler_params=pltpu.CompilerParams(has_side_effects=True),
    )(shard)


def _sibling_exchange(srcs, name, after=None):
    n = len(srcs)
    extra = [] if after is None else [after]

    def body(*refs):
        src_refs, outs = refs[:n], refs[n + len(extra):2 * n + len(extra)]
        send_sems, recv_sems = refs[2 * n + len(extra):]
        sibling = (lax.axis_index("x"), lax.axis_index("y"), 1 - lax.axis_index("c"))
        copies = [_remote(src_refs[k], outs[k], send_sems, recv_sems, k, sibling) for k in range(n)]
        for cp in copies:
            cp.start()
        for cp in copies:
            cp.wait()

    sems = pltpu.SemaphoreType.DMA((n,))
    return pl.pallas_call(
        body, name=name, in_specs=[HBM] * n + [pl.BlockSpec(memory_space=pl.ANY)] * len(extra), out_specs=[HBM] * n,
        out_shape=[jax.ShapeDtypeStruct(s.shape, s.dtype) for s in srcs],
        scratch_shapes=[sems, sems],
        compiler_params=pltpu.CompilerParams(has_side_effects=True),
    )(*srcs, *extra)


SEM = pl.BlockSpec(memory_space=pltpu.SEMAPHORE)
DATAFLOW = pltpu.SideEffectType.DATAFLOW_SIDE_EFFECTING


def _exchange_copies(srcs, to_first, src_refs, land_refs, send_sems, recv_sems):
    x, y, c, me, chips = _mesh_place()
    n = len(srcs)
    sends, recvs = [], []
    for k in range(n):
        if k in to_first:
            base = 3 * n + 4 * to_first.index(k)
            sends.append((me != 0, pltpu.make_async_remote_copy(
                src_ref=src_refs[k], dst_ref=land_refs[k].at[me], send_sem=send_sems.at[base], recv_sem=recv_sems.at[base + me],
                device_id=(0, 0, c), device_id_type=MESH)))
            for s in range(1, N_CHIPS):
                recvs.append((me == 0, pltpu.make_async_remote_copy(
                    src_ref=src_refs[k], dst_ref=land_refs[k].at[s], send_sem=send_sems.at[base], recv_sem=recv_sems.at[base + s],
                    device_id=(s // 2, s % 2, c), device_id_type=MESH)))
        else:
            slab = (lambda t, k=k: src_refs[k]) if srcs[k].ndim == 2 else (lambda t, k=k: src_refs[k].at[t])
            for j, (px, py) in enumerate(chips):
                sends.append((None, _remote(slab(2 * px + py), land_refs[k].at[me], send_sems, recv_sems, 3 * k + j, (px, py, c))))
                recvs.append((None, _remote(slab(me), land_refs[k].at[2 * px + py], send_sems, recv_sems, 3 * k + j, (px, py, c))))
    return sends, recvs


def _when(pred, fn):
    if pred is None:
        fn()
    else:
        pl.when(pred)(fn)


def _exchange_start(srcs, to_first, name, after=None):
    n = len(srcs)
    n_sems = 3 * n + 4 * len(to_first)
    lands = [lax.empty((N_CHIPS,) + s.shape[-2:], s.dtype) for s in srcs]
    extra = [] if after is None else [after]

    def body(*refs):
        src_refs, land_refs = refs[:n], refs[n:2 * n]
        send_sems, recv_sems, token = refs[2 * n + len(extra)], refs[2 * n + len(extra) + 1], refs[-1]
        sends, _ = _exchange_copies(srcs, to_first, src_refs, land_refs, send_sems, recv_sems)
        for pred, cp in sends:
            _when(pred, cp.start)
        token[...] = jnp.zeros_like(token)

    hbm = lambda a: pltpu.HBM(a.shape, a.dtype)
    res = pl.pallas_call(
        body, name=name,
        out_shape=[pltpu.SemaphoreType.DMA((n_sems,)), pltpu.SemaphoreType.DMA((n_sems,))] + [hbm(a) for a in srcs + lands]
        + [jax.ShapeDtypeStruct((8, LANE), F32)],
        in_specs=[HBM] * (2 * n) + [pl.BlockSpec(memory_space=pl.ANY)] * len(extra),
        out_specs=[SEM, SEM] + [HBM] * (2 * n) + [pl.BlockSpec(memory_space=pltpu.VMEM)],
        input_output_aliases={i: 2 + i for i in range(2 * n)},
        compiler_params=pltpu.CompilerParams(has_side_effects=DATAFLOW),
    )(*[pltpu.with_memory_space_constraint(a, pltpu.HBM) for a in srcs + lands], *extra)
    return res[:-1], res[-1]


def _exchange_wait(srcs, to_first, started, after, name):
    n = len(srcs)
    send_sems, recv_sems, thru = started[0], started[1], started[2:]

    def body(*refs):
        src_refs, land_refs, send_ref, recv_ref = refs[:n], refs[n:2 * n], refs[2 * n], refs[2 * n + 1]
        sends, recvs = _exchange_copies(srcs, to_first, src_refs, land_refs, send_ref, recv_ref)
        for pred, cp in sends:
            _when(pred, cp.wait_send)
        for pred, cp in recvs:
            _when(pred, cp.wait_recv)

    res = pl.pallas_call(
        body, name=name, out_shape=[pltpu.HBM(a.shape, a.dtype) for a in thru],
        in_specs=[HBM] * (2 * n) + [SEM, SEM, pl.BlockSpec(memory_space=pl.ANY)], out_specs=[HBM] * (2 * n),
        input_output_aliases={i: i for i in range(2 * n)},
        compiler_params=pltpu.CompilerParams(has_side_effects=DATAFLOW),
    )(*thru, send_sems, recv_sems, after)
    return res[n:]


ROW_TILE = 256
COL_TILE = 256


def _block_tiling(a, b):
    if a <= ROW_TILE or a % ROW_TILE == 0:
        ta = min(a, ROW_TILE)
        return a // ta, (ta, b), lambda i: (i, 0)
    return b // COL_TILE, (a, COL_TILE), lambda i: (0, i)


def _sum_landed(land, own, name, first_land=None, first_own=None):
    _, a, b = land.shape
    steps, tile, at = _block_tiling(a, b)
    extra = first_land is not None

    def body(*refs):
        p_ref, own_ref, o_ref = refs[0], refs[1], refs[-1]
        me = 2 * lax.axis_index("x") + lax.axis_index("y")
        own = own_ref[...].astype(F32)
        slot = lambda t: jnp.where(me == t, own, p_ref[t].astype(F32))
        o_ref[...] = ((slot(0) + slot(1)) + slot(2)) + slot(3)
        if extra:
            fp_ref, fo_ref = refs[2], refs[3]
            r = fo_ref.shape[0]

            @pl.when(me == 0)
            def _():
                f = lambda t: fp_ref[t].astype(F32)
                rows = pl.ds(pl.multiple_of(lax.axis_index("c") * r, 8), r)
                o_ref[rows, :] += ((fo_ref[...].astype(F32) + f(1)) + f(2)) + f(3)

    in_specs = [pl.BlockSpec((N_CHIPS,) + tile, lambda i: (0,) + at(i)), pl.BlockSpec(tile, at)]
    args = [land, own]
    if extra:
        r = first_own.shape[0]
        assert tile[0] == a, "the extra rows need whole columns in a step"
        in_specs += [pl.BlockSpec((N_CHIPS, r, tile[1]), lambda i: (0,) + at(i)), pl.BlockSpec((r, tile[1]), at)]
        args += [first_land, first_own]
    return pl.pallas_call(
        body, name=name, grid=(steps,), in_specs=in_specs, out_specs=pl.BlockSpec(tile, at),
        out_shape=jax.ShapeDtypeStruct((a, b), F32), compiler_params=_params(("parallel",)),
    )(*args)


def _sum_landed_small(lands, owns, name):
    n = len(lands)

    def body(*refs):
        me = 2 * lax.axis_index("x") + lax.axis_index("y")
        for p_ref, own_ref, o_ref in zip(refs[:n], refs[n:2 * n], refs[2 * n:]):
            own = own_ref[...].astype(F32)
            slot = lambda t: jnp.where(me == t, own, p_ref[t].astype(F32))
            o_ref[...] = ((slot(0) + slot(1)) + slot(2)) + slot(3)

    return pl.pallas_call(body, name=name, out_shape=[jax.ShapeDtypeStruct(o.shape, F32) for o in owns],
                          compiler_params=_params(()))(*lands, *owns)


def _adamw_small(mine, theirs, states, name):
    n = len(mine)

    def body(*refs):
        ins, outs = refs[:5 * n], refs[5 * n:]
        for k in range(n):
            a_ref, b_ref, w_ref, m_ref, v_ref = ins[5 * k:5 * k + 5]
            g = a_ref[...] + b_ref[...]
            outs[4 * k][...] = g
            outs[4 * k + 1][...], outs[4 * k + 2][...], outs[4 * k + 3][...] = _adamw_math(g, w_ref[...], m_ref[...], v_ref[...])

    args = [t for k in range(n) for t in (mine[k], theirs[k], *states[k])]
    res = pl.pallas_call(body, name=name, out_shape=[jax.ShapeDtypeStruct(mine[k].shape, F32) for k in range(n) for _ in range(4)],
                         compiler_params=_params(()))(*args)
    return [tuple(res[4 * k:4 * k + 4]) for k in range(n)]


def _add_cast(a, b, name):
    def body(a_ref, b_ref, o_ref):
        o_ref[...] = (a_ref[...] + b_ref[...]).astype(BF16)

    return pl.pallas_call(body, name=name, out_shape=jax.ShapeDtypeStruct(a.shape, BF16),
                          compiler_params=_params(()))(a, b)


class _LaterWeights:
    MID = ("w_branch_a", "w_branch_b", "w_out")

    def __init__(self, blocks, after):
        self.qkv_blocks = [_pad_wuq(blocks["w_uq"]), *_pad_wukv(blocks["w_ukv"])]
        self.mid_blocks = [blocks[n] for n in self.MID]
        self.qkv_started, t1 = _exchange_start(self.qkv_blocks, (), "weights_qkv_start", after)
        self.mid_started, t2 = _exchange_start(self.mid_blocks, (), "weights_mid_start", after)
        self.tokens = [t1, t2]

    @staticmethod
    def _whole(blocks, rows_sharded, started, after, name):
        landed = _exchange_wait(blocks, (), started, after, name)
        me = 2 * lax.axis_index("x") + lax.axis_index("y")
        out = []
        for block, land, by_rows in zip(blocks, landed, rows_sharded):
            w = lax.dynamic_update_index_in_dim(land, block, me, 0)
            a, b = block.shape
            out.append(w.reshape(N_CHIPS * a, b) if by_rows else w.transpose(1, 0, 2).reshape(a, N_CHIPS * b))
        return out

    def qkv(self, after):
        return self._whole(self.qkv_blocks, (True, False, False), self.qkv_started, after, "weights_qkv_wait")

    def mid(self, after):
        return self._whole(self.mid_blocks, (False, False, True), self.mid_started, after, "weights_mid_wait")


class _GradExchange:
    EARLY = ("w_in", "w_branch_a", "w_branch_b", "w_out")
    LATE = ("w_uq", "w_ukv")

    def __init__(self, state):
        self.state = state
        self.outs = {}

    @staticmethod
    def _own(slabs):
        return lax.dynamic_index_in_dim(slabs, 2 * lax.axis_index("x") + lax.axis_index("y"), axis=0, keepdims=False)

    def start_early(self, g):
        full = jnp.concatenate([jnp.zeros((QKV_ROWS, D), F32), g["w_in_rest"]], axis=0)
        g = dict(g, w_in=full)
        self.early = [_split_by_chip(n, g[n]).astype(BF16) for n in self.EARLY]
        self.early_started, token = _exchange_start(self.early, (), "grads_early_start")
        return token

    def start_late(self, g):
        self.early_landed = _exchange_wait(self.early, (), self.early_started, g["w_uq"], "grads_early_wait")
        half = QKV_ROWS // 2
        c = lax.axis_index("c")
        mine = lax.dynamic_slice_in_dim(g["w_in_qkv"], c * half, half, axis=0)
        (theirs,) = _sibling_exchange([lax.dynamic_slice_in_dim(g["w_in_qkv"], (1 - c) * half, half, axis=0)], "sibling_qkv_rows")
        self.late = [_split_by_chip(n, g[n]).astype(BF16) for n in self.LATE] + [_add_cast(mine, theirs, "add_qkv_rows")]
        self.late_started, token = _exchange_start(self.late, (2,), "grads_late_start")
        names = self.EARLY[1:]
        mine = _sum_landed_small(self.early_landed[1:], [self._own(slabs) for slabs in self.early[1:]], "sum_early")
        theirs = _sibling_exchange(mine, "sibling_early", after=token)
        for n, out in zip(names, _adamw_small(mine, theirs, [self.state[n] for n in names], "adamw_early")):
            self.outs[n] = out
        return self.outs[names[-1]][0]

    def finish(self, after):
        late_landed = _exchange_wait(self.late, (2,), self.late_started, after, "grads_late_wait")
        sums = {"w_in": _sum_landed(self.early_landed[0], self._own(self.early[0]), "sum_w_in",
                                    first_land=late_landed[2], first_own=self.late[2])}
        small = _sum_landed_small(late_landed[:2], [self._own(slabs) for slabs in self.late[:2]], "sum_late")
        sums.update(zip(self.LATE, small))
        return sums


def _adamw_math(g, w, m, v):
    nm = ADAM_B1 * m + (1.0 - ADAM_B1) * g
    nv = ADAM_B2 * v + (1.0 - ADAM_B2) * (g * g)
    m_hat = nm / (1.0 - ADAM_B1 ** ADAM_STEP)
    v_hat = nv / (1.0 - ADAM_B2 ** ADAM_STEP)
    return -ADAM_LR * (m_hat / (jnp.sqrt(v_hat) + ADAM_EPS) + ADAM_WD * w), nm, nv


def _adamw(p_mine, p_sibling, w, m, v, name):
    a, b = p_mine.shape
    steps, tile, at = _block_tiling(a, b)

    def body(a_ref, b_ref, w_ref, m_ref, v_ref, g_ref, d_ref, nm_ref, nv_ref):
        g = a_ref[...] + b_ref[...]
        g_ref[...] = g
        d_ref[...], nm_ref[...], nv_ref[...] = _adamw_math(g, w_ref[...], m_ref[...], v_ref[...])

    spec = pl.BlockSpec(tile, at)
    sds = jax.ShapeDtypeStruct((a, b), F32)
    return pl.pallas_call(
        body, name=name, grid=(steps,), in_specs=[spec] * 5, out_specs=[spec] * 4, out_shape=[sds] * 4,
        compiler_params=_params(("parallel",)),
    )(p_mine, p_sibling, w, m, v)


LOSS_AT = (2, 1024)


def _vec_pack(vg, loss):
    names = [n for n, _, _ in VEC_ROWS]

    def body(*refs):
        o_ref = refs[-1]
        lb_ref, loss_ref = refs[len(names)], refs[len(names) + 1]
        o_ref[...] = jnp.zeros_like(o_ref)
        o_ref[LOSS_AT[0]:LOSS_AT[0] + 1, LOSS_AT[1]:LOSS_AT[1] + LANE] = jnp.broadcast_to(loss_ref[...], (1, LANE))
        for (name, row, size), ref in zip(VEC_ROWS, refs):
            if name == "g_hgrn":
                r = lax.broadcasted_iota(jnp.int32, (NH * V_DIM, LANE), 0)
                c = lax.broadcasted_iota(jnp.int32, (NH * V_DIM, LANE), 1)
                fold = ((r % V_DIM) == c).astype(F32)
                o_ref[row:row + 1, 0:LANE] = jnp.dot(ref[...], fold, precision=HIGHEST, preferred_element_type=F32)
            else:
                o_ref[row:row + 1, 0:size] = ref[...]
        o_ref[VEC_LB_ROW:VEC_LB_ROW + 2, 0:512] = lb_ref[...]

    return pl.pallas_call(body, name="vec_pack", out_shape=jax.ShapeDtypeStruct(VEC_SHAPE, F32))(
        *[vg[n] for n in names], vg["lb_logits"], loss)


def _adamw_vec(p_mine, p_sibling, w, m, v):
    names = [n for n, _, _ in VEC_ROWS] + ["lb_logits"]
    k = len(names)

    def body(a_ref, b_ref, *refs):
        ins, outs = refs[:3 * k], refs[3 * k:]
        at = (slice(LOSS_AT[0], LOSS_AT[0] + 1), slice(LOSS_AT[1], LOSS_AT[1] + LANE))
        outs[-1][...] = a_ref[at] + b_ref[at]
        for i, name in enumerate(names):
            if name == "lb_logits":
                rows, cols = slice(VEC_LB_ROW, VEC_LB_ROW + 2), slice(0, 512)
            else:
                _, row, size = VEC_ROWS[i]
                rows, cols = slice(row, row + 1), slice(0, size)
            g = a_ref[rows, cols] + b_ref[rows, cols]
            d, nm, nv = _adamw_math(g, ins[i][...], ins[k + i][...], ins[2 * k + i][...])
            for o_ref, val in zip(outs[4 * i:4 * i + 4], (g, d, nm, nv)):
                o_ref[...] = val

    shapes = [jax.ShapeDtypeStruct(w[n].shape, F32) for n in names for _ in range(4)] + [jax.ShapeDtypeStruct((1, LANE), F32)]
    res = pl.pallas_call(body, name="adamw_vec", out_shape=shapes)(
        p_mine, p_sibling, *[w[n] for n in names], *[m[n] for n in names], *[v[n] for n in names])
    return [{n: res[4 * i + j] for i, n in enumerate(names)} for j in range(4)], res[-1]


WEIGHTS = ("g_pre", "w_in", "b_gate", "g_q", "w_uq", "g_kv", "w_ukv", "lb_logits", "g_hgrn", "w_branch_a", "w_branch_b", "w_out", "g_post")


def kernel(x, g_pre, w_in, b_gate, g_q, w_uq, g_kv, w_ukv, lb_logits, g_hgrn, w_branch_a, w_branch_b, w_out, g_post, loss_target, m_g_pre, m_w_in, m_b_gate, m_g_q, m_w_uq, m_g_kv, m_w_ukv, m_lb_logits, m_g_hgrn, m_w_branch_a, m_w_branch_b, m_w_out, m_g_post, v_g_pre, v_w_in, v_b_gate, v_g_q, v_w_uq, v_g_kv, v_w_ukv, v_lb_logits, v_g_hgrn, v_w_branch_a, v_w_branch_b, v_w_out, v_g_post):
    w = dict(g_pre=g_pre, w_in=w_in, b_gate=b_gate, g_q=g_q, w_uq=w_uq, g_kv=g_kv, w_ukv=w_ukv, lb_logits=lb_logits, g_hgrn=g_hgrn,
             w_branch_a=w_branch_a, w_branch_b=w_branch_b, w_out=w_out, g_post=g_post)
    m = dict(g_pre=m_g_pre, w_in=m_w_in, b_gate=m_b_gate, g_q=m_g_q, w_uq=m_w_uq, g_kv=m_g_kv, w_ukv=m_w_ukv, lb_logits=m_lb_logits,
             g_hgrn=m_g_hgrn, w_branch_a=m_w_branch_a, w_branch_b=m_w_branch_b, w_out=m_w_out, g_post=m_g_post)
    v = dict(g_pre=v_g_pre, w_in=v_w_in, b_gate=v_b_gate, g_q=v_g_q, w_uq=v_w_uq, g_kv=v_g_kv, w_ukv=v_w_ukv, lb_logits=v_lb_logits,
             g_hgrn=v_g_hgrn, w_branch_a=v_w_branch_a, w_branch_b=v_w_branch_b, w_out=v_w_out, g_post=v_g_post)
    blocks = {n: _to_block(n, w[n]).astype(BF16) for n in BIG}
    w_in_all = _gather_w_in(blocks["w_in"])
    weights = _LaterWeights(blocks, w_in_all)
    state = {n: [_to_block(n, t[n]) for t in (w, m, v)] for n in BIG}
    exchange = _GradExchange(state)
    loss, grad_x, _, vec_grads = _local_step(
        x[0], loss_target[0], g_pre, _join_chips("w_in", w_in_all), b_gate, g_q, g_kv, lb_logits, g_hgrn, g_post, weights, exchange)
    vec = _vec_pack(vec_grads, loss)
    vec_started, token = _exchange_start([vec], (), "vec_start")
    sums = exchange.finish(token)
    rest = tuple(sums)
    (vec_landed,) = _exchange_wait([vec], (), vec_started, sums[rest[-1]], "vec_wait")
    mine = [sums[n] for n in rest] + [_sum_landed(vec_landed, vec, "sum_vec")]
    theirs = _sibling_exchange(mine, "sibling_grads")
    done = dict(exchange.outs)
    done["w_in"] = _adamw(mine[0], theirs[0], *state["w_in"], "adamw_w_in")
    small = _adamw_small(mine[1:-1], theirs[1:-1], [state[n] for n in rest[1:]], "adamw_late")
    done.update(zip(rest[1:], small))
    outs = [{}, {}, {}, {}]
    for n in BIG:
        for o, val in zip(outs, done[n]):
            o[n] = _from_block(n, val)
    vec_outs, total = _adamw_vec(mine[-1], theirs[-1], w, m, v)
    for o, vals in zip(outs, vec_outs):
        o.update(vals)
    return (total[0, 0], grad_x[None], *[o[n] for o in outs for n in WEIGHTS])
```

```python
import math

import numpy as np
import jax
import jax.numpy as jnp
from jax import lax
from jax.experimental import pallas as pl
from jax.experimental.pallas import tpu as pltpu

F32 = jnp.float32
BF16 = jnp.bfloat16
HIGHEST = lax.Precision.HIGHEST

D = 1024
NH = 8
QK_NOPE, QK_ROPE, V_DIM = 64, 32, 64
Q_LORA, KV_LORA = 768, 256
CHUNK = 64
HG_BLOCK = 32
EPS = 1e-6
LANE = 128
P_MERGE, P_GA, P_HQ, P_HF, P_HI, P_GB, P_CQ, P_CKV, P_KPE = 0, 2048, 2560, 3072, 3584, 4096, 4608, 5376, 5632
D_P = 5760
O_CQ, O_CKV, O_KPE, O_GA, O_HQ, O_HF, O_HI, O_GB, O_MERGE = 0, 768, 1024, 1056, 1568, 2080, 2592, 3104, 3616

TM = 512
TM_MID = 256
TQ = 1024
ONES_LANE = (LANE - 1, 0)
TH = 256
HG_PAIRS = 4
VMEM_LIMIT = 56 * 1024 * 1024

ADAM_LR, ADAM_B1, ADAM_B2, ADAM_EPS, ADAM_WD, ADAM_STEP = 0.001, 0.9, 0.999, 1e-08, 0.01, 10

NT_DIMS = (((1,), (1,)), ((), ()))
TN_DIMS = (((0,), (0,)), ((), ()))


def _params(sem):
    return pltpu.CompilerParams(dimension_semantics=sem, vmem_limit_bytes=VMEM_LIMIT)


def _mm(a, b):
    return jnp.dot(a, b, preferred_element_type=F32)


def _mm_nt(a, b):
    return lax.dot_general(a, b, NT_DIMS, preferred_element_type=F32)


def _mm_tn(a, b):
    return lax.dot_general(a, b, TN_DIMS, preferred_element_type=F32)


def _sigmoid(z):
    return jax.nn.sigmoid(z)


def _rope(v, c, s1, s2):
    return v * c + pltpu.roll(v, 112, 1) * s1 + pltpu.roll(v, 16, 1) * s2


def _rope_t(dy, c, s1, s2):
    return dy * c + pltpu.roll(dy * s1, 16, 1) + pltpu.roll(dy * s2, 112, 1)


def _rope_tables(s):
    f32 = np.float32
    inv = f32(10000.0) ** (-np.arange(0, QK_ROPE, 2, dtype=f32) / f32(QK_ROPE))
    ang = np.arange(s, dtype=f32)[:, None] * inv[None, :]
    cos, sin = np.cos(ang).astype(f32), np.sin(ang).astype(f32)
    z64, z32, o64, o32 = np.zeros((s, 64), f32), np.zeros((s, 32), f32), np.ones((s, 64), f32), np.ones((s, 32), f32)
    z16 = np.zeros((s, 16), f32)
    c = np.concatenate([o64, cos, cos, o32], axis=1)
    s1 = np.concatenate([z64, -sin, z16, z32], axis=1)
    s2 = np.concatenate([z64, z16, sin, z32], axis=1)
    return jnp.asarray(c), jnp.asarray(s1), jnp.asarray(s2)


def _front_fwd(x, g_pre, w_in_pt, tokens=()):
    s = x.shape[0]
    tokens = list(tokens)

    def body(x_ref, g_ref, w_ref, *refs):
        o_ref, h_ref = refs[len(tokens):]
        xv = x_ref[...]
        r = lax.rsqrt(jnp.mean(xv * xv, axis=-1, keepdims=True) + EPS)
        h = ((xv * r) * g_ref[...]).astype(BF16)
        h_ref[...] = h
        o_ref[...] = _mm_nt(h, w_ref[...])

    return pl.pallas_call(
        body, name="front_fwd", grid=(s // TM,),
        in_specs=[pl.BlockSpec((TM, D), lambda i: (i, 0)), pl.BlockSpec((1, D), lambda i: (0, 0)),
                  pl.BlockSpec((D_P, D), lambda i: (0, 0))] + [pl.BlockSpec((8, LANE), lambda i: (0, 0))] * len(tokens),
        out_specs=[pl.BlockSpec((TM, D_P), lambda i: (i, 0)), pl.BlockSpec((TM, D), lambda i: (i, 0))],
        out_shape=[jax.ShapeDtypeStruct((s, D_P), F32), jax.ShapeDtypeStruct((s, D), BF16)],
        compiler_params=_params(("parallel",)),
    )(x, g_pre, w_in_pt, *tokens)


def _norm_rows(v, g):
    r = lax.rsqrt(jnp.mean(v * v, axis=-1, keepdims=True) + EPS)
    return (v * r) * g, r


def _qkv_fwd(proj, g_q, g_kv, w_uq_p, w_k_p, w_v_p, rc, rs1, rs2):
    s = proj.shape[0]

    def body(cq_ref, ckv_ref, kpe_ref, gq_ref, gkv_ref, wq_ref, wk_ref, wv_ref, c_ref, s1_ref, s2_ref, q_ref, k_ref, v_ref):
        c, s1, s2 = c_ref[...], s1_ref[...], s2_ref[...]
        cqn, _ = _norm_rows(cq_ref[...], gq_ref[...])
        ckvn, _ = _norm_rows(ckv_ref[...], gkv_ref[...])
        ckvn = ckvn.astype(BF16)
        qf = _mm(cqn.astype(BF16), wq_ref[...])
        kf = _mm(ckvn, wk_ref[...])
        vf = _mm(ckvn, wv_ref[...])
        kpe = _rope(kpe_ref[...], c, s1, s2)
        lane = lax.broadcasted_iota(jnp.int32, (TM, LANE), 1)
        for h in range(NH):
            blk = slice(h * LANE, (h + 1) * LANE)
            q_ref[h] = _rope(qf[:, blk], c, s1, s2).astype(BF16)
            k_ref[h] = (kf[:, blk] + kpe).astype(BF16)
            v_ref[h] = jnp.where(lane == ONES_LANE[h % 2], 1.0, vf[:, blk]).astype(BF16)

    row = lambda w, j: pl.BlockSpec((TM, w), lambda i: (i, j))
    full = lambda a: pl.BlockSpec(a.shape, lambda i: (0,) * a.ndim)
    hs = jax.ShapeDtypeStruct((NH, s, LANE), BF16)
    return pl.pallas_call(
        body, name="qkv_fwd", grid=(s // TM,),
        in_specs=[row(Q_LORA, P_CQ // Q_LORA), row(KV_LORA, P_CKV // KV_LORA), row(LANE, P_KPE // LANE),
                  full(g_q), full(g_kv), full(w_uq_p), full(w_k_p), full(w_v_p), row(LANE, 0), row(LANE, 0), row(LANE, 0)],
        out_specs=[pl.BlockSpec((NH, TM, LANE), lambda i: (0, i, 0))] * 3,
        out_shape=[hs, hs, hs],
        compiler_params=_params(("parallel",)),
    )(proj, proj, proj, g_q, g_kv, w_uq_p, w_k_p, w_v_p, rc, rs1, rs2)


LOG2E = 1.4426950408889634
QK_SCALE2 = LOG2E / math.sqrt(QK_NOPE + QK_ROPE)


HQ = TQ // 2


def _diag_visible(n):
    row = lax.broadcasted_iota(jnp.int32, (n, n), 0)
    col = lax.broadcasted_iota(jnp.int32, (n, n), 1)
    return (col // CHUNK) <= (row // CHUNK)


def _attn_fwd(q, k, vv):
    s = q.shape[1]

    def body(q_ref, k_ref, v_ref, o_ref, lse_ref):
        i = pl.program_id(1)
        qs = (q_ref[0], q_ref[1])

        def tiles(t, carry, diag):
            rows = pl.ds(pl.multiple_of(t * TQ, TQ), TQ)
            sc = [_mm_nt(qs[hh], k_ref[hh, rows, :]) for hh in range(2)]
            if diag:
                sc = [jnp.where(_diag_visible(TQ), s_, -jnp.inf) for s_ in sc]
            m_new = [jnp.maximum(carry[hh][0], jnp.max(sc[hh], axis=-1, keepdims=True)) for hh in range(2)]
            alpha = [jnp.exp2((carry[hh][0] - m_new[hh]) * QK_SCALE2) for hh in range(2)]
            p = [jnp.exp2((sc[hh] - m_new[hh]) * QK_SCALE2).astype(BF16) for hh in range(2)]
            acc = [alpha[hh] * carry[hh][1] + _mm(p[hh], v_ref[hh, rows, :]) for hh in range(2)]
            return (m_new[0], acc[0]), (m_new[1], acc[1])

        init = (jnp.full((TQ, 1), -jnp.inf, F32), jnp.zeros((TQ, LANE), F32))
        carry = lax.fori_loop(0, i, lambda t, c: tiles(t, c, False), (init, init))
        carry = tiles(i, carry, True)
        lane = lax.broadcasted_iota(jnp.int32, (TQ, LANE), 1)
        out = jnp.zeros((TQ, LANE), F32)
        for hh in range(2):
            m, acc = carry[hh]
            l = jnp.sum(jnp.where(lane == ONES_LANE[hh], acc, 0.0), axis=-1, keepdims=True)
            out = out + jnp.where((lane < V_DIM) == (hh == 0), acc, 0.0) / l
            lse_ref[hh] = jnp.broadcast_to(m * QK_SCALE2 + jnp.log(l) * LOG2E, (TQ, LANE))
        o_ref[...] = out

    return pl.pallas_call(
        body, name="attn_fwd", grid=(NH // 2, s // TQ),
        in_specs=[pl.BlockSpec((2, TQ, LANE), lambda p, i: (p, i, 0)), pl.BlockSpec((2, s, LANE), lambda p, i: (p, 0, 0)),
                  pl.BlockSpec((2, s, LANE), lambda p, i: (p, 0, 0))],
        out_specs=[pl.BlockSpec((TQ, LANE), lambda p, i: (i, p)), pl.BlockSpec((2, TQ, LANE), lambda p, i: (p, i, 0))],
        out_shape=[jax.ShapeDtypeStruct((s, NH * V_DIM), F32), jax.ShapeDtypeStruct((NH, s, LANE), F32)],
        compiler_params=_params(("parallel", "parallel")),
    )(q, k, vv)


def _lower_bound(lbl):
    a0, a1 = lbl[0:1, :], lbl[1:2, :]
    mx = jnp.maximum(a0, a1)
    e0, e1 = jnp.exp(a0 - mx), jnp.exp(a1 - mx)
    return e0 / (e0 + e1)


def _chunk_cumsum(v, reverse=False):
    pos = lax.broadcasted_iota(jnp.int32, v.shape, 0) % HG_BLOCK
    s = 1
    while s < HG_BLOCK:
        if reverse:
            v = v + jnp.where(pos < HG_BLOCK - s, pltpu.roll(v, TH - s, 0), 0.0)
        else:
            v = v + jnp.where(pos >= s, pltpu.roll(v, s, 0), 0.0)
        s *= 2
    return v


def _hgrn_gates(hq, hf, lb):
    sig = _sigmoid(hf)
    f = lb + (1.0 - lb) * sig
    g = jnp.log(f)
    kk = 1.0 - f
    r = lax.broadcasted_iota(jnp.int32, (TH, TH), 0)
    c = lax.broadcasted_iota(jnp.int32, (TH, TH), 1)
    tri = ((r // HG_BLOCK) == (c // HG_BLOCK)) & (r >= c)
    cum = _chunk_cumsum(g)
    nch = TH // HG_BLOCK
    total = _chunks(cum)[:, HG_BLOCK - 1:HG_BLOCK, :]
    lastb = jnp.broadcast_to(total, (nch, HG_BLOCK, hf.shape[-1])).reshape(hf.shape)
    e, ei, ee = jnp.exp(cum), jnp.exp(-cum), jnp.exp(lastb - cum)
    return dict(sig=sig, f=f, kk=kk, tri=tri, cum=cum, total=total, decay=jnp.exp(total), e=e, ei=ei, ee=ee,
                qd=hq * e, ki=kk * ei, ke=kk * ee)


def _chunks(v):
    return v.reshape(TH // HG_BLOCK, HG_BLOCK, v.shape[-1])


def _bmm_nt(a, b):
    return lax.dot_general(a, b, (((2,), (2,)), ((0,), (0,))), preferred_element_type=F32)


def _bmm_nn(a, b):
    return lax.dot_general(a, b, (((2,), (1,)), ((0,), (0,))), preferred_element_type=F32)


def _bmm_tn(a, b):
    return lax.dot_general(a, b, (((1,), (1,)), ((0,), (0,))), preferred_element_type=F32)


def _pair_masks():
    lane = lax.broadcasted_iota(jnp.int32, (TH, LANE), 1)
    kr = lax.broadcasted_iota(jnp.int32, (LANE, LANE), 0)
    kc = lax.broadcasted_iota(jnp.int32, (LANE, LANE), 1)
    return lane < 64, (kr // 64) == (kc // 64)


def _hgrn_fwd(proj, lbl):
    s = proj.shape[0]
    nch = TH // HG_BLOCK

    def body(hq_ref, hf_ref, hi_ref, lbl_ref, o_ref, st_ref, st):
        @pl.when(pl.program_id(1) == 0)
        def _():
            st[...] = jnp.zeros_like(st)

        m0, bd = _pair_masks()
        gt = _hgrn_gates(hq_ref[...], hf_ref[...], _lower_bound(lbl_ref[...]))
        v_b, qd, qd_b = hi_ref[...].astype(BF16), gt["qd"], gt["qd"].astype(BF16)
        ki_b, ke_b = gt["ki"].astype(BF16), gt["ke"].astype(BF16)
        pairs = [slice(u * LANE, (u + 1) * LANE) for u in range(HG_PAIRS)]
        heads = [(lanes, m0 if hh == 0 else jnp.logical_not(m0)) for lanes in pairs for hh in range(2)]
        a_b = [jnp.where(gt["tri"], _mm_nt(jnp.where(mh, qd[:, lanes], 0.0).astype(BF16), ki_b[:, lanes]), 0.0).astype(BF16)
               for lanes, mh in heads]
        intra = [jnp.where(m0, _mm(a_b[2 * u], v_b[:, lanes]), _mm(a_b[2 * u + 1], v_b[:, lanes])) for u, lanes in enumerate(pairs)]
        upd = [_bmm_tn(_chunks(v_b[:, lanes]), _chunks(ke_b[:, lanes])) for lanes in pairs]
        entering = []
        for u, lanes in enumerate(pairs):
            cur, states = st[u], []
            for n in range(nch):
                states.append(cur)
                cur = gt["decay"][n][:, lanes] * cur + jnp.where(bd, upd[u][n], 0.0)
            st[u] = cur
            entering.append(jnp.stack(states))
            st_ref[u] = entering[u]
        for u, lanes in enumerate(pairs):
            o_ref[:, lanes] = intra[u] + _bmm_nt(_chunks(qd_b[:, lanes]), entering[u].astype(BF16)).reshape(TH, LANE)

    wide = HG_PAIRS * LANE
    col = lambda base: pl.BlockSpec((TH, wide), lambda p, i: (i, base // wide + p))
    return pl.pallas_call(
        body, name="hgrn_fwd", grid=(NH // 2 // HG_PAIRS, s // TH),
        in_specs=[col(P_HQ), col(P_HF), col(P_HI), pl.BlockSpec((2, wide), lambda p, i: (0, p))],
        out_specs=[pl.BlockSpec((TH, wide), lambda p, i: (i, p)),
                   pl.BlockSpec((HG_PAIRS, nch, LANE, LANE), lambda p, i: (p, i, 0, 0))],
        out_shape=[jax.ShapeDtypeStruct((s, 512), F32), jax.ShapeDtypeStruct((NH // 2, s // HG_BLOCK, LANE, LANE), F32)],
        scratch_shapes=[pltpu.VMEM((HG_PAIRS, LANE, LANE), F32)],
        compiler_params=_params(("parallel", "arbitrary")),
    )(proj, proj, proj, lbl)


def _group_sum(v):
    low = lax.broadcasted_iota(jnp.int32, (v.shape[0], LANE), 1) < V_DIM
    blocks = []
    for b in range(v.shape[1] // LANE):
        blk = v[:, b * LANE:(b + 1) * LANE]
        s_low = jnp.sum(jnp.where(low, blk, 0.0), axis=-1, keepdims=True)
        s_high = jnp.sum(jnp.where(low, 0.0, blk), axis=-1, keepdims=True)
        blocks.append(jnp.where(low, s_low, s_high))
    return jnp.concatenate(blocks, axis=1)


def _dsilu(z, sg):
    return sg * (1.0 + z * (1.0 - sg))


def _mid(proj, attn, o_raw, x, tgt, g_hg, b_gate, g_post, wa, wb, w_out):
    s = x.shape[0]

    def body(attn_ref, ga_ref, o_ref, gb_ref, mg_ref, x_ref, t_ref, ghg_ref, bg_ref, gp_ref, wa_ref, wb_ref, wo_ref,
             loss_ref, dout_ref, dattn_ref, dga_ref, dor_ref, dgb_ref, dmg_ref, dwo_ref, dwa_ref, dwb_ref, dgp_ref, dbg_ref, dghg_ref):
        @pl.when(pl.program_id(0) == 0)
        def _():
            for rf in (loss_ref, dwo_ref, dwa_ref, dwb_ref, dgp_ref, dbg_ref, dghg_ref):
                rf[...] = jnp.zeros_like(rf)

        attn, za, orw, zb = attn_ref[...], ga_ref[...], o_ref[...], gb_ref[...]
        ghg, gp = ghg_ref[...], gp_ref[...]
        sga, sgb = _sigmoid(za), _sigmoid(zb)
        sa, sb = za * sga, zb * sgb
        ga = attn * sa
        rh = lax.rsqrt(_group_sum(orw * orw) * (1.0 / V_DIM) + EPS)
        on = (orw * rh) * ghg
        gb = on * sb
        ga_b, gb_b = ga.astype(BF16), gb.astype(BF16)
        ya = _mm(ga_b, wa_ref[...])
        yb = _mm(gb_b, wb_ref[...])
        gates = _sigmoid(mg_ref[...] + bg_ref[...])
        g0, g1 = gates[:, :D], gates[:, D:]
        m_b = (g0 * ya + g1 * yb).astype(BF16)
        y = _mm(m_b, wo_ref[...])
        ry = lax.rsqrt(jnp.mean(y * y, axis=-1, keepdims=True) + EPS)
        out = x_ref[...] + (y * ry) * gp
        err = out - t_ref[...]
        loss_ref[...] += 0.5 * jnp.sum(jnp.mean(err * err, axis=-1, keepdims=True), axis=0, keepdims=True)
        dout = err * (1.0 / D)
        dout_ref[...] = dout
        dgp_ref[...] += jnp.sum(dout * (y * ry), axis=0, keepdims=True)
        dgy = dout * gp
        dy = ry * dgy - y * (ry * ry * ry) * jnp.mean(y * dgy, axis=-1, keepdims=True)
        dy_b = dy.astype(BF16)
        dm = _mm_nt(dy_b, wo_ref[...])
        dya_b, dyb_b = (dm * g0).astype(BF16), (dm * g1).astype(BF16)
        dga = _mm_nt(dya_b, wa_ref[...])
        dgb = _mm_nt(dyb_b, wb_ref[...])
        dwo_ref[...] += _mm_tn(m_b, dy_b)
        dwa_ref[...] += _mm_tn(ga_b, dya_b)
        dwb_ref[...] += _mm_tn(gb_b, dyb_b)
        dg0, dg1 = dm * ya, dm * yb
        dmg = jnp.concatenate([dg0 * g0 * (1.0 - g0), dg1 * g1 * (1.0 - g1)], axis=1)
        dmg_ref[...] = dmg.astype(BF16)
        dbg_ref[...] += jnp.sum(dmg, axis=0, keepdims=True)
        dattn_ref[...] = dga * sa
        dga_ref[...] = (dga * attn * _dsilu(za, sga)).astype(BF16)
        dgb_ref[...] = (dgb * on * _dsilu(zb, sgb)).astype(BF16)
        don = dgb * sb
        dghg_ref[...] += jnp.sum(don * (orw * rh), axis=0, keepdims=True)
        dgo = don * ghg
        dor_ref[...] = rh * dgo - orw * (rh * rh * rh) * (_group_sum(orw * dgo) * (1.0 / V_DIM))

    row = lambda w, j=0: pl.BlockSpec((TM_MID, w), lambda i: (i, j))
    full = lambda a: pl.BlockSpec(a.shape, lambda i: (0,) * a.ndim)
    acc = lambda shape: pl.BlockSpec(shape, lambda i: (0, 0))
    sds = jax.ShapeDtypeStruct
    return pl.pallas_call(
        body, name="mid", grid=(s // TM_MID,),
        in_specs=[row(512), row(512, P_GA // 512), row(512), row(512, P_GB // 512), row(2048, P_MERGE // 2048), row(D), row(D),
                  full(g_hg), full(b_gate), full(g_post), full(wa), full(wb), full(w_out)],
        out_specs=[acc((1, 1)), row(D), row(512), row(512), row(512), row(512), row(2048),
                   acc((D, D)), acc((512, D)), acc((512, D)), acc((1, D)), acc((1, 2048)), acc((1, 512))],
        out_shape=[sds((1, 1), F32), sds((s, D), F32), sds((s, 512), F32), sds((s, 512), BF16), sds((s, 512), F32), sds((s, 512), BF16),
                   sds((s, 2048), BF16), sds((D, D), F32), sds((512, D), F32), sds((512, D), F32), sds((1, D), F32),
                   sds((1, 2048), F32), sds((1, 512), F32)],
        compiler_params=_params(("arbitrary",)),
    )(attn, proj, o_raw, proj, proj, x, tgt, g_hg, b_gate, g_post, wa, wb, w_out)


def _attn_bwd(q, k, vv, attn, dattn, lse, token):
    s = q.shape[1]
    nt = s // TQ
    scale = 1.0 / math.sqrt(QK_NOPE + QK_ROPE)

    def body(q_ref, k_ref, v_ref, o_ref, do_ref, lse_ref, token_ref, dq_ref, dk_ref, dv_ref, do_s, delta_s):
        j = pl.program_id(1)

        @pl.when(j == 0)
        def _():
            dq_ref[...] = jnp.zeros_like(dq_ref)
            lane = lax.broadcasted_iota(jnp.int32, (TQ, LANE), 1)

            @pl.loop(0, nt)
            def _(i):
                rows = pl.ds(pl.multiple_of(i * TQ, TQ), TQ)
                do, o = do_ref[rows, :], o_ref[rows, :]
                for hh in range(2):
                    doh = jnp.where((lane < 64) if hh == 0 else (lane >= 64), do, 0.0)
                    do_s[hh, rows, :] = doh.astype(BF16)
                    delta_s[hh, rows, :] = jnp.broadcast_to(jnp.sum(doh * o, axis=-1, keepdims=True), (TQ, LANE))

        kjs, vjs = (k_ref[0], k_ref[1]), (v_ref[0], v_ref[1])

        def tile(hh, start, size, kj, vj, diag):
            rows = pl.ds(pl.multiple_of(start, size), size)
            wide = lambda a: jnp.concatenate([a] * (kj.shape[0] // LANE), axis=1)
            qi, do_b = q_ref[hh, rows, :], do_s[hh, rows, :]
            sc, dp = _mm_nt(qi, kj), _mm_nt(do_b, vj)
            p = jnp.exp2(sc * QK_SCALE2 - wide(lse_ref[hh, rows, :]))
            if diag:
                p = jnp.where(_diag_visible(size), p, 0.0)
            ds_b = (p * (dp - wide(delta_s[hh, rows, :]))).astype(BF16)
            dv, dk = _mm_tn(do_b, p.astype(BF16)), _mm_tn(qi, ds_b)
            dq_ref[hh, rows, :] += _mm(ds_b, kj)
            return dk, dv

        def step(i, carry):
            new = [tile(hh, i * TQ, TQ, kjs[hh], vjs[hh], False) for hh in range(2)]
            return tuple((carry[hh][0] + new[hh][0], carry[hh][1] + new[hh][1]) for hh in range(2))

        def diagonal(hh):
            k0, k1, v0, v1 = kjs[hh][:HQ], kjs[hh][HQ:], vjs[hh][:HQ], vjs[hh][HQ:]
            a = tile(hh, j * TQ, HQ, k0, v0, True)
            b = tile(hh, j * TQ + HQ, HQ, k0, v0, False)
            c = tile(hh, j * TQ + HQ, HQ, k1, v1, True)
            return jnp.concatenate([a[0] + b[0], c[0]], axis=1), jnp.concatenate([a[1] + b[1], c[1]], axis=1)

        carry = lax.fori_loop(j + 1, nt, step, (diagonal(0), diagonal(1)))
        for hh in range(2):
            dk_ref[hh] = carry[hh][0].T * scale
            dv_ref[hh] = carry[hh][1].T

        @pl.when(j == nt - 1)
        def _():
            dq_ref[...] = dq_ref[...] * scale

    whole = pl.BlockSpec((2, s, LANE), lambda p, j: (p, 0, 0))
    tile_spec = pl.BlockSpec((2, TQ, LANE), lambda p, j: (p, j, 0))
    cols = pl.BlockSpec((s, LANE), lambda p, j: (0, p))
    hs = jax.ShapeDtypeStruct((NH, s, LANE), F32)
    return pl.pallas_call(
        body, name="attn_bwd", grid=(NH // 2, nt),
        in_specs=[whole, tile_spec, tile_spec, cols, cols, whole, pl.BlockSpec((8, LANE), lambda p, j: (0, 0))],
        out_specs=[whole, tile_spec, tile_spec],
        out_shape=[hs, hs, hs],
        scratch_shapes=[pltpu.VMEM((2, s, LANE), BF16), pltpu.VMEM((2, s, LANE), F32)],
        compiler_params=_params(("parallel", "arbitrary")),
    )(q, k, vv, attn, dattn, lse, token)


def _hgrn_bwd(proj, lbl, states, do_raw):
    s = proj.shape[0]
    nt = s // TH
    nch = TH // HG_BLOCK

    def body(hq_ref, hf_ref, hi_ref, lbl_ref, st_ref, do_ref, dh_ref, dlbl_ref, dst, dlb):
        step = pl.program_id(1)

        @pl.when(step == 0)
        def _():
            dst[...] = jnp.zeros_like(dst)
            dlb[...] = jnp.zeros_like(dlb)

        m0, bd = _pair_masks()
        lb = _lower_bound(lbl_ref[...])
        gt = _hgrn_gates(hq_ref[...], hf_ref[...], lb)
        do = do_ref[...]
        qd, ki, ke = gt["qd"], gt["ki"], gt["ke"]
        v_b, do_b = hi_ref[...].astype(BF16), do.astype(BF16)
        qd_b, ki_b, ke_b = qd.astype(BF16), ki.astype(BF16), ke.astype(BF16)
        pairs = [slice(u * LANE, (u + 1) * LANE) for u in range(HG_PAIRS)]
        heads = [(lanes, m0 if hh == 0 else jnp.logical_not(m0)) for lanes in pairs for hh in range(2)]
        a_b = [jnp.where(gt["tri"], _mm_nt(jnp.where(mh, qd[:, lanes], 0.0).astype(BF16), ki_b[:, lanes]), 0.0).astype(BF16)
               for lanes, mh in heads]
        doh_b = [jnp.where(mh, do[:, lanes], 0.0).astype(BF16) for lanes, mh in heads]
        da_b = [jnp.where(gt["tri"], _mm_nt(d, v_b[:, lanes]), 0.0).astype(BF16) for d, (lanes, _) in zip(doh_b, heads)]
        dv_p, dqd_p, dki_p = [], [], []
        for u, lanes in enumerate(pairs):
            e, o = 2 * u, 2 * u + 1
            dv_p.append(_mm_tn(a_b[e], doh_b[e]) + _mm_tn(a_b[o], doh_b[o]))
            dqd_p.append(jnp.where(m0, _mm(da_b[e], ki_b[:, lanes]), _mm(da_b[o], ki_b[:, lanes])))
            dki_p.append(jnp.where(m0, _mm_tn(da_b[e], qd_b[:, lanes]), _mm_tn(da_b[o], qd_b[:, lanes])))
        fed = [_bmm_tn(_chunks(do_b[:, lanes]), _chunks(qd_b[:, lanes])) for lanes in pairs]
        leaving = []
        for u, lanes in enumerate(pairs):
            ds, left = dst[u], [None] * nch
            for n in reversed(range(nch)):
                left[n] = ds
                ds = gt["decay"][n][:, lanes] * ds + jnp.where(bd, fed[u][n], 0.0)
            dst[u] = ds
            leaving.append(jnp.stack(left))
        dke_p, dlast_p = [], []
        for u, lanes in enumerate(pairs):
            entering, leaving_b = st_ref[u], leaving[u].astype(BF16)
            dke3 = _bmm_nn(_chunks(v_b[:, lanes]), leaving_b)
            dv_p[u] = dv_p[u] + _bmm_nt(_chunks(ke_b[:, lanes]), leaving_b).reshape(TH, LANE)
            dqd_p[u] = dqd_p[u] + _bmm_nn(_chunks(do_b[:, lanes]), entering.astype(BF16)).reshape(TH, LANE)
            dke_p.append(dke3.reshape(TH, LANE))
            dlast_p.append(jnp.sum(dke3 * _chunks(ke[:, lanes]), axis=1, keepdims=True)
                           + jnp.sum(leaving[u] * entering, axis=1, keepdims=True) * gt["decay"][:, :, lanes])
        cat = lambda parts: jnp.concatenate(parts, axis=-1)
        dv, dqd, dki, dke, dlast = cat(dv_p), cat(dqd_p), cat(dki_p), cat(dke_p), cat(dlast_p)
        dk = dki * gt["ei"] + dke * gt["ee"]
        dcum = dqd * qd - dki * ki - dke * ke
        dg = _chunk_cumsum(dcum, reverse=True) + jnp.broadcast_to(dlast, (nch, HG_BLOCK, dlast.shape[-1])).reshape(dcum.shape)
        sig = gt["sig"]
        df = dg / gt["f"] - dk
        dlb[...] += jnp.sum(df * (1.0 - sig), axis=0, keepdims=True)
        dh_ref[0] = (dqd * gt["e"]).astype(BF16)
        dh_ref[1] = ((df * (1.0 - lb)) * sig * (1.0 - sig)).astype(BF16)
        dh_ref[2] = dv.astype(BF16)

        @pl.when(step == nt - 1)
        def _():
            lb = _lower_bound(lbl_ref[...])
            da0 = dlb[...] * lb * (1.0 - lb)
            dlbl_ref[...] = jnp.concatenate([da0, -da0], axis=0)

    wide = HG_PAIRS * LANE
    col = lambda base: pl.BlockSpec((TH, wide), lambda p, i: (nt - 1 - i, base // wide + p))
    tile = pl.BlockSpec((TH, wide), lambda p, i: (nt - 1 - i, p))
    sds = jax.ShapeDtypeStruct
    return pl.pallas_call(
        body, name="hgrn_bwd", grid=(NH // 2 // HG_PAIRS, nt),
        in_specs=[col(P_HQ), col(P_HF), col(P_HI), pl.BlockSpec((2, wide), lambda p, i: (0, p)),
                  pl.BlockSpec((HG_PAIRS, nch, LANE, LANE), lambda p, i: (p, nt - 1 - i, 0, 0)), tile],
        out_specs=[pl.BlockSpec((3, TH, wide), lambda p, i: (0, nt - 1 - i, p)), pl.BlockSpec((2, wide), lambda p, i: (0, p))],
        out_shape=[sds((3, s, 512), BF16), sds((2, 512), F32)],
        scratch_shapes=[pltpu.VMEM((HG_PAIRS, LANE, LANE), F32), pltpu.VMEM((1, wide), F32)],
        compiler_params=_params(("parallel", "arbitrary")),
    )(proj, proj, proj, lbl, states, do_raw)


def _norm_rows_bwd(v, r, g, dn):
    dgv = dn * g
    return r * dgv - v * (r * r * r) * jnp.mean(v * dgv, axis=-1, keepdims=True)


def _qkv_bwd(proj, dq, dk, dvv, g_q, g_kv, w_uq_p, w_k_p, w_v_p, rc, rs1, rs2):
    s = proj.shape[0]

    def body(cq_ref, ckv_ref, dq_ref, dk_ref, dv_ref, gq_ref, gkv_ref, wq_ref, wk_ref, wv_ref, c_ref, s1_ref, s2_ref,
             dcq_ref, dckv_ref, dkpe_ref, dwq_ref, dwk_ref, dwv_ref, dgq_ref, dgkv_ref):
        @pl.when(pl.program_id(0) == 0)
        def _():
            for rf in (dwq_ref, dwk_ref, dwv_ref, dgq_ref, dgkv_ref):
                rf[...] = jnp.zeros_like(rf)

        c, s1, s2 = c_ref[...], s1_ref[...], s2_ref[...]
        cq, ckv = cq_ref[...], ckv_ref[...]
        gq, gkv = gq_ref[...], gkv_ref[...]
        cqn, rq = _norm_rows(cq, gq)
        ckvn, rkv = _norm_rows(ckv, gkv)
        cqn_b, ckvn_b = cqn.astype(BF16), ckvn.astype(BF16)
        dqf = jnp.concatenate([_rope_t(dq_ref[h], c, s1, s2) for h in range(NH)], axis=1).astype(BF16)
        dkf = jnp.concatenate([dk_ref[h] for h in range(NH)], axis=1).astype(BF16)
        dvf = jnp.concatenate([dv_ref[h] for h in range(NH)], axis=1).astype(BF16)
        dkpe = dk_ref[0]
        for h in range(1, NH):
            dkpe = dkpe + dk_ref[h]
        lane = lax.broadcasted_iota(jnp.int32, (TM, LANE), 1)
        dkpe = jnp.where((lane >= QK_NOPE) & (lane < QK_NOPE + QK_ROPE), dkpe, 0.0)
        dkpe_ref[...] = _rope_t(dkpe, c, s1, s2).astype(BF16)
        dcqn = _mm_nt(dqf, wq_ref[...])
        dckvn = _mm_nt(dkf, wk_ref[...]) + _mm_nt(dvf, wv_ref[...])
        dwq_ref[...] += _mm_tn(cqn_b, dqf)
        dwk_ref[...] += _mm_tn(ckvn_b, dkf)
        dwv_ref[...] += _mm_tn(ckvn_b, dvf)
        dgq_ref[...] += jnp.sum(dcqn * (cq * rq), axis=0, keepdims=True)
        dgkv_ref[...] += jnp.sum(dckvn * (ckv * rkv), axis=0, keepdims=True)
        dcq_ref[...] = _norm_rows_bwd(cq, rq, gq, dcqn).astype(BF16)
        dckv_ref[...] = _norm_rows_bwd(ckv, rkv, gkv, dckvn).astype(BF16)

    row = lambda w, j=0: pl.BlockSpec((TM, w), lambda i: (i, j))
    full = lambda a: pl.BlockSpec(a.shape, lambda i: (0,) * a.ndim)
    acc = lambda shape: pl.BlockSpec(shape, lambda i: (0, 0))
    heads = pl.BlockSpec((NH, TM, LANE), lambda i: (0, i, 0))
    sds = jax.ShapeDtypeStruct
    return pl.pallas_call(
        body, name="qkv_bwd", grid=(s // TM,),
        in_specs=[row(Q_LORA, P_CQ // Q_LORA), row(KV_LORA, P_CKV // KV_LORA), heads, heads, heads,
                  full(g_q), full(g_kv), full(w_uq_p), full(w_k_p), full(w_v_p), row(LANE), row(LANE), row(LANE)],
        out_specs=[row(Q_LORA), row(KV_LORA), row(LANE), acc((Q_LORA, D)), acc((KV_LORA, D)), acc((KV_LORA, D)),
                   acc((1, Q_LORA)), acc((1, KV_LORA))],
        out_shape=[sds((s, Q_LORA), BF16), sds((s, KV_LORA), BF16), sds((s, LANE), BF16), sds((Q_LORA, D), F32),
                   sds((KV_LORA, D), F32), sds((KV_LORA, D), F32), sds((1, Q_LORA), F32), sds((1, KV_LORA), F32)],
        compiler_params=_params(("arbitrary",)),
    )(proj, proj, dq, dk, dvv, g_q, g_kv, w_uq_p, w_k_p, w_v_p, rc, rs1, rs2)


def _front_bwd(x, dout, dmg, dga, dh3, dgb, dcq, dckv, dkpe, g_pre, w_in_pt, token):
    s = x.shape[0]

    def body(x_ref, do_ref, dmg_ref, dga_ref, dh3_ref, dgb_ref, dcq_ref, dckv_ref, dkpe_ref, g_ref, w_ref, token_ref, gx_ref, dg_ref):
        @pl.when(pl.program_id(0) == 0)
        def _():
            dg_ref[...] = jnp.zeros_like(dg_ref)

        xv, g = x_ref[...], g_ref[...]
        _, r = _norm_rows(xv, g)
        pieces = ((dmg_ref[...], P_MERGE), (dga_ref[...], P_GA), (dh3_ref[0], P_HQ), (dh3_ref[1], P_HF), (dh3_ref[2], P_HI),
                  (dgb_ref[...], P_GB), (dcq_ref[...], P_CQ), (dckv_ref[...], P_CKV), (dkpe_ref[...], P_KPE))
        dh = jnp.zeros((TM, D), F32)
        for piece, off in pieces:
            dh = dh + _mm(piece, w_ref[off:off + piece.shape[1], :])
        dg_ref[...] += jnp.sum(dh * (xv * r), axis=0, keepdims=True)
        gx_ref[...] = do_ref[...] + _norm_rows_bwd(xv, r, g, dh)

    row = lambda w: pl.BlockSpec((TM, w), lambda i: (i, 0))
    full = lambda a: pl.BlockSpec(a.shape, lambda i: (0,) * a.ndim)
    sds = jax.ShapeDtypeStruct
    return pl.pallas_call(
        body, name="front_bwd", grid=(s // TM,),
        in_specs=[row(D), row(D), row(2048), row(512), pl.BlockSpec((3, TM, 512), lambda i: (0, i, 0)), row(512), row(Q_LORA),
                  row(KV_LORA), row(LANE), full(g_pre), full(w_in_pt), pl.BlockSpec(memory_space=pl.ANY)],
        out_specs=[row(D), pl.BlockSpec((1, D), lambda i: (0, 0))],
        out_shape=[sds((s, D), F32), sds((1, D), F32)],
        compiler_params=_params(("arbitrary",)),
    )(x, dout, dmg, dga, dh3, dgb, dcq, dckv, dkpe, g_pre, w_in_pt, token)


TK_GRAD = 1024


def _win_grad(h, pieces, name):
    s = h.shape[0]
    n = len(pieces)

    def body(h_ref, *refs):
        @pl.when(pl.program_id(0) == 0)
        def _():
            for o_ref in refs[n:]:
                o_ref[...] = jnp.zeros_like(o_ref)

        hv = h_ref[...]
        for d_ref, o_ref in zip(refs[:n], refs[n:]):
            if len(d_ref.shape) == 3:
                for k in range(d_ref.shape[0]):
                    o_ref[k] += _mm_tn(d_ref[k], hv)
            else:
                o_ref[...] += _mm_tn(d_ref[...], hv)

    def in_spec(p):
        if p.ndim == 3:
            return pl.BlockSpec((p.shape[0], TK_GRAD, p.shape[2]), lambda kk: (0, kk, 0))
        return pl.BlockSpec((TK_GRAD, p.shape[1]), lambda kk: (kk, 0))

    out_shapes = [(p.shape[0], p.shape[2], D) if p.ndim == 3 else (p.shape[1], D) for p in pieces]
    return pl.pallas_call(
        body, name=name, grid=(s // TK_GRAD,),
        in_specs=[pl.BlockSpec((TK_GRAD, D), lambda kk: (kk, 0))] + [in_spec(p) for p in pieces],
        out_specs=[pl.BlockSpec(sh, lambda kk, nd=len(sh): (0,) * nd) for sh in out_shapes],
        out_shape=[jax.ShapeDtypeStruct(sh, F32) for sh in out_shapes],
        compiler_params=_params(("arbitrary",)),
    )(h, *pieces)


def _pad_win_t(w_in_t):
    z = lambda n: jnp.zeros((n, w_in_t.shape[1]), w_in_t.dtype)
    sl = lambda o, n: w_in_t[o:o + n]
    return jnp.concatenate([sl(O_MERGE, 2048), sl(O_GA, 512), sl(O_HQ, 512), sl(O_HF, 512), sl(O_HI, 512), sl(O_GB, 512),
                            sl(O_CQ, Q_LORA), sl(O_CKV, KV_LORA), z(64), sl(O_KPE, QK_ROPE), z(32)], axis=0)


def _pad_wuq(w_uq):
    rows = w_uq.shape[0]
    w = w_uq.reshape(rows, NH, QK_NOPE + QK_ROPE)
    return jnp.pad(w, ((0, 0), (0, 0), (0, LANE - QK_NOPE - QK_ROPE))).reshape(rows, NH * LANE)


def _unpad_wuq(g):
    return g.reshape(Q_LORA, NH, LANE)[:, :, :QK_NOPE + QK_ROPE].reshape(Q_LORA, NH * (QK_NOPE + QK_ROPE))


def _pad_wukv(w_ukv):
    heads = w_ukv.shape[1] // (QK_NOPE + V_DIM)
    w = w_ukv.reshape(KV_LORA, heads, QK_NOPE + V_DIM)
    w_k = jnp.pad(w[:, :, :QK_NOPE], ((0, 0), (0, 0), (0, LANE - QK_NOPE))).reshape(KV_LORA, heads * LANE)
    wv = w[:, :, QK_NOPE:].reshape(KV_LORA, heads // 2, 2, 1, V_DIM)
    eye = jnp.eye(2, dtype=w.dtype).reshape(1, 1, 2, 2, 1)
    return w_k, (wv * eye).reshape(KV_LORA, heads * LANE)


def _unpad_wukv(gk, gv):
    gk = gk.reshape(KV_LORA, NH, LANE)[:, :, :QK_NOPE]
    gv = gv.reshape(KV_LORA, NH // 2, 2, 2, V_DIM)
    gv = jnp.stack([gv[:, :, 0, 0], gv[:, :, 1, 1]], axis=2).reshape(KV_LORA, NH, V_DIM)
    return jnp.concatenate([gk, gv], axis=-1).reshape(KV_LORA, NH * (QK_NOPE + V_DIM))


def _local_step(x, tgt, g_pre, w_in_t, b_gate, g_q, g_kv, lb_logits, g_hgrn, g_post, weights, exchange=None):
    s = x.shape[0]
    w_in_p = _pad_win_t(w_in_t)
    rc, rs1, rs2 = _rope_tables(s)
    g_hg = jnp.tile(g_hgrn, (1, NH))

    proj, h = _front_fwd(x, g_pre, w_in_p, weights.tokens)
    w_uq_p, w_k_p, w_v_p = weights.qkv(h)
    q, k, vv = _qkv_fwd(proj, g_q, g_kv, w_uq_p, w_k_p, w_v_p, rc, rs1, rs2)
    attn, lse = _attn_fwd(q, k, vv)
    o_raw, states = _hgrn_fwd(proj, lb_logits)
    wa, wb, w_out = weights.mid(o_raw)
    (loss, dout, dattn, dga, dor, dgb, dmg, d_wout, d_wa, d_wb, d_gpost, d_bgate, d_ghg) = _mid(
        proj, attn, o_raw, x, tgt, g_hg, b_gate, g_post, wa, wb, w_out)
    w_mg, w_ga, w_gb = _win_grad(h, [dmg, dga, dgb], "win_grad_mid")
    dh3, d_lbl = _hgrn_bwd(proj, lb_logits, states, dor)
    (w_h3,) = _win_grad(h, [dh3], "win_grad_hgrn")
    d_win_rest = jnp.concatenate([w_ga, w_h3[0], w_h3[1], w_h3[2], w_gb, w_mg], axis=0)
    early = dict(w_in_rest=d_win_rest, w_branch_a=d_wa, w_branch_b=d_wb, w_out=d_wout)
    token = exchange.start_early(early) if exchange else jnp.zeros((8, LANE), F32)
    dq, dk, dvv = _attn_bwd(q, k, vv, attn, dattn, lse, token)
    dcq, dckv, dkpe, d_wuq_p, d_wk_p, d_wv_p, d_gq, d_gkv = _qkv_bwd(proj, dq, dk, dvv, g_q, g_kv, w_uq_p, w_k_p, w_v_p, rc, rs1, rs2)
    w_cq, w_ckv, w_kpe = _win_grad(h, [dcq, dckv, dkpe], "win_grad_qkv")
    d_win_qkv = jnp.concatenate([w_cq, w_ckv, w_kpe[64:64 + QK_ROPE]], axis=0)
    late = dict(w_in_qkv=d_win_qkv, w_uq=_unpad_wuq(d_wuq_p), w_ukv=_unpad_wukv(d_wk_p, d_wv_p))
    token = exchange.start_late(late) if exchange else jnp.zeros((8, LANE), F32)
    grad_x, d_gpre = _front_bwd(x, dout, dmg, dga, dh3, dgb, dcq, dckv, dkpe, g_pre, w_in_p, token)
    vec_grads = dict(g_pre=d_gpre, b_gate=d_bgate, g_q=d_gq, g_kv=d_gkv, lb_logits=d_lbl, g_hgrn=d_ghg, g_post=d_gpost)
    return loss, grad_x, dict(early, **late), vec_grads


SHARD_SHAPES = (("w_in", (1416, 1024)), ("w_uq", (192, 768)), ("w_ukv", (256, 256)), ("w_branch_a", (512, 256)),
                ("w_branch_b", (512, 256)), ("w_out", (256, 1024)))
BIG = tuple(n for n, _ in SHARD_SHAPES)
ROW_SHARDED = ("w_in", "w_uq", "w_out")
N_CHIPS = 4
QKV_ROWS = Q_LORA + KV_LORA + QK_ROPE
W_IN_FORWARD_CUT = 704


def _to_block(name, a):
    return a[0].T if name == "w_in" else a[0]


def _from_block(name, a):
    return a.T[None] if name == "w_in" else a[None]
VEC_ROWS = (("g_pre", 0, 1024), ("b_gate", 1, 2048), ("g_q", 2, 768), ("g_kv", 3, 256), ("g_hgrn", 6, 64), ("g_post", 7, 1024))
VEC_LB_ROW = 4
VEC_SHAPE = (8, 2048)


def _split_by_chip(name, g):
    a, b = dict(SHARD_SHAPES)[name]
    return g.reshape(N_CHIPS, a, b) if name in ROW_SHARDED else g.reshape(a, N_CHIPS, b).transpose(1, 0, 2)


def _join_chips(name, w):
    a, b = dict(SHARD_SHAPES)[name]
    return w.reshape(N_CHIPS * a, b) if name in ROW_SHARDED else w.transpose(1, 0, 2).reshape(a, N_CHIPS * b)


MESH = pl.DeviceIdType.MESH
HBM = pl.BlockSpec(memory_space=pltpu.HBM)


def _mesh_place():
    x, y, c = lax.axis_index("x"), lax.axis_index("y"), lax.axis_index("c")
    return x, y, c, 2 * x + y, [(1 - x, y), (x, 1 - y), (1 - x, 1 - y)]


def _remote(src, dst, send_sems, recv_sems, k, to):
    return pltpu.make_async_remote_copy(src_ref=src, dst_ref=dst, send_sem=send_sems.at[k], recv_sem=recv_sems.at[k],
                                        device_id=to, device_id_type=MESH)


def _gather_w_in(shard):
    a, b = shard.shape
    cut = W_IN_FORWARD_CUT

    def body(src, out, ici_send, ici_recv, d2d_send, d2d_recv, local_sem):
        x, y, c = lax.axis_index("x"), lax.axis_index("y"), lax.axis_index("c")
        me, xn, yn, dg = 2 * x + y, 2 * (1 - x) + y, 2 * x + (1 - y), 2 * (1 - x) + (1 - y)
        to_x, to_y, sibling = (1 - x, y, c), (x, 1 - y, c), (x, y, 1 - c)
        first, rest = pl.ds(0, cut), pl.ds(cut, a - cut)

        whole = lambda ref, which: ref.at[:, pl.ds(pl.multiple_of(which * (b // 2), b // 2), b // 2)]
        own = pltpu.make_async_copy(src, out.at[me], local_sem)
        own.start()
        sends = [_remote(whole(src, c), whole(out.at[me], c), ici_send, ici_recv, 0, to_x),
                 _remote(whole(src, c), whole(out.at[me], c), ici_send, ici_recv, 1, to_y)]
        for cp in sends:
            cp.start()

        def landed(slot, rows, k, d2d_k, src_dev):
            piece = whole(out.at[slot], c) if rows is None else out.at[slot].at[rows, pl.ds(pl.multiple_of(c * (b // 2), b // 2), b // 2)]
            _remote(piece, piece, ici_send, ici_recv, k, src_dev).wait_recv()
            cp = _remote(piece, piece, d2d_send, d2d_recv, d2d_k, sibling)
            cp.start()
            sends.append(cp)
            return piece

        def pass_on(slot, rows, k, to):
            piece = out.at[slot].at[rows, pl.ds(pl.multiple_of(c * (b // 2), b // 2), b // 2)]
            cp = _remote(piece, piece, ici_send, ici_recv, k, to)
            cp.start()
            sends.append(cp)

        landed(xn, None, 0, 0, to_x)
        pass_on(xn, first, 2, to_y)
        landed(yn, None, 1, 1, to_y)
        pass_on(yn, rest, 3, to_x)
        landed(dg, first, 2, 2, to_y)
        landed(dg, rest, 3, 3, to_x)
        other = pl.ds(pl.multiple_of((1 - c) * (b // 2), b // 2), b // 2)
        for d2d_k, (slot, rows) in enumerate(((xn, None), (yn, None), (dg, first), (dg, rest))):
            piece = out.at[slot].at[:, other] if rows is None else out.at[slot].at[rows, other]
            _remote(piece, piece, d2d_send, d2d_recv, d2d_k, sibling).wait_recv()
        for cp in sends:
            cp.wait_send()
        own.wait()

    sems = pltpu.SemaphoreType.DMA((4,))
    return pl.pallas_call(
        body, name="gather_w_in", in_specs=[HBM], out_specs=HBM,
        out_shape=jax.ShapeDtypeStruct((N_CHIPS, a, b), shard.dtype),
        scratch_shapes=[sems, sems, sems, sems, pltpu.SemaphoreType.DMA],
        compiler_params=pltpu.CompilerParams(has_side_effects=True),
    )(shard)


def _sibling_exchange(srcs, name, after=None):
    n = len(srcs)
    extra = [] if after is None else [after]

    def body(*refs):
        src_refs, outs = refs[:n], refs[n + len(extra):2 * n + len(extra)]
        send_sems, recv_sems = refs[2 * n + len(extra):]
        sibling = (lax.axis_index("x"), lax.axis_index("y"), 1 - lax.axis_index("c"))
        copies = [_remote(src_refs[k], outs[k], send_sems, recv_sems, k, sibling) for k in range(n)]
        for cp in copies:
            cp.start()
        for cp in copies:
            cp.wait()

    sems = pltpu.SemaphoreType.DMA((n,))
    return pl.pallas_call(
        body, name=name, in_specs=[HBM] * n + [pl.BlockSpec(memory_space=pl.ANY)] * len(extra), out_specs=[HBM] * n,
        out_shape=[jax.ShapeDtypeStruct(s.shape, s.dtype) for s in srcs],
        scratch_shapes=[sems, sems],
        compiler_params=pltpu.CompilerParams(has_side_effects=True),
    )(*srcs, *extra)


SEM = pl.BlockSpec(memory_space=pltpu.SEMAPHORE)
DATAFLOW = pltpu.SideEffectType.DATAFLOW_SIDE_EFFECTING


def _exchange_copies(srcs, to_first, src_refs, land_refs, send_sems, recv_sems):
    x, y, c, me, chips = _mesh_place()
    n = len(srcs)
    sends, recvs = [], []
    for k in range(n):
        if k in to_first:
            base = 3 * n + 4 * to_first.index(k)
            sends.append((me != 0, pltpu.make_async_remote_copy(
                src_ref=src_refs[k], dst_ref=land_refs[k].at[me], send_sem=send_sems.at[base], recv_sem=recv_sems.at[base + me],
                device_id=(0, 0, c), device_id_type=MESH)))
            for s in range(1, N_CHIPS):
                recvs.append((me == 0, pltpu.make_async_remote_copy(
                    src_ref=src_refs[k], dst_ref=land_refs[k].at[s], send_sem=send_sems.at[base], recv_sem=recv_sems.at[base + s],
                    device_id=(s // 2, s % 2, c), device_id_type=MESH)))
        else:
            slab = (lambda t, k=k: src_refs[k]) if srcs[k].ndim == 2 else (lambda t, k=k: src_refs[k].at[t])
            for j, (px, py) in enumerate(chips):
                sends.append((None, _remote(slab(2 * px + py), land_refs[k].at[me], send_sems, recv_sems, 3 * k + j, (px, py, c))))
                recvs.append((None, _remote(slab(me), land_refs[k].at[2 * px + py], send_sems, recv_sems, 3 * k + j, (px, py, c))))
    return sends, recvs


def _when(pred, fn):
    if pred is None:
        fn()
    else:
        pl.when(pred)(fn)


def _exchange_start(srcs, to_first, name, after=None):
    n = len(srcs)
    n_sems = 3 * n + 4 * len(to_first)
    lands = [lax.empty((N_CHIPS,) + s.shape[-2:], s.dtype) for s in srcs]
    extra = [] if after is None else [after]

    def body(*refs):
        src_refs, land_refs = refs[:n], refs[n:2 * n]
        send_sems, recv_sems, token = refs[2 * n + len(extra)], refs[2 * n + len(extra) + 1], refs[-1]
        sends, _ = _exchange_copies(srcs, to_first, src_refs, land_refs, send_sems, recv_sems)
        for pred, cp in sends:
            _when(pred, cp.start)
        token[...] = jnp.zeros_like(token)

    hbm = lambda a: pltpu.HBM(a.shape, a.dtype)
    res = pl.pallas_call(
        body, name=name,
        out_shape=[pltpu.SemaphoreType.DMA((n_sems,)), pltpu.SemaphoreType.DMA((n_sems,))] + [hbm(a) for a in srcs + lands]
        + [jax.ShapeDtypeStruct((8, LANE), F32)],
        in_specs=[HBM] * (2 * n) + [pl.BlockSpec(memory_space=pl.ANY)] * len(extra),
        out_specs=[SEM, SEM] + [HBM] * (2 * n) + [pl.BlockSpec(memory_space=pltpu.VMEM)],
        input_output_aliases={i: 2 + i for i in range(2 * n)},
        compiler_params=pltpu.CompilerParams(has_side_effects=DATAFLOW),
    )(*[pltpu.with_memory_space_constraint(a, pltpu.HBM) for a in srcs + lands], *extra)
    return res[:-1], res[-1]


def _exchange_wait(srcs, to_first, started, after, name):
    n = len(srcs)
    send_sems, recv_sems, thru = started[0], started[1], started[2:]

    def body(*refs):
        src_refs, land_refs, send_ref, recv_ref = refs[:n], refs[n:2 * n], refs[2 * n], refs[2 * n + 1]
        sends, recvs = _exchange_copies(srcs, to_first, src_refs, land_refs, send_ref, recv_ref)
        for pred, cp in sends:
            _when(pred, cp.wait_send)
        for pred, cp in recvs:
            _when(pred, cp.wait_recv)

    res = pl.pallas_call(
        body, name=name, out_shape=[pltpu.HBM(a.shape, a.dtype) for a in thru],
        in_specs=[HBM] * (2 * n) + [SEM, SEM, pl.BlockSpec(memory_space=pl.ANY)], out_specs=[HBM] * (2 * n),
        input_output_aliases={i: i for i in range(2 * n)},
        compiler_params=pltpu.CompilerParams(has_side_effects=DATAFLOW),
    )(*thru, send_sems, recv_sems, after)
    return res[n:]


ROW_TILE = 256
COL_TILE = 256


def _block_tiling(a, b):
    if a <= ROW_TILE or a % ROW_TILE == 0:
        ta = min(a, ROW_TILE)
        return a // ta, (ta, b), lambda i: (i, 0)
    return b // COL_TILE, (a, COL_TILE), lambda i: (0, i)


def _sum_landed(land, own, name, first_land=None, first_own=None):
    _, a, b = land.shape
    steps, tile, at = _block_tiling(a, b)
    extra = first_land is not None

    def body(*refs):
        p_ref, own_ref, o_ref = refs[0], refs[1], refs[-1]
        me = 2 * lax.axis_index("x") + lax.axis_index("y")
        own = own_ref[...].astype(F32)
        slot = lambda t: jnp.where(me == t, own, p_ref[t].astype(F32))
        o_ref[...] = ((slot(0) + slot(1)) + slot(2)) + slot(3)
        if extra:
            fp_ref, fo_ref = refs[2], refs[3]
            r = fo_ref.shape[0]

            @pl.when(me == 0)
            def _():
                f = lambda t: fp_ref[t].astype(F32)
                rows = pl.ds(pl.multiple_of(lax.axis_index("c") * r, 8), r)
                o_ref[rows, :] += ((fo_ref[...].astype(F32) + f(1)) + f(2)) + f(3)

    in_specs = [pl.BlockSpec((N_CHIPS,) + tile, lambda i: (0,) + at(i)), pl.BlockSpec(tile, at)]
    args = [land, own]
    if extra:
        r = first_own.shape[0]
        assert tile[0] == a, "the extra rows need whole columns in a step"
        in_specs += [pl.BlockSpec((N_CHIPS, r, tile[1]), lambda i: (0,) + at(i)), pl.BlockSpec((r, tile[1]), at)]
        args += [first_land, first_own]
    return pl.pallas_call(
        body, name=name, grid=(steps,), in_specs=in_specs, out_specs=pl.BlockSpec(tile, at),
        out_shape=jax.ShapeDtypeStruct((a, b), F32), compiler_params=_params(("parallel",)),
    )(*args)


def _sum_landed_small(lands, owns, name):
    n = len(lands)

    def body(*refs):
        me = 2 * lax.axis_index("x") + lax.axis_index("y")
        for p_ref, own_ref, o_ref in zip(refs[:n], refs[n:2 * n], refs[2 * n:]):
            own = own_ref[...].astype(F32)
            slot = lambda t: jnp.where(me == t, own, p_ref[t].astype(F32))
            o_ref[...] = ((slot(0) + slot(1)) + slot(2)) + slot(3)

    return pl.pallas_call(body, name=name, out_shape=[jax.ShapeDtypeStruct(o.shape, F32) for o in owns],
                          compiler_params=_params(()))(*lands, *owns)


def _adamw_small(mine, theirs, states, name):
    n = len(mine)

    def body(*refs):
        ins, outs = refs[:5 * n], refs[5 * n:]
        for k in range(n):
            a_ref, b_ref, w_ref, m_ref, v_ref = ins[5 * k:5 * k + 5]
            g = a_ref[...] + b_ref[...]
            outs[4 * k][...] = g
            outs[4 * k + 1][...], outs[4 * k + 2][...], outs[4 * k + 3][...] = _adamw_math(g, w_ref[...], m_ref[...], v_ref[...])

    args = [t for k in range(n) for t in (mine[k], theirs[k], *states[k])]
    res = pl.pallas_call(body, name=name, out_shape=[jax.ShapeDtypeStruct(mine[k].shape, F32) for k in range(n) for _ in range(4)],
                         compiler_params=_params(()))(*args)
    return [tuple(res[4 * k:4 * k + 4]) for k in range(n)]


def _add_cast(a, b, name):
    def body(a_ref, b_ref, o_ref):
        o_ref[...] = (a_ref[...] + b_ref[...]).astype(BF16)

    return pl.pallas_call(body, name=name, out_shape=jax.ShapeDtypeStruct(a.shape, BF16),
                          compiler_params=_params(()))(a, b)


class _LaterWeights:
    MID = ("w_branch_a", "w_branch_b", "w_out")

    def __init__(self, blocks, after):
        self.qkv_blocks = [_pad_wuq(blocks["w_uq"]), *_pad_wukv(blocks["w_ukv"])]
        self.mid_blocks = [blocks[n] for n in self.MID]
        self.qkv_started, t1 = _exchange_start(self.qkv_blocks, (), "weights_qkv_start", after)
        self.mid_started, t2 = _exchange_start(self.mid_blocks, (), "weights_mid_start", after)
        self.tokens = [t1, t2]

    @staticmethod
    def _whole(blocks, rows_sharded, started, after, name):
        landed = _exchange_wait(blocks, (), started, after, name)
        me = 2 * lax.axis_index("x") + lax.axis_index("y")
        out = []
        for block, land, by_rows in zip(blocks, landed, rows_sharded):
            w = lax.dynamic_update_index_in_dim(land, block, me, 0)
            a, b = block.shape
            out.append(w.reshape(N_CHIPS * a, b) if by_rows else w.transpose(1, 0, 2).reshape(a, N_CHIPS * b))
        return out

    def qkv(self, after):
        return self._whole(self.qkv_blocks, (True, False, False), self.qkv_started, after, "weights_qkv_wait")

    def mid(self, after):
        return self._whole(self.mid_blocks, (False, False, True), self.mid_started, after, "weights_mid_wait")


class _GradExchange:
    EARLY = ("w_in", "w_branch_a", "w_branch_b", "w_out")
    LATE = ("w_uq", "w_ukv")

    def __init__(self, state):
        self.state = state
        self.outs = {}

    @staticmethod
    def _own(slabs):
        return lax.dynamic_index_in_dim(slabs, 2 * lax.axis_index("x") + lax.axis_index("y"), axis=0, keepdims=False)

    def start_early(self, g):
        full = jnp.concatenate([jnp.zeros((QKV_ROWS, D), F32), g["w_in_rest"]], axis=0)
        g = dict(g, w_in=full)
        self.early = [_split_by_chip(n, g[n]).astype(BF16) for n in self.EARLY]
        self.early_started, token = _exchange_start(self.early, (), "grads_early_start")
        return token

    def start_late(self, g):
        self.early_landed = _exchange_wait(self.early, (), self.early_started, g["w_uq"], "grads_early_wait")
        half = QKV_ROWS // 2
        c = lax.axis_index("c")
        mine = lax.dynamic_slice_in_dim(g["w_in_qkv"], c * half, half, axis=0)
        (theirs,) = _sibling_exchange([lax.dynamic_slice_in_dim(g["w_in_qkv"], (1 - c) * half, half, axis=0)], "sibling_qkv_rows")
        self.late = [_split_by_chip(n, g[n]).astype(BF16) for n in self.LATE] + [_add_cast(mine, theirs, "add_qkv_rows")]
        self.late_started, token = _exchange_start(self.late, (2,), "grads_late_start")
        names = self.EARLY[1:]
        mine = _sum_landed_small(self.early_landed[1:], [self._own(slabs) for slabs in self.early[1:]], "sum_early")
        theirs = _sibling_exchange(mine, "sibling_early", after=token)
        for n, out in zip(names, _adamw_small(mine, theirs, [self.state[n] for n in names], "adamw_early")):
            self.outs[n] = out
        return self.outs[names[-1]][0]

    def finish(self, after):
        late_landed = _exchange_wait(self.late, (2,), self.late_started, after, "grads_late_wait")
        sums = {"w_in": _sum_landed(self.early_landed[0], self._own(self.early[0]), "sum_w_in",
                                    first_land=late_landed[2], first_own=self.late[2])}
        small = _sum_landed_small(late_landed[:2], [self._own(slabs) for slabs in self.late[:2]], "sum_late")
        sums.update(zip(self.LATE, small))
        return sums


def _adamw_math(g, w, m, v):
    nm = ADAM_B1 * m + (1.0 - ADAM_B1) * g
    nv = ADAM_B2 * v + (1.0 - ADAM_B2) * (g * g)
    m_hat = nm / (1.0 - ADAM_B1 ** ADAM_STEP)
    v_hat = nv / (1.0 - ADAM_B2 ** ADAM_STEP)
    return -ADAM_LR * (m_hat / (jnp.sqrt(v_hat) + ADAM_EPS) + ADAM_WD * w), nm, nv


def _adamw(p_mine, p_sibling, w, m, v, name):
    a, b = p_mine.shape
    steps, tile, at = _block_tiling(a, b)

    def body(a_ref, b_ref, w_ref, m_ref, v_ref, g_ref, d_ref, nm_ref, nv_ref):
        g = a_ref[...] + b_ref[...]
        g_ref[...] = g
        d_ref[...], nm_ref[...], nv_ref[...] = _adamw_math(g, w_ref[...], m_ref[...], v_ref[...])

    spec = pl.BlockSpec(tile, at)
    sds = jax.ShapeDtypeStruct((a, b), F32)
    return pl.pallas_call(
        body, name=name, grid=(steps,), in_specs=[spec] * 5, out_specs=[spec] * 4, out_shape=[sds] * 4,
        compiler_params=_params(("parallel",)),
    )(p_mine, p_sibling, w, m, v)


LOSS_AT = (2, 1024)


def _vec_pack(vg, loss):
    names = [n for n, _, _ in VEC_ROWS]

    def body(*refs):
        o_ref = refs[-1]
        lb_ref, loss_ref = refs[len(names)], refs[len(names) + 1]
        o_ref[...] = jnp.zeros_like(o_ref)
        o_ref[LOSS_AT[0]:LOSS_AT[0] + 1, LOSS_AT[1]:LOSS_AT[1] + LANE] = jnp.broadcast_to(loss_ref[...], (1, LANE))
        for (name, row, size), ref in zip(VEC_ROWS, refs):
            if name == "g_hgrn":
                r = lax.broadcasted_iota(jnp.int32, (NH * V_DIM, LANE), 0)
                c = lax.broadcasted_iota(jnp.int32, (NH * V_DIM, LANE), 1)
                fold = ((r % V_DIM) == c).astype(F32)
                o_ref[row:row + 1, 0:LANE] = jnp.dot(ref[...], fold, precision=HIGHEST, preferred_element_type=F32)
            else:
                o_ref[row:row + 1, 0:size] = ref[...]
        o_ref[VEC_LB_ROW:VEC_LB_ROW + 2, 0:512] = lb_ref[...]

    return pl.pallas_call(body, name="vec_pack", out_shape=jax.ShapeDtypeStruct(VEC_SHAPE, F32))(
        *[vg[n] for n in names], vg["lb_logits"], loss)


def _adamw_vec(p_mine, p_sibling, w, m, v):
    names = [n for n, _, _ in VEC_ROWS] + ["lb_logits"]
    k = len(names)

    def body(a_ref, b_ref, *refs):
        ins, outs = refs[:3 * k], refs[3 * k:]
        at = (slice(LOSS_AT[0], LOSS_AT[0] + 1), slice(LOSS_AT[1], LOSS_AT[1] + LANE))
        outs[-1][...] = a_ref[at] + b_ref[at]
        for i, name in enumerate(names):
            if name == "lb_logits":
                rows, cols = slice(VEC_LB_ROW, VEC_LB_ROW + 2), slice(0, 512)
            else:
                _, row, size = VEC_ROWS[i]
                rows, cols = slice(row, row + 1), slice(0, size)
            g = a_ref[rows, cols] + b_ref[rows, cols]
            d, nm, nv = _adamw_math(g, ins[i][...], ins[k + i][...], ins[2 * k + i][...])
            for o_ref, val in zip(outs[4 * i:4 * i + 4], (g, d, nm, nv)):
                o_ref[...] = val

    shapes = [jax.ShapeDtypeStruct(w[n].shape, F32) for n in names for _ in range(4)] + [jax.ShapeDtypeStruct((1, LANE), F32)]
    res = pl.pallas_call(body, name="adamw_vec", out_shape=shapes)(
        p_mine, p_sibling, *[w[n] for n in names], *[m[n] for n in names], *[v[n] for n in names])
    return [{n: res[4 * i + j] for i, n in enumerate(names)} for j in range(4)], res[-1]


WEIGHTS = ("g_pre", "w_in", "b_gate", "g_q", "w_uq", "g_kv", "w_ukv", "lb_logits", "g_hgrn", "w_branch_a", "w_branch_b", "w_out", "g_post")


def kernel(x, g_pre, w_in, b_gate, g_q, w_uq, g_kv, w_ukv, lb_logits, g_hgrn, w_branch_a, w_branch_b, w_out, g_post, loss_target, m_g_pre, m_w_in, m_b_gate, m_g_q, m_w_uq, m_g_kv, m_w_ukv, m_lb_logits, m_g_hgrn, m_w_branch_a, m_w_branch_b, m_w_out, m_g_post, v_g_pre, v_w_in, v_b_gate, v_g_q, v_w_uq, v_g_kv, v_w_ukv, v_lb_logits, v_g_hgrn, v_w_branch_a, v_w_branch_b, v_w_out, v_g_post):
    w = dict(g_pre=g_pre, w_in=w_in, b_gate=b_gate, g_q=g_q, w_uq=w_uq, g_kv=g_kv, w_ukv=w_ukv, lb_logits=lb_logits, g_hgrn=g_hgrn,
             w_branch_a=w_branch_a, w_branch_b=w_branch_b, w_out=w_out, g_post=g_post)
    m = dict(g_pre=m_g_pre, w_in=m_w_in, b_gate=m_b_gate, g_q=m_g_q, w_uq=m_w_uq, g_kv=m_g_kv, w_ukv=m_w_ukv, lb_logits=m_lb_logits,
             g_hgrn=m_g_hgrn, w_branch_a=m_w_branch_a, w_branch_b=m_w_branch_b, w_out=m_w_out, g_post=m_g_post)
    v = dict(g_pre=v_g_pre, w_in=v_w_in, b_gate=v_b_gate, g_q=v_g_q, w_uq=v_w_uq, g_kv=v_g_kv, w_ukv=v_w_ukv, lb_logits=v_lb_logits,
             g_hgrn=v_g_hgrn, w_branch_a=v_w_branch_a, w_branch_b=v_w_branch_b, w_out=v_w_out, g_post=v_g_post)
    blocks = {n: _to_block(n, w[n]).astype(BF16) for n in BIG}
    w_in_all = _gather_w_in(blocks["w_in"])
    weights = _LaterWeights(blocks, w_in_all)
    state = {n: [_to_block(n, t[n]) for t in (w, m, v)] for n in BIG}
    exchange = _GradExchange(state)
    loss, grad_x, _, vec_grads = _local_step(
        x[0], loss_target[0], g_pre, _join_chips("w_in", w_in_all), b_gate, g_q, g_kv, lb_logits, g_hgrn, g_post, weights, exchange)
    vec = _vec_pack(vec_grads, loss)
    vec_started, token = _exchange_start([vec], (), "vec_start")
    sums = exchange.finish(token)
    rest = tuple(sums)
    (vec_landed,) = _exchange_wait([vec], (), vec_started, sums[rest[-1]], "vec_wait")
    mine = [sums[n] for n in rest] + [_sum_landed(vec_landed, vec, "sum_vec")]
    theirs = _sibling_exchange(mine, "sibling_grads")
    done = dict(exchange.outs)
    done["w_in"] = _adamw(mine[0], theirs[0], *state["w_in"], "adamw_w_in")
    small = _adamw_small(mine[1:-1], theirs[1:-1], [state[n] for n in rest[1:]], "adamw_late")
    done.update(zip(rest[1:], small))
    outs = [{}, {}, {}, {}]
    for n in BIG:
        for o, val in zip(outs, done[n]):
            o[n] = _from_block(n, val)
    vec_outs, total = _adamw_vec(mine[-1], theirs[-1], w, m, v)
    for o, vals in zip(outs, vec_outs):
        o.update(vals)
    return (total[0, 0], grad_x[None], *[o[n] for o in outs for n in WEIGHTS])
```

```python
import math

import numpy as np
import jax
import jax.numpy as jnp
from jax import lax
from jax.experimental import pallas as pl
from jax.experimental.pallas import tpu as pltpu

F32 = jnp.float32
BF16 = jnp.bfloat16
HIGHEST = lax.Precision.HIGHEST

D = 1024
NH = 8
QK_NOPE, QK_ROPE, V_DIM = 64, 32, 64
Q_LORA, KV_LORA = 768, 256
CHUNK = 64
HG_BLOCK = 32
EPS = 1e-6
LANE = 128
P_MERGE, P_GA, P_HQ, P_HF, P_HI, P_GB, P_CQ, P_CKV, P_KPE = 0, 2048, 2560, 3072, 3584, 4096, 4608, 5376, 5632
D_P = 5760
O_CQ, O_CKV, O_KPE, O_GA, O_HQ, O_HF, O_HI, O_GB, O_MERGE = 0, 768, 1024, 1056, 1568, 2080, 2592, 3104, 3616

TM = 512
TM_MID = 256
TQ = 1024
ONES_LANE = (LANE - 1, 0)
TH = 256
HG_PAIRS = 4
VMEM_LIMIT = 56 * 1024 * 1024

ADAM_LR, ADAM_B1, ADAM_B2, ADAM_EPS, ADAM_WD, ADAM_STEP = 0.001, 0.9, 0.999, 1e-08, 0.01, 10

NT_DIMS = (((1,), (1,)), ((), ()))
TN_DIMS = (((0,), (0,)), ((), ()))


def _params(sem):
    return pltpu.CompilerParams(dimension_semantics=sem, vmem_limit_bytes=VMEM_LIMIT)


def _mm(a, b):
    return jnp.dot(a, b, preferred_element_type=F32)


def _mm_nt(a, b):
    return lax.dot_general(a, b, NT_DIMS, preferred_element_type=F32)


def _mm_tn(a, b):
    return lax.dot_general(a, b, TN_DIMS, preferred_element_type=F32)


def _sigmoid(z):
    return jax.nn.sigmoid(z)


def _rope(v, c, s1, s2):
    return v * c + pltpu.roll(v, 112, 1) * s1 + pltpu.roll(v, 16, 1) * s2


def _rope_t(dy, c, s1, s2):
    return dy * c + pltpu.roll(dy * s1, 16, 1) + pltpu.roll(dy * s2, 112, 1)


def _rope_tables(s):
    f32 = np.float32
    inv = f32(10000.0) ** (-np.arange(0, QK_ROPE, 2, dtype=f32) / f32(QK_ROPE))
    ang = np.arange(s, dtype=f32)[:, None] * inv[None, :]
    cos, sin = np.cos(ang).astype(f32), np.sin(ang).astype(f32)
    z64, z32, o64, o32 = np.zeros((s, 64), f32), np.zeros((s, 32), f32), np.ones((s, 64), f32), np.ones((s, 32), f32)
    z16 = np.zeros((s, 16), f32)
    c = np.concatenate([o64, cos, cos, o32], axis=1)
    s1 = np.concatenate([z64, -sin, z16, z32], axis=1)
    s2 = np.concatenate([z64, z16, sin, z32], axis=1)
    return jnp.asarray(c), jnp.asarray(s1), jnp.asarray(s2)


def _front_fwd(x, g_pre, w_in_pt, tokens=()):
    s = x.shape[0]
    tokens = list(tokens)

    def body(x_ref, g_ref, w_ref, *refs):
        o_ref, h_ref = refs[len(tokens):]
        xv = x_ref[...]
        r = lax.rsqrt(jnp.mean(xv * xv, axis=-1, keepdims=True) + EPS)
        h = ((xv * r) * g_ref[...]).astype(BF16)
        h_ref[...] = h
        o_ref[...] = _mm_nt(h, w_ref[...])

    return pl.pallas_call(
        body, name="front_fwd", grid=(s // TM,),
        in_specs=[pl.BlockSpec((TM, D), lambda i: (i, 0)), pl.BlockSpec((1, D), lambda i: (0, 0)),
                  pl.BlockSpec((D_P, D), lambda i: (0, 0))] + [pl.BlockSpec((8, LANE), lambda i: (0, 0))] * len(tokens),
        out_specs=[pl.BlockSpec((TM, D_P), lambda i: (i, 0)), pl.BlockSpec((TM, D), lambda i: (i, 0))],
        out_shape=[jax.ShapeDtypeStruct((s, D_P), F32), jax.ShapeDtypeStruct((s, D), BF16)],
        compiler_params=_params(("parallel",)),
    )(x, g_pre, w_in_pt, *tokens)


def _norm_rows(v, g):
    r = lax.rsqrt(jnp.mean(v * v, axis=-1, keepdims=True) + EPS)
    return (v * r) * g, r


def _qkv_fwd(proj, g_q, g_kv, w_uq_p, w_k_p, w_v_p, rc, rs1, rs2):
    s = proj.shape[0]

    def body(cq_ref, ckv_ref, kpe_ref, gq_ref, gkv_ref, wq_ref, wk_ref, wv_ref, c_ref, s1_ref, s2_ref, q_ref, k_ref, v_ref):
        c, s1, s2 = c_ref[...], s1_ref[...], s2_ref[...]
        cqn, _ = _norm_rows(cq_ref[...], gq_ref[...])
        ckvn, _ = _norm_rows(ckv_ref[...], gkv_ref[...])
        ckvn = ckvn.astype(BF16)
        qf = _mm(cqn.astype(BF16), wq_ref[...])
        kf = _mm(ckvn, wk_ref[...])
        vf = _mm(ckvn, wv_ref[...])
        kpe = _rope(kpe_ref[...], c, s1, s2)
        lane = lax.broadcasted_iota(jnp.int32, (TM, LANE), 1)
        for h in range(NH):
            blk = slice(h * LANE, (h + 1) * LANE)
            q_ref[h] = _rope(qf[:, blk], c, s1, s2).astype(BF16)
            k_ref[h] = (kf[:, blk] + kpe).astype(BF16)
            v_ref[h] = jnp.where(lane == ONES_LANE[h % 2], 1.0, vf[:, blk]).astype(BF16)

    row = lambda w, j: pl.BlockSpec((TM, w), lambda i: (i, j))
    full = lambda a: pl.BlockSpec(a.shape, lambda i: (0,) * a.ndim)
    hs = jax.ShapeDtypeStruct((NH, s, LANE), BF16)
    return pl.pallas_call(
        body, name="qkv_fwd", grid=(s // TM,),
        in_specs=[row(Q_LORA, P_CQ // Q_LORA), row(KV_LORA, P_CKV // KV_LORA), row(LANE, P_KPE // LANE),
                  full(g_q), full(g_kv), full(w_uq_p), full(w_k_p), full(w_v_p), row(LANE, 0), row(LANE, 0), row(LANE, 0)],
        out_specs=[pl.BlockSpec((NH, TM, LANE), lambda i: (0, i, 0))] * 3,
        out_shape=[hs, hs, hs],
        compiler_params=_params(("parallel",)),
    )(proj, proj, proj, g_q, g_kv, w_uq_p, w_k_p, w_v_p, rc, rs1, rs2)


LOG2E = 1.4426950408889634
QK_SCALE2 = LOG2E / math.sqrt(QK_NOPE + QK_ROPE)


HQ = TQ // 2


def _diag_visible(n):
    row = lax.broadcasted_iota(jnp.int32, (n, n), 0)
    col = lax.broadcasted_iota(jnp.int32, (n, n), 1)
    return (col // CHUNK) <= (row // CHUNK)


def _attn_fwd(q, k, vv):
    s = q.shape[1]

    def body(q_ref, k_ref, v_ref, o_ref, lse_ref):
        i = pl.program_id(1)
        qs = (q_ref[0], q_ref[1])

        def tiles(t, carry, diag):
            rows = pl.ds(pl.multiple_of(t * TQ, TQ), TQ)
            sc = [_mm_nt(qs[hh], k_ref[hh, rows, :]) for hh in range(2)]
            if diag:
                sc = [jnp.where(_diag_visible(TQ), s_, -jnp.inf) for s_ in sc]
            m_new = [jnp.maximum(carry[hh][0], jnp.max(sc[hh], axis=-1, keepdims=True)) for hh in range(2)]
            alpha = [jnp.exp2((carry[hh][0] - m_new[hh]) * QK_SCALE2) for hh in range(2)]
            p = [jnp.exp2((sc[hh] - m_new[hh]) * QK_SCALE2).astype(BF16) for hh in range(2)]
            acc = [alpha[hh] * carry[hh][1] + _mm(p[hh], v_ref[hh, rows, :]) for hh in range(2)]
            return (m_new[0], acc[0]), (m_new[1], acc[1])

        init = (jnp.full((TQ, 1), -jnp.inf, F32), jnp.zeros((TQ, LANE), F32))
        carry = lax.fori_loop(0, i, lambda t, c: tiles(t, c, False), (init, init))
        carry = tiles(i, carry, True)
        lane = lax.broadcasted_iota(jnp.int32, (TQ, LANE), 1)
        out = jnp.zeros((TQ, LANE), F32)
        for hh in range(2):
            m, acc = carry[hh]
            l = jnp.sum(jnp.where(lane == ONES_LANE[hh], acc, 0.0), axis=-1, keepdims=True)
            out = out + jnp.where((lane < V_DIM) == (hh == 0), acc, 0.0) / l
            lse_ref[hh] = jnp.broadcast_to(m * QK_SCALE2 + jnp.log(l) * LOG2E, (TQ, LANE))
        o_ref[...] = out

    return pl.pallas_call(
        body, name="attn_fwd", grid=(NH // 2, s // TQ),
        in_specs=[pl.BlockSpec((2, TQ, LANE), lambda p, i: (p, i, 0)), pl.BlockSpec((2, s, LANE), lambda p, i: (p, 0, 0)),
                  pl.BlockSpec((2, s, LANE), lambda p, i: (p, 0, 0))],
        out_specs=[pl.BlockSpec((TQ, LANE), lambda p, i: (i, p)), pl.BlockSpec((2, TQ, LANE), lambda p, i: (p, i, 0))],
        out_shape=[jax.ShapeDtypeStruct((s, NH * V_DIM), F32), jax.ShapeDtypeStruct((NH, s, LANE), F32)],
        compiler_params=_params(("parallel", "parallel")),
    )(q, k, vv)


def _lower_bound(lbl):
    a0, a1 = lbl[0:1, :], lbl[1:2, :]
    mx = jnp.maximum(a0, a1)
    e0, e1 = jnp.exp(a0 - mx), jnp.exp(a1 - mx)
    return e0 / (e0 + e1)


def _chunk_cumsum(v, reverse=False):
    pos = lax.broadcasted_iota(jnp.int32, v.shape, 0) % HG_BLOCK
    s = 1
    while s < HG_BLOCK:
        if reverse:
            v = v + jnp.where(pos < HG_BLOCK - s, pltpu.roll(v, TH - s, 0), 0.0)
        else:
            v = v + jnp.where(pos >= s, pltpu.roll(v, s, 0), 0.0)
        s *= 2
    return v


def _hgrn_gates(hq, hf, lb):
    sig = _sigmoid(hf)
    f = lb + (1.0 - lb) * sig
    g = jnp.log(f)
    kk = 1.0 - f
    r = lax.broadcasted_iota(jnp.int32, (TH, TH), 0)
    c = lax.broadcasted_iota(jnp.int32, (TH, TH), 1)
    tri = ((r // HG_BLOCK) == (c // HG_BLOCK)) & (r >= c)
    cum = _chunk_cumsum(g)
    nch = TH // HG_BLOCK
    total = _chunks(cum)[:, HG_BLOCK - 1:HG_BLOCK, :]
    lastb = jnp.broadcast_to(total, (nch, HG_BLOCK, hf.shape[-1])).reshape(hf.shape)
    e, ei, ee = jnp.exp(cum), jnp.exp(-cum), jnp.exp(lastb - cum)
    return dict(sig=sig, f=f, kk=kk, tri=tri, cum=cum, total=total, decay=jnp.exp(total), e=e, ei=ei, ee=ee,
                qd=hq * e, ki=kk * ei, ke=kk * ee)


def _chunks(v):
    return v.reshape(TH // HG_BLOCK, HG_BLOCK, v.shape[-1])


def _bmm_nt(a, b):
    return lax.dot_general(a, b, (((2,), (2,)), ((0,), (0,))), preferred_element_type=F32)


def _bmm_nn(a, b):
    return lax.dot_general(a, b, (((2,), (1,)), ((0,), (0,))), preferred_element_type=F32)


def _bmm_tn(a, b):
    return lax.dot_general(a, b, (((1,), (1,)), ((0,), (0,))), preferred_element_type=F32)


def _pair_masks():
    lane = lax.broadcasted_iota(jnp.int32, (TH, LANE), 1)
    kr = lax.broadcasted_iota(jnp.int32, (LANE, LANE), 0)
    kc = lax.broadcasted_iota(jnp.int32, (LANE, LANE), 1)
    return lane < 64, (kr // 64) == (kc // 64)


def _hgrn_fwd(proj, lbl):
    s = proj.shape[0]
    nch = TH // HG_BLOCK

    def body(hq_ref, hf_ref, hi_ref, lbl_ref, o_ref, st_ref, st):
        @pl.when(pl.program_id(1) == 0)
        def _():
            st[...] = jnp.zeros_like(st)

        m0, bd = _pair_masks()
        gt = _hgrn_gates(hq_ref[...], hf_ref[...], _lower_bound(lbl_ref[...]))
        v_b, qd, qd_b = hi_ref[...].astype(BF16), gt["qd"], gt["qd"].astype(BF16)
        ki_b, ke_b = gt["ki"].astype(BF16), gt["ke"].astype(BF16)
        pairs = [slice(u * LANE, (u + 1) * LANE) for u in range(HG_PAIRS)]
        heads = [(lanes, m0 if hh == 0 else jnp.logical_not(m0)) for lanes in pairs for hh in range(2)]
        a_b = [jnp.where(gt["tri"], _mm_nt(jnp.where(mh, qd[:, lanes], 0.0).astype(BF16), ki_b[:, lanes]), 0.0).astype(BF16)
               for lanes, mh in heads]
        intra = [jnp.where(m0, _mm(a_b[2 * u], v_b[:, lanes]), _mm(a_b[2 * u + 1], v_b[:, lanes])) for u, lanes in enumerate(pairs)]
        upd = [_bmm_tn(_chunks(v_b[:, lanes]), _chunks(ke_b[:, lanes])) for lanes in pairs]
        entering = []
        for u, lanes in enumerate(pairs):
            cur, states = st[u], []
            for n in range(nch):
                states.append(cur)
                cur = gt["decay"][n][:, lanes] * cur + jnp.where(bd, upd[u][n], 0.0)
            st[u] = cur
            entering.append(jnp.stack(states))
            st_ref[u] = entering[u]
        for u, lanes in enumerate(pairs):
            o_ref[:, lanes] = intra[u] + _bmm_nt(_chunks(qd_b[:, lanes]), entering[u].astype(BF16)).reshape(TH, LANE)

    wide = HG_PAIRS * LANE
    col = lambda base: pl.BlockSpec((TH, wide), lambda p, i: (i, base // wide + p))
    return pl.pallas_call(
        body, name="hgrn_fwd", grid=(NH // 2 // HG_PAIRS, s // TH),
        in_specs=[col(P_HQ), col(P_HF), col(P_HI), pl.BlockSpec((2, wide), lambda p, i: (0, p))],
        out_specs=[pl.BlockSpec((TH, wide), lambda p, i: (i, p)),
                   pl.BlockSpec((HG_PAIRS, nch, LANE, LANE), lambda p, i: (p, i, 0, 0))],
        out_shape=[jax.ShapeDtypeStruct((s, 512), F32), jax.ShapeDtypeStruct((NH // 2, s // HG_BLOCK, LANE, LANE), F32)],
        scratch_shapes=[pltpu.VMEM((HG_PAIRS, LANE, LANE), F32)],
        compiler_params=_params(("parallel", "arbitrary")),
    )(proj, proj, proj, lbl)


def _group_sum(v):
    low = lax.broadcasted_iota(jnp.int32, (v.shape[0], LANE), 1) < V_DIM
    blocks = []
    for b in range(v.shape[1] // LANE):
        blk = v[:, b * LANE:(b + 1) * LANE]
        s_low = jnp.sum(jnp.where(low, blk, 0.0), axis=-1, keepdims=True)
        s_high = jnp.sum(jnp.where(low, 0.0, blk), axis=-1, keepdims=True)
        blocks.append(jnp.where(low, s_low, s_high))
    return jnp.concatenate(blocks, axis=1)


def _dsilu(z, sg):
    return sg * (1.0 + z * (1.0 - sg))


def _mid(proj, attn, o_raw, x, tgt, g_hg, b_gate, g_post, wa, wb, w_out):
    s = x.shape[0]

    def body(attn_ref, ga_ref, o_ref, gb_ref, mg_ref, x_ref, t_ref, ghg_ref, bg_ref, gp_ref, wa_ref, wb_ref, wo_ref,
             loss_ref, dout_ref, dattn_ref, dga_ref, dor_ref, dgb_ref, dmg_ref, dwo_ref, dwa_ref, dwb_ref, dgp_ref, dbg_ref, dghg_ref):
        @pl.when(pl.program_id(0) == 0)
        def _():
            for rf in (loss_ref, dwo_ref, dwa_ref, dwb_ref, dgp_ref, dbg_ref, dghg_ref):
                rf[...] = jnp.zeros_like(rf)

        attn, za, orw, zb = attn_ref[...], ga_ref[...], o_ref[...], gb_ref[...]
        ghg, gp = ghg_ref[...], gp_ref[...]
        sga, sgb = _sigmoid(za), _sigmoid(zb)
        sa, sb = za * sga, zb * sgb
        ga = attn * sa
        rh = lax.rsqrt(_group_sum(orw * orw) * (1.0 / V_DIM) + EPS)
        on = (orw * rh) * ghg
        gb = on * sb
        ga_b, gb_b = ga.astype(BF16), gb.astype(BF16)
        ya = _mm(ga_b, wa_ref[...])
        yb = _mm(gb_b, wb_ref[...])
        gates = _sigmoid(mg_ref[...] + bg_ref[...])
        g0, g1 = gates[:, :D], gates[:, D:]
        m_b = (g0 * ya + g1 * yb).astype(BF16)
        y = _mm(m_b, wo_ref[...])
        ry = lax.rsqrt(jnp.mean(y * y, axis=-1, keepdims=True) + EPS)
        out = x_ref[...] + (y * ry) * gp
        err = out - t_ref[...]
        loss_ref[...] += 0.5 * jnp.sum(jnp.mean(err * err, axis=-1, keepdims=True), axis=0, keepdims=True)
        dout = err * (1.0 / D)
        dout_ref[...] = dout
        dgp_ref[...] += jnp.sum(dout * (y * ry), axis=0, keepdims=True)
        dgy = dout * gp
        dy = ry * dgy - y * (ry * ry * ry) * jnp.mean(y * dgy, axis=-1, keepdims=True)
        dy_b = dy.astype(BF16)
        dm = _mm_nt(dy_b, wo_ref[...])
        dya_b, dyb_b = (dm * g0).astype(BF16), (dm * g1).astype(BF16)
        dga = _mm_nt(dya_b, wa_ref[...])
        dgb = _mm_nt(dyb_b, wb_ref[...])
        dwo_ref[...] += _mm_tn(m_b, dy_b)
        dwa_ref[...] += _mm_tn(ga_b, dya_b)
        dwb_ref[...] += _mm_tn(gb_b, dyb_b)
        dg0, dg1 = dm * ya, dm * yb
        dmg = jnp.concatenate([dg0 * g0 * (1.0 - g0), dg1 * g1 * (1.0 - g1)], axis=1)
        dmg_ref[...] = dmg.astype(BF16)
        dbg_ref[...] += jnp.sum(dmg, axis=0, keepdims=True)
        dattn_ref[...] = dga * sa
        dga_ref[...] = (dga * attn * _dsilu(za, sga)).astype(BF16)
        dgb_ref[...] = (dgb * on * _dsilu(zb, sgb)).astype(BF16)
        don = dgb * sb
        dghg_ref[...] += jnp.sum(don * (orw * rh), axis=0, keepdims=True)
        dgo = don * ghg
        dor_ref[...] = rh * dgo - orw * (rh * rh * rh) * (_group_sum(orw * dgo) * (1.0 / V_DIM))

    row = lambda w, j=0: pl.BlockSpec((TM_MID, w), lambda i: (i, j))
    full = lambda a: pl.BlockSpec(a.shape, lambda i: (0,) * a.ndim)
    acc = lambda shape: pl.BlockSpec(shape, lambda i: (0, 0))
    sds = jax.ShapeDtypeStruct
    return pl.pallas_call(
        body, name="mid", grid=(s // TM_MID,),
        in_specs=[row(512), row(512, P_GA // 512), row(512), row(512, P_GB // 512), row(2048, P_MERGE // 2048), row(D), row(D),
                  full(g_hg), full(b_gate), full(g_post), full(wa), full(wb), full(w_out)],
        out_specs=[acc((1, 1)), row(D), row(512), row(512), row(512), row(512), row(2048),
                   acc((D, D)), acc((512, D)), acc((512, D)), acc((1, D)), acc((1, 2048)), acc((1, 512))],
        out_shape=[sds((1, 1), F32), sds((s, D), F32), sds((s, 512), F32), sds((s, 512), BF16), sds((s, 512), F32), sds((s, 512), BF16),
                   sds((s, 2048), BF16), sds((D, D), F32), sds((512, D), F32), sds((512, D), F32), sds((1, D), F32),
                   sds((1, 2048), F32), sds((1, 512), F32)],
        compiler_params=_params(("arbitrary",)),
    )(attn, proj, o_raw, proj, proj, x, tgt, g_hg, b_gate, g_post, wa, wb, w_out)


def _attn_bwd(q, k, vv, attn, dattn, lse, token):
    s = q.shape[1]
    nt = s // TQ
    scale = 1.0 / math.sqrt(QK_NOPE + QK_ROPE)

    def body(q_ref, k_ref, v_ref, o_ref, do_ref, lse_ref, token_ref, dq_ref, dk_ref, dv_ref, do_s, delta_s):
        j = pl.program_id(1)

        @pl.when(j == 0)
        def _():
            dq_ref[...] = jnp.zeros_like(dq_ref)
            lane = lax.broadcasted_iota(jnp.int32, (TQ, LANE), 1)

            @pl.loop(0, nt)
            def _(i):
                rows = pl.ds(pl.multiple_of(i * TQ, TQ), TQ)
                do, o = do_ref[rows, :], o_ref[rows, :]
                for hh in range(2):
                    doh = jnp.where((lane < 64) if hh == 0 else (lane >= 64), do, 0.0)
                    do_s[hh, rows, :] = doh.astype(BF16)
                    delta_s[hh, rows, :] = jnp.broadcast_to(jnp.sum(doh * o, axis=-1, keepdims=True), (TQ, LANE))

        kjs, vjs = (k_ref[0], k_ref[1]), (v_ref[0], v_ref[1])

        def tile(hh, start, size, kj, vj, diag):
            rows = pl.ds(pl.multiple_of(start, size), size)
            wide = lambda a: jnp.concatenate([a] * (kj.shape[0] // LANE), axis=1)
            qi, do_b = q_ref[hh, rows, :], do_s[hh, rows, :]
            sc, dp = _mm_nt(qi, kj), _mm_nt(do_b, vj)
            p = jnp.exp2(sc * QK_SCALE2 - wide(lse_ref[hh, rows, :]))
            if diag:
                p = jnp.where(_diag_visible(size), p, 0.0)
            ds_b = (p * (dp - wide(delta_s[hh, rows, :]))).astype(BF16)
            dv, dk = _mm_tn(do_b, p.astype(BF16)), _mm_tn(qi, ds_b)
            dq_ref[hh, rows, :] += _mm(ds_b, kj)
            return dk, dv

        def step(i, carry):
            new = [tile(hh, i * TQ, TQ, kjs[hh], vjs[hh], False) for hh in range(2)]
            return tuple((carry[hh][0] + new[hh][0], carry[hh][1] + new[hh][1]) for hh in range(2))

        def diagonal(hh):
            k0, k1, v0, v1 = kjs[hh][:HQ], kjs[hh][HQ:], vjs[hh][:HQ], vjs[hh][HQ:]
            a = tile(hh, j * TQ, HQ, k0, v0, True)
            b = tile(hh, j * TQ + HQ, HQ, k0, v0, False)
            c = tile(hh, j * TQ + HQ, HQ, k1, v1, True)
            return jnp.concatenate([a[0] + b[0], c[0]], axis=1), jnp.concatenate([a[1] + b[1], c[1]], axis=1)

        carry = lax.fori_loop(j + 1, nt, step, (diagonal(0), diagonal(1)))
        for hh in range(2):
            dk_ref[hh] = carry[hh][0].T * scale
            dv_ref[hh] = carry[hh][1].T

        @pl.when(j == nt - 1)
        def _():
            dq_ref[...] = dq_ref[...] * scale

    whole = pl.BlockSpec((2, s, LANE), lambda p, j: (p, 0, 0))
    tile_spec = pl.BlockSpec((2, TQ, LANE), lambda p, j: (p, j, 0))
    cols = pl.BlockSpec((s, LANE), lambda p, j: (0, p))
    hs = jax.ShapeDtypeStruct((NH, s, LANE), F32)
    return pl.pallas_call(
        body, name="attn_bwd", grid=(NH // 2, nt),
        in_specs=[whole, tile_spec, tile_spec, cols, cols, whole, pl.BlockSpec((8, LANE), lambda p, j: (0, 0))],
        out_specs=[whole, tile_spec, tile_spec],
        out_shape=[hs, hs, hs],
        scratch_shapes=[pltpu.VMEM((2, s, LANE), BF16), pltpu.VMEM((2, s, LANE), F32)],
        compiler_params=_params(("parallel", "arbitrary")),
    )(q, k, vv, attn, dattn, lse, token)


def _hgrn_bwd(proj, lbl, states, do_raw):
    s = proj.shape[0]
    nt = s // TH
    nch = TH // HG_BLOCK

    def body(hq_ref, hf_ref, hi_ref, lbl_ref, st_ref, do_ref, dh_ref, dlbl_ref, dst, dlb):
        step = pl.program_id(1)

        @pl.when(step == 0)
        def _():
            dst[...] = jnp.zeros_like(dst)
            dlb[...] = jnp.zeros_like(dlb)

        m0, bd = _pair_masks()
        lb = _lower_bound(lbl_ref[...])
        gt = _hgrn_gates(hq_ref[...], hf_ref[...], lb)
        do = do_ref[...]
        qd, ki, ke = gt["qd"], gt["ki"], gt["ke"]
        v_b, do_b = hi_ref[...].astype(BF16), do.astype(BF16)
        qd_b, ki_b, ke_b = qd.astype(BF16), ki.astype(BF16), ke.astype(BF16)
        pairs = [slice(u * LANE, (u + 1) * LANE) for u in range(HG_PAIRS)]
        heads = [(lanes, m0 if hh == 0 else jnp.logical_not(m0)) for lanes in pairs for hh in range(2)]
        a_b = [jnp.where(gt["tri"], _mm_nt(jnp.where(mh, qd[:, lanes], 0.0).astype(BF16), ki_b[:, lanes]), 0.0).astype(BF16)
               for lanes, mh in heads]
        doh_b = [jnp.where(mh, do[:, lanes], 0.0).astype(BF16) for lanes, mh in heads]
        da_b = [jnp.where(gt["tri"], _mm_nt(d, v_b[:, lanes]), 0.0).astype(BF16) for d, (lanes, _) in zip(doh_b, heads)]
        dv_p, dqd_p, dki_p = [], [], []
        for u, lanes in enumerate(pairs):
            e, o = 2 * u, 2 * u + 1
            dv_p.append(_mm_tn(a_b[e], doh_b[e]) + _mm_tn(a_b[o], doh_b[o]))
            dqd_p.append(jnp.where(m0, _mm(da_b[e], ki_b[:, lanes]), _mm(da_b[o], ki_b[:, lanes])))
            dki_p.append(jnp.where(m0, _mm_tn(da_b[e], qd_b[:, lanes]), _mm_tn(da_b[o], qd_b[:, lanes])))
        fed = [_bmm_tn(_chunks(do_b[:, lanes]), _chunks(qd_b[:, lanes])) for lanes in pairs]
        leaving = []
        for u, lanes in enumerate(pairs):
            ds, left = dst[u], [None] * nch
            for n in reversed(range(nch)):
                left[n] = ds
                ds = gt["decay"][n][:, lanes] * ds + jnp.where(bd, fed[u][n], 0.0)
            dst[u] = ds
            leaving.append(jnp.stack(left))
        dke_p, dlast_p = [], []
        for u, lanes in enumerate(pairs):
            entering, leaving_b = st_ref[u], leaving[u].astype(BF16)
            dke3 = _bmm_nn(_chunks(v_b[:, lanes]), leaving_b)
            dv_p[u] = dv_p[u] + _bmm_nt(_chunks(ke_b[:, lanes]), leaving_b).reshape(TH, LANE)
            dqd_p[u] = dqd_p[u] + _bmm_nn(_chunks(do_b[:, lanes]), entering.astype(BF16)).reshape(TH, LANE)
            dke_p.append(dke3.reshape(TH, LANE))
            dlast_p.append(jnp.sum(dke3 * _chunks(ke[:, lanes]), axis=1, keepdims=True)
                           + jnp.sum(leaving[u] * entering, axis=1, keepdims=True) * gt["decay"][:, :, lanes])
        cat = lambda parts: jnp.concatenate(parts, axis=-1)
        dv, dqd, dki, dke, dlast = cat(dv_p), cat(dqd_p), cat(dki_p), cat(dke_p), cat(dlast_p)
        dk = dki * gt["ei"] + dke * gt["ee"]
        dcum = dqd * qd - dki * ki - dke * ke
        dg = _chunk_cumsum(dcum, reverse=True) + jnp.broadcast_to(dlast, (nch, HG_BLOCK, dlast.shape[-1])).reshape(dcum.shape)
        sig = gt["sig"]
        df = dg / gt["f"] - dk
        dlb[...] += jnp.sum(df * (1.0 - sig), axis=0, keepdims=True)
        dh_ref[0] = (dqd * gt["e"]).astype(BF16)
        dh_ref[1] = ((df * (1.0 - lb)) * sig * (1.0 - sig)).astype(BF16)
        dh_ref[2] = dv.astype(BF16)

        @pl.when(step == nt - 1)
        def _():
            lb = _lower_bound(lbl_ref[...])
            da0 = dlb[...] * lb * (1.0 - lb)
            dlbl_ref[...] = jnp.concatenate([da0, -da0], axis=0)

    wide = HG_PAIRS * LANE
    col = lambda base: pl.BlockSpec((TH, wide), lambda p, i: (nt - 1 - i, base // wide + p))
    tile = pl.BlockSpec((TH, wide), lambda p, i: (nt - 1 - i, p))
    sds = jax.ShapeDtypeStruct
    return pl.pallas_call(
        body, name="hgrn_bwd", grid=(NH // 2 // HG_PAIRS, nt),
        in_specs=[col(P_HQ), col(P_HF), col(P_HI), pl.BlockSpec((2, wide), lambda p, i: (0, p)),
                  pl.BlockSpec((HG_PAIRS, nch, LANE, LANE), lambda p, i: (p, nt - 1 - i, 0, 0)), tile],
        out_specs=[pl.BlockSpec((3, TH, wide), lambda p, i: (0, nt - 1 - i, p)), pl.BlockSpec((2, wide), lambda p, i: (0, p))],
        out_shape=[sds((3, s, 512), BF16), sds((2, 512), F32)],
        scratch_shapes=[pltpu.VMEM((HG_PAIRS, LANE, LANE), F32), pltpu.VMEM((1, wide), F32)],
        compiler_params=_params(("parallel", "arbitrary")),
    )(proj, proj, proj, lbl, states, do_raw)


def _norm_rows_bwd(v, r, g, dn):
    dgv = dn * g
    return r * dgv - v * (r * r * r) * jnp.mean(v * dgv, axis=-1, keepdims=True)


def _qkv_bwd(proj, dq, dk, dvv, g_q, g_kv, w_uq_p, w_k_p, w_v_p, rc, rs1, rs2):
    s = proj.shape[0]

    def body(cq_ref, ckv_ref, dq_ref, dk_ref, dv_ref, gq_ref, gkv_ref, wq_ref, wk_ref, wv_ref, c_ref, s1_ref, s2_ref,
             dcq_ref, dckv_ref, dkpe_ref, dwq_ref, dwk_ref, dwv_ref, dgq_ref, dgkv_ref):
        @pl.when(pl.program_id(0) == 0)
        def _():
            for rf in (dwq_ref, dwk_ref, dwv_ref, dgq_ref, dgkv_ref):
                rf[...] = jnp.zeros_like(rf)

        c, s1, s2 = c_ref[...], s1_ref[...], s2_ref[...]
        cq, ckv = cq_ref[...], ckv_ref[...]
        gq, gkv = gq_ref[...], gkv_ref[...]
        cqn, rq = _norm_rows(cq, gq)
        ckvn, rkv = _norm_rows(ckv, gkv)
        cqn_b, ckvn_b = cqn.astype(BF16), ckvn.astype(BF16)
        dqf = jnp.concatenate([_rope_t(dq_ref[h], c, s1, s2) for h in range(NH)], axis=1).astype(BF16)
        dkf = jnp.concatenate([dk_ref[h] for h in range(NH)], axis=1).astype(BF16)
        dvf = jnp.concatenate([dv_ref[h] for h in range(NH)], axis=1).astype(BF16)
        dkpe = dk_ref[0]
        for h in range(1, NH):
            dkpe = dkpe + dk_ref[h]
        lane = lax.broadcasted_iota(jnp.int32, (TM, LANE), 1)
        dkpe = jnp.where((lane >= QK_NOPE) & (lane < QK_NOPE + QK_ROPE), dkpe, 0.0)
        dkpe_ref[...] = _rope_t(dkpe, c, s1, s2).astype(BF16)
        dcqn = _mm_nt(dqf, wq_ref[...])
        dckvn = _mm_nt(dkf, wk_ref[...]) + _mm_nt(dvf, wv_ref[...])
        dwq_ref[...] += _mm_tn(cqn_b, dqf)
        dwk_ref[...] += _mm_tn(ckvn_b, dkf)
        dwv_ref[...] += _mm_tn(ckvn_b, dvf)
        dgq_ref[...] += jnp.sum(dcqn * (cq * rq), axis=0, keepdims=True)
        dgkv_ref[...] += jnp.sum(dckvn * (ckv * rkv), axis=0, keepdims=True)
        dcq_ref[...] = _norm_rows_bwd(cq, rq, gq, dcqn).astype(BF16)
        dckv_ref[...] = _norm_rows_bwd(ckv, rkv, gkv, dckvn).astype(BF16)

    row = lambda w, j=0: pl.BlockSpec((TM, w), lambda i: (i, j))
    full = lambda a: pl.BlockSpec(a.shape, lambda i: (0,) * a.ndim)
    acc = lambda shape: pl.BlockSpec(shape, lambda i: (0, 0))
    heads = pl.BlockSpec((NH, TM, LANE), lambda i: (0, i, 0))
    sds = jax.ShapeDtypeStruct
    return pl.pallas_call(
        body, name="qkv_bwd", grid=(s // TM,),
        in_specs=[row(Q_LORA, P_CQ // Q_LORA), row(KV_LORA, P_CKV // KV_LORA), heads, heads, heads,
                  full(g_q), full(g_kv), full(w_uq_p), full(w_k_p), full(w_v_p), row(LANE), row(LANE), row(LANE)],
        out_specs=[row(Q_LORA), row(KV_LORA), row(LANE), acc((Q_LORA, D)), acc((KV_LORA, D)), acc((KV_LORA, D)),
                   acc((1, Q_LORA)), acc((1, KV_LORA))],
        out_shape=[sds((s, Q_LORA), BF16), sds((s, KV_LORA), BF16), sds((s, LANE), BF16), sds((Q_LORA, D), F32),
                   sds((KV_LORA, D), F32), sds((KV_LORA, D), F32), sds((1, Q_LORA), F32), sds((1, KV_LORA), F32)],
        compiler_params=_params(("arbitrary",)),
    )(proj, proj, dq, dk, dvv, g_q, g_kv, w_uq_p, w_k_p, w_v_p, rc, rs1, rs2)


def _front_bwd(x, dout, dmg, dga, dh3, dgb, dcq, dckv, dkpe, g_pre, w_in_pt, token):
    s = x.shape[0]

    def body(x_ref, do_ref, dmg_ref, dga_ref, dh3_ref, dgb_ref, dcq_ref, dckv_ref, dkpe_ref, g_ref, w_ref, token_ref, gx_ref, dg_ref):
        @pl.when(pl.program_id(0) == 0)
        def _():
            dg_ref[...] = jnp.zeros_like(dg_ref)

        xv, g = x_ref[...], g_ref[...]
        _, r = _norm_rows(xv, g)
        pieces = ((dmg_ref[...], P_MERGE), (dga_ref[...], P_GA), (dh3_ref[0], P_HQ), (dh3_ref[1], P_HF), (dh3_ref[2], P_HI),
                  (dgb_ref[...], P_GB), (dcq_ref[...], P_CQ), (dckv_ref[...], P_CKV), (dkpe_ref[...], P_KPE))
        dh = jnp.zeros((TM, D), F32)
        for piece, off in pieces:
            dh = dh + _mm(piece, w_ref[off:off + piece.shape[1], :])
        dg_ref[...] += jnp.sum(dh * (xv * r), axis=0, keepdims=True)
        gx_ref[...] = do_ref[...] + _norm_rows_bwd(xv, r, g, dh)

    row = lambda w: pl.BlockSpec((TM, w), lambda i: (i, 0))
    full = lambda a: pl.BlockSpec(a.shape, lambda i: (0,) * a.ndim)
    sds = jax.ShapeDtypeStruct
    return pl.pallas_call(
        body, name="front_bwd", grid=(s // TM,),
        in_specs=[row(D), row(D), row(2048), row(512), pl.BlockSpec((3, TM, 512), lambda i: (0, i, 0)), row(512), row(Q_LORA),
                  row(KV_LORA), row(LANE), full(g_pre), full(w_in_pt), pl.BlockSpec(memory_space=pl.ANY)],
        out_specs=[row(D), pl.BlockSpec((1, D), lambda i: (0, 0))],
        out_shape=[sds((s, D), F32), sds((1, D), F32)],
        compiler_params=_params(("arbitrary",)),
    )(x, dout, dmg, dga, dh3, dgb, dcq, dckv, dkpe, g_pre, w_in_pt, token)


TK_GRAD = 1024


def _win_grad(h, pieces, name):
    s = h.shape[0]
    n = len(pieces)

    def body(h_ref, *refs):
        @pl.when(pl.program_id(0) == 0)
        def _():
            for o_ref in refs[n:]:
                o_ref[...] = jnp.zeros_like(o_ref)

        hv = h_ref[...]
        for d_ref, o_ref in zip(refs[:n], refs[n:]):
            if len(d_ref.shape) == 3:
                for k in range(d_ref.shape[0]):
                    o_ref[k] += _mm_tn(d_ref[k], hv)
            else:
                o_ref[...] += _mm_tn(d_ref[...], hv)

    def in_spec(p):
        if p.ndim == 3:
            return pl.BlockSpec((p.shape[0], TK_GRAD, p.shape[2]), lambda kk: (0, kk, 0))
        return pl.BlockSpec((TK_GRAD, p.shape[1]), lambda kk: (kk, 0))

    out_shapes = [(p.shape[0], p.shape[2], D) if p.ndim == 3 else (p.shape[1], D) for p in pieces]
    return pl.pallas_call(
        body, name=name, grid=(s // TK_GRAD,),
        in_specs=[pl.BlockSpec((TK_GRAD, D), lambda kk: (kk, 0))] + [in_spec(p) for p in pieces],
        out_specs=[pl.BlockSpec(sh, lambda kk, nd=len(sh): (0,) * nd) for sh in out_shapes],
        out_shape=[jax.ShapeDtypeStruct(sh, F32) for sh in out_shapes],
        compiler_params=_params(("arbitrary",)),
    )(h, *pieces)


W_IN_ROWS = 1416


def _pad_win_t(w_in_t):
    z = lambda n: jnp.zeros((n, w_in_t.shape[-1]), w_in_t.dtype)
    assert (P_GB + 512 - P_GA, P_CKV + KV_LORA - P_CQ) == (O_MERGE - O_GA, O_KPE - O_CQ)

    def rows(lo, hi):
        per = W_IN_ROWS
        cuts = [lo] + [b for b in range(per, N_CHIPS * per, per) if lo < b < hi] + [hi]
        return [w_in_t[a // per, a % per:a % per + (b - a)] for a, b in zip(cuts[:-1], cuts[1:])]

    return jnp.concatenate(rows(O_MERGE, O_MERGE + 2048) + rows(O_GA, O_MERGE) + rows(O_CQ, O_KPE) + [z(64)]
                           + rows(O_KPE, O_KPE + QK_ROPE) + [z(32)], axis=0)


def _pad_wuq(w_uq):
    rows = w_uq.shape[0]
    w = w_uq.reshape(rows, NH, QK_NOPE + QK_ROPE)
    return jnp.pad(w, ((0, 0), (0, 0), (0, LANE - QK_NOPE - QK_ROPE))).reshape(rows, NH * LANE)


def _unpad_wuq(g):
    return g.reshape(Q_LORA, NH, LANE)[:, :, :QK_NOPE + QK_ROPE].reshape(Q_LORA, NH * (QK_NOPE + QK_ROPE))


def _pad_wukv(w_ukv):
    heads = w_ukv.shape[1] // (QK_NOPE + V_DIM)
    w = w_ukv.reshape(KV_LORA, heads, QK_NOPE + V_DIM)
    w_k = jnp.pad(w[:, :, :QK_NOPE], ((0, 0), (0, 0), (0, LANE - QK_NOPE))).reshape(KV_LORA, heads * LANE)
    wv = w[:, :, QK_NOPE:].reshape(KV_LORA, heads // 2, 2, 1, V_DIM)
    eye = jnp.eye(2, dtype=w.dtype).reshape(1, 1, 2, 2, 1)
    return w_k, (wv * eye).reshape(KV_LORA, heads * LANE)


def _unpad_wukv(gk, gv):
    gk = gk.reshape(KV_LORA, NH, LANE)[:, :, :QK_NOPE]
    gv = gv.reshape(KV_LORA, NH // 2, 2, 2, V_DIM)
    gv = jnp.stack([gv[:, :, 0, 0], gv[:, :, 1, 1]], axis=2).reshape(KV_LORA, NH, V_DIM)
    return jnp.concatenate([gk, gv], axis=-1).reshape(KV_LORA, NH * (QK_NOPE + V_DIM))


def _local_step(x, tgt, g_pre, w_in_t, b_gate, g_q, g_kv, lb_logits, g_hgrn, g_post, weights, exchange=None):
    s = x.shape[0]
    w_in_p = _pad_win_t(w_in_t)
    rc, rs1, rs2 = _rope_tables(s)
    g_hg = jnp.tile(g_hgrn, (1, NH))

    proj, h = _front_fwd(x, g_pre, w_in_p, weights.tokens)
    w_uq_p, w_k_p, w_v_p = weights.qkv(h)
    q, k, vv = _qkv_fwd(proj, g_q, g_kv, w_uq_p, w_k_p, w_v_p, rc, rs1, rs2)
    attn, lse = _attn_fwd(q, k, vv)
    o_raw, states = _hgrn_fwd(proj, lb_logits)
    wa, wb, w_out = weights.mid(o_raw)
    (loss, dout, dattn, dga, dor, dgb, dmg, d_wout, d_wa, d_wb, d_gpost, d_bgate, d_ghg) = _mid(
        proj, attn, o_raw, x, tgt, g_hg, b_gate, g_post, wa, wb, w_out)
    w_mg, w_ga, w_gb = _win_grad(h, [dmg, dga, dgb], "win_grad_mid")
    dh3, d_lbl = _hgrn_bwd(proj, lb_logits, states, dor)
    (w_h3,) = _win_grad(h, [dh3], "win_grad_hgrn")
    d_win_rest = jnp.concatenate([w_ga, w_h3[0], w_h3[1], w_h3[2], w_gb, w_mg], axis=0)
    early = dict(w_in_rest=d_win_rest, w_branch_a=d_wa, w_branch_b=d_wb, w_out=d_wout)
    token = exchange.start_early(early) if exchange else jnp.zeros((8, LANE), F32)
    dq, dk, dvv = _attn_bwd(q, k, vv, attn, dattn, lse, token)
    dcq, dckv, dkpe, d_wuq_p, d_wk_p, d_wv_p, d_gq, d_gkv = _qkv_bwd(proj, dq, dk, dvv, g_q, g_kv, w_uq_p, w_k_p, w_v_p, rc, rs1, rs2)
    w_cq, w_ckv, w_kpe = _win_grad(h, [dcq, dckv, dkpe], "win_grad_qkv")
    d_win_qkv = jnp.concatenate([w_cq, w_ckv, w_kpe[64:64 + QK_ROPE]], axis=0)
    late = dict(w_in_qkv=d_win_qkv, w_uq=_unpad_wuq(d_wuq_p), w_ukv=_unpad_wukv(d_wk_p, d_wv_p))
    token = exchange.start_late(late) if exchange else jnp.zeros((8, LANE), F32)
    grad_x, d_gpre = _front_bwd(x, dout, dmg, dga, dh3, dgb, dcq, dckv, dkpe, g_pre, w_in_p, token)
    vec_grads = dict(g_pre=d_gpre, b_gate=d_bgate, g_q=d_gq, g_kv=d_gkv, lb_logits=d_lbl, g_hgrn=d_ghg, g_post=d_gpost)
    return loss, grad_x, dict(early, **late), vec_grads


SHARD_SHAPES = (("w_in", (1416, 1024)), ("w_uq", (192, 768)), ("w_ukv", (256, 256)), ("w_branch_a", (512, 256)),
                ("w_branch_b", (512, 256)), ("w_out", (256, 1024)))
BIG = tuple(n for n, _ in SHARD_SHAPES)
ROW_SHARDED = ("w_in", "w_uq", "w_out")
N_CHIPS = 4
QKV_ROWS = Q_LORA + KV_LORA + QK_ROPE
W_IN_FORWARD_CUT = 704


def _to_block(name, a):
    return a[0].T if name == "w_in" else a[0]


def _from_block(name, a):
    return a.T[None] if name == "w_in" else a[None]
VEC_ROWS = (("g_pre", 0, 1024), ("b_gate", 1, 2048), ("g_q", 2, 768), ("g_kv", 3, 256), ("g_hgrn", 6, 64), ("g_post", 7, 1024))
VEC_LB_ROW = 4
VEC_SHAPE = (8, 2048)


def _split_by_chip(name, g):
    a, b = dict(SHARD_SHAPES)[name]
    return g.reshape(N_CHIPS, a, b) if name in ROW_SHARDED else g.reshape(a, N_CHIPS, b).transpose(1, 0, 2)


MESH = pl.DeviceIdType.MESH
HBM = pl.BlockSpec(memory_space=pltpu.HBM)


def _mesh_place():
    x, y, c = lax.axis_index("x"), lax.axis_index("y"), lax.axis_index("c")
    return x, y, c, 2 * x + y, [(1 - x, y), (x, 1 - y), (1 - x, 1 - y)]


def _remote(src, dst, send_sems, recv_sems, k, to):
    return pltpu.make_async_remote_copy(src_ref=src, dst_ref=dst, send_sem=send_sems.at[k], recv_sem=recv_sems.at[k],
                                        device_id=to, device_id_type=MESH)


def _gather_w_in(shard):
    a, b = shard.shape
    cut = W_IN_FORWARD_CUT

    def body(src, out, ici_send, ici_recv, d2d_send, d2d_recv, local_sem):
        x, y, c = lax.axis_index("x"), lax.axis_index("y"), lax.axis_index("c")
        me, xn, yn, dg = 2 * x + y, 2 * (1 - x) + y, 2 * x + (1 - y), 2 * (1 - x) + (1 - y)
        to_x, to_y, sibling = (1 - x, y, c), (x, 1 - y, c), (x, y, 1 - c)
        first, rest = pl.ds(0, cut), pl.ds(cut, a - cut)

        whole = lambda ref, which: ref.at[:, pl.ds(pl.multiple_of(which * (b // 2), b // 2), b // 2)]
        own = pltpu.make_async_copy(src, out.at[me], local_sem)
        own.start()
        sends = [_remote(whole(src, c), whole(out.at[me], c), ici_send, ici_recv, 0, to_x),
                 _remote(whole(src, c), whole(out.at[me], c), ici_send, ici_recv, 1, to_y)]
        for cp in sends:
            cp.start()

        def landed(slot, rows, k, d2d_k, src_dev):
            piece = whole(out.at[slot], c) if rows is None else out.at[slot].at[rows, pl.ds(pl.multiple_of(c * (b // 2), b // 2), b // 2)]
            _remote(piece, piece, ici_send, ici_recv, k, src_dev).wait_recv()
            cp = _remote(piece, piece, d2d_send, d2d_recv, d2d_k, sibling)
            cp.start()
            sends.append(cp)
            return piece

        def pass_on(slot, rows, k, to):
            piece = out.at[slot].at[rows, pl.ds(pl.multiple_of(c * (b // 2), b // 2), b // 2)]
            cp = _remote(piece, piece, ici_send, ici_recv, k, to)
            cp.start()
            sends.append(cp)

        landed(xn, None, 0, 0, to_x)
        pass_on(xn, first, 2, to_y)
        landed(yn, None, 1, 1, to_y)
        pass_on(yn, rest, 3, to_x)
        landed(dg, first, 2, 2, to_y)
        landed(dg, rest, 3, 3, to_x)
        other = pl.ds(pl.multiple_of((1 - c) * (b // 2), b // 2), b // 2)
        for d2d_k, (slot, rows) in enumerate(((xn, None), (yn, None), (dg, first), (dg, rest))):
            piece = out.at[slot].at[:, other] if rows is None else out.at[slot].at[rows, other]
            _remote(piece, piece, d2d_send, d2d_recv, d2d_k, sibling).wait_recv()
        for cp in sends:
            cp.wait_send()
        own.wait()

    sems = pltpu.SemaphoreType.DMA((4,))
    return pl.pallas_call(
        body, name="gather_w_in", in_specs=[HBM], out_specs=HBM,
        out_shape=jax.ShapeDtypeStruct((N_CHIPS, a, b), shard.dtype),
        scratch_shapes=[sems, sems, sems, sems, pltpu.SemaphoreType.DMA],
        compiler_params=pltpu.CompilerParams(has_side_effects=True),
    )(shard)


def _sibling_exchange(srcs, name, after=None):
    n = len(srcs)
    extra = [] if after is None else [after]

    def body(*refs):
        src_refs, outs = refs[:n], refs[n + len(extra):2 * n + len(extra)]
        send_sems, recv_sems = refs[2 * n + len(extra):]
        sibling = (lax.axis_index("x"), lax.axis_index("y"), 1 - lax.axis_index("c"))
        copies = [_remote(src_refs[k], outs[k], send_sems, recv_sems, k, sibling) for k in range(n)]
        for cp in copies:
            cp.start()
        for cp in copies:
            cp.wait()

    sems = pltpu.SemaphoreType.DMA((n,))
    return pl.pallas_call(
        body, name=name, in_specs=[HBM] * n + [pl.BlockSpec(memory_space=pl.ANY)] * len(extra), out_specs=[HBM] * n,
        out_shape=[jax.ShapeDtypeStruct(s.shape, s.dtype) for s in srcs],
        scratch_shapes=[sems, sems],
        compiler_params=pltpu.CompilerParams(has_side_effects=True),
    )(*srcs, *extra)


SEM = pl.BlockSpec(memory_space=pltpu.SEMAPHORE)
DATAFLOW = pltpu.SideEffectType.DATAFLOW_SIDE_EFFECTING


def _exchange_copies(srcs, to_first, src_refs, land_refs, send_sems, recv_sems):
    x, y, c, me, chips = _mesh_place()
    n = len(srcs)
    sends, recvs = [], []
    for k in range(n):
        if k in to_first:
            base = 3 * n + 4 * to_first.index(k)
            sends.append((me != 0, pltpu.make_async_remote_copy(
                src_ref=src_refs[k], dst_ref=land_refs[k].at[me], send_sem=send_sems.at[base], recv_sem=recv_sems.at[base + me],
                device_id=(0, 0, c), device_id_type=MESH)))
            for s in range(1, N_CHIPS):
                recvs.append((me == 0, pltpu.make_async_remote_copy(
                    src_ref=src_refs[k], dst_ref=land_refs[k].at[s], send_sem=send_sems.at[base], recv_sem=recv_sems.at[base + s],
                    device_id=(s // 2, s % 2, c), device_id_type=MESH)))
        else:
            slab = (lambda t, k=k: src_refs[k]) if srcs[k].ndim == 2 else (lambda t, k=k: src_refs[k].at[t])
            for j, (px, py) in enumerate(chips):
                sends.append((None, _remote(slab(2 * px + py), land_refs[k].at[me], send_sems, recv_sems, 3 * k + j, (px, py, c))))
                recvs.append((None, _remote(slab(me), land_refs[k].at[2 * px + py], send_sems, recv_sems, 3 * k + j, (px, py, c))))
    return sends, recvs


def _when(pred, fn):
    if pred is None:
        fn()
    else:
        pl.when(pred)(fn)


def _exchange_start(srcs, to_first, name, after=None):
    n = len(srcs)
    n_sems = 3 * n + 4 * len(to_first)
    lands = [lax.empty((N_CHIPS,) + s.shape[-2:], s.dtype) for s in srcs]
    extra = [] if after is None else [after]

    def body(*refs):
        src_refs, land_refs = refs[:n], refs[n:2 * n]
        send_sems, recv_sems, token = refs[2 * n + len(extra)], refs[2 * n + len(extra) + 1], refs[-1]
        sends, _ = _exchange_copies(srcs, to_first, src_refs, land_refs, send_sems, recv_sems)
        for pred, cp in sends:
            _when(pred, cp.start)
        token[...] = jnp.zeros_like(token)

    hbm = lambda a: pltpu.HBM(a.shape, a.dtype)
    res = pl.pallas_call(
        body, name=name,
        out_shape=[pltpu.SemaphoreType.DMA((n_sems,)), pltpu.SemaphoreType.DMA((n_sems,))] + [hbm(a) for a in srcs + lands]
        + [jax.ShapeDtypeStruct((8, LANE), F32)],
        in_specs=[HBM] * (2 * n) + [pl.BlockSpec(memory_space=pl.ANY)] * len(extra),
        out_specs=[SEM, SEM] + [HBM] * (2 * n) + [pl.BlockSpec(memory_space=pltpu.VMEM)],
        input_output_aliases={i: 2 + i for i in range(2 * n)},
        compiler_params=pltpu.CompilerParams(has_side_effects=DATAFLOW),
    )(*[pltpu.with_memory_space_constraint(a, pltpu.HBM) for a in srcs + lands], *extra)
    return res[:-1], res[-1]


def _exchange_wait(srcs, to_first, started, after, name):
    n = len(srcs)
    send_sems, recv_sems, thru = started[0], started[1], started[2:]

    def body(*refs):
        src_refs, land_refs, send_ref, recv_ref = refs[:n], refs[n:2 * n], refs[2 * n], refs[2 * n + 1]
        sends, recvs = _exchange_copies(srcs, to_first, src_refs, land_refs, send_ref, recv_ref)
        for pred, cp in sends:
            _when(pred, cp.wait_send)
        for pred, cp in recvs:
            _when(pred, cp.wait_recv)

    res = pl.pallas_call(
        body, name=name, out_shape=[pltpu.HBM(a.shape, a.dtype) for a in thru],
        in_specs=[HBM] * (2 * n) + [SEM, SEM, pl.BlockSpec(memory_space=pl.ANY)], out_specs=[HBM] * (2 * n),
        input_output_aliases={i: i for i in range(2 * n)},
        compiler_params=pltpu.CompilerParams(has_side_effects=DATAFLOW),
    )(*thru, send_sems, recv_sems, after)
    return res[n:]


ROW_TILE = 256
COL_TILE = 256


def _block_tiling(a, b):
    if a <= ROW_TILE or a % ROW_TILE == 0:
        ta = min(a, ROW_TILE)
        return a // ta, (ta, b), lambda i: (i, 0)
    return b // COL_TILE, (a, COL_TILE), lambda i: (0, i)


def _sum_landed(land, own, name, first_land=None, first_own=None):
    _, a, b = land.shape
    steps, tile, at = _block_tiling(a, b)
    extra = first_land is not None

    def body(*refs):
        p_ref, own_ref, o_ref = refs[0], refs[1], refs[-1]
        me = 2 * lax.axis_index("x") + lax.axis_index("y")
        own = own_ref[...].astype(F32)
        slot = lambda t: jnp.where(me == t, own, p_ref[t].astype(F32))
        o_ref[...] = ((slot(0) + slot(1)) + slot(2)) + slot(3)
        if extra:
            fp_ref, fo_ref = refs[2], refs[3]
            r = fo_ref.shape[0]

            @pl.when(me == 0)
            def _():
                f = lambda t: fp_ref[t].astype(F32)
                rows = pl.ds(pl.multiple_of(lax.axis_index("c") * r, 8), r)
                o_ref[rows, :] += ((fo_ref[...].astype(F32) + f(1)) + f(2)) + f(3)

    in_specs = [pl.BlockSpec((N_CHIPS,) + tile, lambda i: (0,) + at(i)), pl.BlockSpec(tile, at)]
    args = [land, own]
    if extra:
        r = first_own.shape[0]
        assert tile[0] == a, "the extra rows need whole columns in a step"
        in_specs += [pl.BlockSpec((N_CHIPS, r, tile[1]), lambda i: (0,) + at(i)), pl.BlockSpec((r, tile[1]), at)]
        args += [first_land, first_own]
    return pl.pallas_call(
        body, name=name, grid=(steps,), in_specs=in_specs, out_specs=pl.BlockSpec(tile, at),
        out_shape=jax.ShapeDtypeStruct((a, b), F32), compiler_params=_params(("parallel",)),
    )(*args)


def _sum_landed_small(lands, owns, name):
    n = len(lands)

    def body(*refs):
        me = 2 * lax.axis_index("x") + lax.axis_index("y")
        for p_ref, own_ref, o_ref in zip(refs[:n], refs[n:2 * n], refs[2 * n:]):
            own = own_ref[...].astype(F32)
            slot = lambda t: jnp.where(me == t, own, p_ref[t].astype(F32))
            o_ref[...] = ((slot(0) + slot(1)) + slot(2)) + slot(3)

    return pl.pallas_call(body, name=name, out_shape=[jax.ShapeDtypeStruct(o.shape, F32) for o in owns],
                          compiler_params=_params(()))(*lands, *owns)


def _adamw_small(mine, theirs, states, name):
    n = len(mine)

    def body(*refs):
        ins, outs = refs[:5 * n], refs[5 * n:]
        for k in range(n):
            a_ref, b_ref, w_ref, m_ref, v_ref = ins[5 * k:5 * k + 5]
            g = a_ref[...] + b_ref[...]
            outs[4 * k][...] = g
            outs[4 * k + 1][...], outs[4 * k + 2][...], outs[4 * k + 3][...] = _adamw_math(g, w_ref[...], m_ref[...], v_ref[...])

    args = [t for k in range(n) for t in (mine[k], theirs[k], *states[k])]
    res = pl.pallas_call(body, name=name, out_shape=[jax.ShapeDtypeStruct(mine[k].shape, F32) for k in range(n) for _ in range(4)],
                         compiler_params=_params(()))(*args)
    return [tuple(res[4 * k:4 * k + 4]) for k in range(n)]


def _add_cast(a, b, name):
    def body(a_ref, b_ref, o_ref):
        o_ref[...] = (a_ref[...] + b_ref[...]).astype(BF16)

    return pl.pallas_call(body, name=name, out_shape=jax.ShapeDtypeStruct(a.shape, BF16),
                          compiler_params=_params(()))(a, b)


class _LaterWeights:
    MID = ("w_branch_a", "w_branch_b", "w_out")

    def __init__(self, blocks, after):
        self.qkv_blocks = [_pad_wuq(blocks["w_uq"]), *_pad_wukv(blocks["w_ukv"])]
        self.mid_blocks = [blocks[n] for n in self.MID]
        self.qkv_started, t1 = _exchange_start(self.qkv_blocks, (), "weights_qkv_start", after)
        self.mid_started, t2 = _exchange_start(self.mid_blocks, (), "weights_mid_start", after)
        self.tokens = [t1, t2]

    @staticmethod
    def _whole(blocks, rows_sharded, started, after, name):
        landed = _exchange_wait(blocks, (), started, after, name)
        me = 2 * lax.axis_index("x") + lax.axis_index("y")
        out = []
        for block, land, by_rows in zip(blocks, landed, rows_sharded):
            w = lax.dynamic_update_index_in_dim(land, block, me, 0)
            a, b = block.shape
            out.append(w.reshape(N_CHIPS * a, b) if by_rows else w.transpose(1, 0, 2).reshape(a, N_CHIPS * b))
        return out

    def qkv(self, after):
        return self._whole(self.qkv_blocks, (True, False, False), self.qkv_started, after, "weights_qkv_wait")

    def mid(self, after):
        return self._whole(self.mid_blocks, (False, False, True), self.mid_started, after, "weights_mid_wait")


class _GradExchange:
    EARLY = ("w_in", "w_branch_a", "w_branch_b", "w_out")
    LATE = ("w_uq", "w_ukv")

    def __init__(self, state):
        self.state = state
        self.outs = {}

    @staticmethod
    def _own(slabs):
        return lax.dynamic_index_in_dim(slabs, 2 * lax.axis_index("x") + lax.axis_index("y"), axis=0, keepdims=False)

    def start_early(self, g):
        full = jnp.concatenate([jnp.zeros((QKV_ROWS, D), F32), g["w_in_rest"]], axis=0)
        g = dict(g, w_in=full)
        self.early = [_split_by_chip(n, g[n]).astype(BF16) for n in self.EARLY]
        self.early_started, token = _exchange_start(self.early, (), "grads_early_start")
        return token

    def start_late(self, g):
        self.early_landed = _exchange_wait(self.early, (), self.early_started, g["w_uq"], "grads_early_wait")
        half = QKV_ROWS // 2
        c = lax.axis_index("c")
        mine = lax.dynamic_slice_in_dim(g["w_in_qkv"], c * half, half, axis=0)
        (theirs,) = _sibling_exchange([lax.dynamic_slice_in_dim(g["w_in_qkv"], (1 - c) * half, half, axis=0)], "sibling_qkv_rows")
        self.late = [_split_by_chip(n, g[n]).astype(BF16) for n in self.LATE] + [_add_cast(mine, theirs, "add_qkv_rows")]
        self.late_started, token = _exchange_start(self.late, (2,), "grads_late_start")
        names = self.EARLY[1:]
        mine = _sum_landed_small(self.early_landed[1:], [self._own(slabs) for slabs in self.early[1:]], "sum_early")
        theirs = _sibling_exchange(mine, "sibling_early", after=token)
        for n, out in zip(names, _adamw_small(mine, theirs, [self.state[n] for n in names], "adamw_early")):
            self.outs[n] = out
        return self.outs[names[-1]][0]

    def finish(self, after):
        late_landed = _exchange_wait(self.late, (2,), self.late_started, after, "grads_late_wait")
        sums = {"w_in": _sum_landed(self.early_landed[0], self._own(self.early[0]), "sum_w_in",
                                    first_land=late_landed[2], first_own=self.late[2])}
        small = _sum_landed_small(late_landed[:2], [self._own(slabs) for slabs in self.late[:2]], "sum_late")
        sums.update(zip(self.LATE, small))
        return sums


def _adamw_math(g, w, m, v):
    nm = ADAM_B1 * m + (1.0 - ADAM_B1) * g
    nv = ADAM_B2 * v + (1.0 - ADAM_B2) * (g * g)
    m_hat = nm / (1.0 - ADAM_B1 ** ADAM_STEP)
    v_hat = nv / (1.0 - ADAM_B2 ** ADAM_STEP)
    return -ADAM_LR * (m_hat / (jnp.sqrt(v_hat) + ADAM_EPS) + ADAM_WD * w), nm, nv


def _adamw(p_mine, p_sibling, w, m, v, name):
    a, b = p_mine.shape
    steps, tile, at = _block_tiling(a, b)

    def body(a_ref, b_ref, w_ref, m_ref, v_ref, g_ref, d_ref, nm_ref, nv_ref):
        g = a_ref[...] + b_ref[...]
        g_ref[...] = g
        d_ref[...], nm_ref[...], nv_ref[...] = _adamw_math(g, w_ref[...], m_ref[...], v_ref[...])

    spec = pl.BlockSpec(tile, at)
    sds = jax.ShapeDtypeStruct((a, b), F32)
    return pl.pallas_call(
        body, name=name, grid=(steps,), in_specs=[spec] * 5, out_specs=[spec] * 4, out_shape=[sds] * 4,
        compiler_params=_params(("parallel",)),
    )(p_mine, p_sibling, w, m, v)


LOSS_AT = (2, 1024)


def _vec_pack(vg, loss):
    names = [n for n, _, _ in VEC_ROWS]

    def body(*refs):
        o_ref = refs[-1]
        lb_ref, loss_ref = refs[len(names)], refs[len(names) + 1]
        o_ref[...] = jnp.zeros_like(o_ref)
        o_ref[LOSS_AT[0]:LOSS_AT[0] + 1, LOSS_AT[1]:LOSS_AT[1] + LANE] = jnp.broadcast_to(loss_ref[...], (1, LANE))
        for (name, row, size), ref in zip(VEC_ROWS, refs):
            if name == "g_hgrn":
                r = lax.broadcasted_iota(jnp.int32, (NH * V_DIM, LANE), 0)
                c = lax.broadcasted_iota(jnp.int32, (NH * V_DIM, LANE), 1)
                fold = ((r % V_DIM) == c).astype(F32)
                o_ref[row:row + 1, 0:LANE] = jnp.dot(ref[...], fold, precision=HIGHEST, preferred_element_type=F32)
            else:
                o_ref[row:row + 1, 0:size] = ref[...]
        o_ref[VEC_LB_ROW:VEC_LB_ROW + 2, 0:512] = lb_ref[...]

    return pl.pallas_call(body, name="vec_pack", out_shape=jax.ShapeDtypeStruct(VEC_SHAPE, F32))(
        *[vg[n] for n in names], vg["lb_logits"], loss)


def _adamw_vec(p_mine, p_sibling, w, m, v):
    names = [n for n, _, _ in VEC_ROWS] + ["lb_logits"]
    k = len(names)

    def body(a_ref, b_ref, *refs):
        ins, outs = refs[:3 * k], refs[3 * k:]
        at = (slice(LOSS_AT[0], LOSS_AT[0] + 1), slice(LOSS_AT[1], LOSS_AT[1] + LANE))
        outs[-1][...] = a_ref[at] + b_ref[at]
        for i, name in enumerate(names):
            if name == "lb_logits":
                rows, cols = slice(VEC_LB_ROW, VEC_LB_ROW + 2), slice(0, 512)
            else:
                _, row, size = VEC_ROWS[i]
                rows, cols = slice(row, row + 1), slice(0, size)
            g = a_ref[rows, cols] + b_ref[rows, cols]
            d, nm, nv = _adamw_math(g, ins[i][...], ins[k + i][...], ins[2 * k + i][...])
            for o_ref, val in zip(outs[4 * i:4 * i + 4], (g, d, nm, nv)):
                o_ref[...] = val

    shapes = [jax.ShapeDtypeStruct(w[n].shape, F32) for n in names for _ in range(4)] + [jax.ShapeDtypeStruct((1, LANE), F32)]
    res = pl.pallas_call(body, name="adamw_vec", out_shape=shapes)(
        p_mine, p_sibling, *[w[n] for n in names], *[m[n] for n in names], *[v[n] for n in names])
    return [{n: res[4 * i + j] for i, n in enumerate(names)} for j in range(4)], res[-1]


WEIGHTS = ("g_pre", "w_in", "b_gate", "g_q", "w_uq", "g_kv", "w_ukv", "lb_logits", "g_hgrn", "w_branch_a", "w_branch_b", "w_out", "g_post")


def kernel(x, g_pre, w_in, b_gate, g_q, w_uq, g_kv, w_ukv, lb_logits, g_hgrn, w_branch_a, w_branch_b, w_out, g_post, loss_target, m_g_pre, m_w_in, m_b_gate, m_g_q, m_w_uq, m_g_kv, m_w_ukv, m_lb_logits, m_g_hgrn, m_w_branch_a, m_w_branch_b, m_w_out, m_g_post, v_g_pre, v_w_in, v_b_gate, v_g_q, v_w_uq, v_g_kv, v_w_ukv, v_lb_logits, v_g_hgrn, v_w_branch_a, v_w_branch_b, v_w_out, v_g_post):
    w = dict(g_pre=g_pre, w_in=w_in, b_gate=b_gate, g_q=g_q, w_uq=w_uq, g_kv=g_kv, w_ukv=w_ukv, lb_logits=lb_logits, g_hgrn=g_hgrn,
             w_branch_a=w_branch_a, w_branch_b=w_branch_b, w_out=w_out, g_post=g_post)
    m = dict(g_pre=m_g_pre, w_in=m_w_in, b_gate=m_b_gate, g_q=m_g_q, w_uq=m_w_uq, g_kv=m_g_kv, w_ukv=m_w_ukv, lb_logits=m_lb_logits,
             g_hgrn=m_g_hgrn, w_branch_a=m_w_branch_a, w_branch_b=m_w_branch_b, w_out=m_w_out, g_post=m_g_post)
    v = dict(g_pre=v_g_pre, w_in=v_w_in, b_gate=v_b_gate, g_q=v_g_q, w_uq=v_w_uq, g_kv=v_g_kv, w_ukv=v_w_ukv, lb_logits=v_lb_logits,
             g_hgrn=v_g_hgrn, w_branch_a=v_w_branch_a, w_branch_b=v_w_branch_b, w_out=v_w_out, g_post=v_g_post)
    blocks = {n: _to_block(n, w[n]).astype(BF16) for n in BIG}
    w_in_all = _gather_w_in(blocks["w_in"])
    weights = _LaterWeights(blocks, w_in_all)
    state = {n: [_to_block(n, t[n]) for t in (w, m, v)] for n in BIG}
    exchange = _GradExchange(state)
    loss, grad_x, _, vec_grads = _local_step(
        x[0], loss_target[0], g_pre, w_in_all, b_gate, g_q, g_kv, lb_logits, g_hgrn, g_post, weights, exchange)
    vec = _vec_pack(vec_grads, loss)
    vec_started, token = _exchange_start([vec], (), "vec_start")
    sums = exchange.finish(token)
    rest = tuple(sums)
    (vec_landed,) = _exchange_wait([vec], (), vec_started, sums[rest[-1]], "vec_wait")
    mine = [sums[n] for n in rest] + [_sum_landed(vec_landed, vec, "sum_vec")]
    theirs = _sibling_exchange(mine, "sibling_grads")
    done = dict(exchange.outs)
    done["w_in"] = _adamw(mine[0], theirs[0], *state["w_in"], "adamw_w_in")
    small = _adamw_small(mine[1:-1], theirs[1:-1], [state[n] for n in rest[1:]], "adamw_late")
    done.update(zip(rest[1:], small))
    outs = [{}, {}, {}, {}]
    for n in BIG:
        for o, val in zip(outs, done[n]):
            o[n] = _from_block(n, val)
    vec_outs, total = _adamw_vec(mine[-1], theirs[-1], w, m, v)
    for o, vals in zip(outs, vec_outs):
        o.update(vals)
    return (total[0, 0], grad_x[None], *[o[n] for o in outs for n in WEIGHTS])
```

```python
import math

import numpy as np
import jax
import jax.numpy as jnp
from jax import lax
from jax.experimental import pallas as pl
from jax.experimental.pallas import tpu as pltpu

F32 = jnp.float32
BF16 = jnp.bfloat16
HIGHEST = lax.Precision.HIGHEST

D = 1024
NH = 8
QK_NOPE, QK_ROPE, V_DIM = 64, 32, 64
Q_LORA, KV_LORA = 768, 256
CHUNK = 64
HG_BLOCK = 32
EPS = 1e-6
LANE = 128
P_MERGE, P_GA, P_HQ, P_HF, P_HI, P_GB, P_CQ, P_CKV, P_KPE = 0, 2048, 2560, 3072, 3584, 4096, 4608, 5376, 5632
D_P = 5760
O_CQ, O_CKV, O_KPE, O_GA, O_HQ, O_HF, O_HI, O_GB, O_MERGE = 0, 768, 1024, 1056, 1568, 2080, 2592, 3104, 3616

TM = 512
TM_MID = 256
TQ = 1024
ONES_LANE = (LANE - 1, 0)
TH = 256
HG_PAIRS = 4
VMEM_LIMIT = 56 * 1024 * 1024

ADAM_LR, ADAM_B1, ADAM_B2, ADAM_EPS, ADAM_WD, ADAM_STEP = 0.001, 0.9, 0.999, 1e-08, 0.01, 10

NT_DIMS = (((1,), (1,)), ((), ()))
TN_DIMS = (((0,), (0,)), ((), ()))


def _params(sem):
    return pltpu.CompilerParams(dimension_semantics=sem, vmem_limit_bytes=VMEM_LIMIT)


def _mm(a, b):
    return jnp.dot(a, b, preferred_element_type=F32)


def _mm_nt(a, b):
    return lax.dot_general(a, b, NT_DIMS, preferred_element_type=F32)


def _mm_tn(a, b):
    return lax.dot_general(a, b, TN_DIMS, preferred_element_type=F32)


def _sigmoid(z):
    return jax.nn.sigmoid(z)


def _rope(v, c, s1, s2):
    return v * c + pltpu.roll(v, 112, 1) * s1 + pltpu.roll(v, 16, 1) * s2


def _rope_t(dy, c, s1, s2):
    return dy * c + pltpu.roll(dy * s1, 16, 1) + pltpu.roll(dy * s2, 112, 1)


def _rope_tables(s):
    f32 = np.float32
    inv = f32(10000.0) ** (-np.arange(0, QK_ROPE, 2, dtype=f32) / f32(QK_ROPE))
    ang = np.arange(s, dtype=f32)[:, None] * inv[None, :]
    cos, sin = np.cos(ang).astype(f32), np.sin(ang).astype(f32)
    z64, z32, o64, o32 = np.zeros((s, 64), f32), np.zeros((s, 32), f32), np.ones((s, 64), f32), np.ones((s, 32), f32)
    z16 = np.zeros((s, 16), f32)
    c = np.concatenate([o64, cos, cos, o32], axis=1)
    s1 = np.concatenate([z64, -sin, z16, z32], axis=1)
    s2 = np.concatenate([z64, z16, sin, z32], axis=1)
    return jnp.asarray(c), jnp.asarray(s1), jnp.asarray(s2)


W_IN_RUNS = ((O_MERGE, 2048, P_MERGE), (O_GA, O_MERGE - O_GA, P_GA), (O_CQ, O_KPE - O_CQ, P_CQ))


def _kpe_block(w_in_t):
    z = lambda n: jnp.zeros((n, w_in_t.shape[1]), w_in_t.dtype)
    return jnp.concatenate([z(64), w_in_t[O_KPE:O_KPE + QK_ROPE], z(32)], axis=0)


def _front_fwd(x, g_pre, w_in_t, w_kpe, tokens=()):
    s = x.shape[0]
    tokens = list(tokens)

    def body(x_ref, g_ref, w_ref, k_ref, *refs):
        o_ref, h_ref = refs[len(tokens):]
        xv = x_ref[...]
        r = lax.rsqrt(jnp.mean(xv * xv, axis=-1, keepdims=True) + EPS)
        h = ((xv * r) * g_ref[...]).astype(BF16)
        h_ref[...] = h
        for row, rows, col in W_IN_RUNS:
            o_ref[:, col:col + rows] = _mm_nt(h, w_ref[row:row + rows, :])
        o_ref[:, P_KPE:P_KPE + LANE] = _mm_nt(h, k_ref[...])

    full = lambda a: pl.BlockSpec(a.shape, lambda i: (0,) * a.ndim)
    return pl.pallas_call(
        body, name="front_fwd", grid=(s // TM,),
        in_specs=[pl.BlockSpec((TM, D), lambda i: (i, 0)), pl.BlockSpec((1, D), lambda i: (0, 0)), full(w_in_t), full(w_kpe)]
        + [pl.BlockSpec((8, LANE), lambda i: (0, 0))] * len(tokens),
        out_specs=[pl.BlockSpec((TM, D_P), lambda i: (i, 0)), pl.BlockSpec((TM, D), lambda i: (i, 0))],
        out_shape=[jax.ShapeDtypeStruct((s, D_P), F32), jax.ShapeDtypeStruct((s, D), BF16)],
        compiler_params=_params(("parallel",)),
    )(x, g_pre, w_in_t, w_kpe, *tokens)


def _norm_rows(v, g):
    r = lax.rsqrt(jnp.mean(v * v, axis=-1, keepdims=True) + EPS)
    return (v * r) * g, r


def _qkv_fwd(proj, g_q, g_kv, w_uq_p, w_k_p, w_v_p, rc, rs1, rs2):
    s = proj.shape[0]

    def body(cq_ref, ckv_ref, kpe_ref, gq_ref, gkv_ref, wq_ref, wk_ref, wv_ref, c_ref, s1_ref, s2_ref, q_ref, k_ref, v_ref):
        c, s1, s2 = c_ref[...], s1_ref[...], s2_ref[...]
        cqn, _ = _norm_rows(cq_ref[...], gq_ref[...])
        ckvn, _ = _norm_rows(ckv_ref[...], gkv_ref[...])
        ckvn = ckvn.astype(BF16)
        qf = _mm(cqn.astype(BF16), wq_ref[...])
        kf = _mm(ckvn, wk_ref[...])
        vf = _mm(ckvn, wv_ref[...])
        kpe = _rope(kpe_ref[...], c, s1, s2)
        lane = lax.broadcasted_iota(jnp.int32, (TM, LANE), 1)
        for h in range(NH):
            blk = slice(h * LANE, (h + 1) * LANE)
            q_ref[h] = _rope(qf[:, blk], c, s1, s2).astype(BF16)
            k_ref[h] = (kf[:, blk] + kpe).astype(BF16)
            v_ref[h] = jnp.where(lane == ONES_LANE[h % 2], 1.0, vf[:, blk]).astype(BF16)

    row = lambda w, j: pl.BlockSpec((TM, w), lambda i: (i, j))
    full = lambda a: pl.BlockSpec(a.shape, lambda i: (0,) * a.ndim)
    hs = jax.ShapeDtypeStruct((NH, s, LANE), BF16)
    return pl.pallas_call(
        body, name="qkv_fwd", grid=(s // TM,),
        in_specs=[row(Q_LORA, P_CQ // Q_LORA), row(KV_LORA, P_CKV // KV_LORA), row(LANE, P_KPE // LANE),
                  full(g_q), full(g_kv), full(w_uq_p), full(w_k_p), full(w_v_p), row(LANE, 0), row(LANE, 0), row(LANE, 0)],
        out_specs=[pl.BlockSpec((NH, TM, LANE), lambda i: (0, i, 0))] * 3,
        out_shape=[hs, hs, hs],
        compiler_params=_params(("parallel",)),
    )(proj, proj, proj, g_q, g_kv, w_uq_p, w_k_p, w_v_p, rc, rs1, rs2)


LOG2E = 1.4426950408889634
QK_SCALE2 = LOG2E / math.sqrt(QK_NOPE + QK_ROPE)


HQ = TQ // 2


def _diag_visible(n):
    row = lax.broadcasted_iota(jnp.int32, (n, n), 0)
    col = lax.broadcasted_iota(jnp.int32, (n, n), 1)
    return (col // CHUNK) <= (row // CHUNK)


def _attn_fwd(q, k, vv):
    s = q.shape[1]

    def body(q_ref, k_ref, v_ref, o_ref, lse_ref):
        i = pl.program_id(1)
        qs = (q_ref[0], q_ref[1])

        def tiles(t, carry, diag):
            rows = pl.ds(pl.multiple_of(t * TQ, TQ), TQ)
            sc = [_mm_nt(qs[hh], k_ref[hh, rows, :]) for hh in range(2)]
            if diag:
                sc = [jnp.where(_diag_visible(TQ), s_, -jnp.inf) for s_ in sc]
            m_new = [jnp.maximum(carry[hh][0], jnp.max(sc[hh], axis=-1, keepdims=True)) for hh in range(2)]
            alpha = [jnp.exp2((carry[hh][0] - m_new[hh]) * QK_SCALE2) for hh in range(2)]
            p = [jnp.exp2((sc[hh] - m_new[hh]) * QK_SCALE2).astype(BF16) for hh in range(2)]
            acc = [alpha[hh] * carry[hh][1] + _mm(p[hh], v_ref[hh, rows, :]) for hh in range(2)]
            return (m_new[0], acc[0]), (m_new[1], acc[1])

        init = (jnp.full((TQ, 1), -jnp.inf, F32), jnp.zeros((TQ, LANE), F32))
        carry = lax.fori_loop(0, i, lambda t, c: tiles(t, c, False), (init, init))
        carry = tiles(i, carry, True)
        lane = lax.broadcasted_iota(jnp.int32, (TQ, LANE), 1)
        out = jnp.zeros((TQ, LANE), F32)
        for hh in range(2):
            m, acc = carry[hh]
            l = jnp.sum(jnp.where(lane == ONES_LANE[hh], acc, 0.0), axis=-1, keepdims=True)
            out = out + jnp.where((lane < V_DIM) == (hh == 0), acc, 0.0) / l
            lse_ref[hh] = jnp.broadcast_to(m * QK_SCALE2 + jnp.log(l) * LOG2E, (TQ, LANE))
        o_ref[...] = out

    return pl.pallas_call(
        body, name="attn_fwd", grid=(NH // 2, s // TQ),
        in_specs=[pl.BlockSpec((2, TQ, LANE), lambda p, i: (p, i, 0)), pl.BlockSpec((2, s, LANE), lambda p, i: (p, 0, 0)),
                  pl.BlockSpec((2, s, LANE), lambda p, i: (p, 0, 0))],
        out_specs=[pl.BlockSpec((TQ, LANE), lambda p, i: (i, p)), pl.BlockSpec((2, TQ, LANE), lambda p, i: (p, i, 0))],
        out_shape=[jax.ShapeDtypeStruct((s, NH * V_DIM), F32), jax.ShapeDtypeStruct((NH, s, LANE), F32)],
        compiler_params=_params(("parallel", "parallel")),
    )(q, k, vv)


def _lower_bound(lbl):
    a0, a1 = lbl[0:1, :], lbl[1:2, :]
    mx = jnp.maximum(a0, a1)
    e0, e1 = jnp.exp(a0 - mx), jnp.exp(a1 - mx)
    return e0 / (e0 + e1)


def _chunk_cumsum(v, reverse=False):
    pos = lax.broadcasted_iota(jnp.int32, v.shape, 0) % HG_BLOCK
    s = 1
    while s < HG_BLOCK:
        if reverse:
            v = v + jnp.where(pos < HG_BLOCK - s, pltpu.roll(v, TH - s, 0), 0.0)
        else:
            v = v + jnp.where(pos >= s, pltpu.roll(v, s, 0), 0.0)
        s *= 2
    return v


def _hgrn_gates(hq, hf, lb):
    sig = _sigmoid(hf)
    f = lb + (1.0 - lb) * sig
    g = jnp.log(f)
    kk = 1.0 - f
    r = lax.broadcasted_iota(jnp.int32, (TH, TH), 0)
    c = lax.broadcasted_iota(jnp.int32, (TH, TH), 1)
    tri = ((r // HG_BLOCK) == (c // HG_BLOCK)) & (r >= c)
    cum = _chunk_cumsum(g)
    nch = TH // HG_BLOCK
    total = _chunks(cum)[:, HG_BLOCK - 1:HG_BLOCK, :]
    lastb = jnp.broadcast_to(total, (nch, HG_BLOCK, hf.shape[-1])).reshape(hf.shape)
    e, ei, ee = jnp.exp(cum), jnp.exp(-cum), jnp.exp(lastb - cum)
    return dict(sig=sig, f=f, kk=kk, tri=tri, cum=cum, total=total, decay=jnp.exp(total), e=e, ei=ei, ee=ee,
                qd=hq * e, ki=kk * ei, ke=kk * ee)


def _chunks(v):
    return v.reshape(TH // HG_BLOCK, HG_BLOCK, v.shape[-1])


def _bmm_nt(a, b):
    return lax.dot_general(a, b, (((2,), (2,)), ((0,), (0,))), preferred_element_type=F32)


def _bmm_nn(a, b):
    return lax.dot_general(a, b, (((2,), (1,)), ((0,), (0,))), preferred_element_type=F32)


def _bmm_tn(a, b):
    return lax.dot_general(a, b, (((1,), (1,)), ((0,), (0,))), preferred_element_type=F32)


def _pair_masks():
    lane = lax.broadcasted_iota(jnp.int32, (TH, LANE), 1)
    kr = lax.broadcasted_iota(jnp.int32, (LANE, LANE), 0)
    kc = lax.broadcasted_iota(jnp.int32, (LANE, LANE), 1)
    return lane < 64, (kr // 64) == (kc // 64)


def _hgrn_fwd(proj, lbl):
    s = proj.shape[0]
    nch = TH // HG_BLOCK

    def body(hq_ref, hf_ref, hi_ref, lbl_ref, o_ref, st_ref, st):
        @pl.when(pl.program_id(1) == 0)
        def _():
            st[...] = jnp.zeros_like(st)

        m0, bd = _pair_masks()
        gt = _hgrn_gates(hq_ref[...], hf_ref[...], _lower_bound(lbl_ref[...]))
        v_b, qd, qd_b = hi_ref[...].astype(BF16), gt["qd"], gt["qd"].astype(BF16)
        ki_b, ke_b = gt["ki"].astype(BF16), gt["ke"].astype(BF16)
        pairs = [slice(u * LANE, (u + 1) * LANE) for u in range(HG_PAIRS)]
        heads = [(lanes, m0 if hh == 0 else jnp.logical_not(m0)) for lanes in pairs for hh in range(2)]
        a_b = [jnp.where(gt["tri"], _mm_nt(jnp.where(mh, qd[:, lanes], 0.0).astype(BF16), ki_b[:, lanes]), 0.0).astype(BF16)
               for lanes, mh in heads]
        intra = [jnp.where(m0, _mm(a_b[2 * u], v_b[:, lanes]), _mm(a_b[2 * u + 1], v_b[:, lanes])) for u, lanes in enumerate(pairs)]
        upd = [_bmm_tn(_chunks(v_b[:, lanes]), _chunks(ke_b[:, lanes])) for lanes in pairs]
        entering = []
        for u, lanes in enumerate(pairs):
            cur, states = st[u], []
            for n in range(nch):
                states.append(cur)
                cur = gt["decay"][n][:, lanes] * cur + jnp.where(bd, upd[u][n], 0.0)
            st[u] = cur
            entering.append(jnp.stack(states))
            st_ref[u] = entering[u]
        for u, lanes in enumerate(pairs):
            o_ref[:, lanes] = intra[u] + _bmm_nt(_chunks(qd_b[:, lanes]), entering[u].astype(BF16)).reshape(TH, LANE)

    wide = HG_PAIRS * LANE
    col = lambda base: pl.BlockSpec((TH, wide), lambda p, i: (i, base // wide + p))
    return pl.pallas_call(
        body, name="hgrn_fwd", grid=(NH // 2 // HG_PAIRS, s // TH),
        in_specs=[col(P_HQ), col(P_HF), col(P_HI), pl.BlockSpec((2, wide), lambda p, i: (0, p))],
        out_specs=[pl.BlockSpec((TH, wide), lambda p, i: (i, p)),
                   pl.BlockSpec((HG_PAIRS, nch, LANE, LANE), lambda p, i: (p, i, 0, 0))],
        out_shape=[jax.ShapeDtypeStruct((s, 512), F32), jax.ShapeDtypeStruct((NH // 2, s // HG_BLOCK, LANE, LANE), F32)],
        scratch_shapes=[pltpu.VMEM((HG_PAIRS, LANE, LANE), F32)],
        compiler_params=_params(("parallel", "arbitrary")),
    )(proj, proj, proj, lbl)


def _group_sum(v):
    low = lax.broadcasted_iota(jnp.int32, (v.shape[0], LANE), 1) < V_DIM
    blocks = []
    for b in range(v.shape[1] // LANE):
        blk = v[:, b * LANE:(b + 1) * LANE]
        s_low = jnp.sum(jnp.where(low, blk, 0.0), axis=-1, keepdims=True)
        s_high = jnp.sum(jnp.where(low, 0.0, blk), axis=-1, keepdims=True)
        blocks.append(jnp.where(low, s_low, s_high))
    return jnp.concatenate(blocks, axis=1)


def _dsilu(z, sg):
    return sg * (1.0 + z * (1.0 - sg))


def _mid(proj, attn, o_raw, x, tgt, g_hg, b_gate, g_post, wa, wb, w_out):
    s = x.shape[0]

    def body(attn_ref, ga_ref, o_ref, gb_ref, mg_ref, x_ref, t_ref, ghg_ref, bg_ref, gp_ref, wa_ref, wb_ref, wo_ref,
             loss_ref, dout_ref, dattn_ref, dga_ref, dor_ref, dgb_ref, dmg_ref, dwo_ref, dwa_ref, dwb_ref, dgp_ref, dbg_ref, dghg_ref):
        @pl.when(pl.program_id(0) == 0)
        def _():
            for rf in (loss_ref, dwo_ref, dwa_ref, dwb_ref, dgp_ref, dbg_ref, dghg_ref):
                rf[...] = jnp.zeros_like(rf)

        attn, za, orw, zb = attn_ref[...], ga_ref[...], o_ref[...], gb_ref[...]
        ghg, gp = ghg_ref[...], gp_ref[...]
        sga, sgb = _sigmoid(za), _sigmoid(zb)
        sa, sb = za * sga, zb * sgb
        ga = attn * sa
        rh = lax.rsqrt(_group_sum(orw * orw) * (1.0 / V_DIM) + EPS)
        on = (orw * rh) * ghg
        gb = on * sb
        ga_b, gb_b = ga.astype(BF16), gb.astype(BF16)
        ya = _mm(ga_b, wa_ref[...])
        yb = _mm(gb_b, wb_ref[...])
        gates = _sigmoid(mg_ref[...] + bg_ref[...])
        g0, g1 = gates[:, :D], gates[:, D:]
        m_b = (g0 * ya + g1 * yb).astype(BF16)
        y = _mm(m_b, wo_ref[...])
        ry = lax.rsqrt(jnp.mean(y * y, axis=-1, keepdims=True) + EPS)
        out = x_ref[...] + (y * ry) * gp
        err = out - t_ref[...]
        loss_ref[...] += 0.5 * jnp.sum(jnp.mean(err * err, axis=-1, keepdims=True), axis=0, keepdims=True)
        dout = err * (1.0 / D)
        dout_ref[...] = dout
        dgp_ref[...] += jnp.sum(dout * (y * ry), axis=0, keepdims=True)
        dgy = dout * gp
        dy = ry * dgy - y * (ry * ry * ry) * jnp.mean(y * dgy, axis=-1, keepdims=True)
        dy_b = dy.astype(BF16)
        dm = _mm_nt(dy_b, wo_ref[...])
        dya_b, dyb_b = (dm * g0).astype(BF16), (dm * g1).astype(BF16)
        dga = _mm_nt(dya_b, wa_ref[...])
        dgb = _mm_nt(dyb_b, wb_ref[...])
        dwo_ref[...] += _mm_tn(m_b, dy_b)
        dwa_ref[...] += _mm_tn(ga_b, dya_b)
        dwb_ref[...] += _mm_tn(gb_b, dyb_b)
        dg0, dg1 = dm * ya, dm * yb
        dmg = jnp.concatenate([dg0 * g0 * (1.0 - g0), dg1 * g1 * (1.0 - g1)], axis=1)
        dmg_ref[...] = dmg.astype(BF16)
        dbg_ref[...] += jnp.sum(dmg, axis=0, keepdims=True)
        dattn_ref[...] = dga * sa
        dga_ref[...] = (dga * attn * _dsilu(za, sga)).astype(BF16)
        dgb_ref[...] = (dgb * on * _dsilu(zb, sgb)).astype(BF16)
        don = dgb * sb
        dghg_ref[...] += jnp.sum(don * (orw * rh), axis=0, keepdims=True)
        dgo = don * ghg
        dor_ref[...] = rh * dgo - orw * (rh * rh * rh) * (_group_sum(orw * dgo) * (1.0 / V_DIM))

    row = lambda w, j=0: pl.BlockSpec((TM_MID, w), lambda i: (i, j))
    full = lambda a: pl.BlockSpec(a.shape, lambda i: (0,) * a.ndim)
    acc = lambda shape: pl.BlockSpec(shape, lambda i: (0, 0))
    sds = jax.ShapeDtypeStruct
    return pl.pallas_call(
        body, name="mid", grid=(s // TM_MID,),
        in_specs=[row(512), row(512, P_GA // 512), row(512), row(512, P_GB // 512), row(2048, P_MERGE // 2048), row(D), row(D),
                  full(g_hg), full(b_gate), full(g_post), full(wa), full(wb), full(w_out)],
        out_specs=[acc((1, 1)), row(D), row(512), row(512), row(512), row(512), row(2048),
                   acc((D, D)), acc((512, D)), acc((512, D)), acc((1, D)), acc((1, 2048)), acc((1, 512))],
        out_shape=[sds((1, 1), F32), sds((s, D), F32), sds((s, 512), F32), sds((s, 512), BF16), sds((s, 512), F32), sds((s, 512), BF16),
                   sds((s, 2048), BF16), sds((D, D), F32), sds((512, D), F32), sds((512, D), F32), sds((1, D), F32),
                   sds((1, 2048), F32), sds((1, 512), F32)],
        compiler_params=_params(("arbitrary",)),
    )(attn, proj, o_raw, proj, proj, x, tgt, g_hg, b_gate, g_post, wa, wb, w_out)


def _attn_bwd(q, k, vv, attn, dattn, lse, token):
    s = q.shape[1]
    nt = s // TQ
    scale = 1.0 / math.sqrt(QK_NOPE + QK_ROPE)

    def body(q_ref, k_ref, v_ref, o_ref, do_ref, lse_ref, token_ref, dq_ref, dk_ref, dv_ref, do_s, delta_s):
        j = pl.program_id(1)

        @pl.when(j == 0)
        def _():
            dq_ref[...] = jnp.zeros_like(dq_ref)
            lane = lax.broadcasted_iota(jnp.int32, (TQ, LANE), 1)

            @pl.loop(0, nt)
            def _(i):
                rows = pl.ds(pl.multiple_of(i * TQ, TQ), TQ)
                do, o = do_ref[rows, :], o_ref[rows, :]
                for hh in range(2):
                    doh = jnp.where((lane < 64) if hh == 0 else (lane >= 64), do, 0.0)
                    do_s[hh, rows, :] = doh.astype(BF16)
                    delta_s[hh, rows, :] = jnp.broadcast_to(jnp.sum(doh * o, axis=-1, keepdims=True), (TQ, LANE))

        kjs, vjs = (k_ref[0], k_ref[1]), (v_ref[0], v_ref[1])

        def tile(hh, start, size, kj, vj, diag):
            rows = pl.ds(pl.multiple_of(start, size), size)
            wide = lambda a: jnp.concatenate([a] * (kj.shape[0] // LANE), axis=1)
            qi, do_b = q_ref[hh, rows, :], do_s[hh, rows, :]
            sc, dp = _mm_nt(qi, kj), _mm_nt(do_b, vj)
            p = jnp.exp2(sc * QK_SCALE2 - wide(lse_ref[hh, rows, :]))
            if diag:
                p = jnp.where(_diag_visible(size), p, 0.0)
            ds_b = (p * (dp - wide(delta_s[hh, rows, :]))).astype(BF16)
            dv, dk = _mm_tn(do_b, p.astype(BF16)), _mm_tn(qi, ds_b)
            dq_ref[hh, rows, :] += _mm(ds_b, kj)
            return dk, dv

        def step(i, carry):
            new = [tile(hh, i * TQ, TQ, kjs[hh], vjs[hh], False) for hh in range(2)]
            return tuple((carry[hh][0] + new[hh][0], carry[hh][1] + new[hh][1]) for hh in range(2))

        def diagonal(hh):
            k0, k1, v0, v1 = kjs[hh][:HQ], kjs[hh][HQ:], vjs[hh][:HQ], vjs[hh][HQ:]
            a = tile(hh, j * TQ, HQ, k0, v0, True)
            b = tile(hh, j * TQ + HQ, HQ, k0, v0, False)
            c = tile(hh, j * TQ + HQ, HQ, k1, v1, True)
            return jnp.concatenate([a[0] + b[0], c[0]], axis=1), jnp.concatenate([a[1] + b[1], c[1]], axis=1)

        carry = lax.fori_loop(j + 1, nt, step, (diagonal(0), diagonal(1)))
        for hh in range(2):
            dk_ref[hh] = carry[hh][0].T * scale
            dv_ref[hh] = carry[hh][1].T

        @pl.when(j == nt - 1)
        def _():
            dq_ref[...] = dq_ref[...] * scale

    whole = pl.BlockSpec((2, s, LANE), lambda p, j: (p, 0, 0))
    tile_spec = pl.BlockSpec((2, TQ, LANE), lambda p, j: (p, j, 0))
    cols = pl.BlockSpec((s, LANE), lambda p, j: (0, p))
    hs = jax.ShapeDtypeStruct((NH, s, LANE), F32)
    return pl.pallas_call(
        body, name="attn_bwd", grid=(NH // 2, nt),
        in_specs=[whole, tile_spec, tile_spec, cols, cols, whole, pl.BlockSpec((8, LANE), lambda p, j: (0, 0))],
        out_specs=[whole, tile_spec, tile_spec],
        out_shape=[hs, hs, hs],
        scratch_shapes=[pltpu.VMEM((2, s, LANE), BF16), pltpu.VMEM((2, s, LANE), F32)],
        compiler_params=_params(("parallel", "arbitrary")),
    )(q, k, vv, attn, dattn, lse, token)


def _hgrn_bwd(proj, lbl, states, do_raw):
    s = proj.shape[0]
    nt = s // TH
    nch = TH // HG_BLOCK

    def body(hq_ref, hf_ref, hi_ref, lbl_ref, st_ref, do_ref, dh_ref, dlbl_ref, dst, dlb):
        step = pl.program_id(1)

        @pl.when(step == 0)
        def _():
            dst[...] = jnp.zeros_like(dst)
            dlb[...] = jnp.zeros_like(dlb)

        m0, bd = _pair_masks()
        lb = _lower_bound(lbl_ref[...])
        gt = _hgrn_gates(hq_ref[...], hf_ref[...], lb)
        do = do_ref[...]
        qd, ki, ke = gt["qd"], gt["ki"], gt["ke"]
        v_b, do_b = hi_ref[...].astype(BF16), do.astype(BF16)
        qd_b, ki_b, ke_b = qd.astype(BF16), ki.astype(BF16), ke.astype(BF16)
        pairs = [slice(u * LANE, (u + 1) * LANE) for u in range(HG_PAIRS)]
        heads = [(lanes, m0 if hh == 0 else jnp.logical_not(m0)) for lanes in pairs for hh in range(2)]
        a_b = [jnp.where(gt["tri"], _mm_nt(jnp.where(mh, qd[:, lanes], 0.0).astype(BF16), ki_b[:, lanes]), 0.0).astype(BF16)
               for lanes, mh in heads]
        doh_b = [jnp.where(mh, do[:, lanes], 0.0).astype(BF16) for lanes, mh in heads]
        da_b = [jnp.where(gt["tri"], _mm_nt(d, v_b[:, lanes]), 0.0).astype(BF16) for d, (lanes, _) in zip(doh_b, heads)]
        dv_p, dqd_p, dki_p = [], [], []
        for u, lanes in enumerate(pairs):
            e, o = 2 * u, 2 * u + 1
            dv_p.append(_mm_tn(a_b[e], doh_b[e]) + _mm_tn(a_b[o], doh_b[o]))
            dqd_p.append(jnp.where(m0, _mm(da_b[e], ki_b[:, lanes]), _mm(da_b[o], ki_b[:, lanes])))
            dki_p.append(jnp.where(m0, _mm_tn(da_b[e], qd_b[:, lanes]), _mm_tn(da_b[o], qd_b[:, lanes])))
        fed = [_bmm_tn(_chunks(do_b[:, lanes]), _chunks(qd_b[:, lanes])) for lanes in pairs]
        leaving = []
        for u, lanes in enumerate(pairs):
            ds, left = dst[u], [None] * nch
            for n in reversed(range(nch)):
                left[n] = ds
                ds = gt["decay"][n][:, lanes] * ds + jnp.where(bd, fed[u][n], 0.0)
            dst[u] = ds
            leaving.append(jnp.stack(left))
        dke_p, dlast_p = [], []
        for u, lanes in enumerate(pairs):
            entering, leaving_b = st_ref[u], leaving[u].astype(BF16)
            dke3 = _bmm_nn(_chunks(v_b[:, lanes]), leaving_b)
            dv_p[u] = dv_p[u] + _bmm_nt(_chunks(ke_b[:, lanes]), leaving_b).reshape(TH, LANE)
            dqd_p[u] = dqd_p[u] + _bmm_nn(_chunks(do_b[:, lanes]), entering.astype(BF16)).reshape(TH, LANE)
            dke_p.append(dke3.reshape(TH, LANE))
            dlast_p.append(jnp.sum(dke3 * _chunks(ke[:, lanes]), axis=1, keepdims=True)
                           + jnp.sum(leaving[u] * entering, axis=1, keepdims=True) * gt["decay"][:, :, lanes])
        cat = lambda parts: jnp.concatenate(parts, axis=-1)
        dv, dqd, dki, dke, dlast = cat(dv_p), cat(dqd_p), cat(dki_p), cat(dke_p), cat(dlast_p)
        dk = dki * gt["ei"] + dke * gt["ee"]
        dcum = dqd * qd - dki * ki - dke * ke
        dg = _chunk_cumsum(dcum, reverse=True) + jnp.broadcast_to(dlast, (nch, HG_BLOCK, dlast.shape[-1])).reshape(dcum.shape)
        sig = gt["sig"]
        df = dg / gt["f"] - dk
        dlb[...] += jnp.sum(df * (1.0 - sig), axis=0, keepdims=True)
        dh_ref[0] = (dqd * gt["e"]).astype(BF16)
        dh_ref[1] = ((df * (1.0 - lb)) * sig * (1.0 - sig)).astype(BF16)
        dh_ref[2] = dv.astype(BF16)

        @pl.when(step == nt - 1)
        def _():
            lb = _lower_bound(lbl_ref[...])
            da0 = dlb[...] * lb * (1.0 - lb)
            dlbl_ref[...] = jnp.concatenate([da0, -da0], axis=0)

    wide = HG_PAIRS * LANE
    col = lambda base: pl.BlockSpec((TH, wide), lambda p, i: (nt - 1 - i, base // wide + p))
    tile = pl.BlockSpec((TH, wide), lambda p, i: (nt - 1 - i, p))
    sds = jax.ShapeDtypeStruct
    return pl.pallas_call(
        body, name="hgrn_bwd", grid=(NH // 2 // HG_PAIRS, nt),
        in_specs=[col(P_HQ), col(P_HF), col(P_HI), pl.BlockSpec((2, wide), lambda p, i: (0, p)),
                  pl.BlockSpec((HG_PAIRS, nch, LANE, LANE), lambda p, i: (p, nt - 1 - i, 0, 0)), tile],
        out_specs=[pl.BlockSpec((3, TH, wide), lambda p, i: (0, nt - 1 - i, p)), pl.BlockSpec((2, wide), lambda p, i: (0, p))],
        out_shape=[sds((3, s, 512), BF16), sds((2, 512), F32)],
        scratch_shapes=[pltpu.VMEM((HG_PAIRS, LANE, LANE), F32), pltpu.VMEM((1, wide), F32)],
        compiler_params=_params(("parallel", "arbitrary")),
    )(proj, proj, proj, lbl, states, do_raw)


def _norm_rows_bwd(v, r, g, dn):
    dgv = dn * g
    return r * dgv - v * (r * r * r) * jnp.mean(v * dgv, axis=-1, keepdims=True)


def _qkv_bwd(proj, dq, dk, dvv, g_q, g_kv, w_uq_p, w_k_p, w_v_p, rc, rs1, rs2):
    s = proj.shape[0]

    def body(cq_ref, ckv_ref, dq_ref, dk_ref, dv_ref, gq_ref, gkv_ref, wq_ref, wk_ref, wv_ref, c_ref, s1_ref, s2_ref,
             dcq_ref, dckv_ref, dkpe_ref, dwq_ref, dwk_ref, dwv_ref, dgq_ref, dgkv_ref):
        @pl.when(pl.program_id(0) == 0)
        def _():
            for rf in (dwq_ref, dwk_ref, dwv_ref, dgq_ref, dgkv_ref):
                rf[...] = jnp.zeros_like(rf)

        c, s1, s2 = c_ref[...], s1_ref[...], s2_ref[...]
        cq, ckv = cq_ref[...], ckv_ref[...]
        gq, gkv = gq_ref[...], gkv_ref[...]
        cqn, rq = _norm_rows(cq, gq)
        ckvn, rkv = _norm_rows(ckv, gkv)
        cqn_b, ckvn_b = cqn.astype(BF16), ckvn.astype(BF16)
        dqf = jnp.concatenate([_rope_t(dq_ref[h], c, s1, s2) for h in range(NH)], axis=1).astype(BF16)
        dkf = jnp.concatenate([dk_ref[h] for h in range(NH)], axis=1).astype(BF16)
        dvf = jnp.concatenate([dv_ref[h] for h in range(NH)], axis=1).astype(BF16)
        dkpe = dk_ref[0]
        for h in range(1, NH):
            dkpe = dkpe + dk_ref[h]
        lane = lax.broadcasted_iota(jnp.int32, (TM, LANE), 1)
        dkpe = jnp.where((lane >= QK_NOPE) & (lane < QK_NOPE + QK_ROPE), dkpe, 0.0)
        dkpe_ref[...] = _rope_t(dkpe, c, s1, s2).astype(BF16)
        dcqn = _mm_nt(dqf, wq_ref[...])
        dckvn = _mm_nt(dkf, wk_ref[...]) + _mm_nt(dvf, wv_ref[...])
        dwq_ref[...] += _mm_tn(cqn_b, dqf)
        dwk_ref[...] += _mm_tn(ckvn_b, dkf)
        dwv_ref[...] += _mm_tn(ckvn_b, dvf)
        dgq_ref[...] += jnp.sum(dcqn * (cq * rq), axis=0, keepdims=True)
        dgkv_ref[...] += jnp.sum(dckvn * (ckv * rkv), axis=0, keepdims=True)
        dcq_ref[...] = _norm_rows_bwd(cq, rq, gq, dcqn).astype(BF16)
        dckv_ref[...] = _norm_rows_bwd(ckv, rkv, gkv, dckvn).astype(BF16)

    row = lambda w, j=0: pl.BlockSpec((TM, w), lambda i: (i, j))
    full = lambda a: pl.BlockSpec(a.shape, lambda i: (0,) * a.ndim)
    acc = lambda shape: pl.BlockSpec(shape, lambda i: (0, 0))
    heads = pl.BlockSpec((NH, TM, LANE), lambda i: (0, i, 0))
    sds = jax.ShapeDtypeStruct
    return pl.pallas_call(
        body, name="qkv_bwd", grid=(s // TM,),
        in_specs=[row(Q_LORA, P_CQ // Q_LORA), row(KV_LORA, P_CKV // KV_LORA), heads, heads, heads,
                  full(g_q), full(g_kv), full(w_uq_p), full(w_k_p), full(w_v_p), row(LANE), row(LANE), row(LANE)],
        out_specs=[row(Q_LORA), row(KV_LORA), row(LANE), acc((Q_LORA, D)), acc((KV_LORA, D)), acc((KV_LORA, D)),
                   acc((1, Q_LORA)), acc((1, KV_LORA))],
        out_shape=[sds((s, Q_LORA), BF16), sds((s, KV_LORA), BF16), sds((s, LANE), BF16), sds((Q_LORA, D), F32),
                   sds((KV_LORA, D), F32), sds((KV_LORA, D), F32), sds((1, Q_LORA), F32), sds((1, KV_LORA), F32)],
        compiler_params=_params(("arbitrary",)),
    )(proj, proj, dq, dk, dvv, g_q, g_kv, w_uq_p, w_k_p, w_v_p, rc, rs1, rs2)


def _front_bwd(x, dout, dmg, dga, dh3, dgb, dcq, dckv, dkpe, g_pre, w_in_t, w_kpe, token):
    s = x.shape[0]

    def body(x_ref, do_ref, dmg_ref, dga_ref, dh3_ref, dgb_ref, dcq_ref, dckv_ref, dkpe_ref, g_ref, w_ref, k_ref, token_ref,
             gx_ref, dg_ref):
        @pl.when(pl.program_id(0) == 0)
        def _():
            dg_ref[...] = jnp.zeros_like(dg_ref)

        xv, g = x_ref[...], g_ref[...]
        _, r = _norm_rows(xv, g)
        pieces = ((dmg_ref[...], O_MERGE), (dga_ref[...], O_GA), (dh3_ref[0], O_HQ), (dh3_ref[1], O_HF), (dh3_ref[2], O_HI),
                  (dgb_ref[...], O_GB), (dcq_ref[...], O_CQ), (dckv_ref[...], O_CKV))
        dh = _mm(dkpe_ref[...], k_ref[...])
        for piece, off in pieces:
            dh = dh + _mm(piece, w_ref[off:off + piece.shape[1], :])
        dg_ref[...] += jnp.sum(dh * (xv * r), axis=0, keepdims=True)
        gx_ref[...] = do_ref[...] + _norm_rows_bwd(xv, r, g, dh)

    row = lambda w: pl.BlockSpec((TM, w), lambda i: (i, 0))
    full = lambda a: pl.BlockSpec(a.shape, lambda i: (0,) * a.ndim)
    sds = jax.ShapeDtypeStruct
    return pl.pallas_call(
        body, name="front_bwd", grid=(s // TM,),
        in_specs=[row(D), row(D), row(2048), row(512), pl.BlockSpec((3, TM, 512), lambda i: (0, i, 0)), row(512), row(Q_LORA),
                  row(KV_LORA), row(LANE), full(g_pre), full(w_in_t), full(w_kpe), pl.BlockSpec(memory_space=pl.ANY)],
        out_specs=[row(D), pl.BlockSpec((1, D), lambda i: (0, 0))],
        out_shape=[sds((s, D), F32), sds((1, D), F32)],
        compiler_params=_params(("arbitrary",)),
    )(x, dout, dmg, dga, dh3, dgb, dcq, dckv, dkpe, g_pre, w_in_t, w_kpe, token)


TK_GRAD = 1024


def _win_grad(h, pieces, name):
    s = h.shape[0]
    n = len(pieces)

    def body(h_ref, *refs):
        @pl.when(pl.program_id(0) == 0)
        def _():
            for o_ref in refs[n:]:
                o_ref[...] = jnp.zeros_like(o_ref)

        hv = h_ref[...]
        for d_ref, o_ref in zip(refs[:n], refs[n:]):
            if len(d_ref.shape) == 3:
                for k in range(d_ref.shape[0]):
                    o_ref[k] += _mm_tn(d_ref[k], hv)
            else:
                o_ref[...] += _mm_tn(d_ref[...], hv)

    def in_spec(p):
        if p.ndim == 3:
            return pl.BlockSpec((p.shape[0], TK_GRAD, p.shape[2]), lambda kk: (0, kk, 0))
        return pl.BlockSpec((TK_GRAD, p.shape[1]), lambda kk: (kk, 0))

    out_shapes = [(p.shape[0], p.shape[2], D) if p.ndim == 3 else (p.shape[1], D) for p in pieces]
    return pl.pallas_call(
        body, name=name, grid=(s // TK_GRAD,),
        in_specs=[pl.BlockSpec((TK_GRAD, D), lambda kk: (kk, 0))] + [in_spec(p) for p in pieces],
        out_specs=[pl.BlockSpec(sh, lambda kk, nd=len(sh): (0,) * nd) for sh in out_shapes],
        out_shape=[jax.ShapeDtypeStruct(sh, F32) for sh in out_shapes],
        compiler_params=_params(("arbitrary",)),
    )(h, *pieces)


def _pad_wuq(w_uq):
    rows = w_uq.shape[0]
    w = w_uq.reshape(rows, NH, QK_NOPE + QK_ROPE)
    return jnp.pad(w, ((0, 0), (0, 0), (0, LANE - QK_NOPE - QK_ROPE))).reshape(rows, NH * LANE)


def _unpad_wuq(g):
    return g.reshape(Q_LORA, NH, LANE)[:, :, :QK_NOPE + QK_ROPE].reshape(Q_LORA, NH * (QK_NOPE + QK_ROPE))


def _pad_wukv(w_ukv):
    heads = w_ukv.shape[1] // (QK_NOPE + V_DIM)
    w = w_ukv.reshape(KV_LORA, heads, QK_NOPE + V_DIM)
    w_k = jnp.pad(w[:, :, :QK_NOPE], ((0, 0), (0, 0), (0, LANE - QK_NOPE))).reshape(KV_LORA, heads * LANE)
    wv = w[:, :, QK_NOPE:].reshape(KV_LORA, heads // 2, 2, 1, V_DIM)
    eye = jnp.eye(2, dtype=w.dtype).reshape(1, 1, 2, 2, 1)
    return w_k, (wv * eye).reshape(KV_LORA, heads * LANE)


def _unpad_wukv(gk, gv):
    gk = gk.reshape(KV_LORA, NH, LANE)[:, :, :QK_NOPE]
    gv = gv.reshape(KV_LORA, NH // 2, 2, 2, V_DIM)
    gv = jnp.stack([gv[:, :, 0, 0], gv[:, :, 1, 1]], axis=2).reshape(KV_LORA, NH, V_DIM)
    return jnp.concatenate([gk, gv], axis=-1).reshape(KV_LORA, NH * (QK_NOPE + V_DIM))


def _local_step(x, tgt, g_pre, w_in_t, b_gate, g_q, g_kv, lb_logits, g_hgrn, g_post, weights, exchange=None):
    s = x.shape[0]
    w_kpe = _kpe_block(w_in_t)
    rc, rs1, rs2 = _rope_tables(s)
    g_hg = jnp.tile(g_hgrn, (1, NH))

    proj, h = _front_fwd(x, g_pre, w_in_t, w_kpe, weights.tokens)
    w_uq_p, w_k_p, w_v_p = weights.qkv(h)
    q, k, vv = _qkv_fwd(proj, g_q, g_kv, w_uq_p, w_k_p, w_v_p, rc, rs1, rs2)
    attn, lse = _attn_fwd(q, k, vv)
    o_raw, states = _hgrn_fwd(proj, lb_logits)
    wa, wb, w_out = weights.mid(o_raw)
    (loss, dout, dattn, dga, dor, dgb, dmg, d_wout, d_wa, d_wb, d_gpost, d_bgate, d_ghg) = _mid(
        proj, attn, o_raw, x, tgt, g_hg, b_gate, g_post, wa, wb, w_out)
    w_mg, w_ga, w_gb = _win_grad(h, [dmg, dga, dgb], "win_grad_mid")
    dh3, d_lbl = _hgrn_bwd(proj, lb_logits, states, dor)
    (w_h3,) = _win_grad(h, [dh3], "win_grad_hgrn")
    d_win_rest = jnp.concatenate([w_ga, w_h3[0], w_h3[1], w_h3[2], w_gb, w_mg], axis=0)
    early = dict(w_in_rest=d_win_rest, w_branch_a=d_wa, w_branch_b=d_wb, w_out=d_wout)
    token = exchange.start_early(early) if exchange else jnp.zeros((8, LANE), F32)
    dq, dk, dvv = _attn_bwd(q, k, vv, attn, dattn, lse, token)
    dcq, dckv, dkpe, d_wuq_p, d_wk_p, d_wv_p, d_gq, d_gkv = _qkv_bwd(proj, dq, dk, dvv, g_q, g_kv, w_uq_p, w_k_p, w_v_p, rc, rs1, rs2)
    w_cq, w_ckv, w_kp = _win_grad(h, [dcq, dckv, dkpe], "win_grad_qkv")
    d_win_qkv = jnp.concatenate([w_cq, w_ckv, w_kp[64:64 + QK_ROPE]], axis=0)
    late = dict(w_in_qkv=d_win_qkv, w_uq=_unpad_wuq(d_wuq_p), w_ukv=_unpad_wukv(d_wk_p, d_wv_p))
    token = exchange.start_late(late) if exchange else jnp.zeros((8, LANE), F32)
    grad_x, d_gpre = _front_bwd(x, dout, dmg, dga, dh3, dgb, dcq, dckv, dkpe, g_pre, w_in_t, w_kpe, token)
    vec_grads = dict(g_pre=d_gpre, b_gate=d_bgate, g_q=d_gq, g_kv=d_gkv, lb_logits=d_lbl, g_hgrn=d_ghg, g_post=d_gpost)
    return loss, grad_x, dict(early, **late), vec_grads


SHARD_SHAPES = (("w_in", (1416, 1024)), ("w_uq", (192, 768)), ("w_ukv", (256, 256)), ("w_branch_a", (512, 256)),
                ("w_branch_b", (512, 256)), ("w_out", (256, 1024)))
BIG = tuple(n for n, _ in SHARD_SHAPES)
ROW_SHARDED = ("w_in", "w_uq", "w_out")
N_CHIPS = 4
QKV_ROWS = Q_LORA + KV_LORA + QK_ROPE
W_IN_FORWARD_CUT = 704


def _to_block(name, a):
    return a[0].T if name == "w_in" else a[0]


def _from_block(name, a):
    return a.T[None] if name == "w_in" else a[None]
VEC_ROWS = (("g_pre", 0, 1024), ("b_gate", 1, 2048), ("g_q", 2, 768), ("g_kv", 3, 256), ("g_hgrn", 6, 64), ("g_post", 7, 1024))
VEC_LB_ROW = 4
VEC_SHAPE = (8, 2048)


def _split_by_chip(name, g):
    a, b = dict(SHARD_SHAPES)[name]
    return g.reshape(N_CHIPS, a, b) if name in ROW_SHARDED else g.reshape(a, N_CHIPS, b).transpose(1, 0, 2)


def _join_chips(name, w):
    a, b = dict(SHARD_SHAPES)[name]
    return w.reshape(N_CHIPS * a, b) if name in ROW_SHARDED else w.transpose(1, 0, 2).reshape(a, N_CHIPS * b)


MESH = pl.DeviceIdType.MESH
HBM = pl.BlockSpec(memory_space=pltpu.HBM)


def _mesh_place():
    x, y, c = lax.axis_index("x"), lax.axis_index("y"), lax.axis_index("c")
    return x, y, c, 2 * x + y, [(1 - x, y), (x, 1 - y), (1 - x, 1 - y)]


def _remote(src, dst, send_sems, recv_sems, k, to):
    return pltpu.make_async_remote_copy(src_ref=src, dst_ref=dst, send_sem=send_sems.at[k], recv_sem=recv_sems.at[k],
                                        device_id=to, device_id_type=MESH)


def _gather_w_in(shard):
    a, b = shard.shape
    cut = W_IN_FORWARD_CUT

    def body(src, out, ici_send, ici_recv, d2d_send, d2d_recv, local_sem):
        x, y, c = lax.axis_index("x"), lax.axis_index("y"), lax.axis_index("c")
        me, xn, yn, dg = 2 * x + y, 2 * (1 - x) + y, 2 * x + (1 - y), 2 * (1 - x) + (1 - y)
        to_x, to_y, sibling = (1 - x, y, c), (x, 1 - y, c), (x, y, 1 - c)
        first, rest = pl.ds(0, cut), pl.ds(cut, a - cut)

        whole = lambda ref, which: ref.at[:, pl.ds(pl.multiple_of(which * (b // 2), b // 2), b // 2)]
        own = pltpu.make_async_copy(src, out.at[me], local_sem)
        own.start()
        sends = [_remote(whole(src, c), whole(out.at[me], c), ici_send, ici_recv, 0, to_x),
                 _remote(whole(src, c), whole(out.at[me], c), ici_send, ici_recv, 1, to_y)]
        for cp in sends:
            cp.start()

        def landed(slot, rows, k, d2d_k, src_dev):
            piece = whole(out.at[slot], c) if rows is None else out.at[slot].at[rows, pl.ds(pl.multiple_of(c * (b // 2), b // 2), b // 2)]
            _remote(piece, piece, ici_send, ici_recv, k, src_dev).wait_recv()
            cp = _remote(piece, piece, d2d_send, d2d_recv, d2d_k, sibling)
            cp.start()
            sends.append(cp)
            return piece

        def pass_on(slot, rows, k, to):
            piece = out.at[slot].at[rows, pl.ds(pl.multiple_of(c * (b // 2), b // 2), b // 2)]
            cp = _remote(piece, piece, ici_send, ici_recv, k, to)
            cp.start()
            sends.append(cp)

        landed(xn, None, 0, 0, to_x)
        pass_on(xn, first, 2, to_y)
        landed(yn, None, 1, 1, to_y)
        pass_on(yn, rest, 3, to_x)
        landed(dg, first, 2, 2, to_y)
        landed(dg, rest, 3, 3, to_x)
        other = pl.ds(pl.multiple_of((1 - c) * (b // 2), b // 2), b // 2)
        for d2d_k, (slot, rows) in enumerate(((xn, None), (yn, None), (dg, first), (dg, rest))):
            piece = out.at[slot].at[:, other] if rows is None else out.at[slot].at[rows, other]
            _remote(piece, piece, d2d_send, d2d_recv, d2d_k, sibling).wait_recv()
        for cp in sends:
            cp.wait_send()
        own.wait()

    sems = pltpu.SemaphoreType.DMA((4,))
    return pl.pallas_call(
        body, name="gather_w_in", in_specs=[HBM], out_specs=HBM,
        out_shape=jax.ShapeDtypeStruct((N_CHIPS, a, b), shard.dtype),
        scratch_shapes=[sems, sems, sems, sems, pltpu.SemaphoreType.DMA],
        compiler_params=pltpu.CompilerParams(has_side_effects=True),
    )(shard)


def _sibling_exchange(srcs, name, after=None):
    n = len(srcs)
    extra = [] if after is None else [after]

    def body(*refs):
        src_refs, outs = refs[:n], refs[n + len(extra):2 * n + len(extra)]
        send_sems, recv_sems = refs[2 * n + len(extra):]
        sibling = (lax.axis_index("x"), lax.axis_index("y"), 1 - lax.axis_index("c"))
        copies = [_remote(src_refs[k], outs[k], send_sems, recv_sems, k, sibling) for k in range(n)]
        for cp in copies:
            cp.start()
        for cp in copies:
            cp.wait()

    sems = pltpu.SemaphoreType.DMA((n,))
    return pl.pallas_call(
        body, name=name, in_specs=[HBM] * n + [pl.BlockSpec(memory_space=pl.ANY)] * len(extra), out_specs=[HBM] * n,
        out_shape=[jax.ShapeDtypeStruct(s.shape, s.dtype) for s in srcs],
        scratch_shapes=[sems, sems],
        compiler_params=pltpu.CompilerParams(has_side_effects=True),
    )(*srcs, *extra)


SEM = pl.BlockSpec(memory_space=pltpu.SEMAPHORE)
DATAFLOW = pltpu.SideEffectType.DATAFLOW_SIDE_EFFECTING


def _exchange_copies(srcs, to_first, src_refs, land_refs, send_sems, recv_sems):
    x, y, c, me, chips = _mesh_place()
    n = len(srcs)
    sends, recvs = [], []
    for k in range(n):
        if k in to_first:
            base = 3 * n + 4 * to_first.index(k)
            sends.append((me != 0, pltpu.make_async_remote_copy(
                src_ref=src_refs[k], dst_ref=land_refs[k].at[me], send_sem=send_sems.at[base], recv_sem=recv_sems.at[base + me],
                device_id=(0, 0, c), device_id_type=MESH)))
            for s in range(1, N_CHIPS):
                recvs.append((me == 0, pltpu.make_async_remote_copy(
                    src_ref=src_refs[k], dst_ref=land_refs[k].at[s], send_sem=send_sems.at[base], recv_sem=recv_sems.at[base + s],
                    device_id=(s // 2, s % 2, c), device_id_type=MESH)))
        else:
            slab = (lambda t, k=k: src_refs[k]) if srcs[k].ndim == 2 else (lambda t, k=k: src_refs[k].at[t])
            for j, (px, py) in enumerate(chips):
                sends.append((None, _remote(slab(2 * px + py), land_refs[k].at[me], send_sems, recv_sems, 3 * k + j, (px, py, c))))
                recvs.append((None, _remote(slab(me), land_refs[k].at[2 * px + py], send_sems, recv_sems, 3 * k + j, (px, py, c))))
    return sends, recvs


def _when(pred, fn):
    if pred is None:
        fn()
    else:
        pl.when(pred)(fn)


def _exchange_start(srcs, to_first, name, after=None):
    n = len(srcs)
    n_sems = 3 * n + 4 * len(to_first)
    lands = [lax.empty((N_CHIPS,) + s.shape[-2:], s.dtype) for s in srcs]
    extra = [] if after is None else [after]

    def body(*refs):
        src_refs, land_refs = refs[:n], refs[n:2 * n]
        send_sems, recv_sems, token = refs[2 * n + len(extra)], refs[2 * n + len(extra) + 1], refs[-1]
        sends, _ = _exchange_copies(srcs, to_first, src_refs, land_refs, send_sems, recv_sems)
        for pred, cp in sends:
            _when(pred, cp.start)
        token[...] = jnp.zeros_like(token)

    hbm = lambda a: pltpu.HBM(a.shape, a.dtype)
    res = pl.pallas_call(
        body, name=name,
        out_shape=[pltpu.SemaphoreType.DMA((n_sems,)), pltpu.SemaphoreType.DMA((n_sems,))] + [hbm(a) for a in srcs + lands]
        + [jax.ShapeDtypeStruct((8, LANE), F32)],
        in_specs=[HBM] * (2 * n) + [pl.BlockSpec(memory_space=pl.ANY)] * len(extra),
        out_specs=[SEM, SEM] + [HBM] * (2 * n) + [pl.BlockSpec(memory_space=pltpu.VMEM)],
        input_output_aliases={i: 2 + i for i in range(2 * n)},
        compiler_params=pltpu.CompilerParams(has_side_effects=DATAFLOW),
    )(*[pltpu.with_memory_space_constraint(a, pltpu.HBM) for a in srcs + lands], *extra)
    return res[:-1], res[-1]


def _exchange_wait(srcs, to_first, started, after, name):
    n = len(srcs)
    send_sems, recv_sems, thru = started[0], started[1], started[2:]

    def body(*refs):
        src_refs, land_refs, send_ref, recv_ref = refs[:n], refs[n:2 * n], refs[2 * n], refs[2 * n + 1]
        sends, recvs = _exchange_copies(srcs, to_first, src_refs, land_refs, send_ref, recv_ref)
        for pred, cp in sends:
            _when(pred, cp.wait_send)
        for pred, cp in recvs:
            _when(pred, cp.wait_recv)

    res = pl.pallas_call(
        body, name=name, out_shape=[pltpu.HBM(a.shape, a.dtype) for a in thru],
        in_specs=[HBM] * (2 * n) + [SEM, SEM, pl.BlockSpec(memory_space=pl.ANY)], out_specs=[HBM] * (2 * n),
        input_output_aliases={i: i for i in range(2 * n)},
        compiler_params=pltpu.CompilerParams(has_side_effects=DATAFLOW),
    )(*thru, send_sems, recv_sems, after)
    return res[n:]


ROW_TILE = 256
COL_TILE = 256


def _block_tiling(a, b):
    if a <= ROW_TILE or a % ROW_TILE == 0:
        ta = min(a, ROW_TILE)
        return a // ta, (ta, b), lambda i: (i, 0)
    return b // COL_TILE, (a, COL_TILE), lambda i: (0, i)


def _sum_landed(land, own, name, first_land=None, first_own=None):
    _, a, b = land.shape
    steps, tile, at = _block_tiling(a, b)
    extra = first_land is not None

    def body(*refs):
        p_ref, own_ref, o_ref = refs[0], refs[1], refs[-1]
        me = 2 * lax.axis_index("x") + lax.axis_index("y")
        own = own_ref[...].astype(F32)
        slot = lambda t: jnp.where(me == t, own, p_ref[t].astype(F32))
        o_ref[...] = ((slot(0) + slot(1)) + slot(2)) + slot(3)
        if extra:
            fp_ref, fo_ref = refs[2], refs[3]
            r = fo_ref.shape[0]

            @pl.when(me == 0)
            def _():
                f = lambda t: fp_ref[t].astype(F32)
                rows = pl.ds(pl.multiple_of(lax.axis_index("c") * r, 8), r)
                o_ref[rows, :] += ((fo_ref[...].astype(F32) + f(1)) + f(2)) + f(3)

    in_specs = [pl.BlockSpec((N_CHIPS,) + tile, lambda i: (0,) + at(i)), pl.BlockSpec(tile, at)]
    args = [land, own]
    if extra:
        r = first_own.shape[0]
        assert tile[0] == a, "the extra rows need whole columns in a step"
        in_specs += [pl.BlockSpec((N_CHIPS, r, tile[1]), lambda i: (0,) + at(i)), pl.BlockSpec((r, tile[1]), at)]
        args += [first_land, first_own]
    return pl.pallas_call(
        body, name=name, grid=(steps,), in_specs=in_specs, out_specs=pl.BlockSpec(tile, at),
        out_shape=jax.ShapeDtypeStruct((a, b), F32), compiler_params=_params(("parallel",)),
    )(*args)


def _sum_landed_small(lands, owns, name):
    n = len(lands)

    def body(*refs):
        me = 2 * lax.axis_index("x") + lax.axis_index("y")
        for p_ref, own_ref, o_ref in zip(refs[:n], refs[n:2 * n], refs[2 * n:]):
            own = own_ref[...].astype(F32)
            slot = lambda t: jnp.where(me == t, own, p_ref[t].astype(F32))
            o_ref[...] = ((slot(0) + slot(1)) + slot(2)) + slot(3)

    return pl.pallas_call(body, name=name, out_shape=[jax.ShapeDtypeStruct(o.shape, F32) for o in owns],
                          compiler_params=_params(()))(*lands, *owns)


def _adamw_small(mine, theirs, states, name):
    n = len(mine)

    def body(*refs):
        ins, outs = refs[:5 * n], refs[5 * n:]
        for k in range(n):
            a_ref, b_ref, w_ref, m_ref, v_ref = ins[5 * k:5 * k + 5]
            g = a_ref[...] + b_ref[...]
            outs[4 * k][...] = g
            outs[4 * k + 1][...], outs[4 * k + 2][...], outs[4 * k + 3][...] = _adamw_math(g, w_ref[...], m_ref[...], v_ref[...])

    args = [t for k in range(n) for t in (mine[k], theirs[k], *states[k])]
    res = pl.pallas_call(body, name=name, out_shape=[jax.ShapeDtypeStruct(mine[k].shape, F32) for k in range(n) for _ in range(4)],
                         compiler_params=_params(()))(*args)
    return [tuple(res[4 * k:4 * k + 4]) for k in range(n)]


def _add_cast(a, b, name):
    def body(a_ref, b_ref, o_ref):
        o_ref[...] = (a_ref[...] + b_ref[...]).astype(BF16)

    return pl.pallas_call(body, name=name, out_shape=jax.ShapeDtypeStruct(a.shape, BF16),
                          compiler_params=_params(()))(a, b)


class _LaterWeights:
    MID = ("w_branch_a", "w_branch_b", "w_out")

    def __init__(self, blocks, after):
        self.qkv_blocks = [_pad_wuq(blocks["w_uq"]), *_pad_wukv(blocks["w_ukv"])]
        self.mid_blocks = [blocks[n] for n in self.MID]
        self.qkv_started, t1 = _exchange_start(self.qkv_blocks, (), "weights_qkv_start", after)
        self.mid_started, t2 = _exchange_start(self.mid_blocks, (), "weights_mid_start", after)
        self.tokens = [t1, t2]

    @staticmethod
    def _whole(blocks, rows_sharded, started, after, name):
        landed = _exchange_wait(blocks, (), started, after, name)
        me = 2 * lax.axis_index("x") + lax.axis_index("y")
        out = []
        for block, land, by_rows in zip(blocks, landed, rows_sharded):
            w = lax.dynamic_update_index_in_dim(land, block, me, 0)
            a, b = block.shape
            out.append(w.reshape(N_CHIPS * a, b) if by_rows else w.transpose(1, 0, 2).reshape(a, N_CHIPS * b))
        return out

    def qkv(self, after):
        return self._whole(self.qkv_blocks, (True, False, False), self.qkv_started, after, "weights_qkv_wait")

    def mid(self, after):
        return self._whole(self.mid_blocks, (False, False, True), self.mid_started, after, "weights_mid_wait")


class _GradExchange:
    EARLY = ("w_in", "w_branch_a", "w_branch_b", "w_out")
    LATE = ("w_uq", "w_ukv")

    def __init__(self, state):
        self.state = state
        self.outs = {}

    @staticmethod
    def _own(slabs):
        return lax.dynamic_index_in_dim(slabs, 2 * lax.axis_index("x") + lax.axis_index("y"), axis=0, keepdims=False)

    def start_early(self, g):
        full = jnp.concatenate([jnp.zeros((QKV_ROWS, D), F32), g["w_in_rest"]], axis=0)
        g = dict(g, w_in=full)
        self.early = [_split_by_chip(n, g[n]).astype(BF16) for n in self.EARLY]
        self.early_started, token = _exchange_start(self.early, (), "grads_early_start")
        return token

    def start_late(self, g):
        self.early_landed = _exchange_wait(self.early, (), self.early_started, g["w_uq"], "grads_early_wait")
        half = QKV_ROWS // 2
        c = lax.axis_index("c")
        mine = lax.dynamic_slice_in_dim(g["w_in_qkv"], c * half, half, axis=0)
        (theirs,) = _sibling_exchange([lax.dynamic_slice_in_dim(g["w_in_qkv"], (1 - c) * half, half, axis=0)], "sibling_qkv_rows")
        self.late = [_split_by_chip(n, g[n]).astype(BF16) for n in self.LATE] + [_add_cast(mine, theirs, "add_qkv_rows")]
        self.late_started, token = _exchange_start(self.late, (2,), "grads_late_start")
        names = self.EARLY[1:]
        mine = _sum_landed_small(self.early_landed[1:], [self._own(slabs) for slabs in self.early[1:]], "sum_early")
        theirs = _sibling_exchange(mine, "sibling_early", after=token)
        for n, out in zip(names, _adamw_small(mine, theirs, [self.state[n] for n in names], "adamw_early")):
            self.outs[n] = out
        return self.outs[names[-1]][0]

    def finish(self, after):
        late_landed = _exchange_wait(self.late, (2,), self.late_started, after, "grads_late_wait")
        sums = {"w_in": _sum_landed(self.early_landed[0], self._own(self.early[0]), "sum_w_in",
                                    first_land=late_landed[2], first_own=self.late[2])}
        small = _sum_landed_small(late_landed[:2], [self._own(slabs) for slabs in self.late[:2]], "sum_late")
        sums.update(zip(self.LATE, small))
        return sums


def _adamw_math(g, w, m, v):
    nm = ADAM_B1 * m + (1.0 - ADAM_B1) * g
    nv = ADAM_B2 * v + (1.0 - ADAM_B2) * (g * g)
    m_hat = nm / (1.0 - ADAM_B1 ** ADAM_STEP)
    v_hat = nv / (1.0 - ADAM_B2 ** ADAM_STEP)
    return -ADAM_LR * (m_hat / (jnp.sqrt(v_hat) + ADAM_EPS) + ADAM_WD * w), nm, nv


def _adamw(p_mine, p_sibling, w, m, v, name):
    a, b = p_mine.shape
    steps, tile, at = _block_tiling(a, b)

    def body(a_ref, b_ref, w_ref, m_ref, v_ref, g_ref, d_ref, nm_ref, nv_ref):
        g = a_ref[...] + b_ref[...]
        g_ref[...] = g
        d_ref[...], nm_ref[...], nv_ref[...] = _adamw_math(g, w_ref[...], m_ref[...], v_ref[...])

    spec = pl.BlockSpec(tile, at)
    sds = jax.ShapeDtypeStruct((a, b), F32)
    return pl.pallas_call(
        body, name=name, grid=(steps,), in_specs=[spec] * 5, out_specs=[spec] * 4, out_shape=[sds] * 4,
        compiler_params=_params(("parallel",)),
    )(p_mine, p_sibling, w, m, v)


LOSS_AT = (2, 1024)


def _vec_pack(vg, loss):
    names = [n for n, _, _ in VEC_ROWS]

    def body(*refs):
        o_ref = refs[-1]
        lb_ref, loss_ref = refs[len(names)], refs[len(names) + 1]
        o_ref[...] = jnp.zeros_like(o_ref)
        o_ref[LOSS_AT[0]:LOSS_AT[0] + 1, LOSS_AT[1]:LOSS_AT[1] + LANE] = jnp.broadcast_to(loss_ref[...], (1, LANE))
        for (name, row, size), ref in zip(VEC_ROWS, refs):
            if name == "g_hgrn":
                r = lax.broadcasted_iota(jnp.int32, (NH * V_DIM, LANE), 0)
                c = lax.broadcasted_iota(jnp.int32, (NH * V_DIM, LANE), 1)
                fold = ((r % V_DIM) == c).astype(F32)
                o_ref[row:row + 1, 0:LANE] = jnp.dot(ref[...], fold, precision=HIGHEST, preferred_element_type=F32)
            else:
                o_ref[row:row + 1, 0:size] = ref[...]
        o_ref[VEC_LB_ROW:VEC_LB_ROW + 2, 0:512] = lb_ref[...]

    return pl.pallas_call(body, name="vec_pack", out_shape=jax.ShapeDtypeStruct(VEC_SHAPE, F32))(
        *[vg[n] for n in names], vg["lb_logits"], loss)


def _adamw_vec(p_mine, p_sibling, w, m, v):
    names = [n for n, _, _ in VEC_ROWS] + ["lb_logits"]
    k = len(names)

    def body(a_ref, b_ref, *refs):
        ins, outs = refs[:3 * k], refs[3 * k:]
        at = (slice(LOSS_AT[0], LOSS_AT[0] + 1), slice(LOSS_AT[1], LOSS_AT[1] + LANE))
        outs[-1][...] = a_ref[at] + b_ref[at]
        for i, name in enumerate(names):
            if name == "lb_logits":
                rows, cols = slice(VEC_LB_ROW, VEC_LB_ROW + 2), slice(0, 512)
            else:
                _, row, size = VEC_ROWS[i]
                rows, cols = slice(row, row + 1), slice(0, size)
            g = a_ref[rows, cols] + b_ref[rows, cols]
            d, nm, nv = _adamw_math(g, ins[i][...], ins[k + i][...], ins[2 * k + i][...])
            for o_ref, val in zip(outs[4 * i:4 * i + 4], (g, d, nm, nv)):
                o_ref[...] = val

    shapes = [jax.ShapeDtypeStruct(w[n].shape, F32) for n in names for _ in range(4)] + [jax.ShapeDtypeStruct((1, LANE), F32)]
    res = pl.pallas_call(body, name="adamw_vec", out_shape=shapes)(
        p_mine, p_sibling, *[w[n] for n in names], *[m[n] for n in names], *[v[n] for n in names])
    return [{n: res[4 * i + j] for i, n in enumerate(names)} for j in range(4)], res[-1]


WEIGHTS = ("g_pre", "w_in", "b_gate", "g_q", "w_uq", "g_kv", "w_ukv", "lb_logits", "g_hgrn", "w_branch_a", "w_branch_b", "w_out", "g_post")


def kernel(x, g_pre, w_in, b_gate, g_q, w_uq, g_kv, w_ukv, lb_logits, g_hgrn, w_branch_a, w_branch_b, w_out, g_post, loss_target, m_g_pre, m_w_in, m_b_gate, m_g_q, m_w_uq, m_g_kv, m_w_ukv, m_lb_logits, m_g_hgrn, m_w_branch_a, m_w_branch_b, m_w_out, m_g_post, v_g_pre, v_w_in, v_b_gate, v_g_q, v_w_uq, v_g_kv, v_w_ukv, v_lb_logits, v_g_hgrn, v_w_branch_a, v_w_branch_b, v_w_out, v_g_post):
    w = dict(g_pre=g_pre, w_in=w_in, b_gate=b_gate, g_q=g_q, w_uq=w_uq, g_kv=g_kv, w_ukv=w_ukv, lb_logits=lb_logits, g_hgrn=g_hgrn,
             w_branch_a=w_branch_a, w_branch_b=w_branch_b, w_out=w_out, g_post=g_post)
    m = dict(g_pre=m_g_pre, w_in=m_w_in, b_gate=m_b_gate, g_q=m_g_q, w_uq=m_w_uq, g_kv=m_g_kv, w_ukv=m_w_ukv, lb_logits=m_lb_logits,
             g_hgrn=m_g_hgrn, w_branch_a=m_w_branch_a, w_branch_b=m_w_branch_b, w_out=m_w_out, g_post=m_g_post)
    v = dict(g_pre=v_g_pre, w_in=v_w_in, b_gate=v_b_gate, g_q=v_g_q, w_uq=v_w_uq, g_kv=v_g_kv, w_ukv=v_w_ukv, lb_logits=v_lb_logits,
             g_hgrn=v_g_hgrn, w_branch_a=v_w_branch_a, w_branch_b=v_w_branch_b, w_out=v_w_out, g_post=v_g_post)
    blocks = {n: _to_block(n, w[n]).astype(BF16) for n in BIG}
    w_in_all = _gather_w_in(blocks["w_in"])
    weights = _LaterWeights(blocks, w_in_all)
    state = {n: [_to_block(n, t[n]) for t in (w, m, v)] for n in BIG}
    exchange = _GradExchange(state)
    loss, grad_x, _, vec_grads = _local_step(
        x[0], loss_target[0], g_pre, _join_chips("w_in", w_in_all), b_gate, g_q, g_kv, lb_logits, g_hgrn, g_post, weights, exchange)
    vec = _vec_pack(vec_grads, loss)
    vec_started, token = _exchange_start([vec], (), "vec_start")
    sums = exchange.finish(token)
    rest = tuple(sums)
    (vec_landed,) = _exchange_wait([vec], (), vec_started, sums[rest[-1]], "vec_wait")
    mine = [sums[n] for n in rest] + [_sum_landed(vec_landed, vec, "sum_vec")]
    theirs = _sibling_exchange(mine, "sibling_grads")
    done = dict(exchange.outs)
    done["w_in"] = _adamw(mine[0], theirs[0], *state["w_in"], "adamw_w_in")
    small = _adamw_small(mine[1:-1], theirs[1:-1], [state[n] for n in rest[1:]], "adamw_late")
    done.update(zip(rest[1:], small))
    outs = [{}, {}, {}, {}]
    for n in BIG:
        for o, val in zip(outs, done[n]):
            o[n] = _from_block(n, val)
    vec_outs, total = _adamw_vec(mine[-1], theirs[-1], w, m, v)
    for o, vals in zip(outs, vec_outs):
        o.update(vals)
    return (total[0, 0], grad_x[None], *[o[n] for o in outs for n in WEIGHTS])
```

```python
import math

import numpy as np
import jax
import jax.numpy as jnp
from jax import lax
from jax.experimental import pallas as pl
from jax.experimental.pallas import tpu as pltpu

F32 = jnp.float32
BF16 = jnp.bfloat16
HIGHEST = lax.Precision.HIGHEST

D = 1024
NH = 8
QK_NOPE, QK_ROPE, V_DIM = 64, 32, 64
Q_LORA, KV_LORA = 768, 256
CHUNK = 64
HG_BLOCK = 32
EPS = 1e-6
LANE = 128
P_MERGE, P_GA, P_HQ, P_HF, P_HI, P_GB, P_CQ, P_CKV, P_KPE = 0, 2048, 2560, 3072, 3584, 4096, 4608, 5376, 5632
D_P = 5760
O_CQ, O_CKV, O_KPE, O_GA, O_HQ, O_HF, O_HI, O_GB, O_MERGE = 0, 768, 1024, 1056, 1568, 2080, 2592, 3104, 3616

TM = 512
TM_MID = 256
TQ = 1024
ONES_LANE = (LANE - 1, 0)
TH = 256
HG_PAIRS = 4
VMEM_LIMIT = 56 * 1024 * 1024

ADAM_LR, ADAM_B1, ADAM_B2, ADAM_EPS, ADAM_WD, ADAM_STEP = 0.001, 0.9, 0.999, 1e-08, 0.01, 10

NT_DIMS = (((1,), (1,)), ((), ()))
TN_DIMS = (((0,), (0,)), ((), ()))


def _params(sem):
    return pltpu.CompilerParams(dimension_semantics=sem, vmem_limit_bytes=VMEM_LIMIT)


def _mm(a, b):
    return jnp.dot(a, b, preferred_element_type=F32)


def _mm_nt(a, b):
    return lax.dot_general(a, b, NT_DIMS, preferred_element_type=F32)


def _mm_tn(a, b):
    return lax.dot_general(a, b, TN_DIMS, preferred_element_type=F32)


def _sigmoid(z):
    return jax.nn.sigmoid(z)


def _rope(v, c, s1, s2):
    return v * c + pltpu.roll(v, 112, 1) * s1 + pltpu.roll(v, 16, 1) * s2


def _rope_t(dy, c, s1, s2):
    return dy * c + pltpu.roll(dy * s1, 16, 1) + pltpu.roll(dy * s2, 112, 1)


def _rope_tables(s):
    f32 = np.float32
    inv = f32(10000.0) ** (-np.arange(0, QK_ROPE, 2, dtype=f32) / f32(QK_ROPE))
    ang = np.arange(s, dtype=f32)[:, None] * inv[None, :]
    cos, sin = np.cos(ang).astype(f32), np.sin(ang).astype(f32)
    z64, z32, o64, o32 = np.zeros((s, 64), f32), np.zeros((s, 32), f32), np.ones((s, 64), f32), np.ones((s, 32), f32)
    z16 = np.zeros((s, 16), f32)
    c = np.concatenate([o64, cos, cos, o32], axis=1)
    s1 = np.concatenate([z64, -sin, z16, z32], axis=1)
    s2 = np.concatenate([z64, z16, sin, z32], axis=1)
    return jnp.asarray(c), jnp.asarray(s1), jnp.asarray(s2)


W_IN_RUNS = ((O_MERGE, 2048, P_MERGE), (O_GA, O_MERGE - O_GA, P_GA), (O_CQ, O_KPE - O_CQ, P_CQ))


def _kpe_block(w_in_t):
    z = lambda n: jnp.zeros((n, w_in_t.shape[1]), w_in_t.dtype)
    return jnp.concatenate([z(64), w_in_t[O_KPE:O_KPE + QK_ROPE], z(32)], axis=0)


def _front_fwd(x, g_pre, w_in_t, w_kpe, tokens=()):
    s = x.shape[0]
    tokens = list(tokens)

    def body(x_ref, g_ref, w_ref, k_ref, *refs):
        o_ref, h_ref = refs[len(tokens):]
        xv = x_ref[...]
        r = lax.rsqrt(jnp.mean(xv * xv, axis=-1, keepdims=True) + EPS)
        h = ((xv * r) * g_ref[...]).astype(BF16)
        h_ref[...] = h
        for row, rows, col in W_IN_RUNS:
            o_ref[:, col:col + rows] = _mm_nt(h, w_ref[row:row + rows, :])
        o_ref[:, P_KPE:P_KPE + LANE] = _mm_nt(h, k_ref[...])

    full = lambda a: pl.BlockSpec(a.shape, lambda i: (0,) * a.ndim)
    return pl.pallas_call(
        body, name="front_fwd", grid=(s // TM,),
        in_specs=[pl.BlockSpec((TM, D), lambda i: (i, 0)), pl.BlockSpec((1, D), lambda i: (0, 0)), full(w_in_t), full(w_kpe)]
        + [pl.BlockSpec((8, LANE), lambda i: (0, 0))] * len(tokens),
        out_specs=[pl.BlockSpec((TM, D_P), lambda i: (i, 0)), pl.BlockSpec((TM, D), lambda i: (i, 0))],
        out_shape=[jax.ShapeDtypeStruct((s, D_P), F32), jax.ShapeDtypeStruct((s, D), BF16)],
        compiler_params=_params(("parallel",)),
    )(x, g_pre, w_in_t, w_kpe, *tokens)


def _norm_rows(v, g):
    r = lax.rsqrt(jnp.mean(v * v, axis=-1, keepdims=True) + EPS)
    return (v * r) * g, r


def _qkv_fwd(proj, g_q, g_kv, w_uq_p, w_k_p, w_v_p, rc, rs1, rs2):
    s = proj.shape[0]

    def body(cq_ref, ckv_ref, kpe_ref, gq_ref, gkv_ref, wq_ref, wk_ref, wv_ref, c_ref, s1_ref, s2_ref, q_ref, k_ref, v_ref):
        c, s1, s2 = c_ref[...], s1_ref[...], s2_ref[...]
        cqn, _ = _norm_rows(cq_ref[...], gq_ref[...])
        ckvn, _ = _norm_rows(ckv_ref[...], gkv_ref[...])
        ckvn = ckvn.astype(BF16)
        qf = _mm(cqn.astype(BF16), wq_ref[...])
        kf = _mm(ckvn, wk_ref[...])
        vf = _mm(ckvn, wv_ref[...])
        kpe = _rope(kpe_ref[...], c, s1, s2)
        lane = lax.broadcasted_iota(jnp.int32, (TM, LANE), 1)
        for h in range(NH):
            blk = slice(h * LANE, (h + 1) * LANE)
            q_ref[h] = _rope(qf[:, blk], c, s1, s2).astype(BF16)
            k_ref[h] = (kf[:, blk] + kpe).astype(BF16)
            v_ref[h] = jnp.where(lane == ONES_LANE[h % 2], 1.0, vf[:, blk]).astype(BF16)

    row = lambda w, j: pl.BlockSpec((TM, w), lambda i: (i, j))
    full = lambda a: pl.BlockSpec(a.shape, lambda i: (0,) * a.ndim)
    hs = jax.ShapeDtypeStruct((NH, s, LANE), BF16)
    return pl.pallas_call(
        body, name="qkv_fwd", grid=(s // TM,),
        in_specs=[row(Q_LORA, P_CQ // Q_LORA), row(KV_LORA, P_CKV // KV_LORA), row(LANE, P_KPE // LANE),
                  full(g_q), full(g_kv), full(w_uq_p), full(w_k_p), full(w_v_p), row(LANE, 0), row(LANE, 0), row(LANE, 0)],
        out_specs=[pl.BlockSpec((NH, TM, LANE), lambda i: (0, i, 0))] * 3,
        out_shape=[hs, hs, hs],
        compiler_params=_params(("parallel",)),
    )(proj, proj, proj, g_q, g_kv, w_uq_p, w_k_p, w_v_p, rc, rs1, rs2)


LOG2E = 1.4426950408889634
QK_SCALE2 = LOG2E / math.sqrt(QK_NOPE + QK_ROPE)


HQ = TQ // 2


def _diag_visible(n):
    row = lax.broadcasted_iota(jnp.int32, (n, n), 0)
    col = lax.broadcasted_iota(jnp.int32, (n, n), 1)
    return (col // CHUNK) <= (row // CHUNK)


def _attn_fwd(q, k, vv):
    s = q.shape[1]

    def body(q_ref, k_ref, v_ref, o_ref, lse_ref):
        i = pl.program_id(1)
        qs = (q_ref[0], q_ref[1])

        def tiles(t, carry, diag):
            rows = pl.ds(pl.multiple_of(t * TQ, TQ), TQ)
            sc = [_mm_nt(qs[hh], k_ref[hh, rows, :]) for hh in range(2)]
            if diag:
                sc = [jnp.where(_diag_visible(TQ), s_, -jnp.inf) for s_ in sc]
            m_new = [jnp.maximum(carry[hh][0], jnp.max(sc[hh], axis=-1, keepdims=True)) for hh in range(2)]
            alpha = [jnp.exp2((carry[hh][0] - m_new[hh]) * QK_SCALE2) for hh in range(2)]
            p = [jnp.exp2((sc[hh] - m_new[hh]) * QK_SCALE2).astype(BF16) for hh in range(2)]
            acc = [alpha[hh] * carry[hh][1] + _mm(p[hh], v_ref[hh, rows, :]) for hh in range(2)]
            return (m_new[0], acc[0]), (m_new[1], acc[1])

        init = (jnp.full((TQ, 1), -jnp.inf, F32), jnp.zeros((TQ, LANE), F32))
        carry = lax.fori_loop(0, i, lambda t, c: tiles(t, c, False), (init, init))
        carry = tiles(i, carry, True)
        lane = lax.broadcasted_iota(jnp.int32, (TQ, LANE), 1)
        out = jnp.zeros((TQ, LANE), F32)
        for hh in range(2):
            m, acc = carry[hh]
            l = jnp.sum(jnp.where(lane == ONES_LANE[hh], acc, 0.0), axis=-1, keepdims=True)
            out = out + jnp.where((lane < V_DIM) == (hh == 0), acc, 0.0) / l
            lse_ref[hh] = jnp.broadcast_to(m * QK_SCALE2 + jnp.log(l) * LOG2E, (TQ, LANE))
        o_ref[...] = out

    return pl.pallas_call(
        body, name="attn_fwd", grid=(NH // 2, s // TQ),
        in_specs=[pl.BlockSpec((2, TQ, LANE), lambda p, i: (p, i, 0)), pl.BlockSpec((2, s, LANE), lambda p, i: (p, 0, 0)),
                  pl.BlockSpec((2, s, LANE), lambda p, i: (p, 0, 0))],
        out_specs=[pl.BlockSpec((TQ, LANE), lambda p, i: (i, p)), pl.BlockSpec((2, TQ, LANE), lambda p, i: (p, i, 0))],
        out_shape=[jax.ShapeDtypeStruct((s, NH * V_DIM), F32), jax.ShapeDtypeStruct((NH, s, LANE), F32)],
        compiler_params=_params(("parallel", "parallel")),
    )(q, k, vv)


def _lower_bound(lbl):
    a0, a1 = lbl[0:1, :], lbl[1:2, :]
    mx = jnp.maximum(a0, a1)
    e0, e1 = jnp.exp(a0 - mx), jnp.exp(a1 - mx)
    return e0 / (e0 + e1)


def _chunk_cumsum(v, reverse=False):
    pos = lax.broadcasted_iota(jnp.int32, v.shape, 0) % HG_BLOCK
    s = 1
    while s < HG_BLOCK:
        if reverse:
            v = v + jnp.where(pos < HG_BLOCK - s, pltpu.roll(v, TH - s, 0), 0.0)
        else:
            v = v + jnp.where(pos >= s, pltpu.roll(v, s, 0), 0.0)
        s *= 2
    return v


def _hgrn_gates(hq, hf, lb):
    sig = _sigmoid(hf)
    f = lb + (1.0 - lb) * sig
    g = jnp.log(f)
    kk = 1.0 - f
    r = lax.broadcasted_iota(jnp.int32, (TH, TH), 0)
    c = lax.broadcasted_iota(jnp.int32, (TH, TH), 1)
    tri = ((r // HG_BLOCK) == (c // HG_BLOCK)) & (r >= c)
    cum = _chunk_cumsum(g)
    nch = TH // HG_BLOCK
    total = _chunks(cum)[:, HG_BLOCK - 1:HG_BLOCK, :]
    lastb = jnp.broadcast_to(total, (nch, HG_BLOCK, hf.shape[-1])).reshape(hf.shape)
    e, ei, ee = jnp.exp(cum), jnp.exp(-cum), jnp.exp(lastb - cum)
    return dict(sig=sig, f=f, kk=kk, tri=tri, cum=cum, total=total, decay=jnp.exp(total), e=e, ei=ei, ee=ee,
                qd=hq * e, ki=kk * ei, ke=kk * ee)


def _chunks(v):
    return v.reshape(TH // HG_BLOCK, HG_BLOCK, v.shape[-1])


def _bmm_nt(a, b):
    return lax.dot_general(a, b, (((2,), (2,)), ((0,), (0,))), preferred_element_type=F32)


def _bmm_nn(a, b):
    return lax.dot_general(a, b, (((2,), (1,)), ((0,), (0,))), preferred_element_type=F32)


def _bmm_tn(a, b):
    return lax.dot_general(a, b, (((1,), (1,)), ((0,), (0,))), preferred_element_type=F32)


def _pair_masks():
    lane = lax.broadcasted_iota(jnp.int32, (TH, LANE), 1)
    kr = lax.broadcasted_iota(jnp.int32, (LANE, LANE), 0)
    kc = lax.broadcasted_iota(jnp.int32, (LANE, LANE), 1)
    return lane < 64, (kr // 64) == (kc // 64)


def _hgrn_fwd(proj, lbl):
    s = proj.shape[0]
    nch = TH // HG_BLOCK

    def body(hq_ref, hf_ref, hi_ref, lbl_ref, o_ref, st_ref, st):
        @pl.when(pl.program_id(1) == 0)
        def _():
            st[...] = jnp.zeros_like(st)

        m0, bd = _pair_masks()
        gt = _hgrn_gates(hq_ref[...], hf_ref[...], _lower_bound(lbl_ref[...]))
        v_b, qd, qd_b = hi_ref[...].astype(BF16), gt["qd"], gt["qd"].astype(BF16)
        ki_b, ke_b = gt["ki"].astype(BF16), gt["ke"].astype(BF16)
        pairs = [slice(u * LANE, (u + 1) * LANE) for u in range(HG_PAIRS)]
        heads = [(lanes, m0 if hh == 0 else jnp.logical_not(m0)) for lanes in pairs for hh in range(2)]
        a_b = [jnp.where(gt["tri"], _mm_nt(jnp.where(mh, qd[:, lanes], 0.0).astype(BF16), ki_b[:, lanes]), 0.0).astype(BF16)
               for lanes, mh in heads]
        intra = [jnp.where(m0, _mm(a_b[2 * u], v_b[:, lanes]), _mm(a_b[2 * u + 1], v_b[:, lanes])) for u, lanes in enumerate(pairs)]
        upd = [_bmm_tn(_chunks(v_b[:, lanes]), _chunks(ke_b[:, lanes])) for lanes in pairs]
        entering = []
        for u, lanes in enumerate(pairs):
            cur, states = st[u], []
            for n in range(nch):
                states.append(cur)
                cur = gt["decay"][n][:, lanes] * cur + jnp.where(bd, upd[u][n], 0.0)
            st[u] = cur
            entering.append(jnp.stack(states))
            st_ref[u] = entering[u]
        for u, lanes in enumerate(pairs):
            o_ref[:, lanes] = intra[u] + _bmm_nt(_chunks(qd_b[:, lanes]), entering[u].astype(BF16)).reshape(TH, LANE)

    wide = HG_PAIRS * LANE
    col = lambda base: pl.BlockSpec((TH, wide), lambda p, i: (i, base // wide + p))
    return pl.pallas_call(
        body, name="hgrn_fwd", grid=(NH // 2 // HG_PAIRS, s // TH),
        in_specs=[col(P_HQ), col(P_HF), col(P_HI), pl.BlockSpec((2, wide), lambda p, i: (0, p))],
        out_specs=[pl.BlockSpec((TH, wide), lambda p, i: (i, p)),
                   pl.BlockSpec((HG_PAIRS, nch, LANE, LANE), lambda p, i: (p, i, 0, 0))],
        out_shape=[jax.ShapeDtypeStruct((s, 512), F32), jax.ShapeDtypeStruct((NH // 2, s // HG_BLOCK, LANE, LANE), F32)],
        scratch_shapes=[pltpu.VMEM((HG_PAIRS, LANE, LANE), F32)],
        compiler_params=_params(("parallel", "arbitrary")),
    )(proj, proj, proj, lbl)


def _group_sum(v):
    low = lax.broadcasted_iota(jnp.int32, (v.shape[0], LANE), 1) < V_DIM
    blocks = []
    for b in range(v.shape[1] // LANE):
        blk = v[:, b * LANE:(b + 1) * LANE]
        s_low = jnp.sum(jnp.where(low, blk, 0.0), axis=-1, keepdims=True)
        s_high = jnp.sum(jnp.where(low, 0.0, blk), axis=-1, keepdims=True)
        blocks.append(jnp.where(low, s_low, s_high))
    return jnp.concatenate(blocks, axis=1)


def _dsilu(z, sg):
    return sg * (1.0 + z * (1.0 - sg))


def _mid(proj, attn, o_raw, x, tgt, g_hg, b_gate, g_post, wa, wb, w_out):
    s = x.shape[0]

    def body(attn_ref, ga_ref, o_ref, gb_ref, mg_ref, x_ref, t_ref, ghg_ref, bg_ref, gp_ref, wa_ref, wb_ref, wo_ref,
             loss_ref, dout_ref, dattn_ref, dga_ref, dor_ref, dgb_ref, dmg_ref, dwo_ref, dwa_ref, dwb_ref, dgp_ref, dbg_ref, dghg_ref):
        @pl.when(pl.program_id(0) == 0)
        def _():
            for rf in (loss_ref, dwo_ref, dwa_ref, dwb_ref, dgp_ref, dbg_ref, dghg_ref):
                rf[...] = jnp.zeros_like(rf)

        attn, za, orw, zb = attn_ref[...], ga_ref[...], o_ref[...], gb_ref[...]
        ghg, gp = ghg_ref[...], gp_ref[...]
        sga, sgb = _sigmoid(za), _sigmoid(zb)
        sa, sb = za * sga, zb * sgb
        ga = attn * sa
        rh = lax.rsqrt(_group_sum(orw * orw) * (1.0 / V_DIM) + EPS)
        on = (orw * rh) * ghg
        gb = on * sb
        ga_b, gb_b = ga.astype(BF16), gb.astype(BF16)
        ya = _mm(ga_b, wa_ref[...])
        yb = _mm(gb_b, wb_ref[...])
        gates = _sigmoid(mg_ref[...] + bg_ref[...])
        g0, g1 = gates[:, :D], gates[:, D:]
        m_b = (g0 * ya + g1 * yb).astype(BF16)
        y = _mm(m_b, wo_ref[...])
        ry = lax.rsqrt(jnp.mean(y * y, axis=-1, keepdims=True) + EPS)
        out = x_ref[...] + (y * ry) * gp
        err = out - t_ref[...]
        loss_ref[...] += 0.5 * jnp.sum(jnp.mean(err * err, axis=-1, keepdims=True), axis=0, keepdims=True)
        dout = err * (1.0 / D)
        dout_ref[...] = dout
        dgp_ref[...] += jnp.sum(dout * (y * ry), axis=0, keepdims=True)
        dgy = dout * gp
        dy = ry * dgy - y * (ry * ry * ry) * jnp.mean(y * dgy, axis=-1, keepdims=True)
        dy_b = dy.astype(BF16)
        dm = _mm_nt(dy_b, wo_ref[...])
        dya_b, dyb_b = (dm * g0).astype(BF16), (dm * g1).astype(BF16)
        dga = _mm_nt(dya_b, wa_ref[...])
        dgb = _mm_nt(dyb_b, wb_ref[...])
        dwo_ref[...] += _mm_tn(m_b, dy_b)
        dwa_ref[...] += _mm_tn(ga_b, dya_b)
        dwb_ref[...] += _mm_tn(gb_b, dyb_b)
        dg0, dg1 = dm * ya, dm * yb
        dmg = jnp.concatenate([dg0 * g0 * (1.0 - g0), dg1 * g1 * (1.0 - g1)], axis=1)
        dmg_ref[...] = dmg.astype(BF16)
        dbg_ref[...] += jnp.sum(dmg, axis=0, keepdims=True)
        dattn_ref[...] = dga * sa
        dga_ref[...] = (dga * attn * _dsilu(za, sga)).astype(BF16)
        dgb_ref[...] = (dgb * on * _dsilu(zb, sgb)).astype(BF16)
        don = dgb * sb
        dghg_ref[...] += jnp.sum(don * (orw * rh), axis=0, keepdims=True)
        dgo = don * ghg
        dor_ref[...] = rh * dgo - orw * (rh * rh * rh) * (_group_sum(orw * dgo) * (1.0 / V_DIM))

    row = lambda w, j=0: pl.BlockSpec((TM_MID, w), lambda i: (i, j))
    full = lambda a: pl.BlockSpec(a.shape, lambda i: (0,) * a.ndim)
    acc = lambda shape: pl.BlockSpec(shape, lambda i: (0, 0))
    sds = jax.ShapeDtypeStruct
    return pl.pallas_call(
        body, name="mid", grid=(s // TM_MID,),
        in_specs=[row(512), row(512, P_GA // 512), row(512), row(512, P_GB // 512), row(2048, P_MERGE // 2048), row(D), row(D),
                  full(g_hg), full(b_gate), full(g_post), full(wa), full(wb), full(w_out)],
        out_specs=[acc((1, 1)), row(D), row(512), row(512), row(512), row(512), row(2048),
                   acc((D, D)), acc((512, D)), acc((512, D)), acc((1, D)), acc((1, 2048)), acc((1, 512))],
        out_shape=[sds((1, 1), F32), sds((s, D), F32), sds((s, 512), F32), sds((s, 512), BF16), sds((s, 512), F32), sds((s, 512), BF16),
                   sds((s, 2048), BF16), sds((D, D), F32), sds((512, D), F32), sds((512, D), F32), sds((1, D), F32),
                   sds((1, 2048), F32), sds((1, 512), F32)],
        compiler_params=_params(("arbitrary",)),
    )(attn, proj, o_raw, proj, proj, x, tgt, g_hg, b_gate, g_post, wa, wb, w_out)


def _attn_bwd(q, k, vv, attn, dattn, lse, token):
    s = q.shape[1]
    nt = s // TQ
    scale = 1.0 / math.sqrt(QK_NOPE + QK_ROPE)

    def body(q_ref, k_ref, v_ref, o_ref, do_ref, lse_ref, token_ref, dq_ref, dk_ref, dv_ref, do_s, delta_s):
        j = pl.program_id(1)

        @pl.when(j == 0)
        def _():
            dq_ref[...] = jnp.zeros_like(dq_ref)
            lane = lax.broadcasted_iota(jnp.int32, (TQ, LANE), 1)

            @pl.loop(0, nt)
            def _(i):
                rows = pl.ds(pl.multiple_of(i * TQ, TQ), TQ)
                do, o = do_ref[rows, :], o_ref[rows, :]
                for hh in range(2):
                    doh = jnp.where((lane < 64) if hh == 0 else (lane >= 64), do, 0.0)
                    do_s[hh, rows, :] = doh.astype(BF16)
                    delta_s[hh, rows, :] = jnp.broadcast_to(jnp.sum(doh * o, axis=-1, keepdims=True), (TQ, LANE))

        kjs, vjs = (k_ref[0], k_ref[1]), (v_ref[0], v_ref[1])

        def tile(hh, start, size, kj, vj, diag):
            rows = pl.ds(pl.multiple_of(start, size), size)
            wide = lambda a: jnp.concatenate([a] * (kj.shape[0] // LANE), axis=1)
            qi, do_b = q_ref[hh, rows, :], do_s[hh, rows, :]
            sc, dp = _mm_nt(qi, kj), _mm_nt(do_b, vj)
            p = jnp.exp2(sc * QK_SCALE2 - wide(lse_ref[hh, rows, :]))
            if diag:
                p = jnp.where(_diag_visible(size), p, 0.0)
            ds_b = (p * (dp - wide(delta_s[hh, rows, :]))).astype(BF16)
            dv, dk = _mm_tn(do_b, p.astype(BF16)), _mm_tn(qi, ds_b)
            dq_ref[hh, rows, :] += _mm(ds_b, kj)
            return dk, dv

        def step(i, carry):
            new = [tile(hh, i * TQ, TQ, kjs[hh], vjs[hh], False) for hh in range(2)]
            return tuple((carry[hh][0] + new[hh][0], carry[hh][1] + new[hh][1]) for hh in range(2))

        def diagonal(hh):
            k0, k1, v0, v1 = kjs[hh][:HQ], kjs[hh][HQ:], vjs[hh][:HQ], vjs[hh][HQ:]
            a = tile(hh, j * TQ, HQ, k0, v0, True)
            b = tile(hh, j * TQ + HQ, HQ, k0, v0, False)
            c = tile(hh, j * TQ + HQ, HQ, k1, v1, True)
            return jnp.concatenate([a[0] + b[0], c[0]], axis=1), jnp.concatenate([a[1] + b[1], c[1]], axis=1)

        carry = lax.fori_loop(j + 1, nt, step, (diagonal(0), diagonal(1)))
        for hh in range(2):
            dk_ref[hh] = carry[hh][0].T * scale
            dv_ref[hh] = carry[hh][1].T

        @pl.when(j == nt - 1)
        def _():
            dq_ref[...] = dq_ref[...] * scale

    whole = pl.BlockSpec((2, s, LANE), lambda p, j: (p, 0, 0))
    tile_spec = pl.BlockSpec((2, TQ, LANE), lambda p, j: (p, j, 0))
    cols = pl.BlockSpec((s, LANE), lambda p, j: (0, p))
    hs = jax.ShapeDtypeStruct((NH, s, LANE), F32)
    return pl.pallas_call(
        body, name="attn_bwd", grid=(NH // 2, nt),
        in_specs=[whole, tile_spec, tile_spec, cols, cols, whole, pl.BlockSpec((8, LANE), lambda p, j: (0, 0))],
        out_specs=[whole, tile_spec, tile_spec],
        out_shape=[hs, hs, hs],
        scratch_shapes=[pltpu.VMEM((2, s, LANE), BF16), pltpu.VMEM((2, s, LANE), F32)],
        compiler_params=_params(("parallel", "arbitrary")),
    )(q, k, vv, attn, dattn, lse, token)


def _hgrn_bwd(proj, lbl, states, do_raw):
    s = proj.shape[0]
    nt = s // TH
    nch = TH // HG_BLOCK

    def body(hq_ref, hf_ref, hi_ref, lbl_ref, st_ref, do_ref, dh_ref, dlbl_ref, dst, dlb):
        step = pl.program_id(1)

        @pl.when(step == 0)
        def _():
            dst[...] = jnp.zeros_like(dst)
            dlb[...] = jnp.zeros_like(dlb)

        m0, bd = _pair_masks()
        lb = _lower_bound(lbl_ref[...])
        gt = _hgrn_gates(hq_ref[...], hf_ref[...], lb)
        do = do_ref[...]
        qd, ki, ke = gt["qd"], gt["ki"], gt["ke"]
        v_b, do_b = hi_ref[...].astype(BF16), do.astype(BF16)
        qd_b, ki_b, ke_b = qd.astype(BF16), ki.astype(BF16), ke.astype(BF16)
        pairs = [slice(u * LANE, (u + 1) * LANE) for u in range(HG_PAIRS)]
        heads = [(lanes, m0 if hh == 0 else jnp.logical_not(m0)) for lanes in pairs for hh in range(2)]
        a_b = [jnp.where(gt["tri"], _mm_nt(jnp.where(mh, qd[:, lanes], 0.0).astype(BF16), ki_b[:, lanes]), 0.0).astype(BF16)
               for lanes, mh in heads]
        doh_b = [jnp.where(mh, do[:, lanes], 0.0).astype(BF16) for lanes, mh in heads]
        da_b = [jnp.where(gt["tri"], _mm_nt(d, v_b[:, lanes]), 0.0).astype(BF16) for d, (lanes, _) in zip(doh_b, heads)]
        dv_p, dqd_p, dki_p = [], [], []
        for u, lanes in enumerate(pairs):
            e, o = 2 * u, 2 * u + 1
            dv_p.append(_mm_tn(a_b[e], doh_b[e]) + _mm_tn(a_b[o], doh_b[o]))
            dqd_p.append(jnp.where(m0, _mm(da_b[e], ki_b[:, lanes]), _mm(da_b[o], ki_b[:, lanes])))
            dki_p.append(jnp.where(m0, _mm_tn(da_b[e], qd_b[:, lanes]), _mm_tn(da_b[o], qd_b[:, lanes])))
        fed = [_bmm_tn(_chunks(do_b[:, lanes]), _chunks(qd_b[:, lanes])) for lanes in pairs]
        leaving = []
        for u, lanes in enumerate(pairs):
            ds, left = dst[u], [None] * nch
            for n in reversed(range(nch)):
                left[n] = ds
                ds = gt["decay"][n][:, lanes] * ds + jnp.where(bd, fed[u][n], 0.0)
            dst[u] = ds
            leaving.append(jnp.stack(left))
        dke_p, dlast_p = [], []
        for u, lanes in enumerate(pairs):
            entering, leaving_b = st_ref[u], leaving[u].astype(BF16)
            dke3 = _bmm_nn(_chunks(v_b[:, lanes]), leaving_b)
            dv_p[u] = dv_p[u] + _bmm_nt(_chunks(ke_b[:, lanes]), leaving_b).reshape(TH, LANE)
            dqd_p[u] = dqd_p[u] + _bmm_nn(_chunks(do_b[:, lanes]), entering.astype(BF16)).reshape(TH, LANE)
            dke_p.append(dke3.reshape(TH, LANE))
            dlast_p.append(jnp.sum(dke3 * _chunks(ke[:, lanes]), axis=1, keepdims=True)
                           + jnp.sum(leaving[u] * entering, axis=1, keepdims=True) * gt["decay"][:, :, lanes])
        cat = lambda parts: jnp.concatenate(parts, axis=-1)
        dv, dqd, dki, dke, dlast = cat(dv_p), cat(dqd_p), cat(dki_p), cat(dke_p), cat(dlast_p)
        dk = dki * gt["ei"] + dke * gt["ee"]
        dcum = dqd * qd - dki * ki - dke * ke
        dg = _chunk_cumsum(dcum, reverse=True) + jnp.broadcast_to(dlast, (nch, HG_BLOCK, dlast.shape[-1])).reshape(dcum.shape)
        sig = gt["sig"]
        df = dg / gt["f"] - dk
        dlb[...] += jnp.sum(df * (1.0 - sig), axis=0, keepdims=True)
        dh_ref[0] = (dqd * gt["e"]).astype(BF16)
        dh_ref[1] = ((df * (1.0 - lb)) * sig * (1.0 - sig)).astype(BF16)
        dh_ref[2] = dv.astype(BF16)

        @pl.when(step == nt - 1)
        def _():
            lb = _lower_bound(lbl_ref[...])
            da0 = dlb[...] * lb * (1.0 - lb)
            dlbl_ref[...] = jnp.concatenate([da0, -da0], axis=0)

    wide = HG_PAIRS * LANE
    col = lambda base: pl.BlockSpec((TH, wide), lambda p, i: (nt - 1 - i, base // wide + p))
    tile = pl.BlockSpec((TH, wide), lambda p, i: (nt - 1 - i, p))
    sds = jax.ShapeDtypeStruct
    return pl.pallas_call(
        body, name="hgrn_bwd", grid=(NH // 2 // HG_PAIRS, nt),
        in_specs=[col(P_HQ), col(P_HF), col(P_HI), pl.BlockSpec((2, wide), lambda p, i: (0, p)),
                  pl.BlockSpec((HG_PAIRS, nch, LANE, LANE), lambda p, i: (p, nt - 1 - i, 0, 0)), tile],
        out_specs=[pl.BlockSpec((3, TH, wide), lambda p, i: (0, nt - 1 - i, p)), pl.BlockSpec((2, wide), lambda p, i: (0, p))],
        out_shape=[sds((3, s, 512), BF16), sds((2, 512), F32)],
        scratch_shapes=[pltpu.VMEM((HG_PAIRS, LANE, LANE), F32), pltpu.VMEM((1, wide), F32)],
        compiler_params=_params(("parallel", "arbitrary")),
    )(proj, proj, proj, lbl, states, do_raw)


def _norm_rows_bwd(v, r, g, dn):
    dgv = dn * g
    return r * dgv - v * (r * r * r) * jnp.mean(v * dgv, axis=-1, keepdims=True)


def _qkv_bwd(proj, dq, dk, dvv, g_q, g_kv, w_uq_p, w_k_p, w_v_p, rc, rs1, rs2):
    s = proj.shape[0]
    head_q = QK_NOPE + QK_ROPE

    def body(cq_ref, ckv_ref, dq_ref, dk_ref, dv_ref, gq_ref, gkv_ref, wq_ref, wk_ref, wv_ref, c_ref, s1_ref, s2_ref,
             dcq_ref, dckv_ref, dkpe_ref, dwq_out, dwkv_out, dgq_ref, dgkv_ref, dwq_ref, dwk_ref, dwv_ref):
        @pl.when(pl.program_id(0) == 0)
        def _():
            for rf in (dwq_ref, dwk_ref, dwv_ref, dgq_ref, dgkv_ref):
                rf[...] = jnp.zeros_like(rf)

        c, s1, s2 = c_ref[...], s1_ref[...], s2_ref[...]
        cq, ckv = cq_ref[...], ckv_ref[...]
        gq, gkv = gq_ref[...], gkv_ref[...]
        cqn, rq = _norm_rows(cq, gq)
        ckvn, rkv = _norm_rows(ckv, gkv)
        cqn_b, ckvn_b = cqn.astype(BF16), ckvn.astype(BF16)
        dqf = jnp.concatenate([_rope_t(dq_ref[h], c, s1, s2) for h in range(NH)], axis=1).astype(BF16)
        dkf = jnp.concatenate([dk_ref[h] for h in range(NH)], axis=1).astype(BF16)
        dvf = jnp.concatenate([dv_ref[h] for h in range(NH)], axis=1).astype(BF16)
        dkpe = dk_ref[0]
        for h in range(1, NH):
            dkpe = dkpe + dk_ref[h]
        lane = lax.broadcasted_iota(jnp.int32, (TM, LANE), 1)
        dkpe = jnp.where((lane >= QK_NOPE) & (lane < QK_NOPE + QK_ROPE), dkpe, 0.0)
        dkpe_ref[...] = _rope_t(dkpe, c, s1, s2).astype(BF16)
        dcqn = _mm_nt(dqf, wq_ref[...])
        dckvn = _mm_nt(dkf, wk_ref[...]) + _mm_nt(dvf, wv_ref[...])
        dwq_ref[...] += _mm_tn(cqn_b, dqf)
        dwk_ref[...] += _mm_tn(ckvn_b, dkf)
        dwv_ref[...] += _mm_tn(ckvn_b, dvf)
        dgq_ref[...] += jnp.sum(dcqn * (cq * rq), axis=0, keepdims=True)
        dgkv_ref[...] += jnp.sum(dckvn * (ckv * rkv), axis=0, keepdims=True)
        dcq_ref[...] = _norm_rows_bwd(cq, rq, gq, dcqn).astype(BF16)
        dckv_ref[...] = _norm_rows_bwd(ckv, rkv, gkv, dckvn).astype(BF16)

        @pl.when(pl.program_id(0) == pl.num_programs(0) - 1)
        def _():
            blk = lambda ref, h: ref[:, h * LANE:(h + 1) * LANE]
            lane = lax.broadcasted_iota(jnp.int32, (Q_LORA, LANE), 1)
            for j in range(NH * head_q // LANE):
                h0, w0 = divmod(j * LANE, head_q)
                first = blk(dwq_ref, h0) if w0 == 0 else pltpu.roll(blk(dwq_ref, h0), LANE - w0, 1)
                second = pltpu.roll(blk(dwq_ref, h0 + 1), head_q - w0, 1)
                dwq_out[:, j * LANE:(j + 1) * LANE] = jnp.where(lane < head_q - w0, first, second).astype(BF16)
            lane = lax.broadcasted_iota(jnp.int32, (KV_LORA, LANE), 1)
            for h in range(NH):
                vals = blk(dwv_ref, h) if h % 2 else pltpu.roll(blk(dwv_ref, h), V_DIM, 1)
                both = jnp.where(lane < QK_NOPE, blk(dwk_ref, h), vals).astype(BF16)
                dwkv_out[h // 2, :, (h % 2) * LANE:(h % 2 + 1) * LANE] = both

    row = lambda w, j=0: pl.BlockSpec((TM, w), lambda i: (i, j))
    full = lambda a: pl.BlockSpec(a.shape, lambda i: (0,) * a.ndim)
    acc = lambda *shape: pl.BlockSpec(shape, lambda i: (0,) * len(shape))
    heads = pl.BlockSpec((NH, TM, LANE), lambda i: (0, i, 0))
    sds = jax.ShapeDtypeStruct
    return pl.pallas_call(
        body, name="qkv_bwd", grid=(s // TM,),
        in_specs=[row(Q_LORA, P_CQ // Q_LORA), row(KV_LORA, P_CKV // KV_LORA), heads, heads, heads,
                  full(g_q), full(g_kv), full(w_uq_p), full(w_k_p), full(w_v_p), row(LANE), row(LANE), row(LANE)],
        out_specs=[row(Q_LORA), row(KV_LORA), row(LANE), acc(Q_LORA, NH * head_q), acc(NH // 2, KV_LORA, 2 * LANE),
                   acc(1, Q_LORA), acc(1, KV_LORA)],
        out_shape=[sds((s, Q_LORA), BF16), sds((s, KV_LORA), BF16), sds((s, LANE), BF16), sds((Q_LORA, NH * head_q), BF16),
                   sds((NH // 2, KV_LORA, 2 * LANE), BF16), sds((1, Q_LORA), F32), sds((1, KV_LORA), F32)],
        scratch_shapes=[pltpu.VMEM((Q_LORA, D), F32), pltpu.VMEM((KV_LORA, D), F32), pltpu.VMEM((KV_LORA, D), F32)],
        compiler_params=_params(("arbitrary",)),
    )(proj, proj, dq, dk, dvv, g_q, g_kv, w_uq_p, w_k_p, w_v_p, rc, rs1, rs2)


def _front_bwd(x, dout, dmg, dga, dh3, dgb, dcq, dckv, dkpe, g_pre, w_in_t, w_kpe, token):
    s = x.shape[0]

    def body(x_ref, do_ref, dmg_ref, dga_ref, dh3_ref, dgb_ref, dcq_ref, dckv_ref, dkpe_ref, g_ref, w_ref, k_ref, token_ref,
             gx_ref, dg_ref):
        @pl.when(pl.program_id(0) == 0)
        def _():
            dg_ref[...] = jnp.zeros_like(dg_ref)

        xv, g = x_ref[...], g_ref[...]
        _, r = _norm_rows(xv, g)
        pieces = ((dmg_ref[...], O_MERGE), (dga_ref[...], O_GA), (dh3_ref[0], O_HQ), (dh3_ref[1], O_HF), (dh3_ref[2], O_HI),
                  (dgb_ref[...], O_GB), (dcq_ref[...], O_CQ), (dckv_ref[...], O_CKV))
        dh = _mm(dkpe_ref[...], k_ref[...])
        for piece, off in pieces:
            dh = dh + _mm(piece, w_ref[off:off + piece.shape[1], :])
        dg_ref[...] += jnp.sum(dh * (xv * r), axis=0, keepdims=True)
        gx_ref[...] = do_ref[...] + _norm_rows_bwd(xv, r, g, dh)

    row = lambda w: pl.BlockSpec((TM, w), lambda i: (i, 0))
    full = lambda a: pl.BlockSpec(a.shape, lambda i: (0,) * a.ndim)
    sds = jax.ShapeDtypeStruct
    return pl.pallas_call(
        body, name="front_bwd", grid=(s // TM,),
        in_specs=[row(D), row(D), row(2048), row(512), pl.BlockSpec((3, TM, 512), lambda i: (0, i, 0)), row(512), row(Q_LORA),
                  row(KV_LORA), row(LANE), full(g_pre), full(w_in_t), full(w_kpe), pl.BlockSpec(memory_space=pl.ANY)],
        out_specs=[row(D), pl.BlockSpec((1, D), lambda i: (0, 0))],
        out_shape=[sds((s, D), F32), sds((1, D), F32)],
        compiler_params=_params(("arbitrary",)),
    )(x, dout, dmg, dga, dh3, dgb, dcq, dckv, dkpe, g_pre, w_in_t, w_kpe, token)


TK_GRAD = 1024


def _win_grad(h, pieces, name):
    s = h.shape[0]
    n = len(pieces)

    def body(h_ref, *refs):
        @pl.when(pl.program_id(0) == 0)
        def _():
            for o_ref in refs[n:]:
                o_ref[...] = jnp.zeros_like(o_ref)

        hv = h_ref[...]
        for d_ref, o_ref in zip(refs[:n], refs[n:]):
            if len(d_ref.shape) == 3:
                for k in range(d_ref.shape[0]):
                    o_ref[k] += _mm_tn(d_ref[k], hv)
            else:
                o_ref[...] += _mm_tn(d_ref[...], hv)

    def in_spec(p):
        if p.ndim == 3:
            return pl.BlockSpec((p.shape[0], TK_GRAD, p.shape[2]), lambda kk: (0, kk, 0))
        return pl.BlockSpec((TK_GRAD, p.shape[1]), lambda kk: (kk, 0))

    out_shapes = [(p.shape[0], p.shape[2], D) if p.ndim == 3 else (p.shape[1], D) for p in pieces]
    return pl.pallas_call(
        body, name=name, grid=(s // TK_GRAD,),
        in_specs=[pl.BlockSpec((TK_GRAD, D), lambda kk: (kk, 0))] + [in_spec(p) for p in pieces],
        out_specs=[pl.BlockSpec(sh, lambda kk, nd=len(sh): (0,) * nd) for sh in out_shapes],
        out_shape=[jax.ShapeDtypeStruct(sh, F32) for sh in out_shapes],
        compiler_params=_params(("arbitrary",)),
    )(h, *pieces)


def _pad_wuq(w_uq):
    rows = w_uq.shape[0]
    w = w_uq.reshape(rows, NH, QK_NOPE + QK_ROPE)
    return jnp.pad(w, ((0, 0), (0, 0), (0, LANE - QK_NOPE - QK_ROPE))).reshape(rows, NH * LANE)


def _pad_wukv(w_ukv):
    heads = w_ukv.shape[1] // (QK_NOPE + V_DIM)
    w = w_ukv.reshape(KV_LORA, heads, QK_NOPE + V_DIM)
    w_k = jnp.pad(w[:, :, :QK_NOPE], ((0, 0), (0, 0), (0, LANE - QK_NOPE))).reshape(KV_LORA, heads * LANE)
    wv = w[:, :, QK_NOPE:].reshape(KV_LORA, heads // 2, 2, 1, V_DIM)
    eye = jnp.eye(2, dtype=w.dtype).reshape(1, 1, 2, 2, 1)
    return w_k, (wv * eye).reshape(KV_LORA, heads * LANE)


def _local_step(x, tgt, g_pre, w_in_t, b_gate, g_q, g_kv, lb_logits, g_hgrn, g_post, weights, exchange=None):
    s = x.shape[0]
    w_kpe = _kpe_block(w_in_t)
    rc, rs1, rs2 = _rope_tables(s)
    g_hg = jnp.tile(g_hgrn, (1, NH))

    proj, h = _front_fwd(x, g_pre, w_in_t, w_kpe, weights.tokens)
    w_uq_p, w_k_p, w_v_p = weights.qkv(h)
    q, k, vv = _qkv_fwd(proj, g_q, g_kv, w_uq_p, w_k_p, w_v_p, rc, rs1, rs2)
    attn, lse = _attn_fwd(q, k, vv)
    o_raw, states = _hgrn_fwd(proj, lb_logits)
    wa, wb, w_out = weights.mid(o_raw)
    (loss, dout, dattn, dga, dor, dgb, dmg, d_wout, d_wa, d_wb, d_gpost, d_bgate, d_ghg) = _mid(
        proj, attn, o_raw, x, tgt, g_hg, b_gate, g_post, wa, wb, w_out)
    w_mg, w_ga, w_gb = _win_grad(h, [dmg, dga, dgb], "win_grad_mid")
    dh3, d_lbl = _hgrn_bwd(proj, lb_logits, states, dor)
    (w_h3,) = _win_grad(h, [dh3], "win_grad_hgrn")
    d_win_rest = jnp.concatenate([w_ga, w_h3[0], w_h3[1], w_h3[2], w_gb, w_mg], axis=0)
    early = dict(w_in_rest=d_win_rest, w_branch_a=d_wa, w_branch_b=d_wb, w_out=d_wout)
    token = exchange.start_early(early) if exchange else jnp.zeros((8, LANE), F32)
    dq, dk, dvv = _attn_bwd(q, k, vv, attn, dattn, lse, token)
    dcq, dckv, dkpe, d_wuq, d_wukv, d_gq, d_gkv = _qkv_bwd(proj, dq, dk, dvv, g_q, g_kv, w_uq_p, w_k_p, w_v_p, rc, rs1, rs2)
    w_cq, w_ckv, w_kp = _win_grad(h, [dcq, dckv, dkpe], "win_grad_qkv")
    d_win_qkv = jnp.concatenate([w_cq, w_ckv, w_kp[64:64 + QK_ROPE]], axis=0)
    late = dict(w_in_qkv=d_win_qkv, w_uq=d_wuq, w_ukv=d_wukv)
    token = exchange.start_late(late) if exchange else jnp.zeros((8, LANE), F32)
    grad_x, d_gpre = _front_bwd(x, dout, dmg, dga, dh3, dgb, dcq, dckv, dkpe, g_pre, w_in_t, w_kpe, token)
    vec_grads = dict(g_pre=d_gpre, b_gate=d_bgate, g_q=d_gq, g_kv=d_gkv, lb_logits=d_lbl, g_hgrn=d_ghg, g_post=d_gpost)
    return loss, grad_x, dict(early, **late), vec_grads


SHARD_SHAPES = (("w_in", (1416, 1024)), ("w_uq", (192, 768)), ("w_ukv", (256, 256)), ("w_branch_a", (512, 256)),
                ("w_branch_b", (512, 256)), ("w_out", (256, 1024)))
BIG = tuple(n for n, _ in SHARD_SHAPES)
ROW_SHARDED = ("w_in", "w_uq", "w_out")
N_CHIPS = 4
QKV_ROWS = Q_LORA + KV_LORA + QK_ROPE
W_IN_FORWARD_CUT = 704


def _to_block(name, a):
    return a[0].T if name == "w_in" else a[0]


def _from_block(name, a):
    return a.T[None] if name == "w_in" else a[None]
VEC_ROWS = (("g_pre", 0, 1024), ("b_gate", 1, 2048), ("g_q", 2, 768), ("g_kv", 3, 256), ("g_hgrn", 6, 64), ("g_post", 7, 1024))
VEC_LB_ROW = 4
VEC_SHAPE = (8, 2048)


def _split_by_chip(name, g):
    a, b = dict(SHARD_SHAPES)[name]
    return g.reshape(N_CHIPS, a, b) if name in ROW_SHARDED else g.reshape(a, N_CHIPS, b).transpose(1, 0, 2)


def _join_chips(name, w):
    a, b = dict(SHARD_SHAPES)[name]
    return w.reshape(N_CHIPS * a, b) if name in ROW_SHARDED else w.transpose(1, 0, 2).reshape(a, N_CHIPS * b)


MESH = pl.DeviceIdType.MESH
HBM = pl.BlockSpec(memory_space=pltpu.HBM)


def _mesh_place():
    x, y, c = lax.axis_index("x"), lax.axis_index("y"), lax.axis_index("c")
    return x, y, c, 2 * x + y, [(1 - x, y), (x, 1 - y), (1 - x, 1 - y)]


def _remote(src, dst, send_sems, recv_sems, k, to):
    return pltpu.make_async_remote_copy(src_ref=src, dst_ref=dst, send_sem=send_sems.at[k], recv_sem=recv_sems.at[k],
                                        device_id=to, device_id_type=MESH)


def _gather_w_in(shard):
    a, b = shard.shape
    cut = W_IN_FORWARD_CUT

    def body(src, out, ici_send, ici_recv, d2d_send, d2d_recv, local_sem):
        x, y, c = lax.axis_index("x"), lax.axis_index("y"), lax.axis_index("c")
        me, xn, yn, dg = 2 * x + y, 2 * (1 - x) + y, 2 * x + (1 - y), 2 * (1 - x) + (1 - y)
        to_x, to_y, sibling = (1 - x, y, c), (x, 1 - y, c), (x, y, 1 - c)
        first, rest = pl.ds(0, cut), pl.ds(cut, a - cut)

        whole = lambda ref, which: ref.at[:, pl.ds(pl.multiple_of(which * (b // 2), b // 2), b // 2)]
        own = pltpu.make_async_copy(src, out.at[me], local_sem)
        own.start()
        sends = [_remote(whole(src, c), whole(out.at[me], c), ici_send, ici_recv, 0, to_x),
                 _remote(whole(src, c), whole(out.at[me], c), ici_send, ici_recv, 1, to_y)]
        for cp in sends:
            cp.start()

        def landed(slot, rows, k, d2d_k, src_dev):
            piece = whole(out.at[slot], c) if rows is None else out.at[slot].at[rows, pl.ds(pl.multiple_of(c * (b // 2), b // 2), b // 2)]
            _remote(piece, piece, ici_send, ici_recv, k, src_dev).wait_recv()
            cp = _remote(piece, piece, d2d_send, d2d_recv, d2d_k, sibling)
            cp.start()
            sends.append(cp)
            return piece

        def pass_on(slot, rows, k, to):
            piece = out.at[slot].at[rows, pl.ds(pl.multiple_of(c * (b // 2), b // 2), b // 2)]
            cp = _remote(piece, piece, ici_send, ici_recv, k, to)
            cp.start()
            sends.append(cp)

        landed(xn, None, 0, 0, to_x)
        pass_on(xn, first, 2, to_y)
        landed(yn, None, 1, 1, to_y)
        pass_on(yn, rest, 3, to_x)
        landed(dg, first, 2, 2, to_y)
        landed(dg, rest, 3, 3, to_x)
        other = pl.ds(pl.multiple_of((1 - c) * (b // 2), b // 2), b // 2)
        for d2d_k, (slot, rows) in enumerate(((xn, None), (yn, None), (dg, first), (dg, rest))):
            piece = out.at[slot].at[:, other] if rows is None else out.at[slot].at[rows, other]
            _remote(piece, piece, d2d_send, d2d_recv, d2d_k, sibling).wait_recv()
        for cp in sends:
            cp.wait_send()
        own.wait()

    sems = pltpu.SemaphoreType.DMA((4,))
    return pl.pallas_call(
        body, name="gather_w_in", in_specs=[HBM], out_specs=HBM,
        out_shape=jax.ShapeDtypeStruct((N_CHIPS, a, b), shard.dtype),
        scratch_shapes=[sems, sems, sems, sems, pltpu.SemaphoreType.DMA],
        compiler_params=pltpu.CompilerParams(has_side_effects=True),
    )(shard)


def _sibling_exchange(srcs, name, after=None):
    n = len(srcs)
    extra = [] if after is None else [after]

    def body(*refs):
        src_refs, outs = refs[:n], refs[n + len(extra):2 * n + len(extra)]
        send_sems, recv_sems = refs[2 * n + len(extra):]
        sibling = (lax.axis_index("x"), lax.axis_index("y"), 1 - lax.axis_index("c"))
        copies = [_remote(src_refs[k], outs[k], send_sems, recv_sems, k, sibling) for k in range(n)]
        for cp in copies:
            cp.start()
        for cp in copies:
            cp.wait()

    sems = pltpu.SemaphoreType.DMA((n,))
    return pl.pallas_call(
        body, name=name, in_specs=[HBM] * n + [pl.BlockSpec(memory_space=pl.ANY)] * len(extra), out_specs=[HBM] * n,
        out_shape=[jax.ShapeDtypeStruct(s.shape, s.dtype) for s in srcs],
        scratch_shapes=[sems, sems],
        compiler_params=pltpu.CompilerParams(has_side_effects=True),
    )(*srcs, *extra)


SEM = pl.BlockSpec(memory_space=pltpu.SEMAPHORE)
DATAFLOW = pltpu.SideEffectType.DATAFLOW_SIDE_EFFECTING


def _exchange_copies(srcs, to_first, src_refs, land_refs, send_sems, recv_sems):
    x, y, c, me, chips = _mesh_place()
    n = len(srcs)
    sends, recvs = [], []
    for k in range(n):
        if k in to_first:
            base = 3 * n + 4 * to_first.index(k)
            sends.append((me != 0, pltpu.make_async_remote_copy(
                src_ref=src_refs[k], dst_ref=land_refs[k].at[me], send_sem=send_sems.at[base], recv_sem=recv_sems.at[base + me],
                device_id=(0, 0, c), device_id_type=MESH)))
            for s in range(1, N_CHIPS):
                recvs.append((me == 0, pltpu.make_async_remote_copy(
                    src_ref=src_refs[k], dst_ref=land_refs[k].at[s], send_sem=send_sems.at[base], recv_sem=recv_sems.at[base + s],
                    device_id=(s // 2, s % 2, c), device_id_type=MESH)))
        else:
            slab = (lambda t, k=k: src_refs[k]) if srcs[k].ndim == 2 else (lambda t, k=k: src_refs[k].at[t])
            for j, (px, py) in enumerate(chips):
                sends.append((None, _remote(slab(2 * px + py), land_refs[k].at[me], send_sems, recv_sems, 3 * k + j, (px, py, c))))
                recvs.append((None, _remote(slab(me), land_refs[k].at[2 * px + py], send_sems, recv_sems, 3 * k + j, (px, py, c))))
    return sends, recvs


def _when(pred, fn):
    if pred is None:
        fn()
    else:
        pl.when(pred)(fn)


def _exchange_start(srcs, to_first, name, after=None):
    n = len(srcs)
    n_sems = 3 * n + 4 * len(to_first)
    lands = [lax.empty((N_CHIPS,) + s.shape[-2:], s.dtype) for s in srcs]
    extra = [] if after is None else [after]

    def body(*refs):
        src_refs, land_refs = refs[:n], refs[n:2 * n]
        send_sems, recv_sems, token = refs[2 * n + len(extra)], refs[2 * n + len(extra) + 1], refs[-1]
        sends, _ = _exchange_copies(srcs, to_first, src_refs, land_refs, send_sems, recv_sems)
        for pred, cp in sends:
            _when(pred, cp.start)
        token[...] = jnp.zeros_like(token)

    hbm = lambda a: pltpu.HBM(a.shape, a.dtype)
    res = pl.pallas_call(
        body, name=name,
        out_shape=[pltpu.SemaphoreType.DMA((n_sems,)), pltpu.SemaphoreType.DMA((n_sems,))] + [hbm(a) for a in srcs + lands]
        + [jax.ShapeDtypeStruct((8, LANE), F32)],
        in_specs=[HBM] * (2 * n) + [pl.BlockSpec(memory_space=pl.ANY)] * len(extra),
        out_specs=[SEM, SEM] + [HBM] * (2 * n) + [pl.BlockSpec(memory_space=pltpu.VMEM)],
        input_output_aliases={i: 2 + i for i in range(2 * n)},
        compiler_params=pltpu.CompilerParams(has_side_effects=DATAFLOW),
    )(*[pltpu.with_memory_space_constraint(a, pltpu.HBM) for a in srcs + lands], *extra)
    return res[:-1], res[-1]


def _exchange_wait(srcs, to_first, started, after, name):
    n = len(srcs)
    send_sems, recv_sems, thru = started[0], started[1], started[2:]

    def body(*refs):
        src_refs, land_refs, send_ref, recv_ref = refs[:n], refs[n:2 * n], refs[2 * n], refs[2 * n + 1]
        sends, recvs = _exchange_copies(srcs, to_first, src_refs, land_refs, send_ref, recv_ref)
        for pred, cp in sends:
            _when(pred, cp.wait_send)
        for pred, cp in recvs:
            _when(pred, cp.wait_recv)

    res = pl.pallas_call(
        body, name=name, out_shape=[pltpu.HBM(a.shape, a.dtype) for a in thru],
        in_specs=[HBM] * (2 * n) + [SEM, SEM, pl.BlockSpec(memory_space=pl.ANY)], out_specs=[HBM] * (2 * n),
        input_output_aliases={i: i for i in range(2 * n)},
        compiler_params=pltpu.CompilerParams(has_side_effects=DATAFLOW),
    )(*thru, send_sems, recv_sems, after)
    return res[n:]


ROW_TILE = 256
COL_TILE = 256


def _block_tiling(a, b):
    if a <= ROW_TILE or a % ROW_TILE == 0:
        ta = min(a, ROW_TILE)
        return a // ta, (ta, b), lambda i: (i, 0)
    return b // COL_TILE, (a, COL_TILE), lambda i: (0, i)


def _sum_landed(land, own, name, first_land=None, first_own=None):
    _, a, b = land.shape
    steps, tile, at = _block_tiling(a, b)
    extra = first_land is not None

    def body(*refs):
        p_ref, own_ref, o_ref = refs[0], refs[1], refs[-1]
        me = 2 * lax.axis_index("x") + lax.axis_index("y")
        own = own_ref[...].astype(F32)
        slot = lambda t: jnp.where(me == t, own, p_ref[t].astype(F32))
        o_ref[...] = ((slot(0) + slot(1)) + slot(2)) + slot(3)
        if extra:
            fp_ref, fo_ref = refs[2], refs[3]
            r = fo_ref.shape[0]

            @pl.when(me == 0)
            def _():
                f = lambda t: fp_ref[t].astype(F32)
                rows = pl.ds(pl.multiple_of(lax.axis_index("c") * r, 8), r)
                o_ref[rows, :] += ((fo_ref[...].astype(F32) + f(1)) + f(2)) + f(3)

    in_specs = [pl.BlockSpec((N_CHIPS,) + tile, lambda i: (0,) + at(i)), pl.BlockSpec(tile, at)]
    args = [land, own]
    if extra:
        r = first_own.shape[0]
        assert tile[0] == a, "the extra rows need whole columns in a step"
        in_specs += [pl.BlockSpec((N_CHIPS, r, tile[1]), lambda i: (0,) + at(i)), pl.BlockSpec((r, tile[1]), at)]
        args += [first_land, first_own]
    return pl.pallas_call(
        body, name=name, grid=(steps,), in_specs=in_specs, out_specs=pl.BlockSpec(tile, at),
        out_shape=jax.ShapeDtypeStruct((a, b), F32), compiler_params=_params(("parallel",)),
    )(*args)


def _sum_landed_small(lands, owns, name):
    n = len(lands)

    def body(*refs):
        me = 2 * lax.axis_index("x") + lax.axis_index("y")
        for p_ref, own_ref, o_ref in zip(refs[:n], refs[n:2 * n], refs[2 * n:]):
            own = own_ref[...].astype(F32)
            slot = lambda t: jnp.where(me == t, own, p_ref[t].astype(F32))
            o_ref[...] = ((slot(0) + slot(1)) + slot(2)) + slot(3)

    return pl.pallas_call(body, name=name, out_shape=[jax.ShapeDtypeStruct(o.shape, F32) for o in owns],
                          compiler_params=_params(()))(*lands, *owns)


def _adamw_small(mine, theirs, states, name):
    n = len(mine)

    def body(*refs):
        ins, outs = refs[:5 * n], refs[5 * n:]
        for k in range(n):
            a_ref, b_ref, w_ref, m_ref, v_ref = ins[5 * k:5 * k + 5]
            g = a_ref[...] + b_ref[...]
            outs[4 * k][...] = g
            outs[4 * k + 1][...], outs[4 * k + 2][...], outs[4 * k + 3][...] = _adamw_math(g, w_ref[...], m_ref[...], v_ref[...])

    args = [t for k in range(n) for t in (mine[k], theirs[k], *states[k])]
    res = pl.pallas_call(body, name=name, out_shape=[jax.ShapeDtypeStruct(mine[k].shape, F32) for k in range(n) for _ in range(4)],
                         compiler_params=_params(()))(*args)
    return [tuple(res[4 * k:4 * k + 4]) for k in range(n)]


def _add_cast(a, b, name):
    def body(a_ref, b_ref, o_ref):
        o_ref[...] = (a_ref[...] + b_ref[...]).astype(BF16)

    return pl.pallas_call(body, name=name, out_shape=jax.ShapeDtypeStruct(a.shape, BF16),
                          compiler_params=_params(()))(a, b)


class _LaterWeights:
    MID = ("w_branch_a", "w_branch_b", "w_out")

    def __init__(self, blocks, after):
        self.qkv_blocks = [_pad_wuq(blocks["w_uq"]), *_pad_wukv(blocks["w_ukv"])]
        self.mid_blocks = [blocks[n] for n in self.MID]
        self.qkv_started, t1 = _exchange_start(self.qkv_blocks, (), "weights_qkv_start", after)
        self.mid_started, t2 = _exchange_start(self.mid_blocks, (), "weights_mid_start", after)
        self.tokens = [t1, t2]

    @staticmethod
    def _whole(blocks, rows_sharded, started, after, name):
        landed = _exchange_wait(blocks, (), started, after, name)
        me = 2 * lax.axis_index("x") + lax.axis_index("y")
        out = []
        for block, land, by_rows in zip(blocks, landed, rows_sharded):
            w = lax.dynamic_update_index_in_dim(land, block, me, 0)
            a, b = block.shape
            out.append(w.reshape(N_CHIPS * a, b) if by_rows else w.transpose(1, 0, 2).reshape(a, N_CHIPS * b))
        return out

    def qkv(self, after):
        return self._whole(self.qkv_blocks, (True, False, False), self.qkv_started, after, "weights_qkv_wait")

    def mid(self, after):
        return self._whole(self.mid_blocks, (False, False, True), self.mid_started, after, "weights_mid_wait")


class _GradExchange:
    EARLY = ("w_in", "w_branch_a", "w_branch_b", "w_out")
    LATE = ("w_uq", "w_ukv")

    def __init__(self, state):
        self.state = state
        self.outs = {}

    @staticmethod
    def _own(slabs):
        return lax.dynamic_index_in_dim(slabs, 2 * lax.axis_index("x") + lax.axis_index("y"), axis=0, keepdims=False)

    def start_early(self, g):
        full = jnp.concatenate([jnp.zeros((QKV_ROWS, D), F32), g["w_in_rest"]], axis=0)
        g = dict(g, w_in=full)
        self.early = [_split_by_chip(n, g[n]).astype(BF16) for n in self.EARLY]
        self.early_started, token = _exchange_start(self.early, (), "grads_early_start")
        return token

    def start_late(self, g):
        self.early_landed = _exchange_wait(self.early, (), self.early_started, g["w_uq"], "grads_early_wait")
        half = QKV_ROWS // 2
        c = lax.axis_index("c")
        mine = lax.dynamic_slice_in_dim(g["w_in_qkv"], c * half, half, axis=0)
        (theirs,) = _sibling_exchange([lax.dynamic_slice_in_dim(g["w_in_qkv"], (1 - c) * half, half, axis=0)], "sibling_qkv_rows")
        self.late = [_split_by_chip("w_uq", g["w_uq"]), g["w_ukv"], _add_cast(mine, theirs, "add_qkv_rows")]
        self.late_started, token = _exchange_start(self.late, (2,), "grads_late_start")
        names = self.EARLY[1:]
        mine = _sum_landed_small(self.early_landed[1:], [self._own(slabs) for slabs in self.early[1:]], "sum_early")
        theirs = _sibling_exchange(mine, "sibling_early", after=token)
        for n, out in zip(names, _adamw_small(mine, theirs, [self.state[n] for n in names], "adamw_early")):
            self.outs[n] = out
        return self.outs[names[-1]][0]

    def finish(self, after):
        late_landed = _exchange_wait(self.late, (2,), self.late_started, after, "grads_late_wait")
        sums = {"w_in": _sum_landed(self.early_landed[0], self._own(self.early[0]), "sum_w_in",
                                    first_land=late_landed[2], first_own=self.late[2])}
        small = _sum_landed_small(late_landed[:2], [self._own(slabs) for slabs in self.late[:2]], "sum_late")
        sums.update(zip(self.LATE, small))
        return sums


def _adamw_math(g, w, m, v):
    nm = ADAM_B1 * m + (1.0 - ADAM_B1) * g
    nv = ADAM_B2 * v + (1.0 - ADAM_B2) * (g * g)
    m_hat = nm / (1.0 - ADAM_B1 ** ADAM_STEP)
    v_hat = nv / (1.0 - ADAM_B2 ** ADAM_STEP)
    return -ADAM_LR * (m_hat / (jnp.sqrt(v_hat) + ADAM_EPS) + ADAM_WD * w), nm, nv


def _adamw(p_mine, p_sibling, w, m, v, name):
    a, b = p_mine.shape
    steps, tile, at = _block_tiling(a, b)

    def body(a_ref, b_ref, w_ref, m_ref, v_ref, g_ref, d_ref, nm_ref, nv_ref):
        g = a_ref[...] + b_ref[...]
        g_ref[...] = g
        d_ref[...], nm_ref[...], nv_ref[...] = _adamw_math(g, w_ref[...], m_ref[...], v_ref[...])

    spec = pl.BlockSpec(tile, at)
    sds = jax.ShapeDtypeStruct((a, b), F32)
    return pl.pallas_call(
        body, name=name, grid=(steps,), in_specs=[spec] * 5, out_specs=[spec] * 4, out_shape=[sds] * 4,
        compiler_params=_params(("parallel",)),
    )(p_mine, p_sibling, w, m, v)


LOSS_AT = (2, 1024)


def _vec_pack(vg, loss):
    names = [n for n, _, _ in VEC_ROWS]

    def body(*refs):
        o_ref = refs[-1]
        lb_ref, loss_ref = refs[len(names)], refs[len(names) + 1]
        o_ref[...] = jnp.zeros_like(o_ref)
        o_ref[LOSS_AT[0]:LOSS_AT[0] + 1, LOSS_AT[1]:LOSS_AT[1] + LANE] = jnp.broadcast_to(loss_ref[...], (1, LANE))
        for (name, row, size), ref in zip(VEC_ROWS, refs):
            if name == "g_hgrn":
                r = lax.broadcasted_iota(jnp.int32, (NH * V_DIM, LANE), 0)
                c = lax.broadcasted_iota(jnp.int32, (NH * V_DIM, LANE), 1)
                fold = ((r % V_DIM) == c).astype(F32)
                o_ref[row:row + 1, 0:LANE] = jnp.dot(ref[...], fold, precision=HIGHEST, preferred_element_type=F32)
            else:
                o_ref[row:row + 1, 0:size] = ref[...]
        o_ref[VEC_LB_ROW:VEC_LB_ROW + 2, 0:512] = lb_ref[...]

    return pl.pallas_call(body, name="vec_pack", out_shape=jax.ShapeDtypeStruct(VEC_SHAPE, F32))(
        *[vg[n] for n in names], vg["lb_logits"], loss)


def _adamw_vec(p_mine, p_sibling, w, m, v):
    names = [n for n, _, _ in VEC_ROWS] + ["lb_logits"]
    k = len(names)

    def body(a_ref, b_ref, *refs):
        ins, outs = refs[:3 * k], refs[3 * k:]
        at = (slice(LOSS_AT[0], LOSS_AT[0] + 1), slice(LOSS_AT[1], LOSS_AT[1] + LANE))
        outs[-1][...] = a_ref[at] + b_ref[at]
        for i, name in enumerate(names):
            if name == "lb_logits":
                rows, cols = slice(VEC_LB_ROW, VEC_LB_ROW + 2), slice(0, 512)
            else:
                _, row, size = VEC_ROWS[i]
                rows, cols = slice(row, row + 1), slice(0, size)
            g = a_ref[rows, cols] + b_ref[rows, cols]
            d, nm, nv = _adamw_math(g, ins[i][...], ins[k + i][...], ins[2 * k + i][...])
            for o_ref, val in zip(outs[4 * i:4 * i + 4], (g, d, nm, nv)):
                o_ref[...] = val

    shapes = [jax.ShapeDtypeStruct(w[n].shape, F32) for n in names for _ in range(4)] + [jax.ShapeDtypeStruct((1, LANE), F32)]
    res = pl.pallas_call(body, name="adamw_vec", out_shape=shapes)(
        p_mine, p_sibling, *[w[n] for n in names], *[m[n] for n in names], *[v[n] for n in names])
    return [{n: res[4 * i + j] for i, n in enumerate(names)} for j in range(4)], res[-1]


WEIGHTS = ("g_pre", "w_in", "b_gate", "g_q", "w_uq", "g_kv", "w_ukv", "lb_logits", "g_hgrn", "w_branch_a", "w_branch_b", "w_out", "g_post")


def kernel(x, g_pre, w_in, b_gate, g_q, w_uq, g_kv, w_ukv, lb_logits, g_hgrn, w_branch_a, w_branch_b, w_out, g_post, loss_target, m_g_pre, m_w_in, m_b_gate, m_g_q, m_w_uq, m_g_kv, m_w_ukv, m_lb_logits, m_g_hgrn, m_w_branch_a, m_w_branch_b, m_w_out, m_g_post, v_g_pre, v_w_in, v_b_gate, v_g_q, v_w_uq, v_g_kv, v_w_ukv, v_lb_logits, v_g_hgrn, v_w_branch_a, v_w_branch_b, v_w_out, v_g_post):
    w = dict(g_pre=g_pre, w_in=w_in, b_gate=b_gate, g_q=g_q, w_uq=w_uq, g_kv=g_kv, w_ukv=w_ukv, lb_logits=lb_logits, g_hgrn=g_hgrn,
             w_branch_a=w_branch_a, w_branch_b=w_branch_b, w_out=w_out, g_post=g_post)
    m = dict(g_pre=m_g_pre, w_in=m_w_in, b_gate=m_b_gate, g_q=m_g_q, w_uq=m_w_uq, g_kv=m_g_kv, w_ukv=m_w_ukv, lb_logits=m_lb_logits,
             g_hgrn=m_g_hgrn, w_branch_a=m_w_branch_a, w_branch_b=m_w_branch_b, w_out=m_w_out, g_post=m_g_post)
    v = dict(g_pre=v_g_pre, w_in=v_w_in, b_gate=v_b_gate, g_q=v_g_q, w_uq=v_w_uq, g_kv=v_g_kv, w_ukv=v_w_ukv, lb_logits=v_lb_logits,
             g_hgrn=v_g_hgrn, w_branch_a=v_w_branch_a, w_branch_b=v_w_branch_b, w_out=v_w_out, g_post=v_g_post)
    blocks = {n: _to_block(n, w[n]).astype(BF16) for n in BIG}
    w_in_all = _gather_w_in(blocks["w_in"])
    weights = _LaterWeights(blocks, w_in_all)
    state = {n: [_to_block(n, t[n]) for t in (w, m, v)] for n in BIG}
    exchange = _GradExchange(state)
    loss, grad_x, _, vec_grads = _local_step(
        x[0], loss_target[0], g_pre, _join_chips("w_in", w_in_all), b_gate, g_q, g_kv, lb_logits, g_hgrn, g_post, weights, exchange)
    vec = _vec_pack(vec_grads, loss)
    vec_started, token = _exchange_start([vec], (), "vec_start")
    sums = exchange.finish(token)
    rest = tuple(sums)
    (vec_landed,) = _exchange_wait([vec], (), vec_started, sums[rest[-1]], "vec_wait")
    mine = [sums[n] for n in rest] + [_sum_landed(vec_landed, vec, "sum_vec")]
    theirs = _sibling_exchange(mine, "sibling_grads")
    done = dict(exchange.outs)
    done["w_in"] = _adamw(mine[0], theirs[0], *state["w_in"], "adamw_w_in")
    small = _adamw_small(mine[1:-1], theirs[1:-1], [state[n] for n in rest[1:]], "adamw_late")
    done.update(zip(rest[1:], small))
    outs = [{}, {}, {}, {}]
    for n in BIG:
        for o, val in zip(outs, done[n]):
            o[n] = _from_block(n, val)
    vec_outs, total = _adamw_vec(mine[-1], theirs[-1], w, m, v)
    for o, vals in zip(outs, vec_outs):
        o.update(vals)
    return (total[0, 0], grad_x[None], *[o[n] for o in outs for n in WEIGHTS])
```

```python
import math

import numpy as np
import jax
import jax.numpy as jnp
from jax import lax
from jax.experimental import pallas as pl
from jax.experimental.pallas import tpu as pltpu

F32 = jnp.float32
BF16 = jnp.bfloat16
HIGHEST = lax.Precision.HIGHEST

D = 1024
NH = 8
QK_NOPE, QK_ROPE, V_DIM = 64, 32, 64
Q_LORA, KV_LORA = 768, 256
CHUNK = 64
HG_BLOCK = 32
EPS = 1e-6
LANE = 128
P_MERGE, P_GA, P_HQ, P_HF, P_HI, P_GB, P_CQ, P_CKV, P_KPE = 0, 2048, 2560, 3072, 3584, 4096, 4608, 5376, 5632
D_P = 5760
O_CQ, O_CKV, O_KPE, O_GA, O_HQ, O_HF, O_HI, O_GB, O_MERGE = 0, 768, 1024, 1056, 1568, 2080, 2592, 3104, 3616

TM = 512
TM_MID = 256
TQ = 1024
ONES_LANE = (LANE - 1, 0)
TH = 256
HG_PAIRS = 4
VMEM_LIMIT = 56 * 1024 * 1024

ADAM_LR, ADAM_B1, ADAM_B2, ADAM_EPS, ADAM_WD, ADAM_STEP = 0.001, 0.9, 0.999, 1e-08, 0.01, 10

NT_DIMS = (((1,), (1,)), ((), ()))
TN_DIMS = (((0,), (0,)), ((), ()))


def _params(sem):
    return pltpu.CompilerParams(dimension_semantics=sem, vmem_limit_bytes=VMEM_LIMIT)


def _mm(a, b):
    return jnp.dot(a, b, preferred_element_type=F32)


def _mm_nt(a, b):
    return lax.dot_general(a, b, NT_DIMS, preferred_element_type=F32)


def _mm_tn(a, b):
    return lax.dot_general(a, b, TN_DIMS, preferred_element_type=F32)


def _sigmoid(z):
    return jax.nn.sigmoid(z)


def _rope(v, c, s1, s2):
    return v * c + pltpu.roll(v, 112, 1) * s1 + pltpu.roll(v, 16, 1) * s2


def _rope_t(dy, c, s1, s2):
    return dy * c + pltpu.roll(dy * s1, 16, 1) + pltpu.roll(dy * s2, 112, 1)


def _rope_tables(s):
    f32 = np.float32
    inv = f32(10000.0) ** (-np.arange(0, QK_ROPE, 2, dtype=f32) / f32(QK_ROPE))
    ang = np.arange(s, dtype=f32)[:, None] * inv[None, :]
    cos, sin = np.cos(ang).astype(f32), np.sin(ang).astype(f32)
    z64, z32, o64, o32 = np.zeros((s, 64), f32), np.zeros((s, 32), f32), np.ones((s, 64), f32), np.ones((s, 32), f32)
    z16 = np.zeros((s, 16), f32)
    c = np.concatenate([o64, cos, cos, o32], axis=1)
    s1 = np.concatenate([z64, -sin, z16, z32], axis=1)
    s2 = np.concatenate([z64, z16, sin, z32], axis=1)
    return jnp.asarray(c), jnp.asarray(s1), jnp.asarray(s2)


W_IN_RUNS = ((O_MERGE, 2048, P_MERGE), (O_GA, O_MERGE - O_GA, P_GA), (O_CQ, O_KPE - O_CQ, P_CQ))


def _kpe_block(w_in_t):
    z = lambda n: jnp.zeros((n, w_in_t.shape[1]), w_in_t.dtype)
    return jnp.concatenate([z(64), w_in_t[O_KPE:O_KPE + QK_ROPE], z(32)], axis=0)


def _front_fwd(x, g_pre, w_in_t, w_kpe, tokens=()):
    s = x.shape[0]
    tokens = list(tokens)

    def body(x_ref, g_ref, w_ref, k_ref, *refs):
        o_ref, h_ref = refs[len(tokens):]
        xv = x_ref[...]
        r = lax.rsqrt(jnp.mean(xv * xv, axis=-1, keepdims=True) + EPS)
        h = ((xv * r) * g_ref[...]).astype(BF16)
        h_ref[...] = h
        for row, rows, col in W_IN_RUNS:
            o_ref[:, col:col + rows] = _mm_nt(h, w_ref[row:row + rows, :])
        o_ref[:, P_KPE:P_KPE + LANE] = _mm_nt(h, k_ref[...])

    full = lambda a: pl.BlockSpec(a.shape, lambda i: (0,) * a.ndim)
    return pl.pallas_call(
        body, name="front_fwd", grid=(s // TM,),
        in_specs=[pl.BlockSpec((TM, D), lambda i: (i, 0)), pl.BlockSpec((1, D), lambda i: (0, 0)), full(w_in_t), full(w_kpe)]
        + [pl.BlockSpec((8, LANE), lambda i: (0, 0))] * len(tokens),
        out_specs=[pl.BlockSpec((TM, D_P), lambda i: (i, 0)), pl.BlockSpec((TM, D), lambda i: (i, 0))],
        out_shape=[jax.ShapeDtypeStruct((s, D_P), F32), jax.ShapeDtypeStruct((s, D), BF16)],
        compiler_params=_params(("parallel",)),
    )(x, g_pre, w_in_t, w_kpe, *tokens)


def _norm_rows(v, g):
    r = lax.rsqrt(jnp.mean(v * v, axis=-1, keepdims=True) + EPS)
    return (v * r) * g, r


def _qkv_fwd(proj, g_q, g_kv, w_uq_p, w_k_p, w_v_p, rc, rs1, rs2):
    s = proj.shape[0]

    def body(cq_ref, ckv_ref, kpe_ref, gq_ref, gkv_ref, wq_ref, wk_ref, wv_ref, c_ref, s1_ref, s2_ref, q_ref, k_ref, v_ref):
        c, s1, s2 = c_ref[...], s1_ref[...], s2_ref[...]
        cqn, _ = _norm_rows(cq_ref[...], gq_ref[...])
        ckvn, _ = _norm_rows(ckv_ref[...], gkv_ref[...])
        ckvn = ckvn.astype(BF16)
        qf = _mm(cqn.astype(BF16), wq_ref[...])
        kf = _mm(ckvn, wk_ref[...])
        vf = _mm(ckvn, wv_ref[...])
        kpe = _rope(kpe_ref[...], c, s1, s2)
        lane = lax.broadcasted_iota(jnp.int32, (TM, LANE), 1)
        for h in range(NH):
            blk = slice(h * LANE, (h + 1) * LANE)
            q_ref[h] = _rope(qf[:, blk], c, s1, s2).astype(BF16)
            k_ref[h] = (kf[:, blk] + kpe).astype(BF16)
            v_ref[h] = jnp.where(lane == ONES_LANE[h % 2], 1.0, vf[:, blk]).astype(BF16)

    row = lambda w, j: pl.BlockSpec((TM, w), lambda i: (i, j))
    full = lambda a: pl.BlockSpec(a.shape, lambda i: (0,) * a.ndim)
    hs = jax.ShapeDtypeStruct((NH, s, LANE), BF16)
    return pl.pallas_call(
        body, name="qkv_fwd", grid=(s // TM,),
        in_specs=[row(Q_LORA, P_CQ // Q_LORA), row(KV_LORA, P_CKV // KV_LORA), row(LANE, P_KPE // LANE),
                  full(g_q), full(g_kv), full(w_uq_p), full(w_k_p), full(w_v_p), row(LANE, 0), row(LANE, 0), row(LANE, 0)],
        out_specs=[pl.BlockSpec((NH, TM, LANE), lambda i: (0, i, 0))] * 3,
        out_shape=[hs, hs, hs],
        compiler_params=_params(("parallel",)),
    )(proj, proj, proj, g_q, g_kv, w_uq_p, w_k_p, w_v_p, rc, rs1, rs2)


LOG2E = 1.4426950408889634
QK_SCALE2 = LOG2E / math.sqrt(QK_NOPE + QK_ROPE)


HQ = TQ // 2


def _diag_visible(n):
    row = lax.broadcasted_iota(jnp.int32, (n, n), 0)
    col = lax.broadcasted_iota(jnp.int32, (n, n), 1)
    return (col // CHUNK) <= (row // CHUNK)


def _attn_fwd(q, k, vv):
    s = q.shape[1]

    def body(q_ref, k_ref, v_ref, o_ref, lse_ref):
        i = pl.program_id(1)
        qs = (q_ref[0], q_ref[1])

        def tiles(t, carry, diag):
            rows = pl.ds(pl.multiple_of(t * TQ, TQ), TQ)
            sc = [_mm_nt(qs[hh], k_ref[hh, rows, :]) for hh in range(2)]
            if diag:
                sc = [jnp.where(_diag_visible(TQ), s_, -jnp.inf) for s_ in sc]
            m_new = [jnp.maximum(carry[hh][0], jnp.max(sc[hh], axis=-1, keepdims=True)) for hh in range(2)]
            alpha = [jnp.exp2((carry[hh][0] - m_new[hh]) * QK_SCALE2) for hh in range(2)]
            p = [jnp.exp2((sc[hh] - m_new[hh]) * QK_SCALE2).astype(BF16) for hh in range(2)]
            acc = [alpha[hh] * carry[hh][1] + _mm(p[hh], v_ref[hh, rows, :]) for hh in range(2)]
            return (m_new[0], acc[0]), (m_new[1], acc[1])

        init = (jnp.full((TQ, 1), -jnp.inf, F32), jnp.zeros((TQ, LANE), F32))
        carry = lax.fori_loop(0, i, lambda t, c: tiles(t, c, False), (init, init))
        carry = tiles(i, carry, True)
        lane = lax.broadcasted_iota(jnp.int32, (TQ, LANE), 1)
        out = jnp.zeros((TQ, LANE), F32)
        for hh in range(2):
            m, acc = carry[hh]
            l = jnp.sum(jnp.where(lane == ONES_LANE[hh], acc, 0.0), axis=-1, keepdims=True)
            out = out + jnp.where((lane < V_DIM) == (hh == 0), acc, 0.0) / l
            lse_ref[hh] = jnp.broadcast_to(m * QK_SCALE2 + jnp.log(l) * LOG2E, (TQ, LANE))
        o_ref[...] = out

    return pl.pallas_call(
        body, name="attn_fwd", grid=(NH // 2, s // TQ),
        in_specs=[pl.BlockSpec((2, TQ, LANE), lambda p, i: (p, i, 0)), pl.BlockSpec((2, s, LANE), lambda p, i: (p, 0, 0)),
                  pl.BlockSpec((2, s, LANE), lambda p, i: (p, 0, 0))],
        out_specs=[pl.BlockSpec((TQ, LANE), lambda p, i: (i, p)), pl.BlockSpec((2, TQ, LANE), lambda p, i: (p, i, 0))],
        out_shape=[jax.ShapeDtypeStruct((s, NH * V_DIM), F32), jax.ShapeDtypeStruct((NH, s, LANE), F32)],
        compiler_params=_params(("parallel", "parallel")),
    )(q, k, vv)


def _lower_bound(lbl):
    a0, a1 = lbl[0:1, :], lbl[1:2, :]
    mx = jnp.maximum(a0, a1)
    e0, e1 = jnp.exp(a0 - mx), jnp.exp(a1 - mx)
    return e0 / (e0 + e1)


def _chunk_cumsum(v, reverse=False):
    pos = lax.broadcasted_iota(jnp.int32, v.shape, 0) % HG_BLOCK
    s = 1
    while s < HG_BLOCK:
        if reverse:
            v = v + jnp.where(pos < HG_BLOCK - s, pltpu.roll(v, TH - s, 0), 0.0)
        else:
            v = v + jnp.where(pos >= s, pltpu.roll(v, s, 0), 0.0)
        s *= 2
    return v


def _hgrn_gates(hq, hf, lb):
    sig = _sigmoid(hf)
    f = lb + (1.0 - lb) * sig
    g = jnp.log(f)
    kk = 1.0 - f
    r = lax.broadcasted_iota(jnp.int32, (TH, TH), 0)
    c = lax.broadcasted_iota(jnp.int32, (TH, TH), 1)
    tri = ((r // HG_BLOCK) == (c // HG_BLOCK)) & (r >= c)
    cum = _chunk_cumsum(g)
    nch = TH // HG_BLOCK
    total = _chunks(cum)[:, HG_BLOCK - 1:HG_BLOCK, :]
    lastb = jnp.broadcast_to(total, (nch, HG_BLOCK, hf.shape[-1])).reshape(hf.shape)
    e, ei, ee = jnp.exp(cum), jnp.exp(-cum), jnp.exp(lastb - cum)
    return dict(sig=sig, f=f, kk=kk, tri=tri, cum=cum, total=total, decay=jnp.exp(total), e=e, ei=ei, ee=ee,
                qd=hq * e, ki=kk * ei, ke=kk * ee)


def _chunks(v):
    return v.reshape(TH // HG_BLOCK, HG_BLOCK, v.shape[-1])


def _bmm_nt(a, b):
    return lax.dot_general(a, b, (((2,), (2,)), ((0,), (0,))), preferred_element_type=F32)


def _bmm_nn(a, b):
    return lax.dot_general(a, b, (((2,), (1,)), ((0,), (0,))), preferred_element_type=F32)


def _bmm_tn(a, b):
    return lax.dot_general(a, b, (((1,), (1,)), ((0,), (0,))), preferred_element_type=F32)


def _pair_masks():
    lane = lax.broadcasted_iota(jnp.int32, (TH, LANE), 1)
    kr = lax.broadcasted_iota(jnp.int32, (LANE, LANE), 0)
    kc = lax.broadcasted_iota(jnp.int32, (LANE, LANE), 1)
    return lane < 64, (kr // 64) == (kc // 64)


def _hgrn_fwd(proj, lbl):
    s = proj.shape[0]
    nch = TH // HG_BLOCK

    def body(hq_ref, hf_ref, hi_ref, lbl_ref, o_ref, st_ref, st):
        @pl.when(pl.program_id(1) == 0)
        def _():
            st[...] = jnp.zeros_like(st)

        m0, bd = _pair_masks()
        gt = _hgrn_gates(hq_ref[...], hf_ref[...], _lower_bound(lbl_ref[...]))
        v_b, qd, qd_b = hi_ref[...].astype(BF16), gt["qd"], gt["qd"].astype(BF16)
        ki_b, ke_b = gt["ki"].astype(BF16), gt["ke"].astype(BF16)
        pairs = [slice(u * LANE, (u + 1) * LANE) for u in range(HG_PAIRS)]
        heads = [(lanes, m0 if hh == 0 else jnp.logical_not(m0)) for lanes in pairs for hh in range(2)]
        a_b = [jnp.where(gt["tri"], _mm_nt(jnp.where(mh, qd[:, lanes], 0.0).astype(BF16), ki_b[:, lanes]), 0.0).astype(BF16)
               for lanes, mh in heads]
        intra = [jnp.where(m0, _mm(a_b[2 * u], v_b[:, lanes]), _mm(a_b[2 * u + 1], v_b[:, lanes])) for u, lanes in enumerate(pairs)]
        upd = [_bmm_tn(_chunks(v_b[:, lanes]), _chunks(ke_b[:, lanes])) for lanes in pairs]
        entering = []
        for u, lanes in enumerate(pairs):
            cur, states = st[u], []
            for n in range(nch):
                states.append(cur)
                cur = gt["decay"][n][:, lanes] * cur + jnp.where(bd, upd[u][n], 0.0)
            st[u] = cur
            entering.append(jnp.stack(states))
            st_ref[u] = entering[u]
        for u, lanes in enumerate(pairs):
            o_ref[:, lanes] = intra[u] + _bmm_nt(_chunks(qd_b[:, lanes]), entering[u].astype(BF16)).reshape(TH, LANE)

    wide = HG_PAIRS * LANE
    col = lambda base: pl.BlockSpec((TH, wide), lambda p, i: (i, base // wide + p))
    return pl.pallas_call(
        body, name="hgrn_fwd", grid=(NH // 2 // HG_PAIRS, s // TH),
        in_specs=[col(P_HQ), col(P_HF), col(P_HI), pl.BlockSpec((2, wide), lambda p, i: (0, p))],
        out_specs=[pl.BlockSpec((TH, wide), lambda p, i: (i, p)),
                   pl.BlockSpec((HG_PAIRS, nch, LANE, LANE), lambda p, i: (p, i, 0, 0))],
        out_shape=[jax.ShapeDtypeStruct((s, 512), F32), jax.ShapeDtypeStruct((NH // 2, s // HG_BLOCK, LANE, LANE), F32)],
        scratch_shapes=[pltpu.VMEM((HG_PAIRS, LANE, LANE), F32)],
        compiler_params=_params(("parallel", "arbitrary")),
    )(proj, proj, proj, lbl)


def _group_sum(v):
    low = lax.broadcasted_iota(jnp.int32, (v.shape[0], LANE), 1) < V_DIM
    blocks = []
    for b in range(v.shape[1] // LANE):
        blk = v[:, b * LANE:(b + 1) * LANE]
        s_low = jnp.sum(jnp.where(low, blk, 0.0), axis=-1, keepdims=True)
        s_high = jnp.sum(jnp.where(low, 0.0, blk), axis=-1, keepdims=True)
        blocks.append(jnp.where(low, s_low, s_high))
    return jnp.concatenate(blocks, axis=1)


def _dsilu(z, sg):
    return sg * (1.0 + z * (1.0 - sg))


def _mid(proj, attn, o_raw, x, tgt, g_hg, b_gate, g_post, wa, wb, w_out):
    s = x.shape[0]

    def body(attn_ref, ga_ref, o_ref, gb_ref, mg_ref, x_ref, t_ref, ghg_ref, bg_ref, gp_ref, wa_ref, wb_ref, wo_ref,
             loss_ref, dout_ref, dattn_ref, dga_ref, dor_ref, dgb_ref, dmg_ref, dwo_ref, dwa_ref, dwb_ref, dgp_ref, dbg_ref, dghg_ref):
        @pl.when(pl.program_id(0) == 0)
        def _():
            for rf in (loss_ref, dwo_ref, dwa_ref, dwb_ref, dgp_ref, dbg_ref, dghg_ref):
                rf[...] = jnp.zeros_like(rf)

        attn, za, orw, zb = attn_ref[...], ga_ref[...], o_ref[...], gb_ref[...]
        ghg, gp = ghg_ref[...], gp_ref[...]
        sga, sgb = _sigmoid(za), _sigmoid(zb)
        sa, sb = za * sga, zb * sgb
        ga = attn * sa
        rh = lax.rsqrt(_group_sum(orw * orw) * (1.0 / V_DIM) + EPS)
        on = (orw * rh) * ghg
        gb = on * sb
        ga_b, gb_b = ga.astype(BF16), gb.astype(BF16)
        ya = _mm(ga_b, wa_ref[...])
        yb = _mm(gb_b, wb_ref[...])
        gates = _sigmoid(mg_ref[...] + bg_ref[...])
        g0, g1 = gates[:, :D], gates[:, D:]
        m_b = (g0 * ya + g1 * yb).astype(BF16)
        y = _mm(m_b, wo_ref[...])
        ry = lax.rsqrt(jnp.mean(y * y, axis=-1, keepdims=True) + EPS)
        out = x_ref[...] + (y * ry) * gp
        err = out - t_ref[...]
        loss_ref[...] += 0.5 * jnp.sum(jnp.mean(err * err, axis=-1, keepdims=True), axis=0, keepdims=True)
        dout = err * (1.0 / D)
        dout_ref[...] = dout
        dgp_ref[...] += jnp.sum(dout * (y * ry), axis=0, keepdims=True)
        dgy = dout * gp
        dy = ry * dgy - y * (ry * ry * ry) * jnp.mean(y * dgy, axis=-1, keepdims=True)
        dy_b = dy.astype(BF16)
        dm = _mm_nt(dy_b, wo_ref[...])
        dya_b, dyb_b = (dm * g0).astype(BF16), (dm * g1).astype(BF16)
        dga = _mm_nt(dya_b, wa_ref[...])
        dgb = _mm_nt(dyb_b, wb_ref[...])
        dwo_ref[...] += _mm_tn(m_b, dy_b)
        dwa_ref[...] += _mm_tn(ga_b, dya_b)
        dwb_ref[...] += _mm_tn(gb_b, dyb_b)
        dg0, dg1 = dm * ya, dm * yb
        dmg = jnp.concatenate([dg0 * g0 * (1.0 - g0), dg1 * g1 * (1.0 - g1)], axis=1)
        dmg_ref[...] = dmg.astype(BF16)
        dbg_ref[...] += jnp.sum(dmg, axis=0, keepdims=True)
        dattn_ref[...] = dga * sa
        dga_ref[...] = (dga * attn * _dsilu(za, sga)).astype(BF16)
        dgb_ref[...] = (dgb * on * _dsilu(zb, sgb)).astype(BF16)
        don = dgb * sb
        dghg_ref[...] += jnp.sum(don * (orw * rh), axis=0, keepdims=True)
        dgo = don * ghg
        dor_ref[...] = rh * dgo - orw * (rh * rh * rh) * (_group_sum(orw * dgo) * (1.0 / V_DIM))

    row = lambda w, j=0: pl.BlockSpec((TM_MID, w), lambda i: (i, j))
    full = lambda a: pl.BlockSpec(a.shape, lambda i: (0,) * a.ndim)
    acc = lambda shape: pl.BlockSpec(shape, lambda i: (0, 0))
    sds = jax.ShapeDtypeStruct
    return pl.pallas_call(
        body, name="mid", grid=(s // TM_MID,),
        in_specs=[row(512), row(512, P_GA // 512), row(512), row(512, P_GB // 512), row(2048, P_MERGE // 2048), row(D), row(D),
                  full(g_hg), full(b_gate), full(g_post), full(wa), full(wb), full(w_out)],
        out_specs=[acc((1, 1)), row(D), row(512), row(512), row(512), row(512), row(2048),
                   acc((D, D)), acc((512, D)), acc((512, D)), acc((1, D)), acc((1, 2048)), acc((1, 512))],
        out_shape=[sds((1, 1), F32), sds((s, D), F32), sds((s, 512), F32), sds((s, 512), BF16), sds((s, 512), F32), sds((s, 512), BF16),
                   sds((s, 2048), BF16), sds((D, D), F32), sds((512, D), F32), sds((512, D), F32), sds((1, D), F32),
                   sds((1, 2048), F32), sds((1, 512), F32)],
        compiler_params=_params(("arbitrary",)),
    )(attn, proj, o_raw, proj, proj, x, tgt, g_hg, b_gate, g_post, wa, wb, w_out)


def _attn_bwd(q, k, vv, attn, dattn, lse, token):
    s = q.shape[1]
    nt = s // TQ
    scale = 1.0 / math.sqrt(QK_NOPE + QK_ROPE)

    def body(q_ref, k_ref, v_ref, o_ref, do_ref, lse_ref, token_ref, dq_ref, dk_ref, dv_ref, do_s, delta_s):
        j = pl.program_id(1)

        @pl.when(j == 0)
        def _():
            dq_ref[...] = jnp.zeros_like(dq_ref)
            lane = lax.broadcasted_iota(jnp.int32, (TQ, LANE), 1)

            @pl.loop(0, nt)
            def _(i):
                rows = pl.ds(pl.multiple_of(i * TQ, TQ), TQ)
                do, o = do_ref[rows, :], o_ref[rows, :]
                for hh in range(2):
                    doh = jnp.where((lane < 64) if hh == 0 else (lane >= 64), do, 0.0)
                    do_s[hh, rows, :] = doh.astype(BF16)
                    delta_s[hh, rows, :] = jnp.broadcast_to(jnp.sum(doh * o, axis=-1, keepdims=True), (TQ, LANE))

        kjs, vjs = (k_ref[0], k_ref[1]), (v_ref[0], v_ref[1])

        def tile(hh, start, size, kj, vj, diag):
            rows = pl.ds(pl.multiple_of(start, size), size)
            wide = lambda a: jnp.concatenate([a] * (kj.shape[0] // LANE), axis=1)
            qi, do_b = q_ref[hh, rows, :], do_s[hh, rows, :]
            sc, dp = _mm_nt(qi, kj), _mm_nt(do_b, vj)
            p = jnp.exp2(sc * QK_SCALE2 - wide(lse_ref[hh, rows, :]))
            if diag:
                p = jnp.where(_diag_visible(size), p, 0.0)
            ds_b = (p * (dp - wide(delta_s[hh, rows, :]))).astype(BF16)
            dv, dk = _mm_tn(do_b, p.astype(BF16)), _mm_tn(qi, ds_b)
            dq_ref[hh, rows, :] += _mm(ds_b, kj)
            return dk, dv

        def step(i, carry):
            new = [tile(hh, i * TQ, TQ, kjs[hh], vjs[hh], False) for hh in range(2)]
            return tuple((carry[hh][0] + new[hh][0], carry[hh][1] + new[hh][1]) for hh in range(2))

        def diagonal(hh):
            k0, k1, v0, v1 = kjs[hh][:HQ], kjs[hh][HQ:], vjs[hh][:HQ], vjs[hh][HQ:]
            a = tile(hh, j * TQ, HQ, k0, v0, True)
            b = tile(hh, j * TQ + HQ, HQ, k0, v0, False)
            c = tile(hh, j * TQ + HQ, HQ, k1, v1, True)
            return jnp.concatenate([a[0] + b[0], c[0]], axis=1), jnp.concatenate([a[1] + b[1], c[1]], axis=1)

        carry = lax.fori_loop(j + 1, nt, step, (diagonal(0), diagonal(1)))
        for hh in range(2):
            dk_ref[hh] = carry[hh][0].T * scale
            dv_ref[hh] = carry[hh][1].T

        @pl.when(j == nt - 1)
        def _():
            dq_ref[...] = dq_ref[...] * scale

    whole = pl.BlockSpec((2, s, LANE), lambda p, j: (p, 0, 0))
    tile_spec = pl.BlockSpec((2, TQ, LANE), lambda p, j: (p, j, 0))
    cols = pl.BlockSpec((s, LANE), lambda p, j: (0, p))
    hs = jax.ShapeDtypeStruct((NH, s, LANE), F32)
    return pl.pallas_call(
        body, name="attn_bwd", grid=(NH // 2, nt),
        in_specs=[whole, tile_spec, tile_spec, cols, cols, whole, pl.BlockSpec((8, LANE), lambda p, j: (0, 0))],
        out_specs=[whole, tile_spec, tile_spec],
        out_shape=[hs, hs, hs],
        scratch_shapes=[pltpu.VMEM((2, s, LANE), BF16), pltpu.VMEM((2, s, LANE), F32)],
        compiler_params=_params(("parallel", "arbitrary")),
    )(q, k, vv, attn, dattn, lse, token)


def _hgrn_bwd(proj, lbl, states, do_raw):
    s = proj.shape[0]
    nt = s // TH
    nch = TH // HG_BLOCK

    def body(hq_ref, hf_ref, hi_ref, lbl_ref, st_ref, do_ref, dh_ref, dlbl_ref, dst, dlb):
        step = pl.program_id(1)

        @pl.when(step == 0)
        def _():
            dst[...] = jnp.zeros_like(dst)
            dlb[...] = jnp.zeros_like(dlb)

        m0, bd = _pair_masks()
        lb = _lower_bound(lbl_ref[...])
        gt = _hgrn_gates(hq_ref[...], hf_ref[...], lb)
        do = do_ref[...]
        qd, ki, ke = gt["qd"], gt["ki"], gt["ke"]
        v_b, do_b = hi_ref[...].astype(BF16), do.astype(BF16)
        qd_b, ki_b, ke_b = qd.astype(BF16), ki.astype(BF16), ke.astype(BF16)
        pairs = [slice(u * LANE, (u + 1) * LANE) for u in range(HG_PAIRS)]
        heads = [(lanes, m0 if hh == 0 else jnp.logical_not(m0)) for lanes in pairs for hh in range(2)]
        a_b = [jnp.where(gt["tri"], _mm_nt(jnp.where(mh, qd[:, lanes], 0.0).astype(BF16), ki_b[:, lanes]), 0.0).astype(BF16)
               for lanes, mh in heads]
        doh_b = [jnp.where(mh, do[:, lanes], 0.0).astype(BF16) for lanes, mh in heads]
        da_b = [jnp.where(gt["tri"], _mm_nt(d, v_b[:, lanes]), 0.0).astype(BF16) for d, (lanes, _) in zip(doh_b, heads)]
        dv_p, dqd_p, dki_p = [], [], []
        for u, lanes in enumerate(pairs):
            e, o = 2 * u, 2 * u + 1
            dv_p.append(_mm_tn(a_b[e], doh_b[e]) + _mm_tn(a_b[o], doh_b[o]))
            dqd_p.append(jnp.where(m0, _mm(da_b[e], ki_b[:, lanes]), _mm(da_b[o], ki_b[:, lanes])))
            dki_p.append(jnp.where(m0, _mm_tn(da_b[e], qd_b[:, lanes]), _mm_tn(da_b[o], qd_b[:, lanes])))
        fed = [_bmm_tn(_chunks(do_b[:, lanes]), _chunks(qd_b[:, lanes])) for lanes in pairs]
        leaving = []
        for u, lanes in enumerate(pairs):
            ds, left = dst[u], [None] * nch
            for n in reversed(range(nch)):
                left[n] = ds
                ds = gt["decay"][n][:, lanes] * ds + jnp.where(bd, fed[u][n], 0.0)
            dst[u] = ds
            leaving.append(jnp.stack(left))
        dke_p, dlast_p = [], []
        for u, lanes in enumerate(pairs):
            entering, leaving_b = st_ref[u], leaving[u].astype(BF16)
            dke3 = _bmm_nn(_chunks(v_b[:, lanes]), leaving_b)
            dv_p[u] = dv_p[u] + _bmm_nt(_chunks(ke_b[:, lanes]), leaving_b).reshape(TH, LANE)
            dqd_p[u] = dqd_p[u] + _bmm_nn(_chunks(do_b[:, lanes]), entering.astype(BF16)).reshape(TH, LANE)
            dke_p.append(dke3.reshape(TH, LANE))
            dlast_p.append(jnp.sum(dke3 * _chunks(ke[:, lanes]), axis=1, keepdims=True)
                           + jnp.sum(leaving[u] * entering, axis=1, keepdims=True) * gt["decay"][:, :, lanes])
        cat = lambda parts: jnp.concatenate(parts, axis=-1)
        dv, dqd, dki, dke, dlast = cat(dv_p), cat(dqd_p), cat(dki_p), cat(dke_p), cat(dlast_p)
        dk = dki * gt["ei"] + dke * gt["ee"]
        dcum = dqd * qd - dki * ki - dke * ke
        dg = _chunk_cumsum(dcum, reverse=True) + jnp.broadcast_to(dlast, (nch, HG_BLOCK, dlast.shape[-1])).reshape(dcum.shape)
        sig = gt["sig"]
        df = dg / gt["f"] - dk
        dlb[...] += jnp.sum(df * (1.0 - sig), axis=0, keepdims=True)
        dh_ref[0] = (dqd * gt["e"]).astype(BF16)
        dh_ref[1] = ((df * (1.0 - lb)) * sig * (1.0 - sig)).astype(BF16)
        dh_ref[2] = dv.astype(BF16)

        @pl.when(step == nt - 1)
        def _():
            lb = _lower_bound(lbl_ref[...])
            da0 = dlb[...] * lb * (1.0 - lb)
            dlbl_ref[...] = jnp.concatenate([da0, -da0], axis=0)

    wide = HG_PAIRS * LANE
    col = lambda base: pl.BlockSpec((TH, wide), lambda p, i: (nt - 1 - i, base // wide + p))
    tile = pl.BlockSpec((TH, wide), lambda p, i: (nt - 1 - i, p))
    sds = jax.ShapeDtypeStruct
    return pl.pallas_call(
        body, name="hgrn_bwd", grid=(NH // 2 // HG_PAIRS, nt),
        in_specs=[col(P_HQ), col(P_HF), col(P_HI), pl.BlockSpec((2, wide), lambda p, i: (0, p)),
                  pl.BlockSpec((HG_PAIRS, nch, LANE, LANE), lambda p, i: (p, nt - 1 - i, 0, 0)), tile],
        out_specs=[pl.BlockSpec((3, TH, wide), lambda p, i: (0, nt - 1 - i, p)), pl.BlockSpec((2, wide), lambda p, i: (0, p))],
        out_shape=[sds((3, s, 512), BF16), sds((2, 512), F32)],
        scratch_shapes=[pltpu.VMEM((HG_PAIRS, LANE, LANE), F32), pltpu.VMEM((1, wide), F32)],
        compiler_params=_params(("parallel", "arbitrary")),
    )(proj, proj, proj, lbl, states, do_raw)


def _norm_rows_bwd(v, r, g, dn):
    dgv = dn * g
    return r * dgv - v * (r * r * r) * jnp.mean(v * dgv, axis=-1, keepdims=True)


def _qkv_bwd(proj, dq, dk, dvv, g_q, g_kv, w_uq_p, w_k_p, w_v_p, rc, rs1, rs2):
    s = proj.shape[0]
    head_q = QK_NOPE + QK_ROPE

    def body(cq_ref, ckv_ref, dq_ref, dk_ref, dv_ref, gq_ref, gkv_ref, wq_ref, wk_ref, wv_ref, c_ref, s1_ref, s2_ref,
             dcq_ref, dckv_ref, dkpe_ref, dwq_out, dwkv_out, dgq_ref, dgkv_ref, dwq_ref, dwk_ref, dwv_ref):
        @pl.when(pl.program_id(0) == 0)
        def _():
            for rf in (dwq_ref, dwk_ref, dwv_ref, dgq_ref, dgkv_ref):
                rf[...] = jnp.zeros_like(rf)

        c, s1, s2 = c_ref[...], s1_ref[...], s2_ref[...]
        cq, ckv = cq_ref[...], ckv_ref[...]
        gq, gkv = gq_ref[...], gkv_ref[...]
        cqn, rq = _norm_rows(cq, gq)
        ckvn, rkv = _norm_rows(ckv, gkv)
        cqn_b, ckvn_b = cqn.astype(BF16), ckvn.astype(BF16)
        dqf = jnp.concatenate([_rope_t(dq_ref[h], c, s1, s2) for h in range(NH)], axis=1).astype(BF16)
        dkf = jnp.concatenate([dk_ref[h] for h in range(NH)], axis=1).astype(BF16)
        dvf = jnp.concatenate([dv_ref[h] for h in range(NH)], axis=1).astype(BF16)
        dkpe = dk_ref[0]
        for h in range(1, NH):
            dkpe = dkpe + dk_ref[h]
        lane = lax.broadcasted_iota(jnp.int32, (TM, LANE), 1)
        dkpe = jnp.where((lane >= QK_NOPE) & (lane < QK_NOPE + QK_ROPE), dkpe, 0.0)
        dkpe_ref[...] = _rope_t(dkpe, c, s1, s2).astype(BF16)
        dcqn = _mm_nt(dqf, wq_ref[...])
        dckvn = _mm_nt(dkf, wk_ref[...]) + _mm_nt(dvf, wv_ref[...])
        dwq_ref[...] += _mm_tn(cqn_b, dqf)
        dwk_ref[...] += _mm_tn(ckvn_b, dkf)
        dwv_ref[...] += _mm_tn(ckvn_b, dvf)
        dgq_ref[...] += jnp.sum(dcqn * (cq * rq), axis=0, keepdims=True)
        dgkv_ref[...] += jnp.sum(dckvn * (ckv * rkv), axis=0, keepdims=True)
        dcq_ref[...] = _norm_rows_bwd(cq, rq, gq, dcqn).astype(BF16)
        dckv_ref[...] = _norm_rows_bwd(ckv, rkv, gkv, dckvn).astype(BF16)

        @pl.when(pl.program_id(0) == pl.num_programs(0) - 1)
        def _():
            blk = lambda ref, h: ref[:, h * LANE:(h + 1) * LANE]
            lane = lax.broadcasted_iota(jnp.int32, (Q_LORA, LANE), 1)
            for j in range(NH * head_q // LANE):
                h0, w0 = divmod(j * LANE, head_q)
                first = blk(dwq_ref, h0) if w0 == 0 else pltpu.roll(blk(dwq_ref, h0), LANE - w0, 1)
                second = pltpu.roll(blk(dwq_ref, h0 + 1), head_q - w0, 1)
                dwq_out[:, j * LANE:(j + 1) * LANE] = jnp.where(lane < head_q - w0, first, second).astype(BF16)
            lane = lax.broadcasted_iota(jnp.int32, (KV_LORA, LANE), 1)
            for h in range(NH):
                vals = blk(dwv_ref, h) if h % 2 else pltpu.roll(blk(dwv_ref, h), V_DIM, 1)
                both = jnp.where(lane < QK_NOPE, blk(dwk_ref, h), vals).astype(BF16)
                dwkv_out[h // 2, :, (h % 2) * LANE:(h % 2 + 1) * LANE] = both

    row = lambda w, j=0: pl.BlockSpec((TM, w), lambda i: (i, j))
    full = lambda a: pl.BlockSpec(a.shape, lambda i: (0,) * a.ndim)
    acc = lambda *shape: pl.BlockSpec(shape, lambda i: (0,) * len(shape))
    heads = pl.BlockSpec((NH, TM, LANE), lambda i: (0, i, 0))
    sds = jax.ShapeDtypeStruct
    return pl.pallas_call(
        body, name="qkv_bwd", grid=(s // TM,),
        in_specs=[row(Q_LORA, P_CQ // Q_LORA), row(KV_LORA, P_CKV // KV_LORA), heads, heads, heads,
                  full(g_q), full(g_kv), full(w_uq_p), full(w_k_p), full(w_v_p), row(LANE), row(LANE), row(LANE)],
        out_specs=[row(Q_LORA), row(KV_LORA), row(LANE), acc(Q_LORA, NH * head_q), acc(NH // 2, KV_LORA, 2 * LANE),
                   acc(1, Q_LORA), acc(1, KV_LORA)],
        out_shape=[sds((s, Q_LORA), BF16), sds((s, KV_LORA), BF16), sds((s, LANE), BF16), sds((Q_LORA, NH * head_q), BF16),
                   sds((NH // 2, KV_LORA, 2 * LANE), BF16), sds((1, Q_LORA), F32), sds((1, KV_LORA), F32)],
        scratch_shapes=[pltpu.VMEM((Q_LORA, D), F32), pltpu.VMEM((KV_LORA, D), F32), pltpu.VMEM((KV_LORA, D), F32)],
        compiler_params=_params(("arbitrary",)),
    )(proj, proj, dq, dk, dvv, g_q, g_kv, w_uq_p, w_k_p, w_v_p, rc, rs1, rs2)


def _front_bwd(x, dout, dmg, dga, dh3, dgb, dcq, dckv, dkpe, g_pre, w_in_t, w_kpe, token):
    s = x.shape[0]

    def body(x_ref, do_ref, dmg_ref, dga_ref, dh3_ref, dgb_ref, dcq_ref, dckv_ref, dkpe_ref, g_ref, w_ref, k_ref, token_ref,
             gx_ref, dg_ref):
        @pl.when(pl.program_id(0) == 0)
        def _():
            dg_ref[...] = jnp.zeros_like(dg_ref)

        xv, g = x_ref[...], g_ref[...]
        _, r = _norm_rows(xv, g)
        pieces = ((dmg_ref[...], O_MERGE), (dga_ref[...], O_GA), (dh3_ref[0], O_HQ), (dh3_ref[1], O_HF), (dh3_ref[2], O_HI),
                  (dgb_ref[...], O_GB), (dcq_ref[...], O_CQ), (dckv_ref[...], O_CKV))
        dh = _mm(dkpe_ref[...], k_ref[...])
        for piece, off in pieces:
            dh = dh + _mm(piece, w_ref[off:off + piece.shape[1], :])
        dg_ref[...] += jnp.sum(dh * (xv * r), axis=0, keepdims=True)
        gx_ref[...] = do_ref[...] + _norm_rows_bwd(xv, r, g, dh)

    row = lambda w: pl.BlockSpec((TM, w), lambda i: (i, 0))
    full = lambda a: pl.BlockSpec(a.shape, lambda i: (0,) * a.ndim)
    sds = jax.ShapeDtypeStruct
    return pl.pallas_call(
        body, name="front_bwd", grid=(s // TM,),
        in_specs=[row(D), row(D), row(2048), row(512), pl.BlockSpec((3, TM, 512), lambda i: (0, i, 0)), row(512), row(Q_LORA),
                  row(KV_LORA), row(LANE), full(g_pre), full(w_in_t), full(w_kpe), pl.BlockSpec(memory_space=pl.ANY)],
        out_specs=[row(D), pl.BlockSpec((1, D), lambda i: (0, 0))],
        out_shape=[sds((s, D), F32), sds((1, D), F32)],
        compiler_params=_params(("arbitrary",)),
    )(x, dout, dmg, dga, dh3, dgb, dcq, dckv, dkpe, g_pre, w_in_t, w_kpe, token)


TK_GRAD = 1024


def _win_grad(h, pieces, name):
    s = h.shape[0]
    n = len(pieces)

    def body(h_ref, *refs):
        @pl.when(pl.program_id(0) == 0)
        def _():
            for o_ref in refs[n:]:
                o_ref[...] = jnp.zeros_like(o_ref)

        hv = h_ref[...]
        for d_ref, o_ref in zip(refs[:n], refs[n:]):
            if len(d_ref.shape) == 3:
                for k in range(d_ref.shape[0]):
                    o_ref[k] += _mm_tn(d_ref[k], hv)
            else:
                o_ref[...] += _mm_tn(d_ref[...], hv)

    def in_spec(p):
        if p.ndim == 3:
            return pl.BlockSpec((p.shape[0], TK_GRAD, p.shape[2]), lambda kk: (0, kk, 0))
        return pl.BlockSpec((TK_GRAD, p.shape[1]), lambda kk: (kk, 0))

    out_shapes = [(p.shape[0], p.shape[2], D) if p.ndim == 3 else (p.shape[1], D) for p in pieces]
    return pl.pallas_call(
        body, name=name, grid=(s // TK_GRAD,),
        in_specs=[pl.BlockSpec((TK_GRAD, D), lambda kk: (kk, 0))] + [in_spec(p) for p in pieces],
        out_specs=[pl.BlockSpec(sh, lambda kk, nd=len(sh): (0,) * nd) for sh in out_shapes],
        out_shape=[jax.ShapeDtypeStruct(sh, F32) for sh in out_shapes],
        compiler_params=_params(("arbitrary",)),
    )(h, *pieces)


def _pad_wuq(w_uq):
    rows = w_uq.shape[0]
    w = w_uq.reshape(rows, NH, QK_NOPE + QK_ROPE)
    return jnp.pad(w, ((0, 0), (0, 0), (0, LANE - QK_NOPE - QK_ROPE))).reshape(rows, NH * LANE)


def _pad_wukv(w_ukv):
    heads = w_ukv.shape[1] // (QK_NOPE + V_DIM)
    w = w_ukv.reshape(KV_LORA, heads, QK_NOPE + V_DIM)
    w_k = jnp.pad(w[:, :, :QK_NOPE], ((0, 0), (0, 0), (0, LANE - QK_NOPE))).reshape(KV_LORA, heads * LANE)
    wv = w[:, :, QK_NOPE:].reshape(KV_LORA, heads // 2, 2, 1, V_DIM)
    eye = jnp.eye(2, dtype=w.dtype).reshape(1, 1, 2, 2, 1)
    return w_k, (wv * eye).reshape(KV_LORA, heads * LANE)


def _local_step(x, tgt, g_pre, w_in_t, b_gate, g_q, g_kv, lb_logits, g_hgrn, g_post, weights, exchange=None):
    s = x.shape[0]
    w_kpe = _kpe_block(w_in_t)
    rc, rs1, rs2 = _rope_tables(s)
    g_hg = jnp.tile(g_hgrn, (1, NH))

    proj, h = _front_fwd(x, g_pre, w_in_t, w_kpe, weights.tokens)
    w_uq_p, w_k_p, w_v_p = weights.qkv(h)
    q, k, vv = _qkv_fwd(proj, g_q, g_kv, w_uq_p, w_k_p, w_v_p, rc, rs1, rs2)
    attn, lse = _attn_fwd(q, k, vv)
    o_raw, states = _hgrn_fwd(proj, lb_logits)
    wa, wb, w_out = weights.mid(o_raw)
    (loss, dout, dattn, dga, dor, dgb, dmg, d_wout, d_wa, d_wb, d_gpost, d_bgate, d_ghg) = _mid(
        proj, attn, o_raw, x, tgt, g_hg, b_gate, g_post, wa, wb, w_out)
    w_mg, w_ga, w_gb = _win_grad(h, [dmg, dga, dgb], "win_grad_mid")
    dh3, d_lbl = _hgrn_bwd(proj, lb_logits, states, dor)
    (w_h3,) = _win_grad(h, [dh3], "win_grad_hgrn")
    d_win_rest = jnp.concatenate([w_ga, w_h3[0], w_h3[1], w_h3[2], w_gb, w_mg], axis=0)
    early = dict(w_in_rest=d_win_rest, w_branch_a=d_wa, w_branch_b=d_wb, w_out=d_wout)
    token = exchange.start_early(early) if exchange else jnp.zeros((8, LANE), F32)
    dq, dk, dvv = _attn_bwd(q, k, vv, attn, dattn, lse, token)
    dcq, dckv, dkpe, d_wuq, d_wukv, d_gq, d_gkv = _qkv_bwd(proj, dq, dk, dvv, g_q, g_kv, w_uq_p, w_k_p, w_v_p, rc, rs1, rs2)
    w_cq, w_ckv, w_kp = _win_grad(h, [dcq, dckv, dkpe], "win_grad_qkv")
    d_win_qkv = jnp.concatenate([w_cq, w_ckv, w_kp[64:64 + QK_ROPE]], axis=0)
    late = dict(w_in_qkv=d_win_qkv, w_uq=d_wuq, w_ukv=d_wukv)
    token = exchange.start_late(late) if exchange else jnp.zeros((8, LANE), F32)
    grad_x, d_gpre = _front_bwd(x, dout, dmg, dga, dh3, dgb, dcq, dckv, dkpe, g_pre, w_in_t, w_kpe, token)
    vec_grads = dict(g_pre=d_gpre, b_gate=d_bgate, g_q=d_gq, g_kv=d_gkv, lb_logits=d_lbl, g_hgrn=d_ghg, g_post=d_gpost)
    return loss, grad_x, dict(early, **late), vec_grads


SHARD_SHAPES = (("w_in", (1416, 1024)), ("w_uq", (192, 768)), ("w_ukv", (256, 256)), ("w_branch_a", (512, 256)),
                ("w_branch_b", (512, 256)), ("w_out", (256, 1024)))
BIG = tuple(n for n, _ in SHARD_SHAPES)
ROW_SHARDED = ("w_in", "w_uq", "w_out")
N_CHIPS = 4
QKV_ROWS = Q_LORA + KV_LORA + QK_ROPE
W_IN_FORWARD_CUT = 704


def _to_block(name, a):
    return a[0].T if name == "w_in" else a[0]


def _from_block(name, a):
    return a.T[None] if name == "w_in" else a[None]
VEC_ROWS = (("g_pre", 0, 1024), ("b_gate", 1, 2048), ("g_q", 2, 768), ("g_kv", 3, 256), ("g_hgrn", 6, 64), ("g_post", 7, 1024))
VEC_LB_ROW = 4
VEC_SHAPE = (8, 2048)


def _split_by_chip(name, g):
    a, b = dict(SHARD_SHAPES)[name]
    return g.reshape(N_CHIPS, a, b) if name in ROW_SHARDED else g.reshape(a, N_CHIPS, b).transpose(1, 0, 2)


def _join_chips(name, w):
    a, b = dict(SHARD_SHAPES)[name]
    return w.reshape(N_CHIPS * a, b) if name in ROW_SHARDED else w.transpose(1, 0, 2).reshape(a, N_CHIPS * b)


MESH = pl.DeviceIdType.MESH
HBM = pl.BlockSpec(memory_space=pltpu.HBM)


def _mesh_place():
    x, y, c = lax.axis_index("x"), lax.axis_index("y"), lax.axis_index("c")
    return x, y, c, 2 * x + y, [(1 - x, y), (x, 1 - y), (1 - x, 1 - y)]


def _remote(src, dst, send_sems, recv_sems, k, to):
    return pltpu.make_async_remote_copy(src_ref=src, dst_ref=dst, send_sem=send_sems.at[k], recv_sem=recv_sems.at[k],
                                        device_id=to, device_id_type=MESH)


def _gather_w_in(shard):
    a, b = shard.shape
    cut = W_IN_FORWARD_CUT

    def body(src, out, ici_send, ici_recv, d2d_send, d2d_recv, local_sem):
        x, y, c = lax.axis_index("x"), lax.axis_index("y"), lax.axis_index("c")
        me, xn, yn, dg = 2 * x + y, 2 * (1 - x) + y, 2 * x + (1 - y), 2 * (1 - x) + (1 - y)
        to_x, to_y, sibling = (1 - x, y, c), (x, 1 - y, c), (x, y, 1 - c)
        first, rest = pl.ds(0, cut), pl.ds(cut, a - cut)

        whole = lambda ref, which: ref.at[:, pl.ds(pl.multiple_of(which * (b // 2), b // 2), b // 2)]
        own = pltpu.make_async_copy(src, out.at[me], local_sem)
        own.start()
        sends = [_remote(whole(src, c), whole(out.at[me], c), ici_send, ici_recv, 0, to_x),
                 _remote(whole(src, c), whole(out.at[me], c), ici_send, ici_recv, 1, to_y)]
        for cp in sends:
            cp.start()

        def landed(slot, rows, k, d2d_k, src_dev):
            piece = whole(out.at[slot], c) if rows is None else out.at[slot].at[rows, pl.ds(pl.multiple_of(c * (b // 2), b // 2), b // 2)]
            _remote(piece, piece, ici_send, ici_recv, k, src_dev).wait_recv()
            cp = _remote(piece, piece, d2d_send, d2d_recv, d2d_k, sibling)
            cp.start()
            sends.append(cp)
            return piece

        def pass_on(slot, rows, k, to):
            piece = out.at[slot].at[rows, pl.ds(pl.multiple_of(c * (b // 2), b // 2), b // 2)]
            cp = _remote(piece, piece, ici_send, ici_recv, k, to)
            cp.start()
            sends.append(cp)

        landed(xn, None, 0, 0, to_x)
        pass_on(xn, first, 2, to_y)
        landed(yn, None, 1, 1, to_y)
        pass_on(yn, rest, 3, to_x)
        landed(dg, first, 2, 2, to_y)
        landed(dg, rest, 3, 3, to_x)
        other = pl.ds(pl.multiple_of((1 - c) * (b // 2), b // 2), b // 2)
        for d2d_k, (slot, rows) in enumerate(((xn, None), (yn, None), (dg, first), (dg, rest))):
            piece = out.at[slot].at[:, other] if rows is None else out.at[slot].at[rows, other]
            _remote(piece, piece, d2d_send, d2d_recv, d2d_k, sibling).wait_recv()
        for cp in sends:
            cp.wait_send()
        own.wait()

    sems = pltpu.SemaphoreType.DMA((4,))
    return pl.pallas_call(
        body, name="gather_w_in", in_specs=[HBM], out_specs=HBM,
        out_shape=jax.ShapeDtypeStruct((N_CHIPS, a, b), shard.dtype),
        scratch_shapes=[sems, sems, sems, sems, pltpu.SemaphoreType.DMA],
        compiler_params=pltpu.CompilerParams(has_side_effects=True),
    )(shard)


def _sibling_exchange(srcs, name, after=None):
    n = len(srcs)
    extra = [] if after is None else [after]

    def body(*refs):
        src_refs, outs = refs[:n], refs[n + len(extra):2 * n + len(extra)]
        send_sems, recv_sems = refs[2 * n + len(extra):]
        sibling = (lax.axis_index("x"), lax.axis_index("y"), 1 - lax.axis_index("c"))
        copies = [_remote(src_refs[k], outs[k], send_sems, recv_sems, k, sibling) for k in range(n)]
        for cp in copies:
            cp.start()
        for cp in copies:
            cp.wait()

    sems = pltpu.SemaphoreType.DMA((n,))
    return pl.pallas_call(
        body, name=name, in_specs=[HBM] * n + [pl.BlockSpec(memory_space=pl.ANY)] * len(extra), out_specs=[HBM] * n,
        out_shape=[jax.ShapeDtypeStruct(s.shape, s.dtype) for s in srcs],
        scratch_shapes=[sems, sems],
        compiler_params=pltpu.CompilerParams(has_side_effects=True),
    )(*srcs, *extra)


SEM = pl.BlockSpec(memory_space=pltpu.SEMAPHORE)
DATAFLOW = pltpu.SideEffectType.DATAFLOW_SIDE_EFFECTING


def _exchange_copies(srcs, to_first, src_refs, land_refs, send_sems, recv_sems):
    x, y, c, me, chips = _mesh_place()
    n = len(srcs)
    sends, recvs = [], []
    for k in range(n):
        if k in to_first:
            base = 3 * n + 4 * to_first.index(k)
            sends.append((me != 0, pltpu.make_async_remote_copy(
                src_ref=src_refs[k], dst_ref=land_refs[k].at[me], send_sem=send_sems.at[base], recv_sem=recv_sems.at[base + me],
                device_id=(0, 0, c), device_id_type=MESH)))
            for s in range(1, N_CHIPS):
                recvs.append((me == 0, pltpu.make_async_remote_copy(
                    src_ref=src_refs[k], dst_ref=land_refs[k].at[s], send_sem=send_sems.at[base], recv_sem=recv_sems.at[base + s],
                    device_id=(s // 2, s % 2, c), device_id_type=MESH)))
        else:
            slab = (lambda t, k=k: src_refs[k]) if srcs[k].ndim == 2 else (lambda t, k=k: src_refs[k].at[t])
            for j, (px, py) in enumerate(chips):
                sends.append((None, _remote(slab(2 * px + py), land_refs[k].at[me], send_sems, recv_sems, 3 * k + j, (px, py, c))))
                recvs.append((None, _remote(slab(me), land_refs[k].at[2 * px + py], send_sems, recv_sems, 3 * k + j, (px, py, c))))
    return sends, recvs


def _when(pred, fn):
    if pred is None:
        fn()
    else:
        pl.when(pred)(fn)


def _exchange_start(srcs, to_first, name, after=None):
    n = len(srcs)
    n_sems = 3 * n + 4 * len(to_first)
    lands = [lax.empty((N_CHIPS,) + s.shape[-2:], s.dtype) for s in srcs]
    extra = [] if after is None else [after]

    def body(*refs):
        src_refs, land_refs = refs[:n], refs[n:2 * n]
        send_sems, recv_sems, token = refs[2 * n + len(extra)], refs[2 * n + len(extra) + 1], refs[-1]
        sends, _ = _exchange_copies(srcs, to_first, src_refs, land_refs, send_sems, recv_sems)
        for pred, cp in sends:
            _when(pred, cp.start)
        token[...] = jnp.zeros_like(token)

    hbm = lambda a: pltpu.HBM(a.shape, a.dtype)
    res = pl.pallas_call(
        body, name=name,
        out_shape=[pltpu.SemaphoreType.DMA((n_sems,)), pltpu.SemaphoreType.DMA((n_sems,))] + [hbm(a) for a in srcs + lands]
        + [jax.ShapeDtypeStruct((8, LANE), F32)],
        in_specs=[HBM] * (2 * n) + [pl.BlockSpec(memory_space=pl.ANY)] * len(extra),
        out_specs=[SEM, SEM] + [HBM] * (2 * n) + [pl.BlockSpec(memory_space=pltpu.VMEM)],
        input_output_aliases={i: 2 + i for i in range(2 * n)},
        compiler_params=pltpu.CompilerParams(has_side_effects=DATAFLOW),
    )(*[pltpu.with_memory_space_constraint(a, pltpu.HBM) for a in srcs + lands], *extra)
    return res[:-1], res[-1]


def _exchange_wait(srcs, to_first, started, after, name):
    n = len(srcs)
    send_sems, recv_sems, thru = started[0], started[1], started[2:]

    def body(*refs):
        src_refs, land_refs, send_ref, recv_ref = refs[:n], refs[n:2 * n], refs[2 * n], refs[2 * n + 1]
        sends, recvs = _exchange_copies(srcs, to_first, src_refs, land_refs, send_ref, recv_ref)
        for pred, cp in sends:
            _when(pred, cp.wait_send)
        for pred, cp in recvs:
            _when(pred, cp.wait_recv)

    res = pl.pallas_call(
        body, name=name, out_shape=[pltpu.HBM(a.shape, a.dtype) for a in thru],
        in_specs=[HBM] * (2 * n) + [SEM, SEM, pl.BlockSpec(memory_space=pl.ANY)], out_specs=[HBM] * (2 * n),
        input_output_aliases={i: i for i in range(2 * n)},
        compiler_params=pltpu.CompilerParams(has_side_effects=DATAFLOW),
    )(*thru, send_sems, recv_sems, after)
    return res[:n], res[n:]


ROW_TILE = 256
COL_TILE = 256


def _block_tiling(a, b):
    if a <= ROW_TILE or a % ROW_TILE == 0:
        ta = min(a, ROW_TILE)
        return a // ta, (ta, b), lambda i: (i, 0)
    return b // COL_TILE, (a, COL_TILE), lambda i: (0, i)


def _sum_landed(land, own, name, first_land=None, first_own=None):
    _, a, b = land.shape
    steps, tile, at = _block_tiling(a, b)
    extra = first_land is not None

    def body(me_ref, *refs):
        p_ref, own_ref, o_ref = refs[0], refs[1], refs[-1]
        me = me_ref[0]
        own = own_ref[...].astype(F32)
        slot = lambda t: jnp.where(me == t, own, p_ref[t].astype(F32))
        o_ref[...] = ((slot(0) + slot(1)) + slot(2)) + slot(3)
        if extra:
            fp_ref, fo_ref = refs[2], refs[3]
            r = fo_ref.shape[0]

            @pl.when(me == 0)
            def _():
                f = lambda t: fp_ref[t].astype(F32)
                rows = pl.ds(pl.multiple_of(lax.axis_index("c") * r, 8), r)
                o_ref[rows, :] += ((fo_ref[...].astype(F32) + f(1)) + f(2)) + f(3)

    own_spec = pl.BlockSpec(tile, lambda i, me: at(i)) if own.ndim == 2 else pl.BlockSpec((None,) + tile, lambda i, me: (me[0],) + at(i))
    in_specs = [pl.BlockSpec((N_CHIPS,) + tile, lambda i, me: (0,) + at(i)), own_spec]
    args = [land, own]
    if extra:
        r = first_own.shape[0]
        assert tile[0] == a, "the extra rows need whole columns in a step"
        in_specs += [pl.BlockSpec((N_CHIPS, r, tile[1]), lambda i, me: (0,) + at(i)), pl.BlockSpec((r, tile[1]), lambda i, me: at(i))]
        args += [first_land, first_own]
    me = jnp.reshape(2 * lax.axis_index("x") + lax.axis_index("y"), (1,)).astype(jnp.int32)
    return pl.pallas_call(
        body, name=name,
        grid_spec=pltpu.PrefetchScalarGridSpec(num_scalar_prefetch=1, grid=(steps,), in_specs=in_specs,
                                               out_specs=pl.BlockSpec(tile, lambda i, me: at(i))),
        out_shape=jax.ShapeDtypeStruct((a, b), F32), compiler_params=_params(("parallel",)),
    )(me, *args)


def _sum_landed_small(lands, owns, name):
    n = len(lands)

    def body(*refs):
        me = 2 * lax.axis_index("x") + lax.axis_index("y")
        for p_ref, own_ref, o_ref in zip(refs[:n], refs[n:2 * n], refs[2 * n:]):
            own = own_ref[me].astype(F32)
            slot = lambda t: jnp.where(me == t, own, p_ref[t].astype(F32))
            o_ref[...] = ((slot(0) + slot(1)) + slot(2)) + slot(3)

    return pl.pallas_call(body, name=name, out_shape=[jax.ShapeDtypeStruct(o.shape[1:], F32) for o in owns],
                          compiler_params=_params(()))(*lands, *owns)


def _adamw_small(mine, theirs, states, name):
    n = len(mine)

    def body(*refs):
        ins, outs = refs[:5 * n], refs[5 * n:]
        for k in range(n):
            a_ref, b_ref, w_ref, m_ref, v_ref = ins[5 * k:5 * k + 5]
            g = a_ref[...] + b_ref[...]
            outs[4 * k][...] = g
            outs[4 * k + 1][...], outs[4 * k + 2][...], outs[4 * k + 3][...] = _adamw_math(g, w_ref[...], m_ref[...], v_ref[...])

    args = [t for k in range(n) for t in (mine[k], theirs[k], *states[k])]
    res = pl.pallas_call(body, name=name, out_shape=[jax.ShapeDtypeStruct(mine[k].shape, F32) for k in range(n) for _ in range(4)],
                         compiler_params=_params(()))(*args)
    return [tuple(res[4 * k:4 * k + 4]) for k in range(n)]


def _add_cast(a, b, name):
    def body(a_ref, b_ref, o_ref):
        o_ref[...] = (a_ref[...] + b_ref[...]).astype(BF16)

    return pl.pallas_call(body, name=name, out_shape=jax.ShapeDtypeStruct(a.shape, BF16),
                          compiler_params=_params(()))(a, b)


class _LaterWeights:
    MID = ("w_branch_a", "w_branch_b", "w_out")

    def __init__(self, blocks, after):
        self.qkv_blocks = [_pad_wuq(blocks["w_uq"]), *_pad_wukv(blocks["w_ukv"])]
        self.mid_blocks = [blocks[n] for n in self.MID]
        self.qkv_started, t1 = _exchange_start(self.qkv_blocks, (), "weights_qkv_start", after)
        self.mid_started, t2 = _exchange_start(self.mid_blocks, (), "weights_mid_start", after)
        self.tokens = [t1, t2]

    @staticmethod
    def _whole(blocks, rows_sharded, started, after, name):
        sent, landed = _exchange_wait(blocks, (), started, after, name)
        me = 2 * lax.axis_index("x") + lax.axis_index("y")
        out = []
        for block, land, by_rows in zip(sent, landed, rows_sharded):
            w = lax.dynamic_update_index_in_dim(land, block, me, 0)
            a, b = block.shape
            out.append(w.reshape(N_CHIPS * a, b) if by_rows else w.transpose(1, 0, 2).reshape(a, N_CHIPS * b))
        return out

    def qkv(self, after):
        return self._whole(self.qkv_blocks, (True, False, False), self.qkv_started, after, "weights_qkv_wait")

    def mid(self, after):
        return self._whole(self.mid_blocks, (False, False, True), self.mid_started, after, "weights_mid_wait")


class _GradExchange:
    EARLY = ("w_in", "w_branch_a", "w_branch_b", "w_out")
    LATE = ("w_uq", "w_ukv")

    def __init__(self, state):
        self.state = state
        self.outs = {}

    def start_early(self, g):
        full = jnp.concatenate([jnp.zeros((QKV_ROWS, D), F32), g["w_in_rest"]], axis=0)
        g = dict(g, w_in=full)
        self.early = [_split_by_chip(n, g[n]).astype(BF16) for n in self.EARLY]
        self.early_started, token = _exchange_start(self.early, (), "grads_early_start")
        return token

    def start_late(self, g):
        self.early, self.early_landed = _exchange_wait(self.early, (), self.early_started, g["w_uq"], "grads_early_wait")
        half = QKV_ROWS // 2
        c = lax.axis_index("c")
        mine = lax.dynamic_slice_in_dim(g["w_in_qkv"], c * half, half, axis=0)
        (theirs,) = _sibling_exchange([lax.dynamic_slice_in_dim(g["w_in_qkv"], (1 - c) * half, half, axis=0)], "sibling_qkv_rows")
        self.late = [_split_by_chip("w_uq", g["w_uq"]), g["w_ukv"], _add_cast(mine, theirs, "add_qkv_rows")]
        self.late_started, token = _exchange_start(self.late, (2,), "grads_late_start")
        names = self.EARLY[1:]
        mine = _sum_landed_small(self.early_landed[1:], self.early[1:], "sum_early")
        theirs = _sibling_exchange(mine, "sibling_early", after=token)
        for n, out in zip(names, _adamw_small(mine, theirs, [self.state[n] for n in names], "adamw_early")):
            self.outs[n] = out
        return self.outs[names[-1]][0]

    def finish(self, after):
        late, late_landed = _exchange_wait(self.late, (2,), self.late_started, after, "grads_late_wait")
        sums = {"w_in": _sum_landed(self.early_landed[0], self.early[0], "sum_w_in", first_land=late_landed[2], first_own=late[2])}
        small = _sum_landed_small(late_landed[:2], late[:2], "sum_late")
        sums.update(zip(self.LATE, small))
        return sums


def _adamw_math(g, w, m, v):
    nm = ADAM_B1 * m + (1.0 - ADAM_B1) * g
    nv = ADAM_B2 * v + (1.0 - ADAM_B2) * (g * g)
    m_hat = nm / (1.0 - ADAM_B1 ** ADAM_STEP)
    v_hat = nv / (1.0 - ADAM_B2 ** ADAM_STEP)
    return -ADAM_LR * (m_hat / (jnp.sqrt(v_hat) + ADAM_EPS) + ADAM_WD * w), nm, nv


def _adamw(p_mine, p_sibling, w, m, v, name):
    a, b = p_mine.shape
    steps, tile, at = _block_tiling(a, b)

    def body(a_ref, b_ref, w_ref, m_ref, v_ref, g_ref, d_ref, nm_ref, nv_ref):
        g = a_ref[...] + b_ref[...]
        g_ref[...] = g
        d_ref[...], nm_ref[...], nv_ref[...] = _adamw_math(g, w_ref[...], m_ref[...], v_ref[...])

    spec = pl.BlockSpec(tile, at)
    sds = jax.ShapeDtypeStruct((a, b), F32)
    return pl.pallas_call(
        body, name=name, grid=(steps,), in_specs=[spec] * 5, out_specs=[spec] * 4, out_shape=[sds] * 4,
        compiler_params=_params(("parallel",)),
    )(p_mine, p_sibling, w, m, v)


LOSS_AT = (2, 1024)


def _vec_pack(vg, loss):
    names = [n for n, _, _ in VEC_ROWS]

    def body(*refs):
        o_ref = refs[-1]
        lb_ref, loss_ref = refs[len(names)], refs[len(names) + 1]
        o_ref[...] = jnp.zeros_like(o_ref)
        o_ref[LOSS_AT[0]:LOSS_AT[0] + 1, LOSS_AT[1]:LOSS_AT[1] + LANE] = jnp.broadcast_to(loss_ref[...], (1, LANE))
        for (name, row, size), ref in zip(VEC_ROWS, refs):
            if name == "g_hgrn":
                r = lax.broadcasted_iota(jnp.int32, (NH * V_DIM, LANE), 0)
                c = lax.broadcasted_iota(jnp.int32, (NH * V_DIM, LANE), 1)
                fold = ((r % V_DIM) == c).astype(F32)
                o_ref[row:row + 1, 0:LANE] = jnp.dot(ref[...], fold, precision=HIGHEST, preferred_element_type=F32)
            else:
                o_ref[row:row + 1, 0:size] = ref[...]
        o_ref[VEC_LB_ROW:VEC_LB_ROW + 2, 0:512] = lb_ref[...]

    return pl.pallas_call(body, name="vec_pack", out_shape=jax.ShapeDtypeStruct(VEC_SHAPE, F32))(
        *[vg[n] for n in names], vg["lb_logits"], loss)


def _adamw_vec(p_mine, p_sibling, w, m, v):
    names = [n for n, _, _ in VEC_ROWS] + ["lb_logits"]
    k = len(names)

    def body(a_ref, b_ref, *refs):
        ins, outs = refs[:3 * k], refs[3 * k:]
        at = (slice(LOSS_AT[0], LOSS_AT[0] + 1), slice(LOSS_AT[1], LOSS_AT[1] + LANE))
        outs[-1][...] = a_ref[at] + b_ref[at]
        for i, name in enumerate(names):
            if name == "lb_logits":
                rows, cols = slice(VEC_LB_ROW, VEC_LB_ROW + 2), slice(0, 512)
            else:
                _, row, size = VEC_ROWS[i]
                rows, cols = slice(row, row + 1), slice(0, size)
            g = a_ref[rows, cols] + b_ref[rows, cols]
            d, nm, nv = _adamw_math(g, ins[i][...], ins[k + i][...], ins[2 * k + i][...])
            for o_ref, val in zip(outs[4 * i:4 * i + 4], (g, d, nm, nv)):
                o_ref[...] = val

    shapes = [jax.ShapeDtypeStruct(w[n].shape, F32) for n in names for _ in range(4)] + [jax.ShapeDtypeStruct((1, LANE), F32)]
    res = pl.pallas_call(body, name="adamw_vec", out_shape=shapes)(
        p_mine, p_sibling, *[w[n] for n in names], *[m[n] for n in names], *[v[n] for n in names])
    return [{n: res[4 * i + j] for i, n in enumerate(names)} for j in range(4)], res[-1]


WEIGHTS = ("g_pre", "w_in", "b_gate", "g_q", "w_uq", "g_kv", "w_ukv", "lb_logits", "g_hgrn", "w_branch_a", "w_branch_b", "w_out", "g_post")


def kernel(x, g_pre, w_in, b_gate, g_q, w_uq, g_kv, w_ukv, lb_logits, g_hgrn, w_branch_a, w_branch_b, w_out, g_post, loss_target, m_g_pre, m_w_in, m_b_gate, m_g_q, m_w_uq, m_g_kv, m_w_ukv, m_lb_logits, m_g_hgrn, m_w_branch_a, m_w_branch_b, m_w_out, m_g_post, v_g_pre, v_w_in, v_b_gate, v_g_q, v_w_uq, v_g_kv, v_w_ukv, v_lb_logits, v_g_hgrn, v_w_branch_a, v_w_branch_b, v_w_out, v_g_post):
    w = dict(g_pre=g_pre, w_in=w_in, b_gate=b_gate, g_q=g_q, w_uq=w_uq, g_kv=g_kv, w_ukv=w_ukv, lb_logits=lb_logits, g_hgrn=g_hgrn,
             w_branch_a=w_branch_a, w_branch_b=w_branch_b, w_out=w_out, g_post=g_post)
    m = dict(g_pre=m_g_pre, w_in=m_w_in, b_gate=m_b_gate, g_q=m_g_q, w_uq=m_w_uq, g_kv=m_g_kv, w_ukv=m_w_ukv, lb_logits=m_lb_logits,
             g_hgrn=m_g_hgrn, w_branch_a=m_w_branch_a, w_branch_b=m_w_branch_b, w_out=m_w_out, g_post=m_g_post)
    v = dict(g_pre=v_g_pre, w_in=v_w_in, b_gate=v_b_gate, g_q=v_g_q, w_uq=v_w_uq, g_kv=v_g_kv, w_ukv=v_w_ukv, lb_logits=v_lb_logits,
             g_hgrn=v_g_hgrn, w_branch_a=v_w_branch_a, w_branch_b=v_w_branch_b, w_out=v_w_out, g_post=v_g_post)
    blocks = {n: _to_block(n, w[n]).astype(BF16) for n in BIG}
    w_in_all = _gather_w_in(blocks["w_in"])
    weights = _LaterWeights(blocks, w_in_all)
    state = {n: [_to_block(n, t[n]) for t in (w, m, v)] for n in BIG}
    exchange = _GradExchange(state)
    loss, grad_x, _, vec_grads = _local_step(
        x[0], loss_target[0], g_pre, _join_chips("w_in", w_in_all), b_gate, g_q, g_kv, lb_logits, g_hgrn, g_post, weights, exchange)
    vec = _vec_pack(vec_grads, loss)
    vec_started, token = _exchange_start([vec], (), "vec_start")
    sums = exchange.finish(token)
    rest = tuple(sums)
    (vec,), (vec_landed,) = _exchange_wait([vec], (), vec_started, sums[rest[-1]], "vec_wait")
    mine = [sums[n] for n in rest] + [_sum_landed(vec_landed, vec, "sum_vec")]
    theirs = _sibling_exchange(mine, "sibling_grads")
    done = dict(exchange.outs)
    done["w_in"] = _adamw(mine[0], theirs[0], *state["w_in"], "adamw_w_in")
    small = _adamw_small(mine[1:-1], theirs[1:-1], [state[n] for n in rest[1:]], "adamw_late")
    done.update(zip(rest[1:], small))
    outs = [{}, {}, {}, {}]
    for n in BIG:
        for o, val in zip(outs, done[n]):
            o[n] = _from_block(n, val)
    vec_outs, total = _adamw_vec(mine[-1], theirs[-1], w, m, v)
    for o, vals in zip(outs, vec_outs):
        o.update(vals)
    return (total[0, 0], grad_x[None], *[o[n] for o in outs for n in WEIGHTS])
```

```python
import math

import numpy as np
import jax
import jax.numpy as jnp
from jax import lax
from jax.experimental import pallas as pl
from jax.experimental.pallas import tpu as pltpu

F32 = jnp.float32
BF16 = jnp.bfloat16
HIGHEST = lax.Precision.HIGHEST

D = 1024
NH = 8
QK_NOPE, QK_ROPE, V_DIM = 64, 32, 64
Q_LORA, KV_LORA = 768, 256
CHUNK = 64
HG_BLOCK = 32
EPS = 1e-6
LANE = 128
P_MERGE, P_GA, P_HQ, P_HF, P_HI, P_GB, P_CQ, P_CKV, P_KPE = 0, 2048, 2560, 3072, 3584, 4096, 4608, 5376, 5632
D_P = 5760
O_CQ, O_CKV, O_KPE, O_GA, O_HQ, O_HF, O_HI, O_GB, O_MERGE = 0, 768, 1024, 1056, 1568, 2080, 2592, 3104, 3616

TM = 512
TM_MID = 256
TQ = 1024
ONES_LANE = (LANE - 1, 0)
TH = 256
HG_PAIRS = 4
VMEM_LIMIT = 56 * 1024 * 1024

ADAM_LR, ADAM_B1, ADAM_B2, ADAM_EPS, ADAM_WD, ADAM_STEP = 0.001, 0.9, 0.999, 1e-08, 0.01, 10

NT_DIMS = (((1,), (1,)), ((), ()))
TN_DIMS = (((0,), (0,)), ((), ()))


def _params(sem):
    return pltpu.CompilerParams(dimension_semantics=sem, vmem_limit_bytes=VMEM_LIMIT)


def _mm(a, b):
    return jnp.dot(a, b, preferred_element_type=F32)


def _mm_nt(a, b):
    return lax.dot_general(a, b, NT_DIMS, preferred_element_type=F32)


def _mm_tn(a, b):
    return lax.dot_general(a, b, TN_DIMS, preferred_element_type=F32)


def _sigmoid(z):
    return jax.nn.sigmoid(z)


def _rope(v, c, s1, s2):
    return v * c + pltpu.roll(v, 112, 1) * s1 + pltpu.roll(v, 16, 1) * s2


def _rope_t(dy, c, s1, s2):
    return dy * c + pltpu.roll(dy * s1, 16, 1) + pltpu.roll(dy * s2, 112, 1)


def _rope_tables(s):
    f32 = np.float32
    inv = f32(10000.0) ** (-np.arange(0, QK_ROPE, 2, dtype=f32) / f32(QK_ROPE))
    ang = np.arange(s, dtype=f32)[:, None] * inv[None, :]
    cos, sin = np.cos(ang).astype(f32), np.sin(ang).astype(f32)
    z64, z32, o64, o32 = np.zeros((s, 64), f32), np.zeros((s, 32), f32), np.ones((s, 64), f32), np.ones((s, 32), f32)
    z16 = np.zeros((s, 16), f32)
    c = np.concatenate([o64, cos, cos, o32], axis=1)
    s1 = np.concatenate([z64, -sin, z16, z32], axis=1)
    s2 = np.concatenate([z64, z16, sin, z32], axis=1)
    return jnp.asarray(c), jnp.asarray(s1), jnp.asarray(s2)


W_IN_RUNS = ((O_MERGE, 2048, P_MERGE), (O_GA, O_MERGE - O_GA, P_GA), (O_CQ, O_KPE - O_CQ, P_CQ))


def _kpe_block(w_in_t):
    z = lambda n: jnp.zeros((n, w_in_t.shape[1]), w_in_t.dtype)
    return jnp.concatenate([z(64), w_in_t[O_KPE:O_KPE + QK_ROPE], z(32)], axis=0)


def _front_fwd(x, g_pre, w_in_t, w_kpe, tokens=()):
    s = x.shape[0]
    tokens = list(tokens)

    def body(x_ref, g_ref, w_ref, k_ref, *refs):
        o_ref, h_ref = refs[len(tokens):]
        xv = x_ref[...]
        r = lax.rsqrt(jnp.mean(xv * xv, axis=-1, keepdims=True) + EPS)
        h = ((xv * r) * g_ref[...]).astype(BF16)
        h_ref[...] = h
        for row, rows, col in W_IN_RUNS:
            o_ref[:, col:col + rows] = _mm_nt(h, w_ref[row:row + rows, :])
        o_ref[:, P_KPE:P_KPE + LANE] = _mm_nt(h, k_ref[...])

    full = lambda a: pl.BlockSpec(a.shape, lambda i: (0,) * a.ndim)
    return pl.pallas_call(
        body, name="front_fwd", grid=(s // TM,),
        in_specs=[pl.BlockSpec((TM, D), lambda i: (i, 0)), pl.BlockSpec((1, D), lambda i: (0, 0)), full(w_in_t), full(w_kpe)]
        + [pl.BlockSpec((8, LANE), lambda i: (0, 0))] * len(tokens),
        out_specs=[pl.BlockSpec((TM, D_P), lambda i: (i, 0)), pl.BlockSpec((TM, D), lambda i: (i, 0))],
        out_shape=[jax.ShapeDtypeStruct((s, D_P), F32), jax.ShapeDtypeStruct((s, D), BF16)],
        compiler_params=_params(("parallel",)),
    )(x, g_pre, w_in_t, w_kpe, *tokens)


def _norm_rows(v, g):
    r = lax.rsqrt(jnp.mean(v * v, axis=-1, keepdims=True) + EPS)
    return (v * r) * g, r


def _qkv_fwd(proj, g_q, g_kv, w_uq_p, w_k_p, w_v_p, rc, rs1, rs2):
    s = proj.shape[0]

    def body(cq_ref, ckv_ref, kpe_ref, gq_ref, gkv_ref, wq_ref, wk_ref, wv_ref, c_ref, s1_ref, s2_ref, q_ref, k_ref, v_ref):
        c, s1, s2 = c_ref[...], s1_ref[...], s2_ref[...]
        cqn, _ = _norm_rows(cq_ref[...], gq_ref[...])
        ckvn, _ = _norm_rows(ckv_ref[...], gkv_ref[...])
        ckvn = ckvn.astype(BF16)
        qf = _mm(cqn.astype(BF16), wq_ref[...])
        kf = jnp.concatenate([_mm(ckvn, wk_ref[t]) for t in range(N_CHIPS)], axis=1)
        vf = jnp.concatenate([_mm(ckvn, wv_ref[t]) for t in range(N_CHIPS)], axis=1)
        kpe = _rope(kpe_ref[...], c, s1, s2)
        lane = lax.broadcasted_iota(jnp.int32, (TM, LANE), 1)
        for h in range(NH):
            blk = slice(h * LANE, (h + 1) * LANE)
            q_ref[h] = _rope(qf[:, blk], c, s1, s2).astype(BF16)
            k_ref[h] = (kf[:, blk] + kpe).astype(BF16)
            v_ref[h] = jnp.where(lane == ONES_LANE[h % 2], 1.0, vf[:, blk]).astype(BF16)

    row = lambda w, j: pl.BlockSpec((TM, w), lambda i: (i, j))
    full = lambda a: pl.BlockSpec(a.shape, lambda i: (0,) * a.ndim)
    hs = jax.ShapeDtypeStruct((NH, s, LANE), BF16)
    return pl.pallas_call(
        body, name="qkv_fwd", grid=(s // TM,),
        in_specs=[row(Q_LORA, P_CQ // Q_LORA), row(KV_LORA, P_CKV // KV_LORA), row(LANE, P_KPE // LANE),
                  full(g_q), full(g_kv), full(w_uq_p), full(w_k_p), full(w_v_p), row(LANE, 0), row(LANE, 0), row(LANE, 0)],
        out_specs=[pl.BlockSpec((NH, TM, LANE), lambda i: (0, i, 0))] * 3,
        out_shape=[hs, hs, hs],
        compiler_params=_params(("parallel",)),
    )(proj, proj, proj, g_q, g_kv, w_uq_p, w_k_p, w_v_p, rc, rs1, rs2)


LOG2E = 1.4426950408889634
QK_SCALE2 = LOG2E / math.sqrt(QK_NOPE + QK_ROPE)


HQ = TQ // 2


def _diag_visible(n):
    row = lax.broadcasted_iota(jnp.int32, (n, n), 0)
    col = lax.broadcasted_iota(jnp.int32, (n, n), 1)
    return (col // CHUNK) <= (row // CHUNK)


def _attn_fwd(q, k, vv):
    s = q.shape[1]

    def body(q_ref, k_ref, v_ref, o_ref, lse_ref):
        i = pl.program_id(1)
        qs = (q_ref[0], q_ref[1])

        def tiles(t, carry, diag):
            rows = pl.ds(pl.multiple_of(t * TQ, TQ), TQ)
            sc = [_mm_nt(qs[hh], k_ref[hh, rows, :]) for hh in range(2)]
            if diag:
                sc = [jnp.where(_diag_visible(TQ), s_, -jnp.inf) for s_ in sc]
            m_new = [jnp.maximum(carry[hh][0], jnp.max(sc[hh], axis=-1, keepdims=True)) for hh in range(2)]
            alpha = [jnp.exp2((carry[hh][0] - m_new[hh]) * QK_SCALE2) for hh in range(2)]
            p = [jnp.exp2((sc[hh] - m_new[hh]) * QK_SCALE2).astype(BF16) for hh in range(2)]
            acc = [alpha[hh] * carry[hh][1] + _mm(p[hh], v_ref[hh, rows, :]) for hh in range(2)]
            return (m_new[0], acc[0]), (m_new[1], acc[1])

        init = (jnp.full((TQ, 1), -jnp.inf, F32), jnp.zeros((TQ, LANE), F32))
        carry = lax.fori_loop(0, i, lambda t, c: tiles(t, c, False), (init, init))
        carry = tiles(i, carry, True)
        lane = lax.broadcasted_iota(jnp.int32, (TQ, LANE), 1)
        out = jnp.zeros((TQ, LANE), F32)
        for hh in range(2):
            m, acc = carry[hh]
            l = jnp.sum(jnp.where(lane == ONES_LANE[hh], acc, 0.0), axis=-1, keepdims=True)
            out = out + jnp.where((lane < V_DIM) == (hh == 0), acc, 0.0) / l
            lse_ref[hh] = jnp.broadcast_to(m * QK_SCALE2 + jnp.log(l) * LOG2E, (TQ, LANE))
        o_ref[...] = out

    return pl.pallas_call(
        body, name="attn_fwd", grid=(NH // 2, s // TQ),
        in_specs=[pl.BlockSpec((2, TQ, LANE), lambda p, i: (p, i, 0)), pl.BlockSpec((2, s, LANE), lambda p, i: (p, 0, 0)),
                  pl.BlockSpec((2, s, LANE), lambda p, i: (p, 0, 0))],
        out_specs=[pl.BlockSpec((TQ, LANE), lambda p, i: (i, p)), pl.BlockSpec((2, TQ, LANE), lambda p, i: (p, i, 0))],
        out_shape=[jax.ShapeDtypeStruct((s, NH * V_DIM), F32), jax.ShapeDtypeStruct((NH, s, LANE), F32)],
        compiler_params=_params(("parallel", "parallel")),
    )(q, k, vv)


def _lower_bound(lbl):
    a0, a1 = lbl[0:1, :], lbl[1:2, :]
    mx = jnp.maximum(a0, a1)
    e0, e1 = jnp.exp(a0 - mx), jnp.exp(a1 - mx)
    return e0 / (e0 + e1)


def _chunk_cumsum(v, reverse=False):
    pos = lax.broadcasted_iota(jnp.int32, v.shape, 0) % HG_BLOCK
    s = 1
    while s < HG_BLOCK:
        if reverse:
            v = v + jnp.where(pos < HG_BLOCK - s, pltpu.roll(v, TH - s, 0), 0.0)
        else:
            v = v + jnp.where(pos >= s, pltpu.roll(v, s, 0), 0.0)
        s *= 2
    return v


def _hgrn_gates(hq, hf, lb):
    sig = _sigmoid(hf)
    f = lb + (1.0 - lb) * sig
    g = jnp.log(f)
    kk = 1.0 - f
    r = lax.broadcasted_iota(jnp.int32, (TH, TH), 0)
    c = lax.broadcasted_iota(jnp.int32, (TH, TH), 1)
    tri = ((r // HG_BLOCK) == (c // HG_BLOCK)) & (r >= c)
    cum = _chunk_cumsum(g)
    nch = TH // HG_BLOCK
    total = _chunks(cum)[:, HG_BLOCK - 1:HG_BLOCK, :]
    lastb = jnp.broadcast_to(total, (nch, HG_BLOCK, hf.shape[-1])).reshape(hf.shape)
    e, ei, ee = jnp.exp(cum), jnp.exp(-cum), jnp.exp(lastb - cum)
    return dict(sig=sig, f=f, kk=kk, tri=tri, cum=cum, total=total, decay=jnp.exp(total), e=e, ei=ei, ee=ee,
                qd=hq * e, ki=kk * ei, ke=kk * ee)


def _chunks(v):
    return v.reshape(TH // HG_BLOCK, HG_BLOCK, v.shape[-1])


def _bmm_nt(a, b):
    return lax.dot_general(a, b, (((2,), (2,)), ((0,), (0,))), preferred_element_type=F32)


def _bmm_nn(a, b):
    return lax.dot_general(a, b, (((2,), (1,)), ((0,), (0,))), preferred_element_type=F32)


def _bmm_tn(a, b):
    return lax.dot_general(a, b, (((1,), (1,)), ((0,), (0,))), preferred_element_type=F32)


def _pair_masks():
    lane = lax.broadcasted_iota(jnp.int32, (TH, LANE), 1)
    kr = lax.broadcasted_iota(jnp.int32, (LANE, LANE), 0)
    kc = lax.broadcasted_iota(jnp.int32, (LANE, LANE), 1)
    return lane < 64, (kr // 64) == (kc // 64)


def _hgrn_fwd(proj, lbl):
    s = proj.shape[0]
    nch = TH // HG_BLOCK

    def body(hq_ref, hf_ref, hi_ref, lbl_ref, o_ref, st_ref, st):
        @pl.when(pl.program_id(1) == 0)
        def _():
            st[...] = jnp.zeros_like(st)

        m0, bd = _pair_masks()
        gt = _hgrn_gates(hq_ref[...], hf_ref[...], _lower_bound(lbl_ref[...]))
        v_b, qd, qd_b = hi_ref[...].astype(BF16), gt["qd"], gt["qd"].astype(BF16)
        ki_b, ke_b = gt["ki"].astype(BF16), gt["ke"].astype(BF16)
        pairs = [slice(u * LANE, (u + 1) * LANE) for u in range(HG_PAIRS)]
        heads = [(lanes, m0 if hh == 0 else jnp.logical_not(m0)) for lanes in pairs for hh in range(2)]
        a_b = [jnp.where(gt["tri"], _mm_nt(jnp.where(mh, qd[:, lanes], 0.0).astype(BF16), ki_b[:, lanes]), 0.0).astype(BF16)
               for lanes, mh in heads]
        intra = [jnp.where(m0, _mm(a_b[2 * u], v_b[:, lanes]), _mm(a_b[2 * u + 1], v_b[:, lanes])) for u, lanes in enumerate(pairs)]
        upd = [_bmm_tn(_chunks(v_b[:, lanes]), _chunks(ke_b[:, lanes])) for lanes in pairs]
        entering = []
        for u, lanes in enumerate(pairs):
            cur, states = st[u], []
            for n in range(nch):
                states.append(cur)
                cur = gt["decay"][n][:, lanes] * cur + jnp.where(bd, upd[u][n], 0.0)
            st[u] = cur
            entering.append(jnp.stack(states))
            st_ref[u] = entering[u]
        for u, lanes in enumerate(pairs):
            o_ref[:, lanes] = intra[u] + _bmm_nt(_chunks(qd_b[:, lanes]), entering[u].astype(BF16)).reshape(TH, LANE)

    wide = HG_PAIRS * LANE
    col = lambda base: pl.BlockSpec((TH, wide), lambda p, i: (i, base // wide + p))
    return pl.pallas_call(
        body, name="hgrn_fwd", grid=(NH // 2 // HG_PAIRS, s // TH),
        in_specs=[col(P_HQ), col(P_HF), col(P_HI), pl.BlockSpec((2, wide), lambda p, i: (0, p))],
        out_specs=[pl.BlockSpec((TH, wide), lambda p, i: (i, p)),
                   pl.BlockSpec((HG_PAIRS, nch, LANE, LANE), lambda p, i: (p, i, 0, 0))],
        out_shape=[jax.ShapeDtypeStruct((s, 512), F32), jax.ShapeDtypeStruct((NH // 2, s // HG_BLOCK, LANE, LANE), F32)],
        scratch_shapes=[pltpu.VMEM((HG_PAIRS, LANE, LANE), F32)],
        compiler_params=_params(("parallel", "arbitrary")),
    )(proj, proj, proj, lbl)


def _group_sum(v):
    low = lax.broadcasted_iota(jnp.int32, (v.shape[0], LANE), 1) < V_DIM
    blocks = []
    for b in range(v.shape[1] // LANE):
        blk = v[:, b * LANE:(b + 1) * LANE]
        s_low = jnp.sum(jnp.where(low, blk, 0.0), axis=-1, keepdims=True)
        s_high = jnp.sum(jnp.where(low, 0.0, blk), axis=-1, keepdims=True)
        blocks.append(jnp.where(low, s_low, s_high))
    return jnp.concatenate(blocks, axis=1)


def _dsilu(z, sg):
    return sg * (1.0 + z * (1.0 - sg))


def _mid(proj, attn, o_raw, x, tgt, g_hg, b_gate, g_post, wa, wb, w_out):
    s = x.shape[0]

    def body(attn_ref, ga_ref, o_ref, gb_ref, mg_ref, x_ref, t_ref, ghg_ref, bg_ref, gp_ref, wa_ref, wb_ref, wo_ref,
             loss_ref, dout_ref, dattn_ref, dga_ref, dor_ref, dgb_ref, dmg_ref, dwo_ref, dwa_ref, dwb_ref, dgp_ref, dbg_ref, dghg_ref):
        @pl.when(pl.program_id(0) == 0)
        def _():
            for rf in (loss_ref, dwo_ref, dwa_ref, dwb_ref, dgp_ref, dbg_ref, dghg_ref):
                rf[...] = jnp.zeros_like(rf)

        attn, za, orw, zb = attn_ref[...], ga_ref[...], o_ref[...], gb_ref[...]
        ghg, gp = ghg_ref[...], gp_ref[...]
        sga, sgb = _sigmoid(za), _sigmoid(zb)
        sa, sb = za * sga, zb * sgb
        ga = attn * sa
        rh = lax.rsqrt(_group_sum(orw * orw) * (1.0 / V_DIM) + EPS)
        on = (orw * rh) * ghg
        gb = on * sb
        ga_b, gb_b = ga.astype(BF16), gb.astype(BF16)
        blocks = [slice(t * (D // N_CHIPS), (t + 1) * (D // N_CHIPS)) for t in range(N_CHIPS)]
        ya = jnp.concatenate([_mm(ga_b, wa_ref[t]) for t in range(N_CHIPS)], axis=1)
        yb = jnp.concatenate([_mm(gb_b, wb_ref[t]) for t in range(N_CHIPS)], axis=1)
        gates = _sigmoid(mg_ref[...] + bg_ref[...])
        g0, g1 = gates[:, :D], gates[:, D:]
        m_b = (g0 * ya + g1 * yb).astype(BF16)
        y = _mm(m_b, wo_ref[...])
        ry = lax.rsqrt(jnp.mean(y * y, axis=-1, keepdims=True) + EPS)
        out = x_ref[...] + (y * ry) * gp
        err = out - t_ref[...]
        loss_ref[...] += 0.5 * jnp.sum(jnp.mean(err * err, axis=-1, keepdims=True), axis=0, keepdims=True)
        dout = err * (1.0 / D)
        dout_ref[...] = dout
        dgp_ref[...] += jnp.sum(dout * (y * ry), axis=0, keepdims=True)
        dgy = dout * gp
        dy = ry * dgy - y * (ry * ry * ry) * jnp.mean(y * dgy, axis=-1, keepdims=True)
        dy_b = dy.astype(BF16)
        dm = _mm_nt(dy_b, wo_ref[...])
        dya_b, dyb_b = (dm * g0).astype(BF16), (dm * g1).astype(BF16)
        dga = sum(_mm_nt(dya_b[:, cols], wa_ref[t]) for t, cols in enumerate(blocks))
        dgb = sum(_mm_nt(dyb_b[:, cols], wb_ref[t]) for t, cols in enumerate(blocks))
        dwo_ref[...] += _mm_tn(m_b, dy_b)
        for t, cols in enumerate(blocks):
            dwa_ref[t] += _mm_tn(ga_b, dya_b[:, cols])
            dwb_ref[t] += _mm_tn(gb_b, dyb_b[:, cols])
        dg0, dg1 = dm * ya, dm * yb
        dmg = jnp.concatenate([dg0 * g0 * (1.0 - g0), dg1 * g1 * (1.0 - g1)], axis=1)
        dmg_ref[...] = dmg.astype(BF16)
        dbg_ref[...] += jnp.sum(dmg, axis=0, keepdims=True)
        dattn_ref[...] = dga * sa
        dga_ref[...] = (dga * attn * _dsilu(za, sga)).astype(BF16)
        dgb_ref[...] = (dgb * on * _dsilu(zb, sgb)).astype(BF16)
        don = dgb * sb
        dghg_ref[...] += jnp.sum(don * (orw * rh), axis=0, keepdims=True)
        dgo = don * ghg
        dor_ref[...] = rh * dgo - orw * (rh * rh * rh) * (_group_sum(orw * dgo) * (1.0 / V_DIM))

    row = lambda w, j=0: pl.BlockSpec((TM_MID, w), lambda i: (i, j))
    full = lambda a: pl.BlockSpec(a.shape, lambda i: (0,) * a.ndim)
    acc = lambda shape: pl.BlockSpec(shape, lambda i: (0,) * len(shape))
    slabs = (N_CHIPS, 512, D // N_CHIPS)
    sds = jax.ShapeDtypeStruct
    return pl.pallas_call(
        body, name="mid", grid=(s // TM_MID,),
        in_specs=[row(512), row(512, P_GA // 512), row(512), row(512, P_GB // 512), row(2048, P_MERGE // 2048), row(D), row(D),
                  full(g_hg), full(b_gate), full(g_post), full(wa), full(wb), full(w_out)],
        out_specs=[acc((1, 1)), row(D), row(512), row(512), row(512), row(512), row(2048),
                   acc((D, D)), acc(slabs), acc(slabs), acc((1, D)), acc((1, 2048)), acc((1, 512))],
        out_shape=[sds((1, 1), F32), sds((s, D), F32), sds((s, 512), F32), sds((s, 512), BF16), sds((s, 512), F32), sds((s, 512), BF16),
                   sds((s, 2048), BF16), sds((D, D), F32), sds(slabs, F32), sds(slabs, F32), sds((1, D), F32),
                   sds((1, 2048), F32), sds((1, 512), F32)],
        compiler_params=_params(("arbitrary",)),
    )(attn, proj, o_raw, proj, proj, x, tgt, g_hg, b_gate, g_post, wa, wb, w_out)


def _attn_bwd(q, k, vv, attn, dattn, lse, token):
    s = q.shape[1]
    nt = s // TQ
    scale = 1.0 / math.sqrt(QK_NOPE + QK_ROPE)

    def body(q_ref, k_ref, v_ref, o_ref, do_ref, lse_ref, token_ref, dq_ref, dk_ref, dv_ref, do_s, delta_s):
        j = pl.program_id(1)

        @pl.when(j == 0)
        def _():
            dq_ref[...] = jnp.zeros_like(dq_ref)
            lane = lax.broadcasted_iota(jnp.int32, (TQ, LANE), 1)

            @pl.loop(0, nt)
            def _(i):
                rows = pl.ds(pl.multiple_of(i * TQ, TQ), TQ)
                do, o = do_ref[rows, :], o_ref[rows, :]
                for hh in range(2):
                    doh = jnp.where((lane < 64) if hh == 0 else (lane >= 64), do, 0.0)
                    do_s[hh, rows, :] = doh.astype(BF16)
                    delta_s[hh, rows, :] = jnp.broadcast_to(jnp.sum(doh * o, axis=-1, keepdims=True), (TQ, LANE))

        kjs, vjs = (k_ref[0], k_ref[1]), (v_ref[0], v_ref[1])

        def tile(hh, start, size, kj, vj, diag):
            rows = pl.ds(pl.multiple_of(start, size), size)
            wide = lambda a: jnp.concatenate([a] * (kj.shape[0] // LANE), axis=1)
            qi, do_b = q_ref[hh, rows, :], do_s[hh, rows, :]
            sc, dp = _mm_nt(qi, kj), _mm_nt(do_b, vj)
            p = jnp.exp2(sc * QK_SCALE2 - wide(lse_ref[hh, rows, :]))
            if diag:
                p = jnp.where(_diag_visible(size), p, 0.0)
            ds_b = (p * (dp - wide(delta_s[hh, rows, :]))).astype(BF16)
            dv, dk = _mm_tn(do_b, p.astype(BF16)), _mm_tn(qi, ds_b)
            dq_ref[hh, rows, :] += _mm(ds_b, kj)
            return dk, dv

        def step(i, carry):
            new = [tile(hh, i * TQ, TQ, kjs[hh], vjs[hh], False) for hh in range(2)]
            return tuple((carry[hh][0] + new[hh][0], carry[hh][1] + new[hh][1]) for hh in range(2))

        def diagonal(hh):
            k0, k1, v0, v1 = kjs[hh][:HQ], kjs[hh][HQ:], vjs[hh][:HQ], vjs[hh][HQ:]
            a = tile(hh, j * TQ, HQ, k0, v0, True)
            b = tile(hh, j * TQ + HQ, HQ, k0, v0, False)
            c = tile(hh, j * TQ + HQ, HQ, k1, v1, True)
            return jnp.concatenate([a[0] + b[0], c[0]], axis=1), jnp.concatenate([a[1] + b[1], c[1]], axis=1)

        carry = lax.fori_loop(j + 1, nt, step, (diagonal(0), diagonal(1)))
        for hh in range(2):
            dk_ref[hh] = carry[hh][0].T * scale
            dv_ref[hh] = carry[hh][1].T

        @pl.when(j == nt - 1)
        def _():
            dq_ref[...] = dq_ref[...] * scale

    whole = pl.BlockSpec((2, s, LANE), lambda p, j: (p, 0, 0))
    tile_spec = pl.BlockSpec((2, TQ, LANE), lambda p, j: (p, j, 0))
    cols = pl.BlockSpec((s, LANE), lambda p, j: (0, p))
    hs = jax.ShapeDtypeStruct((NH, s, LANE), F32)
    return pl.pallas_call(
        body, name="attn_bwd", grid=(NH // 2, nt),
        in_specs=[whole, tile_spec, tile_spec, cols, cols, whole, pl.BlockSpec((8, LANE), lambda p, j: (0, 0))],
        out_specs=[whole, tile_spec, tile_spec],
        out_shape=[hs, hs, hs],
        scratch_shapes=[pltpu.VMEM((2, s, LANE), BF16), pltpu.VMEM((2, s, LANE), F32)],
        compiler_params=_params(("parallel", "arbitrary")),
    )(q, k, vv, attn, dattn, lse, token)


def _hgrn_bwd(proj, lbl, states, do_raw):
    s = proj.shape[0]
    nt = s // TH
    nch = TH // HG_BLOCK

    def body(hq_ref, hf_ref, hi_ref, lbl_ref, st_ref, do_ref, dh_ref, dlbl_ref, dst, dlb):
        step = pl.program_id(1)

        @pl.when(step == 0)
        def _():
            dst[...] = jnp.zeros_like(dst)
            dlb[...] = jnp.zeros_like(dlb)

        m0, bd = _pair_masks()
        lb = _lower_bound(lbl_ref[...])
        gt = _hgrn_gates(hq_ref[...], hf_ref[...], lb)
        do = do_ref[...]
        qd, ki, ke = gt["qd"], gt["ki"], gt["ke"]
        v_b, do_b = hi_ref[...].astype(BF16), do.astype(BF16)
        qd_b, ki_b, ke_b = qd.astype(BF16), ki.astype(BF16), ke.astype(BF16)
        pairs = [slice(u * LANE, (u + 1) * LANE) for u in range(HG_PAIRS)]
        heads = [(lanes, m0 if hh == 0 else jnp.logical_not(m0)) for lanes in pairs for hh in range(2)]
        a_b = [jnp.where(gt["tri"], _mm_nt(jnp.where(mh, qd[:, lanes], 0.0).astype(BF16), ki_b[:, lanes]), 0.0).astype(BF16)
               for lanes, mh in heads]
        doh_b = [jnp.where(mh, do[:, lanes], 0.0).astype(BF16) for lanes, mh in heads]
        da_b = [jnp.where(gt["tri"], _mm_nt(d, v_b[:, lanes]), 0.0).astype(BF16) for d, (lanes, _) in zip(doh_b, heads)]
        dv_p, dqd_p, dki_p = [], [], []
        for u, lanes in enumerate(pairs):
            e, o = 2 * u, 2 * u + 1
            dv_p.append(_mm_tn(a_b[e], doh_b[e]) + _mm_tn(a_b[o], doh_b[o]))
            dqd_p.append(jnp.where(m0, _mm(da_b[e], ki_b[:, lanes]), _mm(da_b[o], ki_b[:, lanes])))
            dki_p.append(jnp.where(m0, _mm_tn(da_b[e], qd_b[:, lanes]), _mm_tn(da_b[o], qd_b[:, lanes])))
        fed = [_bmm_tn(_chunks(do_b[:, lanes]), _chunks(qd_b[:, lanes])) for lanes in pairs]
        leaving = []
        for u, lanes in enumerate(pairs):
            ds, left = dst[u], [None] * nch
            for n in reversed(range(nch)):
                left[n] = ds
                ds = gt["decay"][n][:, lanes] * ds + jnp.where(bd, fed[u][n], 0.0)
            dst[u] = ds
            leaving.append(jnp.stack(left))
        dke_p, dlast_p = [], []
        for u, lanes in enumerate(pairs):
            entering, leaving_b = st_ref[u], leaving[u].astype(BF16)
            dke3 = _bmm_nn(_chunks(v_b[:, lanes]), leaving_b)
            dv_p[u] = dv_p[u] + _bmm_nt(_chunks(ke_b[:, lanes]), leaving_b).reshape(TH, LANE)
            dqd_p[u] = dqd_p[u] + _bmm_nn(_chunks(do_b[:, lanes]), entering.astype(BF16)).reshape(TH, LANE)
            dke_p.append(dke3.reshape(TH, LANE))
            dlast_p.append(jnp.sum(dke3 * _chunks(ke[:, lanes]), axis=1, keepdims=True)
                           + jnp.sum(leaving[u] * entering, axis=1, keepdims=True) * gt["decay"][:, :, lanes])
        cat = lambda parts: jnp.concatenate(parts, axis=-1)
        dv, dqd, dki, dke, dlast = cat(dv_p), cat(dqd_p), cat(dki_p), cat(dke_p), cat(dlast_p)
        dk = dki * gt["ei"] + dke * gt["ee"]
        dcum = dqd * qd - dki * ki - dke * ke
        dg = _chunk_cumsum(dcum, reverse=True) + jnp.broadcast_to(dlast, (nch, HG_BLOCK, dlast.shape[-1])).reshape(dcum.shape)
        sig = gt["sig"]
        df = dg / gt["f"] - dk
        dlb[...] += jnp.sum(df * (1.0 - sig), axis=0, keepdims=True)
        dh_ref[0] = (dqd * gt["e"]).astype(BF16)
        dh_ref[1] = ((df * (1.0 - lb)) * sig * (1.0 - sig)).astype(BF16)
        dh_ref[2] = dv.astype(BF16)

        @pl.when(step == nt - 1)
        def _():
            lb = _lower_bound(lbl_ref[...])
            da0 = dlb[...] * lb * (1.0 - lb)
            dlbl_ref[...] = jnp.concatenate([da0, -da0], axis=0)

    wide = HG_PAIRS * LANE
    col = lambda base: pl.BlockSpec((TH, wide), lambda p, i: (nt - 1 - i, base // wide + p))
    tile = pl.BlockSpec((TH, wide), lambda p, i: (nt - 1 - i, p))
    sds = jax.ShapeDtypeStruct
    return pl.pallas_call(
        body, name="hgrn_bwd", grid=(NH // 2 // HG_PAIRS, nt),
        in_specs=[col(P_HQ), col(P_HF), col(P_HI), pl.BlockSpec((2, wide), lambda p, i: (0, p)),
                  pl.BlockSpec((HG_PAIRS, nch, LANE, LANE), lambda p, i: (p, nt - 1 - i, 0, 0)), tile],
        out_specs=[pl.BlockSpec((3, TH, wide), lambda p, i: (0, nt - 1 - i, p)), pl.BlockSpec((2, wide), lambda p, i: (0, p))],
        out_shape=[sds((3, s, 512), BF16), sds((2, 512), F32)],
        scratch_shapes=[pltpu.VMEM((HG_PAIRS, LANE, LANE), F32), pltpu.VMEM((1, wide), F32)],
        compiler_params=_params(("parallel", "arbitrary")),
    )(proj, proj, proj, lbl, states, do_raw)


def _norm_rows_bwd(v, r, g, dn):
    dgv = dn * g
    return r * dgv - v * (r * r * r) * jnp.mean(v * dgv, axis=-1, keepdims=True)


def _qkv_bwd(proj, dq, dk, dvv, g_q, g_kv, w_uq_p, w_k_p, w_v_p, rc, rs1, rs2):
    s = proj.shape[0]
    head_q = QK_NOPE + QK_ROPE

    def body(cq_ref, ckv_ref, dq_ref, dk_ref, dv_ref, gq_ref, gkv_ref, wq_ref, wk_ref, wv_ref, c_ref, s1_ref, s2_ref,
             dcq_ref, dckv_ref, dkpe_ref, dwq_out, dwkv_out, dgq_ref, dgkv_ref, dwq_ref, dwk_ref, dwv_ref):
        @pl.when(pl.program_id(0) == 0)
        def _():
            for rf in (dwq_ref, dwk_ref, dwv_ref, dgq_ref, dgkv_ref):
                rf[...] = jnp.zeros_like(rf)

        c, s1, s2 = c_ref[...], s1_ref[...], s2_ref[...]
        cq, ckv = cq_ref[...], ckv_ref[...]
        gq, gkv = gq_ref[...], gkv_ref[...]
        cqn, rq = _norm_rows(cq, gq)
        ckvn, rkv = _norm_rows(ckv, gkv)
        cqn_b, ckvn_b = cqn.astype(BF16), ckvn.astype(BF16)
        dqf = jnp.concatenate([_rope_t(dq_ref[h], c, s1, s2) for h in range(NH)], axis=1).astype(BF16)
        dkf = jnp.concatenate([dk_ref[h] for h in range(NH)], axis=1).astype(BF16)
        dvf = jnp.concatenate([dv_ref[h] for h in range(NH)], axis=1).astype(BF16)
        dkpe = dk_ref[0]
        for h in range(1, NH):
            dkpe = dkpe + dk_ref[h]
        lane = lax.broadcasted_iota(jnp.int32, (TM, LANE), 1)
        dkpe = jnp.where((lane >= QK_NOPE) & (lane < QK_NOPE + QK_ROPE), dkpe, 0.0)
        dkpe_ref[...] = _rope_t(dkpe, c, s1, s2).astype(BF16)
        dcqn = _mm_nt(dqf, wq_ref[...])
        pair = lambda a, t: a[:, t * 2 * LANE:(t + 1) * 2 * LANE]
        dckvn = sum(_mm_nt(pair(dkf, t), wk_ref[t]) + _mm_nt(pair(dvf, t), wv_ref[t]) for t in range(N_CHIPS))
        dwq_ref[...] += _mm_tn(cqn_b, dqf)
        dwk_ref[...] += _mm_tn(ckvn_b, dkf)
        dwv_ref[...] += _mm_tn(ckvn_b, dvf)
        dgq_ref[...] += jnp.sum(dcqn * (cq * rq), axis=0, keepdims=True)
        dgkv_ref[...] += jnp.sum(dckvn * (ckv * rkv), axis=0, keepdims=True)
        dcq_ref[...] = _norm_rows_bwd(cq, rq, gq, dcqn).astype(BF16)
        dckv_ref[...] = _norm_rows_bwd(ckv, rkv, gkv, dckvn).astype(BF16)

        @pl.when(pl.program_id(0) == pl.num_programs(0) - 1)
        def _():
            blk = lambda ref, h: ref[:, h * LANE:(h + 1) * LANE]
            lane = lax.broadcasted_iota(jnp.int32, (Q_LORA, LANE), 1)
            for j in range(NH * head_q // LANE):
                h0, w0 = divmod(j * LANE, head_q)
                first = blk(dwq_ref, h0) if w0 == 0 else pltpu.roll(blk(dwq_ref, h0), LANE - w0, 1)
                second = pltpu.roll(blk(dwq_ref, h0 + 1), head_q - w0, 1)
                dwq_out[:, j * LANE:(j + 1) * LANE] = jnp.where(lane < head_q - w0, first, second).astype(BF16)
            lane = lax.broadcasted_iota(jnp.int32, (KV_LORA, LANE), 1)
            for h in range(NH):
                vals = blk(dwv_ref, h) if h % 2 else pltpu.roll(blk(dwv_ref, h), V_DIM, 1)
                both = jnp.where(lane < QK_NOPE, blk(dwk_ref, h), vals).astype(BF16)
                dwkv_out[h // 2, :, (h % 2) * LANE:(h % 2 + 1) * LANE] = both

    row = lambda w, j=0: pl.BlockSpec((TM, w), lambda i: (i, j))
    full = lambda a: pl.BlockSpec(a.shape, lambda i: (0,) * a.ndim)
    acc = lambda *shape: pl.BlockSpec(shape, lambda i: (0,) * len(shape))
    heads = pl.BlockSpec((NH, TM, LANE), lambda i: (0, i, 0))
    sds = jax.ShapeDtypeStruct
    return pl.pallas_call(
        body, name="qkv_bwd", grid=(s // TM,),
        in_specs=[row(Q_LORA, P_CQ // Q_LORA), row(KV_LORA, P_CKV // KV_LORA), heads, heads, heads,
                  full(g_q), full(g_kv), full(w_uq_p), full(w_k_p), full(w_v_p), row(LANE), row(LANE), row(LANE)],
        out_specs=[row(Q_LORA), row(KV_LORA), row(LANE), acc(Q_LORA, NH * head_q), acc(NH // 2, KV_LORA, 2 * LANE),
                   acc(1, Q_LORA), acc(1, KV_LORA)],
        out_shape=[sds((s, Q_LORA), BF16), sds((s, KV_LORA), BF16), sds((s, LANE), BF16), sds((Q_LORA, NH * head_q), BF16),
                   sds((NH // 2, KV_LORA, 2 * LANE), BF16), sds((1, Q_LORA), F32), sds((1, KV_LORA), F32)],
        scratch_shapes=[pltpu.VMEM((Q_LORA, D), F32), pltpu.VMEM((KV_LORA, D), F32), pltpu.VMEM((KV_LORA, D), F32)],
        compiler_params=_params(("arbitrary",)),
    )(proj, proj, dq, dk, dvv, g_q, g_kv, w_uq_p, w_k_p, w_v_p, rc, rs1, rs2)


def _front_bwd(x, dout, dmg, dga, dh3, dgb, dcq, dckv, dkpe, g_pre, w_in_t, w_kpe, token):
    s = x.shape[0]

    def body(x_ref, do_ref, dmg_ref, dga_ref, dh3_ref, dgb_ref, dcq_ref, dckv_ref, dkpe_ref, g_ref, w_ref, k_ref, token_ref,
             gx_ref, dg_ref):
        @pl.when(pl.program_id(0) == 0)
        def _():
            dg_ref[...] = jnp.zeros_like(dg_ref)

        xv, g = x_ref[...], g_ref[...]
        _, r = _norm_rows(xv, g)
        pieces = ((dmg_ref[...], O_MERGE), (dga_ref[...], O_GA), (dh3_ref[0], O_HQ), (dh3_ref[1], O_HF), (dh3_ref[2], O_HI),
                  (dgb_ref[...], O_GB), (dcq_ref[...], O_CQ), (dckv_ref[...], O_CKV))
        dh = _mm(dkpe_ref[...], k_ref[...])
        for piece, off in pieces:
            dh = dh + _mm(piece, w_ref[off:off + piece.shape[1], :])
        dg_ref[...] += jnp.sum(dh * (xv * r), axis=0, keepdims=True)
        gx_ref[...] = do_ref[...] + _norm_rows_bwd(xv, r, g, dh)

    row = lambda w: pl.BlockSpec((TM, w), lambda i: (i, 0))
    full = lambda a: pl.BlockSpec(a.shape, lambda i: (0,) * a.ndim)
    sds = jax.ShapeDtypeStruct
    return pl.pallas_call(
        body, name="front_bwd", grid=(s // TM,),
        in_specs=[row(D), row(D), row(2048), row(512), pl.BlockSpec((3, TM, 512), lambda i: (0, i, 0)), row(512), row(Q_LORA),
                  row(KV_LORA), row(LANE), full(g_pre), full(w_in_t), full(w_kpe), pl.BlockSpec(memory_space=pl.ANY)],
        out_specs=[row(D), pl.BlockSpec((1, D), lambda i: (0, 0))],
        out_shape=[sds((s, D), F32), sds((1, D), F32)],
        compiler_params=_params(("arbitrary",)),
    )(x, dout, dmg, dga, dh3, dgb, dcq, dckv, dkpe, g_pre, w_in_t, w_kpe, token)


TK_GRAD = 1024


def _win_grad(h, pieces, name):
    s = h.shape[0]
    n = len(pieces)

    def body(h_ref, *refs):
        @pl.when(pl.program_id(0) == 0)
        def _():
            for o_ref in refs[n:]:
                o_ref[...] = jnp.zeros_like(o_ref)

        hv = h_ref[...]
        for d_ref, o_ref in zip(refs[:n], refs[n:]):
            if len(d_ref.shape) == 3:
                for k in range(d_ref.shape[0]):
                    o_ref[k] += _mm_tn(d_ref[k], hv)
            else:
                o_ref[...] += _mm_tn(d_ref[...], hv)

    def in_spec(p):
        if p.ndim == 3:
            return pl.BlockSpec((p.shape[0], TK_GRAD, p.shape[2]), lambda kk: (0, kk, 0))
        return pl.BlockSpec((TK_GRAD, p.shape[1]), lambda kk: (kk, 0))

    out_shapes = [(p.shape[0], p.shape[2], D) if p.ndim == 3 else (p.shape[1], D) for p in pieces]
    return pl.pallas_call(
        body, name=name, grid=(s // TK_GRAD,),
        in_specs=[pl.BlockSpec((TK_GRAD, D), lambda kk: (kk, 0))] + [in_spec(p) for p in pieces],
        out_specs=[pl.BlockSpec(sh, lambda kk, nd=len(sh): (0,) * nd) for sh in out_shapes],
        out_shape=[jax.ShapeDtypeStruct(sh, F32) for sh in out_shapes],
        compiler_params=_params(("arbitrary",)),
    )(h, *pieces)


QKV_ROWS = Q_LORA + KV_LORA + QK_ROPE


def _win_grad_qkv(h, dcq, dckv, dkpe):
    s = h.shape[0]
    half = QKV_ROWS // 2

    def body(h_ref, cq_ref, ckv_ref, kpe_ref, o_ref):
        @pl.when(pl.program_id(0) == 0)
        def _():
            o_ref[...] = jnp.zeros_like(o_ref)

        hv = h_ref[...]
        g_cq = _mm_tn(cq_ref[...], hv)
        o_ref[0] += g_cq[:half]
        o_ref[1, 0:Q_LORA - half] += g_cq[half:]
        o_ref[1, Q_LORA - half:Q_LORA + KV_LORA - half] += _mm_tn(ckv_ref[...], hv)
        o_ref[1, Q_LORA + KV_LORA - half:] += _mm_tn(kpe_ref[...], hv)[QK_NOPE:QK_NOPE + QK_ROPE]

    rows = lambda a: pl.BlockSpec((TK_GRAD, a.shape[1]), lambda kk: (kk, 0))
    return pl.pallas_call(
        body, name="win_grad_qkv", grid=(s // TK_GRAD,), in_specs=[rows(h), rows(dcq), rows(dckv), rows(dkpe)],
        out_specs=pl.BlockSpec((2, half, D), lambda kk: (0, 0, 0)), out_shape=jax.ShapeDtypeStruct((2, half, D), F32),
        compiler_params=_params(("arbitrary",)),
    )(h, dcq, dckv, dkpe)


def _pad_wuq(w_uq):
    rows = w_uq.shape[0]
    w = w_uq.reshape(rows, NH, QK_NOPE + QK_ROPE)
    return jnp.pad(w, ((0, 0), (0, 0), (0, LANE - QK_NOPE - QK_ROPE))).reshape(rows, NH * LANE)


def _pad_wukv(w_ukv):
    heads = w_ukv.shape[1] // (QK_NOPE + V_DIM)
    w = w_ukv.reshape(KV_LORA, heads, QK_NOPE + V_DIM)
    w_k = jnp.pad(w[:, :, :QK_NOPE], ((0, 0), (0, 0), (0, LANE - QK_NOPE))).reshape(KV_LORA, heads * LANE)
    wv = w[:, :, QK_NOPE:].reshape(KV_LORA, heads // 2, 2, 1, V_DIM)
    eye = jnp.eye(2, dtype=w.dtype).reshape(1, 1, 2, 2, 1)
    return w_k, (wv * eye).reshape(KV_LORA, heads * LANE)


def _local_step(x, tgt, g_pre, w_in_t, b_gate, g_q, g_kv, lb_logits, g_hgrn, g_post, weights, exchange=None):
    s = x.shape[0]
    w_kpe = _kpe_block(w_in_t)
    rc, rs1, rs2 = _rope_tables(s)
    g_hg = jnp.tile(g_hgrn, (1, NH))

    proj, h = _front_fwd(x, g_pre, w_in_t, w_kpe, weights.tokens)
    w_uq_p, w_k_p, w_v_p = weights.qkv(h)
    q, k, vv = _qkv_fwd(proj, g_q, g_kv, w_uq_p, w_k_p, w_v_p, rc, rs1, rs2)
    attn, lse = _attn_fwd(q, k, vv)
    o_raw, states = _hgrn_fwd(proj, lb_logits)
    wa, wb, w_out = weights.mid(o_raw)
    (loss, dout, dattn, dga, dor, dgb, dmg, d_wout, d_wa, d_wb, d_gpost, d_bgate, d_ghg) = _mid(
        proj, attn, o_raw, x, tgt, g_hg, b_gate, g_post, wa, wb, w_out)
    w_mg, w_ga, w_gb = _win_grad(h, [dmg, dga, dgb], "win_grad_mid")
    dh3, d_lbl = _hgrn_bwd(proj, lb_logits, states, dor)
    (w_h3,) = _win_grad(h, [dh3], "win_grad_hgrn")
    d_win_rest = jnp.concatenate([w_ga, w_h3[0], w_h3[1], w_h3[2], w_gb, w_mg], axis=0)
    early = dict(w_in_rest=d_win_rest, w_branch_a=d_wa, w_branch_b=d_wb, w_out=d_wout)
    token = exchange.start_early(early) if exchange else jnp.zeros((8, LANE), F32)
    dq, dk, dvv = _attn_bwd(q, k, vv, attn, dattn, lse, token)
    dcq, dckv, dkpe, d_wuq, d_wukv, d_gq, d_gkv = _qkv_bwd(proj, dq, dk, dvv, g_q, g_kv, w_uq_p, w_k_p, w_v_p, rc, rs1, rs2)
    late = dict(w_in_qkv=_win_grad_qkv(h, dcq, dckv, dkpe), w_uq=d_wuq, w_ukv=d_wukv)
    token = exchange.start_late(late) if exchange else jnp.zeros((8, LANE), F32)
    grad_x, d_gpre = _front_bwd(x, dout, dmg, dga, dh3, dgb, dcq, dckv, dkpe, g_pre, w_in_t, w_kpe, token)
    vec_grads = dict(g_pre=d_gpre, b_gate=d_bgate, g_q=d_gq, g_kv=d_gkv, lb_logits=d_lbl, g_hgrn=d_ghg, g_post=d_gpost)
    return loss, grad_x, dict(early, **late), vec_grads


SHARD_SHAPES = (("w_in", (1416, 1024)), ("w_uq", (192, 768)), ("w_ukv", (256, 256)), ("w_branch_a", (512, 256)),
                ("w_branch_b", (512, 256)), ("w_out", (256, 1024)))
BIG = tuple(n for n, _ in SHARD_SHAPES)
ROW_SHARDED = ("w_in", "w_uq", "w_out")
N_CHIPS = 4
W_IN_FORWARD_CUT = 704


def _to_block(name, a):
    return a[0].T if name == "w_in" else a[0]


def _from_block(name, a):
    return a.T[None] if name == "w_in" else a[None]
VEC_ROWS = (("g_pre", 0, 1024), ("b_gate", 1, 2048), ("g_q", 2, 768), ("g_kv", 3, 256), ("g_hgrn", 6, 64), ("g_post", 7, 1024))
VEC_LB_ROW = 4
VEC_SHAPE = (8, 2048)


def _split_by_chip(name, g):
    a, b = dict(SHARD_SHAPES)[name]
    return g.reshape(N_CHIPS, a, b) if name in ROW_SHARDED else g.reshape(a, N_CHIPS, b).transpose(1, 0, 2)


def _join_chips(name, w):
    a, b = dict(SHARD_SHAPES)[name]
    return w.reshape(N_CHIPS * a, b) if name in ROW_SHARDED else w.transpose(1, 0, 2).reshape(a, N_CHIPS * b)


MESH = pl.DeviceIdType.MESH
HBM = pl.BlockSpec(memory_space=pltpu.HBM)


def _mesh_place():
    x, y, c = lax.axis_index("x"), lax.axis_index("y"), lax.axis_index("c")
    return x, y, c, 2 * x + y, [(1 - x, y), (x, 1 - y), (1 - x, 1 - y)]


def _remote(src, dst, send_sems, recv_sems, k, to):
    return pltpu.make_async_remote_copy(src_ref=src, dst_ref=dst, send_sem=send_sems.at[k], recv_sem=recv_sems.at[k],
                                        device_id=to, device_id_type=MESH)


def _gather_w_in(shard):
    a, b = shard.shape
    cut = W_IN_FORWARD_CUT

    def body(src, out, ici_send, ici_recv, d2d_send, d2d_recv, local_sem):
        x, y, c = lax.axis_index("x"), lax.axis_index("y"), lax.axis_index("c")
        me, xn, yn, dg = 2 * x + y, 2 * (1 - x) + y, 2 * x + (1 - y), 2 * (1 - x) + (1 - y)
        to_x, to_y, sibling = (1 - x, y, c), (x, 1 - y, c), (x, y, 1 - c)
        first, rest = pl.ds(0, cut), pl.ds(cut, a - cut)

        whole = lambda ref, which: ref.at[:, pl.ds(pl.multiple_of(which * (b // 2), b // 2), b // 2)]
        own = pltpu.make_async_copy(src, out.at[me], local_sem)
        own.start()
        sends = [_remote(whole(src, c), whole(out.at[me], c), ici_send, ici_recv, 0, to_x),
                 _remote(whole(src, c), whole(out.at[me], c), ici_send, ici_recv, 1, to_y)]
        for cp in sends:
            cp.start()

        def landed(slot, rows, k, d2d_k, src_dev):
            piece = whole(out.at[slot], c) if rows is None else out.at[slot].at[rows, pl.ds(pl.multiple_of(c * (b // 2), b // 2), b // 2)]
            _remote(piece, piece, ici_send, ici_recv, k, src_dev).wait_recv()
            cp = _remote(piece, piece, d2d_send, d2d_recv, d2d_k, sibling)
            cp.start()
            sends.append(cp)
            return piece

        def pass_on(slot, rows, k, to):
            piece = out.at[slot].at[rows, pl.ds(pl.multiple_of(c * (b // 2), b // 2), b // 2)]
            cp = _remote(piece, piece, ici_send, ici_recv, k, to)
            cp.start()
            sends.append(cp)

        landed(xn, None, 0, 0, to_x)
        pass_on(xn, first, 2, to_y)
        landed(yn, None, 1, 1, to_y)
        pass_on(yn, rest, 3, to_x)
        landed(dg, first, 2, 2, to_y)
        landed(dg, rest, 3, 3, to_x)
        other = pl.ds(pl.multiple_of((1 - c) * (b // 2), b // 2), b // 2)
        for d2d_k, (slot, rows) in enumerate(((xn, None), (yn, None), (dg, first), (dg, rest))):
            piece = out.at[slot].at[:, other] if rows is None else out.at[slot].at[rows, other]
            _remote(piece, piece, d2d_send, d2d_recv, d2d_k, sibling).wait_recv()
        for cp in sends:
            cp.wait_send()
        own.wait()

    sems = pltpu.SemaphoreType.DMA((4,))
    return pl.pallas_call(
        body, name="gather_w_in", in_specs=[HBM], out_specs=HBM,
        out_shape=jax.ShapeDtypeStruct((N_CHIPS, a, b), shard.dtype),
        scratch_shapes=[sems, sems, sems, sems, pltpu.SemaphoreType.DMA],
        compiler_params=pltpu.CompilerParams(has_side_effects=True),
    )(shard)


def _sibling_exchange(srcs, name, after=None, halves=False):
    n = len(srcs)
    extra = [] if after is None else [after]

    def body(*refs):
        src_refs, outs = refs[:n], refs[n + len(extra):2 * n + len(extra)]
        send_sems, recv_sems = refs[2 * n + len(extra):]
        c = lax.axis_index("c")
        sibling = (lax.axis_index("x"), lax.axis_index("y"), 1 - c)
        if halves:
            src_refs = [ref.at[1 - c] for ref in src_refs]
        copies = [_remote(src_refs[k], outs[k], send_sems, recv_sems, k, sibling) for k in range(n)]
        for cp in copies:
            cp.start()
        for cp in copies:
            cp.wait()

    sems = pltpu.SemaphoreType.DMA((n,))
    return pl.pallas_call(
        body, name=name, in_specs=[HBM] * n + [pl.BlockSpec(memory_space=pl.ANY)] * len(extra), out_specs=[HBM] * n,
        out_shape=[jax.ShapeDtypeStruct(s.shape[1:] if halves else s.shape, s.dtype) for s in srcs],
        scratch_shapes=[sems, sems],
        compiler_params=pltpu.CompilerParams(has_side_effects=True),
    )(*srcs, *extra)


SEM = pl.BlockSpec(memory_space=pltpu.SEMAPHORE)
DATAFLOW = pltpu.SideEffectType.DATAFLOW_SIDE_EFFECTING


def _exchange_copies(srcs, to_first, src_refs, land_refs, send_sems, recv_sems):
    x, y, c, me, chips = _mesh_place()
    n = len(srcs)
    sends, recvs = [], []
    for k in range(n):
        if k in to_first:
            base = 3 * n + 4 * to_first.index(k)
            sends.append((me != 0, pltpu.make_async_remote_copy(
                src_ref=src_refs[k], dst_ref=land_refs[k].at[me], send_sem=send_sems.at[base], recv_sem=recv_sems.at[base + me],
                device_id=(0, 0, c), device_id_type=MESH)))
            for s in range(1, N_CHIPS):
                recvs.append((me == 0, pltpu.make_async_remote_copy(
                    src_ref=src_refs[k], dst_ref=land_refs[k].at[s], send_sem=send_sems.at[base], recv_sem=recv_sems.at[base + s],
                    device_id=(s // 2, s % 2, c), device_id_type=MESH)))
        else:
            slab = (lambda t, k=k: src_refs[k]) if srcs[k].ndim == 2 else (lambda t, k=k: src_refs[k].at[t])
            for j, (px, py) in enumerate(chips):
                sends.append((None, _remote(slab(2 * px + py), land_refs[k].at[me], send_sems, recv_sems, 3 * k + j, (px, py, c))))
                recvs.append((None, _remote(slab(me), land_refs[k].at[2 * px + py], send_sems, recv_sems, 3 * k + j, (px, py, c))))
    return sends, recvs


def _when(pred, fn):
    if pred is None:
        fn()
    else:
        pl.when(pred)(fn)


def _exchange_start(srcs, to_first, name, after=None):
    n = len(srcs)
    n_sems = 3 * n + 4 * len(to_first)
    lands = [lax.empty((N_CHIPS,) + s.shape[-2:], s.dtype) for s in srcs]
    extra = [] if after is None else [after]

    def body(*refs):
        src_refs, land_refs = refs[:n], refs[n:2 * n]
        send_sems, recv_sems, token = refs[2 * n + len(extra)], refs[2 * n + len(extra) + 1], refs[-1]
        sends, _ = _exchange_copies(srcs, to_first, src_refs, land_refs, send_sems, recv_sems)
        for pred, cp in sends:
            _when(pred, cp.start)
        token[...] = jnp.zeros_like(token)

    hbm = lambda a: pltpu.HBM(a.shape, a.dtype)
    res = pl.pallas_call(
        body, name=name,
        out_shape=[pltpu.SemaphoreType.DMA((n_sems,)), pltpu.SemaphoreType.DMA((n_sems,))] + [hbm(a) for a in srcs + lands]
        + [jax.ShapeDtypeStruct((8, LANE), F32)],
        in_specs=[HBM] * (2 * n) + [pl.BlockSpec(memory_space=pl.ANY)] * len(extra),
        out_specs=[SEM, SEM] + [HBM] * (2 * n) + [pl.BlockSpec(memory_space=pltpu.VMEM)],
        input_output_aliases={i: 2 + i for i in range(2 * n)},
        compiler_params=pltpu.CompilerParams(has_side_effects=DATAFLOW),
    )(*[pltpu.with_memory_space_constraint(a, pltpu.HBM) for a in srcs + lands], *extra)
    return res[:-1], res[-1]


def _exchange_wait(srcs, to_first, started, after, name):
    n = len(srcs)
    send_sems, recv_sems, thru = started[0], started[1], started[2:]

    def body(*refs):
        src_refs, land_refs, send_ref, recv_ref = refs[:n], refs[n:2 * n], refs[2 * n], refs[2 * n + 1]
        sends, recvs = _exchange_copies(srcs, to_first, src_refs, land_refs, send_ref, recv_ref)
        for pred, cp in sends:
            _when(pred, cp.wait_send)
        for pred, cp in recvs:
            _when(pred, cp.wait_recv)

    res = pl.pallas_call(
        body, name=name, out_shape=[pltpu.HBM(a.shape, a.dtype) for a in thru],
        in_specs=[HBM] * (2 * n) + [SEM, SEM, pl.BlockSpec(memory_space=pl.ANY)], out_specs=[HBM] * (2 * n),
        input_output_aliases={i: i for i in range(2 * n)},
        compiler_params=pltpu.CompilerParams(has_side_effects=DATAFLOW),
    )(*thru, send_sems, recv_sems, after)
    return res[:n], res[n:]


ROW_TILE = 256
COL_TILE = 256


def _block_tiling(a, b):
    if a <= ROW_TILE or a % ROW_TILE == 0:
        ta = min(a, ROW_TILE)
        return a // ta, (ta, b), lambda i: (i, 0)
    return b // COL_TILE, (a, COL_TILE), lambda i: (0, i)


def _sum_landed(land, own, name, first_land=None, first_own=None):
    _, a, b = land.shape
    steps, tile, at = _block_tiling(a, b)
    extra = first_land is not None

    def body(me_ref, *refs):
        p_ref, own_ref, o_ref = refs[0], refs[1], refs[-1]
        me = me_ref[0]
        own = own_ref[...].astype(F32)
        slot = lambda t: jnp.where(me == t, own, p_ref[t].astype(F32))
        o_ref[...] = ((slot(0) + slot(1)) + slot(2)) + slot(3)
        if extra:
            fp_ref, fo_ref = refs[2], refs[3]
            r = fo_ref.shape[0]

            @pl.when(me == 0)
            def _():
                f = lambda t: fp_ref[t].astype(F32)
                rows = pl.ds(pl.multiple_of(lax.axis_index("c") * r, 8), r)
                o_ref[rows, :] += ((fo_ref[...].astype(F32) + f(1)) + f(2)) + f(3)

    own_spec = pl.BlockSpec(tile, lambda i, me: at(i)) if own.ndim == 2 else pl.BlockSpec((None,) + tile, lambda i, me: (me[0],) + at(i))
    in_specs = [pl.BlockSpec((N_CHIPS,) + tile, lambda i, me: (0,) + at(i)), own_spec]
    args = [land, own]
    if extra:
        r = first_own.shape[0]
        assert tile[0] == a, "the extra rows need whole columns in a step"
        in_specs += [pl.BlockSpec((N_CHIPS, r, tile[1]), lambda i, me: (0,) + at(i)), pl.BlockSpec((r, tile[1]), lambda i, me: at(i))]
        args += [first_land, first_own]
    me = jnp.reshape(2 * lax.axis_index("x") + lax.axis_index("y"), (1,)).astype(jnp.int32)
    return pl.pallas_call(
        body, name=name,
        grid_spec=pltpu.PrefetchScalarGridSpec(num_scalar_prefetch=1, grid=(steps,), in_specs=in_specs,
                                               out_specs=pl.BlockSpec(tile, lambda i, me: at(i))),
        out_shape=jax.ShapeDtypeStruct((a, b), F32), compiler_params=_params(("parallel",)),
    )(me, *args)


def _sum_landed_small(lands, owns, name):
    n = len(lands)

    def body(*refs):
        me = 2 * lax.axis_index("x") + lax.axis_index("y")
        for p_ref, own_ref, o_ref in zip(refs[:n], refs[n:2 * n], refs[2 * n:]):
            own = own_ref[me].astype(F32)
            slot = lambda t: jnp.where(me == t, own, p_ref[t].astype(F32))
            o_ref[...] = ((slot(0) + slot(1)) + slot(2)) + slot(3)

    return pl.pallas_call(body, name=name, out_shape=[jax.ShapeDtypeStruct(o.shape[1:], F32) for o in owns],
                          compiler_params=_params(()))(*lands, *owns)


def _adamw_small(mine, theirs, states, name):
    n = len(mine)

    def body(*refs):
        ins, outs = refs[:5 * n], refs[5 * n:]
        for k in range(n):
            a_ref, b_ref, w_ref, m_ref, v_ref = ins[5 * k:5 * k + 5]
            g = a_ref[...] + b_ref[...]
            outs[4 * k][...] = g
            outs[4 * k + 1][...], outs[4 * k + 2][...], outs[4 * k + 3][...] = _adamw_math(g, w_ref[...], m_ref[...], v_ref[...])

    args = [t for k in range(n) for t in (mine[k], theirs[k], *states[k])]
    res = pl.pallas_call(body, name=name, out_shape=[jax.ShapeDtypeStruct(mine[k].shape, F32) for k in range(n) for _ in range(4)],
                         compiler_params=_params(()))(*args)
    return [tuple(res[4 * k:4 * k + 4]) for k in range(n)]


def _add_cast(halves, b, name):
    def body(a_ref, b_ref, o_ref):
        o_ref[...] = (a_ref[lax.axis_index("c")] + b_ref[...]).astype(BF16)

    return pl.pallas_call(body, name=name, out_shape=jax.ShapeDtypeStruct(b.shape, BF16),
                          compiler_params=_params(()))(halves, b)


class _LaterWeights:
    MID = ("w_branch_a", "w_branch_b", "w_out")

    def __init__(self, blocks, after):
        self.qkv_blocks = [_pad_wuq(blocks["w_uq"]), *_pad_wukv(blocks["w_ukv"])]
        self.mid_blocks = [blocks[n] for n in self.MID]
        self.qkv_started, t1 = _exchange_start(self.qkv_blocks, (), "weights_qkv_start", after)
        self.mid_started, t2 = _exchange_start(self.mid_blocks, (), "weights_mid_start", after)
        self.tokens = [t1, t2]

    @staticmethod
    def _whole(blocks, joined, started, after, name):
        _, landed = _exchange_wait(blocks, (), started, after, name)
        me = 2 * lax.axis_index("x") + lax.axis_index("y")
        out = []
        for block, land, join in zip(blocks, landed, joined):
            w = lax.dynamic_update_index_in_dim(land, block, me, 0)
            out.append(w.reshape(N_CHIPS * block.shape[0], block.shape[1]) if join else w)
        return out

    def qkv(self, after):
        return self._whole(self.qkv_blocks, (True, False, False), self.qkv_started, after, "weights_qkv_wait")

    def mid(self, after):
        return self._whole(self.mid_blocks, (False, False, True), self.mid_started, after, "weights_mid_wait")


class _GradExchange:
    EARLY = ("w_in", "w_branch_a", "w_branch_b", "w_out")
    LATE = ("w_uq", "w_ukv")

    def __init__(self, state):
        self.state = state
        self.outs = {}

    def start_early(self, g):
        full = jnp.concatenate([jnp.zeros((QKV_ROWS, D), F32), g["w_in_rest"]], axis=0)
        g = dict(g, w_in=full)
        self.early = [(g[n] if g[n].ndim == 3 else _split_by_chip(n, g[n])).astype(BF16) for n in self.EARLY]
        self.early_started, token = _exchange_start(self.early, (), "grads_early_start")
        return token

    def start_late(self, g):
        self.early, self.early_landed = _exchange_wait(self.early, (), self.early_started, g["w_uq"], "grads_early_wait")
        (theirs,) = _sibling_exchange([g["w_in_qkv"]], "sibling_qkv_rows", halves=True)
        self.late = [_split_by_chip("w_uq", g["w_uq"]), g["w_ukv"], _add_cast(g["w_in_qkv"], theirs, "add_qkv_rows")]
        self.late_started, token = _exchange_start(self.late, (2,), "grads_late_start")
        names = self.EARLY[1:]
        mine = _sum_landed_small(self.early_landed[1:], self.early[1:], "sum_early")
        theirs = _sibling_exchange(mine, "sibling_early", after=token)
        for n, out in zip(names, _adamw_small(mine, theirs, [self.state[n] for n in names], "adamw_early")):
            self.outs[n] = out
        return self.outs[names[-1]][0]

    def finish(self, after):
        late, late_landed = _exchange_wait(self.late, (2,), self.late_started, after, "grads_late_wait")
        sums = {"w_in": _sum_landed(self.early_landed[0], self.early[0], "sum_w_in", first_land=late_landed[2], first_own=late[2])}
        small = _sum_landed_small(late_landed[:2], late[:2], "sum_late")
        sums.update(zip(self.LATE, small))
        return sums


def _adamw_math(g, w, m, v):
    nm = ADAM_B1 * m + (1.0 - ADAM_B1) * g
    nv = ADAM_B2 * v + (1.0 - ADAM_B2) * (g * g)
    m_hat = nm / (1.0 - ADAM_B1 ** ADAM_STEP)
    v_hat = nv / (1.0 - ADAM_B2 ** ADAM_STEP)
    return -ADAM_LR * (m_hat / (jnp.sqrt(v_hat) + ADAM_EPS) + ADAM_WD * w), nm, nv


def _adamw(p_mine, p_sibling, w, m, v, name):
    a, b = p_mine.shape
    steps, tile, at = _block_tiling(a, b)

    def body(a_ref, b_ref, w_ref, m_ref, v_ref, g_ref, d_ref, nm_ref, nv_ref):
        g = a_ref[...] + b_ref[...]
        g_ref[...] = g
        d_ref[...], nm_ref[...], nv_ref[...] = _adamw_math(g, w_ref[...], m_ref[...], v_ref[...])

    spec = pl.BlockSpec(tile, at)
    sds = jax.ShapeDtypeStruct((a, b), F32)
    return pl.pallas_call(
        body, name=name, grid=(steps,), in_specs=[spec] * 5, out_specs=[spec] * 4, out_shape=[sds] * 4,
        compiler_params=_params(("parallel",)),
    )(p_mine, p_sibling, w, m, v)


LOSS_AT = (2, 1024)


def _vec_pack(vg, loss):
    names = [n for n, _, _ in VEC_ROWS]

    def body(*refs):
        o_ref = refs[-1]
        lb_ref, loss_ref = refs[len(names)], refs[len(names) + 1]
        o_ref[...] = jnp.zeros_like(o_ref)
        o_ref[LOSS_AT[0]:LOSS_AT[0] + 1, LOSS_AT[1]:LOSS_AT[1] + LANE] = jnp.broadcast_to(loss_ref[...], (1, LANE))
        for (name, row, size), ref in zip(VEC_ROWS, refs):
            if name == "g_hgrn":
                r = lax.broadcasted_iota(jnp.int32, (NH * V_DIM, LANE), 0)
                c = lax.broadcasted_iota(jnp.int32, (NH * V_DIM, LANE), 1)
                fold = ((r % V_DIM) == c).astype(F32)
                o_ref[row:row + 1, 0:LANE] = jnp.dot(ref[...], fold, precision=HIGHEST, preferred_element_type=F32)
            else:
                o_ref[row:row + 1, 0:size] = ref[...]
        o_ref[VEC_LB_ROW:VEC_LB_ROW + 2, 0:512] = lb_ref[...]

    return pl.pallas_call(body, name="vec_pack", out_shape=jax.ShapeDtypeStruct(VEC_SHAPE, F32))(
        *[vg[n] for n in names], vg["lb_logits"], loss)


def _adamw_vec(p_mine, p_sibling, w, m, v):
    names = [n for n, _, _ in VEC_ROWS] + ["lb_logits"]
    k = len(names)

    def body(a_ref, b_ref, *refs):
        ins, outs = refs[:3 * k], refs[3 * k:]
        at = (slice(LOSS_AT[0], LOSS_AT[0] + 1), slice(LOSS_AT[1], LOSS_AT[1] + LANE))
        outs[-1][...] = a_ref[at] + b_ref[at]
        for i, name in enumerate(names):
            if name == "lb_logits":
                rows, cols = slice(VEC_LB_ROW, VEC_LB_ROW + 2), slice(0, 512)
            else:
                _, row, size = VEC_ROWS[i]
                rows, cols = slice(row, row + 1), slice(0, size)
            g = a_ref[rows, cols] + b_ref[rows, cols]
            d, nm, nv = _adamw_math(g, ins[i][...], ins[k + i][...], ins[2 * k + i][...])
            for o_ref, val in zip(outs[4 * i:4 * i + 4], (g, d, nm, nv)):
                o_ref[...] = val

    shapes = [jax.ShapeDtypeStruct(w[n].shape, F32) for n in names for _ in range(4)] + [jax.ShapeDtypeStruct((1, LANE), F32)]
    res = pl.pallas_call(body, name="adamw_vec", out_shape=shapes)(
        p_mine, p_sibling, *[w[n] for n in names], *[m[n] for n in names], *[v[n] for n in names])
    return [{n: res[4 * i + j] for i, n in enumerate(names)} for j in range(4)], res[-1]


WEIGHTS = ("g_pre", "w_in", "b_gate", "g_q", "w_uq", "g_kv", "w_ukv", "lb_logits", "g_hgrn", "w_branch_a", "w_branch_b", "w_out", "g_post")


def kernel(x, g_pre, w_in, b_gate, g_q, w_uq, g_kv, w_ukv, lb_logits, g_hgrn, w_branch_a, w_branch_b, w_out, g_post, loss_target, m_g_pre, m_w_in, m_b_gate, m_g_q, m_w_uq, m_g_kv, m_w_ukv, m_lb_logits, m_g_hgrn, m_w_branch_a, m_w_branch_b, m_w_out, m_g_post, v_g_pre, v_w_in, v_b_gate, v_g_q, v_w_uq, v_g_kv, v_w_ukv, v_lb_logits, v_g_hgrn, v_w_branch_a, v_w_branch_b, v_w_out, v_g_post):
    w = dict(g_pre=g_pre, w_in=w_in, b_gate=b_gate, g_q=g_q, w_uq=w_uq, g_kv=g_kv, w_ukv=w_ukv, lb_logits=lb_logits, g_hgrn=g_hgrn,
             w_branch_a=w_branch_a, w_branch_b=w_branch_b, w_out=w_out, g_post=g_post)
    m = dict(g_pre=m_g_pre, w_in=m_w_in, b_gate=m_b_gate, g_q=m_g_q, w_uq=m_w_uq, g_kv=m_g_kv, w_ukv=m_w_ukv, lb_logits=m_lb_logits,
             g_hgrn=m_g_hgrn, w_branch_a=m_w_branch_a, w_branch_b=m_w_branch_b, w_out=m_w_out, g_post=m_g_post)
    v = dict(g_pre=v_g_pre, w_in=v_w_in, b_gate=v_b_gate, g_q=v_g_q, w_uq=v_w_uq, g_kv=v_g_kv, w_ukv=v_w_ukv, lb_logits=v_lb_logits,
             g_hgrn=v_g_hgrn, w_branch_a=v_w_branch_a, w_branch_b=v_w_branch_b, w_out=v_w_out, g_post=v_g_post)
    blocks = {n: _to_block(n, w[n]).astype(BF16) for n in BIG}
    w_in_all = _gather_w_in(blocks["w_in"])
    weights = _LaterWeights(blocks, w_in_all)
    state = {n: [_to_block(n, t[n]) for t in (w, m, v)] for n in BIG}
    exchange = _GradExchange(state)
    loss, grad_x, _, vec_grads = _local_step(
        x[0], loss_target[0], g_pre, _join_chips("w_in", w_in_all), b_gate, g_q, g_kv, lb_logits, g_hgrn, g_post, weights, exchange)
    vec = _vec_pack(vec_grads, loss)
    vec_started, token = _exchange_start([vec], (), "vec_start")
    sums = exchange.finish(token)
    rest = tuple(sums)
    (vec,), (vec_landed,) = _exchange_wait([vec], (), vec_started, sums[rest[-1]], "vec_wait")
    mine = [sums[n] for n in rest] + [_sum_landed(vec_landed, vec, "sum_vec")]
    theirs = _sibling_exchange(mine, "sibling_grads")
    done = dict(exchange.outs)
    done["w_in"] = _adamw(mine[0], theirs[0], *state["w_in"], "adamw_w_in")
    small = _adamw_small(mine[1:-1], theirs[1:-1], [state[n] for n in rest[1:]], "adamw_late")
    done.update(zip(rest[1:], small))
    outs = [{}, {}, {}, {}]
    for n in BIG:
        for o, val in zip(outs, done[n]):
            o[n] = _from_block(n, val)
    vec_outs, total = _adamw_vec(mine[-1], theirs[-1], w, m, v)
    for o, vals in zip(outs, vec_outs):
        o.update(vals)
    return (total[0, 0], grad_x[None], *[o[n] for o in outs for n in WEIGHTS])
```

```python
import math

import numpy as np
import jax
import jax.numpy as jnp
from jax import lax
from jax.experimental import pallas as pl
from jax.experimental.pallas import tpu as pltpu

F32 = jnp.float32
BF16 = jnp.bfloat16
HIGHEST = lax.Precision.HIGHEST

D = 1024
NH = 8
QK_NOPE, QK_ROPE, V_DIM = 64, 32, 64
Q_LORA, KV_LORA = 768, 256
CHUNK = 64
HG_BLOCK = 32
EPS = 1e-6
LANE = 128
P_MERGE, P_GA, P_HQ, P_HF, P_HI, P_GB, P_CQ, P_CKV, P_KPE = 0, 2048, 2560, 3072, 3584, 4096, 4608, 5376, 5632
D_P = 5760
O_CQ, O_CKV, O_KPE, O_GA, O_HQ, O_HF, O_HI, O_GB, O_MERGE = 0, 768, 1024, 1056, 1568, 2080, 2592, 3104, 3616

TM = 512
TM_MID = 256
TQ = 1024
ONES_LANE = (LANE - 1, 0)
TH = 256
HG_PAIRS = 4
VMEM_LIMIT = 56 * 1024 * 1024

ADAM_LR, ADAM_B1, ADAM_B2, ADAM_EPS, ADAM_WD, ADAM_STEP = 0.001, 0.9, 0.999, 1e-08, 0.01, 10

NT_DIMS = (((1,), (1,)), ((), ()))
TN_DIMS = (((0,), (0,)), ((), ()))


def _params(sem):
    return pltpu.CompilerParams(dimension_semantics=sem, vmem_limit_bytes=VMEM_LIMIT)


def _mm(a, b):
    return jnp.dot(a, b, preferred_element_type=F32)


def _mm_nt(a, b):
    return lax.dot_general(a, b, NT_DIMS, preferred_element_type=F32)


def _mm_tn(a, b):
    return lax.dot_general(a, b, TN_DIMS, preferred_element_type=F32)


def _sigmoid(z):
    return jax.nn.sigmoid(z)


def _rope(v, c, s1, s2):
    return v * c + pltpu.roll(v, 112, 1) * s1 + pltpu.roll(v, 16, 1) * s2


def _rope_t(dy, c, s1, s2):
    return dy * c + pltpu.roll(dy * s1, 16, 1) + pltpu.roll(dy * s2, 112, 1)


def _rope_tables(s):
    f32 = np.float32
    inv = f32(10000.0) ** (-np.arange(0, QK_ROPE, 2, dtype=f32) / f32(QK_ROPE))
    ang = np.arange(s, dtype=f32)[:, None] * inv[None, :]
    cos, sin = np.cos(ang).astype(f32), np.sin(ang).astype(f32)
    z64, z32, o64, o32 = np.zeros((s, 64), f32), np.zeros((s, 32), f32), np.ones((s, 64), f32), np.ones((s, 32), f32)
    z16 = np.zeros((s, 16), f32)
    c = np.concatenate([o64, cos, cos, o32], axis=1)
    s1 = np.concatenate([z64, -sin, z16, z32], axis=1)
    s2 = np.concatenate([z64, z16, sin, z32], axis=1)
    return jnp.asarray(c), jnp.asarray(s1), jnp.asarray(s2)


W_IN_RUNS = ((O_MERGE, 2048, P_MERGE), (O_GA, O_MERGE - O_GA, P_GA), (O_CQ, O_KPE - O_CQ, P_CQ))


def _kpe_block(w_in_t):
    z = lambda n: jnp.zeros((n, w_in_t.shape[1]), w_in_t.dtype)
    return jnp.concatenate([z(64), w_in_t[O_KPE:O_KPE + QK_ROPE], z(32)], axis=0)


def _front_fwd(x, g_pre, w_in_t, w_kpe, tokens=()):
    s = x.shape[0]
    tokens = list(tokens)

    def body(x_ref, g_ref, w_ref, k_ref, *refs):
        o_ref, h_ref = refs[len(tokens):]
        xv = x_ref[...]
        r = lax.rsqrt(jnp.mean(xv * xv, axis=-1, keepdims=True) + EPS)
        h = ((xv * r) * g_ref[...]).astype(BF16)
        h_ref[...] = h
        for row, rows, col in W_IN_RUNS:
            o_ref[:, col:col + rows] = _mm_nt(h, w_ref[row:row + rows, :])
        o_ref[:, P_KPE:P_KPE + LANE] = _mm_nt(h, k_ref[...])

    full = lambda a: pl.BlockSpec(a.shape, lambda i: (0,) * a.ndim)
    return pl.pallas_call(
        body, name="front_fwd", grid=(s // TM,),
        in_specs=[pl.BlockSpec((TM, D), lambda i: (i, 0)), pl.BlockSpec((1, D), lambda i: (0, 0)), full(w_in_t), full(w_kpe)]
        + [pl.BlockSpec((8, LANE), lambda i: (0, 0))] * len(tokens),
        out_specs=[pl.BlockSpec((TM, D_P), lambda i: (i, 0)), pl.BlockSpec((TM, D), lambda i: (i, 0))],
        out_shape=[jax.ShapeDtypeStruct((s, D_P), F32), jax.ShapeDtypeStruct((s, D), BF16)],
        compiler_params=_params(("parallel",)),
    )(x, g_pre, w_in_t, w_kpe, *tokens)


def _norm_rows(v, g):
    r = lax.rsqrt(jnp.mean(v * v, axis=-1, keepdims=True) + EPS)
    return (v * r) * g, r


def _qkv_fwd(proj, g_q, g_kv, w_uq_p, w_k_p, w_v_p, rc, rs1, rs2):
    s = proj.shape[0]

    def body(cq_ref, ckv_ref, kpe_ref, gq_ref, gkv_ref, wq_ref, wk_ref, wv_ref, c_ref, s1_ref, s2_ref, q_ref, k_ref, v_ref):
        c, s1, s2 = c_ref[...], s1_ref[...], s2_ref[...]
        cqn, _ = _norm_rows(cq_ref[...], gq_ref[...])
        ckvn, _ = _norm_rows(ckv_ref[...], gkv_ref[...])
        ckvn = ckvn.astype(BF16)
        qf = _mm(cqn.astype(BF16), wq_ref[...])
        kf = jnp.concatenate([_mm(ckvn, wk_ref[t]) for t in range(N_CHIPS)], axis=1)
        vf = jnp.concatenate([_mm(ckvn, wv_ref[t]) for t in range(N_CHIPS)], axis=1)
        kpe = _rope(kpe_ref[...], c, s1, s2)
        lane = lax.broadcasted_iota(jnp.int32, (TM, LANE), 1)
        for h in range(NH):
            blk = slice(h * LANE, (h + 1) * LANE)
            q_ref[h] = _rope(qf[:, blk], c, s1, s2).astype(BF16)
            k_ref[h] = (kf[:, blk] + kpe).astype(BF16)
            v_ref[h] = jnp.where(lane == ONES_LANE[h % 2], 1.0, vf[:, blk]).astype(BF16)

    row = lambda w, j: pl.BlockSpec((TM, w), lambda i: (i, j))
    full = lambda a: pl.BlockSpec(a.shape, lambda i: (0,) * a.ndim)
    hs = jax.ShapeDtypeStruct((NH, s, LANE), BF16)
    return pl.pallas_call(
        body, name="qkv_fwd", grid=(s // TM,),
        in_specs=[row(Q_LORA, P_CQ // Q_LORA), row(KV_LORA, P_CKV // KV_LORA), row(LANE, P_KPE // LANE),
                  full(g_q), full(g_kv), full(w_uq_p), full(w_k_p), full(w_v_p), row(LANE, 0), row(LANE, 0), row(LANE, 0)],
        out_specs=[pl.BlockSpec((NH, TM, LANE), lambda i: (0, i, 0))] * 3,
        out_shape=[hs, hs, hs],
        compiler_params=_params(("parallel",)),
    )(proj, proj, proj, g_q, g_kv, w_uq_p, w_k_p, w_v_p, rc, rs1, rs2)


LOG2E = 1.4426950408889634
QK_SCALE2 = LOG2E / math.sqrt(QK_NOPE + QK_ROPE)


HQ = TQ // 2


def _diag_visible(n):
    row = lax.broadcasted_iota(jnp.int32, (n, n), 0)
    col = lax.broadcasted_iota(jnp.int32, (n, n), 1)
    return (col // CHUNK) <= (row // CHUNK)


def _attn_fwd(q, k, vv):
    s = q.shape[1]

    def body(q_ref, k_ref, v_ref, o_ref, lse_ref):
        i = pl.program_id(1)
        qs = (q_ref[0], q_ref[1])

        def tiles(t, carry, diag):
            rows = pl.ds(pl.multiple_of(t * TQ, TQ), TQ)
            sc = [_mm_nt(qs[hh], k_ref[hh, rows, :]) for hh in range(2)]
            if diag:
                sc = [jnp.where(_diag_visible(TQ), s_, -jnp.inf) for s_ in sc]
            m_new = [jnp.maximum(carry[hh][0], jnp.max(sc[hh], axis=-1, keepdims=True)) for hh in range(2)]
            alpha = [jnp.exp2((carry[hh][0] - m_new[hh]) * QK_SCALE2) for hh in range(2)]
            p = [jnp.exp2((sc[hh] - m_new[hh]) * QK_SCALE2).astype(BF16) for hh in range(2)]
            acc = [alpha[hh] * carry[hh][1] + _mm(p[hh], v_ref[hh, rows, :]) for hh in range(2)]
            return (m_new[0], acc[0]), (m_new[1], acc[1])

        init = (jnp.full((TQ, 1), -jnp.inf, F32), jnp.zeros((TQ, LANE), F32))
        carry = lax.fori_loop(0, i, lambda t, c: tiles(t, c, False), (init, init))
        carry = tiles(i, carry, True)
        lane = lax.broadcasted_iota(jnp.int32, (TQ, LANE), 1)
        out = jnp.zeros((TQ, LANE), F32)
        for hh in range(2):
            m, acc = carry[hh]
            l = jnp.sum(jnp.where(lane == ONES_LANE[hh], acc, 0.0), axis=-1, keepdims=True)
            out = out + jnp.where((lane < V_DIM) == (hh == 0), acc, 0.0) / l
            lse_ref[hh] = jnp.broadcast_to(m * QK_SCALE2 + jnp.log(l) * LOG2E, (TQ, LANE))
        o_ref[...] = out

    return pl.pallas_call(
        body, name="attn_fwd", grid=(NH // 2, s // TQ),
        in_specs=[pl.BlockSpec((2, TQ, LANE), lambda p, i: (p, i, 0)), pl.BlockSpec((2, s, LANE), lambda p, i: (p, 0, 0)),
                  pl.BlockSpec((2, s, LANE), lambda p, i: (p, 0, 0))],
        out_specs=[pl.BlockSpec((TQ, LANE), lambda p, i: (i, p)), pl.BlockSpec((2, TQ, LANE), lambda p, i: (p, i, 0))],
        out_shape=[jax.ShapeDtypeStruct((s, NH * V_DIM), F32), jax.ShapeDtypeStruct((NH, s, LANE), F32)],
        compiler_params=_params(("parallel", "parallel")),
    )(q, k, vv)


def _lower_bound(lbl):
    a0, a1 = lbl[0:1, :], lbl[1:2, :]
    mx = jnp.maximum(a0, a1)
    e0, e1 = jnp.exp(a0 - mx), jnp.exp(a1 - mx)
    return e0 / (e0 + e1)


def _chunk_cumsum(v, reverse=False):
    pos = lax.broadcasted_iota(jnp.int32, v.shape, 0) % HG_BLOCK
    s = 1
    while s < HG_BLOCK:
        if reverse:
            v = v + jnp.where(pos < HG_BLOCK - s, pltpu.roll(v, TH - s, 0), 0.0)
        else:
            v = v + jnp.where(pos >= s, pltpu.roll(v, s, 0), 0.0)
        s *= 2
    return v


def _hgrn_gates(hq, hf, lb):
    sig = _sigmoid(hf)
    f = lb + (1.0 - lb) * sig
    g = jnp.log(f)
    kk = 1.0 - f
    r = lax.broadcasted_iota(jnp.int32, (TH, TH), 0)
    c = lax.broadcasted_iota(jnp.int32, (TH, TH), 1)
    tri = ((r // HG_BLOCK) == (c // HG_BLOCK)) & (r >= c)
    cum = _chunk_cumsum(g)
    nch = TH // HG_BLOCK
    total = _chunks(cum)[:, HG_BLOCK - 1:HG_BLOCK, :]
    lastb = jnp.broadcast_to(total, (nch, HG_BLOCK, hf.shape[-1])).reshape(hf.shape)
    e, ei, ee = jnp.exp(cum), jnp.exp(-cum), jnp.exp(lastb - cum)
    return dict(sig=sig, f=f, kk=kk, tri=tri, cum=cum, total=total, decay=jnp.exp(total), e=e, ei=ei, ee=ee,
                qd=hq * e, ki=kk * ei, ke=kk * ee)


def _chunks(v):
    return v.reshape(TH // HG_BLOCK, HG_BLOCK, v.shape[-1])


def _bmm_nt(a, b):
    return lax.dot_general(a, b, (((2,), (2,)), ((0,), (0,))), preferred_element_type=F32)


def _bmm_nn(a, b):
    return lax.dot_general(a, b, (((2,), (1,)), ((0,), (0,))), preferred_element_type=F32)


def _bmm_tn(a, b):
    return lax.dot_general(a, b, (((1,), (1,)), ((0,), (0,))), preferred_element_type=F32)


def _pair_masks():
    lane = lax.broadcasted_iota(jnp.int32, (TH, LANE), 1)
    kr = lax.broadcasted_iota(jnp.int32, (LANE, LANE), 0)
    kc = lax.broadcasted_iota(jnp.int32, (LANE, LANE), 1)
    return lane < 64, (kr // 64) == (kc // 64)


def _hgrn_fwd(proj, lbl):
    s = proj.shape[0]
    nch = TH // HG_BLOCK

    def body(hq_ref, hf_ref, hi_ref, lbl_ref, o_ref, st_ref, st):
        @pl.when(pl.program_id(1) == 0)
        def _():
            st[...] = jnp.zeros_like(st)

        m0, bd = _pair_masks()
        gt = _hgrn_gates(hq_ref[...], hf_ref[...], _lower_bound(lbl_ref[...]))
        v_b, qd, qd_b = hi_ref[...].astype(BF16), gt["qd"], gt["qd"].astype(BF16)
        ki_b, ke_b = gt["ki"].astype(BF16), gt["ke"].astype(BF16)
        pairs = [slice(u * LANE, (u + 1) * LANE) for u in range(HG_PAIRS)]
        heads = [(lanes, m0 if hh == 0 else jnp.logical_not(m0)) for lanes in pairs for hh in range(2)]
        a_b = [jnp.where(gt["tri"], _mm_nt(jnp.where(mh, qd[:, lanes], 0.0).astype(BF16), ki_b[:, lanes]), 0.0).astype(BF16)
               for lanes, mh in heads]
        intra = [jnp.where(m0, _mm(a_b[2 * u], v_b[:, lanes]), _mm(a_b[2 * u + 1], v_b[:, lanes])) for u, lanes in enumerate(pairs)]
        upd = [_bmm_tn(_chunks(v_b[:, lanes]), _chunks(ke_b[:, lanes])) for lanes in pairs]
        entering = []
        for u, lanes in enumerate(pairs):
            cur, states = st[u], []
            for n in range(nch):
                states.append(cur)
                cur = gt["decay"][n][:, lanes] * cur + jnp.where(bd, upd[u][n], 0.0)
            st[u] = cur
            entering.append(jnp.stack(states))
            st_ref[u] = entering[u]
        for u, lanes in enumerate(pairs):
            o_ref[:, lanes] = intra[u] + _bmm_nt(_chunks(qd_b[:, lanes]), entering[u].astype(BF16)).reshape(TH, LANE)

    wide = HG_PAIRS * LANE
    col = lambda base: pl.BlockSpec((TH, wide), lambda p, i: (i, base // wide + p))
    return pl.pallas_call(
        body, name="hgrn_fwd", grid=(NH // 2 // HG_PAIRS, s // TH),
        in_specs=[col(P_HQ), col(P_HF), col(P_HI), pl.BlockSpec((2, wide), lambda p, i: (0, p))],
        out_specs=[pl.BlockSpec((TH, wide), lambda p, i: (i, p)),
                   pl.BlockSpec((HG_PAIRS, nch, LANE, LANE), lambda p, i: (p, i, 0, 0))],
        out_shape=[jax.ShapeDtypeStruct((s, 512), F32), jax.ShapeDtypeStruct((NH // 2, s // HG_BLOCK, LANE, LANE), F32)],
        scratch_shapes=[pltpu.VMEM((HG_PAIRS, LANE, LANE), F32)],
        compiler_params=_params(("parallel", "arbitrary")),
    )(proj, proj, proj, lbl)


def _group_sum(v):
    low = lax.broadcasted_iota(jnp.int32, (v.shape[0], LANE), 1) < V_DIM
    blocks = []
    for b in range(v.shape[1] // LANE):
        blk = v[:, b * LANE:(b + 1) * LANE]
        s_low = jnp.sum(jnp.where(low, blk, 0.0), axis=-1, keepdims=True)
        s_high = jnp.sum(jnp.where(low, 0.0, blk), axis=-1, keepdims=True)
        blocks.append(jnp.where(low, s_low, s_high))
    return jnp.concatenate(blocks, axis=1)


def _dsilu(z, sg):
    return sg * (1.0 + z * (1.0 - sg))


def _mid(proj, attn, o_raw, x, tgt, g_hg, b_gate, g_post, wa, wb, w_out):
    s = x.shape[0]

    def body(attn_ref, ga_ref, o_ref, gb_ref, mg_ref, x_ref, t_ref, ghg_ref, bg_ref, gp_ref, wa_ref, wb_ref, wo_ref,
             loss_ref, dout_ref, dattn_ref, dga_ref, dor_ref, dgb_ref, dmg_ref, dwo_out, dwa_out, dwb_out, dgp_ref, dbg_ref, dghg_ref,
             dwo_ref, dwa_ref, dwb_ref):
        @pl.when(pl.program_id(0) == 0)
        def _():
            for rf in (loss_ref, dwo_ref, dwa_ref, dwb_ref, dgp_ref, dbg_ref, dghg_ref):
                rf[...] = jnp.zeros_like(rf)

        attn, za, orw, zb = attn_ref[...], ga_ref[...], o_ref[...], gb_ref[...]
        ghg, gp = ghg_ref[...], gp_ref[...]
        sga, sgb = _sigmoid(za), _sigmoid(zb)
        sa, sb = za * sga, zb * sgb
        ga = attn * sa
        rh = lax.rsqrt(_group_sum(orw * orw) * (1.0 / V_DIM) + EPS)
        on = (orw * rh) * ghg
        gb = on * sb
        ga_b, gb_b = ga.astype(BF16), gb.astype(BF16)
        blocks = [slice(t * (D // N_CHIPS), (t + 1) * (D // N_CHIPS)) for t in range(N_CHIPS)]
        ya = jnp.concatenate([_mm(ga_b, wa_ref[t]) for t in range(N_CHIPS)], axis=1)
        yb = jnp.concatenate([_mm(gb_b, wb_ref[t]) for t in range(N_CHIPS)], axis=1)
        gates = _sigmoid(mg_ref[...] + bg_ref[...])
        g0, g1 = gates[:, :D], gates[:, D:]
        m_b = (g0 * ya + g1 * yb).astype(BF16)
        y = _mm(m_b, wo_ref[...])
        ry = lax.rsqrt(jnp.mean(y * y, axis=-1, keepdims=True) + EPS)
        out = x_ref[...] + (y * ry) * gp
        err = out - t_ref[...]
        loss_ref[...] += 0.5 * jnp.sum(jnp.mean(err * err, axis=-1, keepdims=True), axis=0, keepdims=True)
        dout = err * (1.0 / D)
        dout_ref[...] = dout
        dgp_ref[...] += jnp.sum(dout * (y * ry), axis=0, keepdims=True)
        dgy = dout * gp
        dy = ry * dgy - y * (ry * ry * ry) * jnp.mean(y * dgy, axis=-1, keepdims=True)
        dy_b = dy.astype(BF16)
        dm = _mm_nt(dy_b, wo_ref[...])
        dya_b, dyb_b = (dm * g0).astype(BF16), (dm * g1).astype(BF16)
        dga = sum(_mm_nt(dya_b[:, cols], wa_ref[t]) for t, cols in enumerate(blocks))
        dgb = sum(_mm_nt(dyb_b[:, cols], wb_ref[t]) for t, cols in enumerate(blocks))
        dwo_ref[...] += _mm_tn(m_b, dy_b)
        for t, cols in enumerate(blocks):
            dwa_ref[t] += _mm_tn(ga_b, dya_b[:, cols])
            dwb_ref[t] += _mm_tn(gb_b, dyb_b[:, cols])
        dg0, dg1 = dm * ya, dm * yb
        dmg = jnp.concatenate([dg0 * g0 * (1.0 - g0), dg1 * g1 * (1.0 - g1)], axis=1)
        dmg_ref[...] = dmg.astype(BF16)
        dbg_ref[...] += jnp.sum(dmg, axis=0, keepdims=True)
        dattn_ref[...] = dga * sa
        dga_ref[...] = (dga * attn * _dsilu(za, sga)).astype(BF16)
        dgb_ref[...] = (dgb * on * _dsilu(zb, sgb)).astype(BF16)
        don = dgb * sb
        dghg_ref[...] += jnp.sum(don * (orw * rh), axis=0, keepdims=True)
        dgo = don * ghg
        dor_ref[...] = rh * dgo - orw * (rh * rh * rh) * (_group_sum(orw * dgo) * (1.0 / V_DIM))

        @pl.when(pl.program_id(0) == pl.num_programs(0) - 1)
        def _():
            for out, rf in ((dwo_out, dwo_ref), (dwa_out, dwa_ref), (dwb_out, dwb_ref)):
                out[...] = rf[...].astype(BF16)

    row = lambda w, j=0: pl.BlockSpec((TM_MID, w), lambda i: (i, j))
    full = lambda a: pl.BlockSpec(a.shape, lambda i: (0,) * a.ndim)
    acc = lambda shape: pl.BlockSpec(shape, lambda i: (0,) * len(shape))
    slabs = (N_CHIPS, 512, D // N_CHIPS)
    sds = jax.ShapeDtypeStruct
    return pl.pallas_call(
        body, name="mid", grid=(s // TM_MID,),
        in_specs=[row(512), row(512, P_GA // 512), row(512), row(512, P_GB // 512), row(2048, P_MERGE // 2048), row(D), row(D),
                  full(g_hg), full(b_gate), full(g_post), full(wa), full(wb), full(w_out)],
        out_specs=[acc((1, 1)), row(D), row(512), row(512), row(512), row(512), row(2048),
                   acc((D, D)), acc(slabs), acc(slabs), acc((1, D)), acc((1, 2048)), acc((1, 512))],
        out_shape=[sds((1, 1), F32), sds((s, D), F32), sds((s, 512), F32), sds((s, 512), BF16), sds((s, 512), F32), sds((s, 512), BF16),
                   sds((s, 2048), BF16), sds((D, D), BF16), sds(slabs, BF16), sds(slabs, BF16), sds((1, D), F32),
                   sds((1, 2048), F32), sds((1, 512), F32)],
        scratch_shapes=[pltpu.VMEM((D, D), F32), pltpu.VMEM(slabs, F32), pltpu.VMEM(slabs, F32)],
        compiler_params=_params(("arbitrary",)),
    )(attn, proj, o_raw, proj, proj, x, tgt, g_hg, b_gate, g_post, wa, wb, w_out)


def _attn_bwd(q, k, vv, attn, dattn, lse, token):
    s = q.shape[1]
    nt = s // TQ
    scale = 1.0 / math.sqrt(QK_NOPE + QK_ROPE)

    def body(q_ref, k_ref, v_ref, o_ref, do_ref, lse_ref, token_ref, dq_ref, dk_ref, dv_ref, do_s, delta_s):
        j = pl.program_id(1)

        @pl.when(j == 0)
        def _():
            dq_ref[...] = jnp.zeros_like(dq_ref)
            lane = lax.broadcasted_iota(jnp.int32, (TQ, LANE), 1)

            @pl.loop(0, nt)
            def _(i):
                rows = pl.ds(pl.multiple_of(i * TQ, TQ), TQ)
                do, o = do_ref[rows, :], o_ref[rows, :]
                for hh in range(2):
                    doh = jnp.where((lane < 64) if hh == 0 else (lane >= 64), do, 0.0)
                    do_s[hh, rows, :] = doh.astype(BF16)
                    delta_s[hh, rows, :] = jnp.broadcast_to(jnp.sum(doh * o, axis=-1, keepdims=True), (TQ, LANE))

        kjs, vjs = (k_ref[0], k_ref[1]), (v_ref[0], v_ref[1])

        def tile(hh, start, size, kj, vj, diag):
            rows = pl.ds(pl.multiple_of(start, size), size)
            wide = lambda a: jnp.concatenate([a] * (kj.shape[0] // LANE), axis=1)
            qi, do_b = q_ref[hh, rows, :], do_s[hh, rows, :]
            sc, dp = _mm_nt(qi, kj), _mm_nt(do_b, vj)
            p = jnp.exp2(sc * QK_SCALE2 - wide(lse_ref[hh, rows, :]))
            if diag:
                p = jnp.where(_diag_visible(size), p, 0.0)
            ds_b = (p * (dp - wide(delta_s[hh, rows, :]))).astype(BF16)
            dv, dk = _mm_tn(do_b, p.astype(BF16)), _mm_tn(qi, ds_b)
            dq_ref[hh, rows, :] += _mm(ds_b, kj)
            return dk, dv

        def step(i, carry):
            new = [tile(hh, i * TQ, TQ, kjs[hh], vjs[hh], False) for hh in range(2)]
            return tuple((carry[hh][0] + new[hh][0], carry[hh][1] + new[hh][1]) for hh in range(2))

        def diagonal(hh):
            k0, k1, v0, v1 = kjs[hh][:HQ], kjs[hh][HQ:], vjs[hh][:HQ], vjs[hh][HQ:]
            a = tile(hh, j * TQ, HQ, k0, v0, True)
            b = tile(hh, j * TQ + HQ, HQ, k0, v0, False)
            c = tile(hh, j * TQ + HQ, HQ, k1, v1, True)
            return jnp.concatenate([a[0] + b[0], c[0]], axis=1), jnp.concatenate([a[1] + b[1], c[1]], axis=1)

        carry = lax.fori_loop(j + 1, nt, step, (diagonal(0), diagonal(1)))
        for hh in range(2):
            dk_ref[hh] = carry[hh][0].T * scale
            dv_ref[hh] = carry[hh][1].T

        @pl.when(j == nt - 1)
        def _():
            dq_ref[...] = dq_ref[...] * scale

    whole = pl.BlockSpec((2, s, LANE), lambda p, j: (p, 0, 0))
    tile_spec = pl.BlockSpec((2, TQ, LANE), lambda p, j: (p, j, 0))
    cols = pl.BlockSpec((s, LANE), lambda p, j: (0, p))
    hs = jax.ShapeDtypeStruct((NH, s, LANE), F32)
    return pl.pallas_call(
        body, name="attn_bwd", grid=(NH // 2, nt),
        in_specs=[whole, tile_spec, tile_spec, cols, cols, whole, pl.BlockSpec((8, LANE), lambda p, j: (0, 0))],
        out_specs=[whole, tile_spec, tile_spec],
        out_shape=[hs, hs, hs],
        scratch_shapes=[pltpu.VMEM((2, s, LANE), BF16), pltpu.VMEM((2, s, LANE), F32)],
        compiler_params=_params(("parallel", "arbitrary")),
    )(q, k, vv, attn, dattn, lse, token)


def _hgrn_bwd(proj, lbl, states, do_raw):
    s = proj.shape[0]
    nt = s // TH
    nch = TH // HG_BLOCK

    def body(hq_ref, hf_ref, hi_ref, lbl_ref, st_ref, do_ref, dh_ref, dlbl_ref, dst, dlb):
        step = pl.program_id(1)

        @pl.when(step == 0)
        def _():
            dst[...] = jnp.zeros_like(dst)
            dlb[...] = jnp.zeros_like(dlb)

        m0, bd = _pair_masks()
        lb = _lower_bound(lbl_ref[...])
        gt = _hgrn_gates(hq_ref[...], hf_ref[...], lb)
        do = do_ref[...]
        qd, ki, ke = gt["qd"], gt["ki"], gt["ke"]
        v_b, do_b = hi_ref[...].astype(BF16), do.astype(BF16)
        qd_b, ki_b, ke_b = qd.astype(BF16), ki.astype(BF16), ke.astype(BF16)
        pairs = [slice(u * LANE, (u + 1) * LANE) for u in range(HG_PAIRS)]
        heads = [(lanes, m0 if hh == 0 else jnp.logical_not(m0)) for lanes in pairs for hh in range(2)]
        a_b = [jnp.where(gt["tri"], _mm_nt(jnp.where(mh, qd[:, lanes], 0.0).astype(BF16), ki_b[:, lanes]), 0.0).astype(BF16)
               for lanes, mh in heads]
        doh_b = [jnp.where(mh, do[:, lanes], 0.0).astype(BF16) for lanes, mh in heads]
        da_b = [jnp.where(gt["tri"], _mm_nt(d, v_b[:, lanes]), 0.0).astype(BF16) for d, (lanes, _) in zip(doh_b, heads)]
        dv_p, dqd_p, dki_p = [], [], []
        for u, lanes in enumerate(pairs):
            e, o = 2 * u, 2 * u + 1
            dv_p.append(_mm_tn(a_b[e], doh_b[e]) + _mm_tn(a_b[o], doh_b[o]))
            dqd_p.append(jnp.where(m0, _mm(da_b[e], ki_b[:, lanes]), _mm(da_b[o], ki_b[:, lanes])))
            dki_p.append(jnp.where(m0, _mm_tn(da_b[e], qd_b[:, lanes]), _mm_tn(da_b[o], qd_b[:, lanes])))
        fed = [_bmm_tn(_chunks(do_b[:, lanes]), _chunks(qd_b[:, lanes])) for lanes in pairs]
        leaving = []
        for u, lanes in enumerate(pairs):
            ds, left = dst[u], [None] * nch
            for n in reversed(range(nch)):
                left[n] = ds
                ds = gt["decay"][n][:, lanes] * ds + jnp.where(bd, fed[u][n], 0.0)
            dst[u] = ds
            leaving.append(jnp.stack(left))
        dke_p, dlast_p = [], []
        for u, lanes in enumerate(pairs):
            entering, leaving_b = st_ref[u], leaving[u].astype(BF16)
            dke3 = _bmm_nn(_chunks(v_b[:, lanes]), leaving_b)
            dv_p[u] = dv_p[u] + _bmm_nt(_chunks(ke_b[:, lanes]), leaving_b).reshape(TH, LANE)
            dqd_p[u] = dqd_p[u] + _bmm_nn(_chunks(do_b[:, lanes]), entering.astype(BF16)).reshape(TH, LANE)
            dke_p.append(dke3.reshape(TH, LANE))
            dlast_p.append(jnp.sum(dke3 * _chunks(ke[:, lanes]), axis=1, keepdims=True)
                           + jnp.sum(leaving[u] * entering, axis=1, keepdims=True) * gt["decay"][:, :, lanes])
        cat = lambda parts: jnp.concatenate(parts, axis=-1)
        dv, dqd, dki, dke, dlast = cat(dv_p), cat(dqd_p), cat(dki_p), cat(dke_p), cat(dlast_p)
        dk = dki * gt["ei"] + dke * gt["ee"]
        dcum = dqd * qd - dki * ki - dke * ke
        dg = _chunk_cumsum(dcum, reverse=True) + jnp.broadcast_to(dlast, (nch, HG_BLOCK, dlast.shape[-1])).reshape(dcum.shape)
        sig = gt["sig"]
        df = dg / gt["f"] - dk
        dlb[...] += jnp.sum(df * (1.0 - sig), axis=0, keepdims=True)
        dh_ref[0] = (dqd * gt["e"]).astype(BF16)
        dh_ref[1] = ((df * (1.0 - lb)) * sig * (1.0 - sig)).astype(BF16)
        dh_ref[2] = dv.astype(BF16)

        @pl.when(step == nt - 1)
        def _():
            lb = _lower_bound(lbl_ref[...])
            da0 = dlb[...] * lb * (1.0 - lb)
            dlbl_ref[...] = jnp.concatenate([da0, -da0], axis=0)

    wide = HG_PAIRS * LANE
    col = lambda base: pl.BlockSpec((TH, wide), lambda p, i: (nt - 1 - i, base // wide + p))
    tile = pl.BlockSpec((TH, wide), lambda p, i: (nt - 1 - i, p))
    sds = jax.ShapeDtypeStruct
    return pl.pallas_call(
        body, name="hgrn_bwd", grid=(NH // 2 // HG_PAIRS, nt),
        in_specs=[col(P_HQ), col(P_HF), col(P_HI), pl.BlockSpec((2, wide), lambda p, i: (0, p)),
                  pl.BlockSpec((HG_PAIRS, nch, LANE, LANE), lambda p, i: (p, nt - 1 - i, 0, 0)), tile],
        out_specs=[pl.BlockSpec((3, TH, wide), lambda p, i: (0, nt - 1 - i, p)), pl.BlockSpec((2, wide), lambda p, i: (0, p))],
        out_shape=[sds((3, s, 512), BF16), sds((2, 512), F32)],
        scratch_shapes=[pltpu.VMEM((HG_PAIRS, LANE, LANE), F32), pltpu.VMEM((1, wide), F32)],
        compiler_params=_params(("parallel", "arbitrary")),
    )(proj, proj, proj, lbl, states, do_raw)


def _norm_rows_bwd(v, r, g, dn):
    dgv = dn * g
    return r * dgv - v * (r * r * r) * jnp.mean(v * dgv, axis=-1, keepdims=True)


def _qkv_bwd(proj, dq, dk, dvv, g_q, g_kv, w_uq_p, w_k_p, w_v_p, rc, rs1, rs2):
    s = proj.shape[0]
    head_q = QK_NOPE + QK_ROPE

    def body(cq_ref, ckv_ref, dq_ref, dk_ref, dv_ref, gq_ref, gkv_ref, wq_ref, wk_ref, wv_ref, c_ref, s1_ref, s2_ref,
             dcq_ref, dckv_ref, dkpe_ref, dwq_out, dwkv_out, dgq_ref, dgkv_ref, dwq_ref, dwk_ref, dwv_ref):
        @pl.when(pl.program_id(0) == 0)
        def _():
            for rf in (dwq_ref, dwk_ref, dwv_ref, dgq_ref, dgkv_ref):
                rf[...] = jnp.zeros_like(rf)

        c, s1, s2 = c_ref[...], s1_ref[...], s2_ref[...]
        cq, ckv = cq_ref[...], ckv_ref[...]
        gq, gkv = gq_ref[...], gkv_ref[...]
        cqn, rq = _norm_rows(cq, gq)
        ckvn, rkv = _norm_rows(ckv, gkv)
        cqn_b, ckvn_b = cqn.astype(BF16), ckvn.astype(BF16)
        dqf = jnp.concatenate([_rope_t(dq_ref[h], c, s1, s2) for h in range(NH)], axis=1).astype(BF16)
        dkf = jnp.concatenate([dk_ref[h] for h in range(NH)], axis=1).astype(BF16)
        dvf = jnp.concatenate([dv_ref[h] for h in range(NH)], axis=1).astype(BF16)
        dkpe = dk_ref[0]
        for h in range(1, NH):
            dkpe = dkpe + dk_ref[h]
        lane = lax.broadcasted_iota(jnp.int32, (TM, LANE), 1)
        dkpe = jnp.where((lane >= QK_NOPE) & (lane < QK_NOPE + QK_ROPE), dkpe, 0.0)
        dkpe_ref[...] = _rope_t(dkpe, c, s1, s2).astype(BF16)
        dcqn = _mm_nt(dqf, wq_ref[...])
        pair = lambda a, t: a[:, t * 2 * LANE:(t + 1) * 2 * LANE]
        dckvn = sum(_mm_nt(pair(dkf, t), wk_ref[t]) + _mm_nt(pair(dvf, t), wv_ref[t]) for t in range(N_CHIPS))
        dwq_ref[...] += _mm_tn(cqn_b, dqf)
        dwk_ref[...] += _mm_tn(ckvn_b, dkf)
        dwv_ref[...] += _mm_tn(ckvn_b, dvf)
        dgq_ref[...] += jnp.sum(dcqn * (cq * rq), axis=0, keepdims=True)
        dgkv_ref[...] += jnp.sum(dckvn * (ckv * rkv), axis=0, keepdims=True)
        dcq_ref[...] = _norm_rows_bwd(cq, rq, gq, dcqn).astype(BF16)
        dckv_ref[...] = _norm_rows_bwd(ckv, rkv, gkv, dckvn).astype(BF16)

        @pl.when(pl.program_id(0) == pl.num_programs(0) - 1)
        def _():
            blk = lambda ref, h: ref[:, h * LANE:(h + 1) * LANE]
            lane = lax.broadcasted_iota(jnp.int32, (Q_LORA, LANE), 1)
            for j in range(NH * head_q // LANE):
                h0, w0 = divmod(j * LANE, head_q)
                first = blk(dwq_ref, h0) if w0 == 0 else pltpu.roll(blk(dwq_ref, h0), LANE - w0, 1)
                second = pltpu.roll(blk(dwq_ref, h0 + 1), head_q - w0, 1)
                dwq_out[:, j * LANE:(j + 1) * LANE] = jnp.where(lane < head_q - w0, first, second).astype(BF16)
            lane = lax.broadcasted_iota(jnp.int32, (KV_LORA, LANE), 1)
            for h in range(NH):
                vals = blk(dwv_ref, h) if h % 2 else pltpu.roll(blk(dwv_ref, h), V_DIM, 1)
                both = jnp.where(lane < QK_NOPE, blk(dwk_ref, h), vals).astype(BF16)
                dwkv_out[h // 2, :, (h % 2) * LANE:(h % 2 + 1) * LANE] = both

    row = lambda w, j=0: pl.BlockSpec((TM, w), lambda i: (i, j))
    full = lambda a: pl.BlockSpec(a.shape, lambda i: (0,) * a.ndim)
    acc = lambda *shape: pl.BlockSpec(shape, lambda i: (0,) * len(shape))
    heads = pl.BlockSpec((NH, TM, LANE), lambda i: (0, i, 0))
    sds = jax.ShapeDtypeStruct
    return pl.pallas_call(
        body, name="qkv_bwd", grid=(s // TM,),
        in_specs=[row(Q_LORA, P_CQ // Q_LORA), row(KV_LORA, P_CKV // KV_LORA), heads, heads, heads,
                  full(g_q), full(g_kv), full(w_uq_p), full(w_k_p), full(w_v_p), row(LANE), row(LANE), row(LANE)],
        out_specs=[row(Q_LORA), row(KV_LORA), row(LANE), acc(Q_LORA, NH * head_q), acc(NH // 2, KV_LORA, 2 * LANE),
                   acc(1, Q_LORA), acc(1, KV_LORA)],
        out_shape=[sds((s, Q_LORA), BF16), sds((s, KV_LORA), BF16), sds((s, LANE), BF16), sds((Q_LORA, NH * head_q), BF16),
                   sds((NH // 2, KV_LORA, 2 * LANE), BF16), sds((1, Q_LORA), F32), sds((1, KV_LORA), F32)],
        scratch_shapes=[pltpu.VMEM((Q_LORA, D), F32), pltpu.VMEM((KV_LORA, D), F32), pltpu.VMEM((KV_LORA, D), F32)],
        compiler_params=_params(("arbitrary",)),
    )(proj, proj, dq, dk, dvv, g_q, g_kv, w_uq_p, w_k_p, w_v_p, rc, rs1, rs2)


def _front_bwd(x, dout, dmg, dga, dh3, dgb, dcq, dckv, dkpe, g_pre, w_in_t, w_kpe, token):
    s = x.shape[0]

    def body(x_ref, do_ref, dmg_ref, dga_ref, dh3_ref, dgb_ref, dcq_ref, dckv_ref, dkpe_ref, g_ref, w_ref, k_ref, token_ref,
             gx_ref, dg_ref):
        @pl.when(pl.program_id(0) == 0)
        def _():
            dg_ref[...] = jnp.zeros_like(dg_ref)

        xv, g = x_ref[...], g_ref[...]
        _, r = _norm_rows(xv, g)
        pieces = ((dmg_ref[...], O_MERGE), (dga_ref[...], O_GA), (dh3_ref[0], O_HQ), (dh3_ref[1], O_HF), (dh3_ref[2], O_HI),
                  (dgb_ref[...], O_GB), (dcq_ref[...], O_CQ), (dckv_ref[...], O_CKV))
        dh = _mm(dkpe_ref[...], k_ref[...])
        for piece, off in pieces:
            dh = dh + _mm(piece, w_ref[off:off + piece.shape[1], :])
        dg_ref[...] += jnp.sum(dh * (xv * r), axis=0, keepdims=True)
        gx_ref[...] = do_ref[...] + _norm_rows_bwd(xv, r, g, dh)

    row = lambda w: pl.BlockSpec((TM, w), lambda i: (i, 0))
    full = lambda a: pl.BlockSpec(a.shape, lambda i: (0,) * a.ndim)
    sds = jax.ShapeDtypeStruct
    return pl.pallas_call(
        body, name="front_bwd", grid=(s // TM,),
        in_specs=[row(D), row(D), row(2048), row(512), pl.BlockSpec((3, TM, 512), lambda i: (0, i, 0)), row(512), row(Q_LORA),
                  row(KV_LORA), row(LANE), full(g_pre), full(w_in_t), full(w_kpe), pl.BlockSpec(memory_space=pl.ANY)],
        out_specs=[row(D), pl.BlockSpec((1, D), lambda i: (0, 0))],
        out_shape=[sds((s, D), F32), sds((1, D), F32)],
        compiler_params=_params(("arbitrary",)),
    )(x, dout, dmg, dga, dh3, dgb, dcq, dckv, dkpe, g_pre, w_in_t, w_kpe, token)


TK_GRAD = 1024


def _win_grad(h, pieces, name):
    s = h.shape[0]
    n = len(pieces)

    def body(h_ref, *refs):
        d_refs, o_refs, sums = refs[:n], refs[n:2 * n], refs[2 * n:]

        @pl.when(pl.program_id(0) == 0)
        def _():
            for s_ref in sums:
                s_ref[...] = jnp.zeros_like(s_ref)

        hv = h_ref[...]
        for d_ref, s_ref in zip(d_refs, sums):
            if len(d_ref.shape) == 3:
                for k in range(d_ref.shape[0]):
                    s_ref[k] += _mm_tn(d_ref[k], hv)
            else:
                s_ref[...] += _mm_tn(d_ref[...], hv)

        @pl.when(pl.program_id(0) == pl.num_programs(0) - 1)
        def _():
            for o_ref, s_ref in zip(o_refs, sums):
                o_ref[...] = s_ref[...].astype(BF16)

    def in_spec(p):
        if p.ndim == 3:
            return pl.BlockSpec((p.shape[0], TK_GRAD, p.shape[2]), lambda kk: (0, kk, 0))
        return pl.BlockSpec((TK_GRAD, p.shape[1]), lambda kk: (kk, 0))

    out_shapes = [(p.shape[0], p.shape[2], D) if p.ndim == 3 else (p.shape[1], D) for p in pieces]
    return pl.pallas_call(
        body, name=name, grid=(s // TK_GRAD,),
        in_specs=[pl.BlockSpec((TK_GRAD, D), lambda kk: (kk, 0))] + [in_spec(p) for p in pieces],
        out_specs=[pl.BlockSpec(sh, lambda kk, nd=len(sh): (0,) * nd) for sh in out_shapes],
        out_shape=[jax.ShapeDtypeStruct(sh, BF16) for sh in out_shapes],
        scratch_shapes=[pltpu.VMEM(sh, F32) for sh in out_shapes],
        compiler_params=_params(("arbitrary",)),
    )(h, *pieces)


QKV_ROWS = Q_LORA + KV_LORA + QK_ROPE


def _win_grad_qkv(h, dcq, dckv, dkpe):
    s = h.shape[0]
    half = QKV_ROWS // 2

    def body(h_ref, cq_ref, ckv_ref, kpe_ref, o_ref):
        @pl.when(pl.program_id(0) == 0)
        def _():
            o_ref[...] = jnp.zeros_like(o_ref)

        hv = h_ref[...]
        g_cq = _mm_tn(cq_ref[...], hv)
        o_ref[0] += g_cq[:half]
        o_ref[1, 0:Q_LORA - half] += g_cq[half:]
        o_ref[1, Q_LORA - half:Q_LORA + KV_LORA - half] += _mm_tn(ckv_ref[...], hv)
        o_ref[1, Q_LORA + KV_LORA - half:] += _mm_tn(kpe_ref[...], hv)[QK_NOPE:QK_NOPE + QK_ROPE]

    rows = lambda a: pl.BlockSpec((TK_GRAD, a.shape[1]), lambda kk: (kk, 0))
    return pl.pallas_call(
        body, name="win_grad_qkv", grid=(s // TK_GRAD,), in_specs=[rows(h), rows(dcq), rows(dckv), rows(dkpe)],
        out_specs=pl.BlockSpec((2, half, D), lambda kk: (0, 0, 0)), out_shape=jax.ShapeDtypeStruct((2, half, D), F32),
        compiler_params=_params(("arbitrary",)),
    )(h, dcq, dckv, dkpe)


def _pad_wuq(w_uq):
    rows = w_uq.shape[0]
    w = w_uq.reshape(rows, NH, QK_NOPE + QK_ROPE)
    return jnp.pad(w, ((0, 0), (0, 0), (0, LANE - QK_NOPE - QK_ROPE))).reshape(rows, NH * LANE)


def _pad_wukv(w_ukv):
    heads = w_ukv.shape[1] // (QK_NOPE + V_DIM)
    w = w_ukv.reshape(KV_LORA, heads, QK_NOPE + V_DIM)
    w_k = jnp.pad(w[:, :, :QK_NOPE], ((0, 0), (0, 0), (0, LANE - QK_NOPE))).reshape(KV_LORA, heads * LANE)
    wv = w[:, :, QK_NOPE:].reshape(KV_LORA, heads // 2, 2, 1, V_DIM)
    eye = jnp.eye(2, dtype=w.dtype).reshape(1, 1, 2, 2, 1)
    return w_k, (wv * eye).reshape(KV_LORA, heads * LANE)


def _local_step(x, tgt, g_pre, w_in_t, b_gate, g_q, g_kv, lb_logits, g_hgrn, g_post, weights, exchange=None):
    s = x.shape[0]
    w_kpe = _kpe_block(w_in_t)
    rc, rs1, rs2 = _rope_tables(s)
    g_hg = jnp.tile(g_hgrn, (1, NH))

    proj, h = _front_fwd(x, g_pre, w_in_t, w_kpe, weights.tokens)
    w_uq_p, w_k_p, w_v_p = weights.qkv(h)
    q, k, vv = _qkv_fwd(proj, g_q, g_kv, w_uq_p, w_k_p, w_v_p, rc, rs1, rs2)
    attn, lse = _attn_fwd(q, k, vv)
    o_raw, states = _hgrn_fwd(proj, lb_logits)
    wa, wb, w_out = weights.mid(o_raw)
    (loss, dout, dattn, dga, dor, dgb, dmg, d_wout, d_wa, d_wb, d_gpost, d_bgate, d_ghg) = _mid(
        proj, attn, o_raw, x, tgt, g_hg, b_gate, g_post, wa, wb, w_out)
    w_mg, w_ga, w_gb = _win_grad(h, [dmg, dga, dgb], "win_grad_mid")
    dh3, d_lbl = _hgrn_bwd(proj, lb_logits, states, dor)
    (w_h3,) = _win_grad(h, [dh3], "win_grad_hgrn")
    d_win_rest = jnp.concatenate([w_ga, w_h3[0], w_h3[1], w_h3[2], w_gb, w_mg], axis=0)
    early = dict(w_in_rest=d_win_rest, w_branch_a=d_wa, w_branch_b=d_wb, w_out=d_wout)
    token = exchange.start_early(early) if exchange else jnp.zeros((8, LANE), F32)
    dq, dk, dvv = _attn_bwd(q, k, vv, attn, dattn, lse, token)
    dcq, dckv, dkpe, d_wuq, d_wukv, d_gq, d_gkv = _qkv_bwd(proj, dq, dk, dvv, g_q, g_kv, w_uq_p, w_k_p, w_v_p, rc, rs1, rs2)
    late = dict(w_in_qkv=_win_grad_qkv(h, dcq, dckv, dkpe), w_uq=d_wuq, w_ukv=d_wukv)
    token = exchange.start_late(late) if exchange else jnp.zeros((8, LANE), F32)
    grad_x, d_gpre = _front_bwd(x, dout, dmg, dga, dh3, dgb, dcq, dckv, dkpe, g_pre, w_in_t, w_kpe, token)
    vec_grads = dict(g_pre=d_gpre, b_gate=d_bgate, g_q=d_gq, g_kv=d_gkv, lb_logits=d_lbl, g_hgrn=d_ghg, g_post=d_gpost)
    return loss, grad_x, dict(early, **late), vec_grads


SHARD_SHAPES = (("w_in", (1416, 1024)), ("w_uq", (192, 768)), ("w_ukv", (256, 256)), ("w_branch_a", (512, 256)),
                ("w_branch_b", (512, 256)), ("w_out", (256, 1024)))
BIG = tuple(n for n, _ in SHARD_SHAPES)
ROW_SHARDED = ("w_in", "w_uq", "w_out")
N_CHIPS = 4
W_IN_FORWARD_CUT = 704


def _to_block(name, a):
    return a[0].T if name == "w_in" else a[0]


def _from_block(name, a):
    return a.T[None] if name == "w_in" else a[None]
VEC_ROWS = (("g_pre", 0, 1024), ("b_gate", 1, 2048), ("g_q", 2, 768), ("g_kv", 3, 256), ("g_hgrn", 6, 64), ("g_post", 7, 1024))
VEC_LB_ROW = 4
VEC_SHAPE = (8, 2048)


def _split_by_chip(name, g):
    a, b = dict(SHARD_SHAPES)[name]
    return g.reshape(N_CHIPS, a, b) if name in ROW_SHARDED else g.reshape(a, N_CHIPS, b).transpose(1, 0, 2)


def _join_chips(name, w):
    a, b = dict(SHARD_SHAPES)[name]
    return w.reshape(N_CHIPS * a, b) if name in ROW_SHARDED else w.transpose(1, 0, 2).reshape(a, N_CHIPS * b)


MESH = pl.DeviceIdType.MESH
HBM = pl.BlockSpec(memory_space=pltpu.HBM)


def _mesh_place():
    x, y, c = lax.axis_index("x"), lax.axis_index("y"), lax.axis_index("c")
    return x, y, c, 2 * x + y, [(1 - x, y), (x, 1 - y), (1 - x, 1 - y)]


def _remote(src, dst, send_sems, recv_sems, k, to):
    return pltpu.make_async_remote_copy(src_ref=src, dst_ref=dst, send_sem=send_sems.at[k], recv_sem=recv_sems.at[k],
                                        device_id=to, device_id_type=MESH)


def _gather_w_in(shard):
    a, b = shard.shape
    cut = W_IN_FORWARD_CUT

    def body(src, out, ici_send, ici_recv, d2d_send, d2d_recv, local_sem):
        x, y, c = lax.axis_index("x"), lax.axis_index("y"), lax.axis_index("c")
        me, xn, yn, dg = 2 * x + y, 2 * (1 - x) + y, 2 * x + (1 - y), 2 * (1 - x) + (1 - y)
        to_x, to_y, sibling = (1 - x, y, c), (x, 1 - y, c), (x, y, 1 - c)
        first, rest = pl.ds(0, cut), pl.ds(cut, a - cut)

        whole = lambda ref, which: ref.at[:, pl.ds(pl.multiple_of(which * (b // 2), b // 2), b // 2)]
        own = pltpu.make_async_copy(src, out.at[me], local_sem)
        own.start()
        sends = [_remote(whole(src, c), whole(out.at[me], c), ici_send, ici_recv, 0, to_x),
                 _remote(whole(src, c), whole(out.at[me], c), ici_send, ici_recv, 1, to_y)]
        for cp in sends:
            cp.start()

        def landed(slot, rows, k, d2d_k, src_dev):
            piece = whole(out.at[slot], c) if rows is None else out.at[slot].at[rows, pl.ds(pl.multiple_of(c * (b // 2), b // 2), b // 2)]
            _remote(piece, piece, ici_send, ici_recv, k, src_dev).wait_recv()
            cp = _remote(piece, piece, d2d_send, d2d_recv, d2d_k, sibling)
            cp.start()
            sends.append(cp)
            return piece

        def pass_on(slot, rows, k, to):
            piece = out.at[slot].at[rows, pl.ds(pl.multiple_of(c * (b // 2), b // 2), b // 2)]
            cp = _remote(piece, piece, ici_send, ici_recv, k, to)
            cp.start()
            sends.append(cp)

        landed(xn, None, 0, 0, to_x)
        pass_on(xn, first, 2, to_y)
        landed(yn, None, 1, 1, to_y)
        pass_on(yn, rest, 3, to_x)
        landed(dg, first, 2, 2, to_y)
        landed(dg, rest, 3, 3, to_x)
        other = pl.ds(pl.multiple_of((1 - c) * (b // 2), b // 2), b // 2)
        for d2d_k, (slot, rows) in enumerate(((xn, None), (yn, None), (dg, first), (dg, rest))):
            piece = out.at[slot].at[:, other] if rows is None else out.at[slot].at[rows, other]
            _remote(piece, piece, d2d_send, d2d_recv, d2d_k, sibling).wait_recv()
        for cp in sends:
            cp.wait_send()
        own.wait()

    sems = pltpu.SemaphoreType.DMA((4,))
    return pl.pallas_call(
        body, name="gather_w_in", in_specs=[HBM], out_specs=HBM,
        out_shape=jax.ShapeDtypeStruct((N_CHIPS, a, b), shard.dtype),
        scratch_shapes=[sems, sems, sems, sems, pltpu.SemaphoreType.DMA],
        compiler_params=pltpu.CompilerParams(has_side_effects=True),
    )(shard)


def _sibling_exchange(srcs, name, after=None, halves=False):
    n = len(srcs)
    extra = [] if after is None else [after]

    def body(*refs):
        src_refs, outs = refs[:n], refs[n + len(extra):2 * n + len(extra)]
        send_sems, recv_sems = refs[2 * n + len(extra):]
        c = lax.axis_index("c")
        sibling = (lax.axis_index("x"), lax.axis_index("y"), 1 - c)
        if halves:
            src_refs = [ref.at[1 - c] for ref in src_refs]
        copies = [_remote(src_refs[k], outs[k], send_sems, recv_sems, k, sibling) for k in range(n)]
        for cp in copies:
            cp.start()
        for cp in copies:
            cp.wait()

    sems = pltpu.SemaphoreType.DMA((n,))
    return pl.pallas_call(
        body, name=name, in_specs=[HBM] * n + [pl.BlockSpec(memory_space=pl.ANY)] * len(extra), out_specs=[HBM] * n,
        out_shape=[jax.ShapeDtypeStruct(s.shape[1:] if halves else s.shape, s.dtype) for s in srcs],
        scratch_shapes=[sems, sems],
        compiler_params=pltpu.CompilerParams(has_side_effects=True),
    )(*srcs, *extra)


SEM = pl.BlockSpec(memory_space=pltpu.SEMAPHORE)
DATAFLOW = pltpu.SideEffectType.DATAFLOW_SIDE_EFFECTING


def _exchange_copies(srcs, to_first, src_refs, land_refs, send_sems, recv_sems):
    x, y, c, me, chips = _mesh_place()
    n = len(srcs)
    sends, recvs = [], []
    for k in range(n):
        if k in to_first:
            base = 3 * n + 4 * to_first.index(k)
            sends.append((me != 0, pltpu.make_async_remote_copy(
                src_ref=src_refs[k], dst_ref=land_refs[k].at[me], send_sem=send_sems.at[base], recv_sem=recv_sems.at[base + me],
                device_id=(0, 0, c), device_id_type=MESH)))
            for s in range(1, N_CHIPS):
                recvs.append((me == 0, pltpu.make_async_remote_copy(
                    src_ref=src_refs[k], dst_ref=land_refs[k].at[s], send_sem=send_sems.at[base], recv_sem=recv_sems.at[base + s],
                    device_id=(s // 2, s % 2, c), device_id_type=MESH)))
        else:
            slab = (lambda t, k=k: src_refs[k]) if srcs[k].ndim == 2 else (lambda t, k=k: src_refs[k].at[t])
            for j, (px, py) in enumerate(chips):
                sends.append((None, _remote(slab(2 * px + py), land_refs[k].at[me], send_sems, recv_sems, 3 * k + j, (px, py, c))))
                recvs.append((None, _remote(slab(me), land_refs[k].at[2 * px + py], send_sems, recv_sems, 3 * k + j, (px, py, c))))
    return sends, recvs


def _when(pred, fn):
    if pred is None:
        fn()
    else:
        pl.when(pred)(fn)


def _exchange_start(srcs, to_first, name, after=None):
    n = len(srcs)
    n_sems = 3 * n + 4 * len(to_first)
    lands = [lax.empty((N_CHIPS,) + s.shape[-2:], s.dtype) for s in srcs]
    extra = [] if after is None else [after]

    def body(*refs):
        src_refs, land_refs = refs[:n], refs[n:2 * n]
        send_sems, recv_sems, token = refs[2 * n + len(extra)], refs[2 * n + len(extra) + 1], refs[-1]
        sends, _ = _exchange_copies(srcs, to_first, src_refs, land_refs, send_sems, recv_sems)
        for pred, cp in sends:
            _when(pred, cp.start)
        token[...] = jnp.zeros_like(token)

    hbm = lambda a: pltpu.HBM(a.shape, a.dtype)
    res = pl.pallas_call(
        body, name=name,
        out_shape=[pltpu.SemaphoreType.DMA((n_sems,)), pltpu.SemaphoreType.DMA((n_sems,))] + [hbm(a) for a in srcs + lands]
        + [jax.ShapeDtypeStruct((8, LANE), F32)],
        in_specs=[HBM] * (2 * n) + [pl.BlockSpec(memory_space=pl.ANY)] * len(extra),
        out_specs=[SEM, SEM] + [HBM] * (2 * n) + [pl.BlockSpec(memory_space=pltpu.VMEM)],
        input_output_aliases={i: 2 + i for i in range(2 * n)},
        compiler_params=pltpu.CompilerParams(has_side_effects=DATAFLOW),
    )(*[pltpu.with_memory_space_constraint(a, pltpu.HBM) for a in srcs + lands], *extra)
    return res[:-1], res[-1]


def _exchange_wait(srcs, to_first, started, after, name):
    n = len(srcs)
    send_sems, recv_sems, thru = started[0], started[1], started[2:]

    def body(*refs):
        src_refs, land_refs, send_ref, recv_ref = refs[:n], refs[n:2 * n], refs[2 * n], refs[2 * n + 1]
        sends, recvs = _exchange_copies(srcs, to_first, src_refs, land_refs, send_ref, recv_ref)
        for pred, cp in sends:
            _when(pred, cp.wait_send)
        for pred, cp in recvs:
            _when(pred, cp.wait_recv)

    res = pl.pallas_call(
        body, name=name, out_shape=[pltpu.HBM(a.shape, a.dtype) for a in thru],
        in_specs=[HBM] * (2 * n) + [SEM, SEM, pl.BlockSpec(memory_space=pl.ANY)], out_specs=[HBM] * (2 * n),
        input_output_aliases={i: i for i in range(2 * n)},
        compiler_params=pltpu.CompilerParams(has_side_effects=DATAFLOW),
    )(*thru, send_sems, recv_sems, after)
    return res[:n], res[n:]


ROW_TILE = 256
COL_TILE = 256


def _block_tiling(a, b):
    if a <= ROW_TILE or a % ROW_TILE == 0:
        ta = min(a, ROW_TILE)
        return a // ta, (ta, b), lambda i: (i, 0)
    return b // COL_TILE, (a, COL_TILE), lambda i: (0, i)


def _sum_landed(land, own, name, first_land=None, first_own=None):
    _, a, b = land.shape
    steps, tile, at = _block_tiling(a, b)
    extra = first_land is not None

    def body(me_ref, *refs):
        p_ref, own_ref, o_ref = refs[0], refs[1], refs[-1]
        me = me_ref[0]
        own = own_ref[...].astype(F32)
        slot = lambda t: jnp.where(me == t, own, p_ref[t].astype(F32))
        o_ref[...] = ((slot(0) + slot(1)) + slot(2)) + slot(3)
        if extra:
            fp_ref, fo_ref = refs[2], refs[3]
            r = fo_ref.shape[0]

            @pl.when(me == 0)
            def _():
                f = lambda t: fp_ref[t].astype(F32)
                rows = pl.ds(pl.multiple_of(lax.axis_index("c") * r, 8), r)
                o_ref[rows, :] += ((fo_ref[...].astype(F32) + f(1)) + f(2)) + f(3)

    own_spec = pl.BlockSpec(tile, lambda i, me: at(i)) if own.ndim == 2 else pl.BlockSpec((None,) + tile, lambda i, me: (me[0],) + at(i))
    in_specs = [pl.BlockSpec((N_CHIPS,) + tile, lambda i, me: (0,) + at(i)), own_spec]
    args = [land, own]
    if extra:
        r = first_own.shape[0]
        assert tile[0] == a, "the extra rows need whole columns in a step"
        in_specs += [pl.BlockSpec((N_CHIPS, r, tile[1]), lambda i, me: (0,) + at(i)), pl.BlockSpec((r, tile[1]), lambda i, me: at(i))]
        args += [first_land, first_own]
    me = jnp.reshape(2 * lax.axis_index("x") + lax.axis_index("y"), (1,)).astype(jnp.int32)
    return pl.pallas_call(
        body, name=name,
        grid_spec=pltpu.PrefetchScalarGridSpec(num_scalar_prefetch=1, grid=(steps,), in_specs=in_specs,
                                               out_specs=pl.BlockSpec(tile, lambda i, me: at(i))),
        out_shape=jax.ShapeDtypeStruct((a, b), F32), compiler_params=_params(("parallel",)),
    )(me, *args)


def _sum_landed_small(lands, owns, name):
    n = len(lands)

    def body(*refs):
        me = 2 * lax.axis_index("x") + lax.axis_index("y")
        for p_ref, own_ref, o_ref in zip(refs[:n], refs[n:2 * n], refs[2 * n:]):
            own = own_ref[me].astype(F32)
            slot = lambda t: jnp.where(me == t, own, p_ref[t].astype(F32))
            o_ref[...] = ((slot(0) + slot(1)) + slot(2)) + slot(3)

    return pl.pallas_call(body, name=name, out_shape=[jax.ShapeDtypeStruct(o.shape[1:], F32) for o in owns],
                          compiler_params=_params(()))(*lands, *owns)


def _adamw_small(mine, theirs, states, name):
    n = len(mine)

    def body(*refs):
        ins, outs = refs[:5 * n], refs[5 * n:]
        for k in range(n):
            a_ref, b_ref, w_ref, m_ref, v_ref = ins[5 * k:5 * k + 5]
            g = a_ref[...] + b_ref[...]
            outs[4 * k][...] = g
            outs[4 * k + 1][...], outs[4 * k + 2][...], outs[4 * k + 3][...] = _adamw_math(g, w_ref[...], m_ref[...], v_ref[...])

    args = [t for k in range(n) for t in (mine[k], theirs[k], *states[k])]
    res = pl.pallas_call(body, name=name, out_shape=[jax.ShapeDtypeStruct(mine[k].shape, F32) for k in range(n) for _ in range(4)],
                         compiler_params=_params(()))(*args)
    return [tuple(res[4 * k:4 * k + 4]) for k in range(n)]


def _add_cast(halves, b, name):
    def body(a_ref, b_ref, o_ref):
        o_ref[...] = (a_ref[lax.axis_index("c")] + b_ref[...]).astype(BF16)

    return pl.pallas_call(body, name=name, out_shape=jax.ShapeDtypeStruct(b.shape, BF16),
                          compiler_params=_params(()))(halves, b)


class _LaterWeights:
    MID = ("w_branch_a", "w_branch_b", "w_out")

    def __init__(self, blocks, after):
        self.qkv_blocks = [_pad_wuq(blocks["w_uq"]), *_pad_wukv(blocks["w_ukv"])]
        self.mid_blocks = [blocks[n] for n in self.MID]
        self.qkv_started, t1 = _exchange_start(self.qkv_blocks, (), "weights_qkv_start", after)
        self.mid_started, t2 = _exchange_start(self.mid_blocks, (), "weights_mid_start", after)
        self.tokens = [t1, t2]

    @staticmethod
    def _whole(blocks, joined, started, after, name):
        _, landed = _exchange_wait(blocks, (), started, after, name)
        me = 2 * lax.axis_index("x") + lax.axis_index("y")
        out = []
        for block, land, join in zip(blocks, landed, joined):
            w = lax.dynamic_update_index_in_dim(land, block, me, 0)
            out.append(w.reshape(N_CHIPS * block.shape[0], block.shape[1]) if join else w)
        return out

    def qkv(self, after):
        return self._whole(self.qkv_blocks, (True, False, False), self.qkv_started, after, "weights_qkv_wait")

    def mid(self, after):
        return self._whole(self.mid_blocks, (False, False, True), self.mid_started, after, "weights_mid_wait")


class _GradExchange:
    EARLY = ("w_in", "w_branch_a", "w_branch_b", "w_out")
    LATE = ("w_uq", "w_ukv")

    def __init__(self, state):
        self.state = state
        self.outs = {}

    def start_early(self, g):
        full = jnp.concatenate([jnp.zeros((QKV_ROWS, D), g["w_in_rest"].dtype), g["w_in_rest"]], axis=0)
        g = dict(g, w_in=full)
        self.early = [(g[n] if g[n].ndim == 3 else _split_by_chip(n, g[n])).astype(BF16) for n in self.EARLY]
        self.early_started, token = _exchange_start(self.early, (), "grads_early_start")
        return token

    def start_late(self, g):
        self.early, self.early_landed = _exchange_wait(self.early, (), self.early_started, g["w_uq"], "grads_early_wait")
        (theirs,) = _sibling_exchange([g["w_in_qkv"]], "sibling_qkv_rows", halves=True)
        self.late = [_split_by_chip("w_uq", g["w_uq"]), g["w_ukv"], _add_cast(g["w_in_qkv"], theirs, "add_qkv_rows")]
        self.late_started, token = _exchange_start(self.late, (2,), "grads_late_start")
        names = self.EARLY[1:]
        mine = _sum_landed_small(self.early_landed[1:], self.early[1:], "sum_early")
        theirs = _sibling_exchange(mine, "sibling_early", after=token)
        for n, out in zip(names, _adamw_small(mine, theirs, [self.state[n] for n in names], "adamw_early")):
            self.outs[n] = out
        return self.outs[names[-1]][0]

    def finish(self, after):
        late, late_landed = _exchange_wait(self.late, (2,), self.late_started, after, "grads_late_wait")
        sums = {"w_in": _sum_landed(self.early_landed[0], self.early[0], "sum_w_in", first_land=late_landed[2], first_own=late[2])}
        small = _sum_landed_small(late_landed[:2], late[:2], "sum_late")
        sums.update(zip(self.LATE, small))
        return sums


def _adamw_math(g, w, m, v):
    nm = ADAM_B1 * m + (1.0 - ADAM_B1) * g
    nv = ADAM_B2 * v + (1.0 - ADAM_B2) * (g * g)
    m_hat = nm / (1.0 - ADAM_B1 ** ADAM_STEP)
    v_hat = nv / (1.0 - ADAM_B2 ** ADAM_STEP)
    return -ADAM_LR * (m_hat / (jnp.sqrt(v_hat) + ADAM_EPS) + ADAM_WD * w), nm, nv


def _adamw(p_mine, p_sibling, w, m, v, name):
    a, b = p_mine.shape
    steps, tile, at = _block_tiling(a, b)

    def body(a_ref, b_ref, w_ref, m_ref, v_ref, g_ref, d_ref, nm_ref, nv_ref):
        g = a_ref[...] + b_ref[...]
        g_ref[...] = g
        d_ref[...], nm_ref[...], nv_ref[...] = _adamw_math(g, w_ref[...], m_ref[...], v_ref[...])

    spec = pl.BlockSpec(tile, at)
    sds = jax.ShapeDtypeStruct((a, b), F32)
    return pl.pallas_call(
        body, name=name, grid=(steps,), in_specs=[spec] * 5, out_specs=[spec] * 4, out_shape=[sds] * 4,
        compiler_params=_params(("parallel",)),
    )(p_mine, p_sibling, w, m, v)


LOSS_AT = (2, 1024)


def _vec_pack(vg, loss):
    names = [n for n, _, _ in VEC_ROWS]

    def body(*refs):
        o_ref = refs[-1]
        lb_ref, loss_ref = refs[len(names)], refs[len(names) + 1]
        o_ref[...] = jnp.zeros_like(o_ref)
        o_ref[LOSS_AT[0]:LOSS_AT[0] + 1, LOSS_AT[1]:LOSS_AT[1] + LANE] = jnp.broadcast_to(loss_ref[...], (1, LANE))
        for (name, row, size), ref in zip(VEC_ROWS, refs):
            if name == "g_hgrn":
                r = lax.broadcasted_iota(jnp.int32, (NH * V_DIM, LANE), 0)
                c = lax.broadcasted_iota(jnp.int32, (NH * V_DIM, LANE), 1)
                fold = ((r % V_DIM) == c).astype(F32)
                o_ref[row:row + 1, 0:LANE] = jnp.dot(ref[...], fold, precision=HIGHEST, preferred_element_type=F32)
            else:
                o_ref[row:row + 1, 0:size] = ref[...]
        o_ref[VEC_LB_ROW:VEC_LB_ROW + 2, 0:512] = lb_ref[...]

    return pl.pallas_call(body, name="vec_pack", out_shape=jax.ShapeDtypeStruct(VEC_SHAPE, F32))(
        *[vg[n] for n in names], vg["lb_logits"], loss)


def _adamw_vec(p_mine, p_sibling, w, m, v):
    names = [n for n, _, _ in VEC_ROWS] + ["lb_logits"]
    k = len(names)

    def body(a_ref, b_ref, *refs):
        ins, outs = refs[:3 * k], refs[3 * k:]
        at = (slice(LOSS_AT[0], LOSS_AT[0] + 1), slice(LOSS_AT[1], LOSS_AT[1] + LANE))
        outs[-1][...] = a_ref[at] + b_ref[at]
        for i, name in enumerate(names):
            if name == "lb_logits":
                rows, cols = slice(VEC_LB_ROW, VEC_LB_ROW + 2), slice(0, 512)
            else:
                _, row, size = VEC_ROWS[i]
                rows, cols = slice(row, row + 1), slice(0, size)
            g = a_ref[rows, cols] + b_ref[rows, cols]
            d, nm, nv = _adamw_math(g, ins[i][...], ins[k + i][...], ins[2 * k + i][...])
            for o_ref, val in zip(outs[4 * i:4 * i + 4], (g, d, nm, nv)):
                o_ref[...] = val

    shapes = [jax.ShapeDtypeStruct(w[n].shape, F32) for n in names for _ in range(4)] + [jax.ShapeDtypeStruct((1, LANE), F32)]
    res = pl.pallas_call(body, name="adamw_vec", out_shape=shapes)(
        p_mine, p_sibling, *[w[n] for n in names], *[m[n] for n in names], *[v[n] for n in names])
    return [{n: res[4 * i + j] for i, n in enumerate(names)} for j in range(4)], res[-1]


WEIGHTS = ("g_pre", "w_in", "b_gate", "g_q", "w_uq", "g_kv", "w_ukv", "lb_logits", "g_hgrn", "w_branch_a", "w_branch_b", "w_out", "g_post")


def kernel(x, g_pre, w_in, b_gate, g_q, w_uq, g_kv, w_ukv, lb_logits, g_hgrn, w_branch_a, w_branch_b, w_out, g_post, loss_target, m_g_pre, m_w_in, m_b_gate, m_g_q, m_w_uq, m_g_kv, m_w_ukv, m_lb_logits, m_g_hgrn, m_w_branch_a, m_w_branch_b, m_w_out, m_g_post, v_g_pre, v_w_in, v_b_gate, v_g_q, v_w_uq, v_g_kv, v_w_ukv, v_lb_logits, v_g_hgrn, v_w_branch_a, v_w_branch_b, v_w_out, v_g_post):
    w = dict(g_pre=g_pre, w_in=w_in, b_gate=b_gate, g_q=g_q, w_uq=w_uq, g_kv=g_kv, w_ukv=w_ukv, lb_logits=lb_logits, g_hgrn=g_hgrn,
             w_branch_a=w_branch_a, w_branch_b=w_branch_b, w_out=w_out, g_post=g_post)
    m = dict(g_pre=m_g_pre, w_in=m_w_in, b_gate=m_b_gate, g_q=m_g_q, w_uq=m_w_uq, g_kv=m_g_kv, w_ukv=m_w_ukv, lb_logits=m_lb_logits,
             g_hgrn=m_g_hgrn, w_branch_a=m_w_branch_a, w_branch_b=m_w_branch_b, w_out=m_w_out, g_post=m_g_post)
    v = dict(g_pre=v_g_pre, w_in=v_w_in, b_gate=v_b_gate, g_q=v_g_q, w_uq=v_w_uq, g_kv=v_g_kv, w_ukv=v_w_ukv, lb_logits=v_lb_logits,
             g_hgrn=v_g_hgrn, w_branch_a=v_w_branch_a, w_branch_b=v_w_branch_b, w_out=v_w_out, g_post=v_g_post)
    blocks = {n: _to_block(n, w[n]).astype(BF16) for n in BIG}
    w_in_all = _gather_w_in(blocks["w_in"])
    weights = _LaterWeights(blocks, w_in_all)
    state = {n: [_to_block(n, t[n]) for t in (w, m, v)] for n in BIG}
    exchange = _GradExchange(state)
    loss, grad_x, _, vec_grads = _local_step(
        x[0], loss_target[0], g_pre, _join_chips("w_in", w_in_all), b_gate, g_q, g_kv, lb_logits, g_hgrn, g_post, weights, exchange)
    vec = _vec_pack(vec_grads, loss)
    vec_started, token = _exchange_start([vec], (), "vec_start")
    sums = exchange.finish(token)
    rest = tuple(sums)
    (vec,), (vec_landed,) = _exchange_wait([vec], (), vec_started, sums[rest[-1]], "vec_wait")
    mine = [sums[n] for n in rest] + [_sum_landed(vec_landed, vec, "sum_vec")]
    theirs = _sibling_exchange(mine, "sibling_grads")
    done = dict(exchange.outs)
    done["w_in"] = _adamw(mine[0], theirs[0], *state["w_in"], "adamw_w_in")
    small = _adamw_small(mine[1:-1], theirs[1:-1], [state[n] for n in rest[1:]], "adamw_late")
    done.update(zip(rest[1:], small))
    outs = [{}, {}, {}, {}]
    for n in BIG:
        for o, val in zip(outs, done[n]):
            o[n] = _from_block(n, val)
    vec_outs, total = _adamw_vec(mine[-1], theirs[-1], w, m, v)
    for o, vals in zip(outs, vec_outs):
        o.update(vals)
    return (total[0, 0], grad_x[None], *[o[n] for o in outs for n in WEIGHTS])
```

```python
import math

import numpy as np
import jax
import jax.numpy as jnp
from jax import lax
from jax.experimental import pallas as pl
from jax.experimental.pallas import tpu as pltpu

F32 = jnp.float32
BF16 = jnp.bfloat16
HIGHEST = lax.Precision.HIGHEST

D = 1024
NH = 8
QK_NOPE, QK_ROPE, V_DIM = 64, 32, 64
Q_LORA, KV_LORA = 768, 256
CHUNK = 64
HG_BLOCK = 32
EPS = 1e-6
LANE = 128
P_MERGE, P_GA, P_HQ, P_HF, P_HI, P_GB, P_CQ, P_CKV, P_KPE = 0, 2048, 2560, 3072, 3584, 4096, 4608, 5376, 5632
D_P = 5760
O_CQ, O_CKV, O_KPE, O_GA, O_HQ, O_HF, O_HI, O_GB, O_MERGE = 0, 768, 1024, 1056, 1568, 2080, 2592, 3104, 3616

TM = 512
TM_MID = 256
TQ = 1024
ONES_LANE = (LANE - 1, 0)
TH = 256
HG_PAIRS = 4
VMEM_LIMIT = 56 * 1024 * 1024

ADAM_LR, ADAM_B1, ADAM_B2, ADAM_EPS, ADAM_WD, ADAM_STEP = 0.001, 0.9, 0.999, 1e-08, 0.01, 10

NT_DIMS = (((1,), (1,)), ((), ()))
TN_DIMS = (((0,), (0,)), ((), ()))


def _params(sem):
    return pltpu.CompilerParams(dimension_semantics=sem, vmem_limit_bytes=VMEM_LIMIT)


def _mm(a, b):
    return jnp.dot(a, b, preferred_element_type=F32)


def _mm_nt(a, b):
    return lax.dot_general(a, b, NT_DIMS, preferred_element_type=F32)


def _mm_tn(a, b):
    return lax.dot_general(a, b, TN_DIMS, preferred_element_type=F32)


def _sigmoid(z):
    return jax.nn.sigmoid(z)


def _rope(v, c, s1, s2):
    return v * c + pltpu.roll(v, 112, 1) * s1 + pltpu.roll(v, 16, 1) * s2


def _rope_t(dy, c, s1, s2):
    return dy * c + pltpu.roll(dy * s1, 16, 1) + pltpu.roll(dy * s2, 112, 1)


def _rope_tables(s):
    f32 = np.float32
    inv = f32(10000.0) ** (-np.arange(0, QK_ROPE, 2, dtype=f32) / f32(QK_ROPE))
    ang = np.arange(s, dtype=f32)[:, None] * inv[None, :]
    cos, sin = np.cos(ang).astype(f32), np.sin(ang).astype(f32)
    z64, z32, o64, o32 = np.zeros((s, 64), f32), np.zeros((s, 32), f32), np.ones((s, 64), f32), np.ones((s, 32), f32)
    z16 = np.zeros((s, 16), f32)
    c = np.concatenate([o64, cos, cos, o32], axis=1)
    s1 = np.concatenate([z64, -sin, z16, z32], axis=1)
    s2 = np.concatenate([z64, z16, sin, z32], axis=1)
    return jnp.asarray(c), jnp.asarray(s1), jnp.asarray(s2)


W_IN_RUNS = ((O_MERGE, 2048, P_MERGE), (O_GA, O_MERGE - O_GA, P_GA), (O_CQ, O_KPE - O_CQ, P_CQ))


def _kpe_block(w_in_t):
    z = lambda n: jnp.zeros((n, w_in_t.shape[1]), w_in_t.dtype)
    return jnp.concatenate([z(64), w_in_t[O_KPE:O_KPE + QK_ROPE], z(32)], axis=0)


def _front_fwd(x, g_pre, w_in_t, w_kpe, tokens=()):
    s = x.shape[0]
    tokens = list(tokens)

    def body(x_ref, g_ref, w_ref, k_ref, *refs):
        o_ref, h_ref = refs[len(tokens):]
        xv = x_ref[...]
        r = lax.rsqrt(jnp.mean(xv * xv, axis=-1, keepdims=True) + EPS)
        h = ((xv * r) * g_ref[...]).astype(BF16)
        h_ref[...] = h
        for row, rows, col in W_IN_RUNS:
            o_ref[:, col:col + rows] = _mm_nt(h, w_ref[row:row + rows, :])
        o_ref[:, P_KPE:P_KPE + LANE] = _mm_nt(h, k_ref[...])

    full = lambda a: pl.BlockSpec(a.shape, lambda i: (0,) * a.ndim)
    return pl.pallas_call(
        body, name="front_fwd", grid=(s // TM,),
        in_specs=[pl.BlockSpec((TM, D), lambda i: (i, 0)), pl.BlockSpec((1, D), lambda i: (0, 0)), full(w_in_t), full(w_kpe)]
        + [pl.BlockSpec((8, LANE), lambda i: (0, 0))] * len(tokens),
        out_specs=[pl.BlockSpec((TM, D_P), lambda i: (i, 0)), pl.BlockSpec((TM, D), lambda i: (i, 0))],
        out_shape=[jax.ShapeDtypeStruct((s, D_P), F32), jax.ShapeDtypeStruct((s, D), BF16)],
        compiler_params=_params(("parallel",)),
    )(x, g_pre, w_in_t, w_kpe, *tokens)


def _norm_rows(v, g):
    r = lax.rsqrt(jnp.mean(v * v, axis=-1, keepdims=True) + EPS)
    return (v * r) * g, r


def _qkv_fwd(proj, g_q, g_kv, w_uq_p, w_k_p, w_v_p, rc, rs1, rs2):
    s = proj.shape[0]

    def body(cq_ref, ckv_ref, kpe_ref, gq_ref, gkv_ref, wq_ref, wk_ref, wv_ref, c_ref, s1_ref, s2_ref, q_ref, k_ref, v_ref):
        c, s1, s2 = c_ref[...], s1_ref[...], s2_ref[...]
        cqn, _ = _norm_rows(cq_ref[...], gq_ref[...])
        ckvn, _ = _norm_rows(ckv_ref[...], gkv_ref[...])
        ckvn = ckvn.astype(BF16)
        qf = _mm(cqn.astype(BF16), wq_ref[...])
        kf = jnp.concatenate([_mm(ckvn, wk_ref[t]) for t in range(N_CHIPS)], axis=1)
        vf = jnp.concatenate([_mm(ckvn, wv_ref[t]) for t in range(N_CHIPS)], axis=1)
        kpe = _rope(kpe_ref[...], c, s1, s2)
        lane = lax.broadcasted_iota(jnp.int32, (TM, LANE), 1)
        for h in range(NH):
            blk = slice(h * LANE, (h + 1) * LANE)
            q_ref[h] = _rope(qf[:, blk], c, s1, s2).astype(BF16)
            k_ref[h] = (kf[:, blk] + kpe).astype(BF16)
            v_ref[h] = jnp.where(lane == ONES_LANE[h % 2], 1.0, vf[:, blk]).astype(BF16)

    row = lambda w, j: pl.BlockSpec((TM, w), lambda i: (i, j))
    full = lambda a: pl.BlockSpec(a.shape, lambda i: (0,) * a.ndim)
    hs = jax.ShapeDtypeStruct((NH, s, LANE), BF16)
    return pl.pallas_call(
        body, name="qkv_fwd", grid=(s // TM,),
        in_specs=[row(Q_LORA, P_CQ // Q_LORA), row(KV_LORA, P_CKV // KV_LORA), row(LANE, P_KPE // LANE),
                  full(g_q), full(g_kv), full(w_uq_p), full(w_k_p), full(w_v_p), row(LANE, 0), row(LANE, 0), row(LANE, 0)],
        out_specs=[pl.BlockSpec((NH, TM, LANE), lambda i: (0, i, 0))] * 3,
        out_shape=[hs, hs, hs],
        compiler_params=_params(("parallel",)),
    )(proj, proj, proj, g_q, g_kv, w_uq_p, w_k_p, w_v_p, rc, rs1, rs2)


LOG2E = 1.4426950408889634
QK_SCALE2 = LOG2E / math.sqrt(QK_NOPE + QK_ROPE)


HQ = TQ // 2


def _diag_visible(n):
    row = lax.broadcasted_iota(jnp.int32, (n, n), 0)
    col = lax.broadcasted_iota(jnp.int32, (n, n), 1)
    return (col // CHUNK) <= (row // CHUNK)


def _attn_fwd(q, k, vv):
    s = q.shape[1]

    def body(q_ref, k_ref, v_ref, o_ref, lse_ref):
        i = pl.program_id(1)
        qs = (q_ref[0], q_ref[1])

        def tiles(t, carry, diag):
            rows = pl.ds(pl.multiple_of(t * TQ, TQ), TQ)
            sc = [_mm_nt(qs[hh], k_ref[hh, rows, :]) for hh in range(2)]
            if diag:
                sc = [jnp.where(_diag_visible(TQ), s_, -jnp.inf) for s_ in sc]
            m_new = [jnp.maximum(carry[hh][0], jnp.max(sc[hh], axis=-1, keepdims=True)) for hh in range(2)]
            alpha = [jnp.exp2((carry[hh][0] - m_new[hh]) * QK_SCALE2) for hh in range(2)]
            p = [jnp.exp2((sc[hh] - m_new[hh]) * QK_SCALE2).astype(BF16) for hh in range(2)]
            acc = [alpha[hh] * carry[hh][1] + _mm(p[hh], v_ref[hh, rows, :]) for hh in range(2)]
            return (m_new[0], acc[0]), (m_new[1], acc[1])

        init = (jnp.full((TQ, 1), -jnp.inf, F32), jnp.zeros((TQ, LANE), F32))
        carry = lax.fori_loop(0, i, lambda t, c: tiles(t, c, False), (init, init))
        carry = tiles(i, carry, True)
        lane = lax.broadcasted_iota(jnp.int32, (TQ, LANE), 1)
        out = jnp.zeros((TQ, LANE), F32)
        for hh in range(2):
            m, acc = carry[hh]
            l = jnp.sum(jnp.where(lane == ONES_LANE[hh], acc, 0.0), axis=-1, keepdims=True)
            out = out + jnp.where((lane < V_DIM) == (hh == 0), acc, 0.0) / l
            lse_ref[hh] = jnp.broadcast_to(m * QK_SCALE2 + jnp.log(l) * LOG2E, (TQ, LANE))
        o_ref[...] = out

    return pl.pallas_call(
        body, name="attn_fwd", grid=(NH // 2, s // TQ),
        in_specs=[pl.BlockSpec((2, TQ, LANE), lambda p, i: (p, i, 0)), pl.BlockSpec((2, s, LANE), lambda p, i: (p, 0, 0)),
                  pl.BlockSpec((2, s, LANE), lambda p, i: (p, 0, 0))],
        out_specs=[pl.BlockSpec((TQ, LANE), lambda p, i: (i, p)), pl.BlockSpec((2, TQ, LANE), lambda p, i: (p, i, 0))],
        out_shape=[jax.ShapeDtypeStruct((s, NH * V_DIM), F32), jax.ShapeDtypeStruct((NH, s, LANE), F32)],
        compiler_params=_params(("parallel", "parallel")),
    )(q, k, vv)


def _lower_bound(lbl):
    a0, a1 = lbl[0:1, :], lbl[1:2, :]
    mx = jnp.maximum(a0, a1)
    e0, e1 = jnp.exp(a0 - mx), jnp.exp(a1 - mx)
    return e0 / (e0 + e1)


def _chunk_cumsum(v, reverse=False):
    pos = lax.broadcasted_iota(jnp.int32, v.shape, 0) % HG_BLOCK
    s = 1
    while s < HG_BLOCK:
        if reverse:
            v = v + jnp.where(pos < HG_BLOCK - s, pltpu.roll(v, TH - s, 0), 0.0)
        else:
            v = v + jnp.where(pos >= s, pltpu.roll(v, s, 0), 0.0)
        s *= 2
    return v


def _hgrn_gates(hq, hf, lb):
    sig = _sigmoid(hf)
    f = lb + (1.0 - lb) * sig
    g = jnp.log(f)
    kk = 1.0 - f
    r = lax.broadcasted_iota(jnp.int32, (TH, TH), 0)
    c = lax.broadcasted_iota(jnp.int32, (TH, TH), 1)
    tri = ((r // HG_BLOCK) == (c // HG_BLOCK)) & (r >= c)
    cum = _chunk_cumsum(g)
    nch = TH // HG_BLOCK
    total = _chunks(cum)[:, HG_BLOCK - 1:HG_BLOCK, :]
    lastb = jnp.broadcast_to(total, (nch, HG_BLOCK, hf.shape[-1])).reshape(hf.shape)
    e, ei, ee = jnp.exp(cum), jnp.exp(-cum), jnp.exp(lastb - cum)
    return dict(sig=sig, f=f, kk=kk, tri=tri, cum=cum, total=total, decay=jnp.exp(total), e=e, ei=ei, ee=ee,
                qd=hq * e, ki=kk * ei, ke=kk * ee)


def _chunks(v):
    return v.reshape(TH // HG_BLOCK, HG_BLOCK, v.shape[-1])


def _bmm_nt(a, b):
    return lax.dot_general(a, b, (((2,), (2,)), ((0,), (0,))), preferred_element_type=F32)


def _bmm_nn(a, b):
    return lax.dot_general(a, b, (((2,), (1,)), ((0,), (0,))), preferred_element_type=F32)


def _bmm_tn(a, b):
    return lax.dot_general(a, b, (((1,), (1,)), ((0,), (0,))), preferred_element_type=F32)


def _pair_masks():
    lane = lax.broadcasted_iota(jnp.int32, (TH, LANE), 1)
    kr = lax.broadcasted_iota(jnp.int32, (LANE, LANE), 0)
    kc = lax.broadcasted_iota(jnp.int32, (LANE, LANE), 1)
    return lane < 64, (kr // 64) == (kc // 64)


def _hgrn_fwd(proj, lbl):
    s = proj.shape[0]
    nch = TH // HG_BLOCK

    def body(hq_ref, hf_ref, hi_ref, lbl_ref, o_ref, st_ref, st):
        @pl.when(pl.program_id(1) == 0)
        def _():
            st[...] = jnp.zeros_like(st)

        m0, bd = _pair_masks()
        gt = _hgrn_gates(hq_ref[...], hf_ref[...], _lower_bound(lbl_ref[...]))
        v_b, qd, qd_b = hi_ref[...].astype(BF16), gt["qd"], gt["qd"].astype(BF16)
        ki_b, ke_b = gt["ki"].astype(BF16), gt["ke"].astype(BF16)
        pairs = [slice(u * LANE, (u + 1) * LANE) for u in range(HG_PAIRS)]
        heads = [(lanes, m0 if hh == 0 else jnp.logical_not(m0)) for lanes in pairs for hh in range(2)]
        a_b = [jnp.where(gt["tri"], _mm_nt(jnp.where(mh, qd[:, lanes], 0.0).astype(BF16), ki_b[:, lanes]), 0.0).astype(BF16)
               for lanes, mh in heads]
        intra = [jnp.where(m0, _mm(a_b[2 * u], v_b[:, lanes]), _mm(a_b[2 * u + 1], v_b[:, lanes])) for u, lanes in enumerate(pairs)]
        upd = [_bmm_tn(_chunks(v_b[:, lanes]), _chunks(ke_b[:, lanes])) for lanes in pairs]
        entering = []
        for u, lanes in enumerate(pairs):
            cur, states = st[u], []
            for n in range(nch):
                states.append(cur)
                cur = gt["decay"][n][:, lanes] * cur + jnp.where(bd, upd[u][n], 0.0)
            st[u] = cur
            entering.append(jnp.stack(states))
            st_ref[u] = entering[u]
        for u, lanes in enumerate(pairs):
            o_ref[:, lanes] = intra[u] + _bmm_nt(_chunks(qd_b[:, lanes]), entering[u].astype(BF16)).reshape(TH, LANE)

    wide = HG_PAIRS * LANE
    col = lambda base: pl.BlockSpec((TH, wide), lambda p, i: (i, base // wide + p))
    return pl.pallas_call(
        body, name="hgrn_fwd", grid=(NH // 2 // HG_PAIRS, s // TH),
        in_specs=[col(P_HQ), col(P_HF), col(P_HI), pl.BlockSpec((2, wide), lambda p, i: (0, p))],
        out_specs=[pl.BlockSpec((TH, wide), lambda p, i: (i, p)),
                   pl.BlockSpec((HG_PAIRS, nch, LANE, LANE), lambda p, i: (p, i, 0, 0))],
        out_shape=[jax.ShapeDtypeStruct((s, 512), F32), jax.ShapeDtypeStruct((NH // 2, s // HG_BLOCK, LANE, LANE), F32)],
        scratch_shapes=[pltpu.VMEM((HG_PAIRS, LANE, LANE), F32)],
        compiler_params=_params(("parallel", "arbitrary")),
    )(proj, proj, proj, lbl)


def _group_sum(v):
    low = lax.broadcasted_iota(jnp.int32, (v.shape[0], LANE), 1) < V_DIM
    blocks = []
    for b in range(v.shape[1] // LANE):
        blk = v[:, b * LANE:(b + 1) * LANE]
        s_low = jnp.sum(jnp.where(low, blk, 0.0), axis=-1, keepdims=True)
        s_high = jnp.sum(jnp.where(low, 0.0, blk), axis=-1, keepdims=True)
        blocks.append(jnp.where(low, s_low, s_high))
    return jnp.concatenate(blocks, axis=1)


def _dsilu(z, sg):
    return sg * (1.0 + z * (1.0 - sg))


def _mid(proj, attn, o_raw, x, tgt, g_hg, b_gate, g_post, wa, wb, w_out):
    s = x.shape[0]

    def body(attn_ref, ga_ref, o_ref, gb_ref, mg_ref, x_ref, t_ref, ghg_ref, bg_ref, gp_ref, wa_ref, wb_ref, wo_ref,
             loss_ref, dout_ref, dattn_ref, dga_ref, dor_ref, dgb_ref, dmg_ref, dwo_out, dwa_out, dwb_out, dgp_ref, dbg_ref, dghg_ref,
             dwo_ref, dwa_ref, dwb_ref):
        @pl.when(pl.program_id(0) == 0)
        def _():
            for rf in (loss_ref, dwo_ref, dwa_ref, dwb_ref, dgp_ref, dbg_ref, dghg_ref):
                rf[...] = jnp.zeros_like(rf)

        attn, za, orw, zb = attn_ref[...], ga_ref[...], o_ref[...], gb_ref[...]
        ghg, gp = ghg_ref[...], gp_ref[...]
        sga, sgb = _sigmoid(za), _sigmoid(zb)
        sa, sb = za * sga, zb * sgb
        ga = attn * sa
        rh = lax.rsqrt(_group_sum(orw * orw) * (1.0 / V_DIM) + EPS)
        on = (orw * rh) * ghg
        gb = on * sb
        ga_b, gb_b = ga.astype(BF16), gb.astype(BF16)
        blocks = [slice(t * (D // N_CHIPS), (t + 1) * (D // N_CHIPS)) for t in range(N_CHIPS)]
        ya = jnp.concatenate([_mm(ga_b, wa_ref[t]) for t in range(N_CHIPS)], axis=1)
        yb = jnp.concatenate([_mm(gb_b, wb_ref[t]) for t in range(N_CHIPS)], axis=1)
        gates = _sigmoid(mg_ref[...] + bg_ref[...])
        g0, g1 = gates[:, :D], gates[:, D:]
        m_b = (g0 * ya + g1 * yb).astype(BF16)
        y = _mm(m_b, wo_ref[...])
        ry = lax.rsqrt(jnp.mean(y * y, axis=-1, keepdims=True) + EPS)
        out = x_ref[...] + (y * ry) * gp
        err = out - t_ref[...]
        loss_ref[...] += 0.5 * jnp.sum(jnp.mean(err * err, axis=-1, keepdims=True), axis=0, keepdims=True)
        dout = err * (1.0 / D)
        dout_ref[...] = dout
        dgp_ref[...] += jnp.sum(dout * (y * ry), axis=0, keepdims=True)
        dgy = dout * gp
        dy = ry * dgy - y * (ry * ry * ry) * jnp.mean(y * dgy, axis=-1, keepdims=True)
        dy_b = dy.astype(BF16)
        dm = _mm_nt(dy_b, wo_ref[...])
        dya_b, dyb_b = (dm * g0).astype(BF16), (dm * g1).astype(BF16)
        dga = sum(_mm_nt(dya_b[:, cols], wa_ref[t]) for t, cols in enumerate(blocks))
        dgb = sum(_mm_nt(dyb_b[:, cols], wb_ref[t]) for t, cols in enumerate(blocks))
        dwo_ref[...] += _mm_tn(m_b, dy_b)
        for t, cols in enumerate(blocks):
            dwa_ref[t] += _mm_tn(ga_b, dya_b[:, cols])
            dwb_ref[t] += _mm_tn(gb_b, dyb_b[:, cols])
        dg0, dg1 = dm * ya, dm * yb
        dmg = jnp.concatenate([dg0 * g0 * (1.0 - g0), dg1 * g1 * (1.0 - g1)], axis=1)
        dmg_ref[...] = dmg.astype(BF16)
        dbg_ref[...] += jnp.sum(dmg, axis=0, keepdims=True)
        dattn_ref[...] = dga * sa
        dga_ref[...] = (dga * attn * _dsilu(za, sga)).astype(BF16)
        dgb_ref[...] = (dgb * on * _dsilu(zb, sgb)).astype(BF16)
        don = dgb * sb
        dghg_ref[...] += jnp.sum(don * (orw * rh), axis=0, keepdims=True)
        dgo = don * ghg
        dor_ref[...] = rh * dgo - orw * (rh * rh * rh) * (_group_sum(orw * dgo) * (1.0 / V_DIM))

        @pl.when(pl.program_id(0) == pl.num_programs(0) - 1)
        def _():
            for out, rf in ((dwo_out, dwo_ref), (dwa_out, dwa_ref), (dwb_out, dwb_ref)):
                out[...] = rf[...].astype(BF16)

    row = lambda w, j=0: pl.BlockSpec((TM_MID, w), lambda i: (i, j))
    full = lambda a: pl.BlockSpec(a.shape, lambda i: (0,) * a.ndim)
    acc = lambda shape: pl.BlockSpec(shape, lambda i: (0,) * len(shape))
    slabs = (N_CHIPS, 512, D // N_CHIPS)
    sds = jax.ShapeDtypeStruct
    return pl.pallas_call(
        body, name="mid", grid=(s // TM_MID,),
        in_specs=[row(512), row(512, P_GA // 512), row(512), row(512, P_GB // 512), row(2048, P_MERGE // 2048), row(D), row(D),
                  full(g_hg), full(b_gate), full(g_post), full(wa), full(wb), full(w_out)],
        out_specs=[acc((1, 1)), row(D), row(512), row(512), row(512), row(512), row(2048),
                   acc((D, D)), acc(slabs), acc(slabs), acc((1, D)), acc((1, 2048)), acc((1, 512))],
        out_shape=[sds((1, 1), F32), sds((s, D), F32), sds((s, 512), F32), sds((s, 512), BF16), sds((s, 512), F32), sds((s, 512), BF16),
                   sds((s, 2048), BF16), sds((D, D), BF16), sds(slabs, BF16), sds(slabs, BF16), sds((1, D), F32),
                   sds((1, 2048), F32), sds((1, 512), F32)],
        scratch_shapes=[pltpu.VMEM((D, D), F32), pltpu.VMEM(slabs, F32), pltpu.VMEM(slabs, F32)],
        compiler_params=_params(("arbitrary",)),
    )(attn, proj, o_raw, proj, proj, x, tgt, g_hg, b_gate, g_post, wa, wb, w_out)


def _attn_bwd(q, k, vv, attn, dattn, lse, token):
    s = q.shape[1]
    nt = s // TQ
    scale = 1.0 / math.sqrt(QK_NOPE + QK_ROPE)

    def body(q_ref, k_ref, v_ref, o_ref, do_ref, lse_ref, token_ref, dq_ref, dk_ref, dv_ref, do_s, delta_s):
        j = pl.program_id(1)

        @pl.when(j == 0)
        def _():
            dq_ref[...] = jnp.zeros_like(dq_ref)
            lane = lax.broadcasted_iota(jnp.int32, (TQ, LANE), 1)

            @pl.loop(0, nt)
            def _(i):
                rows = pl.ds(pl.multiple_of(i * TQ, TQ), TQ)
                do, o = do_ref[rows, :], o_ref[rows, :]
                for hh in range(2):
                    doh = jnp.where((lane < 64) if hh == 0 else (lane >= 64), do, 0.0)
                    do_s[hh, rows, :] = doh.astype(BF16)
                    delta_s[hh, rows, :] = jnp.broadcast_to(jnp.sum(doh * o, axis=-1, keepdims=True), (TQ, LANE))

        kjs, vjs = (k_ref[0], k_ref[1]), (v_ref[0], v_ref[1])

        def tile(hh, start, size, kj, vj, diag):
            rows = pl.ds(pl.multiple_of(start, size), size)
            wide = lambda a: jnp.concatenate([a] * (kj.shape[0] // LANE), axis=1)
            qi, do_b = q_ref[hh, rows, :], do_s[hh, rows, :]
            sc, dp = _mm_nt(qi, kj), _mm_nt(do_b, vj)
            p = jnp.exp2(sc * QK_SCALE2 - wide(lse_ref[hh, rows, :]))
            if diag:
                p = jnp.where(_diag_visible(size), p, 0.0)
            ds_b = (p * (dp - wide(delta_s[hh, rows, :]))).astype(BF16)
            dv, dk = _mm_tn(do_b, p.astype(BF16)), _mm_tn(qi, ds_b)
            dq_ref[hh, rows, :] += _mm(ds_b, kj)
            return dk, dv

        def step(i, carry):
            new = [tile(hh, i * TQ, TQ, kjs[hh], vjs[hh], False) for hh in range(2)]
            return tuple((carry[hh][0] + new[hh][0], carry[hh][1] + new[hh][1]) for hh in range(2))

        def diagonal(hh):
            k0, k1, v0, v1 = kjs[hh][:HQ], kjs[hh][HQ:], vjs[hh][:HQ], vjs[hh][HQ:]
            a = tile(hh, j * TQ, HQ, k0, v0, True)
            b = tile(hh, j * TQ + HQ, HQ, k0, v0, False)
            c = tile(hh, j * TQ + HQ, HQ, k1, v1, True)
            return jnp.concatenate([a[0] + b[0], c[0]], axis=1), jnp.concatenate([a[1] + b[1], c[1]], axis=1)

        carry = lax.fori_loop(j + 1, nt, step, (diagonal(0), diagonal(1)))
        for hh in range(2):
            dk_ref[hh] = carry[hh][0].T * scale
            dv_ref[hh] = carry[hh][1].T

        @pl.when(j == nt - 1)
        def _():
            dq_ref[...] = dq_ref[...] * scale

    whole = pl.BlockSpec((2, s, LANE), lambda p, j: (p, 0, 0))
    tile_spec = pl.BlockSpec((2, TQ, LANE), lambda p, j: (p, j, 0))
    cols = pl.BlockSpec((s, LANE), lambda p, j: (0, p))
    hs = jax.ShapeDtypeStruct((NH, s, LANE), F32)
    return pl.pallas_call(
        body, name="attn_bwd", grid=(NH // 2, nt),
        in_specs=[whole, tile_spec, tile_spec, cols, cols, whole, pl.BlockSpec((8, LANE), lambda p, j: (0, 0))],
        out_specs=[whole, tile_spec, tile_spec],
        out_shape=[hs, hs, hs],
        scratch_shapes=[pltpu.VMEM((2, s, LANE), BF16), pltpu.VMEM((2, s, LANE), F32)],
        compiler_params=_params(("parallel", "arbitrary")),
    )(q, k, vv, attn, dattn, lse, token)


def _hgrn_bwd(proj, lbl, states, do_raw):
    s = proj.shape[0]
    nt = s // TH
    nch = TH // HG_BLOCK

    def body(hq_ref, hf_ref, hi_ref, lbl_ref, st_ref, do_ref, dh_ref, dlbl_ref, dst, dlb):
        step = pl.program_id(1)

        @pl.when(step == 0)
        def _():
            dst[...] = jnp.zeros_like(dst)
            dlb[...] = jnp.zeros_like(dlb)

        m0, bd = _pair_masks()
        lb = _lower_bound(lbl_ref[...])
        gt = _hgrn_gates(hq_ref[...], hf_ref[...], lb)
        do = do_ref[...]
        qd, ki, ke = gt["qd"], gt["ki"], gt["ke"]
        v_b, do_b = hi_ref[...].astype(BF16), do.astype(BF16)
        qd_b, ki_b, ke_b = qd.astype(BF16), ki.astype(BF16), ke.astype(BF16)
        pairs = [slice(u * LANE, (u + 1) * LANE) for u in range(HG_PAIRS)]
        heads = [(lanes, m0 if hh == 0 else jnp.logical_not(m0)) for lanes in pairs for hh in range(2)]
        a_b = [jnp.where(gt["tri"], _mm_nt(jnp.where(mh, qd[:, lanes], 0.0).astype(BF16), ki_b[:, lanes]), 0.0).astype(BF16)
               for lanes, mh in heads]
        doh_b = [jnp.where(mh, do[:, lanes], 0.0).astype(BF16) for lanes, mh in heads]
        da_b = [jnp.where(gt["tri"], _mm_nt(d, v_b[:, lanes]), 0.0).astype(BF16) for d, (lanes, _) in zip(doh_b, heads)]
        dv_p, dqd_p, dki_p = [], [], []
        for u, lanes in enumerate(pairs):
            e, o = 2 * u, 2 * u + 1
            dv_p.append(_mm_tn(a_b[e], doh_b[e]) + _mm_tn(a_b[o], doh_b[o]))
            dqd_p.append(jnp.where(m0, _mm(da_b[e], ki_b[:, lanes]), _mm(da_b[o], ki_b[:, lanes])))
            dki_p.append(jnp.where(m0, _mm_tn(da_b[e], qd_b[:, lanes]), _mm_tn(da_b[o], qd_b[:, lanes])))
        fed = [_bmm_tn(_chunks(do_b[:, lanes]), _chunks(qd_b[:, lanes])) for lanes in pairs]
        leaving = []
        for u, lanes in enumerate(pairs):
            ds, left = dst[u], [None] * nch
            for n in reversed(range(nch)):
                left[n] = ds
                ds = gt["decay"][n][:, lanes] * ds + jnp.where(bd, fed[u][n], 0.0)
            dst[u] = ds
            leaving.append(jnp.stack(left))
        dke_p, dlast_p = [], []
        for u, lanes in enumerate(pairs):
            entering, leaving_b = st_ref[u], leaving[u].astype(BF16)
            dke3 = _bmm_nn(_chunks(v_b[:, lanes]), leaving_b)
            dv_p[u] = dv_p[u] + _bmm_nt(_chunks(ke_b[:, lanes]), leaving_b).reshape(TH, LANE)
            dqd_p[u] = dqd_p[u] + _bmm_nn(_chunks(do_b[:, lanes]), entering.astype(BF16)).reshape(TH, LANE)
            dke_p.append(dke3.reshape(TH, LANE))
            dlast_p.append(jnp.sum(dke3 * _chunks(ke[:, lanes]), axis=1, keepdims=True)
                           + jnp.sum(leaving[u] * entering, axis=1, keepdims=True) * gt["decay"][:, :, lanes])
        cat = lambda parts: jnp.concatenate(parts, axis=-1)
        dv, dqd, dki, dke, dlast = cat(dv_p), cat(dqd_p), cat(dki_p), cat(dke_p), cat(dlast_p)
        dk = dki * gt["ei"] + dke * gt["ee"]
        dcum = dqd * qd - dki * ki - dke * ke
        dg = _chunk_cumsum(dcum, reverse=True) + jnp.broadcast_to(dlast, (nch, HG_BLOCK, dlast.shape[-1])).reshape(dcum.shape)
        sig = gt["sig"]
        df = dg / gt["f"] - dk
        dlb[...] += jnp.sum(df * (1.0 - sig), axis=0, keepdims=True)
        dh_ref[0] = (dqd * gt["e"]).astype(BF16)
        dh_ref[1] = ((df * (1.0 - lb)) * sig * (1.0 - sig)).astype(BF16)
        dh_ref[2] = dv.astype(BF16)

        @pl.when(step == nt - 1)
        def _():
            lb = _lower_bound(lbl_ref[...])
            da0 = dlb[...] * lb * (1.0 - lb)
            dlbl_ref[...] = jnp.concatenate([da0, -da0], axis=0)

    wide = HG_PAIRS * LANE
    col = lambda base: pl.BlockSpec((TH, wide), lambda p, i: (nt - 1 - i, base // wide + p))
    tile = pl.BlockSpec((TH, wide), lambda p, i: (nt - 1 - i, p))
    sds = jax.ShapeDtypeStruct
    return pl.pallas_call(
        body, name="hgrn_bwd", grid=(NH // 2 // HG_PAIRS, nt),
        in_specs=[col(P_HQ), col(P_HF), col(P_HI), pl.BlockSpec((2, wide), lambda p, i: (0, p)),
                  pl.BlockSpec((HG_PAIRS, nch, LANE, LANE), lambda p, i: (p, nt - 1 - i, 0, 0)), tile],
        out_specs=[pl.BlockSpec((3, TH, wide), lambda p, i: (0, nt - 1 - i, p)), pl.BlockSpec((2, wide), lambda p, i: (0, p))],
        out_shape=[sds((3, s, 512), BF16), sds((2, 512), F32)],
        scratch_shapes=[pltpu.VMEM((HG_PAIRS, LANE, LANE), F32), pltpu.VMEM((1, wide), F32)],
        compiler_params=_params(("parallel", "arbitrary")),
    )(proj, proj, proj, lbl, states, do_raw)


def _norm_rows_bwd(v, r, g, dn):
    dgv = dn * g
    return r * dgv - v * (r * r * r) * jnp.mean(v * dgv, axis=-1, keepdims=True)


def _qkv_bwd(proj, dq, dk, dvv, g_q, g_kv, w_uq_p, w_k_p, w_v_p, rc, rs1, rs2):
    s = proj.shape[0]
    head_q = QK_NOPE + QK_ROPE

    def body(cq_ref, ckv_ref, dq_ref, dk_ref, dv_ref, gq_ref, gkv_ref, wq_ref, wk_ref, wv_ref, c_ref, s1_ref, s2_ref,
             dcq_ref, dckv_ref, dkpe_ref, dwq_out, dwkv_out, dgq_ref, dgkv_ref, dwq_ref, dwk_ref, dwv_ref):
        @pl.when(pl.program_id(0) == 0)
        def _():
            for rf in (dwq_ref, dwk_ref, dwv_ref, dgq_ref, dgkv_ref):
                rf[...] = jnp.zeros_like(rf)

        c, s1, s2 = c_ref[...], s1_ref[...], s2_ref[...]
        cq, ckv = cq_ref[...], ckv_ref[...]
        gq, gkv = gq_ref[...], gkv_ref[...]
        cqn, rq = _norm_rows(cq, gq)
        ckvn, rkv = _norm_rows(ckv, gkv)
        cqn_b, ckvn_b = cqn.astype(BF16), ckvn.astype(BF16)
        dqf = jnp.concatenate([_rope_t(dq_ref[h], c, s1, s2) for h in range(NH)], axis=1).astype(BF16)
        dkf = jnp.concatenate([dk_ref[h] for h in range(NH)], axis=1).astype(BF16)
        dvf = jnp.concatenate([dv_ref[h] for h in range(NH)], axis=1).astype(BF16)
        dkpe = dk_ref[0]
        for h in range(1, NH):
            dkpe = dkpe + dk_ref[h]
        lane = lax.broadcasted_iota(jnp.int32, (TM, LANE), 1)
        dkpe = jnp.where((lane >= QK_NOPE) & (lane < QK_NOPE + QK_ROPE), dkpe, 0.0)
        dkpe_ref[...] = _rope_t(dkpe, c, s1, s2).astype(BF16)
        dcqn = _mm_nt(dqf, wq_ref[...])
        pair = lambda a, t: a[:, t * 2 * LANE:(t + 1) * 2 * LANE]
        dckvn = sum(_mm_nt(pair(dkf, t), wk_ref[t]) + _mm_nt(pair(dvf, t), wv_ref[t]) for t in range(N_CHIPS))
        dwq_ref[...] += _mm_tn(cqn_b, dqf)
        dwk_ref[...] += _mm_tn(ckvn_b, dkf)
        dwv_ref[...] += _mm_tn(ckvn_b, dvf)
        dgq_ref[...] += jnp.sum(dcqn * (cq * rq), axis=0, keepdims=True)
        dgkv_ref[...] += jnp.sum(dckvn * (ckv * rkv), axis=0, keepdims=True)
        dcq_ref[...] = _norm_rows_bwd(cq, rq, gq, dcqn).astype(BF16)
        dckv_ref[...] = _norm_rows_bwd(ckv, rkv, gkv, dckvn).astype(BF16)

        @pl.when(pl.program_id(0) == pl.num_programs(0) - 1)
        def _():
            blk = lambda ref, h: ref[:, h * LANE:(h + 1) * LANE]
            lane = lax.broadcasted_iota(jnp.int32, (Q_LORA, LANE), 1)
            for j in range(NH * head_q // LANE):
                h0, w0 = divmod(j * LANE, head_q)
                first = blk(dwq_ref, h0) if w0 == 0 else pltpu.roll(blk(dwq_ref, h0), LANE - w0, 1)
                second = pltpu.roll(blk(dwq_ref, h0 + 1), head_q - w0, 1)
                dwq_out[:, j * LANE:(j + 1) * LANE] = jnp.where(lane < head_q - w0, first, second).astype(BF16)
            lane = lax.broadcasted_iota(jnp.int32, (KV_LORA, LANE), 1)
            for h in range(NH):
                vals = blk(dwv_ref, h) if h % 2 else pltpu.roll(blk(dwv_ref, h), V_DIM, 1)
                both = jnp.where(lane < QK_NOPE, blk(dwk_ref, h), vals).astype(BF16)
                dwkv_out[h // 2, :, (h % 2) * LANE:(h % 2 + 1) * LANE] = both

    row = lambda w, j=0: pl.BlockSpec((TM, w), lambda i: (i, j))
    full = lambda a: pl.BlockSpec(a.shape, lambda i: (0,) * a.ndim)
    acc = lambda *shape: pl.BlockSpec(shape, lambda i: (0,) * len(shape))
    heads = pl.BlockSpec((NH, TM, LANE), lambda i: (0, i, 0))
    sds = jax.ShapeDtypeStruct
    return pl.pallas_call(
        body, name="qkv_bwd", grid=(s // TM,),
        in_specs=[row(Q_LORA, P_CQ // Q_LORA), row(KV_LORA, P_CKV // KV_LORA), heads, heads, heads,
                  full(g_q), full(g_kv), full(w_uq_p), full(w_k_p), full(w_v_p), row(LANE), row(LANE), row(LANE)],
        out_specs=[row(Q_LORA), row(KV_LORA), row(LANE), acc(Q_LORA, NH * head_q), acc(NH // 2, KV_LORA, 2 * LANE),
                   acc(1, Q_LORA), acc(1, KV_LORA)],
        out_shape=[sds((s, Q_LORA), BF16), sds((s, KV_LORA), BF16), sds((s, LANE), BF16), sds((Q_LORA, NH * head_q), BF16),
                   sds((NH // 2, KV_LORA, 2 * LANE), BF16), sds((1, Q_LORA), F32), sds((1, KV_LORA), F32)],
        scratch_shapes=[pltpu.VMEM((Q_LORA, D), F32), pltpu.VMEM((KV_LORA, D), F32), pltpu.VMEM((KV_LORA, D), F32)],
        compiler_params=_params(("arbitrary",)),
    )(proj, proj, dq, dk, dvv, g_q, g_kv, w_uq_p, w_k_p, w_v_p, rc, rs1, rs2)


def _front_bwd(x, dout, dmg, dga, dh3, dgb, dcq, dckv, dkpe, g_pre, w_in_t, w_kpe, token):
    s = x.shape[0]

    def body(x_ref, do_ref, dmg_ref, dga_ref, dh3_ref, dgb_ref, dcq_ref, dckv_ref, dkpe_ref, g_ref, w_ref, k_ref, token_ref,
             gx_ref, dg_ref):
        @pl.when(pl.program_id(0) == 0)
        def _():
            dg_ref[...] = jnp.zeros_like(dg_ref)

        xv, g = x_ref[...], g_ref[...]
        _, r = _norm_rows(xv, g)
        pieces = ((dmg_ref[...], O_MERGE), (dga_ref[...], O_GA), (dh3_ref[0], O_HQ), (dh3_ref[1], O_HF), (dh3_ref[2], O_HI),
                  (dgb_ref[...], O_GB), (dcq_ref[...], O_CQ), (dckv_ref[...], O_CKV))
        dh = _mm(dkpe_ref[...], k_ref[...])
        for piece, off in pieces:
            dh = dh + _mm(piece, w_ref[off:off + piece.shape[1], :])
        dg_ref[...] += jnp.sum(dh * (xv * r), axis=0, keepdims=True)
        gx_ref[...] = do_ref[...] + _norm_rows_bwd(xv, r, g, dh)

    row = lambda w: pl.BlockSpec((TM, w), lambda i: (i, 0))
    full = lambda a: pl.BlockSpec(a.shape, lambda i: (0,) * a.ndim)
    sds = jax.ShapeDtypeStruct
    return pl.pallas_call(
        body, name="front_bwd", grid=(s // TM,),
        in_specs=[row(D), row(D), row(2048), row(512), pl.BlockSpec((3, TM, 512), lambda i: (0, i, 0)), row(512), row(Q_LORA),
                  row(KV_LORA), row(LANE), full(g_pre), full(w_in_t), full(w_kpe), pl.BlockSpec(memory_space=pl.ANY)],
        out_specs=[row(D), pl.BlockSpec((1, D), lambda i: (0, 0))],
        out_shape=[sds((s, D), F32), sds((1, D), F32)],
        compiler_params=_params(("arbitrary",)),
    )(x, dout, dmg, dga, dh3, dgb, dcq, dckv, dkpe, g_pre, w_in_t, w_kpe, token)


TK_GRAD = 1024


def _win_grad(h, pieces, name):
    s = h.shape[0]
    n = len(pieces)

    def body(h_ref, *refs):
        d_refs, o_refs, sums = refs[:n], refs[n:2 * n], refs[2 * n:]

        @pl.when(pl.program_id(0) == 0)
        def _():
            for s_ref in sums:
                s_ref[...] = jnp.zeros_like(s_ref)

        hv = h_ref[...]
        for d_ref, s_ref in zip(d_refs, sums):
            if len(d_ref.shape) == 3:
                for k in range(d_ref.shape[0]):
                    s_ref[k] += _mm_tn(d_ref[k], hv)
            else:
                s_ref[...] += _mm_tn(d_ref[...], hv)

        @pl.when(pl.program_id(0) == pl.num_programs(0) - 1)
        def _():
            for o_ref, s_ref in zip(o_refs, sums):
                o_ref[...] = s_ref[...].astype(BF16)

    def in_spec(p):
        if p.ndim == 3:
            return pl.BlockSpec((p.shape[0], TK_GRAD, p.shape[2]), lambda kk: (0, kk, 0))
        return pl.BlockSpec((TK_GRAD, p.shape[1]), lambda kk: (kk, 0))

    out_shapes = [(p.shape[0], p.shape[2], D) if p.ndim == 3 else (p.shape[1], D) for p in pieces]
    return pl.pallas_call(
        body, name=name, grid=(s // TK_GRAD,),
        in_specs=[pl.BlockSpec((TK_GRAD, D), lambda kk: (kk, 0))] + [in_spec(p) for p in pieces],
        out_specs=[pl.BlockSpec(sh, lambda kk, nd=len(sh): (0,) * nd) for sh in out_shapes],
        out_shape=[jax.ShapeDtypeStruct(sh, BF16) for sh in out_shapes],
        scratch_shapes=[pltpu.VMEM(sh, F32) for sh in out_shapes],
        compiler_params=_params(("arbitrary",)),
    )(h, *pieces)


QKV_ROWS = Q_LORA + KV_LORA + QK_ROPE


def _win_grad_qkv(h, dcq, dckv, dkpe):
    s = h.shape[0]
    half = QKV_ROWS // 2

    def body(h_ref, cq_ref, ckv_ref, kpe_ref, o_ref):
        @pl.when(pl.program_id(0) == 0)
        def _():
            o_ref[...] = jnp.zeros_like(o_ref)

        hv = h_ref[...]
        g_cq = _mm_tn(cq_ref[...], hv)
        o_ref[0] += g_cq[:half]
        o_ref[1, 0:Q_LORA - half] += g_cq[half:]
        o_ref[1, Q_LORA - half:Q_LORA + KV_LORA - half] += _mm_tn(ckv_ref[...], hv)
        o_ref[1, Q_LORA + KV_LORA - half:] += _mm_tn(kpe_ref[...], hv)[QK_NOPE:QK_NOPE + QK_ROPE]

    rows = lambda a: pl.BlockSpec((TK_GRAD, a.shape[1]), lambda kk: (kk, 0))
    return pl.pallas_call(
        body, name="win_grad_qkv", grid=(s // TK_GRAD,), in_specs=[rows(h), rows(dcq), rows(dckv), rows(dkpe)],
        out_specs=pl.BlockSpec((2, half, D), lambda kk: (0, 0, 0)), out_shape=jax.ShapeDtypeStruct((2, half, D), F32),
        compiler_params=_params(("arbitrary",)),
    )(h, dcq, dckv, dkpe)


def _pad_wuq(w_uq):
    rows = w_uq.shape[0]
    w = w_uq.reshape(rows, NH, QK_NOPE + QK_ROPE)
    return jnp.pad(w, ((0, 0), (0, 0), (0, LANE - QK_NOPE - QK_ROPE))).reshape(rows, NH * LANE)


def _pad_wukv(w_ukv):
    heads = w_ukv.shape[1] // (QK_NOPE + V_DIM)
    w = w_ukv.reshape(KV_LORA, heads, QK_NOPE + V_DIM)
    w_k = jnp.pad(w[:, :, :QK_NOPE], ((0, 0), (0, 0), (0, LANE - QK_NOPE))).reshape(KV_LORA, heads * LANE)
    wv = w[:, :, QK_NOPE:].reshape(KV_LORA, heads // 2, 2, 1, V_DIM)
    eye = jnp.eye(2, dtype=w.dtype).reshape(1, 1, 2, 2, 1)
    return w_k, (wv * eye).reshape(KV_LORA, heads * LANE)


def _local_step(x, tgt, g_pre, w_in_t, b_gate, g_q, g_kv, lb_logits, g_hgrn, g_post, weights, exchange=None):
    s = x.shape[0]
    w_kpe = _kpe_block(w_in_t)
    rc, rs1, rs2 = _rope_tables(s)
    g_hg = jnp.tile(g_hgrn, (1, NH))

    proj, h = _front_fwd(x, g_pre, w_in_t, w_kpe, weights.tokens)
    w_uq_p, w_k_p, w_v_p = weights.qkv(h)
    q, k, vv = _qkv_fwd(proj, g_q, g_kv, w_uq_p, w_k_p, w_v_p, rc, rs1, rs2)
    attn, lse = _attn_fwd(q, k, vv)
    o_raw, states = _hgrn_fwd(proj, lb_logits)
    wa, wb, w_out = weights.mid(o_raw)
    (loss, dout, dattn, dga, dor, dgb, dmg, d_wout, d_wa, d_wb, d_gpost, d_bgate, d_ghg) = _mid(
        proj, attn, o_raw, x, tgt, g_hg, b_gate, g_post, wa, wb, w_out)
    w_mg, w_ga, w_gb = _win_grad(h, [dmg, dga, dgb], "win_grad_mid")
    dh3, d_lbl = _hgrn_bwd(proj, lb_logits, states, dor)
    (w_h3,) = _win_grad(h, [dh3], "win_grad_hgrn")
    d_win_rest = jnp.concatenate([w_ga, w_h3[0], w_h3[1], w_h3[2], w_gb, w_mg], axis=0)
    early = dict(w_in_rest=d_win_rest, w_branch_a=d_wa, w_branch_b=d_wb, w_out=d_wout)
    token = exchange.start_early(early) if exchange else jnp.zeros((8, LANE), F32)
    dq, dk, dvv = _attn_bwd(q, k, vv, attn, dattn, lse, token)
    dcq, dckv, dkpe, d_wuq, d_wukv, d_gq, d_gkv = _qkv_bwd(proj, dq, dk, dvv, g_q, g_kv, w_uq_p, w_k_p, w_v_p, rc, rs1, rs2)
    late = dict(w_in_qkv=_win_grad_qkv(h, dcq, dckv, dkpe), w_uq=d_wuq, w_ukv=d_wukv)
    token = exchange.start_late(late) if exchange else jnp.zeros((8, LANE), F32)
    grad_x, d_gpre = _front_bwd(x, dout, dmg, dga, dh3, dgb, dcq, dckv, dkpe, g_pre, w_in_t, w_kpe, token)
    vec_grads = dict(g_pre=d_gpre, b_gate=d_bgate, g_q=d_gq, g_kv=d_gkv, lb_logits=d_lbl, g_hgrn=d_ghg, g_post=d_gpost)
    return loss, grad_x, dict(early, **late), vec_grads


SHARD_SHAPES = (("w_in", (1416, 1024)), ("w_uq", (192, 768)), ("w_ukv", (256, 256)), ("w_branch_a", (512, 256)),
                ("w_branch_b", (512, 256)), ("w_out", (256, 1024)))
BIG = tuple(n for n, _ in SHARD_SHAPES)
ROW_SHARDED = ("w_in", "w_uq", "w_out")
N_CHIPS = 4
W_IN_FORWARD_CUT = 704


def _to_block(name, a):
    return a[0].T if name == "w_in" else a[0]


def _from_block(name, a):
    return a.T[None] if name == "w_in" else a[None]
VEC_ROWS = (("g_pre", 0, 1024), ("b_gate", 1, 2048), ("g_q", 2, 768), ("g_kv", 3, 256), ("g_hgrn", 6, 64), ("g_post", 7, 1024))
VEC_LB_ROW = 4
VEC_SHAPE = (8, 2048)


def _split_by_chip(name, g):
    a, b = dict(SHARD_SHAPES)[name]
    return g.reshape(N_CHIPS, a, b) if name in ROW_SHARDED else g.reshape(a, N_CHIPS, b).transpose(1, 0, 2)


def _join_chips(name, w):
    a, b = dict(SHARD_SHAPES)[name]
    return w.reshape(N_CHIPS * a, b) if name in ROW_SHARDED else w.transpose(1, 0, 2).reshape(a, N_CHIPS * b)


MESH = pl.DeviceIdType.MESH
HBM = pl.BlockSpec(memory_space=pltpu.HBM)


def _mesh_place():
    x, y, c = lax.axis_index("x"), lax.axis_index("y"), lax.axis_index("c")
    return x, y, c, 2 * x + y, [(1 - x, y), (x, 1 - y), (1 - x, 1 - y)]


def _remote(src, dst, send_sems, recv_sems, k, to):
    return pltpu.make_async_remote_copy(src_ref=src, dst_ref=dst, send_sem=send_sems.at[k], recv_sem=recv_sems.at[k],
                                        device_id=to, device_id_type=MESH)


def _gather_w_in(shard):
    a, b = shard.shape
    cut = W_IN_FORWARD_CUT

    def body(src, out, ici_send, ici_recv, d2d_send, d2d_recv, local_sem):
        x, y, c = lax.axis_index("x"), lax.axis_index("y"), lax.axis_index("c")
        me, xn, yn, dg = 2 * x + y, 2 * (1 - x) + y, 2 * x + (1 - y), 2 * (1 - x) + (1 - y)
        to_x, to_y, sibling = (1 - x, y, c), (x, 1 - y, c), (x, y, 1 - c)
        first, rest = pl.ds(0, cut), pl.ds(cut, a - cut)

        whole = lambda ref, which: ref.at[:, pl.ds(pl.multiple_of(which * (b // 2), b // 2), b // 2)]
        own = pltpu.make_async_copy(src, out.at[me], local_sem)
        own.start()
        sends = [_remote(whole(src, c), whole(out.at[me], c), ici_send, ici_recv, 0, to_x),
                 _remote(whole(src, c), whole(out.at[me], c), ici_send, ici_recv, 1, to_y)]
        for cp in sends:
            cp.start()

        def landed(slot, rows, k, d2d_k, src_dev):
            piece = whole(out.at[slot], c) if rows is None else out.at[slot].at[rows, pl.ds(pl.multiple_of(c * (b // 2), b // 2), b // 2)]
            _remote(piece, piece, ici_send, ici_recv, k, src_dev).wait_recv()
            cp = _remote(piece, piece, d2d_send, d2d_recv, d2d_k, sibling)
            cp.start()
            sends.append(cp)
            return piece

        def pass_on(slot, rows, k, to):
            piece = out.at[slot].at[rows, pl.ds(pl.multiple_of(c * (b // 2), b // 2), b // 2)]
            cp = _remote(piece, piece, ici_send, ici_recv, k, to)
            cp.start()
            sends.append(cp)

        landed(xn, None, 0, 0, to_x)
        pass_on(xn, first, 2, to_y)
        landed(yn, None, 1, 1, to_y)
        pass_on(yn, rest, 3, to_x)
        landed(dg, first, 2, 2, to_y)
        landed(dg, rest, 3, 3, to_x)
        other = pl.ds(pl.multiple_of((1 - c) * (b // 2), b // 2), b // 2)
        for d2d_k, (slot, rows) in enumerate(((xn, None), (yn, None), (dg, first), (dg, rest))):
            piece = out.at[slot].at[:, other] if rows is None else out.at[slot].at[rows, other]
            _remote(piece, piece, d2d_send, d2d_recv, d2d_k, sibling).wait_recv()
        for cp in sends:
            cp.wait_send()
        own.wait()

    sems = pltpu.SemaphoreType.DMA((4,))
    return pl.pallas_call(
        body, name="gather_w_in", in_specs=[HBM], out_specs=HBM,
        out_shape=jax.ShapeDtypeStruct((N_CHIPS, a, b), shard.dtype),
        scratch_shapes=[sems, sems, sems, sems, pltpu.SemaphoreType.DMA],
        compiler_params=pltpu.CompilerParams(has_side_effects=True),
    )(shard)


def _sibling_exchange(srcs, name, after=None, halves=False):
    n = len(srcs)
    extra = [] if after is None else [after]

    def body(*refs):
        src_refs, outs = refs[:n], refs[n + len(extra):2 * n + len(extra)]
        send_sems, recv_sems = refs[2 * n + len(extra):]
        c = lax.axis_index("c")
        sibling = (lax.axis_index("x"), lax.axis_index("y"), 1 - c)
        if halves:
            src_refs = [ref.at[1 - c] for ref in src_refs]
        copies = [_remote(src_refs[k], outs[k], send_sems, recv_sems, k, sibling) for k in range(n)]
        for cp in copies:
            cp.start()
        for cp in copies:
            cp.wait()

    sems = pltpu.SemaphoreType.DMA((n,))
    return pl.pallas_call(
        body, name=name, in_specs=[HBM] * n + [pl.BlockSpec(memory_space=pl.ANY)] * len(extra), out_specs=[HBM] * n,
        out_shape=[jax.ShapeDtypeStruct(s.shape[1:] if halves else s.shape, s.dtype) for s in srcs],
        scratch_shapes=[sems, sems],
        compiler_params=pltpu.CompilerParams(has_side_effects=True),
    )(*srcs, *extra)


SEM = pl.BlockSpec(memory_space=pltpu.SEMAPHORE)
DATAFLOW = pltpu.SideEffectType.DATAFLOW_SIDE_EFFECTING


def _exchange_copies(srcs, to_first, src_refs, land_refs, send_sems, recv_sems):
    x, y, c, me, chips = _mesh_place()
    n = len(srcs)
    sends, recvs = [], []
    for k in range(n):
        if k in to_first:
            base = 3 * n + 4 * to_first.index(k)
            sends.append((me != 0, pltpu.make_async_remote_copy(
                src_ref=src_refs[k], dst_ref=land_refs[k].at[me], send_sem=send_sems.at[base], recv_sem=recv_sems.at[base + me],
                device_id=(0, 0, c), device_id_type=MESH)))
            for s in range(1, N_CHIPS):
                recvs.append((me == 0, pltpu.make_async_remote_copy(
                    src_ref=src_refs[k], dst_ref=land_refs[k].at[s], send_sem=send_sems.at[base], recv_sem=recv_sems.at[base + s],
                    device_id=(s // 2, s % 2, c), device_id_type=MESH)))
        else:
            slab = (lambda t, k=k: src_refs[k]) if srcs[k].ndim == 2 else (lambda t, k=k: src_refs[k].at[t])
            for j, (px, py) in enumerate(chips):
                sends.append((None, _remote(slab(2 * px + py), land_refs[k].at[me], send_sems, recv_sems, 3 * k + j, (px, py, c))))
                recvs.append((None, _remote(slab(me), land_refs[k].at[2 * px + py], send_sems, recv_sems, 3 * k + j, (px, py, c))))
    return sends, recvs


def _when(pred, fn):
    if pred is None:
        fn()
    else:
        pl.when(pred)(fn)


def _exchange_start(srcs, to_first, name, after=None):
    n = len(srcs)
    n_sems = 3 * n + 4 * len(to_first)
    lands = [lax.empty((N_CHIPS,) + s.shape[-2:], s.dtype) for s in srcs]
    extra = [] if after is None else [after]

    def body(*refs):
        src_refs, land_refs = refs[:n], refs[n:2 * n]
        send_sems, recv_sems, token = refs[2 * n + len(extra)], refs[2 * n + len(extra) + 1], refs[-1]
        sends, _ = _exchange_copies(srcs, to_first, src_refs, land_refs, send_sems, recv_sems)
        for pred, cp in sends:
            _when(pred, cp.start)
        token[...] = jnp.zeros_like(token)

    hbm = lambda a: pltpu.HBM(a.shape, a.dtype)
    res = pl.pallas_call(
        body, name=name,
        out_shape=[pltpu.SemaphoreType.DMA((n_sems,)), pltpu.SemaphoreType.DMA((n_sems,))] + [hbm(a) for a in srcs + lands]
        + [jax.ShapeDtypeStruct((8, LANE), F32)],
        in_specs=[HBM] * (2 * n) + [pl.BlockSpec(memory_space=pl.ANY)] * len(extra),
        out_specs=[SEM, SEM] + [HBM] * (2 * n) + [pl.BlockSpec(memory_space=pltpu.VMEM)],
        input_output_aliases={i: 2 + i for i in range(2 * n)},
        compiler_params=pltpu.CompilerParams(has_side_effects=DATAFLOW),
    )(*[pltpu.with_memory_space_constraint(a, pltpu.HBM) for a in srcs + lands], *extra)
    return res[:-1], res[-1]


def _exchange_wait(srcs, to_first, started, after, name):
    n = len(srcs)
    send_sems, recv_sems, thru = started[0], started[1], started[2:]

    def body(*refs):
        src_refs, land_refs, send_ref, recv_ref = refs[:n], refs[n:2 * n], refs[2 * n], refs[2 * n + 1]
        sends, recvs = _exchange_copies(srcs, to_first, src_refs, land_refs, send_ref, recv_ref)
        for pred, cp in sends:
            _when(pred, cp.wait_send)
        for pred, cp in recvs:
            _when(pred, cp.wait_recv)

    res = pl.pallas_call(
        body, name=name, out_shape=[pltpu.HBM(a.shape, a.dtype) for a in thru],
        in_specs=[HBM] * (2 * n) + [SEM, SEM, pl.BlockSpec(memory_space=pl.ANY)], out_specs=[HBM] * (2 * n),
        input_output_aliases={i: i for i in range(2 * n)},
        compiler_params=pltpu.CompilerParams(has_side_effects=DATAFLOW),
    )(*thru, send_sems, recv_sems, after)
    return res[:n], res[n:]


ROW_TILE = 256
COL_TILE = 256


def _block_tiling(a, b):
    if a <= ROW_TILE or a % ROW_TILE == 0:
        ta = min(a, ROW_TILE)
        return a // ta, (ta, b), lambda i: (i, 0)
    return b // COL_TILE, (a, COL_TILE), lambda i: (0, i)


def _sum_landed(land, own, name, first_land=None, first_own=None):
    _, a, b = land.shape
    steps, tile, at = _block_tiling(a, b)
    extra = first_land is not None

    def body(me_ref, *refs):
        p_ref, own_ref, o_ref = refs[0], refs[1], refs[-1]
        me = me_ref[0]
        own = own_ref[...].astype(F32)
        slot = lambda t: jnp.where(me == t, own, p_ref[t].astype(F32))
        o_ref[...] = ((slot(0) + slot(1)) + slot(2)) + slot(3)
        if extra:
            fp_ref, fo_ref = refs[2], refs[3]
            r = fo_ref.shape[0]

            @pl.when(me == 0)
            def _():
                f = lambda t: fp_ref[t].astype(F32)
                rows = pl.ds(pl.multiple_of(lax.axis_index("c") * r, 8), r)
                o_ref[rows, :] += ((fo_ref[...].astype(F32) + f(1)) + f(2)) + f(3)

    own_spec = pl.BlockSpec(tile, lambda i, me: at(i)) if own.ndim == 2 else pl.BlockSpec((None,) + tile, lambda i, me: (me[0],) + at(i))
    in_specs = [pl.BlockSpec((N_CHIPS,) + tile, lambda i, me: (0,) + at(i)), own_spec]
    args = [land, own]
    if extra:
        r = first_own.shape[0]
        assert tile[0] == a, "the extra rows need whole columns in a step"
        in_specs += [pl.BlockSpec((N_CHIPS, r, tile[1]), lambda i, me: (0,) + at(i)), pl.BlockSpec((r, tile[1]), lambda i, me: at(i))]
        args += [first_land, first_own]
    me = jnp.reshape(2 * lax.axis_index("x") + lax.axis_index("y"), (1,)).astype(jnp.int32)
    return pl.pallas_call(
        body, name=name,
        grid_spec=pltpu.PrefetchScalarGridSpec(num_scalar_prefetch=1, grid=(steps,), in_specs=in_specs,
                                               out_specs=pl.BlockSpec(tile, lambda i, me: at(i))),
        out_shape=jax.ShapeDtypeStruct((a, b), F32), compiler_params=_params(("parallel",)),
    )(me, *args)


def _sum_landed_small(lands, owns, name):
    n = len(lands)

    def body(*refs):
        me = 2 * lax.axis_index("x") + lax.axis_index("y")
        for p_ref, own_ref, o_ref in zip(refs[:n], refs[n:2 * n], refs[2 * n:]):
            own = own_ref[me].astype(F32)
            slot = lambda t: jnp.where(me == t, own, p_ref[t].astype(F32))
            o_ref[...] = ((slot(0) + slot(1)) + slot(2)) + slot(3)

    return pl.pallas_call(body, name=name, out_shape=[jax.ShapeDtypeStruct(o.shape[1:], F32) for o in owns],
                          compiler_params=_params(()))(*lands, *owns)


def _adamw_small(mine, theirs, states, name):
    n = len(mine)

    def body(*refs):
        ins, outs = refs[:5 * n], refs[5 * n:]
        for k in range(n):
            a_ref, b_ref, w_ref, m_ref, v_ref = ins[5 * k:5 * k + 5]
            g = a_ref[...] + b_ref[...]
            outs[4 * k][...] = g
            outs[4 * k + 1][...], outs[4 * k + 2][...], outs[4 * k + 3][...] = _adamw_math(g, w_ref[...], m_ref[...], v_ref[...])

    args = [t for k in range(n) for t in (mine[k], theirs[k], *states[k])]
    res = pl.pallas_call(body, name=name, out_shape=[jax.ShapeDtypeStruct(mine[k].shape, F32) for k in range(n) for _ in range(4)],
                         compiler_params=_params(()))(*args)
    return [tuple(res[4 * k:4 * k + 4]) for k in range(n)]


def _add_cast(halves, b, name):
    def body(a_ref, b_ref, o_ref):
        o_ref[...] = (a_ref[lax.axis_index("c")] + b_ref[...]).astype(BF16)

    return pl.pallas_call(body, name=name, out_shape=jax.ShapeDtypeStruct(b.shape, BF16),
                          compiler_params=_params(()))(halves, b)


class _LaterWeights:
    MID = ("w_branch_a", "w_branch_b", "w_out")

    def __init__(self, blocks, after):
        self.qkv_blocks = [_pad_wuq(blocks["w_uq"]), *_pad_wukv(blocks["w_ukv"])]
        self.mid_blocks = [blocks[n] for n in self.MID]
        self.qkv_started, t1 = _exchange_start(self.qkv_blocks, (), "weights_qkv_start", after)
        self.mid_started, t2 = _exchange_start(self.mid_blocks, (), "weights_mid_start", after)
        self.tokens = [t1, t2]

    @staticmethod
    def _whole(blocks, joined, started, after, name):
        _, landed = _exchange_wait(blocks, (), started, after, name)
        me = 2 * lax.axis_index("x") + lax.axis_index("y")
        out = []
        for block, land, join in zip(blocks, landed, joined):
            w = lax.dynamic_update_index_in_dim(land, block, me, 0)
            out.append(w.reshape(N_CHIPS * block.shape[0], block.shape[1]) if join else w)
        return out

    def qkv(self, after):
        return self._whole(self.qkv_blocks, (True, False, False), self.qkv_started, after, "weights_qkv_wait")

    def mid(self, after):
        return self._whole(self.mid_blocks, (False, False, True), self.mid_started, after, "weights_mid_wait")


class _GradExchange:
    EARLY = ("w_in", "w_branch_a", "w_branch_b", "w_out")
    LATE = ("w_uq", "w_ukv")

    def __init__(self, state):
        self.state = state
        self.outs = {}

    def start_early(self, g):
        full = jnp.concatenate([jnp.zeros((QKV_ROWS, D), g["w_in_rest"].dtype), g["w_in_rest"]], axis=0)
        g = dict(g, w_in=full)
        self.early = [(g[n] if g[n].ndim == 3 else _split_by_chip(n, g[n])).astype(BF16) for n in self.EARLY]
        self.early_started, token = _exchange_start(self.early, (), "grads_early_start")
        return token

    def start_late(self, g):
        self.early, self.early_landed = _exchange_wait(self.early, (), self.early_started, g["w_uq"], "grads_early_wait")
        (theirs,) = _sibling_exchange([g["w_in_qkv"]], "sibling_qkv_rows", halves=True)
        self.late = [_split_by_chip("w_uq", g["w_uq"]), g["w_ukv"], _add_cast(g["w_in_qkv"], theirs, "add_qkv_rows")]
        self.late_started, token = _exchange_start(self.late, (2,), "grads_late_start")
        names = self.EARLY[1:]
        mine = _sum_landed_small(self.early_landed[1:], self.early[1:], "sum_early")
        theirs = _sibling_exchange(mine, "sibling_early", after=token)
        for n, out in zip(names, _adamw_small(mine, theirs, [self.state[n] for n in names], "adamw_early")):
            self.outs[n] = out
        return self.outs[names[-1]][0]

    def finish(self, after):
        late, late_landed = _exchange_wait(self.late, (2,), self.late_started, after, "grads_late_wait")
        self.outs["w_in"] = _sum_exchange_adamw(self.early_landed[0], self.early[0], late_landed[2], late[2], *self.state["w_in"],
                                                "adamw_w_in")
        return dict(zip(self.LATE, _sum_landed_small(late_landed[:2], late[:2], "sum_late")))


def _adamw_math(g, w, m, v):
    nm = ADAM_B1 * m + (1.0 - ADAM_B1) * g
    nv = ADAM_B2 * v + (1.0 - ADAM_B2) * (g * g)
    m_hat = nm / (1.0 - ADAM_B1 ** ADAM_STEP)
    v_hat = nv / (1.0 - ADAM_B2 ** ADAM_STEP)
    return -ADAM_LR * (m_hat / (jnp.sqrt(v_hat) + ADAM_EPS) + ADAM_WD * w), nm, nv


def _sum_exchange_adamw(land, own, first_land, first_own, w, m, v, name):
    _, a, b = land.shape
    steps, tile, at = _block_tiling(a, b)
    assert tile[0] == a, "the extra rows need whole columns in a step"
    r = first_own.shape[0]

    def body(me_ref, p_ref, own_ref, fp_ref, fo_ref, w_ref, m_ref, v_ref, g_ref, d_ref, nm_ref, nv_ref,
             mine_s, theirs_s, send_sems, recv_sems):
        pass_, i = pl.program_id(0), pl.program_id(1)
        me, c = me_ref[0], lax.axis_index("c")
        sibling = (lax.axis_index("x"), lax.axis_index("y"), 1 - c)
        copy = _remote(mine_s.at[i], theirs_s.at[i], send_sems, recv_sems, i, sibling)

        @pl.when(pass_ == 0)
        def _():
            own = own_ref[...].astype(F32)
            slot = lambda t: jnp.where(me == t, own, p_ref[t].astype(F32))
            mine_s[i] = ((slot(0) + slot(1)) + slot(2)) + slot(3)

            @pl.when(me == 0)
            def _():
                f = lambda t: fp_ref[t].astype(F32)
                rows = pl.ds(pl.multiple_of(c * r, 8), r)
                mine_s[i, rows, :] += ((fo_ref[...].astype(F32) + f(1)) + f(2)) + f(3)

            copy.start()

        @pl.when(pass_ == 1)
        def _():
            copy.wait()
            g = mine_s[i] + theirs_s[i]
            g_ref[...] = g
            d_ref[...], nm_ref[...], nv_ref[...] = _adamw_math(g, w_ref[...], m_ref[...], v_ref[...])

    first = lambda p, i: i * (1 - p) + (steps - 1) * p
    second = lambda p, i: i * p
    in_specs = [pl.BlockSpec((N_CHIPS,) + tile, lambda p, i, me: (0,) + at(first(p, i))),
                pl.BlockSpec((None,) + tile, lambda p, i, me: (me[0],) + at(first(p, i))),
                pl.BlockSpec((N_CHIPS, r, tile[1]), lambda p, i, me: (0,) + at(first(p, i))),
                pl.BlockSpec((r, tile[1]), lambda p, i, me: at(first(p, i)))]
    state = pl.BlockSpec(tile, lambda p, i, me: at(second(p, i)))
    sds = jax.ShapeDtypeStruct((a, b), F32)
    me = jnp.reshape(2 * lax.axis_index("x") + lax.axis_index("y"), (1,)).astype(jnp.int32)
    kept = pltpu.VMEM((steps,) + tile, F32)
    sems = pltpu.SemaphoreType.DMA((steps,))
    return pl.pallas_call(
        body, name=name,
        grid_spec=pltpu.PrefetchScalarGridSpec(num_scalar_prefetch=1, grid=(2, steps), in_specs=in_specs + [state] * 3,
                                               out_specs=[state] * 4, scratch_shapes=[kept, kept, sems, sems]),
        out_shape=[sds] * 4,
        compiler_params=pltpu.CompilerParams(dimension_semantics=("arbitrary", "arbitrary"), vmem_limit_bytes=VMEM_LIMIT,
                                             has_side_effects=True),
    )(me, land, own, first_land, first_own, w, m, v)


LOSS_AT = (2, 1024)


def _vec_pack(vg, loss):
    names = [n for n, _, _ in VEC_ROWS]

    def body(*refs):
        o_ref = refs[-1]
        lb_ref, loss_ref = refs[len(names)], refs[len(names) + 1]
        o_ref[...] = jnp.zeros_like(o_ref)
        o_ref[LOSS_AT[0]:LOSS_AT[0] + 1, LOSS_AT[1]:LOSS_AT[1] + LANE] = jnp.broadcast_to(loss_ref[...], (1, LANE))
        for (name, row, size), ref in zip(VEC_ROWS, refs):
            if name == "g_hgrn":
                r = lax.broadcasted_iota(jnp.int32, (NH * V_DIM, LANE), 0)
                c = lax.broadcasted_iota(jnp.int32, (NH * V_DIM, LANE), 1)
                fold = ((r % V_DIM) == c).astype(F32)
                o_ref[row:row + 1, 0:LANE] = jnp.dot(ref[...], fold, precision=HIGHEST, preferred_element_type=F32)
            else:
                o_ref[row:row + 1, 0:size] = ref[...]
        o_ref[VEC_LB_ROW:VEC_LB_ROW + 2, 0:512] = lb_ref[...]

    return pl.pallas_call(body, name="vec_pack", out_shape=jax.ShapeDtypeStruct(VEC_SHAPE, F32))(
        *[vg[n] for n in names], vg["lb_logits"], loss)


def _adamw_vec(p_mine, p_sibling, w, m, v):
    names = [n for n, _, _ in VEC_ROWS] + ["lb_logits"]
    k = len(names)

    def body(a_ref, b_ref, *refs):
        ins, outs = refs[:3 * k], refs[3 * k:]
        at = (slice(LOSS_AT[0], LOSS_AT[0] + 1), slice(LOSS_AT[1], LOSS_AT[1] + LANE))
        outs[-1][...] = a_ref[at] + b_ref[at]
        for i, name in enumerate(names):
            if name == "lb_logits":
                rows, cols = slice(VEC_LB_ROW, VEC_LB_ROW + 2), slice(0, 512)
            else:
                _, row, size = VEC_ROWS[i]
                rows, cols = slice(row, row + 1), slice(0, size)
            g = a_ref[rows, cols] + b_ref[rows, cols]
            d, nm, nv = _adamw_math(g, ins[i][...], ins[k + i][...], ins[2 * k + i][...])
            for o_ref, val in zip(outs[4 * i:4 * i + 4], (g, d, nm, nv)):
                o_ref[...] = val

    shapes = [jax.ShapeDtypeStruct(w[n].shape, F32) for n in names for _ in range(4)] + [jax.ShapeDtypeStruct((1, LANE), F32)]
    res = pl.pallas_call(body, name="adamw_vec", out_shape=shapes)(
        p_mine, p_sibling, *[w[n] for n in names], *[m[n] for n in names], *[v[n] for n in names])
    return [{n: res[4 * i + j] for i, n in enumerate(names)} for j in range(4)], res[-1]


WEIGHTS = ("g_pre", "w_in", "b_gate", "g_q", "w_uq", "g_kv", "w_ukv", "lb_logits", "g_hgrn", "w_branch_a", "w_branch_b", "w_out", "g_post")


def kernel(x, g_pre, w_in, b_gate, g_q, w_uq, g_kv, w_ukv, lb_logits, g_hgrn, w_branch_a, w_branch_b, w_out, g_post, loss_target, m_g_pre, m_w_in, m_b_gate, m_g_q, m_w_uq, m_g_kv, m_w_ukv, m_lb_logits, m_g_hgrn, m_w_branch_a, m_w_branch_b, m_w_out, m_g_post, v_g_pre, v_w_in, v_b_gate, v_g_q, v_w_uq, v_g_kv, v_w_ukv, v_lb_logits, v_g_hgrn, v_w_branch_a, v_w_branch_b, v_w_out, v_g_post):
    w = dict(g_pre=g_pre, w_in=w_in, b_gate=b_gate, g_q=g_q, w_uq=w_uq, g_kv=g_kv, w_ukv=w_ukv, lb_logits=lb_logits, g_hgrn=g_hgrn,
             w_branch_a=w_branch_a, w_branch_b=w_branch_b, w_out=w_out, g_post=g_post)
    m = dict(g_pre=m_g_pre, w_in=m_w_in, b_gate=m_b_gate, g_q=m_g_q, w_uq=m_w_uq, g_kv=m_g_kv, w_ukv=m_w_ukv, lb_logits=m_lb_logits,
             g_hgrn=m_g_hgrn, w_branch_a=m_w_branch_a, w_branch_b=m_w_branch_b, w_out=m_w_out, g_post=m_g_post)
    v = dict(g_pre=v_g_pre, w_in=v_w_in, b_gate=v_b_gate, g_q=v_g_q, w_uq=v_w_uq, g_kv=v_g_kv, w_ukv=v_w_ukv, lb_logits=v_lb_logits,
             g_hgrn=v_g_hgrn, w_branch_a=v_w_branch_a, w_branch_b=v_w_branch_b, w_out=v_w_out, g_post=v_g_post)
    blocks = {n: _to_block(n, w[n]).astype(BF16) for n in BIG}
    w_in_all = _gather_w_in(blocks["w_in"])
    weights = _LaterWeights(blocks, w_in_all)
    state = {n: [_to_block(n, t[n]) for t in (w, m, v)] for n in BIG}
    exchange = _GradExchange(state)
    loss, grad_x, _, vec_grads = _local_step(
        x[0], loss_target[0], g_pre, _join_chips("w_in", w_in_all), b_gate, g_q, g_kv, lb_logits, g_hgrn, g_post, weights, exchange)
    vec = _vec_pack(vec_grads, loss)
    vec_started, token = _exchange_start([vec], (), "vec_start")
    sums = exchange.finish(token)
    rest = tuple(sums)
    (vec,), (vec_landed,) = _exchange_wait([vec], (), vec_started, sums[rest[-1]], "vec_wait")
    mine = [sums[n] for n in rest] + [_sum_landed(vec_landed, vec, "sum_vec")]
    theirs = _sibling_exchange(mine, "sibling_grads")
    done = dict(exchange.outs)
    small = _adamw_small(mine[:-1], theirs[:-1], [state[n] for n in rest], "adamw_late")
    done.update(zip(rest, small))
    outs = [{}, {}, {}, {}]
    for n in BIG:
        for o, val in zip(outs, done[n]):
            o[n] = _from_block(n, val)
    vec_outs, total = _adamw_vec(mine[-1], theirs[-1], w, m, v)
    for o, vals in zip(outs, vec_outs):
        o.update(vals)
    return (total[0, 0], grad_x[None], *[o[n] for o in outs for n in WEIGHTS])
```

```python
import math

import numpy as np
import jax
import jax.numpy as jnp
from jax import lax
from jax.experimental import pallas as pl
from jax.experimental.pallas import tpu as pltpu

F32 = jnp.float32
BF16 = jnp.bfloat16
HIGHEST = lax.Precision.HIGHEST

D = 1024
NH = 8
QK_NOPE, QK_ROPE, V_DIM = 64, 32, 64
Q_LORA, KV_LORA = 768, 256
CHUNK = 64
HG_BLOCK = 32
EPS = 1e-6
LANE = 128
P_MERGE, P_GA, P_HQ, P_HF, P_HI, P_GB, P_CQ, P_CKV, P_KPE = 0, 2048, 2560, 3072, 3584, 4096, 4608, 5376, 5632
D_P = 5760
O_CQ, O_CKV, O_KPE, O_GA, O_HQ, O_HF, O_HI, O_GB, O_MERGE = 0, 768, 1024, 1056, 1568, 2080, 2592, 3104, 3616

TM = 512
TM_MID = 256
TQ = 1024
ONES_LANE = (LANE - 1, 0)
TH = 256
HG_PAIRS = 4
VMEM_LIMIT = 56 * 1024 * 1024

ADAM_LR, ADAM_B1, ADAM_B2, ADAM_EPS, ADAM_WD, ADAM_STEP = 0.001, 0.9, 0.999, 1e-08, 0.01, 10

NT_DIMS = (((1,), (1,)), ((), ()))
TN_DIMS = (((0,), (0,)), ((), ()))


def _params(sem):
    return pltpu.CompilerParams(dimension_semantics=sem, vmem_limit_bytes=VMEM_LIMIT)


def _mm(a, b):
    return jnp.dot(a, b, preferred_element_type=F32)


def _mm_nt(a, b):
    return lax.dot_general(a, b, NT_DIMS, preferred_element_type=F32)


def _mm_tn(a, b):
    return lax.dot_general(a, b, TN_DIMS, preferred_element_type=F32)


def _sigmoid(z):
    return jax.nn.sigmoid(z)


def _rope(v, c, s1, s2):
    return v * c + pltpu.roll(v, 112, 1) * s1 + pltpu.roll(v, 16, 1) * s2


def _rope_t(dy, c, s1, s2):
    return dy * c + pltpu.roll(dy * s1, 16, 1) + pltpu.roll(dy * s2, 112, 1)


def _rope_tables(s):
    f32 = np.float32
    inv = f32(10000.0) ** (-np.arange(0, QK_ROPE, 2, dtype=f32) / f32(QK_ROPE))
    ang = np.arange(s, dtype=f32)[:, None] * inv[None, :]
    cos, sin = np.cos(ang).astype(f32), np.sin(ang).astype(f32)
    z64, z32, o64, o32 = np.zeros((s, 64), f32), np.zeros((s, 32), f32), np.ones((s, 64), f32), np.ones((s, 32), f32)
    z16 = np.zeros((s, 16), f32)
    c = np.concatenate([o64, cos, cos, o32], axis=1)
    s1 = np.concatenate([z64, -sin, z16, z32], axis=1)
    s2 = np.concatenate([z64, z16, sin, z32], axis=1)
    return jnp.asarray(c), jnp.asarray(s1), jnp.asarray(s2)


W_IN_RUNS = ((O_MERGE, 2048, P_MERGE), (O_GA, O_MERGE - O_GA, P_GA), (O_CQ, O_KPE - O_CQ, P_CQ))


def _kpe_block(w_in_t):
    z = lambda n: jnp.zeros((n, w_in_t.shape[1]), w_in_t.dtype)
    return jnp.concatenate([z(64), w_in_t[O_KPE:O_KPE + QK_ROPE], z(32)], axis=0)


def _front_fwd(x, g_pre, w_in_t, w_kpe, tokens=()):
    s = x.shape[0]
    tokens = list(tokens)

    def body(x_ref, g_ref, w_ref, k_ref, *refs):
        o_ref, h_ref = refs[len(tokens):]
        xv = x_ref[...]
        r = lax.rsqrt(jnp.mean(xv * xv, axis=-1, keepdims=True) + EPS)
        h = ((xv * r) * g_ref[...]).astype(BF16)
        h_ref[...] = h
        for row, rows, col in W_IN_RUNS:
            o_ref[:, col:col + rows] = _mm_nt(h, w_ref[row:row + rows, :])
        o_ref[:, P_KPE:P_KPE + LANE] = _mm_nt(h, k_ref[...])

    full = lambda a: pl.BlockSpec(a.shape, lambda i: (0,) * a.ndim)
    return pl.pallas_call(
        body, name="front_fwd", grid=(s // TM,),
        in_specs=[pl.BlockSpec((TM, D), lambda i: (i, 0)), pl.BlockSpec((1, D), lambda i: (0, 0)), full(w_in_t), full(w_kpe)]
        + [pl.BlockSpec((8, LANE), lambda i: (0, 0))] * len(tokens),
        out_specs=[pl.BlockSpec((TM, D_P), lambda i: (i, 0)), pl.BlockSpec((TM, D), lambda i: (i, 0))],
        out_shape=[jax.ShapeDtypeStruct((s, D_P), F32), jax.ShapeDtypeStruct((s, D), BF16)],
        compiler_params=_params(("parallel",)),
    )(x, g_pre, w_in_t, w_kpe, *tokens)


def _norm_rows(v, g):
    r = lax.rsqrt(jnp.mean(v * v, axis=-1, keepdims=True) + EPS)
    return (v * r) * g, r


def _qkv_fwd(proj, g_q, g_kv, w_uq_p, w_k_p, w_v_p, rc, rs1, rs2):
    s = proj.shape[0]

    def body(cq_ref, ckv_ref, kpe_ref, gq_ref, gkv_ref, wq_ref, wk_ref, wv_ref, c_ref, s1_ref, s2_ref, q_ref, k_ref, v_ref):
        c, s1, s2 = c_ref[...], s1_ref[...], s2_ref[...]
        cqn, _ = _norm_rows(cq_ref[...], gq_ref[...])
        ckvn, _ = _norm_rows(ckv_ref[...], gkv_ref[...])
        ckvn = ckvn.astype(BF16)
        qf = _mm(cqn.astype(BF16), wq_ref[...])
        kf = jnp.concatenate([_mm(ckvn, wk_ref[t]) for t in range(N_CHIPS)], axis=1)
        vf = jnp.concatenate([_mm(ckvn, wv_ref[t]) for t in range(N_CHIPS)], axis=1)
        kpe = _rope(kpe_ref[...], c, s1, s2)
        lane = lax.broadcasted_iota(jnp.int32, (TM, LANE), 1)
        for h in range(NH):
            blk = slice(h * LANE, (h + 1) * LANE)
            q_ref[h] = _rope(qf[:, blk], c, s1, s2).astype(BF16)
            k_ref[h] = (kf[:, blk] + kpe).astype(BF16)
            v_ref[h] = jnp.where(lane == ONES_LANE[h % 2], 1.0, vf[:, blk]).astype(BF16)

    row = lambda w, j: pl.BlockSpec((TM, w), lambda i: (i, j))
    full = lambda a: pl.BlockSpec(a.shape, lambda i: (0,) * a.ndim)
    hs = jax.ShapeDtypeStruct((NH, s, LANE), BF16)
    return pl.pallas_call(
        body, name="qkv_fwd", grid=(s // TM,),
        in_specs=[row(Q_LORA, P_CQ // Q_LORA), row(KV_LORA, P_CKV // KV_LORA), row(LANE, P_KPE // LANE),
                  full(g_q), full(g_kv), full(w_uq_p), full(w_k_p), full(w_v_p), row(LANE, 0), row(LANE, 0), row(LANE, 0)],
        out_specs=[pl.BlockSpec((NH, TM, LANE), lambda i: (0, i, 0))] * 3,
        out_shape=[hs, hs, hs],
        compiler_params=_params(("parallel",)),
    )(proj, proj, proj, g_q, g_kv, w_uq_p, w_k_p, w_v_p, rc, rs1, rs2)


LOG2E = 1.4426950408889634
QK_SCALE2 = LOG2E / math.sqrt(QK_NOPE + QK_ROPE)


HQ = TQ // 2


def _diag_visible(n):
    row = lax.broadcasted_iota(jnp.int32, (n, n), 0)
    col = lax.broadcasted_iota(jnp.int32, (n, n), 1)
    return (col // CHUNK) <= (row // CHUNK)


def _attn_fwd(q, k, vv):
    s = q.shape[1]

    def body(q_ref, k_ref, v_ref, o_ref, lse_ref):
        i = pl.program_id(1)
        qs = (q_ref[0], q_ref[1])

        def tiles(t, carry, diag):
            rows = pl.ds(pl.multiple_of(t * TQ, TQ), TQ)
            sc = [_mm_nt(qs[hh], k_ref[hh, rows, :]) for hh in range(2)]
            if diag:
                sc = [jnp.where(_diag_visible(TQ), s_, -jnp.inf) for s_ in sc]
            m_new = [jnp.maximum(carry[hh][0], jnp.max(sc[hh], axis=-1, keepdims=True)) for hh in range(2)]
            alpha = [jnp.exp2((carry[hh][0] - m_new[hh]) * QK_SCALE2) for hh in range(2)]
            p = [jnp.exp2((sc[hh] - m_new[hh]) * QK_SCALE2).astype(BF16) for hh in range(2)]
            acc = [alpha[hh] * carry[hh][1] + _mm(p[hh], v_ref[hh, rows, :]) for hh in range(2)]
            return (m_new[0], acc[0]), (m_new[1], acc[1])

        init = (jnp.full((TQ, 1), -jnp.inf, F32), jnp.zeros((TQ, LANE), F32))
        carry = lax.fori_loop(0, i, lambda t, c: tiles(t, c, False), (init, init))
        carry = tiles(i, carry, True)
        lane = lax.broadcasted_iota(jnp.int32, (TQ, LANE), 1)
        out = jnp.zeros((TQ, LANE), F32)
        for hh in range(2):
            m, acc = carry[hh]
            l = jnp.sum(jnp.where(lane == ONES_LANE[hh], acc, 0.0), axis=-1, keepdims=True)
            out = out + jnp.where((lane < V_DIM) == (hh == 0), acc, 0.0) / l
            lse_ref[hh] = jnp.broadcast_to(m * QK_SCALE2 + jnp.log(l) * LOG2E, (TQ, LANE))
        o_ref[...] = out

    return pl.pallas_call(
        body, name="attn_fwd", grid=(NH // 2, s // TQ),
        in_specs=[pl.BlockSpec((2, TQ, LANE), lambda p, i: (p, i, 0)), pl.BlockSpec((2, s, LANE), lambda p, i: (p, 0, 0)),
                  pl.BlockSpec((2, s, LANE), lambda p, i: (p, 0, 0))],
        out_specs=[pl.BlockSpec((TQ, LANE), lambda p, i: (i, p)), pl.BlockSpec((2, TQ, LANE), lambda p, i: (p, i, 0))],
        out_shape=[jax.ShapeDtypeStruct((s, NH * V_DIM), F32), jax.ShapeDtypeStruct((NH, s, LANE), F32)],
        compiler_params=_params(("parallel", "parallel")),
    )(q, k, vv)


def _lower_bound(lbl):
    a0, a1 = lbl[0:1, :], lbl[1:2, :]
    mx = jnp.maximum(a0, a1)
    e0, e1 = jnp.exp(a0 - mx), jnp.exp(a1 - mx)
    return e0 / (e0 + e1)


def _chunk_cumsum(v, reverse=False):
    pos = lax.broadcasted_iota(jnp.int32, v.shape, 0) % HG_BLOCK
    s = 1
    while s < HG_BLOCK:
        if reverse:
            v = v + jnp.where(pos < HG_BLOCK - s, pltpu.roll(v, TH - s, 0), 0.0)
        else:
            v = v + jnp.where(pos >= s, pltpu.roll(v, s, 0), 0.0)
        s *= 2
    return v


def _hgrn_gates(hq, hf, lb):
    sig = _sigmoid(hf)
    f = lb + (1.0 - lb) * sig
    g = jnp.log(f)
    kk = 1.0 - f
    r = lax.broadcasted_iota(jnp.int32, (TH, TH), 0)
    c = lax.broadcasted_iota(jnp.int32, (TH, TH), 1)
    tri = ((r // HG_BLOCK) == (c // HG_BLOCK)) & (r >= c)
    cum = _chunk_cumsum(g)
    nch = TH // HG_BLOCK
    total = _chunks(cum)[:, HG_BLOCK - 1:HG_BLOCK, :]
    lastb = jnp.broadcast_to(total, (nch, HG_BLOCK, hf.shape[-1])).reshape(hf.shape)
    e, ei, ee = jnp.exp(cum), jnp.exp(-cum), jnp.exp(lastb - cum)
    return dict(sig=sig, f=f, kk=kk, tri=tri, cum=cum, total=total, decay=jnp.exp(total), e=e, ei=ei, ee=ee,
                qd=hq * e, ki=kk * ei, ke=kk * ee)


def _chunks(v):
    return v.reshape(TH // HG_BLOCK, HG_BLOCK, v.shape[-1])


def _bmm_nt(a, b):
    return lax.dot_general(a, b, (((2,), (2,)), ((0,), (0,))), preferred_element_type=F32)


def _bmm_nn(a, b):
    return lax.dot_general(a, b, (((2,), (1,)), ((0,), (0,))), preferred_element_type=F32)


def _bmm_tn(a, b):
    return lax.dot_general(a, b, (((1,), (1,)), ((0,), (0,))), preferred_element_type=F32)


def _pair_masks():
    lane = lax.broadcasted_iota(jnp.int32, (TH, LANE), 1)
    kr = lax.broadcasted_iota(jnp.int32, (LANE, LANE), 0)
    kc = lax.broadcasted_iota(jnp.int32, (LANE, LANE), 1)
    return lane < 64, (kr // 64) == (kc // 64)


def _hgrn_fwd(proj, lbl):
    s = proj.shape[0]
    nch = TH // HG_BLOCK

    def body(hq_ref, hf_ref, hi_ref, lbl_ref, o_ref, st_ref, st):
        @pl.when(pl.program_id(1) == 0)
        def _():
            st[...] = jnp.zeros_like(st)

        m0, bd = _pair_masks()
        gt = _hgrn_gates(hq_ref[...], hf_ref[...], _lower_bound(lbl_ref[...]))
        v_b, qd, qd_b = hi_ref[...].astype(BF16), gt["qd"], gt["qd"].astype(BF16)
        ki_b, ke_b = gt["ki"].astype(BF16), gt["ke"].astype(BF16)
        pairs = [slice(u * LANE, (u + 1) * LANE) for u in range(HG_PAIRS)]
        heads = [(lanes, m0 if hh == 0 else jnp.logical_not(m0)) for lanes in pairs for hh in range(2)]
        a_b = [jnp.where(gt["tri"], _mm_nt(jnp.where(mh, qd[:, lanes], 0.0).astype(BF16), ki_b[:, lanes]), 0.0).astype(BF16)
               for lanes, mh in heads]
        intra = [jnp.where(m0, _mm(a_b[2 * u], v_b[:, lanes]), _mm(a_b[2 * u + 1], v_b[:, lanes])) for u, lanes in enumerate(pairs)]
        upd = [_bmm_tn(_chunks(v_b[:, lanes]), _chunks(ke_b[:, lanes])) for lanes in pairs]
        entering = []
        for u, lanes in enumerate(pairs):
            cur, states = st[u], []
            for n in range(nch):
                states.append(cur)
                cur = gt["decay"][n][:, lanes] * cur + jnp.where(bd, upd[u][n], 0.0)
            st[u] = cur
            entering.append(jnp.stack(states))
            st_ref[u] = entering[u]
        for u, lanes in enumerate(pairs):
            o_ref[:, lanes] = intra[u] + _bmm_nt(_chunks(qd_b[:, lanes]), entering[u].astype(BF16)).reshape(TH, LANE)

    wide = HG_PAIRS * LANE
    col = lambda base: pl.BlockSpec((TH, wide), lambda p, i: (i, base // wide + p))
    return pl.pallas_call(
        body, name="hgrn_fwd", grid=(NH // 2 // HG_PAIRS, s // TH),
        in_specs=[col(P_HQ), col(P_HF), col(P_HI), pl.BlockSpec((2, wide), lambda p, i: (0, p))],
        out_specs=[pl.BlockSpec((TH, wide), lambda p, i: (i, p)),
                   pl.BlockSpec((HG_PAIRS, nch, LANE, LANE), lambda p, i: (p, i, 0, 0))],
        out_shape=[jax.ShapeDtypeStruct((s, 512), F32), jax.ShapeDtypeStruct((NH // 2, s // HG_BLOCK, LANE, LANE), F32)],
        scratch_shapes=[pltpu.VMEM((HG_PAIRS, LANE, LANE), F32)],
        compiler_params=_params(("parallel", "arbitrary")),
    )(proj, proj, proj, lbl)


def _group_sum(v):
    low = lax.broadcasted_iota(jnp.int32, (v.shape[0], LANE), 1) < V_DIM
    blocks = []
    for b in range(v.shape[1] // LANE):
        blk = v[:, b * LANE:(b + 1) * LANE]
        s_low = jnp.sum(jnp.where(low, blk, 0.0), axis=-1, keepdims=True)
        s_high = jnp.sum(jnp.where(low, 0.0, blk), axis=-1, keepdims=True)
        blocks.append(jnp.where(low, s_low, s_high))
    return jnp.concatenate(blocks, axis=1)


def _dsilu(z, sg):
    return sg * (1.0 + z * (1.0 - sg))


def _mid(proj, attn, o_raw, x, tgt, g_hg, b_gate, g_post, wa, wb, w_out):
    s = x.shape[0]

    def body(attn_ref, ga_ref, o_ref, gb_ref, mg_ref, x_ref, t_ref, ghg_ref, bg_ref, gp_ref, wa_ref, wb_ref, wo_ref,
             loss_ref, dout_ref, dattn_ref, dga_ref, dor_ref, dgb_ref, dmg_ref, dwo_out, dwa_out, dwb_out, dgp_ref, dbg_ref, dghg_ref,
             dwo_ref, dwa_ref, dwb_ref):
        @pl.when(pl.program_id(0) == 0)
        def _():
            for rf in (loss_ref, dwo_ref, dwa_ref, dwb_ref, dgp_ref, dbg_ref, dghg_ref):
                rf[...] = jnp.zeros_like(rf)

        attn, za, orw, zb = attn_ref[...], ga_ref[...], o_ref[...], gb_ref[...]
        ghg, gp = ghg_ref[...], gp_ref[...]
        sga, sgb = _sigmoid(za), _sigmoid(zb)
        sa, sb = za * sga, zb * sgb
        ga = attn * sa
        rh = lax.rsqrt(_group_sum(orw * orw) * (1.0 / V_DIM) + EPS)
        on = (orw * rh) * ghg
        gb = on * sb
        ga_b, gb_b = ga.astype(BF16), gb.astype(BF16)
        blocks = [slice(t * (D // N_CHIPS), (t + 1) * (D // N_CHIPS)) for t in range(N_CHIPS)]
        ya = jnp.concatenate([_mm(ga_b, wa_ref[t]) for t in range(N_CHIPS)], axis=1)
        yb = jnp.concatenate([_mm(gb_b, wb_ref[t]) for t in range(N_CHIPS)], axis=1)
        gates = _sigmoid(mg_ref[...] + bg_ref[...])
        g0, g1 = gates[:, :D], gates[:, D:]
        m_b = (g0 * ya + g1 * yb).astype(BF16)
        y = _mm(m_b, wo_ref[...])
        ry = lax.rsqrt(jnp.mean(y * y, axis=-1, keepdims=True) + EPS)
        out = x_ref[...] + (y * ry) * gp
        err = out - t_ref[...]
        loss_ref[...] += 0.5 * jnp.sum(jnp.mean(err * err, axis=-1, keepdims=True), axis=0, keepdims=True)
        dout = err * (1.0 / D)
        dout_ref[...] = dout
        dgp_ref[...] += jnp.sum(dout * (y * ry), axis=0, keepdims=True)
        dgy = dout * gp
        dy = ry * dgy - y * (ry * ry * ry) * jnp.mean(y * dgy, axis=-1, keepdims=True)
        dy_b = dy.astype(BF16)
        dm = _mm_nt(dy_b, wo_ref[...])
        dya_b, dyb_b = (dm * g0).astype(BF16), (dm * g1).astype(BF16)
        dga = sum(_mm_nt(dya_b[:, cols], wa_ref[t]) for t, cols in enumerate(blocks))
        dgb = sum(_mm_nt(dyb_b[:, cols], wb_ref[t]) for t, cols in enumerate(blocks))
        dwo_ref[...] += _mm_tn(m_b, dy_b)
        for t, cols in enumerate(blocks):
            dwa_ref[t] += _mm_tn(ga_b, dya_b[:, cols])
            dwb_ref[t] += _mm_tn(gb_b, dyb_b[:, cols])
        dg0, dg1 = dm * ya, dm * yb
        dmg = jnp.concatenate([dg0 * g0 * (1.0 - g0), dg1 * g1 * (1.0 - g1)], axis=1)
        dmg_ref[...] = dmg.astype(BF16)
        dbg_ref[...] += jnp.sum(dmg, axis=0, keepdims=True)
        dattn_ref[...] = dga * sa
        dga_ref[...] = (dga * attn * _dsilu(za, sga)).astype(BF16)
        dgb_ref[...] = (dgb * on * _dsilu(zb, sgb)).astype(BF16)
        don = dgb * sb
        dghg_ref[...] += jnp.sum(don * (orw * rh), axis=0, keepdims=True)
        dgo = don * ghg
        dor_ref[...] = rh * dgo - orw * (rh * rh * rh) * (_group_sum(orw * dgo) * (1.0 / V_DIM))

        @pl.when(pl.program_id(0) == pl.num_programs(0) - 1)
        def _():
            for out, rf in ((dwo_out, dwo_ref), (dwa_out, dwa_ref), (dwb_out, dwb_ref)):
                out[...] = rf[...].astype(BF16)

    row = lambda w, j=0: pl.BlockSpec((TM_MID, w), lambda i: (i, j))
    full = lambda a: pl.BlockSpec(a.shape, lambda i: (0,) * a.ndim)
    acc = lambda shape: pl.BlockSpec(shape, lambda i: (0,) * len(shape))
    slabs = (N_CHIPS, 512, D // N_CHIPS)
    sds = jax.ShapeDtypeStruct
    return pl.pallas_call(
        body, name="mid", grid=(s // TM_MID,),
        in_specs=[row(512), row(512, P_GA // 512), row(512), row(512, P_GB // 512), row(2048, P_MERGE // 2048), row(D), row(D),
                  full(g_hg), full(b_gate), full(g_post), full(wa), full(wb), full(w_out)],
        out_specs=[acc((1, 1)), row(D), row(512), row(512), row(512), row(512), row(2048),
                   acc((D, D)), acc(slabs), acc(slabs), acc((1, D)), acc((1, 2048)), acc((1, 512))],
        out_shape=[sds((1, 1), F32), sds((s, D), F32), sds((s, 512), F32), sds((s, 512), BF16), sds((s, 512), F32), sds((s, 512), BF16),
                   sds((s, 2048), BF16), sds((D, D), BF16), sds(slabs, BF16), sds(slabs, BF16), sds((1, D), F32),
                   sds((1, 2048), F32), sds((1, 512), F32)],
        scratch_shapes=[pltpu.VMEM((D, D), F32), pltpu.VMEM(slabs, F32), pltpu.VMEM(slabs, F32)],
        compiler_params=_params(("arbitrary",)),
    )(attn, proj, o_raw, proj, proj, x, tgt, g_hg, b_gate, g_post, wa, wb, w_out)


def _attn_bwd(q, k, vv, attn, dattn, lse, token):
    s = q.shape[1]
    nt = s // TQ
    scale = 1.0 / math.sqrt(QK_NOPE + QK_ROPE)

    def body(q_ref, k_ref, v_ref, o_ref, do_ref, lse_ref, token_ref, dq_ref, dk_ref, dv_ref, do_s, delta_s):
        j = pl.program_id(1)

        @pl.when(j == 0)
        def _():
            dq_ref[...] = jnp.zeros_like(dq_ref)
            lane = lax.broadcasted_iota(jnp.int32, (TQ, LANE), 1)

            @pl.loop(0, nt)
            def _(i):
                rows = pl.ds(pl.multiple_of(i * TQ, TQ), TQ)
                do, o = do_ref[rows, :], o_ref[rows, :]
                for hh in range(2):
                    doh = jnp.where((lane < 64) if hh == 0 else (lane >= 64), do, 0.0)
                    do_s[hh, rows, :] = doh.astype(BF16)
                    delta_s[hh, rows, :] = jnp.broadcast_to(jnp.sum(doh * o, axis=-1, keepdims=True), (TQ, LANE))

        kjs, vjs = (k_ref[0], k_ref[1]), (v_ref[0], v_ref[1])

        def tile(hh, start, size, kj, vj, diag):
            rows = pl.ds(pl.multiple_of(start, size), size)
            wide = lambda a: jnp.concatenate([a] * (kj.shape[0] // LANE), axis=1)
            qi, do_b = q_ref[hh, rows, :], do_s[hh, rows, :]
            sc, dp = _mm_nt(qi, kj), _mm_nt(do_b, vj)
            p = jnp.exp2(sc * QK_SCALE2 - wide(lse_ref[hh, rows, :]))
            if diag:
                p = jnp.where(_diag_visible(size), p, 0.0)
            ds_b = (p * (dp - wide(delta_s[hh, rows, :]))).astype(BF16)
            dv, dk = _mm_tn(do_b, p.astype(BF16)), _mm_tn(qi, ds_b)
            dq_ref[hh, rows, :] += _mm(ds_b, kj)
            return dk, dv

        def step(i, carry):
            new = [tile(hh, i * TQ, TQ, kjs[hh], vjs[hh], False) for hh in range(2)]
            return tuple((carry[hh][0] + new[hh][0], carry[hh][1] + new[hh][1]) for hh in range(2))

        def diagonal(hh):
            k0, k1, v0, v1 = kjs[hh][:HQ], kjs[hh][HQ:], vjs[hh][:HQ], vjs[hh][HQ:]
            a = tile(hh, j * TQ, HQ, k0, v0, True)
            b = tile(hh, j * TQ + HQ, HQ, k0, v0, False)
            c = tile(hh, j * TQ + HQ, HQ, k1, v1, True)
            return jnp.concatenate([a[0] + b[0], c[0]], axis=1), jnp.concatenate([a[1] + b[1], c[1]], axis=1)

        carry = lax.fori_loop(j + 1, nt, step, (diagonal(0), diagonal(1)))
        for hh in range(2):
            dk_ref[hh] = carry[hh][0].T * scale
            dv_ref[hh] = carry[hh][1].T

        @pl.when(j == nt - 1)
        def _():
            dq_ref[...] = dq_ref[...] * scale

    whole = pl.BlockSpec((2, s, LANE), lambda p, j: (p, 0, 0))
    tile_spec = pl.BlockSpec((2, TQ, LANE), lambda p, j: (p, j, 0))
    cols = pl.BlockSpec((s, LANE), lambda p, j: (0, p))
    hs = jax.ShapeDtypeStruct((NH, s, LANE), F32)
    return pl.pallas_call(
        body, name="attn_bwd", grid=(NH // 2, nt),
        in_specs=[whole, tile_spec, tile_spec, cols, cols, whole, pl.BlockSpec((8, LANE), lambda p, j: (0, 0))],
        out_specs=[whole, tile_spec, tile_spec],
        out_shape=[hs, hs, hs],
        scratch_shapes=[pltpu.VMEM((2, s, LANE), BF16), pltpu.VMEM((2, s, LANE), F32)],
        compiler_params=_params(("parallel", "arbitrary")),
    )(q, k, vv, attn, dattn, lse, token)


def _hgrn_bwd(proj, lbl, states, do_raw):
    s = proj.shape[0]
    nt = s // TH
    nch = TH // HG_BLOCK

    def body(hq_ref, hf_ref, hi_ref, lbl_ref, st_ref, do_ref, dh_ref, dlbl_ref, dst, dlb):
        step = pl.program_id(1)

        @pl.when(step == 0)
        def _():
            dst[...] = jnp.zeros_like(dst)
            dlb[...] = jnp.zeros_like(dlb)

        m0, bd = _pair_masks()
        lb = _lower_bound(lbl_ref[...])
        gt = _hgrn_gates(hq_ref[...], hf_ref[...], lb)
        do = do_ref[...]
        qd, ki, ke = gt["qd"], gt["ki"], gt["ke"]
        v_b, do_b = hi_ref[...].astype(BF16), do.astype(BF16)
        qd_b, ki_b, ke_b = qd.astype(BF16), ki.astype(BF16), ke.astype(BF16)
        pairs = [slice(u * LANE, (u + 1) * LANE) for u in range(HG_PAIRS)]
        heads = [(lanes, m0 if hh == 0 else jnp.logical_not(m0)) for lanes in pairs for hh in range(2)]
        a_b = [jnp.where(gt["tri"], _mm_nt(jnp.where(mh, qd[:, lanes], 0.0).astype(BF16), ki_b[:, lanes]), 0.0).astype(BF16)
               for lanes, mh in heads]
        doh_b = [jnp.where(mh, do[:, lanes], 0.0).astype(BF16) for lanes, mh in heads]
        da_b = [jnp.where(gt["tri"], _mm_nt(d, v_b[:, lanes]), 0.0).astype(BF16) for d, (lanes, _) in zip(doh_b, heads)]
        dv_p, dqd_p, dki_p = [], [], []
        for u, lanes in enumerate(pairs):
            e, o = 2 * u, 2 * u + 1
            dv_p.append(_mm_tn(a_b[e], doh_b[e]) + _mm_tn(a_b[o], doh_b[o]))
            dqd_p.append(jnp.where(m0, _mm(da_b[e], ki_b[:, lanes]), _mm(da_b[o], ki_b[:, lanes])))
            dki_p.append(jnp.where(m0, _mm_tn(da_b[e], qd_b[:, lanes]), _mm_tn(da_b[o], qd_b[:, lanes])))
        fed = [_bmm_tn(_chunks(do_b[:, lanes]), _chunks(qd_b[:, lanes])) for lanes in pairs]
        leaving = []
        for u, lanes in enumerate(pairs):
            ds, left = dst[u], [None] * nch
            for n in reversed(range(nch)):
                left[n] = ds
                ds = gt["decay"][n][:, lanes] * ds + jnp.where(bd, fed[u][n], 0.0)
            dst[u] = ds
            leaving.append(jnp.stack(left))
        dke_p, dlast_p = [], []
        for u, lanes in enumerate(pairs):
            entering, leaving_b = st_ref[u], leaving[u].astype(BF16)
            dke3 = _bmm_nn(_chunks(v_b[:, lanes]), leaving_b)
            dv_p[u] = dv_p[u] + _bmm_nt(_chunks(ke_b[:, lanes]), leaving_b).reshape(TH, LANE)
            dqd_p[u] = dqd_p[u] + _bmm_nn(_chunks(do_b[:, lanes]), entering.astype(BF16)).reshape(TH, LANE)
            dke_p.append(dke3.reshape(TH, LANE))
            dlast_p.append(jnp.sum(dke3 * _chunks(ke[:, lanes]), axis=1, keepdims=True)
                           + jnp.sum(leaving[u] * entering, axis=1, keepdims=True) * gt["decay"][:, :, lanes])
        cat = lambda parts: jnp.concatenate(parts, axis=-1)
        dv, dqd, dki, dke, dlast = cat(dv_p), cat(dqd_p), cat(dki_p), cat(dke_p), cat(dlast_p)
        dk = dki * gt["ei"] + dke * gt["ee"]
        dcum = dqd * qd - dki * ki - dke * ke
        dg = _chunk_cumsum(dcum, reverse=True) + jnp.broadcast_to(dlast, (nch, HG_BLOCK, dlast.shape[-1])).reshape(dcum.shape)
        sig = gt["sig"]
        df = dg / gt["f"] - dk
        dlb[...] += jnp.sum(df * (1.0 - sig), axis=0, keepdims=True)
        dh_ref[0] = (dqd * gt["e"]).astype(BF16)
        dh_ref[1] = ((df * (1.0 - lb)) * sig * (1.0 - sig)).astype(BF16)
        dh_ref[2] = dv.astype(BF16)

        @pl.when(step == nt - 1)
        def _():
            lb = _lower_bound(lbl_ref[...])
            da0 = dlb[...] * lb * (1.0 - lb)
            dlbl_ref[...] = jnp.concatenate([da0, -da0], axis=0)

    wide = HG_PAIRS * LANE
    col = lambda base: pl.BlockSpec((TH, wide), lambda p, i: (nt - 1 - i, base // wide + p))
    tile = pl.BlockSpec((TH, wide), lambda p, i: (nt - 1 - i, p))
    sds = jax.ShapeDtypeStruct
    return pl.pallas_call(
        body, name="hgrn_bwd", grid=(NH // 2 // HG_PAIRS, nt),
        in_specs=[col(P_HQ), col(P_HF), col(P_HI), pl.BlockSpec((2, wide), lambda p, i: (0, p)),
                  pl.BlockSpec((HG_PAIRS, nch, LANE, LANE), lambda p, i: (p, nt - 1 - i, 0, 0)), tile],
        out_specs=[pl.BlockSpec((3, TH, wide), lambda p, i: (0, nt - 1 - i, p)), pl.BlockSpec((2, wide), lambda p, i: (0, p))],
        out_shape=[sds((3, s, 512), BF16), sds((2, 512), F32)],
        scratch_shapes=[pltpu.VMEM((HG_PAIRS, LANE, LANE), F32), pltpu.VMEM((1, wide), F32)],
        compiler_params=_params(("parallel", "arbitrary")),
    )(proj, proj, proj, lbl, states, do_raw)


def _norm_rows_bwd(v, r, g, dn):
    dgv = dn * g
    return r * dgv - v * (r * r * r) * jnp.mean(v * dgv, axis=-1, keepdims=True)


def _qkv_bwd(proj, dq, dk, dvv, g_q, g_kv, w_uq_p, w_k_p, w_v_p, rc, rs1, rs2):
    s = proj.shape[0]
    head_q = QK_NOPE + QK_ROPE

    def body(cq_ref, ckv_ref, dq_ref, dk_ref, dv_ref, gq_ref, gkv_ref, wq_ref, wk_ref, wv_ref, c_ref, s1_ref, s2_ref,
             dcq_ref, dckv_ref, dkpe_ref, dwq_out, dwkv_out, dgq_ref, dgkv_ref, dwq_ref, dwk_ref, dwv_ref):
        @pl.when(pl.program_id(0) == 0)
        def _():
            for rf in (dwq_ref, dwk_ref, dwv_ref, dgq_ref, dgkv_ref):
                rf[...] = jnp.zeros_like(rf)

        c, s1, s2 = c_ref[...], s1_ref[...], s2_ref[...]
        cq, ckv = cq_ref[...], ckv_ref[...]
        gq, gkv = gq_ref[...], gkv_ref[...]
        cqn, rq = _norm_rows(cq, gq)
        ckvn, rkv = _norm_rows(ckv, gkv)
        cqn_b, ckvn_b = cqn.astype(BF16), ckvn.astype(BF16)
        dqf = jnp.concatenate([_rope_t(dq_ref[h], c, s1, s2) for h in range(NH)], axis=1).astype(BF16)
        dkf = jnp.concatenate([dk_ref[h] for h in range(NH)], axis=1).astype(BF16)
        dvf = jnp.concatenate([dv_ref[h] for h in range(NH)], axis=1).astype(BF16)
        dkpe = dk_ref[0]
        for h in range(1, NH):
            dkpe = dkpe + dk_ref[h]
        lane = lax.broadcasted_iota(jnp.int32, (TM, LANE), 1)
        dkpe = jnp.where((lane >= QK_NOPE) & (lane < QK_NOPE + QK_ROPE), dkpe, 0.0)
        dkpe_ref[...] = _rope_t(dkpe, c, s1, s2).astype(BF16)
        dcqn = _mm_nt(dqf, wq_ref[...])
        pair = lambda a, t: a[:, t * 2 * LANE:(t + 1) * 2 * LANE]
        dckvn = sum(_mm_nt(pair(dkf, t), wk_ref[t]) + _mm_nt(pair(dvf, t), wv_ref[t]) for t in range(N_CHIPS))
        dwq_ref[...] += _mm_tn(cqn_b, dqf)
        dwk_ref[...] += _mm_tn(ckvn_b, dkf)
        dwv_ref[...] += _mm_tn(ckvn_b, dvf)
        dgq_ref[...] += jnp.sum(dcqn * (cq * rq), axis=0, keepdims=True)
        dgkv_ref[...] += jnp.sum(dckvn * (ckv * rkv), axis=0, keepdims=True)
        dcq_ref[...] = _norm_rows_bwd(cq, rq, gq, dcqn).astype(BF16)
        dckv_ref[...] = _norm_rows_bwd(ckv, rkv, gkv, dckvn).astype(BF16)

        @pl.when(pl.program_id(0) == pl.num_programs(0) - 1)
        def _():
            blk = lambda ref, h: ref[:, h * LANE:(h + 1) * LANE]
            lane = lax.broadcasted_iota(jnp.int32, (Q_LORA, LANE), 1)
            for j in range(NH * head_q // LANE):
                h0, w0 = divmod(j * LANE, head_q)
                first = blk(dwq_ref, h0) if w0 == 0 else pltpu.roll(blk(dwq_ref, h0), LANE - w0, 1)
                second = pltpu.roll(blk(dwq_ref, h0 + 1), head_q - w0, 1)
                dwq_out[:, j * LANE:(j + 1) * LANE] = jnp.where(lane < head_q - w0, first, second).astype(BF16)
            lane = lax.broadcasted_iota(jnp.int32, (KV_LORA, LANE), 1)
            for h in range(NH):
                vals = blk(dwv_ref, h) if h % 2 else pltpu.roll(blk(dwv_ref, h), V_DIM, 1)
                both = jnp.where(lane < QK_NOPE, blk(dwk_ref, h), vals).astype(BF16)
                dwkv_out[h // 2, :, (h % 2) * LANE:(h % 2 + 1) * LANE] = both

    row = lambda w, j=0: pl.BlockSpec((TM, w), lambda i: (i, j))
    full = lambda a: pl.BlockSpec(a.shape, lambda i: (0,) * a.ndim)
    acc = lambda *shape: pl.BlockSpec(shape, lambda i: (0,) * len(shape))
    heads = pl.BlockSpec((NH, TM, LANE), lambda i: (0, i, 0))
    sds = jax.ShapeDtypeStruct
    return pl.pallas_call(
        body, name="qkv_bwd", grid=(s // TM,),
        in_specs=[row(Q_LORA, P_CQ // Q_LORA), row(KV_LORA, P_CKV // KV_LORA), heads, heads, heads,
                  full(g_q), full(g_kv), full(w_uq_p), full(w_k_p), full(w_v_p), row(LANE), row(LANE), row(LANE)],
        out_specs=[row(Q_LORA), row(KV_LORA), row(LANE), acc(Q_LORA, NH * head_q), acc(NH // 2, KV_LORA, 2 * LANE),
                   acc(1, Q_LORA), acc(1, KV_LORA)],
        out_shape=[sds((s, Q_LORA), BF16), sds((s, KV_LORA), BF16), sds((s, LANE), BF16), sds((Q_LORA, NH * head_q), BF16),
                   sds((NH // 2, KV_LORA, 2 * LANE), BF16), sds((1, Q_LORA), F32), sds((1, KV_LORA), F32)],
        scratch_shapes=[pltpu.VMEM((Q_LORA, D), F32), pltpu.VMEM((KV_LORA, D), F32), pltpu.VMEM((KV_LORA, D), F32)],
        compiler_params=_params(("arbitrary",)),
    )(proj, proj, dq, dk, dvv, g_q, g_kv, w_uq_p, w_k_p, w_v_p, rc, rs1, rs2)


def _front_bwd(x, dout, dmg, dga, dh3, dgb, dcq, dckv, dkpe, g_pre, w_in_t, w_kpe, token):
    s = x.shape[0]

    def body(x_ref, do_ref, dmg_ref, dga_ref, dh3_ref, dgb_ref, dcq_ref, dckv_ref, dkpe_ref, g_ref, w_ref, k_ref, token_ref,
             gx_ref, dg_ref):
        @pl.when(pl.program_id(0) == 0)
        def _():
            dg_ref[...] = jnp.zeros_like(dg_ref)

        xv, g = x_ref[...], g_ref[...]
        _, r = _norm_rows(xv, g)
        pieces = ((dmg_ref[...], O_MERGE), (dga_ref[...], O_GA), (dh3_ref[0], O_HQ), (dh3_ref[1], O_HF), (dh3_ref[2], O_HI),
                  (dgb_ref[...], O_GB), (dcq_ref[...], O_CQ), (dckv_ref[...], O_CKV))
        dh = _mm(dkpe_ref[...], k_ref[...])
        for piece, off in pieces:
            dh = dh + _mm(piece, w_ref[off:off + piece.shape[1], :])
        dg_ref[...] += jnp.sum(dh * (xv * r), axis=0, keepdims=True)
        gx_ref[...] = do_ref[...] + _norm_rows_bwd(xv, r, g, dh)

    row = lambda w: pl.BlockSpec((TM, w), lambda i: (i, 0))
    full = lambda a: pl.BlockSpec(a.shape, lambda i: (0,) * a.ndim)
    sds = jax.ShapeDtypeStruct
    return pl.pallas_call(
        body, name="front_bwd", grid=(s // TM,),
        in_specs=[row(D), row(D), row(2048), row(512), pl.BlockSpec((3, TM, 512), lambda i: (0, i, 0)), row(512), row(Q_LORA),
                  row(KV_LORA), row(LANE), full(g_pre), full(w_in_t), full(w_kpe), pl.BlockSpec(memory_space=pl.ANY)],
        out_specs=[row(D), pl.BlockSpec((1, D), lambda i: (0, 0))],
        out_shape=[sds((s, D), F32), sds((1, D), F32)],
        compiler_params=_params(("arbitrary",)),
    )(x, dout, dmg, dga, dh3, dgb, dcq, dckv, dkpe, g_pre, w_in_t, w_kpe, token)


TK_GRAD = 1024


def _win_grad(h, pieces, name):
    s = h.shape[0]
    n = len(pieces)

    def body(h_ref, *refs):
        d_refs, o_refs, sums = refs[:n], refs[n:2 * n], refs[2 * n:]

        @pl.when(pl.program_id(0) == 0)
        def _():
            for s_ref in sums:
                s_ref[...] = jnp.zeros_like(s_ref)

        hv = h_ref[...]
        for d_ref, s_ref in zip(d_refs, sums):
            if len(d_ref.shape) == 3:
                for k in range(d_ref.shape[0]):
                    s_ref[k] += _mm_tn(d_ref[k], hv)
            else:
                s_ref[...] += _mm_tn(d_ref[...], hv)

        @pl.when(pl.program_id(0) == pl.num_programs(0) - 1)
        def _():
            for o_ref, s_ref in zip(o_refs, sums):
                o_ref[...] = s_ref[...].astype(BF16)

    def in_spec(p):
        if p.ndim == 3:
            return pl.BlockSpec((p.shape[0], TK_GRAD, p.shape[2]), lambda kk: (0, kk, 0))
        return pl.BlockSpec((TK_GRAD, p.shape[1]), lambda kk: (kk, 0))

    out_shapes = [(p.shape[0], p.shape[2], D) if p.ndim == 3 else (p.shape[1], D) for p in pieces]
    return pl.pallas_call(
        body, name=name, grid=(s // TK_GRAD,),
        in_specs=[pl.BlockSpec((TK_GRAD, D), lambda kk: (kk, 0))] + [in_spec(p) for p in pieces],
        out_specs=[pl.BlockSpec(sh, lambda kk, nd=len(sh): (0,) * nd) for sh in out_shapes],
        out_shape=[jax.ShapeDtypeStruct(sh, BF16) for sh in out_shapes],
        scratch_shapes=[pltpu.VMEM(sh, F32) for sh in out_shapes],
        compiler_params=_params(("arbitrary",)),
    )(h, *pieces)


QKV_ROWS = Q_LORA + KV_LORA + QK_ROPE


def _win_grad_qkv(h, dcq, dckv, dkpe):
    s = h.shape[0]
    half = QKV_ROWS // 2

    def body(h_ref, cq_ref, ckv_ref, kpe_ref, o_ref):
        @pl.when(pl.program_id(0) == 0)
        def _():
            o_ref[...] = jnp.zeros_like(o_ref)

        hv = h_ref[...]
        g_cq = _mm_tn(cq_ref[...], hv)
        o_ref[0] += g_cq[:half]
        o_ref[1, 0:Q_LORA - half] += g_cq[half:]
        o_ref[1, Q_LORA - half:Q_LORA + KV_LORA - half] += _mm_tn(ckv_ref[...], hv)
        o_ref[1, Q_LORA + KV_LORA - half:] += _mm_tn(kpe_ref[...], hv)[QK_NOPE:QK_NOPE + QK_ROPE]

    rows = lambda a: pl.BlockSpec((TK_GRAD, a.shape[1]), lambda kk: (kk, 0))
    return pl.pallas_call(
        body, name="win_grad_qkv", grid=(s // TK_GRAD,), in_specs=[rows(h), rows(dcq), rows(dckv), rows(dkpe)],
        out_specs=pl.BlockSpec((2, half, D), lambda kk: (0, 0, 0)), out_shape=jax.ShapeDtypeStruct((2, half, D), F32),
        compiler_params=_params(("arbitrary",)),
    )(h, dcq, dckv, dkpe)


def _pad_wuq(w_uq):
    rows = w_uq.shape[0]
    w = w_uq.reshape(rows, NH, QK_NOPE + QK_ROPE)
    return jnp.pad(w, ((0, 0), (0, 0), (0, LANE - QK_NOPE - QK_ROPE))).reshape(rows, NH * LANE)


def _pad_wukv(w_ukv):
    heads = w_ukv.shape[1] // (QK_NOPE + V_DIM)
    w = w_ukv.reshape(KV_LORA, heads, QK_NOPE + V_DIM)
    w_k = jnp.pad(w[:, :, :QK_NOPE], ((0, 0), (0, 0), (0, LANE - QK_NOPE))).reshape(KV_LORA, heads * LANE)
    wv = w[:, :, QK_NOPE:].reshape(KV_LORA, heads // 2, 2, 1, V_DIM)
    eye = jnp.eye(2, dtype=w.dtype).reshape(1, 1, 2, 2, 1)
    return w_k, (wv * eye).reshape(KV_LORA, heads * LANE)


def _local_step(x, tgt, g_pre, w_in_t, b_gate, g_q, g_kv, lb_logits, g_hgrn, g_post, weights, exchange=None):
    s = x.shape[0]
    w_kpe = _kpe_block(w_in_t)
    rc, rs1, rs2 = _rope_tables(s)
    g_hg = jnp.tile(g_hgrn, (1, NH))

    proj, h = _front_fwd(x, g_pre, w_in_t, w_kpe, weights.tokens)
    w_uq_p, w_k_p, w_v_p = weights.qkv(h)
    q, k, vv = _qkv_fwd(proj, g_q, g_kv, w_uq_p, w_k_p, w_v_p, rc, rs1, rs2)
    attn, lse = _attn_fwd(q, k, vv)
    o_raw, states = _hgrn_fwd(proj, lb_logits)
    wa, wb, w_out = weights.mid(o_raw)
    (loss, dout, dattn, dga, dor, dgb, dmg, d_wout, d_wa, d_wb, d_gpost, d_bgate, d_ghg) = _mid(
        proj, attn, o_raw, x, tgt, g_hg, b_gate, g_post, wa, wb, w_out)
    w_mg, w_ga, w_gb = _win_grad(h, [dmg, dga, dgb], "win_grad_mid")
    dh3, d_lbl = _hgrn_bwd(proj, lb_logits, states, dor)
    (w_h3,) = _win_grad(h, [dh3], "win_grad_hgrn")
    d_win_rest = jnp.concatenate([w_ga, w_h3[0], w_h3[1], w_h3[2], w_gb, w_mg], axis=0)
    early = dict(w_in_rest=d_win_rest, w_branch_a=d_wa, w_branch_b=d_wb, w_out=d_wout)
    token = exchange.start_early(early) if exchange else jnp.zeros((8, LANE), F32)
    dq, dk, dvv = _attn_bwd(q, k, vv, attn, dattn, lse, token)
    dcq, dckv, dkpe, d_wuq, d_wukv, d_gq, d_gkv = _qkv_bwd(proj, dq, dk, dvv, g_q, g_kv, w_uq_p, w_k_p, w_v_p, rc, rs1, rs2)
    late = dict(w_in_qkv=_win_grad_qkv(h, dcq, dckv, dkpe), w_uq=d_wuq, w_ukv=d_wukv)
    token = exchange.start_late(late) if exchange else jnp.zeros((8, LANE), F32)
    grad_x, d_gpre = _front_bwd(x, dout, dmg, dga, dh3, dgb, dcq, dckv, dkpe, g_pre, w_in_t, w_kpe, token)
    vec_grads = dict(g_pre=d_gpre, b_gate=d_bgate, g_q=d_gq, g_kv=d_gkv, lb_logits=d_lbl, g_hgrn=d_ghg, g_post=d_gpost)
    return loss, grad_x, dict(early, **late), vec_grads


SHARD_SHAPES = (("w_in", (1416, 1024)), ("w_uq", (192, 768)), ("w_ukv", (256, 256)), ("w_branch_a", (512, 256)),
                ("w_branch_b", (512, 256)), ("w_out", (256, 1024)))
BIG = tuple(n for n, _ in SHARD_SHAPES)
ROW_SHARDED = ("w_in", "w_uq", "w_out")
N_CHIPS = 4
W_IN_FORWARD_CUT = 704


def _to_block(name, a):
    return a[0].T if name == "w_in" else a[0]


def _from_block(name, a):
    return a.T[None] if name == "w_in" else a[None]
VEC_ROWS = (("g_pre", 0, 1024), ("b_gate", 1, 2048), ("g_q", 2, 768), ("g_kv", 3, 256), ("g_hgrn", 6, 64), ("g_post", 7, 1024))
VEC_LB_ROW = 4
VEC_SHAPE = (8, 2048)


def _split_by_chip(name, g):
    a, b = dict(SHARD_SHAPES)[name]
    return g.reshape(N_CHIPS, a, b) if name in ROW_SHARDED else g.reshape(a, N_CHIPS, b).transpose(1, 0, 2)


def _join_chips(name, w):
    a, b = dict(SHARD_SHAPES)[name]
    return w.reshape(N_CHIPS * a, b) if name in ROW_SHARDED else w.transpose(1, 0, 2).reshape(a, N_CHIPS * b)


MESH = pl.DeviceIdType.MESH
HBM = pl.BlockSpec(memory_space=pltpu.HBM)


def _mesh_place():
    x, y, c = lax.axis_index("x"), lax.axis_index("y"), lax.axis_index("c")
    return x, y, c, 2 * x + y, [(1 - x, y), (x, 1 - y), (1 - x, 1 - y)]


def _remote(src, dst, send_sems, recv_sems, k, to):
    return pltpu.make_async_remote_copy(src_ref=src, dst_ref=dst, send_sem=send_sems.at[k], recv_sem=recv_sems.at[k],
                                        device_id=to, device_id_type=MESH)


def _gather_w_in(shard):
    a, b = shard.shape
    cut = W_IN_FORWARD_CUT

    def body(src, out, ici_send, ici_recv, d2d_send, d2d_recv, local_sem):
        x, y, c = lax.axis_index("x"), lax.axis_index("y"), lax.axis_index("c")
        me, xn, yn, dg = 2 * x + y, 2 * (1 - x) + y, 2 * x + (1 - y), 2 * (1 - x) + (1 - y)
        to_x, to_y, sibling = (1 - x, y, c), (x, 1 - y, c), (x, y, 1 - c)
        first, rest = pl.ds(0, cut), pl.ds(cut, a - cut)

        whole = lambda ref, which: ref.at[:, pl.ds(pl.multiple_of(which * (b // 2), b // 2), b // 2)]
        own = pltpu.make_async_copy(src, out.at[me], local_sem)
        own.start()
        sends = [_remote(whole(src, c), whole(out.at[me], c), ici_send, ici_recv, 0, to_x),
                 _remote(whole(src, c), whole(out.at[me], c), ici_send, ici_recv, 1, to_y)]
        for cp in sends:
            cp.start()

        def landed(slot, rows, k, d2d_k, src_dev):
            piece = whole(out.at[slot], c) if rows is None else out.at[slot].at[rows, pl.ds(pl.multiple_of(c * (b // 2), b // 2), b // 2)]
            _remote(piece, piece, ici_send, ici_recv, k, src_dev).wait_recv()
            cp = _remote(piece, piece, d2d_send, d2d_recv, d2d_k, sibling)
            cp.start()
            sends.append(cp)
            return piece

        def pass_on(slot, rows, k, to):
            piece = out.at[slot].at[rows, pl.ds(pl.multiple_of(c * (b // 2), b // 2), b // 2)]
            cp = _remote(piece, piece, ici_send, ici_recv, k, to)
            cp.start()
            sends.append(cp)

        landed(xn, None, 0, 0, to_x)
        pass_on(xn, first, 2, to_y)
        landed(yn, None, 1, 1, to_y)
        pass_on(yn, rest, 3, to_x)
        landed(dg, first, 2, 2, to_y)
        landed(dg, rest, 3, 3, to_x)
        other = pl.ds(pl.multiple_of((1 - c) * (b // 2), b // 2), b // 2)
        for d2d_k, (slot, rows) in enumerate(((xn, None), (yn, None), (dg, first), (dg, rest))):
            piece = out.at[slot].at[:, other] if rows is None else out.at[slot].at[rows, other]
            _remote(piece, piece, d2d_send, d2d_recv, d2d_k, sibling).wait_recv()
        for cp in sends:
            cp.wait_send()
        own.wait()

    sems = pltpu.SemaphoreType.DMA((4,))
    return pl.pallas_call(
        body, name="gather_w_in", in_specs=[HBM], out_specs=HBM,
        out_shape=jax.ShapeDtypeStruct((N_CHIPS, a, b), shard.dtype),
        scratch_shapes=[sems, sems, sems, sems, pltpu.SemaphoreType.DMA],
        compiler_params=pltpu.CompilerParams(has_side_effects=True),
    )(shard)


def _sibling_exchange(srcs, name, after=None, halves=False):
    n = len(srcs)
    extra = [] if after is None else [after]

    def body(*refs):
        src_refs, outs = refs[:n], refs[n + len(extra):2 * n + len(extra)]
        send_sems, recv_sems = refs[2 * n + len(extra):]
        c = lax.axis_index("c")
        sibling = (lax.axis_index("x"), lax.axis_index("y"), 1 - c)
        if halves:
            src_refs = [ref.at[1 - c] for ref in src_refs]
        copies = [_remote(src_refs[k], outs[k], send_sems, recv_sems, k, sibling) for k in range(n)]
        for cp in copies:
            cp.start()
        for cp in copies:
            cp.wait()

    sems = pltpu.SemaphoreType.DMA((n,))
    return pl.pallas_call(
        body, name=name, in_specs=[HBM] * n + [pl.BlockSpec(memory_space=pl.ANY)] * len(extra), out_specs=[HBM] * n,
        out_shape=[jax.ShapeDtypeStruct(s.shape[1:] if halves else s.shape, s.dtype) for s in srcs],
        scratch_shapes=[sems, sems],
        compiler_params=pltpu.CompilerParams(has_side_effects=True),
    )(*srcs, *extra)


SEM = pl.BlockSpec(memory_space=pltpu.SEMAPHORE)
DATAFLOW = pltpu.SideEffectType.DATAFLOW_SIDE_EFFECTING


def _exchange_copies(srcs, to_first, src_refs, land_refs, send_sems, recv_sems):
    x, y, c, me, chips = _mesh_place()
    n = len(srcs)
    sends, recvs = [], []
    for k in range(n):
        if k in to_first:
            base = 3 * n + 4 * to_first.index(k)
            sends.append((me != 0, pltpu.make_async_remote_copy(
                src_ref=src_refs[k], dst_ref=land_refs[k].at[me], send_sem=send_sems.at[base], recv_sem=recv_sems.at[base + me],
                device_id=(0, 0, c), device_id_type=MESH)))
            for s in range(1, N_CHIPS):
                recvs.append((me == 0, pltpu.make_async_remote_copy(
                    src_ref=src_refs[k], dst_ref=land_refs[k].at[s], send_sem=send_sems.at[base], recv_sem=recv_sems.at[base + s],
                    device_id=(s // 2, s % 2, c), device_id_type=MESH)))
        else:
            slab = (lambda t, k=k: src_refs[k]) if srcs[k].ndim == 2 else (lambda t, k=k: src_refs[k].at[t])
            for j, (px, py) in enumerate(chips):
                sends.append((None, _remote(slab(2 * px + py), land_refs[k].at[me], send_sems, recv_sems, 3 * k + j, (px, py, c))))
                recvs.append((None, _remote(slab(me), land_refs[k].at[2 * px + py], send_sems, recv_sems, 3 * k + j, (px, py, c))))
    return sends, recvs


def _when(pred, fn):
    if pred is None:
        fn()
    else:
        pl.when(pred)(fn)


def _exchange_start(srcs, to_first, name, after=None):
    n = len(srcs)
    n_sems = 3 * n + 4 * len(to_first)
    lands = [lax.empty((N_CHIPS,) + s.shape[-2:], s.dtype) for s in srcs]
    extra = [] if after is None else [after]

    def body(*refs):
        src_refs, land_refs = refs[:n], refs[n:2 * n]
        send_sems, recv_sems, token = refs[2 * n + len(extra)], refs[2 * n + len(extra) + 1], refs[-1]
        sends, _ = _exchange_copies(srcs, to_first, src_refs, land_refs, send_sems, recv_sems)
        for pred, cp in sends:
            _when(pred, cp.start)
        token[...] = jnp.zeros_like(token)

    hbm = lambda a: pltpu.HBM(a.shape, a.dtype)
    res = pl.pallas_call(
        body, name=name,
        out_shape=[pltpu.SemaphoreType.DMA((n_sems,)), pltpu.SemaphoreType.DMA((n_sems,))] + [hbm(a) for a in srcs + lands]
        + [jax.ShapeDtypeStruct((8, LANE), F32)],
        in_specs=[HBM] * (2 * n) + [pl.BlockSpec(memory_space=pl.ANY)] * len(extra),
        out_specs=[SEM, SEM] + [HBM] * (2 * n) + [pl.BlockSpec(memory_space=pltpu.VMEM)],
        input_output_aliases={i: 2 + i for i in range(2 * n)},
        compiler_params=pltpu.CompilerParams(has_side_effects=DATAFLOW),
    )(*[pltpu.with_memory_space_constraint(a, pltpu.HBM) for a in srcs + lands], *extra)
    return res[:-1], res[-1]


def _exchange_wait(srcs, to_first, started, after, name):
    n = len(srcs)
    send_sems, recv_sems, thru = started[0], started[1], started[2:]

    def body(*refs):
        src_refs, land_refs, send_ref, recv_ref = refs[:n], refs[n:2 * n], refs[2 * n], refs[2 * n + 1]
        sends, recvs = _exchange_copies(srcs, to_first, src_refs, land_refs, send_ref, recv_ref)
        for pred, cp in sends:
            _when(pred, cp.wait_send)
        for pred, cp in recvs:
            _when(pred, cp.wait_recv)

    res = pl.pallas_call(
        body, name=name, out_shape=[pltpu.HBM(a.shape, a.dtype) for a in thru],
        in_specs=[HBM] * (2 * n) + [SEM, SEM, pl.BlockSpec(memory_space=pl.ANY)], out_specs=[HBM] * (2 * n),
        input_output_aliases={i: i for i in range(2 * n)},
        compiler_params=pltpu.CompilerParams(has_side_effects=DATAFLOW),
    )(*thru, send_sems, recv_sems, after)
    return res[:n], res[n:]


ROW_TILE = 256
COL_TILE = 256


def _block_tiling(a, b):
    if a <= ROW_TILE or a % ROW_TILE == 0:
        ta = min(a, ROW_TILE)
        return a // ta, (ta, b), lambda i: (i, 0)
    return b // COL_TILE, (a, COL_TILE), lambda i: (0, i)


def _sum_landed(land, own, name, first_land=None, first_own=None):
    _, a, b = land.shape
    steps, tile, at = _block_tiling(a, b)
    extra = first_land is not None

    def body(me_ref, *refs):
        p_ref, own_ref, o_ref = refs[0], refs[1], refs[-1]
        me = me_ref[0]
        own = own_ref[...].astype(F32)
        slot = lambda t: jnp.where(me == t, own, p_ref[t].astype(F32))
        o_ref[...] = ((slot(0) + slot(1)) + slot(2)) + slot(3)
        if extra:
            fp_ref, fo_ref = refs[2], refs[3]
            r = fo_ref.shape[0]

            @pl.when(me == 0)
            def _():
                f = lambda t: fp_ref[t].astype(F32)
                rows = pl.ds(pl.multiple_of(lax.axis_index("c") * r, 8), r)
                o_ref[rows, :] += ((fo_ref[...].astype(F32) + f(1)) + f(2)) + f(3)

    own_spec = pl.BlockSpec(tile, lambda i, me: at(i)) if own.ndim == 2 else pl.BlockSpec((None,) + tile, lambda i, me: (me[0],) + at(i))
    in_specs = [pl.BlockSpec((N_CHIPS,) + tile, lambda i, me: (0,) + at(i)), own_spec]
    args = [land, own]
    if extra:
        r = first_own.shape[0]
        assert tile[0] == a, "the extra rows need whole columns in a step"
        in_specs += [pl.BlockSpec((N_CHIPS, r, tile[1]), lambda i, me: (0,) + at(i)), pl.BlockSpec((r, tile[1]), lambda i, me: at(i))]
        args += [first_land, first_own]
    me = jnp.reshape(2 * lax.axis_index("x") + lax.axis_index("y"), (1,)).astype(jnp.int32)
    return pl.pallas_call(
        body, name=name,
        grid_spec=pltpu.PrefetchScalarGridSpec(num_scalar_prefetch=1, grid=(steps,), in_specs=in_specs,
                                               out_specs=pl.BlockSpec(tile, lambda i, me: at(i))),
        out_shape=jax.ShapeDtypeStruct((a, b), F32), compiler_params=_params(("parallel",)),
    )(me, *args)


def _sum_landed_small(lands, owns, name):
    n = len(lands)

    def body(*refs):
        me = 2 * lax.axis_index("x") + lax.axis_index("y")
        for p_ref, own_ref, o_ref in zip(refs[:n], refs[n:2 * n], refs[2 * n:]):
            own = own_ref[me].astype(F32)
            slot = lambda t: jnp.where(me == t, own, p_ref[t].astype(F32))
            o_ref[...] = ((slot(0) + slot(1)) + slot(2)) + slot(3)

    return pl.pallas_call(body, name=name, out_shape=[jax.ShapeDtypeStruct(o.shape[1:], F32) for o in owns],
                          compiler_params=_params(()))(*lands, *owns)


def _adamw_small(mine, theirs, states, name):
    n = len(mine)

    def body(*refs):
        ins, outs = refs[:5 * n], refs[5 * n:]
        for k in range(n):
            a_ref, b_ref, w_ref, m_ref, v_ref = ins[5 * k:5 * k + 5]
            g = a_ref[...] + b_ref[...]
            outs[4 * k][...] = g
            outs[4 * k + 1][...], outs[4 * k + 2][...], outs[4 * k + 3][...] = _adamw_math(g, w_ref[...], m_ref[...], v_ref[...])

    args = [t for k in range(n) for t in (mine[k], theirs[k], *states[k])]
    res = pl.pallas_call(body, name=name, out_shape=[jax.ShapeDtypeStruct(mine[k].shape, F32) for k in range(n) for _ in range(4)],
                         compiler_params=_params(()))(*args)
    return [tuple(res[4 * k:4 * k + 4]) for k in range(n)]


def _add_cast(halves, b, name):
    def body(a_ref, b_ref, o_ref):
        o_ref[...] = (a_ref[lax.axis_index("c")] + b_ref[...]).astype(BF16)

    return pl.pallas_call(body, name=name, out_shape=jax.ShapeDtypeStruct(b.shape, BF16),
                          compiler_params=_params(()))(halves, b)


class _LaterWeights:
    MID = ("w_branch_a", "w_branch_b", "w_out")

    def __init__(self, blocks, after):
        self.qkv_blocks = [_pad_wuq(blocks["w_uq"]), *_pad_wukv(blocks["w_ukv"])]
        self.mid_blocks = [blocks[n] for n in self.MID]
        self.qkv_started, t1 = _exchange_start(self.qkv_blocks, (), "weights_qkv_start", after)
        self.mid_started, t2 = _exchange_start(self.mid_blocks, (), "weights_mid_start", after)
        self.tokens = [t1, t2]

    @staticmethod
    def _whole(blocks, joined, started, after, name):
        _, landed = _exchange_wait(blocks, (), started, after, name)
        me = 2 * lax.axis_index("x") + lax.axis_index("y")
        out = []
        for block, land, join in zip(blocks, landed, joined):
            w = lax.dynamic_update_index_in_dim(land, block, me, 0)
            out.append(w.reshape(N_CHIPS * block.shape[0], block.shape[1]) if join else w)
        return out

    def qkv(self, after):
        return self._whole(self.qkv_blocks, (True, False, False), self.qkv_started, after, "weights_qkv_wait")

    def mid(self, after):
        return self._whole(self.mid_blocks, (False, False, True), self.mid_started, after, "weights_mid_wait")


class _GradExchange:
    EARLY = ("w_in", "w_branch_a", "w_branch_b", "w_out")
    LATE = ("w_uq", "w_ukv")

    def __init__(self, state):
        self.state = state
        self.outs = {}

    def start_early(self, g):
        full = jnp.concatenate([jnp.zeros((QKV_ROWS, D), g["w_in_rest"].dtype), g["w_in_rest"]], axis=0)
        g = dict(g, w_in=full)
        self.early = [(g[n] if g[n].ndim == 3 else _split_by_chip(n, g[n])).astype(BF16) for n in self.EARLY]
        self.early_started, token = _exchange_start(self.early, (), "grads_early_start")
        return token

    def start_late(self, g):
        self.early, self.early_landed = _exchange_wait(self.early, (), self.early_started, g["w_uq"], "grads_early_wait")
        (theirs,) = _sibling_exchange([g["w_in_qkv"]], "sibling_qkv_rows", halves=True)
        self.late = [_split_by_chip("w_uq", g["w_uq"]), g["w_ukv"], _add_cast(g["w_in_qkv"], theirs, "add_qkv_rows")]
        self.late_started, token = _exchange_start(self.late, (2,), "grads_late_start")
        names = self.EARLY[1:]
        mine = _sum_landed_small(self.early_landed[1:], self.early[1:], "sum_early")
        theirs = _sibling_exchange(mine, "sibling_early", after=token)
        for n, out in zip(names, _adamw_small(mine, theirs, [self.state[n] for n in names], "adamw_early")):
            self.outs[n] = out
        return self.outs[names[-1]][0]

    def finish(self, after):
        late, late_landed = _exchange_wait(self.late, (2,), self.late_started, after, "grads_late_wait")
        grad = _sum_exchange(self.early_landed[0], self.early[0], late_landed[2], late[2], "sum_w_in")
        self.outs["w_in"] = _adamw(grad, *self.state["w_in"], "adamw_w_in")
        return dict(zip(self.LATE, _sum_landed_small(late_landed[:2], late[:2], "sum_late")))


def _adamw_math(g, w, m, v):
    nm = ADAM_B1 * m + (1.0 - ADAM_B1) * g
    nv = ADAM_B2 * v + (1.0 - ADAM_B2) * (g * g)
    m_hat = nm / (1.0 - ADAM_B1 ** ADAM_STEP)
    v_hat = nv / (1.0 - ADAM_B2 ** ADAM_STEP)
    return -ADAM_LR * (m_hat / (jnp.sqrt(v_hat) + ADAM_EPS) + ADAM_WD * w), nm, nv


def _sum_exchange(land, own, first_land, first_own, name):
    _, a, b = land.shape
    steps, tile, at = _block_tiling(a, b)
    assert tile[0] == a, "the extra rows need whole columns in a step"
    r = first_own.shape[0]

    def body(me_ref, p_ref, own_ref, fp_ref, fo_ref, g_ref, mine_s, theirs_s, send_sems, recv_sems):
        pass_, i = pl.program_id(0), pl.program_id(1)
        me, c = me_ref[0], lax.axis_index("c")
        sibling = (lax.axis_index("x"), lax.axis_index("y"), 1 - c)
        copy = _remote(mine_s.at[i], theirs_s.at[i], send_sems, recv_sems, i, sibling)

        @pl.when(pass_ == 0)
        def _():
            own = own_ref[...].astype(F32)
            slot = lambda t: jnp.where(me == t, own, p_ref[t].astype(F32))
            mine_s[i] = ((slot(0) + slot(1)) + slot(2)) + slot(3)

            @pl.when(me == 0)
            def _():
                f = lambda t: fp_ref[t].astype(F32)
                rows = pl.ds(pl.multiple_of(c * r, 8), r)
                mine_s[i, rows, :] += ((fo_ref[...].astype(F32) + f(1)) + f(2)) + f(3)

            copy.start()

        @pl.when(pass_ == 1)
        def _():
            copy.wait()
            g_ref[...] = mine_s[i] + theirs_s[i]

    first = lambda p, i: i * (1 - p) + (steps - 1) * p
    in_specs = [pl.BlockSpec((N_CHIPS,) + tile, lambda p, i, me: (0,) + at(first(p, i))),
                pl.BlockSpec((None,) + tile, lambda p, i, me: (me[0],) + at(first(p, i))),
                pl.BlockSpec((N_CHIPS, r, tile[1]), lambda p, i, me: (0,) + at(first(p, i))),
                pl.BlockSpec((r, tile[1]), lambda p, i, me: at(first(p, i)))]
    me = jnp.reshape(2 * lax.axis_index("x") + lax.axis_index("y"), (1,)).astype(jnp.int32)
    kept = pltpu.VMEM((steps,) + tile, F32)
    sems = pltpu.SemaphoreType.DMA((steps,))
    return pl.pallas_call(
        body, name=name,
        grid_spec=pltpu.PrefetchScalarGridSpec(num_scalar_prefetch=1, grid=(2, steps), in_specs=in_specs,
                                               out_specs=pl.BlockSpec(tile, lambda p, i, me: at(i * p)),
                                               scratch_shapes=[kept, kept, sems, sems]),
        out_shape=jax.ShapeDtypeStruct((a, b), F32),
        compiler_params=_params(("arbitrary", "arbitrary")),
    )(me, land, own, first_land, first_own)


def _adamw(g, w, m, v, name):
    a, b = g.shape
    steps, tile, at = _block_tiling(a, b)

    def body(g_ref, w_ref, m_ref, v_ref, go_ref, d_ref, nm_ref, nv_ref):
        g = g_ref[...]
        go_ref[...] = g
        d_ref[...], nm_ref[...], nv_ref[...] = _adamw_math(g, w_ref[...], m_ref[...], v_ref[...])

    spec = pl.BlockSpec(tile, at)
    sds = jax.ShapeDtypeStruct((a, b), F32)
    return pl.pallas_call(
        body, name=name, grid=(steps,), in_specs=[spec] * 4, out_specs=[spec] * 4, out_shape=[sds] * 4,
        compiler_params=_params(("parallel",)),
    )(g, w, m, v)


LOSS_AT = (2, 1024)


def _vec_pack(vg, loss):
    names = [n for n, _, _ in VEC_ROWS]

    def body(*refs):
        o_ref = refs[-1]
        lb_ref, loss_ref = refs[len(names)], refs[len(names) + 1]
        o_ref[...] = jnp.zeros_like(o_ref)
        o_ref[LOSS_AT[0]:LOSS_AT[0] + 1, LOSS_AT[1]:LOSS_AT[1] + LANE] = jnp.broadcast_to(loss_ref[...], (1, LANE))
        for (name, row, size), ref in zip(VEC_ROWS, refs):
            if name == "g_hgrn":
                r = lax.broadcasted_iota(jnp.int32, (NH * V_DIM, LANE), 0)
                c = lax.broadcasted_iota(jnp.int32, (NH * V_DIM, LANE), 1)
                fold = ((r % V_DIM) == c).astype(F32)
                o_ref[row:row + 1, 0:LANE] = jnp.dot(ref[...], fold, precision=HIGHEST, preferred_element_type=F32)
            else:
                o_ref[row:row + 1, 0:size] = ref[...]
        o_ref[VEC_LB_ROW:VEC_LB_ROW + 2, 0:512] = lb_ref[...]

    return pl.pallas_call(body, name="vec_pack", out_shape=jax.ShapeDtypeStruct(VEC_SHAPE, F32))(
        *[vg[n] for n in names], vg["lb_logits"], loss)


def _adamw_vec(p_mine, p_sibling, w, m, v):
    names = [n for n, _, _ in VEC_ROWS] + ["lb_logits"]
    k = len(names)

    def body(a_ref, b_ref, *refs):
        ins, outs = refs[:3 * k], refs[3 * k:]
        at = (slice(LOSS_AT[0], LOSS_AT[0] + 1), slice(LOSS_AT[1], LOSS_AT[1] + LANE))
        outs[-1][...] = a_ref[at] + b_ref[at]
        for i, name in enumerate(names):
            if name == "lb_logits":
                rows, cols = slice(VEC_LB_ROW, VEC_LB_ROW + 2), slice(0, 512)
            else:
                _, row, size = VEC_ROWS[i]
                rows, cols = slice(row, row + 1), slice(0, size)
            g = a_ref[rows, cols] + b_ref[rows, cols]
            d, nm, nv = _adamw_math(g, ins[i][...], ins[k + i][...], ins[2 * k + i][...])
            for o_ref, val in zip(outs[4 * i:4 * i + 4], (g, d, nm, nv)):
                o_ref[...] = val

    shapes = [jax.ShapeDtypeStruct(w[n].shape, F32) for n in names for _ in range(4)] + [jax.ShapeDtypeStruct((1, LANE), F32)]
    res = pl.pallas_call(body, name="adamw_vec", out_shape=shapes)(
        p_mine, p_sibling, *[w[n] for n in names], *[m[n] for n in names], *[v[n] for n in names])
    return [{n: res[4 * i + j] for i, n in enumerate(names)} for j in range(4)], res[-1]


WEIGHTS = ("g_pre", "w_in", "b_gate", "g_q", "w_uq", "g_kv", "w_ukv", "lb_logits", "g_hgrn", "w_branch_a", "w_branch_b", "w_out", "g_post")


def kernel(x, g_pre, w_in, b_gate, g_q, w_uq, g_kv, w_ukv, lb_logits, g_hgrn, w_branch_a, w_branch_b, w_out, g_post, loss_target, m_g_pre, m_w_in, m_b_gate, m_g_q, m_w_uq, m_g_kv, m_w_ukv, m_lb_logits, m_g_hgrn, m_w_branch_a, m_w_branch_b, m_w_out, m_g_post, v_g_pre, v_w_in, v_b_gate, v_g_q, v_w_uq, v_g_kv, v_w_ukv, v_lb_logits, v_g_hgrn, v_w_branch_a, v_w_branch_b, v_w_out, v_g_post):
    w = dict(g_pre=g_pre, w_in=w_in, b_gate=b_gate, g_q=g_q, w_uq=w_uq, g_kv=g_kv, w_ukv=w_ukv, lb_logits=lb_logits, g_hgrn=g_hgrn,
             w_branch_a=w_branch_a, w_branch_b=w_branch_b, w_out=w_out, g_post=g_post)
    m = dict(g_pre=m_g_pre, w_in=m_w_in, b_gate=m_b_gate, g_q=m_g_q, w_uq=m_w_uq, g_kv=m_g_kv, w_ukv=m_w_ukv, lb_logits=m_lb_logits,
             g_hgrn=m_g_hgrn, w_branch_a=m_w_branch_a, w_branch_b=m_w_branch_b, w_out=m_w_out, g_post=m_g_post)
    v = dict(g_pre=v_g_pre, w_in=v_w_in, b_gate=v_b_gate, g_q=v_g_q, w_uq=v_w_uq, g_kv=v_g_kv, w_ukv=v_w_ukv, lb_logits=v_lb_logits,
             g_hgrn=v_g_hgrn, w_branch_a=v_w_branch_a, w_branch_b=v_w_branch_b, w_out=v_w_out, g_post=v_g_post)
    blocks = {n: _to_block(n, w[n]).astype(BF16) for n in BIG}
    w_in_all = _gather_w_in(blocks["w_in"])
    weights = _LaterWeights(blocks, w_in_all)
    state = {n: [_to_block(n, t[n]) for t in (w, m, v)] for n in BIG}
    exchange = _GradExchange(state)
    loss, grad_x, _, vec_grads = _local_step(
        x[0], loss_target[0], g_pre, _join_chips("w_in", w_in_all), b_gate, g_q, g_kv, lb_logits, g_hgrn, g_post, weights, exchange)
    vec = _vec_pack(vec_grads, loss)
    vec_started, token = _exchange_start([vec], (), "vec_start")
    sums = exchange.finish(token)
    rest = tuple(sums)
    (vec,), (vec_landed,) = _exchange_wait([vec], (), vec_started, sums[rest[-1]], "vec_wait")
    mine = [sums[n] for n in rest] + [_sum_landed(vec_landed, vec, "sum_vec")]
    theirs = _sibling_exchange(mine, "sibling_grads")
    done = dict(exchange.outs)
    small = _adamw_small(mine[:-1], theirs[:-1], [state[n] for n in rest], "adamw_late")
    done.update(zip(rest, small))
    outs = [{}, {}, {}, {}]
    for n in BIG:
        for o, val in zip(outs, done[n]):
            o[n] = _from_block(n, val)
    vec_outs, total = _adamw_vec(mine[-1], theirs[-1], w, m, v)
    for o, vals in zip(outs, vec_outs):
        o.update(vals)
    return (total[0, 0], grad_x[None], *[o[n] for o in outs for n in WEIGHTS])
```

```python
import math

import numpy as np
import jax
import jax.numpy as jnp
from jax import lax
from jax.experimental import pallas as pl
from jax.experimental.pallas import tpu as pltpu

F32 = jnp.float32
BF16 = jnp.bfloat16
HIGHEST = lax.Precision.HIGHEST

D = 1024
NH = 8
QK_NOPE, QK_ROPE, V_DIM = 64, 32, 64
Q_LORA, KV_LORA = 768, 256
CHUNK = 64
HG_BLOCK = 32
EPS = 1e-6
LANE = 128
P_MERGE, P_GA, P_HQ, P_HF, P_HI, P_GB, P_CQ, P_CKV, P_KPE = 0, 2048, 2560, 3072, 3584, 4096, 4608, 5376, 5632
D_P = 5760
O_CQ, O_CKV, O_KPE, O_GA, O_HQ, O_HF, O_HI, O_GB, O_MERGE = 0, 768, 1024, 1056, 1568, 2080, 2592, 3104, 3616

TM = 512
TM_MID = 256
TQ = 1024
ONES_LANE = (LANE - 1, 0)
TH = 256
HG_PAIRS = 4
VMEM_LIMIT = 56 * 1024 * 1024

ADAM_LR, ADAM_B1, ADAM_B2, ADAM_EPS, ADAM_WD, ADAM_STEP = 0.001, 0.9, 0.999, 1e-08, 0.01, 10

NT_DIMS = (((1,), (1,)), ((), ()))
TN_DIMS = (((0,), (0,)), ((), ()))


def _params(sem):
    return pltpu.CompilerParams(dimension_semantics=sem, vmem_limit_bytes=VMEM_LIMIT)


def _mm(a, b):
    return jnp.dot(a, b, preferred_element_type=F32)


def _mm_nt(a, b):
    return lax.dot_general(a, b, NT_DIMS, preferred_element_type=F32)


def _mm_tn(a, b):
    return lax.dot_general(a, b, TN_DIMS, preferred_element_type=F32)


def _sigmoid(z):
    return jax.nn.sigmoid(z)


def _rope(v, c, s1, s2):
    return v * c + pltpu.roll(v, 112, 1) * s1 + pltpu.roll(v, 16, 1) * s2


def _rope_t(dy, c, s1, s2):
    return dy * c + pltpu.roll(dy * s1, 16, 1) + pltpu.roll(dy * s2, 112, 1)


def _rope_tables(s):
    f32 = np.float32
    inv = f32(10000.0) ** (-np.arange(0, QK_ROPE, 2, dtype=f32) / f32(QK_ROPE))
    ang = np.arange(s, dtype=f32)[:, None] * inv[None, :]
    cos, sin = np.cos(ang).astype(f32), np.sin(ang).astype(f32)
    z64, z32, o64, o32 = np.zeros((s, 64), f32), np.zeros((s, 32), f32), np.ones((s, 64), f32), np.ones((s, 32), f32)
    z16 = np.zeros((s, 16), f32)
    c = np.concatenate([o64, cos, cos, o32], axis=1)
    s1 = np.concatenate([z64, -sin, z16, z32], axis=1)
    s2 = np.concatenate([z64, z16, sin, z32], axis=1)
    return jnp.asarray(c), jnp.asarray(s1), jnp.asarray(s2)


W_IN_RUNS = ((O_MERGE, 2048, P_MERGE), (O_GA, O_MERGE - O_GA, P_GA), (O_CQ, O_KPE - O_CQ, P_CQ))


def _kpe_block(w_in_t):
    z = lambda n: jnp.zeros((n, w_in_t.shape[1]), w_in_t.dtype)
    return jnp.concatenate([z(64), w_in_t[O_KPE:O_KPE + QK_ROPE], z(32)], axis=0)


def _front_fwd(x, g_pre, w_in_t, w_kpe, tokens=()):
    s = x.shape[0]
    tokens = list(tokens)

    def body(x_ref, g_ref, w_ref, k_ref, *refs):
        o_ref, h_ref = refs[len(tokens):]
        xv = x_ref[...]
        r = lax.rsqrt(jnp.mean(xv * xv, axis=-1, keepdims=True) + EPS)
        h = ((xv * r) * g_ref[...]).astype(BF16)
        h_ref[...] = h
        for row, rows, col in W_IN_RUNS:
            o_ref[:, col:col + rows] = _mm_nt(h, w_ref[row:row + rows, :])
        o_ref[:, P_KPE:P_KPE + LANE] = _mm_nt(h, k_ref[...])

    full = lambda a: pl.BlockSpec(a.shape, lambda i: (0,) * a.ndim)
    return pl.pallas_call(
        body, name="front_fwd", grid=(s // TM,),
        in_specs=[pl.BlockSpec((TM, D), lambda i: (i, 0)), pl.BlockSpec((1, D), lambda i: (0, 0)), full(w_in_t), full(w_kpe)]
        + [pl.BlockSpec((8, LANE), lambda i: (0, 0))] * len(tokens),
        out_specs=[pl.BlockSpec((TM, D_P), lambda i: (i, 0)), pl.BlockSpec((TM, D), lambda i: (i, 0))],
        out_shape=[jax.ShapeDtypeStruct((s, D_P), F32), jax.ShapeDtypeStruct((s, D), BF16)],
        compiler_params=_params(("parallel",)),
    )(x, g_pre, w_in_t, w_kpe, *tokens)


def _norm_rows(v, g):
    r = lax.rsqrt(jnp.mean(v * v, axis=-1, keepdims=True) + EPS)
    return (v * r) * g, r


def _qkv_fwd(proj, g_q, g_kv, w_uq_p, w_k_p, w_v_p, rc, rs1, rs2):
    s = proj.shape[0]

    def body(cq_ref, ckv_ref, kpe_ref, gq_ref, gkv_ref, wq_ref, wk_ref, wv_ref, c_ref, s1_ref, s2_ref, q_ref, k_ref, v_ref):
        c, s1, s2 = c_ref[...], s1_ref[...], s2_ref[...]
        cqn, _ = _norm_rows(cq_ref[...], gq_ref[...])
        ckvn, _ = _norm_rows(ckv_ref[...], gkv_ref[...])
        ckvn = ckvn.astype(BF16)
        qf = _mm(cqn.astype(BF16), wq_ref[...])
        kf = jnp.concatenate([_mm(ckvn, wk_ref[t]) for t in range(N_CHIPS)], axis=1)
        vf = jnp.concatenate([_mm(ckvn, wv_ref[t]) for t in range(N_CHIPS)], axis=1)
        kpe = _rope(kpe_ref[...], c, s1, s2)
        lane = lax.broadcasted_iota(jnp.int32, (TM, LANE), 1)
        for h in range(NH):
            blk = slice(h * LANE, (h + 1) * LANE)
            q_ref[h] = _rope(qf[:, blk], c, s1, s2).astype(BF16)
            k_ref[h] = (kf[:, blk] + kpe).astype(BF16)
            v_ref[h] = jnp.where(lane == ONES_LANE[h % 2], 1.0, vf[:, blk]).astype(BF16)

    row = lambda w, j: pl.BlockSpec((TM, w), lambda i: (i, j))
    full = lambda a: pl.BlockSpec(a.shape, lambda i: (0,) * a.ndim)
    hs = jax.ShapeDtypeStruct((NH, s, LANE), BF16)
    return pl.pallas_call(
        body, name="qkv_fwd", grid=(s // TM,),
        in_specs=[row(Q_LORA, P_CQ // Q_LORA), row(KV_LORA, P_CKV // KV_LORA), row(LANE, P_KPE // LANE),
                  full(g_q), full(g_kv), full(w_uq_p), full(w_k_p), full(w_v_p), row(LANE, 0), row(LANE, 0), row(LANE, 0)],
        out_specs=[pl.BlockSpec((NH, TM, LANE), lambda i: (0, i, 0))] * 3,
        out_shape=[hs, hs, hs],
        compiler_params=_params(("parallel",)),
    )(proj, proj, proj, g_q, g_kv, w_uq_p, w_k_p, w_v_p, rc, rs1, rs2)


LOG2E = 1.4426950408889634
QK_SCALE2 = LOG2E / math.sqrt(QK_NOPE + QK_ROPE)


HQ = TQ // 2


def _diag_visible(n):
    row = lax.broadcasted_iota(jnp.int32, (n, n), 0)
    col = lax.broadcasted_iota(jnp.int32, (n, n), 1)
    return (col // CHUNK) <= (row // CHUNK)


def _attn_fwd(q, k, vv):
    s = q.shape[1]

    def body(q_ref, k_ref, v_ref, o_ref, lse_ref):
        i = pl.program_id(1)
        qs = (q_ref[0], q_ref[1])

        def tiles(t, carry, diag):
            rows = pl.ds(pl.multiple_of(t * TQ, TQ), TQ)
            sc = [_mm_nt(qs[hh], k_ref[hh, rows, :]) for hh in range(2)]
            if diag:
                sc = [jnp.where(_diag_visible(TQ), s_, -jnp.inf) for s_ in sc]
            m_new = [jnp.maximum(carry[hh][0], jnp.max(sc[hh], axis=-1, keepdims=True)) for hh in range(2)]
            alpha = [jnp.exp2((carry[hh][0] - m_new[hh]) * QK_SCALE2) for hh in range(2)]
            p = [jnp.exp2((sc[hh] - m_new[hh]) * QK_SCALE2).astype(BF16) for hh in range(2)]
            acc = [alpha[hh] * carry[hh][1] + _mm(p[hh], v_ref[hh, rows, :]) for hh in range(2)]
            return (m_new[0], acc[0]), (m_new[1], acc[1])

        init = (jnp.full((TQ, 1), -jnp.inf, F32), jnp.zeros((TQ, LANE), F32))
        carry = lax.fori_loop(0, i, lambda t, c: tiles(t, c, False), (init, init))
        carry = tiles(i, carry, True)
        lane = lax.broadcasted_iota(jnp.int32, (TQ, LANE), 1)
        out = jnp.zeros((TQ, LANE), F32)
        for hh in range(2):
            m, acc = carry[hh]
            l = jnp.sum(jnp.where(lane == ONES_LANE[hh], acc, 0.0), axis=-1, keepdims=True)
            out = out + jnp.where((lane < V_DIM) == (hh == 0), acc, 0.0) / l
            lse_ref[hh] = jnp.broadcast_to(m * QK_SCALE2 + jnp.log(l) * LOG2E, (TQ, LANE))
        o_ref[...] = out

    return pl.pallas_call(
        body, name="attn_fwd", grid=(NH // 2, s // TQ),
        in_specs=[pl.BlockSpec((2, TQ, LANE), lambda p, i: (p, i, 0)), pl.BlockSpec((2, s, LANE), lambda p, i: (p, 0, 0)),
                  pl.BlockSpec((2, s, LANE), lambda p, i: (p, 0, 0))],
        out_specs=[pl.BlockSpec((TQ, LANE), lambda p, i: (i, p)), pl.BlockSpec((2, TQ, LANE), lambda p, i: (p, i, 0))],
        out_shape=[jax.ShapeDtypeStruct((s, NH * V_DIM), F32), jax.ShapeDtypeStruct((NH, s, LANE), F32)],
        compiler_params=_params(("parallel", "parallel")),
    )(q, k, vv)


def _lower_bound(lbl):
    a0, a1 = lbl[0:1, :], lbl[1:2, :]
    mx = jnp.maximum(a0, a1)
    e0, e1 = jnp.exp(a0 - mx), jnp.exp(a1 - mx)
    return e0 / (e0 + e1)


def _chunk_cumsum(v, reverse=False):
    pos = lax.broadcasted_iota(jnp.int32, v.shape, 0) % HG_BLOCK
    s = 1
    while s < HG_BLOCK:
        if reverse:
            v = v + jnp.where(pos < HG_BLOCK - s, pltpu.roll(v, TH - s, 0), 0.0)
        else:
            v = v + jnp.where(pos >= s, pltpu.roll(v, s, 0), 0.0)
        s *= 2
    return v


def _hgrn_gates(hq, hf, lb):
    sig = _sigmoid(hf)
    f = lb + (1.0 - lb) * sig
    g = jnp.log(f)
    kk = 1.0 - f
    r = lax.broadcasted_iota(jnp.int32, (TH, TH), 0)
    c = lax.broadcasted_iota(jnp.int32, (TH, TH), 1)
    tri = ((r // HG_BLOCK) == (c // HG_BLOCK)) & (r >= c)
    cum = _chunk_cumsum(g)
    nch = TH // HG_BLOCK
    total = _chunks(cum)[:, HG_BLOCK - 1:HG_BLOCK, :]
    lastb = jnp.broadcast_to(total, (nch, HG_BLOCK, hf.shape[-1])).reshape(hf.shape)
    e, ei, ee = jnp.exp(cum), jnp.exp(-cum), jnp.exp(lastb - cum)
    return dict(sig=sig, f=f, kk=kk, tri=tri, cum=cum, total=total, decay=jnp.exp(total), e=e, ei=ei, ee=ee,
                qd=hq * e, ki=kk * ei, ke=kk * ee)


def _chunks(v):
    return v.reshape(TH // HG_BLOCK, HG_BLOCK, v.shape[-1])


def _bmm_nt(a, b):
    return lax.dot_general(a, b, (((2,), (2,)), ((0,), (0,))), preferred_element_type=F32)


def _bmm_nn(a, b):
    return lax.dot_general(a, b, (((2,), (1,)), ((0,), (0,))), preferred_element_type=F32)


def _bmm_tn(a, b):
    return lax.dot_general(a, b, (((1,), (1,)), ((0,), (0,))), preferred_element_type=F32)


def _pair_masks():
    lane = lax.broadcasted_iota(jnp.int32, (TH, LANE), 1)
    kr = lax.broadcasted_iota(jnp.int32, (LANE, LANE), 0)
    kc = lax.broadcasted_iota(jnp.int32, (LANE, LANE), 1)
    return lane < 64, (kr // 64) == (kc // 64)


def _hgrn_fwd(proj, lbl):
    s = proj.shape[0]
    nch = TH // HG_BLOCK

    def body(hq_ref, hf_ref, hi_ref, lbl_ref, o_ref, st_ref, st):
        @pl.when(pl.program_id(1) == 0)
        def _():
            st[...] = jnp.zeros_like(st)

        m0, bd = _pair_masks()
        gt = _hgrn_gates(hq_ref[...], hf_ref[...], _lower_bound(lbl_ref[...]))
        v_b, qd, qd_b = hi_ref[...].astype(BF16), gt["qd"], gt["qd"].astype(BF16)
        ki_b, ke_b = gt["ki"].astype(BF16), gt["ke"].astype(BF16)
        pairs = [slice(u * LANE, (u + 1) * LANE) for u in range(HG_PAIRS)]
        heads = [(lanes, m0 if hh == 0 else jnp.logical_not(m0)) for lanes in pairs for hh in range(2)]
        a_b = [jnp.where(gt["tri"], _mm_nt(jnp.where(mh, qd[:, lanes], 0.0).astype(BF16), ki_b[:, lanes]), 0.0).astype(BF16)
               for lanes, mh in heads]
        intra = [jnp.where(m0, _mm(a_b[2 * u], v_b[:, lanes]), _mm(a_b[2 * u + 1], v_b[:, lanes])) for u, lanes in enumerate(pairs)]
        upd = [_bmm_tn(_chunks(v_b[:, lanes]), _chunks(ke_b[:, lanes])) for lanes in pairs]
        entering = []
        for u, lanes in enumerate(pairs):
            cur, states = st[u], []
            for n in range(nch):
                states.append(cur)
                cur = gt["decay"][n][:, lanes] * cur + jnp.where(bd, upd[u][n], 0.0)
            st[u] = cur
            entering.append(jnp.stack(states))
            st_ref[u] = entering[u]
        for u, lanes in enumerate(pairs):
            o_ref[:, lanes] = intra[u] + _bmm_nt(_chunks(qd_b[:, lanes]), entering[u].astype(BF16)).reshape(TH, LANE)

    wide = HG_PAIRS * LANE
    col = lambda base: pl.BlockSpec((TH, wide), lambda p, i: (i, base // wide + p))
    return pl.pallas_call(
        body, name="hgrn_fwd", grid=(NH // 2 // HG_PAIRS, s // TH),
        in_specs=[col(P_HQ), col(P_HF), col(P_HI), pl.BlockSpec((2, wide), lambda p, i: (0, p))],
        out_specs=[pl.BlockSpec((TH, wide), lambda p, i: (i, p)),
                   pl.BlockSpec((HG_PAIRS, nch, LANE, LANE), lambda p, i: (p, i, 0, 0))],
        out_shape=[jax.ShapeDtypeStruct((s, 512), F32), jax.ShapeDtypeStruct((NH // 2, s // HG_BLOCK, LANE, LANE), F32)],
        scratch_shapes=[pltpu.VMEM((HG_PAIRS, LANE, LANE), F32)],
        compiler_params=_params(("parallel", "arbitrary")),
    )(proj, proj, proj, lbl)


def _group_sum(v):
    low = lax.broadcasted_iota(jnp.int32, (v.shape[0], LANE), 1) < V_DIM
    blocks = []
    for b in range(v.shape[1] // LANE):
        blk = v[:, b * LANE:(b + 1) * LANE]
        s_low = jnp.sum(jnp.where(low, blk, 0.0), axis=-1, keepdims=True)
        s_high = jnp.sum(jnp.where(low, 0.0, blk), axis=-1, keepdims=True)
        blocks.append(jnp.where(low, s_low, s_high))
    return jnp.concatenate(blocks, axis=1)


def _dsilu(z, sg):
    return sg * (1.0 + z * (1.0 - sg))


def _mid(proj, attn, o_raw, x, tgt, g_hg, b_gate, g_post, wa, wb, w_out):
    s = x.shape[0]

    def body(attn_ref, ga_ref, o_ref, gb_ref, mg_ref, x_ref, t_ref, ghg_ref, bg_ref, gp_ref, wa_ref, wb_ref, wo_ref,
             loss_ref, dout_ref, dattn_ref, dga_ref, dor_ref, dgb_ref, dmg_ref, dwo_out, dwa_out, dwb_out, dgp_ref, dbg_ref, dghg_ref,
             dwo_ref, dwa_ref, dwb_ref):
        @pl.when(pl.program_id(0) == 0)
        def _():
            for rf in (loss_ref, dwo_ref, dwa_ref, dwb_ref, dgp_ref, dbg_ref, dghg_ref):
                rf[...] = jnp.zeros_like(rf)

        attn, za, orw, zb = attn_ref[...], ga_ref[...], o_ref[...], gb_ref[...]
        ghg, gp = ghg_ref[...], gp_ref[...]
        sga, sgb = _sigmoid(za), _sigmoid(zb)
        sa, sb = za * sga, zb * sgb
        ga = attn * sa
        rh = lax.rsqrt(_group_sum(orw * orw) * (1.0 / V_DIM) + EPS)
        on = (orw * rh) * ghg
        gb = on * sb
        ga_b, gb_b = ga.astype(BF16), gb.astype(BF16)
        blocks = [slice(t * (D // N_CHIPS), (t + 1) * (D // N_CHIPS)) for t in range(N_CHIPS)]
        ya = jnp.concatenate([_mm(ga_b, wa_ref[t]) for t in range(N_CHIPS)], axis=1)
        yb = jnp.concatenate([_mm(gb_b, wb_ref[t]) for t in range(N_CHIPS)], axis=1)
        gates = _sigmoid(mg_ref[...] + bg_ref[...])
        g0, g1 = gates[:, :D], gates[:, D:]
        m_b = (g0 * ya + g1 * yb).astype(BF16)
        y = _mm(m_b, wo_ref[...])
        ry = lax.rsqrt(jnp.mean(y * y, axis=-1, keepdims=True) + EPS)
        out = x_ref[...] + (y * ry) * gp
        err = out - t_ref[...]
        loss_ref[...] += 0.5 * jnp.sum(jnp.mean(err * err, axis=-1, keepdims=True), axis=0, keepdims=True)
        dout = err * (1.0 / D)
        dout_ref[...] = dout
        dgp_ref[...] += jnp.sum(dout * (y * ry), axis=0, keepdims=True)
        dgy = dout * gp
        dy = ry * dgy - y * (ry * ry * ry) * jnp.mean(y * dgy, axis=-1, keepdims=True)
        dy_b = dy.astype(BF16)
        dm = _mm_nt(dy_b, wo_ref[...])
        dya_b, dyb_b = (dm * g0).astype(BF16), (dm * g1).astype(BF16)
        dga = sum(_mm_nt(dya_b[:, cols], wa_ref[t]) for t, cols in enumerate(blocks))
        dgb = sum(_mm_nt(dyb_b[:, cols], wb_ref[t]) for t, cols in enumerate(blocks))
        dwo_ref[...] += _mm_tn(m_b, dy_b)
        for t, cols in enumerate(blocks):
            dwa_ref[t] += _mm_tn(ga_b, dya_b[:, cols])
            dwb_ref[t] += _mm_tn(gb_b, dyb_b[:, cols])
        dg0, dg1 = dm * ya, dm * yb
        dmg = jnp.concatenate([dg0 * g0 * (1.0 - g0), dg1 * g1 * (1.0 - g1)], axis=1)
        dmg_ref[...] = dmg.astype(BF16)
        dbg_ref[...] += jnp.sum(dmg, axis=0, keepdims=True)
        dattn_ref[...] = dga * sa
        dga_ref[...] = (dga * attn * _dsilu(za, sga)).astype(BF16)
        dgb_ref[...] = (dgb * on * _dsilu(zb, sgb)).astype(BF16)
        don = dgb * sb
        dghg_ref[...] += jnp.sum(don * (orw * rh), axis=0, keepdims=True)
        dgo = don * ghg
        dor_ref[...] = rh * dgo - orw * (rh * rh * rh) * (_group_sum(orw * dgo) * (1.0 / V_DIM))

        @pl.when(pl.program_id(0) == pl.num_programs(0) - 1)
        def _():
            for out, rf in ((dwo_out, dwo_ref), (dwa_out, dwa_ref), (dwb_out, dwb_ref)):
                out[...] = rf[...].astype(BF16)

    row = lambda w, j=0: pl.BlockSpec((TM_MID, w), lambda i: (i, j))
    full = lambda a: pl.BlockSpec(a.shape, lambda i: (0,) * a.ndim)
    acc = lambda shape: pl.BlockSpec(shape, lambda i: (0,) * len(shape))
    slabs = (N_CHIPS, 512, D // N_CHIPS)
    sds = jax.ShapeDtypeStruct
    return pl.pallas_call(
        body, name="mid", grid=(s // TM_MID,),
        in_specs=[row(512), row(512, P_GA // 512), row(512), row(512, P_GB // 512), row(2048, P_MERGE // 2048), row(D), row(D),
                  full(g_hg), full(b_gate), full(g_post), full(wa), full(wb), full(w_out)],
        out_specs=[acc((1, 1)), row(D), row(512), row(512), row(512), row(512), row(2048),
                   acc((D, D)), acc(slabs), acc(slabs), acc((1, D)), acc((1, 2048)), acc((1, 512))],
        out_shape=[sds((1, 1), F32), sds((s, D), F32), sds((s, 512), F32), sds((s, 512), BF16), sds((s, 512), F32), sds((s, 512), BF16),
                   sds((s, 2048), BF16), sds((D, D), BF16), sds(slabs, BF16), sds(slabs, BF16), sds((1, D), F32),
                   sds((1, 2048), F32), sds((1, 512), F32)],
        scratch_shapes=[pltpu.VMEM((D, D), F32), pltpu.VMEM(slabs, F32), pltpu.VMEM(slabs, F32)],
        compiler_params=_params(("arbitrary",)),
    )(attn, proj, o_raw, proj, proj, x, tgt, g_hg, b_gate, g_post, wa, wb, w_out)


def _attn_bwd(q, k, vv, attn, dattn, lse, token):
    s = q.shape[1]
    nt = s // TQ
    scale = 1.0 / math.sqrt(QK_NOPE + QK_ROPE)

    def body(q_ref, k_ref, v_ref, o_ref, do_ref, lse_ref, token_ref, dq_ref, dk_ref, dv_ref, do_s, delta_s):
        j = pl.program_id(1)

        @pl.when(j == 0)
        def _():
            dq_ref[...] = jnp.zeros_like(dq_ref)
            lane = lax.broadcasted_iota(jnp.int32, (TQ, LANE), 1)

            @pl.loop(0, nt)
            def _(i):
                rows = pl.ds(pl.multiple_of(i * TQ, TQ), TQ)
                do, o = do_ref[rows, :], o_ref[rows, :]
                for hh in range(2):
                    doh = jnp.where((lane < 64) if hh == 0 else (lane >= 64), do, 0.0)
                    do_s[hh, rows, :] = doh.astype(BF16)
                    delta_s[hh, rows, :] = jnp.broadcast_to(jnp.sum(doh * o, axis=-1, keepdims=True), (TQ, LANE))

        kjs, vjs = (k_ref[0], k_ref[1]), (v_ref[0], v_ref[1])

        def tile(hh, start, size, kj, vj, diag):
            rows = pl.ds(pl.multiple_of(start, size), size)
            wide = lambda a: jnp.concatenate([a] * (kj.shape[0] // LANE), axis=1)
            qi, do_b = q_ref[hh, rows, :], do_s[hh, rows, :]
            sc, dp = _mm_nt(qi, kj), _mm_nt(do_b, vj)
            p = jnp.exp2(sc * QK_SCALE2 - wide(lse_ref[hh, rows, :]))
            if diag:
                p = jnp.where(_diag_visible(size), p, 0.0)
            ds_b = (p * (dp - wide(delta_s[hh, rows, :]))).astype(BF16)
            dv, dk = _mm_tn(do_b, p.astype(BF16)), _mm_tn(qi, ds_b)
            dq_ref[hh, rows, :] += _mm(ds_b, kj)
            return dk, dv

        def step(i, carry):
            new = [tile(hh, i * TQ, TQ, kjs[hh], vjs[hh], False) for hh in range(2)]
            return tuple((carry[hh][0] + new[hh][0], carry[hh][1] + new[hh][1]) for hh in range(2))

        def diagonal(hh):
            k0, k1, v0, v1 = kjs[hh][:HQ], kjs[hh][HQ:], vjs[hh][:HQ], vjs[hh][HQ:]
            a = tile(hh, j * TQ, HQ, k0, v0, True)
            b = tile(hh, j * TQ + HQ, HQ, k0, v0, False)
            c = tile(hh, j * TQ + HQ, HQ, k1, v1, True)
            return jnp.concatenate([a[0] + b[0], c[0]], axis=1), jnp.concatenate([a[1] + b[1], c[1]], axis=1)

        carry = lax.fori_loop(j + 1, nt, step, (diagonal(0), diagonal(1)))
        for hh in range(2):
            dk_ref[hh] = carry[hh][0].T * scale
            dv_ref[hh] = carry[hh][1].T

        @pl.when(j == nt - 1)
        def _():
            dq_ref[...] = dq_ref[...] * scale

    whole = pl.BlockSpec((2, s, LANE), lambda p, j: (p, 0, 0))
    tile_spec = pl.BlockSpec((2, TQ, LANE), lambda p, j: (p, j, 0))
    cols = pl.BlockSpec((s, LANE), lambda p, j: (0, p))
    hs = jax.ShapeDtypeStruct((NH, s, LANE), F32)
    return pl.pallas_call(
        body, name="attn_bwd", grid=(NH // 2, nt),
        in_specs=[whole, tile_spec, tile_spec, cols, cols, whole, pl.BlockSpec((8, LANE), lambda p, j: (0, 0))],
        out_specs=[whole, tile_spec, tile_spec],
        out_shape=[hs, hs, hs],
        scratch_shapes=[pltpu.VMEM((2, s, LANE), BF16), pltpu.VMEM((2, s, LANE), F32)],
        compiler_params=_params(("parallel", "arbitrary")),
    )(q, k, vv, attn, dattn, lse, token)


def _hgrn_bwd(proj, lbl, states, do_raw):
    s = proj.shape[0]
    nt = s // TH
    nch = TH // HG_BLOCK

    def body(hq_ref, hf_ref, hi_ref, lbl_ref, st_ref, do_ref, dh_ref, dlbl_ref, dst, dlb):
        step = pl.program_id(1)

        @pl.when(step == 0)
        def _():
            dst[...] = jnp.zeros_like(dst)
            dlb[...] = jnp.zeros_like(dlb)

        m0, bd = _pair_masks()
        lb = _lower_bound(lbl_ref[...])
        gt = _hgrn_gates(hq_ref[...], hf_ref[...], lb)
        do = do_ref[...]
        qd, ki, ke = gt["qd"], gt["ki"], gt["ke"]
        v_b, do_b = hi_ref[...].astype(BF16), do.astype(BF16)
        qd_b, ki_b, ke_b = qd.astype(BF16), ki.astype(BF16), ke.astype(BF16)
        pairs = [slice(u * LANE, (u + 1) * LANE) for u in range(HG_PAIRS)]
        heads = [(lanes, m0 if hh == 0 else jnp.logical_not(m0)) for lanes in pairs for hh in range(2)]
        a_b = [jnp.where(gt["tri"], _mm_nt(jnp.where(mh, qd[:, lanes], 0.0).astype(BF16), ki_b[:, lanes]), 0.0).astype(BF16)
               for lanes, mh in heads]
        doh_b = [jnp.where(mh, do[:, lanes], 0.0).astype(BF16) for lanes, mh in heads]
        da_b = [jnp.where(gt["tri"], _mm_nt(d, v_b[:, lanes]), 0.0).astype(BF16) for d, (lanes, _) in zip(doh_b, heads)]
        dv_p, dqd_p, dki_p = [], [], []
        for u, lanes in enumerate(pairs):
            e, o = 2 * u, 2 * u + 1
            dv_p.append(_mm_tn(a_b[e], doh_b[e]) + _mm_tn(a_b[o], doh_b[o]))
            dqd_p.append(jnp.where(m0, _mm(da_b[e], ki_b[:, lanes]), _mm(da_b[o], ki_b[:, lanes])))
            dki_p.append(jnp.where(m0, _mm_tn(da_b[e], qd_b[:, lanes]), _mm_tn(da_b[o], qd_b[:, lanes])))
        fed = [_bmm_tn(_chunks(do_b[:, lanes]), _chunks(qd_b[:, lanes])) for lanes in pairs]
        leaving = []
        for u, lanes in enumerate(pairs):
            ds, left = dst[u], [None] * nch
            for n in reversed(range(nch)):
                left[n] = ds
                ds = gt["decay"][n][:, lanes] * ds + jnp.where(bd, fed[u][n], 0.0)
            dst[u] = ds
            leaving.append(jnp.stack(left))
        dke_p, dlast_p = [], []
        for u, lanes in enumerate(pairs):
            entering, leaving_b = st_ref[u], leaving[u].astype(BF16)
            dke3 = _bmm_nn(_chunks(v_b[:, lanes]), leaving_b)
            dv_p[u] = dv_p[u] + _bmm_nt(_chunks(ke_b[:, lanes]), leaving_b).reshape(TH, LANE)
            dqd_p[u] = dqd_p[u] + _bmm_nn(_chunks(do_b[:, lanes]), entering.astype(BF16)).reshape(TH, LANE)
            dke_p.append(dke3.reshape(TH, LANE))
            dlast_p.append(jnp.sum(dke3 * _chunks(ke[:, lanes]), axis=1, keepdims=True)
                           + jnp.sum(leaving[u] * entering, axis=1, keepdims=True) * gt["decay"][:, :, lanes])
        cat = lambda parts: jnp.concatenate(parts, axis=-1)
        dv, dqd, dki, dke, dlast = cat(dv_p), cat(dqd_p), cat(dki_p), cat(dke_p), cat(dlast_p)
        dk = dki * gt["ei"] + dke * gt["ee"]
        dcum = dqd * qd - dki * ki - dke * ke
        dg = _chunk_cumsum(dcum, reverse=True) + jnp.broadcast_to(dlast, (nch, HG_BLOCK, dlast.shape[-1])).reshape(dcum.shape)
        sig = gt["sig"]
        df = dg / gt["f"] - dk
        dlb[...] += jnp.sum(df * (1.0 - sig), axis=0, keepdims=True)
        dh_ref[0] = (dqd * gt["e"]).astype(BF16)
        dh_ref[1] = ((df * (1.0 - lb)) * sig * (1.0 - sig)).astype(BF16)
        dh_ref[2] = dv.astype(BF16)

        @pl.when(step == nt - 1)
        def _():
            lb = _lower_bound(lbl_ref[...])
            da0 = dlb[...] * lb * (1.0 - lb)
            dlbl_ref[...] = jnp.concatenate([da0, -da0], axis=0)

    wide = HG_PAIRS * LANE
    col = lambda base: pl.BlockSpec((TH, wide), lambda p, i: (nt - 1 - i, base // wide + p))
    tile = pl.BlockSpec((TH, wide), lambda p, i: (nt - 1 - i, p))
    sds = jax.ShapeDtypeStruct
    return pl.pallas_call(
        body, name="hgrn_bwd", grid=(NH // 2 // HG_PAIRS, nt),
        in_specs=[col(P_HQ), col(P_HF), col(P_HI), pl.BlockSpec((2, wide), lambda p, i: (0, p)),
                  pl.BlockSpec((HG_PAIRS, nch, LANE, LANE), lambda p, i: (p, nt - 1 - i, 0, 0)), tile],
        out_specs=[pl.BlockSpec((3, TH, wide), lambda p, i: (0, nt - 1 - i, p)), pl.BlockSpec((2, wide), lambda p, i: (0, p))],
        out_shape=[sds((3, s, 512), BF16), sds((2, 512), F32)],
        scratch_shapes=[pltpu.VMEM((HG_PAIRS, LANE, LANE), F32), pltpu.VMEM((1, wide), F32)],
        compiler_params=_params(("parallel", "arbitrary")),
    )(proj, proj, proj, lbl, states, do_raw)


def _norm_rows_bwd(v, r, g, dn):
    dgv = dn * g
    return r * dgv - v * (r * r * r) * jnp.mean(v * dgv, axis=-1, keepdims=True)


def _qkv_bwd(proj, dq, dk, dvv, g_q, g_kv, w_uq_p, w_k_p, w_v_p, rc, rs1, rs2):
    s = proj.shape[0]
    head_q = QK_NOPE + QK_ROPE

    def body(cq_ref, ckv_ref, dq_ref, dk_ref, dv_ref, gq_ref, gkv_ref, wq_ref, wk_ref, wv_ref, c_ref, s1_ref, s2_ref,
             dcq_ref, dckv_ref, dkpe_ref, dwq_out, dwkv_out, dgq_ref, dgkv_ref, dwq_ref, dwk_ref, dwv_ref):
        @pl.when(pl.program_id(0) == 0)
        def _():
            for rf in (dwq_ref, dwk_ref, dwv_ref, dgq_ref, dgkv_ref):
                rf[...] = jnp.zeros_like(rf)

        c, s1, s2 = c_ref[...], s1_ref[...], s2_ref[...]
        cq, ckv = cq_ref[...], ckv_ref[...]
        gq, gkv = gq_ref[...], gkv_ref[...]
        cqn, rq = _norm_rows(cq, gq)
        ckvn, rkv = _norm_rows(ckv, gkv)
        cqn_b, ckvn_b = cqn.astype(BF16), ckvn.astype(BF16)
        dqf = jnp.concatenate([_rope_t(dq_ref[h], c, s1, s2) for h in range(NH)], axis=1).astype(BF16)
        dkf = jnp.concatenate([dk_ref[h] for h in range(NH)], axis=1).astype(BF16)
        dvf = jnp.concatenate([dv_ref[h] for h in range(NH)], axis=1).astype(BF16)
        dkpe = dk_ref[0]
        for h in range(1, NH):
            dkpe = dkpe + dk_ref[h]
        lane = lax.broadcasted_iota(jnp.int32, (TM, LANE), 1)
        dkpe = jnp.where((lane >= QK_NOPE) & (lane < QK_NOPE + QK_ROPE), dkpe, 0.0)
        dkpe_ref[...] = _rope_t(dkpe, c, s1, s2).astype(BF16)
        dcqn = _mm_nt(dqf, wq_ref[...])
        pair = lambda a, t: a[:, t * 2 * LANE:(t + 1) * 2 * LANE]
        dckvn = sum(_mm_nt(pair(dkf, t), wk_ref[t]) + _mm_nt(pair(dvf, t), wv_ref[t]) for t in range(N_CHIPS))
        dwq_ref[...] += _mm_tn(cqn_b, dqf)
        dwk_ref[...] += _mm_tn(ckvn_b, dkf)
        dwv_ref[...] += _mm_tn(ckvn_b, dvf)
        dgq_ref[...] += jnp.sum(dcqn * (cq * rq), axis=0, keepdims=True)
        dgkv_ref[...] += jnp.sum(dckvn * (ckv * rkv), axis=0, keepdims=True)
        dcq_ref[...] = _norm_rows_bwd(cq, rq, gq, dcqn).astype(BF16)
        dckv_ref[...] = _norm_rows_bwd(ckv, rkv, gkv, dckvn).astype(BF16)

        @pl.when(pl.program_id(0) == pl.num_programs(0) - 1)
        def _():
            blk = lambda ref, h: ref[:, h * LANE:(h + 1) * LANE]
            lane = lax.broadcasted_iota(jnp.int32, (Q_LORA, LANE), 1)
            for j in range(NH * head_q // LANE):
                h0, w0 = divmod(j * LANE, head_q)
                first = blk(dwq_ref, h0) if w0 == 0 else pltpu.roll(blk(dwq_ref, h0), LANE - w0, 1)
                second = pltpu.roll(blk(dwq_ref, h0 + 1), head_q - w0, 1)
                dwq_out[:, j * LANE:(j + 1) * LANE] = jnp.where(lane < head_q - w0, first, second).astype(BF16)
            lane = lax.broadcasted_iota(jnp.int32, (KV_LORA, LANE), 1)
            for h in range(NH):
                vals = blk(dwv_ref, h) if h % 2 else pltpu.roll(blk(dwv_ref, h), V_DIM, 1)
                both = jnp.where(lane < QK_NOPE, blk(dwk_ref, h), vals).astype(BF16)
                dwkv_out[h // 2, :, (h % 2) * LANE:(h % 2 + 1) * LANE] = both

    row = lambda w, j=0: pl.BlockSpec((TM, w), lambda i: (i, j))
    full = lambda a: pl.BlockSpec(a.shape, lambda i: (0,) * a.ndim)
    acc = lambda *shape: pl.BlockSpec(shape, lambda i: (0,) * len(shape))
    heads = pl.BlockSpec((NH, TM, LANE), lambda i: (0, i, 0))
    sds = jax.ShapeDtypeStruct
    return pl.pallas_call(
        body, name="qkv_bwd", grid=(s // TM,),
        in_specs=[row(Q_LORA, P_CQ // Q_LORA), row(KV_LORA, P_CKV // KV_LORA), heads, heads, heads,
                  full(g_q), full(g_kv), full(w_uq_p), full(w_k_p), full(w_v_p), row(LANE), row(LANE), row(LANE)],
        out_specs=[row(Q_LORA), row(KV_LORA), row(LANE), acc(Q_LORA, NH * head_q), acc(NH // 2, KV_LORA, 2 * LANE),
                   acc(1, Q_LORA), acc(1, KV_LORA)],
        out_shape=[sds((s, Q_LORA), BF16), sds((s, KV_LORA), BF16), sds((s, LANE), BF16), sds((Q_LORA, NH * head_q), BF16),
                   sds((NH // 2, KV_LORA, 2 * LANE), BF16), sds((1, Q_LORA), F32), sds((1, KV_LORA), F32)],
        scratch_shapes=[pltpu.VMEM((Q_LORA, D), F32), pltpu.VMEM((KV_LORA, D), F32), pltpu.VMEM((KV_LORA, D), F32)],
        compiler_params=_params(("arbitrary",)),
    )(proj, proj, dq, dk, dvv, g_q, g_kv, w_uq_p, w_k_p, w_v_p, rc, rs1, rs2)


def _front_bwd(x, dout, dmg, dga, dh3, dgb, dcq, dckv, dkpe, g_pre, w_in_t, w_kpe, token):
    s = x.shape[0]

    def body(x_ref, do_ref, dmg_ref, dga_ref, dh3_ref, dgb_ref, dcq_ref, dckv_ref, dkpe_ref, g_ref, w_ref, k_ref, token_ref,
             gx_ref, dg_ref):
        @pl.when(pl.program_id(0) == 0)
        def _():
            dg_ref[...] = jnp.zeros_like(dg_ref)

        xv, g = x_ref[...], g_ref[...]
        _, r = _norm_rows(xv, g)
        pieces = ((dmg_ref[...], O_MERGE), (dga_ref[...], O_GA), (dh3_ref[0], O_HQ), (dh3_ref[1], O_HF), (dh3_ref[2], O_HI),
                  (dgb_ref[...], O_GB), (dcq_ref[...], O_CQ), (dckv_ref[...], O_CKV))
        dh = _mm(dkpe_ref[...], k_ref[...])
        for piece, off in pieces:
            dh = dh + _mm(piece, w_ref[off:off + piece.shape[1], :])
        dg_ref[...] += jnp.sum(dh * (xv * r), axis=0, keepdims=True)
        gx_ref[...] = do_ref[...] + _norm_rows_bwd(xv, r, g, dh)

    row = lambda w: pl.BlockSpec((TM, w), lambda i: (i, 0))
    full = lambda a: pl.BlockSpec(a.shape, lambda i: (0,) * a.ndim)
    sds = jax.ShapeDtypeStruct
    return pl.pallas_call(
        body, name="front_bwd", grid=(s // TM,),
        in_specs=[row(D), row(D), row(2048), row(512), pl.BlockSpec((3, TM, 512), lambda i: (0, i, 0)), row(512), row(Q_LORA),
                  row(KV_LORA), row(LANE), full(g_pre), full(w_in_t), full(w_kpe), pl.BlockSpec(memory_space=pl.ANY)],
        out_specs=[row(D), pl.BlockSpec((1, D), lambda i: (0, 0))],
        out_shape=[sds((s, D), F32), sds((1, D), F32)],
        compiler_params=_params(("arbitrary",)),
    )(x, dout, dmg, dga, dh3, dgb, dcq, dckv, dkpe, g_pre, w_in_t, w_kpe, token)


TK_GRAD = 1024


def _win_grad(h, pieces, name):
    s = h.shape[0]
    n = len(pieces)

    def body(h_ref, *refs):
        d_refs, o_refs, sums = refs[:n], refs[n:2 * n], refs[2 * n:]

        @pl.when(pl.program_id(0) == 0)
        def _():
            for s_ref in sums:
                s_ref[...] = jnp.zeros_like(s_ref)

        hv = h_ref[...]
        for d_ref, s_ref in zip(d_refs, sums):
            if len(d_ref.shape) == 3:
                for k in range(d_ref.shape[0]):
                    s_ref[k] += _mm_tn(d_ref[k], hv)
            else:
                s_ref[...] += _mm_tn(d_ref[...], hv)

        @pl.when(pl.program_id(0) == pl.num_programs(0) - 1)
        def _():
            for o_ref, s_ref in zip(o_refs, sums):
                o_ref[...] = s_ref[...].astype(BF16)

    def in_spec(p):
        if p.ndim == 3:
            return pl.BlockSpec((p.shape[0], TK_GRAD, p.shape[2]), lambda kk: (0, kk, 0))
        return pl.BlockSpec((TK_GRAD, p.shape[1]), lambda kk: (kk, 0))

    out_shapes = [(p.shape[0], p.shape[2], D) if p.ndim == 3 else (p.shape[1], D) for p in pieces]
    return pl.pallas_call(
        body, name=name, grid=(s // TK_GRAD,),
        in_specs=[pl.BlockSpec((TK_GRAD, D), lambda kk: (kk, 0))] + [in_spec(p) for p in pieces],
        out_specs=[pl.BlockSpec(sh, lambda kk, nd=len(sh): (0,) * nd) for sh in out_shapes],
        out_shape=[jax.ShapeDtypeStruct(sh, BF16) for sh in out_shapes],
        scratch_shapes=[pltpu.VMEM(sh, F32) for sh in out_shapes],
        compiler_params=_params(("arbitrary",)),
    )(h, *pieces)


QKV_ROWS = Q_LORA + KV_LORA + QK_ROPE


def _win_grad_qkv(h, dcq, dckv, dkpe):
    s = h.shape[0]
    half = QKV_ROWS // 2

    def body(h_ref, cq_ref, ckv_ref, kpe_ref, o_ref):
        @pl.when(pl.program_id(0) == 0)
        def _():
            o_ref[...] = jnp.zeros_like(o_ref)

        hv = h_ref[...]
        g_cq = _mm_tn(cq_ref[...], hv)
        o_ref[0] += g_cq[:half]
        o_ref[1, 0:Q_LORA - half] += g_cq[half:]
        o_ref[1, Q_LORA - half:Q_LORA + KV_LORA - half] += _mm_tn(ckv_ref[...], hv)
        o_ref[1, Q_LORA + KV_LORA - half:] += _mm_tn(kpe_ref[...], hv)[QK_NOPE:QK_NOPE + QK_ROPE]

    rows = lambda a: pl.BlockSpec((TK_GRAD, a.shape[1]), lambda kk: (kk, 0))
    return pl.pallas_call(
        body, name="win_grad_qkv", grid=(s // TK_GRAD,), in_specs=[rows(h), rows(dcq), rows(dckv), rows(dkpe)],
        out_specs=pl.BlockSpec((2, half, D), lambda kk: (0, 0, 0)), out_shape=jax.ShapeDtypeStruct((2, half, D), F32),
        compiler_params=_params(("arbitrary",)),
    )(h, dcq, dckv, dkpe)


def _pad_wuq(w_uq):
    rows = w_uq.shape[0]
    w = w_uq.reshape(rows, NH, QK_NOPE + QK_ROPE)
    return jnp.pad(w, ((0, 0), (0, 0), (0, LANE - QK_NOPE - QK_ROPE))).reshape(rows, NH * LANE)


def _pad_wukv(w_ukv):
    heads = w_ukv.shape[1] // (QK_NOPE + V_DIM)
    w = w_ukv.reshape(KV_LORA, heads, QK_NOPE + V_DIM)
    w_k = jnp.pad(w[:, :, :QK_NOPE], ((0, 0), (0, 0), (0, LANE - QK_NOPE))).reshape(KV_LORA, heads * LANE)
    wv = w[:, :, QK_NOPE:].reshape(KV_LORA, heads // 2, 2, 1, V_DIM)
    eye = jnp.eye(2, dtype=w.dtype).reshape(1, 1, 2, 2, 1)
    return w_k, (wv * eye).reshape(KV_LORA, heads * LANE)


def _local_step(x, tgt, g_pre, w_in_t, b_gate, g_q, g_kv, lb_logits, g_hgrn, g_post, weights, exchange=None):
    s = x.shape[0]
    w_kpe = _kpe_block(w_in_t)
    rc, rs1, rs2 = _rope_tables(s)
    g_hg = jnp.tile(g_hgrn, (1, NH))

    proj, h = _front_fwd(x, g_pre, w_in_t, w_kpe, weights.tokens)
    w_uq_p, w_k_p, w_v_p = weights.qkv(h)
    q, k, vv = _qkv_fwd(proj, g_q, g_kv, w_uq_p, w_k_p, w_v_p, rc, rs1, rs2)
    attn, lse = _attn_fwd(q, k, vv)
    o_raw, states = _hgrn_fwd(proj, lb_logits)
    wa, wb, w_out = weights.mid(o_raw)
    (loss, dout, dattn, dga, dor, dgb, dmg, d_wout, d_wa, d_wb, d_gpost, d_bgate, d_ghg) = _mid(
        proj, attn, o_raw, x, tgt, g_hg, b_gate, g_post, wa, wb, w_out)
    w_mg, w_ga, w_gb = _win_grad(h, [dmg, dga, dgb], "win_grad_mid")
    dh3, d_lbl = _hgrn_bwd(proj, lb_logits, states, dor)
    (w_h3,) = _win_grad(h, [dh3], "win_grad_hgrn")
    d_win_rest = jnp.concatenate([w_ga, w_h3[0], w_h3[1], w_h3[2], w_gb, w_mg], axis=0)
    early = dict(w_in_rest=d_win_rest, w_branch_a=d_wa, w_branch_b=d_wb, w_out=d_wout)
    token = exchange.start_early(early) if exchange else jnp.zeros((8, LANE), F32)
    dq, dk, dvv = _attn_bwd(q, k, vv, attn, dattn, lse, token)
    dcq, dckv, dkpe, d_wuq, d_wukv, d_gq, d_gkv = _qkv_bwd(proj, dq, dk, dvv, g_q, g_kv, w_uq_p, w_k_p, w_v_p, rc, rs1, rs2)
    late = dict(w_in_qkv=_win_grad_qkv(h, dcq, dckv, dkpe), w_uq=d_wuq, w_ukv=d_wukv)
    token = exchange.start_late(late) if exchange else jnp.zeros((8, LANE), F32)
    grad_x, d_gpre = _front_bwd(x, dout, dmg, dga, dh3, dgb, dcq, dckv, dkpe, g_pre, w_in_t, w_kpe, token)
    vec_grads = dict(g_pre=d_gpre, b_gate=d_bgate, g_q=d_gq, g_kv=d_gkv, lb_logits=d_lbl, g_hgrn=d_ghg, g_post=d_gpost)
    return loss, grad_x, dict(early, **late), vec_grads


SHARD_SHAPES = (("w_in", (1416, 1024)), ("w_uq", (192, 768)), ("w_ukv", (256, 256)), ("w_branch_a", (512, 256)),
                ("w_branch_b", (512, 256)), ("w_out", (256, 1024)))
BIG = tuple(n for n, _ in SHARD_SHAPES)
ROW_SHARDED = ("w_in", "w_uq", "w_out")
N_CHIPS = 4
W_IN_FORWARD_CUT = 704


def _to_block(name, a):
    return a[0].T if name == "w_in" else a[0]


def _from_block(name, a):
    return a.T[None] if name == "w_in" else a[None]
VEC_ROWS = (("g_pre", 0, 1024), ("b_gate", 1, 2048), ("g_q", 2, 768), ("g_kv", 3, 256), ("g_hgrn", 6, 64), ("g_post", 7, 1024))
VEC_LB_ROW = 4
VEC_SHAPE = (8, 2048)


def _split_by_chip(name, g):
    a, b = dict(SHARD_SHAPES)[name]
    return g.reshape(N_CHIPS, a, b) if name in ROW_SHARDED else g.reshape(a, N_CHIPS, b).transpose(1, 0, 2)


def _join_chips(name, w):
    a, b = dict(SHARD_SHAPES)[name]
    return w.reshape(N_CHIPS * a, b) if name in ROW_SHARDED else w.transpose(1, 0, 2).reshape(a, N_CHIPS * b)


MESH = pl.DeviceIdType.MESH
HBM = pl.BlockSpec(memory_space=pltpu.HBM)


def _mesh_place():
    x, y, c = lax.axis_index("x"), lax.axis_index("y"), lax.axis_index("c")
    return x, y, c, 2 * x + y, [(1 - x, y), (x, 1 - y), (1 - x, 1 - y)]


def _remote(src, dst, send_sems, recv_sems, k, to):
    return pltpu.make_async_remote_copy(src_ref=src, dst_ref=dst, send_sem=send_sems.at[k], recv_sem=recv_sems.at[k],
                                        device_id=to, device_id_type=MESH)


def _gather_w_in(shard):
    a, b = shard.shape
    cut = W_IN_FORWARD_CUT

    def body(src, out, ici_send, ici_recv, d2d_send, d2d_recv, local_sem):
        x, y, c = lax.axis_index("x"), lax.axis_index("y"), lax.axis_index("c")
        me, xn, yn, dg = 2 * x + y, 2 * (1 - x) + y, 2 * x + (1 - y), 2 * (1 - x) + (1 - y)
        to_x, to_y, sibling = (1 - x, y, c), (x, 1 - y, c), (x, y, 1 - c)
        first, rest = pl.ds(0, cut), pl.ds(cut, a - cut)

        whole = lambda ref, which: ref.at[:, pl.ds(pl.multiple_of(which * (b // 2), b // 2), b // 2)]
        own = pltpu.make_async_copy(src, out.at[me], local_sem)
        own.start()
        sends = [_remote(whole(src, c), whole(out.at[me], c), ici_send, ici_recv, 0, to_x),
                 _remote(whole(src, c), whole(out.at[me], c), ici_send, ici_recv, 1, to_y)]
        for cp in sends:
            cp.start()

        def landed(slot, rows, k, d2d_k, src_dev):
            piece = whole(out.at[slot], c) if rows is None else out.at[slot].at[rows, pl.ds(pl.multiple_of(c * (b // 2), b // 2), b // 2)]
            _remote(piece, piece, ici_send, ici_recv, k, src_dev).wait_recv()
            cp = _remote(piece, piece, d2d_send, d2d_recv, d2d_k, sibling)
            cp.start()
            sends.append(cp)
            return piece

        def pass_on(slot, rows, k, to):
            piece = out.at[slot].at[rows, pl.ds(pl.multiple_of(c * (b // 2), b // 2), b // 2)]
            cp = _remote(piece, piece, ici_send, ici_recv, k, to)
            cp.start()
            sends.append(cp)

        landed(xn, None, 0, 0, to_x)
        pass_on(xn, first, 2, to_y)
        landed(yn, None, 1, 1, to_y)
        pass_on(yn, rest, 3, to_x)
        landed(dg, first, 2, 2, to_y)
        landed(dg, rest, 3, 3, to_x)
        other = pl.ds(pl.multiple_of((1 - c) * (b // 2), b // 2), b // 2)
        for d2d_k, (slot, rows) in enumerate(((xn, None), (yn, None), (dg, first), (dg, rest))):
            piece = out.at[slot].at[:, other] if rows is None else out.at[slot].at[rows, other]
            _remote(piece, piece, d2d_send, d2d_recv, d2d_k, sibling).wait_recv()
        for cp in sends:
            cp.wait_send()
        own.wait()

    sems = pltpu.SemaphoreType.DMA((4,))
    return pl.pallas_call(
        body, name="gather_w_in", in_specs=[HBM], out_specs=HBM,
        out_shape=jax.ShapeDtypeStruct((N_CHIPS, a, b), shard.dtype),
        scratch_shapes=[sems, sems, sems, sems, pltpu.SemaphoreType.DMA],
        compiler_params=pltpu.CompilerParams(has_side_effects=True),
    )(shard)


def _sibling_exchange(srcs, name, after=None, halves=False):
    n = len(srcs)
    extra = [] if after is None else [after]

    def body(*refs):
        src_refs, outs = refs[:n], refs[n + len(extra):2 * n + len(extra)]
        send_sems, recv_sems = refs[2 * n + len(extra):]
        c = lax.axis_index("c")
        sibling = (lax.axis_index("x"), lax.axis_index("y"), 1 - c)
        if halves:
            src_refs = [ref.at[1 - c] for ref in src_refs]
        copies = [_remote(src_refs[k], outs[k], send_sems, recv_sems, k, sibling) for k in range(n)]
        for cp in copies:
            cp.start()
        for cp in copies:
            cp.wait()

    sems = pltpu.SemaphoreType.DMA((n,))
    return pl.pallas_call(
        body, name=name, in_specs=[HBM] * n + [pl.BlockSpec(memory_space=pl.ANY)] * len(extra), out_specs=[HBM] * n,
        out_shape=[jax.ShapeDtypeStruct(s.shape[1:] if halves else s.shape, s.dtype) for s in srcs],
        scratch_shapes=[sems, sems],
        compiler_params=pltpu.CompilerParams(has_side_effects=True),
    )(*srcs, *extra)


SEM = pl.BlockSpec(memory_space=pltpu.SEMAPHORE)
DATAFLOW = pltpu.SideEffectType.DATAFLOW_SIDE_EFFECTING


def _exchange_copies(srcs, to_first, src_refs, land_refs, send_sems, recv_sems):
    x, y, c, me, chips = _mesh_place()
    n = len(srcs)
    sends, recvs = [], []
    for k in range(n):
        if k in to_first:
            base = 3 * n + 4 * to_first.index(k)
            sends.append((me != 0, pltpu.make_async_remote_copy(
                src_ref=src_refs[k], dst_ref=land_refs[k].at[me], send_sem=send_sems.at[base], recv_sem=recv_sems.at[base + me],
                device_id=(0, 0, c), device_id_type=MESH)))
            for s in range(1, N_CHIPS):
                recvs.append((me == 0, pltpu.make_async_remote_copy(
                    src_ref=src_refs[k], dst_ref=land_refs[k].at[s], send_sem=send_sems.at[base], recv_sem=recv_sems.at[base + s],
                    device_id=(s // 2, s % 2, c), device_id_type=MESH)))
        else:
            slab = (lambda t, k=k: src_refs[k]) if srcs[k].ndim == 2 else (lambda t, k=k: src_refs[k].at[t])
            for j, (px, py) in enumerate(chips):
                sends.append((None, _remote(slab(2 * px + py), land_refs[k].at[me], send_sems, recv_sems, 3 * k + j, (px, py, c))))
                recvs.append((None, _remote(slab(me), land_refs[k].at[2 * px + py], send_sems, recv_sems, 3 * k + j, (px, py, c))))
    return sends, recvs


def _when(pred, fn):
    if pred is None:
        fn()
    else:
        pl.when(pred)(fn)


def _exchange_start(srcs, to_first, name, after=None):
    n = len(srcs)
    n_sems = 3 * n + 4 * len(to_first)
    lands = [lax.empty((N_CHIPS,) + s.shape[-2:], s.dtype) for s in srcs]
    extra = [] if after is None else [after]

    def body(*refs):
        src_refs, land_refs = refs[:n], refs[n:2 * n]
        send_sems, recv_sems, token = refs[2 * n + len(extra)], refs[2 * n + len(extra) + 1], refs[-1]
        sends, _ = _exchange_copies(srcs, to_first, src_refs, land_refs, send_sems, recv_sems)
        for pred, cp in sends:
            _when(pred, cp.start)
        token[...] = jnp.zeros_like(token)

    hbm = lambda a: pltpu.HBM(a.shape, a.dtype)
    res = pl.pallas_call(
        body, name=name,
        out_shape=[pltpu.SemaphoreType.DMA((n_sems,)), pltpu.SemaphoreType.DMA((n_sems,))] + [hbm(a) for a in srcs + lands]
        + [jax.ShapeDtypeStruct((8, LANE), F32)],
        in_specs=[HBM] * (2 * n) + [pl.BlockSpec(memory_space=pl.ANY)] * len(extra),
        out_specs=[SEM, SEM] + [HBM] * (2 * n) + [pl.BlockSpec(memory_space=pltpu.VMEM)],
        input_output_aliases={i: 2 + i for i in range(2 * n)},
        compiler_params=pltpu.CompilerParams(has_side_effects=DATAFLOW),
    )(*[pltpu.with_memory_space_constraint(a, pltpu.HBM) for a in srcs + lands], *extra)
    return res[:-1], res[-1]


def _exchange_wait(srcs, to_first, started, after, name):
    n = len(srcs)
    send_sems, recv_sems, thru = started[0], started[1], started[2:]

    def body(*refs):
        src_refs, land_refs, send_ref, recv_ref = refs[:n], refs[n:2 * n], refs[2 * n], refs[2 * n + 1]
        sends, recvs = _exchange_copies(srcs, to_first, src_refs, land_refs, send_ref, recv_ref)
        for pred, cp in sends:
            _when(pred, cp.wait_send)
        for pred, cp in recvs:
            _when(pred, cp.wait_recv)

    res = pl.pallas_call(
        body, name=name, out_shape=[pltpu.HBM(a.shape, a.dtype) for a in thru],
        in_specs=[HBM] * (2 * n) + [SEM, SEM, pl.BlockSpec(memory_space=pl.ANY)], out_specs=[HBM] * (2 * n),
        input_output_aliases={i: i for i in range(2 * n)},
        compiler_params=pltpu.CompilerParams(has_side_effects=DATAFLOW),
    )(*thru, send_sems, recv_sems, after)
    return res[:n], res[n:]


ROW_TILE = 256
COL_TILE = 256


def _block_tiling(a, b):
    if a <= ROW_TILE or a % ROW_TILE == 0:
        ta = min(a, ROW_TILE)
        return a // ta, (ta, b), lambda i: (i, 0)
    return b // COL_TILE, (a, COL_TILE), lambda i: (0, i)


def _sum_landed(land, own, name):
    def body(p_ref, own_ref, o_ref):
        me = 2 * lax.axis_index("x") + lax.axis_index("y")
        own = own_ref[...].astype(F32)
        slot = lambda t: jnp.where(me == t, own, p_ref[t].astype(F32))
        o_ref[...] = ((slot(0) + slot(1)) + slot(2)) + slot(3)

    return pl.pallas_call(body, name=name, out_shape=jax.ShapeDtypeStruct(own.shape, F32), compiler_params=_params(()))(land, own)


def _sum_landed_small(lands, owns, name):
    n = len(lands)

    def body(*refs):
        me = 2 * lax.axis_index("x") + lax.axis_index("y")
        for p_ref, own_ref, o_ref in zip(refs[:n], refs[n:2 * n], refs[2 * n:]):
            own = own_ref[me].astype(F32)
            slot = lambda t: jnp.where(me == t, own, p_ref[t].astype(F32))
            o_ref[...] = ((slot(0) + slot(1)) + slot(2)) + slot(3)

    return pl.pallas_call(body, name=name, out_shape=[jax.ShapeDtypeStruct(o.shape[1:], F32) for o in owns],
                          compiler_params=_params(()))(*lands, *owns)


def _adamw_small(mine, theirs, states, name):
    n = len(mine)

    def body(*refs):
        ins, outs = refs[:5 * n], refs[5 * n:]
        for k in range(n):
            a_ref, b_ref, w_ref, m_ref, v_ref = ins[5 * k:5 * k + 5]
            g = a_ref[...] + b_ref[...]
            outs[4 * k][...] = g
            outs[4 * k + 1][...], outs[4 * k + 2][...], outs[4 * k + 3][...] = _adamw_math(g, w_ref[...], m_ref[...], v_ref[...])

    args = [t for k in range(n) for t in (mine[k], theirs[k], *states[k])]
    res = pl.pallas_call(body, name=name, out_shape=[jax.ShapeDtypeStruct(mine[k].shape, F32) for k in range(n) for _ in range(4)],
                         compiler_params=_params(()))(*args)
    return [tuple(res[4 * k:4 * k + 4]) for k in range(n)]


def _add_cast(halves, b, name):
    def body(a_ref, b_ref, o_ref):
        o_ref[...] = (a_ref[lax.axis_index("c")] + b_ref[...]).astype(BF16)

    return pl.pallas_call(body, name=name, out_shape=jax.ShapeDtypeStruct(b.shape, BF16),
                          compiler_params=_params(()))(halves, b)


class _LaterWeights:
    MID = ("w_branch_a", "w_branch_b", "w_out")

    def __init__(self, blocks, after):
        self.qkv_blocks = [_pad_wuq(blocks["w_uq"]), *_pad_wukv(blocks["w_ukv"])]
        self.mid_blocks = [blocks[n] for n in self.MID]
        self.qkv_started, t1 = _exchange_start(self.qkv_blocks, (), "weights_qkv_start", after)
        self.mid_started, t2 = _exchange_start(self.mid_blocks, (), "weights_mid_start", after)
        self.tokens = [t1, t2]

    @staticmethod
    def _whole(blocks, joined, started, after, name):
        _, landed = _exchange_wait(blocks, (), started, after, name)
        me = 2 * lax.axis_index("x") + lax.axis_index("y")
        out = []
        for block, land, join in zip(blocks, landed, joined):
            w = lax.dynamic_update_index_in_dim(land, block, me, 0)
            out.append(w.reshape(N_CHIPS * block.shape[0], block.shape[1]) if join else w)
        return out

    def qkv(self, after):
        return self._whole(self.qkv_blocks, (True, False, False), self.qkv_started, after, "weights_qkv_wait")

    def mid(self, after):
        return self._whole(self.mid_blocks, (False, False, True), self.mid_started, after, "weights_mid_wait")


class _GradExchange:
    EARLY = ("w_in", "w_branch_a", "w_branch_b", "w_out")
    LATE = ("w_uq", "w_ukv")

    def __init__(self, state):
        self.state = state
        self.outs = {}

    def start_early(self, g):
        full = jnp.concatenate([jnp.zeros((QKV_ROWS, D), g["w_in_rest"].dtype), g["w_in_rest"]], axis=0)
        g = dict(g, w_in=full)
        self.early = [(g[n] if g[n].ndim == 3 else _split_by_chip(n, g[n])).astype(BF16) for n in self.EARLY]
        self.early_started, token = _exchange_start(self.early, (), "grads_early_start")
        return token

    def start_late(self, g):
        self.early, self.early_landed = _exchange_wait(self.early, (), self.early_started, g["w_uq"], "grads_early_wait")
        (theirs,) = _sibling_exchange([g["w_in_qkv"]], "sibling_qkv_rows", halves=True)
        self.late = [_split_by_chip("w_uq", g["w_uq"]), g["w_ukv"], _add_cast(g["w_in_qkv"], theirs, "add_qkv_rows")]
        self.late_started, token = _exchange_start(self.late, (2,), "grads_late_start")
        names = self.EARLY[1:]
        mine = _sum_landed_small(self.early_landed[1:], self.early[1:], "sum_early")
        theirs = _sibling_exchange(mine, "sibling_early", after=token)
        for n, out in zip(names, _adamw_small(mine, theirs, [self.state[n] for n in names], "adamw_early")):
            self.outs[n] = out
        return self.outs[names[-1]][0]

    def finish(self, after):
        late, late_landed = _exchange_wait(self.late, (2,), self.late_started, after, "grads_late_wait")
        grad = _sum_exchange(self.early_landed[0], self.early[0], late_landed[2], late[2], "sum_w_in")
        self.outs["w_in"] = _adamw(grad, *self.state["w_in"], "adamw_w_in")
        return dict(zip(self.LATE, _sum_landed_small(late_landed[:2], late[:2], "sum_late")))


def _adamw_math(g, w, m, v):
    nm = ADAM_B1 * m + (1.0 - ADAM_B1) * g
    nv = ADAM_B2 * v + (1.0 - ADAM_B2) * (g * g)
    m_hat = nm / (1.0 - ADAM_B1 ** ADAM_STEP)
    v_hat = nv / (1.0 - ADAM_B2 ** ADAM_STEP)
    return -ADAM_LR * (m_hat / (jnp.sqrt(v_hat) + ADAM_EPS) + ADAM_WD * w), nm, nv


def _sum_exchange(land, own, first_land, first_own, name):
    _, a, b = land.shape
    steps, tile, at = _block_tiling(a, b)
    assert tile[0] == a, "the extra rows need whole columns in a step"
    r = first_own.shape[0]

    def body(me_ref, p_ref, own_ref, fp_ref, fo_ref, g_ref, mine_s, theirs_s, send_sems, recv_sems):
        pass_, i = pl.program_id(0), pl.program_id(1)
        me, c = me_ref[0], lax.axis_index("c")
        sibling = (lax.axis_index("x"), lax.axis_index("y"), 1 - c)
        copy = _remote(mine_s.at[i], theirs_s.at[i], send_sems, recv_sems, i, sibling)

        @pl.when(pass_ == 0)
        def _():
            own = own_ref[...].astype(F32)
            slot = lambda t: jnp.where(me == t, own, p_ref[t].astype(F32))
            mine_s[i] = ((slot(0) + slot(1)) + slot(2)) + slot(3)

            @pl.when(me == 0)
            def _():
                f = lambda t: fp_ref[t].astype(F32)
                rows = pl.ds(pl.multiple_of(c * r, 8), r)
                mine_s[i, rows, :] += ((fo_ref[...].astype(F32) + f(1)) + f(2)) + f(3)

            copy.start()

        @pl.when(pass_ == 1)
        def _():
            copy.wait()
            g_ref[...] = mine_s[i] + theirs_s[i]

    first = lambda p, i: i * (1 - p) + (steps - 1) * p
    in_specs = [pl.BlockSpec((N_CHIPS,) + tile, lambda p, i, me: (0,) + at(first(p, i))),
                pl.BlockSpec((None,) + tile, lambda p, i, me: (me[0],) + at(first(p, i))),
                pl.BlockSpec((N_CHIPS, r, tile[1]), lambda p, i, me: (0,) + at(first(p, i))),
                pl.BlockSpec((r, tile[1]), lambda p, i, me: at(first(p, i)))]
    me = jnp.reshape(2 * lax.axis_index("x") + lax.axis_index("y"), (1,)).astype(jnp.int32)
    kept = pltpu.VMEM((steps,) + tile, F32)
    sems = pltpu.SemaphoreType.DMA((steps,))
    return pl.pallas_call(
        body, name=name,
        grid_spec=pltpu.PrefetchScalarGridSpec(num_scalar_prefetch=1, grid=(2, steps), in_specs=in_specs,
                                               out_specs=pl.BlockSpec(tile, lambda p, i, me: at(i * p)),
                                               scratch_shapes=[kept, kept, sems, sems]),
        out_shape=jax.ShapeDtypeStruct((a, b), F32),
        compiler_params=_params(("arbitrary", "arbitrary")),
    )(me, land, own, first_land, first_own)


def _adamw(g, w, m, v, name):
    a, b = g.shape
    steps, tile, at = _block_tiling(a, b)

    def body(g_ref, w_ref, m_ref, v_ref, go_ref, d_ref, nm_ref, nv_ref):
        g = g_ref[...]
        go_ref[...] = g
        d_ref[...], nm_ref[...], nv_ref[...] = _adamw_math(g, w_ref[...], m_ref[...], v_ref[...])

    spec = pl.BlockSpec(tile, at)
    sds = jax.ShapeDtypeStruct((a, b), F32)
    return pl.pallas_call(
        body, name=name, grid=(steps,), in_specs=[spec] * 4, out_specs=[spec] * 4, out_shape=[sds] * 4,
        compiler_params=_params(("parallel",)),
    )(g, w, m, v)


LOSS_AT = (2, 1024)


def _vec_pack(vg, loss):
    names = [n for n, _, _ in VEC_ROWS]

    def body(*refs):
        o_ref = refs[-1]
        lb_ref, loss_ref = refs[len(names)], refs[len(names) + 1]
        o_ref[...] = jnp.zeros_like(o_ref)
        o_ref[LOSS_AT[0]:LOSS_AT[0] + 1, LOSS_AT[1]:LOSS_AT[1] + LANE] = jnp.broadcast_to(loss_ref[...], (1, LANE))
        for (name, row, size), ref in zip(VEC_ROWS, refs):
            if name == "g_hgrn":
                r = lax.broadcasted_iota(jnp.int32, (NH * V_DIM, LANE), 0)
                c = lax.broadcasted_iota(jnp.int32, (NH * V_DIM, LANE), 1)
                fold = ((r % V_DIM) == c).astype(F32)
                o_ref[row:row + 1, 0:LANE] = jnp.dot(ref[...], fold, precision=HIGHEST, preferred_element_type=F32)
            else:
                o_ref[row:row + 1, 0:size] = ref[...]
        o_ref[VEC_LB_ROW:VEC_LB_ROW + 2, 0:512] = lb_ref[...]

    return pl.pallas_call(body, name="vec_pack", out_shape=jax.ShapeDtypeStruct(VEC_SHAPE, F32))(
        *[vg[n] for n in names], vg["lb_logits"], loss)


def _adamw_vec(p_mine, p_sibling, w, m, v):
    names = [n for n, _, _ in VEC_ROWS] + ["lb_logits"]
    k = len(names)

    def body(a_ref, b_ref, *refs):
        ins, outs = refs[:3 * k], refs[3 * k:]
        at = (slice(LOSS_AT[0], LOSS_AT[0] + 1), slice(LOSS_AT[1], LOSS_AT[1] + LANE))
        outs[-1][...] = a_ref[at] + b_ref[at]
        for i, name in enumerate(names):
            if name == "lb_logits":
                rows, cols = slice(VEC_LB_ROW, VEC_LB_ROW + 2), slice(0, 512)
            else:
                _, row, size = VEC_ROWS[i]
                rows, cols = slice(row, row + 1), slice(0, size)
            g = a_ref[rows, cols] + b_ref[rows, cols]
            d, nm, nv = _adamw_math(g, ins[i][...], ins[k + i][...], ins[2 * k + i][...])
            for o_ref, val in zip(outs[4 * i:4 * i + 4], (g, d, nm, nv)):
                o_ref[...] = val

    shapes = [jax.ShapeDtypeStruct(w[n].shape, F32) for n in names for _ in range(4)] + [jax.ShapeDtypeStruct((1, LANE), F32)]
    res = pl.pallas_call(body, name="adamw_vec", out_shape=shapes)(
        p_mine, p_sibling, *[w[n] for n in names], *[m[n] for n in names], *[v[n] for n in names])
    return [{n: res[4 * i + j] for i, n in enumerate(names)} for j in range(4)], res[-1]


WEIGHTS = ("g_pre", "w_in", "b_gate", "g_q", "w_uq", "g_kv", "w_ukv", "lb_logits", "g_hgrn", "w_branch_a", "w_branch_b", "w_out", "g_post")


def kernel(x, g_pre, w_in, b_gate, g_q, w_uq, g_kv, w_ukv, lb_logits, g_hgrn, w_branch_a, w_branch_b, w_out, g_post, loss_target, m_g_pre, m_w_in, m_b_gate, m_g_q, m_w_uq, m_g_kv, m_w_ukv, m_lb_logits, m_g_hgrn, m_w_branch_a, m_w_branch_b, m_w_out, m_g_post, v_g_pre, v_w_in, v_b_gate, v_g_q, v_w_uq, v_g_kv, v_w_ukv, v_lb_logits, v_g_hgrn, v_w_branch_a, v_w_branch_b, v_w_out, v_g_post):
    w = dict(g_pre=g_pre, w_in=w_in, b_gate=b_gate, g_q=g_q, w_uq=w_uq, g_kv=g_kv, w_ukv=w_ukv, lb_logits=lb_logits, g_hgrn=g_hgrn,
             w_branch_a=w_branch_a, w_branch_b=w_branch_b, w_out=w_out, g_post=g_post)
    m = dict(g_pre=m_g_pre, w_in=m_w_in, b_gate=m_b_gate, g_q=m_g_q, w_uq=m_w_uq, g_kv=m_g_kv, w_ukv=m_w_ukv, lb_logits=m_lb_logits,
             g_hgrn=m_g_hgrn, w_branch_a=m_w_branch_a, w_branch_b=m_w_branch_b, w_out=m_w_out, g_post=m_g_post)
    v = dict(g_pre=v_g_pre, w_in=v_w_in, b_gate=v_b_gate, g_q=v_g_q, w_uq=v_w_uq, g_kv=v_g_kv, w_ukv=v_w_ukv, lb_logits=v_lb_logits,
             g_hgrn=v_g_hgrn, w_branch_a=v_w_branch_a, w_branch_b=v_w_branch_b, w_out=v_w_out, g_post=v_g_post)
    blocks = {n: _to_block(n, w[n]).astype(BF16) for n in BIG}
    w_in_all = _gather_w_in(blocks["w_in"])
    weights = _LaterWeights(blocks, w_in_all)
    state = {n: [_to_block(n, t[n]) for t in (w, m, v)] for n in BIG}
    exchange = _GradExchange(state)
    loss, grad_x, _, vec_grads = _local_step(
        x[0], loss_target[0], g_pre, _join_chips("w_in", w_in_all), b_gate, g_q, g_kv, lb_logits, g_hgrn, g_post, weights, exchange)
    vec = _vec_pack(vec_grads, loss)
    vec_started, token = _exchange_start([vec], (), "vec_start")
    sums = exchange.finish(token)
    rest = tuple(sums)
    (vec,), (vec_landed,) = _exchange_wait([vec], (), vec_started, exchange.outs["w_in"][0], "vec_wait")
    mine = [sums[n] for n in rest] + [_sum_landed(vec_landed, vec, "sum_vec")]
    theirs = _sibling_exchange(mine, "sibling_grads")
    done = dict(exchange.outs)
    small = _adamw_small(mine[:-1], theirs[:-1], [state[n] for n in rest], "adamw_late")
    done.update(zip(rest, small))
    outs = [{}, {}, {}, {}]
    for n in BIG:
        for o, val in zip(outs, done[n]):
            o[n] = _from_block(n, val)
    vec_outs, total = _adamw_vec(mine[-1], theirs[-1], w, m, v)
    for o, vals in zip(outs, vec_outs):
        o.update(vals)
    return (total[0, 0], grad_x[None], *[o[n] for o in outs for n in WEIGHTS])
```

```python
import math

import numpy as np
import jax
import jax.numpy as jnp
from jax import lax
from jax.experimental import pallas as pl
from jax.experimental.pallas import tpu as pltpu

F32 = jnp.float32
BF16 = jnp.bfloat16
HIGHEST = lax.Precision.HIGHEST

D = 1024
NH = 8
QK_NOPE, QK_ROPE, V_DIM = 64, 32, 64
Q_LORA, KV_LORA = 768, 256
CHUNK = 64
HG_BLOCK = 32
EPS = 1e-6
LANE = 128
P_MERGE, P_GA, P_HQ, P_HF, P_HI, P_GB, P_CQ, P_CKV, P_KPE = 0, 2048, 2560, 3072, 3584, 4096, 4608, 5376, 5632
D_P = 5760
O_CQ, O_CKV, O_KPE, O_GA, O_HQ, O_HF, O_HI, O_GB, O_MERGE = 0, 768, 1024, 1056, 1568, 2080, 2592, 3104, 3616

TM = 512
TM_MID = 256
TQ = 1024
ONES_LANE = (LANE - 1, 0)
TH = 256
HG_PAIRS = 4
VMEM_LIMIT = 56 * 1024 * 1024

ADAM_LR, ADAM_B1, ADAM_B2, ADAM_EPS, ADAM_WD, ADAM_STEP = 0.001, 0.9, 0.999, 1e-08, 0.01, 10

NT_DIMS = (((1,), (1,)), ((), ()))
TN_DIMS = (((0,), (0,)), ((), ()))


def _params(sem):
    return pltpu.CompilerParams(dimension_semantics=sem, vmem_limit_bytes=VMEM_LIMIT)


def _mm(a, b):
    return jnp.dot(a, b, preferred_element_type=F32)


def _mm_nt(a, b):
    return lax.dot_general(a, b, NT_DIMS, preferred_element_type=F32)


def _mm_tn(a, b):
    return lax.dot_general(a, b, TN_DIMS, preferred_element_type=F32)


def _sigmoid(z):
    return jax.nn.sigmoid(z)


def _rope(v, c, s1, s2):
    return v * c + pltpu.roll(v, 112, 1) * s1 + pltpu.roll(v, 16, 1) * s2


def _rope_t(dy, c, s1, s2):
    return dy * c + pltpu.roll(dy * s1, 16, 1) + pltpu.roll(dy * s2, 112, 1)


def _rope_tables(s):
    f32 = np.float32
    inv = f32(10000.0) ** (-np.arange(0, QK_ROPE, 2, dtype=f32) / f32(QK_ROPE))
    ang = np.arange(s, dtype=f32)[:, None] * inv[None, :]
    cos, sin = np.cos(ang).astype(f32), np.sin(ang).astype(f32)
    z64, z32, o64, o32 = np.zeros((s, 64), f32), np.zeros((s, 32), f32), np.ones((s, 64), f32), np.ones((s, 32), f32)
    z16 = np.zeros((s, 16), f32)
    c = np.concatenate([o64, cos, cos, o32], axis=1)
    s1 = np.concatenate([z64, -sin, z16, z32], axis=1)
    s2 = np.concatenate([z64, z16, sin, z32], axis=1)
    return jnp.asarray(c), jnp.asarray(s1), jnp.asarray(s2)


W_IN_RUNS = ((O_MERGE, 2048, P_MERGE), (O_GA, O_MERGE - O_GA, P_GA), (O_CQ, O_KPE - O_CQ, P_CQ))


def _kpe_block(w_in_t):
    z = lambda n: jnp.zeros((n, w_in_t.shape[1]), w_in_t.dtype)
    return jnp.concatenate([z(64), w_in_t[O_KPE:O_KPE + QK_ROPE], z(32)], axis=0)


def _front_fwd(x, g_pre, w_in_t, w_kpe, tokens=()):
    s = x.shape[0]
    tokens = list(tokens)

    def body(x_ref, g_ref, w_ref, k_ref, *refs):
        o_ref, h_ref = refs[len(tokens):]
        xv = x_ref[...]
        r = lax.rsqrt(jnp.mean(xv * xv, axis=-1, keepdims=True) + EPS)
        h = ((xv * r) * g_ref[...]).astype(BF16)
        h_ref[...] = h
        for row, rows, col in W_IN_RUNS:
            o_ref[:, col:col + rows] = _mm_nt(h, w_ref[row:row + rows, :])
        o_ref[:, P_KPE:P_KPE + LANE] = _mm_nt(h, k_ref[...])

    full = lambda a: pl.BlockSpec(a.shape, lambda i: (0,) * a.ndim)
    return pl.pallas_call(
        body, name="front_fwd", grid=(s // TM,),
        in_specs=[pl.BlockSpec((TM, D), lambda i: (i, 0)), pl.BlockSpec((1, D), lambda i: (0, 0)), full(w_in_t), full(w_kpe)]
        + [pl.BlockSpec((8, LANE), lambda i: (0, 0))] * len(tokens),
        out_specs=[pl.BlockSpec((TM, D_P), lambda i: (i, 0)), pl.BlockSpec((TM, D), lambda i: (i, 0))],
        out_shape=[jax.ShapeDtypeStruct((s, D_P), F32), jax.ShapeDtypeStruct((s, D), BF16)],
        compiler_params=_params(("parallel",)),
    )(x, g_pre, w_in_t, w_kpe, *tokens)


def _norm_rows(v, g):
    r = lax.rsqrt(jnp.mean(v * v, axis=-1, keepdims=True) + EPS)
    return (v * r) * g, r


def _qkv_fwd(proj, g_q, g_kv, w_uq_p, w_k_p, w_v_p, rc, rs1, rs2):
    s = proj.shape[0]

    def body(cq_ref, ckv_ref, kpe_ref, gq_ref, gkv_ref, wq_ref, wk_ref, wv_ref, c_ref, s1_ref, s2_ref, q_ref, k_ref, v_ref):
        c, s1, s2 = c_ref[...], s1_ref[...], s2_ref[...]
        cqn, _ = _norm_rows(cq_ref[...], gq_ref[...])
        ckvn, _ = _norm_rows(ckv_ref[...], gkv_ref[...])
        ckvn = ckvn.astype(BF16)
        qf = _mm(cqn.astype(BF16), wq_ref[...])
        kf = jnp.concatenate([_mm(ckvn, wk_ref[t]) for t in range(N_CHIPS)], axis=1)
        vf = jnp.concatenate([_mm(ckvn, wv_ref[t]) for t in range(N_CHIPS)], axis=1)
        kpe = _rope(kpe_ref[...], c, s1, s2)
        lane = lax.broadcasted_iota(jnp.int32, (TM, LANE), 1)
        for h in range(NH):
            blk = slice(h * LANE, (h + 1) * LANE)
            q_ref[h] = _rope(qf[:, blk], c, s1, s2).astype(BF16)
            k_ref[h] = (kf[:, blk] + kpe).astype(BF16)
            v_ref[h] = jnp.where(lane == ONES_LANE[h % 2], 1.0, vf[:, blk]).astype(BF16)

    row = lambda w, j: pl.BlockSpec((TM, w), lambda i: (i, j))
    full = lambda a: pl.BlockSpec(a.shape, lambda i: (0,) * a.ndim)
    hs = jax.ShapeDtypeStruct((NH, s, LANE), BF16)
    return pl.pallas_call(
        body, name="qkv_fwd", grid=(s // TM,),
        in_specs=[row(Q_LORA, P_CQ // Q_LORA), row(KV_LORA, P_CKV // KV_LORA), row(LANE, P_KPE // LANE),
                  full(g_q), full(g_kv), full(w_uq_p), full(w_k_p), full(w_v_p), row(LANE, 0), row(LANE, 0), row(LANE, 0)],
        out_specs=[pl.BlockSpec((NH, TM, LANE), lambda i: (0, i, 0))] * 3,
        out_shape=[hs, hs, hs],
        compiler_params=_params(("parallel",)),
    )(proj, proj, proj, g_q, g_kv, w_uq_p, w_k_p, w_v_p, rc, rs1, rs2)


LOG2E = 1.4426950408889634
QK_SCALE2 = LOG2E / math.sqrt(QK_NOPE + QK_ROPE)


HQ = TQ // 2


def _diag_visible(n):
    row = lax.broadcasted_iota(jnp.int32, (n, n), 0)
    col = lax.broadcasted_iota(jnp.int32, (n, n), 1)
    return (col // CHUNK) <= (row // CHUNK)


def _attn_fwd(q, k, vv):
    s = q.shape[1]

    def body(q_ref, k_ref, v_ref, o_ref, lse_ref):
        i = pl.program_id(1)
        qs = (q_ref[0], q_ref[1])

        def tiles(t, carry, diag):
            rows = pl.ds(pl.multiple_of(t * TQ, TQ), TQ)
            sc = [_mm_nt(qs[hh], k_ref[hh, rows, :]) for hh in range(2)]
            if diag:
                sc = [jnp.where(_diag_visible(TQ), s_, -jnp.inf) for s_ in sc]
            m_new = [jnp.maximum(carry[hh][0], jnp.max(sc[hh], axis=-1, keepdims=True)) for hh in range(2)]
            alpha = [jnp.exp2((carry[hh][0] - m_new[hh]) * QK_SCALE2) for hh in range(2)]
            p = [jnp.exp2((sc[hh] - m_new[hh]) * QK_SCALE2).astype(BF16) for hh in range(2)]
            acc = [alpha[hh] * carry[hh][1] + _mm(p[hh], v_ref[hh, rows, :]) for hh in range(2)]
            return (m_new[0], acc[0]), (m_new[1], acc[1])

        init = (jnp.full((TQ, 1), -jnp.inf, F32), jnp.zeros((TQ, LANE), F32))
        carry = lax.fori_loop(0, i, lambda t, c: tiles(t, c, False), (init, init))
        carry = tiles(i, carry, True)
        lane = lax.broadcasted_iota(jnp.int32, (TQ, LANE), 1)
        out = jnp.zeros((TQ, LANE), F32)
        for hh in range(2):
            m, acc = carry[hh]
            l = jnp.sum(jnp.where(lane == ONES_LANE[hh], acc, 0.0), axis=-1, keepdims=True)
            out = out + jnp.where((lane < V_DIM) == (hh == 0), acc, 0.0) / l
            lse_ref[hh] = jnp.broadcast_to(m * QK_SCALE2 + jnp.log(l) * LOG2E, (TQ, LANE))
        o_ref[...] = out

    return pl.pallas_call(
        body, name="attn_fwd", grid=(NH // 2, s // TQ),
        in_specs=[pl.BlockSpec((2, TQ, LANE), lambda p, i: (p, i, 0)), pl.BlockSpec((2, s, LANE), lambda p, i: (p, 0, 0)),
                  pl.BlockSpec((2, s, LANE), lambda p, i: (p, 0, 0))],
        out_specs=[pl.BlockSpec((TQ, LANE), lambda p, i: (i, p)), pl.BlockSpec((2, TQ, LANE), lambda p, i: (p, i, 0))],
        out_shape=[jax.ShapeDtypeStruct((s, NH * V_DIM), F32), jax.ShapeDtypeStruct((NH, s, LANE), F32)],
        compiler_params=_params(("parallel", "parallel")),
    )(q, k, vv)


def _lower_bound(lbl):
    a0, a1 = lbl[0:1, :], lbl[1:2, :]
    mx = jnp.maximum(a0, a1)
    e0, e1 = jnp.exp(a0 - mx), jnp.exp(a1 - mx)
    return e0 / (e0 + e1)


def _chunk_cumsum(v, reverse=False):
    pos = lax.broadcasted_iota(jnp.int32, v.shape, 0) % HG_BLOCK
    s = 1
    while s < HG_BLOCK:
        if reverse:
            v = v + jnp.where(pos < HG_BLOCK - s, pltpu.roll(v, TH - s, 0), 0.0)
        else:
            v = v + jnp.where(pos >= s, pltpu.roll(v, s, 0), 0.0)
        s *= 2
    return v


def _hgrn_gates(hq, hf, lb):
    sig = _sigmoid(hf)
    f = lb + (1.0 - lb) * sig
    g = jnp.log(f)
    kk = 1.0 - f
    r = lax.broadcasted_iota(jnp.int32, (TH, TH), 0)
    c = lax.broadcasted_iota(jnp.int32, (TH, TH), 1)
    tri = ((r // HG_BLOCK) == (c // HG_BLOCK)) & (r >= c)
    cum = _chunk_cumsum(g)
    nch = TH // HG_BLOCK
    total = _chunks(cum)[:, HG_BLOCK - 1:HG_BLOCK, :]
    lastb = jnp.broadcast_to(total, (nch, HG_BLOCK, hf.shape[-1])).reshape(hf.shape)
    e, ei, ee = jnp.exp(cum), jnp.exp(-cum), jnp.exp(lastb - cum)
    return dict(sig=sig, f=f, kk=kk, tri=tri, cum=cum, total=total, decay=jnp.exp(total), e=e, ei=ei, ee=ee,
                qd=hq * e, ki=kk * ei, ke=kk * ee)


def _chunks(v):
    return v.reshape(TH // HG_BLOCK, HG_BLOCK, v.shape[-1])


def _bmm_nt(a, b):
    return lax.dot_general(a, b, (((2,), (2,)), ((0,), (0,))), preferred_element_type=F32)


def _bmm_nn(a, b):
    return lax.dot_general(a, b, (((2,), (1,)), ((0,), (0,))), preferred_element_type=F32)


def _bmm_tn(a, b):
    return lax.dot_general(a, b, (((1,), (1,)), ((0,), (0,))), preferred_element_type=F32)


def _pair_masks():
    lane = lax.broadcasted_iota(jnp.int32, (TH, LANE), 1)
    kr = lax.broadcasted_iota(jnp.int32, (LANE, LANE), 0)
    kc = lax.broadcasted_iota(jnp.int32, (LANE, LANE), 1)
    return lane < 64, (kr // 64) == (kc // 64)


def _hgrn_fwd(proj, lbl):
    s = proj.shape[0]
    nch = TH // HG_BLOCK

    def body(hq_ref, hf_ref, hi_ref, lbl_ref, o_ref, st_ref, st):
        @pl.when(pl.program_id(1) == 0)
        def _():
            st[...] = jnp.zeros_like(st)

        m0, bd = _pair_masks()
        gt = _hgrn_gates(hq_ref[...], hf_ref[...], _lower_bound(lbl_ref[...]))
        v_b, qd, qd_b = hi_ref[...].astype(BF16), gt["qd"], gt["qd"].astype(BF16)
        ki_b, ke_b = gt["ki"].astype(BF16), gt["ke"].astype(BF16)
        pairs = [slice(u * LANE, (u + 1) * LANE) for u in range(HG_PAIRS)]
        heads = [(lanes, m0 if hh == 0 else jnp.logical_not(m0)) for lanes in pairs for hh in range(2)]
        a_b = [jnp.where(gt["tri"], _mm_nt(jnp.where(mh, qd[:, lanes], 0.0).astype(BF16), ki_b[:, lanes]), 0.0).astype(BF16)
               for lanes, mh in heads]
        intra = [jnp.where(m0, _mm(a_b[2 * u], v_b[:, lanes]), _mm(a_b[2 * u + 1], v_b[:, lanes])) for u, lanes in enumerate(pairs)]
        upd = [_bmm_tn(_chunks(v_b[:, lanes]), _chunks(ke_b[:, lanes])) for lanes in pairs]
        entering = []
        for u, lanes in enumerate(pairs):
            cur, states = st[u], []
            for n in range(nch):
                states.append(cur)
                cur = gt["decay"][n][:, lanes] * cur + jnp.where(bd, upd[u][n], 0.0)
            st[u] = cur
            entering.append(jnp.stack(states))
            st_ref[u] = entering[u]
        for u, lanes in enumerate(pairs):
            o_ref[:, lanes] = intra[u] + _bmm_nt(_chunks(qd_b[:, lanes]), entering[u].astype(BF16)).reshape(TH, LANE)

    wide = HG_PAIRS * LANE
    col = lambda base: pl.BlockSpec((TH, wide), lambda p, i: (i, base // wide + p))
    return pl.pallas_call(
        body, name="hgrn_fwd", grid=(NH // 2 // HG_PAIRS, s // TH),
        in_specs=[col(P_HQ), col(P_HF), col(P_HI), pl.BlockSpec((2, wide), lambda p, i: (0, p))],
        out_specs=[pl.BlockSpec((TH, wide), lambda p, i: (i, p)),
                   pl.BlockSpec((HG_PAIRS, nch, LANE, LANE), lambda p, i: (p, i, 0, 0))],
        out_shape=[jax.ShapeDtypeStruct((s, 512), F32), jax.ShapeDtypeStruct((NH // 2, s // HG_BLOCK, LANE, LANE), F32)],
        scratch_shapes=[pltpu.VMEM((HG_PAIRS, LANE, LANE), F32)],
        compiler_params=_params(("parallel", "arbitrary")),
    )(proj, proj, proj, lbl)


def _group_sum(v):
    low = lax.broadcasted_iota(jnp.int32, (v.shape[0], LANE), 1) < V_DIM
    blocks = []
    for b in range(v.shape[1] // LANE):
        blk = v[:, b * LANE:(b + 1) * LANE]
        s_low = jnp.sum(jnp.where(low, blk, 0.0), axis=-1, keepdims=True)
        s_high = jnp.sum(jnp.where(low, 0.0, blk), axis=-1, keepdims=True)
        blocks.append(jnp.where(low, s_low, s_high))
    return jnp.concatenate(blocks, axis=1)


def _dsilu(z, sg):
    return sg * (1.0 + z * (1.0 - sg))


def _mid(proj, attn, o_raw, x, tgt, g_hg, b_gate, g_post, wa, wb, w_out):
    s = x.shape[0]

    def body(attn_ref, ga_ref, o_ref, gb_ref, mg_ref, x_ref, t_ref, ghg_ref, bg_ref, gp_ref, wa_ref, wb_ref, wo_ref,
             loss_ref, dout_ref, dattn_ref, dga_ref, dor_ref, dgb_ref, dmg_ref, dwo_out, dwa_out, dwb_out, dgp_ref, dbg_ref, dghg_ref,
             dwo_ref, dwa_ref, dwb_ref):
        @pl.when(pl.program_id(0) == 0)
        def _():
            for rf in (loss_ref, dwo_ref, dwa_ref, dwb_ref, dgp_ref, dbg_ref, dghg_ref):
                rf[...] = jnp.zeros_like(rf)

        attn, za, orw, zb = attn_ref[...], ga_ref[...], o_ref[...], gb_ref[...]
        ghg, gp = ghg_ref[...], gp_ref[...]
        sga, sgb = _sigmoid(za), _sigmoid(zb)
        sa, sb = za * sga, zb * sgb
        ga = attn * sa
        rh = lax.rsqrt(_group_sum(orw * orw) * (1.0 / V_DIM) + EPS)
        on = (orw * rh) * ghg
        gb = on * sb
        ga_b, gb_b = ga.astype(BF16), gb.astype(BF16)
        blocks = [slice(t * (D // N_CHIPS), (t + 1) * (D // N_CHIPS)) for t in range(N_CHIPS)]
        ya = jnp.concatenate([_mm(ga_b, wa_ref[t]) for t in range(N_CHIPS)], axis=1)
        yb = jnp.concatenate([_mm(gb_b, wb_ref[t]) for t in range(N_CHIPS)], axis=1)
        gates = _sigmoid(mg_ref[...] + bg_ref[...])
        g0, g1 = gates[:, :D], gates[:, D:]
        m_b = (g0 * ya + g1 * yb).astype(BF16)
        y = _mm(m_b, wo_ref[...])
        ry = lax.rsqrt(jnp.mean(y * y, axis=-1, keepdims=True) + EPS)
        out = x_ref[...] + (y * ry) * gp
        err = out - t_ref[...]
        loss_ref[...] += 0.5 * jnp.sum(jnp.mean(err * err, axis=-1, keepdims=True), axis=0, keepdims=True)
        dout = err * (1.0 / D)
        dout_ref[...] = dout
        dgp_ref[...] += jnp.sum(dout * (y * ry), axis=0, keepdims=True)
        dgy = dout * gp
        dy = ry * dgy - y * (ry * ry * ry) * jnp.mean(y * dgy, axis=-1, keepdims=True)
        dy_b = dy.astype(BF16)
        dm = _mm_nt(dy_b, wo_ref[...])
        dya_b, dyb_b = (dm * g0).astype(BF16), (dm * g1).astype(BF16)
        dga = sum(_mm_nt(dya_b[:, cols], wa_ref[t]) for t, cols in enumerate(blocks))
        dgb = sum(_mm_nt(dyb_b[:, cols], wb_ref[t]) for t, cols in enumerate(blocks))
        dwo_ref[...] += _mm_tn(m_b, dy_b)
        for t, cols in enumerate(blocks):
            dwa_ref[t] += _mm_tn(ga_b, dya_b[:, cols])
            dwb_ref[t] += _mm_tn(gb_b, dyb_b[:, cols])
        dg0, dg1 = dm * ya, dm * yb
        dmg = jnp.concatenate([dg0 * g0 * (1.0 - g0), dg1 * g1 * (1.0 - g1)], axis=1)
        dmg_ref[...] = dmg.astype(BF16)
        dbg_ref[...] += jnp.sum(dmg, axis=0, keepdims=True)
        dattn_ref[...] = dga * sa
        dga_ref[...] = (dga * attn * _dsilu(za, sga)).astype(BF16)
        dgb_ref[...] = (dgb * on * _dsilu(zb, sgb)).astype(BF16)
        don = dgb * sb
        dghg_ref[...] += jnp.sum(don * (orw * rh), axis=0, keepdims=True)
        dgo = don * ghg
        dor_ref[...] = rh * dgo - orw * (rh * rh * rh) * (_group_sum(orw * dgo) * (1.0 / V_DIM))

        @pl.when(pl.program_id(0) == pl.num_programs(0) - 1)
        def _():
            for out, rf in ((dwo_out, dwo_ref), (dwa_out, dwa_ref), (dwb_out, dwb_ref)):
                out[...] = rf[...].astype(BF16)

    row = lambda w, j=0: pl.BlockSpec((TM_MID, w), lambda i: (i, j))
    full = lambda a: pl.BlockSpec(a.shape, lambda i: (0,) * a.ndim)
    acc = lambda shape: pl.BlockSpec(shape, lambda i: (0,) * len(shape))
    slabs = (N_CHIPS, 512, D // N_CHIPS)
    sds = jax.ShapeDtypeStruct
    return pl.pallas_call(
        body, name="mid", grid=(s // TM_MID,),
        in_specs=[row(512), row(512, P_GA // 512), row(512), row(512, P_GB // 512), row(2048, P_MERGE // 2048), row(D), row(D),
                  full(g_hg), full(b_gate), full(g_post), full(wa), full(wb), full(w_out)],
        out_specs=[acc((1, 1)), row(D), row(512), row(512), row(512), row(512), row(2048),
                   acc((D, D)), acc(slabs), acc(slabs), acc((1, D)), acc((1, 2048)), acc((1, 512))],
        out_shape=[sds((1, 1), F32), sds((s, D), F32), sds((s, 512), F32), sds((s, 512), BF16), sds((s, 512), F32), sds((s, 512), BF16),
                   sds((s, 2048), BF16), sds((D, D), BF16), sds(slabs, BF16), sds(slabs, BF16), sds((1, D), F32),
                   sds((1, 2048), F32), sds((1, 512), F32)],
        scratch_shapes=[pltpu.VMEM((D, D), F32), pltpu.VMEM(slabs, F32), pltpu.VMEM(slabs, F32)],
        compiler_params=_params(("arbitrary",)),
    )(attn, proj, o_raw, proj, proj, x, tgt, g_hg, b_gate, g_post, wa, wb, w_out)


def _attn_bwd(q, k, vv, attn, dattn, lse, token):
    s = q.shape[1]
    nt = s // TQ
    scale = 1.0 / math.sqrt(QK_NOPE + QK_ROPE)

    def body(q_ref, k_ref, v_ref, o_ref, do_ref, lse_ref, token_ref, dq_ref, dk_ref, dv_ref, do_s, delta_s):
        j = pl.program_id(1)

        @pl.when(j == 0)
        def _():
            dq_ref[...] = jnp.zeros_like(dq_ref)
            lane = lax.broadcasted_iota(jnp.int32, (TQ, LANE), 1)

            @pl.loop(0, nt)
            def _(i):
                rows = pl.ds(pl.multiple_of(i * TQ, TQ), TQ)
                do, o = do_ref[rows, :], o_ref[rows, :]
                for hh in range(2):
                    doh = jnp.where((lane < 64) if hh == 0 else (lane >= 64), do, 0.0)
                    do_s[hh, rows, :] = doh.astype(BF16)
                    delta_s[hh, rows, :] = jnp.broadcast_to(jnp.sum(doh * o, axis=-1, keepdims=True), (TQ, LANE))

        kjs, vjs = (k_ref[0], k_ref[1]), (v_ref[0], v_ref[1])

        def tile(hh, start, size, kj, vj, diag):
            rows = pl.ds(pl.multiple_of(start, size), size)
            wide = lambda a: jnp.concatenate([a] * (kj.shape[0] // LANE), axis=1)
            qi, do_b = q_ref[hh, rows, :], do_s[hh, rows, :]
            sc, dp = _mm_nt(qi, kj), _mm_nt(do_b, vj)
            p = jnp.exp2(sc * QK_SCALE2 - wide(lse_ref[hh, rows, :]))
            if diag:
                p = jnp.where(_diag_visible(size), p, 0.0)
            ds_b = (p * (dp - wide(delta_s[hh, rows, :]))).astype(BF16)
            dv, dk = _mm_tn(do_b, p.astype(BF16)), _mm_tn(qi, ds_b)
            dq_ref[hh, rows, :] += _mm(ds_b, kj)
            return dk, dv

        def step(i, carry):
            new = [tile(hh, i * TQ, TQ, kjs[hh], vjs[hh], False) for hh in range(2)]
            return tuple((carry[hh][0] + new[hh][0], carry[hh][1] + new[hh][1]) for hh in range(2))

        def diagonal(hh):
            k0, k1, v0, v1 = kjs[hh][:HQ], kjs[hh][HQ:], vjs[hh][:HQ], vjs[hh][HQ:]
            a = tile(hh, j * TQ, HQ, k0, v0, True)
            b = tile(hh, j * TQ + HQ, HQ, k0, v0, False)
            c = tile(hh, j * TQ + HQ, HQ, k1, v1, True)
            return jnp.concatenate([a[0] + b[0], c[0]], axis=1), jnp.concatenate([a[1] + b[1], c[1]], axis=1)

        carry = lax.fori_loop(j + 1, nt, step, (diagonal(0), diagonal(1)))
        for hh in range(2):
            dk_ref[hh] = carry[hh][0].T * scale
            dv_ref[hh] = carry[hh][1].T

        @pl.when(j == nt - 1)
        def _():
            dq_ref[...] = dq_ref[...] * scale

    whole = pl.BlockSpec((2, s, LANE), lambda p, j: (p, 0, 0))
    tile_spec = pl.BlockSpec((2, TQ, LANE), lambda p, j: (p, j, 0))
    cols = pl.BlockSpec((s, LANE), lambda p, j: (0, p))
    hs = jax.ShapeDtypeStruct((NH, s, LANE), F32)
    return pl.pallas_call(
        body, name="attn_bwd", grid=(NH // 2, nt),
        in_specs=[whole, tile_spec, tile_spec, cols, cols, whole, pl.BlockSpec((8, LANE), lambda p, j: (0, 0))],
        out_specs=[whole, tile_spec, tile_spec],
        out_shape=[hs, hs, hs],
        scratch_shapes=[pltpu.VMEM((2, s, LANE), BF16), pltpu.VMEM((2, s, LANE), F32)],
        compiler_params=_params(("parallel", "arbitrary")),
    )(q, k, vv, attn, dattn, lse, token)


def _hgrn_bwd(proj, lbl, states, do_raw):
    s = proj.shape[0]
    nt = s // TH
    nch = TH // HG_BLOCK

    def body(hq_ref, hf_ref, hi_ref, lbl_ref, st_ref, do_ref, dh_ref, dlbl_ref, dst, dlb):
        step = pl.program_id(1)

        @pl.when(step == 0)
        def _():
            dst[...] = jnp.zeros_like(dst)
            dlb[...] = jnp.zeros_like(dlb)

        m0, bd = _pair_masks()
        lb = _lower_bound(lbl_ref[...])
        gt = _hgrn_gates(hq_ref[...], hf_ref[...], lb)
        do = do_ref[...]
        qd, ki, ke = gt["qd"], gt["ki"], gt["ke"]
        v_b, do_b = hi_ref[...].astype(BF16), do.astype(BF16)
        qd_b, ki_b, ke_b = qd.astype(BF16), ki.astype(BF16), ke.astype(BF16)
        pairs = [slice(u * LANE, (u + 1) * LANE) for u in range(HG_PAIRS)]
        heads = [(lanes, m0 if hh == 0 else jnp.logical_not(m0)) for lanes in pairs for hh in range(2)]
        a_b = [jnp.where(gt["tri"], _mm_nt(jnp.where(mh, qd[:, lanes], 0.0).astype(BF16), ki_b[:, lanes]), 0.0).astype(BF16)
               for lanes, mh in heads]
        doh_b = [jnp.where(mh, do[:, lanes], 0.0).astype(BF16) for lanes, mh in heads]
        da_b = [jnp.where(gt["tri"], _mm_nt(d, v_b[:, lanes]), 0.0).astype(BF16) for d, (lanes, _) in zip(doh_b, heads)]
        dv_p, dqd_p, dki_p = [], [], []
        for u, lanes in enumerate(pairs):
            e, o = 2 * u, 2 * u + 1
            dv_p.append(_mm_tn(a_b[e], doh_b[e]) + _mm_tn(a_b[o], doh_b[o]))
            dqd_p.append(jnp.where(m0, _mm(da_b[e], ki_b[:, lanes]), _mm(da_b[o], ki_b[:, lanes])))
            dki_p.append(jnp.where(m0, _mm_tn(da_b[e], qd_b[:, lanes]), _mm_tn(da_b[o], qd_b[:, lanes])))
        fed = [_bmm_tn(_chunks(do_b[:, lanes]), _chunks(qd_b[:, lanes])) for lanes in pairs]
        leaving = []
        for u, lanes in enumerate(pairs):
            ds, left = dst[u], [None] * nch
            for n in reversed(range(nch)):
                left[n] = ds
                ds = gt["decay"][n][:, lanes] * ds + jnp.where(bd, fed[u][n], 0.0)
            dst[u] = ds
            leaving.append(jnp.stack(left))
        dke_p, dlast_p = [], []
        for u, lanes in enumerate(pairs):
            entering, leaving_b = st_ref[u], leaving[u].astype(BF16)
            dke3 = _bmm_nn(_chunks(v_b[:, lanes]), leaving_b)
            dv_p[u] = dv_p[u] + _bmm_nt(_chunks(ke_b[:, lanes]), leaving_b).reshape(TH, LANE)
            dqd_p[u] = dqd_p[u] + _bmm_nn(_chunks(do_b[:, lanes]), entering.astype(BF16)).reshape(TH, LANE)
            dke_p.append(dke3.reshape(TH, LANE))
            dlast_p.append(jnp.sum(dke3 * _chunks(ke[:, lanes]), axis=1, keepdims=True)
                           + jnp.sum(leaving[u] * entering, axis=1, keepdims=True) * gt["decay"][:, :, lanes])
        cat = lambda parts: jnp.concatenate(parts, axis=-1)
        dv, dqd, dki, dke, dlast = cat(dv_p), cat(dqd_p), cat(dki_p), cat(dke_p), cat(dlast_p)
        dk = dki * gt["ei"] + dke * gt["ee"]
        dcum = dqd * qd - dki * ki - dke * ke
        dg = _chunk_cumsum(dcum, reverse=True) + jnp.broadcast_to(dlast, (nch, HG_BLOCK, dlast.shape[-1])).reshape(dcum.shape)
        sig = gt["sig"]
        df = dg / gt["f"] - dk
        dlb[...] += jnp.sum(df * (1.0 - sig), axis=0, keepdims=True)
        dh_ref[0] = (dqd * gt["e"]).astype(BF16)
        dh_ref[1] = ((df * (1.0 - lb)) * sig * (1.0 - sig)).astype(BF16)
        dh_ref[2] = dv.astype(BF16)

        @pl.when(step == nt - 1)
        def _():
            lb = _lower_bound(lbl_ref[...])
            da0 = dlb[...] * lb * (1.0 - lb)
            dlbl_ref[...] = jnp.concatenate([da0, -da0], axis=0)

    wide = HG_PAIRS * LANE
    col = lambda base: pl.BlockSpec((TH, wide), lambda p, i: (nt - 1 - i, base // wide + p))
    tile = pl.BlockSpec((TH, wide), lambda p, i: (nt - 1 - i, p))
    sds = jax.ShapeDtypeStruct
    return pl.pallas_call(
        body, name="hgrn_bwd", grid=(NH // 2 // HG_PAIRS, nt),
        in_specs=[col(P_HQ), col(P_HF), col(P_HI), pl.BlockSpec((2, wide), lambda p, i: (0, p)),
                  pl.BlockSpec((HG_PAIRS, nch, LANE, LANE), lambda p, i: (p, nt - 1 - i, 0, 0)), tile],
        out_specs=[pl.BlockSpec((3, TH, wide), lambda p, i: (0, nt - 1 - i, p)), pl.BlockSpec((2, wide), lambda p, i: (0, p))],
        out_shape=[sds((3, s, 512), BF16), sds((2, 512), F32)],
        scratch_shapes=[pltpu.VMEM((HG_PAIRS, LANE, LANE), F32), pltpu.VMEM((1, wide), F32)],
        compiler_params=_params(("parallel", "arbitrary")),
    )(proj, proj, proj, lbl, states, do_raw)


def _norm_rows_bwd(v, r, g, dn):
    dgv = dn * g
    return r * dgv - v * (r * r * r) * jnp.mean(v * dgv, axis=-1, keepdims=True)


def _qkv_bwd(proj, dq, dk, dvv, g_q, g_kv, w_uq_p, w_k_p, w_v_p, rc, rs1, rs2):
    s = proj.shape[0]
    head_q = QK_NOPE + QK_ROPE

    def body(cq_ref, ckv_ref, dq_ref, dk_ref, dv_ref, gq_ref, gkv_ref, wq_ref, wk_ref, wv_ref, c_ref, s1_ref, s2_ref,
             dcq_ref, dckv_ref, dkpe_ref, dwq_out, dwkv_out, dgq_ref, dgkv_ref, dwq_ref, dwk_ref, dwv_ref):
        @pl.when(pl.program_id(0) == 0)
        def _():
            for rf in (dwq_ref, dwk_ref, dwv_ref, dgq_ref, dgkv_ref):
                rf[...] = jnp.zeros_like(rf)

        c, s1, s2 = c_ref[...], s1_ref[...], s2_ref[...]
        cq, ckv = cq_ref[...], ckv_ref[...]
        gq, gkv = gq_ref[...], gkv_ref[...]
        cqn, rq = _norm_rows(cq, gq)
        ckvn, rkv = _norm_rows(ckv, gkv)
        cqn_b, ckvn_b = cqn.astype(BF16), ckvn.astype(BF16)
        dqf = jnp.concatenate([_rope_t(dq_ref[h], c, s1, s2) for h in range(NH)], axis=1).astype(BF16)
        dkf = jnp.concatenate([dk_ref[h] for h in range(NH)], axis=1).astype(BF16)
        dvf = jnp.concatenate([dv_ref[h] for h in range(NH)], axis=1).astype(BF16)
        dkpe = dk_ref[0]
        for h in range(1, NH):
            dkpe = dkpe + dk_ref[h]
        lane = lax.broadcasted_iota(jnp.int32, (TM, LANE), 1)
        dkpe = jnp.where((lane >= QK_NOPE) & (lane < QK_NOPE + QK_ROPE), dkpe, 0.0)
        dkpe_ref[...] = _rope_t(dkpe, c, s1, s2).astype(BF16)
        dcqn = _mm_nt(dqf, wq_ref[...])
        pair = lambda a, t: a[:, t * 2 * LANE:(t + 1) * 2 * LANE]
        dckvn = sum(_mm_nt(pair(dkf, t), wk_ref[t]) + _mm_nt(pair(dvf, t), wv_ref[t]) for t in range(N_CHIPS))
        dwq_ref[...] += _mm_tn(cqn_b, dqf)
        dwk_ref[...] += _mm_tn(ckvn_b, dkf)
        dwv_ref[...] += _mm_tn(ckvn_b, dvf)
        dgq_ref[...] += jnp.sum(dcqn * (cq * rq), axis=0, keepdims=True)
        dgkv_ref[...] += jnp.sum(dckvn * (ckv * rkv), axis=0, keepdims=True)
        dcq_ref[...] = _norm_rows_bwd(cq, rq, gq, dcqn).astype(BF16)
        dckv_ref[...] = _norm_rows_bwd(ckv, rkv, gkv, dckvn).astype(BF16)

        @pl.when(pl.program_id(0) == pl.num_programs(0) - 1)
        def _():
            blk = lambda ref, h: ref[:, h * LANE:(h + 1) * LANE]
            lane = lax.broadcasted_iota(jnp.int32, (Q_LORA, LANE), 1)
            for j in range(NH * head_q // LANE):
                h0, w0 = divmod(j * LANE, head_q)
                first = blk(dwq_ref, h0) if w0 == 0 else pltpu.roll(blk(dwq_ref, h0), LANE - w0, 1)
                second = pltpu.roll(blk(dwq_ref, h0 + 1), head_q - w0, 1)
                dwq_out[:, j * LANE:(j + 1) * LANE] = jnp.where(lane < head_q - w0, first, second).astype(BF16)
            lane = lax.broadcasted_iota(jnp.int32, (KV_LORA, LANE), 1)
            for h in range(NH):
                vals = blk(dwv_ref, h) if h % 2 else pltpu.roll(blk(dwv_ref, h), V_DIM, 1)
                both = jnp.where(lane < QK_NOPE, blk(dwk_ref, h), vals).astype(BF16)
                dwkv_out[h // 2, :, (h % 2) * LANE:(h % 2 + 1) * LANE] = both

    row = lambda w, j=0: pl.BlockSpec((TM, w), lambda i: (i, j))
    full = lambda a: pl.BlockSpec(a.shape, lambda i: (0,) * a.ndim)
    acc = lambda *shape: pl.BlockSpec(shape, lambda i: (0,) * len(shape))
    heads = pl.BlockSpec((NH, TM, LANE), lambda i: (0, i, 0))
    sds = jax.ShapeDtypeStruct
    return pl.pallas_call(
        body, name="qkv_bwd", grid=(s // TM,),
        in_specs=[row(Q_LORA, P_CQ // Q_LORA), row(KV_LORA, P_CKV // KV_LORA), heads, heads, heads,
                  full(g_q), full(g_kv), full(w_uq_p), full(w_k_p), full(w_v_p), row(LANE), row(LANE), row(LANE)],
        out_specs=[row(Q_LORA), row(KV_LORA), row(LANE), acc(Q_LORA, NH * head_q), acc(NH // 2, KV_LORA, 2 * LANE),
                   acc(1, Q_LORA), acc(1, KV_LORA)],
        out_shape=[sds((s, Q_LORA), BF16), sds((s, KV_LORA), BF16), sds((s, LANE), BF16), sds((Q_LORA, NH * head_q), BF16),
                   sds((NH // 2, KV_LORA, 2 * LANE), BF16), sds((1, Q_LORA), F32), sds((1, KV_LORA), F32)],
        scratch_shapes=[pltpu.VMEM((Q_LORA, D), F32), pltpu.VMEM((KV_LORA, D), F32), pltpu.VMEM((KV_LORA, D), F32)],
        compiler_params=_params(("arbitrary",)),
    )(proj, proj, dq, dk, dvv, g_q, g_kv, w_uq_p, w_k_p, w_v_p, rc, rs1, rs2)


def _front_bwd(x, dout, dmg, dga, dh3, dgb, dcq, dckv, dkpe, g_pre, w_in_t, w_kpe, token):
    s = x.shape[0]

    def body(x_ref, do_ref, dmg_ref, dga_ref, dh3_ref, dgb_ref, dcq_ref, dckv_ref, dkpe_ref, g_ref, w_ref, k_ref, token_ref,
             gx_ref, dg_ref):
        @pl.when(pl.program_id(0) == 0)
        def _():
            dg_ref[...] = jnp.zeros_like(dg_ref)

        xv, g = x_ref[...], g_ref[...]
        _, r = _norm_rows(xv, g)
        pieces = ((dmg_ref[...], O_MERGE), (dga_ref[...], O_GA), (dh3_ref[0], O_HQ), (dh3_ref[1], O_HF), (dh3_ref[2], O_HI),
                  (dgb_ref[...], O_GB), (dcq_ref[...], O_CQ), (dckv_ref[...], O_CKV))
        dh = _mm(dkpe_ref[...], k_ref[...])
        for piece, off in pieces:
            dh = dh + _mm(piece, w_ref[off:off + piece.shape[1], :])
        dg_ref[...] += jnp.sum(dh * (xv * r), axis=0, keepdims=True)
        gx_ref[...] = do_ref[...] + _norm_rows_bwd(xv, r, g, dh)

    row = lambda w: pl.BlockSpec((TM, w), lambda i: (i, 0))
    full = lambda a: pl.BlockSpec(a.shape, lambda i: (0,) * a.ndim)
    sds = jax.ShapeDtypeStruct
    return pl.pallas_call(
        body, name="front_bwd", grid=(s // TM,),
        in_specs=[row(D), row(D), row(2048), row(512), pl.BlockSpec((3, TM, 512), lambda i: (0, i, 0)), row(512), row(Q_LORA),
                  row(KV_LORA), row(LANE), full(g_pre), full(w_in_t), full(w_kpe), pl.BlockSpec(memory_space=pl.ANY)],
        out_specs=[row(D), pl.BlockSpec((1, D), lambda i: (0, 0))],
        out_shape=[sds((s, D), F32), sds((1, D), F32)],
        compiler_params=_params(("arbitrary",)),
    )(x, dout, dmg, dga, dh3, dgb, dcq, dckv, dkpe, g_pre, w_in_t, w_kpe, token)


TK_GRAD = 1024


def _win_grad(h, pieces, name):
    s = h.shape[0]
    n = len(pieces)

    def body(h_ref, *refs):
        d_refs, o_refs, sums = refs[:n], refs[n:2 * n], refs[2 * n:]

        @pl.when(pl.program_id(0) == 0)
        def _():
            for s_ref in sums:
                s_ref[...] = jnp.zeros_like(s_ref)

        hv = h_ref[...]
        for d_ref, s_ref in zip(d_refs, sums):
            if len(d_ref.shape) == 3:
                for k in range(d_ref.shape[0]):
                    s_ref[k] += _mm_tn(d_ref[k], hv)
            else:
                s_ref[...] += _mm_tn(d_ref[...], hv)

        @pl.when(pl.program_id(0) == pl.num_programs(0) - 1)
        def _():
            for o_ref, s_ref in zip(o_refs, sums):
                o_ref[...] = s_ref[...].astype(BF16)

    def in_spec(p):
        if p.ndim == 3:
            return pl.BlockSpec((p.shape[0], TK_GRAD, p.shape[2]), lambda kk: (0, kk, 0))
        return pl.BlockSpec((TK_GRAD, p.shape[1]), lambda kk: (kk, 0))

    out_shapes = [(p.shape[0], p.shape[2], D) if p.ndim == 3 else (p.shape[1], D) for p in pieces]
    return pl.pallas_call(
        body, name=name, grid=(s // TK_GRAD,),
        in_specs=[pl.BlockSpec((TK_GRAD, D), lambda kk: (kk, 0))] + [in_spec(p) for p in pieces],
        out_specs=[pl.BlockSpec(sh, lambda kk, nd=len(sh): (0,) * nd) for sh in out_shapes],
        out_shape=[jax.ShapeDtypeStruct(sh, BF16) for sh in out_shapes],
        scratch_shapes=[pltpu.VMEM(sh, F32) for sh in out_shapes],
        compiler_params=_params(("arbitrary",)),
    )(h, *pieces)


QKV_ROWS = Q_LORA + KV_LORA + QK_ROPE


def _win_grad_qkv(h, dcq, dckv, dkpe, share):
    s = h.shape[0]
    half = QKV_ROWS // 2

    def body(h_ref, cq_ref, ckv_ref, kpe_ref, o_ref, *scratch):
        sums = scratch[0] if share else o_ref

        @pl.when(pl.program_id(0) == 0)
        def _():
            sums[...] = jnp.zeros_like(sums)

        hv = h_ref[...]
        g_cq = _mm_tn(cq_ref[...], hv)
        sums[0] += g_cq[:half]
        sums[1, 0:Q_LORA - half] += g_cq[half:]
        sums[1, Q_LORA - half:Q_LORA + KV_LORA - half] += _mm_tn(ckv_ref[...], hv)
        sums[1, Q_LORA + KV_LORA - half:] += _mm_tn(kpe_ref[...], hv)[QK_NOPE:QK_NOPE + QK_ROPE]

        if share:
            _, theirs, send_sem, recv_sem = scratch

            @pl.when(pl.program_id(0) == pl.num_programs(0) - 1)
            def _():
                c = lax.axis_index("c")
                copy = _remote(sums.at[1 - c], theirs, send_sem, recv_sem, 0, (lax.axis_index("x"), lax.axis_index("y"), 1 - c))
                copy.start()
                copy.wait()
                o_ref[...] = (sums[c] + theirs[...]).astype(BF16)

    rows = lambda a: pl.BlockSpec((TK_GRAD, a.shape[1]), lambda kk: (kk, 0))
    sem = pltpu.SemaphoreType.DMA((1,))
    shape, dtype = ((half, D), BF16) if share else ((2, half, D), F32)
    return pl.pallas_call(
        body, name="win_grad_qkv", grid=(s // TK_GRAD,), in_specs=[rows(h), rows(dcq), rows(dckv), rows(dkpe)],
        out_specs=pl.BlockSpec(shape, lambda kk: (0,) * len(shape)), out_shape=jax.ShapeDtypeStruct(shape, dtype),
        scratch_shapes=[pltpu.VMEM((2, half, D), F32), pltpu.VMEM((half, D), F32), sem, sem] if share else [],
        compiler_params=_params(("arbitrary",)),
    )(h, dcq, dckv, dkpe)


def _pad_wuq(w_uq):
    rows = w_uq.shape[0]
    w = w_uq.reshape(rows, NH, QK_NOPE + QK_ROPE)
    return jnp.pad(w, ((0, 0), (0, 0), (0, LANE - QK_NOPE - QK_ROPE))).reshape(rows, NH * LANE)


def _pad_wukv(w_ukv):
    heads = w_ukv.shape[1] // (QK_NOPE + V_DIM)
    w = w_ukv.reshape(KV_LORA, heads, QK_NOPE + V_DIM)
    w_k = jnp.pad(w[:, :, :QK_NOPE], ((0, 0), (0, 0), (0, LANE - QK_NOPE))).reshape(KV_LORA, heads * LANE)
    wv = w[:, :, QK_NOPE:].reshape(KV_LORA, heads // 2, 2, 1, V_DIM)
    eye = jnp.eye(2, dtype=w.dtype).reshape(1, 1, 2, 2, 1)
    return w_k, (wv * eye).reshape(KV_LORA, heads * LANE)


def _local_step(x, tgt, g_pre, w_in_t, b_gate, g_q, g_kv, lb_logits, g_hgrn, g_post, weights, exchange=None):
    s = x.shape[0]
    w_kpe = _kpe_block(w_in_t)
    rc, rs1, rs2 = _rope_tables(s)
    g_hg = jnp.tile(g_hgrn, (1, NH))

    proj, h = _front_fwd(x, g_pre, w_in_t, w_kpe, weights.tokens)
    w_uq_p, w_k_p, w_v_p = weights.qkv(h)
    q, k, vv = _qkv_fwd(proj, g_q, g_kv, w_uq_p, w_k_p, w_v_p, rc, rs1, rs2)
    attn, lse = _attn_fwd(q, k, vv)
    o_raw, states = _hgrn_fwd(proj, lb_logits)
    wa, wb, w_out = weights.mid(o_raw)
    (loss, dout, dattn, dga, dor, dgb, dmg, d_wout, d_wa, d_wb, d_gpost, d_bgate, d_ghg) = _mid(
        proj, attn, o_raw, x, tgt, g_hg, b_gate, g_post, wa, wb, w_out)
    w_mg, w_ga, w_gb = _win_grad(h, [dmg, dga, dgb], "win_grad_mid")
    dh3, d_lbl = _hgrn_bwd(proj, lb_logits, states, dor)
    (w_h3,) = _win_grad(h, [dh3], "win_grad_hgrn")
    d_win_rest = jnp.concatenate([w_ga, w_h3[0], w_h3[1], w_h3[2], w_gb, w_mg], axis=0)
    early = dict(w_in_rest=d_win_rest, w_branch_a=d_wa, w_branch_b=d_wb, w_out=d_wout)
    token = exchange.start_early(early) if exchange else jnp.zeros((8, LANE), F32)
    dq, dk, dvv = _attn_bwd(q, k, vv, attn, dattn, lse, token)
    dcq, dckv, dkpe, d_wuq, d_wukv, d_gq, d_gkv = _qkv_bwd(proj, dq, dk, dvv, g_q, g_kv, w_uq_p, w_k_p, w_v_p, rc, rs1, rs2)
    late = dict(w_in_qkv=_win_grad_qkv(h, dcq, dckv, dkpe, share=exchange is not None), w_uq=d_wuq, w_ukv=d_wukv)
    token = exchange.start_late(late) if exchange else jnp.zeros((8, LANE), F32)
    grad_x, d_gpre = _front_bwd(x, dout, dmg, dga, dh3, dgb, dcq, dckv, dkpe, g_pre, w_in_t, w_kpe, token)
    vec_grads = dict(g_pre=d_gpre, b_gate=d_bgate, g_q=d_gq, g_kv=d_gkv, lb_logits=d_lbl, g_hgrn=d_ghg, g_post=d_gpost)
    return loss, grad_x, dict(early, **late), vec_grads


SHARD_SHAPES = (("w_in", (1416, 1024)), ("w_uq", (192, 768)), ("w_ukv", (256, 256)), ("w_branch_a", (512, 256)),
                ("w_branch_b", (512, 256)), ("w_out", (256, 1024)))
BIG = tuple(n for n, _ in SHARD_SHAPES)
ROW_SHARDED = ("w_in", "w_uq", "w_out")
N_CHIPS = 4
W_IN_FORWARD_CUT = 704


def _to_block(name, a):
    return a[0].T if name == "w_in" else a[0]


def _from_block(name, a):
    return a.T[None] if name == "w_in" else a[None]
VEC_ROWS = (("g_pre", 0, 1024), ("b_gate", 1, 2048), ("g_q", 2, 768), ("g_kv", 3, 256), ("g_hgrn", 6, 64), ("g_post", 7, 1024))
VEC_LB_ROW = 4
VEC_SHAPE = (8, 2048)


def _split_by_chip(name, g):
    a, b = dict(SHARD_SHAPES)[name]
    return g.reshape(N_CHIPS, a, b) if name in ROW_SHARDED else g.reshape(a, N_CHIPS, b).transpose(1, 0, 2)


def _join_chips(name, w):
    a, b = dict(SHARD_SHAPES)[name]
    return w.reshape(N_CHIPS * a, b) if name in ROW_SHARDED else w.transpose(1, 0, 2).reshape(a, N_CHIPS * b)


MESH = pl.DeviceIdType.MESH
HBM = pl.BlockSpec(memory_space=pltpu.HBM)


def _mesh_place():
    x, y, c = lax.axis_index("x"), lax.axis_index("y"), lax.axis_index("c")
    return x, y, c, 2 * x + y, [(1 - x, y), (x, 1 - y), (1 - x, 1 - y)]


def _remote(src, dst, send_sems, recv_sems, k, to):
    return pltpu.make_async_remote_copy(src_ref=src, dst_ref=dst, send_sem=send_sems.at[k], recv_sem=recv_sems.at[k],
                                        device_id=to, device_id_type=MESH)


def _gather_w_in(shard):
    a, b = shard.shape
    cut = W_IN_FORWARD_CUT

    def body(src, out, ici_send, ici_recv, d2d_send, d2d_recv, local_sem):
        x, y, c = lax.axis_index("x"), lax.axis_index("y"), lax.axis_index("c")
        me, xn, yn, dg = 2 * x + y, 2 * (1 - x) + y, 2 * x + (1 - y), 2 * (1 - x) + (1 - y)
        to_x, to_y, sibling = (1 - x, y, c), (x, 1 - y, c), (x, y, 1 - c)
        first, rest = pl.ds(0, cut), pl.ds(cut, a - cut)

        whole = lambda ref, which: ref.at[:, pl.ds(pl.multiple_of(which * (b // 2), b // 2), b // 2)]
        own = pltpu.make_async_copy(src, out.at[me], local_sem)
        own.start()
        sends = [_remote(whole(src, c), whole(out.at[me], c), ici_send, ici_recv, 0, to_x),
                 _remote(whole(src, c), whole(out.at[me], c), ici_send, ici_recv, 1, to_y)]
        for cp in sends:
            cp.start()

        def landed(slot, rows, k, d2d_k, src_dev):
            piece = whole(out.at[slot], c) if rows is None else out.at[slot].at[rows, pl.ds(pl.multiple_of(c * (b // 2), b // 2), b // 2)]
            _remote(piece, piece, ici_send, ici_recv, k, src_dev).wait_recv()
            cp = _remote(piece, piece, d2d_send, d2d_recv, d2d_k, sibling)
            cp.start()
            sends.append(cp)
            return piece

        def pass_on(slot, rows, k, to):
            piece = out.at[slot].at[rows, pl.ds(pl.multiple_of(c * (b // 2), b // 2), b // 2)]
            cp = _remote(piece, piece, ici_send, ici_recv, k, to)
            cp.start()
            sends.append(cp)

        landed(xn, None, 0, 0, to_x)
        pass_on(xn, first, 2, to_y)
        landed(yn, None, 1, 1, to_y)
        pass_on(yn, rest, 3, to_x)
        landed(dg, first, 2, 2, to_y)
        landed(dg, rest, 3, 3, to_x)
        other = pl.ds(pl.multiple_of((1 - c) * (b // 2), b // 2), b // 2)
        for d2d_k, (slot, rows) in enumerate(((xn, None), (yn, None), (dg, first), (dg, rest))):
            piece = out.at[slot].at[:, other] if rows is None else out.at[slot].at[rows, other]
            _remote(piece, piece, d2d_send, d2d_recv, d2d_k, sibling).wait_recv()
        for cp in sends:
            cp.wait_send()
        own.wait()

    sems = pltpu.SemaphoreType.DMA((4,))
    return pl.pallas_call(
        body, name="gather_w_in", in_specs=[HBM], out_specs=HBM,
        out_shape=jax.ShapeDtypeStruct((N_CHIPS, a, b), shard.dtype),
        scratch_shapes=[sems, sems, sems, sems, pltpu.SemaphoreType.DMA],
        compiler_params=pltpu.CompilerParams(has_side_effects=True),
    )(shard)


def _sibling_exchange(srcs, name, after=None):
    n = len(srcs)
    extra = [] if after is None else [after]

    def body(*refs):
        src_refs, outs = refs[:n], refs[n + len(extra):2 * n + len(extra)]
        send_sems, recv_sems = refs[2 * n + len(extra):]
        sibling = (lax.axis_index("x"), lax.axis_index("y"), 1 - lax.axis_index("c"))
        copies = [_remote(src_refs[k], outs[k], send_sems, recv_sems, k, sibling) for k in range(n)]
        for cp in copies:
            cp.start()
        for cp in copies:
            cp.wait()

    sems = pltpu.SemaphoreType.DMA((n,))
    return pl.pallas_call(
        body, name=name, in_specs=[HBM] * n + [pl.BlockSpec(memory_space=pl.ANY)] * len(extra), out_specs=[HBM] * n,
        out_shape=[jax.ShapeDtypeStruct(s.shape, s.dtype) for s in srcs],
        scratch_shapes=[sems, sems],
        compiler_params=pltpu.CompilerParams(has_side_effects=True),
    )(*srcs, *extra)


SEM = pl.BlockSpec(memory_space=pltpu.SEMAPHORE)
DATAFLOW = pltpu.SideEffectType.DATAFLOW_SIDE_EFFECTING


def _exchange_copies(srcs, to_first, src_refs, land_refs, send_sems, recv_sems):
    x, y, c, me, chips = _mesh_place()
    n = len(srcs)
    sends, recvs = [], []
    for k in range(n):
        if k in to_first:
            base = 3 * n + 4 * to_first.index(k)
            sends.append((me != 0, pltpu.make_async_remote_copy(
                src_ref=src_refs[k], dst_ref=land_refs[k].at[me], send_sem=send_sems.at[base], recv_sem=recv_sems.at[base + me],
                device_id=(0, 0, c), device_id_type=MESH)))
            for s in range(1, N_CHIPS):
                recvs.append((me == 0, pltpu.make_async_remote_copy(
                    src_ref=src_refs[k], dst_ref=land_refs[k].at[s], send_sem=send_sems.at[base], recv_sem=recv_sems.at[base + s],
                    device_id=(s // 2, s % 2, c), device_id_type=MESH)))
        else:
            slab = (lambda t, k=k: src_refs[k]) if srcs[k].ndim == 2 else (lambda t, k=k: src_refs[k].at[t])
            for j, (px, py) in enumerate(chips):
                sends.append((None, _remote(slab(2 * px + py), land_refs[k].at[me], send_sems, recv_sems, 3 * k + j, (px, py, c))))
                recvs.append((None, _remote(slab(me), land_refs[k].at[2 * px + py], send_sems, recv_sems, 3 * k + j, (px, py, c))))
    return sends, recvs


def _when(pred, fn):
    if pred is None:
        fn()
    else:
        pl.when(pred)(fn)


def _exchange_start(srcs, to_first, name, after=None):
    n = len(srcs)
    n_sems = 3 * n + 4 * len(to_first)
    lands = [lax.empty((N_CHIPS,) + s.shape[-2:], s.dtype) for s in srcs]
    extra = [] if after is None else [after]

    def body(*refs):
        src_refs, land_refs = refs[:n], refs[n:2 * n]
        send_sems, recv_sems, token = refs[2 * n + len(extra)], refs[2 * n + len(extra) + 1], refs[-1]
        sends, _ = _exchange_copies(srcs, to_first, src_refs, land_refs, send_sems, recv_sems)
        for pred, cp in sends:
            _when(pred, cp.start)
        token[...] = jnp.zeros_like(token)

    hbm = lambda a: pltpu.HBM(a.shape, a.dtype)
    res = pl.pallas_call(
        body, name=name,
        out_shape=[pltpu.SemaphoreType.DMA((n_sems,)), pltpu.SemaphoreType.DMA((n_sems,))] + [hbm(a) for a in srcs + lands]
        + [jax.ShapeDtypeStruct((8, LANE), F32)],
        in_specs=[HBM] * (2 * n) + [pl.BlockSpec(memory_space=pl.ANY)] * len(extra),
        out_specs=[SEM, SEM] + [HBM] * (2 * n) + [pl.BlockSpec(memory_space=pltpu.VMEM)],
        input_output_aliases={i: 2 + i for i in range(2 * n)},
        compiler_params=pltpu.CompilerParams(has_side_effects=DATAFLOW),
    )(*[pltpu.with_memory_space_constraint(a, pltpu.HBM) for a in srcs + lands], *extra)
    return res[:-1], res[-1]


def _exchange_wait(srcs, to_first, started, after, name):
    n = len(srcs)
    send_sems, recv_sems, thru = started[0], started[1], started[2:]

    def body(*refs):
        src_refs, land_refs, send_ref, recv_ref = refs[:n], refs[n:2 * n], refs[2 * n], refs[2 * n + 1]
        sends, recvs = _exchange_copies(srcs, to_first, src_refs, land_refs, send_ref, recv_ref)
        for pred, cp in sends:
            _when(pred, cp.wait_send)
        for pred, cp in recvs:
            _when(pred, cp.wait_recv)

    res = pl.pallas_call(
        body, name=name, out_shape=[pltpu.HBM(a.shape, a.dtype) for a in thru],
        in_specs=[HBM] * (2 * n) + [SEM, SEM, pl.BlockSpec(memory_space=pl.ANY)], out_specs=[HBM] * (2 * n),
        input_output_aliases={i: i for i in range(2 * n)},
        compiler_params=pltpu.CompilerParams(has_side_effects=DATAFLOW),
    )(*thru, send_sems, recv_sems, after)
    return res[:n], res[n:]


ROW_TILE = 256
COL_TILE = 256


def _block_tiling(a, b):
    if a <= ROW_TILE or a % ROW_TILE == 0:
        ta = min(a, ROW_TILE)
        return a // ta, (ta, b), lambda i: (i, 0)
    return b // COL_TILE, (a, COL_TILE), lambda i: (0, i)


def _sum_landed(land, own, name):
    def body(p_ref, own_ref, o_ref):
        me = 2 * lax.axis_index("x") + lax.axis_index("y")
        own = own_ref[...].astype(F32)
        slot = lambda t: jnp.where(me == t, own, p_ref[t].astype(F32))
        o_ref[...] = ((slot(0) + slot(1)) + slot(2)) + slot(3)

    return pl.pallas_call(body, name=name, out_shape=jax.ShapeDtypeStruct(own.shape, F32), compiler_params=_params(()))(land, own)


def _sum_landed_small(lands, owns, name):
    n = len(lands)

    def body(*refs):
        me = 2 * lax.axis_index("x") + lax.axis_index("y")
        for p_ref, own_ref, o_ref in zip(refs[:n], refs[n:2 * n], refs[2 * n:]):
            own = own_ref[me].astype(F32)
            slot = lambda t: jnp.where(me == t, own, p_ref[t].astype(F32))
            o_ref[...] = ((slot(0) + slot(1)) + slot(2)) + slot(3)

    return pl.pallas_call(body, name=name, out_shape=[jax.ShapeDtypeStruct(o.shape[1:], F32) for o in owns],
                          compiler_params=_params(()))(*lands, *owns)


def _adamw_small(mine, theirs, states, name):
    n = len(mine)

    def body(*refs):
        ins, outs = refs[:5 * n], refs[5 * n:]
        for k in range(n):
            a_ref, b_ref, w_ref, m_ref, v_ref = ins[5 * k:5 * k + 5]
            g = a_ref[...] + b_ref[...]
            outs[4 * k][...] = g
            outs[4 * k + 1][...], outs[4 * k + 2][...], outs[4 * k + 3][...] = _adamw_math(g, w_ref[...], m_ref[...], v_ref[...])

    args = [t for k in range(n) for t in (mine[k], theirs[k], *states[k])]
    res = pl.pallas_call(body, name=name, out_shape=[jax.ShapeDtypeStruct(mine[k].shape, F32) for k in range(n) for _ in range(4)],
                         compiler_params=_params(()))(*args)
    return [tuple(res[4 * k:4 * k + 4]) for k in range(n)]


class _LaterWeights:
    MID = ("w_branch_a", "w_branch_b", "w_out")

    def __init__(self, blocks, after):
        self.qkv_blocks = [_pad_wuq(blocks["w_uq"]), *_pad_wukv(blocks["w_ukv"])]
        self.mid_blocks = [blocks[n] for n in self.MID]
        self.qkv_started, t1 = _exchange_start(self.qkv_blocks, (), "weights_qkv_start", after)
        self.mid_started, t2 = _exchange_start(self.mid_blocks, (), "weights_mid_start", after)
        self.tokens = [t1, t2]

    @staticmethod
    def _whole(blocks, joined, started, after, name):
        _, landed = _exchange_wait(blocks, (), started, after, name)
        me = 2 * lax.axis_index("x") + lax.axis_index("y")
        out = []
        for block, land, join in zip(blocks, landed, joined):
            w = lax.dynamic_update_index_in_dim(land, block, me, 0)
            out.append(w.reshape(N_CHIPS * block.shape[0], block.shape[1]) if join else w)
        return out

    def qkv(self, after):
        return self._whole(self.qkv_blocks, (True, False, False), self.qkv_started, after, "weights_qkv_wait")

    def mid(self, after):
        return self._whole(self.mid_blocks, (False, False, True), self.mid_started, after, "weights_mid_wait")


class _GradExchange:
    EARLY = ("w_in", "w_branch_a", "w_branch_b", "w_out")
    LATE = ("w_uq", "w_ukv")

    def __init__(self, state):
        self.state = state
        self.outs = {}

    def start_early(self, g):
        full = jnp.concatenate([jnp.zeros((QKV_ROWS, D), g["w_in_rest"].dtype), g["w_in_rest"]], axis=0)
        g = dict(g, w_in=full)
        self.early = [(g[n] if g[n].ndim == 3 else _split_by_chip(n, g[n])).astype(BF16) for n in self.EARLY]
        self.early_started, token = _exchange_start(self.early, (), "grads_early_start")
        return token

    def start_late(self, g):
        self.early, self.early_landed = _exchange_wait(self.early, (), self.early_started, g["w_uq"], "grads_early_wait")
        self.late = [_split_by_chip("w_uq", g["w_uq"]), g["w_ukv"], g["w_in_qkv"]]
        self.late_started, token = _exchange_start(self.late, (2,), "grads_late_start")
        names = self.EARLY[1:]
        mine = _sum_landed_small(self.early_landed[1:], self.early[1:], "sum_early")
        theirs = _sibling_exchange(mine, "sibling_early", after=token)
        for n, out in zip(names, _adamw_small(mine, theirs, [self.state[n] for n in names], "adamw_early")):
            self.outs[n] = out
        return self.outs[names[-1]][0]

    def finish(self, after):
        late, late_landed = _exchange_wait(self.late, (2,), self.late_started, after, "grads_late_wait")
        grad = _sum_exchange(self.early_landed[0], self.early[0], late_landed[2], late[2], "sum_w_in")
        self.outs["w_in"] = _adamw(grad, *self.state["w_in"], "adamw_w_in")
        return dict(zip(self.LATE, _sum_landed_small(late_landed[:2], late[:2], "sum_late")))


def _adamw_math(g, w, m, v):
    nm = ADAM_B1 * m + (1.0 - ADAM_B1) * g
    nv = ADAM_B2 * v + (1.0 - ADAM_B2) * (g * g)
    m_hat = nm / (1.0 - ADAM_B1 ** ADAM_STEP)
    v_hat = nv / (1.0 - ADAM_B2 ** ADAM_STEP)
    return -ADAM_LR * (m_hat / (jnp.sqrt(v_hat) + ADAM_EPS) + ADAM_WD * w), nm, nv


def _sum_exchange(land, own, first_land, first_own, name):
    _, a, b = land.shape
    steps, tile, at = _block_tiling(a, b)
    assert tile[0] == a, "the extra rows need whole columns in a step"
    r = first_own.shape[0]

    def body(me_ref, p_ref, own_ref, fp_ref, fo_ref, g_ref, mine_s, theirs_s, send_sems, recv_sems):
        pass_, i = pl.program_id(0), pl.program_id(1)
        me, c = me_ref[0], lax.axis_index("c")
        sibling = (lax.axis_index("x"), lax.axis_index("y"), 1 - c)
        copy = _remote(mine_s.at[i], theirs_s.at[i], send_sems, recv_sems, i, sibling)

        @pl.when(pass_ == 0)
        def _():
            own = own_ref[...].astype(F32)
            slot = lambda t: jnp.where(me == t, own, p_ref[t].astype(F32))
            mine_s[i] = ((slot(0) + slot(1)) + slot(2)) + slot(3)

            @pl.when(me == 0)
            def _():
                f = lambda t: fp_ref[t].astype(F32)
                rows = pl.ds(pl.multiple_of(c * r, 8), r)
                mine_s[i, rows, :] += ((fo_ref[...].astype(F32) + f(1)) + f(2)) + f(3)

            copy.start()

        @pl.when(pass_ == 1)
        def _():
            copy.wait()
            g_ref[...] = mine_s[i] + theirs_s[i]

    first = lambda p, i: i * (1 - p) + (steps - 1) * p
    in_specs = [pl.BlockSpec((N_CHIPS,) + tile, lambda p, i, me: (0,) + at(first(p, i))),
                pl.BlockSpec((None,) + tile, lambda p, i, me: (me[0],) + at(first(p, i))),
                pl.BlockSpec((N_CHIPS, r, tile[1]), lambda p, i, me: (0,) + at(first(p, i))),
                pl.BlockSpec((r, tile[1]), lambda p, i, me: at(first(p, i)))]
    me = jnp.reshape(2 * lax.axis_index("x") + lax.axis_index("y"), (1,)).astype(jnp.int32)
    kept = pltpu.VMEM((steps,) + tile, F32)
    sems = pltpu.SemaphoreType.DMA((steps,))
    return pl.pallas_call(
        body, name=name,
        grid_spec=pltpu.PrefetchScalarGridSpec(num_scalar_prefetch=1, grid=(2, steps), in_specs=in_specs,
                                               out_specs=pl.BlockSpec(tile, lambda p, i, me: at(i * p)),
                                               scratch_shapes=[kept, kept, sems, sems]),
        out_shape=jax.ShapeDtypeStruct((a, b), F32),
        compiler_params=_params(("arbitrary", "arbitrary")),
    )(me, land, own, first_land, first_own)


def _adamw(g, w, m, v, name):
    a, b = g.shape
    steps, tile, at = _block_tiling(a, b)

    def body(g_ref, w_ref, m_ref, v_ref, go_ref, d_ref, nm_ref, nv_ref):
        g = g_ref[...]
        go_ref[...] = g
        d_ref[...], nm_ref[...], nv_ref[...] = _adamw_math(g, w_ref[...], m_ref[...], v_ref[...])

    spec = pl.BlockSpec(tile, at)
    sds = jax.ShapeDtypeStruct((a, b), F32)
    return pl.pallas_call(
        body, name=name, grid=(steps,), in_specs=[spec] * 4, out_specs=[spec] * 4, out_shape=[sds] * 4,
        compiler_params=_params(("parallel",)),
    )(g, w, m, v)


LOSS_AT = (2, 1024)


def _vec_pack(vg, loss):
    names = [n for n, _, _ in VEC_ROWS]

    def body(*refs):
        o_ref = refs[-1]
        lb_ref, loss_ref = refs[len(names)], refs[len(names) + 1]
        o_ref[...] = jnp.zeros_like(o_ref)
        o_ref[LOSS_AT[0]:LOSS_AT[0] + 1, LOSS_AT[1]:LOSS_AT[1] + LANE] = jnp.broadcast_to(loss_ref[...], (1, LANE))
        for (name, row, size), ref in zip(VEC_ROWS, refs):
            if name == "g_hgrn":
                r = lax.broadcasted_iota(jnp.int32, (NH * V_DIM, LANE), 0)
                c = lax.broadcasted_iota(jnp.int32, (NH * V_DIM, LANE), 1)
                fold = ((r % V_DIM) == c).astype(F32)
                o_ref[row:row + 1, 0:LANE] = jnp.dot(ref[...], fold, precision=HIGHEST, preferred_element_type=F32)
            else:
                o_ref[row:row + 1, 0:size] = ref[...]
        o_ref[VEC_LB_ROW:VEC_LB_ROW + 2, 0:512] = lb_ref[...]

    return pl.pallas_call(body, name="vec_pack", out_shape=jax.ShapeDtypeStruct(VEC_SHAPE, F32))(
        *[vg[n] for n in names], vg["lb_logits"], loss)


def _adamw_vec(p_mine, p_sibling, w, m, v):
    names = [n for n, _, _ in VEC_ROWS] + ["lb_logits"]
    k = len(names)

    def body(a_ref, b_ref, *refs):
        ins, outs = refs[:3 * k], refs[3 * k:]
        at = (slice(LOSS_AT[0], LOSS_AT[0] + 1), slice(LOSS_AT[1], LOSS_AT[1] + LANE))
        outs[-1][...] = a_ref[at] + b_ref[at]
        for i, name in enumerate(names):
            if name == "lb_logits":
                rows, cols = slice(VEC_LB_ROW, VEC_LB_ROW + 2), slice(0, 512)
            else:
                _, row, size = VEC_ROWS[i]
                rows, cols = slice(row, row + 1), slice(0, size)
            g = a_ref[rows, cols] + b_ref[rows, cols]
            d, nm, nv = _adamw_math(g, ins[i][...], ins[k + i][...], ins[2 * k + i][...])
            for o_ref, val in zip(outs[4 * i:4 * i + 4], (g, d, nm, nv)):
                o_ref[...] = val

    shapes = [jax.ShapeDtypeStruct(w[n].shape, F32) for n in names for _ in range(4)] + [jax.ShapeDtypeStruct((1, LANE), F32)]
    res = pl.pallas_call(body, name="adamw_vec", out_shape=shapes)(
        p_mine, p_sibling, *[w[n] for n in names], *[m[n] for n in names], *[v[n] for n in names])
    return [{n: res[4 * i + j] for i, n in enumerate(names)} for j in range(4)], res[-1]


WEIGHTS = ("g_pre", "w_in", "b_gate", "g_q", "w_uq", "g_kv", "w_ukv", "lb_logits", "g_hgrn", "w_branch_a", "w_branch_b", "w_out", "g_post")


def kernel(x, g_pre, w_in, b_gate, g_q, w_uq, g_kv, w_ukv, lb_logits, g_hgrn, w_branch_a, w_branch_b, w_out, g_post, loss_target, m_g_pre, m_w_in, m_b_gate, m_g_q, m_w_uq, m_g_kv, m_w_ukv, m_lb_logits, m_g_hgrn, m_w_branch_a, m_w_branch_b, m_w_out, m_g_post, v_g_pre, v_w_in, v_b_gate, v_g_q, v_w_uq, v_g_kv, v_w_ukv, v_lb_logits, v_g_hgrn, v_w_branch_a, v_w_branch_b, v_w_out, v_g_post):
    w = dict(g_pre=g_pre, w_in=w_in, b_gate=b_gate, g_q=g_q, w_uq=w_uq, g_kv=g_kv, w_ukv=w_ukv, lb_logits=lb_logits, g_hgrn=g_hgrn,
             w_branch_a=w_branch_a, w_branch_b=w_branch_b, w_out=w_out, g_post=g_post)
    m = dict(g_pre=m_g_pre, w_in=m_w_in, b_gate=m_b_gate, g_q=m_g_q, w_uq=m_w_uq, g_kv=m_g_kv, w_ukv=m_w_ukv, lb_logits=m_lb_logits,
             g_hgrn=m_g_hgrn, w_branch_a=m_w_branch_a, w_branch_b=m_w_branch_b, w_out=m_w_out, g_post=m_g_post)
    v = dict(g_pre=v_g_pre, w_in=v_w_in, b_gate=v_b_gate, g_q=v_g_q, w_uq=v_w_uq, g_kv=v_g_kv, w_ukv=v_w_ukv, lb_logits=v_lb_logits,
             g_hgrn=v_g_hgrn, w_branch_a=v_w_branch_a, w_branch_b=v_w_branch_b, w_out=v_w_out, g_post=v_g_post)
    blocks = {n: _to_block(n, w[n]).astype(BF16) for n in BIG}
    w_in_all = _gather_w_in(blocks["w_in"])
    weights = _LaterWeights(blocks, w_in_all)
    state = {n: [_to_block(n, t[n]) for t in (w, m, v)] for n in BIG}
    exchange = _GradExchange(state)
    loss, grad_x, _, vec_grads = _local_step(
        x[0], loss_target[0], g_pre, _join_chips("w_in", w_in_all), b_gate, g_q, g_kv, lb_logits, g_hgrn, g_post, weights, exchange)
    vec = _vec_pack(vec_grads, loss)
    vec_started, token = _exchange_start([vec], (), "vec_start")
    sums = exchange.finish(token)
    rest = tuple(sums)
    (vec,), (vec_landed,) = _exchange_wait([vec], (), vec_started, exchange.outs["w_in"][0], "vec_wait")
    mine = [sums[n] for n in rest] + [_sum_landed(vec_landed, vec, "sum_vec")]
    theirs = _sibling_exchange(mine, "sibling_grads")
    done = dict(exchange.outs)
    small = _adamw_small(mine[:-1], theirs[:-1], [state[n] for n in rest], "adamw_late")
    done.update(zip(rest, small))
    outs = [{}, {}, {}, {}]
    for n in BIG:
        for o, val in zip(outs, done[n]):
            o[n] = _from_block(n, val)
    vec_outs, total = _adamw_vec(mine[-1], theirs[-1], w, m, v)
    for o, vals in zip(outs, vec_outs):
        o.update(vals)
    return (total[0, 0], grad_x[None], *[o[n] for o in outs for n in WEIGHTS])
```

```python
import math

import numpy as np
import jax
import jax.numpy as jnp
from jax import lax
from jax.experimental import pallas as pl
from jax.experimental.pallas import tpu as pltpu

F32 = jnp.float32
BF16 = jnp.bfloat16
HIGHEST = lax.Precision.HIGHEST

D = 1024
NH = 8
QK_NOPE, QK_ROPE, V_DIM = 64, 32, 64
Q_LORA, KV_LORA = 768, 256
CHUNK = 64
HG_BLOCK = 32
EPS = 1e-6
LANE = 128
P_MERGE, P_GA, P_HQ, P_HF, P_HI, P_GB, P_CQ, P_CKV, P_KPE = 0, 2048, 2560, 3072, 3584, 4096, 4608, 5376, 5632
D_P = 5760
O_CQ, O_CKV, O_KPE, O_GA, O_HQ, O_HF, O_HI, O_GB, O_MERGE = 0, 768, 1024, 1056, 1568, 2080, 2592, 3104, 3616

TM = 512
TM_MID = 256
TQ = 1024
ONES_LANE = (LANE - 1, 0)
TH = 256
HG_PAIRS = 4
VMEM_LIMIT = 56 * 1024 * 1024

ADAM_LR, ADAM_B1, ADAM_B2, ADAM_EPS, ADAM_WD, ADAM_STEP = 0.001, 0.9, 0.999, 1e-08, 0.01, 10

NT_DIMS = (((1,), (1,)), ((), ()))
TN_DIMS = (((0,), (0,)), ((), ()))


def _params(sem):
    return pltpu.CompilerParams(dimension_semantics=sem, vmem_limit_bytes=VMEM_LIMIT)


def _mm(a, b):
    return jnp.dot(a, b, preferred_element_type=F32)


def _mm_nt(a, b):
    return lax.dot_general(a, b, NT_DIMS, preferred_element_type=F32)


def _mm_tn(a, b):
    return lax.dot_general(a, b, TN_DIMS, preferred_element_type=F32)


def _sigmoid(z):
    return jax.nn.sigmoid(z)


def _rope(v, c, s1, s2):
    return v * c + pltpu.roll(v, 112, 1) * s1 + pltpu.roll(v, 16, 1) * s2


def _rope_t(dy, c, s1, s2):
    return dy * c + pltpu.roll(dy * s1, 16, 1) + pltpu.roll(dy * s2, 112, 1)


def _rope_tables(s):
    f32 = np.float32
    inv = f32(10000.0) ** (-np.arange(0, QK_ROPE, 2, dtype=f32) / f32(QK_ROPE))
    ang = np.arange(s, dtype=f32)[:, None] * inv[None, :]
    cos, sin = np.cos(ang).astype(f32), np.sin(ang).astype(f32)
    z64, z32, o64, o32 = np.zeros((s, 64), f32), np.zeros((s, 32), f32), np.ones((s, 64), f32), np.ones((s, 32), f32)
    z16 = np.zeros((s, 16), f32)
    c = np.concatenate([o64, cos, cos, o32], axis=1)
    s1 = np.concatenate([z64, -sin, z16, z32], axis=1)
    s2 = np.concatenate([z64, z16, sin, z32], axis=1)
    return jnp.asarray(c), jnp.asarray(s1), jnp.asarray(s2)


W_IN_RUNS = ((O_MERGE, 2048, P_MERGE), (O_GA, O_MERGE - O_GA, P_GA), (O_CQ, O_KPE - O_CQ, P_CQ))


def _kpe_block(w_in_t):
    z = lambda n: jnp.zeros((n, w_in_t.shape[1]), w_in_t.dtype)
    return jnp.concatenate([z(64), w_in_t[O_KPE:O_KPE + QK_ROPE], z(32)], axis=0)


def _front_fwd(x, g_pre, w_in_t, w_kpe, tokens=()):
    s = x.shape[0]
    tokens = list(tokens)

    def body(x_ref, g_ref, w_ref, k_ref, *refs):
        o_ref, h_ref = refs[len(tokens):]
        xv = x_ref[...]
        r = lax.rsqrt(jnp.mean(xv * xv, axis=-1, keepdims=True) + EPS)
        h = ((xv * r) * g_ref[...]).astype(BF16)
        h_ref[...] = h
        for row, rows, col in W_IN_RUNS:
            o_ref[:, col:col + rows] = _mm_nt(h, w_ref[row:row + rows, :])
        o_ref[:, P_KPE:P_KPE + LANE] = _mm_nt(h, k_ref[...])

    full = lambda a: pl.BlockSpec(a.shape, lambda i: (0,) * a.ndim)
    return pl.pallas_call(
        body, name="front_fwd", grid=(s // TM,),
        in_specs=[pl.BlockSpec((TM, D), lambda i: (i, 0)), pl.BlockSpec((1, D), lambda i: (0, 0)), full(w_in_t), full(w_kpe)]
        + [pl.BlockSpec((8, LANE), lambda i: (0, 0))] * len(tokens),
        out_specs=[pl.BlockSpec((TM, D_P), lambda i: (i, 0)), pl.BlockSpec((TM, D), lambda i: (i, 0))],
        out_shape=[jax.ShapeDtypeStruct((s, D_P), F32), jax.ShapeDtypeStruct((s, D), BF16)],
        compiler_params=_params(("parallel",)),
    )(x, g_pre, w_in_t, w_kpe, *tokens)


def _norm_rows(v, g):
    r = lax.rsqrt(jnp.mean(v * v, axis=-1, keepdims=True) + EPS)
    return (v * r) * g, r


def _qkv_fwd(proj, g_q, g_kv, w_uq_p, w_k_p, w_v_p, rc, rs1, rs2):
    s = proj.shape[0]

    def body(cq_ref, ckv_ref, kpe_ref, gq_ref, gkv_ref, wq_ref, wk_ref, wv_ref, c_ref, s1_ref, s2_ref, q_ref, k_ref, v_ref):
        c, s1, s2 = c_ref[...], s1_ref[...], s2_ref[...]
        cqn, _ = _norm_rows(cq_ref[...], gq_ref[...])
        ckvn, _ = _norm_rows(ckv_ref[...], gkv_ref[...])
        ckvn = ckvn.astype(BF16)
        qf = _mm(cqn.astype(BF16), wq_ref[...])
        kf = jnp.concatenate([_mm(ckvn, wk_ref[t]) for t in range(N_CHIPS)], axis=1)
        vf = jnp.concatenate([_mm(ckvn, wv_ref[t]) for t in range(N_CHIPS)], axis=1)
        kpe = _rope(kpe_ref[...], c, s1, s2)
        lane = lax.broadcasted_iota(jnp.int32, (TM, LANE), 1)
        for h in range(NH):
            blk = slice(h * LANE, (h + 1) * LANE)
            q_ref[h] = _rope(qf[:, blk], c, s1, s2).astype(BF16)
            k_ref[h] = (kf[:, blk] + kpe).astype(BF16)
            v_ref[h] = jnp.where(lane == ONES_LANE[h % 2], 1.0, vf[:, blk]).astype(BF16)

    row = lambda w, j: pl.BlockSpec((TM, w), lambda i: (i, j))
    full = lambda a: pl.BlockSpec(a.shape, lambda i: (0,) * a.ndim)
    hs = jax.ShapeDtypeStruct((NH, s, LANE), BF16)
    return pl.pallas_call(
        body, name="qkv_fwd", grid=(s // TM,),
        in_specs=[row(Q_LORA, P_CQ // Q_LORA), row(KV_LORA, P_CKV // KV_LORA), row(LANE, P_KPE // LANE),
                  full(g_q), full(g_kv), full(w_uq_p), full(w_k_p), full(w_v_p), row(LANE, 0), row(LANE, 0), row(LANE, 0)],
        out_specs=[pl.BlockSpec((NH, TM, LANE), lambda i: (0, i, 0))] * 3,
        out_shape=[hs, hs, hs],
        compiler_params=_params(("parallel",)),
    )(proj, proj, proj, g_q, g_kv, w_uq_p, w_k_p, w_v_p, rc, rs1, rs2)


LOG2E = 1.4426950408889634
QK_SCALE2 = LOG2E / math.sqrt(QK_NOPE + QK_ROPE)


HQ = TQ // 2


def _diag_visible(n):
    row = lax.broadcasted_iota(jnp.int32, (n, n), 0)
    col = lax.broadcasted_iota(jnp.int32, (n, n), 1)
    return (col // CHUNK) <= (row // CHUNK)


def _attn_fwd(q, k, vv):
    s = q.shape[1]

    def body(q_ref, k_ref, v_ref, o_ref, lse_ref):
        i = pl.program_id(1)
        qs = (q_ref[0], q_ref[1])

        def tiles(t, carry, diag):
            rows = pl.ds(pl.multiple_of(t * TQ, TQ), TQ)
            sc = [_mm_nt(qs[hh], k_ref[hh, rows, :]) for hh in range(2)]
            if diag:
                sc = [jnp.where(_diag_visible(TQ), s_, -jnp.inf) for s_ in sc]
            m_new = [jnp.maximum(carry[hh][0], jnp.max(sc[hh], axis=-1, keepdims=True)) for hh in range(2)]
            alpha = [jnp.exp2((carry[hh][0] - m_new[hh]) * QK_SCALE2) for hh in range(2)]
            p = [jnp.exp2((sc[hh] - m_new[hh]) * QK_SCALE2).astype(BF16) for hh in range(2)]
            acc = [alpha[hh] * carry[hh][1] + _mm(p[hh], v_ref[hh, rows, :]) for hh in range(2)]
            return (m_new[0], acc[0]), (m_new[1], acc[1])

        init = (jnp.full((TQ, 1), -jnp.inf, F32), jnp.zeros((TQ, LANE), F32))
        carry = lax.fori_loop(0, i, lambda t, c: tiles(t, c, False), (init, init))
        carry = tiles(i, carry, True)
        lane = lax.broadcasted_iota(jnp.int32, (TQ, LANE), 1)
        out = jnp.zeros((TQ, LANE), F32)
        for hh in range(2):
            m, acc = carry[hh]
            l = jnp.sum(jnp.where(lane == ONES_LANE[hh], acc, 0.0), axis=-1, keepdims=True)
            out = out + jnp.where((lane < V_DIM) == (hh == 0), acc, 0.0) / l
            lse_ref[hh] = jnp.broadcast_to(m * QK_SCALE2 + jnp.log(l) * LOG2E, (TQ, LANE))
        o_ref[...] = out

    return pl.pallas_call(
        body, name="attn_fwd", grid=(NH // 2, s // TQ),
        in_specs=[pl.BlockSpec((2, TQ, LANE), lambda p, i: (p, i, 0)), pl.BlockSpec((2, s, LANE), lambda p, i: (p, 0, 0)),
                  pl.BlockSpec((2, s, LANE), lambda p, i: (p, 0, 0))],
        out_specs=[pl.BlockSpec((TQ, LANE), lambda p, i: (i, p)), pl.BlockSpec((2, TQ, LANE), lambda p, i: (p, i, 0))],
        out_shape=[jax.ShapeDtypeStruct((s, NH * V_DIM), F32), jax.ShapeDtypeStruct((NH, s, LANE), F32)],
        compiler_params=_params(("parallel", "parallel")),
    )(q, k, vv)


def _lower_bound(lbl):
    a0, a1 = lbl[0:1, :], lbl[1:2, :]
    mx = jnp.maximum(a0, a1)
    e0, e1 = jnp.exp(a0 - mx), jnp.exp(a1 - mx)
    return e0 / (e0 + e1)


def _chunk_cumsum(v, reverse=False):
    pos = lax.broadcasted_iota(jnp.int32, v.shape, 0) % HG_BLOCK
    s = 1
    while s < HG_BLOCK:
        if reverse:
            v = v + jnp.where(pos < HG_BLOCK - s, pltpu.roll(v, TH - s, 0), 0.0)
        else:
            v = v + jnp.where(pos >= s, pltpu.roll(v, s, 0), 0.0)
        s *= 2
    return v


def _hgrn_gates(hq, hf, lb):
    sig = _sigmoid(hf)
    f = lb + (1.0 - lb) * sig
    g = jnp.log(f)
    kk = 1.0 - f
    r = lax.broadcasted_iota(jnp.int32, (TH, TH), 0)
    c = lax.broadcasted_iota(jnp.int32, (TH, TH), 1)
    tri = ((r // HG_BLOCK) == (c // HG_BLOCK)) & (r >= c)
    cum = _chunk_cumsum(g)
    nch = TH // HG_BLOCK
    total = _chunks(cum)[:, HG_BLOCK - 1:HG_BLOCK, :]
    lastb = jnp.broadcast_to(total, (nch, HG_BLOCK, hf.shape[-1])).reshape(hf.shape)
    e, ei, ee = jnp.exp(cum), jnp.exp(-cum), jnp.exp(lastb - cum)
    return dict(sig=sig, f=f, kk=kk, tri=tri, cum=cum, total=total, decay=jnp.exp(total), e=e, ei=ei, ee=ee,
                qd=hq * e, ki=kk * ei, ke=kk * ee)


def _chunks(v):
    return v.reshape(TH // HG_BLOCK, HG_BLOCK, v.shape[-1])


def _bmm_nt(a, b):
    return lax.dot_general(a, b, (((2,), (2,)), ((0,), (0,))), preferred_element_type=F32)


def _bmm_nn(a, b):
    return lax.dot_general(a, b, (((2,), (1,)), ((0,), (0,))), preferred_element_type=F32)


def _bmm_tn(a, b):
    return lax.dot_general(a, b, (((1,), (1,)), ((0,), (0,))), preferred_element_type=F32)


def _pair_masks():
    lane = lax.broadcasted_iota(jnp.int32, (TH, LANE), 1)
    kr = lax.broadcasted_iota(jnp.int32, (LANE, LANE), 0)
    kc = lax.broadcasted_iota(jnp.int32, (LANE, LANE), 1)
    return lane < 64, (kr // 64) == (kc // 64)


def _hgrn_fwd(proj, lbl):
    s = proj.shape[0]
    nch = TH // HG_BLOCK

    def body(hq_ref, hf_ref, hi_ref, lbl_ref, o_ref, st_ref, st):
        @pl.when(pl.program_id(1) == 0)
        def _():
            st[...] = jnp.zeros_like(st)

        m0, bd = _pair_masks()
        gt = _hgrn_gates(hq_ref[...], hf_ref[...], _lower_bound(lbl_ref[...]))
        v_b, qd, qd_b = hi_ref[...].astype(BF16), gt["qd"], gt["qd"].astype(BF16)
        ki_b, ke_b = gt["ki"].astype(BF16), gt["ke"].astype(BF16)
        pairs = [slice(u * LANE, (u + 1) * LANE) for u in range(HG_PAIRS)]
        heads = [(lanes, m0 if hh == 0 else jnp.logical_not(m0)) for lanes in pairs for hh in range(2)]
        a_b = [jnp.where(gt["tri"], _mm_nt(jnp.where(mh, qd[:, lanes], 0.0).astype(BF16), ki_b[:, lanes]), 0.0).astype(BF16)
               for lanes, mh in heads]
        intra = [jnp.where(m0, _mm(a_b[2 * u], v_b[:, lanes]), _mm(a_b[2 * u + 1], v_b[:, lanes])) for u, lanes in enumerate(pairs)]
        upd = [_bmm_tn(_chunks(v_b[:, lanes]), _chunks(ke_b[:, lanes])) for lanes in pairs]
        entering = []
        for u, lanes in enumerate(pairs):
            cur, states = st[u], []
            for n in range(nch):
                states.append(cur)
                cur = gt["decay"][n][:, lanes] * cur + jnp.where(bd, upd[u][n], 0.0)
            st[u] = cur
            entering.append(jnp.stack(states))
            st_ref[u] = entering[u]
        for u, lanes in enumerate(pairs):
            o_ref[:, lanes] = intra[u] + _bmm_nt(_chunks(qd_b[:, lanes]), entering[u].astype(BF16)).reshape(TH, LANE)

    wide = HG_PAIRS * LANE
    col = lambda base: pl.BlockSpec((TH, wide), lambda p, i: (i, base // wide + p))
    return pl.pallas_call(
        body, name="hgrn_fwd", grid=(NH // 2 // HG_PAIRS, s // TH),
        in_specs=[col(P_HQ), col(P_HF), col(P_HI), pl.BlockSpec((2, wide), lambda p, i: (0, p))],
        out_specs=[pl.BlockSpec((TH, wide), lambda p, i: (i, p)),
                   pl.BlockSpec((HG_PAIRS, nch, LANE, LANE), lambda p, i: (p, i, 0, 0))],
        out_shape=[jax.ShapeDtypeStruct((s, 512), F32), jax.ShapeDtypeStruct((NH // 2, s // HG_BLOCK, LANE, LANE), F32)],
        scratch_shapes=[pltpu.VMEM((HG_PAIRS, LANE, LANE), F32)],
        compiler_params=_params(("parallel", "arbitrary")),
    )(proj, proj, proj, lbl)


def _group_sum(v):
    low = lax.broadcasted_iota(jnp.int32, (v.shape[0], LANE), 1) < V_DIM
    blocks = []
    for b in range(v.shape[1] // LANE):
        blk = v[:, b * LANE:(b + 1) * LANE]
        s_low = jnp.sum(jnp.where(low, blk, 0.0), axis=-1, keepdims=True)
        s_high = jnp.sum(jnp.where(low, 0.0, blk), axis=-1, keepdims=True)
        blocks.append(jnp.where(low, s_low, s_high))
    return jnp.concatenate(blocks, axis=1)


def _dsilu(z, sg):
    return sg * (1.0 + z * (1.0 - sg))


def _mid(proj, attn, o_raw, x, tgt, g_hg, b_gate, g_post, wa, wb, w_out):
    s = x.shape[0]

    def body(attn_ref, ga_ref, o_ref, gb_ref, mg_ref, x_ref, t_ref, ghg_ref, bg_ref, gp_ref, wa_ref, wb_ref, wo_ref,
             loss_ref, dout_ref, dattn_ref, dga_ref, dor_ref, dgb_ref, dmg_ref, dwo_out, dwa_out, dwb_out, dgp_ref, dbg_ref, dghg_ref,
             dwo_ref, dwa_ref, dwb_ref):
        @pl.when(pl.program_id(0) == 0)
        def _():
            for rf in (loss_ref, dwo_ref, dwa_ref, dwb_ref, dgp_ref, dbg_ref, dghg_ref):
                rf[...] = jnp.zeros_like(rf)

        attn, za, orw, zb = attn_ref[...], ga_ref[...], o_ref[...], gb_ref[...]
        ghg, gp = ghg_ref[...], gp_ref[...]
        sga, sgb = _sigmoid(za), _sigmoid(zb)
        sa, sb = za * sga, zb * sgb
        ga = attn * sa
        rh = lax.rsqrt(_group_sum(orw * orw) * (1.0 / V_DIM) + EPS)
        on = (orw * rh) * ghg
        gb = on * sb
        ga_b, gb_b = ga.astype(BF16), gb.astype(BF16)
        blocks = [slice(t * (D // N_CHIPS), (t + 1) * (D // N_CHIPS)) for t in range(N_CHIPS)]
        ya = jnp.concatenate([_mm(ga_b, wa_ref[t]) for t in range(N_CHIPS)], axis=1)
        yb = jnp.concatenate([_mm(gb_b, wb_ref[t]) for t in range(N_CHIPS)], axis=1)
        gates = _sigmoid(mg_ref[...] + bg_ref[...])
        g0, g1 = gates[:, :D], gates[:, D:]
        m_b = (g0 * ya + g1 * yb).astype(BF16)
        y = _mm(m_b, wo_ref[...])
        ry = lax.rsqrt(jnp.mean(y * y, axis=-1, keepdims=True) + EPS)
        out = x_ref[...] + (y * ry) * gp
        err = out - t_ref[...]
        loss_ref[...] += 0.5 * jnp.sum(jnp.mean(err * err, axis=-1, keepdims=True), axis=0, keepdims=True)
        dout = err * (1.0 / D)
        dout_ref[...] = dout
        dgp_ref[...] += jnp.sum(dout * (y * ry), axis=0, keepdims=True)
        dgy = dout * gp
        dy = ry * dgy - y * (ry * ry * ry) * jnp.mean(y * dgy, axis=-1, keepdims=True)
        dy_b = dy.astype(BF16)
        dm = _mm_nt(dy_b, wo_ref[...])
        dya_b, dyb_b = (dm * g0).astype(BF16), (dm * g1).astype(BF16)
        dga = sum(_mm_nt(dya_b[:, cols], wa_ref[t]) for t, cols in enumerate(blocks))
        dgb = sum(_mm_nt(dyb_b[:, cols], wb_ref[t]) for t, cols in enumerate(blocks))
        dwo_ref[...] += _mm_tn(m_b, dy_b)
        for t, cols in enumerate(blocks):
            dwa_ref[t] += _mm_tn(ga_b, dya_b[:, cols])
            dwb_ref[t] += _mm_tn(gb_b, dyb_b[:, cols])
        dg0, dg1 = dm * ya, dm * yb
        dmg = jnp.concatenate([dg0 * g0 * (1.0 - g0), dg1 * g1 * (1.0 - g1)], axis=1)
        dmg_ref[...] = dmg.astype(BF16)
        dbg_ref[...] += jnp.sum(dmg, axis=0, keepdims=True)
        dattn_ref[...] = dga * sa
        dga_ref[...] = (dga * attn * _dsilu(za, sga)).astype(BF16)
        dgb_ref[...] = (dgb * on * _dsilu(zb, sgb)).astype(BF16)
        don = dgb * sb
        dghg_ref[...] += jnp.sum(don * (orw * rh), axis=0, keepdims=True)
        dgo = don * ghg
        dor_ref[...] = rh * dgo - orw * (rh * rh * rh) * (_group_sum(orw * dgo) * (1.0 / V_DIM))

        @pl.when(pl.program_id(0) == pl.num_programs(0) - 1)
        def _():
            for out, rf in ((dwo_out, dwo_ref), (dwa_out, dwa_ref), (dwb_out, dwb_ref)):
                out[...] = rf[...].astype(BF16)

    row = lambda w, j=0: pl.BlockSpec((TM_MID, w), lambda i: (i, j))
    full = lambda a: pl.BlockSpec(a.shape, lambda i: (0,) * a.ndim)
    acc = lambda shape: pl.BlockSpec(shape, lambda i: (0,) * len(shape))
    slabs = (N_CHIPS, 512, D // N_CHIPS)
    sds = jax.ShapeDtypeStruct
    return pl.pallas_call(
        body, name="mid", grid=(s // TM_MID,),
        in_specs=[row(512), row(512, P_GA // 512), row(512), row(512, P_GB // 512), row(2048, P_MERGE // 2048), row(D), row(D),
                  full(g_hg), full(b_gate), full(g_post), full(wa), full(wb), full(w_out)],
        out_specs=[acc((1, 1)), row(D), row(512), row(512), row(512), row(512), row(2048),
                   acc((D, D)), acc(slabs), acc(slabs), acc((1, D)), acc((1, 2048)), acc((1, 512))],
        out_shape=[sds((1, 1), F32), sds((s, D), F32), sds((s, 512), F32), sds((s, 512), BF16), sds((s, 512), F32), sds((s, 512), BF16),
                   sds((s, 2048), BF16), sds((D, D), BF16), sds(slabs, BF16), sds(slabs, BF16), sds((1, D), F32),
                   sds((1, 2048), F32), sds((1, 512), F32)],
        scratch_shapes=[pltpu.VMEM((D, D), F32), pltpu.VMEM(slabs, F32), pltpu.VMEM(slabs, F32)],
        compiler_params=_params(("arbitrary",)),
    )(attn, proj, o_raw, proj, proj, x, tgt, g_hg, b_gate, g_post, wa, wb, w_out)


def _attn_bwd(q, k, vv, attn, dattn, lse, token):
    s = q.shape[1]
    nt = s // TQ
    scale = 1.0 / math.sqrt(QK_NOPE + QK_ROPE)

    def body(q_ref, k_ref, v_ref, o_ref, do_ref, lse_ref, token_ref, dq_ref, dk_ref, dv_ref, do_s, delta_s):
        j = pl.program_id(1)

        @pl.when(j == 0)
        def _():
            dq_ref[...] = jnp.zeros_like(dq_ref)
            lane = lax.broadcasted_iota(jnp.int32, (TQ, LANE), 1)

            @pl.loop(0, nt)
            def _(i):
                rows = pl.ds(pl.multiple_of(i * TQ, TQ), TQ)
                do, o = do_ref[rows, :], o_ref[rows, :]
                for hh in range(2):
                    doh = jnp.where((lane < 64) if hh == 0 else (lane >= 64), do, 0.0)
                    do_s[hh, rows, :] = doh.astype(BF16)
                    delta_s[hh, rows, :] = jnp.broadcast_to(jnp.sum(doh * o, axis=-1, keepdims=True), (TQ, LANE))

        kjs, vjs = (k_ref[0], k_ref[1]), (v_ref[0], v_ref[1])

        def tile(hh, start, size, kj, vj, diag):
            rows = pl.ds(pl.multiple_of(start, size), size)
            wide = lambda a: jnp.concatenate([a] * (kj.shape[0] // LANE), axis=1)
            qi, do_b = q_ref[hh, rows, :], do_s[hh, rows, :]
            sc, dp = _mm_nt(qi, kj), _mm_nt(do_b, vj)
            p = jnp.exp2(sc * QK_SCALE2 - wide(lse_ref[hh, rows, :]))
            if diag:
                p = jnp.where(_diag_visible(size), p, 0.0)
            ds_b = (p * (dp - wide(delta_s[hh, rows, :]))).astype(BF16)
            dv, dk = _mm_tn(do_b, p.astype(BF16)), _mm_tn(qi, ds_b)
            dq_ref[hh, rows, :] += _mm(ds_b, kj)
            return dk, dv

        def step(i, carry):
            new = [tile(hh, i * TQ, TQ, kjs[hh], vjs[hh], False) for hh in range(2)]
            return tuple((carry[hh][0] + new[hh][0], carry[hh][1] + new[hh][1]) for hh in range(2))

        def diagonal(hh):
            k0, k1, v0, v1 = kjs[hh][:HQ], kjs[hh][HQ:], vjs[hh][:HQ], vjs[hh][HQ:]
            a = tile(hh, j * TQ, HQ, k0, v0, True)
            b = tile(hh, j * TQ + HQ, HQ, k0, v0, False)
            c = tile(hh, j * TQ + HQ, HQ, k1, v1, True)
            return jnp.concatenate([a[0] + b[0], c[0]], axis=1), jnp.concatenate([a[1] + b[1], c[1]], axis=1)

        carry = lax.fori_loop(j + 1, nt, step, (diagonal(0), diagonal(1)))
        for hh in range(2):
            dk_ref[hh] = carry[hh][0].T * scale
            dv_ref[hh] = carry[hh][1].T

        @pl.when(j == nt - 1)
        def _():
            dq_ref[...] = dq_ref[...] * scale

    whole = pl.BlockSpec((2, s, LANE), lambda p, j: (p, 0, 0))
    tile_spec = pl.BlockSpec((2, TQ, LANE), lambda p, j: (p, j, 0))
    cols = pl.BlockSpec((s, LANE), lambda p, j: (0, p))
    hs = jax.ShapeDtypeStruct((NH, s, LANE), F32)
    return pl.pallas_call(
        body, name="attn_bwd", grid=(NH // 2, nt),
        in_specs=[whole, tile_spec, tile_spec, cols, cols, whole, pl.BlockSpec((8, LANE), lambda p, j: (0, 0))],
        out_specs=[whole, tile_spec, tile_spec],
        out_shape=[hs, hs, hs],
        scratch_shapes=[pltpu.VMEM((2, s, LANE), BF16), pltpu.VMEM((2, s, LANE), F32)],
        compiler_params=_params(("parallel", "arbitrary")),
    )(q, k, vv, attn, dattn, lse, token)


def _hgrn_bwd(proj, lbl, states, do_raw):
    s = proj.shape[0]
    nt = s // TH
    nch = TH // HG_BLOCK

    def body(hq_ref, hf_ref, hi_ref, lbl_ref, st_ref, do_ref, dh_ref, dlbl_ref, dst, dlb):
        step = pl.program_id(1)

        @pl.when(step == 0)
        def _():
            dst[...] = jnp.zeros_like(dst)
            dlb[...] = jnp.zeros_like(dlb)

        m0, bd = _pair_masks()
        lb = _lower_bound(lbl_ref[...])
        gt = _hgrn_gates(hq_ref[...], hf_ref[...], lb)
        do = do_ref[...]
        qd, ki, ke = gt["qd"], gt["ki"], gt["ke"]
        v_b, do_b = hi_ref[...].astype(BF16), do.astype(BF16)
        qd_b, ki_b, ke_b = qd.astype(BF16), ki.astype(BF16), ke.astype(BF16)
        pairs = [slice(u * LANE, (u + 1) * LANE) for u in range(HG_PAIRS)]
        heads = [(lanes, m0 if hh == 0 else jnp.logical_not(m0)) for lanes in pairs for hh in range(2)]
        a_b = [jnp.where(gt["tri"], _mm_nt(jnp.where(mh, qd[:, lanes], 0.0).astype(BF16), ki_b[:, lanes]), 0.0).astype(BF16)
               for lanes, mh in heads]
        doh_b = [jnp.where(mh, do[:, lanes], 0.0).astype(BF16) for lanes, mh in heads]
        da_b = [jnp.where(gt["tri"], _mm_nt(d, v_b[:, lanes]), 0.0).astype(BF16) for d, (lanes, _) in zip(doh_b, heads)]
        dv_p, dqd_p, dki_p = [], [], []
        for u, lanes in enumerate(pairs):
            e, o = 2 * u, 2 * u + 1
            dv_p.append(_mm_tn(a_b[e], doh_b[e]) + _mm_tn(a_b[o], doh_b[o]))
            dqd_p.append(jnp.where(m0, _mm(da_b[e], ki_b[:, lanes]), _mm(da_b[o], ki_b[:, lanes])))
            dki_p.append(jnp.where(m0, _mm_tn(da_b[e], qd_b[:, lanes]), _mm_tn(da_b[o], qd_b[:, lanes])))
        fed = [_bmm_tn(_chunks(do_b[:, lanes]), _chunks(qd_b[:, lanes])) for lanes in pairs]
        leaving = []
        for u, lanes in enumerate(pairs):
            ds, left = dst[u], [None] * nch
            for n in reversed(range(nch)):
                left[n] = ds
                ds = gt["decay"][n][:, lanes] * ds + jnp.where(bd, fed[u][n], 0.0)
            dst[u] = ds
            leaving.append(jnp.stack(left))
        dke_p, dlast_p = [], []
        for u, lanes in enumerate(pairs):
            entering, leaving_b = st_ref[u], leaving[u].astype(BF16)
            dke3 = _bmm_nn(_chunks(v_b[:, lanes]), leaving_b)
            dv_p[u] = dv_p[u] + _bmm_nt(_chunks(ke_b[:, lanes]), leaving_b).reshape(TH, LANE)
            dqd_p[u] = dqd_p[u] + _bmm_nn(_chunks(do_b[:, lanes]), entering.astype(BF16)).reshape(TH, LANE)
            dke_p.append(dke3.reshape(TH, LANE))
            dlast_p.append(jnp.sum(dke3 * _chunks(ke[:, lanes]), axis=1, keepdims=True)
                           + jnp.sum(leaving[u] * entering, axis=1, keepdims=True) * gt["decay"][:, :, lanes])
        cat = lambda parts: jnp.concatenate(parts, axis=-1)
        dv, dqd, dki, dke, dlast = cat(dv_p), cat(dqd_p), cat(dki_p), cat(dke_p), cat(dlast_p)
        dk = dki * gt["ei"] + dke * gt["ee"]
        dcum = dqd * qd - dki * ki - dke * ke
        dg = _chunk_cumsum(dcum, reverse=True) + jnp.broadcast_to(dlast, (nch, HG_BLOCK, dlast.shape[-1])).reshape(dcum.shape)
        sig = gt["sig"]
        df = dg / gt["f"] - dk
        dlb[...] += jnp.sum(df * (1.0 - sig), axis=0, keepdims=True)
        dh_ref[0] = (dqd * gt["e"]).astype(BF16)
        dh_ref[1] = ((df * (1.0 - lb)) * sig * (1.0 - sig)).astype(BF16)
        dh_ref[2] = dv.astype(BF16)

        @pl.when(step == nt - 1)
        def _():
            lb = _lower_bound(lbl_ref[...])
            da0 = dlb[...] * lb * (1.0 - lb)
            dlbl_ref[...] = jnp.concatenate([da0, -da0], axis=0)

    wide = HG_PAIRS * LANE
    col = lambda base: pl.BlockSpec((TH, wide), lambda p, i: (nt - 1 - i, base // wide + p))
    tile = pl.BlockSpec((TH, wide), lambda p, i: (nt - 1 - i, p))
    sds = jax.ShapeDtypeStruct
    return pl.pallas_call(
        body, name="hgrn_bwd", grid=(NH // 2 // HG_PAIRS, nt),
        in_specs=[col(P_HQ), col(P_HF), col(P_HI), pl.BlockSpec((2, wide), lambda p, i: (0, p)),
                  pl.BlockSpec((HG_PAIRS, nch, LANE, LANE), lambda p, i: (p, nt - 1 - i, 0, 0)), tile],
        out_specs=[pl.BlockSpec((3, TH, wide), lambda p, i: (0, nt - 1 - i, p)), pl.BlockSpec((2, wide), lambda p, i: (0, p))],
        out_shape=[sds((3, s, 512), BF16), sds((2, 512), F32)],
        scratch_shapes=[pltpu.VMEM((HG_PAIRS, LANE, LANE), F32), pltpu.VMEM((1, wide), F32)],
        compiler_params=_params(("parallel", "arbitrary")),
    )(proj, proj, proj, lbl, states, do_raw)


def _norm_rows_bwd(v, r, g, dn):
    dgv = dn * g
    return r * dgv - v * (r * r * r) * jnp.mean(v * dgv, axis=-1, keepdims=True)


def _qkv_bwd(proj, dq, dk, dvv, g_q, g_kv, w_uq_p, w_k_p, w_v_p, rc, rs1, rs2):
    s = proj.shape[0]
    head_q = QK_NOPE + QK_ROPE

    def body(cq_ref, ckv_ref, dq_ref, dk_ref, dv_ref, gq_ref, gkv_ref, wq_ref, wk_ref, wv_ref, c_ref, s1_ref, s2_ref,
             dcq_ref, dckv_ref, dkpe_ref, dwq_out, dwkv_out, dgq_ref, dgkv_ref, dwq_ref, dwk_ref, dwv_ref):
        @pl.when(pl.program_id(0) == 0)
        def _():
            for rf in (dwq_ref, dwk_ref, dwv_ref, dgq_ref, dgkv_ref):
                rf[...] = jnp.zeros_like(rf)

        c, s1, s2 = c_ref[...], s1_ref[...], s2_ref[...]
        cq, ckv = cq_ref[...], ckv_ref[...]
        gq, gkv = gq_ref[...], gkv_ref[...]
        cqn, rq = _norm_rows(cq, gq)
        ckvn, rkv = _norm_rows(ckv, gkv)
        cqn_b, ckvn_b = cqn.astype(BF16), ckvn.astype(BF16)
        dqf = jnp.concatenate([_rope_t(dq_ref[h], c, s1, s2) for h in range(NH)], axis=1).astype(BF16)
        dkf = jnp.concatenate([dk_ref[h] for h in range(NH)], axis=1).astype(BF16)
        dvf = jnp.concatenate([dv_ref[h] for h in range(NH)], axis=1).astype(BF16)
        dkpe = dk_ref[0]
        for h in range(1, NH):
            dkpe = dkpe + dk_ref[h]
        lane = lax.broadcasted_iota(jnp.int32, (TM, LANE), 1)
        dkpe = jnp.where((lane >= QK_NOPE) & (lane < QK_NOPE + QK_ROPE), dkpe, 0.0)
        dkpe_ref[...] = _rope_t(dkpe, c, s1, s2).astype(BF16)
        dcqn = _mm_nt(dqf, wq_ref[...])
        pair = lambda a, t: a[:, t * 2 * LANE:(t + 1) * 2 * LANE]
        dckvn = sum(_mm_nt(pair(dkf, t), wk_ref[t]) + _mm_nt(pair(dvf, t), wv_ref[t]) for t in range(N_CHIPS))
        dwq_ref[...] += _mm_tn(cqn_b, dqf)
        dwk_ref[...] += _mm_tn(ckvn_b, dkf)
        dwv_ref[...] += _mm_tn(ckvn_b, dvf)
        dgq_ref[...] += jnp.sum(dcqn * (cq * rq), axis=0, keepdims=True)
        dgkv_ref[...] += jnp.sum(dckvn * (ckv * rkv), axis=0, keepdims=True)
        dcq_ref[...] = _norm_rows_bwd(cq, rq, gq, dcqn).astype(BF16)
        dckv_ref[...] = _norm_rows_bwd(ckv, rkv, gkv, dckvn).astype(BF16)

        @pl.when(pl.program_id(0) == pl.num_programs(0) - 1)
        def _():
            blk = lambda ref, h: ref[:, h * LANE:(h + 1) * LANE]
            lane = lax.broadcasted_iota(jnp.int32, (Q_LORA, LANE), 1)
            for j in range(NH * head_q // LANE):
                h0, w0 = divmod(j * LANE, head_q)
                first = blk(dwq_ref, h0) if w0 == 0 else pltpu.roll(blk(dwq_ref, h0), LANE - w0, 1)
                second = pltpu.roll(blk(dwq_ref, h0 + 1), head_q - w0, 1)
                dwq_out[:, j * LANE:(j + 1) * LANE] = jnp.where(lane < head_q - w0, first, second).astype(BF16)
            lane = lax.broadcasted_iota(jnp.int32, (KV_LORA, LANE), 1)
            for h in range(NH):
                vals = blk(dwv_ref, h) if h % 2 else pltpu.roll(blk(dwv_ref, h), V_DIM, 1)
                both = jnp.where(lane < QK_NOPE, blk(dwk_ref, h), vals).astype(BF16)
                dwkv_out[h // 2, :, (h % 2) * LANE:(h % 2 + 1) * LANE] = both

    row = lambda w, j=0: pl.BlockSpec((TM, w), lambda i: (i, j))
    full = lambda a: pl.BlockSpec(a.shape, lambda i: (0,) * a.ndim)
    acc = lambda *shape: pl.BlockSpec(shape, lambda i: (0,) * len(shape))
    heads = pl.BlockSpec((NH, TM, LANE), lambda i: (0, i, 0))
    sds = jax.ShapeDtypeStruct
    return pl.pallas_call(
        body, name="qkv_bwd", grid=(s // TM,),
        in_specs=[row(Q_LORA, P_CQ // Q_LORA), row(KV_LORA, P_CKV // KV_LORA), heads, heads, heads,
                  full(g_q), full(g_kv), full(w_uq_p), full(w_k_p), full(w_v_p), row(LANE), row(LANE), row(LANE)],
        out_specs=[row(Q_LORA), row(KV_LORA), row(LANE), acc(Q_LORA, NH * head_q), acc(NH // 2, KV_LORA, 2 * LANE),
                   acc(1, Q_LORA), acc(1, KV_LORA)],
        out_shape=[sds((s, Q_LORA), BF16), sds((s, KV_LORA), BF16), sds((s, LANE), BF16), sds((Q_LORA, NH * head_q), BF16),
                   sds((NH // 2, KV_LORA, 2 * LANE), BF16), sds((1, Q_LORA), F32), sds((1, KV_LORA), F32)],
        scratch_shapes=[pltpu.VMEM((Q_LORA, D), F32), pltpu.VMEM((KV_LORA, D), F32), pltpu.VMEM((KV_LORA, D), F32)],
        compiler_params=_params(("arbitrary",)),
    )(proj, proj, dq, dk, dvv, g_q, g_kv, w_uq_p, w_k_p, w_v_p, rc, rs1, rs2)


def _front_bwd(x, dout, dmg, dga, dh3, dgb, dcq, dckv, dkpe, g_pre, w_in_t, w_kpe, token):
    s = x.shape[0]

    def body(x_ref, do_ref, dmg_ref, dga_ref, dh3_ref, dgb_ref, dcq_ref, dckv_ref, dkpe_ref, g_ref, w_ref, k_ref, token_ref,
             gx_ref, dg_ref):
        @pl.when(pl.program_id(0) == 0)
        def _():
            dg_ref[...] = jnp.zeros_like(dg_ref)

        xv, g = x_ref[...], g_ref[...]
        _, r = _norm_rows(xv, g)
        pieces = ((dmg_ref[...], O_MERGE), (dga_ref[...], O_GA), (dh3_ref[0], O_HQ), (dh3_ref[1], O_HF), (dh3_ref[2], O_HI),
                  (dgb_ref[...], O_GB), (dcq_ref[...], O_CQ), (dckv_ref[...], O_CKV))
        dh = _mm(dkpe_ref[...], k_ref[...])
        for piece, off in pieces:
            dh = dh + _mm(piece, w_ref[off:off + piece.shape[1], :])
        dg_ref[...] += jnp.sum(dh * (xv * r), axis=0, keepdims=True)
        gx_ref[...] = do_ref[...] + _norm_rows_bwd(xv, r, g, dh)

    row = lambda w: pl.BlockSpec((TM, w), lambda i: (i, 0))
    full = lambda a: pl.BlockSpec(a.shape, lambda i: (0,) * a.ndim)
    sds = jax.ShapeDtypeStruct
    return pl.pallas_call(
        body, name="front_bwd", grid=(s // TM,),
        in_specs=[row(D), row(D), row(2048), row(512), pl.BlockSpec((3, TM, 512), lambda i: (0, i, 0)), row(512), row(Q_LORA),
                  row(KV_LORA), row(LANE), full(g_pre), full(w_in_t), full(w_kpe), pl.BlockSpec(memory_space=pl.ANY)],
        out_specs=[row(D), pl.BlockSpec((1, D), lambda i: (0, 0))],
        out_shape=[sds((s, D), F32), sds((1, D), F32)],
        compiler_params=_params(("arbitrary",)),
    )(x, dout, dmg, dga, dh3, dgb, dcq, dckv, dkpe, g_pre, w_in_t, w_kpe, token)


TK_GRAD = 1024


def _win_grad(h, pieces, name):
    s = h.shape[0]
    n = len(pieces)

    def body(h_ref, *refs):
        d_refs, o_refs, sums = refs[:n], refs[n:2 * n], refs[2 * n:]

        @pl.when(pl.program_id(0) == 0)
        def _():
            for s_ref in sums:
                s_ref[...] = jnp.zeros_like(s_ref)

        hv = h_ref[...]
        for d_ref, s_ref in zip(d_refs, sums):
            if len(d_ref.shape) == 3:
                for k in range(d_ref.shape[0]):
                    s_ref[k] += _mm_tn(d_ref[k], hv)
            else:
                s_ref[...] += _mm_tn(d_ref[...], hv)

        @pl.when(pl.program_id(0) == pl.num_programs(0) - 1)
        def _():
            for o_ref, s_ref in zip(o_refs, sums):
                o_ref[...] = s_ref[...].astype(BF16)

    def in_spec(p):
        if p.ndim == 3:
            return pl.BlockSpec((p.shape[0], TK_GRAD, p.shape[2]), lambda kk: (0, kk, 0))
        return pl.BlockSpec((TK_GRAD, p.shape[1]), lambda kk: (kk, 0))

    out_shapes = [(p.shape[0], p.shape[2], D) if p.ndim == 3 else (p.shape[1], D) for p in pieces]
    return pl.pallas_call(
        body, name=name, grid=(s // TK_GRAD,),
        in_specs=[pl.BlockSpec((TK_GRAD, D), lambda kk: (kk, 0))] + [in_spec(p) for p in pieces],
        out_specs=[pl.BlockSpec(sh, lambda kk, nd=len(sh): (0,) * nd) for sh in out_shapes],
        out_shape=[jax.ShapeDtypeStruct(sh, BF16) for sh in out_shapes],
        scratch_shapes=[pltpu.VMEM(sh, F32) for sh in out_shapes],
        compiler_params=_params(("arbitrary",)),
    )(h, *pieces)


QKV_ROWS = Q_LORA + KV_LORA + QK_ROPE


def _win_grad_qkv(h, dcq, dckv, dkpe, share):
    s = h.shape[0]
    half = QKV_ROWS // 2

    def body(h_ref, cq_ref, ckv_ref, kpe_ref, o_ref, *scratch):
        sums = scratch[0] if share else o_ref

        @pl.when(pl.program_id(0) == 0)
        def _():
            sums[...] = jnp.zeros_like(sums)

        hv = h_ref[...]
        g_cq = _mm_tn(cq_ref[...], hv)
        sums[0] += g_cq[:half]
        sums[1, 0:Q_LORA - half] += g_cq[half:]
        sums[1, Q_LORA - half:Q_LORA + KV_LORA - half] += _mm_tn(ckv_ref[...], hv)
        sums[1, Q_LORA + KV_LORA - half:] += _mm_tn(kpe_ref[...], hv)[QK_NOPE:QK_NOPE + QK_ROPE]

        if share:
            _, theirs, send_sem, recv_sem = scratch

            @pl.when(pl.program_id(0) == pl.num_programs(0) - 1)
            def _():
                c = lax.axis_index("c")
                copy = _remote(sums.at[1 - c], theirs, send_sem, recv_sem, 0, (lax.axis_index("x"), lax.axis_index("y"), 1 - c))
                copy.start()
                copy.wait()
                o_ref[...] = (sums[c] + theirs[...]).astype(BF16)

    rows = lambda a: pl.BlockSpec((TK_GRAD, a.shape[1]), lambda kk: (kk, 0))
    sem = pltpu.SemaphoreType.DMA((1,))
    shape, dtype = ((half, D), BF16) if share else ((2, half, D), F32)
    return pl.pallas_call(
        body, name="win_grad_qkv", grid=(s // TK_GRAD,), in_specs=[rows(h), rows(dcq), rows(dckv), rows(dkpe)],
        out_specs=pl.BlockSpec(shape, lambda kk: (0,) * len(shape)), out_shape=jax.ShapeDtypeStruct(shape, dtype),
        scratch_shapes=[pltpu.VMEM((2, half, D), F32), pltpu.VMEM((half, D), F32), sem, sem] if share else [],
        compiler_params=_params(("arbitrary",)),
    )(h, dcq, dckv, dkpe)


def _pad_wuq(w_uq):
    rows = w_uq.shape[0]
    w = w_uq.reshape(rows, NH, QK_NOPE + QK_ROPE)
    return jnp.pad(w, ((0, 0), (0, 0), (0, LANE - QK_NOPE - QK_ROPE))).reshape(rows, NH * LANE)


def _pad_wukv(w_ukv):
    heads = w_ukv.shape[1] // (QK_NOPE + V_DIM)
    w = w_ukv.reshape(KV_LORA, heads, QK_NOPE + V_DIM)
    w_k = jnp.pad(w[:, :, :QK_NOPE], ((0, 0), (0, 0), (0, LANE - QK_NOPE))).reshape(KV_LORA, heads * LANE)
    wv = w[:, :, QK_NOPE:].reshape(KV_LORA, heads // 2, 2, 1, V_DIM)
    eye = jnp.eye(2, dtype=w.dtype).reshape(1, 1, 2, 2, 1)
    return w_k, (wv * eye).reshape(KV_LORA, heads * LANE)


def _local_step(x, tgt, g_pre, w_in_t, b_gate, g_q, g_kv, lb_logits, g_hgrn, g_post, weights, exchange=None):
    s = x.shape[0]
    w_kpe = _kpe_block(w_in_t)
    rc, rs1, rs2 = _rope_tables(s)
    g_hg = jnp.tile(g_hgrn, (1, NH))

    proj, h = _front_fwd(x, g_pre, w_in_t, w_kpe, weights.tokens)
    w_uq_p, w_k_p, w_v_p = weights.qkv(h)
    q, k, vv = _qkv_fwd(proj, g_q, g_kv, w_uq_p, w_k_p, w_v_p, rc, rs1, rs2)
    attn, lse = _attn_fwd(q, k, vv)
    o_raw, states = _hgrn_fwd(proj, lb_logits)
    wa, wb, w_out = weights.mid(o_raw)
    (loss, dout, dattn, dga, dor, dgb, dmg, d_wout, d_wa, d_wb, d_gpost, d_bgate, d_ghg) = _mid(
        proj, attn, o_raw, x, tgt, g_hg, b_gate, g_post, wa, wb, w_out)
    w_mg, w_ga, w_gb = _win_grad(h, [dmg, dga, dgb], "win_grad_mid")
    dh3, d_lbl = _hgrn_bwd(proj, lb_logits, states, dor)
    (w_h3,) = _win_grad(h, [dh3], "win_grad_hgrn")
    d_win_rest = jnp.concatenate([w_ga, w_h3[0], w_h3[1], w_h3[2], w_gb, w_mg], axis=0)
    early = dict(w_in_rest=d_win_rest, w_branch_a=d_wa, w_branch_b=d_wb, w_out=d_wout)
    token = exchange.start_early(early) if exchange else jnp.zeros((8, LANE), F32)
    dq, dk, dvv = _attn_bwd(q, k, vv, attn, dattn, lse, token)
    dcq, dckv, dkpe, d_wuq, d_wukv, d_gq, d_gkv = _qkv_bwd(proj, dq, dk, dvv, g_q, g_kv, w_uq_p, w_k_p, w_v_p, rc, rs1, rs2)
    late = dict(w_in_qkv=_win_grad_qkv(h, dcq, dckv, dkpe, share=exchange is not None), w_uq=d_wuq, w_ukv=d_wukv)
    token = exchange.start_late(late) if exchange else jnp.zeros((8, LANE), F32)
    grad_x, d_gpre = _front_bwd(x, dout, dmg, dga, dh3, dgb, dcq, dckv, dkpe, g_pre, w_in_t, w_kpe, token)
    vec_grads = dict(g_pre=d_gpre, b_gate=d_bgate, g_q=d_gq, g_kv=d_gkv, lb_logits=d_lbl, g_hgrn=d_ghg, g_post=d_gpost)
    return loss, grad_x, dict(early, **late), vec_grads


SHARD_SHAPES = (("w_in", (1416, 1024)), ("w_uq", (192, 768)), ("w_ukv", (256, 256)), ("w_branch_a", (512, 256)),
                ("w_branch_b", (512, 256)), ("w_out", (256, 1024)))
BIG = tuple(n for n, _ in SHARD_SHAPES)
ROW_SHARDED = ("w_in", "w_uq", "w_out")
N_CHIPS = 4
W_IN_FORWARD_CUT = 704


def _to_block(name, a):
    return a[0].T if name == "w_in" else a[0]


def _from_block(name, a):
    return a.T[None] if name == "w_in" else a[None]
VEC_ROWS = (("g_pre", 0, 1024), ("b_gate", 1, 2048), ("g_q", 2, 768), ("g_kv", 3, 256), ("g_hgrn", 6, 64), ("g_post", 7, 1024))
VEC_LB_ROW = 4
VEC_SHAPE = (8, 2048)


def _split_by_chip(name, g):
    a, b = dict(SHARD_SHAPES)[name]
    return g.reshape(N_CHIPS, a, b) if name in ROW_SHARDED else g.reshape(a, N_CHIPS, b).transpose(1, 0, 2)


def _join_chips(name, w):
    a, b = dict(SHARD_SHAPES)[name]
    return w.reshape(N_CHIPS * a, b) if name in ROW_SHARDED else w.transpose(1, 0, 2).reshape(a, N_CHIPS * b)


MESH = pl.DeviceIdType.MESH
HBM = pl.BlockSpec(memory_space=pltpu.HBM)


def _mesh_place():
    x, y, c = lax.axis_index("x"), lax.axis_index("y"), lax.axis_index("c")
    return x, y, c, 2 * x + y, [(1 - x, y), (x, 1 - y), (1 - x, 1 - y)]


def _remote(src, dst, send_sems, recv_sems, k, to):
    return pltpu.make_async_remote_copy(src_ref=src, dst_ref=dst, send_sem=send_sems.at[k], recv_sem=recv_sems.at[k],
                                        device_id=to, device_id_type=MESH)


def _gather_w_in(shard):
    a, b = shard.shape
    cut = W_IN_FORWARD_CUT

    def body(src, out, ici_send, ici_recv, d2d_send, d2d_recv, local_sem):
        x, y, c = lax.axis_index("x"), lax.axis_index("y"), lax.axis_index("c")
        me, xn, yn, dg = 2 * x + y, 2 * (1 - x) + y, 2 * x + (1 - y), 2 * (1 - x) + (1 - y)
        to_x, to_y, sibling = (1 - x, y, c), (x, 1 - y, c), (x, y, 1 - c)
        first, rest = pl.ds(0, cut), pl.ds(cut, a - cut)

        whole = lambda ref, which: ref.at[:, pl.ds(pl.multiple_of(which * (b // 2), b // 2), b // 2)]
        own = pltpu.make_async_copy(src, out.at[me], local_sem)
        own.start()
        sends = [_remote(whole(src, c), whole(out.at[me], c), ici_send, ici_recv, 0, to_x),
                 _remote(whole(src, c), whole(out.at[me], c), ici_send, ici_recv, 1, to_y)]
        for cp in sends:
            cp.start()

        def landed(slot, rows, k, d2d_k, src_dev):
            piece = whole(out.at[slot], c) if rows is None else out.at[slot].at[rows, pl.ds(pl.multiple_of(c * (b // 2), b // 2), b // 2)]
            _remote(piece, piece, ici_send, ici_recv, k, src_dev).wait_recv()
            cp = _remote(piece, piece, d2d_send, d2d_recv, d2d_k, sibling)
            cp.start()
            sends.append(cp)
            return piece

        def pass_on(slot, rows, k, to):
            piece = out.at[slot].at[rows, pl.ds(pl.multiple_of(c * (b // 2), b // 2), b // 2)]
            cp = _remote(piece, piece, ici_send, ici_recv, k, to)
            cp.start()
            sends.append(cp)

        landed(xn, None, 0, 0, to_x)
        pass_on(xn, first, 2, to_y)
        landed(yn, None, 1, 1, to_y)
        pass_on(yn, rest, 3, to_x)
        landed(dg, first, 2, 2, to_y)
        landed(dg, rest, 3, 3, to_x)
        other = pl.ds(pl.multiple_of((1 - c) * (b // 2), b // 2), b // 2)
        for d2d_k, (slot, rows) in enumerate(((xn, None), (yn, None), (dg, first), (dg, rest))):
            piece = out.at[slot].at[:, other] if rows is None else out.at[slot].at[rows, other]
            _remote(piece, piece, d2d_send, d2d_recv, d2d_k, sibling).wait_recv()
        for cp in sends:
            cp.wait_send()
        own.wait()

    sems = pltpu.SemaphoreType.DMA((4,))
    return pl.pallas_call(
        body, name="gather_w_in", in_specs=[HBM], out_specs=HBM,
        out_shape=jax.ShapeDtypeStruct((N_CHIPS, a, b), shard.dtype),
        scratch_shapes=[sems, sems, sems, sems, pltpu.SemaphoreType.DMA],
        compiler_params=pltpu.CompilerParams(has_side_effects=True),
    )(shard)


SEM =pl.BlockSpec(memory_space=pltpu.SEMAPHORE)
DATAFLOW = pltpu.SideEffectType.DATAFLOW_SIDE_EFFECTING


def _exchange_copies(srcs, to_first, src_refs, land_refs, send_sems, recv_sems):
    x, y, c, me, chips = _mesh_place()
    n = len(srcs)
    sends, recvs = [], []
    for k in range(n):
        if k in to_first:
            base = 3 * n + 4 * to_first.index(k)
            sends.append((me != 0, pltpu.make_async_remote_copy(
                src_ref=src_refs[k], dst_ref=land_refs[k].at[me], send_sem=send_sems.at[base], recv_sem=recv_sems.at[base + me],
                device_id=(0, 0, c), device_id_type=MESH)))
            for s in range(1, N_CHIPS):
                recvs.append((me == 0, pltpu.make_async_remote_copy(
                    src_ref=src_refs[k], dst_ref=land_refs[k].at[s], send_sem=send_sems.at[base], recv_sem=recv_sems.at[base + s],
                    device_id=(s // 2, s % 2, c), device_id_type=MESH)))
        else:
            slab = (lambda t, k=k: src_refs[k]) if srcs[k].ndim == 2 else (lambda t, k=k: src_refs[k].at[t])
            for j, (px, py) in enumerate(chips):
                sends.append((None, _remote(slab(2 * px + py), land_refs[k].at[me], send_sems, recv_sems, 3 * k + j, (px, py, c))))
                recvs.append((None, _remote(slab(me), land_refs[k].at[2 * px + py], send_sems, recv_sems, 3 * k + j, (px, py, c))))
    return sends, recvs


def _when(pred, fn):
    if pred is None:
        fn()
    else:
        pl.when(pred)(fn)


def _exchange_start(srcs, to_first, name, after=None):
    n = len(srcs)
    n_sems = 3 * n + 4 * len(to_first)
    lands = [lax.empty((N_CHIPS,) + s.shape[-2:], s.dtype) for s in srcs]
    extra = [] if after is None else [after]

    def body(*refs):
        src_refs, land_refs = refs[:n], refs[n:2 * n]
        send_sems, recv_sems, token = refs[2 * n + len(extra)], refs[2 * n + len(extra) + 1], refs[-1]
        sends, _ = _exchange_copies(srcs, to_first, src_refs, land_refs, send_sems, recv_sems)
        for pred, cp in sends:
            _when(pred, cp.start)
        token[...] = jnp.zeros_like(token)

    hbm = lambda a: pltpu.HBM(a.shape, a.dtype)
    res = pl.pallas_call(
        body, name=name,
        out_shape=[pltpu.SemaphoreType.DMA((n_sems,)), pltpu.SemaphoreType.DMA((n_sems,))] + [hbm(a) for a in srcs + lands]
        + [jax.ShapeDtypeStruct((8, LANE), F32)],
        in_specs=[HBM] * (2 * n) + [pl.BlockSpec(memory_space=pl.ANY)] * len(extra),
        out_specs=[SEM, SEM] + [HBM] * (2 * n) + [pl.BlockSpec(memory_space=pltpu.VMEM)],
        input_output_aliases={i: 2 + i for i in range(2 * n)},
        compiler_params=pltpu.CompilerParams(has_side_effects=DATAFLOW),
    )(*[pltpu.with_memory_space_constraint(a, pltpu.HBM) for a in srcs + lands], *extra)
    return res[:-1], res[-1]


def _exchange_wait(srcs, to_first, started, after, name):
    n = len(srcs)
    send_sems, recv_sems, thru = started[0], started[1], started[2:]

    def body(*refs):
        src_refs, land_refs, send_ref, recv_ref = refs[:n], refs[n:2 * n], refs[2 * n], refs[2 * n + 1]
        sends, recvs = _exchange_copies(srcs, to_first, src_refs, land_refs, send_ref, recv_ref)
        for pred, cp in sends:
            _when(pred, cp.wait_send)
        for pred, cp in recvs:
            _when(pred, cp.wait_recv)

    res = pl.pallas_call(
        body, name=name, out_shape=[pltpu.HBM(a.shape, a.dtype) for a in thru],
        in_specs=[HBM] * (2 * n) + [SEM, SEM, pl.BlockSpec(memory_space=pl.ANY)], out_specs=[HBM] * (2 * n),
        input_output_aliases={i: i for i in range(2 * n)},
        compiler_params=pltpu.CompilerParams(has_side_effects=DATAFLOW),
    )(*thru, send_sems, recv_sems, after)
    return res[:n], res[n:]


ROW_TILE = 256
COL_TILE = 256


def _block_tiling(a, b):
    if a <= ROW_TILE or a % ROW_TILE == 0:
        ta = min(a, ROW_TILE)
        return a // ta, (ta, b), lambda i: (i, 0)
    return b // COL_TILE, (a, COL_TILE), lambda i: (0, i)


def _sum_share_small(lands, owns, name, after=None):
    n = len(lands)
    extra = [] if after is None else [after]

    def body(*refs):
        p_refs, own_refs = refs[:n], refs[n:2 * n]
        o_refs, mine, theirs = (refs[j * n + len(extra):(j + 1) * n + len(extra)] for j in (2, 3, 4))
        send_sems, recv_sems = refs[5 * n + len(extra):]
        me = 2 * lax.axis_index("x") + lax.axis_index("y")
        sibling = (lax.axis_index("x"), lax.axis_index("y"), 1 - lax.axis_index("c"))
        copies = [_remote(mine[k], theirs[k], send_sems, recv_sems, k, sibling) for k in range(n)]
        for p_ref, own_ref, mine_ref, copy in zip(p_refs, own_refs, mine, copies):
            own = (own_ref[me] if len(own_ref.shape) == 3 else own_ref[...]).astype(F32)
            slot = lambda t: jnp.where(me == t, own, p_ref[t].astype(F32))
            mine_ref[...] = ((slot(0) + slot(1)) + slot(2)) + slot(3)
            copy.start()
        for o_ref, mine_ref, theirs_ref, copy in zip(o_refs, mine, theirs, copies):
            copy.wait()
            o_ref[...] = mine_ref[...] + theirs_ref[...]

    vmem = pl.BlockSpec(memory_space=pltpu.VMEM)
    kept = [pltpu.VMEM(land.shape[1:], F32) for land in lands]
    sems = pltpu.SemaphoreType.DMA((n,))
    return pl.pallas_call(
        body, name=name, in_specs=[vmem] * (2 * n) + [pl.BlockSpec(memory_space=pl.ANY)] * len(extra), out_specs=[vmem] * n,
        out_shape=[jax.ShapeDtypeStruct(land.shape[1:], F32) for land in lands], scratch_shapes=kept + kept + [sems, sems],
        compiler_params=_params(()),
    )(*lands, *owns, *extra)


def _adamw_small(grads, states, name):
    n = len(grads)

    def body(*refs):
        ins, outs = refs[:4 * n], refs[4 * n:]
        for k in range(n):
            g_ref, w_ref, m_ref, v_ref = ins[4 * k:4 * k + 4]
            g = g_ref[...]
            outs[4 * k][...] = g
            outs[4 * k + 1][...], outs[4 * k + 2][...], outs[4 * k + 3][...] = _adamw_math(g, w_ref[...], m_ref[...], v_ref[...])

    args = [t for k in range(n) for t in (grads[k], *states[k])]
    res = pl.pallas_call(body, name=name, out_shape=[jax.ShapeDtypeStruct(grads[k].shape, F32) for k in range(n) for _ in range(4)],
                         compiler_params=_params(()))(*args)
    return [tuple(res[4 * k:4 * k + 4]) for k in range(n)]


class _LaterWeights:
    MID = ("w_branch_a", "w_branch_b", "w_out")

    def __init__(self, blocks, after):
        self.qkv_blocks = [_pad_wuq(blocks["w_uq"]), *_pad_wukv(blocks["w_ukv"])]
        self.mid_blocks = [blocks[n] for n in self.MID]
        self.qkv_started, t1 = _exchange_start(self.qkv_blocks, (), "weights_qkv_start", after)
        self.mid_started, t2 = _exchange_start(self.mid_blocks, (), "weights_mid_start", after)
        self.tokens = [t1, t2]

    @staticmethod
    def _whole(blocks, joined, started, after, name):
        _, landed = _exchange_wait(blocks, (), started, after, name)
        me = 2 * lax.axis_index("x") + lax.axis_index("y")
        out = []
        for block, land, join in zip(blocks, landed, joined):
            w = lax.dynamic_update_index_in_dim(land, block, me, 0)
            out.append(w.reshape(N_CHIPS * block.shape[0], block.shape[1]) if join else w)
        return out

    def qkv(self, after):
        return self._whole(self.qkv_blocks, (True, False, False), self.qkv_started, after, "weights_qkv_wait")

    def mid(self, after):
        return self._whole(self.mid_blocks, (False, False, True), self.mid_started, after, "weights_mid_wait")


class _GradExchange:
    EARLY = ("w_in", "w_branch_a", "w_branch_b", "w_out")
    LATE = ("w_uq", "w_ukv")

    def __init__(self, state):
        self.state = state
        self.outs = {}

    def start_early(self, g):
        full = jnp.concatenate([jnp.zeros((QKV_ROWS, D), g["w_in_rest"].dtype), g["w_in_rest"]], axis=0)
        g = dict(g, w_in=full)
        self.early = [(g[n] if g[n].ndim == 3 else _split_by_chip(n, g[n])).astype(BF16) for n in self.EARLY]
        self.early_started, token = _exchange_start(self.early, (), "grads_early_start")
        return token

    def start_late(self, g):
        self.early, self.early_landed = _exchange_wait(self.early, (), self.early_started, g["w_uq"], "grads_early_wait")
        self.late = [_split_by_chip("w_uq", g["w_uq"]), g["w_ukv"], g["w_in_qkv"]]
        self.late_started, token = _exchange_start(self.late, (2,), "grads_late_start")
        names = self.EARLY[1:]
        grads = _sum_share_small(self.early_landed[1:], self.early[1:], "sum_early", after=token)
        for n, out in zip(names, _adamw_small(grads, [self.state[n] for n in names], "adamw_early")):
            self.outs[n] = out
        return self.outs[names[-1]][0]

    def finish(self, after):
        late, late_landed = _exchange_wait(self.late, (2,), self.late_started, after, "grads_late_wait")
        grad = _sum_exchange(self.early_landed[0], self.early[0], late_landed[2], late[2], "sum_w_in")
        self.outs["w_in"] = _adamw(grad, *self.state["w_in"], "adamw_w_in")
        return list(late[:2]), list(late_landed[:2])


def _adamw_math(g, w, m, v):
    nm = ADAM_B1 * m + (1.0 - ADAM_B1) * g
    nv = ADAM_B2 * v + (1.0 - ADAM_B2) * (g * g)
    m_hat = nm / (1.0 - ADAM_B1 ** ADAM_STEP)
    v_hat = nv / (1.0 - ADAM_B2 ** ADAM_STEP)
    return -ADAM_LR * (m_hat / (jnp.sqrt(v_hat) + ADAM_EPS) + ADAM_WD * w), nm, nv


def _sum_exchange(land, own, first_land, first_own, name):
    _, a, b = land.shape
    steps, tile, at = _block_tiling(a, b)
    assert tile[0] == a, "the extra rows need whole columns in a step"
    r = first_own.shape[0]

    def body(me_ref, p_ref, own_ref, fp_ref, fo_ref, g_ref, mine_s, theirs_s, send_sems, recv_sems):
        pass_, i = pl.program_id(0), pl.program_id(1)
        me, c = me_ref[0], lax.axis_index("c")
        sibling = (lax.axis_index("x"), lax.axis_index("y"), 1 - c)
        copy = _remote(mine_s.at[i], theirs_s.at[i], send_sems, recv_sems, i, sibling)

        @pl.when(pass_ == 0)
        def _():
            own = own_ref[...].astype(F32)
            slot = lambda t: jnp.where(me == t, own, p_ref[t].astype(F32))
            mine_s[i] = ((slot(0) + slot(1)) + slot(2)) + slot(3)

            @pl.when(me == 0)
            def _():
                f = lambda t: fp_ref[t].astype(F32)
                rows = pl.ds(pl.multiple_of(c * r, 8), r)
                mine_s[i, rows, :] += ((fo_ref[...].astype(F32) + f(1)) + f(2)) + f(3)

            copy.start()

        @pl.when(pass_ == 1)
        def _():
            copy.wait()
            g_ref[...] = mine_s[i] + theirs_s[i]

    first = lambda p, i: i * (1 - p) + (steps - 1) * p
    in_specs = [pl.BlockSpec((N_CHIPS,) + tile, lambda p, i, me: (0,) + at(first(p, i))),
                pl.BlockSpec((None,) + tile, lambda p, i, me: (me[0],) + at(first(p, i))),
                pl.BlockSpec((N_CHIPS, r, tile[1]), lambda p, i, me: (0,) + at(first(p, i))),
                pl.BlockSpec((r, tile[1]), lambda p, i, me: at(first(p, i)))]
    me = jnp.reshape(2 * lax.axis_index("x") + lax.axis_index("y"), (1,)).astype(jnp.int32)
    kept = pltpu.VMEM((steps,) + tile, F32)
    sems = pltpu.SemaphoreType.DMA((steps,))
    return pl.pallas_call(
        body, name=name,
        grid_spec=pltpu.PrefetchScalarGridSpec(num_scalar_prefetch=1, grid=(2, steps), in_specs=in_specs,
                                               out_specs=pl.BlockSpec(tile, lambda p, i, me: at(i * p)),
                                               scratch_shapes=[kept, kept, sems, sems]),
        out_shape=jax.ShapeDtypeStruct((a, b), F32),
        compiler_params=_params(("arbitrary", "arbitrary")),
    )(me, land, own, first_land, first_own)


def _adamw(g, w, m, v, name):
    a, b = g.shape
    steps, tile, at = _block_tiling(a, b)

    def body(g_ref, w_ref, m_ref, v_ref, go_ref, d_ref, nm_ref, nv_ref):
        g = g_ref[...]
        go_ref[...] = g
        d_ref[...], nm_ref[...], nv_ref[...] = _adamw_math(g, w_ref[...], m_ref[...], v_ref[...])

    spec = pl.BlockSpec(tile, at)
    sds = jax.ShapeDtypeStruct((a, b), F32)
    return pl.pallas_call(
        body, name=name, grid=(steps,), in_specs=[spec] * 4, out_specs=[spec] * 4, out_shape=[sds] * 4,
        compiler_params=_params(("parallel",)),
    )(g, w, m, v)


LOSS_AT = (2, 1024)


def _vec_pack(vg, loss):
    names = [n for n, _, _ in VEC_ROWS]

    def body(*refs):
        o_ref = refs[-1]
        lb_ref, loss_ref = refs[len(names)], refs[len(names) + 1]
        o_ref[...] = jnp.zeros_like(o_ref)
        o_ref[LOSS_AT[0]:LOSS_AT[0] + 1, LOSS_AT[1]:LOSS_AT[1] + LANE] = jnp.broadcast_to(loss_ref[...], (1, LANE))
        for (name, row, size), ref in zip(VEC_ROWS, refs):
            if name == "g_hgrn":
                r = lax.broadcasted_iota(jnp.int32, (NH * V_DIM, LANE), 0)
                c = lax.broadcasted_iota(jnp.int32, (NH * V_DIM, LANE), 1)
                fold = ((r % V_DIM) == c).astype(F32)
                o_ref[row:row + 1, 0:LANE] = jnp.dot(ref[...], fold, precision=HIGHEST, preferred_element_type=F32)
            else:
                o_ref[row:row + 1, 0:size] = ref[...]
        o_ref[VEC_LB_ROW:VEC_LB_ROW + 2, 0:512] = lb_ref[...]

    return pl.pallas_call(body, name="vec_pack", out_shape=jax.ShapeDtypeStruct(VEC_SHAPE, F32))(
        *[vg[n] for n in names], vg["lb_logits"], loss)


def _adamw_vec(block, w, m, v):
    names = [n for n, _, _ in VEC_ROWS] + ["lb_logits"]
    k = len(names)

    def body(g_ref, *refs):
        ins, outs = refs[:3 * k], refs[3 * k:]
        outs[-1][...] = g_ref[LOSS_AT[0]:LOSS_AT[0] + 1, LOSS_AT[1]:LOSS_AT[1] + LANE]
        for i, name in enumerate(names):
            if name == "lb_logits":
                rows, cols = slice(VEC_LB_ROW, VEC_LB_ROW + 2), slice(0, 512)
            else:
                _, row, size = VEC_ROWS[i]
                rows, cols = slice(row, row + 1), slice(0, size)
            g = g_ref[rows, cols]
            d, nm, nv = _adamw_math(g, ins[i][...], ins[k + i][...], ins[2 * k + i][...])
            for o_ref, val in zip(outs[4 * i:4 * i + 4], (g, d, nm, nv)):
                o_ref[...] = val

    shapes = [jax.ShapeDtypeStruct(w[n].shape, F32) for n in names for _ in range(4)] + [jax.ShapeDtypeStruct((1, LANE), F32)]
    res = pl.pallas_call(body, name="adamw_vec", out_shape=shapes)(
        block, *[w[n] for n in names], *[m[n] for n in names], *[v[n] for n in names])
    return [{n: res[4 * i + j] for i, n in enumerate(names)} for j in range(4)], res[-1]


WEIGHTS = ("g_pre", "w_in", "b_gate", "g_q", "w_uq", "g_kv", "w_ukv", "lb_logits", "g_hgrn", "w_branch_a", "w_branch_b", "w_out", "g_post")


def kernel(x, g_pre, w_in, b_gate, g_q, w_uq, g_kv, w_ukv, lb_logits, g_hgrn, w_branch_a, w_branch_b, w_out, g_post, loss_target, m_g_pre, m_w_in, m_b_gate, m_g_q, m_w_uq, m_g_kv, m_w_ukv, m_lb_logits, m_g_hgrn, m_w_branch_a, m_w_branch_b, m_w_out, m_g_post, v_g_pre, v_w_in, v_b_gate, v_g_q, v_w_uq, v_g_kv, v_w_ukv, v_lb_logits, v_g_hgrn, v_w_branch_a, v_w_branch_b, v_w_out, v_g_post):
    w = dict(g_pre=g_pre, w_in=w_in, b_gate=b_gate, g_q=g_q, w_uq=w_uq, g_kv=g_kv, w_ukv=w_ukv, lb_logits=lb_logits, g_hgrn=g_hgrn,
             w_branch_a=w_branch_a, w_branch_b=w_branch_b, w_out=w_out, g_post=g_post)
    m = dict(g_pre=m_g_pre, w_in=m_w_in, b_gate=m_b_gate, g_q=m_g_q, w_uq=m_w_uq, g_kv=m_g_kv, w_ukv=m_w_ukv, lb_logits=m_lb_logits,
             g_hgrn=m_g_hgrn, w_branch_a=m_w_branch_a, w_branch_b=m_w_branch_b, w_out=m_w_out, g_post=m_g_post)
    v = dict(g_pre=v_g_pre, w_in=v_w_in, b_gate=v_b_gate, g_q=v_g_q, w_uq=v_w_uq, g_kv=v_g_kv, w_ukv=v_w_ukv, lb_logits=v_lb_logits,
             g_hgrn=v_g_hgrn, w_branch_a=v_w_branch_a, w_branch_b=v_w_branch_b, w_out=v_w_out, g_post=v_g_post)
    blocks = {n: _to_block(n, w[n]).astype(BF16) for n in BIG}
    w_in_all = _gather_w_in(blocks["w_in"])
    weights = _LaterWeights(blocks, w_in_all)
    state = {n: [_to_block(n, t[n]) for t in (w, m, v)] for n in BIG}
    exchange = _GradExchange(state)
    loss, grad_x, _, vec_grads = _local_step(
        x[0], loss_target[0], g_pre, _join_chips("w_in", w_in_all), b_gate, g_q, g_kv, lb_logits, g_hgrn, g_post, weights, exchange)
    vec = _vec_pack(vec_grads, loss)
    vec_started, token = _exchange_start([vec], (), "vec_start")
    late_sent, late_landed = exchange.finish(token)
    (vec,), (vec_landed,) = _exchange_wait([vec], (), vec_started, exchange.outs["w_in"][0], "vec_wait")
    grads = _sum_share_small(late_landed + [vec_landed], late_sent + [vec], "sum_late")
    done = dict(exchange.outs)
    done.update(zip(exchange.LATE, _adamw_small(grads[:-1], [state[n] for n in exchange.LATE], "adamw_late")))
    outs = [{}, {}, {}, {}]
    for n in BIG:
        for o, val in zip(outs, done[n]):
            o[n] = _from_block(n, val)
    vec_outs, total = _adamw_vec(grads[-1], w, m, v)
    for o, vals in zip(outs, vec_outs):
        o.update(vals)
    return (total[0, 0], grad_x[None], *[o[n] for o in outs for n in WEIGHTS])
```

```python
import math

import numpy as np
import jax
import jax.numpy as jnp
from jax import lax
from jax.experimental import pallas as pl
from jax.experimental.pallas import tpu as pltpu

F32 = jnp.float32
BF16 = jnp.bfloat16
HIGHEST = lax.Precision.HIGHEST

D = 1024
NH = 8
QK_NOPE, QK_ROPE, V_DIM = 64, 32, 64
Q_LORA, KV_LORA = 768, 256
CHUNK = 64
HG_BLOCK = 32
EPS = 1e-6
LANE = 128
P_MERGE, P_GA, P_HQ, P_HF, P_HI, P_GB, P_CQ, P_CKV, P_KPE = 0, 2048, 2560, 3072, 3584, 4096, 4608, 5376, 5632
D_P = 5760
O_CQ, O_CKV, O_KPE, O_GA, O_HQ, O_HF, O_HI, O_GB, O_MERGE = 0, 768, 1024, 1056, 1568, 2080, 2592, 3104, 3616

TM = 512
TM_MID = 256
TQ = 1024
ONES_LANE = (LANE - 1, 0)
TH = 256
HG_PAIRS = 4
VMEM_LIMIT = 56 * 1024 * 1024

ADAM_LR, ADAM_B1, ADAM_B2, ADAM_EPS, ADAM_WD, ADAM_STEP = 0.001, 0.9, 0.999, 1e-08, 0.01, 10

NT_DIMS = (((1,), (1,)), ((), ()))
TN_DIMS = (((0,), (0,)), ((), ()))


def _params(sem):
    return pltpu.CompilerParams(dimension_semantics=sem, vmem_limit_bytes=VMEM_LIMIT)


def _mm(a, b):
    return jnp.dot(a, b, preferred_element_type=F32)


def _mm_nt(a, b):
    return lax.dot_general(a, b, NT_DIMS, preferred_element_type=F32)


def _mm_tn(a, b):
    return lax.dot_general(a, b, TN_DIMS, preferred_element_type=F32)


def _sigmoid(z):
    return jax.nn.sigmoid(z)


def _rope(v, c, s1, s2):
    return v * c + pltpu.roll(v, 112, 1) * s1 + pltpu.roll(v, 16, 1) * s2


def _rope_t(dy, c, s1, s2):
    return dy * c + pltpu.roll(dy * s1, 16, 1) + pltpu.roll(dy * s2, 112, 1)


def _rope_tables(s):
    f32 = np.float32
    inv = f32(10000.0) ** (-np.arange(0, QK_ROPE, 2, dtype=f32) / f32(QK_ROPE))
    ang = np.arange(s, dtype=f32)[:, None] * inv[None, :]
    cos, sin = np.cos(ang).astype(f32), np.sin(ang).astype(f32)
    z64, z32, o64, o32 = np.zeros((s, 64), f32), np.zeros((s, 32), f32), np.ones((s, 64), f32), np.ones((s, 32), f32)
    z16 = np.zeros((s, 16), f32)
    c = np.concatenate([o64, cos, cos, o32], axis=1)
    s1 = np.concatenate([z64, -sin, z16, z32], axis=1)
    s2 = np.concatenate([z64, z16, sin, z32], axis=1)
    return jnp.asarray(c), jnp.asarray(s1), jnp.asarray(s2)


W_IN_RUNS = ((O_MERGE, 2048, P_MERGE), (O_GA, O_MERGE - O_GA, P_GA), (O_CQ, O_KPE - O_CQ, P_CQ))


def _kpe_block(w_in_t):
    z = lambda n: jnp.zeros((n, w_in_t.shape[1]), w_in_t.dtype)
    return jnp.concatenate([z(64), w_in_t[O_KPE:O_KPE + QK_ROPE], z(32)], axis=0)


def _front_fwd(x, g_pre, w_in_t, w_kpe, tokens=()):
    s = x.shape[0]
    tokens = list(tokens)

    def body(x_ref, g_ref, w_ref, k_ref, *refs):
        o_ref, h_ref = refs[len(tokens):]
        xv = x_ref[...]
        r = lax.rsqrt(jnp.mean(xv * xv, axis=-1, keepdims=True) + EPS)
        h = ((xv * r) * g_ref[...]).astype(BF16)
        h_ref[...] = h
        for row, rows, col in W_IN_RUNS:
            o_ref[:, col:col + rows] = _mm_nt(h, w_ref[row:row + rows, :])
        o_ref[:, P_KPE:P_KPE + LANE] = _mm_nt(h, k_ref[...])

    full = lambda a: pl.BlockSpec(a.shape, lambda i: (0,) * a.ndim)
    return pl.pallas_call(
        body, name="front_fwd", grid=(s // TM,),
        in_specs=[pl.BlockSpec((TM, D), lambda i: (i, 0)), pl.BlockSpec((1, D), lambda i: (0, 0)), full(w_in_t), full(w_kpe)]
        + [pl.BlockSpec((8, LANE), lambda i: (0, 0))] * len(tokens),
        out_specs=[pl.BlockSpec((TM, D_P), lambda i: (i, 0)), pl.BlockSpec((TM, D), lambda i: (i, 0))],
        out_shape=[jax.ShapeDtypeStruct((s, D_P), F32), jax.ShapeDtypeStruct((s, D), BF16)],
        compiler_params=_params(("parallel",)),
    )(x, g_pre, w_in_t, w_kpe, *tokens)


def _norm_rows(v, g):
    r = lax.rsqrt(jnp.mean(v * v, axis=-1, keepdims=True) + EPS)
    return (v * r) * g, r


def _qkv_fwd(proj, g_q, g_kv, w_uq_p, w_k_p, w_v_p, rc, rs1, rs2):
    s = proj.shape[0]

    def body(cq_ref, ckv_ref, kpe_ref, gq_ref, gkv_ref, wq_ref, wk_ref, wv_ref, c_ref, s1_ref, s2_ref, q_ref, k_ref, v_ref):
        c, s1, s2 = c_ref[...], s1_ref[...], s2_ref[...]
        cqn, _ = _norm_rows(cq_ref[...], gq_ref[...])
        ckvn, _ = _norm_rows(ckv_ref[...], gkv_ref[...])
        ckvn = ckvn.astype(BF16)
        qf = _mm(cqn.astype(BF16), wq_ref[...])
        kf = jnp.concatenate([_mm(ckvn, wk_ref[t]) for t in range(N_CHIPS)], axis=1)
        vf = jnp.concatenate([_mm(ckvn, wv_ref[t]) for t in range(N_CHIPS)], axis=1)
        kpe = _rope(kpe_ref[...], c, s1, s2)
        lane = lax.broadcasted_iota(jnp.int32, (TM, LANE), 1)
        for h in range(NH):
            blk = slice(h * LANE, (h + 1) * LANE)
            q_ref[h] = _rope(qf[:, blk], c, s1, s2).astype(BF16)
            k_ref[h] = (kf[:, blk] + kpe).astype(BF16)
            v_ref[h] = jnp.where(lane == ONES_LANE[h % 2], 1.0, vf[:, blk]).astype(BF16)

    row = lambda w, j: pl.BlockSpec((TM, w), lambda i: (i, j))
    full = lambda a: pl.BlockSpec(a.shape, lambda i: (0,) * a.ndim)
    hs = jax.ShapeDtypeStruct((NH, s, LANE), BF16)
    return pl.pallas_call(
        body, name="qkv_fwd", grid=(s // TM,),
        in_specs=[row(Q_LORA, P_CQ // Q_LORA), row(KV_LORA, P_CKV // KV_LORA), row(LANE, P_KPE // LANE),
                  full(g_q), full(g_kv), full(w_uq_p), full(w_k_p), full(w_v_p), row(LANE, 0), row(LANE, 0), row(LANE, 0)],
        out_specs=[pl.BlockSpec((NH, TM, LANE), lambda i: (0, i, 0))] * 3,
        out_shape=[hs, hs, hs],
        compiler_params=_params(("parallel",)),
    )(proj, proj, proj, g_q, g_kv, w_uq_p, w_k_p, w_v_p, rc, rs1, rs2)


LOG2E = 1.4426950408889634
QK_SCALE2 = LOG2E / math.sqrt(QK_NOPE + QK_ROPE)


HQ = TQ // 2


def _diag_visible(n):
    row = lax.broadcasted_iota(jnp.int32, (n, n), 0)
    col = lax.broadcasted_iota(jnp.int32, (n, n), 1)
    return (col // CHUNK) <= (row // CHUNK)


def _attn_fwd(q, k, vv):
    s = q.shape[1]

    def body(q_ref, k_ref, v_ref, o_ref, lse_ref):
        i = pl.program_id(1)
        qs = (q_ref[0], q_ref[1])

        def tiles(t, carry, diag):
            rows = pl.ds(pl.multiple_of(t * TQ, TQ), TQ)
            sc = [_mm_nt(qs[hh], k_ref[hh, rows, :]) for hh in range(2)]
            if diag:
                sc = [jnp.where(_diag_visible(TQ), s_, -jnp.inf) for s_ in sc]
            m_new = [jnp.maximum(carry[hh][0], jnp.max(sc[hh], axis=-1, keepdims=True)) for hh in range(2)]
            alpha = [jnp.exp2((carry[hh][0] - m_new[hh]) * QK_SCALE2) for hh in range(2)]
            p = [jnp.exp2((sc[hh] - m_new[hh]) * QK_SCALE2).astype(BF16) for hh in range(2)]
            acc = [alpha[hh] * carry[hh][1] + _mm(p[hh], v_ref[hh, rows, :]) for hh in range(2)]
            return (m_new[0], acc[0]), (m_new[1], acc[1])

        init = (jnp.full((TQ, 1), -jnp.inf, F32), jnp.zeros((TQ, LANE), F32))
        carry = lax.fori_loop(0, i, lambda t, c: tiles(t, c, False), (init, init))
        carry = tiles(i, carry, True)
        lane = lax.broadcasted_iota(jnp.int32, (TQ, LANE), 1)
        out = jnp.zeros((TQ, LANE), F32)
        for hh in range(2):
            m, acc = carry[hh]
            l = jnp.sum(jnp.where(lane == ONES_LANE[hh], acc, 0.0), axis=-1, keepdims=True)
            out = out + jnp.where((lane < V_DIM) == (hh == 0), acc, 0.0) / l
            lse_ref[hh] = jnp.broadcast_to(m * QK_SCALE2 + jnp.log(l) * LOG2E, (TQ, LANE))
        o_ref[...] = out

    return pl.pallas_call(
        body, name="attn_fwd", grid=(NH // 2, s // TQ),
        in_specs=[pl.BlockSpec((2, TQ, LANE), lambda p, i: (p, i, 0)), pl.BlockSpec((2, s, LANE), lambda p, i: (p, 0, 0)),
                  pl.BlockSpec((2, s, LANE), lambda p, i: (p, 0, 0))],
        out_specs=[pl.BlockSpec((TQ, LANE), lambda p, i: (i, p)), pl.BlockSpec((2, TQ, LANE), lambda p, i: (p, i, 0))],
        out_shape=[jax.ShapeDtypeStruct((s, NH * V_DIM), F32), jax.ShapeDtypeStruct((NH, s, LANE), F32)],
        compiler_params=_params(("parallel", "parallel")),
    )(q, k, vv)


def _lower_bound(lbl):
    a0, a1 = lbl[0:1, :], lbl[1:2, :]
    mx = jnp.maximum(a0, a1)
    e0, e1 = jnp.exp(a0 - mx), jnp.exp(a1 - mx)
    return e0 / (e0 + e1)


def _chunk_cumsum(v, reverse=False):
    pos = lax.broadcasted_iota(jnp.int32, v.shape, 0) % HG_BLOCK
    s = 1
    while s < HG_BLOCK:
        if reverse:
            v = v + jnp.where(pos < HG_BLOCK - s, pltpu.roll(v, TH - s, 0), 0.0)
        else:
            v = v + jnp.where(pos >= s, pltpu.roll(v, s, 0), 0.0)
        s *= 2
    return v


def _hgrn_gates(hq, hf, lb):
    sig = _sigmoid(hf)
    f = lb + (1.0 - lb) * sig
    g = jnp.log(f)
    kk = 1.0 - f
    r = lax.broadcasted_iota(jnp.int32, (TH, TH), 0)
    c = lax.broadcasted_iota(jnp.int32, (TH, TH), 1)
    tri = ((r // HG_BLOCK) == (c // HG_BLOCK)) & (r >= c)
    cum = _chunk_cumsum(g)
    nch = TH // HG_BLOCK
    total = _chunks(cum)[:, HG_BLOCK - 1:HG_BLOCK, :]
    lastb = jnp.broadcast_to(total, (nch, HG_BLOCK, hf.shape[-1])).reshape(hf.shape)
    e, ei, ee = jnp.exp(cum), jnp.exp(-cum), jnp.exp(lastb - cum)
    return dict(sig=sig, f=f, kk=kk, tri=tri, cum=cum, total=total, decay=jnp.exp(total), e=e, ei=ei, ee=ee,
                qd=hq * e, ki=kk * ei, ke=kk * ee)


def _chunks(v):
    return v.reshape(TH // HG_BLOCK, HG_BLOCK, v.shape[-1])


def _bmm_nt(a, b):
    return lax.dot_general(a, b, (((2,), (2,)), ((0,), (0,))), preferred_element_type=F32)


def _bmm_nn(a, b):
    return lax.dot_general(a, b, (((2,), (1,)), ((0,), (0,))), preferred_element_type=F32)


def _bmm_tn(a, b):
    return lax.dot_general(a, b, (((1,), (1,)), ((0,), (0,))), preferred_element_type=F32)


def _pair_masks():
    lane = lax.broadcasted_iota(jnp.int32, (TH, LANE), 1)
    kr = lax.broadcasted_iota(jnp.int32, (LANE, LANE), 0)
    kc = lax.broadcasted_iota(jnp.int32, (LANE, LANE), 1)
    return lane < 64, (kr // 64) == (kc // 64)


def _hgrn_fwd(proj, lbl):
    s = proj.shape[0]
    nch = TH // HG_BLOCK

    def body(hq_ref, hf_ref, hi_ref, lbl_ref, o_ref, st_ref, st):
        @pl.when(pl.program_id(1) == 0)
        def _():
            st[...] = jnp.zeros_like(st)

        m0, bd = _pair_masks()
        gt = _hgrn_gates(hq_ref[...], hf_ref[...], _lower_bound(lbl_ref[...]))
        v_b, qd, qd_b = hi_ref[...].astype(BF16), gt["qd"], gt["qd"].astype(BF16)
        ki_b, ke_b = gt["ki"].astype(BF16), gt["ke"].astype(BF16)
        pairs = [slice(u * LANE, (u + 1) * LANE) for u in range(HG_PAIRS)]
        heads = [(lanes, m0 if hh == 0 else jnp.logical_not(m0)) for lanes in pairs for hh in range(2)]
        a_b = [jnp.where(gt["tri"], _mm_nt(jnp.where(mh, qd[:, lanes], 0.0).astype(BF16), ki_b[:, lanes]), 0.0).astype(BF16)
               for lanes, mh in heads]
        intra = [jnp.where(m0, _mm(a_b[2 * u], v_b[:, lanes]), _mm(a_b[2 * u + 1], v_b[:, lanes])) for u, lanes in enumerate(pairs)]
        upd = [_bmm_tn(_chunks(v_b[:, lanes]), _chunks(ke_b[:, lanes])) for lanes in pairs]
        entering = []
        for u, lanes in enumerate(pairs):
            cur, states = st[u], []
            for n in range(nch):
                states.append(cur)
                cur = gt["decay"][n][:, lanes] * cur + jnp.where(bd, upd[u][n], 0.0)
            st[u] = cur
            entering.append(jnp.stack(states))
            st_ref[u] = entering[u]
        for u, lanes in enumerate(pairs):
            o_ref[:, lanes] = intra[u] + _bmm_nt(_chunks(qd_b[:, lanes]), entering[u].astype(BF16)).reshape(TH, LANE)

    wide = HG_PAIRS * LANE
    col = lambda base: pl.BlockSpec((TH, wide), lambda p, i: (i, base // wide + p))
    return pl.pallas_call(
        body, name="hgrn_fwd", grid=(NH // 2 // HG_PAIRS, s // TH),
        in_specs=[col(P_HQ), col(P_HF), col(P_HI), pl.BlockSpec((2, wide), lambda p, i: (0, p))],
        out_specs=[pl.BlockSpec((TH, wide), lambda p, i: (i, p)),
                   pl.BlockSpec((HG_PAIRS, nch, LANE, LANE), lambda p, i: (p, i, 0, 0))],
        out_shape=[jax.ShapeDtypeStruct((s, 512), F32), jax.ShapeDtypeStruct((NH // 2, s // HG_BLOCK, LANE, LANE), F32)],
        scratch_shapes=[pltpu.VMEM((HG_PAIRS, LANE, LANE), F32)],
        compiler_params=_params(("parallel", "arbitrary")),
    )(proj, proj, proj, lbl)


def _group_sum(v):
    low = lax.broadcasted_iota(jnp.int32, (v.shape[0], LANE), 1) < V_DIM
    blocks = []
    for b in range(v.shape[1] // LANE):
        blk = v[:, b * LANE:(b + 1) * LANE]
        s_low = jnp.sum(jnp.where(low, blk, 0.0), axis=-1, keepdims=True)
        s_high = jnp.sum(jnp.where(low, 0.0, blk), axis=-1, keepdims=True)
        blocks.append(jnp.where(low, s_low, s_high))
    return jnp.concatenate(blocks, axis=1)


def _dsilu(z, sg):
    return sg * (1.0 + z * (1.0 - sg))


def _mid(proj, attn, o_raw, x, tgt, g_hg, b_gate, g_post, wa, wb, w_out):
    s = x.shape[0]

    def body(attn_ref, ga_ref, o_ref, gb_ref, mg_ref, x_ref, t_ref, ghg_ref, bg_ref, gp_ref, wa_ref, wb_ref, wo_ref,
             loss_ref, dout_ref, dattn_ref, dga_ref, dor_ref, dgb_ref, dmg_ref, dwo_out, dwa_out, dwb_out, dgp_ref, dbg_ref, dghg_ref,
             dwo_ref, dwa_ref, dwb_ref):
        @pl.when(pl.program_id(0) == 0)
        def _():
            for rf in (loss_ref, dwo_ref, dwa_ref, dwb_ref, dgp_ref, dbg_ref, dghg_ref):
                rf[...] = jnp.zeros_like(rf)

        attn, za, orw, zb = attn_ref[...], ga_ref[...], o_ref[...], gb_ref[...]
        ghg, gp = ghg_ref[...], gp_ref[...]
        sga, sgb = _sigmoid(za), _sigmoid(zb)
        sa, sb = za * sga, zb * sgb
        ga = attn * sa
        rh = lax.rsqrt(_group_sum(orw * orw) * (1.0 / V_DIM) + EPS)
        on = (orw * rh) * ghg
        gb = on * sb
        ga_b, gb_b = ga.astype(BF16), gb.astype(BF16)
        blocks = [slice(t * (D // N_CHIPS), (t + 1) * (D // N_CHIPS)) for t in range(N_CHIPS)]
        ya = jnp.concatenate([_mm(ga_b, wa_ref[t]) for t in range(N_CHIPS)], axis=1)
        yb = jnp.concatenate([_mm(gb_b, wb_ref[t]) for t in range(N_CHIPS)], axis=1)
        gates = _sigmoid(mg_ref[...] + bg_ref[...])
        g0, g1 = gates[:, :D], gates[:, D:]
        m_b = (g0 * ya + g1 * yb).astype(BF16)
        y = _mm(m_b, wo_ref[...])
        ry = lax.rsqrt(jnp.mean(y * y, axis=-1, keepdims=True) + EPS)
        out = x_ref[...] + (y * ry) * gp
        err = out - t_ref[...]
        loss_ref[...] += 0.5 * jnp.sum(jnp.mean(err * err, axis=-1, keepdims=True), axis=0, keepdims=True)
        dout = err * (1.0 / D)
        dout_ref[...] = dout
        dgp_ref[...] += jnp.sum(dout * (y * ry), axis=0, keepdims=True)
        dgy = dout * gp
        dy = ry * dgy - y * (ry * ry * ry) * jnp.mean(y * dgy, axis=-1, keepdims=True)
        dy_b = dy.astype(BF16)
        dm = _mm_nt(dy_b, wo_ref[...])
        dya_b, dyb_b = (dm * g0).astype(BF16), (dm * g1).astype(BF16)
        dga = sum(_mm_nt(dya_b[:, cols], wa_ref[t]) for t, cols in enumerate(blocks))
        dgb = sum(_mm_nt(dyb_b[:, cols], wb_ref[t]) for t, cols in enumerate(blocks))
        dwo_ref[...] += _mm_tn(m_b, dy_b)
        for t, cols in enumerate(blocks):
            dwa_ref[t] += _mm_tn(ga_b, dya_b[:, cols])
            dwb_ref[t] += _mm_tn(gb_b, dyb_b[:, cols])
        dg0, dg1 = dm * ya, dm * yb
        dmg = jnp.concatenate([dg0 * g0 * (1.0 - g0), dg1 * g1 * (1.0 - g1)], axis=1)
        dmg_ref[...] = dmg.astype(BF16)
        dbg_ref[...] += jnp.sum(dmg, axis=0, keepdims=True)
        dattn_ref[...] = dga * sa
        dga_ref[...] = (dga * attn * _dsilu(za, sga)).astype(BF16)
        dgb_ref[...] = (dgb * on * _dsilu(zb, sgb)).astype(BF16)
        don = dgb * sb
        dghg_ref[...] += jnp.sum(don * (orw * rh), axis=0, keepdims=True)
        dgo = don * ghg
        dor_ref[...] = rh * dgo - orw * (rh * rh * rh) * (_group_sum(orw * dgo) * (1.0 / V_DIM))

        @pl.when(pl.program_id(0) == pl.num_programs(0) - 1)
        def _():
            for out, rf in ((dwo_out, dwo_ref), (dwa_out, dwa_ref), (dwb_out, dwb_ref)):
                out[...] = rf[...].astype(BF16)

    row = lambda w, j=0: pl.BlockSpec((TM_MID, w), lambda i: (i, j))
    full = lambda a: pl.BlockSpec(a.shape, lambda i: (0,) * a.ndim)
    acc = lambda shape: pl.BlockSpec(shape, lambda i: (0,) * len(shape))
    slabs = (N_CHIPS, 512, D // N_CHIPS)
    sds = jax.ShapeDtypeStruct
    return pl.pallas_call(
        body, name="mid", grid=(s // TM_MID,),
        in_specs=[row(512), row(512, P_GA // 512), row(512), row(512, P_GB // 512), row(2048, P_MERGE // 2048), row(D), row(D),
                  full(g_hg), full(b_gate), full(g_post), full(wa), full(wb), full(w_out)],
        out_specs=[acc((1, 1)), row(D), row(512), row(512), row(512), row(512), row(2048),
                   acc((D, D)), acc(slabs), acc(slabs), acc((1, D)), acc((1, 2048)), acc((1, 512))],
        out_shape=[sds((1, 1), F32), sds((s, D), F32), sds((s, 512), F32), sds((s, 512), BF16), sds((s, 512), F32), sds((s, 512), BF16),
                   sds((s, 2048), BF16), sds((D, D), BF16), sds(slabs, BF16), sds(slabs, BF16), sds((1, D), F32),
                   sds((1, 2048), F32), sds((1, 512), F32)],
        scratch_shapes=[pltpu.VMEM((D, D), F32), pltpu.VMEM(slabs, F32), pltpu.VMEM(slabs, F32)],
        compiler_params=_params(("arbitrary",)),
    )(attn, proj, o_raw, proj, proj, x, tgt, g_hg, b_gate, g_post, wa, wb, w_out)


def _attn_bwd(q, k, vv, attn, dattn, lse, token):
    s = q.shape[1]
    nt = s // TQ
    scale = 1.0 / math.sqrt(QK_NOPE + QK_ROPE)

    def body(q_ref, k_ref, v_ref, o_ref, do_ref, lse_ref, token_ref, dq_ref, dk_ref, dv_ref, do_s, delta_s):
        j = pl.program_id(1)

        @pl.when(j == 0)
        def _():
            dq_ref[...] = jnp.zeros_like(dq_ref)
            lane = lax.broadcasted_iota(jnp.int32, (TQ, LANE), 1)

            @pl.loop(0, nt)
            def _(i):
                rows = pl.ds(pl.multiple_of(i * TQ, TQ), TQ)
                do, o = do_ref[rows, :], o_ref[rows, :]
                for hh in range(2):
                    doh = jnp.where((lane < 64) if hh == 0 else (lane >= 64), do, 0.0)
                    do_s[hh, rows, :] = doh.astype(BF16)
                    delta_s[hh, rows, :] = jnp.broadcast_to(jnp.sum(doh * o, axis=-1, keepdims=True), (TQ, LANE))

        kjs, vjs = (k_ref[0], k_ref[1]), (v_ref[0], v_ref[1])

        def tile(hh, start, size, kj, vj, diag):
            rows = pl.ds(pl.multiple_of(start, size), size)
            wide = lambda a: jnp.concatenate([a] * (kj.shape[0] // LANE), axis=1)
            qi, do_b = q_ref[hh, rows, :], do_s[hh, rows, :]
            sc, dp = _mm_nt(qi, kj), _mm_nt(do_b, vj)
            p = jnp.exp2(sc * QK_SCALE2 - wide(lse_ref[hh, rows, :]))
            if diag:
                p = jnp.where(_diag_visible(size), p, 0.0)
            ds_b = (p * (dp - wide(delta_s[hh, rows, :]))).astype(BF16)
            dv, dk = _mm_tn(do_b, p.astype(BF16)), _mm_tn(qi, ds_b)
            dq_ref[hh, rows, :] += _mm(ds_b, kj)
            return dk, dv

        def step(i, carry):
            new = [tile(hh, i * TQ, TQ, kjs[hh], vjs[hh], False) for hh in range(2)]
            return tuple((carry[hh][0] + new[hh][0], carry[hh][1] + new[hh][1]) for hh in range(2))

        def diagonal(hh):
            k0, k1, v0, v1 = kjs[hh][:HQ], kjs[hh][HQ:], vjs[hh][:HQ], vjs[hh][HQ:]
            a = tile(hh, j * TQ, HQ, k0, v0, True)
            b = tile(hh, j * TQ + HQ, HQ, k0, v0, False)
            c = tile(hh, j * TQ + HQ, HQ, k1, v1, True)
            return jnp.concatenate([a[0] + b[0], c[0]], axis=1), jnp.concatenate([a[1] + b[1], c[1]], axis=1)

        carry = lax.fori_loop(j + 1, nt, step, (diagonal(0), diagonal(1)))
        for hh in range(2):
            dk_ref[hh] = carry[hh][0].T * scale
            dv_ref[hh] = carry[hh][1].T

        @pl.when(j == nt - 1)
        def _():
            dq_ref[...] = dq_ref[...] * scale

    whole = pl.BlockSpec((2, s, LANE), lambda p, j: (p, 0, 0))
    tile_spec = pl.BlockSpec((2, TQ, LANE), lambda p, j: (p, j, 0))
    cols = pl.BlockSpec((s, LANE), lambda p, j: (0, p))
    hs = jax.ShapeDtypeStruct((NH, s, LANE), F32)
    return pl.pallas_call(
        body, name="attn_bwd", grid=(NH // 2, nt),
        in_specs=[whole, tile_spec, tile_spec, cols, cols, whole, pl.BlockSpec((8, LANE), lambda p, j: (0, 0))],
        out_specs=[whole, tile_spec, tile_spec],
        out_shape=[hs, hs, hs],
        scratch_shapes=[pltpu.VMEM((2, s, LANE), BF16), pltpu.VMEM((2, s, LANE), F32)],
        compiler_params=_params(("parallel", "arbitrary")),
    )(q, k, vv, attn, dattn, lse, token)


def _hgrn_bwd(proj, lbl, states, do_raw):
    s = proj.shape[0]
    nt = s // TH
    nch = TH // HG_BLOCK

    def body(hq_ref, hf_ref, hi_ref, lbl_ref, st_ref, do_ref, dh_ref, dlbl_ref, dst, dlb):
        step = pl.program_id(1)

        @pl.when(step == 0)
        def _():
            dst[...] = jnp.zeros_like(dst)
            dlb[...] = jnp.zeros_like(dlb)

        m0, bd = _pair_masks()
        lb = _lower_bound(lbl_ref[...])
        gt = _hgrn_gates(hq_ref[...], hf_ref[...], lb)
        do = do_ref[...]
        qd, ki, ke = gt["qd"], gt["ki"], gt["ke"]
        v_b, do_b = hi_ref[...].astype(BF16), do.astype(BF16)
        qd_b, ki_b, ke_b = qd.astype(BF16), ki.astype(BF16), ke.astype(BF16)
        pairs = [slice(u * LANE, (u + 1) * LANE) for u in range(HG_PAIRS)]
        heads = [(lanes, m0 if hh == 0 else jnp.logical_not(m0)) for lanes in pairs for hh in range(2)]
        a_b = [jnp.where(gt["tri"], _mm_nt(jnp.where(mh, qd[:, lanes], 0.0).astype(BF16), ki_b[:, lanes]), 0.0).astype(BF16)
               for lanes, mh in heads]
        doh_b = [jnp.where(mh, do[:, lanes], 0.0).astype(BF16) for lanes, mh in heads]
        da_b = [jnp.where(gt["tri"], _mm_nt(d, v_b[:, lanes]), 0.0).astype(BF16) for d, (lanes, _) in zip(doh_b, heads)]
        dv_p, dqd_p, dki_p = [], [], []
        for u, lanes in enumerate(pairs):
            e, o = 2 * u, 2 * u + 1
            dv_p.append(_mm_tn(a_b[e], doh_b[e]) + _mm_tn(a_b[o], doh_b[o]))
            dqd_p.append(jnp.where(m0, _mm(da_b[e], ki_b[:, lanes]), _mm(da_b[o], ki_b[:, lanes])))
            dki_p.append(jnp.where(m0, _mm_tn(da_b[e], qd_b[:, lanes]), _mm_tn(da_b[o], qd_b[:, lanes])))
        fed = [_bmm_tn(_chunks(do_b[:, lanes]), _chunks(qd_b[:, lanes])) for lanes in pairs]
        leaving = []
        for u, lanes in enumerate(pairs):
            ds, left = dst[u], [None] * nch
            for n in reversed(range(nch)):
                left[n] = ds
                ds = gt["decay"][n][:, lanes] * ds + jnp.where(bd, fed[u][n], 0.0)
            dst[u] = ds
            leaving.append(jnp.stack(left))
        dke_p, dlast_p = [], []
        for u, lanes in enumerate(pairs):
            entering, leaving_b = st_ref[u], leaving[u].astype(BF16)
            dke3 = _bmm_nn(_chunks(v_b[:, lanes]), leaving_b)
            dv_p[u] = dv_p[u] + _bmm_nt(_chunks(ke_b[:, lanes]), leaving_b).reshape(TH, LANE)
            dqd_p[u] = dqd_p[u] + _bmm_nn(_chunks(do_b[:, lanes]), entering.astype(BF16)).reshape(TH, LANE)
            dke_p.append(dke3.reshape(TH, LANE))
            dlast_p.append(jnp.sum(dke3 * _chunks(ke[:, lanes]), axis=1, keepdims=True)
                           + jnp.sum(leaving[u] * entering, axis=1, keepdims=True) * gt["decay"][:, :, lanes])
        cat = lambda parts: jnp.concatenate(parts, axis=-1)
        dv, dqd, dki, dke, dlast = cat(dv_p), cat(dqd_p), cat(dki_p), cat(dke_p), cat(dlast_p)
        dk = dki * gt["ei"] + dke * gt["ee"]
        dcum = dqd * qd - dki * ki - dke * ke
        dg = _chunk_cumsum(dcum, reverse=True) + jnp.broadcast_to(dlast, (nch, HG_BLOCK, dlast.shape[-1])).reshape(dcum.shape)
        sig = gt["sig"]
        df = dg / gt["f"] - dk
        dlb[...] += jnp.sum(df * (1.0 - sig), axis=0, keepdims=True)
        dh_ref[0] = (dqd * gt["e"]).astype(BF16)
        dh_ref[1] = ((df * (1.0 - lb)) * sig * (1.0 - sig)).astype(BF16)
        dh_ref[2] = dv.astype(BF16)

        @pl.when(step == nt - 1)
        def _():
            lb = _lower_bound(lbl_ref[...])
            da0 = dlb[...] * lb * (1.0 - lb)
            dlbl_ref[...] = jnp.concatenate([da0, -da0], axis=0)

    wide = HG_PAIRS * LANE
    col = lambda base: pl.BlockSpec((TH, wide), lambda p, i: (nt - 1 - i, base // wide + p))
    tile = pl.BlockSpec((TH, wide), lambda p, i: (nt - 1 - i, p))
    sds = jax.ShapeDtypeStruct
    return pl.pallas_call(
        body, name="hgrn_bwd", grid=(NH // 2 // HG_PAIRS, nt),
        in_specs=[col(P_HQ), col(P_HF), col(P_HI), pl.BlockSpec((2, wide), lambda p, i: (0, p)),
                  pl.BlockSpec((HG_PAIRS, nch, LANE, LANE), lambda p, i: (p, nt - 1 - i, 0, 0)), tile],
        out_specs=[pl.BlockSpec((3, TH, wide), lambda p, i: (0, nt - 1 - i, p)), pl.BlockSpec((2, wide), lambda p, i: (0, p))],
        out_shape=[sds((3, s, 512), BF16), sds((2, 512), F32)],
        scratch_shapes=[pltpu.VMEM((HG_PAIRS, LANE, LANE), F32), pltpu.VMEM((1, wide), F32)],
        compiler_params=_params(("parallel", "arbitrary")),
    )(proj, proj, proj, lbl, states, do_raw)


def _norm_rows_bwd(v, r, g, dn):
    dgv = dn * g
    return r * dgv - v * (r * r * r) * jnp.mean(v * dgv, axis=-1, keepdims=True)


def _qkv_bwd(proj, dq, dk, dvv, g_q, g_kv, w_uq_p, w_k_p, w_v_p, rc, rs1, rs2):
    s = proj.shape[0]
    head_q = QK_NOPE + QK_ROPE

    def body(cq_ref, ckv_ref, dq_ref, dk_ref, dv_ref, gq_ref, gkv_ref, wq_ref, wk_ref, wv_ref, c_ref, s1_ref, s2_ref,
             dcq_ref, dckv_ref, dkpe_ref, dwq_out, dwkv_out, dgq_ref, dgkv_ref, dwq_ref, dwk_ref, dwv_ref):
        @pl.when(pl.program_id(0) == 0)
        def _():
            for rf in (dwq_ref, dwk_ref, dwv_ref, dgq_ref, dgkv_ref):
                rf[...] = jnp.zeros_like(rf)

        c, s1, s2 = c_ref[...], s1_ref[...], s2_ref[...]
        cq, ckv = cq_ref[...], ckv_ref[...]
        gq, gkv = gq_ref[...], gkv_ref[...]
        cqn, rq = _norm_rows(cq, gq)
        ckvn, rkv = _norm_rows(ckv, gkv)
        cqn_b, ckvn_b = cqn.astype(BF16), ckvn.astype(BF16)
        dqf = jnp.concatenate([_rope_t(dq_ref[h], c, s1, s2) for h in range(NH)], axis=1).astype(BF16)
        dkf = jnp.concatenate([dk_ref[h] for h in range(NH)], axis=1).astype(BF16)
        dvf = jnp.concatenate([dv_ref[h] for h in range(NH)], axis=1).astype(BF16)
        dkpe = dk_ref[0]
        for h in range(1, NH):
            dkpe = dkpe + dk_ref[h]
        lane = lax.broadcasted_iota(jnp.int32, (TM, LANE), 1)
        dkpe = jnp.where((lane >= QK_NOPE) & (lane < QK_NOPE + QK_ROPE), dkpe, 0.0)
        dkpe_ref[...] = _rope_t(dkpe, c, s1, s2).astype(BF16)
        dcqn = _mm_nt(dqf, wq_ref[...])
        pair = lambda a, t: a[:, t * 2 * LANE:(t + 1) * 2 * LANE]
        dckvn = sum(_mm_nt(pair(dkf, t), wk_ref[t]) + _mm_nt(pair(dvf, t), wv_ref[t]) for t in range(N_CHIPS))
        dwq_ref[...] += _mm_tn(cqn_b, dqf)
        dwk_ref[...] += _mm_tn(ckvn_b, dkf)
        dwv_ref[...] += _mm_tn(ckvn_b, dvf)
        dgq_ref[...] += jnp.sum(dcqn * (cq * rq), axis=0, keepdims=True)
        dgkv_ref[...] += jnp.sum(dckvn * (ckv * rkv), axis=0, keepdims=True)
        dcq_ref[...] = _norm_rows_bwd(cq, rq, gq, dcqn).astype(BF16)
        dckv_ref[...] = _norm_rows_bwd(ckv, rkv, gkv, dckvn).astype(BF16)

        @pl.when(pl.program_id(0) == pl.num_programs(0) - 1)
        def _():
            blk = lambda ref, h: ref[:, h * LANE:(h + 1) * LANE]
            lane = lax.broadcasted_iota(jnp.int32, (Q_LORA, LANE), 1)
            for j in range(NH * head_q // LANE):
                h0, w0 = divmod(j * LANE, head_q)
                first = blk(dwq_ref, h0) if w0 == 0 else pltpu.roll(blk(dwq_ref, h0), LANE - w0, 1)
                second = pltpu.roll(blk(dwq_ref, h0 + 1), head_q - w0, 1)
                dwq_out[:, j * LANE:(j + 1) * LANE] = jnp.where(lane < head_q - w0, first, second).astype(BF16)
            lane = lax.broadcasted_iota(jnp.int32, (KV_LORA, LANE), 1)
            for h in range(NH):
                vals = blk(dwv_ref, h) if h % 2 else pltpu.roll(blk(dwv_ref, h), V_DIM, 1)
                both = jnp.where(lane < QK_NOPE, blk(dwk_ref, h), vals).astype(BF16)
                dwkv_out[h // 2, :, (h % 2) * LANE:(h % 2 + 1) * LANE] = both

    row = lambda w, j=0: pl.BlockSpec((TM, w), lambda i: (i, j))
    full = lambda a: pl.BlockSpec(a.shape, lambda i: (0,) * a.ndim)
    acc = lambda *shape: pl.BlockSpec(shape, lambda i: (0,) * len(shape))
    heads = pl.BlockSpec((NH, TM, LANE), lambda i: (0, i, 0))
    sds = jax.ShapeDtypeStruct
    return pl.pallas_call(
        body, name="qkv_bwd", grid=(s // TM,),
        in_specs=[row(Q_LORA, P_CQ // Q_LORA), row(KV_LORA, P_CKV // KV_LORA), heads, heads, heads,
                  full(g_q), full(g_kv), full(w_uq_p), full(w_k_p), full(w_v_p), row(LANE), row(LANE), row(LANE)],
        out_specs=[row(Q_LORA), row(KV_LORA), row(LANE), acc(Q_LORA, NH * head_q), acc(NH // 2, KV_LORA, 2 * LANE),
                   acc(1, Q_LORA), acc(1, KV_LORA)],
        out_shape=[sds((s, Q_LORA), BF16), sds((s, KV_LORA), BF16), sds((s, LANE), BF16), sds((Q_LORA, NH * head_q), BF16),
                   sds((NH // 2, KV_LORA, 2 * LANE), BF16), sds((1, Q_LORA), F32), sds((1, KV_LORA), F32)],
        scratch_shapes=[pltpu.VMEM((Q_LORA, D), F32), pltpu.VMEM((KV_LORA, D), F32), pltpu.VMEM((KV_LORA, D), F32)],
        compiler_params=_params(("arbitrary",)),
    )(proj, proj, dq, dk, dvv, g_q, g_kv, w_uq_p, w_k_p, w_v_p, rc, rs1, rs2)


def _front_bwd(x, dout, dmg, dga, dh3, dgb, dcq, dckv, dkpe, g_pre, w_in_t, w_kpe, token):
    s = x.shape[0]

    def body(x_ref, do_ref, dmg_ref, dga_ref, dh3_ref, dgb_ref, dcq_ref, dckv_ref, dkpe_ref, g_ref, w_ref, k_ref, token_ref,
             gx_ref, dg_ref):
        @pl.when(pl.program_id(0) == 0)
        def _():
            dg_ref[...] = jnp.zeros_like(dg_ref)

        xv, g = x_ref[...], g_ref[...]
        _, r = _norm_rows(xv, g)
        pieces = ((dmg_ref[...], O_MERGE), (dga_ref[...], O_GA), (dh3_ref[0], O_HQ), (dh3_ref[1], O_HF), (dh3_ref[2], O_HI),
                  (dgb_ref[...], O_GB), (dcq_ref[...], O_CQ), (dckv_ref[...], O_CKV))
        dh = _mm(dkpe_ref[...], k_ref[...])
        for piece, off in pieces:
            dh = dh + _mm(piece, w_ref[off:off + piece.shape[1], :])
        dg_ref[...] += jnp.sum(dh * (xv * r), axis=0, keepdims=True)
        gx_ref[...] = do_ref[...] + _norm_rows_bwd(xv, r, g, dh)

    row = lambda w: pl.BlockSpec((TM, w), lambda i: (i, 0))
    full = lambda a: pl.BlockSpec(a.shape, lambda i: (0,) * a.ndim)
    sds = jax.ShapeDtypeStruct
    return pl.pallas_call(
        body, name="front_bwd", grid=(s // TM,),
        in_specs=[row(D), row(D), row(2048), row(512), pl.BlockSpec((3, TM, 512), lambda i: (0, i, 0)), row(512), row(Q_LORA),
                  row(KV_LORA), row(LANE), full(g_pre), full(w_in_t), full(w_kpe), pl.BlockSpec(memory_space=pl.ANY)],
        out_specs=[row(D), pl.BlockSpec((1, D), lambda i: (0, 0))],
        out_shape=[sds((s, D), F32), sds((1, D), F32)],
        compiler_params=_params(("arbitrary",)),
    )(x, dout, dmg, dga, dh3, dgb, dcq, dckv, dkpe, g_pre, w_in_t, w_kpe, token)


TK_GRAD = 1024


QKV_ROWS = Q_LORA + KV_LORA + QK_ROPE
D_IN = O_MERGE + 2048


def _win_grad(h, pieces, offsets, name, whole=None):
    s = h.shape[0]
    n = len(pieces)
    new = whole is None
    shapes = [(p.shape[0] * p.shape[2], D) if p.ndim == 3 else (p.shape[1], D) for p in pieces]

    def body(h_ref, *refs):
        d_refs, whole_ref, scratch = refs[:n], refs[n if new else n + 1], refs[n + (1 if new else 2):]
        sums, rounded, sems = scratch[:n], scratch[n:2 * n], scratch[-1]

        @pl.when(pl.program_id(0) == 0)
        def _():
            for s_ref in sums:
                s_ref[...] = jnp.zeros_like(s_ref)

        hv = h_ref[...]
        for d_ref, s_ref in zip(d_refs, sums):
            if len(d_ref.shape) == 3:
                w = d_ref.shape[2]
                for k in range(d_ref.shape[0]):
                    s_ref[k * w:(k + 1) * w] += _mm_tn(d_ref[k], hv)
            else:
                s_ref[...] += _mm_tn(d_ref[...], hv)

        @pl.when(pl.program_id(0) == pl.num_programs(0) - 1)
        def _():
            copies = []
            for j, (s_ref, r_ref, row) in enumerate(zip(sums, rounded, offsets)):
                r_ref[...] = s_ref[...].astype(BF16)
                copies.append(pltpu.make_async_copy(r_ref, whole_ref.at[pl.ds(row, r_ref.shape[0])], sems.at[j]))
            if new:
                zeros = scratch[2 * n]
                zeros[...] = jnp.zeros_like(zeros)
                copies.append(pltpu.make_async_copy(zeros, whole_ref.at[pl.ds(0, QKV_ROWS)], sems.at[n]))
            for copy in copies:
                copy.start()
            for copy in copies:
                copy.wait()

    def in_spec(p):
        if p.ndim == 3:
            return pl.BlockSpec((p.shape[0], TK_GRAD, p.shape[2]), lambda kk: (0, kk, 0))
        return pl.BlockSpec((TK_GRAD, p.shape[1]), lambda kk: (kk, 0))

    anywhere = pl.BlockSpec(memory_space=pl.ANY)
    return pl.pallas_call(
        body, name=name, grid=(s // TK_GRAD,),
        in_specs=[pl.BlockSpec((TK_GRAD, D), lambda kk: (kk, 0))] + [in_spec(p) for p in pieces] + ([] if new else [anywhere]),
        out_specs=anywhere, out_shape=jax.ShapeDtypeStruct((D_IN, D), BF16),
        input_output_aliases={} if new else {n + 1: 0},
        scratch_shapes=[pltpu.VMEM(sh, F32) for sh in shapes] + [pltpu.VMEM(sh, BF16) for sh in shapes]
        + ([pltpu.VMEM((QKV_ROWS, D), BF16)] if new else []) + [pltpu.SemaphoreType.DMA((n + 1,))],
        compiler_params=_params(("arbitrary",)),
    )(h, *pieces, *([] if new else [whole]))


def _win_grad_qkv(h, dcq, dckv, dkpe, share):
    s = h.shape[0]
    half = QKV_ROWS // 2

    def body(h_ref, cq_ref, ckv_ref, kpe_ref, o_ref, *scratch):
        sums = scratch[0] if share else o_ref

        @pl.when(pl.program_id(0) == 0)
        def _():
            sums[...] = jnp.zeros_like(sums)

        hv = h_ref[...]
        g_cq = _mm_tn(cq_ref[...], hv)
        sums[0] += g_cq[:half]
        sums[1, 0:Q_LORA - half] += g_cq[half:]
        sums[1, Q_LORA - half:Q_LORA + KV_LORA - half] += _mm_tn(ckv_ref[...], hv)
        sums[1, Q_LORA + KV_LORA - half:] += _mm_tn(kpe_ref[...], hv)[QK_NOPE:QK_NOPE + QK_ROPE]

        if share:
            _, theirs, send_sem, recv_sem = scratch

            @pl.when(pl.program_id(0) == pl.num_programs(0) - 1)
            def _():
                c = lax.axis_index("c")
                copy = _remote(sums.at[1 - c], theirs, send_sem, recv_sem, 0, (lax.axis_index("x"), lax.axis_index("y"), 1 - c))
                copy.start()
                copy.wait()
                o_ref[...] = (sums[c] + theirs[...]).astype(BF16)

    rows = lambda a: pl.BlockSpec((TK_GRAD, a.shape[1]), lambda kk: (kk, 0))
    sem = pltpu.SemaphoreType.DMA((1,))
    shape, dtype = ((half, D), BF16) if share else ((2, half, D), F32)
    return pl.pallas_call(
        body, name="win_grad_qkv", grid=(s // TK_GRAD,), in_specs=[rows(h), rows(dcq), rows(dckv), rows(dkpe)],
        out_specs=pl.BlockSpec(shape, lambda kk: (0,) * len(shape)), out_shape=jax.ShapeDtypeStruct(shape, dtype),
        scratch_shapes=[pltpu.VMEM((2, half, D), F32), pltpu.VMEM((half, D), F32), sem, sem] if share else [],
        compiler_params=_params(("arbitrary",)),
    )(h, dcq, dckv, dkpe)


def _pad_wuq(w_uq):
    rows = w_uq.shape[0]
    w = w_uq.reshape(rows, NH, QK_NOPE + QK_ROPE)
    return jnp.pad(w, ((0, 0), (0, 0), (0, LANE - QK_NOPE - QK_ROPE))).reshape(rows, NH * LANE)


def _pad_wukv(w_ukv):
    heads = w_ukv.shape[1] // (QK_NOPE + V_DIM)
    w = w_ukv.reshape(KV_LORA, heads, QK_NOPE + V_DIM)
    w_k = jnp.pad(w[:, :, :QK_NOPE], ((0, 0), (0, 0), (0, LANE - QK_NOPE))).reshape(KV_LORA, heads * LANE)
    wv = w[:, :, QK_NOPE:].reshape(KV_LORA, heads // 2, 2, 1, V_DIM)
    eye = jnp.eye(2, dtype=w.dtype).reshape(1, 1, 2, 2, 1)
    return w_k, (wv * eye).reshape(KV_LORA, heads * LANE)


def _local_step(x, tgt, g_pre, w_in_t, b_gate, g_q, g_kv, lb_logits, g_hgrn, g_post, weights, exchange=None):
    s = x.shape[0]
    w_kpe = _kpe_block(w_in_t)
    rc, rs1, rs2 = _rope_tables(s)
    g_hg = jnp.tile(g_hgrn, (1, NH))

    proj, h = _front_fwd(x, g_pre, w_in_t, w_kpe, weights.tokens)
    w_uq_p, w_k_p, w_v_p = weights.qkv(h)
    q, k, vv = _qkv_fwd(proj, g_q, g_kv, w_uq_p, w_k_p, w_v_p, rc, rs1, rs2)
    attn, lse = _attn_fwd(q, k, vv)
    o_raw, states = _hgrn_fwd(proj, lb_logits)
    wa, wb, w_out = weights.mid(o_raw)
    (loss, dout, dattn, dga, dor, dgb, dmg, d_wout, d_wa, d_wb, d_gpost, d_bgate, d_ghg) = _mid(
        proj, attn, o_raw, x, tgt, g_hg, b_gate, g_post, wa, wb, w_out)
    d_win = _win_grad(h, [dmg, dga, dgb], [O_MERGE, O_GA, O_GB], "win_grad_mid")
    dh3, d_lbl = _hgrn_bwd(proj, lb_logits, states, dor)
    d_win = _win_grad(h, [dh3], [O_HQ], "win_grad_hgrn", d_win)
    early = dict(w_in=d_win, w_branch_a=d_wa, w_branch_b=d_wb, w_out=d_wout)
    token = exchange.start_early(early) if exchange else jnp.zeros((8, LANE), F32)
    dq, dk, dvv = _attn_bwd(q, k, vv, attn, dattn, lse, token)
    dcq, dckv, dkpe, d_wuq, d_wukv, d_gq, d_gkv = _qkv_bwd(proj, dq, dk, dvv, g_q, g_kv, w_uq_p, w_k_p, w_v_p, rc, rs1, rs2)
    late = dict(w_in_qkv=_win_grad_qkv(h, dcq, dckv, dkpe, share=exchange is not None), w_uq=d_wuq, w_ukv=d_wukv)
    token = exchange.start_late(late) if exchange else jnp.zeros((8, LANE), F32)
    grad_x, d_gpre = _front_bwd(x, dout, dmg, dga, dh3, dgb, dcq, dckv, dkpe, g_pre, w_in_t, w_kpe, token)
    vec_grads = dict(g_pre=d_gpre, b_gate=d_bgate, g_q=d_gq, g_kv=d_gkv, lb_logits=d_lbl, g_hgrn=d_ghg, g_post=d_gpost)
    return loss, grad_x, dict(early, **late), vec_grads


SHARD_SHAPES = (("w_in", (1416, 1024)), ("w_uq", (192, 768)), ("w_ukv", (256, 256)), ("w_branch_a", (512, 256)),
                ("w_branch_b", (512, 256)), ("w_out", (256, 1024)))
BIG = tuple(n for n, _ in SHARD_SHAPES)
ROW_SHARDED = ("w_in", "w_uq", "w_out")
N_CHIPS = 4
W_IN_FORWARD_CUT = 704


def _to_block(name, a):
    return a[0].T if name == "w_in" else a[0]


def _from_block(name, a):
    return a.T[None] if name == "w_in" else a[None]
VEC_ROWS = (("g_pre", 0, 1024), ("b_gate", 1, 2048), ("g_q", 2, 768), ("g_kv", 3, 256), ("g_hgrn", 6, 64), ("g_post", 7, 1024))
VEC_LB_ROW = 4
VEC_SHAPE = (8, 2048)


def _split_by_chip(name, g):
    a, b = dict(SHARD_SHAPES)[name]
    return g.reshape(N_CHIPS, a, b) if name in ROW_SHARDED else g.reshape(a, N_CHIPS, b).transpose(1, 0, 2)


def _join_chips(name, w):
    a, b = dict(SHARD_SHAPES)[name]
    return w.reshape(N_CHIPS * a, b) if name in ROW_SHARDED else w.transpose(1, 0, 2).reshape(a, N_CHIPS * b)


MESH = pl.DeviceIdType.MESH
HBM = pl.BlockSpec(memory_space=pltpu.HBM)


def _mesh_place():
    x, y, c = lax.axis_index("x"), lax.axis_index("y"), lax.axis_index("c")
    return x, y, c, 2 * x + y, [(1 - x, y), (x, 1 - y), (1 - x, 1 - y)]


def _remote(src, dst, send_sems, recv_sems, k, to):
    return pltpu.make_async_remote_copy(src_ref=src, dst_ref=dst, send_sem=send_sems.at[k], recv_sem=recv_sems.at[k],
                                        device_id=to, device_id_type=MESH)


def _gather_w_in(shard):
    a, b = shard.shape
    cut = W_IN_FORWARD_CUT

    def body(src, out, ici_send, ici_recv, d2d_send, d2d_recv, local_sem):
        x, y, c = lax.axis_index("x"), lax.axis_index("y"), lax.axis_index("c")
        me, xn, yn, dg = 2 * x + y, 2 * (1 - x) + y, 2 * x + (1 - y), 2 * (1 - x) + (1 - y)
        to_x, to_y, sibling = (1 - x, y, c), (x, 1 - y, c), (x, y, 1 - c)
        first, rest = pl.ds(0, cut), pl.ds(cut, a - cut)

        whole = lambda ref, which: ref.at[:, pl.ds(pl.multiple_of(which * (b // 2), b // 2), b // 2)]
        own = pltpu.make_async_copy(src, out.at[me], local_sem)
        own.start()
        sends = [_remote(whole(src, c), whole(out.at[me], c), ici_send, ici_recv, 0, to_x),
                 _remote(whole(src, c), whole(out.at[me], c), ici_send, ici_recv, 1, to_y)]
        for cp in sends:
            cp.start()

        def landed(slot, rows, k, d2d_k, src_dev):
            piece = whole(out.at[slot], c) if rows is None else out.at[slot].at[rows, pl.ds(pl.multiple_of(c * (b // 2), b // 2), b // 2)]
            _remote(piece, piece, ici_send, ici_recv, k, src_dev).wait_recv()
            cp = _remote(piece, piece, d2d_send, d2d_recv, d2d_k, sibling)
            cp.start()
            sends.append(cp)
            return piece

        def pass_on(slot, rows, k, to):
            piece = out.at[slot].at[rows, pl.ds(pl.multiple_of(c * (b // 2), b // 2), b // 2)]
            cp = _remote(piece, piece, ici_send, ici_recv, k, to)
            cp.start()
            sends.append(cp)

        landed(xn, None, 0, 0, to_x)
        pass_on(xn, first, 2, to_y)
        landed(yn, None, 1, 1, to_y)
        pass_on(yn, rest, 3, to_x)
        landed(dg, first, 2, 2, to_y)
        landed(dg, rest, 3, 3, to_x)
        other = pl.ds(pl.multiple_of((1 - c) * (b // 2), b // 2), b // 2)
        for d2d_k, (slot, rows) in enumerate(((xn, None), (yn, None), (dg, first), (dg, rest))):
            piece = out.at[slot].at[:, other] if rows is None else out.at[slot].at[rows, other]
            _remote(piece, piece, d2d_send, d2d_recv, d2d_k, sibling).wait_recv()
        for cp in sends:
            cp.wait_send()
        own.wait()

    sems = pltpu.SemaphoreType.DMA((4,))
    return pl.pallas_call(
        body, name="gather_w_in", in_specs=[HBM], out_specs=HBM,
        out_shape=jax.ShapeDtypeStruct((N_CHIPS, a, b), shard.dtype),
        scratch_shapes=[sems, sems, sems, sems, pltpu.SemaphoreType.DMA],
        compiler_params=pltpu.CompilerParams(has_side_effects=True),
    )(shard)


SEM =pl.BlockSpec(memory_space=pltpu.SEMAPHORE)
DATAFLOW = pltpu.SideEffectType.DATAFLOW_SIDE_EFFECTING


def _exchange_copies(srcs, to_first, src_refs, land_refs, send_sems, recv_sems):
    x, y, c, me, chips = _mesh_place()
    n = len(srcs)
    sends, recvs = [], []
    for k in range(n):
        if k in to_first:
            base = 3 * n + 4 * to_first.index(k)
            sends.append((me != 0, pltpu.make_async_remote_copy(
                src_ref=src_refs[k], dst_ref=land_refs[k].at[me], send_sem=send_sems.at[base], recv_sem=recv_sems.at[base + me],
                device_id=(0, 0, c), device_id_type=MESH)))
            for s in range(1, N_CHIPS):
                recvs.append((me == 0, pltpu.make_async_remote_copy(
                    src_ref=src_refs[k], dst_ref=land_refs[k].at[s], send_sem=send_sems.at[base], recv_sem=recv_sems.at[base + s],
                    device_id=(s // 2, s % 2, c), device_id_type=MESH)))
        else:
            slab = (lambda t, k=k: src_refs[k]) if srcs[k].ndim == 2 else (lambda t, k=k: src_refs[k].at[t])
            for j, (px, py) in enumerate(chips):
                sends.append((None, _remote(slab(2 * px + py), land_refs[k].at[me], send_sems, recv_sems, 3 * k + j, (px, py, c))))
                recvs.append((None, _remote(slab(me), land_refs[k].at[2 * px + py], send_sems, recv_sems, 3 * k + j, (px, py, c))))
    return sends, recvs


def _when(pred, fn):
    if pred is None:
        fn()
    else:
        pl.when(pred)(fn)


def _exchange_start(srcs, to_first, name, after=None):
    n = len(srcs)
    n_sems = 3 * n + 4 * len(to_first)
    lands = [lax.empty((N_CHIPS,) + s.shape[-2:], s.dtype) for s in srcs]
    extra = [] if after is None else [after]

    def body(*refs):
        src_refs, land_refs = refs[:n], refs[n:2 * n]
        send_sems, recv_sems, token = refs[2 * n + len(extra)], refs[2 * n + len(extra) + 1], refs[-1]
        sends, _ = _exchange_copies(srcs, to_first, src_refs, land_refs, send_sems, recv_sems)
        for pred, cp in sends:
            _when(pred, cp.start)
        token[...] = jnp.zeros_like(token)

    hbm = lambda a: pltpu.HBM(a.shape, a.dtype)
    res = pl.pallas_call(
        body, name=name,
        out_shape=[pltpu.SemaphoreType.DMA((n_sems,)), pltpu.SemaphoreType.DMA((n_sems,))] + [hbm(a) for a in srcs + lands]
        + [jax.ShapeDtypeStruct((8, LANE), F32)],
        in_specs=[HBM] * (2 * n) + [pl.BlockSpec(memory_space=pl.ANY)] * len(extra),
        out_specs=[SEM, SEM] + [HBM] * (2 * n) + [pl.BlockSpec(memory_space=pltpu.VMEM)],
        input_output_aliases={i: 2 + i for i in range(2 * n)},
        compiler_params=pltpu.CompilerParams(has_side_effects=DATAFLOW),
    )(*[pltpu.with_memory_space_constraint(a, pltpu.HBM) for a in srcs + lands], *extra)
    return res[:-1], res[-1]


def _exchange_wait(srcs, to_first, started, after, name):
    n = len(srcs)
    send_sems, recv_sems, thru = started[0], started[1], started[2:]

    def body(*refs):
        src_refs, land_refs, send_ref, recv_ref = refs[:n], refs[n:2 * n], refs[2 * n], refs[2 * n + 1]
        sends, recvs = _exchange_copies(srcs, to_first, src_refs, land_refs, send_ref, recv_ref)
        for pred, cp in sends:
            _when(pred, cp.wait_send)
        for pred, cp in recvs:
            _when(pred, cp.wait_recv)

    res = pl.pallas_call(
        body, name=name, out_shape=[pltpu.HBM(a.shape, a.dtype) for a in thru],
        in_specs=[HBM] * (2 * n) + [SEM, SEM, pl.BlockSpec(memory_space=pl.ANY)], out_specs=[HBM] * (2 * n),
        input_output_aliases={i: i for i in range(2 * n)},
        compiler_params=pltpu.CompilerParams(has_side_effects=DATAFLOW),
    )(*thru, send_sems, recv_sems, after)
    return res[:n], res[n:]


ROW_TILE = 256
COL_TILE = 256


def _block_tiling(a, b):
    if a <= ROW_TILE or a % ROW_TILE == 0:
        ta = min(a, ROW_TILE)
        return a // ta, (ta, b), lambda i: (i, 0)
    return b // COL_TILE, (a, COL_TILE), lambda i: (0, i)


def _sum_share_small(lands, owns, name, after=None):
    n = len(lands)
    extra = [] if after is None else [after]

    def body(*refs):
        p_refs, own_refs = refs[:n], refs[n:2 * n]
        o_refs, mine, theirs = (refs[j * n + len(extra):(j + 1) * n + len(extra)] for j in (2, 3, 4))
        send_sems, recv_sems = refs[5 * n + len(extra):]
        me = 2 * lax.axis_index("x") + lax.axis_index("y")
        sibling = (lax.axis_index("x"), lax.axis_index("y"), 1 - lax.axis_index("c"))
        copies = [_remote(mine[k], theirs[k], send_sems, recv_sems, k, sibling) for k in range(n)]
        for p_ref, own_ref, mine_ref, copy in zip(p_refs, own_refs, mine, copies):
            own = (own_ref[me] if len(own_ref.shape) == 3 else own_ref[...]).astype(F32)
            slot = lambda t: jnp.where(me == t, own, p_ref[t].astype(F32))
            mine_ref[...] = ((slot(0) + slot(1)) + slot(2)) + slot(3)
            copy.start()
        for o_ref, mine_ref, theirs_ref, copy in zip(o_refs, mine, theirs, copies):
            copy.wait()
            o_ref[...] = mine_ref[...] + theirs_ref[...]

    vmem = pl.BlockSpec(memory_space=pltpu.VMEM)
    kept = [pltpu.VMEM(land.shape[1:], F32) for land in lands]
    sems = pltpu.SemaphoreType.DMA((n,))
    return pl.pallas_call(
        body, name=name, in_specs=[vmem] * (2 * n) + [pl.BlockSpec(memory_space=pl.ANY)] * len(extra), out_specs=[vmem] * n,
        out_shape=[jax.ShapeDtypeStruct(land.shape[1:], F32) for land in lands], scratch_shapes=kept + kept + [sems, sems],
        compiler_params=_params(()),
    )(*lands, *owns, *extra)


def _adamw_small(grads, states, name):
    n = len(grads)

    def body(*refs):
        ins, outs = refs[:4 * n], refs[4 * n:]
        for k in range(n):
            g_ref, w_ref, m_ref, v_ref = ins[4 * k:4 * k + 4]
            g = g_ref[...]
            outs[4 * k][...] = g
            outs[4 * k + 1][...], outs[4 * k + 2][...], outs[4 * k + 3][...] = _adamw_math(g, w_ref[...], m_ref[...], v_ref[...])

    args = [t for k in range(n) for t in (grads[k], *states[k])]
    res = pl.pallas_call(body, name=name, out_shape=[jax.ShapeDtypeStruct(grads[k].shape, F32) for k in range(n) for _ in range(4)],
                         compiler_params=_params(()))(*args)
    return [tuple(res[4 * k:4 * k + 4]) for k in range(n)]


class _LaterWeights:
    MID = ("w_branch_a", "w_branch_b", "w_out")

    def __init__(self, blocks, after):
        self.qkv_blocks = [_pad_wuq(blocks["w_uq"]), *_pad_wukv(blocks["w_ukv"])]
        self.mid_blocks = [blocks[n] for n in self.MID]
        self.qkv_started, t1 = _exchange_start(self.qkv_blocks, (), "weights_qkv_start", after)
        self.mid_started, t2 = _exchange_start(self.mid_blocks, (), "weights_mid_start", after)
        self.tokens = [t1, t2]

    @staticmethod
    def _whole(blocks, joined, started, after, name):
        _, landed = _exchange_wait(blocks, (), started, after, name)
        me = 2 * lax.axis_index("x") + lax.axis_index("y")
        out = []
        for block, land, join in zip(blocks, landed, joined):
            w = lax.dynamic_update_index_in_dim(land, block, me, 0)
            out.append(w.reshape(N_CHIPS * block.shape[0], block.shape[1]) if join else w)
        return out

    def qkv(self, after):
        return self._whole(self.qkv_blocks, (True, False, False), self.qkv_started, after, "weights_qkv_wait")

    def mid(self, after):
        return self._whole(self.mid_blocks, (False, False, True), self.mid_started, after, "weights_mid_wait")


class _GradExchange:
    EARLY = ("w_in", "w_branch_a", "w_branch_b", "w_out")
    LATE = ("w_uq", "w_ukv")

    def __init__(self, state):
        self.state = state
        self.outs = {}

    def start_early(self, g):
        self.early = [(g[n] if g[n].ndim == 3 else _split_by_chip(n, g[n])).astype(BF16) for n in self.EARLY]
        self.early_started, token = _exchange_start(self.early, (), "grads_early_start")
        return token

    def start_late(self, g):
        self.early, self.early_landed = _exchange_wait(self.early, (), self.early_started, g["w_uq"], "grads_early_wait")
        self.late = [_split_by_chip("w_uq", g["w_uq"]), g["w_ukv"], g["w_in_qkv"]]
        self.late_started, token = _exchange_start(self.late, (2,), "grads_late_start")
        names = self.EARLY[1:]
        grads = _sum_share_small(self.early_landed[1:], self.early[1:], "sum_early", after=token)
        for n, out in zip(names, _adamw_small(grads, [self.state[n] for n in names], "adamw_early")):
            self.outs[n] = out
        return self.outs[names[-1]][0]

    def finish(self, after):
        late, late_landed = _exchange_wait(self.late, (2,), self.late_started, after, "grads_late_wait")
        grad = _sum_exchange(self.early_landed[0], self.early[0], late_landed[2], late[2], "sum_w_in")
        self.outs["w_in"] = _adamw(grad, *self.state["w_in"], "adamw_w_in")
        return list(late[:2]), list(late_landed[:2])


def _adamw_math(g, w, m, v):
    nm = ADAM_B1 * m + (1.0 - ADAM_B1) * g
    nv = ADAM_B2 * v + (1.0 - ADAM_B2) * (g * g)
    m_hat = nm / (1.0 - ADAM_B1 ** ADAM_STEP)
    v_hat = nv / (1.0 - ADAM_B2 ** ADAM_STEP)
    return -ADAM_LR * (m_hat / (jnp.sqrt(v_hat) + ADAM_EPS) + ADAM_WD * w), nm, nv


def _sum_exchange(land, own, first_land, first_own, name):
    _, a, b = land.shape
    steps, tile, at = _block_tiling(a, b)
    assert tile[0] == a, "the extra rows need whole columns in a step"
    r = first_own.shape[0]

    def body(me_ref, p_ref, own_ref, fp_ref, fo_ref, g_ref, mine_s, theirs_s, send_sems, recv_sems):
        pass_, i = pl.program_id(0), pl.program_id(1)
        me, c = me_ref[0], lax.axis_index("c")
        sibling = (lax.axis_index("x"), lax.axis_index("y"), 1 - c)
        copy = _remote(mine_s.at[i], theirs_s.at[i], send_sems, recv_sems, i, sibling)

        @pl.when(pass_ == 0)
        def _():
            own = own_ref[...].astype(F32)
            slot = lambda t: jnp.where(me == t, own, p_ref[t].astype(F32))
            mine_s[i] = ((slot(0) + slot(1)) + slot(2)) + slot(3)

            @pl.when(me == 0)
            def _():
                f = lambda t: fp_ref[t].astype(F32)
                rows = pl.ds(pl.multiple_of(c * r, 8), r)
                mine_s[i, rows, :] += ((fo_ref[...].astype(F32) + f(1)) + f(2)) + f(3)

            copy.start()

        @pl.when(pass_ == 1)
        def _():
            copy.wait()
            g_ref[...] = mine_s[i] + theirs_s[i]

    first = lambda p, i: i * (1 - p) + (steps - 1) * p
    in_specs = [pl.BlockSpec((N_CHIPS,) + tile, lambda p, i, me: (0,) + at(first(p, i))),
                pl.BlockSpec((None,) + tile, lambda p, i, me: (me[0],) + at(first(p, i))),
                pl.BlockSpec((N_CHIPS, r, tile[1]), lambda p, i, me: (0,) + at(first(p, i))),
                pl.BlockSpec((r, tile[1]), lambda p, i, me: at(first(p, i)))]
    me = jnp.reshape(2 * lax.axis_index("x") + lax.axis_index("y"), (1,)).astype(jnp.int32)
    kept = pltpu.VMEM((steps,) + tile, F32)
    sems = pltpu.SemaphoreType.DMA((steps,))
    return pl.pallas_call(
        body, name=name,
        grid_spec=pltpu.PrefetchScalarGridSpec(num_scalar_prefetch=1, grid=(2, steps), in_specs=in_specs,
                                               out_specs=pl.BlockSpec(tile, lambda p, i, me: at(i * p)),
                                               scratch_shapes=[kept, kept, sems, sems]),
        out_shape=jax.ShapeDtypeStruct((a, b), F32),
        compiler_params=_params(("arbitrary", "arbitrary")),
    )(me, land, own, first_land, first_own)


def _adamw(g, w, m, v, name):
    a, b = g.shape
    steps, tile, at = _block_tiling(a, b)

    def body(g_ref, w_ref, m_ref, v_ref, go_ref, d_ref, nm_ref, nv_ref):
        g = g_ref[...]
        go_ref[...] = g
        d_ref[...], nm_ref[...], nv_ref[...] = _adamw_math(g, w_ref[...], m_ref[...], v_ref[...])

    spec = pl.BlockSpec(tile, at)
    sds = jax.ShapeDtypeStruct((a, b), F32)
    return pl.pallas_call(
        body, name=name, grid=(steps,), in_specs=[spec] * 4, out_specs=[spec] * 4, out_shape=[sds] * 4,
        compiler_params=_params(("parallel",)),
    )(g, w, m, v)


LOSS_AT = (2, 1024)


def _vec_pack(vg, loss):
    names = [n for n, _, _ in VEC_ROWS]

    def body(*refs):
        o_ref = refs[-1]
        lb_ref, loss_ref = refs[len(names)], refs[len(names) + 1]
        o_ref[...] = jnp.zeros_like(o_ref)
        o_ref[LOSS_AT[0]:LOSS_AT[0] + 1, LOSS_AT[1]:LOSS_AT[1] + LANE] = jnp.broadcast_to(loss_ref[...], (1, LANE))
        for (name, row, size), ref in zip(VEC_ROWS, refs):
            if name == "g_hgrn":
                r = lax.broadcasted_iota(jnp.int32, (NH * V_DIM, LANE), 0)
                c = lax.broadcasted_iota(jnp.int32, (NH * V_DIM, LANE), 1)
                fold = ((r % V_DIM) == c).astype(F32)
                o_ref[row:row + 1, 0:LANE] = jnp.dot(ref[...], fold, precision=HIGHEST, preferred_element_type=F32)
            else:
                o_ref[row:row + 1, 0:size] = ref[...]
        o_ref[VEC_LB_ROW:VEC_LB_ROW + 2, 0:512] = lb_ref[...]

    return pl.pallas_call(body, name="vec_pack", out_shape=jax.ShapeDtypeStruct(VEC_SHAPE, F32))(
        *[vg[n] for n in names], vg["lb_logits"], loss)


def _adamw_vec(block, w, m, v):
    names = [n for n, _, _ in VEC_ROWS] + ["lb_logits"]
    k = len(names)

    def body(g_ref, *refs):
        ins, outs = refs[:3 * k], refs[3 * k:]
        outs[-1][...] = g_ref[LOSS_AT[0]:LOSS_AT[0] + 1, LOSS_AT[1]:LOSS_AT[1] + LANE]
        for i, name in enumerate(names):
            if name == "lb_logits":
                rows, cols = slice(VEC_LB_ROW, VEC_LB_ROW + 2), slice(0, 512)
            else:
                _, row, size = VEC_ROWS[i]
                rows, cols = slice(row, row + 1), slice(0, size)
            g = g_ref[rows, cols]
            d, nm, nv = _adamw_math(g, ins[i][...], ins[k + i][...], ins[2 * k + i][...])
            for o_ref, val in zip(outs[4 * i:4 * i + 4], (g, d, nm, nv)):
                o_ref[...] = val

    shapes = [jax.ShapeDtypeStruct(w[n].shape, F32) for n in names for _ in range(4)] + [jax.ShapeDtypeStruct((1, LANE), F32)]
    res = pl.pallas_call(body, name="adamw_vec", out_shape=shapes)(
        block, *[w[n] for n in names], *[m[n] for n in names], *[v[n] for n in names])
    return [{n: res[4 * i + j] for i, n in enumerate(names)} for j in range(4)], res[-1]


WEIGHTS = ("g_pre", "w_in", "b_gate", "g_q", "w_uq", "g_kv", "w_ukv", "lb_logits", "g_hgrn", "w_branch_a", "w_branch_b", "w_out", "g_post")


def kernel(x, g_pre, w_in, b_gate, g_q, w_uq, g_kv, w_ukv, lb_logits, g_hgrn, w_branch_a, w_branch_b, w_out, g_post, loss_target, m_g_pre, m_w_in, m_b_gate, m_g_q, m_w_uq, m_g_kv, m_w_ukv, m_lb_logits, m_g_hgrn, m_w_branch_a, m_w_branch_b, m_w_out, m_g_post, v_g_pre, v_w_in, v_b_gate, v_g_q, v_w_uq, v_g_kv, v_w_ukv, v_lb_logits, v_g_hgrn, v_w_branch_a, v_w_branch_b, v_w_out, v_g_post):
    w = dict(g_pre=g_pre, w_in=w_in, b_gate=b_gate, g_q=g_q, w_uq=w_uq, g_kv=g_kv, w_ukv=w_ukv, lb_logits=lb_logits, g_hgrn=g_hgrn,
             w_branch_a=w_branch_a, w_branch_b=w_branch_b, w_out=w_out, g_post=g_post)
    m = dict(g_pre=m_g_pre, w_in=m_w_in, b_gate=m_b_gate, g_q=m_g_q, w_uq=m_w_uq, g_kv=m_g_kv, w_ukv=m_w_ukv, lb_logits=m_lb_logits,
             g_hgrn=m_g_hgrn, w_branch_a=m_w_branch_a, w_branch_b=m_w_branch_b, w_out=m_w_out, g_post=m_g_post)
    v = dict(g_pre=v_g_pre, w_in=v_w_in, b_gate=v_b_gate, g_q=v_g_q, w_uq=v_w_uq, g_kv=v_g_kv, w_ukv=v_w_ukv, lb_logits=v_lb_logits,
             g_hgrn=v_g_hgrn, w_branch_a=v_w_branch_a, w_branch_b=v_w_branch_b, w_out=v_w_out, g_post=v_g_post)
    blocks = {n: _to_block(n, w[n]).astype(BF16) for n in BIG}
    w_in_all = _gather_w_in(blocks["w_in"])
    weights = _LaterWeights(blocks, w_in_all)
    state = {n: [_to_block(n, t[n]) for t in (w, m, v)] for n in BIG}
    exchange = _GradExchange(state)
    loss, grad_x, _, vec_grads = _local_step(
        x[0], loss_target[0], g_pre, _join_chips("w_in", w_in_all), b_gate, g_q, g_kv, lb_logits, g_hgrn, g_post, weights, exchange)
    vec = _vec_pack(vec_grads, loss)
    vec_started, token = _exchange_start([vec], (), "vec_start")
    late_sent, late_landed = exchange.finish(token)
    (vec,), (vec_landed,) = _exchange_wait([vec], (), vec_started, exchange.outs["w_in"][0], "vec_wait")
    grads = _sum_share_small(late_landed + [vec_landed], late_sent + [vec], "sum_late")
    done = dict(exchange.outs)
    done.update(zip(exchange.LATE, _adamw_small(grads[:-1], [state[n] for n in exchange.LATE], "adamw_late")))
    outs = [{}, {}, {}, {}]
    for n in BIG:
        for o, val in zip(outs, done[n]):
            o[n] = _from_block(n, val)
    vec_outs, total = _adamw_vec(grads[-1], w, m, v)
    for o, vals in zip(outs, vec_outs):
        o.update(vals)
    return (total[0, 0], grad_x[None], *[o[n] for o in outs for n in WEIGHTS])
```

```python
import math

import numpy as np
import jax
import jax.numpy as jnp
from jax import lax
from jax.experimental import pallas as pl
from jax.experimental.pallas import tpu as pltpu

F32 = jnp.float32
BF16 = jnp.bfloat16
HIGHEST = lax.Precision.HIGHEST

D = 1024
NH = 8
QK_NOPE, QK_ROPE, V_DIM = 64, 32, 64
Q_LORA, KV_LORA = 768, 256
CHUNK = 64
HG_BLOCK = 32
EPS = 1e-6
LANE = 128
P_MERGE, P_GA, P_HQ, P_HF, P_HI, P_GB, P_CQ, P_CKV, P_KPE = 0, 2048, 2560, 3072, 3584, 4096, 4608, 5376, 5632
D_P = 5760
O_CQ, O_CKV, O_KPE, O_GA, O_HQ, O_HF, O_HI, O_GB, O_MERGE = 0, 768, 1024, 1056, 1568, 2080, 2592, 3104, 3616

TM = 512
TM_MID = 256
TQ = 1024
ONES_LANE = (LANE - 1, 0)
TH = 256
HG_PAIRS = 4
VMEM_LIMIT = 56 * 1024 * 1024

ADAM_LR, ADAM_B1, ADAM_B2, ADAM_EPS, ADAM_WD, ADAM_STEP = 0.001, 0.9, 0.999, 1e-08, 0.01, 10

NT_DIMS = (((1,), (1,)), ((), ()))
TN_DIMS = (((0,), (0,)), ((), ()))


def _params(sem):
    return pltpu.CompilerParams(dimension_semantics=sem, vmem_limit_bytes=VMEM_LIMIT)


def _mm(a, b):
    return jnp.dot(a, b, preferred_element_type=F32)


def _mm_nt(a, b):
    return lax.dot_general(a, b, NT_DIMS, preferred_element_type=F32)


def _mm_tn(a, b):
    return lax.dot_general(a, b, TN_DIMS, preferred_element_type=F32)


def _sigmoid(z):
    return jax.nn.sigmoid(z)


def _rope(v, c, s1, s2):
    return v * c + pltpu.roll(v, 112, 1) * s1 + pltpu.roll(v, 16, 1) * s2


def _rope_t(dy, c, s1, s2):
    return dy * c + pltpu.roll(dy * s1, 16, 1) + pltpu.roll(dy * s2, 112, 1)


def _rope_tables(s):
    f32 = np.float32
    inv = f32(10000.0) ** (-np.arange(0, QK_ROPE, 2, dtype=f32) / f32(QK_ROPE))
    ang = np.arange(s, dtype=f32)[:, None] * inv[None, :]
    cos, sin = np.cos(ang).astype(f32), np.sin(ang).astype(f32)
    z64, z32, o64, o32 = np.zeros((s, 64), f32), np.zeros((s, 32), f32), np.ones((s, 64), f32), np.ones((s, 32), f32)
    z16 = np.zeros((s, 16), f32)
    c = np.concatenate([o64, cos, cos, o32], axis=1)
    s1 = np.concatenate([z64, -sin, z16, z32], axis=1)
    s2 = np.concatenate([z64, z16, sin, z32], axis=1)
    return jnp.asarray(c), jnp.asarray(s1), jnp.asarray(s2)


W_IN_RUNS = ((O_MERGE, 2048, P_MERGE), (O_GA, O_MERGE - O_GA, P_GA), (O_CQ, O_KPE - O_CQ, P_CQ))


def _kpe_block(w_in_t):
    z = lambda n: jnp.zeros((n, w_in_t.shape[1]), w_in_t.dtype)
    return jnp.concatenate([z(64), w_in_t[O_KPE:O_KPE + QK_ROPE], z(32)], axis=0)


def _front_fwd(x, g_pre, w_in_t, w_kpe, tokens=()):
    s = x.shape[0]
    tokens = list(tokens)

    def body(x_ref, g_ref, w_ref, k_ref, *refs):
        o_ref, h_ref = refs[len(tokens):]
        xv = x_ref[...]
        r = lax.rsqrt(jnp.mean(xv * xv, axis=-1, keepdims=True) + EPS)
        h = ((xv * r) * g_ref[...]).astype(BF16)
        h_ref[...] = h
        for row, rows, col in W_IN_RUNS:
            o_ref[:, col:col + rows] = _mm_nt(h, w_ref[row:row + rows, :])
        o_ref[:, P_KPE:P_KPE + LANE] = _mm_nt(h, k_ref[...])

    full = lambda a: pl.BlockSpec(a.shape, lambda i: (0,) * a.ndim)
    return pl.pallas_call(
        body, name="front_fwd", grid=(s // TM,),
        in_specs=[pl.BlockSpec((TM, D), lambda i: (i, 0)), pl.BlockSpec((1, D), lambda i: (0, 0)), full(w_in_t), full(w_kpe)]
        + [pl.BlockSpec((8, LANE), lambda i: (0, 0))] * len(tokens),
        out_specs=[pl.BlockSpec((TM, D_P), lambda i: (i, 0)), pl.BlockSpec((TM, D), lambda i: (i, 0))],
        out_shape=[jax.ShapeDtypeStruct((s, D_P), F32), jax.ShapeDtypeStruct((s, D), BF16)],
        compiler_params=_params(("parallel",)),
    )(x, g_pre, w_in_t, w_kpe, *tokens)


def _norm_rows(v, g):
    r = lax.rsqrt(jnp.mean(v * v, axis=-1, keepdims=True) + EPS)
    return (v * r) * g, r


def _qkv_fwd(proj, g_q, g_kv, w_uq_p, w_k_p, w_v_p, rc, rs1, rs2):
    s = proj.shape[0]

    def body(cq_ref, ckv_ref, kpe_ref, gq_ref, gkv_ref, wq_ref, wk_ref, wv_ref, c_ref, s1_ref, s2_ref, q_ref, k_ref, v_ref):
        c, s1, s2 = c_ref[...], s1_ref[...], s2_ref[...]
        cqn, _ = _norm_rows(cq_ref[...], gq_ref[...])
        ckvn, _ = _norm_rows(ckv_ref[...], gkv_ref[...])
        ckvn = ckvn.astype(BF16)
        qf = _mm(cqn.astype(BF16), wq_ref[...])
        kf = jnp.concatenate([_mm(ckvn, wk_ref[t]) for t in range(N_CHIPS)], axis=1)
        vf = jnp.concatenate([_mm(ckvn, wv_ref[t]) for t in range(N_CHIPS)], axis=1)
        kpe = _rope(kpe_ref[...], c, s1, s2)
        lane = lax.broadcasted_iota(jnp.int32, (TM, LANE), 1)
        for h in range(NH):
            blk = slice(h * LANE, (h + 1) * LANE)
            q_ref[h] = _rope(qf[:, blk], c, s1, s2).astype(BF16)
            k_ref[h] = (kf[:, blk] + kpe).astype(BF16)
            v_ref[h] = jnp.where(lane == ONES_LANE[h % 2], 1.0, vf[:, blk]).astype(BF16)

    row = lambda w, j: pl.BlockSpec((TM, w), lambda i: (i, j))
    full = lambda a: pl.BlockSpec(a.shape, lambda i: (0,) * a.ndim)
    hs = jax.ShapeDtypeStruct((NH, s, LANE), BF16)
    return pl.pallas_call(
        body, name="qkv_fwd", grid=(s // TM,),
        in_specs=[row(Q_LORA, P_CQ // Q_LORA), row(KV_LORA, P_CKV // KV_LORA), row(LANE, P_KPE // LANE),
                  full(g_q), full(g_kv), full(w_uq_p), full(w_k_p), full(w_v_p), row(LANE, 0), row(LANE, 0), row(LANE, 0)],
        out_specs=[pl.BlockSpec((NH, TM, LANE), lambda i: (0, i, 0))] * 3,
        out_shape=[hs, hs, hs],
        compiler_params=_params(("parallel",)),
    )(proj, proj, proj, g_q, g_kv, w_uq_p, w_k_p, w_v_p, rc, rs1, rs2)


LOG2E = 1.4426950408889634
QK_SCALE2 = LOG2E / math.sqrt(QK_NOPE + QK_ROPE)


HQ = TQ // 2


def _diag_visible(n):
    row = lax.broadcasted_iota(jnp.int32, (n, n), 0)
    col = lax.broadcasted_iota(jnp.int32, (n, n), 1)
    return (col // CHUNK) <= (row // CHUNK)


def _attn_fwd(q, k, vv):
    s = q.shape[1]

    def body(q_ref, k_ref, v_ref, o_ref, lse_ref):
        i = pl.program_id(1)
        qs = (q_ref[0], q_ref[1])

        def tiles(t, carry, diag):
            rows = pl.ds(pl.multiple_of(t * TQ, TQ), TQ)
            sc = [_mm_nt(qs[hh], k_ref[hh, rows, :]) for hh in range(2)]
            if diag:
                sc = [jnp.where(_diag_visible(TQ), s_, -jnp.inf) for s_ in sc]
            m_new = [jnp.maximum(carry[hh][0], jnp.max(sc[hh], axis=-1, keepdims=True)) for hh in range(2)]
            alpha = [jnp.exp2((carry[hh][0] - m_new[hh]) * QK_SCALE2) for hh in range(2)]
            p = [jnp.exp2((sc[hh] - m_new[hh]) * QK_SCALE2).astype(BF16) for hh in range(2)]
            acc = [alpha[hh] * carry[hh][1] + _mm(p[hh], v_ref[hh, rows, :]) for hh in range(2)]
            return (m_new[0], acc[0]), (m_new[1], acc[1])

        init = (jnp.full((TQ, 1), -jnp.inf, F32), jnp.zeros((TQ, LANE), F32))
        carry = lax.fori_loop(0, i, lambda t, c: tiles(t, c, False), (init, init))
        carry = tiles(i, carry, True)
        lane = lax.broadcasted_iota(jnp.int32, (TQ, LANE), 1)
        out = jnp.zeros((TQ, LANE), F32)
        for hh in range(2):
            m, acc = carry[hh]
            l = jnp.sum(jnp.where(lane == ONES_LANE[hh], acc, 0.0), axis=-1, keepdims=True)
            out = out + jnp.where((lane < V_DIM) == (hh == 0), acc, 0.0) / l
            lse_ref[hh] = jnp.broadcast_to(m * QK_SCALE2 + jnp.log(l) * LOG2E, (TQ, LANE))
        o_ref[...] = out

    return pl.pallas_call(
        body, name="attn_fwd", grid=(NH // 2, s // TQ),
        in_specs=[pl.BlockSpec((2, TQ, LANE), lambda p, i: (p, i, 0)), pl.BlockSpec((2, s, LANE), lambda p, i: (p, 0, 0)),
                  pl.BlockSpec((2, s, LANE), lambda p, i: (p, 0, 0))],
        out_specs=[pl.BlockSpec((TQ, LANE), lambda p, i: (i, p)), pl.BlockSpec((2, TQ, LANE), lambda p, i: (p, i, 0))],
        out_shape=[jax.ShapeDtypeStruct((s, NH * V_DIM), F32), jax.ShapeDtypeStruct((NH, s, LANE), F32)],
        compiler_params=_params(("parallel", "parallel")),
    )(q, k, vv)


def _lower_bound(lbl):
    a0, a1 = lbl[0:1, :], lbl[1:2, :]
    mx = jnp.maximum(a0, a1)
    e0, e1 = jnp.exp(a0 - mx), jnp.exp(a1 - mx)
    return e0 / (e0 + e1)


def _chunk_cumsum(v, reverse=False):
    pos = lax.broadcasted_iota(jnp.int32, v.shape, 0) % HG_BLOCK
    s = 1
    while s < HG_BLOCK:
        if reverse:
            v = v + jnp.where(pos < HG_BLOCK - s, pltpu.roll(v, TH - s, 0), 0.0)
        else:
            v = v + jnp.where(pos >= s, pltpu.roll(v, s, 0), 0.0)
        s *= 2
    return v


def _hgrn_gates(hq, hf, lb):
    sig = _sigmoid(hf)
    f = lb + (1.0 - lb) * sig
    g = jnp.log(f)
    kk = 1.0 - f
    r = lax.broadcasted_iota(jnp.int32, (TH, TH), 0)
    c = lax.broadcasted_iota(jnp.int32, (TH, TH), 1)
    tri = ((r // HG_BLOCK) == (c // HG_BLOCK)) & (r >= c)
    cum = _chunk_cumsum(g)
    nch = TH // HG_BLOCK
    total = _chunks(cum)[:, HG_BLOCK - 1:HG_BLOCK, :]
    lastb = jnp.broadcast_to(total, (nch, HG_BLOCK, hf.shape[-1])).reshape(hf.shape)
    e, ei, ee = jnp.exp(cum), jnp.exp(-cum), jnp.exp(lastb - cum)
    return dict(sig=sig, f=f, kk=kk, tri=tri, cum=cum, total=total, decay=jnp.exp(total), e=e, ei=ei, ee=ee,
                qd=hq * e, ki=kk * ei, ke=kk * ee)


def _chunks(v):
    return v.reshape(TH // HG_BLOCK, HG_BLOCK, v.shape[-1])


def _bmm_nt(a, b):
    return lax.dot_general(a, b, (((2,), (2,)), ((0,), (0,))), preferred_element_type=F32)


def _bmm_nn(a, b):
    return lax.dot_general(a, b, (((2,), (1,)), ((0,), (0,))), preferred_element_type=F32)


def _bmm_tn(a, b):
    return lax.dot_general(a, b, (((1,), (1,)), ((0,), (0,))), preferred_element_type=F32)


def _pair_masks():
    lane = lax.broadcasted_iota(jnp.int32, (TH, LANE), 1)
    kr = lax.broadcasted_iota(jnp.int32, (LANE, LANE), 0)
    kc = lax.broadcasted_iota(jnp.int32, (LANE, LANE), 1)
    return lane < 64, (kr // 64) == (kc // 64)


def _hgrn_fwd(proj, lbl):
    s = proj.shape[0]
    nch = TH // HG_BLOCK

    def body(hq_ref, hf_ref, hi_ref, lbl_ref, o_ref, st_ref, st):
        @pl.when(pl.program_id(1) == 0)
        def _():
            st[...] = jnp.zeros_like(st)

        m0, bd = _pair_masks()
        gt = _hgrn_gates(hq_ref[...], hf_ref[...], _lower_bound(lbl_ref[...]))
        v_b, qd, qd_b = hi_ref[...].astype(BF16), gt["qd"], gt["qd"].astype(BF16)
        ki_b, ke_b = gt["ki"].astype(BF16), gt["ke"].astype(BF16)
        pairs = [slice(u * LANE, (u + 1) * LANE) for u in range(HG_PAIRS)]
        heads = [(lanes, m0 if hh == 0 else jnp.logical_not(m0)) for lanes in pairs for hh in range(2)]
        a_b = [jnp.where(gt["tri"], _mm_nt(jnp.where(mh, qd[:, lanes], 0.0).astype(BF16), ki_b[:, lanes]), 0.0).astype(BF16)
               for lanes, mh in heads]
        intra = [jnp.where(m0, _mm(a_b[2 * u], v_b[:, lanes]), _mm(a_b[2 * u + 1], v_b[:, lanes])) for u, lanes in enumerate(pairs)]
        upd = [_bmm_tn(_chunks(v_b[:, lanes]), _chunks(ke_b[:, lanes])) for lanes in pairs]
        entering = []
        for u, lanes in enumerate(pairs):
            cur, states = st[u], []
            for n in range(nch):
                states.append(cur)
                cur = gt["decay"][n][:, lanes] * cur + jnp.where(bd, upd[u][n], 0.0)
            st[u] = cur
            entering.append(jnp.stack(states))
            st_ref[u] = entering[u]
        for u, lanes in enumerate(pairs):
            o_ref[:, lanes] = intra[u] + _bmm_nt(_chunks(qd_b[:, lanes]), entering[u].astype(BF16)).reshape(TH, LANE)

    wide = HG_PAIRS * LANE
    col = lambda base: pl.BlockSpec((TH, wide), lambda p, i: (i, base // wide + p))
    return pl.pallas_call(
        body, name="hgrn_fwd", grid=(NH // 2 // HG_PAIRS, s // TH),
        in_specs=[col(P_HQ), col(P_HF), col(P_HI), pl.BlockSpec((2, wide), lambda p, i: (0, p))],
        out_specs=[pl.BlockSpec((TH, wide), lambda p, i: (i, p)),
                   pl.BlockSpec((HG_PAIRS, nch, LANE, LANE), lambda p, i: (p, i, 0, 0))],
        out_shape=[jax.ShapeDtypeStruct((s, 512), F32), jax.ShapeDtypeStruct((NH // 2, s // HG_BLOCK, LANE, LANE), F32)],
        scratch_shapes=[pltpu.VMEM((HG_PAIRS, LANE, LANE), F32)],
        compiler_params=_params(("parallel", "arbitrary")),
    )(proj, proj, proj, lbl)


def _group_sum(v):
    low = lax.broadcasted_iota(jnp.int32, (v.shape[0], LANE), 1) < V_DIM
    blocks = []
    for b in range(v.shape[1] // LANE):
        blk = v[:, b * LANE:(b + 1) * LANE]
        s_low = jnp.sum(jnp.where(low, blk, 0.0), axis=-1, keepdims=True)
        s_high = jnp.sum(jnp.where(low, 0.0, blk), axis=-1, keepdims=True)
        blocks.append(jnp.where(low, s_low, s_high))
    return jnp.concatenate(blocks, axis=1)


def _dsilu(z, sg):
    return sg * (1.0 + z * (1.0 - sg))


def _mid(proj, attn, o_raw, x, tgt, g_hg, b_gate, g_post, wa, wb, w_out):
    s = x.shape[0]

    def body(attn_ref, ga_ref, o_ref, gb_ref, mg_ref, x_ref, t_ref, ghg_ref, bg_ref, gp_ref, wa_ref, wb_ref, wo_ref,
             loss_ref, dout_ref, dattn_ref, dga_ref, dor_ref, dgb_ref, dmg_ref, dwo_out, dwa_out, dwb_out, dgp_ref, dbg_ref, dghg_ref,
             dwo_ref, dwa_ref, dwb_ref):
        @pl.when(pl.program_id(0) == 0)
        def _():
            for rf in (loss_ref, dwo_ref, dwa_ref, dwb_ref, dgp_ref, dbg_ref, dghg_ref):
                rf[...] = jnp.zeros_like(rf)

        attn, za, orw, zb = attn_ref[...], ga_ref[...], o_ref[...], gb_ref[...]
        ghg, gp = ghg_ref[...], gp_ref[...]
        sga, sgb = _sigmoid(za), _sigmoid(zb)
        sa, sb = za * sga, zb * sgb
        ga = attn * sa
        rh = lax.rsqrt(_group_sum(orw * orw) * (1.0 / V_DIM) + EPS)
        on = (orw * rh) * ghg
        gb = on * sb
        ga_b, gb_b = ga.astype(BF16), gb.astype(BF16)
        blocks = [slice(t * (D // N_CHIPS), (t + 1) * (D // N_CHIPS)) for t in range(N_CHIPS)]
        ya = jnp.concatenate([_mm(ga_b, wa_ref[t]) for t in range(N_CHIPS)], axis=1)
        yb = jnp.concatenate([_mm(gb_b, wb_ref[t]) for t in range(N_CHIPS)], axis=1)
        gates = _sigmoid(mg_ref[...] + bg_ref[...])
        g0, g1 = gates[:, :D], gates[:, D:]
        m_b = (g0 * ya + g1 * yb).astype(BF16)
        y = _mm(m_b, wo_ref[...])
        ry = lax.rsqrt(jnp.mean(y * y, axis=-1, keepdims=True) + EPS)
        out = x_ref[...] + (y * ry) * gp
        err = out - t_ref[...]
        loss_ref[...] += 0.5 * jnp.sum(jnp.mean(err * err, axis=-1, keepdims=True), axis=0, keepdims=True)
        dout = err * (1.0 / D)
        dout_ref[...] = dout
        dgp_ref[...] += jnp.sum(dout * (y * ry), axis=0, keepdims=True)
        dgy = dout * gp
        dy = ry * dgy - y * (ry * ry * ry) * jnp.mean(y * dgy, axis=-1, keepdims=True)
        dy_b = dy.astype(BF16)
        dm = _mm_nt(dy_b, wo_ref[...])
        dya_b, dyb_b = (dm * g0).astype(BF16), (dm * g1).astype(BF16)
        dga = sum(_mm_nt(dya_b[:, cols], wa_ref[t]) for t, cols in enumerate(blocks))
        dgb = sum(_mm_nt(dyb_b[:, cols], wb_ref[t]) for t, cols in enumerate(blocks))
        dwo_ref[...] += _mm_tn(m_b, dy_b)
        for t, cols in enumerate(blocks):
            dwa_ref[t] += _mm_tn(ga_b, dya_b[:, cols])
            dwb_ref[t] += _mm_tn(gb_b, dyb_b[:, cols])
        dg0, dg1 = dm * ya, dm * yb
        dmg = jnp.concatenate([dg0 * g0 * (1.0 - g0), dg1 * g1 * (1.0 - g1)], axis=1)
        dmg_ref[...] = dmg.astype(BF16)
        dbg_ref[...] += jnp.sum(dmg, axis=0, keepdims=True)
        dattn_ref[...] = dga * sa
        dga_ref[...] = (dga * attn * _dsilu(za, sga)).astype(BF16)
        dgb_ref[...] = (dgb * on * _dsilu(zb, sgb)).astype(BF16)
        don = dgb * sb
        dghg_ref[...] += jnp.sum(don * (orw * rh), axis=0, keepdims=True)
        dgo = don * ghg
        dor_ref[...] = rh * dgo - orw * (rh * rh * rh) * (_group_sum(orw * dgo) * (1.0 / V_DIM))

        @pl.when(pl.program_id(0) == pl.num_programs(0) - 1)
        def _():
            for out, rf in ((dwo_out, dwo_ref), (dwa_out, dwa_ref), (dwb_out, dwb_ref)):
                out[...] = rf[...].astype(BF16)

    row = lambda w, j=0: pl.BlockSpec((TM_MID, w), lambda i: (i, j))
    full = lambda a: pl.BlockSpec(a.shape, lambda i: (0,) * a.ndim)
    acc = lambda shape: pl.BlockSpec(shape, lambda i: (0,) * len(shape))
    slabs = (N_CHIPS, 512, D // N_CHIPS)
    sds = jax.ShapeDtypeStruct
    return pl.pallas_call(
        body, name="mid", grid=(s // TM_MID,),
        in_specs=[row(512), row(512, P_GA // 512), row(512), row(512, P_GB // 512), row(2048, P_MERGE // 2048), row(D), row(D),
                  full(g_hg), full(b_gate), full(g_post), full(wa), full(wb), full(w_out)],
        out_specs=[acc((1, 1)), row(D), row(512), row(512), row(512), row(512), row(2048),
                   acc((D, D)), acc(slabs), acc(slabs), acc((1, D)), acc((1, 2048)), acc((1, 512))],
        out_shape=[sds((1, 1), F32), sds((s, D), F32), sds((s, 512), F32), sds((s, 512), BF16), sds((s, 512), F32), sds((s, 512), BF16),
                   sds((s, 2048), BF16), sds((D, D), BF16), sds(slabs, BF16), sds(slabs, BF16), sds((1, D), F32),
                   sds((1, 2048), F32), sds((1, 512), F32)],
        scratch_shapes=[pltpu.VMEM((D, D), F32), pltpu.VMEM(slabs, F32), pltpu.VMEM(slabs, F32)],
        compiler_params=_params(("arbitrary",)),
    )(attn, proj, o_raw, proj, proj, x, tgt, g_hg, b_gate, g_post, wa, wb, w_out)


def _attn_bwd(q, k, vv, attn, dattn, lse, token):
    s = q.shape[1]
    nt = s // TQ
    scale = 1.0 / math.sqrt(QK_NOPE + QK_ROPE)

    def body(q_ref, k_ref, v_ref, o_ref, do_ref, lse_ref, token_ref, dq_ref, dk_ref, dv_ref, do_s, delta_s):
        j = pl.program_id(1)

        @pl.when(j == 0)
        def _():
            dq_ref[...] = jnp.zeros_like(dq_ref)
            lane = lax.broadcasted_iota(jnp.int32, (TQ, LANE), 1)

            @pl.loop(0, nt)
            def _(i):
                rows = pl.ds(pl.multiple_of(i * TQ, TQ), TQ)
                do, o = do_ref[rows, :], o_ref[rows, :]
                for hh in range(2):
                    doh = jnp.where((lane < 64) if hh == 0 else (lane >= 64), do, 0.0)
                    do_s[hh, rows, :] = doh.astype(BF16)
                    delta_s[hh, rows, :] = jnp.broadcast_to(jnp.sum(doh * o, axis=-1, keepdims=True), (TQ, LANE))

        kjs, vjs = (k_ref[0], k_ref[1]), (v_ref[0], v_ref[1])

        def tile(hh, start, size, kj, vj, diag):
            rows = pl.ds(pl.multiple_of(start, size), size)
            wide = lambda a: jnp.concatenate([a] * (kj.shape[0] // LANE), axis=1)
            qi, do_b = q_ref[hh, rows, :], do_s[hh, rows, :]
            sc, dp = _mm_nt(qi, kj), _mm_nt(do_b, vj)
            p = jnp.exp2(sc * QK_SCALE2 - wide(lse_ref[hh, rows, :]))
            if diag:
                p = jnp.where(_diag_visible(size), p, 0.0)
            ds_b = (p * (dp - wide(delta_s[hh, rows, :]))).astype(BF16)
            dv, dk = _mm_tn(do_b, p.astype(BF16)), _mm_tn(qi, ds_b)
            dq_ref[hh, rows, :] += _mm(ds_b, kj)
            return dk, dv

        def step(i, carry):
            new = [tile(hh, i * TQ, TQ, kjs[hh], vjs[hh], False) for hh in range(2)]
            return tuple((carry[hh][0] + new[hh][0], carry[hh][1] + new[hh][1]) for hh in range(2))

        def diagonal(hh):
            k0, k1, v0, v1 = kjs[hh][:HQ], kjs[hh][HQ:], vjs[hh][:HQ], vjs[hh][HQ:]
            a = tile(hh, j * TQ, HQ, k0, v0, True)
            b = tile(hh, j * TQ + HQ, HQ, k0, v0, False)
            c = tile(hh, j * TQ + HQ, HQ, k1, v1, True)
            return jnp.concatenate([a[0] + b[0], c[0]], axis=1), jnp.concatenate([a[1] + b[1], c[1]], axis=1)

        carry = lax.fori_loop(j + 1, nt, step, (diagonal(0), diagonal(1)))
        for hh in range(2):
            dk_ref[hh] = carry[hh][0].T * scale
            dv_ref[hh] = carry[hh][1].T

        @pl.when(j == nt - 1)
        def _():
            dq_ref[...] = dq_ref[...] * scale

    whole = pl.BlockSpec((2, s, LANE), lambda p, j: (p, 0, 0))
    tile_spec = pl.BlockSpec((2, TQ, LANE), lambda p, j: (p, j, 0))
    cols = pl.BlockSpec((s, LANE), lambda p, j: (0, p))
    hs = jax.ShapeDtypeStruct((NH, s, LANE), F32)
    return pl.pallas_call(
        body, name="attn_bwd", grid=(NH // 2, nt),
        in_specs=[whole, tile_spec, tile_spec, cols, cols, whole, pl.BlockSpec((8, LANE), lambda p, j: (0, 0))],
        out_specs=[whole, tile_spec, tile_spec],
        out_shape=[hs, hs, hs],
        scratch_shapes=[pltpu.VMEM((2, s, LANE), BF16), pltpu.VMEM((2, s, LANE), F32)],
        compiler_params=_params(("parallel", "arbitrary")),
    )(q, k, vv, attn, dattn, lse, token)


def _hgrn_bwd(proj, lbl, states, do_raw):
    s = proj.shape[0]
    nt = s // TH
    nch = TH // HG_BLOCK

    def body(hq_ref, hf_ref, hi_ref, lbl_ref, st_ref, do_ref, dh_ref, dlbl_ref, dst, dlb):
        step = pl.program_id(1)

        @pl.when(step == 0)
        def _():
            dst[...] = jnp.zeros_like(dst)
            dlb[...] = jnp.zeros_like(dlb)

        m0, bd = _pair_masks()
        lb = _lower_bound(lbl_ref[...])
        gt = _hgrn_gates(hq_ref[...], hf_ref[...], lb)
        do = do_ref[...]
        qd, ki, ke = gt["qd"], gt["ki"], gt["ke"]
        v_b, do_b = hi_ref[...].astype(BF16), do.astype(BF16)
        qd_b, ki_b, ke_b = qd.astype(BF16), ki.astype(BF16), ke.astype(BF16)
        pairs = [slice(u * LANE, (u + 1) * LANE) for u in range(HG_PAIRS)]
        heads = [(lanes, m0 if hh == 0 else jnp.logical_not(m0)) for lanes in pairs for hh in range(2)]
        a_b = [jnp.where(gt["tri"], _mm_nt(jnp.where(mh, qd[:, lanes], 0.0).astype(BF16), ki_b[:, lanes]), 0.0).astype(BF16)
               for lanes, mh in heads]
        doh_b = [jnp.where(mh, do[:, lanes], 0.0).astype(BF16) for lanes, mh in heads]
        da_b = [jnp.where(gt["tri"], _mm_nt(d, v_b[:, lanes]), 0.0).astype(BF16) for d, (lanes, _) in zip(doh_b, heads)]
        dv_p, dqd_p, dki_p = [], [], []
        for u, lanes in enumerate(pairs):
            e, o = 2 * u, 2 * u + 1
            dv_p.append(_mm_tn(a_b[e], doh_b[e]) + _mm_tn(a_b[o], doh_b[o]))
            dqd_p.append(jnp.where(m0, _mm(da_b[e], ki_b[:, lanes]), _mm(da_b[o], ki_b[:, lanes])))
            dki_p.append(jnp.where(m0, _mm_tn(da_b[e], qd_b[:, lanes]), _mm_tn(da_b[o], qd_b[:, lanes])))
        fed = [_bmm_tn(_chunks(do_b[:, lanes]), _chunks(qd_b[:, lanes])) for lanes in pairs]
        leaving = []
        for u, lanes in enumerate(pairs):
            ds, left = dst[u], [None] * nch
            for n in reversed(range(nch)):
                left[n] = ds
                ds = gt["decay"][n][:, lanes] * ds + jnp.where(bd, fed[u][n], 0.0)
            dst[u] = ds
            leaving.append(jnp.stack(left))
        dke_p, dlast_p = [], []
        for u, lanes in enumerate(pairs):
            entering, leaving_b = st_ref[u], leaving[u].astype(BF16)
            dke3 = _bmm_nn(_chunks(v_b[:, lanes]), leaving_b)
            dv_p[u] = dv_p[u] + _bmm_nt(_chunks(ke_b[:, lanes]), leaving_b).reshape(TH, LANE)
            dqd_p[u] = dqd_p[u] + _bmm_nn(_chunks(do_b[:, lanes]), entering.astype(BF16)).reshape(TH, LANE)
            dke_p.append(dke3.reshape(TH, LANE))
            dlast_p.append(jnp.sum(dke3 * _chunks(ke[:, lanes]), axis=1, keepdims=True)
                           + jnp.sum(leaving[u] * entering, axis=1, keepdims=True) * gt["decay"][:, :, lanes])
        cat = lambda parts: jnp.concatenate(parts, axis=-1)
        dv, dqd, dki, dke, dlast = cat(dv_p), cat(dqd_p), cat(dki_p), cat(dke_p), cat(dlast_p)
        dk = dki * gt["ei"] + dke * gt["ee"]
        dcum = dqd * qd - dki * ki - dke * ke
        dg = _chunk_cumsum(dcum, reverse=True) + jnp.broadcast_to(dlast, (nch, HG_BLOCK, dlast.shape[-1])).reshape(dcum.shape)
        sig = gt["sig"]
        df = dg / gt["f"] - dk
        dlb[...] += jnp.sum(df * (1.0 - sig), axis=0, keepdims=True)
        dh_ref[0] = (dqd * gt["e"]).astype(BF16)
        dh_ref[1] = ((df * (1.0 - lb)) * sig * (1.0 - sig)).astype(BF16)
        dh_ref[2] = dv.astype(BF16)

        @pl.when(step == nt - 1)
        def _():
            lb = _lower_bound(lbl_ref[...])
            da0 = dlb[...] * lb * (1.0 - lb)
            dlbl_ref[...] = jnp.concatenate([da0, -da0], axis=0)

    wide = HG_PAIRS * LANE
    col = lambda base: pl.BlockSpec((TH, wide), lambda p, i: (nt - 1 - i, base // wide + p))
    tile = pl.BlockSpec((TH, wide), lambda p, i: (nt - 1 - i, p))
    sds = jax.ShapeDtypeStruct
    return pl.pallas_call(
        body, name="hgrn_bwd", grid=(NH // 2 // HG_PAIRS, nt),
        in_specs=[col(P_HQ), col(P_HF), col(P_HI), pl.BlockSpec((2, wide), lambda p, i: (0, p)),
                  pl.BlockSpec((HG_PAIRS, nch, LANE, LANE), lambda p, i: (p, nt - 1 - i, 0, 0)), tile],
        out_specs=[pl.BlockSpec((3, TH, wide), lambda p, i: (0, nt - 1 - i, p)), pl.BlockSpec((2, wide), lambda p, i: (0, p))],
        out_shape=[sds((3, s, 512), BF16), sds((2, 512), F32)],
        scratch_shapes=[pltpu.VMEM((HG_PAIRS, LANE, LANE), F32), pltpu.VMEM((1, wide), F32)],
        compiler_params=_params(("parallel", "arbitrary")),
    )(proj, proj, proj, lbl, states, do_raw)


def _norm_rows_bwd(v, r, g, dn):
    dgv = dn * g
    return r * dgv - v * (r * r * r) * jnp.mean(v * dgv, axis=-1, keepdims=True)


def _qkv_bwd(proj, dq, dk, dvv, g_q, g_kv, w_uq_p, w_k_p, w_v_p, rc, rs1, rs2):
    s = proj.shape[0]
    head_q = QK_NOPE + QK_ROPE

    def body(cq_ref, ckv_ref, dq_ref, dk_ref, dv_ref, gq_ref, gkv_ref, wq_ref, wk_ref, wv_ref, c_ref, s1_ref, s2_ref,
             dcq_ref, dckv_ref, dkpe_ref, dwq_out, dwkv_out, dgq_ref, dgkv_ref, dwq_ref, dwk_ref, dwv_ref):
        @pl.when(pl.program_id(0) == 0)
        def _():
            for rf in (dwq_ref, dwk_ref, dwv_ref, dgq_ref, dgkv_ref):
                rf[...] = jnp.zeros_like(rf)

        c, s1, s2 = c_ref[...], s1_ref[...], s2_ref[...]
        cq, ckv = cq_ref[...], ckv_ref[...]
        gq, gkv = gq_ref[...], gkv_ref[...]
        cqn, rq = _norm_rows(cq, gq)
        ckvn, rkv = _norm_rows(ckv, gkv)
        cqn_b, ckvn_b = cqn.astype(BF16), ckvn.astype(BF16)
        dqf = jnp.concatenate([_rope_t(dq_ref[h], c, s1, s2) for h in range(NH)], axis=1).astype(BF16)
        dkf = jnp.concatenate([dk_ref[h] for h in range(NH)], axis=1).astype(BF16)
        dvf = jnp.concatenate([dv_ref[h] for h in range(NH)], axis=1).astype(BF16)
        dkpe = dk_ref[0]
        for h in range(1, NH):
            dkpe = dkpe + dk_ref[h]
        lane = lax.broadcasted_iota(jnp.int32, (TM, LANE), 1)
        dkpe = jnp.where((lane >= QK_NOPE) & (lane < QK_NOPE + QK_ROPE), dkpe, 0.0)
        dkpe_ref[...] = _rope_t(dkpe, c, s1, s2).astype(BF16)
        dcqn = _mm_nt(dqf, wq_ref[...])
        pair = lambda a, t: a[:, t * 2 * LANE:(t + 1) * 2 * LANE]
        dckvn = sum(_mm_nt(pair(dkf, t), wk_ref[t]) + _mm_nt(pair(dvf, t), wv_ref[t]) for t in range(N_CHIPS))
        dwq_ref[...] += _mm_tn(cqn_b, dqf)
        dwk_ref[...] += _mm_tn(ckvn_b, dkf)
        dwv_ref[...] += _mm_tn(ckvn_b, dvf)
        dgq_ref[...] += jnp.sum(dcqn * (cq * rq), axis=0, keepdims=True)
        dgkv_ref[...] += jnp.sum(dckvn * (ckv * rkv), axis=0, keepdims=True)
        dcq_ref[...] = _norm_rows_bwd(cq, rq, gq, dcqn).astype(BF16)
        dckv_ref[...] = _norm_rows_bwd(ckv, rkv, gkv, dckvn).astype(BF16)

        @pl.when(pl.program_id(0) == pl.num_programs(0) - 1)
        def _():
            blk = lambda ref, h: ref[:, h * LANE:(h + 1) * LANE]
            lane = lax.broadcasted_iota(jnp.int32, (Q_LORA, LANE), 1)
            for j in range(NH * head_q // LANE):
                h0, w0 = divmod(j * LANE, head_q)
                first = blk(dwq_ref, h0) if w0 == 0 else pltpu.roll(blk(dwq_ref, h0), LANE - w0, 1)
                second = pltpu.roll(blk(dwq_ref, h0 + 1), head_q - w0, 1)
                dwq_out[:, j * LANE:(j + 1) * LANE] = jnp.where(lane < head_q - w0, first, second).astype(BF16)
            lane = lax.broadcasted_iota(jnp.int32, (KV_LORA, LANE), 1)
            for h in range(NH):
                vals = blk(dwv_ref, h) if h % 2 else pltpu.roll(blk(dwv_ref, h), V_DIM, 1)
                both = jnp.where(lane < QK_NOPE, blk(dwk_ref, h), vals).astype(BF16)
                dwkv_out[h // 2, :, (h % 2) * LANE:(h % 2 + 1) * LANE] = both

    row = lambda w, j=0: pl.BlockSpec((TM, w), lambda i: (i, j))
    full = lambda a: pl.BlockSpec(a.shape, lambda i: (0,) * a.ndim)
    acc = lambda *shape: pl.BlockSpec(shape, lambda i: (0,) * len(shape))
    heads = pl.BlockSpec((NH, TM, LANE), lambda i: (0, i, 0))
    sds = jax.ShapeDtypeStruct
    return pl.pallas_call(
        body, name="qkv_bwd", grid=(s // TM,),
        in_specs=[row(Q_LORA, P_CQ // Q_LORA), row(KV_LORA, P_CKV // KV_LORA), heads, heads, heads,
                  full(g_q), full(g_kv), full(w_uq_p), full(w_k_p), full(w_v_p), row(LANE), row(LANE), row(LANE)],
        out_specs=[row(Q_LORA), row(KV_LORA), row(LANE), acc(Q_LORA, NH * head_q), acc(NH // 2, KV_LORA, 2 * LANE),
                   acc(1, Q_LORA), acc(1, KV_LORA)],
        out_shape=[sds((s, Q_LORA), BF16), sds((s, KV_LORA), BF16), sds((s, LANE), BF16), sds((Q_LORA, NH * head_q), BF16),
                   sds((NH // 2, KV_LORA, 2 * LANE), BF16), sds((1, Q_LORA), F32), sds((1, KV_LORA), F32)],
        scratch_shapes=[pltpu.VMEM((Q_LORA, D), F32), pltpu.VMEM((KV_LORA, D), F32), pltpu.VMEM((KV_LORA, D), F32)],
        compiler_params=_params(("arbitrary",)),
    )(proj, proj, dq, dk, dvv, g_q, g_kv, w_uq_p, w_k_p, w_v_p, rc, rs1, rs2)


def _front_bwd(x, dout, dmg, dga, dh3, dgb, dcq, dckv, dkpe, g_pre, w_in_t, w_kpe, token):
    s = x.shape[0]

    def body(x_ref, do_ref, dmg_ref, dga_ref, dh3_ref, dgb_ref, dcq_ref, dckv_ref, dkpe_ref, g_ref, w_ref, k_ref, token_ref,
             gx_ref, dg_ref):
        @pl.when(pl.program_id(0) == 0)
        def _():
            dg_ref[...] = jnp.zeros_like(dg_ref)

        xv, g = x_ref[...], g_ref[...]
        _, r = _norm_rows(xv, g)
        pieces = ((dmg_ref[...], O_MERGE), (dga_ref[...], O_GA), (dh3_ref[0], O_HQ), (dh3_ref[1], O_HF), (dh3_ref[2], O_HI),
                  (dgb_ref[...], O_GB), (dcq_ref[...], O_CQ), (dckv_ref[...], O_CKV))
        dh = _mm(dkpe_ref[...], k_ref[...])
        for piece, off in pieces:
            dh = dh + _mm(piece, w_ref[off:off + piece.shape[1], :])
        dg_ref[...] += jnp.sum(dh * (xv * r), axis=0, keepdims=True)
        gx_ref[...] = do_ref[...] + _norm_rows_bwd(xv, r, g, dh)

    row = lambda w: pl.BlockSpec((TM, w), lambda i: (i, 0))
    full = lambda a: pl.BlockSpec(a.shape, lambda i: (0,) * a.ndim)
    sds = jax.ShapeDtypeStruct
    return pl.pallas_call(
        body, name="front_bwd", grid=(s // TM,),
        in_specs=[row(D), row(D), row(2048), row(512), pl.BlockSpec((3, TM, 512), lambda i: (0, i, 0)), row(512), row(Q_LORA),
                  row(KV_LORA), row(LANE), full(g_pre), full(w_in_t), full(w_kpe), pl.BlockSpec(memory_space=pl.ANY)],
        out_specs=[row(D), pl.BlockSpec((1, D), lambda i: (0, 0))],
        out_shape=[sds((s, D), F32), sds((1, D), F32)],
        compiler_params=_params(("arbitrary",)),
    )(x, dout, dmg, dga, dh3, dgb, dcq, dckv, dkpe, g_pre, w_in_t, w_kpe, token)


TK_GRAD = 1024


QKV_ROWS = Q_LORA + KV_LORA + QK_ROPE
D_IN = O_MERGE + 2048


def _win_grad(h, pieces, offsets, name, whole=None):
    s = h.shape[0]
    n = len(pieces)
    new = whole is None
    shapes = [(p.shape[0] * p.shape[2], D) if p.ndim == 3 else (p.shape[1], D) for p in pieces]

    def body(h_ref, *refs):
        d_refs, whole_ref, scratch = refs[:n], refs[n if new else n + 1], refs[n + (1 if new else 2):]
        sums, rounded, sems = scratch[:n], scratch[n:2 * n], scratch[-1]

        @pl.when(pl.program_id(0) == 0)
        def _():
            for s_ref in sums:
                s_ref[...] = jnp.zeros_like(s_ref)

        hv = h_ref[...]
        for d_ref, s_ref in zip(d_refs, sums):
            if len(d_ref.shape) == 3:
                w = d_ref.shape[2]
                for k in range(d_ref.shape[0]):
                    s_ref[k * w:(k + 1) * w] += _mm_tn(d_ref[k], hv)
            else:
                s_ref[...] += _mm_tn(d_ref[...], hv)

        @pl.when(pl.program_id(0) == pl.num_programs(0) - 1)
        def _():
            copies = []
            for j, (s_ref, r_ref, row) in enumerate(zip(sums, rounded, offsets)):
                r_ref[...] = s_ref[...].astype(BF16)
                copies.append(pltpu.make_async_copy(r_ref, whole_ref.at[pl.ds(row, r_ref.shape[0])], sems.at[j]))
            if new:
                zeros = scratch[2 * n]
                zeros[...] = jnp.zeros_like(zeros)
                copies.append(pltpu.make_async_copy(zeros, whole_ref.at[pl.ds(0, QKV_ROWS)], sems.at[n]))
            for copy in copies:
                copy.start()
            for copy in copies:
                copy.wait()

    def in_spec(p):
        if p.ndim == 3:
            return pl.BlockSpec((p.shape[0], TK_GRAD, p.shape[2]), lambda kk: (0, kk, 0))
        return pl.BlockSpec((TK_GRAD, p.shape[1]), lambda kk: (kk, 0))

    anywhere = pl.BlockSpec(memory_space=pl.ANY)
    return pl.pallas_call(
        body, name=name, grid=(s // TK_GRAD,),
        in_specs=[pl.BlockSpec((TK_GRAD, D), lambda kk: (kk, 0))] + [in_spec(p) for p in pieces] + ([] if new else [anywhere]),
        out_specs=anywhere, out_shape=jax.ShapeDtypeStruct((D_IN, D), BF16),
        input_output_aliases={} if new else {n + 1: 0},
        scratch_shapes=[pltpu.VMEM(sh, F32) for sh in shapes] + [pltpu.VMEM(sh, BF16) for sh in shapes]
        + ([pltpu.VMEM((QKV_ROWS, D), BF16)] if new else []) + [pltpu.SemaphoreType.DMA((n + 1,))],
        compiler_params=_params(("arbitrary",)),
    )(h, *pieces, *([] if new else [whole]))


def _win_grad_qkv(h, dcq, dckv, dkpe, share):
    s = h.shape[0]
    half = QKV_ROWS // 2

    def body(h_ref, cq_ref, ckv_ref, kpe_ref, o_ref, *scratch):
        sums = scratch[0] if share else o_ref

        @pl.when(pl.program_id(0) == 0)
        def _():
            sums[...] = jnp.zeros_like(sums)

        hv = h_ref[...]
        g_cq = _mm_tn(cq_ref[...], hv)
        sums[0] += g_cq[:half]
        sums[1, 0:Q_LORA - half] += g_cq[half:]
        sums[1, Q_LORA - half:Q_LORA + KV_LORA - half] += _mm_tn(ckv_ref[...], hv)
        sums[1, Q_LORA + KV_LORA - half:] += _mm_tn(kpe_ref[...], hv)[QK_NOPE:QK_NOPE + QK_ROPE]

        if share:
            _, theirs, send_sem, recv_sem = scratch

            @pl.when(pl.program_id(0) == pl.num_programs(0) - 1)
            def _():
                c = lax.axis_index("c")
                copy = _remote(sums.at[1 - c], theirs, send_sem, recv_sem, 0, (lax.axis_index("x"), lax.axis_index("y"), 1 - c))
                copy.start()
                copy.wait()
                o_ref[...] = (sums[c] + theirs[...]).astype(BF16)

    rows = lambda a: pl.BlockSpec((TK_GRAD, a.shape[1]), lambda kk: (kk, 0))
    sem = pltpu.SemaphoreType.DMA((1,))
    shape, dtype = ((half, D), BF16) if share else ((2, half, D), F32)
    return pl.pallas_call(
        body, name="win_grad_qkv", grid=(s // TK_GRAD,), in_specs=[rows(h), rows(dcq), rows(dckv), rows(dkpe)],
        out_specs=pl.BlockSpec(shape, lambda kk: (0,) * len(shape)), out_shape=jax.ShapeDtypeStruct(shape, dtype),
        scratch_shapes=[pltpu.VMEM((2, half, D), F32), pltpu.VMEM((half, D), F32), sem, sem] if share else [],
        compiler_params=_params(("arbitrary",)),
    )(h, dcq, dckv, dkpe)


def _pad_wuq(w_uq):
    rows = w_uq.shape[0]
    w = w_uq.reshape(rows, NH, QK_NOPE + QK_ROPE)
    return jnp.pad(w, ((0, 0), (0, 0), (0, LANE - QK_NOPE - QK_ROPE))).reshape(rows, NH * LANE)


def _pad_wukv(w_ukv):
    heads = w_ukv.shape[1] // (QK_NOPE + V_DIM)
    w = w_ukv.reshape(KV_LORA, heads, QK_NOPE + V_DIM)
    w_k = jnp.pad(w[:, :, :QK_NOPE], ((0, 0), (0, 0), (0, LANE - QK_NOPE))).reshape(KV_LORA, heads * LANE)
    wv = w[:, :, QK_NOPE:].reshape(KV_LORA, heads // 2, 2, 1, V_DIM)
    eye = jnp.eye(2, dtype=w.dtype).reshape(1, 1, 2, 2, 1)
    return w_k, (wv * eye).reshape(KV_LORA, heads * LANE)


def _local_step(x, tgt, g_pre, w_in_t, b_gate, g_q, g_kv, lb_logits, g_hgrn, g_post, weights, exchange=None):
    s = x.shape[0]
    w_kpe = _kpe_block(w_in_t)
    rc, rs1, rs2 = _rope_tables(s)
    g_hg = jnp.tile(g_hgrn, (1, NH))

    proj, h = _front_fwd(x, g_pre, w_in_t, w_kpe, weights.tokens)
    w_uq_p, w_k_p, w_v_p = weights.qkv(h)
    q, k, vv = _qkv_fwd(proj, g_q, g_kv, w_uq_p, w_k_p, w_v_p, rc, rs1, rs2)
    attn, lse = _attn_fwd(q, k, vv)
    o_raw, states = _hgrn_fwd(proj, lb_logits)
    wa, wb, w_out = weights.mid(o_raw)
    (loss, dout, dattn, dga, dor, dgb, dmg, d_wout, d_wa, d_wb, d_gpost, d_bgate, d_ghg) = _mid(
        proj, attn, o_raw, x, tgt, g_hg, b_gate, g_post, wa, wb, w_out)
    d_win = _win_grad(h, [dmg, dga, dgb], [O_MERGE, O_GA, O_GB], "win_grad_mid")
    dh3, d_lbl = _hgrn_bwd(proj, lb_logits, states, dor)
    d_win = _win_grad(h, [dh3], [O_HQ], "win_grad_hgrn", d_win)
    early = dict(w_in=d_win, w_branch_a=d_wa, w_branch_b=d_wb, w_out=d_wout)
    token = exchange.start_early(early) if exchange else jnp.zeros((8, LANE), F32)
    dq, dk, dvv = _attn_bwd(q, k, vv, attn, dattn, lse, token)
    dcq, dckv, dkpe, d_wuq, d_wukv, d_gq, d_gkv = _qkv_bwd(proj, dq, dk, dvv, g_q, g_kv, w_uq_p, w_k_p, w_v_p, rc, rs1, rs2)
    late = dict(w_in_qkv=_win_grad_qkv(h, dcq, dckv, dkpe, share=exchange is not None), w_uq=d_wuq, w_ukv=d_wukv)
    token = exchange.start_late(late) if exchange else jnp.zeros((8, LANE), F32)
    grad_x, d_gpre = _front_bwd(x, dout, dmg, dga, dh3, dgb, dcq, dckv, dkpe, g_pre, w_in_t, w_kpe, token)
    vec_grads = dict(g_pre=d_gpre, b_gate=d_bgate, g_q=d_gq, g_kv=d_gkv, lb_logits=d_lbl, g_hgrn=d_ghg, g_post=d_gpost)
    return loss, grad_x, dict(early, **late), vec_grads


SHARD_SHAPES = (("w_in", (1416, 1024)), ("w_uq", (192, 768)), ("w_ukv", (256, 256)), ("w_branch_a", (512, 256)),
                ("w_branch_b", (512, 256)), ("w_out", (256, 1024)))
BIG = tuple(n for n, _ in SHARD_SHAPES)
ROW_SHARDED = ("w_in", "w_uq", "w_out")
N_CHIPS = 4
W_IN_FORWARD_CUT = 704


def _to_block(name, a):
    return a[0].T if name == "w_in" else a[0]


def _from_block(name, a):
    return a.T[None] if name == "w_in" else a[None]
VEC_ROWS = (("g_pre", 0, 1024), ("b_gate", 1, 2048), ("g_q", 2, 768), ("g_kv", 3, 256), ("g_hgrn", 6, 64), ("g_post", 7, 1024))
VEC_LB_ROW = 4
VEC_SHAPE = (8, 2048)


def _split_by_chip(name, g):
    a, b = dict(SHARD_SHAPES)[name]
    return g.reshape(N_CHIPS, a, b) if name in ROW_SHARDED else g.reshape(a, N_CHIPS, b).transpose(1, 0, 2)


def _join_chips(name, w):
    a, b = dict(SHARD_SHAPES)[name]
    return w.reshape(N_CHIPS * a, b) if name in ROW_SHARDED else w.transpose(1, 0, 2).reshape(a, N_CHIPS * b)


MESH = pl.DeviceIdType.MESH
HBM = pl.BlockSpec(memory_space=pltpu.HBM)


def _mesh_place():
    x, y, c = lax.axis_index("x"), lax.axis_index("y"), lax.axis_index("c")
    return x, y, c, 2 * x + y, [(1 - x, y), (x, 1 - y), (1 - x, 1 - y)]


def _remote(src, dst, send_sems, recv_sems, k, to):
    return pltpu.make_async_remote_copy(src_ref=src, dst_ref=dst, send_sem=send_sems.at[k], recv_sem=recv_sems.at[k],
                                        device_id=to, device_id_type=MESH)


def _gather_w_in(shard):
    a, b = shard.shape
    cut = W_IN_FORWARD_CUT

    def body(src, out, ici_send, ici_recv, d2d_send, d2d_recv, local_sem):
        x, y, c = lax.axis_index("x"), lax.axis_index("y"), lax.axis_index("c")
        me, xn, yn, dg = 2 * x + y, 2 * (1 - x) + y, 2 * x + (1 - y), 2 * (1 - x) + (1 - y)
        to_x, to_y, sibling = (1 - x, y, c), (x, 1 - y, c), (x, y, 1 - c)
        first, rest = pl.ds(0, cut), pl.ds(cut, a - cut)

        whole = lambda ref, which: ref.at[:, pl.ds(pl.multiple_of(which * (b // 2), b // 2), b // 2)]
        own = pltpu.make_async_copy(src, out.at[me], local_sem)
        own.start()
        sends = [_remote(whole(src, c), whole(out.at[me], c), ici_send, ici_recv, 0, to_x),
                 _remote(whole(src, c), whole(out.at[me], c), ici_send, ici_recv, 1, to_y)]
        for cp in sends:
            cp.start()

        def landed(slot, rows, k, d2d_k, src_dev):
            piece = whole(out.at[slot], c) if rows is None else out.at[slot].at[rows, pl.ds(pl.multiple_of(c * (b // 2), b // 2), b // 2)]
            _remote(piece, piece, ici_send, ici_recv, k, src_dev).wait_recv()
            cp = _remote(piece, piece, d2d_send, d2d_recv, d2d_k, sibling)
            cp.start()
            sends.append(cp)
            return piece

        def pass_on(slot, rows, k, to):
            piece = out.at[slot].at[rows, pl.ds(pl.multiple_of(c * (b // 2), b // 2), b // 2)]
            cp = _remote(piece, piece, ici_send, ici_recv, k, to)
            cp.start()
            sends.append(cp)

        landed(xn, None, 0, 0, to_x)
        pass_on(xn, first, 2, to_y)
        landed(yn, None, 1, 1, to_y)
        pass_on(yn, rest, 3, to_x)
        landed(dg, first, 2, 2, to_y)
        landed(dg, rest, 3, 3, to_x)
        other = pl.ds(pl.multiple_of((1 - c) * (b // 2), b // 2), b // 2)
        for d2d_k, (slot, rows) in enumerate(((xn, None), (yn, None), (dg, first), (dg, rest))):
            piece = out.at[slot].at[:, other] if rows is None else out.at[slot].at[rows, other]
            _remote(piece, piece, d2d_send, d2d_recv, d2d_k, sibling).wait_recv()
        for cp in sends:
            cp.wait_send()
        own.wait()

    sems = pltpu.SemaphoreType.DMA((4,))
    return pl.pallas_call(
        body, name="gather_w_in", in_specs=[HBM], out_specs=HBM,
        out_shape=jax.ShapeDtypeStruct((N_CHIPS, a, b), shard.dtype),
        scratch_shapes=[sems, sems, sems, sems, pltpu.SemaphoreType.DMA],
        compiler_params=pltpu.CompilerParams(has_side_effects=True),
    )(shard)


SEM =pl.BlockSpec(memory_space=pltpu.SEMAPHORE)
DATAFLOW = pltpu.SideEffectType.DATAFLOW_SIDE_EFFECTING


def _exchange_copies(srcs, to_first, src_refs, land_refs, send_sems, recv_sems):
    x, y, c, me, chips = _mesh_place()
    n = len(srcs)
    sends, recvs = [], []
    for k in range(n):
        if k in to_first:
            base = 3 * n + 4 * to_first.index(k)
            sends.append((me != 0, pltpu.make_async_remote_copy(
                src_ref=src_refs[k], dst_ref=land_refs[k].at[me], send_sem=send_sems.at[base], recv_sem=recv_sems.at[base + me],
                device_id=(0, 0, c), device_id_type=MESH)))
            for s in range(1, N_CHIPS):
                recvs.append((me == 0, pltpu.make_async_remote_copy(
                    src_ref=src_refs[k], dst_ref=land_refs[k].at[s], send_sem=send_sems.at[base], recv_sem=recv_sems.at[base + s],
                    device_id=(s // 2, s % 2, c), device_id_type=MESH)))
        else:
            slab = (lambda t, k=k: src_refs[k]) if srcs[k].ndim == 2 else (lambda t, k=k: src_refs[k].at[t])
            for j, (px, py) in enumerate(chips):
                sends.append((None, _remote(slab(2 * px + py), land_refs[k].at[me], send_sems, recv_sems, 3 * k + j, (px, py, c))))
                recvs.append((None, _remote(slab(me), land_refs[k].at[2 * px + py], send_sems, recv_sems, 3 * k + j, (px, py, c))))
    return sends, recvs


def _when(pred, fn):
    if pred is None:
        fn()
    else:
        pl.when(pred)(fn)


def _exchange_start(srcs, to_first, name, after=None):
    n = len(srcs)
    n_sems = 3 * n + 4 * len(to_first)
    lands = [lax.empty((N_CHIPS,) + s.shape[-2:], s.dtype) for s in srcs]
    extra = [] if after is None else [after]

    def body(*refs):
        src_refs, land_refs = refs[:n], refs[n:2 * n]
        send_sems, recv_sems, token = refs[2 * n + len(extra)], refs[2 * n + len(extra) + 1], refs[-1]
        sends, _ = _exchange_copies(srcs, to_first, src_refs, land_refs, send_sems, recv_sems)
        for pred, cp in sends:
            _when(pred, cp.start)
        token[...] = jnp.zeros_like(token)

    hbm = lambda a: pltpu.HBM(a.shape, a.dtype)
    res = pl.pallas_call(
        body, name=name,
        out_shape=[pltpu.SemaphoreType.DMA((n_sems,)), pltpu.SemaphoreType.DMA((n_sems,))] + [hbm(a) for a in srcs + lands]
        + [jax.ShapeDtypeStruct((8, LANE), F32)],
        in_specs=[HBM] * (2 * n) + [pl.BlockSpec(memory_space=pl.ANY)] * len(extra),
        out_specs=[SEM, SEM] + [HBM] * (2 * n) + [pl.BlockSpec(memory_space=pltpu.VMEM)],
        input_output_aliases={i: 2 + i for i in range(2 * n)},
        compiler_params=pltpu.CompilerParams(has_side_effects=DATAFLOW),
    )(*[pltpu.with_memory_space_constraint(a, pltpu.HBM) for a in srcs + lands], *extra)
    return res[:-1], res[-1]


def _exchange_wait(srcs, to_first, started, after, name):
    n = len(srcs)
    send_sems, recv_sems, thru = started[0], started[1], started[2:]

    def body(*refs):
        src_refs, land_refs, send_ref, recv_ref = refs[:n], refs[n:2 * n], refs[2 * n], refs[2 * n + 1]
        sends, recvs = _exchange_copies(srcs, to_first, src_refs, land_refs, send_ref, recv_ref)
        for pred, cp in sends:
            _when(pred, cp.wait_send)
        for pred, cp in recvs:
            _when(pred, cp.wait_recv)

    res = pl.pallas_call(
        body, name=name, out_shape=[pltpu.HBM(a.shape, a.dtype) for a in thru],
        in_specs=[HBM] * (2 * n) + [SEM, SEM, pl.BlockSpec(memory_space=pl.ANY)], out_specs=[HBM] * (2 * n),
        input_output_aliases={i: i for i in range(2 * n)},
        compiler_params=pltpu.CompilerParams(has_side_effects=DATAFLOW),
    )(*thru, send_sems, recv_sems, after)
    return res[:n], res[n:]


ROW_TILE = 256
COL_TILE = 256


def _block_tiling(a, b):
    if a <= ROW_TILE or a % ROW_TILE == 0:
        ta = min(a, ROW_TILE)
        return a // ta, (ta, b), lambda i: (i, 0)
    return b // COL_TILE, (a, COL_TILE), lambda i: (0, i)


def _sum_share_small(lands, owns, name, after=None):
    n = len(lands)
    extra = [] if after is None else [after]

    def body(*refs):
        p_refs, own_refs = refs[:n], refs[n:2 * n]
        o_refs, mine, theirs = (refs[j * n + len(extra):(j + 1) * n + len(extra)] for j in (2, 3, 4))
        send_sems, recv_sems = refs[5 * n + len(extra):]
        me = 2 * lax.axis_index("x") + lax.axis_index("y")
        sibling = (lax.axis_index("x"), lax.axis_index("y"), 1 - lax.axis_index("c"))
        copies = [_remote(mine[k], theirs[k], send_sems, recv_sems, k, sibling) for k in range(n)]
        for p_ref, own_ref, mine_ref, copy in zip(p_refs, own_refs, mine, copies):
            own = (own_ref[me] if len(own_ref.shape) == 3 else own_ref[...]).astype(F32)
            slot = lambda t: jnp.where(me == t, own, p_ref[t].astype(F32))
            mine_ref[...] = ((slot(0) + slot(1)) + slot(2)) + slot(3)
            copy.start()
        for o_ref, mine_ref, theirs_ref, copy in zip(o_refs, mine, theirs, copies):
            copy.wait()
            o_ref[...] = mine_ref[...] + theirs_ref[...]

    vmem = pl.BlockSpec(memory_space=pltpu.VMEM)
    kept = [pltpu.VMEM(land.shape[1:], F32) for land in lands]
    sems = pltpu.SemaphoreType.DMA((n,))
    return pl.pallas_call(
        body, name=name, in_specs=[vmem] * (2 * n) + [pl.BlockSpec(memory_space=pl.ANY)] * len(extra), out_specs=[vmem] * n,
        out_shape=[jax.ShapeDtypeStruct(land.shape[1:], F32) for land in lands], scratch_shapes=kept + kept + [sems, sems],
        compiler_params=_params(()),
    )(*lands, *owns, *extra)


def _adamw_small(grads, states, name):
    n = len(grads)

    def body(*refs):
        ins, outs = refs[:4 * n], refs[4 * n:]
        for k in range(n):
            g_ref, w_ref, m_ref, v_ref = ins[4 * k:4 * k + 4]
            g = g_ref[...]
            outs[4 * k][...] = g
            outs[4 * k + 1][...], outs[4 * k + 2][...], outs[4 * k + 3][...] = _adamw_math(g, w_ref[...], m_ref[...], v_ref[...])

    args = [t for k in range(n) for t in (grads[k], *states[k])]
    res = pl.pallas_call(body, name=name, out_shape=[jax.ShapeDtypeStruct(grads[k].shape, F32) for k in range(n) for _ in range(4)],
                         compiler_params=_params(()))(*args)
    return [tuple(res[4 * k:4 * k + 4]) for k in range(n)]


class _LaterWeights:
    MID = ("w_branch_a", "w_branch_b", "w_out")

    def __init__(self, blocks, after):
        self.qkv_blocks = [_pad_wuq(blocks["w_uq"]), *_pad_wukv(blocks["w_ukv"])]
        self.mid_blocks = [blocks[n] for n in self.MID]
        self.qkv_started, t1 = _exchange_start(self.qkv_blocks, (), "weights_qkv_start", after)
        self.mid_started, t2 = _exchange_start(self.mid_blocks, (), "weights_mid_start", after)
        self.tokens = [t1, t2]

    @staticmethod
    def _whole(blocks, joined, started, after, name):
        _, landed = _exchange_wait(blocks, (), started, after, name)
        me = 2 * lax.axis_index("x") + lax.axis_index("y")
        out = []
        for block, land, join in zip(blocks, landed, joined):
            w = lax.dynamic_update_index_in_dim(land, block, me, 0)
            out.append(w.reshape(N_CHIPS * block.shape[0], block.shape[1]) if join else w)
        return out

    def qkv(self, after):
        return self._whole(self.qkv_blocks, (True, False, False), self.qkv_started, after, "weights_qkv_wait")

    def mid(self, after):
        return self._whole(self.mid_blocks, (False, False, True), self.mid_started, after, "weights_mid_wait")


class _GradExchange:
    EARLY = ("w_in", "w_branch_a", "w_branch_b", "w_out")
    LATE = ("w_uq", "w_ukv")

    def __init__(self, state):
        self.state = state
        self.outs = {}

    def start_early(self, g):
        self.early = [(g[n] if g[n].ndim == 3 else _split_by_chip(n, g[n])).astype(BF16) for n in self.EARLY]
        self.early_started, token = _exchange_start(self.early, (), "grads_early_start")
        return token

    def start_late(self, g):
        self.early, self.early_landed = _exchange_wait(self.early, (), self.early_started, g["w_uq"], "grads_early_wait")
        self.late = [_split_by_chip("w_uq", g["w_uq"]), g["w_ukv"], g["w_in_qkv"]]
        self.late_started, token = _exchange_start(self.late, (2,), "grads_late_start")
        names = self.EARLY[1:]
        grads = _sum_share_small(self.early_landed[1:], self.early[1:], "sum_early", after=token)
        for n, out in zip(names, _adamw_small(grads, [self.state[n] for n in names], "adamw_early")):
            self.outs[n] = out
        return self.outs[names[-1]][0]

    def finish(self, after):
        late, late_landed = _exchange_wait(self.late, (2,), self.late_started, after, "grads_late_wait")
        mine, theirs = _sum_exchange(self.early_landed[0], self.early[0], late_landed[2], late[2], "sum_w_in")
        self.outs["w_in"] = _adamw(mine, theirs, *self.state["w_in"], "adamw_w_in")
        return list(late[:2]), list(late_landed[:2])


def _adamw_math(g, w, m, v):
    nm = ADAM_B1 * m + (1.0 - ADAM_B1) * g
    nv = ADAM_B2 * v + (1.0 - ADAM_B2) * (g * g)
    m_hat = nm / (1.0 - ADAM_B1 ** ADAM_STEP)
    v_hat = nv / (1.0 - ADAM_B2 ** ADAM_STEP)
    return -ADAM_LR * (m_hat / (jnp.sqrt(v_hat) + ADAM_EPS) + ADAM_WD * w), nm, nv


def _sum_exchange(land, own, first_land, first_own, name):
    _, a, b = land.shape
    steps, tile, at = _block_tiling(a, b)
    assert tile[0] == a, "the extra rows need whole columns in a step"
    r = first_own.shape[0]

    def body(me_ref, p_ref, own_ref, fp_ref, fo_ref, mine_ref, theirs_ref, sent_s, send_sems, recv_sems):
        i = pl.program_id(0)
        me, c = me_ref[0], lax.axis_index("c")
        sibling = (lax.axis_index("x"), lax.axis_index("y"), 1 - c)
        copy = lambda k, cols: _remote(sent_s.at[k], theirs_ref.at[:, cols], send_sems, recv_sems, k, sibling)
        own = own_ref[...].astype(F32)
        slot = lambda t: jnp.where(me == t, own, p_ref[t].astype(F32))
        sent_s[i] = ((slot(0) + slot(1)) + slot(2)) + slot(3)

        @pl.when(me == 0)
        def _():
            f = lambda t: fp_ref[t].astype(F32)
            rows = pl.ds(pl.multiple_of(c * r, 8), r)
            sent_s[i, rows, :] += ((fo_ref[...].astype(F32) + f(1)) + f(2)) + f(3)

        mine_ref[...] = sent_s[i]
        copy(i, pl.ds(pl.multiple_of(i * tile[1], tile[1]), tile[1])).start()

        @pl.when(i == steps - 1)
        def _():
            for k in range(steps):
                copy(k, pl.ds(k * tile[1], tile[1])).wait()

    in_specs = [pl.BlockSpec((N_CHIPS,) + tile, lambda i, me: (0,) + at(i)),
                pl.BlockSpec((None,) + tile, lambda i, me: (me[0],) + at(i)),
                pl.BlockSpec((N_CHIPS, r, tile[1]), lambda i, me: (0,) + at(i)),
                pl.BlockSpec((r, tile[1]), lambda i, me: at(i))]
    me = jnp.reshape(2 * lax.axis_index("x") + lax.axis_index("y"), (1,)).astype(jnp.int32)
    sems = pltpu.SemaphoreType.DMA((steps,))
    sds = jax.ShapeDtypeStruct((a, b), F32)
    return pl.pallas_call(
        body, name=name,
        grid_spec=pltpu.PrefetchScalarGridSpec(
            num_scalar_prefetch=1, grid=(steps,), in_specs=in_specs,
            out_specs=[pl.BlockSpec(tile, lambda i, me: at(i)), pl.BlockSpec(memory_space=pl.ANY)],
            scratch_shapes=[pltpu.VMEM((steps,) + tile, F32), sems, sems]),
        out_shape=[sds, sds],
        compiler_params=_params(("arbitrary",)),
    )(me, land, own, first_land, first_own)


def _adamw(p_mine, p_sibling, w, m, v, name):
    a, b = p_mine.shape
    steps, tile, at = _block_tiling(a, b)

    def body(a_ref, b_ref, w_ref, m_ref, v_ref, g_ref, d_ref, nm_ref, nv_ref):
        g = a_ref[...] + b_ref[...]
        g_ref[...] = g
        d_ref[...], nm_ref[...], nv_ref[...] = _adamw_math(g, w_ref[...], m_ref[...], v_ref[...])

    spec = pl.BlockSpec(tile, at)
    sds = jax.ShapeDtypeStruct((a, b), F32)
    return pl.pallas_call(
        body, name=name, grid=(steps,), in_specs=[spec] * 5, out_specs=[spec] * 4, out_shape=[sds] * 4,
        compiler_params=_params(("parallel",)),
    )(p_mine, p_sibling, w, m, v)


LOSS_AT = (2, 1024)


def _vec_pack(vg, loss):
    names = [n for n, _, _ in VEC_ROWS]

    def body(*refs):
        o_ref = refs[-1]
        lb_ref, loss_ref = refs[len(names)], refs[len(names) + 1]
        o_ref[...] = jnp.zeros_like(o_ref)
        o_ref[LOSS_AT[0]:LOSS_AT[0] + 1, LOSS_AT[1]:LOSS_AT[1] + LANE] = jnp.broadcast_to(loss_ref[...], (1, LANE))
        for (name, row, size), ref in zip(VEC_ROWS, refs):
            if name == "g_hgrn":
                r = lax.broadcasted_iota(jnp.int32, (NH * V_DIM, LANE), 0)
                c = lax.broadcasted_iota(jnp.int32, (NH * V_DIM, LANE), 1)
                fold = ((r % V_DIM) == c).astype(F32)
                o_ref[row:row + 1, 0:LANE] = jnp.dot(ref[...], fold, precision=HIGHEST, preferred_element_type=F32)
            else:
                o_ref[row:row + 1, 0:size] = ref[...]
        o_ref[VEC_LB_ROW:VEC_LB_ROW + 2, 0:512] = lb_ref[...]

    return pl.pallas_call(body, name="vec_pack", out_shape=jax.ShapeDtypeStruct(VEC_SHAPE, F32))(
        *[vg[n] for n in names], vg["lb_logits"], loss)


def _adamw_vec(block, w, m, v):
    names = [n for n, _, _ in VEC_ROWS] + ["lb_logits"]
    k = len(names)

    def body(g_ref, *refs):
        ins, outs = refs[:3 * k], refs[3 * k:]
        outs[-1][...] = g_ref[LOSS_AT[0]:LOSS_AT[0] + 1, LOSS_AT[1]:LOSS_AT[1] + LANE]
        for i, name in enumerate(names):
            if name == "lb_logits":
                rows, cols = slice(VEC_LB_ROW, VEC_LB_ROW + 2), slice(0, 512)
            else:
                _, row, size = VEC_ROWS[i]
                rows, cols = slice(row, row + 1), slice(0, size)
            g = g_ref[rows, cols]
            d, nm, nv = _adamw_math(g, ins[i][...], ins[k + i][...], ins[2 * k + i][...])
            for o_ref, val in zip(outs[4 * i:4 * i + 4], (g, d, nm, nv)):
                o_ref[...] = val

    shapes = [jax.ShapeDtypeStruct(w[n].shape, F32) for n in names for _ in range(4)] + [jax.ShapeDtypeStruct((1, LANE), F32)]
    res = pl.pallas_call(body, name="adamw_vec", out_shape=shapes)(
        block, *[w[n] for n in names], *[m[n] for n in names], *[v[n] for n in names])
    return [{n: res[4 * i + j] for i, n in enumerate(names)} for j in range(4)], res[-1]


WEIGHTS = ("g_pre", "w_in", "b_gate", "g_q", "w_uq", "g_kv", "w_ukv", "lb_logits", "g_hgrn", "w_branch_a", "w_branch_b", "w_out", "g_post")


def kernel(x, g_pre, w_in, b_gate, g_q, w_uq, g_kv, w_ukv, lb_logits, g_hgrn, w_branch_a, w_branch_b, w_out, g_post, loss_target, m_g_pre, m_w_in, m_b_gate, m_g_q, m_w_uq, m_g_kv, m_w_ukv, m_lb_logits, m_g_hgrn, m_w_branch_a, m_w_branch_b, m_w_out, m_g_post, v_g_pre, v_w_in, v_b_gate, v_g_q, v_w_uq, v_g_kv, v_w_ukv, v_lb_logits, v_g_hgrn, v_w_branch_a, v_w_branch_b, v_w_out, v_g_post):
    w = dict(g_pre=g_pre, w_in=w_in, b_gate=b_gate, g_q=g_q, w_uq=w_uq, g_kv=g_kv, w_ukv=w_ukv, lb_logits=lb_logits, g_hgrn=g_hgrn,
             w_branch_a=w_branch_a, w_branch_b=w_branch_b, w_out=w_out, g_post=g_post)
    m = dict(g_pre=m_g_pre, w_in=m_w_in, b_gate=m_b_gate, g_q=m_g_q, w_uq=m_w_uq, g_kv=m_g_kv, w_ukv=m_w_ukv, lb_logits=m_lb_logits,
             g_hgrn=m_g_hgrn, w_branch_a=m_w_branch_a, w_branch_b=m_w_branch_b, w_out=m_w_out, g_post=m_g_post)
    v = dict(g_pre=v_g_pre, w_in=v_w_in, b_gate=v_b_gate, g_q=v_g_q, w_uq=v_w_uq, g_kv=v_g_kv, w_ukv=v_w_ukv, lb_logits=v_lb_logits,
             g_hgrn=v_g_hgrn, w_branch_a=v_w_branch_a, w_branch_b=v_w_branch_b, w_out=v_w_out, g_post=v_g_post)
    blocks = {n: _to_block(n, w[n]).astype(BF16) for n in BIG}
    w_in_all = _gather_w_in(blocks["w_in"])
    weights = _LaterWeights(blocks, w_in_all)
    state = {n: [_to_block(n, t[n]) for t in (w, m, v)] for n in BIG}
    exchange = _GradExchange(state)
    loss, grad_x, _, vec_grads = _local_step(
        x[0], loss_target[0], g_pre, _join_chips("w_in", w_in_all), b_gate, g_q, g_kv, lb_logits, g_hgrn, g_post, weights, exchange)
    vec = _vec_pack(vec_grads, loss)
    vec_started, token = _exchange_start([vec], (), "vec_start")
    late_sent, late_landed = exchange.finish(token)
    (vec,), (vec_landed,) = _exchange_wait([vec], (), vec_started, exchange.outs["w_in"][0], "vec_wait")
    grads = _sum_share_small(late_landed + [vec_landed], late_sent + [vec], "sum_late")
    done = dict(exchange.outs)
    done.update(zip(exchange.LATE, _adamw_small(grads[:-1], [state[n] for n in exchange.LATE], "adamw_late")))
    outs = [{}, {}, {}, {}]
    for n in BIG:
        for o, val in zip(outs, done[n]):
            o[n] = _from_block(n, val)
    vec_outs, total = _adamw_vec(grads[-1], w, m, v)
    for o, vals in zip(outs, vec_outs):
        o.update(vals)
    return (total[0, 0], grad_x[None], *[o[n] for o in outs for n in WEIGHTS])
```

```python
import math

import numpy as np
import jax
import jax.numpy as jnp
from jax import lax
from jax.experimental import pallas as pl
from jax.experimental.pallas import tpu as pltpu

F32 = jnp.float32
BF16 = jnp.bfloat16
HIGHEST = lax.Precision.HIGHEST

D = 1024
NH = 8
QK_NOPE, QK_ROPE, V_DIM = 64, 32, 64
Q_LORA, KV_LORA = 768, 256
CHUNK = 64
HG_BLOCK = 32
EPS = 1e-6
LANE = 128
P_MERGE, P_GA, P_HQ, P_HF, P_HI, P_GB, P_CQ, P_CKV, P_KPE = 0, 2048, 2560, 3072, 3584, 4096, 4608, 5376, 5632
D_P = 5760
O_CQ, O_CKV, O_KPE, O_GA, O_HQ, O_HF, O_HI, O_GB, O_MERGE = 0, 768, 1024, 1056, 1568, 2080, 2592, 3104, 3616

TM = 512
TM_MID = 256
TQ = 1024
ONES_LANE = (LANE - 1, 0)
TH = 256
HG_PAIRS = 4
VMEM_LIMIT = 56 * 1024 * 1024

ADAM_LR, ADAM_B1, ADAM_B2, ADAM_EPS, ADAM_WD, ADAM_STEP = 0.001, 0.9, 0.999, 1e-08, 0.01, 10

NT_DIMS = (((1,), (1,)), ((), ()))
TN_DIMS = (((0,), (0,)), ((), ()))


def _params(sem):
    return pltpu.CompilerParams(dimension_semantics=sem, vmem_limit_bytes=VMEM_LIMIT)


def _mm(a, b):
    return jnp.dot(a, b, preferred_element_type=F32)


def _mm_nt(a, b):
    return lax.dot_general(a, b, NT_DIMS, preferred_element_type=F32)


def _mm_tn(a, b):
    return lax.dot_general(a, b, TN_DIMS, preferred_element_type=F32)


def _sigmoid(z):
    return jax.nn.sigmoid(z)


def _rope(v, c, s1, s2):
    return v * c + pltpu.roll(v, 112, 1) * s1 + pltpu.roll(v, 16, 1) * s2


def _rope_t(dy, c, s1, s2):
    return dy * c + pltpu.roll(dy * s1, 16, 1) + pltpu.roll(dy * s2, 112, 1)


def _rope_tables(s):
    f32 = np.float32
    inv = f32(10000.0) ** (-np.arange(0, QK_ROPE, 2, dtype=f32) / f32(QK_ROPE))
    ang = np.arange(s, dtype=f32)[:, None] * inv[None, :]
    cos, sin = np.cos(ang).astype(f32), np.sin(ang).astype(f32)
    z64, z32, o64, o32 = np.zeros((s, 64), f32), np.zeros((s, 32), f32), np.ones((s, 64), f32), np.ones((s, 32), f32)
    z16 = np.zeros((s, 16), f32)
    c = np.concatenate([o64, cos, cos, o32], axis=1)
    s1 = np.concatenate([z64, -sin, z16, z32], axis=1)
    s2 = np.concatenate([z64, z16, sin, z32], axis=1)
    return jnp.asarray(c), jnp.asarray(s1), jnp.asarray(s2)


W_IN_RUNS = ((O_MERGE, 2048, P_MERGE), (O_GA, O_MERGE - O_GA, P_GA), (O_CQ, O_KPE - O_CQ, P_CQ))


def _kpe_block(w_in_t):
    z = lambda n: jnp.zeros((n, w_in_t.shape[1]), w_in_t.dtype)
    return jnp.concatenate([z(64), w_in_t[O_KPE:O_KPE + QK_ROPE], z(32)], axis=0)


def _front_fwd(x, g_pre, w_in_t, w_kpe, tokens=()):
    s = x.shape[0]
    tokens = list(tokens)

    def body(x_ref, g_ref, w_ref, k_ref, *refs):
        o_ref, h_ref = refs[len(tokens):]
        xv = x_ref[...]
        r = lax.rsqrt(jnp.mean(xv * xv, axis=-1, keepdims=True) + EPS)
        h = ((xv * r) * g_ref[...]).astype(BF16)
        h_ref[...] = h
        for row, rows, col in W_IN_RUNS:
            o_ref[:, col:col + rows] = _mm_nt(h, w_ref[row:row + rows, :])
        o_ref[:, P_KPE:P_KPE + LANE] = _mm_nt(h, k_ref[...])

    full = lambda a: pl.BlockSpec(a.shape, lambda i: (0,) * a.ndim)
    return pl.pallas_call(
        body, name="front_fwd", grid=(s // TM,),
        in_specs=[pl.BlockSpec((TM, D), lambda i: (i, 0)), pl.BlockSpec((1, D), lambda i: (0, 0)), full(w_in_t), full(w_kpe)]
        + [pl.BlockSpec((8, LANE), lambda i: (0, 0))] * len(tokens),
        out_specs=[pl.BlockSpec((TM, D_P), lambda i: (i, 0)), pl.BlockSpec((TM, D), lambda i: (i, 0))],
        out_shape=[jax.ShapeDtypeStruct((s, D_P), F32), jax.ShapeDtypeStruct((s, D), BF16)],
        compiler_params=_params(("parallel",)),
    )(x, g_pre, w_in_t, w_kpe, *tokens)


def _norm_rows(v, g):
    r = lax.rsqrt(jnp.mean(v * v, axis=-1, keepdims=True) + EPS)
    return (v * r) * g, r


def _qkv_fwd(proj, g_q, g_kv, w_uq_p, w_k_p, w_v_p, rc, rs1, rs2):
    s = proj.shape[0]

    def body(cq_ref, ckv_ref, kpe_ref, gq_ref, gkv_ref, wq_ref, wk_ref, wv_ref, c_ref, s1_ref, s2_ref, q_ref, k_ref, v_ref):
        c, s1, s2 = c_ref[...], s1_ref[...], s2_ref[...]
        cqn, _ = _norm_rows(cq_ref[...], gq_ref[...])
        ckvn, _ = _norm_rows(ckv_ref[...], gkv_ref[...])
        ckvn = ckvn.astype(BF16)
        qf = _mm(cqn.astype(BF16), wq_ref[...])
        kf = jnp.concatenate([_mm(ckvn, wk_ref[t]) for t in range(N_CHIPS)], axis=1)
        vf = jnp.concatenate([_mm(ckvn, wv_ref[t]) for t in range(N_CHIPS)], axis=1)
        kpe = _rope(kpe_ref[...], c, s1, s2)
        lane = lax.broadcasted_iota(jnp.int32, (TM, LANE), 1)
        for h in range(NH):
            blk = slice(h * LANE, (h + 1) * LANE)
            q_ref[h] = _rope(qf[:, blk], c, s1, s2).astype(BF16)
            k_ref[h] = (kf[:, blk] + kpe).astype(BF16)
            v_ref[h] = jnp.where(lane == ONES_LANE[h % 2], 1.0, vf[:, blk]).astype(BF16)

    row = lambda w, j: pl.BlockSpec((TM, w), lambda i: (i, j))
    full = lambda a: pl.BlockSpec(a.shape, lambda i: (0,) * a.ndim)
    hs = jax.ShapeDtypeStruct((NH, s, LANE), BF16)
    return pl.pallas_call(
        body, name="qkv_fwd", grid=(s // TM,),
        in_specs=[row(Q_LORA, P_CQ // Q_LORA), row(KV_LORA, P_CKV // KV_LORA), row(LANE, P_KPE // LANE),
                  full(g_q), full(g_kv), full(w_uq_p), full(w_k_p), full(w_v_p), row(LANE, 0), row(LANE, 0), row(LANE, 0)],
        out_specs=[pl.BlockSpec((NH, TM, LANE), lambda i: (0, i, 0))] * 3,
        out_shape=[hs, hs, hs],
        compiler_params=_params(("parallel",)),
    )(proj, proj, proj, g_q, g_kv, w_uq_p, w_k_p, w_v_p, rc, rs1, rs2)


LOG2E = 1.4426950408889634
QK_SCALE2 = LOG2E / math.sqrt(QK_NOPE + QK_ROPE)


HQ = TQ // 2


def _diag_visible(n):
    row = lax.broadcasted_iota(jnp.int32, (n, n), 0)
    col = lax.broadcasted_iota(jnp.int32, (n, n), 1)
    return (col // CHUNK) <= (row // CHUNK)


def _attn_fwd(q, k, vv):
    s = q.shape[1]

    def body(q_ref, k_ref, v_ref, o_ref, lse_ref):
        i = pl.program_id(1)
        qs = (q_ref[0], q_ref[1])

        def tiles(t, carry, diag):
            rows = pl.ds(pl.multiple_of(t * TQ, TQ), TQ)
            sc = [_mm_nt(qs[hh], k_ref[hh, rows, :]) for hh in range(2)]
            if diag:
                sc = [jnp.where(_diag_visible(TQ), s_, -jnp.inf) for s_ in sc]
            m_new = [jnp.maximum(carry[hh][0], jnp.max(sc[hh], axis=-1, keepdims=True)) for hh in range(2)]
            alpha = [jnp.exp2((carry[hh][0] - m_new[hh]) * QK_SCALE2) for hh in range(2)]
            p = [jnp.exp2((sc[hh] - m_new[hh]) * QK_SCALE2).astype(BF16) for hh in range(2)]
            acc = [alpha[hh] * carry[hh][1] + _mm(p[hh], v_ref[hh, rows, :]) for hh in range(2)]
            return (m_new[0], acc[0]), (m_new[1], acc[1])

        init = (jnp.full((TQ, 1), -jnp.inf, F32), jnp.zeros((TQ, LANE), F32))
        carry = lax.fori_loop(0, i, lambda t, c: tiles(t, c, False), (init, init))
        carry = tiles(i, carry, True)
        lane = lax.broadcasted_iota(jnp.int32, (TQ, LANE), 1)
        out = jnp.zeros((TQ, LANE), F32)
        for hh in range(2):
            m, acc = carry[hh]
            l = jnp.sum(jnp.where(lane == ONES_LANE[hh], acc, 0.0), axis=-1, keepdims=True)
            out = out + jnp.where((lane < V_DIM) == (hh == 0), acc, 0.0) / l
            lse_ref[hh] = jnp.broadcast_to(m * QK_SCALE2 + jnp.log(l) * LOG2E, (TQ, LANE))
        o_ref[...] = out

    return pl.pallas_call(
        body, name="attn_fwd", grid=(NH // 2, s // TQ),
        in_specs=[pl.BlockSpec((2, TQ, LANE), lambda p, i: (p, i, 0)), pl.BlockSpec((2, s, LANE), lambda p, i: (p, 0, 0)),
                  pl.BlockSpec((2, s, LANE), lambda p, i: (p, 0, 0))],
        out_specs=[pl.BlockSpec((TQ, LANE), lambda p, i: (i, p)), pl.BlockSpec((2, TQ, LANE), lambda p, i: (p, i, 0))],
        out_shape=[jax.ShapeDtypeStruct((s, NH * V_DIM), F32), jax.ShapeDtypeStruct((NH, s, LANE), F32)],
        compiler_params=_params(("parallel", "parallel")),
    )(q, k, vv)


def _lower_bound(lbl):
    a0, a1 = lbl[0:1, :], lbl[1:2, :]
    mx = jnp.maximum(a0, a1)
    e0, e1 = jnp.exp(a0 - mx), jnp.exp(a1 - mx)
    return e0 / (e0 + e1)


def _chunk_cumsum(v, reverse=False):
    pos = lax.broadcasted_iota(jnp.int32, v.shape, 0) % HG_BLOCK
    s = 1
    while s < HG_BLOCK:
        if reverse:
            v = v + jnp.where(pos < HG_BLOCK - s, pltpu.roll(v, TH - s, 0), 0.0)
        else:
            v = v + jnp.where(pos >= s, pltpu.roll(v, s, 0), 0.0)
        s *= 2
    return v


def _hgrn_gates(hq, hf, lb):
    sig = _sigmoid(hf)
    f = lb + (1.0 - lb) * sig
    g = jnp.log(f)
    kk = 1.0 - f
    r = lax.broadcasted_iota(jnp.int32, (TH, TH), 0)
    c = lax.broadcasted_iota(jnp.int32, (TH, TH), 1)
    tri = ((r // HG_BLOCK) == (c // HG_BLOCK)) & (r >= c)
    cum = _chunk_cumsum(g)
    nch = TH // HG_BLOCK
    total = _chunks(cum)[:, HG_BLOCK - 1:HG_BLOCK, :]
    lastb = jnp.broadcast_to(total, (nch, HG_BLOCK, hf.shape[-1])).reshape(hf.shape)
    e, ei, ee = jnp.exp(cum), jnp.exp(-cum), jnp.exp(lastb - cum)
    return dict(sig=sig, f=f, kk=kk, tri=tri, cum=cum, total=total, decay=jnp.exp(total), e=e, ei=ei, ee=ee,
                qd=hq * e, ki=kk * ei, ke=kk * ee)


def _chunks(v):
    return v.reshape(TH // HG_BLOCK, HG_BLOCK, v.shape[-1])


def _bmm_nt(a, b):
    return lax.dot_general(a, b, (((2,), (2,)), ((0,), (0,))), preferred_element_type=F32)


def _bmm_nn(a, b):
    return lax.dot_general(a, b, (((2,), (1,)), ((0,), (0,))), preferred_element_type=F32)


def _bmm_tn(a, b):
    return lax.dot_general(a, b, (((1,), (1,)), ((0,), (0,))), preferred_element_type=F32)


def _pair_masks():
    lane = lax.broadcasted_iota(jnp.int32, (TH, LANE), 1)
    kr = lax.broadcasted_iota(jnp.int32, (LANE, LANE), 0)
    kc = lax.broadcasted_iota(jnp.int32, (LANE, LANE), 1)
    return lane < 64, (kr // 64) == (kc // 64)


def _hgrn_fwd(proj, lbl):
    s = proj.shape[0]
    nch = TH // HG_BLOCK

    def body(hq_ref, hf_ref, hi_ref, lbl_ref, o_ref, st_ref, st):
        @pl.when(pl.program_id(1) == 0)
        def _():
            st[...] = jnp.zeros_like(st)

        m0, bd = _pair_masks()
        gt = _hgrn_gates(hq_ref[...], hf_ref[...], _lower_bound(lbl_ref[...]))
        v_b, qd, qd_b = hi_ref[...].astype(BF16), gt["qd"], gt["qd"].astype(BF16)
        ki_b, ke_b = gt["ki"].astype(BF16), gt["ke"].astype(BF16)
        pairs = [slice(u * LANE, (u + 1) * LANE) for u in range(HG_PAIRS)]
        heads = [(lanes, m0 if hh == 0 else jnp.logical_not(m0)) for lanes in pairs for hh in range(2)]
        a_b = [jnp.where(gt["tri"], _mm_nt(jnp.where(mh, qd[:, lanes], 0.0).astype(BF16), ki_b[:, lanes]), 0.0).astype(BF16)
               for lanes, mh in heads]
        intra = [jnp.where(m0, _mm(a_b[2 * u], v_b[:, lanes]), _mm(a_b[2 * u + 1], v_b[:, lanes])) for u, lanes in enumerate(pairs)]
        upd = [_bmm_tn(_chunks(v_b[:, lanes]), _chunks(ke_b[:, lanes])) for lanes in pairs]
        entering = []
        for u, lanes in enumerate(pairs):
            cur, states = st[u], []
            for n in range(nch):
                states.append(cur)
                cur = gt["decay"][n][:, lanes] * cur + jnp.where(bd, upd[u][n], 0.0)
            st[u] = cur
            entering.append(jnp.stack(states))
            st_ref[u] = entering[u]
        for u, lanes in enumerate(pairs):
            o_ref[:, lanes] = intra[u] + _bmm_nt(_chunks(qd_b[:, lanes]), entering[u].astype(BF16)).reshape(TH, LANE)

    wide = HG_PAIRS * LANE
    col = lambda base: pl.BlockSpec((TH, wide), lambda p, i: (i, base // wide + p))
    return pl.pallas_call(
        body, name="hgrn_fwd", grid=(NH // 2 // HG_PAIRS, s // TH),
        in_specs=[col(P_HQ), col(P_HF), col(P_HI), pl.BlockSpec((2, wide), lambda p, i: (0, p))],
        out_specs=[pl.BlockSpec((TH, wide), lambda p, i: (i, p)),
                   pl.BlockSpec((HG_PAIRS, nch, LANE, LANE), lambda p, i: (p, i, 0, 0))],
        out_shape=[jax.ShapeDtypeStruct((s, 512), F32), jax.ShapeDtypeStruct((NH // 2, s // HG_BLOCK, LANE, LANE), F32)],
        scratch_shapes=[pltpu.VMEM((HG_PAIRS, LANE, LANE), F32)],
        compiler_params=_params(("parallel", "arbitrary")),
    )(proj, proj, proj, lbl)


def _group_sum(v):
    low = lax.broadcasted_iota(jnp.int32, (v.shape[0], LANE), 1) < V_DIM
    blocks = []
    for b in range(v.shape[1] // LANE):
        blk = v[:, b * LANE:(b + 1) * LANE]
        s_low = jnp.sum(jnp.where(low, blk, 0.0), axis=-1, keepdims=True)
        s_high = jnp.sum(jnp.where(low, 0.0, blk), axis=-1, keepdims=True)
        blocks.append(jnp.where(low, s_low, s_high))
    return jnp.concatenate(blocks, axis=1)


def _dsilu(z, sg):
    return sg * (1.0 + z * (1.0 - sg))


def _mid(proj, attn, o_raw, x, tgt, g_hg, b_gate, g_post, wa, wb, w_out):
    s = x.shape[0]

    def body(attn_ref, ga_ref, o_ref, gb_ref, mg_ref, x_ref, t_ref, ghg_ref, bg_ref, gp_ref, wa_ref, wb_ref, wo_ref,
             loss_ref, dout_ref, dattn_ref, dga_ref, dor_ref, dgb_ref, dmg_ref, dwo_out, dwa_out, dwb_out, dgp_ref, dbg_ref, dghg_ref,
             dwo_ref, dwa_ref, dwb_ref):
        @pl.when(pl.program_id(0) == 0)
        def _():
            for rf in (loss_ref, dwo_ref, dwa_ref, dwb_ref, dgp_ref, dbg_ref, dghg_ref):
                rf[...] = jnp.zeros_like(rf)

        attn, za, orw, zb = attn_ref[...], ga_ref[...], o_ref[...], gb_ref[...]
        ghg, gp = ghg_ref[...], gp_ref[...]
        sga, sgb = _sigmoid(za), _sigmoid(zb)
        sa, sb = za * sga, zb * sgb
        ga = attn * sa
        rh = lax.rsqrt(_group_sum(orw * orw) * (1.0 / V_DIM) + EPS)
        on = (orw * rh) * ghg
        gb = on * sb
        ga_b, gb_b = ga.astype(BF16), gb.astype(BF16)
        blocks = [slice(t * (D // N_CHIPS), (t + 1) * (D // N_CHIPS)) for t in range(N_CHIPS)]
        ya = jnp.concatenate([_mm(ga_b, wa_ref[t]) for t in range(N_CHIPS)], axis=1)
        yb = jnp.concatenate([_mm(gb_b, wb_ref[t]) for t in range(N_CHIPS)], axis=1)
        gates = _sigmoid(mg_ref[...] + bg_ref[...])
        g0, g1 = gates[:, :D], gates[:, D:]
        m_b = (g0 * ya + g1 * yb).astype(BF16)
        y = _mm(m_b, wo_ref[...])
        ry = lax.rsqrt(jnp.mean(y * y, axis=-1, keepdims=True) + EPS)
        out = x_ref[...] + (y * ry) * gp
        err = out - t_ref[...]
        loss_ref[...] += 0.5 * jnp.sum(jnp.mean(err * err, axis=-1, keepdims=True), axis=0, keepdims=True)
        dout = err * (1.0 / D)
        dout_ref[...] = dout
        dgp_ref[...] += jnp.sum(dout * (y * ry), axis=0, keepdims=True)
        dgy = dout * gp
        dy = ry * dgy - y * (ry * ry * ry) * jnp.mean(y * dgy, axis=-1, keepdims=True)
        dy_b = dy.astype(BF16)
        dm = _mm_nt(dy_b, wo_ref[...])
        dya_b, dyb_b = (dm * g0).astype(BF16), (dm * g1).astype(BF16)
        dga = sum(_mm_nt(dya_b[:, cols], wa_ref[t]) for t, cols in enumerate(blocks))
        dgb = sum(_mm_nt(dyb_b[:, cols], wb_ref[t]) for t, cols in enumerate(blocks))
        dwo_ref[...] += _mm_tn(m_b, dy_b)
        for t, cols in enumerate(blocks):
            dwa_ref[t] += _mm_tn(ga_b, dya_b[:, cols])
            dwb_ref[t] += _mm_tn(gb_b, dyb_b[:, cols])
        dg0, dg1 = dm * ya, dm * yb
        dmg = jnp.concatenate([dg0 * g0 * (1.0 - g0), dg1 * g1 * (1.0 - g1)], axis=1)
        dmg_ref[...] = dmg.astype(BF16)
        dbg_ref[...] += jnp.sum(dmg, axis=0, keepdims=True)
        dattn_ref[...] = dga * sa
        dga_ref[...] = (dga * attn * _dsilu(za, sga)).astype(BF16)
        dgb_ref[...] = (dgb * on * _dsilu(zb, sgb)).astype(BF16)
        don = dgb * sb
        dghg_ref[...] += jnp.sum(don * (orw * rh), axis=0, keepdims=True)
        dgo = don * ghg
        dor_ref[...] = rh * dgo - orw * (rh * rh * rh) * (_group_sum(orw * dgo) * (1.0 / V_DIM))

        @pl.when(pl.program_id(0) == pl.num_programs(0) - 1)
        def _():
            for out, rf in ((dwo_out, dwo_ref), (dwa_out, dwa_ref), (dwb_out, dwb_ref)):
                out[...] = rf[...].astype(BF16)

    row = lambda w, j=0: pl.BlockSpec((TM_MID, w), lambda i: (i, j))
    full = lambda a: pl.BlockSpec(a.shape, lambda i: (0,) * a.ndim)
    acc = lambda shape: pl.BlockSpec(shape, lambda i: (0,) * len(shape))
    slabs = (N_CHIPS, 512, D // N_CHIPS)
    sds = jax.ShapeDtypeStruct
    return pl.pallas_call(
        body, name="mid", grid=(s // TM_MID,),
        in_specs=[row(512), row(512, P_GA // 512), row(512), row(512, P_GB // 512), row(2048, P_MERGE // 2048), row(D), row(D),
                  full(g_hg), full(b_gate), full(g_post), full(wa), full(wb), full(w_out)],
        out_specs=[acc((1, 1)), row(D), row(512), row(512), row(512), row(512), row(2048),
                   acc((D, D)), acc(slabs), acc(slabs), acc((1, D)), acc((1, 2048)), acc((1, 512))],
        out_shape=[sds((1, 1), F32), sds((s, D), F32), sds((s, 512), F32), sds((s, 512), BF16), sds((s, 512), F32), sds((s, 512), BF16),
                   sds((s, 2048), BF16), sds((D, D), BF16), sds(slabs, BF16), sds(slabs, BF16), sds((1, D), F32),
                   sds((1, 2048), F32), sds((1, 512), F32)],
        scratch_shapes=[pltpu.VMEM((D, D), F32), pltpu.VMEM(slabs, F32), pltpu.VMEM(slabs, F32)],
        compiler_params=_params(("arbitrary",)),
    )(attn, proj, o_raw, proj, proj, x, tgt, g_hg, b_gate, g_post, wa, wb, w_out)


def _attn_bwd(q, k, vv, attn, dattn, lse, token):
    s = q.shape[1]
    nt = s // TQ
    scale = 1.0 / math.sqrt(QK_NOPE + QK_ROPE)

    def body(q_ref, k_ref, v_ref, o_ref, do_ref, lse_ref, token_ref, dq_ref, dk_ref, dv_ref, do_s, delta_s):
        j = pl.program_id(1)

        @pl.when(j == 0)
        def _():
            dq_ref[...] = jnp.zeros_like(dq_ref)
            lane = lax.broadcasted_iota(jnp.int32, (TQ, LANE), 1)

            @pl.loop(0, nt)
            def _(i):
                rows = pl.ds(pl.multiple_of(i * TQ, TQ), TQ)
                do, o = do_ref[rows, :], o_ref[rows, :]
                for hh in range(2):
                    doh = jnp.where((lane < 64) if hh == 0 else (lane >= 64), do, 0.0)
                    do_s[hh, rows, :] = doh.astype(BF16)
                    delta_s[hh, rows, :] = jnp.broadcast_to(jnp.sum(doh * o, axis=-1, keepdims=True), (TQ, LANE))

        kjs, vjs = (k_ref[0], k_ref[1]), (v_ref[0], v_ref[1])

        def tile(hh, start, size, kj, vj, diag):
            rows = pl.ds(pl.multiple_of(start, size), size)
            wide = lambda a: jnp.concatenate([a] * (kj.shape[0] // LANE), axis=1)
            qi, do_b = q_ref[hh, rows, :], do_s[hh, rows, :]
            sc, dp = _mm_nt(qi, kj), _mm_nt(do_b, vj)
            p = jnp.exp2(sc * QK_SCALE2 - wide(lse_ref[hh, rows, :]))
            if diag:
                p = jnp.where(_diag_visible(size), p, 0.0)
            ds_b = (p * (dp - wide(delta_s[hh, rows, :]))).astype(BF16)
            dv, dk = _mm_tn(do_b, p.astype(BF16)), _mm_tn(qi, ds_b)
            dq_ref[hh, rows, :] += _mm(ds_b, kj)
            return dk, dv

        def step(i, carry):
            new = [tile(hh, i * TQ, TQ, kjs[hh], vjs[hh], False) for hh in range(2)]
            return tuple((carry[hh][0] + new[hh][0], carry[hh][1] + new[hh][1]) for hh in range(2))

        def diagonal(hh):
            k0, k1, v0, v1 = kjs[hh][:HQ], kjs[hh][HQ:], vjs[hh][:HQ], vjs[hh][HQ:]
            a = tile(hh, j * TQ, HQ, k0, v0, True)
            b = tile(hh, j * TQ + HQ, HQ, k0, v0, False)
            c = tile(hh, j * TQ + HQ, HQ, k1, v1, True)
            return jnp.concatenate([a[0] + b[0], c[0]], axis=1), jnp.concatenate([a[1] + b[1], c[1]], axis=1)

        carry = lax.fori_loop(j + 1, nt, step, (diagonal(0), diagonal(1)))
        for hh in range(2):
            dk_ref[hh] = carry[hh][0].T * scale
            dv_ref[hh] = carry[hh][1].T

        @pl.when(j == nt - 1)
        def _():
            dq_ref[...] = dq_ref[...] * scale

    whole = pl.BlockSpec((2, s, LANE), lambda p, j: (p, 0, 0))
    tile_spec = pl.BlockSpec((2, TQ, LANE), lambda p, j: (p, j, 0))
    cols = pl.BlockSpec((s, LANE), lambda p, j: (0, p))
    hs = jax.ShapeDtypeStruct((NH, s, LANE), F32)
    return pl.pallas_call(
        body, name="attn_bwd", grid=(NH // 2, nt),
        in_specs=[whole, tile_spec, tile_spec, cols, cols, whole, pl.BlockSpec((8, LANE), lambda p, j: (0, 0))],
        out_specs=[whole, tile_spec, tile_spec],
        out_shape=[hs, hs, hs],
        scratch_shapes=[pltpu.VMEM((2, s, LANE), BF16), pltpu.VMEM((2, s, LANE), F32)],
        compiler_params=_params(("parallel", "arbitrary")),
    )(q, k, vv, attn, dattn, lse, token)


def _hgrn_bwd(proj, lbl, states, do_raw):
    s = proj.shape[0]
    nt = s // TH
    nch = TH // HG_BLOCK

    def body(hq_ref, hf_ref, hi_ref, lbl_ref, st_ref, do_ref, dh_ref, dlbl_ref, dst, dlb):
        step = pl.program_id(1)

        @pl.when(step == 0)
        def _():
            dst[...] = jnp.zeros_like(dst)
            dlb[...] = jnp.zeros_like(dlb)

        m0, bd = _pair_masks()
        lb = _lower_bound(lbl_ref[...])
        gt = _hgrn_gates(hq_ref[...], hf_ref[...], lb)
        do = do_ref[...]
        qd, ki, ke = gt["qd"], gt["ki"], gt["ke"]
        v_b, do_b = hi_ref[...].astype(BF16), do.astype(BF16)
        qd_b, ki_b, ke_b = qd.astype(BF16), ki.astype(BF16), ke.astype(BF16)
        pairs = [slice(u * LANE, (u + 1) * LANE) for u in range(HG_PAIRS)]
        heads = [(lanes, m0 if hh == 0 else jnp.logical_not(m0)) for lanes in pairs for hh in range(2)]
        a_b = [jnp.where(gt["tri"], _mm_nt(jnp.where(mh, qd[:, lanes], 0.0).astype(BF16), ki_b[:, lanes]), 0.0).astype(BF16)
               for lanes, mh in heads]
        doh_b = [jnp.where(mh, do[:, lanes], 0.0).astype(BF16) for lanes, mh in heads]
        da_b = [jnp.where(gt["tri"], _mm_nt(d, v_b[:, lanes]), 0.0).astype(BF16) for d, (lanes, _) in zip(doh_b, heads)]
        dv_p, dqd_p, dki_p = [], [], []
        for u, lanes in enumerate(pairs):
            e, o = 2 * u, 2 * u + 1
            dv_p.append(_mm_tn(a_b[e], doh_b[e]) + _mm_tn(a_b[o], doh_b[o]))
            dqd_p.append(jnp.where(m0, _mm(da_b[e], ki_b[:, lanes]), _mm(da_b[o], ki_b[:, lanes])))
            dki_p.append(jnp.where(m0, _mm_tn(da_b[e], qd_b[:, lanes]), _mm_tn(da_b[o], qd_b[:, lanes])))
        fed = [_bmm_tn(_chunks(do_b[:, lanes]), _chunks(qd_b[:, lanes])) for lanes in pairs]
        leaving = []
        for u, lanes in enumerate(pairs):
            ds, left = dst[u], [None] * nch
            for n in reversed(range(nch)):
                left[n] = ds
                ds = gt["decay"][n][:, lanes] * ds + jnp.where(bd, fed[u][n], 0.0)
            dst[u] = ds
            leaving.append(jnp.stack(left))
        dke_p, dlast_p = [], []
        for u, lanes in enumerate(pairs):
            entering, leaving_b = st_ref[u], leaving[u].astype(BF16)
            dke3 = _bmm_nn(_chunks(v_b[:, lanes]), leaving_b)
            dv_p[u] = dv_p[u] + _bmm_nt(_chunks(ke_b[:, lanes]), leaving_b).reshape(TH, LANE)
            dqd_p[u] = dqd_p[u] + _bmm_nn(_chunks(do_b[:, lanes]), entering.astype(BF16)).reshape(TH, LANE)
            dke_p.append(dke3.reshape(TH, LANE))
            dlast_p.append(jnp.sum(dke3 * _chunks(ke[:, lanes]), axis=1, keepdims=True)
                           + jnp.sum(leaving[u] * entering, axis=1, keepdims=True) * gt["decay"][:, :, lanes])
        cat = lambda parts: jnp.concatenate(parts, axis=-1)
        dv, dqd, dki, dke, dlast = cat(dv_p), cat(dqd_p), cat(dki_p), cat(dke_p), cat(dlast_p)
        dk = dki * gt["ei"] + dke * gt["ee"]
        dcum = dqd * qd - dki * ki - dke * ke
        dg = _chunk_cumsum(dcum, reverse=True) + jnp.broadcast_to(dlast, (nch, HG_BLOCK, dlast.shape[-1])).reshape(dcum.shape)
        sig = gt["sig"]
        df = dg / gt["f"] - dk
        dlb[...] += jnp.sum(df * (1.0 - sig), axis=0, keepdims=True)
        dh_ref[0] = (dqd * gt["e"]).astype(BF16)
        dh_ref[1] = ((df * (1.0 - lb)) * sig * (1.0 - sig)).astype(BF16)
        dh_ref[2] = dv.astype(BF16)

        @pl.when(step == nt - 1)
        def _():
            lb = _lower_bound(lbl_ref[...])
            da0 = dlb[...] * lb * (1.0 - lb)
            dlbl_ref[...] = jnp.concatenate([da0, -da0], axis=0)

    wide = HG_PAIRS * LANE
    col = lambda base: pl.BlockSpec((TH, wide), lambda p, i: (nt - 1 - i, base // wide + p))
    tile = pl.BlockSpec((TH, wide), lambda p, i: (nt - 1 - i, p))
    sds = jax.ShapeDtypeStruct
    return pl.pallas_call(
        body, name="hgrn_bwd", grid=(NH // 2 // HG_PAIRS, nt),
        in_specs=[col(P_HQ), col(P_HF), col(P_HI), pl.BlockSpec((2, wide), lambda p, i: (0, p)),
                  pl.BlockSpec((HG_PAIRS, nch, LANE, LANE), lambda p, i: (p, nt - 1 - i, 0, 0)), tile],
        out_specs=[pl.BlockSpec((3, TH, wide), lambda p, i: (0, nt - 1 - i, p)), pl.BlockSpec((2, wide), lambda p, i: (0, p))],
        out_shape=[sds((3, s, 512), BF16), sds((2, 512), F32)],
        scratch_shapes=[pltpu.VMEM((HG_PAIRS, LANE, LANE), F32), pltpu.VMEM((1, wide), F32)],
        compiler_params=_params(("parallel", "arbitrary")),
    )(proj, proj, proj, lbl, states, do_raw)


def _norm_rows_bwd(v, r, g, dn):
    dgv = dn * g
    return r * dgv - v * (r * r * r) * jnp.mean(v * dgv, axis=-1, keepdims=True)


def _qkv_bwd(proj, dq, dk, dvv, g_q, g_kv, w_uq_p, w_k_p, w_v_p, rc, rs1, rs2):
    s = proj.shape[0]
    head_q = QK_NOPE + QK_ROPE

    def body(cq_ref, ckv_ref, dq_ref, dk_ref, dv_ref, gq_ref, gkv_ref, wq_ref, wk_ref, wv_ref, c_ref, s1_ref, s2_ref,
             dcq_ref, dckv_ref, dkpe_ref, dwq_out, dwkv_out, dgq_ref, dgkv_ref, dwq_ref, dwk_ref, dwv_ref):
        @pl.when(pl.program_id(0) == 0)
        def _():
            for rf in (dwq_ref, dwk_ref, dwv_ref, dgq_ref, dgkv_ref):
                rf[...] = jnp.zeros_like(rf)

        c, s1, s2 = c_ref[...], s1_ref[...], s2_ref[...]
        cq, ckv = cq_ref[...], ckv_ref[...]
        gq, gkv = gq_ref[...], gkv_ref[...]
        cqn, rq = _norm_rows(cq, gq)
        ckvn, rkv = _norm_rows(ckv, gkv)
        cqn_b, ckvn_b = cqn.astype(BF16), ckvn.astype(BF16)
        dqf = jnp.concatenate([_rope_t(dq_ref[h], c, s1, s2) for h in range(NH)], axis=1).astype(BF16)
        dkf = jnp.concatenate([dk_ref[h] for h in range(NH)], axis=1).astype(BF16)
        dvf = jnp.concatenate([dv_ref[h] for h in range(NH)], axis=1).astype(BF16)
        dkpe = dk_ref[0]
        for h in range(1, NH):
            dkpe = dkpe + dk_ref[h]
        lane = lax.broadcasted_iota(jnp.int32, (TM, LANE), 1)
        dkpe = jnp.where((lane >= QK_NOPE) & (lane < QK_NOPE + QK_ROPE), dkpe, 0.0)
        dkpe_ref[...] = _rope_t(dkpe, c, s1, s2).astype(BF16)
        dcqn = _mm_nt(dqf, wq_ref[...])
        pair = lambda a, t: a[:, t * 2 * LANE:(t + 1) * 2 * LANE]
        dckvn = sum(_mm_nt(pair(dkf, t), wk_ref[t]) + _mm_nt(pair(dvf, t), wv_ref[t]) for t in range(N_CHIPS))
        dwq_ref[...] += _mm_tn(cqn_b, dqf)
        dwk_ref[...] += _mm_tn(ckvn_b, dkf)
        dwv_ref[...] += _mm_tn(ckvn_b, dvf)
        dgq_ref[...] += jnp.sum(dcqn * (cq * rq), axis=0, keepdims=True)
        dgkv_ref[...] += jnp.sum(dckvn * (ckv * rkv), axis=0, keepdims=True)
        dcq_ref[...] = _norm_rows_bwd(cq, rq, gq, dcqn).astype(BF16)
        dckv_ref[...] = _norm_rows_bwd(ckv, rkv, gkv, dckvn).astype(BF16)

        @pl.when(pl.program_id(0) == pl.num_programs(0) - 1)
        def _():
            blk = lambda ref, h: ref[:, h * LANE:(h + 1) * LANE]
            lane = lax.broadcasted_iota(jnp.int32, (Q_LORA, LANE), 1)
            for j in range(NH * head_q // LANE):
                h0, w0 = divmod(j * LANE, head_q)
                first = blk(dwq_ref, h0) if w0 == 0 else pltpu.roll(blk(dwq_ref, h0), LANE - w0, 1)
                second = pltpu.roll(blk(dwq_ref, h0 + 1), head_q - w0, 1)
                dwq_out[:, j * LANE:(j + 1) * LANE] = jnp.where(lane < head_q - w0, first, second).astype(BF16)
            lane = lax.broadcasted_iota(jnp.int32, (KV_LORA, LANE), 1)
            for h in range(NH):
                vals = blk(dwv_ref, h) if h % 2 else pltpu.roll(blk(dwv_ref, h), V_DIM, 1)
                both = jnp.where(lane < QK_NOPE, blk(dwk_ref, h), vals).astype(BF16)
                dwkv_out[h // 2, :, (h % 2) * LANE:(h % 2 + 1) * LANE] = both

    row = lambda w, j=0: pl.BlockSpec((TM, w), lambda i: (i, j))
    full = lambda a: pl.BlockSpec(a.shape, lambda i: (0,) * a.ndim)
    acc = lambda *shape: pl.BlockSpec(shape, lambda i: (0,) * len(shape))
    heads = pl.BlockSpec((NH, TM, LANE), lambda i: (0, i, 0))
    sds = jax.ShapeDtypeStruct
    return pl.pallas_call(
        body, name="qkv_bwd", grid=(s // TM,),
        in_specs=[row(Q_LORA, P_CQ // Q_LORA), row(KV_LORA, P_CKV // KV_LORA), heads, heads, heads,
                  full(g_q), full(g_kv), full(w_uq_p), full(w_k_p), full(w_v_p), row(LANE), row(LANE), row(LANE)],
        out_specs=[row(Q_LORA), row(KV_LORA), row(LANE), acc(Q_LORA, NH * head_q), acc(NH // 2, KV_LORA, 2 * LANE),
                   acc(1, Q_LORA), acc(1, KV_LORA)],
        out_shape=[sds((s, Q_LORA), BF16), sds((s, KV_LORA), BF16), sds((s, LANE), BF16), sds((Q_LORA, NH * head_q), BF16),
                   sds((NH // 2, KV_LORA, 2 * LANE), BF16), sds((1, Q_LORA), F32), sds((1, KV_LORA), F32)],
        scratch_shapes=[pltpu.VMEM((Q_LORA, D), F32), pltpu.VMEM((KV_LORA, D), F32), pltpu.VMEM((KV_LORA, D), F32)],
        compiler_params=_params(("arbitrary",)),
    )(proj, proj, dq, dk, dvv, g_q, g_kv, w_uq_p, w_k_p, w_v_p, rc, rs1, rs2)


def _front_bwd(x, dout, dmg, dga, dh3, dgb, dcq, dckv, dkpe, g_pre, w_in_t, w_kpe, token):
    s = x.shape[0]

    def body(x_ref, do_ref, dmg_ref, dga_ref, dh3_ref, dgb_ref, dcq_ref, dckv_ref, dkpe_ref, g_ref, w_ref, k_ref, token_ref,
             gx_ref, dg_ref):
        @pl.when(pl.program_id(0) == 0)
        def _():
            dg_ref[...] = jnp.zeros_like(dg_ref)

        xv, g = x_ref[...], g_ref[...]
        _, r = _norm_rows(xv, g)
        pieces = ((dmg_ref[...], O_MERGE), (dga_ref[...], O_GA), (dh3_ref[0], O_HQ), (dh3_ref[1], O_HF), (dh3_ref[2], O_HI),
                  (dgb_ref[...], O_GB), (dcq_ref[...], O_CQ), (dckv_ref[...], O_CKV))
        dh = _mm(dkpe_ref[...], k_ref[...])
        for piece, off in pieces:
            dh = dh + _mm(piece, w_ref[off:off + piece.shape[1], :])
        dg_ref[...] += jnp.sum(dh * (xv * r), axis=0, keepdims=True)
        gx_ref[...] = do_ref[...] + _norm_rows_bwd(xv, r, g, dh)

    row = lambda w: pl.BlockSpec((TM, w), lambda i: (i, 0))
    full = lambda a: pl.BlockSpec(a.shape, lambda i: (0,) * a.ndim)
    sds = jax.ShapeDtypeStruct
    return pl.pallas_call(
        body, name="front_bwd", grid=(s // TM,),
        in_specs=[row(D), row(D), row(2048), row(512), pl.BlockSpec((3, TM, 512), lambda i: (0, i, 0)), row(512), row(Q_LORA),
                  row(KV_LORA), row(LANE), full(g_pre), full(w_in_t), full(w_kpe), pl.BlockSpec(memory_space=pl.ANY)],
        out_specs=[row(D), pl.BlockSpec((1, D), lambda i: (0, 0))],
        out_shape=[sds((s, D), F32), sds((1, D), F32)],
        compiler_params=_params(("arbitrary",)),
    )(x, dout, dmg, dga, dh3, dgb, dcq, dckv, dkpe, g_pre, w_in_t, w_kpe, token)


TK_GRAD = 1024


QKV_ROWS = Q_LORA + KV_LORA + QK_ROPE
D_IN = O_MERGE + 2048


def _win_grad(h, pieces, offsets, name, whole=None):
    s = h.shape[0]
    n = len(pieces)
    new = whole is None
    shapes = [(p.shape[0] * p.shape[2], D) if p.ndim == 3 else (p.shape[1], D) for p in pieces]

    def body(h_ref, *refs):
        d_refs, whole_ref, scratch = refs[:n], refs[n if new else n + 1], refs[n + (1 if new else 2):]
        sums, rounded, sems = scratch[:n], scratch[n:2 * n], scratch[-1]

        @pl.when(pl.program_id(0) == 0)
        def _():
            for s_ref in sums:
                s_ref[...] = jnp.zeros_like(s_ref)

        hv = h_ref[...]
        for d_ref, s_ref in zip(d_refs, sums):
            if len(d_ref.shape) == 3:
                w = d_ref.shape[2]
                for k in range(d_ref.shape[0]):
                    s_ref[k * w:(k + 1) * w] += _mm_tn(d_ref[k], hv)
            else:
                s_ref[...] += _mm_tn(d_ref[...], hv)

        @pl.when(pl.program_id(0) == pl.num_programs(0) - 1)
        def _():
            copies = []
            for j, (s_ref, r_ref, row) in enumerate(zip(sums, rounded, offsets)):
                r_ref[...] = s_ref[...].astype(BF16)
                copies.append(pltpu.make_async_copy(r_ref, whole_ref.at[pl.ds(row, r_ref.shape[0])], sems.at[j]))
            if new:
                zeros = scratch[2 * n]
                zeros[...] = jnp.zeros_like(zeros)
                copies.append(pltpu.make_async_copy(zeros, whole_ref.at[pl.ds(0, QKV_ROWS)], sems.at[n]))
            for copy in copies:
                copy.start()
            for copy in copies:
                copy.wait()

    def in_spec(p):
        if p.ndim == 3:
            return pl.BlockSpec((p.shape[0], TK_GRAD, p.shape[2]), lambda kk: (0, kk, 0))
        return pl.BlockSpec((TK_GRAD, p.shape[1]), lambda kk: (kk, 0))

    anywhere = pl.BlockSpec(memory_space=pl.ANY)
    return pl.pallas_call(
        body, name=name, grid=(s // TK_GRAD,),
        in_specs=[pl.BlockSpec((TK_GRAD, D), lambda kk: (kk, 0))] + [in_spec(p) for p in pieces] + ([] if new else [anywhere]),
        out_specs=anywhere, out_shape=jax.ShapeDtypeStruct((D_IN, D), BF16),
        input_output_aliases={} if new else {n + 1: 0},
        scratch_shapes=[pltpu.VMEM(sh, F32) for sh in shapes] + [pltpu.VMEM(sh, BF16) for sh in shapes]
        + ([pltpu.VMEM((QKV_ROWS, D), BF16)] if new else []) + [pltpu.SemaphoreType.DMA((n + 1,))],
        compiler_params=_params(("arbitrary",)),
    )(h, *pieces, *([] if new else [whole]))


def _win_grad_qkv(h, dcq, dckv, dkpe, share):
    s = h.shape[0]
    half = QKV_ROWS // 2

    def body(h_ref, cq_ref, ckv_ref, kpe_ref, o_ref, *scratch):
        sums = scratch[0] if share else o_ref

        @pl.when(pl.program_id(0) == 0)
        def _():
            sums[...] = jnp.zeros_like(sums)

        hv = h_ref[...]
        g_cq = _mm_tn(cq_ref[...], hv)
        sums[0] += g_cq[:half]
        sums[1, 0:Q_LORA - half] += g_cq[half:]
        sums[1, Q_LORA - half:Q_LORA + KV_LORA - half] += _mm_tn(ckv_ref[...], hv)
        sums[1, Q_LORA + KV_LORA - half:] += _mm_tn(kpe_ref[...], hv)[QK_NOPE:QK_NOPE + QK_ROPE]

        if share:
            _, theirs, send_sem, recv_sem = scratch

            @pl.when(pl.program_id(0) == pl.num_programs(0) - 1)
            def _():
                c = lax.axis_index("c")
                copy = _remote(sums.at[1 - c], theirs, send_sem, recv_sem, 0, (lax.axis_index("x"), lax.axis_index("y"), 1 - c))
                copy.start()
                copy.wait()
                o_ref[...] = (sums[c] + theirs[...]).astype(BF16)

    rows = lambda a: pl.BlockSpec((TK_GRAD, a.shape[1]), lambda kk: (kk, 0))
    sem = pltpu.SemaphoreType.DMA((1,))
    shape, dtype = ((half, D), BF16) if share else ((2, half, D), F32)
    return pl.pallas_call(
        body, name="win_grad_qkv", grid=(s // TK_GRAD,), in_specs=[rows(h), rows(dcq), rows(dckv), rows(dkpe)],
        out_specs=pl.BlockSpec(shape, lambda kk: (0,) * len(shape)), out_shape=jax.ShapeDtypeStruct(shape, dtype),
        scratch_shapes=[pltpu.VMEM((2, half, D), F32), pltpu.VMEM((half, D), F32), sem, sem] if share else [],
        compiler_params=_params(("arbitrary",)),
    )(h, dcq, dckv, dkpe)


def _pad_wuq(w_uq):
    rows = w_uq.shape[0]
    w = w_uq.reshape(rows, NH, QK_NOPE + QK_ROPE)
    return jnp.pad(w, ((0, 0), (0, 0), (0, LANE - QK_NOPE - QK_ROPE))).reshape(rows, NH * LANE)


def _pad_wukv(w_ukv):
    heads = w_ukv.shape[1] // (QK_NOPE + V_DIM)
    w = w_ukv.reshape(KV_LORA, heads, QK_NOPE + V_DIM)
    w_k = jnp.pad(w[:, :, :QK_NOPE], ((0, 0), (0, 0), (0, LANE - QK_NOPE))).reshape(KV_LORA, heads * LANE)
    wv = w[:, :, QK_NOPE:].reshape(KV_LORA, heads // 2, 2, 1, V_DIM)
    eye = jnp.eye(2, dtype=w.dtype).reshape(1, 1, 2, 2, 1)
    return w_k, (wv * eye).reshape(KV_LORA, heads * LANE)


def _local_step(x, tgt, g_pre, w_in_t, b_gate, g_q, g_kv, lb_logits, g_hgrn, g_post, weights, exchange=None):
    s = x.shape[0]
    w_kpe = _kpe_block(w_in_t)
    rc, rs1, rs2 = _rope_tables(s)
    g_hg = jnp.tile(g_hgrn, (1, NH))

    proj, h = _front_fwd(x, g_pre, w_in_t, w_kpe, weights.tokens)
    w_uq_p, w_k_p, w_v_p = weights.qkv(h)
    q, k, vv = _qkv_fwd(proj, g_q, g_kv, w_uq_p, w_k_p, w_v_p, rc, rs1, rs2)
    attn, lse = _attn_fwd(q, k, vv)
    o_raw, states = _hgrn_fwd(proj, lb_logits)
    wa, wb, w_out = weights.mid(o_raw)
    (loss, dout, dattn, dga, dor, dgb, dmg, d_wout, d_wa, d_wb, d_gpost, d_bgate, d_ghg) = _mid(
        proj, attn, o_raw, x, tgt, g_hg, b_gate, g_post, wa, wb, w_out)
    d_win = _win_grad(h, [dmg, dga, dgb], [O_MERGE, O_GA, O_GB], "win_grad_mid")
    dh3, d_lbl = _hgrn_bwd(proj, lb_logits, states, dor)
    d_win = _win_grad(h, [dh3], [O_HQ], "win_grad_hgrn", d_win)
    early = dict(w_in=d_win, w_branch_a=d_wa, w_branch_b=d_wb, w_out=d_wout)
    token = exchange.start_early(early) if exchange else jnp.zeros((8, LANE), F32)
    dq, dk, dvv = _attn_bwd(q, k, vv, attn, dattn, lse, token)
    dcq, dckv, dkpe, d_wuq, d_wukv, d_gq, d_gkv = _qkv_bwd(proj, dq, dk, dvv, g_q, g_kv, w_uq_p, w_k_p, w_v_p, rc, rs1, rs2)
    late = dict(w_in_qkv=_win_grad_qkv(h, dcq, dckv, dkpe, share=exchange is not None), w_uq=d_wuq, w_ukv=d_wukv)
    token = exchange.start_late(late) if exchange else jnp.zeros((8, LANE), F32)
    grad_x, d_gpre = _front_bwd(x, dout, dmg, dga, dh3, dgb, dcq, dckv, dkpe, g_pre, w_in_t, w_kpe, token)
    vec_grads = dict(g_pre=d_gpre, b_gate=d_bgate, g_q=d_gq, g_kv=d_gkv, lb_logits=d_lbl, g_hgrn=d_ghg, g_post=d_gpost)
    return loss, grad_x, dict(early, **late), vec_grads


SHARD_SHAPES = (("w_in", (1416, 1024)), ("w_uq", (192, 768)), ("w_ukv", (256, 256)), ("w_branch_a", (512, 256)),
                ("w_branch_b", (512, 256)), ("w_out", (256, 1024)))
BIG = tuple(n for n, _ in SHARD_SHAPES)
ROW_SHARDED = ("w_in", "w_uq", "w_out")
N_CHIPS = 4
W_IN_FORWARD_CUT = 704


def _to_block(name, a):
    return a[0].T if name == "w_in" else a[0]


def _from_block(name, a):
    return a.T[None] if name == "w_in" else a[None]
VEC_ROWS = (("g_pre", 0, 1024), ("b_gate", 1, 2048), ("g_q", 2, 768), ("g_kv", 3, 256), ("g_hgrn", 6, 64), ("g_post", 7, 1024))
VEC_LB_ROW = 4
VEC_SHAPE = (8, 2048)


def _split_by_chip(name, g):
    a, b = dict(SHARD_SHAPES)[name]
    return g.reshape(N_CHIPS, a, b) if name in ROW_SHARDED else g.reshape(a, N_CHIPS, b).transpose(1, 0, 2)


def _join_chips(name, w):
    a, b = dict(SHARD_SHAPES)[name]
    return w.reshape(N_CHIPS * a, b) if name in ROW_SHARDED else w.transpose(1, 0, 2).reshape(a, N_CHIPS * b)


MESH = pl.DeviceIdType.MESH
HBM = pl.BlockSpec(memory_space=pltpu.HBM)


def _mesh_place():
    x, y, c = lax.axis_index("x"), lax.axis_index("y"), lax.axis_index("c")
    return x, y, c, 2 * x + y, [(1 - x, y), (x, 1 - y), (1 - x, 1 - y)]


def _remote(src, dst, send_sems, recv_sems, k, to):
    return pltpu.make_async_remote_copy(src_ref=src, dst_ref=dst, send_sem=send_sems.at[k], recv_sem=recv_sems.at[k],
                                        device_id=to, device_id_type=MESH)


def _gather_w_in(shard):
    a, b = shard.shape
    cut = W_IN_FORWARD_CUT

    def body(src, out, ici_send, ici_recv, d2d_send, d2d_recv, local_sem):
        x, y, c = lax.axis_index("x"), lax.axis_index("y"), lax.axis_index("c")
        me, xn, yn, dg = 2 * x + y, 2 * (1 - x) + y, 2 * x + (1 - y), 2 * (1 - x) + (1 - y)
        to_x, to_y, sibling = (1 - x, y, c), (x, 1 - y, c), (x, y, 1 - c)
        first, rest = pl.ds(0, cut), pl.ds(cut, a - cut)

        whole = lambda ref, which: ref.at[:, pl.ds(pl.multiple_of(which * (b // 2), b // 2), b // 2)]
        own = pltpu.make_async_copy(src, out.at[me], local_sem)
        own.start()
        sends = [_remote(whole(src, c), whole(out.at[me], c), ici_send, ici_recv, 0, to_x),
                 _remote(whole(src, c), whole(out.at[me], c), ici_send, ici_recv, 1, to_y)]
        for cp in sends:
            cp.start()

        def landed(slot, rows, k, d2d_k, src_dev):
            piece = whole(out.at[slot], c) if rows is None else out.at[slot].at[rows, pl.ds(pl.multiple_of(c * (b // 2), b // 2), b // 2)]
            _remote(piece, piece, ici_send, ici_recv, k, src_dev).wait_recv()
            cp = _remote(piece, piece, d2d_send, d2d_recv, d2d_k, sibling)
            cp.start()
            sends.append(cp)
            return piece

        def pass_on(slot, rows, k, to):
            piece = out.at[slot].at[rows, pl.ds(pl.multiple_of(c * (b // 2), b // 2), b // 2)]
            cp = _remote(piece, piece, ici_send, ici_recv, k, to)
            cp.start()
            sends.append(cp)

        landed(xn, None, 0, 0, to_x)
        pass_on(xn, first, 2, to_y)
        landed(yn, None, 1, 1, to_y)
        pass_on(yn, rest, 3, to_x)
        landed(dg, first, 2, 2, to_y)
        landed(dg, rest, 3, 3, to_x)
        other = pl.ds(pl.multiple_of((1 - c) * (b // 2), b // 2), b // 2)
        for d2d_k, (slot, rows) in enumerate(((xn, None), (yn, None), (dg, first), (dg, rest))):
            piece = out.at[slot].at[:, other] if rows is None else out.at[slot].at[rows, other]
            _remote(piece, piece, d2d_send, d2d_recv, d2d_k, sibling).wait_recv()
        for cp in sends:
            cp.wait_send()
        own.wait()

    sems = pltpu.SemaphoreType.DMA((4,))
    return pl.pallas_call(
        body, name="gather_w_in", in_specs=[HBM], out_specs=HBM,
        out_shape=jax.ShapeDtypeStruct((N_CHIPS, a, b), shard.dtype),
        scratch_shapes=[sems, sems, sems, sems, pltpu.SemaphoreType.DMA],
        compiler_params=pltpu.CompilerParams(has_side_effects=True),
    )(shard)


SEM =pl.BlockSpec(memory_space=pltpu.SEMAPHORE)
DATAFLOW = pltpu.SideEffectType.DATAFLOW_SIDE_EFFECTING


def _exchange_copies(srcs, to_first, src_refs, land_refs, send_sems, recv_sems):
    x, y, c, me, chips = _mesh_place()
    n = len(srcs)
    sends, recvs = [], []
    for k in range(n):
        if k in to_first:
            base = 3 * n + 4 * to_first.index(k)
            sends.append((me != 0, pltpu.make_async_remote_copy(
                src_ref=src_refs[k], dst_ref=land_refs[k].at[me], send_sem=send_sems.at[base], recv_sem=recv_sems.at[base + me],
                device_id=(0, 0, c), device_id_type=MESH)))
            for s in range(1, N_CHIPS):
                recvs.append((me == 0, pltpu.make_async_remote_copy(
                    src_ref=src_refs[k], dst_ref=land_refs[k].at[s], send_sem=send_sems.at[base], recv_sem=recv_sems.at[base + s],
                    device_id=(s // 2, s % 2, c), device_id_type=MESH)))
        else:
            slab = (lambda t, k=k: src_refs[k]) if srcs[k].ndim == 2 else (lambda t, k=k: src_refs[k].at[t])
            for j, (px, py) in enumerate(chips):
                sends.append((None, _remote(slab(2 * px + py), land_refs[k].at[me], send_sems, recv_sems, 3 * k + j, (px, py, c))))
                recvs.append((None, _remote(slab(me), land_refs[k].at[2 * px + py], send_sems, recv_sems, 3 * k + j, (px, py, c))))
    return sends, recvs


def _when(pred, fn):
    if pred is None:
        fn()
    else:
        pl.when(pred)(fn)


def _exchange_start(srcs, to_first, name, after=None):
    n = len(srcs)
    n_sems = 3 * n + 4 * len(to_first)
    lands = [lax.empty((N_CHIPS,) + s.shape[-2:], s.dtype) for s in srcs]
    extra = [] if after is None else [after]

    def body(*refs):
        src_refs, land_refs = refs[:n], refs[n:2 * n]
        send_sems, recv_sems, token = refs[2 * n + len(extra)], refs[2 * n + len(extra) + 1], refs[-1]
        sends, _ = _exchange_copies(srcs, to_first, src_refs, land_refs, send_sems, recv_sems)
        for pred, cp in sends:
            _when(pred, cp.start)
        token[...] = jnp.zeros_like(token)

    hbm = lambda a: pltpu.HBM(a.shape, a.dtype)
    res = pl.pallas_call(
        body, name=name,
        out_shape=[pltpu.SemaphoreType.DMA((n_sems,)), pltpu.SemaphoreType.DMA((n_sems,))] + [hbm(a) for a in srcs + lands]
        + [jax.ShapeDtypeStruct((8, LANE), F32)],
        in_specs=[HBM] * (2 * n) + [pl.BlockSpec(memory_space=pl.ANY)] * len(extra),
        out_specs=[SEM, SEM] + [HBM] * (2 * n) + [pl.BlockSpec(memory_space=pltpu.VMEM)],
        input_output_aliases={i: 2 + i for i in range(2 * n)},
        compiler_params=pltpu.CompilerParams(has_side_effects=DATAFLOW),
    )(*[pltpu.with_memory_space_constraint(a, pltpu.HBM) for a in srcs + lands], *extra)
    return res[:-1], res[-1]


def _exchange_wait(srcs, to_first, started, after, name):
    n = len(srcs)
    send_sems, recv_sems, thru = started[0], started[1], started[2:]

    def body(*refs):
        src_refs, land_refs, send_ref, recv_ref = refs[:n], refs[n:2 * n], refs[2 * n], refs[2 * n + 1]
        sends, recvs = _exchange_copies(srcs, to_first, src_refs, land_refs, send_ref, recv_ref)
        for pred, cp in sends:
            _when(pred, cp.wait_send)
        for pred, cp in recvs:
            _when(pred, cp.wait_recv)

    res = pl.pallas_call(
        body, name=name, out_shape=[pltpu.HBM(a.shape, a.dtype) for a in thru],
        in_specs=[HBM] * (2 * n) + [SEM, SEM, pl.BlockSpec(memory_space=pl.ANY)], out_specs=[HBM] * (2 * n),
        input_output_aliases={i: i for i in range(2 * n)},
        compiler_params=pltpu.CompilerParams(has_side_effects=DATAFLOW),
    )(*thru, send_sems, recv_sems, after)
    return res[:n], res[n:]


ROW_TILE = 256
COL_TILE = 512


def _block_tiling(a, b):
    if a <= ROW_TILE or a % ROW_TILE == 0:
        ta = min(a, ROW_TILE)
        return a // ta, (ta, b), lambda i: (i, 0)
    return b // COL_TILE, (a, COL_TILE), lambda i: (0, i)


def _sum_share_small(lands, owns, name, after=None):
    n = len(lands)
    extra = [] if after is None else [after]

    def body(*refs):
        p_refs, own_refs = refs[:n], refs[n:2 * n]
        o_refs, mine, theirs = (refs[j * n + len(extra):(j + 1) * n + len(extra)] for j in (2, 3, 4))
        send_sems, recv_sems = refs[5 * n + len(extra):]
        me = 2 * lax.axis_index("x") + lax.axis_index("y")
        sibling = (lax.axis_index("x"), lax.axis_index("y"), 1 - lax.axis_index("c"))
        copies = [_remote(mine[k], theirs[k], send_sems, recv_sems, k, sibling) for k in range(n)]
        for p_ref, own_ref, mine_ref, copy in zip(p_refs, own_refs, mine, copies):
            own = (own_ref[me] if len(own_ref.shape) == 3 else own_ref[...]).astype(F32)
            slot = lambda t: jnp.where(me == t, own, p_ref[t].astype(F32))
            mine_ref[...] = ((slot(0) + slot(1)) + slot(2)) + slot(3)
            copy.start()
        for o_ref, mine_ref, theirs_ref, copy in zip(o_refs, mine, theirs, copies):
            copy.wait()
            o_ref[...] = mine_ref[...] + theirs_ref[...]

    vmem = pl.BlockSpec(memory_space=pltpu.VMEM)
    kept = [pltpu.VMEM(land.shape[1:], F32) for land in lands]
    sems = pltpu.SemaphoreType.DMA((n,))
    return pl.pallas_call(
        body, name=name, in_specs=[vmem] * (2 * n) + [pl.BlockSpec(memory_space=pl.ANY)] * len(extra), out_specs=[vmem] * n,
        out_shape=[jax.ShapeDtypeStruct(land.shape[1:], F32) for land in lands], scratch_shapes=kept + kept + [sems, sems],
        compiler_params=_params(()),
    )(*lands, *owns, *extra)


def _adamw_small(grads, states, name):
    n = len(grads)

    def body(*refs):
        ins, outs = refs[:4 * n], refs[4 * n:]
        for k in range(n):
            g_ref, w_ref, m_ref, v_ref = ins[4 * k:4 * k + 4]
            g = g_ref[...]
            outs[4 * k][...] = g
            outs[4 * k + 1][...], outs[4 * k + 2][...], outs[4 * k + 3][...] = _adamw_math(g, w_ref[...], m_ref[...], v_ref[...])

    args = [t for k in range(n) for t in (grads[k], *states[k])]
    res = pl.pallas_call(body, name=name, out_shape=[jax.ShapeDtypeStruct(grads[k].shape, F32) for k in range(n) for _ in range(4)],
                         compiler_params=_params(()))(*args)
    return [tuple(res[4 * k:4 * k + 4]) for k in range(n)]


class _LaterWeights:
    MID = ("w_branch_a", "w_branch_b", "w_out")

    def __init__(self, blocks, after):
        self.qkv_blocks = [_pad_wuq(blocks["w_uq"]), *_pad_wukv(blocks["w_ukv"])]
        self.mid_blocks = [blocks[n] for n in self.MID]
        self.qkv_started, t1 = _exchange_start(self.qkv_blocks, (), "weights_qkv_start", after)
        self.mid_started, t2 = _exchange_start(self.mid_blocks, (), "weights_mid_start", after)
        self.tokens = [t1, t2]

    @staticmethod
    def _whole(blocks, joined, started, after, name):
        _, landed = _exchange_wait(blocks, (), started, after, name)
        me = 2 * lax.axis_index("x") + lax.axis_index("y")
        out = []
        for block, land, join in zip(blocks, landed, joined):
            w = lax.dynamic_update_index_in_dim(land, block, me, 0)
            out.append(w.reshape(N_CHIPS * block.shape[0], block.shape[1]) if join else w)
        return out

    def qkv(self, after):
        return self._whole(self.qkv_blocks, (True, False, False), self.qkv_started, after, "weights_qkv_wait")

    def mid(self, after):
        return self._whole(self.mid_blocks, (False, False, True), self.mid_started, after, "weights_mid_wait")


class _GradExchange:
    EARLY = ("w_in", "w_branch_a", "w_branch_b", "w_out")
    LATE = ("w_uq", "w_ukv")

    def __init__(self, state):
        self.state = state
        self.outs = {}

    def start_early(self, g):
        self.early = [(g[n] if g[n].ndim == 3 else _split_by_chip(n, g[n])).astype(BF16) for n in self.EARLY]
        self.early_started, token = _exchange_start(self.early, (), "grads_early_start")
        return token

    def start_late(self, g):
        self.early, self.early_landed = _exchange_wait(self.early, (), self.early_started, g["w_uq"], "grads_early_wait")
        self.late = [_split_by_chip("w_uq", g["w_uq"]), g["w_ukv"], g["w_in_qkv"]]
        self.late_started, token = _exchange_start(self.late, (2,), "grads_late_start")
        names = self.EARLY[1:]
        grads = _sum_share_small(self.early_landed[1:], self.early[1:], "sum_early", after=token)
        for n, out in zip(names, _adamw_small(grads, [self.state[n] for n in names], "adamw_early")):
            self.outs[n] = out
        return self.outs[names[-1]][0]

    def finish(self, after):
        late, late_landed = _exchange_wait(self.late, (2,), self.late_started, after, "grads_late_wait")
        grad = _sum_exchange(self.early_landed[0], self.early[0], late_landed[2], late[2], "sum_w_in")
        self.outs["w_in"] = _adamw(grad, *self.state["w_in"], "adamw_w_in")
        return list(late[:2]), list(late_landed[:2])


def _adamw_math(g, w, m, v):
    nm = ADAM_B1 * m + (1.0 - ADAM_B1) * g
    nv = ADAM_B2 * v + (1.0 - ADAM_B2) * (g * g)
    m_hat = nm / (1.0 - ADAM_B1 ** ADAM_STEP)
    v_hat = nv / (1.0 - ADAM_B2 ** ADAM_STEP)
    return -ADAM_LR * (m_hat / (jnp.sqrt(v_hat) + ADAM_EPS) + ADAM_WD * w), nm, nv


def _sum_exchange(land, own, first_land, first_own, name):
    _, a, b = land.shape
    steps, tile, at = _block_tiling(a, b)
    assert tile[0] == a, "the extra rows need whole columns in a step"
    r = first_own.shape[0]

    def body(me_ref, p_ref, own_ref, fp_ref, fo_ref, g_ref, mine_s, theirs_s, send_sems, recv_sems):
        pass_, i = pl.program_id(0), pl.program_id(1)
        me, c = me_ref[0], lax.axis_index("c")
        sibling = (lax.axis_index("x"), lax.axis_index("y"), 1 - c)
        copy = _remote(mine_s.at[i], theirs_s.at[i], send_sems, recv_sems, i, sibling)

        @pl.when(pass_ == 0)
        def _():
            own = own_ref[...].astype(F32)
            slot = lambda t: jnp.where(me == t, own, p_ref[t].astype(F32))
            mine_s[i] = ((slot(0) + slot(1)) + slot(2)) + slot(3)

            @pl.when(me == 0)
            def _():
                f = lambda t: fp_ref[t].astype(F32)
                rows = pl.ds(pl.multiple_of(c * r, 8), r)
                mine_s[i, rows, :] += ((fo_ref[...].astype(F32) + f(1)) + f(2)) + f(3)

            copy.start()

        @pl.when(pass_ == 1)
        def _():
            copy.wait()
            g_ref[...] = mine_s[i] + theirs_s[i]

    first = lambda p, i: i * (1 - p) + (steps - 1) * p
    in_specs = [pl.BlockSpec((N_CHIPS,) + tile, lambda p, i, me: (0,) + at(first(p, i))),
                pl.BlockSpec((None,) + tile, lambda p, i, me: (me[0],) + at(first(p, i))),
                pl.BlockSpec((N_CHIPS, r, tile[1]), lambda p, i, me: (0,) + at(first(p, i))),
                pl.BlockSpec((r, tile[1]), lambda p, i, me: at(first(p, i)))]
    me = jnp.reshape(2 * lax.axis_index("x") + lax.axis_index("y"), (1,)).astype(jnp.int32)
    kept = pltpu.VMEM((steps,) + tile, F32)
    sems = pltpu.SemaphoreType.DMA((steps,))
    return pl.pallas_call(
        body, name=name,
        grid_spec=pltpu.PrefetchScalarGridSpec(num_scalar_prefetch=1, grid=(2, steps), in_specs=in_specs,
                                               out_specs=pl.BlockSpec(tile, lambda p, i, me: at(i * p)),
                                               scratch_shapes=[kept, kept, sems, sems]),
        out_shape=jax.ShapeDtypeStruct((a, b), F32),
        compiler_params=_params(("arbitrary", "arbitrary")),
    )(me, land, own, first_land, first_own)


def _adamw(g, w, m, v, name):
    a, b = g.shape
    steps, tile, at = _block_tiling(a, b)

    def body(g_ref, w_ref, m_ref, v_ref, go_ref, d_ref, nm_ref, nv_ref):
        g = g_ref[...]
        go_ref[...] = g
        d_ref[...], nm_ref[...], nv_ref[...] = _adamw_math(g, w_ref[...], m_ref[...], v_ref[...])

    spec = pl.BlockSpec(tile, at)
    sds = jax.ShapeDtypeStruct((a, b), F32)
    return pl.pallas_call(
        body, name=name, grid=(steps,), in_specs=[spec] * 4, out_specs=[spec] * 4, out_shape=[sds] * 4,
        compiler_params=_params(("parallel",)),
    )(g, w, m, v)


LOSS_AT = (2, 1024)


def _vec_pack(vg, loss):
    names = [n for n, _, _ in VEC_ROWS]

    def body(*refs):
        o_ref = refs[-1]
        lb_ref, loss_ref = refs[len(names)], refs[len(names) + 1]
        o_ref[...] = jnp.zeros_like(o_ref)
        o_ref[LOSS_AT[0]:LOSS_AT[0] + 1, LOSS_AT[1]:LOSS_AT[1] + LANE] = jnp.broadcast_to(loss_ref[...], (1, LANE))
        for (name, row, size), ref in zip(VEC_ROWS, refs):
            if name == "g_hgrn":
                r = lax.broadcasted_iota(jnp.int32, (NH * V_DIM, LANE), 0)
                c = lax.broadcasted_iota(jnp.int32, (NH * V_DIM, LANE), 1)
                fold = ((r % V_DIM) == c).astype(F32)
                o_ref[row:row + 1, 0:LANE] = jnp.dot(ref[...], fold, precision=HIGHEST, preferred_element_type=F32)
            else:
                o_ref[row:row + 1, 0:size] = ref[...]
        o_ref[VEC_LB_ROW:VEC_LB_ROW + 2, 0:512] = lb_ref[...]

    return pl.pallas_call(body, name="vec_pack", out_shape=jax.ShapeDtypeStruct(VEC_SHAPE, F32))(
        *[vg[n] for n in names], vg["lb_logits"], loss)


def _adamw_vec(block, w, m, v):
    names = [n for n, _, _ in VEC_ROWS] + ["lb_logits"]
    k = len(names)

    def body(g_ref, *refs):
        ins, outs = refs[:3 * k], refs[3 * k:]
        outs[-1][...] = g_ref[LOSS_AT[0]:LOSS_AT[0] + 1, LOSS_AT[1]:LOSS_AT[1] + LANE]
        for i, name in enumerate(names):
            if name == "lb_logits":
                rows, cols = slice(VEC_LB_ROW, VEC_LB_ROW + 2), slice(0, 512)
            else:
                _, row, size = VEC_ROWS[i]
                rows, cols = slice(row, row + 1), slice(0, size)
            g = g_ref[rows, cols]
            d, nm, nv = _adamw_math(g, ins[i][...], ins[k + i][...], ins[2 * k + i][...])
            for o_ref, val in zip(outs[4 * i:4 * i + 4], (g, d, nm, nv)):
                o_ref[...] = val

    shapes = [jax.ShapeDtypeStruct(w[n].shape, F32) for n in names for _ in range(4)] + [jax.ShapeDtypeStruct((1, LANE), F32)]
    res = pl.pallas_call(body, name="adamw_vec", out_shape=shapes)(
        block, *[w[n] for n in names], *[m[n] for n in names], *[v[n] for n in names])
    return [{n: res[4 * i + j] for i, n in enumerate(names)} for j in range(4)], res[-1]


WEIGHTS = ("g_pre", "w_in", "b_gate", "g_q", "w_uq", "g_kv", "w_ukv", "lb_logits", "g_hgrn", "w_branch_a", "w_branch_b", "w_out", "g_post")


def kernel(x, g_pre, w_in, b_gate, g_q, w_uq, g_kv, w_ukv, lb_logits, g_hgrn, w_branch_a, w_branch_b, w_out, g_post, loss_target, m_g_pre, m_w_in, m_b_gate, m_g_q, m_w_uq, m_g_kv, m_w_ukv, m_lb_logits, m_g_hgrn, m_w_branch_a, m_w_branch_b, m_w_out, m_g_post, v_g_pre, v_w_in, v_b_gate, v_g_q, v_w_uq, v_g_kv, v_w_ukv, v_lb_logits, v_g_hgrn, v_w_branch_a, v_w_branch_b, v_w_out, v_g_post):
    w = dict(g_pre=g_pre, w_in=w_in, b_gate=b_gate, g_q=g_q, w_uq=w_uq, g_kv=g_kv, w_ukv=w_ukv, lb_logits=lb_logits, g_hgrn=g_hgrn,
             w_branch_a=w_branch_a, w_branch_b=w_branch_b, w_out=w_out, g_post=g_post)
    m = dict(g_pre=m_g_pre, w_in=m_w_in, b_gate=m_b_gate, g_q=m_g_q, w_uq=m_w_uq, g_kv=m_g_kv, w_ukv=m_w_ukv, lb_logits=m_lb_logits,
             g_hgrn=m_g_hgrn, w_branch_a=m_w_branch_a, w_branch_b=m_w_branch_b, w_out=m_w_out, g_post=m_g_post)
    v = dict(g_pre=v_g_pre, w_in=v_w_in, b_gate=v_b_gate, g_q=v_g_q, w_uq=v_w_uq, g_kv=v_g_kv, w_ukv=v_w_ukv, lb_logits=v_lb_logits,
             g_hgrn=v_g_hgrn, w_branch_a=v_w_branch_a, w_branch_b=v_w_branch_b, w_out=v_w_out, g_post=v_g_post)
    blocks = {n: _to_block(n, w[n]).astype(BF16) for n in BIG}
    w_in_all = _gather_w_in(blocks["w_in"])
    weights = _LaterWeights(blocks, w_in_all)
    state = {n: [_to_block(n, t[n]) for t in (w, m, v)] for n in BIG}
    exchange = _GradExchange(state)
    loss, grad_x, _, vec_grads = _local_step(
        x[0], loss_target[0], g_pre, _join_chips("w_in", w_in_all), b_gate, g_q, g_kv, lb_logits, g_hgrn, g_post, weights, exchange)
    vec = _vec_pack(vec_grads, loss)
    vec_started, token = _exchange_start([vec], (), "vec_start")
    late_sent, late_landed = exchange.finish(token)
    (vec,), (vec_landed,) = _exchange_wait([vec], (), vec_started, exchange.outs["w_in"][0], "vec_wait")
    grads = _sum_share_small(late_landed + [vec_landed], late_sent + [vec], "sum_late")
    done = dict(exchange.outs)
    done.update(zip(exchange.LATE, _adamw_small(grads[:-1], [state[n] for n in exchange.LATE], "adamw_late")))
    outs = [{}, {}, {}, {}]
    for n in BIG:
        for o, val in zip(outs, done[n]):
            o[n] = _from_block(n, val)
    vec_outs, total = _adamw_vec(grads[-1], w, m, v)
    for o, vals in zip(outs, vec_outs):
        o.update(vals)
    return (total[0, 0], grad_x[None], *[o[n] for o in outs for n in WEIGHTS])
```

```python
import math

import numpy as np
import jax
import jax.numpy as jnp
from jax import lax
from jax.experimental import pallas as pl
from jax.experimental.pallas import tpu as pltpu

F32 = jnp.float32
BF16 = jnp.bfloat16
HIGHEST = lax.Precision.HIGHEST

D = 1024
NH = 8
QK_NOPE, QK_ROPE, V_DIM = 64, 32, 64
Q_LORA, KV_LORA = 768, 256
CHUNK = 64
HG_BLOCK = 32
EPS = 1e-6
LANE = 128
P_MERGE, P_GA, P_HQ, P_HF, P_HI, P_GB, P_CQ, P_CKV, P_KPE = 0, 2048, 2560, 3072, 3584, 4096, 4608, 5376, 5632
D_P = 5760
O_CQ, O_CKV, O_KPE, O_GA, O_HQ, O_HF, O_HI, O_GB, O_MERGE = 0, 768, 1024, 1056, 1568, 2080, 2592, 3104, 3616

TM = 512
TM_MID = 256
TQ = 1024
ONES_LANE = (LANE - 1, 0)
TH = 256
HG_PAIRS = 4
VMEM_LIMIT = 56 * 1024 * 1024

ADAM_LR, ADAM_B1, ADAM_B2, ADAM_EPS, ADAM_WD, ADAM_STEP = 0.001, 0.9, 0.999, 1e-08, 0.01, 10

NT_DIMS = (((1,), (1,)), ((), ()))
TN_DIMS = (((0,), (0,)), ((), ()))


def _params(sem):
    return pltpu.CompilerParams(dimension_semantics=sem, vmem_limit_bytes=VMEM_LIMIT)


def _mm(a, b):
    return jnp.dot(a, b, preferred_element_type=F32)


def _mm_nt(a, b):
    return lax.dot_general(a, b, NT_DIMS, preferred_element_type=F32)


def _mm_tn(a, b):
    return lax.dot_general(a, b, TN_DIMS, preferred_element_type=F32)


def _sigmoid(z):
    return jax.nn.sigmoid(z)


def _rope(v, c, s1, s2):
    return v * c + pltpu.roll(v, 112, 1) * s1 + pltpu.roll(v, 16, 1) * s2


def _rope_t(dy, c, s1, s2):
    return dy * c + pltpu.roll(dy * s1, 16, 1) + pltpu.roll(dy * s2, 112, 1)


def _rope_tables(s):
    f32 = np.float32
    inv = f32(10000.0) ** (-np.arange(0, QK_ROPE, 2, dtype=f32) / f32(QK_ROPE))
    ang = np.arange(s, dtype=f32)[:, None] * inv[None, :]
    cos, sin = np.cos(ang).astype(f32), np.sin(ang).astype(f32)
    z64, z32, o64, o32 = np.zeros((s, 64), f32), np.zeros((s, 32), f32), np.ones((s, 64), f32), np.ones((s, 32), f32)
    z16 = np.zeros((s, 16), f32)
    c = np.concatenate([o64, cos, cos, o32], axis=1)
    s1 = np.concatenate([z64, -sin, z16, z32], axis=1)
    s2 = np.concatenate([z64, z16, sin, z32], axis=1)
    return jnp.asarray(c), jnp.asarray(s1), jnp.asarray(s2)


W_IN_RUNS = ((O_MERGE, 2048, P_MERGE), (O_GA, O_MERGE - O_GA, P_GA), (O_CQ, O_KPE - O_CQ, P_CQ))


def _kpe_block(w_in_t):
    z = lambda n: jnp.zeros((n, w_in_t.shape[1]), w_in_t.dtype)
    return jnp.concatenate([z(64), w_in_t[O_KPE:O_KPE + QK_ROPE], z(32)], axis=0)


def _front_fwd(x, g_pre, w_in_t, w_kpe, tokens=()):
    s = x.shape[0]
    tokens = list(tokens)

    def body(x_ref, g_ref, w_ref, k_ref, *refs):
        o_ref, h_ref = refs[len(tokens):]
        xv = x_ref[...]
        r = lax.rsqrt(jnp.mean(xv * xv, axis=-1, keepdims=True) + EPS)
        h = ((xv * r) * g_ref[...]).astype(BF16)
        h_ref[...] = h
        for row, rows, col in W_IN_RUNS:
            o_ref[:, col:col + rows] = _mm_nt(h, w_ref[row:row + rows, :])
        o_ref[:, P_KPE:P_KPE + LANE] = _mm_nt(h, k_ref[...])

    full = lambda a: pl.BlockSpec(a.shape, lambda i: (0,) * a.ndim)
    return pl.pallas_call(
        body, name="front_fwd", grid=(s // TM,),
        in_specs=[pl.BlockSpec((TM, D), lambda i: (i, 0)), pl.BlockSpec((1, D), lambda i: (0, 0)), full(w_in_t), full(w_kpe)]
        + [pl.BlockSpec((8, LANE), lambda i: (0, 0))] * len(tokens),
        out_specs=[pl.BlockSpec((TM, D_P), lambda i: (i, 0)), pl.BlockSpec((TM, D), lambda i: (i, 0))],
        out_shape=[jax.ShapeDtypeStruct((s, D_P), F32), jax.ShapeDtypeStruct((s, D), BF16)],
        compiler_params=_params(("parallel",)),
    )(x, g_pre, w_in_t, w_kpe, *tokens)


def _norm_rows(v, g):
    r = lax.rsqrt(jnp.mean(v * v, axis=-1, keepdims=True) + EPS)
    return (v * r) * g, r


def _qkv_fwd(proj, g_q, g_kv, w_uq_p, w_k_p, w_v_p, rc, rs1, rs2):
    s = proj.shape[0]

    def body(cq_ref, ckv_ref, kpe_ref, gq_ref, gkv_ref, wq_ref, wk_ref, wv_ref, c_ref, s1_ref, s2_ref, q_ref, k_ref, v_ref):
        c, s1, s2 = c_ref[...], s1_ref[...], s2_ref[...]
        cqn, _ = _norm_rows(cq_ref[...], gq_ref[...])
        ckvn, _ = _norm_rows(ckv_ref[...], gkv_ref[...])
        ckvn = ckvn.astype(BF16)
        qf = _mm(cqn.astype(BF16), wq_ref[...])
        kf = jnp.concatenate([_mm(ckvn, wk_ref[t]) for t in range(N_CHIPS)], axis=1)
        vf = jnp.concatenate([_mm(ckvn, wv_ref[t]) for t in range(N_CHIPS)], axis=1)
        kpe = _rope(kpe_ref[...], c, s1, s2)
        lane = lax.broadcasted_iota(jnp.int32, (TM, LANE), 1)
        for h in range(NH):
            blk = slice(h * LANE, (h + 1) * LANE)
            q_ref[h] = _rope(qf[:, blk], c, s1, s2).astype(BF16)
            k_ref[h] = (kf[:, blk] + kpe).astype(BF16)
            v_ref[h] = jnp.where(lane == ONES_LANE[h % 2], 1.0, vf[:, blk]).astype(BF16)

    row = lambda w, j: pl.BlockSpec((TM, w), lambda i: (i, j))
    full = lambda a: pl.BlockSpec(a.shape, lambda i: (0,) * a.ndim)
    hs = jax.ShapeDtypeStruct((NH, s, LANE), BF16)
    return pl.pallas_call(
        body, name="qkv_fwd", grid=(s // TM,),
        in_specs=[row(Q_LORA, P_CQ // Q_LORA), row(KV_LORA, P_CKV // KV_LORA), row(LANE, P_KPE // LANE),
                  full(g_q), full(g_kv), full(w_uq_p), full(w_k_p), full(w_v_p), row(LANE, 0), row(LANE, 0), row(LANE, 0)],
        out_specs=[pl.BlockSpec((NH, TM, LANE), lambda i: (0, i, 0))] * 3,
        out_shape=[hs, hs, hs],
        compiler_params=_params(("parallel",)),
    )(proj, proj, proj, g_q, g_kv, w_uq_p, w_k_p, w_v_p, rc, rs1, rs2)


LOG2E = 1.4426950408889634
QK_SCALE2 = LOG2E / math.sqrt(QK_NOPE + QK_ROPE)


HQ = TQ // 2


def _diag_visible(n):
    row = lax.broadcasted_iota(jnp.int32, (n, n), 0)
    col = lax.broadcasted_iota(jnp.int32, (n, n), 1)
    return (col // CHUNK) <= (row // CHUNK)


def _attn_fwd(q, k, vv):
    s = q.shape[1]

    def body(q_ref, k_ref, v_ref, o_ref, lse_ref):
        i = pl.program_id(1)
        qs = (q_ref[0], q_ref[1])

        def tiles(t, carry, diag):
            rows = pl.ds(pl.multiple_of(t * TQ, TQ), TQ)
            sc = [_mm_nt(qs[hh], k_ref[hh, rows, :]) for hh in range(2)]
            if diag:
                sc = [jnp.where(_diag_visible(TQ), s_, -jnp.inf) for s_ in sc]
            m_new = [jnp.maximum(carry[hh][0], jnp.max(sc[hh], axis=-1, keepdims=True)) for hh in range(2)]
            alpha = [jnp.exp2((carry[hh][0] - m_new[hh]) * QK_SCALE2) for hh in range(2)]
            p = [jnp.exp2((sc[hh] - m_new[hh]) * QK_SCALE2).astype(BF16) for hh in range(2)]
            acc = [alpha[hh] * carry[hh][1] + _mm(p[hh], v_ref[hh, rows, :]) for hh in range(2)]
            return (m_new[0], acc[0]), (m_new[1], acc[1])

        init = (jnp.full((TQ, 1), -jnp.inf, F32), jnp.zeros((TQ, LANE), F32))
        carry = lax.fori_loop(0, i, lambda t, c: tiles(t, c, False), (init, init))
        carry = tiles(i, carry, True)
        lane = lax.broadcasted_iota(jnp.int32, (TQ, LANE), 1)
        out = jnp.zeros((TQ, LANE), F32)
        for hh in range(2):
            m, acc = carry[hh]
            l = jnp.sum(jnp.where(lane == ONES_LANE[hh], acc, 0.0), axis=-1, keepdims=True)
            out = out + jnp.where((lane < V_DIM) == (hh == 0), acc, 0.0) / l
            lse_ref[hh] = jnp.broadcast_to(m * QK_SCALE2 + jnp.log(l) * LOG2E, (TQ, LANE))
        o_ref[...] = out

    return pl.pallas_call(
        body, name="attn_fwd", grid=(NH // 2, s // TQ),
        in_specs=[pl.BlockSpec((2, TQ, LANE), lambda p, i: (p, i, 0)), pl.BlockSpec((2, s, LANE), lambda p, i: (p, 0, 0)),
                  pl.BlockSpec((2, s, LANE), lambda p, i: (p, 0, 0))],
        out_specs=[pl.BlockSpec((TQ, LANE), lambda p, i: (i, p)), pl.BlockSpec((2, TQ, LANE), lambda p, i: (p, i, 0))],
        out_shape=[jax.ShapeDtypeStruct((s, NH * V_DIM), F32), jax.ShapeDtypeStruct((NH, s, LANE), F32)],
        compiler_params=_params(("parallel", "parallel")),
    )(q, k, vv)


def _lower_bound(lbl):
    a0, a1 = lbl[0:1, :], lbl[1:2, :]
    mx = jnp.maximum(a0, a1)
    e0, e1 = jnp.exp(a0 - mx), jnp.exp(a1 - mx)
    return e0 / (e0 + e1)


def _chunk_cumsum(v, reverse=False):
    pos = lax.broadcasted_iota(jnp.int32, v.shape, 0) % HG_BLOCK
    s = 1
    while s < HG_BLOCK:
        if reverse:
            v = v + jnp.where(pos < HG_BLOCK - s, pltpu.roll(v, TH - s, 0), 0.0)
        else:
            v = v + jnp.where(pos >= s, pltpu.roll(v, s, 0), 0.0)
        s *= 2
    return v


def _hgrn_gates(hq, hf, lb):
    sig = _sigmoid(hf)
    f = lb + (1.0 - lb) * sig
    g = jnp.log(f)
    kk = 1.0 - f
    r = lax.broadcasted_iota(jnp.int32, (TH, TH), 0)
    c = lax.broadcasted_iota(jnp.int32, (TH, TH), 1)
    tri = ((r // HG_BLOCK) == (c // HG_BLOCK)) & (r >= c)
    cum = _chunk_cumsum(g)
    nch = TH // HG_BLOCK
    total = _chunks(cum)[:, HG_BLOCK - 1:HG_BLOCK, :]
    lastb = jnp.broadcast_to(total, (nch, HG_BLOCK, hf.shape[-1])).reshape(hf.shape)
    e, ei, ee = jnp.exp(cum), jnp.exp(-cum), jnp.exp(lastb - cum)
    return dict(sig=sig, f=f, kk=kk, tri=tri, cum=cum, total=total, decay=jnp.exp(total), e=e, ei=ei, ee=ee,
                qd=hq * e, ki=kk * ei, ke=kk * ee)


def _chunks(v):
    return v.reshape(TH // HG_BLOCK, HG_BLOCK, v.shape[-1])


def _bmm_nt(a, b):
    return lax.dot_general(a, b, (((2,), (2,)), ((0,), (0,))), preferred_element_type=F32)


def _bmm_nn(a, b):
    return lax.dot_general(a, b, (((2,), (1,)), ((0,), (0,))), preferred_element_type=F32)


def _bmm_tn(a, b):
    return lax.dot_general(a, b, (((1,), (1,)), ((0,), (0,))), preferred_element_type=F32)


def _pair_masks():
    lane = lax.broadcasted_iota(jnp.int32, (TH, LANE), 1)
    kr = lax.broadcasted_iota(jnp.int32, (LANE, LANE), 0)
    kc = lax.broadcasted_iota(jnp.int32, (LANE, LANE), 1)
    return lane < 64, (kr // 64) == (kc // 64)


def _hgrn_fwd(proj, lbl):
    s = proj.shape[0]
    nch = TH // HG_BLOCK

    def body(hq_ref, hf_ref, hi_ref, lbl_ref, o_ref, st_ref, st):
        @pl.when(pl.program_id(1) == 0)
        def _():
            st[...] = jnp.zeros_like(st)

        m0, bd = _pair_masks()
        gt = _hgrn_gates(hq_ref[...], hf_ref[...], _lower_bound(lbl_ref[...]))
        v_b, qd, qd_b = hi_ref[...].astype(BF16), gt["qd"], gt["qd"].astype(BF16)
        ki_b, ke_b = gt["ki"].astype(BF16), gt["ke"].astype(BF16)
        pairs = [slice(u * LANE, (u + 1) * LANE) for u in range(HG_PAIRS)]
        heads = [(lanes, m0 if hh == 0 else jnp.logical_not(m0)) for lanes in pairs for hh in range(2)]
        a_b = [jnp.where(gt["tri"], _mm_nt(jnp.where(mh, qd[:, lanes], 0.0).astype(BF16), ki_b[:, lanes]), 0.0).astype(BF16)
               for lanes, mh in heads]
        intra = [jnp.where(m0, _mm(a_b[2 * u], v_b[:, lanes]), _mm(a_b[2 * u + 1], v_b[:, lanes])) for u, lanes in enumerate(pairs)]
        upd = [_bmm_tn(_chunks(v_b[:, lanes]), _chunks(ke_b[:, lanes])) for lanes in pairs]
        entering = []
        for u, lanes in enumerate(pairs):
            cur, states = st[u], []
            for n in range(nch):
                states.append(cur)
                cur = gt["decay"][n][:, lanes] * cur + jnp.where(bd, upd[u][n], 0.0)
            st[u] = cur
            entering.append(jnp.stack(states))
            st_ref[u] = entering[u]
        for u, lanes in enumerate(pairs):
            o_ref[:, lanes] = intra[u] + _bmm_nt(_chunks(qd_b[:, lanes]), entering[u].astype(BF16)).reshape(TH, LANE)

    wide = HG_PAIRS * LANE
    col = lambda base: pl.BlockSpec((TH, wide), lambda p, i: (i, base // wide + p))
    return pl.pallas_call(
        body, name="hgrn_fwd", grid=(NH // 2 // HG_PAIRS, s // TH),
        in_specs=[col(P_HQ), col(P_HF), col(P_HI), pl.BlockSpec((2, wide), lambda p, i: (0, p))],
        out_specs=[pl.BlockSpec((TH, wide), lambda p, i: (i, p)),
                   pl.BlockSpec((HG_PAIRS, nch, LANE, LANE), lambda p, i: (p, i, 0, 0))],
        out_shape=[jax.ShapeDtypeStruct((s, 512), F32), jax.ShapeDtypeStruct((NH // 2, s // HG_BLOCK, LANE, LANE), F32)],
        scratch_shapes=[pltpu.VMEM((HG_PAIRS, LANE, LANE), F32)],
        compiler_params=_params(("parallel", "arbitrary")),
    )(proj, proj, proj, lbl)


def _group_sum(v):
    low = lax.broadcasted_iota(jnp.int32, (v.shape[0], LANE), 1) < V_DIM
    blocks = []
    for b in range(v.shape[1] // LANE):
        blk = v[:, b * LANE:(b + 1) * LANE]
        s_low = jnp.sum(jnp.where(low, blk, 0.0), axis=-1, keepdims=True)
        s_high = jnp.sum(jnp.where(low, 0.0, blk), axis=-1, keepdims=True)
        blocks.append(jnp.where(low, s_low, s_high))
    return jnp.concatenate(blocks, axis=1)


def _dsilu(z, sg):
    return sg * (1.0 + z * (1.0 - sg))


def _mid(proj, attn, o_raw, x, tgt, g_hg, b_gate, g_post, wa, wb, w_out):
    s = x.shape[0]

    def body(attn_ref, ga_ref, o_ref, gb_ref, mg_ref, x_ref, t_ref, ghg_ref, bg_ref, gp_ref, wa_ref, wb_ref, wo_ref,
             loss_ref, dout_ref, dattn_ref, dga_ref, dor_ref, dgb_ref, dmg_ref, dwo_out, dwa_out, dwb_out, dgp_ref, dbg_ref, dghg_ref,
             dwo_ref, dwa_ref, dwb_ref):
        @pl.when(pl.program_id(0) == 0)
        def _():
            for rf in (loss_ref, dwo_ref, dwa_ref, dwb_ref, dgp_ref, dbg_ref, dghg_ref):
                rf[...] = jnp.zeros_like(rf)

        attn, za, orw, zb = attn_ref[...], ga_ref[...], o_ref[...], gb_ref[...]
        ghg, gp = ghg_ref[...], gp_ref[...]
        sga, sgb = _sigmoid(za), _sigmoid(zb)
        sa, sb = za * sga, zb * sgb
        ga = attn * sa
        rh = lax.rsqrt(_group_sum(orw * orw) * (1.0 / V_DIM) + EPS)
        on = (orw * rh) * ghg
        gb = on * sb
        ga_b, gb_b = ga.astype(BF16), gb.astype(BF16)
        blocks = [slice(t * (D // N_CHIPS), (t + 1) * (D // N_CHIPS)) for t in range(N_CHIPS)]
        ya = jnp.concatenate([_mm(ga_b, wa_ref[t]) for t in range(N_CHIPS)], axis=1)
        yb = jnp.concatenate([_mm(gb_b, wb_ref[t]) for t in range(N_CHIPS)], axis=1)
        gates = _sigmoid(mg_ref[...] + bg_ref[...])
        g0, g1 = gates[:, :D], gates[:, D:]
        m_b = (g0 * ya + g1 * yb).astype(BF16)
        y = _mm(m_b, wo_ref[...])
        ry = lax.rsqrt(jnp.mean(y * y, axis=-1, keepdims=True) + EPS)
        out = x_ref[...] + (y * ry) * gp
        err = out - t_ref[...]
        loss_ref[...] += 0.5 * jnp.sum(jnp.mean(err * err, axis=-1, keepdims=True), axis=0, keepdims=True)
        dout = err * (1.0 / D)
        dout_ref[...] = dout
        dgp_ref[...] += jnp.sum(dout * (y * ry), axis=0, keepdims=True)
        dgy = dout * gp
        dy = ry * dgy - y * (ry * ry * ry) * jnp.mean(y * dgy, axis=-1, keepdims=True)
        dy_b = dy.astype(BF16)
        dm = _mm_nt(dy_b, wo_ref[...])
        dya_b, dyb_b = (dm * g0).astype(BF16), (dm * g1).astype(BF16)
        dga = sum(_mm_nt(dya_b[:, cols], wa_ref[t]) for t, cols in enumerate(blocks))
        dgb = sum(_mm_nt(dyb_b[:, cols], wb_ref[t]) for t, cols in enumerate(blocks))
        dwo_ref[...] += _mm_tn(m_b, dy_b)
        for t, cols in enumerate(blocks):
            dwa_ref[t] += _mm_tn(ga_b, dya_b[:, cols])
            dwb_ref[t] += _mm_tn(gb_b, dyb_b[:, cols])
        dg0, dg1 = dm * ya, dm * yb
        dmg = jnp.concatenate([dg0 * g0 * (1.0 - g0), dg1 * g1 * (1.0 - g1)], axis=1)
        dmg_ref[...] = dmg.astype(BF16)
        dbg_ref[...] += jnp.sum(dmg, axis=0, keepdims=True)
        dattn_ref[...] = dga * sa
        dga_ref[...] = (dga * attn * _dsilu(za, sga)).astype(BF16)
        dgb_ref[...] = (dgb * on * _dsilu(zb, sgb)).astype(BF16)
        don = dgb * sb
        dghg_ref[...] += jnp.sum(don * (orw * rh), axis=0, keepdims=True)
        dgo = don * ghg
        dor_ref[...] = rh * dgo - orw * (rh * rh * rh) * (_group_sum(orw * dgo) * (1.0 / V_DIM))

        @pl.when(pl.program_id(0) == pl.num_programs(0) - 1)
        def _():
            for out, rf in ((dwo_out, dwo_ref), (dwa_out, dwa_ref), (dwb_out, dwb_ref)):
                out[...] = rf[...].astype(BF16)

    row = lambda w, j=0: pl.BlockSpec((TM_MID, w), lambda i: (i, j))
    full = lambda a: pl.BlockSpec(a.shape, lambda i: (0,) * a.ndim)
    acc = lambda shape: pl.BlockSpec(shape, lambda i: (0,) * len(shape))
    slabs = (N_CHIPS, 512, D // N_CHIPS)
    sds = jax.ShapeDtypeStruct
    return pl.pallas_call(
        body, name="mid", grid=(s // TM_MID,),
        in_specs=[row(512), row(512, P_GA // 512), row(512), row(512, P_GB // 512), row(2048, P_MERGE // 2048), row(D), row(D),
                  full(g_hg), full(b_gate), full(g_post), full(wa), full(wb), full(w_out)],
        out_specs=[acc((1, 1)), row(D), row(512), row(512), row(512), row(512), row(2048),
                   acc((D, D)), acc(slabs), acc(slabs), acc((1, D)), acc((1, 2048)), acc((1, 512))],
        out_shape=[sds((1, 1), F32), sds((s, D), F32), sds((s, 512), F32), sds((s, 512), BF16), sds((s, 512), F32), sds((s, 512), BF16),
                   sds((s, 2048), BF16), sds((D, D), BF16), sds(slabs, BF16), sds(slabs, BF16), sds((1, D), F32),
                   sds((1, 2048), F32), sds((1, 512), F32)],
        scratch_shapes=[pltpu.VMEM((D, D), F32), pltpu.VMEM(slabs, F32), pltpu.VMEM(slabs, F32)],
        compiler_params=_params(("arbitrary",)),
    )(attn, proj, o_raw, proj, proj, x, tgt, g_hg, b_gate, g_post, wa, wb, w_out)


def _attn_bwd(q, k, vv, attn, dattn, lse, token):
    s = q.shape[1]
    nt = s // TQ
    scale = 1.0 / math.sqrt(QK_NOPE + QK_ROPE)

    def body(q_ref, k_ref, v_ref, o_ref, do_ref, lse_ref, token_ref, dq_ref, dk_ref, dv_ref, do_s, delta_s):
        j = pl.program_id(1)

        @pl.when(j == 0)
        def _():
            dq_ref[...] = jnp.zeros_like(dq_ref)
            lane = lax.broadcasted_iota(jnp.int32, (TQ, LANE), 1)

            @pl.loop(0, nt)
            def _(i):
                rows = pl.ds(pl.multiple_of(i * TQ, TQ), TQ)
                do, o = do_ref[rows, :], o_ref[rows, :]
                for hh in range(2):
                    doh = jnp.where((lane < 64) if hh == 0 else (lane >= 64), do, 0.0)
                    do_s[hh, rows, :] = doh.astype(BF16)
                    delta_s[hh, rows, :] = jnp.broadcast_to(jnp.sum(doh * o, axis=-1, keepdims=True), (TQ, LANE))

        kjs, vjs = (k_ref[0], k_ref[1]), (v_ref[0], v_ref[1])

        def tile(hh, start, size, kj, vj, diag):
            rows = pl.ds(pl.multiple_of(start, size), size)
            wide = lambda a: jnp.concatenate([a] * (kj.shape[0] // LANE), axis=1)
            qi, do_b = q_ref[hh, rows, :], do_s[hh, rows, :]
            sc, dp = _mm_nt(qi, kj), _mm_nt(do_b, vj)
            p = jnp.exp2(sc * QK_SCALE2 - wide(lse_ref[hh, rows, :]))
            if diag:
                p = jnp.where(_diag_visible(size), p, 0.0)
            ds_b = (p * (dp - wide(delta_s[hh, rows, :]))).astype(BF16)
            dv, dk = _mm_tn(do_b, p.astype(BF16)), _mm_tn(qi, ds_b)
            dq_ref[hh, rows, :] += _mm(ds_b, kj)
            return dk, dv

        def step(i, carry):
            new = [tile(hh, i * TQ, TQ, kjs[hh], vjs[hh], False) for hh in range(2)]
            return tuple((carry[hh][0] + new[hh][0], carry[hh][1] + new[hh][1]) for hh in range(2))

        def diagonal(hh):
            k0, k1, v0, v1 = kjs[hh][:HQ], kjs[hh][HQ:], vjs[hh][:HQ], vjs[hh][HQ:]
            a = tile(hh, j * TQ, HQ, k0, v0, True)
            b = tile(hh, j * TQ + HQ, HQ, k0, v0, False)
            c = tile(hh, j * TQ + HQ, HQ, k1, v1, True)
            return jnp.concatenate([a[0] + b[0], c[0]], axis=1), jnp.concatenate([a[1] + b[1], c[1]], axis=1)

        carry = lax.fori_loop(j + 1, nt, step, (diagonal(0), diagonal(1)))
        for hh in range(2):
            dk_ref[hh] = carry[hh][0].T * scale
            dv_ref[hh] = carry[hh][1].T

        @pl.when(j == nt - 1)
        def _():
            dq_ref[...] = dq_ref[...] * scale

    whole = pl.BlockSpec((2, s, LANE), lambda p, j: (p, 0, 0))
    tile_spec = pl.BlockSpec((2, TQ, LANE), lambda p, j: (p, j, 0))
    cols = pl.BlockSpec((s, LANE), lambda p, j: (0, p))
    hs = jax.ShapeDtypeStruct((NH, s, LANE), F32)
    return pl.pallas_call(
        body, name="attn_bwd", grid=(NH // 2, nt),
        in_specs=[whole, tile_spec, tile_spec, cols, cols, whole, pl.BlockSpec((8, LANE), lambda p, j: (0, 0))],
        out_specs=[whole, tile_spec, tile_spec],
        out_shape=[hs, hs, hs],
        scratch_shapes=[pltpu.VMEM((2, s, LANE), BF16), pltpu.VMEM((2, s, LANE), F32)],
        compiler_params=_params(("parallel", "arbitrary")),
    )(q, k, vv, attn, dattn, lse, token)


def _hgrn_bwd(proj, lbl, states, do_raw):
    s = proj.shape[0]
    nt = s // TH
    nch = TH // HG_BLOCK

    def body(hq_ref, hf_ref, hi_ref, lbl_ref, st_ref, do_ref, dh_ref, dlbl_ref, dst, dlb):
        step = pl.program_id(1)

        @pl.when(step == 0)
        def _():
            dst[...] = jnp.zeros_like(dst)
            dlb[...] = jnp.zeros_like(dlb)

        m0, bd = _pair_masks()
        lb = _lower_bound(lbl_ref[...])
        gt = _hgrn_gates(hq_ref[...], hf_ref[...], lb)
        do = do_ref[...]
        qd, ki, ke = gt["qd"], gt["ki"], gt["ke"]
        v_b, do_b = hi_ref[...].astype(BF16), do.astype(BF16)
        qd_b, ki_b, ke_b = qd.astype(BF16), ki.astype(BF16), ke.astype(BF16)
        pairs = [slice(u * LANE, (u + 1) * LANE) for u in range(HG_PAIRS)]
        heads = [(lanes, m0 if hh == 0 else jnp.logical_not(m0)) for lanes in pairs for hh in range(2)]
        a_b = [jnp.where(gt["tri"], _mm_nt(jnp.where(mh, qd[:, lanes], 0.0).astype(BF16), ki_b[:, lanes]), 0.0).astype(BF16)
               for lanes, mh in heads]
        doh_b = [jnp.where(mh, do[:, lanes], 0.0).astype(BF16) for lanes, mh in heads]
        da_b = [jnp.where(gt["tri"], _mm_nt(d, v_b[:, lanes]), 0.0).astype(BF16) for d, (lanes, _) in zip(doh_b, heads)]
        dv_p, dqd_p, dki_p = [], [], []
        for u, lanes in enumerate(pairs):
            e, o = 2 * u, 2 * u + 1
            dv_p.append(_mm_tn(a_b[e], doh_b[e]) + _mm_tn(a_b[o], doh_b[o]))
            dqd_p.append(jnp.where(m0, _mm(da_b[e], ki_b[:, lanes]), _mm(da_b[o], ki_b[:, lanes])))
            dki_p.append(jnp.where(m0, _mm_tn(da_b[e], qd_b[:, lanes]), _mm_tn(da_b[o], qd_b[:, lanes])))
        fed = [_bmm_tn(_chunks(do_b[:, lanes]), _chunks(qd_b[:, lanes])) for lanes in pairs]
        leaving = []
        for u, lanes in enumerate(pairs):
            ds, left = dst[u], [None] * nch
            for n in reversed(range(nch)):
                left[n] = ds
                ds = gt["decay"][n][:, lanes] * ds + jnp.where(bd, fed[u][n], 0.0)
            dst[u] = ds
            leaving.append(jnp.stack(left))
        dke_p, dlast_p = [], []
        for u, lanes in enumerate(pairs):
            entering, leaving_b = st_ref[u], leaving[u].astype(BF16)
            dke3 = _bmm_nn(_chunks(v_b[:, lanes]), leaving_b)
            dv_p[u] = dv_p[u] + _bmm_nt(_chunks(ke_b[:, lanes]), leaving_b).reshape(TH, LANE)
            dqd_p[u] = dqd_p[u] + _bmm_nn(_chunks(do_b[:, lanes]), entering.astype(BF16)).reshape(TH, LANE)
            dke_p.append(dke3.reshape(TH, LANE))
            dlast_p.append(jnp.sum(dke3 * _chunks(ke[:, lanes]), axis=1, keepdims=True)
                           + jnp.sum(leaving[u] * entering, axis=1, keepdims=True) * gt["decay"][:, :, lanes])
        cat = lambda parts: jnp.concatenate(parts, axis=-1)
        dv, dqd, dki, dke, dlast = cat(dv_p), cat(dqd_p), cat(dki_p), cat(dke_p), cat(dlast_p)
        dk = dki * gt["ei"] + dke * gt["ee"]
        dcum = dqd * qd - dki * ki - dke * ke
        dg = _chunk_cumsum(dcum, reverse=True) + jnp.broadcast_to(dlast, (nch, HG_BLOCK, dlast.shape[-1])).reshape(dcum.shape)
        sig = gt["sig"]
        df = dg / gt["f"] - dk
        dlb[...] += jnp.sum(df * (1.0 - sig), axis=0, keepdims=True)
        dh_ref[0] = (dqd * gt["e"]).astype(BF16)
        dh_ref[1] = ((df * (1.0 - lb)) * sig * (1.0 - sig)).astype(BF16)
        dh_ref[2] = dv.astype(BF16)

        @pl.when(step == nt - 1)
        def _():
            lb = _lower_bound(lbl_ref[...])
            da0 = dlb[...] * lb * (1.0 - lb)
            dlbl_ref[...] = jnp.concatenate([da0, -da0], axis=0)

    wide = HG_PAIRS * LANE
    col = lambda base: pl.BlockSpec((TH, wide), lambda p, i: (nt - 1 - i, base // wide + p))
    tile = pl.BlockSpec((TH, wide), lambda p, i: (nt - 1 - i, p))
    sds = jax.ShapeDtypeStruct
    return pl.pallas_call(
        body, name="hgrn_bwd", grid=(NH // 2 // HG_PAIRS, nt),
        in_specs=[col(P_HQ), col(P_HF), col(P_HI), pl.BlockSpec((2, wide), lambda p, i: (0, p)),
                  pl.BlockSpec((HG_PAIRS, nch, LANE, LANE), lambda p, i: (p, nt - 1 - i, 0, 0)), tile],
        out_specs=[pl.BlockSpec((3, TH, wide), lambda p, i: (0, nt - 1 - i, p)), pl.BlockSpec((2, wide), lambda p, i: (0, p))],
        out_shape=[sds((3, s, 512), BF16), sds((2, 512), F32)],
        scratch_shapes=[pltpu.VMEM((HG_PAIRS, LANE, LANE), F32), pltpu.VMEM((1, wide), F32)],
        compiler_params=_params(("parallel", "arbitrary")),
    )(proj, proj, proj, lbl, states, do_raw)


def _norm_rows_bwd(v, r, g, dn):
    dgv = dn * g
    return r * dgv - v * (r * r * r) * jnp.mean(v * dgv, axis=-1, keepdims=True)


def _qkv_bwd(proj, dq, dk, dvv, g_q, g_kv, w_uq_p, w_k_p, w_v_p, rc, rs1, rs2):
    s = proj.shape[0]
    head_q = QK_NOPE + QK_ROPE

    def body(cq_ref, ckv_ref, dq_ref, dk_ref, dv_ref, gq_ref, gkv_ref, wq_ref, wk_ref, wv_ref, c_ref, s1_ref, s2_ref,
             dcq_ref, dckv_ref, dkpe_ref, dwq_out, dwkv_out, dgq_ref, dgkv_ref, dwq_ref, dwk_ref, dwv_ref):
        @pl.when(pl.program_id(0) == 0)
        def _():
            for rf in (dwq_ref, dwk_ref, dwv_ref, dgq_ref, dgkv_ref):
                rf[...] = jnp.zeros_like(rf)

        c, s1, s2 = c_ref[...], s1_ref[...], s2_ref[...]
        cq, ckv = cq_ref[...], ckv_ref[...]
        gq, gkv = gq_ref[...], gkv_ref[...]
        cqn, rq = _norm_rows(cq, gq)
        ckvn, rkv = _norm_rows(ckv, gkv)
        cqn_b, ckvn_b = cqn.astype(BF16), ckvn.astype(BF16)
        dqf = jnp.concatenate([_rope_t(dq_ref[h], c, s1, s2) for h in range(NH)], axis=1).astype(BF16)
        dkf = jnp.concatenate([dk_ref[h] for h in range(NH)], axis=1).astype(BF16)
        dvf = jnp.concatenate([dv_ref[h] for h in range(NH)], axis=1).astype(BF16)
        dkpe = dk_ref[0]
        for h in range(1, NH):
            dkpe = dkpe + dk_ref[h]
        lane = lax.broadcasted_iota(jnp.int32, (TM, LANE), 1)
        dkpe = jnp.where((lane >= QK_NOPE) & (lane < QK_NOPE + QK_ROPE), dkpe, 0.0)
        dkpe_ref[...] = _rope_t(dkpe, c, s1, s2).astype(BF16)
        dcqn = _mm_nt(dqf, wq_ref[...])
        pair = lambda a, t: a[:, t * 2 * LANE:(t + 1) * 2 * LANE]
        dckvn = sum(_mm_nt(pair(dkf, t), wk_ref[t]) + _mm_nt(pair(dvf, t), wv_ref[t]) for t in range(N_CHIPS))
        dwq_ref[...] += _mm_tn(cqn_b, dqf)
        dwk_ref[...] += _mm_tn(ckvn_b, dkf)
        dwv_ref[...] += _mm_tn(ckvn_b, dvf)
        dgq_ref[...] += jnp.sum(dcqn * (cq * rq), axis=0, keepdims=True)
        dgkv_ref[...] += jnp.sum(dckvn * (ckv * rkv), axis=0, keepdims=True)
        dcq_ref[...] = _norm_rows_bwd(cq, rq, gq, dcqn).astype(BF16)
        dckv_ref[...] = _norm_rows_bwd(ckv, rkv, gkv, dckvn).astype(BF16)

        @pl.when(pl.program_id(0) == pl.num_programs(0) - 1)
        def _():
            blk = lambda ref, h: ref[:, h * LANE:(h + 1) * LANE]
            lane = lax.broadcasted_iota(jnp.int32, (Q_LORA, LANE), 1)
            for j in range(NH * head_q // LANE):
                h0, w0 = divmod(j * LANE, head_q)
                first = blk(dwq_ref, h0) if w0 == 0 else pltpu.roll(blk(dwq_ref, h0), LANE - w0, 1)
                second = pltpu.roll(blk(dwq_ref, h0 + 1), head_q - w0, 1)
                dwq_out[:, j * LANE:(j + 1) * LANE] = jnp.where(lane < head_q - w0, first, second).astype(BF16)
            lane = lax.broadcasted_iota(jnp.int32, (KV_LORA, LANE), 1)
            for h in range(NH):
                vals = blk(dwv_ref, h) if h % 2 else pltpu.roll(blk(dwv_ref, h), V_DIM, 1)
                both = jnp.where(lane < QK_NOPE, blk(dwk_ref, h), vals).astype(BF16)
                dwkv_out[h // 2, :, (h % 2) * LANE:(h % 2 + 1) * LANE] = both

    row = lambda w, j=0: pl.BlockSpec((TM, w), lambda i: (i, j))
    full = lambda a: pl.BlockSpec(a.shape, lambda i: (0,) * a.ndim)
    acc = lambda *shape: pl.BlockSpec(shape, lambda i: (0,) * len(shape))
    heads = pl.BlockSpec((NH, TM, LANE), lambda i: (0, i, 0))
    sds = jax.ShapeDtypeStruct
    return pl.pallas_call(
        body, name="qkv_bwd", grid=(s // TM,),
        in_specs=[row(Q_LORA, P_CQ // Q_LORA), row(KV_LORA, P_CKV // KV_LORA), heads, heads, heads,
                  full(g_q), full(g_kv), full(w_uq_p), full(w_k_p), full(w_v_p), row(LANE), row(LANE), row(LANE)],
        out_specs=[row(Q_LORA), row(KV_LORA), row(LANE), acc(Q_LORA, NH * head_q), acc(NH // 2, KV_LORA, 2 * LANE),
                   acc(1, Q_LORA), acc(1, KV_LORA)],
        out_shape=[sds((s, Q_LORA), BF16), sds((s, KV_LORA), BF16), sds((s, LANE), BF16), sds((Q_LORA, NH * head_q), BF16),
                   sds((NH // 2, KV_LORA, 2 * LANE), BF16), sds((1, Q_LORA), F32), sds((1, KV_LORA), F32)],
        scratch_shapes=[pltpu.VMEM((Q_LORA, D), F32), pltpu.VMEM((KV_LORA, D), F32), pltpu.VMEM((KV_LORA, D), F32)],
        compiler_params=_params(("arbitrary",)),
    )(proj, proj, dq, dk, dvv, g_q, g_kv, w_uq_p, w_k_p, w_v_p, rc, rs1, rs2)


def _front_bwd(x, dout, dmg, dga, dh3, dgb, dcq, dckv, dkpe, g_pre, w_in_t, w_kpe, token):
    s = x.shape[0]

    def body(x_ref, do_ref, dmg_ref, dga_ref, dh3_ref, dgb_ref, dcq_ref, dckv_ref, dkpe_ref, g_ref, w_ref, k_ref, token_ref,
             gx_ref, dg_ref):
        @pl.when(pl.program_id(0) == 0)
        def _():
            dg_ref[...] = jnp.zeros_like(dg_ref)

        xv, g = x_ref[...], g_ref[...]
        _, r = _norm_rows(xv, g)
        pieces = ((dmg_ref[...], O_MERGE), (dga_ref[...], O_GA), (dh3_ref[0], O_HQ), (dh3_ref[1], O_HF), (dh3_ref[2], O_HI),
                  (dgb_ref[...], O_GB), (dcq_ref[...], O_CQ), (dckv_ref[...], O_CKV))
        dh = _mm(dkpe_ref[...], k_ref[...])
        for piece, off in pieces:
            dh = dh + _mm(piece, w_ref[off:off + piece.shape[1], :])
        dg_ref[...] += jnp.sum(dh * (xv * r), axis=0, keepdims=True)
        gx_ref[...] = do_ref[...] + _norm_rows_bwd(xv, r, g, dh)

    row = lambda w: pl.BlockSpec((TM, w), lambda i: (i, 0))
    full = lambda a: pl.BlockSpec(a.shape, lambda i: (0,) * a.ndim)
    sds = jax.ShapeDtypeStruct
    return pl.pallas_call(
        body, name="front_bwd", grid=(s // TM,),
        in_specs=[row(D), row(D), row(2048), row(512), pl.BlockSpec((3, TM, 512), lambda i: (0, i, 0)), row(512), row(Q_LORA),
                  row(KV_LORA), row(LANE), full(g_pre), full(w_in_t), full(w_kpe), pl.BlockSpec(memory_space=pl.ANY)],
        out_specs=[row(D), pl.BlockSpec((1, D), lambda i: (0, 0))],
        out_shape=[sds((s, D), F32), sds((1, D), F32)],
        compiler_params=_params(("arbitrary",)),
    )(x, dout, dmg, dga, dh3, dgb, dcq, dckv, dkpe, g_pre, w_in_t, w_kpe, token)


TK_GRAD = 1024


QKV_ROWS = Q_LORA + KV_LORA + QK_ROPE
D_IN = O_MERGE + 2048


def _win_grad(h, pieces, offsets, name, whole=None):
    s = h.shape[0]
    n = len(pieces)
    new = whole is None
    shapes = [(p.shape[0] * p.shape[2], D) if p.ndim == 3 else (p.shape[1], D) for p in pieces]

    def body(h_ref, *refs):
        d_refs, whole_ref, scratch = refs[:n], refs[n if new else n + 1], refs[n + (1 if new else 2):]
        sums, rounded, sems = scratch[:n], scratch[n:2 * n], scratch[-1]

        @pl.when(pl.program_id(0) == 0)
        def _():
            for s_ref in sums:
                s_ref[...] = jnp.zeros_like(s_ref)

        hv = h_ref[...]
        for d_ref, s_ref in zip(d_refs, sums):
            if len(d_ref.shape) == 3:
                w = d_ref.shape[2]
                for k in range(d_ref.shape[0]):
                    s_ref[k * w:(k + 1) * w] += _mm_tn(d_ref[k], hv)
            else:
                s_ref[...] += _mm_tn(d_ref[...], hv)

        @pl.when(pl.program_id(0) == pl.num_programs(0) - 1)
        def _():
            copies = []
            for j, (s_ref, r_ref, row) in enumerate(zip(sums, rounded, offsets)):
                r_ref[...] = s_ref[...].astype(BF16)
                copies.append(pltpu.make_async_copy(r_ref, whole_ref.at[pl.ds(row, r_ref.shape[0])], sems.at[j]))
            if new:
                zeros = scratch[2 * n]
                zeros[...] = jnp.zeros_like(zeros)
                copies.append(pltpu.make_async_copy(zeros, whole_ref.at[pl.ds(0, QKV_ROWS)], sems.at[n]))
            for copy in copies:
                copy.start()
            for copy in copies:
                copy.wait()

    def in_spec(p):
        if p.ndim == 3:
            return pl.BlockSpec((p.shape[0], TK_GRAD, p.shape[2]), lambda kk: (0, kk, 0))
        return pl.BlockSpec((TK_GRAD, p.shape[1]), lambda kk: (kk, 0))

    anywhere = pl.BlockSpec(memory_space=pl.ANY)
    return pl.pallas_call(
        body, name=name, grid=(s // TK_GRAD,),
        in_specs=[pl.BlockSpec((TK_GRAD, D), lambda kk: (kk, 0))] + [in_spec(p) for p in pieces] + ([] if new else [anywhere]),
        out_specs=anywhere, out_shape=jax.ShapeDtypeStruct((D_IN, D), BF16),
        input_output_aliases={} if new else {n + 1: 0},
        scratch_shapes=[pltpu.VMEM(sh, F32) for sh in shapes] + [pltpu.VMEM(sh, BF16) for sh in shapes]
        + ([pltpu.VMEM((QKV_ROWS, D), BF16)] if new else []) + [pltpu.SemaphoreType.DMA((n + 1,))],
        compiler_params=_params(("arbitrary",)),
    )(h, *pieces, *([] if new else [whole]))


def _win_grad_qkv(h, dcq, dckv, dkpe, share):
    s = h.shape[0]
    half = QKV_ROWS // 2

    def body(h_ref, cq_ref, ckv_ref, kpe_ref, o_ref, *scratch):
        sums = scratch[0] if share else o_ref

        @pl.when(pl.program_id(0) == 0)
        def _():
            sums[...] = jnp.zeros_like(sums)

        hv = h_ref[...]
        g_cq = _mm_tn(cq_ref[...], hv)
        sums[0] += g_cq[:half]
        sums[1, 0:Q_LORA - half] += g_cq[half:]
        sums[1, Q_LORA - half:Q_LORA + KV_LORA - half] += _mm_tn(ckv_ref[...], hv)
        sums[1, Q_LORA + KV_LORA - half:] += _mm_tn(kpe_ref[...], hv)[QK_NOPE:QK_NOPE + QK_ROPE]

        if share:
            _, theirs, send_sem, recv_sem = scratch

            @pl.when(pl.program_id(0) == pl.num_programs(0) - 1)
            def _():
                c = lax.axis_index("c")
                copy = _remote(sums.at[1 - c], theirs, send_sem, recv_sem, 0, (lax.axis_index("x"), lax.axis_index("y"), 1 - c))
                copy.start()
                copy.wait()
                o_ref[...] = (sums[c] + theirs[...]).astype(BF16)

    rows = lambda a: pl.BlockSpec((TK_GRAD, a.shape[1]), lambda kk: (kk, 0))
    sem = pltpu.SemaphoreType.DMA((1,))
    shape, dtype = ((half, D), BF16) if share else ((2, half, D), F32)
    return pl.pallas_call(
        body, name="win_grad_qkv", grid=(s // TK_GRAD,), in_specs=[rows(h), rows(dcq), rows(dckv), rows(dkpe)],
        out_specs=pl.BlockSpec(shape, lambda kk: (0,) * len(shape)), out_shape=jax.ShapeDtypeStruct(shape, dtype),
        scratch_shapes=[pltpu.VMEM((2, half, D), F32), pltpu.VMEM((half, D), F32), sem, sem] if share else [],
        compiler_params=_params(("arbitrary",)),
    )(h, dcq, dckv, dkpe)


def _pad_wuq(w_uq):
    rows = w_uq.shape[0]
    w = w_uq.reshape(rows, NH, QK_NOPE + QK_ROPE)
    return jnp.pad(w, ((0, 0), (0, 0), (0, LANE - QK_NOPE - QK_ROPE))).reshape(rows, NH * LANE)


def _pad_wukv(w_ukv):
    heads = w_ukv.shape[1] // (QK_NOPE + V_DIM)
    w = w_ukv.reshape(KV_LORA, heads, QK_NOPE + V_DIM)
    w_k = jnp.pad(w[:, :, :QK_NOPE], ((0, 0), (0, 0), (0, LANE - QK_NOPE))).reshape(KV_LORA, heads * LANE)
    wv = w[:, :, QK_NOPE:].reshape(KV_LORA, heads // 2, 2, 1, V_DIM)
    eye = jnp.eye(2, dtype=w.dtype).reshape(1, 1, 2, 2, 1)
    return w_k, (wv * eye).reshape(KV_LORA, heads * LANE)


def _local_step(x, tgt, g_pre, w_in_t, b_gate, g_q, g_kv, lb_logits, g_hgrn, g_post, weights, exchange=None):
    s = x.shape[0]
    w_kpe = _kpe_block(w_in_t)
    rc, rs1, rs2 = _rope_tables(s)
    g_hg = jnp.tile(g_hgrn, (1, NH))

    proj, h = _front_fwd(x, g_pre, w_in_t, w_kpe, weights.tokens)
    w_uq_p, w_k_p, w_v_p = weights.qkv(h)
    q, k, vv = _qkv_fwd(proj, g_q, g_kv, w_uq_p, w_k_p, w_v_p, rc, rs1, rs2)
    attn, lse = _attn_fwd(q, k, vv)
    o_raw, states = _hgrn_fwd(proj, lb_logits)
    wa, wb, w_out = weights.mid(o_raw)
    (loss, dout, dattn, dga, dor, dgb, dmg, d_wout, d_wa, d_wb, d_gpost, d_bgate, d_ghg) = _mid(
        proj, attn, o_raw, x, tgt, g_hg, b_gate, g_post, wa, wb, w_out)
    d_win = _win_grad(h, [dmg, dga, dgb], [O_MERGE, O_GA, O_GB], "win_grad_mid")
    dh3, d_lbl = _hgrn_bwd(proj, lb_logits, states, dor)
    d_win = _win_grad(h, [dh3], [O_HQ], "win_grad_hgrn", d_win)
    early = dict(w_in=d_win, w_branch_a=d_wa, w_branch_b=d_wb, w_out=d_wout)
    token = exchange.start_early(early) if exchange else jnp.zeros((8, LANE), F32)
    dq, dk, dvv = _attn_bwd(q, k, vv, attn, dattn, lse, token)
    dcq, dckv, dkpe, d_wuq, d_wukv, d_gq, d_gkv = _qkv_bwd(proj, dq, dk, dvv, g_q, g_kv, w_uq_p, w_k_p, w_v_p, rc, rs1, rs2)
    late = dict(w_in_qkv=_win_grad_qkv(h, dcq, dckv, dkpe, share=exchange is not None), w_uq=d_wuq, w_ukv=d_wukv)
    token = exchange.start_late(late) if exchange else jnp.zeros((8, LANE), F32)
    grad_x, d_gpre = _front_bwd(x, dout, dmg, dga, dh3, dgb, dcq, dckv, dkpe, g_pre, w_in_t, w_kpe, token)
    vec_grads = dict(g_pre=d_gpre, b_gate=d_bgate, g_q=d_gq, g_kv=d_gkv, lb_logits=d_lbl, g_hgrn=d_ghg, g_post=d_gpost)
    return loss, grad_x, dict(early, **late), vec_grads


SHARD_SHAPES = (("w_in", (1416, 1024)), ("w_uq", (192, 768)), ("w_ukv", (256, 256)), ("w_branch_a", (512, 256)),
                ("w_branch_b", (512, 256)), ("w_out", (256, 1024)))
BIG = tuple(n for n, _ in SHARD_SHAPES)
ROW_SHARDED = ("w_in", "w_uq", "w_out")
N_CHIPS = 4
W_IN_FORWARD_CUT = 704


def _to_block(name, a):
    return a[0].T if name == "w_in" else a[0]


def _from_block(name, a):
    return a.T[None] if name == "w_in" else a[None]
VEC_ROWS = (("g_pre", 0, 1024), ("b_gate", 1, 2048), ("g_q", 2, 768), ("g_kv", 3, 256), ("g_hgrn", 6, 64), ("g_post", 7, 1024))
VEC_LB_ROW = 4
VEC_SHAPE = (8, 2048)


def _split_by_chip(name, g):
    a, b = dict(SHARD_SHAPES)[name]
    return g.reshape(N_CHIPS, a, b) if name in ROW_SHARDED else g.reshape(a, N_CHIPS, b).transpose(1, 0, 2)


def _join_chips(name, w):
    a, b = dict(SHARD_SHAPES)[name]
    return w.reshape(N_CHIPS * a, b) if name in ROW_SHARDED else w.transpose(1, 0, 2).reshape(a, N_CHIPS * b)


MESH = pl.DeviceIdType.MESH
HBM = pl.BlockSpec(memory_space=pltpu.HBM)


def _mesh_place():
    x, y, c = lax.axis_index("x"), lax.axis_index("y"), lax.axis_index("c")
    return x, y, c, 2 * x + y, [(1 - x, y), (x, 1 - y), (1 - x, 1 - y)]


def _remote(src, dst, send_sems, recv_sems, k, to):
    return pltpu.make_async_remote_copy(src_ref=src, dst_ref=dst, send_sem=send_sems.at[k], recv_sem=recv_sems.at[k],
                                        device_id=to, device_id_type=MESH)


def _gather_w_in(shard, between):
    a, b = shard.shape
    cut = W_IN_FORWARD_CUT
    whole = lambda ref, which: ref.at[:, pl.ds(pl.multiple_of(which * (b // 2), b // 2), b // 2)]

    def first_copies(src, out, ici_send, ici_recv, local_sem):
        x, y, c = lax.axis_index("x"), lax.axis_index("y"), lax.axis_index("c")
        me = 2 * x + y
        return (pltpu.make_async_copy(src, out.at[me], local_sem.at[0]),
                [_remote(whole(src, c), whole(out.at[me], c), ici_send, ici_recv, 0, (1 - x, y, c)),
                 _remote(whole(src, c), whole(out.at[me], c), ici_send, ici_recv, 1, (x, 1 - y, c))])

    def start_body(src, out_in, ici_send, ici_recv, local_sem, src_out, out, token):
        own, sends = first_copies(src, out, ici_send, ici_recv, local_sem)
        own.start()
        for cp in sends:
            cp.start()
        token[...] = jnp.zeros_like(token)

    sems = pltpu.SemaphoreType.DMA((4,))
    hbm = lambda shape: pltpu.HBM(shape, shard.dtype)
    ici_send, ici_recv, local_sem, src_thru, out_started, token = pl.pallas_call(
        start_body, name="gather_w_in_start",
        out_shape=[sems, sems, pltpu.SemaphoreType.DMA((1,)), hbm((a, b)), hbm((N_CHIPS, a, b)), jax.ShapeDtypeStruct((8, LANE), F32)],
        in_specs=[HBM, HBM], out_specs=[SEM, SEM, SEM, HBM, HBM, pl.BlockSpec(memory_space=pltpu.VMEM)],
        input_output_aliases={0: 3, 1: 4},
        compiler_params=pltpu.CompilerParams(has_side_effects=DATAFLOW),
    )(pltpu.with_memory_space_constraint(shard, pltpu.HBM),
      pltpu.with_memory_space_constraint(lax.empty((N_CHIPS, a, b), shard.dtype), pltpu.HBM))
    result, after = between(token)

    def wait_body(src, out_in, ici_send, ici_recv, local_sem, after_ref, src_out, out):
        own, sends = first_copies(src, out, ici_send, ici_recv, local_sem)
        own.wait()
        for cp in sends:
            cp.wait_send()
            cp.wait_recv()

    _, direct = pl.pallas_call(
        wait_body, name="gather_w_in_wait", out_shape=[hbm((a, b)), hbm((N_CHIPS, a, b))],
        in_specs=[HBM, HBM, SEM, SEM, SEM, pl.BlockSpec(memory_space=pl.ANY)], out_specs=[HBM, HBM],
        input_output_aliases={0: 0, 1: 1},
        compiler_params=pltpu.CompilerParams(has_side_effects=DATAFLOW),
    )(src_thru, out_started, ici_send, ici_recv, local_sem, after)

    def body(out_in, out, ici_send, ici_recv, d2d_send, d2d_recv):
        x, y, c = lax.axis_index("x"), lax.axis_index("y"), lax.axis_index("c")
        xn, yn, dg = 2 * (1 - x) + y, 2 * x + (1 - y), 2 * (1 - x) + (1 - y)
        to_x, to_y, sibling = (1 - x, y, c), (x, 1 - y, c), (x, y, 1 - c)
        first, rest = pl.ds(0, cut), pl.ds(cut, a - cut)
        mine = pl.ds(pl.multiple_of(c * (b // 2), b // 2), b // 2)
        sends = []

        def to_sibling(slot, rows, d2d_k):
            piece = out.at[slot].at[:, mine] if rows is None else out.at[slot].at[rows, mine]
            cp = _remote(piece, piece, d2d_send, d2d_recv, d2d_k, sibling)
            cp.start()
            sends.append(cp)

        def landed(slot, rows, k, d2d_k, src_dev):
            piece = out.at[slot].at[rows, mine]
            _remote(piece, piece, ici_send, ici_recv, k, src_dev).wait_recv()
            to_sibling(slot, rows, d2d_k)

        def pass_on(slot, rows, k, to):
            piece = out.at[slot].at[rows, mine]
            cp = _remote(piece, piece, ici_send, ici_recv, k, to)
            cp.start()
            sends.append(cp)

        pass_on(xn, first, 2, to_y)
        pass_on(yn, rest, 3, to_x)
        to_sibling(xn, None, 0)
        to_sibling(yn, None, 1)
        landed(dg, first, 2, 2, to_y)
        landed(dg, rest, 3, 3, to_x)
        other = pl.ds(pl.multiple_of((1 - c) * (b // 2), b // 2), b // 2)
        for d2d_k, (slot, rows) in enumerate(((xn, None), (yn, None), (dg, first), (dg, rest))):
            piece = out.at[slot].at[:, other] if rows is None else out.at[slot].at[rows, other]
            _remote(piece, piece, d2d_send, d2d_recv, d2d_k, sibling).wait_recv()
        for cp in sends:
            cp.wait_send()

    gathered = pl.pallas_call(
        body, name="gather_w_in", out_shape=jax.ShapeDtypeStruct((N_CHIPS, a, b), shard.dtype),
        in_specs=[HBM], out_specs=HBM, input_output_aliases={0: 0}, scratch_shapes=[sems, sems, sems, sems],
        compiler_params=pltpu.CompilerParams(has_side_effects=True),
    )(direct)
    return gathered, result


SEM =pl.BlockSpec(memory_space=pltpu.SEMAPHORE)
DATAFLOW = pltpu.SideEffectType.DATAFLOW_SIDE_EFFECTING


def _exchange_copies(srcs, to_first, src_refs, land_refs, send_sems, recv_sems):
    x, y, c, me, chips = _mesh_place()
    n = len(srcs)
    sends, recvs = [], []
    for k in range(n):
        if k in to_first:
            base = 3 * n + 4 * to_first.index(k)
            sends.append((me != 0, pltpu.make_async_remote_copy(
                src_ref=src_refs[k], dst_ref=land_refs[k].at[me], send_sem=send_sems.at[base], recv_sem=recv_sems.at[base + me],
                device_id=(0, 0, c), device_id_type=MESH)))
            for s in range(1, N_CHIPS):
                recvs.append((me == 0, pltpu.make_async_remote_copy(
                    src_ref=src_refs[k], dst_ref=land_refs[k].at[s], send_sem=send_sems.at[base], recv_sem=recv_sems.at[base + s],
                    device_id=(s // 2, s % 2, c), device_id_type=MESH)))
        else:
            slab = (lambda t, k=k: src_refs[k]) if srcs[k].ndim == 2 else (lambda t, k=k: src_refs[k].at[t])
            for j, (px, py) in enumerate(chips):
                sends.append((None, _remote(slab(2 * px + py), land_refs[k].at[me], send_sems, recv_sems, 3 * k + j, (px, py, c))))
                recvs.append((None, _remote(slab(me), land_refs[k].at[2 * px + py], send_sems, recv_sems, 3 * k + j, (px, py, c))))
    return sends, recvs


def _when(pred, fn):
    if pred is None:
        fn()
    else:
        pl.when(pred)(fn)


def _exchange_start(srcs, to_first, name, after=None):
    n = len(srcs)
    n_sems = 3 * n + 4 * len(to_first)
    lands = [lax.empty((N_CHIPS,) + s.shape[-2:], s.dtype) for s in srcs]
    extra = [] if after is None else [after]

    def body(*refs):
        src_refs, land_refs = refs[:n], refs[n:2 * n]
        send_sems, recv_sems, token = refs[2 * n + len(extra)], refs[2 * n + len(extra) + 1], refs[-1]
        sends, _ = _exchange_copies(srcs, to_first, src_refs, land_refs, send_sems, recv_sems)
        for pred, cp in sends:
            _when(pred, cp.start)
        token[...] = jnp.zeros_like(token)

    hbm = lambda a: pltpu.HBM(a.shape, a.dtype)
    res = pl.pallas_call(
        body, name=name,
        out_shape=[pltpu.SemaphoreType.DMA((n_sems,)), pltpu.SemaphoreType.DMA((n_sems,))] + [hbm(a) for a in srcs + lands]
        + [jax.ShapeDtypeStruct((8, LANE), F32)],
        in_specs=[HBM] * (2 * n) + [pl.BlockSpec(memory_space=pl.ANY)] * len(extra),
        out_specs=[SEM, SEM] + [HBM] * (2 * n) + [pl.BlockSpec(memory_space=pltpu.VMEM)],
        input_output_aliases={i: 2 + i for i in range(2 * n)},
        compiler_params=pltpu.CompilerParams(has_side_effects=DATAFLOW),
    )(*[pltpu.with_memory_space_constraint(a, pltpu.HBM) for a in srcs + lands], *extra)
    return res[:-1], res[-1]


def _exchange_wait(srcs, to_first, started, after, name):
    n = len(srcs)
    send_sems, recv_sems, thru = started[0], started[1], started[2:]

    def body(*refs):
        src_refs, land_refs, send_ref, recv_ref = refs[:n], refs[n:2 * n], refs[2 * n], refs[2 * n + 1]
        sends, recvs = _exchange_copies(srcs, to_first, src_refs, land_refs, send_ref, recv_ref)
        for pred, cp in sends:
            _when(pred, cp.wait_send)
        for pred, cp in recvs:
            _when(pred, cp.wait_recv)

    res = pl.pallas_call(
        body, name=name, out_shape=[pltpu.HBM(a.shape, a.dtype) for a in thru],
        in_specs=[HBM] * (2 * n) + [SEM, SEM, pl.BlockSpec(memory_space=pl.ANY)], out_specs=[HBM] * (2 * n),
        input_output_aliases={i: i for i in range(2 * n)},
        compiler_params=pltpu.CompilerParams(has_side_effects=DATAFLOW),
    )(*thru, send_sems, recv_sems, after)
    return res[:n], res[n:]


ROW_TILE = 256
COL_TILE = 256


def _block_tiling(a, b):
    if a <= ROW_TILE or a % ROW_TILE == 0:
        ta = min(a, ROW_TILE)
        return a // ta, (ta, b), lambda i: (i, 0)
    return b // COL_TILE, (a, COL_TILE), lambda i: (0, i)


def _sum_share_small(lands, owns, name, after=None):
    n = len(lands)
    extra = [] if after is None else [after]

    def body(*refs):
        p_refs, own_refs = refs[:n], refs[n:2 * n]
        o_refs, mine, theirs = (refs[j * n + len(extra):(j + 1) * n + len(extra)] for j in (2, 3, 4))
        send_sems, recv_sems = refs[5 * n + len(extra):]
        me = 2 * lax.axis_index("x") + lax.axis_index("y")
        sibling = (lax.axis_index("x"), lax.axis_index("y"), 1 - lax.axis_index("c"))
        copies = [_remote(mine[k], theirs[k], send_sems, recv_sems, k, sibling) for k in range(n)]
        for p_ref, own_ref, mine_ref, copy in zip(p_refs, own_refs, mine, copies):
            own = (own_ref[me] if len(own_ref.shape) == 3 else own_ref[...]).astype(F32)
            slot = lambda t: jnp.where(me == t, own, p_ref[t].astype(F32))
            mine_ref[...] = ((slot(0) + slot(1)) + slot(2)) + slot(3)
            copy.start()
        for o_ref, mine_ref, theirs_ref, copy in zip(o_refs, mine, theirs, copies):
            copy.wait()
            o_ref[...] = mine_ref[...] + theirs_ref[...]

    vmem = pl.BlockSpec(memory_space=pltpu.VMEM)
    kept = [pltpu.VMEM(land.shape[1:], F32) for land in lands]
    sems = pltpu.SemaphoreType.DMA((n,))
    return pl.pallas_call(
        body, name=name, in_specs=[vmem] * (2 * n) + [pl.BlockSpec(memory_space=pl.ANY)] * len(extra), out_specs=[vmem] * n,
        out_shape=[jax.ShapeDtypeStruct(land.shape[1:], F32) for land in lands], scratch_shapes=kept + kept + [sems, sems],
        compiler_params=_params(()),
    )(*lands, *owns, *extra)


def _adamw_small(grads, states, name):
    n = len(grads)

    def body(*refs):
        ins, outs = refs[:4 * n], refs[4 * n:]
        for k in range(n):
            g_ref, w_ref, m_ref, v_ref = ins[4 * k:4 * k + 4]
            g = g_ref[...]
            outs[4 * k][...] = g
            outs[4 * k + 1][...], outs[4 * k + 2][...], outs[4 * k + 3][...] = _adamw_math(g, w_ref[...], m_ref[...], v_ref[...])

    args = [t for k in range(n) for t in (grads[k], *states[k])]
    res = pl.pallas_call(body, name=name, out_shape=[jax.ShapeDtypeStruct(grads[k].shape, F32) for k in range(n) for _ in range(4)],
                         compiler_params=_params(()))(*args)
    return [tuple(res[4 * k:4 * k + 4]) for k in range(n)]


class _LaterWeights:
    MID = ("w_branch_a", "w_branch_b", "w_out")

    def __init__(self, blocks, after):
        self.qkv_blocks = [_pad_wuq(blocks["w_uq"]), *_pad_wukv(blocks["w_ukv"])]
        self.mid_blocks = [blocks[n] for n in self.MID]
        self.qkv_started, t1 = _exchange_start(self.qkv_blocks, (), "weights_qkv_start", after)
        self.mid_started, t2 = _exchange_start(self.mid_blocks, (), "weights_mid_start", after)
        self.tokens = [t1, t2]

    @staticmethod
    def _whole(blocks, joined, started, after, name):
        _, landed = _exchange_wait(blocks, (), started, after, name)
        me = 2 * lax.axis_index("x") + lax.axis_index("y")
        out = []
        for block, land, join in zip(blocks, landed, joined):
            w = lax.dynamic_update_index_in_dim(land, block, me, 0)
            out.append(w.reshape(N_CHIPS * block.shape[0], block.shape[1]) if join else w)
        return out

    def qkv(self, after):
        return self._whole(self.qkv_blocks, (True, False, False), self.qkv_started, after, "weights_qkv_wait")

    def mid(self, after):
        return self._whole(self.mid_blocks, (False, False, True), self.mid_started, after, "weights_mid_wait")


class _GradExchange:
    EARLY = ("w_in", "w_branch_a", "w_branch_b", "w_out")
    LATE = ("w_uq", "w_ukv")

    def __init__(self, state):
        self.state = state
        self.outs = {}

    def start_early(self, g):
        self.early = [(g[n] if g[n].ndim == 3 else _split_by_chip(n, g[n])).astype(BF16) for n in self.EARLY]
        self.early_started, token = _exchange_start(self.early, (), "grads_early_start")
        return token

    def start_late(self, g):
        self.early, self.early_landed = _exchange_wait(self.early, (), self.early_started, g["w_uq"], "grads_early_wait")
        self.late = [_split_by_chip("w_uq", g["w_uq"]), g["w_ukv"], g["w_in_qkv"]]
        self.late_started, token = _exchange_start(self.late, (2,), "grads_late_start")
        names = self.EARLY[1:]
        grads = _sum_share_small(self.early_landed[1:], self.early[1:], "sum_early", after=token)
        for n, out in zip(names, _adamw_small(grads, [self.state[n] for n in names], "adamw_early")):
            self.outs[n] = out
        return self.outs[names[-1]][0]

    def finish(self, after):
        late, late_landed = _exchange_wait(self.late, (2,), self.late_started, after, "grads_late_wait")
        grad = _sum_exchange(self.early_landed[0], self.early[0], late_landed[2], late[2], "sum_w_in")
        self.outs["w_in"] = _adamw(grad, *self.state["w_in"], "adamw_w_in")
        return list(late[:2]), list(late_landed[:2])


def _adamw_math(g, w, m, v):
    nm = ADAM_B1 * m + (1.0 - ADAM_B1) * g
    nv = ADAM_B2 * v + (1.0 - ADAM_B2) * (g * g)
    m_hat = nm / (1.0 - ADAM_B1 ** ADAM_STEP)
    v_hat = nv / (1.0 - ADAM_B2 ** ADAM_STEP)
    return -ADAM_LR * (m_hat / (jnp.sqrt(v_hat) + ADAM_EPS) + ADAM_WD * w), nm, nv


def _sum_exchange(land, own, first_land, first_own, name):
    _, a, b = land.shape
    steps, tile, at = _block_tiling(a, b)
    assert tile[0] == a, "the extra rows need whole columns in a step"
    r = first_own.shape[0]

    def body(me_ref, p_ref, own_ref, fp_ref, fo_ref, g_ref, mine_s, theirs_s, send_sems, recv_sems):
        pass_, i = pl.program_id(0), pl.program_id(1)
        me, c = me_ref[0], lax.axis_index("c")
        sibling = (lax.axis_index("x"), lax.axis_index("y"), 1 - c)
        copy = _remote(mine_s.at[i], theirs_s.at[i], send_sems, recv_sems, i, sibling)

        @pl.when(pass_ == 0)
        def _():
            own = own_ref[...].astype(F32)
            slot = lambda t: jnp.where(me == t, own, p_ref[t].astype(F32))
            mine_s[i] = ((slot(0) + slot(1)) + slot(2)) + slot(3)

            @pl.when(me == 0)
            def _():
                f = lambda t: fp_ref[t].astype(F32)
                rows = pl.ds(pl.multiple_of(c * r, 8), r)
                mine_s[i, rows, :] += ((fo_ref[...].astype(F32) + f(1)) + f(2)) + f(3)

            copy.start()

        @pl.when(pass_ == 1)
        def _():
            copy.wait()
            g_ref[...] = mine_s[i] + theirs_s[i]

    first = lambda p, i: i * (1 - p) + (steps - 1) * p
    in_specs = [pl.BlockSpec((N_CHIPS,) + tile, lambda p, i, me: (0,) + at(first(p, i))),
                pl.BlockSpec((None,) + tile, lambda p, i, me: (me[0],) + at(first(p, i))),
                pl.BlockSpec((N_CHIPS, r, tile[1]), lambda p, i, me: (0,) + at(first(p, i))),
                pl.BlockSpec((r, tile[1]), lambda p, i, me: at(first(p, i)))]
    me = jnp.reshape(2 * lax.axis_index("x") + lax.axis_index("y"), (1,)).astype(jnp.int32)
    kept = pltpu.VMEM((steps,) + tile, F32)
    sems = pltpu.SemaphoreType.DMA((steps,))
    return pl.pallas_call(
        body, name=name,
        grid_spec=pltpu.PrefetchScalarGridSpec(num_scalar_prefetch=1, grid=(2, steps), in_specs=in_specs,
                                               out_specs=pl.BlockSpec(tile, lambda p, i, me: at(i * p)),
                                               scratch_shapes=[kept, kept, sems, sems]),
        out_shape=jax.ShapeDtypeStruct((a, b), F32),
        compiler_params=_params(("arbitrary", "arbitrary")),
    )(me, land, own, first_land, first_own)


def _adamw(g, w, m, v, name):
    a, b = g.shape
    steps, tile, at = _block_tiling(a, b)

    def body(g_ref, w_ref, m_ref, v_ref, go_ref, d_ref, nm_ref, nv_ref):
        g = g_ref[...]
        go_ref[...] = g
        d_ref[...], nm_ref[...], nv_ref[...] = _adamw_math(g, w_ref[...], m_ref[...], v_ref[...])

    spec = pl.BlockSpec(tile, at)
    sds = jax.ShapeDtypeStruct((a, b), F32)
    return pl.pallas_call(
        body, name=name, grid=(steps,), in_specs=[spec] * 4, out_specs=[spec] * 4, out_shape=[sds] * 4,
        compiler_params=_params(("parallel",)),
    )(g, w, m, v)


LOSS_AT = (2, 1024)


def _vec_pack(vg, loss):
    names = [n for n, _, _ in VEC_ROWS]

    def body(*refs):
        o_ref = refs[-1]
        lb_ref, loss_ref = refs[len(names)], refs[len(names) + 1]
        o_ref[...] = jnp.zeros_like(o_ref)
        o_ref[LOSS_AT[0]:LOSS_AT[0] + 1, LOSS_AT[1]:LOSS_AT[1] + LANE] = jnp.broadcast_to(loss_ref[...], (1, LANE))
        for (name, row, size), ref in zip(VEC_ROWS, refs):
            if name == "g_hgrn":
                r = lax.broadcasted_iota(jnp.int32, (NH * V_DIM, LANE), 0)
                c = lax.broadcasted_iota(jnp.int32, (NH * V_DIM, LANE), 1)
                fold = ((r % V_DIM) == c).astype(F32)
                o_ref[row:row + 1, 0:LANE] = jnp.dot(ref[...], fold, precision=HIGHEST, preferred_element_type=F32)
            else:
                o_ref[row:row + 1, 0:size] = ref[...]
        o_ref[VEC_LB_ROW:VEC_LB_ROW + 2, 0:512] = lb_ref[...]

    return pl.pallas_call(body, name="vec_pack", out_shape=jax.ShapeDtypeStruct(VEC_SHAPE, F32))(
        *[vg[n] for n in names], vg["lb_logits"], loss)


def _adamw_vec(block, w, m, v):
    names = [n for n, _, _ in VEC_ROWS] + ["lb_logits"]
    k = len(names)

    def body(g_ref, *refs):
        ins, outs = refs[:3 * k], refs[3 * k:]
        outs[-1][...] = g_ref[LOSS_AT[0]:LOSS_AT[0] + 1, LOSS_AT[1]:LOSS_AT[1] + LANE]
        for i, name in enumerate(names):
            if name == "lb_logits":
                rows, cols = slice(VEC_LB_ROW, VEC_LB_ROW + 2), slice(0, 512)
            else:
                _, row, size = VEC_ROWS[i]
                rows, cols = slice(row, row + 1), slice(0, size)
            g = g_ref[rows, cols]
            d, nm, nv = _adamw_math(g, ins[i][...], ins[k + i][...], ins[2 * k + i][...])
            for o_ref, val in zip(outs[4 * i:4 * i + 4], (g, d, nm, nv)):
                o_ref[...] = val

    shapes = [jax.ShapeDtypeStruct(w[n].shape, F32) for n in names for _ in range(4)] + [jax.ShapeDtypeStruct((1, LANE), F32)]
    res = pl.pallas_call(body, name="adamw_vec", out_shape=shapes)(
        block, *[w[n] for n in names], *[m[n] for n in names], *[v[n] for n in names])
    return [{n: res[4 * i + j] for i, n in enumerate(names)} for j in range(4)], res[-1]


WEIGHTS = ("g_pre", "w_in", "b_gate", "g_q", "w_uq", "g_kv", "w_ukv", "lb_logits", "g_hgrn", "w_branch_a", "w_branch_b", "w_out", "g_post")


def kernel(x, g_pre, w_in, b_gate, g_q, w_uq, g_kv, w_ukv, lb_logits, g_hgrn, w_branch_a, w_branch_b, w_out, g_post, loss_target, m_g_pre, m_w_in, m_b_gate, m_g_q, m_w_uq, m_g_kv, m_w_ukv, m_lb_logits, m_g_hgrn, m_w_branch_a, m_w_branch_b, m_w_out, m_g_post, v_g_pre, v_w_in, v_b_gate, v_g_q, v_w_uq, v_g_kv, v_w_ukv, v_lb_logits, v_g_hgrn, v_w_branch_a, v_w_branch_b, v_w_out, v_g_post):
    w = dict(g_pre=g_pre, w_in=w_in, b_gate=b_gate, g_q=g_q, w_uq=w_uq, g_kv=g_kv, w_ukv=w_ukv, lb_logits=lb_logits, g_hgrn=g_hgrn,
             w_branch_a=w_branch_a, w_branch_b=w_branch_b, w_out=w_out, g_post=g_post)
    m = dict(g_pre=m_g_pre, w_in=m_w_in, b_gate=m_b_gate, g_q=m_g_q, w_uq=m_w_uq, g_kv=m_g_kv, w_ukv=m_w_ukv, lb_logits=m_lb_logits,
             g_hgrn=m_g_hgrn, w_branch_a=m_w_branch_a, w_branch_b=m_w_branch_b, w_out=m_w_out, g_post=m_g_post)
    v = dict(g_pre=v_g_pre, w_in=v_w_in, b_gate=v_b_gate, g_q=v_g_q, w_uq=v_w_uq, g_kv=v_g_kv, w_ukv=v_w_ukv, lb_logits=v_lb_logits,
             g_hgrn=v_g_hgrn, w_branch_a=v_w_branch_a, w_branch_b=v_w_branch_b, w_out=v_w_out, g_post=v_g_post)
    blocks = {n: _to_block(n, w[n]).astype(BF16) for n in BIG}
    def later_weights(token):
        weights = _LaterWeights(blocks, token)
        return weights, weights.tokens[-1]

    w_in_all, weights = _gather_w_in(blocks["w_in"], later_weights)
    state = {n: [_to_block(n, t[n]) for t in (w, m, v)] for n in BIG}
    exchange = _GradExchange(state)
    loss, grad_x, _, vec_grads = _local_step(
        x[0], loss_target[0], g_pre, _join_chips("w_in", w_in_all), b_gate, g_q, g_kv, lb_logits, g_hgrn, g_post, weights, exchange)
    vec = _vec_pack(vec_grads, loss)
    vec_started, token = _exchange_start([vec], (), "vec_start")
    late_sent, late_landed = exchange.finish(token)
    (vec,), (vec_landed,) = _exchange_wait([vec], (), vec_started, exchange.outs["w_in"][0], "vec_wait")
    grads = _sum_share_small(late_landed + [vec_landed], late_sent + [vec], "sum_late")
    done = dict(exchange.outs)
    done.update(zip(exchange.LATE, _adamw_small(grads[:-1], [state[n] for n in exchange.LATE], "adamw_late")))
    outs = [{}, {}, {}, {}]
    for n in BIG:
        for o, val in zip(outs, done[n]):
            o[n] = _from_block(n, val)
    vec_outs, total = _adamw_vec(grads[-1], w, m, v)
    for o, vals in zip(outs, vec_outs):
        o.update(vals)
    return (total[0, 0], grad_x[None], *[o[n] for o in outs for n in WEIGHTS])
```

```python
import math

import numpy as np
import jax
import jax.numpy as jnp
from jax import lax
from jax.experimental import pallas as pl
from jax.experimental.pallas import tpu as pltpu

F32 = jnp.float32
BF16 = jnp.bfloat16
HIGHEST = lax.Precision.HIGHEST

D = 1024
NH = 8
QK_NOPE, QK_ROPE, V_DIM = 64, 32, 64
Q_LORA, KV_LORA = 768, 256
CHUNK = 64
HG_BLOCK = 32
EPS = 1e-6
LANE = 128
P_MERGE, P_GA, P_HQ, P_HF, P_HI, P_GB, P_CQ, P_CKV, P_KPE = 0, 2048, 2560, 3072, 3584, 4096, 4608, 5376, 5632
D_P = 5760
O_CQ, O_CKV, O_KPE, O_GA, O_HQ, O_HF, O_HI, O_GB, O_MERGE = 0, 768, 1024, 1056, 1568, 2080, 2592, 3104, 3616

TM = 512
TM_MID = 256
TQ = 1024
ONES_LANE = (LANE - 1, 0)
TH = 256
HG_PAIRS = 4
VMEM_LIMIT = 56 * 1024 * 1024

ADAM_LR, ADAM_B1, ADAM_B2, ADAM_EPS, ADAM_WD, ADAM_STEP = 0.001, 0.9, 0.999, 1e-08, 0.01, 10

NT_DIMS = (((1,), (1,)), ((), ()))
TN_DIMS = (((0,), (0,)), ((), ()))


def _params(sem, fusible=None):
    return pltpu.CompilerParams(dimension_semantics=sem, vmem_limit_bytes=VMEM_LIMIT, allow_input_fusion=fusible)


def _mm(a, b):
    return jnp.dot(a, b, preferred_element_type=F32)


def _mm_nt(a, b):
    return lax.dot_general(a, b, NT_DIMS, preferred_element_type=F32)


def _mm_tn(a, b):
    return lax.dot_general(a, b, TN_DIMS, preferred_element_type=F32)


def _sigmoid(z):
    return jax.nn.sigmoid(z)


def _rope(v, c, s1, s2):
    return v * c + pltpu.roll(v, 112, 1) * s1 + pltpu.roll(v, 16, 1) * s2


def _rope_t(dy, c, s1, s2):
    return dy * c + pltpu.roll(dy * s1, 16, 1) + pltpu.roll(dy * s2, 112, 1)


def _rope_tables(s):
    f32 = np.float32
    inv = f32(10000.0) ** (-np.arange(0, QK_ROPE, 2, dtype=f32) / f32(QK_ROPE))
    ang = np.arange(s, dtype=f32)[:, None] * inv[None, :]
    cos, sin = np.cos(ang).astype(f32), np.sin(ang).astype(f32)
    z64, z32, o64, o32 = np.zeros((s, 64), f32), np.zeros((s, 32), f32), np.ones((s, 64), f32), np.ones((s, 32), f32)
    z16 = np.zeros((s, 16), f32)
    c = np.concatenate([o64, cos, cos, o32], axis=1)
    s1 = np.concatenate([z64, -sin, z16, z32], axis=1)
    s2 = np.concatenate([z64, z16, sin, z32], axis=1)
    return jnp.asarray(c), jnp.asarray(s1), jnp.asarray(s2)


W_IN_RUNS = ((O_MERGE, 2048, P_MERGE), (O_GA, O_MERGE - O_GA, P_GA), (O_CQ, O_KPE - O_CQ, P_CQ))


def _kpe_block(w_in_t):
    z = lambda n: jnp.zeros((n, w_in_t.shape[1]), w_in_t.dtype)
    return jnp.concatenate([z(64), w_in_t[O_KPE:O_KPE + QK_ROPE], z(32)], axis=0)


def _front_fwd(x, g_pre, w_in_t, w_kpe, tokens=()):
    s = x.shape[0]
    tokens = list(tokens)

    def body(x_ref, g_ref, w_ref, k_ref, *refs):
        o_ref, h_ref = refs[len(tokens):]
        xv = x_ref[...]
        r = lax.rsqrt(jnp.mean(xv * xv, axis=-1, keepdims=True) + EPS)
        h = ((xv * r) * g_ref[...]).astype(BF16)
        h_ref[...] = h
        for row, rows, col in W_IN_RUNS:
            o_ref[:, col:col + rows] = _mm_nt(h, w_ref[row:row + rows, :])
        o_ref[:, P_KPE:P_KPE + LANE] = _mm_nt(h, k_ref[...])

    full = lambda a: pl.BlockSpec(a.shape, lambda i: (0,) * a.ndim)
    return pl.pallas_call(
        body, name="front_fwd", grid=(s // TM,),
        in_specs=[pl.BlockSpec((TM, D), lambda i: (i, 0)), pl.BlockSpec((1, D), lambda i: (0, 0)), full(w_in_t), full(w_kpe)]
        + [pl.BlockSpec((8, LANE), lambda i: (0, 0))] * len(tokens),
        out_specs=[pl.BlockSpec((TM, D_P), lambda i: (i, 0)), pl.BlockSpec((TM, D), lambda i: (i, 0))],
        out_shape=[jax.ShapeDtypeStruct((s, D_P), F32), jax.ShapeDtypeStruct((s, D), BF16)],
        compiler_params=_params(("parallel",)),
    )(x, g_pre, w_in_t, w_kpe, *tokens)


def _norm_rows(v, g):
    r = lax.rsqrt(jnp.mean(v * v, axis=-1, keepdims=True) + EPS)
    return (v * r) * g, r


def _qkv_fwd(proj, g_q, g_kv, w_uq_p, w_k_p, w_v_p, rc, rs1, rs2):
    s = proj.shape[0]

    def body(cq_ref, ckv_ref, kpe_ref, gq_ref, gkv_ref, wq_ref, wk_ref, wv_ref, c_ref, s1_ref, s2_ref, q_ref, k_ref, v_ref):
        c, s1, s2 = c_ref[...], s1_ref[...], s2_ref[...]
        cqn, _ = _norm_rows(cq_ref[...], gq_ref[...])
        ckvn, _ = _norm_rows(ckv_ref[...], gkv_ref[...])
        ckvn = ckvn.astype(BF16)
        qf = _mm(cqn.astype(BF16), wq_ref[...])
        kf = jnp.concatenate([_mm(ckvn, wk_ref[t]) for t in range(N_CHIPS)], axis=1)
        vf = jnp.concatenate([_mm(ckvn, wv_ref[t]) for t in range(N_CHIPS)], axis=1)
        kpe = _rope(kpe_ref[...], c, s1, s2)
        lane = lax.broadcasted_iota(jnp.int32, (TM, LANE), 1)
        for h in range(NH):
            blk = slice(h * LANE, (h + 1) * LANE)
            q_ref[h] = _rope(qf[:, blk], c, s1, s2).astype(BF16)
            k_ref[h] = (kf[:, blk] + kpe).astype(BF16)
            v_ref[h] = jnp.where(lane == ONES_LANE[h % 2], 1.0, vf[:, blk]).astype(BF16)

    row = lambda w, j: pl.BlockSpec((TM, w), lambda i: (i, j))
    full = lambda a: pl.BlockSpec(a.shape, lambda i: (0,) * a.ndim)
    hs = jax.ShapeDtypeStruct((NH, s, LANE), BF16)
    return pl.pallas_call(
        body, name="qkv_fwd", grid=(s // TM,),
        in_specs=[row(Q_LORA, P_CQ // Q_LORA), row(KV_LORA, P_CKV // KV_LORA), row(LANE, P_KPE // LANE),
                  full(g_q), full(g_kv), full(w_uq_p), full(w_k_p), full(w_v_p), row(LANE, 0), row(LANE, 0), row(LANE, 0)],
        out_specs=[pl.BlockSpec((NH, TM, LANE), lambda i: (0, i, 0))] * 3,
        out_shape=[hs, hs, hs],
        compiler_params=_params(("parallel",), [False] * 5 + [True] * 3 + [False] * 3),
    )(proj, proj, proj, g_q, g_kv, w_uq_p, w_k_p, w_v_p, rc, rs1, rs2)


LOG2E = 1.4426950408889634
QK_SCALE2 = LOG2E / math.sqrt(QK_NOPE + QK_ROPE)


HQ = TQ // 2


def _diag_visible(n):
    row = lax.broadcasted_iota(jnp.int32, (n, n), 0)
    col = lax.broadcasted_iota(jnp.int32, (n, n), 1)
    return (col // CHUNK) <= (row // CHUNK)


def _attn_fwd(q, k, vv):
    s = q.shape[1]

    def body(q_ref, k_ref, v_ref, o_ref, lse_ref):
        i = pl.program_id(1)
        qs = (q_ref[0], q_ref[1])

        def tiles(t, carry, diag):
            rows = pl.ds(pl.multiple_of(t * TQ, TQ), TQ)
            sc = [_mm_nt(qs[hh], k_ref[hh, rows, :]) for hh in range(2)]
            if diag:
                sc = [jnp.where(_diag_visible(TQ), s_, -jnp.inf) for s_ in sc]
            m_new = [jnp.maximum(carry[hh][0], jnp.max(sc[hh], axis=-1, keepdims=True)) for hh in range(2)]
            alpha = [jnp.exp2((carry[hh][0] - m_new[hh]) * QK_SCALE2) for hh in range(2)]
            p = [jnp.exp2((sc[hh] - m_new[hh]) * QK_SCALE2).astype(BF16) for hh in range(2)]
            acc = [alpha[hh] * carry[hh][1] + _mm(p[hh], v_ref[hh, rows, :]) for hh in range(2)]
            return (m_new[0], acc[0]), (m_new[1], acc[1])

        init = (jnp.full((TQ, 1), -jnp.inf, F32), jnp.zeros((TQ, LANE), F32))
        carry = lax.fori_loop(0, i, lambda t, c: tiles(t, c, False), (init, init))
        carry = tiles(i, carry, True)
        lane = lax.broadcasted_iota(jnp.int32, (TQ, LANE), 1)
        out = jnp.zeros((TQ, LANE), F32)
        for hh in range(2):
            m, acc = carry[hh]
            l = jnp.sum(jnp.where(lane == ONES_LANE[hh], acc, 0.0), axis=-1, keepdims=True)
            out = out + jnp.where((lane < V_DIM) == (hh == 0), acc, 0.0) / l
            lse_ref[hh] = jnp.broadcast_to(m * QK_SCALE2 + jnp.log(l) * LOG2E, (TQ, LANE))
        o_ref[...] = out

    return pl.pallas_call(
        body, name="attn_fwd", grid=(NH // 2, s // TQ),
        in_specs=[pl.BlockSpec((2, TQ, LANE), lambda p, i: (p, i, 0)), pl.BlockSpec((2, s, LANE), lambda p, i: (p, 0, 0)),
                  pl.BlockSpec((2, s, LANE), lambda p, i: (p, 0, 0))],
        out_specs=[pl.BlockSpec((TQ, LANE), lambda p, i: (i, p)), pl.BlockSpec((2, TQ, LANE), lambda p, i: (p, i, 0))],
        out_shape=[jax.ShapeDtypeStruct((s, NH * V_DIM), F32), jax.ShapeDtypeStruct((NH, s, LANE), F32)],
        compiler_params=_params(("parallel", "parallel")),
    )(q, k, vv)


def _lower_bound(lbl):
    a0, a1 = lbl[0:1, :], lbl[1:2, :]
    mx = jnp.maximum(a0, a1)
    e0, e1 = jnp.exp(a0 - mx), jnp.exp(a1 - mx)
    return e0 / (e0 + e1)


def _chunk_cumsum(v, reverse=False):
    pos = lax.broadcasted_iota(jnp.int32, v.shape, 0) % HG_BLOCK
    s = 1
    while s < HG_BLOCK:
        if reverse:
            v = v + jnp.where(pos < HG_BLOCK - s, pltpu.roll(v, TH - s, 0), 0.0)
        else:
            v = v + jnp.where(pos >= s, pltpu.roll(v, s, 0), 0.0)
        s *= 2
    return v


def _hgrn_gates(hq, hf, lb):
    sig = _sigmoid(hf)
    f = lb + (1.0 - lb) * sig
    g = jnp.log(f)
    kk = 1.0 - f
    r = lax.broadcasted_iota(jnp.int32, (TH, TH), 0)
    c = lax.broadcasted_iota(jnp.int32, (TH, TH), 1)
    tri = ((r // HG_BLOCK) == (c // HG_BLOCK)) & (r >= c)
    cum = _chunk_cumsum(g)
    nch = TH // HG_BLOCK
    total = _chunks(cum)[:, HG_BLOCK - 1:HG_BLOCK, :]
    lastb = jnp.broadcast_to(total, (nch, HG_BLOCK, hf.shape[-1])).reshape(hf.shape)
    e, ei, ee = jnp.exp(cum), jnp.exp(-cum), jnp.exp(lastb - cum)
    return dict(sig=sig, f=f, kk=kk, tri=tri, cum=cum, total=total, decay=jnp.exp(total), e=e, ei=ei, ee=ee,
                qd=hq * e, ki=kk * ei, ke=kk * ee)


def _chunks(v):
    return v.reshape(TH // HG_BLOCK, HG_BLOCK, v.shape[-1])


def _bmm_nt(a, b):
    return lax.dot_general(a, b, (((2,), (2,)), ((0,), (0,))), preferred_element_type=F32)


def _bmm_nn(a, b):
    return lax.dot_general(a, b, (((2,), (1,)), ((0,), (0,))), preferred_element_type=F32)


def _bmm_tn(a, b):
    return lax.dot_general(a, b, (((1,), (1,)), ((0,), (0,))), preferred_element_type=F32)


def _pair_masks():
    lane = lax.broadcasted_iota(jnp.int32, (TH, LANE), 1)
    kr = lax.broadcasted_iota(jnp.int32, (LANE, LANE), 0)
    kc = lax.broadcasted_iota(jnp.int32, (LANE, LANE), 1)
    return lane < 64, (kr // 64) == (kc // 64)


def _hgrn_fwd(proj, lbl):
    s = proj.shape[0]
    nch = TH // HG_BLOCK

    def body(hq_ref, hf_ref, hi_ref, lbl_ref, o_ref, st_ref, st):
        @pl.when(pl.program_id(1) == 0)
        def _():
            st[...] = jnp.zeros_like(st)

        m0, bd = _pair_masks()
        gt = _hgrn_gates(hq_ref[...], hf_ref[...], _lower_bound(lbl_ref[...]))
        v_b, qd, qd_b = hi_ref[...].astype(BF16), gt["qd"], gt["qd"].astype(BF16)
        ki_b, ke_b = gt["ki"].astype(BF16), gt["ke"].astype(BF16)
        pairs = [slice(u * LANE, (u + 1) * LANE) for u in range(HG_PAIRS)]
        heads = [(lanes, m0 if hh == 0 else jnp.logical_not(m0)) for lanes in pairs for hh in range(2)]
        a_b = [jnp.where(gt["tri"], _mm_nt(jnp.where(mh, qd[:, lanes], 0.0).astype(BF16), ki_b[:, lanes]), 0.0).astype(BF16)
               for lanes, mh in heads]
        intra = [jnp.where(m0, _mm(a_b[2 * u], v_b[:, lanes]), _mm(a_b[2 * u + 1], v_b[:, lanes])) for u, lanes in enumerate(pairs)]
        upd = [_bmm_tn(_chunks(v_b[:, lanes]), _chunks(ke_b[:, lanes])) for lanes in pairs]
        entering = []
        for u, lanes in enumerate(pairs):
            cur, states = st[u], []
            for n in range(nch):
                states.append(cur)
                cur = gt["decay"][n][:, lanes] * cur + jnp.where(bd, upd[u][n], 0.0)
            st[u] = cur
            entering.append(jnp.stack(states))
            st_ref[u] = entering[u]
        for u, lanes in enumerate(pairs):
            o_ref[:, lanes] = intra[u] + _bmm_nt(_chunks(qd_b[:, lanes]), entering[u].astype(BF16)).reshape(TH, LANE)

    wide = HG_PAIRS * LANE
    col = lambda base: pl.BlockSpec((TH, wide), lambda p, i: (i, base // wide + p))
    return pl.pallas_call(
        body, name="hgrn_fwd", grid=(NH // 2 // HG_PAIRS, s // TH),
        in_specs=[col(P_HQ), col(P_HF), col(P_HI), pl.BlockSpec((2, wide), lambda p, i: (0, p))],
        out_specs=[pl.BlockSpec((TH, wide), lambda p, i: (i, p)),
                   pl.BlockSpec((HG_PAIRS, nch, LANE, LANE), lambda p, i: (p, i, 0, 0))],
        out_shape=[jax.ShapeDtypeStruct((s, 512), F32), jax.ShapeDtypeStruct((NH // 2, s // HG_BLOCK, LANE, LANE), F32)],
        scratch_shapes=[pltpu.VMEM((HG_PAIRS, LANE, LANE), F32)],
        compiler_params=_params(("parallel", "arbitrary")),
    )(proj, proj, proj, lbl)


def _group_sum(v):
    low = lax.broadcasted_iota(jnp.int32, (v.shape[0], LANE), 1) < V_DIM
    blocks = []
    for b in range(v.shape[1] // LANE):
        blk = v[:, b * LANE:(b + 1) * LANE]
        s_low = jnp.sum(jnp.where(low, blk, 0.0), axis=-1, keepdims=True)
        s_high = jnp.sum(jnp.where(low, 0.0, blk), axis=-1, keepdims=True)
        blocks.append(jnp.where(low, s_low, s_high))
    return jnp.concatenate(blocks, axis=1)


def _dsilu(z, sg):
    return sg * (1.0 + z * (1.0 - sg))


def _mid(proj, attn, o_raw, x, tgt, g_hg, b_gate, g_post, wa, wb, w_out):
    s = x.shape[0]

    def body(attn_ref, ga_ref, o_ref, gb_ref, mg_ref, x_ref, t_ref, ghg_ref, bg_ref, gp_ref, wa_ref, wb_ref, wo_ref,
             loss_ref, dout_ref, dattn_ref, dga_ref, dor_ref, dgb_ref, dmg_ref, dwo_out, dwa_out, dwb_out, dgp_ref, dbg_ref, dghg_ref,
             dwo_ref, dwa_ref, dwb_ref):
        @pl.when(pl.program_id(0) == 0)
        def _():
            for rf in (loss_ref, dwo_ref, dwa_ref, dwb_ref, dgp_ref, dbg_ref, dghg_ref):
                rf[...] = jnp.zeros_like(rf)

        attn, za, orw, zb = attn_ref[...], ga_ref[...], o_ref[...], gb_ref[...]
        ghg, gp = ghg_ref[...], gp_ref[...]
        sga, sgb = _sigmoid(za), _sigmoid(zb)
        sa, sb = za * sga, zb * sgb
        ga = attn * sa
        rh = lax.rsqrt(_group_sum(orw * orw) * (1.0 / V_DIM) + EPS)
        on = (orw * rh) * ghg
        gb = on * sb
        ga_b, gb_b = ga.astype(BF16), gb.astype(BF16)
        blocks = [slice(t * (D // N_CHIPS), (t + 1) * (D // N_CHIPS)) for t in range(N_CHIPS)]
        ya = jnp.concatenate([_mm(ga_b, wa_ref[t]) for t in range(N_CHIPS)], axis=1)
        yb = jnp.concatenate([_mm(gb_b, wb_ref[t]) for t in range(N_CHIPS)], axis=1)
        gates = _sigmoid(mg_ref[...] + bg_ref[...])
        g0, g1 = gates[:, :D], gates[:, D:]
        m_b = (g0 * ya + g1 * yb).astype(BF16)
        y = _mm(m_b, wo_ref[...])
        ry = lax.rsqrt(jnp.mean(y * y, axis=-1, keepdims=True) + EPS)
        out = x_ref[...] + (y * ry) * gp
        err = out - t_ref[...]
        loss_ref[...] += 0.5 * jnp.sum(jnp.mean(err * err, axis=-1, keepdims=True), axis=0, keepdims=True)
        dout = err * (1.0 / D)
        dout_ref[...] = dout
        dgp_ref[...] += jnp.sum(dout * (y * ry), axis=0, keepdims=True)
        dgy = dout * gp
        dy = ry * dgy - y * (ry * ry * ry) * jnp.mean(y * dgy, axis=-1, keepdims=True)
        dy_b = dy.astype(BF16)
        dm = _mm_nt(dy_b, wo_ref[...])
        dya_b, dyb_b = (dm * g0).astype(BF16), (dm * g1).astype(BF16)
        dga = sum(_mm_nt(dya_b[:, cols], wa_ref[t]) for t, cols in enumerate(blocks))
        dgb = sum(_mm_nt(dyb_b[:, cols], wb_ref[t]) for t, cols in enumerate(blocks))
        dwo_ref[...] += _mm_tn(m_b, dy_b)
        for t, cols in enumerate(blocks):
            dwa_ref[t] += _mm_tn(ga_b, dya_b[:, cols])
            dwb_ref[t] += _mm_tn(gb_b, dyb_b[:, cols])
        dg0, dg1 = dm * ya, dm * yb
        dmg = jnp.concatenate([dg0 * g0 * (1.0 - g0), dg1 * g1 * (1.0 - g1)], axis=1)
        dmg_ref[...] = dmg.astype(BF16)
        dbg_ref[...] += jnp.sum(dmg, axis=0, keepdims=True)
        dattn_ref[...] = dga * sa
        dga_ref[...] = (dga * attn * _dsilu(za, sga)).astype(BF16)
        dgb_ref[...] = (dgb * on * _dsilu(zb, sgb)).astype(BF16)
        don = dgb * sb
        dghg_ref[...] += jnp.sum(don * (orw * rh), axis=0, keepdims=True)
        dgo = don * ghg
        dor_ref[...] = rh * dgo - orw * (rh * rh * rh) * (_group_sum(orw * dgo) * (1.0 / V_DIM))

        @pl.when(pl.program_id(0) == pl.num_programs(0) - 1)
        def _():
            for out, rf in ((dwo_out, dwo_ref), (dwa_out, dwa_ref), (dwb_out, dwb_ref)):
                out[...] = rf[...].astype(BF16)

    row = lambda w, j=0: pl.BlockSpec((TM_MID, w), lambda i: (i, j))
    full = lambda a: pl.BlockSpec(a.shape, lambda i: (0,) * a.ndim)
    acc = lambda shape: pl.BlockSpec(shape, lambda i: (0,) * len(shape))
    slabs = (N_CHIPS, 512, D // N_CHIPS)
    sds = jax.ShapeDtypeStruct
    return pl.pallas_call(
        body, name="mid", grid=(s // TM_MID,),
        in_specs=[row(512), row(512, P_GA // 512), row(512), row(512, P_GB // 512), row(2048, P_MERGE // 2048), row(D), row(D),
                  full(g_hg), full(b_gate), full(g_post), full(wa), full(wb), full(w_out)],
        out_specs=[acc((1, 1)), row(D), row(512), row(512), row(512), row(512), row(2048),
                   acc((D, D)), acc(slabs), acc(slabs), acc((1, D)), acc((1, 2048)), acc((1, 512))],
        out_shape=[sds((1, 1), F32), sds((s, D), F32), sds((s, 512), F32), sds((s, 512), BF16), sds((s, 512), F32), sds((s, 512), BF16),
                   sds((s, 2048), BF16), sds((D, D), BF16), sds(slabs, BF16), sds(slabs, BF16), sds((1, D), F32),
                   sds((1, 2048), F32), sds((1, 512), F32)],
        scratch_shapes=[pltpu.VMEM((D, D), F32), pltpu.VMEM(slabs, F32), pltpu.VMEM(slabs, F32)],
        compiler_params=_params(("arbitrary",), [False] * 10 + [True] * 3),
    )(attn, proj, o_raw, proj, proj, x, tgt, g_hg, b_gate, g_post, wa, wb, w_out)


def _attn_bwd(q, k, vv, attn, dattn, lse, token):
    s = q.shape[1]
    nt = s // TQ
    scale = 1.0 / math.sqrt(QK_NOPE + QK_ROPE)

    def body(q_ref, k_ref, v_ref, o_ref, do_ref, lse_ref, token_ref, dq_ref, dk_ref, dv_ref, do_s, delta_s):
        j = pl.program_id(1)

        @pl.when(j == 0)
        def _():
            dq_ref[...] = jnp.zeros_like(dq_ref)
            lane = lax.broadcasted_iota(jnp.int32, (TQ, LANE), 1)

            @pl.loop(0, nt)
            def _(i):
                rows = pl.ds(pl.multiple_of(i * TQ, TQ), TQ)
                do, o = do_ref[rows, :], o_ref[rows, :]
                for hh in range(2):
                    doh = jnp.where((lane < 64) if hh == 0 else (lane >= 64), do, 0.0)
                    do_s[hh, rows, :] = doh.astype(BF16)
                    delta_s[hh, rows, :] = jnp.broadcast_to(jnp.sum(doh * o, axis=-1, keepdims=True), (TQ, LANE))

        kjs, vjs = (k_ref[0], k_ref[1]), (v_ref[0], v_ref[1])

        def tile(hh, start, size, kj, vj, diag):
            rows = pl.ds(pl.multiple_of(start, size), size)
            wide = lambda a: jnp.concatenate([a] * (kj.shape[0] // LANE), axis=1)
            qi, do_b = q_ref[hh, rows, :], do_s[hh, rows, :]
            sc, dp = _mm_nt(qi, kj), _mm_nt(do_b, vj)
            p = jnp.exp2(sc * QK_SCALE2 - wide(lse_ref[hh, rows, :]))
            if diag:
                p = jnp.where(_diag_visible(size), p, 0.0)
            ds_b = (p * (dp - wide(delta_s[hh, rows, :]))).astype(BF16)
            dv, dk = _mm_tn(do_b, p.astype(BF16)), _mm_tn(qi, ds_b)
            dq_ref[hh, rows, :] += _mm(ds_b, kj)
            return dk, dv

        def step(i, carry):
            new = [tile(hh, i * TQ, TQ, kjs[hh], vjs[hh], False) for hh in range(2)]
            return tuple((carry[hh][0] + new[hh][0], carry[hh][1] + new[hh][1]) for hh in range(2))

        def diagonal(hh):
            k0, k1, v0, v1 = kjs[hh][:HQ], kjs[hh][HQ:], vjs[hh][:HQ], vjs[hh][HQ:]
            a = tile(hh, j * TQ, HQ, k0, v0, True)
            b = tile(hh, j * TQ + HQ, HQ, k0, v0, False)
            c = tile(hh, j * TQ + HQ, HQ, k1, v1, True)
            return jnp.concatenate([a[0] + b[0], c[0]], axis=1), jnp.concatenate([a[1] + b[1], c[1]], axis=1)

        carry = lax.fori_loop(j + 1, nt, step, (diagonal(0), diagonal(1)))
        for hh in range(2):
            dk_ref[hh] = carry[hh][0].T * scale
            dv_ref[hh] = carry[hh][1].T

        @pl.when(j == nt - 1)
        def _():
            dq_ref[...] = dq_ref[...] * scale

    whole = pl.BlockSpec((2, s, LANE), lambda p, j: (p, 0, 0))
    tile_spec = pl.BlockSpec((2, TQ, LANE), lambda p, j: (p, j, 0))
    cols = pl.BlockSpec((s, LANE), lambda p, j: (0, p))
    hs = jax.ShapeDtypeStruct((NH, s, LANE), F32)
    return pl.pallas_call(
        body, name="attn_bwd", grid=(NH // 2, nt),
        in_specs=[whole, tile_spec, tile_spec, cols, cols, whole, pl.BlockSpec((8, LANE), lambda p, j: (0, 0))],
        out_specs=[whole, tile_spec, tile_spec],
        out_shape=[hs, hs, hs],
        scratch_shapes=[pltpu.VMEM((2, s, LANE), BF16), pltpu.VMEM((2, s, LANE), F32)],
        compiler_params=_params(("parallel", "arbitrary")),
    )(q, k, vv, attn, dattn, lse, token)


def _hgrn_bwd(proj, lbl, states, do_raw):
    s = proj.shape[0]
    nt = s // TH
    nch = TH // HG_BLOCK

    def body(hq_ref, hf_ref, hi_ref, lbl_ref, st_ref, do_ref, dh_ref, dlbl_ref, dst, dlb):
        step = pl.program_id(1)

        @pl.when(step == 0)
        def _():
            dst[...] = jnp.zeros_like(dst)
            dlb[...] = jnp.zeros_like(dlb)

        m0, bd = _pair_masks()
        lb = _lower_bound(lbl_ref[...])
        gt = _hgrn_gates(hq_ref[...], hf_ref[...], lb)
        do = do_ref[...]
        qd, ki, ke = gt["qd"], gt["ki"], gt["ke"]
        v_b, do_b = hi_ref[...].astype(BF16), do.astype(BF16)
        qd_b, ki_b, ke_b = qd.astype(BF16), ki.astype(BF16), ke.astype(BF16)
        pairs = [slice(u * LANE, (u + 1) * LANE) for u in range(HG_PAIRS)]
        heads = [(lanes, m0 if hh == 0 else jnp.logical_not(m0)) for lanes in pairs for hh in range(2)]
        a_b = [jnp.where(gt["tri"], _mm_nt(jnp.where(mh, qd[:, lanes], 0.0).astype(BF16), ki_b[:, lanes]), 0.0).astype(BF16)
               for lanes, mh in heads]
        doh_b = [jnp.where(mh, do[:, lanes], 0.0).astype(BF16) for lanes, mh in heads]
        da_b = [jnp.where(gt["tri"], _mm_nt(d, v_b[:, lanes]), 0.0).astype(BF16) for d, (lanes, _) in zip(doh_b, heads)]
        dv_p, dqd_p, dki_p = [], [], []
        for u, lanes in enumerate(pairs):
            e, o = 2 * u, 2 * u + 1
            dv_p.append(_mm_tn(a_b[e], doh_b[e]) + _mm_tn(a_b[o], doh_b[o]))
            dqd_p.append(jnp.where(m0, _mm(da_b[e], ki_b[:, lanes]), _mm(da_b[o], ki_b[:, lanes])))
            dki_p.append(jnp.where(m0, _mm_tn(da_b[e], qd_b[:, lanes]), _mm_tn(da_b[o], qd_b[:, lanes])))
        fed = [_bmm_tn(_chunks(do_b[:, lanes]), _chunks(qd_b[:, lanes])) for lanes in pairs]
        leaving = []
        for u, lanes in enumerate(pairs):
            ds, left = dst[u], [None] * nch
            for n in reversed(range(nch)):
                left[n] = ds
                ds = gt["decay"][n][:, lanes] * ds + jnp.where(bd, fed[u][n], 0.0)
            dst[u] = ds
            leaving.append(jnp.stack(left))
        dke_p, dlast_p = [], []
        for u, lanes in enumerate(pairs):
            entering, leaving_b = st_ref[u], leaving[u].astype(BF16)
            dke3 = _bmm_nn(_chunks(v_b[:, lanes]), leaving_b)
            dv_p[u] = dv_p[u] + _bmm_nt(_chunks(ke_b[:, lanes]), leaving_b).reshape(TH, LANE)
            dqd_p[u] = dqd_p[u] + _bmm_nn(_chunks(do_b[:, lanes]), entering.astype(BF16)).reshape(TH, LANE)
            dke_p.append(dke3.reshape(TH, LANE))
            dlast_p.append(jnp.sum(dke3 * _chunks(ke[:, lanes]), axis=1, keepdims=True)
                           + jnp.sum(leaving[u] * entering, axis=1, keepdims=True) * gt["decay"][:, :, lanes])
        cat = lambda parts: jnp.concatenate(parts, axis=-1)
        dv, dqd, dki, dke, dlast = cat(dv_p), cat(dqd_p), cat(dki_p), cat(dke_p), cat(dlast_p)
        dk = dki * gt["ei"] + dke * gt["ee"]
        dcum = dqd * qd - dki * ki - dke * ke
        dg = _chunk_cumsum(dcum, reverse=True) + jnp.broadcast_to(dlast, (nch, HG_BLOCK, dlast.shape[-1])).reshape(dcum.shape)
        sig = gt["sig"]
        df = dg / gt["f"] - dk
        dlb[...] += jnp.sum(df * (1.0 - sig), axis=0, keepdims=True)
        dh_ref[0] = (dqd * gt["e"]).astype(BF16)
        dh_ref[1] = ((df * (1.0 - lb)) * sig * (1.0 - sig)).astype(BF16)
        dh_ref[2] = dv.astype(BF16)

        @pl.when(step == nt - 1)
        def _():
            lb = _lower_bound(lbl_ref[...])
            da0 = dlb[...] * lb * (1.0 - lb)
            dlbl_ref[...] = jnp.concatenate([da0, -da0], axis=0)

    wide = HG_PAIRS * LANE
    col = lambda base: pl.BlockSpec((TH, wide), lambda p, i: (nt - 1 - i, base // wide + p))
    tile = pl.BlockSpec((TH, wide), lambda p, i: (nt - 1 - i, p))
    sds = jax.ShapeDtypeStruct
    return pl.pallas_call(
        body, name="hgrn_bwd", grid=(NH // 2 // HG_PAIRS, nt),
        in_specs=[col(P_HQ), col(P_HF), col(P_HI), pl.BlockSpec((2, wide), lambda p, i: (0, p)),
                  pl.BlockSpec((HG_PAIRS, nch, LANE, LANE), lambda p, i: (p, nt - 1 - i, 0, 0)), tile],
        out_specs=[pl.BlockSpec((3, TH, wide), lambda p, i: (0, nt - 1 - i, p)), pl.BlockSpec((2, wide), lambda p, i: (0, p))],
        out_shape=[sds((3, s, 512), BF16), sds((2, 512), F32)],
        scratch_shapes=[pltpu.VMEM((HG_PAIRS, LANE, LANE), F32), pltpu.VMEM((1, wide), F32)],
        compiler_params=_params(("parallel", "arbitrary")),
    )(proj, proj, proj, lbl, states, do_raw)


def _norm_rows_bwd(v, r, g, dn):
    dgv = dn * g
    return r * dgv - v * (r * r * r) * jnp.mean(v * dgv, axis=-1, keepdims=True)


def _qkv_bwd(proj, dq, dk, dvv, g_q, g_kv, w_uq_p, w_k_p, w_v_p, rc, rs1, rs2):
    s = proj.shape[0]
    head_q = QK_NOPE + QK_ROPE

    def body(cq_ref, ckv_ref, dq_ref, dk_ref, dv_ref, gq_ref, gkv_ref, wq_ref, wk_ref, wv_ref, c_ref, s1_ref, s2_ref,
             dcq_ref, dckv_ref, dkpe_ref, dwq_out, dwkv_out, dgq_ref, dgkv_ref, dwq_ref, dwk_ref, dwv_ref):
        @pl.when(pl.program_id(0) == 0)
        def _():
            for rf in (dwq_ref, dwk_ref, dwv_ref, dgq_ref, dgkv_ref):
                rf[...] = jnp.zeros_like(rf)

        c, s1, s2 = c_ref[...], s1_ref[...], s2_ref[...]
        cq, ckv = cq_ref[...], ckv_ref[...]
        gq, gkv = gq_ref[...], gkv_ref[...]
        cqn, rq = _norm_rows(cq, gq)
        ckvn, rkv = _norm_rows(ckv, gkv)
        cqn_b, ckvn_b = cqn.astype(BF16), ckvn.astype(BF16)
        dqf = jnp.concatenate([_rope_t(dq_ref[h], c, s1, s2) for h in range(NH)], axis=1).astype(BF16)
        dkf = jnp.concatenate([dk_ref[h] for h in range(NH)], axis=1).astype(BF16)
        dvf = jnp.concatenate([dv_ref[h] for h in range(NH)], axis=1).astype(BF16)
        dkpe = dk_ref[0]
        for h in range(1, NH):
            dkpe = dkpe + dk_ref[h]
        lane = lax.broadcasted_iota(jnp.int32, (TM, LANE), 1)
        dkpe = jnp.where((lane >= QK_NOPE) & (lane < QK_NOPE + QK_ROPE), dkpe, 0.0)
        dkpe_ref[...] = _rope_t(dkpe, c, s1, s2).astype(BF16)
        dcqn = _mm_nt(dqf, wq_ref[...])
        pair = lambda a, t: a[:, t * 2 * LANE:(t + 1) * 2 * LANE]
        dckvn = sum(_mm_nt(pair(dkf, t), wk_ref[t]) + _mm_nt(pair(dvf, t), wv_ref[t]) for t in range(N_CHIPS))
        dwq_ref[...] += _mm_tn(cqn_b, dqf)
        dwk_ref[...] += _mm_tn(ckvn_b, dkf)
        dwv_ref[...] += _mm_tn(ckvn_b, dvf)
        dgq_ref[...] += jnp.sum(dcqn * (cq * rq), axis=0, keepdims=True)
        dgkv_ref[...] += jnp.sum(dckvn * (ckv * rkv), axis=0, keepdims=True)
        dcq_ref[...] = _norm_rows_bwd(cq, rq, gq, dcqn).astype(BF16)
        dckv_ref[...] = _norm_rows_bwd(ckv, rkv, gkv, dckvn).astype(BF16)

        @pl.when(pl.program_id(0) == pl.num_programs(0) - 1)
        def _():
            blk = lambda ref, h: ref[:, h * LANE:(h + 1) * LANE]
            lane = lax.broadcasted_iota(jnp.int32, (Q_LORA, LANE), 1)
            for j in range(NH * head_q // LANE):
                h0, w0 = divmod(j * LANE, head_q)
                first = blk(dwq_ref, h0) if w0 == 0 else pltpu.roll(blk(dwq_ref, h0), LANE - w0, 1)
                second = pltpu.roll(blk(dwq_ref, h0 + 1), head_q - w0, 1)
                dwq_out[:, j * LANE:(j + 1) * LANE] = jnp.where(lane < head_q - w0, first, second).astype(BF16)
            lane = lax.broadcasted_iota(jnp.int32, (KV_LORA, LANE), 1)
            for h in range(NH):
                vals = blk(dwv_ref, h) if h % 2 else pltpu.roll(blk(dwv_ref, h), V_DIM, 1)
                both = jnp.where(lane < QK_NOPE, blk(dwk_ref, h), vals).astype(BF16)
                dwkv_out[h // 2, :, (h % 2) * LANE:(h % 2 + 1) * LANE] = both

    row = lambda w, j=0: pl.BlockSpec((TM, w), lambda i: (i, j))
    full = lambda a: pl.BlockSpec(a.shape, lambda i: (0,) * a.ndim)
    acc = lambda *shape: pl.BlockSpec(shape, lambda i: (0,) * len(shape))
    heads = pl.BlockSpec((NH, TM, LANE), lambda i: (0, i, 0))
    sds = jax.ShapeDtypeStruct
    return pl.pallas_call(
        body, name="qkv_bwd", grid=(s // TM,),
        in_specs=[row(Q_LORA, P_CQ // Q_LORA), row(KV_LORA, P_CKV // KV_LORA), heads, heads, heads,
                  full(g_q), full(g_kv), full(w_uq_p), full(w_k_p), full(w_v_p), row(LANE), row(LANE), row(LANE)],
        out_specs=[row(Q_LORA), row(KV_LORA), row(LANE), acc(Q_LORA, NH * head_q), acc(NH // 2, KV_LORA, 2 * LANE),
                   acc(1, Q_LORA), acc(1, KV_LORA)],
        out_shape=[sds((s, Q_LORA), BF16), sds((s, KV_LORA), BF16), sds((s, LANE), BF16), sds((Q_LORA, NH * head_q), BF16),
                   sds((NH // 2, KV_LORA, 2 * LANE), BF16), sds((1, Q_LORA), F32), sds((1, KV_LORA), F32)],
        scratch_shapes=[pltpu.VMEM((Q_LORA, D), F32), pltpu.VMEM((KV_LORA, D), F32), pltpu.VMEM((KV_LORA, D), F32)],
        compiler_params=_params(("arbitrary",)),
    )(proj, proj, dq, dk, dvv, g_q, g_kv, w_uq_p, w_k_p, w_v_p, rc, rs1, rs2)


def _front_bwd(x, dout, dmg, dga, dh3, dgb, dcq, dckv, dkpe, g_pre, w_in_t, w_kpe, token):
    s = x.shape[0]

    def body(x_ref, do_ref, dmg_ref, dga_ref, dh3_ref, dgb_ref, dcq_ref, dckv_ref, dkpe_ref, g_ref, w_ref, k_ref, token_ref,
             gx_ref, dg_ref):
        @pl.when(pl.program_id(0) == 0)
        def _():
            dg_ref[...] = jnp.zeros_like(dg_ref)

        xv, g = x_ref[...], g_ref[...]
        _, r = _norm_rows(xv, g)
        pieces = ((dmg_ref[...], O_MERGE), (dga_ref[...], O_GA), (dh3_ref[0], O_HQ), (dh3_ref[1], O_HF), (dh3_ref[2], O_HI),
                  (dgb_ref[...], O_GB), (dcq_ref[...], O_CQ), (dckv_ref[...], O_CKV))
        dh = _mm(dkpe_ref[...], k_ref[...])
        for piece, off in pieces:
            dh = dh + _mm(piece, w_ref[off:off + piece.shape[1], :])
        dg_ref[...] += jnp.sum(dh * (xv * r), axis=0, keepdims=True)
        gx_ref[...] = do_ref[...] + _norm_rows_bwd(xv, r, g, dh)

    row = lambda w: pl.BlockSpec((TM, w), lambda i: (i, 0))
    full = lambda a: pl.BlockSpec(a.shape, lambda i: (0,) * a.ndim)
    sds = jax.ShapeDtypeStruct
    return pl.pallas_call(
        body, name="front_bwd", grid=(s // TM,),
        in_specs=[row(D), row(D), row(2048), row(512), pl.BlockSpec((3, TM, 512), lambda i: (0, i, 0)), row(512), row(Q_LORA),
                  row(KV_LORA), row(LANE), full(g_pre), full(w_in_t), full(w_kpe), pl.BlockSpec(memory_space=pl.ANY)],
        out_specs=[row(D), pl.BlockSpec((1, D), lambda i: (0, 0))],
        out_shape=[sds((s, D), F32), sds((1, D), F32)],
        compiler_params=_params(("arbitrary",)),
    )(x, dout, dmg, dga, dh3, dgb, dcq, dckv, dkpe, g_pre, w_in_t, w_kpe, token)


TK_GRAD = 1024


QKV_ROWS = Q_LORA + KV_LORA + QK_ROPE
D_IN = O_MERGE + 2048


def _win_grad(h, pieces, offsets, name, whole=None):
    s = h.shape[0]
    n = len(pieces)
    new = whole is None
    shapes = [(p.shape[0] * p.shape[2], D) if p.ndim == 3 else (p.shape[1], D) for p in pieces]

    def body(h_ref, *refs):
        d_refs, whole_ref, scratch = refs[:n], refs[n if new else n + 1], refs[n + (1 if new else 2):]
        sums, rounded, sems = scratch[:n], scratch[n:2 * n], scratch[-1]

        @pl.when(pl.program_id(0) == 0)
        def _():
            for s_ref in sums:
                s_ref[...] = jnp.zeros_like(s_ref)

        hv = h_ref[...]
        for d_ref, s_ref in zip(d_refs, sums):
            if len(d_ref.shape) == 3:
                w = d_ref.shape[2]
                for k in range(d_ref.shape[0]):
                    s_ref[k * w:(k + 1) * w] += _mm_tn(d_ref[k], hv)
            else:
                s_ref[...] += _mm_tn(d_ref[...], hv)

        @pl.when(pl.program_id(0) == pl.num_programs(0) - 1)
        def _():
            copies = []
            for j, (s_ref, r_ref, row) in enumerate(zip(sums, rounded, offsets)):
                r_ref[...] = s_ref[...].astype(BF16)
                copies.append(pltpu.make_async_copy(r_ref, whole_ref.at[pl.ds(row, r_ref.shape[0])], sems.at[j]))
            if new:
                zeros = scratch[2 * n]
                zeros[...] = jnp.zeros_like(zeros)
                copies.append(pltpu.make_async_copy(zeros, whole_ref.at[pl.ds(0, QKV_ROWS)], sems.at[n]))
            for copy in copies:
                copy.start()
            for copy in copies:
                copy.wait()

    def in_spec(p):
        if p.ndim == 3:
            return pl.BlockSpec((p.shape[0], TK_GRAD, p.shape[2]), lambda kk: (0, kk, 0))
        return pl.BlockSpec((TK_GRAD, p.shape[1]), lambda kk: (kk, 0))

    anywhere = pl.BlockSpec(memory_space=pl.ANY)
    return pl.pallas_call(
        body, name=name, grid=(s // TK_GRAD,),
        in_specs=[pl.BlockSpec((TK_GRAD, D), lambda kk: (kk, 0))] + [in_spec(p) for p in pieces] + ([] if new else [anywhere]),
        out_specs=anywhere, out_shape=jax.ShapeDtypeStruct((D_IN, D), BF16),
        input_output_aliases={} if new else {n + 1: 0},
        scratch_shapes=[pltpu.VMEM(sh, F32) for sh in shapes] + [pltpu.VMEM(sh, BF16) for sh in shapes]
        + ([pltpu.VMEM((QKV_ROWS, D), BF16)] if new else []) + [pltpu.SemaphoreType.DMA((n + 1,))],
        compiler_params=_params(("arbitrary",)),
    )(h, *pieces, *([] if new else [whole]))


def _win_grad_qkv(h, dcq, dckv, dkpe, share):
    s = h.shape[0]
    half = QKV_ROWS // 2

    def body(h_ref, cq_ref, ckv_ref, kpe_ref, o_ref, *scratch):
        sums = scratch[0] if share else o_ref

        @pl.when(pl.program_id(0) == 0)
        def _():
            sums[...] = jnp.zeros_like(sums)

        hv = h_ref[...]
        g_cq = _mm_tn(cq_ref[...], hv)
        sums[0] += g_cq[:half]
        sums[1, 0:Q_LORA - half] += g_cq[half:]
        sums[1, Q_LORA - half:Q_LORA + KV_LORA - half] += _mm_tn(ckv_ref[...], hv)
        sums[1, Q_LORA + KV_LORA - half:] += _mm_tn(kpe_ref[...], hv)[QK_NOPE:QK_NOPE + QK_ROPE]

        if share:
            _, theirs, send_sem, recv_sem = scratch

            @pl.when(pl.program_id(0) == pl.num_programs(0) - 1)
            def _():
                c = lax.axis_index("c")
                copy = _remote(sums.at[1 - c], theirs, send_sem, recv_sem, 0, (lax.axis_index("x"), lax.axis_index("y"), 1 - c))
                copy.start()
                copy.wait()
                o_ref[...] = (sums[c] + theirs[...]).astype(BF16)

    rows = lambda a: pl.BlockSpec((TK_GRAD, a.shape[1]), lambda kk: (kk, 0))
    sem = pltpu.SemaphoreType.DMA((1,))
    shape, dtype = ((half, D), BF16) if share else ((2, half, D), F32)
    return pl.pallas_call(
        body, name="win_grad_qkv", grid=(s // TK_GRAD,), in_specs=[rows(h), rows(dcq), rows(dckv), rows(dkpe)],
        out_specs=pl.BlockSpec(shape, lambda kk: (0,) * len(shape)), out_shape=jax.ShapeDtypeStruct(shape, dtype),
        scratch_shapes=[pltpu.VMEM((2, half, D), F32), pltpu.VMEM((half, D), F32), sem, sem] if share else [],
        compiler_params=_params(("arbitrary",)),
    )(h, dcq, dckv, dkpe)


def _pad_wuq(w_uq):
    rows = w_uq.shape[0]
    w = w_uq.reshape(rows, NH, QK_NOPE + QK_ROPE)
    return jnp.pad(w, ((0, 0), (0, 0), (0, LANE - QK_NOPE - QK_ROPE))).reshape(rows, NH * LANE)


def _pad_wukv(w_ukv):
    heads = w_ukv.shape[1] // (QK_NOPE + V_DIM)
    w = w_ukv.reshape(KV_LORA, heads, QK_NOPE + V_DIM)
    w_k = jnp.pad(w[:, :, :QK_NOPE], ((0, 0), (0, 0), (0, LANE - QK_NOPE))).reshape(KV_LORA, heads * LANE)
    wv = w[:, :, QK_NOPE:].reshape(KV_LORA, heads // 2, 2, 1, V_DIM)
    eye = jnp.eye(2, dtype=w.dtype).reshape(1, 1, 2, 2, 1)
    return w_k, (wv * eye).reshape(KV_LORA, heads * LANE)


def _local_step(x, tgt, g_pre, w_in_t, b_gate, g_q, g_kv, lb_logits, g_hgrn, g_post, weights, exchange=None):
    s = x.shape[0]
    w_kpe = _kpe_block(w_in_t)
    rc, rs1, rs2 = _rope_tables(s)
    g_hg = jnp.tile(g_hgrn, (1, NH))

    proj, h = _front_fwd(x, g_pre, w_in_t, w_kpe, weights.tokens)
    w_uq_p, w_k_p, w_v_p = weights.qkv(h)
    q, k, vv = _qkv_fwd(proj, g_q, g_kv, w_uq_p, w_k_p, w_v_p, rc, rs1, rs2)
    attn, lse = _attn_fwd(q, k, vv)
    o_raw, states = _hgrn_fwd(proj, lb_logits)
    wa, wb, w_out = weights.mid(o_raw)
    (loss, dout, dattn, dga, dor, dgb, dmg, d_wout, d_wa, d_wb, d_gpost, d_bgate, d_ghg) = _mid(
        proj, attn, o_raw, x, tgt, g_hg, b_gate, g_post, wa, wb, w_out)
    d_win = _win_grad(h, [dmg, dga, dgb], [O_MERGE, O_GA, O_GB], "win_grad_mid")
    dh3, d_lbl = _hgrn_bwd(proj, lb_logits, states, dor)
    d_win = _win_grad(h, [dh3], [O_HQ], "win_grad_hgrn", d_win)
    early = dict(w_in=d_win, w_branch_a=d_wa, w_branch_b=d_wb, w_out=d_wout)
    token = exchange.start_early(early) if exchange else jnp.zeros((8, LANE), F32)
    dq, dk, dvv = _attn_bwd(q, k, vv, attn, dattn, lse, token)
    dcq, dckv, dkpe, d_wuq, d_wukv, d_gq, d_gkv = _qkv_bwd(proj, dq, dk, dvv, g_q, g_kv, w_uq_p, w_k_p, w_v_p, rc, rs1, rs2)
    late = dict(w_in_qkv=_win_grad_qkv(h, dcq, dckv, dkpe, share=exchange is not None), w_uq=d_wuq, w_ukv=d_wukv)
    token = exchange.start_late(late) if exchange else jnp.zeros((8, LANE), F32)
    grad_x, d_gpre = _front_bwd(x, dout, dmg, dga, dh3, dgb, dcq, dckv, dkpe, g_pre, w_in_t, w_kpe, token)
    vec_grads = dict(g_pre=d_gpre, b_gate=d_bgate, g_q=d_gq, g_kv=d_gkv, lb_logits=d_lbl, g_hgrn=d_ghg, g_post=d_gpost)
    return loss, grad_x, dict(early, **late), vec_grads


SHARD_SHAPES = (("w_in", (1416, 1024)), ("w_uq", (192, 768)), ("w_ukv", (256, 256)), ("w_branch_a", (512, 256)),
                ("w_branch_b", (512, 256)), ("w_out", (256, 1024)))
BIG = tuple(n for n, _ in SHARD_SHAPES)
ROW_SHARDED = ("w_in", "w_uq", "w_out")
N_CHIPS = 4
W_IN_FORWARD_CUT = 704


def _to_block(name, a):
    return a[0].T if name == "w_in" else a[0]


def _from_block(name, a):
    return a.T[None] if name == "w_in" else a[None]
VEC_ROWS = (("g_pre", 0, 1024), ("b_gate", 1, 2048), ("g_q", 2, 768), ("g_kv", 3, 256), ("g_hgrn", 6, 64), ("g_post", 7, 1024))
VEC_LB_ROW = 4
VEC_SHAPE = (8, 2048)


def _split_by_chip(name, g):
    a, b = dict(SHARD_SHAPES)[name]
    return g.reshape(N_CHIPS, a, b) if name in ROW_SHARDED else g.reshape(a, N_CHIPS, b).transpose(1, 0, 2)


def _join_chips(name, w):
    a, b = dict(SHARD_SHAPES)[name]
    return w.reshape(N_CHIPS * a, b) if name in ROW_SHARDED else w.transpose(1, 0, 2).reshape(a, N_CHIPS * b)


MESH = pl.DeviceIdType.MESH
HBM = pl.BlockSpec(memory_space=pltpu.HBM)


def _mesh_place():
    x, y, c = lax.axis_index("x"), lax.axis_index("y"), lax.axis_index("c")
    return x, y, c, 2 * x + y, [(1 - x, y), (x, 1 - y), (1 - x, 1 - y)]


def _remote(src, dst, send_sems, recv_sems, k, to):
    return pltpu.make_async_remote_copy(src_ref=src, dst_ref=dst, send_sem=send_sems.at[k], recv_sem=recv_sems.at[k],
                                        device_id=to, device_id_type=MESH)


def _gather_w_in(shard):
    a, b = shard.shape
    cut = W_IN_FORWARD_CUT

    def body(src, out, ici_send, ici_recv, d2d_send, d2d_recv, local_sem):
        x, y, c = lax.axis_index("x"), lax.axis_index("y"), lax.axis_index("c")
        me, xn, yn, dg = 2 * x + y, 2 * (1 - x) + y, 2 * x + (1 - y), 2 * (1 - x) + (1 - y)
        to_x, to_y, sibling = (1 - x, y, c), (x, 1 - y, c), (x, y, 1 - c)
        first, rest = pl.ds(0, cut), pl.ds(cut, a - cut)

        whole = lambda ref, which: ref.at[:, pl.ds(pl.multiple_of(which * (b // 2), b // 2), b // 2)]
        own = pltpu.make_async_copy(src, out.at[me], local_sem)
        own.start()
        sends = [_remote(whole(src, c), whole(out.at[me], c), ici_send, ici_recv, 0, to_x),
                 _remote(whole(src, c), whole(out.at[me], c), ici_send, ici_recv, 1, to_y)]
        for cp in sends:
            cp.start()

        def landed(slot, rows, k, d2d_k, src_dev):
            piece = whole(out.at[slot], c) if rows is None else out.at[slot].at[rows, pl.ds(pl.multiple_of(c * (b // 2), b // 2), b // 2)]
            _remote(piece, piece, ici_send, ici_recv, k, src_dev).wait_recv()
            cp = _remote(piece, piece, d2d_send, d2d_recv, d2d_k, sibling)
            cp.start()
            sends.append(cp)
            return piece

        def pass_on(slot, rows, k, to):
            piece = out.at[slot].at[rows, pl.ds(pl.multiple_of(c * (b // 2), b // 2), b // 2)]
            cp = _remote(piece, piece, ici_send, ici_recv, k, to)
            cp.start()
            sends.append(cp)

        landed(xn, None, 0, 0, to_x)
        pass_on(xn, first, 2, to_y)
        landed(yn, None, 1, 1, to_y)
        pass_on(yn, rest, 3, to_x)
        landed(dg, first, 2, 2, to_y)
        landed(dg, rest, 3, 3, to_x)
        other = pl.ds(pl.multiple_of((1 - c) * (b // 2), b // 2), b // 2)
        for d2d_k, (slot, rows) in enumerate(((xn, None), (yn, None), (dg, first), (dg, rest))):
            piece = out.at[slot].at[:, other] if rows is None else out.at[slot].at[rows, other]
            _remote(piece, piece, d2d_send, d2d_recv, d2d_k, sibling).wait_recv()
        for cp in sends:
            cp.wait_send()
        own.wait()

    sems = pltpu.SemaphoreType.DMA((4,))
    return pl.pallas_call(
        body, name="gather_w_in", in_specs=[HBM], out_specs=HBM,
        out_shape=jax.ShapeDtypeStruct((N_CHIPS, a, b), shard.dtype),
        scratch_shapes=[sems, sems, sems, sems, pltpu.SemaphoreType.DMA],
        compiler_params=pltpu.CompilerParams(has_side_effects=True),
    )(shard)


SEM =pl.BlockSpec(memory_space=pltpu.SEMAPHORE)
DATAFLOW = pltpu.SideEffectType.DATAFLOW_SIDE_EFFECTING


def _exchange_copies(srcs, to_first, src_refs, land_refs, send_sems, recv_sems):
    x, y, c, me, chips = _mesh_place()
    n = len(srcs)
    sends, recvs = [], []
    for k in range(n):
        if k in to_first:
            base = 3 * n + 4 * to_first.index(k)
            sends.append((me != 0, pltpu.make_async_remote_copy(
                src_ref=src_refs[k], dst_ref=land_refs[k].at[me], send_sem=send_sems.at[base], recv_sem=recv_sems.at[base + me],
                device_id=(0, 0, c), device_id_type=MESH)))
            for s in range(1, N_CHIPS):
                recvs.append((me == 0, pltpu.make_async_remote_copy(
                    src_ref=src_refs[k], dst_ref=land_refs[k].at[s], send_sem=send_sems.at[base], recv_sem=recv_sems.at[base + s],
                    device_id=(s // 2, s % 2, c), device_id_type=MESH)))
        else:
            slab = (lambda t, k=k: src_refs[k]) if srcs[k].ndim == 2 else (lambda t, k=k: src_refs[k].at[t])
            for j, (px, py) in enumerate(chips):
                sends.append((None, _remote(slab(2 * px + py), land_refs[k].at[me], send_sems, recv_sems, 3 * k + j, (px, py, c))))
                recvs.append((None, _remote(slab(me), land_refs[k].at[2 * px + py], send_sems, recv_sems, 3 * k + j, (px, py, c))))
    return sends, recvs


def _when(pred, fn):
    if pred is None:
        fn()
    else:
        pl.when(pred)(fn)


def _exchange_start(srcs, to_first, name, after=None):
    n = len(srcs)
    n_sems = 3 * n + 4 * len(to_first)
    lands = [lax.empty((N_CHIPS,) + s.shape[-2:], s.dtype) for s in srcs]
    extra = [] if after is None else [after]

    def body(*refs):
        src_refs, land_refs = refs[:n], refs[n:2 * n]
        send_sems, recv_sems, token = refs[2 * n + len(extra)], refs[2 * n + len(extra) + 1], refs[-1]
        sends, _ = _exchange_copies(srcs, to_first, src_refs, land_refs, send_sems, recv_sems)
        for pred, cp in sends:
            _when(pred, cp.start)
        token[...] = jnp.zeros_like(token)

    hbm = lambda a: pltpu.HBM(a.shape, a.dtype)
    res = pl.pallas_call(
        body, name=name,
        out_shape=[pltpu.SemaphoreType.DMA((n_sems,)), pltpu.SemaphoreType.DMA((n_sems,))] + [hbm(a) for a in srcs + lands]
        + [jax.ShapeDtypeStruct((8, LANE), F32)],
        in_specs=[HBM] * (2 * n) + [pl.BlockSpec(memory_space=pl.ANY)] * len(extra),
        out_specs=[SEM, SEM] + [HBM] * (2 * n) + [pl.BlockSpec(memory_space=pltpu.VMEM)],
        input_output_aliases={i: 2 + i for i in range(2 * n)},
        compiler_params=pltpu.CompilerParams(has_side_effects=DATAFLOW),
    )(*[pltpu.with_memory_space_constraint(a, pltpu.HBM) for a in srcs + lands], *extra)
    return res[:-1], res[-1]


def _exchange_wait(srcs, to_first, started, after, name):
    n = len(srcs)
    send_sems, recv_sems, thru = started[0], started[1], started[2:]

    def body(*refs):
        src_refs, land_refs, send_ref, recv_ref = refs[:n], refs[n:2 * n], refs[2 * n], refs[2 * n + 1]
        sends, recvs = _exchange_copies(srcs, to_first, src_refs, land_refs, send_ref, recv_ref)
        for pred, cp in sends:
            _when(pred, cp.wait_send)
        for pred, cp in recvs:
            _when(pred, cp.wait_recv)

    res = pl.pallas_call(
        body, name=name, out_shape=[pltpu.HBM(a.shape, a.dtype) for a in thru],
        in_specs=[HBM] * (2 * n) + [SEM, SEM, pl.BlockSpec(memory_space=pl.ANY)], out_specs=[HBM] * (2 * n),
        input_output_aliases={i: i for i in range(2 * n)},
        compiler_params=pltpu.CompilerParams(has_side_effects=DATAFLOW),
    )(*thru, send_sems, recv_sems, after)
    return res[:n], res[n:]


ROW_TILE = 256
COL_TILE = 256


def _block_tiling(a, b):
    if a <= ROW_TILE or a % ROW_TILE == 0:
        ta = min(a, ROW_TILE)
        return a // ta, (ta, b), lambda i: (i, 0)
    return b // COL_TILE, (a, COL_TILE), lambda i: (0, i)


def _sum_share_small(lands, owns, name, after=None):
    n = len(lands)
    extra = [] if after is None else [after]

    def body(*refs):
        p_refs, own_refs = refs[:n], refs[n:2 * n]
        o_refs, mine, theirs = (refs[j * n + len(extra):(j + 1) * n + len(extra)] for j in (2, 3, 4))
        send_sems, recv_sems = refs[5 * n + len(extra):]
        me = 2 * lax.axis_index("x") + lax.axis_index("y")
        sibling = (lax.axis_index("x"), lax.axis_index("y"), 1 - lax.axis_index("c"))
        copies = [_remote(mine[k], theirs[k], send_sems, recv_sems, k, sibling) for k in range(n)]
        for p_ref, own_ref, mine_ref, copy in zip(p_refs, own_refs, mine, copies):
            own = (own_ref[me] if len(own_ref.shape) == 3 else own_ref[...]).astype(F32)
            slot = lambda t: jnp.where(me == t, own, p_ref[t].astype(F32))
            mine_ref[...] = ((slot(0) + slot(1)) + slot(2)) + slot(3)
            copy.start()
        for o_ref, mine_ref, theirs_ref, copy in zip(o_refs, mine, theirs, copies):
            copy.wait()
            o_ref[...] = mine_ref[...] + theirs_ref[...]

    vmem = pl.BlockSpec(memory_space=pltpu.VMEM)
    kept = [pltpu.VMEM(land.shape[1:], F32) for land in lands]
    sems = pltpu.SemaphoreType.DMA((n,))
    return pl.pallas_call(
        body, name=name, in_specs=[vmem] * (2 * n) + [pl.BlockSpec(memory_space=pl.ANY)] * len(extra), out_specs=[vmem] * n,
        out_shape=[jax.ShapeDtypeStruct(land.shape[1:], F32) for land in lands], scratch_shapes=kept + kept + [sems, sems],
        compiler_params=_params(()),
    )(*lands, *owns, *extra)


def _adamw_small(grads, states, name):
    n = len(grads)

    def body(*refs):
        ins, outs = refs[:4 * n], refs[4 * n:]
        for k in range(n):
            g_ref, w_ref, m_ref, v_ref = ins[4 * k:4 * k + 4]
            g = g_ref[...]
            outs[4 * k][...] = g
            outs[4 * k + 1][...], outs[4 * k + 2][...], outs[4 * k + 3][...] = _adamw_math(g, w_ref[...], m_ref[...], v_ref[...])

    args = [t for k in range(n) for t in (grads[k], *states[k])]
    res = pl.pallas_call(body, name=name, out_shape=[jax.ShapeDtypeStruct(grads[k].shape, F32) for k in range(n) for _ in range(4)],
                         compiler_params=_params(()))(*args)
    return [tuple(res[4 * k:4 * k + 4]) for k in range(n)]


class _LaterWeights:
    MID = ("w_branch_a", "w_branch_b", "w_out")

    def __init__(self, blocks, after):
        self.qkv_blocks = [_pad_wuq(blocks["w_uq"]), *_pad_wukv(blocks["w_ukv"])]
        self.mid_blocks = [blocks[n] for n in self.MID]
        self.qkv_started, t1 = _exchange_start(self.qkv_blocks, (), "weights_qkv_start", after)
        self.mid_started, t2 = _exchange_start(self.mid_blocks, (), "weights_mid_start", after)
        self.tokens = [t1, t2]

    @staticmethod
    def _whole(blocks, joined, started, after, name):
        _, landed = _exchange_wait(blocks, (), started, after, name)
        me = 2 * lax.axis_index("x") + lax.axis_index("y")
        out = []
        for block, land, join in zip(blocks, landed, joined):
            w = lax.dynamic_update_index_in_dim(land, block, me, 0)
            out.append(w.reshape(N_CHIPS * block.shape[0], block.shape[1]) if join else w)
        return out

    def qkv(self, after):
        return self._whole(self.qkv_blocks, (True, False, False), self.qkv_started, after, "weights_qkv_wait")

    def mid(self, after):
        return self._whole(self.mid_blocks, (False, False, True), self.mid_started, after, "weights_mid_wait")


class _GradExchange:
    EARLY = ("w_in", "w_branch_a", "w_branch_b", "w_out")
    LATE = ("w_uq", "w_ukv")

    def __init__(self, state):
        self.state = state
        self.outs = {}

    def start_early(self, g):
        self.early = [(g[n] if g[n].ndim == 3 else _split_by_chip(n, g[n])).astype(BF16) for n in self.EARLY]
        self.early_started, token = _exchange_start(self.early, (), "grads_early_start")
        return token

    def start_late(self, g):
        self.early, self.early_landed = _exchange_wait(self.early, (), self.early_started, g["w_uq"], "grads_early_wait")
        self.late = [_split_by_chip("w_uq", g["w_uq"]), g["w_ukv"], g["w_in_qkv"]]
        self.late_started, token = _exchange_start(self.late, (2,), "grads_late_start")
        names = self.EARLY[1:]
        grads = _sum_share_small(self.early_landed[1:], self.early[1:], "sum_early", after=token)
        for n, out in zip(names, _adamw_small(grads, [self.state[n] for n in names], "adamw_early")):
            self.outs[n] = out
        return self.outs[names[-1]][0]

    def finish(self, after):
        late, late_landed = _exchange_wait(self.late, (2,), self.late_started, after, "grads_late_wait")
        grad = _sum_exchange(self.early_landed[0], self.early[0], late_landed[2], late[2], "sum_w_in")
        self.outs["w_in"] = _adamw(grad, *self.state["w_in"], "adamw_w_in")
        return list(late[:2]), list(late_landed[:2])


def _adamw_math(g, w, m, v):
    nm = ADAM_B1 * m + (1.0 - ADAM_B1) * g
    nv = ADAM_B2 * v + (1.0 - ADAM_B2) * (g * g)
    m_hat = nm / (1.0 - ADAM_B1 ** ADAM_STEP)
    v_hat = nv / (1.0 - ADAM_B2 ** ADAM_STEP)
    return -ADAM_LR * (m_hat / (jnp.sqrt(v_hat) + ADAM_EPS) + ADAM_WD * w), nm, nv


def _sum_exchange(land, own, first_land, first_own, name):
    _, a, b = land.shape
    steps, tile, at = _block_tiling(a, b)
    assert tile[0] == a, "the extra rows need whole columns in a step"
    r = first_own.shape[0]

    def body(me_ref, p_ref, own_ref, fp_ref, fo_ref, g_ref, mine_s, theirs_s, send_sems, recv_sems):
        pass_, i = pl.program_id(0), pl.program_id(1)
        me, c = me_ref[0], lax.axis_index("c")
        sibling = (lax.axis_index("x"), lax.axis_index("y"), 1 - c)
        copy = _remote(mine_s.at[i], theirs_s.at[i], send_sems, recv_sems, i, sibling)

        @pl.when(pass_ == 0)
        def _():
            own = own_ref[...].astype(F32)
            slot = lambda t: jnp.where(me == t, own, p_ref[t].astype(F32))
            mine_s[i] = ((slot(0) + slot(1)) + slot(2)) + slot(3)

            @pl.when(me == 0)
            def _():
                f = lambda t: fp_ref[t].astype(F32)
                rows = pl.ds(pl.multiple_of(c * r, 8), r)
                mine_s[i, rows, :] += ((fo_ref[...].astype(F32) + f(1)) + f(2)) + f(3)

            copy.start()

        @pl.when(pass_ == 1)
        def _():
            copy.wait()
            g_ref[...] = mine_s[i] + theirs_s[i]

    first = lambda p, i: i * (1 - p) + (steps - 1) * p
    in_specs = [pl.BlockSpec((N_CHIPS,) + tile, lambda p, i, me: (0,) + at(first(p, i))),
                pl.BlockSpec((None,) + tile, lambda p, i, me: (me[0],) + at(first(p, i))),
                pl.BlockSpec((N_CHIPS, r, tile[1]), lambda p, i, me: (0,) + at(first(p, i))),
                pl.BlockSpec((r, tile[1]), lambda p, i, me: at(first(p, i)))]
    me = jnp.reshape(2 * lax.axis_index("x") + lax.axis_index("y"), (1,)).astype(jnp.int32)
    kept = pltpu.VMEM((steps,) + tile, F32)
    sems = pltpu.SemaphoreType.DMA((steps,))
    return pl.pallas_call(
        body, name=name,
        grid_spec=pltpu.PrefetchScalarGridSpec(num_scalar_prefetch=1, grid=(2, steps), in_specs=in_specs,
                                               out_specs=pl.BlockSpec(tile, lambda p, i, me: at(i * p)),
                                               scratch_shapes=[kept, kept, sems, sems]),
        out_shape=jax.ShapeDtypeStruct((a, b), F32),
        compiler_params=_params(("arbitrary", "arbitrary")),
    )(me, land, own, first_land, first_own)


def _adamw(g, w, m, v, name):
    a, b = g.shape
    steps, tile, at = _block_tiling(a, b)

    def body(g_ref, w_ref, m_ref, v_ref, go_ref, d_ref, nm_ref, nv_ref):
        g = g_ref[...]
        go_ref[...] = g
        d_ref[...], nm_ref[...], nv_ref[...] = _adamw_math(g, w_ref[...], m_ref[...], v_ref[...])

    spec = pl.BlockSpec(tile, at)
    sds = jax.ShapeDtypeStruct((a, b), F32)
    return pl.pallas_call(
        body, name=name, grid=(steps,), in_specs=[spec] * 4, out_specs=[spec] * 4, out_shape=[sds] * 4,
        compiler_params=_params(("parallel",)),
    )(g, w, m, v)


LOSS_AT = (2, 1024)


def _vec_pack(vg, loss):
    names = [n for n, _, _ in VEC_ROWS]

    def body(*refs):
        o_ref = refs[-1]
        lb_ref, loss_ref = refs[len(names)], refs[len(names) + 1]
        o_ref[...] = jnp.zeros_like(o_ref)
        o_ref[LOSS_AT[0]:LOSS_AT[0] + 1, LOSS_AT[1]:LOSS_AT[1] + LANE] = jnp.broadcast_to(loss_ref[...], (1, LANE))
        for (name, row, size), ref in zip(VEC_ROWS, refs):
            if name == "g_hgrn":
                r = lax.broadcasted_iota(jnp.int32, (NH * V_DIM, LANE), 0)
                c = lax.broadcasted_iota(jnp.int32, (NH * V_DIM, LANE), 1)
                fold = ((r % V_DIM) == c).astype(F32)
                o_ref[row:row + 1, 0:LANE] = jnp.dot(ref[...], fold, precision=HIGHEST, preferred_element_type=F32)
            else:
                o_ref[row:row + 1, 0:size] = ref[...]
        o_ref[VEC_LB_ROW:VEC_LB_ROW + 2, 0:512] = lb_ref[...]

    return pl.pallas_call(body, name="vec_pack", out_shape=jax.ShapeDtypeStruct(VEC_SHAPE, F32))(
        *[vg[n] for n in names], vg["lb_logits"], loss)


def _adamw_vec(block, w, m, v):
    names = [n for n, _, _ in VEC_ROWS] + ["lb_logits"]
    k = len(names)

    def body(g_ref, *refs):
        ins, outs = refs[:3 * k], refs[3 * k:]
        outs[-1][...] = g_ref[LOSS_AT[0]:LOSS_AT[0] + 1, LOSS_AT[1]:LOSS_AT[1] + LANE]
        for i, name in enumerate(names):
            if name == "lb_logits":
                rows, cols = slice(VEC_LB_ROW, VEC_LB_ROW + 2), slice(0, 512)
            else:
                _, row, size = VEC_ROWS[i]
                rows, cols = slice(row, row + 1), slice(0, size)
            g = g_ref[rows, cols]
            d, nm, nv = _adamw_math(g, ins[i][...], ins[k + i][...], ins[2 * k + i][...])
            for o_ref, val in zip(outs[4 * i:4 * i + 4], (g, d, nm, nv)):
                o_ref[...] = val

    shapes = [jax.ShapeDtypeStruct(w[n].shape, F32) for n in names for _ in range(4)] + [jax.ShapeDtypeStruct((1, LANE), F32)]
    res = pl.pallas_call(body, name="adamw_vec", out_shape=shapes)(
        block, *[w[n] for n in names], *[m[n] for n in names], *[v[n] for n in names])
    return [{n: res[4 * i + j] for i, n in enumerate(names)} for j in range(4)], res[-1]


WEIGHTS = ("g_pre", "w_in", "b_gate", "g_q", "w_uq", "g_kv", "w_ukv", "lb_logits", "g_hgrn", "w_branch_a", "w_branch_b", "w_out", "g_post")


def kernel(x, g_pre, w_in, b_gate, g_q, w_uq, g_kv, w_ukv, lb_logits, g_hgrn, w_branch_a, w_branch_b, w_out, g_post, loss_target, m_g_pre, m_w_in, m_b_gate, m_g_q, m_w_uq, m_g_kv, m_w_ukv, m_lb_logits, m_g_hgrn, m_w_branch_a, m_w_branch_b, m_w_out, m_g_post, v_g_pre, v_w_in, v_b_gate, v_g_q, v_w_uq, v_g_kv, v_w_ukv, v_lb_logits, v_g_hgrn, v_w_branch_a, v_w_branch_b, v_w_out, v_g_post):
    w = dict(g_pre=g_pre, w_in=w_in, b_gate=b_gate, g_q=g_q, w_uq=w_uq, g_kv=g_kv, w_ukv=w_ukv, lb_logits=lb_logits, g_hgrn=g_hgrn,
             w_branch_a=w_branch_a, w_branch_b=w_branch_b, w_out=w_out, g_post=g_post)
    m = dict(g_pre=m_g_pre, w_in=m_w_in, b_gate=m_b_gate, g_q=m_g_q, w_uq=m_w_uq, g_kv=m_g_kv, w_ukv=m_w_ukv, lb_logits=m_lb_logits,
             g_hgrn=m_g_hgrn, w_branch_a=m_w_branch_a, w_branch_b=m_w_branch_b, w_out=m_w_out, g_post=m_g_post)
    v = dict(g_pre=v_g_pre, w_in=v_w_in, b_gate=v_b_gate, g_q=v_g_q, w_uq=v_w_uq, g_kv=v_g_kv, w_ukv=v_w_ukv, lb_logits=v_lb_logits,
             g_hgrn=v_g_hgrn, w_branch_a=v_w_branch_a, w_branch_b=v_w_branch_b, w_out=v_w_out, g_post=v_g_post)
    blocks = {n: _to_block(n, w[n]).astype(BF16) for n in BIG}
    w_in_all = _gather_w_in(blocks["w_in"])
    weights = _LaterWeights(blocks, w_in_all)
    state = {n: [_to_block(n, t[n]) for t in (w, m, v)] for n in BIG}
    exchange = _GradExchange(state)
    loss, grad_x, _, vec_grads = _local_step(
        x[0], loss_target[0], g_pre, _join_chips("w_in", w_in_all), b_gate, g_q, g_kv, lb_logits, g_hgrn, g_post, weights, exchange)
    vec = _vec_pack(vec_grads, loss)
    vec_started, token = _exchange_start([vec], (), "vec_start")
    late_sent, late_landed = exchange.finish(token)
    (vec,), (vec_landed,) = _exchange_wait([vec], (), vec_started, exchange.outs["w_in"][0], "vec_wait")
    grads = _sum_share_small(late_landed + [vec_landed], late_sent + [vec], "sum_late")
    done = dict(exchange.outs)
    done.update(zip(exchange.LATE, _adamw_small(grads[:-1], [state[n] for n in exchange.LATE], "adamw_late")))
    outs = [{}, {}, {}, {}]
    for n in BIG:
        for o, val in zip(outs, done[n]):
            o[n] = _from_block(n, val)
    vec_outs, total = _adamw_vec(grads[-1], w, m, v)
    for o, vals in zip(outs, vec_outs):
        o.update(vals)
    return (total[0, 0], grad_x[None], *[o[n] for o in outs for n in WEIGHTS])
```

```python
import math

import numpy as np
import jax
import jax.numpy as jnp
from jax import lax
from jax.experimental import pallas as pl
from jax.experimental.pallas import tpu as pltpu

F32 = jnp.float32
BF16 = jnp.bfloat16
HIGHEST = lax.Precision.HIGHEST

D = 1024
NH = 8
QK_NOPE, QK_ROPE, V_DIM = 64, 32, 64
Q_LORA, KV_LORA = 768, 256
CHUNK = 64
HG_BLOCK = 32
EPS = 1e-6
LANE = 128
P_MERGE, P_GA, P_HQ, P_HF, P_HI, P_GB, P_CQ, P_CKV, P_KPE = 0, 2048, 2560, 3072, 3584, 4096, 4608, 5376, 5632
D_P = 5760
O_CQ, O_CKV, O_KPE, O_GA, O_HQ, O_HF, O_HI, O_GB, O_MERGE = 0, 768, 1024, 1056, 1568, 2080, 2592, 3104, 3616

TM = 512
TM_MID = 256
TQ = 1024
ONES_LANE = (LANE - 1, 0)
TH = 256
HG_PAIRS = 4
VMEM_LIMIT = 56 * 1024 * 1024

ADAM_LR, ADAM_B1, ADAM_B2, ADAM_EPS, ADAM_WD, ADAM_STEP = 0.001, 0.9, 0.999, 1e-08, 0.01, 10

NT_DIMS = (((1,), (1,)), ((), ()))
TN_DIMS = (((0,), (0,)), ((), ()))


def _params(sem):
    return pltpu.CompilerParams(dimension_semantics=sem, vmem_limit_bytes=VMEM_LIMIT)


def _mm(a, b):
    return jnp.dot(a, b, preferred_element_type=F32)


def _mm_nt(a, b):
    return lax.dot_general(a, b, NT_DIMS, preferred_element_type=F32)


def _mm_tn(a, b):
    return lax.dot_general(a, b, TN_DIMS, preferred_element_type=F32)


def _sigmoid(z):
    return jax.nn.sigmoid(z)


def _rope(v, c, s1, s2):
    return v * c + pltpu.roll(v, 112, 1) * s1 + pltpu.roll(v, 16, 1) * s2


def _rope_t(dy, c, s1, s2):
    return dy * c + pltpu.roll(dy * s1, 16, 1) + pltpu.roll(dy * s2, 112, 1)


def _rope_tables(s):
    f32 = np.float32
    inv = f32(10000.0) ** (-np.arange(0, QK_ROPE, 2, dtype=f32) / f32(QK_ROPE))
    ang = np.arange(s, dtype=f32)[:, None] * inv[None, :]
    cos, sin = np.cos(ang).astype(f32), np.sin(ang).astype(f32)
    z64, z32, o64, o32 = np.zeros((s, 64), f32), np.zeros((s, 32), f32), np.ones((s, 64), f32), np.ones((s, 32), f32)
    z16 = np.zeros((s, 16), f32)
    c = np.concatenate([o64, cos, cos, o32], axis=1)
    s1 = np.concatenate([z64, -sin, z16, z32], axis=1)
    s2 = np.concatenate([z64, z16, sin, z32], axis=1)
    return jnp.asarray(c), jnp.asarray(s1), jnp.asarray(s2)


W_IN_RUNS = ((O_MERGE, 2048, P_MERGE), (O_GA, O_MERGE - O_GA, P_GA), (O_CQ, O_KPE - O_CQ, P_CQ))


def _kpe_block(w_in_t):
    z = lambda n: jnp.zeros((n, w_in_t.shape[1]), w_in_t.dtype)
    return jnp.concatenate([z(64), w_in_t[O_KPE:O_KPE + QK_ROPE], z(32)], axis=0)


def _front_fwd(x, g_pre, w_in_t, w_kpe, tokens=()):
    s = x.shape[0]
    tokens = list(tokens)

    def body(x_ref, g_ref, w_ref, k_ref, *refs):
        o_ref, h_ref = refs[len(tokens):]
        xv = x_ref[...]
        r = lax.rsqrt(jnp.mean(xv * xv, axis=-1, keepdims=True) + EPS)
        h = ((xv * r) * g_ref[...]).astype(BF16)
        h_ref[...] = h
        for row, rows, col in W_IN_RUNS:
            o_ref[:, col:col + rows] = _mm_nt(h, w_ref[row:row + rows, :])
        o_ref[:, P_KPE:P_KPE + LANE] = _mm_nt(h, k_ref[...])

    full = lambda a: pl.BlockSpec(a.shape, lambda i: (0,) * a.ndim)
    return pl.pallas_call(
        body, name="front_fwd", grid=(s // TM,),
        in_specs=[pl.BlockSpec((TM, D), lambda i: (i, 0)), pl.BlockSpec((1, D), lambda i: (0, 0)), full(w_in_t), full(w_kpe)]
        + [pl.BlockSpec((8, LANE), lambda i: (0, 0))] * len(tokens),
        out_specs=[pl.BlockSpec((TM, D_P), lambda i: (i, 0)), pl.BlockSpec((TM, D), lambda i: (i, 0))],
        out_shape=[jax.ShapeDtypeStruct((s, D_P), F32), jax.ShapeDtypeStruct((s, D), BF16)],
        compiler_params=_params(("parallel",)),
    )(x, g_pre, w_in_t, w_kpe, *tokens)


def _norm_rows(v, g):
    r = lax.rsqrt(jnp.mean(v * v, axis=-1, keepdims=True) + EPS)
    return (v * r) * g, r


def _qkv_fwd(proj, g_q, g_kv, w_uq_p, w_k_p, w_v_p, rc, rs1, rs2):
    s = proj.shape[0]

    def body(cq_ref, ckv_ref, kpe_ref, gq_ref, gkv_ref, wq_ref, wk_ref, wv_ref, c_ref, s1_ref, s2_ref, q_ref, k_ref, v_ref):
        c, s1, s2 = c_ref[...], s1_ref[...], s2_ref[...]
        cqn, _ = _norm_rows(cq_ref[...], gq_ref[...])
        ckvn, _ = _norm_rows(ckv_ref[...], gkv_ref[...])
        ckvn = ckvn.astype(BF16)
        qf = _mm(cqn.astype(BF16), wq_ref[...])
        kf = jnp.concatenate([_mm(ckvn, wk_ref[t]) for t in range(N_CHIPS)], axis=1)
        vf = jnp.concatenate([_mm(ckvn, wv_ref[t]) for t in range(N_CHIPS)], axis=1)
        kpe = _rope(kpe_ref[...], c, s1, s2)
        lane = lax.broadcasted_iota(jnp.int32, (TM, LANE), 1)
        for h in range(NH):
            blk = slice(h * LANE, (h + 1) * LANE)
            q_ref[h] = _rope(qf[:, blk], c, s1, s2).astype(BF16)
            k_ref[h] = (kf[:, blk] + kpe).astype(BF16)
            v_ref[h] = jnp.where(lane == ONES_LANE[h % 2], 1.0, vf[:, blk]).astype(BF16)

    row = lambda w, j: pl.BlockSpec((TM, w), lambda i: (i, j))
    full = lambda a: pl.BlockSpec(a.shape, lambda i: (0,) * a.ndim)
    hs = jax.ShapeDtypeStruct((NH, s, LANE), BF16)
    return pl.pallas_call(
        body, name="qkv_fwd", grid=(s // TM,),
        in_specs=[row(Q_LORA, P_CQ // Q_LORA), row(KV_LORA, P_CKV // KV_LORA), row(LANE, P_KPE // LANE),
                  full(g_q), full(g_kv), full(w_uq_p), full(w_k_p), full(w_v_p), row(LANE, 0), row(LANE, 0), row(LANE, 0)],
        out_specs=[pl.BlockSpec((NH, TM, LANE), lambda i: (0, i, 0))] * 3,
        out_shape=[hs, hs, hs],
        compiler_params=_params(("parallel",)),
    )(proj, proj, proj, g_q, g_kv, w_uq_p, w_k_p, w_v_p, rc, rs1, rs2)


LOG2E = 1.4426950408889634
QK_SCALE2 = LOG2E / math.sqrt(QK_NOPE + QK_ROPE)


HQ = TQ // 2


def _diag_visible(n):
    row = lax.broadcasted_iota(jnp.int32, (n, n), 0)
    col = lax.broadcasted_iota(jnp.int32, (n, n), 1)
    return (col // CHUNK) <= (row // CHUNK)


def _attn_fwd(q, k, vv):
    s = q.shape[1]

    def body(q_ref, k_ref, v_ref, o_ref, lse_ref):
        i = pl.program_id(1)
        qs = (q_ref[0], q_ref[1])

        def tiles(t, carry, diag):
            rows = pl.ds(pl.multiple_of(t * TQ, TQ), TQ)
            sc = [_mm_nt(qs[hh], k_ref[hh, rows, :]) for hh in range(2)]
            if diag:
                sc = [jnp.where(_diag_visible(TQ), s_, -jnp.inf) for s_ in sc]
            m_new = [jnp.maximum(carry[hh][0], jnp.max(sc[hh], axis=-1, keepdims=True)) for hh in range(2)]
            alpha = [jnp.exp2((carry[hh][0] - m_new[hh]) * QK_SCALE2) for hh in range(2)]
            p = [jnp.exp2((sc[hh] - m_new[hh]) * QK_SCALE2).astype(BF16) for hh in range(2)]
            acc = [alpha[hh] * carry[hh][1] + _mm(p[hh], v_ref[hh, rows, :]) for hh in range(2)]
            return (m_new[0], acc[0]), (m_new[1], acc[1])

        init = (jnp.full((TQ, 1), -jnp.inf, F32), jnp.zeros((TQ, LANE), F32))
        carry = lax.fori_loop(0, i, lambda t, c: tiles(t, c, False), (init, init))
        carry = tiles(i, carry, True)
        lane = lax.broadcasted_iota(jnp.int32, (TQ, LANE), 1)
        out = jnp.zeros((TQ, LANE), F32)
        for hh in range(2):
            m, acc = carry[hh]
            l = jnp.sum(jnp.where(lane == ONES_LANE[hh], acc, 0.0), axis=-1, keepdims=True)
            out = out + jnp.where((lane < V_DIM) == (hh == 0), acc, 0.0) / l
            lse_ref[hh] = jnp.broadcast_to(m * QK_SCALE2 + jnp.log(l) * LOG2E, (TQ, LANE))
        o_ref[...] = out

    return pl.pallas_call(
        body, name="attn_fwd", grid=(NH // 2, s // TQ),
        in_specs=[pl.BlockSpec((2, TQ, LANE), lambda p, i: (p, i, 0)), pl.BlockSpec((2, s, LANE), lambda p, i: (p, 0, 0)),
                  pl.BlockSpec((2, s, LANE), lambda p, i: (p, 0, 0))],
        out_specs=[pl.BlockSpec((TQ, LANE), lambda p, i: (i, p)), pl.BlockSpec((2, TQ, LANE), lambda p, i: (p, i, 0))],
        out_shape=[jax.ShapeDtypeStruct((s, NH * V_DIM), F32), jax.ShapeDtypeStruct((NH, s, LANE), F32)],
        compiler_params=_params(("parallel", "parallel")),
    )(q, k, vv)


def _lower_bound(lbl):
    a0, a1 = lbl[0:1, :], lbl[1:2, :]
    mx = jnp.maximum(a0, a1)
    e0, e1 = jnp.exp(a0 - mx), jnp.exp(a1 - mx)
    return e0 / (e0 + e1)


def _chunk_cumsum(v, reverse=False):
    pos = lax.broadcasted_iota(jnp.int32, v.shape, 0) % HG_BLOCK
    s = 1
    while s < HG_BLOCK:
        if reverse:
            v = v + jnp.where(pos < HG_BLOCK - s, pltpu.roll(v, TH - s, 0), 0.0)
        else:
            v = v + jnp.where(pos >= s, pltpu.roll(v, s, 0), 0.0)
        s *= 2
    return v


def _hgrn_gates(hq, hf, lb):
    sig = _sigmoid(hf)
    f = lb + (1.0 - lb) * sig
    g = jnp.log(f)
    kk = 1.0 - f
    r = lax.broadcasted_iota(jnp.int32, (TH, TH), 0)
    c = lax.broadcasted_iota(jnp.int32, (TH, TH), 1)
    tri = ((r // HG_BLOCK) == (c // HG_BLOCK)) & (r >= c)
    cum = _chunk_cumsum(g)
    nch = TH // HG_BLOCK
    total = _chunks(cum)[:, HG_BLOCK - 1:HG_BLOCK, :]
    lastb = jnp.broadcast_to(total, (nch, HG_BLOCK, hf.shape[-1])).reshape(hf.shape)
    e, ei, ee = jnp.exp(cum), jnp.exp(-cum), jnp.exp(lastb - cum)
    return dict(sig=sig, f=f, kk=kk, tri=tri, cum=cum, total=total, decay=jnp.exp(total), e=e, ei=ei, ee=ee,
                qd=hq * e, ki=kk * ei, ke=kk * ee)


def _chunks(v):
    return v.reshape(TH // HG_BLOCK, HG_BLOCK, v.shape[-1])


def _bmm_nt(a, b):
    return lax.dot_general(a, b, (((2,), (2,)), ((0,), (0,))), preferred_element_type=F32)


def _bmm_nn(a, b):
    return lax.dot_general(a, b, (((2,), (1,)), ((0,), (0,))), preferred_element_type=F32)


def _bmm_tn(a, b):
    return lax.dot_general(a, b, (((1,), (1,)), ((0,), (0,))), preferred_element_type=F32)


def _pair_masks():
    lane = lax.broadcasted_iota(jnp.int32, (TH, LANE), 1)
    kr = lax.broadcasted_iota(jnp.int32, (LANE, LANE), 0)
    kc = lax.broadcasted_iota(jnp.int32, (LANE, LANE), 1)
    return lane < 64, (kr // 64) == (kc // 64)


def _hgrn_fwd(proj, lbl):
    s = proj.shape[0]
    nch = TH // HG_BLOCK

    def body(hq_ref, hf_ref, hi_ref, lbl_ref, o_ref, st_ref, st):
        @pl.when(pl.program_id(1) == 0)
        def _():
            st[...] = jnp.zeros_like(st)

        m0, bd = _pair_masks()
        gt = _hgrn_gates(hq_ref[...], hf_ref[...], _lower_bound(lbl_ref[...]))
        v_b, qd, qd_b = hi_ref[...].astype(BF16), gt["qd"], gt["qd"].astype(BF16)
        ki_b, ke_b = gt["ki"].astype(BF16), gt["ke"].astype(BF16)
        pairs = [slice(u * LANE, (u + 1) * LANE) for u in range(HG_PAIRS)]
        heads = [(lanes, m0 if hh == 0 else jnp.logical_not(m0)) for lanes in pairs for hh in range(2)]
        a_b = [jnp.where(gt["tri"], _mm_nt(jnp.where(mh, qd[:, lanes], 0.0).astype(BF16), ki_b[:, lanes]), 0.0).astype(BF16)
               for lanes, mh in heads]
        intra = [jnp.where(m0, _mm(a_b[2 * u], v_b[:, lanes]), _mm(a_b[2 * u + 1], v_b[:, lanes])) for u, lanes in enumerate(pairs)]
        upd = [_bmm_tn(_chunks(v_b[:, lanes]), _chunks(ke_b[:, lanes])) for lanes in pairs]
        entering = []
        for u, lanes in enumerate(pairs):
            cur, states = st[u], []
            for n in range(nch):
                states.append(cur)
                cur = gt["decay"][n][:, lanes] * cur + jnp.where(bd, upd[u][n], 0.0)
            st[u] = cur
            entering.append(jnp.stack(states))
            st_ref[u] = entering[u]
        for u, lanes in enumerate(pairs):
            o_ref[:, lanes] = intra[u] + _bmm_nt(_chunks(qd_b[:, lanes]), entering[u].astype(BF16)).reshape(TH, LANE)

    wide = HG_PAIRS * LANE
    col = lambda base: pl.BlockSpec((TH, wide), lambda p, i: (i, base // wide + p))
    return pl.pallas_call(
        body, name="hgrn_fwd", grid=(NH // 2 // HG_PAIRS, s // TH),
        in_specs=[col(P_HQ), col(P_HF), col(P_HI), pl.BlockSpec((2, wide), lambda p, i: (0, p))],
        out_specs=[pl.BlockSpec((TH, wide), lambda p, i: (i, p)),
                   pl.BlockSpec((HG_PAIRS, nch, LANE, LANE), lambda p, i: (p, i, 0, 0))],
        out_shape=[jax.ShapeDtypeStruct((s, 512), F32), jax.ShapeDtypeStruct((NH // 2, s // HG_BLOCK, LANE, LANE), F32)],
        scratch_shapes=[pltpu.VMEM((HG_PAIRS, LANE, LANE), F32)],
        compiler_params=_params(("parallel", "arbitrary")),
    )(proj, proj, proj, lbl)


def _group_sum(v):
    low = lax.broadcasted_iota(jnp.int32, (v.shape[0], LANE), 1) < V_DIM
    blocks = []
    for b in range(v.shape[1] // LANE):
        blk = v[:, b * LANE:(b + 1) * LANE]
        s_low = jnp.sum(jnp.where(low, blk, 0.0), axis=-1, keepdims=True)
        s_high = jnp.sum(jnp.where(low, 0.0, blk), axis=-1, keepdims=True)
        blocks.append(jnp.where(low, s_low, s_high))
    return jnp.concatenate(blocks, axis=1)


def _dsilu(z, sg):
    return sg * (1.0 + z * (1.0 - sg))


def _mid(proj, attn, o_raw, x, tgt, g_hg, b_gate, g_post, wa, wb, w_out):
    s = x.shape[0]

    def body(attn_ref, ga_ref, o_ref, gb_ref, mg_ref, x_ref, t_ref, ghg_ref, bg_ref, gp_ref, wa_ref, wb_ref, wo_ref,
             loss_ref, dout_ref, dattn_ref, dga_ref, dor_ref, dgb_ref, dmg_ref, dwo_out, dwa_out, dwb_out, dgp_ref, dbg_ref, dghg_ref,
             dwo_ref, dwa_ref, dwb_ref):
        @pl.when(pl.program_id(0) == 0)
        def _():
            for rf in (loss_ref, dwo_ref, dwa_ref, dwb_ref, dgp_ref, dbg_ref, dghg_ref):
                rf[...] = jnp.zeros_like(rf)

        attn, za, orw, zb = attn_ref[...], ga_ref[...], o_ref[...], gb_ref[...]
        ghg, gp = ghg_ref[...], gp_ref[...]
        sga, sgb = _sigmoid(za), _sigmoid(zb)
        sa, sb = za * sga, zb * sgb
        ga = attn * sa
        rh = lax.rsqrt(_group_sum(orw * orw) * (1.0 / V_DIM) + EPS)
        on = (orw * rh) * ghg
        gb = on * sb
        ga_b, gb_b = ga.astype(BF16), gb.astype(BF16)
        blocks = [slice(t * (D // N_CHIPS), (t + 1) * (D // N_CHIPS)) for t in range(N_CHIPS)]
        ya = jnp.concatenate([_mm(ga_b, wa_ref[t]) for t in range(N_CHIPS)], axis=1)
        yb = jnp.concatenate([_mm(gb_b, wb_ref[t]) for t in range(N_CHIPS)], axis=1)
        gates = _sigmoid(mg_ref[...] + bg_ref[...])
        g0, g1 = gates[:, :D], gates[:, D:]
        m_b = (g0 * ya + g1 * yb).astype(BF16)
        y = _mm(m_b, wo_ref[...])
        ry = lax.rsqrt(jnp.mean(y * y, axis=-1, keepdims=True) + EPS)
        out = x_ref[...] + (y * ry) * gp
        err = out - t_ref[...]
        loss_ref[...] += 0.5 * jnp.sum(jnp.mean(err * err, axis=-1, keepdims=True), axis=0, keepdims=True)
        dout = err * (1.0 / D)
        dout_ref[...] = dout
        dgp_ref[...] += jnp.sum(dout * (y * ry), axis=0, keepdims=True)
        dgy = dout * gp
        dy = ry * dgy - y * (ry * ry * ry) * jnp.mean(y * dgy, axis=-1, keepdims=True)
        dy_b = dy.astype(BF16)
        dm = _mm_nt(dy_b, wo_ref[...])
        dya_b, dyb_b = (dm * g0).astype(BF16), (dm * g1).astype(BF16)
        dga = sum(_mm_nt(dya_b[:, cols], wa_ref[t]) for t, cols in enumerate(blocks))
        dgb = sum(_mm_nt(dyb_b[:, cols], wb_ref[t]) for t, cols in enumerate(blocks))
        dwo_ref[...] += _mm_tn(m_b, dy_b)
        for t, cols in enumerate(blocks):
            dwa_ref[t] += _mm_tn(ga_b, dya_b[:, cols])
            dwb_ref[t] += _mm_tn(gb_b, dyb_b[:, cols])
        dg0, dg1 = dm * ya, dm * yb
        dmg = jnp.concatenate([dg0 * g0 * (1.0 - g0), dg1 * g1 * (1.0 - g1)], axis=1)
        dmg_ref[...] = dmg.astype(BF16)
        dbg_ref[...] += jnp.sum(dmg, axis=0, keepdims=True)
        dattn_ref[...] = dga * sa
        dga_ref[...] = (dga * attn * _dsilu(za, sga)).astype(BF16)
        dgb_ref[...] = (dgb * on * _dsilu(zb, sgb)).astype(BF16)
        don = dgb * sb
        dghg_ref[...] += jnp.sum(don * (orw * rh), axis=0, keepdims=True)
        dgo = don * ghg
        dor_ref[...] = rh * dgo - orw * (rh * rh * rh) * (_group_sum(orw * dgo) * (1.0 / V_DIM))

        @pl.when(pl.program_id(0) == pl.num_programs(0) - 1)
        def _():
            for out, rf in ((dwo_out, dwo_ref), (dwa_out, dwa_ref), (dwb_out, dwb_ref)):
                out[...] = rf[...].astype(BF16)

    row = lambda w, j=0: pl.BlockSpec((TM_MID, w), lambda i: (i, j))
    full = lambda a: pl.BlockSpec(a.shape, lambda i: (0,) * a.ndim)
    acc = lambda shape: pl.BlockSpec(shape, lambda i: (0,) * len(shape))
    slabs = (N_CHIPS, 512, D // N_CHIPS)
    sds = jax.ShapeDtypeStruct
    return pl.pallas_call(
        body, name="mid", grid=(s // TM_MID,),
        in_specs=[row(512), row(512, P_GA // 512), row(512), row(512, P_GB // 512), row(2048, P_MERGE // 2048), row(D), row(D),
                  full(g_hg), full(b_gate), full(g_post), full(wa), full(wb), full(w_out)],
        out_specs=[acc((1, 1)), row(D), row(512), row(512), row(512), row(512), row(2048),
                   acc((D, D)), acc(slabs), acc(slabs), acc((1, D)), acc((1, 2048)), acc((1, 512))],
        out_shape=[sds((1, 1), F32), sds((s, D), F32), sds((s, 512), F32), sds((s, 512), BF16), sds((s, 512), F32), sds((s, 512), BF16),
                   sds((s, 2048), BF16), sds((D, D), BF16), sds(slabs, BF16), sds(slabs, BF16), sds((1, D), F32),
                   sds((1, 2048), F32), sds((1, 512), F32)],
        scratch_shapes=[pltpu.VMEM((D, D), F32), pltpu.VMEM(slabs, F32), pltpu.VMEM(slabs, F32)],
        compiler_params=_params(("arbitrary",)),
    )(attn, proj, o_raw, proj, proj, x, tgt, g_hg, b_gate, g_post, wa, wb, w_out)


def _attn_bwd(q, k, vv, attn, dattn, lse, token):
    s = q.shape[1]
    nt = s // TQ
    scale = 1.0 / math.sqrt(QK_NOPE + QK_ROPE)

    def body(q_ref, k_ref, v_ref, o_ref, do_ref, lse_ref, token_ref, dq_ref, dk_ref, dv_ref, do_s, delta_s):
        j = pl.program_id(1)

        @pl.when(j == 0)
        def _():
            dq_ref[...] = jnp.zeros_like(dq_ref)
            lane = lax.broadcasted_iota(jnp.int32, (TQ, LANE), 1)

            @pl.loop(0, nt)
            def _(i):
                rows = pl.ds(pl.multiple_of(i * TQ, TQ), TQ)
                do, o = do_ref[rows, :], o_ref[rows, :]
                for hh in range(2):
                    doh = jnp.where((lane < 64) if hh == 0 else (lane >= 64), do, 0.0)
                    do_s[hh, rows, :] = doh.astype(BF16)
                    delta_s[hh, rows, :] = jnp.broadcast_to(jnp.sum(doh * o, axis=-1, keepdims=True), (TQ, LANE))

        kjs, vjs = (k_ref[0], k_ref[1]), (v_ref[0], v_ref[1])

        def tile(hh, start, size, kj, vj, diag):
            rows = pl.ds(pl.multiple_of(start, size), size)
            wide = lambda a: jnp.concatenate([a] * (kj.shape[0] // LANE), axis=1)
            qi, do_b = q_ref[hh, rows, :], do_s[hh, rows, :]
            sc, dp = _mm_nt(qi, kj), _mm_nt(do_b, vj)
            p = jnp.exp2(sc * QK_SCALE2 - wide(lse_ref[hh, rows, :]))
            if diag:
                p = jnp.where(_diag_visible(size), p, 0.0)
            ds_b = (p * (dp - wide(delta_s[hh, rows, :]))).astype(BF16)
            dv, dk = _mm_tn(do_b, p.astype(BF16)), _mm_tn(qi, ds_b)
            dq_ref[hh, rows, :] += _mm(ds_b, kj)
            return dk, dv

        def step(i, carry):
            new = [tile(hh, i * TQ, TQ, kjs[hh], vjs[hh], False) for hh in range(2)]
            return tuple((carry[hh][0] + new[hh][0], carry[hh][1] + new[hh][1]) for hh in range(2))

        def diagonal(hh):
            k0, k1, v0, v1 = kjs[hh][:HQ], kjs[hh][HQ:], vjs[hh][:HQ], vjs[hh][HQ:]
            a = tile(hh, j * TQ, HQ, k0, v0, True)
            b = tile(hh, j * TQ + HQ, HQ, k0, v0, False)
            c = tile(hh, j * TQ + HQ, HQ, k1, v1, True)
            return jnp.concatenate([a[0] + b[0], c[0]], axis=1), jnp.concatenate([a[1] + b[1], c[1]], axis=1)

        carry = lax.fori_loop(j + 1, nt, step, (diagonal(0), diagonal(1)))
        for hh in range(2):
            dk_ref[hh] = carry[hh][0].T * scale
            dv_ref[hh] = carry[hh][1].T

        @pl.when(j == nt - 1)
        def _():
            dq_ref[...] = dq_ref[...] * scale

    whole = pl.BlockSpec((2, s, LANE), lambda p, j: (p, 0, 0))
    tile_spec = pl.BlockSpec((2, TQ, LANE), lambda p, j: (p, j, 0))
    cols = pl.BlockSpec((s, LANE), lambda p, j: (0, p))
    hs = jax.ShapeDtypeStruct((NH, s, LANE), F32)
    return pl.pallas_call(
        body, name="attn_bwd", grid=(NH // 2, nt),
        in_specs=[whole, tile_spec, tile_spec, cols, cols, whole, pl.BlockSpec((8, LANE), lambda p, j: (0, 0))],
        out_specs=[whole, tile_spec, tile_spec],
        out_shape=[hs, hs, hs],
        scratch_shapes=[pltpu.VMEM((2, s, LANE), BF16), pltpu.VMEM((2, s, LANE), F32)],
        compiler_params=_params(("parallel", "arbitrary")),
    )(q, k, vv, attn, dattn, lse, token)


def _hgrn_bwd(proj, lbl, states, do_raw):
    s = proj.shape[0]
    nt = s // TH
    nch = TH // HG_BLOCK

    def body(hq_ref, hf_ref, hi_ref, lbl_ref, st_ref, do_ref, dh_ref, dlbl_ref, dst, dlb):
        step = pl.program_id(1)

        @pl.when(step == 0)
        def _():
            dst[...] = jnp.zeros_like(dst)
            dlb[...] = jnp.zeros_like(dlb)

        m0, bd = _pair_masks()
        lb = _lower_bound(lbl_ref[...])
        gt = _hgrn_gates(hq_ref[...], hf_ref[...], lb)
        do = do_ref[...]
        qd, ki, ke = gt["qd"], gt["ki"], gt["ke"]
        v_b, do_b = hi_ref[...].astype(BF16), do.astype(BF16)
        qd_b, ki_b, ke_b = qd.astype(BF16), ki.astype(BF16), ke.astype(BF16)
        pairs = [slice(u * LANE, (u + 1) * LANE) for u in range(HG_PAIRS)]
        heads = [(lanes, m0 if hh == 0 else jnp.logical_not(m0)) for lanes in pairs for hh in range(2)]
        a_b = [jnp.where(gt["tri"], _mm_nt(jnp.where(mh, qd[:, lanes], 0.0).astype(BF16), ki_b[:, lanes]), 0.0).astype(BF16)
               for lanes, mh in heads]
        doh_b = [jnp.where(mh, do[:, lanes], 0.0).astype(BF16) for lanes, mh in heads]
        da_b = [jnp.where(gt["tri"], _mm_nt(d, v_b[:, lanes]), 0.0).astype(BF16) for d, (lanes, _) in zip(doh_b, heads)]
        dv_p, dqd_p, dki_p = [], [], []
        for u, lanes in enumerate(pairs):
            e, o = 2 * u, 2 * u + 1
            dv_p.append(_mm_tn(a_b[e], doh_b[e]) + _mm_tn(a_b[o], doh_b[o]))
            dqd_p.append(jnp.where(m0, _mm(da_b[e], ki_b[:, lanes]), _mm(da_b[o], ki_b[:, lanes])))
            dki_p.append(jnp.where(m0, _mm_tn(da_b[e], qd_b[:, lanes]), _mm_tn(da_b[o], qd_b[:, lanes])))
        fed = [_bmm_tn(_chunks(do_b[:, lanes]), _chunks(qd_b[:, lanes])) for lanes in pairs]
        leaving = []
        for u, lanes in enumerate(pairs):
            ds, left = dst[u], [None] * nch
            for n in reversed(range(nch)):
                left[n] = ds
                ds = gt["decay"][n][:, lanes] * ds + jnp.where(bd, fed[u][n], 0.0)
            dst[u] = ds
            leaving.append(jnp.stack(left))
        dke_p, dlast_p = [], []
        for u, lanes in enumerate(pairs):
            entering, leaving_b = st_ref[u], leaving[u].astype(BF16)
            dke3 = _bmm_nn(_chunks(v_b[:, lanes]), leaving_b)
            dv_p[u] = dv_p[u] + _bmm_nt(_chunks(ke_b[:, lanes]), leaving_b).reshape(TH, LANE)
            dqd_p[u] = dqd_p[u] + _bmm_nn(_chunks(do_b[:, lanes]), entering.astype(BF16)).reshape(TH, LANE)
            dke_p.append(dke3.reshape(TH, LANE))
            dlast_p.append(jnp.sum(dke3 * _chunks(ke[:, lanes]), axis=1, keepdims=True)
                           + jnp.sum(leaving[u] * entering, axis=1, keepdims=True) * gt["decay"][:, :, lanes])
        cat = lambda parts: jnp.concatenate(parts, axis=-1)
        dv, dqd, dki, dke, dlast = cat(dv_p), cat(dqd_p), cat(dki_p), cat(dke_p), cat(dlast_p)
        dk = dki * gt["ei"] + dke * gt["ee"]
        dcum = dqd * qd - dki * ki - dke * ke
        dg = _chunk_cumsum(dcum, reverse=True) + jnp.broadcast_to(dlast, (nch, HG_BLOCK, dlast.shape[-1])).reshape(dcum.shape)
        sig = gt["sig"]
        df = dg / gt["f"] - dk
        dlb[...] += jnp.sum(df * (1.0 - sig), axis=0, keepdims=True)
        dh_ref[0] = (dqd * gt["e"]).astype(BF16)
        dh_ref[1] = ((df * (1.0 - lb)) * sig * (1.0 - sig)).astype(BF16)
        dh_ref[2] = dv.astype(BF16)

        @pl.when(step == nt - 1)
        def _():
            lb = _lower_bound(lbl_ref[...])
            da0 = dlb[...] * lb * (1.0 - lb)
            dlbl_ref[...] = jnp.concatenate([da0, -da0], axis=0)

    wide = HG_PAIRS * LANE
    col = lambda base: pl.BlockSpec((TH, wide), lambda p, i: (nt - 1 - i, base // wide + p))
    tile = pl.BlockSpec((TH, wide), lambda p, i: (nt - 1 - i, p))
    sds = jax.ShapeDtypeStruct
    return pl.pallas_call(
        body, name="hgrn_bwd", grid=(NH // 2 // HG_PAIRS, nt),
        in_specs=[col(P_HQ), col(P_HF), col(P_HI), pl.BlockSpec((2, wide), lambda p, i: (0, p)),
                  pl.BlockSpec((HG_PAIRS, nch, LANE, LANE), lambda p, i: (p, nt - 1 - i, 0, 0)), tile],
        out_specs=[pl.BlockSpec((3, TH, wide), lambda p, i: (0, nt - 1 - i, p)), pl.BlockSpec((2, wide), lambda p, i: (0, p))],
        out_shape=[sds((3, s, 512), BF16), sds((2, 512), F32)],
        scratch_shapes=[pltpu.VMEM((HG_PAIRS, LANE, LANE), F32), pltpu.VMEM((1, wide), F32)],
        compiler_params=_params(("parallel", "arbitrary")),
    )(proj, proj, proj, lbl, states, do_raw)


def _norm_rows_bwd(v, r, g, dn):
    dgv = dn * g
    return r * dgv - v * (r * r * r) * jnp.mean(v * dgv, axis=-1, keepdims=True)


def _qkv_bwd(proj, dq, dk, dvv, g_q, g_kv, w_uq_p, w_k_p, w_v_p, rc, rs1, rs2):
    s = proj.shape[0]
    head_q = QK_NOPE + QK_ROPE

    def body(cq_ref, ckv_ref, dq_ref, dk_ref, dv_ref, gq_ref, gkv_ref, wq_ref, wk_ref, wv_ref, c_ref, s1_ref, s2_ref,
             dcq_ref, dckv_ref, dkpe_ref, dwq_out, dwkv_out, dgq_ref, dgkv_ref, dwq_ref, dwk_ref, dwv_ref):
        @pl.when(pl.program_id(0) == 0)
        def _():
            for rf in (dwq_ref, dwk_ref, dwv_ref, dgq_ref, dgkv_ref):
                rf[...] = jnp.zeros_like(rf)

        c, s1, s2 = c_ref[...], s1_ref[...], s2_ref[...]
        cq, ckv = cq_ref[...], ckv_ref[...]
        gq, gkv = gq_ref[...], gkv_ref[...]
        cqn, rq = _norm_rows(cq, gq)
        ckvn, rkv = _norm_rows(ckv, gkv)
        cqn_b, ckvn_b = cqn.astype(BF16), ckvn.astype(BF16)
        dqf = jnp.concatenate([_rope_t(dq_ref[h], c, s1, s2) for h in range(NH)], axis=1).astype(BF16)
        dkf = jnp.concatenate([dk_ref[h] for h in range(NH)], axis=1).astype(BF16)
        dvf = jnp.concatenate([dv_ref[h] for h in range(NH)], axis=1).astype(BF16)
        dkpe = dk_ref[0]
        for h in range(1, NH):
            dkpe = dkpe + dk_ref[h]
        lane = lax.broadcasted_iota(jnp.int32, (TM, LANE), 1)
        dkpe = jnp.where((lane >= QK_NOPE) & (lane < QK_NOPE + QK_ROPE), dkpe, 0.0)
        dkpe_ref[...] = _rope_t(dkpe, c, s1, s2).astype(BF16)
        dcqn = _mm_nt(dqf, wq_ref[...])
        pair = lambda a, t: a[:, t * 2 * LANE:(t + 1) * 2 * LANE]
        dckvn = sum(_mm_nt(pair(dkf, t), wk_ref[t]) + _mm_nt(pair(dvf, t), wv_ref[t]) for t in range(N_CHIPS))
        dwq_ref[...] += _mm_tn(cqn_b, dqf)
        dwk_ref[...] += _mm_tn(ckvn_b, dkf)
        dwv_ref[...] += _mm_tn(ckvn_b, dvf)
        dgq_ref[...] += jnp.sum(dcqn * (cq * rq), axis=0, keepdims=True)
        dgkv_ref[...] += jnp.sum(dckvn * (ckv * rkv), axis=0, keepdims=True)
        dcq_ref[...] = _norm_rows_bwd(cq, rq, gq, dcqn).astype(BF16)
        dckv_ref[...] = _norm_rows_bwd(ckv, rkv, gkv, dckvn).astype(BF16)

        @pl.when(pl.program_id(0) == pl.num_programs(0) - 1)
        def _():
            blk = lambda ref, h: ref[:, h * LANE:(h + 1) * LANE]
            lane = lax.broadcasted_iota(jnp.int32, (Q_LORA, LANE), 1)
            for j in range(NH * head_q // LANE):
                h0, w0 = divmod(j * LANE, head_q)
                first = blk(dwq_ref, h0) if w0 == 0 else pltpu.roll(blk(dwq_ref, h0), LANE - w0, 1)
                second = pltpu.roll(blk(dwq_ref, h0 + 1), head_q - w0, 1)
                dwq_out[:, j * LANE:(j + 1) * LANE] = jnp.where(lane < head_q - w0, first, second).astype(BF16)
            lane = lax.broadcasted_iota(jnp.int32, (KV_LORA, LANE), 1)
            for h in range(NH):
                vals = blk(dwv_ref, h) if h % 2 else pltpu.roll(blk(dwv_ref, h), V_DIM, 1)
                both = jnp.where(lane < QK_NOPE, blk(dwk_ref, h), vals).astype(BF16)
                dwkv_out[h // 2, :, (h % 2) * LANE:(h % 2 + 1) * LANE] = both

    row = lambda w, j=0: pl.BlockSpec((TM, w), lambda i: (i, j))
    full = lambda a: pl.BlockSpec(a.shape, lambda i: (0,) * a.ndim)
    acc = lambda *shape: pl.BlockSpec(shape, lambda i: (0,) * len(shape))
    heads = pl.BlockSpec((NH, TM, LANE), lambda i: (0, i, 0))
    sds = jax.ShapeDtypeStruct
    return pl.pallas_call(
        body, name="qkv_bwd", grid=(s // TM,),
        in_specs=[row(Q_LORA, P_CQ // Q_LORA), row(KV_LORA, P_CKV // KV_LORA), heads, heads, heads,
                  full(g_q), full(g_kv), full(w_uq_p), full(w_k_p), full(w_v_p), row(LANE), row(LANE), row(LANE)],
        out_specs=[row(Q_LORA), row(KV_LORA), row(LANE), acc(Q_LORA, NH * head_q), acc(NH // 2, KV_LORA, 2 * LANE),
                   acc(1, Q_LORA), acc(1, KV_LORA)],
        out_shape=[sds((s, Q_LORA), BF16), sds((s, KV_LORA), BF16), sds((s, LANE), BF16), sds((Q_LORA, NH * head_q), BF16),
                   sds((NH // 2, KV_LORA, 2 * LANE), BF16), sds((1, Q_LORA), F32), sds((1, KV_LORA), F32)],
        scratch_shapes=[pltpu.VMEM((Q_LORA, D), F32), pltpu.VMEM((KV_LORA, D), F32), pltpu.VMEM((KV_LORA, D), F32)],
        compiler_params=_params(("arbitrary",)),
    )(proj, proj, dq, dk, dvv, g_q, g_kv, w_uq_p, w_k_p, w_v_p, rc, rs1, rs2)


def _front_bwd(x, dout, dmg, dga, dh3, dgb, dcq, dckv, dkpe, g_pre, w_in_t, w_kpe, token):
    s = x.shape[0]

    def body(x_ref, do_ref, dmg_ref, dga_ref, dh3_ref, dgb_ref, dcq_ref, dckv_ref, dkpe_ref, g_ref, w_ref, k_ref, token_ref,
             gx_ref, dg_ref):
        @pl.when(pl.program_id(0) == 0)
        def _():
            dg_ref[...] = jnp.zeros_like(dg_ref)

        xv, g = x_ref[...], g_ref[...]
        _, r = _norm_rows(xv, g)
        pieces = ((dmg_ref[...], O_MERGE), (dga_ref[...], O_GA), (dh3_ref[0], O_HQ), (dh3_ref[1], O_HF), (dh3_ref[2], O_HI),
                  (dgb_ref[...], O_GB), (dcq_ref[...], O_CQ), (dckv_ref[...], O_CKV))
        dh = _mm(dkpe_ref[...], k_ref[...])
        for piece, off in pieces:
            dh = dh + _mm(piece, w_ref[off:off + piece.shape[1], :])
        dg_ref[...] += jnp.sum(dh * (xv * r), axis=0, keepdims=True)
        gx_ref[...] = do_ref[...] + _norm_rows_bwd(xv, r, g, dh)

    row = lambda w: pl.BlockSpec((TM, w), lambda i: (i, 0))
    full = lambda a: pl.BlockSpec(a.shape, lambda i: (0,) * a.ndim)
    sds = jax.ShapeDtypeStruct
    return pl.pallas_call(
        body, name="front_bwd", grid=(s // TM,),
        in_specs=[row(D), row(D), row(2048), row(512), pl.BlockSpec((3, TM, 512), lambda i: (0, i, 0)), row(512), row(Q_LORA),
                  row(KV_LORA), row(LANE), full(g_pre), full(w_in_t), full(w_kpe), pl.BlockSpec(memory_space=pl.ANY)],
        out_specs=[row(D), pl.BlockSpec((1, D), lambda i: (0, 0))],
        out_shape=[sds((s, D), F32), sds((1, D), F32)],
        compiler_params=_params(("arbitrary",)),
    )(x, dout, dmg, dga, dh3, dgb, dcq, dckv, dkpe, g_pre, w_in_t, w_kpe, token)


TK_GRAD = 1024


QKV_ROWS = Q_LORA + KV_LORA + QK_ROPE
D_IN = O_MERGE + 2048


def _win_grad(h, pieces, offsets, name, whole=None):
    s = h.shape[0]
    n = len(pieces)
    new = whole is None
    shapes = [(p.shape[0] * p.shape[2], D) if p.ndim == 3 else (p.shape[1], D) for p in pieces]

    def body(h_ref, *refs):
        d_refs, whole_ref, scratch = refs[:n], refs[n if new else n + 1], refs[n + (1 if new else 2):]
        sums, rounded, sems = scratch[:n], scratch[n:2 * n], scratch[-1]

        @pl.when(pl.program_id(0) == 0)
        def _():
            for s_ref in sums:
                s_ref[...] = jnp.zeros_like(s_ref)

        hv = h_ref[...]
        for d_ref, s_ref in zip(d_refs, sums):
            if len(d_ref.shape) == 3:
                w = d_ref.shape[2]
                for k in range(d_ref.shape[0]):
                    s_ref[k * w:(k + 1) * w] += _mm_tn(d_ref[k], hv)
            else:
                s_ref[...] += _mm_tn(d_ref[...], hv)

        @pl.when(pl.program_id(0) == pl.num_programs(0) - 1)
        def _():
            copies = []
            for j, (s_ref, r_ref, row) in enumerate(zip(sums, rounded, offsets)):
                r_ref[...] = s_ref[...].astype(BF16)
                copies.append(pltpu.make_async_copy(r_ref, whole_ref.at[pl.ds(row, r_ref.shape[0])], sems.at[j]))
            if new:
                zeros = scratch[2 * n]
                zeros[...] = jnp.zeros_like(zeros)
                copies.append(pltpu.make_async_copy(zeros, whole_ref.at[pl.ds(0, QKV_ROWS)], sems.at[n]))
            for copy in copies:
                copy.start()
            for copy in copies:
                copy.wait()

    def in_spec(p):
        if p.ndim == 3:
            return pl.BlockSpec((p.shape[0], TK_GRAD, p.shape[2]), lambda kk: (0, kk, 0))
        return pl.BlockSpec((TK_GRAD, p.shape[1]), lambda kk: (kk, 0))

    anywhere = pl.BlockSpec(memory_space=pl.ANY)
    return pl.pallas_call(
        body, name=name, grid=(s // TK_GRAD,),
        in_specs=[pl.BlockSpec((TK_GRAD, D), lambda kk: (kk, 0))] + [in_spec(p) for p in pieces] + ([] if new else [anywhere]),
        out_specs=anywhere, out_shape=jax.ShapeDtypeStruct((D_IN, D), BF16),
        input_output_aliases={} if new else {n + 1: 0},
        scratch_shapes=[pltpu.VMEM(sh, F32) for sh in shapes] + [pltpu.VMEM(sh, BF16) for sh in shapes]
        + ([pltpu.VMEM((QKV_ROWS, D), BF16)] if new else []) + [pltpu.SemaphoreType.DMA((n + 1,))],
        compiler_params=_params(("arbitrary",)),
    )(h, *pieces, *([] if new else [whole]))


def _win_grad_qkv(h, dcq, dckv, dkpe, share):
    s = h.shape[0]
    half = QKV_ROWS // 2

    def body(h_ref, cq_ref, ckv_ref, kpe_ref, o_ref, *scratch):
        sums = scratch[0] if share else o_ref

        @pl.when(pl.program_id(0) == 0)
        def _():
            sums[...] = jnp.zeros_like(sums)

        hv = h_ref[...]
        g_cq = _mm_tn(cq_ref[...], hv)
        sums[0] += g_cq[:half]
        sums[1, 0:Q_LORA - half] += g_cq[half:]
        sums[1, Q_LORA - half:Q_LORA + KV_LORA - half] += _mm_tn(ckv_ref[...], hv)
        sums[1, Q_LORA + KV_LORA - half:] += _mm_tn(kpe_ref[...], hv)[QK_NOPE:QK_NOPE + QK_ROPE]

        if share:
            _, theirs, send_sem, recv_sem = scratch

            @pl.when(pl.program_id(0) == pl.num_programs(0) - 1)
            def _():
                c = lax.axis_index("c")
                copy = _remote(sums.at[1 - c], theirs, send_sem, recv_sem, 0, (lax.axis_index("x"), lax.axis_index("y"), 1 - c))
                copy.start()
                copy.wait()
                o_ref[...] = (sums[c] + theirs[...]).astype(BF16)

    rows = lambda a: pl.BlockSpec((TK_GRAD, a.shape[1]), lambda kk: (kk, 0))
    sem = pltpu.SemaphoreType.DMA((1,))
    shape, dtype = ((half, D), BF16) if share else ((2, half, D), F32)
    return pl.pallas_call(
        body, name="win_grad_qkv", grid=(s // TK_GRAD,), in_specs=[rows(h), rows(dcq), rows(dckv), rows(dkpe)],
        out_specs=pl.BlockSpec(shape, lambda kk: (0,) * len(shape)), out_shape=jax.ShapeDtypeStruct(shape, dtype),
        scratch_shapes=[pltpu.VMEM((2, half, D), F32), pltpu.VMEM((half, D), F32), sem, sem] if share else [],
        compiler_params=_params(("arbitrary",)),
    )(h, dcq, dckv, dkpe)


def _pad_wuq(w_uq):
    rows = w_uq.shape[0]
    w = w_uq.reshape(rows, NH, QK_NOPE + QK_ROPE)
    return jnp.pad(w, ((0, 0), (0, 0), (0, LANE - QK_NOPE - QK_ROPE))).reshape(rows, NH * LANE)


def _pad_wukv(w_ukv):
    heads = w_ukv.shape[1] // (QK_NOPE + V_DIM)
    w = w_ukv.reshape(KV_LORA, heads, QK_NOPE + V_DIM)
    w_k = jnp.pad(w[:, :, :QK_NOPE], ((0, 0), (0, 0), (0, LANE - QK_NOPE))).reshape(KV_LORA, heads * LANE)
    wv = w[:, :, QK_NOPE:].reshape(KV_LORA, heads // 2, 2, 1, V_DIM)
    eye = jnp.eye(2, dtype=w.dtype).reshape(1, 1, 2, 2, 1)
    return w_k, (wv * eye).reshape(KV_LORA, heads * LANE)


def _local_step(x, tgt, g_pre, w_in_t, b_gate, g_q, g_kv, lb_logits, g_hgrn, g_post, weights, exchange=None):
    s = x.shape[0]
    w_kpe = _kpe_block(w_in_t)
    rc, rs1, rs2 = _rope_tables(s)
    g_hg = jnp.tile(g_hgrn, (1, NH))

    proj, h = _front_fwd(x, g_pre, w_in_t, w_kpe, weights.tokens)
    w_uq_p, w_k_p, w_v_p = weights.qkv(h)
    q, k, vv = _qkv_fwd(proj, g_q, g_kv, w_uq_p, w_k_p, w_v_p, rc, rs1, rs2)
    attn, lse = _attn_fwd(q, k, vv)
    o_raw, states = _hgrn_fwd(proj, lb_logits)
    wa, wb, w_out = weights.mid(o_raw)
    (loss, dout, dattn, dga, dor, dgb, dmg, d_wout, d_wa, d_wb, d_gpost, d_bgate, d_ghg) = _mid(
        proj, attn, o_raw, x, tgt, g_hg, b_gate, g_post, wa, wb, w_out)
    d_win = _win_grad(h, [dmg, dga, dgb], [O_MERGE, O_GA, O_GB], "win_grad_mid")
    dh3, d_lbl = _hgrn_bwd(proj, lb_logits, states, dor)
    d_win = _win_grad(h, [dh3], [O_HQ], "win_grad_hgrn", d_win)
    early = dict(w_in=d_win, w_branch_a=d_wa, w_branch_b=d_wb, w_out=d_wout)
    token = exchange.start_early(early) if exchange else jnp.zeros((8, LANE), F32)
    dq, dk, dvv = _attn_bwd(q, k, vv, attn, dattn, lse, token)
    dcq, dckv, dkpe, d_wuq, d_wukv, d_gq, d_gkv = _qkv_bwd(proj, dq, dk, dvv, g_q, g_kv, w_uq_p, w_k_p, w_v_p, rc, rs1, rs2)
    late = dict(w_in_qkv=_win_grad_qkv(h, dcq, dckv, dkpe, share=exchange is not None), w_uq=d_wuq, w_ukv=d_wukv)
    token = exchange.start_late(late) if exchange else jnp.zeros((8, LANE), F32)
    grad_x, d_gpre = _front_bwd(x, dout, dmg, dga, dh3, dgb, dcq, dckv, dkpe, g_pre, w_in_t, w_kpe, token)
    vec_grads = dict(g_pre=d_gpre, b_gate=d_bgate, g_q=d_gq, g_kv=d_gkv, lb_logits=d_lbl, g_hgrn=d_ghg, g_post=d_gpost)
    return loss, grad_x, dict(early, **late), vec_grads


SHARD_SHAPES = (("w_in", (1416, 1024)), ("w_uq", (192, 768)), ("w_ukv", (256, 256)), ("w_branch_a", (512, 256)),
                ("w_branch_b", (512, 256)), ("w_out", (256, 1024)))
BIG = tuple(n for n, _ in SHARD_SHAPES)
ROW_SHARDED = ("w_in", "w_uq", "w_out")
N_CHIPS = 4
W_IN_FORWARD_CUT = 704


def _to_block(name, a):
    return a[0].T if name == "w_in" else a[0]


def _from_block(name, a):
    return a.T[None] if name == "w_in" else a[None]
VEC_ROWS = (("g_pre", 0, 1024), ("b_gate", 1, 2048), ("g_q", 2, 768), ("g_kv", 3, 256), ("g_hgrn", 6, 64), ("g_post", 7, 1024))
VEC_LB_ROW = 4
VEC_SHAPE = (8, 2048)


def _split_by_chip(name, g):
    a, b = dict(SHARD_SHAPES)[name]
    return g.reshape(N_CHIPS, a, b) if name in ROW_SHARDED else g.reshape(a, N_CHIPS, b).transpose(1, 0, 2)


def _join_chips(name, w):
    a, b = dict(SHARD_SHAPES)[name]
    return w.reshape(N_CHIPS * a, b) if name in ROW_SHARDED else w.transpose(1, 0, 2).reshape(a, N_CHIPS * b)


MESH = pl.DeviceIdType.MESH
HBM = pl.BlockSpec(memory_space=pltpu.HBM)


def _mesh_place():
    x, y, c = lax.axis_index("x"), lax.axis_index("y"), lax.axis_index("c")
    return x, y, c, 2 * x + y, [(1 - x, y), (x, 1 - y), (1 - x, 1 - y)]


def _remote(src, dst, send_sems, recv_sems, k, to):
    return pltpu.make_async_remote_copy(src_ref=src, dst_ref=dst, send_sem=send_sems.at[k], recv_sem=recv_sems.at[k],
                                        device_id=to, device_id_type=MESH)


def _gather_w_in(shard):
    a, b = shard.shape
    cut = W_IN_FORWARD_CUT

    def body(src, out, ici_send, ici_recv, d2d_send, d2d_recv, local_sem):
        x, y, c = lax.axis_index("x"), lax.axis_index("y"), lax.axis_index("c")
        me, xn, yn, dg = 2 * x + y, 2 * (1 - x) + y, 2 * x + (1 - y), 2 * (1 - x) + (1 - y)
        to_x, to_y, sibling = (1 - x, y, c), (x, 1 - y, c), (x, y, 1 - c)
        first, rest = pl.ds(0, cut), pl.ds(cut, a - cut)

        whole = lambda ref, which: ref.at[:, pl.ds(pl.multiple_of(which * (b // 2), b // 2), b // 2)]
        own = pltpu.make_async_copy(src, out.at[me], local_sem)
        sends = [_remote(whole(src, c), whole(out.at[me], c), ici_send, ici_recv, 0, to_x),
                 _remote(whole(src, c), whole(out.at[me], c), ici_send, ici_recv, 1, to_y)]
        for cp in sends:
            cp.start()
        own.start()

        def landed(slot, rows, k, d2d_k, src_dev):
            piece = whole(out.at[slot], c) if rows is None else out.at[slot].at[rows, pl.ds(pl.multiple_of(c * (b // 2), b // 2), b // 2)]
            _remote(piece, piece, ici_send, ici_recv, k, src_dev).wait_recv()
            cp = _remote(piece, piece, d2d_send, d2d_recv, d2d_k, sibling)
            cp.start()
            sends.append(cp)
            return piece

        def pass_on(slot, rows, k, to):
            piece = out.at[slot].at[rows, pl.ds(pl.multiple_of(c * (b // 2), b // 2), b // 2)]
            cp = _remote(piece, piece, ici_send, ici_recv, k, to)
            cp.start()
            sends.append(cp)

        landed(xn, None, 0, 0, to_x)
        pass_on(xn, first, 2, to_y)
        landed(yn, None, 1, 1, to_y)
        pass_on(yn, rest, 3, to_x)
        landed(dg, first, 2, 2, to_y)
        landed(dg, rest, 3, 3, to_x)
        other = pl.ds(pl.multiple_of((1 - c) * (b // 2), b // 2), b // 2)
        for d2d_k, (slot, rows) in enumerate(((xn, None), (yn, None), (dg, first), (dg, rest))):
            piece = out.at[slot].at[:, other] if rows is None else out.at[slot].at[rows, other]
            _remote(piece, piece, d2d_send, d2d_recv, d2d_k, sibling).wait_recv()
        for cp in sends:
            cp.wait_send()
        own.wait()

    sems = pltpu.SemaphoreType.DMA((4,))
    return pl.pallas_call(
        body, name="gather_w_in", in_specs=[HBM], out_specs=HBM,
        out_shape=jax.ShapeDtypeStruct((N_CHIPS, a, b), shard.dtype),
        scratch_shapes=[sems, sems, sems, sems, pltpu.SemaphoreType.DMA],
        compiler_params=pltpu.CompilerParams(has_side_effects=True),
    )(shard)


SEM =pl.BlockSpec(memory_space=pltpu.SEMAPHORE)
DATAFLOW = pltpu.SideEffectType.DATAFLOW_SIDE_EFFECTING


def _exchange_copies(srcs, to_first, src_refs, land_refs, send_sems, recv_sems):
    x, y, c, me, chips = _mesh_place()
    n = len(srcs)
    sends, recvs = [], []
    for k in range(n):
        if k in to_first:
            base = 3 * n + 4 * to_first.index(k)
            sends.append((me != 0, pltpu.make_async_remote_copy(
                src_ref=src_refs[k], dst_ref=land_refs[k].at[me], send_sem=send_sems.at[base], recv_sem=recv_sems.at[base + me],
                device_id=(0, 0, c), device_id_type=MESH)))
            for s in range(1, N_CHIPS):
                recvs.append((me == 0, pltpu.make_async_remote_copy(
                    src_ref=src_refs[k], dst_ref=land_refs[k].at[s], send_sem=send_sems.at[base], recv_sem=recv_sems.at[base + s],
                    device_id=(s // 2, s % 2, c), device_id_type=MESH)))
        else:
            slab = (lambda t, k=k: src_refs[k]) if srcs[k].ndim == 2 else (lambda t, k=k: src_refs[k].at[t])
            for j, (px, py) in enumerate(chips):
                sends.append((None, _remote(slab(2 * px + py), land_refs[k].at[me], send_sems, recv_sems, 3 * k + j, (px, py, c))))
                recvs.append((None, _remote(slab(me), land_refs[k].at[2 * px + py], send_sems, recv_sems, 3 * k + j, (px, py, c))))
    return sends, recvs


def _when(pred, fn):
    if pred is None:
        fn()
    else:
        pl.when(pred)(fn)


def _exchange_start(srcs, to_first, name, after=None):
    n = len(srcs)
    n_sems = 3 * n + 4 * len(to_first)
    lands = [lax.empty((N_CHIPS,) + s.shape[-2:], s.dtype) for s in srcs]
    extra = [] if after is None else [after]

    def body(*refs):
        src_refs, land_refs = refs[:n], refs[n:2 * n]
        send_sems, recv_sems, token = refs[2 * n + len(extra)], refs[2 * n + len(extra) + 1], refs[-1]
        sends, _ = _exchange_copies(srcs, to_first, src_refs, land_refs, send_sems, recv_sems)
        for pred, cp in sends:
            _when(pred, cp.start)
        token[...] = jnp.zeros_like(token)

    hbm = lambda a: pltpu.HBM(a.shape, a.dtype)
    res = pl.pallas_call(
        body, name=name,
        out_shape=[pltpu.SemaphoreType.DMA((n_sems,)), pltpu.SemaphoreType.DMA((n_sems,))] + [hbm(a) for a in srcs + lands]
        + [jax.ShapeDtypeStruct((8, LANE), F32)],
        in_specs=[HBM] * (2 * n) + [pl.BlockSpec(memory_space=pl.ANY)] * len(extra),
        out_specs=[SEM, SEM] + [HBM] * (2 * n) + [pl.BlockSpec(memory_space=pltpu.VMEM)],
        input_output_aliases={i: 2 + i for i in range(2 * n)},
        compiler_params=pltpu.CompilerParams(has_side_effects=DATAFLOW),
    )(*[pltpu.with_memory_space_constraint(a, pltpu.HBM) for a in srcs + lands], *extra)
    return res[:-1], res[-1]


def _exchange_wait(srcs, to_first, started, after, name):
    n = len(srcs)
    send_sems, recv_sems, thru = started[0], started[1], started[2:]

    def body(*refs):
        src_refs, land_refs, send_ref, recv_ref = refs[:n], refs[n:2 * n], refs[2 * n], refs[2 * n + 1]
        sends, recvs = _exchange_copies(srcs, to_first, src_refs, land_refs, send_ref, recv_ref)
        for pred, cp in sends:
            _when(pred, cp.wait_send)
        for pred, cp in recvs:
            _when(pred, cp.wait_recv)

    res = pl.pallas_call(
        body, name=name, out_shape=[pltpu.HBM(a.shape, a.dtype) for a in thru],
        in_specs=[HBM] * (2 * n) + [SEM, SEM, pl.BlockSpec(memory_space=pl.ANY)], out_specs=[HBM] * (2 * n),
        input_output_aliases={i: i for i in range(2 * n)},
        compiler_params=pltpu.CompilerParams(has_side_effects=DATAFLOW),
    )(*thru, send_sems, recv_sems, after)
    return res[:n], res[n:]


ROW_TILE = 256
COL_TILE = 256


def _block_tiling(a, b):
    if a <= ROW_TILE or a % ROW_TILE == 0:
        ta = min(a, ROW_TILE)
        return a // ta, (ta, b), lambda i: (i, 0)
    return b // COL_TILE, (a, COL_TILE), lambda i: (0, i)


def _sum_share_small(lands, owns, name, after=None):
    n = len(lands)
    extra = [] if after is None else [after]

    def body(*refs):
        p_refs, own_refs = refs[:n], refs[n:2 * n]
        o_refs, mine, theirs = (refs[j * n + len(extra):(j + 1) * n + len(extra)] for j in (2, 3, 4))
        send_sems, recv_sems = refs[5 * n + len(extra):]
        me = 2 * lax.axis_index("x") + lax.axis_index("y")
        sibling = (lax.axis_index("x"), lax.axis_index("y"), 1 - lax.axis_index("c"))
        copies = [_remote(mine[k], theirs[k], send_sems, recv_sems, k, sibling) for k in range(n)]
        for p_ref, own_ref, mine_ref, copy in zip(p_refs, own_refs, mine, copies):
            own = (own_ref[me] if len(own_ref.shape) == 3 else own_ref[...]).astype(F32)
            slot = lambda t: jnp.where(me == t, own, p_ref[t].astype(F32))
            mine_ref[...] = ((slot(0) + slot(1)) + slot(2)) + slot(3)
            copy.start()
        for o_ref, mine_ref, theirs_ref, copy in zip(o_refs, mine, theirs, copies):
            copy.wait()
            o_ref[...] = mine_ref[...] + theirs_ref[...]

    vmem = pl.BlockSpec(memory_space=pltpu.VMEM)
    kept = [pltpu.VMEM(land.shape[1:], F32) for land in lands]
    sems = pltpu.SemaphoreType.DMA((n,))
    return pl.pallas_call(
        body, name=name, in_specs=[vmem] * (2 * n) + [pl.BlockSpec(memory_space=pl.ANY)] * len(extra), out_specs=[vmem] * n,
        out_shape=[jax.ShapeDtypeStruct(land.shape[1:], F32) for land in lands], scratch_shapes=kept + kept + [sems, sems],
        compiler_params=_params(()),
    )(*lands, *owns, *extra)


def _adamw_small(grads, states, name):
    n = len(grads)

    def body(*refs):
        ins, outs = refs[:4 * n], refs[4 * n:]
        for k in range(n):
            g_ref, w_ref, m_ref, v_ref = ins[4 * k:4 * k + 4]
            g = g_ref[...]
            outs[4 * k][...] = g
            outs[4 * k + 1][...], outs[4 * k + 2][...], outs[4 * k + 3][...] = _adamw_math(g, w_ref[...], m_ref[...], v_ref[...])

    args = [t for k in range(n) for t in (grads[k], *states[k])]
    res = pl.pallas_call(body, name=name, out_shape=[jax.ShapeDtypeStruct(grads[k].shape, F32) for k in range(n) for _ in range(4)],
                         compiler_params=_params(()))(*args)
    return [tuple(res[4 * k:4 * k + 4]) for k in range(n)]


class _LaterWeights:
    MID = ("w_branch_a", "w_branch_b", "w_out")

    def __init__(self, blocks, after):
        self.qkv_blocks = [_pad_wuq(blocks["w_uq"]), *_pad_wukv(blocks["w_ukv"])]
        self.mid_blocks = [blocks[n] for n in self.MID]
        self.qkv_started, t1 = _exchange_start(self.qkv_blocks, (), "weights_qkv_start", after)
        self.mid_started, t2 = _exchange_start(self.mid_blocks, (), "weights_mid_start", after)
        self.tokens = [t1, t2]

    @staticmethod
    def _whole(blocks, joined, started, after, name):
        _, landed = _exchange_wait(blocks, (), started, after, name)
        me = 2 * lax.axis_index("x") + lax.axis_index("y")
        out = []
        for block, land, join in zip(blocks, landed, joined):
            w = lax.dynamic_update_index_in_dim(land, block, me, 0)
            out.append(w.reshape(N_CHIPS * block.shape[0], block.shape[1]) if join else w)
        return out

    def qkv(self, after):
        return self._whole(self.qkv_blocks, (True, False, False), self.qkv_started, after, "weights_qkv_wait")

    def mid(self, after):
        return self._whole(self.mid_blocks, (False, False, True), self.mid_started, after, "weights_mid_wait")


class _GradExchange:
    EARLY = ("w_in", "w_branch_a", "w_branch_b", "w_out")
    LATE = ("w_uq", "w_ukv")

    def __init__(self, state):
        self.state = state
        self.outs = {}

    def start_early(self, g):
        self.early = [(g[n] if g[n].ndim == 3 else _split_by_chip(n, g[n])).astype(BF16) for n in self.EARLY]
        self.early_started, token = _exchange_start(self.early, (), "grads_early_start")
        return token

    def start_late(self, g):
        self.early, self.early_landed = _exchange_wait(self.early, (), self.early_started, g["w_uq"], "grads_early_wait")
        self.late = [_split_by_chip("w_uq", g["w_uq"]), g["w_ukv"], g["w_in_qkv"]]
        self.late_started, token = _exchange_start(self.late, (2,), "grads_late_start")
        names = self.EARLY[1:]
        grads = _sum_share_small(self.early_landed[1:], self.early[1:], "sum_early", after=token)
        for n, out in zip(names, _adamw_small(grads, [self.state[n] for n in names], "adamw_early")):
            self.outs[n] = out
        return self.outs[names[-1]][0]

    def finish(self, after):
        late, late_landed = _exchange_wait(self.late, (2,), self.late_started, after, "grads_late_wait")
        grad = _sum_exchange(self.early_landed[0], self.early[0], late_landed[2], late[2], "sum_w_in")
        self.outs["w_in"] = _adamw(grad, *self.state["w_in"], "adamw_w_in")
        return list(late[:2]), list(late_landed[:2])


def _adamw_math(g, w, m, v):
    nm = ADAM_B1 * m + (1.0 - ADAM_B1) * g
    nv = ADAM_B2 * v + (1.0 - ADAM_B2) * (g * g)
    m_hat = nm / (1.0 - ADAM_B1 ** ADAM_STEP)
    v_hat = nv / (1.0 - ADAM_B2 ** ADAM_STEP)
    return -ADAM_LR * (m_hat / (jnp.sqrt(v_hat) + ADAM_EPS) + ADAM_WD * w), nm, nv


def _sum_exchange(land, own, first_land, first_own, name):
    _, a, b = land.shape
    steps, tile, at = _block_tiling(a, b)
    assert tile[0] == a, "the extra rows need whole columns in a step"
    r = first_own.shape[0]

    def body(me_ref, p_ref, own_ref, fp_ref, fo_ref, g_ref, mine_s, theirs_s, send_sems, recv_sems):
        pass_, i = pl.program_id(0), pl.program_id(1)
        me, c = me_ref[0], lax.axis_index("c")
        sibling = (lax.axis_index("x"), lax.axis_index("y"), 1 - c)
        copy = _remote(mine_s.at[i], theirs_s.at[i], send_sems, recv_sems, i, sibling)

        @pl.when(pass_ == 0)
        def _():
            own = own_ref[...].astype(F32)
            slot = lambda t: jnp.where(me == t, own, p_ref[t].astype(F32))
            mine_s[i] = ((slot(0) + slot(1)) + slot(2)) + slot(3)

            @pl.when(me == 0)
            def _():
                f = lambda t: fp_ref[t].astype(F32)
                rows = pl.ds(pl.multiple_of(c * r, 8), r)
                mine_s[i, rows, :] += ((fo_ref[...].astype(F32) + f(1)) + f(2)) + f(3)

            copy.start()

        @pl.when(pass_ == 1)
        def _():
            copy.wait()
            g_ref[...] = mine_s[i] + theirs_s[i]

    first = lambda p, i: i * (1 - p) + (steps - 1) * p
    in_specs = [pl.BlockSpec((N_CHIPS,) + tile, lambda p, i, me: (0,) + at(first(p, i))),
                pl.BlockSpec((None,) + tile, lambda p, i, me: (me[0],) + at(first(p, i))),
                pl.BlockSpec((N_CHIPS, r, tile[1]), lambda p, i, me: (0,) + at(first(p, i))),
                pl.BlockSpec((r, tile[1]), lambda p, i, me: at(first(p, i)))]
    me = jnp.reshape(2 * lax.axis_index("x") + lax.axis_index("y"), (1,)).astype(jnp.int32)
    kept = pltpu.VMEM((steps,) + tile, F32)
    sems = pltpu.SemaphoreType.DMA((steps,))
    return pl.pallas_call(
        body, name=name,
        grid_spec=pltpu.PrefetchScalarGridSpec(num_scalar_prefetch=1, grid=(2, steps), in_specs=in_specs,
                                               out_specs=pl.BlockSpec(tile, lambda p, i, me: at(i * p)),
                                               scratch_shapes=[kept, kept, sems, sems]),
        out_shape=jax.ShapeDtypeStruct((a, b), F32),
        compiler_params=_params(("arbitrary", "arbitrary")),
    )(me, land, own, first_land, first_own)


def _adamw(g, w, m, v, name):
    a, b = g.shape
    steps, tile, at = _block_tiling(a, b)

    def body(g_ref, w_ref, m_ref, v_ref, go_ref, d_ref, nm_ref, nv_ref):
        g = g_ref[...]
        go_ref[...] = g
        d_ref[...], nm_ref[...], nv_ref[...] = _adamw_math(g, w_ref[...], m_ref[...], v_ref[...])

    spec = pl.BlockSpec(tile, at)
    sds = jax.ShapeDtypeStruct((a, b), F32)
    return pl.pallas_call(
        body, name=name, grid=(steps,), in_specs=[spec] * 4, out_specs=[spec] * 4, out_shape=[sds] * 4,
        compiler_params=_params(("parallel",)),
    )(g, w, m, v)


LOSS_AT = (2, 1024)


def _vec_pack(vg, loss):
    names = [n for n, _, _ in VEC_ROWS]

    def body(*refs):
        o_ref = refs[-1]
        lb_ref, loss_ref = refs[len(names)], refs[len(names) + 1]
        o_ref[...] = jnp.zeros_like(o_ref)
        o_ref[LOSS_AT[0]:LOSS_AT[0] + 1, LOSS_AT[1]:LOSS_AT[1] + LANE] = jnp.broadcast_to(loss_ref[...], (1, LANE))
        for (name, row, size), ref in zip(VEC_ROWS, refs):
            if name == "g_hgrn":
                r = lax.broadcasted_iota(jnp.int32, (NH * V_DIM, LANE), 0)
                c = lax.broadcasted_iota(jnp.int32, (NH * V_DIM, LANE), 1)
                fold = ((r % V_DIM) == c).astype(F32)
                o_ref[row:row + 1, 0:LANE] = jnp.dot(ref[...], fold, precision=HIGHEST, preferred_element_type=F32)
            else:
                o_ref[row:row + 1, 0:size] = ref[...]
        o_ref[VEC_LB_ROW:VEC_LB_ROW + 2, 0:512] = lb_ref[...]

    return pl.pallas_call(body, name="vec_pack", out_shape=jax.ShapeDtypeStruct(VEC_SHAPE, F32))(
        *[vg[n] for n in names], vg["lb_logits"], loss)


def _adamw_vec(block, w, m, v):
    names = [n for n, _, _ in VEC_ROWS] + ["lb_logits"]
    k = len(names)

    def body(g_ref, *refs):
        ins, outs = refs[:3 * k], refs[3 * k:]
        outs[-1][...] = g_ref[LOSS_AT[0]:LOSS_AT[0] + 1, LOSS_AT[1]:LOSS_AT[1] + LANE]
        for i, name in enumerate(names):
            if name == "lb_logits":
                rows, cols = slice(VEC_LB_ROW, VEC_LB_ROW + 2), slice(0, 512)
            else:
                _, row, size = VEC_ROWS[i]
                rows, cols = slice(row, row + 1), slice(0, size)
            g = g_ref[rows, cols]
            d, nm, nv = _adamw_math(g, ins[i][...], ins[k + i][...], ins[2 * k + i][...])
            for o_ref, val in zip(outs[4 * i:4 * i + 4], (g, d, nm, nv)):
                o_ref[...] = val

    shapes = [jax.ShapeDtypeStruct(w[n].shape, F32) for n in names for _ in range(4)] + [jax.ShapeDtypeStruct((1, LANE), F32)]
    res = pl.pallas_call(body, name="adamw_vec", out_shape=shapes)(
        block, *[w[n] for n in names], *[m[n] for n in names], *[v[n] for n in names])
    return [{n: res[4 * i + j] for i, n in enumerate(names)} for j in range(4)], res[-1]


WEIGHTS = ("g_pre", "w_in", "b_gate", "g_q", "w_uq", "g_kv", "w_ukv", "lb_logits", "g_hgrn", "w_branch_a", "w_branch_b", "w_out", "g_post")


def kernel(x, g_pre, w_in, b_gate, g_q, w_uq, g_kv, w_ukv, lb_logits, g_hgrn, w_branch_a, w_branch_b, w_out, g_post, loss_target, m_g_pre, m_w_in, m_b_gate, m_g_q, m_w_uq, m_g_kv, m_w_ukv, m_lb_logits, m_g_hgrn, m_w_branch_a, m_w_branch_b, m_w_out, m_g_post, v_g_pre, v_w_in, v_b_gate, v_g_q, v_w_uq, v_g_kv, v_w_ukv, v_lb_logits, v_g_hgrn, v_w_branch_a, v_w_branch_b, v_w_out, v_g_post):
    w = dict(g_pre=g_pre, w_in=w_in, b_gate=b_gate, g_q=g_q, w_uq=w_uq, g_kv=g_kv, w_ukv=w_ukv, lb_logits=lb_logits, g_hgrn=g_hgrn,
             w_branch_a=w_branch_a, w_branch_b=w_branch_b, w_out=w_out, g_post=g_post)
    m = dict(g_pre=m_g_pre, w_in=m_w_in, b_gate=m_b_gate, g_q=m_g_q, w_uq=m_w_uq, g_kv=m_g_kv, w_ukv=m_w_ukv, lb_logits=m_lb_logits,
             g_hgrn=m_g_hgrn, w_branch_a=m_w_branch_a, w_branch_b=m_w_branch_b, w_out=m_w_out, g_post=m_g_post)
    v = dict(g_pre=v_g_pre, w_in=v_w_in, b_gate=v_b_gate, g_q=v_g_q, w_uq=v_w_uq, g_kv=v_g_kv, w_ukv=v_w_ukv, lb_logits=v_lb_logits,
             g_hgrn=v_g_hgrn, w_branch_a=v_w_branch_a, w_branch_b=v_w_branch_b, w_out=v_w_out, g_post=v_g_post)
    blocks = {n: _to_block(n, w[n]).astype(BF16) for n in BIG}
    w_in_all = _gather_w_in(blocks["w_in"])
    weights = _LaterWeights(blocks, w_in_all)
    state = {n: [_to_block(n, t[n]) for t in (w, m, v)] for n in BIG}
    exchange = _GradExchange(state)
    loss, grad_x, _, vec_grads = _local_step(
        x[0], loss_target[0], g_pre, _join_chips("w_in", w_in_all), b_gate, g_q, g_kv, lb_logits, g_hgrn, g_post, weights, exchange)
    vec = _vec_pack(vec_grads, loss)
    vec_started, token = _exchange_start([vec], (), "vec_start")
    late_sent, late_landed = exchange.finish(token)
    (vec,), (vec_landed,) = _exchange_wait([vec], (), vec_started, exchange.outs["w_in"][0], "vec_wait")
    grads = _sum_share_small(late_landed + [vec_landed], late_sent + [vec], "sum_late")
    done = dict(exchange.outs)
    done.update(zip(exchange.LATE, _adamw_small(grads[:-1], [state[n] for n in exchange.LATE], "adamw_late")))
    outs = [{}, {}, {}, {}]
    for n in BIG:
        for o, val in zip(outs, done[n]):
            o[n] = _from_block(n, val)
    vec_outs, total = _adamw_vec(grads[-1], w, m, v)
    for o, vals in zip(outs, vec_outs):
        o.update(vals)
    return (total[0, 0], grad_x[None], *[o[n] for o in outs for n in WEIGHTS])
```
